```python
import jax, jax.numpy as jnp
from jax import lax
import numpy as np

D_MODEL = 1024
BATCH = 8
SEQ = 8192
DEPTH = 1

MLA_HEADS = 8
MLA_Q_RANK = 256
MLA_KV_RANK = 128
MLA_NOPE_DIM = 64
MLA_ROPE_DIM = 32
MLA_V_DIM = 64
MLA_QK_DIM = MLA_NOPE_DIM + MLA_ROPE_DIM
Q_BLOCK = 128
RET_HEADS = 8
RET_QK_DIM = D_MODEL // (2 * RET_HEADS)
RET_V_DIM = 2 * RET_QK_DIM
RET_CHUNK = 128
FFN_HIDDEN = -(-8 * D_MODEL // (3 * 256)) * 256
ROPE_THETA = 10000.0
EPS = 1e-6

IN_SPLITS = [
    MLA_Q_RANK,
    MLA_KV_RANK,
    MLA_ROPE_DIM,
    RET_HEADS * RET_QK_DIM,
    RET_HEADS * RET_QK_DIM,
    RET_HEADS * RET_V_DIM,
    RET_HEADS * RET_V_DIM,
    2 * D_MODEL,
]
IN_WIDTH = sum(IN_SPLITS)

kernel_name = "hybrid_mla_retention_gated_block"


def _rms(xf):
    return xf * lax.rsqrt(jnp.mean(xf * xf, axis=-1, keepdims=True) + EPS)


def rms_norm(x, g):
    y = _rms(x.astype(jnp.float32)) * g.astype(jnp.float32)
    return y.astype(x.dtype)


def rope(x, positions):
    half = x.shape[-1] // 2
    inv = ROPE_THETA ** (-jnp.arange(half, dtype=jnp.float32) / half)
    ang = positions.astype(jnp.float32)[..., None] * inv
    cos = jnp.cos(ang)[:, :, None, :]
    sin = jnp.sin(ang)[:, :, None, :]
    xf = x.astype(jnp.float32)
    x1, x2 = xf[..., :half], xf[..., half:]
    out = jnp.concatenate([x1 * cos - x2 * sin, x2 * cos + x1 * sin], axis=-1)
    return out.astype(x.dtype)


def mla_attention(c_q, c_kv, k_rope, positions, g_q_a, w_q_b, g_kv_a, w_kv_b, g_qn, g_kn):
    B, S, _ = c_q.shape
    H = MLA_HEADS
    q = (rms_norm(c_q, g_q_a) @ w_q_b).reshape(B, S, H, MLA_QK_DIM)
    kv = (rms_norm(c_kv, g_kv_a) @ w_kv_b).reshape(B, S, H, MLA_NOPE_DIM + MLA_V_DIM)
    k_nope, v = kv[..., :MLA_NOPE_DIM], kv[..., MLA_NOPE_DIM:]
    k_r = jnp.broadcast_to(k_rope[:, :, None, :], (B, S, H, MLA_ROPE_DIM))
    k = jnp.concatenate([k_nope, k_r], axis=-1)
    q = rms_norm(q, g_qn)
    k = rms_norm(k, g_kn)
    q = jnp.concatenate([q[..., :MLA_NOPE_DIM], rope(q[..., MLA_NOPE_DIM:], positions)], axis=-1)
    k = jnp.concatenate([k[..., :MLA_NOPE_DIM], rope(k[..., MLA_NOPE_DIM:], positions)], axis=-1)
    q = q.astype(jnp.float32).transpose(0, 2, 1, 3)
    k = k.astype(jnp.float32).transpose(0, 2, 1, 3)
    v = v.astype(jnp.float32).transpose(0, 2, 1, 3)
    scale = MLA_QK_DIM ** -0.5
    nb = S // Q_BLOCK
    qb = q.reshape(B, H, nb, Q_BLOCK, MLA_QK_DIM).transpose(2, 0, 1, 3, 4)

    def attend(q_blk):
        s = jnp.einsum('bhqd,bhkd->bhqk', q_blk, k) * scale
        p = jax.nn.softmax(s, axis=-1)
        return jnp.einsum('bhqk,bhkv->bhqv', p, v)

    o = lax.map(attend, qb)
    o = o.transpose(1, 0, 3, 2, 4).reshape(B, S, H * MLA_V_DIM)
    return o


def retention_dir(q, k, v, log_gamma, strict):
    B, H, S, dk = q.shape
    dv = v.shape[-1]
    C = RET_CHUNK
    n = S // C
    idx = jnp.arange(C, dtype=jnp.float32)
    diff = idx[:, None] - idx[None, :]
    mask = diff > 0 if strict else diff >= 0
    decay_in = jnp.where(mask, jnp.exp(log_gamma[:, None, None] * jnp.maximum(diff, 0.0)), 0.0)
    q_decay = jnp.exp(log_gamma[:, None] * (idx + 1.0))[..., None]
    k_decay = jnp.exp(log_gamma[:, None] * (C - 1.0 - idx))[..., None]
    chunk_decay = jnp.exp(log_gamma * C)[:, None, None]

    def to_chunks(a):
        return a.reshape(B, H, n, C, a.shape[-1]).transpose(2, 0, 1, 3, 4)

    def step(state, inp):
        qi, ki, vi = inp
        inner = jnp.einsum('bhcd,bhed->bhce', qi, ki) * decay_in
        inner = jnp.einsum('bhce,bhev->bhcv', inner, vi)
        cross = jnp.einsum('bhcd,bhdv->bhcv', qi * q_decay, state)
        new_state = state * chunk_decay + jnp.einsum('bhcd,bhcv->bhdv', ki * k_decay, vi)
        return new_state, inner + cross

    state0 = jnp.zeros((B, H, dk, dv), jnp.float32)
    _, out = lax.scan(step, state0, (to_chunks(q), to_chunks(k), to_chunks(v)))
    return out.transpose(1, 2, 0, 3, 4).reshape(B, H, S, dv)


def bidirectional_retention(q, k, v, decay_fwd, decay_bwd):
    lg_f = -jnp.exp(decay_fwd.astype(jnp.float32))
    lg_b = -jnp.exp(decay_bwd.astype(jnp.float32))
    fwd = retention_dir(q, k, v, lg_f, False)
    flip = lambda a: jnp.flip(a, axis=2)
    bwd = flip(retention_dir(flip(q), flip(k), flip(v), lg_b, True))
    return fwd + bwd


def _fwd_setup_inputs(seed: int = 0) -> dict:
    key = jax.random.key(seed)
    ks = jax.random.split(key, 20)
    f32 = jnp.float32

    def w(k, fan_in, fan_out):
        return jax.random.normal(k, (fan_in, fan_out), f32) * fan_in ** -0.5

    def gain(k, n):
        return 1.0 + 0.02 * jax.random.normal(k, (n,), f32)

    gamma0 = 1.0 - 2.0 ** (-5.0 - jnp.arange(RET_HEADS, dtype=f32))
    decay_base = jnp.log(-jnp.log(gamma0))
    x = jax.random.normal(ks[0], (BATCH, SEQ, D_MODEL), f32)
    positions = (jnp.arange(SEQ, dtype=jnp.int32)[None, :]
                 + jax.random.randint(ks[1], (BATCH, 1), 0, SEQ, dtype=jnp.int32))
    return {
        "x": x,
        "positions": positions,
        "g_mix": gain(ks[2], D_MODEL),
        "w_in": w(ks[3], D_MODEL, IN_WIDTH),
        "g_q_a": gain(ks[4], MLA_Q_RANK),
        "w_q_b": w(ks[5], MLA_Q_RANK, MLA_HEADS * MLA_QK_DIM),
        "g_kv_a": gain(ks[6], MLA_KV_RANK),
        "w_kv_b": w(ks[7], MLA_KV_RANK, MLA_HEADS * (MLA_NOPE_DIM + MLA_V_DIM)),
        "g_qn": gain(ks[8], MLA_QK_DIM),
        "g_kn": gain(ks[9], MLA_QK_DIM),
        "w_mla_out": w(ks[10], MLA_HEADS * MLA_V_DIM, D_MODEL),
        "ret_decay_fwd": decay_base + 0.05 * jax.random.normal(ks[11], (RET_HEADS,), f32),
        "ret_decay_bwd": decay_base + 0.05 * jax.random.normal(ks[12], (RET_HEADS,), f32),
        "w_ret_out": w(ks[13], RET_HEADS * RET_V_DIM, D_MODEL),
        "w_out": w(ks[14], D_MODEL, D_MODEL),
        "g_ffn": gain(ks[15], D_MODEL),
        "w_gate_up": w(ks[16], D_MODEL, 2 * FFN_HIDDEN),
        "w_down": w(ks[17], FFN_HIDDEN, D_MODEL),
    }


def _fwd_reference(x, positions, g_mix, w_in, g_q_a, w_q_b, g_kv_a, w_kv_b, g_qn, g_kn,
              w_mla_out, ret_decay_fwd, ret_decay_bwd, w_ret_out, w_out,
              g_ffn, w_gate_up, w_down):
    B, S, D = x.shape
    split_idx = np.cumsum(IN_SPLITS)[:-1].tolist()
    for _ in range(DEPTH):
        h = rms_norm(x, g_mix)
        proj = h @ w_in
        c_q, c_kv, k_rope, q_r, k_r, v_r, g_r, gate_logits = jnp.split(proj, split_idx, axis=-1)

        o_a = mla_attention(c_q, c_kv, k_rope, positions, g_q_a, w_q_b, g_kv_a, w_kv_b, g_qn, g_kn)
        y_a = o_a.astype(x.dtype) @ w_mla_out

        q_r = rope(q_r.reshape(B, S, RET_HEADS, RET_QK_DIM), positions)
        k_r = rope(k_r.reshape(B, S, RET_HEADS, RET_QK_DIM), positions)
        q_r = q_r.astype(jnp.float32).transpose(0, 2, 1, 3)
        k_r = k_r.astype(jnp.float32).transpose(0, 2, 1, 3) * (RET_QK_DIM ** -0.5)
        v_r = v_r.reshape(B, S, RET_HEADS, RET_V_DIM).astype(jnp.float32).transpose(0, 2, 1, 3)
        ret = bidirectional_retention(q_r, k_r, v_r, ret_decay_fwd, ret_decay_bwd)
        ret = _rms(ret).transpose(0, 2, 1, 3).reshape(B, S, RET_HEADS * RET_V_DIM)
        o_b = (jax.nn.silu(g_r.astype(jnp.float32)) * ret).astype(x.dtype)
        y_b = o_b @ w_ret_out

        gates = jax.nn.sigmoid(gate_logits.astype(jnp.float32))
        merged = gates[..., :D] * y_a.astype(jnp.float32) + gates[..., D:] * y_b.astype(jnp.float32)
        x = x + merged.astype(x.dtype) @ w_out

        h2 = rms_norm(x, g_ffn)
        gu = h2 @ w_gate_up
        gate, up = gu[..., :FFN_HIDDEN], gu[..., FFN_HIDDEN:]
        x = x + (jax.nn.silu(gate) * up) @ w_down
    return x


import jax as _jax
import jax.numpy as _jnp

TWIN_FORMAT = 'train_step'
FWD_PARAMS = ['x', 'positions', 'g_mix', 'w_in', 'g_q_a', 'w_q_b', 'g_kv_a', 'w_kv_b', 'g_qn', 'g_kn', 'w_mla_out', 'ret_decay_fwd', 'ret_decay_bwd', 'w_ret_out', 'w_out', 'g_ffn', 'w_gate_up', 'w_down']
TWIN_WEIGHTS = ['g_mix', 'w_in', 'g_q_a', 'w_q_b', 'g_kv_a', 'w_kv_b', 'g_qn', 'g_kn', 'w_mla_out', 'ret_decay_fwd', 'ret_decay_bwd', 'w_ret_out', 'w_out', 'g_ffn', 'w_gate_up', 'w_down']
TWIN_DIFF_INPUT = 'x'
TWIN_INPUTS = ['x', 'positions', 'g_mix', 'w_in', 'g_q_a', 'w_q_b', 'g_kv_a', 'w_kv_b', 'g_qn', 'g_kn', 'w_mla_out', 'ret_decay_fwd', 'ret_decay_bwd', 'w_ret_out', 'w_out', 'g_ffn', 'w_gate_up', 'w_down', 'loss_target', 'm_g_mix', 'm_w_in', 'm_g_q_a', 'm_w_q_b', 'm_g_kv_a', 'm_w_kv_b', 'm_g_qn', 'm_g_kn', 'm_w_mla_out', 'm_ret_decay_fwd', 'm_ret_decay_bwd', 'm_w_ret_out', 'm_w_out', 'm_g_ffn', 'm_w_gate_up', 'm_w_down', 'v_g_mix', 'v_w_in', 'v_g_q_a', 'v_w_q_b', 'v_g_kv_a', 'v_w_kv_b', 'v_g_qn', 'v_g_kn', 'v_w_mla_out', 'v_ret_decay_fwd', 'v_ret_decay_bwd', 'v_w_ret_out', 'v_w_out', 'v_g_ffn', 'v_w_gate_up', 'v_w_down']
TWIN_OUTPUTS = ['loss', 'grad_x', 'grad_g_mix', 'grad_w_in', 'grad_g_q_a', 'grad_w_q_b', 'grad_g_kv_a', 'grad_w_kv_b', 'grad_g_qn', 'grad_g_kn', 'grad_w_mla_out', 'grad_ret_decay_fwd', 'grad_ret_decay_bwd', 'grad_w_ret_out', 'grad_w_out', 'grad_g_ffn', 'grad_w_gate_up', 'grad_w_down', 'delta_g_mix', 'delta_w_in', 'delta_g_q_a', 'delta_w_q_b', 'delta_g_kv_a', 'delta_w_kv_b', 'delta_g_qn', 'delta_g_kn', 'delta_w_mla_out', 'delta_ret_decay_fwd', 'delta_ret_decay_bwd', 'delta_w_ret_out', 'delta_w_out', 'delta_g_ffn', 'delta_w_gate_up', 'delta_w_down', 'new_m_g_mix', 'new_m_w_in', 'new_m_g_q_a', 'new_m_w_q_b', 'new_m_g_kv_a', 'new_m_w_kv_b', 'new_m_g_qn', 'new_m_g_kn', 'new_m_w_mla_out', 'new_m_ret_decay_fwd', 'new_m_ret_decay_bwd', 'new_m_w_ret_out', 'new_m_w_out', 'new_m_g_ffn', 'new_m_w_gate_up', 'new_m_w_down', 'new_v_g_mix', 'new_v_w_in', 'new_v_g_q_a', 'new_v_w_q_b', 'new_v_g_kv_a', 'new_v_w_kv_b', 'new_v_g_qn', 'new_v_g_kn', 'new_v_w_mla_out', 'new_v_ret_decay_fwd', 'new_v_ret_decay_bwd', 'new_v_w_ret_out', 'new_v_w_out', 'new_v_g_ffn', 'new_v_w_gate_up', 'new_v_w_down']
TWIN_LEAF_KINDS = {'loss': 'loss', 'grad_x': 'grad_x', 'grad_g_mix': 'grad_w', 'grad_w_in': 'grad_w', 'grad_g_q_a': 'grad_w', 'grad_w_q_b': 'grad_w', 'grad_g_kv_a': 'grad_w', 'grad_w_kv_b': 'grad_w', 'grad_g_qn': 'grad_w', 'grad_g_kn': 'grad_w', 'grad_w_mla_out': 'grad_w', 'grad_ret_decay_fwd': 'grad_w', 'grad_ret_decay_bwd': 'grad_w', 'grad_w_ret_out': 'grad_w', 'grad_w_out': 'grad_w', 'grad_g_ffn': 'grad_w', 'grad_w_gate_up': 'grad_w', 'grad_w_down': 'grad_w', 'delta_g_mix': 'delta_w', 'delta_w_in': 'delta_w', 'delta_g_q_a': 'delta_w', 'delta_w_q_b': 'delta_w', 'delta_g_kv_a': 'delta_w', 'delta_w_kv_b': 'delta_w', 'delta_g_qn': 'delta_w', 'delta_g_kn': 'delta_w', 'delta_w_mla_out': 'delta_w', 'delta_ret_decay_fwd': 'delta_w', 'delta_ret_decay_bwd': 'delta_w', 'delta_w_ret_out': 'delta_w', 'delta_w_out': 'delta_w', 'delta_g_ffn': 'delta_w', 'delta_w_gate_up': 'delta_w', 'delta_w_down': 'delta_w', 'new_m_g_mix': 'new_m', 'new_m_w_in': 'new_m', 'new_m_g_q_a': 'new_m', 'new_m_w_q_b': 'new_m', 'new_m_g_kv_a': 'new_m', 'new_m_w_kv_b': 'new_m', 'new_m_g_qn': 'new_m', 'new_m_g_kn': 'new_m', 'new_m_w_mla_out': 'new_m', 'new_m_ret_decay_fwd': 'new_m', 'new_m_ret_decay_bwd': 'new_m', 'new_m_w_ret_out': 'new_m', 'new_m_w_out': 'new_m', 'new_m_g_ffn': 'new_m', 'new_m_w_gate_up': 'new_m', 'new_m_w_down': 'new_m', 'new_v_g_mix': 'new_v', 'new_v_w_in': 'new_v', 'new_v_g_q_a': 'new_v', 'new_v_w_q_b': 'new_v', 'new_v_g_kv_a': 'new_v', 'new_v_w_kv_b': 'new_v', 'new_v_g_qn': 'new_v', 'new_v_g_kn': 'new_v', 'new_v_w_mla_out': 'new_v', 'new_v_ret_decay_fwd': 'new_v', 'new_v_ret_decay_bwd': 'new_v', 'new_v_w_ret_out': 'new_v', 'new_v_w_out': 'new_v', 'new_v_g_ffn': 'new_v', 'new_v_w_gate_up': 'new_v', 'new_v_w_down': 'new_v'}


def _forward(args):
    return _fwd_reference(*[args[k] for k in FWD_PARAMS])


def _output_shape():
    def fwd():
        inp = _fwd_setup_inputs(0)
        return _fwd_reference(*[inp[k] for k in FWD_PARAMS])
    out = _jax.eval_shape(fwd)
    return out.shape, out.dtype

N_MICROBATCH = 1
ADAM_LR = 0.001
ADAM_B1 = 0.9
ADAM_B2 = 0.999
ADAM_EPS = 1e-08
ADAM_WD = 0.01
ADAM_STEP = 10
PER_EXAMPLE_BATCH_AXIS = {'x': 0, 'positions': 0, 'loss_target': 0}
SHARED_INPUTS = []
_WEIGHT_DTYPES = {'g_mix': _jnp.float32, 'w_in': _jnp.float32, 'g_q_a': _jnp.float32, 'w_q_b': _jnp.float32, 'g_kv_a': _jnp.float32, 'w_kv_b': _jnp.float32, 'g_qn': _jnp.float32, 'g_kn': _jnp.float32, 'w_mla_out': _jnp.float32, 'ret_decay_fwd': _jnp.float32, 'ret_decay_bwd': _jnp.float32, 'w_ret_out': _jnp.float32, 'w_out': _jnp.float32, 'g_ffn': _jnp.float32, 'w_gate_up': _jnp.float32, 'w_down': _jnp.float32}
MOMENT_SCALE = {'g_mix': 8.544061e+00, 'w_in': 1.847659e-01, 'g_q_a': 1.031965e-01, 'w_q_b': 4.808091e-02, 'g_kv_a': 7.837317e-01, 'w_kv_b': 6.433503e-02, 'g_qn': 9.076080e-01, 'g_kn': 9.052465e-01, 'w_mla_out': 4.783209e-02, 'ret_decay_fwd': 4.799337e-01, 'ret_decay_bwd': 1.063000e+00, 'w_ret_out': 2.496616e-01, 'w_out': 2.570739e-01, 'g_ffn': 4.947056e+01, 'w_gate_up': 2.610459e-01, 'w_down': 4.446887e-01}


def _to_microbatches(a, axis):
    t = _jnp.moveaxis(a, axis, 0)
    t = t.reshape((N_MICROBATCH, t.shape[0] // N_MICROBATCH) + t.shape[1:])
    return _jnp.moveaxis(t, 1, axis + 1)


def setup_inputs(seed: int = 0) -> dict:
    inp = _fwd_setup_inputs(seed)
    key = _jax.random.fold_in(_jax.random.key(seed), 7919)
    shape, _ = _output_shape()
    out = dict(inp)
    out["loss_target"] = _jax.random.normal(_jax.random.fold_in(key, 0), shape, _jnp.float32)
    for i, name in enumerate(TWIN_WEIGHTS):
        w = inp[name].astype(_jnp.float32)
        if MOMENT_SCALE is None:
            s = _jnp.sqrt(_jnp.mean(_jnp.square(w)) + 1e-30)
        else:
            s = MOMENT_SCALE[name]
        km, kv = _jax.random.split(_jax.random.fold_in(key, i + 1))
        out[name] = w
        out["m_" + name] = s * _jax.random.normal(km, w.shape, _jnp.float32)
        out["v_" + name] = (s * s) * _jax.random.uniform(kv, w.shape, _jnp.float32, 0.5, 1.5)
    if N_MICROBATCH > 1:
        for name, axis in PER_EXAMPLE_BATCH_AXIS.items():
            out[name] = _to_microbatches(out[name], axis)
    return {'x': out['x'], 'positions': out['positions'], 'g_mix': out['g_mix'], 'w_in': out['w_in'], 'g_q_a': out['g_q_a'], 'w_q_b': out['w_q_b'], 'g_kv_a': out['g_kv_a'], 'w_kv_b': out['w_kv_b'], 'g_qn': out['g_qn'], 'g_kn': out['g_kn'], 'w_mla_out': out['w_mla_out'], 'ret_decay_fwd': out['ret_decay_fwd'], 'ret_decay_bwd': out['ret_decay_bwd'], 'w_ret_out': out['w_ret_out'], 'w_out': out['w_out'], 'g_ffn': out['g_ffn'], 'w_gate_up': out['w_gate_up'], 'w_down': out['w_down'], 'loss_target': out['loss_target'], 'm_g_mix': out['m_g_mix'], 'm_w_in': out['m_w_in'], 'm_g_q_a': out['m_g_q_a'], 'm_w_q_b': out['m_w_q_b'], 'm_g_kv_a': out['m_g_kv_a'], 'm_w_kv_b': out['m_w_kv_b'], 'm_g_qn': out['m_g_qn'], 'm_g_kn': out['m_g_kn'], 'm_w_mla_out': out['m_w_mla_out'], 'm_ret_decay_fwd': out['m_ret_decay_fwd'], 'm_ret_decay_bwd': out['m_ret_decay_bwd'], 'm_w_ret_out': out['m_w_ret_out'], 'm_w_out': out['m_w_out'], 'm_g_ffn': out['m_g_ffn'], 'm_w_gate_up': out['m_w_gate_up'], 'm_w_down': out['m_w_down'], 'v_g_mix': out['v_g_mix'], 'v_w_in': out['v_w_in'], 'v_g_q_a': out['v_g_q_a'], 'v_w_q_b': out['v_w_q_b'], 'v_g_kv_a': out['v_g_kv_a'], 'v_w_kv_b': out['v_w_kv_b'], 'v_g_qn': out['v_g_qn'], 'v_g_kn': out['v_g_kn'], 'v_w_mla_out': out['v_w_mla_out'], 'v_ret_decay_fwd': out['v_ret_decay_fwd'], 'v_ret_decay_bwd': out['v_ret_decay_bwd'], 'v_w_ret_out': out['v_w_ret_out'], 'v_w_out': out['v_w_out'], 'v_g_ffn': out['v_g_ffn'], 'v_w_gate_up': out['v_w_gate_up'], 'v_w_down': out['v_w_down']}


def _loss(weights, diff, rest, loss_target):
    with _jax.named_scope("forward"):
        args = {**rest, TWIN_DIFF_INPUT: diff, **{k: w.astype(_WEIGHT_DTYPES[k]) for k, w in weights.items()}}
        y = _forward(args)
    with _jax.named_scope("loss_head"):
        err = _jnp.square(y.astype(_jnp.float32) - loss_target)
        return 0.5 * _jnp.sum(_jnp.mean(err, axis=-1)) if err.ndim else 0.5 * err


def _adamw(w, g, m, v):
    m = ADAM_B1 * m + (1.0 - ADAM_B1) * g
    v = ADAM_B2 * v + (1.0 - ADAM_B2) * _jnp.square(g)
    m_hat = m / (1.0 - ADAM_B1 ** ADAM_STEP)
    v_hat = v / (1.0 - ADAM_B2 ** ADAM_STEP)
    delta = -ADAM_LR * (m_hat / (_jnp.sqrt(v_hat) + ADAM_EPS) + ADAM_WD * w)
    return delta, m, v


def reference(x, positions, g_mix, w_in, g_q_a, w_q_b, g_kv_a, w_kv_b, g_qn, g_kn, w_mla_out, ret_decay_fwd, ret_decay_bwd, w_ret_out, w_out, g_ffn, w_gate_up, w_down, loss_target, m_g_mix, m_w_in, m_g_q_a, m_w_q_b, m_g_kv_a, m_w_kv_b, m_g_qn, m_g_kn, m_w_mla_out, m_ret_decay_fwd, m_ret_decay_bwd, m_w_ret_out, m_w_out, m_g_ffn, m_w_gate_up, m_w_down, v_g_mix, v_w_in, v_g_q_a, v_w_q_b, v_g_kv_a, v_w_kv_b, v_g_qn, v_g_kn, v_w_mla_out, v_ret_decay_fwd, v_ret_decay_bwd, v_w_ret_out, v_w_out, v_g_ffn, v_w_gate_up, v_w_down):
    given = dict(x=x, positions=positions, g_mix=g_mix, w_in=w_in, g_q_a=g_q_a, w_q_b=w_q_b, g_kv_a=g_kv_a, w_kv_b=w_kv_b, g_qn=g_qn, g_kn=g_kn, w_mla_out=w_mla_out, ret_decay_fwd=ret_decay_fwd, ret_decay_bwd=ret_decay_bwd, w_ret_out=w_ret_out, w_out=w_out, g_ffn=g_ffn, w_gate_up=w_gate_up, w_down=w_down, loss_target=loss_target, m_g_mix=m_g_mix, m_w_in=m_w_in, m_g_q_a=m_g_q_a, m_w_q_b=m_w_q_b, m_g_kv_a=m_g_kv_a, m_w_kv_b=m_w_kv_b, m_g_qn=m_g_qn, m_g_kn=m_g_kn, m_w_mla_out=m_w_mla_out, m_ret_decay_fwd=m_ret_decay_fwd, m_ret_decay_bwd=m_ret_decay_bwd, m_w_ret_out=m_w_ret_out, m_w_out=m_w_out, m_g_ffn=m_g_ffn, m_w_gate_up=m_w_gate_up, m_w_down=m_w_down, v_g_mix=v_g_mix, v_w_in=v_w_in, v_g_q_a=v_g_q_a, v_w_q_b=v_w_q_b, v_g_kv_a=v_g_kv_a, v_w_kv_b=v_w_kv_b, v_g_qn=v_g_qn, v_g_kn=v_g_kn, v_w_mla_out=v_w_mla_out, v_ret_decay_fwd=v_ret_decay_fwd, v_ret_decay_bwd=v_ret_decay_bwd, v_w_ret_out=v_w_ret_out, v_w_out=v_w_out, v_g_ffn=v_g_ffn, v_w_gate_up=v_w_gate_up, v_w_down=v_w_down)
    weights = {n: given[n] for n in TWIN_WEIGHTS}
    shared = {n: given[n] for n in SHARED_INPUTS}
    per_example = {n: given[n] for n in ['x', 'positions']}
    grad_fn = _jax.value_and_grad(_loss, argnums=(0, 1))

    def one_microbatch(ex, loss_target):
        ex = dict(ex)
        diff = ex.pop(TWIN_DIFF_INPUT)
        return grad_fn(weights, diff, {**shared, **ex}, loss_target)

    if N_MICROBATCH == 1:
        loss, (grad_w, grad_x) = one_microbatch(per_example, given["loss_target"])
    else:
        def body(carry, xs):
            loss_sum, grad_sum = carry
            l_k, (gw_k, gx_k) = one_microbatch(xs[0], xs[1])
            with _jax.named_scope("update"):
                return (loss_sum + l_k, _jax.tree.map(_jnp.add, grad_sum, gw_k)), gx_k

        init = (_jnp.zeros((), _jnp.float32), _jax.tree.map(_jnp.zeros_like, weights))
        (loss, grad_w), grad_x = _jax.lax.scan(body, init, (per_example, given["loss_target"]))
    with _jax.named_scope("update"):
        delta_w, new_m, new_v = {}, {}, {}
        for n in TWIN_WEIGHTS:
            delta_w[n], new_m[n], new_v[n] = _adamw(weights[n], grad_w[n], given["m_" + n], given["v_" + n])
    return (loss, grad_x, *[grad_w[n] for n in TWIN_WEIGHTS], *[delta_w[n] for n in TWIN_WEIGHTS],
            *[new_m[n] for n in TWIN_WEIGHTS], *[new_v[n] for n in TWIN_WEIGHTS])
```

```python
import functools

import numpy as np
import jax
import jax.numpy as jnp
from jax import lax
from jax.experimental import pallas as pl
from jax.experimental.pallas import tpu as pltpu

F32 = jnp.float32
MXU = jnp.bfloat16
WIRE = jnp.bfloat16
GWIRE = jnp.float32

N_DEV = 8
D_MODEL = 1024
HEADS = 8
LANES = 128
Q_RANK, KV_RANK = 256, 128
NOPE, ROPE_M, V_M = 64, 32, 64
QK_M = NOPE + ROPE_M
RQK, RV = 64, 128
CHUNK = 128
FFN = 2816
IN_WIDTH = 5536
THETA = 10000.0
EPS = 1e-6
LR, B1, B2, AEPS, WD, STEP = 0.001, 0.9, 0.999, 1e-08, 0.01, 10
VMEM_LIMIT = 56 * 1024 * 1024

NN = ((1,), (0,))
NT = ((1,), (1,))
TN = ((0,), (0,))

P_GATES, P_VR, P_GR, P_QR, P_KR, P_CQ, P_CKV, P_KROPE, P_WIDTH = 0, 2048, 3072, 4096, 4608, 5120, 5376, 5504, 5632
O_CQ, O_CKV, O_KROPE, O_QR, O_KR, O_VR, O_GR, O_GATES = 0, 256, 384, 416, 928, 1440, 2464, 3488


def _dot(a, b, dims):
    return lax.dot_general(a, b, (dims, ((), ())), preferred_element_type=F32)


def _pick(dim, cands):
    for c in cands:
        if dim % c == 0:
            return c
    return dim


def _mla_lane_of_dim():
    lane = np.zeros(QK_M, np.int64)
    for d in range(NOPE):
        lane[d] = 16 + d if d < 48 else 80 + (d - 48)
    for r in range(ROPE_M):
        lane[NOPE + r] = r if r < 16 else 64 + (r - 16)
    return lane


def _layout_maps():
    m = {}
    src = -np.ones(P_WIDTH, np.int64)
    src[P_GATES:P_GATES + 2048] = O_GATES + np.arange(2048)
    src[P_VR:P_VR + 1024] = O_VR + np.arange(1024)
    src[P_GR:P_GR + 1024] = O_GR + np.arange(1024)
    for base_p, base_o in ((P_QR, O_QR), (P_KR, O_KR)):
        for j in range(4):
            for s in range(4):
                head = 2 * j + (s % 2)
                d0 = 32 * (s // 2)
                src[base_p + 128 * j + 32 * s + np.arange(32)] = base_o + head * RQK + d0 + np.arange(32)
    src[P_CQ:P_CQ + 256] = O_CQ + np.arange(256)
    src[P_CKV:P_CKV + 128] = O_CKV + np.arange(128)
    src[P_KROPE + np.arange(16)] = O_KROPE + np.arange(16)
    src[P_KROPE + 64 + np.arange(16)] = O_KROPE + 16 + np.arange(16)
    m["win_src"] = src
    inv = np.zeros(IN_WIDTH, np.int64)
    inv[src[src >= 0]] = np.nonzero(src >= 0)[0]
    m["win_inv"] = inv
    lane = _mla_lane_of_dim()
    qsrc = -np.ones(HEADS * LANES, np.int64)
    qinv = np.zeros(HEADS * QK_M, np.int64)
    for h in range(HEADS):
        qsrc[h * LANES + lane] = h * QK_M + np.arange(QK_M)
        qinv[h * QK_M + np.arange(QK_M)] = h * LANES + lane
    m["wq_src"], m["wq_inv"] = qsrc, qinv
    ksrc = -np.ones(HEADS * LANES, np.int64)
    vsrc = -np.ones(HEADS * LANES, np.int64)
    kvinv = np.zeros(HEADS * 128, np.int64)
    for h in range(HEADS):
        ksrc[h * LANES + lane[:NOPE]] = h * 128 + np.arange(NOPE)
        vsrc[h * LANES + np.arange(V_M)] = h * 128 + NOPE + np.arange(V_M)
        kvinv[h * 128 + np.arange(NOPE)] = h * LANES + lane[:NOPE]
        kvinv[h * 128 + NOPE + np.arange(V_M)] = HEADS * LANES + h * LANES + np.arange(V_M)
    m["wk_src"], m["wv_src"], m["wkv_inv"] = ksrc, vsrc, kvinv
    gsrc = -np.ones(LANES, np.int64)
    gsrc[lane] = np.arange(QK_M)
    m["g96_src"], m["g96_inv"] = gsrc, lane
    osrc = -np.ones(HEADS * LANES, np.int64)
    oinv = np.zeros(HEADS * V_M, np.int64)
    for h in range(HEADS):
        osrc[h * LANES + np.arange(V_M)] = h * V_M + np.arange(V_M)
        oinv[h * V_M + np.arange(V_M)] = h * LANES + np.arange(V_M)
    m["wo_src"], m["wo_inv"] = osrc, oinv
    return m


_MAPS = _layout_maps()


def _take_pad(a, src, axis):
    idx = jnp.asarray(np.where(src >= 0, src, 0), jnp.int32)
    out = jnp.take(a, idx, axis=axis)
    shape = [1] * a.ndim
    shape[axis] = src.shape[0]
    mask = jnp.asarray((src >= 0).reshape(shape))
    return jnp.where(mask, out, jnp.zeros((), a.dtype))


def _rowwise(name, fn, rows, ts, ins, outs, accs=(), ncol=1):
    n_in, n_out, n_acc = len(ins), len(outs), len(accs)

    def colmap(col):
        if callable(col):
            return lambda i, j: (i, col(j))
        return lambda i, j: (i, col)

    arrays, in_specs = [], []
    for arr, spec in ins:
        arrays.append(arr)
        if spec is None:
            in_specs.append(pl.BlockSpec(arr.shape, functools.partial(lambda i, j, nd: (0,) * nd, nd=arr.ndim)))
        else:
            in_specs.append(pl.BlockSpec((ts, spec[0]), colmap(spec[1])))
    out_shape, out_specs = [], []
    for total, dtype, width, col in outs:
        out_shape.append(jax.ShapeDtypeStruct((rows, total), dtype))
        out_specs.append(pl.BlockSpec((ts, width), colmap(col)))
    for shp in accs:
        out_shape.append(jax.ShapeDtypeStruct(shp, F32))
        out_specs.append(pl.BlockSpec(shp, functools.partial(lambda i, j, nd: (0,) * nd, nd=len(shp))))

    def body(*refs):
        vals = [r[...] for r in refs[:n_in]]
        res = fn(*vals)
        if not isinstance(res, (tuple, list)):
            res = (res,)
        for r, v in zip(refs[n_in:n_in + n_out], res[:n_out]):
            r[...] = v.astype(r.dtype)
        if n_acc:
            first = jnp.logical_and(pl.program_id(0) == 0, pl.program_id(1) == 0)
            for r, v in zip(refs[n_in + n_out:], res[n_out:]):
                @pl.when(first)
                def _(r=r):
                    r[...] = jnp.zeros_like(r)
                r[...] += v.astype(F32)

    res = pl.pallas_call(
        body, name=name, grid=(rows // ts, ncol), in_specs=in_specs, out_specs=out_specs, out_shape=out_shape,
        compiler_params=pltpu.CompilerParams(dimension_semantics=("arbitrary", "arbitrary"), vmem_limit_bytes=VMEM_LIMIT),
    )(*arrays)
    return res


def _mm(name, a, b, mode, add=None, out_dtype=F32):
    if mode == "nn":
        (M, K), N = a.shape, b.shape[1]
    elif mode == "nt":
        (M, K), N = a.shape, b.shape[0]
    else:
        (K, M), N = a.shape, b.shape[1]
    tm = _pick(M, (1024, 512, 256, 128))
    tn = _pick(N, (512, 256, 128))
    if mode == "tn":
        tm = _pick(M, (512, 256, 128))
        tk = _pick(K, (512, 256, 128))
    else:
        tk = K if K <= 2816 else _pick(K, (1024, 512, 256, 128))
    nk = K // tk
    dims = {"nn": NN, "nt": NT, "tn": TN}[mode]
    a_spec = pl.BlockSpec((tk, tm), lambda i, j, k: (k, i)) if mode == "tn" else pl.BlockSpec((tm, tk), lambda i, j, k: (i, k))
    b_spec = pl.BlockSpec((tn, tk), lambda i, j, k: (j, k)) if mode == "nt" else pl.BlockSpec((tk, tn), lambda i, j, k: (k, j))
    o_spec = pl.BlockSpec((tm, tn), lambda i, j, k: (i, j))
    has_add = add is not None

    def body(*refs):
        a_ref, b_ref = refs[0], refs[1]
        add_ref = refs[2] if has_add else None
        o_ref, acc_ref = refs[-2], refs[-1]
        k = pl.program_id(2)

        @pl.when(k == 0)
        def _():
            acc_ref[...] = jnp.zeros_like(acc_ref)

        acc_ref[...] += _dot(a_ref[...], b_ref[...], dims)

        @pl.when(k == nk - 1)
        def _():
            r = acc_ref[...]
            if has_add:
                r = r + add_ref[...]
            o_ref[...] = r.astype(o_ref.dtype)

    args = [a, b] + ([add] if has_add else [])
    specs = [a_spec, b_spec] + ([o_spec] if has_add else [])
    return pl.pallas_call(
        body, name=name, grid=(M // tm, N // tn, nk), in_specs=specs, out_specs=o_spec,
        out_shape=jax.ShapeDtypeStruct((M, N), out_dtype), scratch_shapes=[pltpu.VMEM((tm, tn), F32)],
        compiler_params=pltpu.CompilerParams(dimension_semantics=("parallel", "parallel", "arbitrary"), vmem_limit_bytes=VMEM_LIMIT),
    )(*args)


@jax.custom_vjp
def _swap64(x):
    return pltpu.roll(x, 64, 1)


_swap64.defvjp(lambda x: (_swap64(x), None), lambda _, g: (_swap64(g),))


@jax.custom_vjp
def _mxdot(a, b):
    return _dot(a.astype(MXU), b.astype(MXU), NN)


def _mxdot_bwd(res, g):
    a, b = res
    gb = g.astype(MXU)
    return _dot(gb, b.astype(MXU), NT), _dot(a.astype(MXU), gb, TN)


_mxdot.defvjp(lambda a, b: (_mxdot(a, b), (a, b)), _mxdot_bwd)


def _rms(x):
    return x * lax.rsqrt(jnp.mean(x * x, axis=-1, keepdims=True) + EPS)


def _rmsg_fn(x, g):
    return _rms(x) * g


def _silu(x):
    return x * jax.nn.sigmoid(x)


def _tables_fn(pos, inv_m, sgn_m, inv_r, sgn_r):
    am, ar = pos * inv_m, pos * inv_r
    return jnp.cos(am), jnp.sin(am) * sgn_m, jnp.cos(ar), jnp.sin(ar) * sgn_r


def _head_blocks(t):
    return [t[:, LANES * h:LANES * (h + 1)] for h in range(t.shape[1] // LANES)]


def _mla_prep_fn(cq, ckv, kr, cosm, sinm, gqa, gkva, gqn, gkn, wq, wk, wv):
    cqn = _rms(cq) * gqa
    ckvn = _rms(ckv) * gkva
    q_raw = _mxdot(cqn, wq)
    k_raw = _mxdot(ckvn, wk)
    v = _mxdot(ckvn, wv)

    def norm_rope(blocks, g, extra):
        outs = []
        for b in blocks:
            if extra is not None:
                b = b + extra
            n = b * lax.rsqrt(jnp.sum(b * b, axis=-1, keepdims=True) * (1.0 / QK_M) + EPS) * g
            outs.append(n * cosm + _swap64(n) * sinm)
        return jnp.concatenate(outs, axis=1)

    q = norm_rope(_head_blocks(q_raw), gqn, None)
    k = norm_rope(_head_blocks(k_raw), gkn, kr)
    return q, k, v


def _ret_prep_fn(qr, kr, cosr, sinr):
    def rope(t, scale):
        return jnp.concatenate([(b * cosr + _swap64(b) * sinr) * scale for b in _head_blocks(t)], axis=1)
    return rope(qr, 1.0), rope(kr, RQK ** -0.5)


def _ret_post_fn(rf, rb, gr):
    ret = rf + rb
    outs = []
    for b, g in zip(_head_blocks(ret), _head_blocks(gr)):
        outs.append(_silu(g) * _rms(b))
    return jnp.concatenate(outs, axis=1)


def _merge_fn(ga, gb, ya, yb):
    return jax.nn.sigmoid(ga) * ya + jax.nn.sigmoid(gb) * yb


def _swiglu_fn(gate, up):
    return _silu(gate) * up


def _loss_fn(x2, tgt):
    d = x2 - tgt
    return d * (1.0 / D_MODEL), 0.5 * jnp.sum(d * d, axis=0, keepdims=True) * (1.0 / D_MODEL)


def _adamw_fn(parts, w, m, v):
    g = parts[0]
    for p in range(1, N_DEV):
        g = g + parts[p]
    g = g.astype(F32)
    m2 = B1 * m + (1.0 - B1) * g
    v2 = B2 * v + (1.0 - B2) * jnp.square(g)
    m_hat = m2 / (1.0 - B1 ** STEP)
    v_hat = v2 / (1.0 - B2 ** STEP)
    delta = -LR * (m_hat / (jnp.sqrt(v_hat) + AEPS) + WD * w)
    return g, delta, m2, v2


def _flash_fwd(q, k, v):
    S = q.shape[0]
    tq, tk = _pick(S, (512, 256, 128)), _pick(S, (512, 256, 128))
    nkv = S // tk
    scale = QK_M ** -0.5

    def body(q_ref, k_ref, v_ref, o_ref, obf_ref, lse_ref, m_sc, l_sc, acc_sc):
        ki = pl.program_id(2)

        @pl.when(ki == 0)
        def _():
            m_sc[...] = jnp.full_like(m_sc, -jnp.inf)
            l_sc[...] = jnp.zeros_like(l_sc)
            acc_sc[...] = jnp.zeros_like(acc_sc)

        s = _dot(q_ref[...], k_ref[...], NT) * scale
        m_prev = m_sc[...]
        m_new = jnp.maximum(m_prev, jnp.max(s, axis=-1, keepdims=True))
        alpha = jnp.exp(m_prev - m_new)
        p = jnp.exp(s - m_new)
        l_sc[...] = alpha * l_sc[...] + jnp.sum(p, axis=-1, keepdims=True)
        acc_sc[...] = alpha * acc_sc[...] + _dot(p.astype(MXU), v_ref[...], NN)
        m_sc[...] = m_new

        @pl.when(ki == nkv - 1)
        def _():
            o = acc_sc[...] / l_sc[...]
            o_ref[...] = o
            obf_ref[...] = o.astype(MXU)
            lse_ref[0] = m_sc[...] + jnp.log(l_sc[...])

    qspec = pl.BlockSpec((tq, LANES), lambda h, i, j: (i, h))
    kspec = pl.BlockSpec((tk, LANES), lambda h, i, j: (j, h))
    return pl.pallas_call(
        body, name="flash_fwd", grid=(HEADS, S // tq, nkv), in_specs=[qspec, kspec, kspec],
        out_specs=[qspec, qspec, pl.BlockSpec((1, tq, 1), lambda h, i, j: (h, i, 0))],
        out_shape=[jax.ShapeDtypeStruct((S, HEADS * LANES), F32), jax.ShapeDtypeStruct((S, HEADS * LANES), MXU),
                   jax.ShapeDtypeStruct((HEADS, S, 1), F32)],
        scratch_shapes=[pltpu.VMEM((tq, 1), F32), pltpu.VMEM((tq, 1), F32), pltpu.VMEM((tq, LANES), F32)],
        compiler_params=pltpu.CompilerParams(dimension_semantics=("parallel", "parallel", "arbitrary"), vmem_limit_bytes=VMEM_LIMIT),
    )(q, k, v)


def _flash_bwd(q, k, v, o, do, lse):
    S = q.shape[0]
    tq, tk = _pick(S, (512, 256, 128)), _pick(S, (512, 256, 128))
    scale = QK_M ** -0.5

    def body(q_ref, k_ref, v_ref, o_ref, do_ref, lse_ref, dq_ref, dk_ref, dv_ref):
        kj, qi = pl.program_id(1), pl.program_id(2)

        @pl.when(jnp.logical_and(kj == 0, qi == 0))
        def _():
            dq_ref[...] = jnp.zeros_like(dq_ref)

        @pl.when(qi == 0)
        def _():
            dk_ref[...] = jnp.zeros_like(dk_ref)
            dv_ref[...] = jnp.zeros_like(dv_ref)

        qb, kb, vb = q_ref[...], k_ref[...], v_ref[...]
        do = do_ref[...]
        do_bf = do.astype(MXU)
        s = _dot(qb, kb, NT) * scale
        p = jnp.exp(s - lse_ref[0])
        dv_ref[...] += _dot(p.astype(MXU), do_bf, TN)
        dp = _dot(do_bf, vb, NT)
        delta = jnp.sum(do * o_ref[...], axis=-1, keepdims=True)
        ds = (p * (dp - delta) * scale).astype(MXU)
        dk_ref[...] += _dot(ds, qb, TN)
        rows = pl.ds(pl.multiple_of(qi * tq, tq), tq)
        dq_ref[rows, :] += _dot(ds, kb, NN)

    qspec = pl.BlockSpec((tq, LANES), lambda h, j, i: (i, h))
    kspec = pl.BlockSpec((tk, LANES), lambda h, j, i: (j, h))
    full = jax.ShapeDtypeStruct((S, HEADS * LANES), F32)
    return pl.pallas_call(
        body, name="flash_bwd", grid=(HEADS, S // tk, S // tq),
        in_specs=[qspec, kspec, kspec, qspec, qspec, pl.BlockSpec((1, tq, 1), lambda h, j, i: (h, i, 0))],
        out_specs=[pl.BlockSpec((S, LANES), lambda h, j, i: (0, h)), kspec, kspec],
        out_shape=[full, full, full],
        compiler_params=pltpu.CompilerParams(dimension_semantics=("parallel", "arbitrary", "arbitrary"), vmem_limit_bytes=VMEM_LIMIT),
    )(q, k, v, o, do, lse)


def _ret_consts(lgh, head, rev):
    C = CHUNK
    lane = lax.broadcasted_iota(jnp.int32, (1, LANES), 1)
    hm = ((lane // 32) % 2 == head % 2).astype(F32)
    r = lax.broadcasted_iota(jnp.int32, (C, C), 0)
    c = lax.broadcasted_iota(jnp.int32, (C, C), 1)
    diff = ((c - r) if rev else (r - c)).astype(F32)
    mask = (diff > 0) if rev else (diff >= 0)
    dpos = jnp.maximum(diff, 0.0)
    din = jnp.where(mask, jnp.exp(lgh * dpos), 0.0)
    idx = lax.broadcasted_iota(jnp.int32, (C, 1), 0).astype(F32)
    eq = (C - idx) if rev else (idx + 1.0)
    ek = idx if rev else (C - 1.0 - idx)
    qd, kd = jnp.exp(lgh * eq), jnp.exp(lgh * ek)
    cd = jnp.exp(lgh * jnp.full((1, 1), float(C), F32))
    return hm, din, dpos, qd, kd, cd, eq, ek


def _ret_fwd(name, qt, kt, proj, lg, rev):
    S = qt.shape[0]
    C = CHUNK
    TB = _pick(S, (512, 256, 128))
    cb, nb = TB // C, S // TB
    blk = (lambda g: nb - 1 - g) if rev else (lambda g: g)

    def body(lg_ref, q_ref, k_ref, v_ref, o_ref, st_ref, state_sc):
        h, g = pl.program_id(0), pl.program_id(1)

        @pl.when(g == 0)
        def _():
            state_sc[...] = jnp.zeros_like(state_sc)

        hm, din, _, qd, kd, cd, _, _ = _ret_consts(lg_ref[h], h, rev)
        for cc in (reversed(range(cb)) if rev else range(cb)):
            rows = pl.ds(cc * C, C)
            q, k, v = q_ref[rows, :] * hm, k_ref[rows, :] * hm, v_ref[rows, :].astype(MXU)
            st = state_sc[...]
            st_ref[0, cc] = st
            a = _dot(q.astype(MXU), k.astype(MXU), NT) * din
            inner = _dot(a.astype(MXU), v, NN)
            cross = _dot((q * qd).astype(MXU), st.astype(MXU), NN)
            o_ref[rows, :] = inner + cross
            state_sc[...] = st * cd + _dot((k * kd).astype(MXU), v, TN)

    return pl.pallas_call(
        body, name=name, grid=(HEADS, nb),
        in_specs=[pl.BlockSpec(memory_space=pltpu.SMEM),
                  pl.BlockSpec((TB, LANES), lambda h, g: (blk(g), h // 2)),
                  pl.BlockSpec((TB, LANES), lambda h, g: (blk(g), h // 2)),
                  pl.BlockSpec((TB, LANES), lambda h, g: (blk(g), P_VR // LANES + h))],
        out_specs=[pl.BlockSpec((TB, LANES), lambda h, g: (blk(g), h)),
                   pl.BlockSpec((1, cb, LANES, LANES), lambda h, g: (h, blk(g), 0, 0))],
        out_shape=[jax.ShapeDtypeStruct((S, HEADS * LANES), F32), jax.ShapeDtypeStruct((HEADS, S // C, LANES, LANES), F32)],
        scratch_shapes=[pltpu.VMEM((LANES, LANES), F32)],
        compiler_params=pltpu.CompilerParams(dimension_semantics=("parallel", "arbitrary"), vmem_limit_bytes=VMEM_LIMIT),
    )(lg, qt, kt, proj)


def _ret_bwd(name, qt, kt, proj, dret, states, lg, rev):
    S = qt.shape[0]
    C = CHUNK
    TB = _pick(S, (512, 256, 128))
    cb, nb = TB // C, S // TB
    blk = (lambda g: g) if rev else (lambda g: nb - 1 - g)

    def body(lg_ref, q_ref, k_ref, v_ref, do_ref, st_ref, dq_ref, dk_ref, dv_ref, dlg_ref, ds_sc, acc_cc, acc_q, acc_k, acc_s):
        h, g = pl.program_id(0), pl.program_id(1)

        @pl.when(g == 0)
        def _():
            ds_sc[...] = jnp.zeros_like(ds_sc)
            acc_cc[...] = jnp.zeros_like(acc_cc)
            acc_q[...] = jnp.zeros_like(acc_q)
            acc_k[...] = jnp.zeros_like(acc_k)
            acc_s[...] = jnp.zeros_like(acc_s)

        lgh = lg_ref[h]
        hm, din, dpos, qd, kd, cd, eq, ek = _ret_consts(lgh, h, rev)
        for cc in (range(cb) if rev else reversed(range(cb))):
            rows = pl.ds(cc * C, C)
            q, k = q_ref[rows, :] * hm, k_ref[rows, :] * hm
            qb, kb, vb = q.astype(MXU), k.astype(MXU), v_ref[rows, :].astype(MXU)
            dob = do_ref[rows, :].astype(MXU)
            st = st_ref[0, cc]
            dsn = ds_sc[...]
            dsnb = dsn.astype(MXU)
            a = _dot(qb, kb, NT)
            dp = _dot(dob, vb, NT)
            da = (dp * din).astype(MXU)
            dqs = _dot(dob, st.astype(MXU), NT)
            vds = _dot(vb, dsnb, NT)
            dq_ref[rows, :] = (_dot(da, kb, NN) + dqs * qd) * hm
            dk_ref[rows, :] = (_dot(da, qb, TN) + vds * kd) * hm
            dv_ref[rows, :] = _dot((a * din).astype(MXU), dob, TN) + _dot((k * kd).astype(MXU), dsnb, NN)
            ds_sc[...] = dsn * cd + _dot((q * qd).astype(MXU), dob, TN)
            acc_cc[...] += dp * a * din * dpos
            acc_q[...] += dqs * q * (qd * eq)
            acc_k[...] += vds * k * (kd * ek)
            acc_s[...] += dsn * st * (cd * float(C))

        @pl.when(g == nb - 1)
        def _():
            tot = (jnp.sum(acc_cc[...], keepdims=True) + jnp.sum(acc_q[...], keepdims=True)
                   + jnp.sum(acc_k[...], keepdims=True) + jnp.sum(acc_s[...], keepdims=True))
            dlg_ref[0] = jnp.broadcast_to(tot * lgh, (8, LANES))

    full = jax.ShapeDtypeStruct((S, HEADS * LANES), F32)
    hspec = pl.BlockSpec((TB, LANES), lambda h, g: (blk(g), h))
    return pl.pallas_call(
        body, name=name, grid=(HEADS, nb),
        in_specs=[pl.BlockSpec(memory_space=pltpu.SMEM),
                  pl.BlockSpec((TB, LANES), lambda h, g: (blk(g), h // 2)),
                  pl.BlockSpec((TB, LANES), lambda h, g: (blk(g), h // 2)),
                  pl.BlockSpec((TB, LANES), lambda h, g: (blk(g), P_VR // LANES + h)),
                  hspec,
                  pl.BlockSpec((1, cb, LANES, LANES), lambda h, g: (h, blk(g), 0, 0))],
        out_specs=[hspec, hspec, hspec, pl.BlockSpec((1, 8, LANES), lambda h, g: (h, 0, 0))],
        out_shape=[full, full, full, jax.ShapeDtypeStruct((HEADS, 8, LANES), F32)],
        scratch_shapes=[pltpu.VMEM((LANES, LANES), F32), pltpu.VMEM((C, C), F32), pltpu.VMEM((C, LANES), F32),
                        pltpu.VMEM((C, LANES), F32), pltpu.VMEM((LANES, LANES), F32)],
        compiler_params=pltpu.CompilerParams(dimension_semantics=("parallel", "arbitrary"), vmem_limit_bytes=VMEM_LIMIT),
    )(lg, qt, kt, proj, dret, states)


def _rope_consts():
    inv16 = THETA ** (-jnp.arange(16, dtype=F32) / 16)
    inv32 = THETA ** (-jnp.arange(32, dtype=F32) / 32)
    lane = np.arange(LANES)
    in_r1, in_r2 = lane < 16, (lane >= 64) & (lane < 80)
    inv_m = jnp.where(jnp.asarray(in_r1 | in_r2), jnp.take(inv16, jnp.asarray(lane % 16, jnp.int32)), 0.0)[None, :]
    sgn_m = jnp.asarray(np.where(in_r1, -1.0, np.where(in_r2, 1.0, 0.0)), F32)[None, :]
    inv_r = jnp.take(inv32, jnp.asarray(lane % 32, jnp.int32))[None, :]
    sgn_r = jnp.asarray(np.where(lane < 64, -1.0, 1.0), F32)[None, :]
    return inv_m, sgn_m, inv_r, sgn_r


def _local_step(x, pos, tgt, gains, W):
    S = x.shape[0]
    ts = _pick(S, (256, 128))
    ts_wide = _pick(S, (128,))
    R = lambda a, w=None, c=0: (a, ((a.shape[1] if w is None else w), c))
    W_ = lambda a: (a, None)

    win = _take_pad(W["w_in"], _MAPS["win_src"], 1)
    wq = _take_pad(W["w_q_b"], _MAPS["wq_src"], 1)
    wk = _take_pad(W["w_kv_b"], _MAPS["wk_src"], 1)
    wv = _take_pad(W["w_kv_b"], _MAPS["wv_src"], 1)
    wmla = _take_pad(W["w_mla_out"], _MAPS["wo_src"], 0)
    wret, wout, wgu, wdown = W["w_ret_out"], W["w_out"], W["w_gate_up"], W["w_down"]
    gqn = _take_pad(gains["g_qn"], _MAPS["g96_src"], 1)
    gkn = _take_pad(gains["g_kn"], _MAPS["g96_src"], 1)
    g_mix, g_q_a, g_kv_a, g_ffn = gains["g_mix"], gains["g_q_a"], gains["g_kv_a"], gains["g_ffn"]
    lg_f = -jnp.exp(gains["ret_decay_fwd"][0])
    lg_b = -jnp.exp(gains["ret_decay_bwd"][0])

    consts = list(_rope_consts())
    cosm, sinm, cosr, sinr = _rowwise("rope_tables", _tables_fn, S, ts, [R(pos)] + [W_(c) for c in consts],
                                      [(LANES, F32, LANES, 0)] * 4)

    (h,) = _rowwise("rms_mix", _rmsg_fn, S, ts, [R(x), W_(g_mix)], [(D_MODEL, MXU, D_MODEL, 0)])
    proj = _mm("in_proj", h, win, "nn")
    seg = lambda off, w: (proj, (w, off // w))
    mla_ins = [seg(P_CQ, 256), seg(P_CKV, 128), seg(P_KROPE, 128), R(cosm), R(sinm),
               W_(g_q_a), W_(g_kv_a), W_(gqn), W_(gkn), W_(wq), W_(wk), W_(wv)]
    q, k, v = _rowwise("mla_prep", _mla_prep_fn, S, ts, mla_ins, [(HEADS * LANES, MXU, HEADS * LANES, 0)] * 3)
    o, o_bf, lse = _flash_fwd(q, k, v)
    y_a = _mm("mla_out", o_bf, wmla, "nn")

    ret_ins = [seg(P_QR, 512), seg(P_KR, 512), R(cosr), R(sinr)]
    qt, kt = _rowwise("ret_prep", _ret_prep_fn, S, ts, ret_ins, [(512, F32, 512, 0)] * 2)
    ret_f, st_f = _ret_fwd("ret_fwd_f", qt, kt, proj, lg_f, False)
    ret_b, st_b = _ret_fwd("ret_fwd_b", qt, kt, proj, lg_b, True)
    post_ins = [R(ret_f), R(ret_b), seg(P_GR, 1024)]
    (o_b,) = _rowwise("ret_post", _ret_post_fn, S, ts, post_ins, [(1024, MXU, 1024, 0)])
    y_b = _mm("ret_out", o_b, wret, "nn")

    merge_ins = [seg(P_GATES, 1024), (proj, (1024, 1)), R(y_a), R(y_b)]
    (merged,) = _rowwise("merge", _merge_fn, S, ts, merge_ins, [(D_MODEL, MXU, D_MODEL, 0)])
    x1 = _mm("out_proj", merged, wout, "nn", add=x)
    (h2,) = _rowwise("rms_ffn", _rmsg_fn, S, ts, [R(x1), W_(g_ffn)], [(D_MODEL, MXU, D_MODEL, 0)])
    gu = _mm("gate_up", h2, wgu, "nn")
    nfc = FFN // 256
    glu_ins = [(gu, (256, lambda j: j)), (gu, (256, lambda j: nfc + j))]
    (act,) = _rowwise("swiglu", _swiglu_fn, S, ts, glu_ins, [(FFN, MXU, 256, lambda j: j)], ncol=nfc)
    x2 = _mm("down_proj", act, wdown, "nn", add=x1)
    dx2, dx2_bf, loss_rows = _rowwise("loss", lambda a, b: (lambda d, l: (d, d, l))(*_loss_fn(a, b)), S, ts, [R(x2), R(tgt)],
                                      [(D_MODEL, F32, D_MODEL, 0), (D_MODEL, MXU, D_MODEL, 0)], accs=[(1, D_MODEL)])

    gW = {}
    gW["w_down"] = _mm("d_w_down", act, dx2_bf, "tn")
    dact = _mm("d_act", dx2_bf, wdown, "nt")

    def glu_bwd(gate, up, da):
        _, vjp = jax.vjp(_swiglu_fn, gate, up)
        return vjp(da)

    dgu_ins = glu_ins + [(dact, (256, lambda j: j))]
    dgate, dup = _rowwise("swiglu_bwd", glu_bwd, S, ts, dgu_ins, [(FFN, MXU, 256, lambda j: j)] * 2, ncol=nfc)
    dgu = jnp.concatenate([dgate, dup], axis=1)
    gW["w_gate_up"] = _mm("d_w_gate_up", h2, dgu, "tn")
    dh2 = _mm("d_h2", dgu, wgu, "nt")

    def rms_bwd(xx, g, dh, dres):
        _, vjp = jax.vjp(_rmsg_fn, xx, g)
        dx, dg = vjp(dh)
        dx = dx + dres
        return dx, dx, dg

    dx1, dx1_bf, dg_ffn = _rowwise("rms_ffn_bwd", rms_bwd, S, ts, [R(x1), W_(g_ffn), R(dh2), R(dx2)],
                                   [(D_MODEL, F32, D_MODEL, 0), (D_MODEL, MXU, D_MODEL, 0)], accs=[(1, D_MODEL)])
    gW["w_out"] = _mm("d_w_out", merged, dx1_bf, "tn")
    dmerged = _mm("d_merged", dx1_bf, wout, "nt")

    def merge_bwd(ga, gb, ya, yb, dm):
        _, vjp = jax.vjp(_merge_fn, ga, gb, ya, yb)
        return vjp(dm)

    dga, dgb, dy_a, dy_b = _rowwise("merge_bwd", merge_bwd, S, ts, merge_ins + [R(dmerged)], [(D_MODEL, MXU, D_MODEL, 0)] * 4)
    gW["w_ret_out"] = _mm("d_w_ret_out", o_b, dy_b, "tn")
    do_b = _mm("d_o_b", dy_b, wret, "nt")

    def post_bwd(rf, rb, gr, dob):
        _, vjp = jax.vjp(_ret_post_fn, rf, rb, gr)
        drf, _, dgr = vjp(dob)
        return drf, dgr

    dret, dg_r = _rowwise("ret_post_bwd", post_bwd, S, ts, post_ins + [R(do_b)], [(1024, F32, 1024, 0), (1024, MXU, 1024, 0)])
    dq_f, dk_f, dv_f, dlg_f = _ret_bwd("ret_bwd_f", qt, kt, proj, dret, st_f, lg_f, False)
    dq_b, dk_b, dv_b, dlg_b = _ret_bwd("ret_bwd_b", qt, kt, proj, dret, st_b, lg_b, True)

    def ret_prep_bwd(qr, kr, cosr_, sinr_, dqf, dqb, dkf, dkb, dvf, dvb):
        _, vjp = jax.vjp(lambda a, b: _ret_prep_fn(a, b, cosr_, sinr_), qr, kr)
        pair = lambda t: jnp.concatenate([t[:, 256 * j:256 * j + 128] + t[:, 256 * j + 128:256 * j + 256] for j in range(4)], axis=1)
        dqr, dkr = vjp((pair(dqf + dqb), pair(dkf + dkb)))
        return dqr, dkr, dvf + dvb

    dq_r, dk_r, dv_r = _rowwise("ret_prep_bwd", ret_prep_bwd, S, ts, ret_ins + [R(t) for t in (dq_f, dq_b, dk_f, dk_b, dv_f, dv_b)],
                                [(512, MXU, 512, 0), (512, MXU, 512, 0), (1024, MXU, 1024, 0)])

    gW_mla_p = _mm("d_w_mla_out", o_bf, dy_a, "tn")
    do = _mm("d_o", dy_a, wmla, "nt")
    dq, dk, dv = _flash_bwd(q, k, v, o, do, lse)

    def mla_prep_bwd(cq, ckv, kr, cosm_, sinm_, gqa, gkva, gqn_, gkn_, wq_, wk_, wv_, dq_, dk_, dv_):
        f = lambda cq, ckv, kr, gqa, gkva, gqn_, gkn_, wq_, wk_, wv_: _mla_prep_fn(cq, ckv, kr, cosm_, sinm_, gqa, gkva, gqn_, gkn_, wq_, wk_, wv_)
        _, vjp = jax.vjp(f, cq, ckv, kr, gqa, gkva, gqn_, gkn_, wq_.astype(F32), wk_.astype(F32), wv_.astype(F32))
        return vjp((dq_, dk_, dv_))

    mb = _rowwise("mla_prep_bwd", mla_prep_bwd, S, ts_wide, mla_ins + [R(dq), R(dk), R(dv)],
                  [(256, MXU, 256, 0), (128, MXU, 128, 0), (128, MXU, 128, 0)],
                  accs=[(1, 256), (1, 128), (1, LANES), (1, LANES), (256, HEADS * LANES), (128, HEADS * LANES), (128, HEADS * LANES)])
    dc_q, dc_kv, dk_rope, dg_q_a, dg_kv_a, dgqn_p, dgkn_p, dwq_p, dwk_p, dwv_p = mb

    dproj = jnp.concatenate([dga, dgb, dv_r, dg_r, dq_r, dk_r, dc_q, dc_kv, dk_rope], axis=1)
    gwin_p = _mm("d_w_in", h, dproj, "tn")
    dh = _mm("d_h", dproj, win, "nt")
    grad_x, _, dg_mix = _rowwise("rms_mix_bwd", rms_bwd, S, ts, [R(x), W_(g_mix), R(dh), R(dx1)],
                                 [(D_MODEL, F32, D_MODEL, 0), (D_MODEL, MXU, D_MODEL, 0)], accs=[(1, D_MODEL)])

    gW["w_in"] = jnp.take(gwin_p, jnp.asarray(_MAPS["win_inv"], jnp.int32), axis=1)
    gW["w_q_b"] = jnp.take(dwq_p, jnp.asarray(_MAPS["wq_inv"], jnp.int32), axis=1)
    gW["w_kv_b"] = jnp.take(jnp.concatenate([dwk_p, dwv_p], axis=1), jnp.asarray(_MAPS["wkv_inv"], jnp.int32), axis=1)
    gW["w_mla_out"] = jnp.take(gW_mla_p, jnp.asarray(_MAPS["wo_inv"], jnp.int32), axis=0)
    lane96 = jnp.asarray(_MAPS["g96_inv"], jnp.int32)
    gG = {"g_mix": dg_mix, "g_q_a": dg_q_a, "g_kv_a": dg_kv_a, "g_qn": jnp.take(dgqn_p, lane96, axis=1),
          "g_kn": jnp.take(dgkn_p, lane96, axis=1), "ret_decay_fwd": dlg_f[:, 0, 0][None, :], "ret_decay_bwd": dlg_b[:, 0, 0][None, :],
          "g_ffn": dg_ffn}
    return loss_rows, grad_x, gG, gW


MATS = [("w_in", (1024, 5536), 1), ("w_q_b", (256, 768), 1), ("w_kv_b", (128, 1024), 1), ("w_mla_out", (512, 1024), 1),
        ("w_ret_out", (1024, 1024), 0), ("w_out", (1024, 1024), 0), ("w_gate_up", (1024, 5632), 1), ("w_down", (2816, 1024), 0)]
GAINS = [("g_mix", 1024), ("g_q_a", 256), ("g_kv_a", 128), ("g_qn", 96), ("g_kn", 96), ("ret_decay_fwd", 8), ("ret_decay_bwd", 8),
         ("g_ffn", 1024)]
ORDER = ["g_mix", "w_in", "g_q_a", "w_q_b", "g_kv_a", "w_kv_b", "g_qn", "g_kn", "w_mla_out", "ret_decay_fwd", "ret_decay_bwd",
         "w_ret_out", "w_out", "g_ffn", "w_gate_up", "w_down"]
MAT_LEN = sum(s[0] * s[1] // N_DEV for _, s, _ in MATS)
GAIN_LEN = sum(n for _, n in GAINS)
PACK_COLS = 1024
PACK_ROWS = -(-(MAT_LEN + GAIN_LEN) // (PACK_COLS * 16)) * 16
PACK_LEN = PACK_ROWS * PACK_COLS


def _shard_shape(shape, axis):
    return (shape[0] // N_DEV, shape[1]) if axis == 0 else (shape[0], shape[1] // N_DEV)


def _pack_shards(mats, gains, dtype):
    parts = [mats[n].reshape(-1).astype(dtype) for n, _, _ in MATS] + [gains[n].reshape(-1).astype(dtype) for n, _ in GAINS]
    flat = jnp.concatenate(parts)
    return jnp.pad(flat, (0, PACK_LEN - flat.shape[0])).reshape(PACK_ROWS, PACK_COLS)


def _unpack_shards(packed):
    flat = packed.reshape(-1)
    out, off = {}, 0
    for n, shape, axis in MATS:
        shp = _shard_shape(shape, axis)
        out[n] = flat[off:off + shp[0] * shp[1]].reshape(shp)
        off += shp[0] * shp[1]
    for n, ln in GAINS:
        out[n] = flat[off:off + ln]
        off += ln
    return out


def _unpack_gathered(gathered):
    flat = gathered.reshape(N_DEV, -1)
    out, off = {}, 0
    for n, shape, axis in MATS:
        shp = _shard_shape(shape, axis)
        piece = flat[:, off:off + shp[0] * shp[1]].reshape((N_DEV,) + shp)
        out[n] = piece.reshape(shape) if axis == 0 else piece.transpose(1, 0, 2).reshape(shape)
        off += shp[0] * shp[1]
    return out


def _pack_full_grads(gW, gG, dtype):
    parts = []
    for n, shape, axis in MATS:
        shp = _shard_shape(shape, axis)
        g = gW[n].astype(dtype)
        piece = g.reshape((N_DEV,) + shp) if axis == 0 else g.reshape(shape[0], N_DEV, shp[1]).transpose(1, 0, 2)
        parts.append(piece.reshape(N_DEV, -1))
    for n, ln in GAINS:
        parts.append(jnp.broadcast_to(gG[n].reshape(1, ln).astype(dtype), (N_DEV, ln)))
    flat = jnp.concatenate(parts, axis=1)
    return jnp.pad(flat, ((0, 0), (0, PACK_LEN - flat.shape[1]))).reshape(N_DEV, PACK_ROWS, PACK_COLS)


def _all_gather(shard):
    rows, cols = shard.shape

    def body(x_ref, out_ref, send_sems, recv_sems, local_sem):
        x, y, c = lax.axis_index("x"), lax.axis_index("y"), lax.axis_index("c")
        me, sibling = (x, y, c), (x, y, 1 - c)
        chips = [(1 - x, y), (x, 1 - y), (1 - x, 1 - y)]

        def slot(px, py, pc):
            return out_ref.at[4 * px + 2 * py + pc]

        def copy(k, block, to, src=None):
            return pltpu.make_async_remote_copy(
                src_ref=slot(*block) if src is None else src, dst_ref=slot(*block),
                send_sem=send_sems.at[k], recv_sem=recv_sems.at[k], device_id=to, device_id_type=pl.DeviceIdType.MESH)

        mine = pltpu.make_async_copy(x_ref, slot(*me), local_sem)
        mine.start()
        first = [copy(0, me, sibling, src=x_ref)]
        first += [copy(1 + j, me, (*chip, c), src=x_ref) for j, chip in enumerate(chips)]
        for cp in first:
            cp.start()
        passed = [copy(4 + j, (*chip, c), sibling) for j, chip in enumerate(chips)]
        for j, chip in enumerate(chips):
            copy(1 + j, (*chip, c), me).wait_recv()
            passed[j].start()
        copy(0, sibling, me).wait_recv()
        for j, chip in enumerate(chips):
            copy(4 + j, (*chip, 1 - c), me).wait_recv()
        for cp in first + passed:
            cp.wait_send()
        mine.wait()

    return pl.pallas_call(
        body, name="all_gather_weights", out_shape=jax.ShapeDtypeStruct((N_DEV, rows, cols), shard.dtype),
        in_specs=[pl.BlockSpec(memory_space=pl.ANY)], out_specs=pl.BlockSpec(memory_space=pl.ANY),
        scratch_shapes=[pltpu.SemaphoreType.DMA((7,)), pltpu.SemaphoreType.DMA((7,)), pltpu.SemaphoreType.DMA],
    )(shard)


def _all_to_all(pieces):
    def body(in_ref, out_ref, send_sems, recv_sems, local_sem):
        x, y, c = lax.axis_index("x"), lax.axis_index("y"), lax.axis_index("c")
        my_id = 4 * x + 2 * y + c
        flips = [(fx, fy, fc) for fx in (0, 1) for fy in (0, 1) for fc in (0, 1)][1:]

        def peer(f):
            return (x ^ f[0], y ^ f[1], c ^ f[2])

        def copy(kk, f):
            p = peer(f)
            return pltpu.make_async_remote_copy(
                src_ref=in_ref.at[4 * p[0] + 2 * p[1] + p[2]], dst_ref=out_ref.at[my_id],
                send_sem=send_sems.at[kk], recv_sem=recv_sems.at[kk], device_id=p, device_id_type=pl.DeviceIdType.MESH)

        mine = pltpu.make_async_copy(in_ref.at[my_id], out_ref.at[my_id], local_sem)
        mine.start()
        copies = [copy(kk, f) for kk, f in enumerate(flips)]
        for cp in copies:
            cp.start()
        for cp in copies:
            cp.wait_recv()
        for cp in copies:
            cp.wait_send()
        mine.wait()

    return pl.pallas_call(
        body, name="all_to_all_grads", out_shape=jax.ShapeDtypeStruct(pieces.shape, pieces.dtype),
        in_specs=[pl.BlockSpec(memory_space=pl.ANY)], out_specs=pl.BlockSpec(memory_space=pl.ANY),
        scratch_shapes=[pltpu.SemaphoreType.DMA((7,)), pltpu.SemaphoreType.DMA((7,)), pltpu.SemaphoreType.DMA],
    )(pieces)


def _adamw(parts, w, m, v):
    rows, cols = w.shape
    tr = _pick(rows, (192, 96, 64, 32, 16))
    pspec = pl.BlockSpec((N_DEV, tr, cols), lambda i: (0, i, 0))
    rspec = pl.BlockSpec((tr, cols), lambda i: (i, 0))

    def body(p_ref, w_ref, m_ref, v_ref, g_ref, d_ref, m2_ref, v2_ref):
        g, d, m2, v2 = _adamw_fn([p_ref[s] for s in range(N_DEV)], w_ref[...], m_ref[...], v_ref[...])
        g_ref[...], d_ref[...], m2_ref[...], v2_ref[...] = g, d, m2, v2

    return pl.pallas_call(
        body, name="adamw", grid=(rows // tr,), in_specs=[pspec, rspec, rspec, rspec], out_specs=[rspec] * 4,
        out_shape=[jax.ShapeDtypeStruct((rows, cols), F32)] * 4,
        compiler_params=pltpu.CompilerParams(dimension_semantics=("parallel",), vmem_limit_bytes=VMEM_LIMIT),
    )(parts, w, m, v)


def kernel(x, positions, g_mix, w_in, g_q_a, w_q_b, g_kv_a, w_kv_b, g_qn, g_kn, w_mla_out, ret_decay_fwd, ret_decay_bwd, w_ret_out, w_out, g_ffn, w_gate_up, w_down, loss_target, m_g_mix, m_w_in, m_g_q_a, m_w_q_b, m_g_kv_a, m_w_kv_b, m_g_qn, m_g_kn, m_w_mla_out, m_ret_decay_fwd, m_ret_decay_bwd, m_w_ret_out, m_w_out, m_g_ffn, m_w_gate_up, m_w_down, v_g_mix, v_w_in, v_g_q_a, v_w_q_b, v_g_kv_a, v_w_kv_b, v_g_qn, v_g_kn, v_w_mla_out, v_ret_decay_fwd, v_ret_decay_bwd, v_w_ret_out, v_w_out, v_g_ffn, v_w_gate_up, v_w_down):
    w = dict(g_mix=g_mix, w_in=w_in, g_q_a=g_q_a, w_q_b=w_q_b, g_kv_a=g_kv_a, w_kv_b=w_kv_b, g_qn=g_qn, g_kn=g_kn, w_mla_out=w_mla_out,
             ret_decay_fwd=ret_decay_fwd, ret_decay_bwd=ret_decay_bwd, w_ret_out=w_ret_out, w_out=w_out, g_ffn=g_ffn,
             w_gate_up=w_gate_up, w_down=w_down)
    m = dict(g_mix=m_g_mix, w_in=m_w_in, g_q_a=m_g_q_a, w_q_b=m_w_q_b, g_kv_a=m_g_kv_a, w_kv_b=m_w_kv_b, g_qn=m_g_qn, g_kn=m_g_kn,
             w_mla_out=m_w_mla_out, ret_decay_fwd=m_ret_decay_fwd, ret_decay_bwd=m_ret_decay_bwd, w_ret_out=m_w_ret_out, w_out=m_w_out,
             g_ffn=m_g_ffn, w_gate_up=m_w_gate_up, w_down=m_w_down)
    v = dict(g_mix=v_g_mix, w_in=v_w_in, g_q_a=v_g_q_a, w_q_b=v_w_q_b, g_kv_a=v_g_kv_a, w_kv_b=v_w_kv_b, g_qn=v_g_qn, g_kn=v_g_kn,
             w_mla_out=v_w_mla_out, ret_decay_fwd=v_ret_decay_fwd, ret_decay_bwd=v_ret_decay_bwd, w_ret_out=v_w_ret_out, w_out=v_w_out,
             g_ffn=v_g_ffn, w_gate_up=v_w_gate_up, w_down=v_w_down)
    gains = {n: w[n].reshape(1, ln) for n, ln in GAINS}

    gathered = _all_gather(_pack_shards(w, gains, WIRE))
    W = _unpack_gathered(gathered)
    S = x.shape[1]
    pos = positions.reshape(S, 1).astype(F32)
    loss_rows, grad_x, gG, gW = _local_step(x.reshape(S, D_MODEL), pos, loss_target.reshape(S, D_MODEL), gains, W)
    loss = lax.psum(jnp.sum(loss_rows), ("x", "y", "c"))

    parts = _all_to_all(_pack_full_grads(gW, gG, GWIRE))
    pk = lambda d: _pack_shards(d, {n: d[n].reshape(1, ln) for n, ln in GAINS}, F32)
    g_p, d_p, m_p, v_p = _adamw(parts, pk(w), pk(m), pk(v))
    outs = [_unpack_shards(t) for t in (g_p, d_p, m_p, v_p)]
    return (loss, grad_x.reshape(x.shape), *[o[n] for o in outs for n in ORDER])
```

```python
import functools

import numpy as np
import jax
import jax.numpy as jnp
from jax import lax
from jax.experimental import pallas as pl
from jax.experimental.pallas import tpu as pltpu

F32 = jnp.float32
MXU = jnp.bfloat16
WIRE = jnp.bfloat16
GWIRE = jnp.float32

N_DEV = 8
D_MODEL = 1024
HEADS = 8
LANES = 128
Q_RANK, KV_RANK = 256, 128
NOPE, ROPE_M, V_M = 64, 32, 64
QK_M = NOPE + ROPE_M
RQK, RV = 64, 128
CHUNK = 128
FFN = 2816
IN_WIDTH = 5536
THETA = 10000.0
EPS = 1e-6
LR, B1, B2, AEPS, WD, STEP = 0.001, 0.9, 0.999, 1e-08, 0.01, 10
VMEM_LIMIT = 56 * 1024 * 1024

NN = ((1,), (0,))
NT = ((1,), (1,))
TN = ((0,), (0,))

P_GATES, P_VR, P_GR, P_QR, P_KR, P_CQ, P_CKV, P_KROPE, P_WIDTH = 0, 2048, 3072, 4096, 4608, 5120, 5376, 5504, 5632
O_CQ, O_CKV, O_KROPE, O_QR, O_KR, O_VR, O_GR, O_GATES = 0, 256, 384, 416, 928, 1440, 2464, 3488


def _dot(a, b, dims):
    return lax.dot_general(a, b, (dims, ((), ())), preferred_element_type=F32)


def _pick(dim, cands):
    for c in cands:
        if dim % c == 0:
            return c
    return dim


def _pairs(t):
    return t.reshape(t.shape[0], 4, 2, 2, 32).transpose(0, 1, 3, 2, 4).reshape(t.shape[0], 512)


def _win_pad(w):
    z = jnp.zeros((w.shape[0], 48), w.dtype)
    kr = w[:, O_KROPE:O_KROPE + 32]
    return jnp.concatenate([w[:, O_GATES:], w[:, O_VR:O_VR + 1024], w[:, O_GR:O_GR + 1024], _pairs(w[:, O_QR:O_QR + 512]),
                            _pairs(w[:, O_KR:O_KR + 512]), w[:, :O_CKV], w[:, O_CKV:O_KROPE], kr[:, :16], z, kr[:, 16:], z], axis=1)


def _win_unpad(g):
    return jnp.concatenate([g[:, P_CQ:P_CQ + 256], g[:, P_CKV:P_CKV + 128], g[:, P_KROPE:P_KROPE + 16], g[:, P_KROPE + 64:P_KROPE + 80],
                            _pairs(g[:, P_QR:P_QR + 512]), _pairs(g[:, P_KR:P_KR + 512]), g[:, P_VR:P_VR + 1024],
                            g[:, P_GR:P_GR + 1024], g[:, P_GATES:P_GATES + 2048]], axis=1)


def _qk_pad(t):
    z = jnp.zeros(t.shape[:-1] + (32,), t.dtype)
    return jnp.concatenate([t[..., 64:80], t[..., 0:48], t[..., 80:96], t[..., 48:64], z], axis=-1)


def _qk_unpad(p):
    return jnp.concatenate([p[..., 16:64], p[..., 80:96], p[..., 0:16], p[..., 64:80]], axis=-1)


def _wq_pad(w):
    return _qk_pad(w.reshape(Q_RANK, HEADS, QK_M)).reshape(Q_RANK, HEADS * LANES)


def _wq_unpad(g):
    return _qk_unpad(g.reshape(Q_RANK, HEADS, LANES)).reshape(Q_RANK, HEADS * QK_M)


def _wkv_pad(w):
    t = w.reshape(KV_RANK, HEADS, NOPE + V_M)
    z = lambda n: jnp.zeros((KV_RANK, HEADS, n), w.dtype)
    wk = jnp.concatenate([z(16), t[..., 0:48], z(16), t[..., 48:64], z(32)], axis=-1)
    wv = jnp.concatenate([t[..., 64:128], z(64)], axis=-1)
    return wk.reshape(KV_RANK, HEADS * LANES), wv.reshape(KV_RANK, HEADS * LANES)


def _wkv_unpad(dwk, dwv):
    k, v = dwk.reshape(KV_RANK, HEADS, LANES), dwv.reshape(KV_RANK, HEADS, LANES)
    return jnp.concatenate([k[..., 16:64], k[..., 80:96], v[..., 0:64]], axis=-1).reshape(KV_RANK, HEADS * (NOPE + V_M))


def _wmla_pad(w):
    t = w.reshape(HEADS, V_M, D_MODEL)
    return jnp.concatenate([t, jnp.zeros_like(t)], axis=1).reshape(HEADS * LANES, D_MODEL)


def _wmla_unpad(g):
    return g.reshape(HEADS, LANES, D_MODEL)[:, :V_M].reshape(HEADS * V_M, D_MODEL)


def _rowwise(name, fn, rows, ts, ins, outs, accs=(), ncol=1):
    n_in, n_out, n_acc = len(ins), len(outs), len(accs)

    def colmap(col):
        if callable(col):
            return lambda i, j: (i, col(j))
        return lambda i, j: (i, col)

    arrays, in_specs = [], []
    for arr, spec in ins:
        arrays.append(arr)
        if spec is None:
            in_specs.append(pl.BlockSpec(arr.shape, functools.partial(lambda i, j, nd: (0,) * nd, nd=arr.ndim)))
        else:
            in_specs.append(pl.BlockSpec((ts, spec[0]), colmap(spec[1])))
    out_shape, out_specs = [], []
    for total, dtype, width, col in outs:
        out_shape.append(jax.ShapeDtypeStruct((rows, total), dtype))
        out_specs.append(pl.BlockSpec((ts, width), colmap(col)))
    for shp in accs:
        out_shape.append(jax.ShapeDtypeStruct(shp, F32))
        out_specs.append(pl.BlockSpec(shp, functools.partial(lambda i, j, nd: (0,) * nd, nd=len(shp))))

    def body(*refs):
        vals = [r[...] for r in refs[:n_in]]
        res = fn(*vals)
        if not isinstance(res, (tuple, list)):
            res = (res,)
        for r, v in zip(refs[n_in:n_in + n_out], res[:n_out]):
            r[...] = v.astype(r.dtype)
        if n_acc:
            first = jnp.logical_and(pl.program_id(0) == 0, pl.program_id(1) == 0)
            for r, v in zip(refs[n_in + n_out:], res[n_out:]):
                @pl.when(first)
                def _(r=r):
                    r[...] = jnp.zeros_like(r)
                r[...] += v.astype(F32)

    res = pl.pallas_call(
        body, name=name, grid=(rows // ts, ncol), in_specs=in_specs, out_specs=out_specs, out_shape=out_shape,
        compiler_params=pltpu.CompilerParams(dimension_semantics=("arbitrary", "arbitrary"), vmem_limit_bytes=VMEM_LIMIT),
    )(*arrays)
    return res


def _mm(name, a, b, mode, add=None, out_dtype=F32):
    if mode == "nn":
        (M, K), N = a.shape, b.shape[1]
    elif mode == "nt":
        (M, K), N = a.shape, b.shape[0]
    else:
        (K, M), N = a.shape, b.shape[1]
    tm = _pick(M, (1024, 512, 256, 128))
    tn = _pick(N, (512, 256, 128))
    if mode == "tn":
        tm = _pick(M, (512, 256, 128))
        tk = _pick(K, (512, 256, 128))
    else:
        tk = K if K <= 2816 else _pick(K, (1024, 512, 256, 128))
    nk = K // tk
    dims = {"nn": NN, "nt": NT, "tn": TN}[mode]
    a_spec = pl.BlockSpec((tk, tm), lambda i, j, k: (k, i)) if mode == "tn" else pl.BlockSpec((tm, tk), lambda i, j, k: (i, k))
    b_spec = pl.BlockSpec((tn, tk), lambda i, j, k: (j, k)) if mode == "nt" else pl.BlockSpec((tk, tn), lambda i, j, k: (k, j))
    o_spec = pl.BlockSpec((tm, tn), lambda i, j, k: (i, j))
    has_add = add is not None

    def body(*refs):
        a_ref, b_ref = refs[0], refs[1]
        add_ref = refs[2] if has_add else None
        o_ref, acc_ref = refs[-2], refs[-1]
        k = pl.program_id(2)

        @pl.when(k == 0)
        def _():
            acc_ref[...] = jnp.zeros_like(acc_ref)

        acc_ref[...] += _dot(a_ref[...], b_ref[...], dims)

        @pl.when(k == nk - 1)
        def _():
            r = acc_ref[...]
            if has_add:
                r = r + add_ref[...]
            o_ref[...] = r.astype(o_ref.dtype)

    args = [a, b] + ([add] if has_add else [])
    specs = [a_spec, b_spec] + ([o_spec] if has_add else [])
    return pl.pallas_call(
        body, name=name, grid=(M // tm, N // tn, nk), in_specs=specs, out_specs=o_spec,
        out_shape=jax.ShapeDtypeStruct((M, N), out_dtype), scratch_shapes=[pltpu.VMEM((tm, tn), F32)],
        compiler_params=pltpu.CompilerParams(dimension_semantics=("parallel", "parallel", "arbitrary"), vmem_limit_bytes=VMEM_LIMIT),
    )(*args)


@jax.custom_vjp
def _swap64(x):
    return pltpu.roll(x, 64, 1)


_swap64.defvjp(lambda x: (_swap64(x), None), lambda _, g: (_swap64(g),))


@jax.custom_vjp
def _mxdot(a, b):
    return _dot(a.astype(MXU), b.astype(MXU), NN)


def _mxdot_bwd(res, g):
    a, b = res
    gb = g.astype(MXU)
    return _dot(gb, b.astype(MXU), NT), _dot(a.astype(MXU), gb, TN)


_mxdot.defvjp(lambda a, b: (_mxdot(a, b), (a, b)), _mxdot_bwd)


def _rms(x):
    return x * lax.rsqrt(jnp.mean(x * x, axis=-1, keepdims=True) + EPS)


def _rmsg_fn(x, g):
    return _rms(x) * g


def _silu(x):
    return x * jax.nn.sigmoid(x)


def _tables_fn(pos, inv_m, sgn_m, inv_r, sgn_r):
    am, ar = pos * inv_m, pos * inv_r
    return jnp.cos(am), jnp.sin(am) * sgn_m, jnp.cos(ar), jnp.sin(ar) * sgn_r


def _head_blocks(t):
    return [t[:, LANES * h:LANES * (h + 1)] for h in range(t.shape[1] // LANES)]


def _mla_prep_fn(cq, ckv, kr, cosm, sinm, gqa, gkva, gqn, gkn, wq, wk, wv):
    cqn = _rms(cq) * gqa
    ckvn = _rms(ckv) * gkva
    q_raw = _mxdot(cqn, wq)
    k_raw = _mxdot(ckvn, wk)
    lane = lax.broadcasted_iota(jnp.int32, (1, HEADS * LANES), 1)
    v = _mxdot(ckvn, wv) + (lane % LANES == V_M).astype(F32)

    def norm_rope(blocks, g, extra):
        outs = []
        for b in blocks:
            if extra is not None:
                b = b + extra
            n = b * lax.rsqrt(jnp.sum(b * b, axis=-1, keepdims=True) * (1.0 / QK_M) + EPS) * g
            outs.append(n * cosm + _swap64(n) * sinm)
        return jnp.concatenate(outs, axis=1)

    q = norm_rope(_head_blocks(q_raw), gqn, None)
    k = norm_rope(_head_blocks(k_raw), gkn, kr)
    return q, k, v


def _ret_prep_fn(qr, kr, cosr, sinr):
    def rope(t, scale):
        return jnp.concatenate([(b * cosr + _swap64(b) * sinr) * scale for b in _head_blocks(t)], axis=1)
    return rope(qr, 1.0), rope(kr, RQK ** -0.5)


def _ret_post_fn(rf, rb, gr):
    ret = rf + rb
    outs = []
    for b, g in zip(_head_blocks(ret), _head_blocks(gr)):
        outs.append(_silu(g) * _rms(b))
    return jnp.concatenate(outs, axis=1)


def _merge_fn(ga, gb, ya, yb):
    return jax.nn.sigmoid(ga) * ya + jax.nn.sigmoid(gb) * yb


def _swiglu_fn(gate, up):
    return _silu(gate) * up


def _loss_fn(x2, tgt):
    d = x2 - tgt
    return d * (1.0 / D_MODEL), 0.5 * jnp.sum(d * d, axis=0, keepdims=True) * (1.0 / D_MODEL)


def _adamw_fn(parts, w, m, v):
    g = parts[0]
    for p in range(1, N_DEV):
        g = g + parts[p]
    g = g.astype(F32)
    m2 = B1 * m + (1.0 - B1) * g
    v2 = B2 * v + (1.0 - B2) * jnp.square(g)
    m_hat = m2 / (1.0 - B1 ** STEP)
    v_hat = v2 / (1.0 - B2 ** STEP)
    delta = -LR * (m_hat / (jnp.sqrt(v_hat) + AEPS) + WD * w)
    return g, delta, m2, v2


SCALE = QK_M ** -0.5
LOG2E = 1.4426950408889634
FLASH_ROWS = 32


def _flash_fwd(q, k, v):
    S = q.shape[0]
    tq = _pick(S, (512, 256, 128))
    tk = _pick(S, (2048, 1024, 512, 256, 128))
    ncb = tk // LANES
    mrows = 64
    erows = 16
    c = SCALE * LOG2E

    def body(q_ref, k_ref, v_ref, o_ref, obf_ref, lse_ref, s_sc, p_sc, m_sc, a_sc, acc_sc):
        m_sc[...] = jnp.full_like(m_sc, -jnp.inf)
        acc_sc[...] = jnp.zeros_like(acc_sc)
        qb = q_ref[...]

        def kv_step(j, carry):
            kv_rows = pl.ds(pl.multiple_of(j * tk, tk), tk)
            s_sc[...] = _dot(qb, k_ref[kv_rows, :], NT)

            def max_step(r, carry2):
                rows = pl.ds(pl.multiple_of(r * mrows, mrows), mrows)
                m_prev = m_sc[rows, :]
                m_new = jnp.maximum(m_prev, jnp.broadcast_to(jnp.max(s_sc[rows, :], axis=-1, keepdims=True), (mrows, LANES)))
                a_sc[rows, :] = jnp.exp2((m_prev - m_new) * c)
                m_sc[rows, :] = m_new
                return carry2

            lax.fori_loop(0, tq // mrows, max_step, 0, unroll=4)

            def exp_step(r, carry2):
                rows = pl.ds(pl.multiple_of(r * erows, erows), erows)
                m_b = m_sc[rows, :]
                for cb in range(ncb):
                    sl = slice(LANES * cb, LANES * (cb + 1))
                    p_sc[rows, sl] = jnp.exp2((s_sc[rows, sl] - m_b) * c).astype(p_sc.dtype)
                return carry2

            lax.fori_loop(0, tq // erows, exp_step, 0, unroll=2)
            acc_sc[...] = a_sc[...] * acc_sc[...] + _dot(p_sc[...], v_ref[kv_rows, :], NN)
            return carry

        lax.fori_loop(0, S // tk, kv_step, 0)
        acc = acc_sc[...]
        lane = lax.broadcasted_iota(jnp.int32, (1, LANES), 1)
        l = jnp.sum(jnp.where(lane == V_M, acc, 0.0), axis=-1, keepdims=True)
        o = acc / l
        o_ref[...] = o
        obf_ref[...] = o.astype(obf_ref.dtype)
        lse_ref[...] = m_sc[...] * c + jnp.log2(jnp.broadcast_to(l, (tq, LANES)))

    qspec = pl.BlockSpec((tq, LANES), lambda h, i: (i, h))
    kspec = pl.BlockSpec((S, LANES), lambda h, i: (0, h))
    full = jax.ShapeDtypeStruct((S, HEADS * LANES), F32)
    return pl.pallas_call(
        body, name="flash_fwd", grid=(HEADS, S // tq), in_specs=[qspec, kspec, kspec], out_specs=[qspec, qspec, qspec],
        out_shape=[full, jax.ShapeDtypeStruct((S, HEADS * LANES), MXU), full],
        scratch_shapes=[pltpu.VMEM((tq, tk), F32), pltpu.VMEM((tq, tk), MXU), pltpu.VMEM((tq, LANES), F32),
                        pltpu.VMEM((tq, LANES), F32), pltpu.VMEM((tq, LANES), F32)],
        compiler_params=pltpu.CompilerParams(dimension_semantics=("parallel", "arbitrary"), vmem_limit_bytes=VMEM_LIMIT),
    )(q, k, v)


def _delta_fn(o, do):
    outs = [jnp.broadcast_to(jnp.sum(a * b, axis=-1, keepdims=True), a.shape) for a, b in zip(_head_blocks(o), _head_blocks(do))]
    return do, jnp.concatenate(outs, axis=1)


def _flash_bwd(q, k, v, do, lse, delta):
    S = q.shape[0]
    tq = tk = _pick(S, (512, 256, 128))
    ncb = tk // LANES
    c = SCALE * LOG2E

    def body(q_ref, k_ref, v_ref, do_ref, lse_ref, dl_ref, dq_ref, dk_ref, dv_ref, s_sc, dp_sc, p_sc, ds_sc, dk_sc, dv_sc):
        @pl.when(pl.program_id(1) == 0)
        def _():
            dq_ref[...] = jnp.zeros_like(dq_ref)

        dk_sc[...] = jnp.zeros_like(dk_sc)
        dv_sc[...] = jnp.zeros_like(dv_sc)
        kb, vb = k_ref[...], v_ref[...]

        def q_step(i, carry):
            q_rows = pl.ds(pl.multiple_of(i * tq, tq), tq)
            qb, dob = q_ref[q_rows, :], do_ref[q_rows, :]
            s_sc[...] = _dot(qb, kb, NT)
            dp_sc[...] = _dot(dob, vb, NT)

            def row_step(r, carry2):
                rows = pl.ds(pl.multiple_of(r * FLASH_ROWS, FLASH_ROWS), FLASH_ROWS)
                grows = pl.ds(pl.multiple_of(i * tq + r * FLASH_ROWS, FLASH_ROWS), FLASH_ROWS)
                lse_b, dl_b = lse_ref[grows, :], dl_ref[grows, :]
                for cb in range(ncb):
                    sl = slice(LANES * cb, LANES * (cb + 1))
                    p = jnp.exp2(s_sc[rows, sl] * c - lse_b)
                    p_sc[rows, sl] = p.astype(p_sc.dtype)
                    ds_sc[rows, sl] = (p * (dp_sc[rows, sl] - dl_b) * SCALE).astype(ds_sc.dtype)
                return carry2

            lax.fori_loop(0, tq // FLASH_ROWS, row_step, 0, unroll=2)
            dv_sc[...] += _dot(p_sc[...], dob, TN)
            dk_sc[...] += _dot(ds_sc[...], qb, TN)
            dq_ref[q_rows, :] += _dot(ds_sc[...], kb, NN)
            return carry

        lax.fori_loop(0, S // tq, q_step, 0)
        dk_ref[...] = dk_sc[...]
        dv_ref[...] = dv_sc[...]

    hspec = pl.BlockSpec((S, LANES), lambda h, j: (0, h))
    kspec = pl.BlockSpec((tk, LANES), lambda h, j: (j, h))
    full = jax.ShapeDtypeStruct((S, HEADS * LANES), F32)
    return pl.pallas_call(
        body, name="flash_bwd", grid=(HEADS, S // tk), in_specs=[hspec, kspec, kspec, hspec, hspec, hspec],
        out_specs=[hspec, kspec, kspec], out_shape=[full, full, full],
        scratch_shapes=[pltpu.VMEM((tq, tk), F32), pltpu.VMEM((tq, tk), F32), pltpu.VMEM((tq, tk), MXU), pltpu.VMEM((tq, tk), MXU),
                        pltpu.VMEM((tk, LANES), F32), pltpu.VMEM((tk, LANES), F32)],
        compiler_params=pltpu.CompilerParams(dimension_semantics=("parallel", "arbitrary"), vmem_limit_bytes=VMEM_LIMIT),
    )(q, k, v, do, lse, delta)


def _ret_consts(lgh, head, rev):
    C = CHUNK
    lane = lax.broadcasted_iota(jnp.int32, (1, LANES), 1)
    hm = ((lane // 32) % 2 == head % 2).astype(F32)
    r = lax.broadcasted_iota(jnp.int32, (C, C), 0)
    c = lax.broadcasted_iota(jnp.int32, (C, C), 1)
    diff = ((c - r) if rev else (r - c)).astype(F32)
    mask = (diff > 0) if rev else (diff >= 0)
    dpos = jnp.maximum(diff, 0.0)
    din = jnp.where(mask, jnp.exp(lgh * dpos), 0.0)
    idx = lax.broadcasted_iota(jnp.int32, (C, 1), 0).astype(F32)
    eq = (C - idx) if rev else (idx + 1.0)
    ek = idx if rev else (C - 1.0 - idx)
    qd, kd = jnp.exp(lgh * eq), jnp.exp(lgh * ek)
    cd = jnp.exp(lgh * jnp.full((1, 1), float(C), F32))
    return hm, din, dpos, qd, kd, cd, eq, ek


def _ret_fwd(name, qt, kt, proj, lg, rev):
    S = qt.shape[0]
    C = CHUNK
    TB = _pick(S, (512, 256, 128))
    cb, nb = TB // C, S // TB
    blk = (lambda g: nb - 1 - g) if rev else (lambda g: g)

    def body(lg_ref, q_ref, k_ref, v_ref, o_ref, st_ref, state_sc):
        h, g = pl.program_id(0), pl.program_id(1)

        @pl.when(g == 0)
        def _():
            state_sc[...] = jnp.zeros_like(state_sc)

        hm, din, _, qd, kd, cd, _, _ = _ret_consts(lg_ref[h], h, rev)
        for cc in (reversed(range(cb)) if rev else range(cb)):
            rows = pl.ds(cc * C, C)
            q, k, v = q_ref[rows, :] * hm, k_ref[rows, :] * hm, v_ref[rows, :].astype(MXU)
            st = state_sc[...]
            st_ref[0, cc] = st
            a = _dot(q.astype(MXU), k.astype(MXU), NT) * din
            inner = _dot(a.astype(MXU), v, NN)
            cross = _dot((q * qd).astype(MXU), st.astype(MXU), NN)
            o_ref[rows, :] = inner + cross
            state_sc[...] = st * cd + _dot((k * kd).astype(MXU), v, TN)

    return pl.pallas_call(
        body, name=name, grid=(HEADS, nb),
        in_specs=[pl.BlockSpec(memory_space=pltpu.SMEM),
                  pl.BlockSpec((TB, LANES), lambda h, g: (blk(g), h // 2)),
                  pl.BlockSpec((TB, LANES), lambda h, g: (blk(g), h // 2)),
                  pl.BlockSpec((TB, LANES), lambda h, g: (blk(g), P_VR // LANES + h))],
        out_specs=[pl.BlockSpec((TB, LANES), lambda h, g: (blk(g), h)),
                   pl.BlockSpec((1, cb, LANES, LANES), lambda h, g: (h, blk(g), 0, 0))],
        out_shape=[jax.ShapeDtypeStruct((S, HEADS * LANES), F32), jax.ShapeDtypeStruct((HEADS, S // C, LANES, LANES), F32)],
        scratch_shapes=[pltpu.VMEM((LANES, LANES), F32)],
        compiler_params=pltpu.CompilerParams(dimension_semantics=("parallel", "arbitrary"), vmem_limit_bytes=VMEM_LIMIT),
    )(lg, qt, kt, proj)


def _ret_bwd(name, qt, kt, proj, dret, states, lg, rev):
    S = qt.shape[0]
    C = CHUNK
    TB = _pick(S, (512, 256, 128))
    cb, nb = TB // C, S // TB
    blk = (lambda g: g) if rev else (lambda g: nb - 1 - g)

    def body(lg_ref, q_ref, k_ref, v_ref, do_ref, st_ref, dq_ref, dk_ref, dv_ref, dlg_ref, ds_sc, acc_cc, acc_q, acc_k, acc_s):
        h, g = pl.program_id(0), pl.program_id(1)

        @pl.when(g == 0)
        def _():
            ds_sc[...] = jnp.zeros_like(ds_sc)
            acc_cc[...] = jnp.zeros_like(acc_cc)
            acc_q[...] = jnp.zeros_like(acc_q)
            acc_k[...] = jnp.zeros_like(acc_k)
            acc_s[...] = jnp.zeros_like(acc_s)

        lgh = lg_ref[h]
        hm, din, dpos, qd, kd, cd, eq, ek = _ret_consts(lgh, h, rev)
        for cc in (range(cb) if rev else reversed(range(cb))):
            rows = pl.ds(cc * C, C)
            q, k = q_ref[rows, :] * hm, k_ref[rows, :] * hm
            qb, kb, vb = q.astype(MXU), k.astype(MXU), v_ref[rows, :].astype(MXU)
            dob = do_ref[rows, :].astype(MXU)
            st = st_ref[0, cc]
            dsn = ds_sc[...]
            dsnb = dsn.astype(MXU)
            a = _dot(qb, kb, NT)
            dp = _dot(dob, vb, NT)
            da = (dp * din).astype(MXU)
            dqs = _dot(dob, st.astype(MXU), NT)
            vds = _dot(vb, dsnb, NT)
            dq_ref[rows, :] = (_dot(da, kb, NN) + dqs * qd) * hm
            dk_ref[rows, :] = (_dot(da, qb, TN) + vds * kd) * hm
            dv_ref[rows, :] = _dot((a * din).astype(MXU), dob, TN) + _dot((k * kd).astype(MXU), dsnb, NN)
            ds_sc[...] = dsn * cd + _dot((q * qd).astype(MXU), dob, TN)
            acc_cc[...] += dp * a * din * dpos
            acc_q[...] += dqs * q * (qd * eq)
            acc_k[...] += vds * k * (kd * ek)
            acc_s[...] += dsn * st * (cd * float(C))

        @pl.when(g == nb - 1)
        def _():
            tot = (jnp.sum(acc_cc[...], keepdims=True) + jnp.sum(acc_q[...], keepdims=True)
                   + jnp.sum(acc_k[...], keepdims=True) + jnp.sum(acc_s[...], keepdims=True))
            dlg_ref[0] = jnp.broadcast_to(tot * lgh, (8, LANES))

    full = jax.ShapeDtypeStruct((S, HEADS * LANES), F32)
    hspec = pl.BlockSpec((TB, LANES), lambda h, g: (blk(g), h))
    return pl.pallas_call(
        body, name=name, grid=(HEADS, nb),
        in_specs=[pl.BlockSpec(memory_space=pltpu.SMEM),
                  pl.BlockSpec((TB, LANES), lambda h, g: (blk(g), h // 2)),
                  pl.BlockSpec((TB, LANES), lambda h, g: (blk(g), h // 2)),
                  pl.BlockSpec((TB, LANES), lambda h, g: (blk(g), P_VR // LANES + h)),
                  hspec,
                  pl.BlockSpec((1, cb, LANES, LANES), lambda h, g: (h, blk(g), 0, 0))],
        out_specs=[hspec, hspec, hspec, pl.BlockSpec((1, 8, LANES), lambda h, g: (h, 0, 0))],
        out_shape=[full, full, full, jax.ShapeDtypeStruct((HEADS, 8, LANES), F32)],
        scratch_shapes=[pltpu.VMEM((LANES, LANES), F32), pltpu.VMEM((C, C), F32), pltpu.VMEM((C, LANES), F32),
                        pltpu.VMEM((C, LANES), F32), pltpu.VMEM((LANES, LANES), F32)],
        compiler_params=pltpu.CompilerParams(dimension_semantics=("parallel", "arbitrary"), vmem_limit_bytes=VMEM_LIMIT),
    )(lg, qt, kt, proj, dret, states)


def _rope_consts():
    inv16 = THETA ** (-jnp.arange(16, dtype=F32) / 16)
    inv32 = THETA ** (-jnp.arange(32, dtype=F32) / 32)
    lane = np.arange(LANES)
    z48 = jnp.zeros((48,), F32)
    inv_m = jnp.concatenate([inv16, z48, inv16, z48])[None, :]
    sgn_m = jnp.asarray(np.where(lane < 16, -1.0, np.where((lane >= 64) & (lane < 80), 1.0, 0.0)), F32)[None, :]
    inv_r = jnp.concatenate([inv32] * 4)[None, :]
    sgn_r = jnp.asarray(np.where(lane < 64, -1.0, 1.0), F32)[None, :]
    return inv_m, sgn_m, inv_r, sgn_r


def _local_step(x, pos, tgt, gains, W):
    S = x.shape[0]
    ts = _pick(S, (256, 128))
    ts_wide = _pick(S, (128,))
    R = lambda a, w=None, c=0: (a, ((a.shape[1] if w is None else w), c))
    W_ = lambda a: (a, None)

    win = _win_pad(W["w_in"])
    wq = _wq_pad(W["w_q_b"])
    wk, wv = _wkv_pad(W["w_kv_b"])
    wmla = _wmla_pad(W["w_mla_out"])
    wret, wout, wgu, wdown = W["w_ret_out"], W["w_out"], W["w_gate_up"], W["w_down"]
    gqn, gkn = _qk_pad(gains["g_qn"]), _qk_pad(gains["g_kn"])
    g_mix, g_q_a, g_kv_a, g_ffn = gains["g_mix"], gains["g_q_a"], gains["g_kv_a"], gains["g_ffn"]
    lg_f = -jnp.exp(gains["ret_decay_fwd"][0])
    lg_b = -jnp.exp(gains["ret_decay_bwd"][0])

    consts = list(_rope_consts())
    cosm, sinm, cosr, sinr = _rowwise("rope_tables", _tables_fn, S, ts, [R(pos)] + [W_(c) for c in consts],
                                      [(LANES, F32, LANES, 0)] * 4)

    (h,) = _rowwise("rms_mix", _rmsg_fn, S, ts, [R(x), W_(g_mix)], [(D_MODEL, MXU, D_MODEL, 0)])
    proj = _mm("in_proj", h, win, "nn")
    seg = lambda off, w: (proj, (w, off // w))
    mla_ins = [seg(P_CQ, 256), seg(P_CKV, 128), seg(P_KROPE, 128), R(cosm), R(sinm),
               W_(g_q_a), W_(g_kv_a), W_(gqn), W_(gkn), W_(wq), W_(wk), W_(wv)]
    q, k, v = _rowwise("mla_prep", _mla_prep_fn, S, ts, mla_ins, [(HEADS * LANES, MXU, HEADS * LANES, 0)] * 3)
    o, o_bf, lse = _flash_fwd(q, k, v)
    y_a = _mm("mla_out", o_bf, wmla, "nn")

    ret_ins = [seg(P_QR, 512), seg(P_KR, 512), R(cosr), R(sinr)]
    qt, kt = _rowwise("ret_prep", _ret_prep_fn, S, ts, ret_ins, [(512, F32, 512, 0)] * 2)
    ret_f, st_f = _ret_fwd("ret_fwd_f", qt, kt, proj, lg_f, False)
    ret_b, st_b = _ret_fwd("ret_fwd_b", qt, kt, proj, lg_b, True)
    post_ins = [R(ret_f), R(ret_b), seg(P_GR, 1024)]
    (o_b,) = _rowwise("ret_post", _ret_post_fn, S, ts, post_ins, [(1024, MXU, 1024, 0)])
    y_b = _mm("ret_out", o_b, wret, "nn")

    merge_ins = [seg(P_GATES, 1024), (proj, (1024, 1)), R(y_a), R(y_b)]
    (merged,) = _rowwise("merge", _merge_fn, S, ts, merge_ins, [(D_MODEL, MXU, D_MODEL, 0)])
    x1 = _mm("out_proj", merged, wout, "nn", add=x)
    (h2,) = _rowwise("rms_ffn", _rmsg_fn, S, ts, [R(x1), W_(g_ffn)], [(D_MODEL, MXU, D_MODEL, 0)])
    gu = _mm("gate_up", h2, wgu, "nn")
    (act,) = _rowwise("swiglu", lambda t: _swiglu_fn(t[:, :FFN], t[:, FFN:]), S, ts_wide, [R(gu)], [(FFN, MXU, FFN, 0)])
    x2 = _mm("down_proj", act, wdown, "nn", add=x1)
    dx2, dx2_bf, loss_rows = _rowwise("loss", lambda a, b: (lambda d, l: (d, d, l))(*_loss_fn(a, b)), S, ts, [R(x2), R(tgt)],
                                      [(D_MODEL, F32, D_MODEL, 0), (D_MODEL, MXU, D_MODEL, 0)], accs=[(1, D_MODEL)])

    gW = {}
    gW["w_down"] = _mm("d_w_down", act, dx2_bf, "tn")
    dact = _mm("d_act", dx2_bf, wdown, "nt")

    def glu_bwd(t, da):
        _, vjp = jax.vjp(_swiglu_fn, t[:, :FFN], t[:, FFN:])
        return jnp.concatenate(vjp(da), axis=1)

    (dgu,) = _rowwise("swiglu_bwd", glu_bwd, S, ts_wide, [R(gu), R(dact)], [(2 * FFN, MXU, 2 * FFN, 0)])
    gW["w_gate_up"] = _mm("d_w_gate_up", h2, dgu, "tn")
    dh2 = _mm("d_h2", dgu, wgu, "nt")

    def rms_bwd(xx, g, dh, dres):
        _, vjp = jax.vjp(_rmsg_fn, xx, g)
        dx, dg = vjp(dh)
        dx = dx + dres
        return dx, dx, dg

    dx1, dx1_bf, dg_ffn = _rowwise("rms_ffn_bwd", rms_bwd, S, ts, [R(x1), W_(g_ffn), R(dh2), R(dx2)],
                                   [(D_MODEL, F32, D_MODEL, 0), (D_MODEL, MXU, D_MODEL, 0)], accs=[(1, D_MODEL)])
    gW["w_out"] = _mm("d_w_out", merged, dx1_bf, "tn")
    dmerged = _mm("d_merged", dx1_bf, wout, "nt")

    def merge_bwd(ga, gb, ya, yb, dm):
        _, vjp = jax.vjp(_merge_fn, ga, gb, ya, yb)
        return vjp(dm)

    dga, dgb, dy_a, dy_b = _rowwise("merge_bwd", merge_bwd, S, ts, merge_ins + [R(dmerged)], [(D_MODEL, MXU, D_MODEL, 0)] * 4)
    gW["w_ret_out"] = _mm("d_w_ret_out", o_b, dy_b, "tn")
    do_b = _mm("d_o_b", dy_b, wret, "nt")

    def post_bwd(rf, rb, gr, dob):
        _, vjp = jax.vjp(_ret_post_fn, rf, rb, gr)
        drf, _, dgr = vjp(dob)
        return drf, dgr

    dret, dg_r = _rowwise("ret_post_bwd", post_bwd, S, ts, post_ins + [R(do_b)], [(1024, F32, 1024, 0), (1024, MXU, 1024, 0)])
    dq_f, dk_f, dv_f, dlg_f = _ret_bwd("ret_bwd_f", qt, kt, proj, dret, st_f, lg_f, False)
    dq_b, dk_b, dv_b, dlg_b = _ret_bwd("ret_bwd_b", qt, kt, proj, dret, st_b, lg_b, True)

    def ret_prep_bwd(qr, kr, cosr_, sinr_, dqf, dqb, dkf, dkb, dvf, dvb):
        _, vjp = jax.vjp(lambda a, b: _ret_prep_fn(a, b, cosr_, sinr_), qr, kr)
        pair = lambda t: jnp.concatenate([t[:, 256 * j:256 * j + 128] + t[:, 256 * j + 128:256 * j + 256] for j in range(4)], axis=1)
        dqr, dkr = vjp((pair(dqf + dqb), pair(dkf + dkb)))
        return dqr, dkr, dvf + dvb

    dq_r, dk_r, dv_r = _rowwise("ret_prep_bwd", ret_prep_bwd, S, ts, ret_ins + [R(t) for t in (dq_f, dq_b, dk_f, dk_b, dv_f, dv_b)],
                                [(512, MXU, 512, 0), (512, MXU, 512, 0), (1024, MXU, 1024, 0)])

    gW_mla_p = _mm("d_w_mla_out", o_bf, dy_a, "tn")
    do = _mm("d_o", dy_a, wmla, "nt")
    do_bf, delta = _rowwise("attn_delta", _delta_fn, S, ts, [R(o), R(do)], [(HEADS * LANES, MXU, HEADS * LANES, 0), (HEADS * LANES, F32, HEADS * LANES, 0)])
    dq, dk, dv = _flash_bwd(q, k, v, do_bf, lse, delta)

    def mla_prep_bwd(cq, ckv, kr, cosm_, sinm_, gqa, gkva, gqn_, gkn_, wq_, wk_, wv_, dq_, dk_, dv_):
        f = lambda cq, ckv, kr, gqa, gkva, gqn_, gkn_, wq_, wk_, wv_: _mla_prep_fn(cq, ckv, kr, cosm_, sinm_, gqa, gkva, gqn_, gkn_, wq_, wk_, wv_)
        _, vjp = jax.vjp(f, cq, ckv, kr, gqa, gkva, gqn_, gkn_, wq_.astype(F32), wk_.astype(F32), wv_.astype(F32))
        return vjp((dq_, dk_, dv_))

    mb = _rowwise("mla_prep_bwd", mla_prep_bwd, S, ts_wide, mla_ins + [R(dq), R(dk), R(dv)],
                  [(256, MXU, 256, 0), (128, MXU, 128, 0), (128, MXU, 128, 0)],
                  accs=[(1, 256), (1, 128), (1, LANES), (1, LANES), (256, HEADS * LANES), (128, HEADS * LANES), (128, HEADS * LANES)])
    dc_q, dc_kv, dk_rope, dg_q_a, dg_kv_a, dgqn_p, dgkn_p, dwq_p, dwk_p, dwv_p = mb

    dproj = jnp.concatenate([dga, dgb, dv_r, dg_r, dq_r, dk_r, dc_q, dc_kv, dk_rope], axis=1)
    gwin_p = _mm("d_w_in", h, dproj, "tn")
    dh = _mm("d_h", dproj, win, "nt")
    grad_x, _, dg_mix = _rowwise("rms_mix_bwd", rms_bwd, S, ts, [R(x), W_(g_mix), R(dh), R(dx1)],
                                 [(D_MODEL, F32, D_MODEL, 0), (D_MODEL, MXU, D_MODEL, 0)], accs=[(1, D_MODEL)])

    gW["w_in"] = _win_unpad(gwin_p)
    gW["w_q_b"] = _wq_unpad(dwq_p)
    gW["w_kv_b"] = _wkv_unpad(dwk_p, dwv_p)
    gW["w_mla_out"] = _wmla_unpad(gW_mla_p)
    gG = {"g_mix": dg_mix, "g_q_a": dg_q_a, "g_kv_a": dg_kv_a, "g_qn": _qk_unpad(dgqn_p),
          "g_kn": _qk_unpad(dgkn_p), "ret_decay_fwd": dlg_f[:, 0, 0][None, :], "ret_decay_bwd": dlg_b[:, 0, 0][None, :],
          "g_ffn": dg_ffn}
    return loss_rows, grad_x, gG, gW


MATS = [("w_in", (1024, 5536), 1), ("w_q_b", (256, 768), 1), ("w_kv_b", (128, 1024), 1), ("w_mla_out", (512, 1024), 1),
        ("w_ret_out", (1024, 1024), 0), ("w_out", (1024, 1024), 0), ("w_gate_up", (1024, 5632), 1), ("w_down", (2816, 1024), 0)]
GAINS = [("g_mix", 1024), ("g_q_a", 256), ("g_kv_a", 128), ("g_qn", 96), ("g_kn", 96), ("ret_decay_fwd", 8), ("ret_decay_bwd", 8),
         ("g_ffn", 1024)]
ORDER = ["g_mix", "w_in", "g_q_a", "w_q_b", "g_kv_a", "w_kv_b", "g_qn", "g_kn", "w_mla_out", "ret_decay_fwd", "ret_decay_bwd",
         "w_ret_out", "w_out", "g_ffn", "w_gate_up", "w_down"]
MAT_LEN = sum(s[0] * s[1] // N_DEV for _, s, _ in MATS)
GAIN_LEN = sum(n for _, n in GAINS)
PACK_COLS = 1024
PACK_ROWS = -(-(MAT_LEN + GAIN_LEN) // (PACK_COLS * 16)) * 16
PACK_LEN = PACK_ROWS * PACK_COLS


def _shard_shape(shape, axis):
    return (shape[0] // N_DEV, shape[1]) if axis == 0 else (shape[0], shape[1] // N_DEV)


def _pack_shards(mats, gains, dtype):
    parts = [mats[n].reshape(-1).astype(dtype) for n, _, _ in MATS] + [gains[n].reshape(-1).astype(dtype) for n, _ in GAINS]
    flat = jnp.concatenate(parts)
    return jnp.pad(flat, (0, PACK_LEN - flat.shape[0])).reshape(PACK_ROWS, PACK_COLS)


def _unpack_shards(packed):
    flat = packed.reshape(-1)
    out, off = {}, 0
    for n, shape, axis in MATS:
        shp = _shard_shape(shape, axis)
        out[n] = flat[off:off + shp[0] * shp[1]].reshape(shp)
        off += shp[0] * shp[1]
    for n, ln in GAINS:
        out[n] = flat[off:off + ln]
        off += ln
    return out


def _unpack_gathered(gathered):
    flat = gathered.reshape(N_DEV, -1)
    out, off = {}, 0
    for n, shape, axis in MATS:
        shp = _shard_shape(shape, axis)
        piece = flat[:, off:off + shp[0] * shp[1]].reshape((N_DEV,) + shp)
        out[n] = piece.reshape(shape) if axis == 0 else piece.transpose(1, 0, 2).reshape(shape)
        off += shp[0] * shp[1]
    return out


def _pack_full_grads(gW, gG, dtype):
    parts = []
    for n, shape, axis in MATS:
        shp = _shard_shape(shape, axis)
        g = gW[n].astype(dtype)
        piece = g.reshape((N_DEV,) + shp) if axis == 0 else g.reshape(shape[0], N_DEV, shp[1]).transpose(1, 0, 2)
        parts.append(piece.reshape(N_DEV, -1))
    for n, ln in GAINS:
        parts.append(jnp.broadcast_to(gG[n].reshape(1, ln).astype(dtype), (N_DEV, ln)))
    flat = jnp.concatenate(parts, axis=1)
    return jnp.pad(flat, ((0, 0), (0, PACK_LEN - flat.shape[1]))).reshape(N_DEV, PACK_ROWS, PACK_COLS)


def _all_gather(shard):
    rows, cols = shard.shape

    def body(x_ref, out_ref, send_sems, recv_sems, local_sem):
        x, y, c = lax.axis_index("x"), lax.axis_index("y"), lax.axis_index("c")
        me, sibling = (x, y, c), (x, y, 1 - c)
        chips = [(1 - x, y), (x, 1 - y), (1 - x, 1 - y)]

        def slot(px, py, pc):
            return out_ref.at[4 * px + 2 * py + pc]

        def copy(k, block, to, src=None):
            return pltpu.make_async_remote_copy(
                src_ref=slot(*block) if src is None else src, dst_ref=slot(*block),
                send_sem=send_sems.at[k], recv_sem=recv_sems.at[k], device_id=to, device_id_type=pl.DeviceIdType.MESH)

        mine = pltpu.make_async_copy(x_ref, slot(*me), local_sem)
        mine.start()
        first = [copy(0, me, sibling, src=x_ref)]
        first += [copy(1 + j, me, (*chip, c), src=x_ref) for j, chip in enumerate(chips)]
        for cp in first:
            cp.start()
        passed = [copy(4 + j, (*chip, c), sibling) for j, chip in enumerate(chips)]
        for j, chip in enumerate(chips):
            copy(1 + j, (*chip, c), me).wait_recv()
            passed[j].start()
        copy(0, sibling, me).wait_recv()
        for j, chip in enumerate(chips):
            copy(4 + j, (*chip, 1 - c), me).wait_recv()
        for cp in first + passed:
            cp.wait_send()
        mine.wait()

    return pl.pallas_call(
        body, name="all_gather_weights", out_shape=jax.ShapeDtypeStruct((N_DEV, rows, cols), shard.dtype),
        in_specs=[pl.BlockSpec(memory_space=pl.ANY)], out_specs=pl.BlockSpec(memory_space=pl.ANY),
        scratch_shapes=[pltpu.SemaphoreType.DMA((7,)), pltpu.SemaphoreType.DMA((7,)), pltpu.SemaphoreType.DMA],
    )(shard)


def _all_to_all(pieces):
    def body(in_ref, out_ref, send_sems, recv_sems, local_sem):
        x, y, c = lax.axis_index("x"), lax.axis_index("y"), lax.axis_index("c")
        my_id = 4 * x + 2 * y + c
        flips = [(fx, fy, fc) for fx in (0, 1) for fy in (0, 1) for fc in (0, 1)][1:]

        def peer(f):
            return (x ^ f[0], y ^ f[1], c ^ f[2])

        def copy(kk, f):
            p = peer(f)
            return pltpu.make_async_remote_copy(
                src_ref=in_ref.at[4 * p[0] + 2 * p[1] + p[2]], dst_ref=out_ref.at[my_id],
                send_sem=send_sems.at[kk], recv_sem=recv_sems.at[kk], device_id=p, device_id_type=pl.DeviceIdType.MESH)

        mine = pltpu.make_async_copy(in_ref.at[my_id], out_ref.at[my_id], local_sem)
        mine.start()
        copies = [copy(kk, f) for kk, f in enumerate(flips)]
        for cp in copies:
            cp.start()
        for cp in copies:
            cp.wait_recv()
        for cp in copies:
            cp.wait_send()
        mine.wait()

    return pl.pallas_call(
        body, name="all_to_all_grads", out_shape=jax.ShapeDtypeStruct(pieces.shape, pieces.dtype),
        in_specs=[pl.BlockSpec(memory_space=pl.ANY)], out_specs=pl.BlockSpec(memory_space=pl.ANY),
        scratch_shapes=[pltpu.SemaphoreType.DMA((7,)), pltpu.SemaphoreType.DMA((7,)), pltpu.SemaphoreType.DMA],
    )(pieces)


def _adamw(parts, w, m, v):
    rows, cols = w.shape
    tr = _pick(rows, (192, 96, 64, 32, 16))
    pspec = pl.BlockSpec((N_DEV, tr, cols), lambda i: (0, i, 0))
    rspec = pl.BlockSpec((tr, cols), lambda i: (i, 0))

    def body(p_ref, w_ref, m_ref, v_ref, g_ref, d_ref, m2_ref, v2_ref):
        g, d, m2, v2 = _adamw_fn([p_ref[s] for s in range(N_DEV)], w_ref[...], m_ref[...], v_ref[...])
        g_ref[...], d_ref[...], m2_ref[...], v2_ref[...] = g, d, m2, v2

    return pl.pallas_call(
        body, name="adamw", grid=(rows // tr,), in_specs=[pspec, rspec, rspec, rspec], out_specs=[rspec] * 4,
        out_shape=[jax.ShapeDtypeStruct((rows, cols), F32)] * 4,
        compiler_params=pltpu.CompilerParams(dimension_semantics=("parallel",), vmem_limit_bytes=VMEM_LIMIT),
    )(parts, w, m, v)


def kernel(x, positions, g_mix, w_in, g_q_a, w_q_b, g_kv_a, w_kv_b, g_qn, g_kn, w_mla_out, ret_decay_fwd, ret_decay_bwd, w_ret_out, w_out, g_ffn, w_gate_up, w_down, loss_target, m_g_mix, m_w_in, m_g_q_a, m_w_q_b, m_g_kv_a, m_w_kv_b, m_g_qn, m_g_kn, m_w_mla_out, m_ret_decay_fwd, m_ret_decay_bwd, m_w_ret_out, m_w_out, m_g_ffn, m_w_gate_up, m_w_down, v_g_mix, v_w_in, v_g_q_a, v_w_q_b, v_g_kv_a, v_w_kv_b, v_g_qn, v_g_kn, v_w_mla_out, v_ret_decay_fwd, v_ret_decay_bwd, v_w_ret_out, v_w_out, v_g_ffn, v_w_gate_up, v_w_down):
    w = dict(g_mix=g_mix, w_in=w_in, g_q_a=g_q_a, w_q_b=w_q_b, g_kv_a=g_kv_a, w_kv_b=w_kv_b, g_qn=g_qn, g_kn=g_kn, w_mla_out=w_mla_out,
             ret_decay_fwd=ret_decay_fwd, ret_decay_bwd=ret_decay_bwd, w_ret_out=w_ret_out, w_out=w_out, g_ffn=g_ffn,
             w_gate_up=w_gate_up, w_down=w_down)
    m = dict(g_mix=m_g_mix, w_in=m_w_in, g_q_a=m_g_q_a, w_q_b=m_w_q_b, g_kv_a=m_g_kv_a, w_kv_b=m_w_kv_b, g_qn=m_g_qn, g_kn=m_g_kn,
             w_mla_out=m_w_mla_out, ret_decay_fwd=m_ret_decay_fwd, ret_decay_bwd=m_ret_decay_bwd, w_ret_out=m_w_ret_out, w_out=m_w_out,
             g_ffn=m_g_ffn, w_gate_up=m_w_gate_up, w_down=m_w_down)
    v = dict(g_mix=v_g_mix, w_in=v_w_in, g_q_a=v_g_q_a, w_q_b=v_w_q_b, g_kv_a=v_g_kv_a, w_kv_b=v_w_kv_b, g_qn=v_g_qn, g_kn=v_g_kn,
             w_mla_out=v_w_mla_out, ret_decay_fwd=v_ret_decay_fwd, ret_decay_bwd=v_ret_decay_bwd, w_ret_out=v_w_ret_out, w_out=v_w_out,
             g_ffn=v_g_ffn, w_gate_up=v_w_gate_up, w_down=v_w_down)
    gains = {n: w[n].reshape(1, ln) for n, ln in GAINS}

    gathered = _all_gather(_pack_shards(w, gains, WIRE))
    W = _unpack_gathered(gathered)
    S = x.shape[1]
    pos = positions.reshape(S, 1).astype(F32)
    loss_rows, grad_x, gG, gW = _local_step(x.reshape(S, D_MODEL), pos, loss_target.reshape(S, D_MODEL), gains, W)
    loss = lax.psum(jnp.sum(loss_rows), ("x", "y", "c"))

    parts = _all_to_all(_pack_full_grads(gW, gG, GWIRE))
    pk = lambda d: _pack_shards(d, {n: d[n].reshape(1, ln) for n, ln in GAINS}, F32)
    g_p, d_p, m_p, v_p = _adamw(parts, pk(w), pk(m), pk(v))
    outs = [_unpack_shards(t) for t in (g_p, d_p, m_p, v_p)]
    return (loss, grad_x.reshape(x.shape), *[o[n] for o in outs for n in ORDER])
```

```python
import functools

import numpy as np
import jax
import jax.numpy as jnp
from jax import lax
from jax.experimental import pallas as pl
from jax.experimental.pallas import tpu as pltpu

F32 = jnp.float32
MXU = jnp.bfloat16
WIRE = jnp.bfloat16
GWIRE = jnp.float32

N_DEV = 8
D_MODEL = 1024
HEADS = 8
LANES = 128
Q_RANK, KV_RANK = 256, 128
NOPE, ROPE_M, V_M = 64, 32, 64
QK_M = NOPE + ROPE_M
RQK, RV = 64, 128
CHUNK = 128
FFN = 2816
IN_WIDTH = 5536
THETA = 10000.0
EPS = 1e-6
LR, B1, B2, AEPS, WD, STEP = 0.001, 0.9, 0.999, 1e-08, 0.01, 10
VMEM_LIMIT = 56 * 1024 * 1024

NN = ((1,), (0,))
NT = ((1,), (1,))
TN = ((0,), (0,))

P_GATES, P_VR, P_GR, P_QR, P_KR, P_CQ, P_CKV, P_KROPE, P_WIDTH = 0, 2048, 3072, 4096, 4608, 5120, 5376, 5504, 5632
O_CQ, O_CKV, O_KROPE, O_QR, O_KR, O_VR, O_GR, O_GATES = 0, 256, 384, 416, 928, 1440, 2464, 3488


def _dot(a, b, dims):
    return lax.dot_general(a, b, (dims, ((), ())), preferred_element_type=F32)


def _pick(dim, cands):
    for c in cands:
        if dim % c == 0:
            return c
    return dim


def _pairs(t):
    return t.reshape(t.shape[0], 4, 2, 2, 32).transpose(0, 1, 3, 2, 4).reshape(t.shape[0], 512)


def _win_pad(w):
    z = jnp.zeros((w.shape[0], 48), w.dtype)
    kr = w[:, O_KROPE:O_KROPE + 32]
    return jnp.concatenate([w[:, O_GATES:], w[:, O_VR:O_VR + 1024], w[:, O_GR:O_GR + 1024], _pairs(w[:, O_QR:O_QR + 512]),
                            _pairs(w[:, O_KR:O_KR + 512]), w[:, :O_CKV], w[:, O_CKV:O_KROPE], kr[:, :16], z, kr[:, 16:], z], axis=1)


def _win_unpad(g):
    return jnp.concatenate([g[:, P_CQ:P_CQ + 256], g[:, P_CKV:P_CKV + 128], g[:, P_KROPE:P_KROPE + 16], g[:, P_KROPE + 64:P_KROPE + 80],
                            _pairs(g[:, P_QR:P_QR + 512]), _pairs(g[:, P_KR:P_KR + 512]), g[:, P_VR:P_VR + 1024],
                            g[:, P_GR:P_GR + 1024], g[:, P_GATES:P_GATES + 2048]], axis=1)


def _qk_pad(t):
    z = jnp.zeros(t.shape[:-1] + (32,), t.dtype)
    return jnp.concatenate([t[..., 64:80], t[..., 0:48], t[..., 80:96], t[..., 48:64], z], axis=-1)


def _qk_unpad(p):
    return jnp.concatenate([p[..., 16:64], p[..., 80:96], p[..., 0:16], p[..., 64:80]], axis=-1)


def _wq_pad(w):
    return _qk_pad(w.reshape(Q_RANK, HEADS, QK_M)).reshape(Q_RANK, HEADS * LANES)


def _wq_unpad(g):
    return _qk_unpad(g.reshape(Q_RANK, HEADS, LANES)).reshape(Q_RANK, HEADS * QK_M)


def _wkv_pad(w):
    t = w.reshape(KV_RANK, HEADS, NOPE + V_M)
    z = lambda n: jnp.zeros((KV_RANK, HEADS, n), w.dtype)
    wk = jnp.concatenate([z(16), t[..., 0:48], z(16), t[..., 48:64], z(32)], axis=-1)
    wv = jnp.concatenate([t[..., 64:128], z(64)], axis=-1)
    return wk.reshape(KV_RANK, HEADS * LANES), wv.reshape(KV_RANK, HEADS * LANES)


def _wkv_unpad(dwk, dwv):
    k, v = dwk.reshape(KV_RANK, HEADS, LANES), dwv.reshape(KV_RANK, HEADS, LANES)
    return jnp.concatenate([k[..., 16:64], k[..., 80:96], v[..., 0:64]], axis=-1).reshape(KV_RANK, HEADS * (NOPE + V_M))


def _wmla_pad(w):
    t = w.reshape(HEADS, V_M, D_MODEL)
    return jnp.concatenate([t, jnp.zeros_like(t)], axis=1).reshape(HEADS * LANES, D_MODEL)


def _wmla_unpad(g):
    return g.reshape(HEADS, LANES, D_MODEL)[:, :V_M].reshape(HEADS * V_M, D_MODEL)


def _rowwise(name, fn, rows, ts, ins, outs, accs=(), ncol=1):
    n_in, n_out, n_acc = len(ins), len(outs), len(accs)

    def colmap(col):
        if callable(col):
            return lambda i, j: (i, col(j))
        return lambda i, j: (i, col)

    arrays, in_specs = [], []
    for arr, spec in ins:
        arrays.append(arr)
        if spec is None:
            in_specs.append(pl.BlockSpec(arr.shape, functools.partial(lambda i, j, nd: (0,) * nd, nd=arr.ndim)))
        else:
            in_specs.append(pl.BlockSpec((ts, spec[0]), colmap(spec[1])))
    out_shape, out_specs = [], []
    for total, dtype, width, col in outs:
        out_shape.append(jax.ShapeDtypeStruct((rows, total), dtype))
        out_specs.append(pl.BlockSpec((ts, width), colmap(col)))
    for shp in accs:
        out_shape.append(jax.ShapeDtypeStruct(shp, F32))
        out_specs.append(pl.BlockSpec(shp, functools.partial(lambda i, j, nd: (0,) * nd, nd=len(shp))))

    def body(*refs):
        vals = [r[...] for r in refs[:n_in]]
        res = fn(*vals)
        if not isinstance(res, (tuple, list)):
            res = (res,)
        for r, v in zip(refs[n_in:n_in + n_out], res[:n_out]):
            r[...] = v.astype(r.dtype)
        if n_acc:
            first = jnp.logical_and(pl.program_id(0) == 0, pl.program_id(1) == 0)
            for r, v in zip(refs[n_in + n_out:], res[n_out:]):
                @pl.when(first)
                def _(r=r):
                    r[...] = jnp.zeros_like(r)
                r[...] += v.astype(F32)

    res = pl.pallas_call(
        body, name=name, grid=(rows // ts, ncol), in_specs=in_specs, out_specs=out_specs, out_shape=out_shape,
        compiler_params=pltpu.CompilerParams(dimension_semantics=("arbitrary", "arbitrary"), vmem_limit_bytes=VMEM_LIMIT),
    )(*arrays)
    return res


def _mm(name, a, b, mode, add=None, out_dtype=F32):
    if mode == "nn":
        (M, K), N = a.shape, b.shape[1]
    elif mode == "nt":
        (M, K), N = a.shape, b.shape[0]
    else:
        (K, M), N = a.shape, b.shape[1]
    tm = _pick(M, (1024, 512, 256, 128))
    tn = _pick(N, (512, 256, 128))
    if mode == "tn":
        tm = _pick(M, (512, 256, 128))
        tk = _pick(K, (512, 256, 128))
    else:
        tk = K if K <= 2816 else _pick(K, (1024, 512, 256, 128))
    nk = K // tk
    dims = {"nn": NN, "nt": NT, "tn": TN}[mode]
    a_spec = pl.BlockSpec((tk, tm), lambda i, j, k: (k, i)) if mode == "tn" else pl.BlockSpec((tm, tk), lambda i, j, k: (i, k))
    b_spec = pl.BlockSpec((tn, tk), lambda i, j, k: (j, k)) if mode == "nt" else pl.BlockSpec((tk, tn), lambda i, j, k: (k, j))
    o_spec = pl.BlockSpec((tm, tn), lambda i, j, k: (i, j))
    has_add = add is not None

    def body(*refs):
        a_ref, b_ref = refs[0], refs[1]
        add_ref = refs[2] if has_add else None
        o_ref, acc_ref = refs[-2], refs[-1]
        k = pl.program_id(2)

        @pl.when(k == 0)
        def _():
            acc_ref[...] = jnp.zeros_like(acc_ref)

        acc_ref[...] += _dot(a_ref[...], b_ref[...], dims)

        @pl.when(k == nk - 1)
        def _():
            r = acc_ref[...]
            if has_add:
                r = r + add_ref[...]
            o_ref[...] = r.astype(o_ref.dtype)

    args = [a, b] + ([add] if has_add else [])
    specs = [a_spec, b_spec] + ([o_spec] if has_add else [])
    return pl.pallas_call(
        body, name=name, grid=(M // tm, N // tn, nk), in_specs=specs, out_specs=o_spec,
        out_shape=jax.ShapeDtypeStruct((M, N), out_dtype), scratch_shapes=[pltpu.VMEM((tm, tn), F32)],
        compiler_params=pltpu.CompilerParams(dimension_semantics=("parallel", "parallel", "arbitrary"), vmem_limit_bytes=VMEM_LIMIT),
    )(*args)


@jax.custom_vjp
def _swap64(x):
    return pltpu.roll(x, 64, 1)


_swap64.defvjp(lambda x: (_swap64(x), None), lambda _, g: (_swap64(g),))


@jax.custom_vjp
def _mxdot(a, b):
    return _dot(a.astype(MXU), b.astype(MXU), NN)


def _mxdot_bwd(res, g):
    a, b = res
    gb = g.astype(MXU)
    return _dot(gb, b.astype(MXU), NT), _dot(a.astype(MXU), gb, TN)


_mxdot.defvjp(lambda a, b: (_mxdot(a, b), (a, b)), _mxdot_bwd)


def _rms(x):
    return x * lax.rsqrt(jnp.mean(x * x, axis=-1, keepdims=True) + EPS)


def _rmsg_fn(x, g):
    return _rms(x) * g


def _silu(x):
    return x * jax.nn.sigmoid(x)


def _tables_fn(pos, inv_m, sgn_m, inv_r, sgn_r):
    am, ar = pos * inv_m, pos * inv_r
    return jnp.cos(am), jnp.sin(am) * sgn_m, jnp.cos(ar), jnp.sin(ar) * sgn_r


def _head_blocks(t):
    return [t[:, LANES * h:LANES * (h + 1)] for h in range(t.shape[1] // LANES)]


def _mla_prep_fn(cq, ckv, kr, cosm, sinm, gqa, gkva, gqn, gkn, wq, wk, wv):
    cqn = _rms(cq) * gqa
    ckvn = _rms(ckv) * gkva
    q_raw = _mxdot(cqn, wq)
    k_raw = _mxdot(ckvn, wk)
    lane = lax.broadcasted_iota(jnp.int32, (1, HEADS * LANES), 1)
    v = _mxdot(ckvn, wv) + (lane % LANES == V_M).astype(F32)

    def norm_rope(blocks, g, extra):
        outs = []
        for b in blocks:
            if extra is not None:
                b = b + extra
            n = b * lax.rsqrt(jnp.sum(b * b, axis=-1, keepdims=True) * (1.0 / QK_M) + EPS) * g
            outs.append(n * cosm + _swap64(n) * sinm)
        return jnp.concatenate(outs, axis=1)

    q = norm_rope(_head_blocks(q_raw), gqn, None)
    k = norm_rope(_head_blocks(k_raw), gkn, kr)
    return q, k, v


def _ret_prep_fn(qr, kr, cosr, sinr):
    def rope(t, scale):
        return jnp.concatenate([(b * cosr + _swap64(b) * sinr) * scale for b in _head_blocks(t)], axis=1)
    return rope(qr, 1.0), rope(kr, RQK ** -0.5)


def _ret_post_fn(rf, rb, gr):
    ret = rf + rb
    outs = []
    for b, g in zip(_head_blocks(ret), _head_blocks(gr)):
        outs.append(_silu(g) * _rms(b))
    return jnp.concatenate(outs, axis=1)


def _merge_fn(ga, gb, ya, yb):
    return jax.nn.sigmoid(ga) * ya + jax.nn.sigmoid(gb) * yb


def _swiglu_fn(gate, up):
    return _silu(gate) * up


def _loss_fn(x2, tgt):
    d = x2 - tgt
    return d * (1.0 / D_MODEL), 0.5 * jnp.sum(d * d, axis=0, keepdims=True) * (1.0 / D_MODEL)


def _adamw_fn(parts, w, m, v):
    g = parts[0]
    for p in range(1, N_DEV):
        g = g + parts[p]
    g = g.astype(F32)
    m2 = B1 * m + (1.0 - B1) * g
    v2 = B2 * v + (1.0 - B2) * jnp.square(g)
    m_hat = m2 / (1.0 - B1 ** STEP)
    v_hat = v2 / (1.0 - B2 ** STEP)
    delta = -LR * (m_hat / (jnp.sqrt(v_hat) + AEPS) + WD * w)
    return g, delta, m2, v2


SCALE = QK_M ** -0.5
LOG2E = 1.4426950408889634
FLASH_ROWS = 32


def _flash_fwd(q, k, v):
    S = q.shape[0]
    tq = _pick(S, (512, 256, 128))
    tk = _pick(S, (2048, 1024, 512, 256, 128))
    ncb = tk // LANES
    mrows = 64
    erows = 16
    c = SCALE * LOG2E

    def body(q_ref, k_ref, v_ref, o_ref, obf_ref, lse_ref, s_sc, p_sc, m_sc, a_sc, acc_sc):
        m_sc[...] = jnp.full_like(m_sc, -jnp.inf)
        acc_sc[...] = jnp.zeros_like(acc_sc)
        qb = q_ref[...]

        def kv_step(j, carry):
            kv_rows = pl.ds(pl.multiple_of(j * tk, tk), tk)
            s_sc[...] = _dot(qb, k_ref[kv_rows, :], NT)

            def max_step(r, carry2):
                rows = pl.ds(pl.multiple_of(r * mrows, mrows), mrows)
                m_prev = m_sc[rows, :]
                m_new = jnp.maximum(m_prev, jnp.broadcast_to(jnp.max(s_sc[rows, :], axis=-1, keepdims=True), (mrows, LANES)))
                a_sc[rows, :] = jnp.exp2((m_prev - m_new) * c)
                m_sc[rows, :] = m_new
                return carry2

            lax.fori_loop(0, tq // mrows, max_step, 0, unroll=4)

            def exp_step(r, carry2):
                rows = pl.ds(pl.multiple_of(r * erows, erows), erows)
                m_b = m_sc[rows, :]
                for cb in range(ncb):
                    sl = slice(LANES * cb, LANES * (cb + 1))
                    p_sc[rows, sl] = jnp.exp2((s_sc[rows, sl] - m_b) * c).astype(p_sc.dtype)
                return carry2

            lax.fori_loop(0, tq // erows, exp_step, 0, unroll=2)
            acc_sc[...] = a_sc[...] * acc_sc[...] + _dot(p_sc[...], v_ref[kv_rows, :], NN)
            return carry

        lax.fori_loop(0, S // tk, kv_step, 0)
        acc = acc_sc[...]
        lane = lax.broadcasted_iota(jnp.int32, (1, LANES), 1)
        l = jnp.sum(jnp.where(lane == V_M, acc, 0.0), axis=-1, keepdims=True)
        o = acc / l
        o_ref[...] = o
        obf_ref[...] = o.astype(obf_ref.dtype)
        lse_ref[...] = m_sc[...] * c + jnp.log2(jnp.broadcast_to(l, (tq, LANES)))

    qspec = pl.BlockSpec((tq, LANES), lambda h, i: (i, h))
    kspec = pl.BlockSpec((S, LANES), lambda h, i: (0, h))
    full = jax.ShapeDtypeStruct((S, HEADS * LANES), F32)
    return pl.pallas_call(
        body, name="flash_fwd", grid=(HEADS, S // tq), in_specs=[qspec, kspec, kspec], out_specs=[qspec, qspec, qspec],
        out_shape=[full, jax.ShapeDtypeStruct((S, HEADS * LANES), MXU), full],
        scratch_shapes=[pltpu.VMEM((tq, tk), F32), pltpu.VMEM((tq, tk), MXU), pltpu.VMEM((tq, LANES), F32),
                        pltpu.VMEM((tq, LANES), F32), pltpu.VMEM((tq, LANES), F32)],
        compiler_params=pltpu.CompilerParams(dimension_semantics=("parallel", "arbitrary"), vmem_limit_bytes=VMEM_LIMIT),
    )(q, k, v)


def _delta_fn(o, do):
    outs = [jnp.broadcast_to(jnp.sum(a * b, axis=-1, keepdims=True), a.shape) for a, b in zip(_head_blocks(o), _head_blocks(do))]
    return do, jnp.concatenate(outs, axis=1)


def _flash_bwd(q, k, v, do, lse, delta):
    S = q.shape[0]
    tq = tk = _pick(S, (512, 256, 128))
    ncb = tk // LANES
    c = SCALE * LOG2E

    def body(q_ref, k_ref, v_ref, do_ref, lse_ref, dl_ref, dq_ref, dk_ref, dv_ref, s_sc, dp_sc, p_sc, ds_sc, dk_sc, dv_sc):
        @pl.when(pl.program_id(1) == 0)
        def _():
            dq_ref[...] = jnp.zeros_like(dq_ref)

        dk_sc[...] = jnp.zeros_like(dk_sc)
        dv_sc[...] = jnp.zeros_like(dv_sc)
        kb, vb = k_ref[...], v_ref[...]

        def q_step(i, carry):
            q_rows = pl.ds(pl.multiple_of(i * tq, tq), tq)
            qb, dob = q_ref[q_rows, :], do_ref[q_rows, :]
            s_sc[...] = _dot(qb, kb, NT)
            dp_sc[...] = _dot(dob, vb, NT)

            def row_step(r, carry2):
                rows = pl.ds(pl.multiple_of(r * FLASH_ROWS, FLASH_ROWS), FLASH_ROWS)
                grows = pl.ds(pl.multiple_of(i * tq + r * FLASH_ROWS, FLASH_ROWS), FLASH_ROWS)
                lse_b, dl_b = lse_ref[grows, :], dl_ref[grows, :]
                for cb in range(ncb):
                    sl = slice(LANES * cb, LANES * (cb + 1))
                    p = jnp.exp2(s_sc[rows, sl] * c - lse_b)
                    p_sc[rows, sl] = p.astype(p_sc.dtype)
                    ds_sc[rows, sl] = (p * (dp_sc[rows, sl] - dl_b) * SCALE).astype(ds_sc.dtype)
                return carry2

            lax.fori_loop(0, tq // FLASH_ROWS, row_step, 0, unroll=2)
            dv_sc[...] += _dot(p_sc[...], dob, TN)
            dk_sc[...] += _dot(ds_sc[...], qb, TN)
            dq_ref[q_rows, :] += _dot(ds_sc[...], kb, NN)
            return carry

        lax.fori_loop(0, S // tq, q_step, 0)
        dk_ref[...] = dk_sc[...]
        dv_ref[...] = dv_sc[...]

    hspec = pl.BlockSpec((S, LANES), lambda h, j: (0, h))
    kspec = pl.BlockSpec((tk, LANES), lambda h, j: (j, h))
    full = jax.ShapeDtypeStruct((S, HEADS * LANES), F32)
    return pl.pallas_call(
        body, name="flash_bwd", grid=(HEADS, S // tk), in_specs=[hspec, kspec, kspec, hspec, hspec, hspec],
        out_specs=[hspec, kspec, kspec], out_shape=[full, full, full],
        scratch_shapes=[pltpu.VMEM((tq, tk), F32), pltpu.VMEM((tq, tk), F32), pltpu.VMEM((tq, tk), MXU), pltpu.VMEM((tq, tk), MXU),
                        pltpu.VMEM((tk, LANES), F32), pltpu.VMEM((tk, LANES), F32)],
        compiler_params=pltpu.CompilerParams(dimension_semantics=("parallel", "arbitrary"), vmem_limit_bytes=VMEM_LIMIT),
    )(q, k, v, do, lse, delta)


def _ret_consts(lgh, head, rev):
    C = CHUNK
    lane = lax.broadcasted_iota(jnp.int32, (1, LANES), 1)
    hm = ((lane // 32) % 2 == head % 2).astype(F32)
    r = lax.broadcasted_iota(jnp.int32, (C, C), 0)
    c = lax.broadcasted_iota(jnp.int32, (C, C), 1)
    diff = ((c - r) if rev else (r - c)).astype(F32)
    mask = (diff > 0) if rev else (diff >= 0)
    dpos = jnp.maximum(diff, 0.0)
    din = jnp.where(mask, jnp.exp(lgh * dpos), 0.0)
    idx = lax.broadcasted_iota(jnp.int32, (C, 1), 0).astype(F32)
    eq = (C - idx) if rev else (idx + 1.0)
    ek = idx if rev else (C - 1.0 - idx)
    qd, kd = jnp.exp(lgh * eq), jnp.exp(lgh * ek)
    cd = jnp.exp(lgh * jnp.full((1, 1), float(C), F32))
    return hm, din, dpos, qd, kd, cd, eq, ek


def _ret_fwd(name, qt, kt, proj, lg, rev):
    S = qt.shape[0]
    C = CHUNK
    TB = _pick(S, (512, 256, 128))
    cb, nb = TB // C, S // TB
    blk = (lambda g: nb - 1 - g) if rev else (lambda g: g)

    def body(lg_ref, q_ref, k_ref, v_ref, o_ref, st_ref, state_sc):
        h, g = pl.program_id(0), pl.program_id(1)

        @pl.when(g == 0)
        def _():
            state_sc[...] = jnp.zeros_like(state_sc)

        hm, din, _, qd, kd, cd, _, _ = _ret_consts(lg_ref[h], h, rev)
        for cc in (reversed(range(cb)) if rev else range(cb)):
            rows = pl.ds(cc * C, C)
            q, k, v = q_ref[rows, :] * hm, k_ref[rows, :] * hm, v_ref[rows, :].astype(MXU)
            st = state_sc[...]
            st_ref[0, cc] = st
            a = _dot(q.astype(MXU), k.astype(MXU), NT) * din
            inner = _dot(a.astype(MXU), v, NN)
            cross = _dot((q * qd).astype(MXU), st.astype(MXU), NN)
            o_ref[rows, :] = inner + cross
            state_sc[...] = st * cd + _dot((k * kd).astype(MXU), v, TN)

    return pl.pallas_call(
        body, name=name, grid=(HEADS, nb),
        in_specs=[pl.BlockSpec(memory_space=pltpu.SMEM),
                  pl.BlockSpec((TB, LANES), lambda h, g: (blk(g), h // 2)),
                  pl.BlockSpec((TB, LANES), lambda h, g: (blk(g), h // 2)),
                  pl.BlockSpec((TB, LANES), lambda h, g: (blk(g), P_VR // LANES + h))],
        out_specs=[pl.BlockSpec((TB, LANES), lambda h, g: (blk(g), h)),
                   pl.BlockSpec((1, cb, LANES, LANES), lambda h, g: (h, blk(g), 0, 0))],
        out_shape=[jax.ShapeDtypeStruct((S, HEADS * LANES), F32), jax.ShapeDtypeStruct((HEADS, S // C, LANES, LANES), F32)],
        scratch_shapes=[pltpu.VMEM((LANES, LANES), F32)],
        compiler_params=pltpu.CompilerParams(dimension_semantics=("parallel", "arbitrary"), vmem_limit_bytes=VMEM_LIMIT),
    )(lg, qt, kt, proj)


def _ret_bwd(name, qt, kt, proj, dret, states, lg, rev):
    S = qt.shape[0]
    C = CHUNK
    TB = _pick(S, (512, 256, 128))
    cb, nb = TB // C, S // TB
    blk = (lambda g: g) if rev else (lambda g: nb - 1 - g)

    def body(lg_ref, q_ref, k_ref, v_ref, do_ref, st_ref, dq_ref, dk_ref, dv_ref, dlg_ref, ds_sc, acc_cc, acc_q, acc_k, acc_s):
        h, g = pl.program_id(0), pl.program_id(1)

        @pl.when(g == 0)
        def _():
            ds_sc[...] = jnp.zeros_like(ds_sc)
            acc_cc[...] = jnp.zeros_like(acc_cc)
            acc_q[...] = jnp.zeros_like(acc_q)
            acc_k[...] = jnp.zeros_like(acc_k)
            acc_s[...] = jnp.zeros_like(acc_s)

        lgh = lg_ref[h]
        hm, din, dpos, qd, kd, cd, eq, ek = _ret_consts(lgh, h, rev)
        for cc in (range(cb) if rev else reversed(range(cb))):
            rows = pl.ds(cc * C, C)
            q, k = q_ref[rows, :] * hm, k_ref[rows, :] * hm
            qb, kb, vb = q.astype(MXU), k.astype(MXU), v_ref[rows, :].astype(MXU)
            dob = do_ref[rows, :].astype(MXU)
            st = st_ref[0, cc]
            dsn = ds_sc[...]
            dsnb = dsn.astype(MXU)
            a = _dot(qb, kb, NT)
            dp = _dot(dob, vb, NT)
            da = (dp * din).astype(MXU)
            dqs = _dot(dob, st.astype(MXU), NT)
            vds = _dot(vb, dsnb, NT)
            dq_ref[rows, :] = (_dot(da, kb, NN) + dqs * qd) * hm
            dk_ref[rows, :] = (_dot(da, qb, TN) + vds * kd) * hm
            dv_ref[rows, :] = _dot((a * din).astype(MXU), dob, TN) + _dot((k * kd).astype(MXU), dsnb, NN)
            ds_sc[...] = dsn * cd + _dot((q * qd).astype(MXU), dob, TN)
            acc_cc[...] += dp * a * din * dpos
            acc_q[...] += dqs * q * (qd * eq)
            acc_k[...] += vds * k * (kd * ek)
            acc_s[...] += dsn * st * (cd * float(C))

        @pl.when(g == nb - 1)
        def _():
            tot = (jnp.sum(acc_cc[...], keepdims=True) + jnp.sum(acc_q[...], keepdims=True)
                   + jnp.sum(acc_k[...], keepdims=True) + jnp.sum(acc_s[...], keepdims=True))
            dlg_ref[0] = jnp.broadcast_to(tot * lgh, (8, LANES))

    full = jax.ShapeDtypeStruct((S, HEADS * LANES), F32)
    hspec = pl.BlockSpec((TB, LANES), lambda h, g: (blk(g), h))
    return pl.pallas_call(
        body, name=name, grid=(HEADS, nb),
        in_specs=[pl.BlockSpec(memory_space=pltpu.SMEM),
                  pl.BlockSpec((TB, LANES), lambda h, g: (blk(g), h // 2)),
                  pl.BlockSpec((TB, LANES), lambda h, g: (blk(g), h // 2)),
                  pl.BlockSpec((TB, LANES), lambda h, g: (blk(g), P_VR // LANES + h)),
                  hspec,
                  pl.BlockSpec((1, cb, LANES, LANES), lambda h, g: (h, blk(g), 0, 0))],
        out_specs=[hspec, hspec, hspec, pl.BlockSpec((1, 8, LANES), lambda h, g: (h, 0, 0))],
        out_shape=[full, full, full, jax.ShapeDtypeStruct((HEADS, 8, LANES), F32)],
        scratch_shapes=[pltpu.VMEM((LANES, LANES), F32), pltpu.VMEM((C, C), F32), pltpu.VMEM((C, LANES), F32),
                        pltpu.VMEM((C, LANES), F32), pltpu.VMEM((LANES, LANES), F32)],
        compiler_params=pltpu.CompilerParams(dimension_semantics=("parallel", "arbitrary"), vmem_limit_bytes=VMEM_LIMIT),
    )(lg, qt, kt, proj, dret, states)


def _rope_consts():
    inv16 = THETA ** (-jnp.arange(16, dtype=F32) / 16)
    inv32 = THETA ** (-jnp.arange(32, dtype=F32) / 32)
    lane = np.arange(LANES)
    z48 = jnp.zeros((48,), F32)
    inv_m = jnp.concatenate([inv16, z48, inv16, z48])[None, :]
    sgn_m = jnp.asarray(np.where(lane < 16, -1.0, np.where((lane >= 64) & (lane < 80), 1.0, 0.0)), F32)[None, :]
    inv_r = jnp.concatenate([inv32] * 4)[None, :]
    sgn_r = jnp.asarray(np.where(lane < 64, -1.0, 1.0), F32)[None, :]
    return inv_m, sgn_m, inv_r, sgn_r


def _local_step(x, pos, tgt, gains, W):
    S = x.shape[0]
    ts = _pick(S, (256, 128))
    ts_wide = _pick(S, (128,))
    R = lambda a, w=None, c=0: (a, ((a.shape[1] if w is None else w), c))
    W_ = lambda a: (a, None)

    win = _win_pad(W["w_in"])
    wq = _wq_pad(W["w_q_b"])
    wk, wv = _wkv_pad(W["w_kv_b"])
    wmla = _wmla_pad(W["w_mla_out"])
    wret, wout, wgu, wdown = W["w_ret_out"], W["w_out"], W["w_gate_up"], W["w_down"]
    gqn, gkn = _qk_pad(gains["g_qn"]), _qk_pad(gains["g_kn"])
    g_mix, g_q_a, g_kv_a, g_ffn = gains["g_mix"], gains["g_q_a"], gains["g_kv_a"], gains["g_ffn"]
    lg_f = -jnp.exp(gains["ret_decay_fwd"][0])
    lg_b = -jnp.exp(gains["ret_decay_bwd"][0])

    consts = list(_rope_consts())
    cosm, sinm, cosr, sinr = _rowwise("rope_tables", _tables_fn, S, ts, [R(pos)] + [W_(c) for c in consts],
                                      [(LANES, F32, LANES, 0)] * 4)

    (h,) = _rowwise("rms_mix", _rmsg_fn, S, ts, [R(x), W_(g_mix)], [(D_MODEL, MXU, D_MODEL, 0)])
    proj = _mm("in_proj", h, win, "nn")
    seg = lambda off, w: (proj, (w, off // w))
    mla_ins = [seg(P_CQ, 256), seg(P_CKV, 128), seg(P_KROPE, 128), R(cosm), R(sinm),
               W_(g_q_a), W_(g_kv_a), W_(gqn), W_(gkn), W_(wq), W_(wk), W_(wv)]
    q, k, v = _rowwise("mla_prep", _mla_prep_fn, S, ts, mla_ins, [(HEADS * LANES, MXU, HEADS * LANES, 0)] * 3)
    o, o_bf, lse = _flash_fwd(q, k, v)
    y_a = _mm("mla_out", o_bf, wmla, "nn")

    ret_ins = [seg(P_QR, 512), seg(P_KR, 512), R(cosr), R(sinr)]
    qt, kt = _rowwise("ret_prep", _ret_prep_fn, S, ts, ret_ins, [(512, F32, 512, 0)] * 2)
    ret_f, st_f = _ret_fwd("ret_fwd_f", qt, kt, proj, lg_f, False)
    ret_b, st_b = _ret_fwd("ret_fwd_b", qt, kt, proj, lg_b, True)
    post_ins = [R(ret_f), R(ret_b), seg(P_GR, 1024)]
    (o_b,) = _rowwise("ret_post", _ret_post_fn, S, ts, post_ins, [(1024, MXU, 1024, 0)])
    y_b = _mm("ret_out", o_b, wret, "nn")

    merge_ins = [seg(P_GATES, 1024), (proj, (1024, 1)), R(y_a), R(y_b)]
    (merged,) = _rowwise("merge", _merge_fn, S, ts, merge_ins, [(D_MODEL, MXU, D_MODEL, 0)])
    x1 = _mm("out_proj", merged, wout, "nn", add=x)
    (h2,) = _rowwise("rms_ffn", _rmsg_fn, S, ts, [R(x1), W_(g_ffn)], [(D_MODEL, MXU, D_MODEL, 0)])
    gu = _mm("gate_up", h2, wgu, "nn")
    (act,) = _rowwise("swiglu", lambda t: _swiglu_fn(t[:, :FFN], t[:, FFN:]), S, ts_wide, [R(gu)], [(FFN, MXU, FFN, 0)])
    x2 = _mm("down_proj", act, wdown, "nn", add=x1)
    dx2, dx2_bf, loss_rows = _rowwise("loss", lambda a, b: (lambda d, l: (d, d, l))(*_loss_fn(a, b)), S, ts, [R(x2), R(tgt)],
                                      [(D_MODEL, F32, D_MODEL, 0), (D_MODEL, MXU, D_MODEL, 0)], accs=[(1, D_MODEL)])

    gW = {}
    gW["w_down"] = _mm("d_w_down", act, dx2_bf, "tn")
    dact = _mm("d_act", dx2_bf, wdown, "nt")

    def glu_bwd(t, da):
        _, vjp = jax.vjp(_swiglu_fn, t[:, :FFN], t[:, FFN:])
        return jnp.concatenate(vjp(da), axis=1)

    (dgu,) = _rowwise("swiglu_bwd", glu_bwd, S, ts_wide, [R(gu), R(dact)], [(2 * FFN, MXU, 2 * FFN, 0)])
    gW["w_gate_up"] = _mm("d_w_gate_up", h2, dgu, "tn")
    dh2 = _mm("d_h2", dgu, wgu, "nt")

    def rms_bwd(xx, g, dh, dres):
        _, vjp = jax.vjp(_rmsg_fn, xx, g)
        dx, dg = vjp(dh)
        dx = dx + dres
        return dx, dx, dg

    dx1, dx1_bf, dg_ffn = _rowwise("rms_ffn_bwd", rms_bwd, S, ts, [R(x1), W_(g_ffn), R(dh2), R(dx2)],
                                   [(D_MODEL, F32, D_MODEL, 0), (D_MODEL, MXU, D_MODEL, 0)], accs=[(1, D_MODEL)])
    gW["w_out"] = _mm("d_w_out", merged, dx1_bf, "tn")
    dmerged = _mm("d_merged", dx1_bf, wout, "nt")

    def merge_bwd(ga, gb, ya, yb, dm):
        _, vjp = jax.vjp(_merge_fn, ga, gb, ya, yb)
        return vjp(dm)

    dga, dgb, dy_a, dy_b = _rowwise("merge_bwd", merge_bwd, S, ts, merge_ins + [R(dmerged)], [(D_MODEL, MXU, D_MODEL, 0)] * 4)
    gW["w_ret_out"] = _mm("d_w_ret_out", o_b, dy_b, "tn")
    do_b = _mm("d_o_b", dy_b, wret, "nt")

    def post_bwd(rf, rb, gr, dob):
        _, vjp = jax.vjp(_ret_post_fn, rf, rb, gr)
        drf, _, dgr = vjp(dob)
        return drf, dgr

    dret, dg_r = _rowwise("ret_post_bwd", post_bwd, S, ts, post_ins + [R(do_b)], [(1024, F32, 1024, 0), (1024, MXU, 1024, 0)])
    dq_f, dk_f, dv_f, dlg_f = _ret_bwd("ret_bwd_f", qt, kt, proj, dret, st_f, lg_f, False)
    dq_b, dk_b, dv_b, dlg_b = _ret_bwd("ret_bwd_b", qt, kt, proj, dret, st_b, lg_b, True)

    def ret_prep_bwd(qr, kr, cosr_, sinr_, dqf, dqb, dkf, dkb, dvf, dvb):
        _, vjp = jax.vjp(lambda a, b: _ret_prep_fn(a, b, cosr_, sinr_), qr, kr)
        pair = lambda t: jnp.concatenate([t[:, 256 * j:256 * j + 128] + t[:, 256 * j + 128:256 * j + 256] for j in range(4)], axis=1)
        dqr, dkr = vjp((pair(dqf + dqb), pair(dkf + dkb)))
        return dqr, dkr, dvf + dvb

    dq_r, dk_r, dv_r = _rowwise("ret_prep_bwd", ret_prep_bwd, S, ts, ret_ins + [R(t) for t in (dq_f, dq_b, dk_f, dk_b, dv_f, dv_b)],
                                [(512, MXU, 512, 0), (512, MXU, 512, 0), (1024, MXU, 1024, 0)])

    gW_mla_p = _mm("d_w_mla_out", o_bf, dy_a, "tn")
    do = _mm("d_o", dy_a, wmla, "nt")
    do_bf, delta = _rowwise("attn_delta", _delta_fn, S, ts, [R(o), R(do)], [(HEADS * LANES, MXU, HEADS * LANES, 0), (HEADS * LANES, F32, HEADS * LANES, 0)])
    dq, dk, dv = _flash_bwd(q, k, v, do_bf, lse, delta)

    def mla_prep_bwd(cq, ckv, kr, cosm_, sinm_, gqa, gkva, gqn_, gkn_, wq_, wk_, wv_, dq_, dk_, dv_):
        f = lambda cq, ckv, kr, gqa, gkva, gqn_, gkn_, wq_, wk_, wv_: _mla_prep_fn(cq, ckv, kr, cosm_, sinm_, gqa, gkva, gqn_, gkn_, wq_, wk_, wv_)
        _, vjp = jax.vjp(f, cq, ckv, kr, gqa, gkva, gqn_, gkn_, wq_.astype(F32), wk_.astype(F32), wv_.astype(F32))
        return vjp((dq_, dk_, dv_))

    mb = _rowwise("mla_prep_bwd", mla_prep_bwd, S, ts_wide, mla_ins + [R(dq), R(dk), R(dv)],
                  [(256, MXU, 256, 0), (128, MXU, 128, 0), (128, MXU, 128, 0)],
                  accs=[(1, 256), (1, 128), (1, LANES), (1, LANES), (256, HEADS * LANES), (128, HEADS * LANES), (128, HEADS * LANES)])
    dc_q, dc_kv, dk_rope, dg_q_a, dg_kv_a, dgqn_p, dgkn_p, dwq_p, dwk_p, dwv_p = mb

    dproj = jnp.concatenate([dga, dgb, dv_r, dg_r, dq_r, dk_r, dc_q, dc_kv, dk_rope], axis=1)
    gwin_p = _mm("d_w_in", h, dproj, "tn")
    dh = _mm("d_h", dproj, win, "nt")
    grad_x, _, dg_mix = _rowwise("rms_mix_bwd", rms_bwd, S, ts, [R(x), W_(g_mix), R(dh), R(dx1)],
                                 [(D_MODEL, F32, D_MODEL, 0), (D_MODEL, MXU, D_MODEL, 0)], accs=[(1, D_MODEL)])

    gW["w_in"] = _win_unpad(gwin_p)
    gW["w_q_b"] = _wq_unpad(dwq_p)
    gW["w_kv_b"] = _wkv_unpad(dwk_p, dwv_p)
    gW["w_mla_out"] = _wmla_unpad(gW_mla_p)
    gG = {"g_mix": dg_mix, "g_q_a": dg_q_a, "g_kv_a": dg_kv_a, "g_qn": _qk_unpad(dgqn_p),
          "g_kn": _qk_unpad(dgkn_p), "ret_decay_fwd": dlg_f[:, 0, 0][None, :], "ret_decay_bwd": dlg_b[:, 0, 0][None, :],
          "g_ffn": dg_ffn}
    return loss_rows, grad_x, gG, gW


MATS = [("w_in", (1024, 5536), 1), ("w_q_b", (256, 768), 1), ("w_kv_b", (128, 1024), 1), ("w_mla_out", (512, 1024), 1),
        ("w_ret_out", (1024, 1024), 0), ("w_out", (1024, 1024), 0), ("w_gate_up", (1024, 5632), 1), ("w_down", (2816, 1024), 0)]
GAINS = [("g_mix", 1024), ("g_q_a", 256), ("g_kv_a", 128), ("g_qn", 96), ("g_kn", 96), ("ret_decay_fwd", 8), ("ret_decay_bwd", 8),
         ("g_ffn", 1024)]
ORDER = ["g_mix", "w_in", "g_q_a", "w_q_b", "g_kv_a", "w_kv_b", "g_qn", "g_kn", "w_mla_out", "ret_decay_fwd", "ret_decay_bwd",
         "w_ret_out", "w_out", "g_ffn", "w_gate_up", "w_down"]
GAIN_LEN = sum(n for _, n in GAINS)
GAIN_PAD = -(-GAIN_LEN // LANES) * LANES


def _pack_gains(d):
    row = jnp.concatenate([d[n].reshape(1, ln).astype(F32) for n, ln in GAINS], axis=1)
    return jnp.pad(row, ((0, 0), (0, GAIN_PAD - GAIN_LEN)))


def _unpack_gains(row):
    out, off = {}, 0
    for n, ln in GAINS:
        out[n] = row[0, off:off + ln]
        off += ln
    return out


def _unshard(pieces, axis):
    if axis == 0:
        return pieces.reshape((N_DEV * pieces.shape[1], pieces.shape[2]))
    return jnp.concatenate([pieces[p] for p in range(N_DEV)], axis=1)


def _reshard(full, axis):
    if axis == 0:
        return full.reshape((N_DEV, full.shape[0] // N_DEV, full.shape[1]))
    c = full.shape[1] // N_DEV
    return jnp.stack([full[:, c * p:c * (p + 1)] for p in range(N_DEV)])


def _all_gather(shards):
    n = len(shards)

    def body(*refs):
        x_refs, out_refs = refs[:n], refs[n:2 * n]
        send_sems, recv_sems, local_sems = refs[2 * n:]
        x, y, c = lax.axis_index("x"), lax.axis_index("y"), lax.axis_index("c")
        me, sibling = (x, y, c), (x, y, 1 - c)
        chips = [(1 - x, y), (x, 1 - y), (1 - x, 1 - y)]

        def slot(a, px, py, pc):
            return out_refs[a].at[4 * px + 2 * py + pc]

        def copy(a, k, block, to, from_input=False):
            return pltpu.make_async_remote_copy(
                src_ref=x_refs[a] if from_input else slot(a, *block), dst_ref=slot(a, *block),
                send_sem=send_sems.at[a, k], recv_sem=recv_sems.at[a, k], device_id=to, device_id_type=pl.DeviceIdType.MESH)

        mine = [pltpu.make_async_copy(x_refs[a], slot(a, *me), local_sems.at[a]) for a in range(n)]
        first = [copy(a, 0, me, sibling, True) for a in range(n)]
        first += [copy(a, 1 + j, me, (*chip, c), True) for j, chip in enumerate(chips) for a in range(n)]
        for cp in mine + first:
            cp.start()
        passed = []
        for j, chip in enumerate(chips):
            for a in range(n):
                copy(a, 1 + j, (*chip, c), me).wait_recv()
                passed.append(copy(a, 4 + j, (*chip, c), sibling))
                passed[-1].start()
        for a in range(n):
            copy(a, 0, sibling, me).wait_recv()
        for j, chip in enumerate(chips):
            for a in range(n):
                copy(a, 4 + j, (*chip, 1 - c), me).wait_recv()
        for cp in first + passed:
            cp.wait_send()
        for cp in mine:
            cp.wait()

    any_spec = pl.BlockSpec(memory_space=pl.ANY)
    return pl.pallas_call(
        body, name="all_gather_weights", out_shape=[jax.ShapeDtypeStruct((N_DEV,) + s.shape, s.dtype) for s in shards],
        in_specs=[any_spec] * n, out_specs=[any_spec] * n,
        scratch_shapes=[pltpu.SemaphoreType.DMA((n, 7)), pltpu.SemaphoreType.DMA((n, 7)), pltpu.SemaphoreType.DMA((n,))],
    )(*shards)


def _all_to_all(pieces):
    n = len(pieces)

    def body(*refs):
        in_refs, out_refs = refs[:n], refs[n:2 * n]
        send_sems, recv_sems, local_sems = refs[2 * n:]
        x, y, c = lax.axis_index("x"), lax.axis_index("y"), lax.axis_index("c")
        my_id = 4 * x + 2 * y + c
        flips = [(fx, fy, fc) for fx in (0, 1) for fy in (0, 1) for fc in (0, 1)][1:]

        def copy(a, kk, f):
            p = (x ^ f[0], y ^ f[1], c ^ f[2])
            return pltpu.make_async_remote_copy(
                src_ref=in_refs[a].at[4 * p[0] + 2 * p[1] + p[2]], dst_ref=out_refs[a].at[my_id],
                send_sem=send_sems.at[a, kk], recv_sem=recv_sems.at[a, kk], device_id=p, device_id_type=pl.DeviceIdType.MESH)

        mine = [pltpu.make_async_copy(in_refs[a].at[my_id], out_refs[a].at[my_id], local_sems.at[a]) for a in range(n)]
        copies = [copy(a, kk, f) for kk, f in enumerate(flips) for a in range(n)]
        for cp in mine + copies:
            cp.start()
        for cp in copies:
            cp.wait_recv()
        for cp in copies:
            cp.wait_send()
        for cp in mine:
            cp.wait()

    any_spec = pl.BlockSpec(memory_space=pl.ANY)
    return pl.pallas_call(
        body, name="all_to_all_grads", out_shape=[jax.ShapeDtypeStruct(p.shape, p.dtype) for p in pieces],
        in_specs=[any_spec] * n, out_specs=[any_spec] * n,
        scratch_shapes=[pltpu.SemaphoreType.DMA((n, 7)), pltpu.SemaphoreType.DMA((n, 7)), pltpu.SemaphoreType.DMA((n,))],
    )(*pieces)


def _adamw(name, parts, w, m, v):
    rows, cols = w.shape
    tr = _pick(rows, (128, 64, 32, 16, 8))
    pspec = pl.BlockSpec((N_DEV, tr, cols), lambda i: (0, i, 0))
    rspec = pl.BlockSpec((tr, cols), lambda i: (i, 0))

    def body(p_ref, w_ref, m_ref, v_ref, g_ref, d_ref, m2_ref, v2_ref):
        g, d, m2, v2 = _adamw_fn([p_ref[s] for s in range(N_DEV)], w_ref[...], m_ref[...], v_ref[...])
        g_ref[...], d_ref[...], m2_ref[...], v2_ref[...] = g, d, m2, v2

    return pl.pallas_call(
        body, name=name, grid=(rows // tr,), in_specs=[pspec, rspec, rspec, rspec], out_specs=[rspec] * 4,
        out_shape=[jax.ShapeDtypeStruct((rows, cols), F32)] * 4,
        compiler_params=pltpu.CompilerParams(dimension_semantics=("parallel",), vmem_limit_bytes=VMEM_LIMIT),
    )(parts, w, m, v)


def kernel(x, positions, g_mix, w_in, g_q_a, w_q_b, g_kv_a, w_kv_b, g_qn, g_kn, w_mla_out, ret_decay_fwd, ret_decay_bwd, w_ret_out, w_out, g_ffn, w_gate_up, w_down, loss_target, m_g_mix, m_w_in, m_g_q_a, m_w_q_b, m_g_kv_a, m_w_kv_b, m_g_qn, m_g_kn, m_w_mla_out, m_ret_decay_fwd, m_ret_decay_bwd, m_w_ret_out, m_w_out, m_g_ffn, m_w_gate_up, m_w_down, v_g_mix, v_w_in, v_g_q_a, v_w_q_b, v_g_kv_a, v_w_kv_b, v_g_qn, v_g_kn, v_w_mla_out, v_ret_decay_fwd, v_ret_decay_bwd, v_w_ret_out, v_w_out, v_g_ffn, v_w_gate_up, v_w_down):
    w = dict(g_mix=g_mix, w_in=w_in, g_q_a=g_q_a, w_q_b=w_q_b, g_kv_a=g_kv_a, w_kv_b=w_kv_b, g_qn=g_qn, g_kn=g_kn, w_mla_out=w_mla_out,
             ret_decay_fwd=ret_decay_fwd, ret_decay_bwd=ret_decay_bwd, w_ret_out=w_ret_out, w_out=w_out, g_ffn=g_ffn,
             w_gate_up=w_gate_up, w_down=w_down)
    m = dict(g_mix=m_g_mix, w_in=m_w_in, g_q_a=m_g_q_a, w_q_b=m_w_q_b, g_kv_a=m_g_kv_a, w_kv_b=m_w_kv_b, g_qn=m_g_qn, g_kn=m_g_kn,
             w_mla_out=m_w_mla_out, ret_decay_fwd=m_ret_decay_fwd, ret_decay_bwd=m_ret_decay_bwd, w_ret_out=m_w_ret_out, w_out=m_w_out,
             g_ffn=m_g_ffn, w_gate_up=m_w_gate_up, w_down=m_w_down)
    v = dict(g_mix=v_g_mix, w_in=v_w_in, g_q_a=v_g_q_a, w_q_b=v_w_q_b, g_kv_a=v_g_kv_a, w_kv_b=v_w_kv_b, g_qn=v_g_qn, g_kn=v_g_kn,
             w_mla_out=v_w_mla_out, ret_decay_fwd=v_ret_decay_fwd, ret_decay_bwd=v_ret_decay_bwd, w_ret_out=v_w_ret_out, w_out=v_w_out,
             g_ffn=v_g_ffn, w_gate_up=v_w_gate_up, w_down=v_w_down)
    gains = {n: w[n].reshape(1, ln) for n, ln in GAINS}

    gathered = _all_gather([w[n].astype(WIRE) for n, _, _ in MATS])
    W = {n: _unshard(g, axis) for (n, _, axis), g in zip(MATS, gathered)}
    S = x.shape[1]
    pos = positions.reshape(S, 1).astype(F32)
    loss_rows, grad_x, gG, gW = _local_step(x.reshape(S, D_MODEL), pos, loss_target.reshape(S, D_MODEL), gains, W)
    loss = lax.psum(jnp.sum(loss_rows), ("x", "y", "c"))

    pieces = [_reshard(gW[n], axis).astype(GWIRE) for n, _, axis in MATS]
    pieces.append(jnp.broadcast_to(_pack_gains(gG)[None], (N_DEV, 1, GAIN_PAD)))
    parts = _all_to_all(pieces)
    out = [dict() for _ in range(4)]
    for (n, _, _), p in zip(MATS, parts):
        for o, r in zip(out, _adamw("adamw_" + n, p, w[n], m[n], v[n])):
            o[n] = r
    for o, r in zip(out, _adamw("adamw_gains", parts[-1], _pack_gains(w), _pack_gains(m), _pack_gains(v))):
        o.update(_unpack_gains(r))
    return (loss, grad_x.reshape(x.shape), *[o[n] for o in out for n in ORDER])
```

```python
import functools

import numpy as np
import jax
import jax.numpy as jnp
from jax import lax
from jax.experimental import pallas as pl
from jax.experimental.pallas import tpu as pltpu

F32 = jnp.float32
MXU = jnp.bfloat16
WIRE = jnp.bfloat16
GWIRE = jnp.bfloat16

N_DEV = 8
D_MODEL = 1024
HEADS = 8
LANES = 128
Q_RANK, KV_RANK = 256, 128
NOPE, ROPE_M, V_M = 64, 32, 64
QK_M = NOPE + ROPE_M
RQK, RV = 64, 128
CHUNK = 128
FFN = 2816
IN_WIDTH = 5536
THETA = 10000.0
EPS = 1e-6
LR, B1, B2, AEPS, WD, STEP = 0.001, 0.9, 0.999, 1e-08, 0.01, 10
VMEM_LIMIT = 56 * 1024 * 1024

NN = ((1,), (0,))
NT = ((1,), (1,))
TN = ((0,), (0,))

P_GATES, P_VR, P_GR, P_QR, P_KR, P_CQ, P_CKV, P_KROPE, P_WIDTH = 0, 2048, 3072, 4096, 4608, 5120, 5376, 5504, 5632
O_CQ, O_CKV, O_KROPE, O_QR, O_KR, O_VR, O_GR, O_GATES = 0, 256, 384, 416, 928, 1440, 2464, 3488


def _dot(a, b, dims):
    return lax.dot_general(a, b, (dims, ((), ())), preferred_element_type=F32)


def _pick(dim, cands):
    for c in cands:
        if dim % c == 0:
            return c
    return dim


def _pairs(t):
    return t.reshape(t.shape[0], 4, 2, 2, 32).transpose(0, 1, 3, 2, 4).reshape(t.shape[0], 512)


def _win_pad(w):
    z = jnp.zeros((w.shape[0], 48), w.dtype)
    kr = w[:, O_KROPE:O_KROPE + 32]
    return jnp.concatenate([w[:, O_GATES:], w[:, O_VR:O_VR + 1024], w[:, O_GR:O_GR + 1024], _pairs(w[:, O_QR:O_QR + 512]),
                            _pairs(w[:, O_KR:O_KR + 512]), w[:, :O_CKV], w[:, O_CKV:O_KROPE], kr[:, :16], z, kr[:, 16:], z], axis=1)


def _win_unpad(g):
    return jnp.concatenate([g[:, P_CQ:P_CQ + 256], g[:, P_CKV:P_CKV + 128], g[:, P_KROPE:P_KROPE + 16], g[:, P_KROPE + 64:P_KROPE + 80],
                            _pairs(g[:, P_QR:P_QR + 512]), _pairs(g[:, P_KR:P_KR + 512]), g[:, P_VR:P_VR + 1024],
                            g[:, P_GR:P_GR + 1024], g[:, P_GATES:P_GATES + 2048]], axis=1)


def _qk_pad(t):
    z = jnp.zeros(t.shape[:-1] + (32,), t.dtype)
    return jnp.concatenate([t[..., 64:80], t[..., 0:48], t[..., 80:96], t[..., 48:64], z], axis=-1)


def _qk_unpad(p):
    return jnp.concatenate([p[..., 16:64], p[..., 80:96], p[..., 0:16], p[..., 64:80]], axis=-1)


def _wq_pad(w):
    return _qk_pad(w.reshape(Q_RANK, HEADS, QK_M)).reshape(Q_RANK, HEADS * LANES)


def _wq_unpad(g):
    return _qk_unpad(g.reshape(Q_RANK, HEADS, LANES)).reshape(Q_RANK, HEADS * QK_M)


def _wkv_pad(w):
    t = w.reshape(KV_RANK, HEADS, NOPE + V_M)
    z = lambda n: jnp.zeros((KV_RANK, HEADS, n), w.dtype)
    wk = jnp.concatenate([z(16), t[..., 0:48], z(16), t[..., 48:64], z(32)], axis=-1)
    wv = jnp.concatenate([t[..., 64:128], z(64)], axis=-1)
    return wk.reshape(KV_RANK, HEADS * LANES), wv.reshape(KV_RANK, HEADS * LANES)


def _wkv_unpad(dwk, dwv):
    k, v = dwk.reshape(KV_RANK, HEADS, LANES), dwv.reshape(KV_RANK, HEADS, LANES)
    return jnp.concatenate([k[..., 16:64], k[..., 80:96], v[..., 0:64]], axis=-1).reshape(KV_RANK, HEADS * (NOPE + V_M))


def _wmla_pad(w):
    t = w.reshape(HEADS, V_M, D_MODEL)
    return jnp.concatenate([t, jnp.zeros_like(t)], axis=1).reshape(HEADS * LANES, D_MODEL)


def _wmla_unpad(g):
    return g.reshape(HEADS, LANES, D_MODEL)[:, :V_M].reshape(HEADS * V_M, D_MODEL)


def _rowwise(name, fn, rows, ts, ins, outs, accs=(), ncol=1):
    n_in, n_out, n_acc = len(ins), len(outs), len(accs)

    def colmap(col):
        if callable(col):
            return lambda i, j: (i, col(j))
        return lambda i, j: (i, col)

    arrays, in_specs = [], []
    for arr, spec in ins:
        arrays.append(arr)
        if spec is None:
            in_specs.append(pl.BlockSpec(arr.shape, functools.partial(lambda i, j, nd: (0,) * nd, nd=arr.ndim)))
        else:
            in_specs.append(pl.BlockSpec((ts, spec[0]), colmap(spec[1])))
    out_shape, out_specs = [], []
    for total, dtype, width, col in outs:
        out_shape.append(jax.ShapeDtypeStruct((rows, total), dtype))
        out_specs.append(pl.BlockSpec((ts, width), colmap(col)))
    for shp in accs:
        out_shape.append(jax.ShapeDtypeStruct(shp, F32))
        out_specs.append(pl.BlockSpec(shp, functools.partial(lambda i, j, nd: (0,) * nd, nd=len(shp))))

    def body(*refs):
        vals = [r[...] for r in refs[:n_in]]
        res = fn(*vals)
        if not isinstance(res, (tuple, list)):
            res = (res,)
        for r, v in zip(refs[n_in:n_in + n_out], res[:n_out]):
            r[...] = v.astype(r.dtype)
        if n_acc:
            first = jnp.logical_and(pl.program_id(0) == 0, pl.program_id(1) == 0)
            for r, v in zip(refs[n_in + n_out:], res[n_out:]):
                @pl.when(first)
                def _(r=r):
                    r[...] = jnp.zeros_like(r)
                r[...] += v.astype(F32)

    res = pl.pallas_call(
        body, name=name, grid=(rows // ts, ncol), in_specs=in_specs, out_specs=out_specs, out_shape=out_shape,
        compiler_params=pltpu.CompilerParams(dimension_semantics=("arbitrary", "arbitrary"), vmem_limit_bytes=VMEM_LIMIT),
    )(*arrays)
    return res


MM_OPERAND_BYTES = 24 * 1024 * 1024


def _mm(name, a, b, mode, add=None):
    if mode == "nn":
        (M, K), N = a.shape, b.shape[1]
    elif mode == "nt":
        (M, K), N = a.shape, b.shape[0]
    else:
        (K, M), N = a.shape, b.shape[1]
    tm = _pick(M, (512, 256, 128)) if mode == "tn" else _pick(M, (1024, 512, 256, 128))
    tn = _pick(N, (512, 256, 128))
    fits = lambda t: 2 * (tm + tn) * t * a.dtype.itemsize <= MM_OPERAND_BYTES
    tk = next(t for t in (K, 4096, 2816, 2048, 1408, 1024, 512, 256, 128) if K % t == 0 and (fits(t) or t == 128))
    nk = K // tk
    dims = {"nn": NN, "nt": NT, "tn": TN}[mode]
    a_spec = pl.BlockSpec((tk, tm), lambda i, j, k: (k, i)) if mode == "tn" else pl.BlockSpec((tm, tk), lambda i, j, k: (i, k))
    b_spec = pl.BlockSpec((tn, tk), lambda i, j, k: (j, k)) if mode == "nt" else pl.BlockSpec((tk, tn), lambda i, j, k: (k, j))
    o_spec = pl.BlockSpec((tm, tn), lambda i, j, k: (i, j))
    has_add = add is not None

    def body(*refs):
        a_ref, b_ref, o_ref = refs[0], refs[1], refs[-1]
        d = _dot(a_ref[...], b_ref[...], dims)
        first = (d + refs[2][...]) if has_add else d
        if nk == 1:
            o_ref[...] = first
        else:
            k = pl.program_id(2)

            @pl.when(k == 0)
            def _():
                o_ref[...] = first

            @pl.when(k > 0)
            def _():
                o_ref[...] += d

    args = [a, b] + ([add] if has_add else [])
    specs = [a_spec, b_spec] + ([o_spec] if has_add else [])
    return pl.pallas_call(
        body, name=name, grid=(M // tm, N // tn, nk), in_specs=specs, out_specs=o_spec,
        out_shape=jax.ShapeDtypeStruct((M, N), F32),
        compiler_params=pltpu.CompilerParams(dimension_semantics=("parallel", "parallel", "arbitrary"), vmem_limit_bytes=VMEM_LIMIT),
    )(*args)


@jax.custom_vjp
def _swap64(x):
    return pltpu.roll(x, 64, 1)


_swap64.defvjp(lambda x: (_swap64(x), None), lambda _, g: (_swap64(g),))


@jax.custom_vjp
def _mxdot(a, b):
    return _dot(a.astype(MXU), b.astype(MXU), NN)


def _mxdot_bwd(res, g):
    a, b = res
    gb = g.astype(MXU)
    return _dot(gb, b.astype(MXU), NT), _dot(a.astype(MXU), gb, TN)


_mxdot.defvjp(lambda a, b: (_mxdot(a, b), (a, b)), _mxdot_bwd)


def _rms(x):
    return x * lax.rsqrt(jnp.mean(x * x, axis=-1, keepdims=True) + EPS)


def _rmsg_fn(x, g):
    return _rms(x) * g


def _silu(x):
    return x * jax.nn.sigmoid(x)


def _tables_fn(pos, inv_m, sgn_m, inv_r, sgn_r):
    am, ar = pos * inv_m, pos * inv_r
    return jnp.cos(am), jnp.sin(am) * sgn_m, jnp.cos(ar), jnp.sin(ar) * sgn_r


def _head_blocks(t):
    return [t[:, LANES * h:LANES * (h + 1)] for h in range(t.shape[1] // LANES)]


def _mla_prep_fn(cq, ckv, kr, cosm, sinm, gqa, gkva, gqn, gkn, wq, wk, wv):
    cqn = _rms(cq) * gqa
    ckvn = _rms(ckv) * gkva
    q_raw = _mxdot(cqn, wq)
    k_raw = _mxdot(ckvn, wk)
    lane = lax.broadcasted_iota(jnp.int32, (1, HEADS * LANES), 1)
    v = _mxdot(ckvn, wv) + (lane % LANES == V_M).astype(F32)

    def norm_rope(blocks, g, extra):
        outs = []
        for b in blocks:
            if extra is not None:
                b = b + extra
            n = b * lax.rsqrt(jnp.sum(b * b, axis=-1, keepdims=True) * (1.0 / QK_M) + EPS) * g
            outs.append(n * cosm + _swap64(n) * sinm)
        return jnp.concatenate(outs, axis=1)

    q = norm_rope(_head_blocks(q_raw), gqn, None)
    k = norm_rope(_head_blocks(k_raw), gkn, kr)
    return q, k, v


def _ret_prep_fn(qr, kr, cosr, sinr):
    def rope(t, scale):
        return jnp.concatenate([(b * cosr + _swap64(b) * sinr) * scale for b in _head_blocks(t)], axis=1)
    return rope(qr, 1.0), rope(kr, RQK ** -0.5)


def _ret_post_fn(rf, rb, gr):
    ret = rf + rb
    outs = []
    for b, g in zip(_head_blocks(ret), _head_blocks(gr)):
        outs.append(_silu(g) * _rms(b))
    return jnp.concatenate(outs, axis=1)


def _merge_fn(ga, gb, ya, yb):
    return jax.nn.sigmoid(ga) * ya + jax.nn.sigmoid(gb) * yb


def _swiglu_fn(gate, up):
    return _silu(gate) * up


def _loss_fn(x2, tgt):
    d = x2 - tgt
    return d * (1.0 / D_MODEL), 0.5 * jnp.sum(d * d, axis=0, keepdims=True) * (1.0 / D_MODEL)


def _adamw_fn(parts, w, m, v):
    g = parts[0].astype(F32)
    for p in range(1, N_DEV):
        g = g + parts[p].astype(F32)
    m2 = B1 * m + (1.0 - B1) * g
    v2 = B2 * v + (1.0 - B2) * jnp.square(g)
    m_hat = m2 / (1.0 - B1 ** STEP)
    v_hat = v2 / (1.0 - B2 ** STEP)
    delta = -LR * (m_hat / (jnp.sqrt(v_hat) + AEPS) + WD * w)
    return g, delta, m2, v2


SCALE = QK_M ** -0.5
LOG2E = 1.4426950408889634
FLASH_ROWS = 32


def _flash_fwd(q, k, v):
    S = q.shape[0]
    tq = _pick(S, (512, 256, 128))
    tk = _pick(S, (2048, 1024, 512, 256, 128))
    ncb = tk // LANES
    mrows = 64
    erows = 16
    c = SCALE * LOG2E

    def body(q_ref, k_ref, v_ref, o_ref, obf_ref, lse_ref, s_sc, p_sc, m_sc, a_sc, acc_sc):
        m_sc[...] = jnp.full_like(m_sc, -jnp.inf)
        acc_sc[...] = jnp.zeros_like(acc_sc)
        qb = q_ref[...]

        def kv_step(j, carry):
            kv_rows = pl.ds(pl.multiple_of(j * tk, tk), tk)
            s_sc[...] = _dot(qb, k_ref[kv_rows, :], NT)

            def max_step(r, carry2):
                rows = pl.ds(pl.multiple_of(r * mrows, mrows), mrows)
                m_prev = m_sc[rows, :]
                m_new = jnp.maximum(m_prev, jnp.broadcast_to(jnp.max(s_sc[rows, :], axis=-1, keepdims=True), (mrows, LANES)))
                a_sc[rows, :] = jnp.exp2((m_prev - m_new) * c)
                m_sc[rows, :] = m_new
                return carry2

            lax.fori_loop(0, tq // mrows, max_step, 0, unroll=4)

            def exp_step(r, carry2):
                rows = pl.ds(pl.multiple_of(r * erows, erows), erows)
                m_b = m_sc[rows, :]
                for cb in range(ncb):
                    sl = slice(LANES * cb, LANES * (cb + 1))
                    p_sc[rows, sl] = jnp.exp2((s_sc[rows, sl] - m_b) * c).astype(p_sc.dtype)
                return carry2

            lax.fori_loop(0, tq // erows, exp_step, 0, unroll=2)
            acc_sc[...] = a_sc[...] * acc_sc[...] + _dot(p_sc[...], v_ref[kv_rows, :], NN)
            return carry

        lax.fori_loop(0, S // tk, kv_step, 0)
        acc = acc_sc[...]
        lane = lax.broadcasted_iota(jnp.int32, (1, LANES), 1)
        l = jnp.sum(jnp.where(lane == V_M, acc, 0.0), axis=-1, keepdims=True)
        o = acc / l
        o_ref[...] = o
        obf_ref[...] = o.astype(obf_ref.dtype)
        lse_ref[...] = m_sc[...] * c + jnp.log2(jnp.broadcast_to(l, (tq, LANES)))

    qspec = pl.BlockSpec((tq, LANES), lambda h, i: (i, h))
    kspec = pl.BlockSpec((S, LANES), lambda h, i: (0, h))
    full = jax.ShapeDtypeStruct((S, HEADS * LANES), F32)
    return pl.pallas_call(
        body, name="flash_fwd", grid=(HEADS, S // tq), in_specs=[qspec, kspec, kspec], out_specs=[qspec, qspec, qspec],
        out_shape=[full, jax.ShapeDtypeStruct((S, HEADS * LANES), MXU), full],
        scratch_shapes=[pltpu.VMEM((tq, tk), F32), pltpu.VMEM((tq, tk), MXU), pltpu.VMEM((tq, LANES), F32),
                        pltpu.VMEM((tq, LANES), F32), pltpu.VMEM((tq, LANES), F32)],
        compiler_params=pltpu.CompilerParams(dimension_semantics=("parallel", "arbitrary"), vmem_limit_bytes=VMEM_LIMIT),
    )(q, k, v)


def _delta_fn(o, do):
    outs = [jnp.broadcast_to(jnp.sum(a * b, axis=-1, keepdims=True), a.shape) for a, b in zip(_head_blocks(o), _head_blocks(do))]
    return do, jnp.concatenate(outs, axis=1)


def _flash_bwd(q, k, v, do, lse, delta):
    S = q.shape[0]
    tq = tk = _pick(S, (512, 256, 128))
    ncb = tk // LANES
    c = SCALE * LOG2E

    def body(q_ref, k_ref, v_ref, do_ref, lse_ref, dl_ref, dq_ref, dk_ref, dv_ref, s_sc, dp_sc, p_sc, ds_sc, dk_sc, dv_sc):
        @pl.when(pl.program_id(1) == 0)
        def _():
            dq_ref[...] = jnp.zeros_like(dq_ref)

        dk_sc[...] = jnp.zeros_like(dk_sc)
        dv_sc[...] = jnp.zeros_like(dv_sc)
        kb, vb = k_ref[...], v_ref[...]

        def q_step(i, carry):
            q_rows = pl.ds(pl.multiple_of(i * tq, tq), tq)
            qb, dob = q_ref[q_rows, :], do_ref[q_rows, :]
            s_sc[...] = _dot(qb, kb, NT)
            dp_sc[...] = _dot(dob, vb, NT)

            def row_step(r, carry2):
                rows = pl.ds(pl.multiple_of(r * FLASH_ROWS, FLASH_ROWS), FLASH_ROWS)
                grows = pl.ds(pl.multiple_of(i * tq + r * FLASH_ROWS, FLASH_ROWS), FLASH_ROWS)
                lse_b, dl_b = lse_ref[grows, :], dl_ref[grows, :]
                for cb in range(ncb):
                    sl = slice(LANES * cb, LANES * (cb + 1))
                    p = jnp.exp2(s_sc[rows, sl] * c - lse_b)
                    p_sc[rows, sl] = p.astype(p_sc.dtype)
                    ds_sc[rows, sl] = (p * (dp_sc[rows, sl] - dl_b) * SCALE).astype(ds_sc.dtype)
                return carry2

            lax.fori_loop(0, tq // FLASH_ROWS, row_step, 0, unroll=2)
            dv_sc[...] += _dot(p_sc[...], dob, TN)
            dk_sc[...] += _dot(ds_sc[...], qb, TN)
            dq_ref[q_rows, :] += _dot(ds_sc[...], kb, NN)
            return carry

        lax.fori_loop(0, S // tq, q_step, 0)
        dk_ref[...] = dk_sc[...]
        dv_ref[...] = dv_sc[...]

    hspec = pl.BlockSpec((S, LANES), lambda h, j: (0, h))
    kspec = pl.BlockSpec((tk, LANES), lambda h, j: (j, h))
    full = jax.ShapeDtypeStruct((S, HEADS * LANES), F32)
    return pl.pallas_call(
        body, name="flash_bwd", grid=(HEADS, S // tk), in_specs=[hspec, kspec, kspec, hspec, hspec, hspec],
        out_specs=[hspec, kspec, kspec], out_shape=[full, full, full],
        scratch_shapes=[pltpu.VMEM((tq, tk), F32), pltpu.VMEM((tq, tk), F32), pltpu.VMEM((tq, tk), MXU), pltpu.VMEM((tq, tk), MXU),
                        pltpu.VMEM((tk, LANES), F32), pltpu.VMEM((tk, LANES), F32)],
        compiler_params=pltpu.CompilerParams(dimension_semantics=("parallel", "arbitrary"), vmem_limit_bytes=VMEM_LIMIT),
    )(q, k, v, do, lse, delta)


def _ret_consts(lgh, head, rev):
    C = CHUNK
    lane = lax.broadcasted_iota(jnp.int32, (1, LANES), 1)
    hm = ((lane // 32) % 2 == head % 2).astype(F32)
    r = lax.broadcasted_iota(jnp.int32, (C, C), 0)
    c = lax.broadcasted_iota(jnp.int32, (C, C), 1)
    diff = ((c - r) if rev else (r - c)).astype(F32)
    mask = (diff > 0) if rev else (diff >= 0)
    dpos = jnp.maximum(diff, 0.0)
    din = jnp.where(mask, jnp.exp(lgh * dpos), 0.0)
    idx = lax.broadcasted_iota(jnp.int32, (C, 1), 0).astype(F32)
    eq = (C - idx) if rev else (idx + 1.0)
    ek = idx if rev else (C - 1.0 - idx)
    qd, kd = jnp.exp(lgh * eq), jnp.exp(lgh * ek)
    cd = jnp.exp(lgh * jnp.full((1, 1), float(C), F32))
    return hm, din, dpos, qd, kd, cd, eq, ek


def _ret_fwd(name, qt, kt, proj, lg, rev):
    S = qt.shape[0]
    C = CHUNK
    TB = _pick(S, (512, 256, 128))
    cb, nb = TB // C, S // TB
    blk = (lambda g: nb - 1 - g) if rev else (lambda g: g)

    def body(lg_ref, q_ref, k_ref, v_ref, o_ref, st_ref, state_sc):
        h, g = pl.program_id(0), pl.program_id(1)

        @pl.when(g == 0)
        def _():
            state_sc[...] = jnp.zeros_like(state_sc)

        hm, din, _, qd, kd, cd, _, _ = _ret_consts(lg_ref[h], h, rev)
        for cc in (reversed(range(cb)) if rev else range(cb)):
            rows = pl.ds(cc * C, C)
            q, k, v = q_ref[rows, :] * hm, k_ref[rows, :] * hm, v_ref[rows, :].astype(MXU)
            st = state_sc[...]
            st_ref[0, cc] = st
            a = _dot(q.astype(MXU), k.astype(MXU), NT) * din
            inner = _dot(a.astype(MXU), v, NN)
            cross = _dot((q * qd).astype(MXU), st.astype(MXU), NN)
            o_ref[rows, :] = inner + cross
            state_sc[...] = st * cd + _dot((k * kd).astype(MXU), v, TN)

    return pl.pallas_call(
        body, name=name, grid=(HEADS, nb),
        in_specs=[pl.BlockSpec(memory_space=pltpu.SMEM),
                  pl.BlockSpec((TB, LANES), lambda h, g: (blk(g), h // 2)),
                  pl.BlockSpec((TB, LANES), lambda h, g: (blk(g), h // 2)),
                  pl.BlockSpec((TB, LANES), lambda h, g: (blk(g), P_VR // LANES + h))],
        out_specs=[pl.BlockSpec((TB, LANES), lambda h, g: (blk(g), h)),
                   pl.BlockSpec((1, cb, LANES, LANES), lambda h, g: (h, blk(g), 0, 0))],
        out_shape=[jax.ShapeDtypeStruct((S, HEADS * LANES), F32), jax.ShapeDtypeStruct((HEADS, S // C, LANES, LANES), F32)],
        scratch_shapes=[pltpu.VMEM((LANES, LANES), F32)],
        compiler_params=pltpu.CompilerParams(dimension_semantics=("parallel", "arbitrary"), vmem_limit_bytes=VMEM_LIMIT),
    )(lg, qt, kt, proj)


def _ret_bwd(name, qt, kt, proj, dret, states, lg, rev):
    S = qt.shape[0]
    C = CHUNK
    TB = _pick(S, (512, 256, 128))
    cb, nb = TB // C, S // TB
    blk = (lambda g: g) if rev else (lambda g: nb - 1 - g)

    def body(lg_ref, q_ref, k_ref, v_ref, do_ref, st_ref, dq_ref, dk_ref, dv_ref, dlg_ref, ds_sc, acc_cc, acc_q, acc_k, acc_s):
        h, g = pl.program_id(0), pl.program_id(1)

        @pl.when(g == 0)
        def _():
            ds_sc[...] = jnp.zeros_like(ds_sc)
            acc_cc[...] = jnp.zeros_like(acc_cc)
            acc_q[...] = jnp.zeros_like(acc_q)
            acc_k[...] = jnp.zeros_like(acc_k)
            acc_s[...] = jnp.zeros_like(acc_s)

        lgh = lg_ref[h]
        hm, din, dpos, qd, kd, cd, eq, ek = _ret_consts(lgh, h, rev)
        for cc in (range(cb) if rev else reversed(range(cb))):
            rows = pl.ds(cc * C, C)
            q, k = q_ref[rows, :] * hm, k_ref[rows, :] * hm
            qb, kb, vb = q.astype(MXU), k.astype(MXU), v_ref[rows, :].astype(MXU)
            dob = do_ref[rows, :].astype(MXU)
            st = st_ref[0, cc]
            dsn = ds_sc[...]
            dsnb = dsn.astype(MXU)
            a = _dot(qb, kb, NT)
            dp = _dot(dob, vb, NT)
            da = (dp * din).astype(MXU)
            dqs = _dot(dob, st.astype(MXU), NT)
            vds = _dot(vb, dsnb, NT)
            dq_ref[rows, :] = (_dot(da, kb, NN) + dqs * qd) * hm
            dk_ref[rows, :] = (_dot(da, qb, TN) + vds * kd) * hm
            dv_ref[rows, :] = _dot((a * din).astype(MXU), dob, TN) + _dot((k * kd).astype(MXU), dsnb, NN)
            ds_sc[...] = dsn * cd + _dot((q * qd).astype(MXU), dob, TN)
            acc_cc[...] += dp * a * din * dpos
            acc_q[...] += dqs * q * (qd * eq)
            acc_k[...] += vds * k * (kd * ek)
            acc_s[...] += dsn * st * (cd * float(C))

        @pl.when(g == nb - 1)
        def _():
            tot = (jnp.sum(acc_cc[...], keepdims=True) + jnp.sum(acc_q[...], keepdims=True)
                   + jnp.sum(acc_k[...], keepdims=True) + jnp.sum(acc_s[...], keepdims=True))
            dlg_ref[0] = jnp.broadcast_to(tot * lgh, (8, LANES))

    full = jax.ShapeDtypeStruct((S, HEADS * LANES), F32)
    hspec = pl.BlockSpec((TB, LANES), lambda h, g: (blk(g), h))
    return pl.pallas_call(
        body, name=name, grid=(HEADS, nb),
        in_specs=[pl.BlockSpec(memory_space=pltpu.SMEM),
                  pl.BlockSpec((TB, LANES), lambda h, g: (blk(g), h // 2)),
                  pl.BlockSpec((TB, LANES), lambda h, g: (blk(g), h // 2)),
                  pl.BlockSpec((TB, LANES), lambda h, g: (blk(g), P_VR // LANES + h)),
                  hspec,
                  pl.BlockSpec((1, cb, LANES, LANES), lambda h, g: (h, blk(g), 0, 0))],
        out_specs=[hspec, hspec, hspec, pl.BlockSpec((1, 8, LANES), lambda h, g: (h, 0, 0))],
        out_shape=[full, full, full, jax.ShapeDtypeStruct((HEADS, 8, LANES), F32)],
        scratch_shapes=[pltpu.VMEM((LANES, LANES), F32), pltpu.VMEM((C, C), F32), pltpu.VMEM((C, LANES), F32),
                        pltpu.VMEM((C, LANES), F32), pltpu.VMEM((LANES, LANES), F32)],
        compiler_params=pltpu.CompilerParams(dimension_semantics=("parallel", "arbitrary"), vmem_limit_bytes=VMEM_LIMIT),
    )(lg, qt, kt, proj, dret, states)


def _rope_consts():
    inv16 = THETA ** (-jnp.arange(16, dtype=F32) / 16)
    inv32 = THETA ** (-jnp.arange(32, dtype=F32) / 32)
    lane = np.arange(LANES)
    z48 = jnp.zeros((48,), F32)
    inv_m = jnp.concatenate([inv16, z48, inv16, z48])[None, :]
    sgn_m = jnp.asarray(np.where(lane < 16, -1.0, np.where((lane >= 64) & (lane < 80), 1.0, 0.0)), F32)[None, :]
    inv_r = jnp.concatenate([inv32] * 4)[None, :]
    sgn_r = jnp.asarray(np.where(lane < 64, -1.0, 1.0), F32)[None, :]
    return inv_m, sgn_m, inv_r, sgn_r


def _local_step(x, pos, tgt, gains, W):
    S = x.shape[0]
    ts = _pick(S, (256, 128))
    ts_wide = _pick(S, (128,))
    R = lambda a, w=None, c=0: (a, ((a.shape[1] if w is None else w), c))
    W_ = lambda a: (a, None)

    win = _win_pad(W["w_in"])
    wq = _wq_pad(W["w_q_b"])
    wk, wv = _wkv_pad(W["w_kv_b"])
    wmla = _wmla_pad(W["w_mla_out"])
    wret, wout, wgu, wdown = W["w_ret_out"], W["w_out"], W["w_gate_up"], W["w_down"]
    gqn, gkn = _qk_pad(gains["g_qn"]), _qk_pad(gains["g_kn"])
    g_mix, g_q_a, g_kv_a, g_ffn = gains["g_mix"], gains["g_q_a"], gains["g_kv_a"], gains["g_ffn"]
    lg_f = -jnp.exp(gains["ret_decay_fwd"][0])
    lg_b = -jnp.exp(gains["ret_decay_bwd"][0])

    consts = list(_rope_consts())
    cosm, sinm, cosr, sinr = _rowwise("rope_tables", _tables_fn, S, ts, [R(pos)] + [W_(c) for c in consts],
                                      [(LANES, F32, LANES, 0)] * 4)

    (h,) = _rowwise("rms_mix", _rmsg_fn, S, ts, [R(x), W_(g_mix)], [(D_MODEL, MXU, D_MODEL, 0)])
    proj = _mm("in_proj", h, win, "nn")
    seg = lambda off, w: (proj, (w, off // w))
    mla_ins = [seg(P_CQ, 256), seg(P_CKV, 128), seg(P_KROPE, 128), R(cosm), R(sinm),
               W_(g_q_a), W_(g_kv_a), W_(gqn), W_(gkn), W_(wq), W_(wk), W_(wv)]
    q, k, v = _rowwise("mla_prep", _mla_prep_fn, S, ts, mla_ins, [(HEADS * LANES, MXU, HEADS * LANES, 0)] * 3)
    o, o_bf, lse = _flash_fwd(q, k, v)
    y_a = _mm("mla_out", o_bf, wmla, "nn")

    ret_ins = [seg(P_QR, 512), seg(P_KR, 512), R(cosr), R(sinr)]
    qt, kt = _rowwise("ret_prep", _ret_prep_fn, S, ts, ret_ins, [(512, F32, 512, 0)] * 2)
    ret_f, st_f = _ret_fwd("ret_fwd_f", qt, kt, proj, lg_f, False)
    ret_b, st_b = _ret_fwd("ret_fwd_b", qt, kt, proj, lg_b, True)
    post_ins = [R(ret_f), R(ret_b), seg(P_GR, 1024)]
    (o_b,) = _rowwise("ret_post", _ret_post_fn, S, ts, post_ins, [(1024, MXU, 1024, 0)])
    y_b = _mm("ret_out", o_b, wret, "nn")

    merge_ins = [seg(P_GATES, 1024), (proj, (1024, 1)), R(y_a), R(y_b)]
    (merged,) = _rowwise("merge", _merge_fn, S, ts, merge_ins, [(D_MODEL, MXU, D_MODEL, 0)])
    x1 = _mm("out_proj", merged, wout, "nn", add=x)
    (h2,) = _rowwise("rms_ffn", _rmsg_fn, S, ts, [R(x1), W_(g_ffn)], [(D_MODEL, MXU, D_MODEL, 0)])
    gu = _mm("gate_up", h2, wgu, "nn")
    (act,) = _rowwise("swiglu", lambda t: _swiglu_fn(t[:, :FFN], t[:, FFN:]), S, ts_wide, [R(gu)], [(FFN, MXU, FFN, 0)])
    x2 = _mm("down_proj", act, wdown, "nn", add=x1)
    dx2, dx2_bf, loss_rows = _rowwise("loss", lambda a, b: (lambda d, l: (d, d, l))(*_loss_fn(a, b)), S, ts, [R(x2), R(tgt)],
                                      [(D_MODEL, F32, D_MODEL, 0), (D_MODEL, MXU, D_MODEL, 0)], accs=[(1, D_MODEL)])

    gW = {}
    gW["w_down"] = _mm("d_w_down", act, dx2_bf, "tn")
    dact = _mm("d_act", dx2_bf, wdown, "nt")

    def glu_bwd(t, da):
        _, vjp = jax.vjp(_swiglu_fn, t[:, :FFN], t[:, FFN:])
        return jnp.concatenate(vjp(da), axis=1)

    (dgu,) = _rowwise("swiglu_bwd", glu_bwd, S, ts_wide, [R(gu), R(dact)], [(2 * FFN, MXU, 2 * FFN, 0)])
    gW["w_gate_up"] = _mm("d_w_gate_up", h2, dgu, "tn")
    dh2 = _mm("d_h2", dgu, wgu, "nt")

    def rms_bwd(xx, g, dh, dres):
        _, vjp = jax.vjp(_rmsg_fn, xx, g)
        dx, dg = vjp(dh)
        dx = dx + dres
        return dx, dx, dg

    dx1, dx1_bf, dg_ffn = _rowwise("rms_ffn_bwd", rms_bwd, S, ts, [R(x1), W_(g_ffn), R(dh2), R(dx2)],
                                   [(D_MODEL, F32, D_MODEL, 0), (D_MODEL, MXU, D_MODEL, 0)], accs=[(1, D_MODEL)])
    gW["w_out"] = _mm("d_w_out", merged, dx1_bf, "tn")
    dmerged = _mm("d_merged", dx1_bf, wout, "nt")

    def merge_bwd(ga, gb, ya, yb, dm):
        _, vjp = jax.vjp(_merge_fn, ga, gb, ya, yb)
        return vjp(dm)

    dga, dgb, dy_a, dy_b = _rowwise("merge_bwd", merge_bwd, S, ts, merge_ins + [R(dmerged)], [(D_MODEL, MXU, D_MODEL, 0)] * 4)
    gW["w_ret_out"] = _mm("d_w_ret_out", o_b, dy_b, "tn")
    do_b = _mm("d_o_b", dy_b, wret, "nt")

    def post_bwd(rf, rb, gr, dob):
        _, vjp = jax.vjp(_ret_post_fn, rf, rb, gr)
        drf, _, dgr = vjp(dob)
        return drf, dgr

    dret, dg_r = _rowwise("ret_post_bwd", post_bwd, S, ts, post_ins + [R(do_b)], [(1024, F32, 1024, 0), (1024, MXU, 1024, 0)])
    dq_f, dk_f, dv_f, dlg_f = _ret_bwd("ret_bwd_f", qt, kt, proj, dret, st_f, lg_f, False)
    dq_b, dk_b, dv_b, dlg_b = _ret_bwd("ret_bwd_b", qt, kt, proj, dret, st_b, lg_b, True)

    def ret_prep_bwd(qr, kr, cosr_, sinr_, dqf, dqb, dkf, dkb, dvf, dvb):
        _, vjp = jax.vjp(lambda a, b: _ret_prep_fn(a, b, cosr_, sinr_), qr, kr)
        pair = lambda t: jnp.concatenate([t[:, 256 * j:256 * j + 128] + t[:, 256 * j + 128:256 * j + 256] for j in range(4)], axis=1)
        dqr, dkr = vjp((pair(dqf + dqb), pair(dkf + dkb)))
        return dqr, dkr, dvf + dvb

    dq_r, dk_r, dv_r = _rowwise("ret_prep_bwd", ret_prep_bwd, S, ts, ret_ins + [R(t) for t in (dq_f, dq_b, dk_f, dk_b, dv_f, dv_b)],
                                [(512, MXU, 512, 0), (512, MXU, 512, 0), (1024, MXU, 1024, 0)])

    gW_mla_p = _mm("d_w_mla_out", o_bf, dy_a, "tn")
    do = _mm("d_o", dy_a, wmla, "nt")
    do_bf, delta = _rowwise("attn_delta", _delta_fn, S, ts, [R(o), R(do)], [(HEADS * LANES, MXU, HEADS * LANES, 0), (HEADS * LANES, F32, HEADS * LANES, 0)])
    dq, dk, dv = _flash_bwd(q, k, v, do_bf, lse, delta)

    def mla_prep_bwd(cq, ckv, kr, cosm_, sinm_, gqa, gkva, gqn_, gkn_, wq_, wk_, wv_, dq_, dk_, dv_):
        f = lambda cq, ckv, kr, gqa, gkva, gqn_, gkn_, wq_, wk_, wv_: _mla_prep_fn(cq, ckv, kr, cosm_, sinm_, gqa, gkva, gqn_, gkn_, wq_, wk_, wv_)
        _, vjp = jax.vjp(f, cq, ckv, kr, gqa, gkva, gqn_, gkn_, wq_.astype(F32), wk_.astype(F32), wv_.astype(F32))
        return vjp((dq_, dk_, dv_))

    mb = _rowwise("mla_prep_bwd", mla_prep_bwd, S, ts_wide, mla_ins + [R(dq), R(dk), R(dv)],
                  [(256, MXU, 256, 0), (128, MXU, 128, 0), (128, MXU, 128, 0)],
                  accs=[(1, 256), (1, 128), (1, LANES), (1, LANES), (256, HEADS * LANES), (128, HEADS * LANES), (128, HEADS * LANES)])
    dc_q, dc_kv, dk_rope, dg_q_a, dg_kv_a, dgqn_p, dgkn_p, dwq_p, dwk_p, dwv_p = mb

    dproj = jnp.concatenate([dga, dgb, dv_r, dg_r, dq_r, dk_r, dc_q, dc_kv, dk_rope], axis=1)
    gwin_p = _mm("d_w_in", h, dproj, "tn")
    dh = _mm("d_h", dproj, win, "nt")
    grad_x, _, dg_mix = _rowwise("rms_mix_bwd", rms_bwd, S, ts, [R(x), W_(g_mix), R(dh), R(dx1)],
                                 [(D_MODEL, F32, D_MODEL, 0), (D_MODEL, MXU, D_MODEL, 0)], accs=[(1, D_MODEL)])

    gW["w_in"] = _win_unpad(gwin_p)
    gW["w_q_b"] = _wq_unpad(dwq_p)
    gW["w_kv_b"] = _wkv_unpad(dwk_p, dwv_p)
    gW["w_mla_out"] = _wmla_unpad(gW_mla_p)
    gG = {"g_mix": dg_mix, "g_q_a": dg_q_a, "g_kv_a": dg_kv_a, "g_qn": _qk_unpad(dgqn_p),
          "g_kn": _qk_unpad(dgkn_p), "ret_decay_fwd": dlg_f[:, 0, 0][None, :], "ret_decay_bwd": dlg_b[:, 0, 0][None, :],
          "g_ffn": dg_ffn}
    return loss_rows, grad_x, gG, gW


MATS = [("w_in", (1024, 5536), 1), ("w_q_b", (256, 768), 1), ("w_kv_b", (128, 1024), 1), ("w_mla_out", (512, 1024), 1),
        ("w_ret_out", (1024, 1024), 0), ("w_out", (1024, 1024), 0), ("w_gate_up", (1024, 5632), 1), ("w_down", (2816, 1024), 0)]
GAINS = [("g_mix", 1024), ("g_q_a", 256), ("g_kv_a", 128), ("g_qn", 96), ("g_kn", 96), ("ret_decay_fwd", 8), ("ret_decay_bwd", 8),
         ("g_ffn", 1024)]
ORDER = ["g_mix", "w_in", "g_q_a", "w_q_b", "g_kv_a", "w_kv_b", "g_qn", "g_kn", "w_mla_out", "ret_decay_fwd", "ret_decay_bwd",
         "w_ret_out", "w_out", "g_ffn", "w_gate_up", "w_down"]
GAIN_LEN = sum(n for _, n in GAINS)
GAIN_PAD = -(-GAIN_LEN // LANES) * LANES


def _pack_gains(d):
    row = jnp.concatenate([d[n].reshape(1, ln).astype(F32) for n, ln in GAINS], axis=1)
    return jnp.pad(row, ((0, 0), (0, GAIN_PAD - GAIN_LEN)))


def _unpack_gains(row):
    out, off = {}, 0
    for n, ln in GAINS:
        out[n] = row[0, off:off + ln]
        off += ln
    return out


def _unshard(pieces, axis):
    if axis == 0:
        return pieces.reshape((N_DEV * pieces.shape[1], pieces.shape[2]))
    return jnp.concatenate([pieces[p] for p in range(N_DEV)], axis=1)


def _reshard(full, axis):
    if axis == 0:
        return full.reshape((N_DEV, full.shape[0] // N_DEV, full.shape[1]))
    c = full.shape[1] // N_DEV
    return jnp.stack([full[:, c * p:c * (p + 1)] for p in range(N_DEV)])


def _all_gather(shards):
    n = len(shards)

    def body(*refs):
        x_refs, out_refs = refs[:n], refs[n:2 * n]
        send_sems, recv_sems, local_sems = refs[2 * n:]
        x, y, c = lax.axis_index("x"), lax.axis_index("y"), lax.axis_index("c")
        me, sibling = (x, y, c), (x, y, 1 - c)
        chips = [(1 - x, y), (x, 1 - y), (1 - x, 1 - y)]

        def slot(a, px, py, pc):
            return out_refs[a].at[4 * px + 2 * py + pc]

        def copy(a, k, block, to, from_input=False):
            return pltpu.make_async_remote_copy(
                src_ref=x_refs[a] if from_input else slot(a, *block), dst_ref=slot(a, *block),
                send_sem=send_sems.at[a, k], recv_sem=recv_sems.at[a, k], device_id=to, device_id_type=pl.DeviceIdType.MESH)

        mine = [pltpu.make_async_copy(x_refs[a], slot(a, *me), local_sems.at[a]) for a in range(n)]
        first = [copy(a, 0, me, sibling, True) for a in range(n)]
        first += [copy(a, 1 + j, me, (*chip, c), True) for j, chip in enumerate(chips) for a in range(n)]
        for cp in mine + first:
            cp.start()
        passed = []
        for j, chip in enumerate(chips):
            for a in range(n):
                copy(a, 1 + j, (*chip, c), me).wait_recv()
                passed.append(copy(a, 4 + j, (*chip, c), sibling))
                passed[-1].start()
        for a in range(n):
            copy(a, 0, sibling, me).wait_recv()
        for j, chip in enumerate(chips):
            for a in range(n):
                copy(a, 4 + j, (*chip, 1 - c), me).wait_recv()
        for cp in first + passed:
            cp.wait_send()
        for cp in mine:
            cp.wait()

    any_spec = pl.BlockSpec(memory_space=pl.ANY)
    return pl.pallas_call(
        body, name="all_gather_weights", out_shape=[jax.ShapeDtypeStruct((N_DEV,) + s.shape, s.dtype) for s in shards],
        in_specs=[any_spec] * n, out_specs=[any_spec] * n,
        scratch_shapes=[pltpu.SemaphoreType.DMA((n, 7)), pltpu.SemaphoreType.DMA((n, 7)), pltpu.SemaphoreType.DMA((n,))],
    )(*shards)


def _all_to_all(pieces):
    n = len(pieces)

    def body(*refs):
        in_refs, out_refs = refs[:n], refs[n:2 * n]
        send_sems, recv_sems, local_sems = refs[2 * n:]
        x, y, c = lax.axis_index("x"), lax.axis_index("y"), lax.axis_index("c")
        my_id = 4 * x + 2 * y + c
        flips = [(fx, fy, fc) for fx in (0, 1) for fy in (0, 1) for fc in (0, 1)][1:]

        def copy(a, kk, f):
            p = (x ^ f[0], y ^ f[1], c ^ f[2])
            return pltpu.make_async_remote_copy(
                src_ref=in_refs[a].at[4 * p[0] + 2 * p[1] + p[2]], dst_ref=out_refs[a].at[my_id],
                send_sem=send_sems.at[a, kk], recv_sem=recv_sems.at[a, kk], device_id=p, device_id_type=pl.DeviceIdType.MESH)

        mine = [pltpu.make_async_copy(in_refs[a].at[my_id], out_refs[a].at[my_id], local_sems.at[a]) for a in range(n)]
        copies = [copy(a, kk, f) for kk, f in enumerate(flips) for a in range(n)]
        for cp in mine + copies:
            cp.start()
        for cp in copies:
            cp.wait_recv()
        for cp in copies:
            cp.wait_send()
        for cp in mine:
            cp.wait()

    any_spec = pl.BlockSpec(memory_space=pl.ANY)
    return pl.pallas_call(
        body, name="all_to_all_grads", out_shape=[jax.ShapeDtypeStruct(p.shape, p.dtype) for p in pieces],
        in_specs=[any_spec] * n, out_specs=[any_spec] * n,
        scratch_shapes=[pltpu.SemaphoreType.DMA((n, 7)), pltpu.SemaphoreType.DMA((n, 7)), pltpu.SemaphoreType.DMA((n,))],
    )(*pieces)


def _adamw(name, parts, w, m, v):
    rows, cols = w.shape
    tr = _pick(rows, (128, 64, 32, 16, 8))
    pspec = pl.BlockSpec((N_DEV, tr, cols), lambda i: (0, i, 0))
    rspec = pl.BlockSpec((tr, cols), lambda i: (i, 0))

    def body(p_ref, w_ref, m_ref, v_ref, g_ref, d_ref, m2_ref, v2_ref):
        g, d, m2, v2 = _adamw_fn([p_ref[s] for s in range(N_DEV)], w_ref[...], m_ref[...], v_ref[...])
        g_ref[...], d_ref[...], m2_ref[...], v2_ref[...] = g, d, m2, v2

    return pl.pallas_call(
        body, name=name, grid=(rows // tr,), in_specs=[pspec, rspec, rspec, rspec], out_specs=[rspec] * 4,
        out_shape=[jax.ShapeDtypeStruct((rows, cols), F32)] * 4,
        compiler_params=pltpu.CompilerParams(dimension_semantics=("parallel",), vmem_limit_bytes=VMEM_LIMIT),
    )(parts, w, m, v)


def kernel(x, positions, g_mix, w_in, g_q_a, w_q_b, g_kv_a, w_kv_b, g_qn, g_kn, w_mla_out, ret_decay_fwd, ret_decay_bwd, w_ret_out, w_out, g_ffn, w_gate_up, w_down, loss_target, m_g_mix, m_w_in, m_g_q_a, m_w_q_b, m_g_kv_a, m_w_kv_b, m_g_qn, m_g_kn, m_w_mla_out, m_ret_decay_fwd, m_ret_decay_bwd, m_w_ret_out, m_w_out, m_g_ffn, m_w_gate_up, m_w_down, v_g_mix, v_w_in, v_g_q_a, v_w_q_b, v_g_kv_a, v_w_kv_b, v_g_qn, v_g_kn, v_w_mla_out, v_ret_decay_fwd, v_ret_decay_bwd, v_w_ret_out, v_w_out, v_g_ffn, v_w_gate_up, v_w_down):
    w = dict(g_mix=g_mix, w_in=w_in, g_q_a=g_q_a, w_q_b=w_q_b, g_kv_a=g_kv_a, w_kv_b=w_kv_b, g_qn=g_qn, g_kn=g_kn, w_mla_out=w_mla_out,
             ret_decay_fwd=ret_decay_fwd, ret_decay_bwd=ret_decay_bwd, w_ret_out=w_ret_out, w_out=w_out, g_ffn=g_ffn,
             w_gate_up=w_gate_up, w_down=w_down)
    m = dict(g_mix=m_g_mix, w_in=m_w_in, g_q_a=m_g_q_a, w_q_b=m_w_q_b, g_kv_a=m_g_kv_a, w_kv_b=m_w_kv_b, g_qn=m_g_qn, g_kn=m_g_kn,
             w_mla_out=m_w_mla_out, ret_decay_fwd=m_ret_decay_fwd, ret_decay_bwd=m_ret_decay_bwd, w_ret_out=m_w_ret_out, w_out=m_w_out,
             g_ffn=m_g_ffn, w_gate_up=m_w_gate_up, w_down=m_w_down)
    v = dict(g_mix=v_g_mix, w_in=v_w_in, g_q_a=v_g_q_a, w_q_b=v_w_q_b, g_kv_a=v_g_kv_a, w_kv_b=v_w_kv_b, g_qn=v_g_qn, g_kn=v_g_kn,
             w_mla_out=v_w_mla_out, ret_decay_fwd=v_ret_decay_fwd, ret_decay_bwd=v_ret_decay_bwd, w_ret_out=v_w_ret_out, w_out=v_w_out,
             g_ffn=v_g_ffn, w_gate_up=v_w_gate_up, w_down=v_w_down)
    gains = {n: w[n].reshape(1, ln) for n, ln in GAINS}

    gathered = _all_gather([w[n].astype(WIRE) for n, _, _ in MATS])
    W = {n: _unshard(g, axis) for (n, _, axis), g in zip(MATS, gathered)}
    S = x.shape[1]
    pos = positions.reshape(S, 1).astype(F32)
    loss_rows, grad_x, gG, gW = _local_step(x.reshape(S, D_MODEL), pos, loss_target.reshape(S, D_MODEL), gains, W)
    loss = lax.psum(jnp.sum(loss_rows), ("x", "y", "c"))

    pieces = [_reshard(gW[n], axis).astype(GWIRE) for n, _, axis in MATS]
    pieces.append(jnp.broadcast_to(_pack_gains(gG)[None], (N_DEV, 1, GAIN_PAD)))
    parts = _all_to_all(pieces)
    out = [dict() for _ in range(4)]
    for (n, _, _), p in zip(MATS, parts):
        for o, r in zip(out, _adamw("adamw_" + n, p, w[n], m[n], v[n])):
            o[n] = r
    for o, r in zip(out, _adamw("adamw_gains", parts[-1], _pack_gains(w), _pack_gains(m), _pack_gains(v))):
        o.update(_unpack_gains(r))
    return (loss, grad_x.reshape(x.shape), *[o[n] for o in out for n in ORDER])
```

```python
import functools

import numpy as np
import jax
import jax.numpy as jnp
from jax import lax
from jax.experimental import pallas as pl
from jax.experimental.pallas import tpu as pltpu

F32 = jnp.float32
MXU = jnp.bfloat16
WIRE = jnp.bfloat16
GWIRE = jnp.bfloat16

N_DEV = 8
D_MODEL = 1024
HEADS = 8
LANES = 128
Q_RANK, KV_RANK = 256, 128
NOPE, ROPE_M, V_M = 64, 32, 64
QK_M = NOPE + ROPE_M
RQK, RV = 64, 128
CHUNK = 128
FFN = 2816
IN_WIDTH = 5536
THETA = 10000.0
EPS = 1e-6
LR, B1, B2, AEPS, WD, STEP = 0.001, 0.9, 0.999, 1e-08, 0.01, 10
VMEM_LIMIT = 56 * 1024 * 1024

NN = ((1,), (0,))
NT = ((1,), (1,))
TN = ((0,), (0,))

P_GATES, P_VR, P_GR, P_QR, P_KR, P_CQ, P_CKV, P_KROPE, P_WIDTH = 0, 2048, 3072, 4096, 4608, 5120, 5376, 5504, 5632
O_CQ, O_CKV, O_KROPE, O_QR, O_KR, O_VR, O_GR, O_GATES = 0, 256, 384, 416, 928, 1440, 2464, 3488


def _dot(a, b, dims):
    return lax.dot_general(a, b, (dims, ((), ())), preferred_element_type=F32)


def _pick(dim, cands):
    for c in cands:
        if dim % c == 0:
            return c
    return dim


def _pairs(t):
    return t.reshape(t.shape[0], 4, 2, 2, 32).transpose(0, 1, 3, 2, 4).reshape(t.shape[0], 512)


def _win_pad(w):
    z = jnp.zeros((w.shape[0], 48), w.dtype)
    kr = w[:, O_KROPE:O_KROPE + 32]
    return jnp.concatenate([w[:, O_GATES:], w[:, O_VR:O_VR + 1024], w[:, O_GR:O_GR + 1024], _pairs(w[:, O_QR:O_QR + 512]),
                            _pairs(w[:, O_KR:O_KR + 512]), w[:, :O_CKV], w[:, O_CKV:O_KROPE], kr[:, :16], z, kr[:, 16:], z], axis=1)


def _win_unpad(g):
    return jnp.concatenate([g[:, P_CQ:P_CQ + 256], g[:, P_CKV:P_CKV + 128], g[:, P_KROPE:P_KROPE + 16], g[:, P_KROPE + 64:P_KROPE + 80],
                            _pairs(g[:, P_QR:P_QR + 512]), _pairs(g[:, P_KR:P_KR + 512]), g[:, P_VR:P_VR + 1024],
                            g[:, P_GR:P_GR + 1024], g[:, P_GATES:P_GATES + 2048]], axis=1)


def _qk_pad(t):
    z = jnp.zeros(t.shape[:-1] + (32,), t.dtype)
    return jnp.concatenate([t[..., 64:80], t[..., 0:48], t[..., 80:96], t[..., 48:64], z], axis=-1)


def _qk_unpad(p):
    return jnp.concatenate([p[..., 16:64], p[..., 80:96], p[..., 0:16], p[..., 64:80]], axis=-1)


def _wq_pad(w):
    return _qk_pad(w.reshape(Q_RANK, HEADS, QK_M)).reshape(Q_RANK, HEADS * LANES)


def _wq_unpad(g):
    return _qk_unpad(g.reshape(Q_RANK, HEADS, LANES)).reshape(Q_RANK, HEADS * QK_M)


def _wkv_pad(w):
    t = w.reshape(KV_RANK, HEADS, NOPE + V_M)
    z = lambda n: jnp.zeros((KV_RANK, HEADS, n), w.dtype)
    wk = jnp.concatenate([z(16), t[..., 0:48], z(16), t[..., 48:64], z(32)], axis=-1)
    wv = jnp.concatenate([t[..., 64:128], z(64)], axis=-1)
    return wk.reshape(KV_RANK, HEADS * LANES), wv.reshape(KV_RANK, HEADS * LANES)


def _wkv_unpad(dwk, dwv):
    k, v = dwk.reshape(KV_RANK, HEADS, LANES), dwv.reshape(KV_RANK, HEADS, LANES)
    return jnp.concatenate([k[..., 16:64], k[..., 80:96], v[..., 0:64]], axis=-1).reshape(KV_RANK, HEADS * (NOPE + V_M))


def _wmla_pad(w):
    t = w.reshape(HEADS, V_M, D_MODEL)
    return jnp.concatenate([t, jnp.zeros_like(t)], axis=1).reshape(HEADS * LANES, D_MODEL)


def _wmla_unpad(g):
    return g.reshape(HEADS, LANES, D_MODEL)[:, :V_M].reshape(HEADS * V_M, D_MODEL)


def _rowwise(name, fn, rows, ts, ins, outs, accs=(), ncol=1):
    n_in, n_out, n_acc = len(ins), len(outs), len(accs)

    def colmap(col):
        if callable(col):
            return lambda i, j: (i, col(j))
        return lambda i, j: (i, col)

    arrays, in_specs = [], []
    for arr, spec in ins:
        arrays.append(arr)
        if spec is None:
            in_specs.append(pl.BlockSpec(arr.shape, functools.partial(lambda i, j, nd: (0,) * nd, nd=arr.ndim)))
        else:
            in_specs.append(pl.BlockSpec((ts, spec[0]), colmap(spec[1])))
    out_shape, out_specs = [], []
    for total, dtype, width, col in outs:
        out_shape.append(jax.ShapeDtypeStruct((rows, total), dtype))
        out_specs.append(pl.BlockSpec((ts, width), colmap(col)))
    for shp in accs:
        out_shape.append(jax.ShapeDtypeStruct(shp, F32))
        out_specs.append(pl.BlockSpec(shp, functools.partial(lambda i, j, nd: (0,) * nd, nd=len(shp))))

    def body(*refs):
        vals = [r[...] for r in refs[:n_in]]
        res = fn(*vals)
        if not isinstance(res, (tuple, list)):
            res = (res,)
        for r, v in zip(refs[n_in:n_in + n_out], res[:n_out]):
            r[...] = v.astype(r.dtype)
        if n_acc:
            first = jnp.logical_and(pl.program_id(0) == 0, pl.program_id(1) == 0)
            for r, v in zip(refs[n_in + n_out:], res[n_out:]):
                @pl.when(first)
                def _(r=r):
                    r[...] = jnp.zeros_like(r)
                r[...] += v.astype(F32)

    res = pl.pallas_call(
        body, name=name, grid=(rows // ts, ncol), in_specs=in_specs, out_specs=out_specs, out_shape=out_shape,
        compiler_params=pltpu.CompilerParams(dimension_semantics=("arbitrary", "arbitrary"), vmem_limit_bytes=VMEM_LIMIT),
    )(*arrays)
    return res


MM_OPERAND_BYTES = 24 * 1024 * 1024


def _mm(name, a, b, mode, add=None):
    if mode == "nn":
        (M, K), N = a.shape, b.shape[1]
    elif mode == "nt":
        (M, K), N = a.shape, b.shape[0]
    else:
        (K, M), N = a.shape, b.shape[1]
    tm = _pick(M, (512, 256, 128)) if mode == "tn" else _pick(M, (1024, 512, 256, 128))
    tn = _pick(N, (512, 256, 128))
    fits = lambda t: 2 * (tm + tn) * t * a.dtype.itemsize <= MM_OPERAND_BYTES
    tk = next(t for t in (K, 4096, 2816, 2048, 1408, 1024, 512, 256, 128) if K % t == 0 and (fits(t) or t == 128))
    nk = K // tk
    dims = {"nn": NN, "nt": NT, "tn": TN}[mode]
    a_spec = pl.BlockSpec((tk, tm), lambda i, j, k: (k, i)) if mode == "tn" else pl.BlockSpec((tm, tk), lambda i, j, k: (i, k))
    b_spec = pl.BlockSpec((tn, tk), lambda i, j, k: (j, k)) if mode == "nt" else pl.BlockSpec((tk, tn), lambda i, j, k: (k, j))
    o_spec = pl.BlockSpec((tm, tn), lambda i, j, k: (i, j))
    has_add = add is not None

    def body(*refs):
        a_ref, b_ref, o_ref = refs[0], refs[1], refs[-1]
        d = _dot(a_ref[...], b_ref[...], dims)
        first = (d + refs[2][...]) if has_add else d
        if nk == 1:
            o_ref[...] = first
        else:
            k = pl.program_id(2)

            @pl.when(k == 0)
            def _():
                o_ref[...] = first

            @pl.when(k > 0)
            def _():
                o_ref[...] += d

    args = [a, b] + ([add] if has_add else [])
    specs = [a_spec, b_spec] + ([o_spec] if has_add else [])
    return pl.pallas_call(
        body, name=name, grid=(M // tm, N // tn, nk), in_specs=specs, out_specs=o_spec,
        out_shape=jax.ShapeDtypeStruct((M, N), F32),
        compiler_params=pltpu.CompilerParams(dimension_semantics=("parallel", "parallel", "arbitrary"), vmem_limit_bytes=VMEM_LIMIT),
    )(*args)


@jax.custom_vjp
def _swap64(x):
    return pltpu.roll(x, 64, 1)


_swap64.defvjp(lambda x: (_swap64(x), None), lambda _, g: (_swap64(g),))


@jax.custom_vjp
def _mxdot(a, b):
    return _dot(a.astype(MXU), b.astype(MXU), NN)


def _mxdot_bwd(res, g):
    a, b = res
    gb = g.astype(MXU)
    return _dot(gb, b.astype(MXU), NT), _dot(a.astype(MXU), gb, TN)


_mxdot.defvjp(lambda a, b: (_mxdot(a, b), (a, b)), _mxdot_bwd)


def _rms(x):
    return x * lax.rsqrt(jnp.mean(x * x, axis=-1, keepdims=True) + EPS)


def _rmsg_fn(x, g):
    return _rms(x) * g


def _silu(x):
    return x * jax.nn.sigmoid(x)


def _tables_fn(pos, inv_m, sgn_m, inv_r, sgn_r):
    am, ar = pos * inv_m, pos * inv_r
    return jnp.cos(am), jnp.sin(am) * sgn_m, jnp.cos(ar), jnp.sin(ar) * sgn_r


def _head_blocks(t):
    return [t[:, LANES * h:LANES * (h + 1)] for h in range(t.shape[1] // LANES)]


def _mla_prep_fn(cq, ckv, kr, cosm, sinm, gqa, gkva, gqn, gkn, wq, wk, wv):
    cqn = _rms(cq) * gqa
    ckvn = _rms(ckv) * gkva
    q_raw = _mxdot(cqn, wq)
    k_raw = _mxdot(ckvn, wk)
    lane = lax.broadcasted_iota(jnp.int32, (1, HEADS * LANES), 1)
    v = _mxdot(ckvn, wv) + (lane % LANES == V_M).astype(F32)

    def norm_rope(blocks, g, extra):
        outs = []
        for b in blocks:
            if extra is not None:
                b = b + extra
            n = b * lax.rsqrt(jnp.sum(b * b, axis=-1, keepdims=True) * (1.0 / QK_M) + EPS) * g
            outs.append(n * cosm + _swap64(n) * sinm)
        return jnp.concatenate(outs, axis=1)

    q = norm_rope(_head_blocks(q_raw), gqn, None)
    k = norm_rope(_head_blocks(k_raw), gkn, kr)
    return q, k, v


def _ret_prep_fn(qr, kr, cosr, sinr):
    def rope(t, scale):
        return jnp.concatenate([(b * cosr + _swap64(b) * sinr) * scale for b in _head_blocks(t)], axis=1)
    return rope(qr, 1.0), rope(kr, RQK ** -0.5)


def _ret_post_fn(rf, rb, gr):
    ret = rf + rb
    outs = []
    for b, g in zip(_head_blocks(ret), _head_blocks(gr)):
        outs.append(_silu(g) * _rms(b))
    return jnp.concatenate(outs, axis=1)


def _merge_fn(ga, gb, ya, yb):
    return jax.nn.sigmoid(ga) * ya + jax.nn.sigmoid(gb) * yb


def _swiglu_fn(gate, up):
    return _silu(gate) * up


def _loss_fn(x2, tgt):
    d = x2 - tgt
    return d * (1.0 / D_MODEL), 0.5 * jnp.sum(d * d, axis=0, keepdims=True) * (1.0 / D_MODEL)


def _adamw_fn(parts, w, m, v):
    g = parts[0].astype(F32)
    for p in range(1, N_DEV):
        g = g + parts[p].astype(F32)
    m2 = B1 * m + (1.0 - B1) * g
    v2 = B2 * v + (1.0 - B2) * jnp.square(g)
    m_hat = m2 / (1.0 - B1 ** STEP)
    v_hat = v2 / (1.0 - B2 ** STEP)
    delta = -LR * (m_hat / (jnp.sqrt(v_hat) + AEPS) + WD * w)
    return g, delta, m2, v2


SCALE = QK_M ** -0.5
LOG2E = 1.4426950408889634
FLASH_ROWS = 32


def _flash_fwd(q, k, v):
    S = q.shape[0]
    tq = tk = _pick(S, (512, 256, 128))
    ncb = tk // LANES
    nkv = S // tk
    assert nkv % 2 == 0, "kv tiles are processed in pairs"
    mrows = 64
    c = SCALE * LOG2E

    def body(q_ref, k_ref, v_ref, o_ref, obf_ref, lse_ref, s_a, p_a, s_b, p_b, m_sc, a_sc, acc_sc):
        m_sc[...] = jnp.full_like(m_sc, -jnp.inf)
        acc_sc[...] = jnp.zeros_like(acc_sc)
        qb = q_ref[...]

        def scores(j, s_buf):
            s_buf[...] = _dot(qb, k_ref[pl.ds(pl.multiple_of(j * tk, tk), tk), :], NT)

        def stage(j, s_buf, p_buf, s_next):
            scores(jnp.minimum(j + 1, nkv - 1), s_next)
            for r in range(tq // mrows):
                rows = slice(r * mrows, (r + 1) * mrows)
                cols = [s_buf[rows, LANES * cb:LANES * (cb + 1)] for cb in range(ncb)]
                m_prev = m_sc[rows, :]
                row_max = jnp.max(functools.reduce(jnp.maximum, cols), axis=-1, keepdims=True)
                m_new = jnp.maximum(m_prev, jnp.broadcast_to(row_max, (mrows, LANES)))
                a_sc[rows, :] = jnp.exp2((m_prev - m_new) * c)
                m_sc[rows, :] = m_new
                for cb in range(ncb):
                    p_buf[rows, LANES * cb:LANES * (cb + 1)] = jnp.exp2((cols[cb] - m_new) * c).astype(p_buf.dtype)
            acc_sc[...] = a_sc[...] * acc_sc[...] + _dot(p_buf[...], v_ref[pl.ds(pl.multiple_of(j * tk, tk), tk), :], NN)

        scores(0, s_a)

        def pair_step(t, carry):
            stage(2 * t, s_a, p_a, s_b)
            stage(2 * t + 1, s_b, p_b, s_a)
            return carry

        lax.fori_loop(0, nkv // 2, pair_step, 0)
        acc = acc_sc[...]
        lane = lax.broadcasted_iota(jnp.int32, (1, LANES), 1)
        l = jnp.sum(jnp.where(lane == V_M, acc, 0.0), axis=-1, keepdims=True)
        o = acc / l
        o_ref[...] = o
        obf_ref[...] = o.astype(obf_ref.dtype)
        lse_ref[...] = m_sc[...] * c + jnp.log2(jnp.broadcast_to(l, (tq, LANES)))

    qspec = pl.BlockSpec((tq, LANES), lambda h, i: (i, h))
    kspec = pl.BlockSpec((S, LANES), lambda h, i: (0, h))
    full = jax.ShapeDtypeStruct((S, HEADS * LANES), F32)
    return pl.pallas_call(
        body, name="flash_fwd", grid=(HEADS, S // tq), in_specs=[qspec, kspec, kspec], out_specs=[qspec, qspec, qspec],
        out_shape=[full, jax.ShapeDtypeStruct((S, HEADS * LANES), MXU), full],
        scratch_shapes=[pltpu.VMEM((tq, tk), F32), pltpu.VMEM((tq, tk), MXU)] * 2 + [pltpu.VMEM((tq, LANES), F32)] * 3,
        compiler_params=pltpu.CompilerParams(dimension_semantics=("parallel", "arbitrary"), vmem_limit_bytes=VMEM_LIMIT),
    )(q, k, v)


def _delta_fn(o, do):
    outs = [jnp.broadcast_to(jnp.sum(a * b, axis=-1, keepdims=True), a.shape) for a, b in zip(_head_blocks(o), _head_blocks(do))]
    return do, jnp.concatenate(outs, axis=1)


def _flash_bwd(q, k, v, do, lse, delta):
    S = q.shape[0]
    tq = tk = _pick(S, (512, 256, 128))
    ncb = tk // LANES
    c = SCALE * LOG2E

    nq = S // tq
    assert nq % 2 == 0, "q tiles are processed in pairs"

    def body(q_ref, k_ref, v_ref, do_ref, lse_ref, dl_ref, dq_ref, dk_ref, dv_ref, s_a, dp_a, p_a, ds_a, s_b, dp_b, p_b, ds_b, dk_sc, dv_sc):
        @pl.when(pl.program_id(1) == 0)
        def _():
            dq_ref[...] = jnp.zeros_like(dq_ref)

        dk_sc[...] = jnp.zeros_like(dk_sc)
        dv_sc[...] = jnp.zeros_like(dv_sc)
        kb, vb = k_ref[...], v_ref[...]

        def scores(i, s_buf, dp_buf):
            q_rows = pl.ds(pl.multiple_of(i * tq, tq), tq)
            s_buf[...] = _dot(q_ref[q_rows, :], kb, NT)
            dp_buf[...] = _dot(do_ref[q_rows, :], vb, NT)

        def stage(i, s_buf, dp_buf, p_buf, ds_buf, s_next, dp_next):
            scores(jnp.minimum(i + 1, nq - 1), s_next, dp_next)
            for r in range(tq // FLASH_ROWS):
                rows = slice(r * FLASH_ROWS, (r + 1) * FLASH_ROWS)
                grows = pl.ds(pl.multiple_of(i * tq + r * FLASH_ROWS, FLASH_ROWS), FLASH_ROWS)
                lse_b, dl_b = lse_ref[grows, :], dl_ref[grows, :]
                for cb in range(ncb):
                    sl = slice(LANES * cb, LANES * (cb + 1))
                    p = jnp.exp2(s_buf[rows, sl] * c - lse_b)
                    p_buf[rows, sl] = p.astype(p_buf.dtype)
                    ds_buf[rows, sl] = (p * (dp_buf[rows, sl] - dl_b) * SCALE).astype(ds_buf.dtype)
            q_rows = pl.ds(pl.multiple_of(i * tq, tq), tq)
            dv_sc[...] += _dot(p_buf[...], do_ref[q_rows, :], TN)
            dk_sc[...] += _dot(ds_buf[...], q_ref[q_rows, :], TN)
            dq_ref[q_rows, :] += _dot(ds_buf[...], kb, NN)

        scores(0, s_a, dp_a)

        def pair_step(t, carry):
            stage(2 * t, s_a, dp_a, p_a, ds_a, s_b, dp_b)
            stage(2 * t + 1, s_b, dp_b, p_b, ds_b, s_a, dp_a)
            return carry

        lax.fori_loop(0, nq // 2, pair_step, 0)
        dk_ref[...] = dk_sc[...]
        dv_ref[...] = dv_sc[...]

    hspec = pl.BlockSpec((S, LANES), lambda h, j: (0, h))
    kspec = pl.BlockSpec((tk, LANES), lambda h, j: (j, h))
    full = jax.ShapeDtypeStruct((S, HEADS * LANES), F32)
    tile_bufs = [pltpu.VMEM((tq, tk), F32), pltpu.VMEM((tq, tk), F32), pltpu.VMEM((tq, tk), MXU), pltpu.VMEM((tq, tk), MXU)]
    return pl.pallas_call(
        body, name="flash_bwd", grid=(HEADS, S // tk), in_specs=[hspec, kspec, kspec, hspec, hspec, hspec],
        out_specs=[hspec, kspec, kspec], out_shape=[full, full, full],
        scratch_shapes=tile_bufs + tile_bufs + [pltpu.VMEM((tk, LANES), F32), pltpu.VMEM((tk, LANES), F32)],
        compiler_params=pltpu.CompilerParams(dimension_semantics=("parallel", "arbitrary"), vmem_limit_bytes=VMEM_LIMIT),
    )(q, k, v, do, lse, delta)


def _ret_consts(lgh, head, rev):
    C = CHUNK
    lane = lax.broadcasted_iota(jnp.int32, (1, LANES), 1)
    hm = ((lane // 32) % 2 == head % 2).astype(F32)
    r = lax.broadcasted_iota(jnp.int32, (C, C), 0)
    c = lax.broadcasted_iota(jnp.int32, (C, C), 1)
    diff = ((c - r) if rev else (r - c)).astype(F32)
    mask = (diff > 0) if rev else (diff >= 0)
    dpos = jnp.maximum(diff, 0.0)
    din = jnp.where(mask, jnp.exp(lgh * dpos), 0.0)
    idx = lax.broadcasted_iota(jnp.int32, (C, 1), 0).astype(F32)
    eq = (C - idx) if rev else (idx + 1.0)
    ek = idx if rev else (C - 1.0 - idx)
    qd, kd = jnp.exp(lgh * eq), jnp.exp(lgh * ek)
    cd = jnp.exp(lgh * jnp.full((1, 1), float(C), F32))
    return hm, din, dpos, qd, kd, cd, eq, ek


def _ret_fwd(name, qt, kt, proj, lg, rev):
    S = qt.shape[0]
    C = CHUNK
    TB = _pick(S, (512, 256, 128))
    cb, nb = TB // C, S // TB
    blk = (lambda g: nb - 1 - g) if rev else (lambda g: g)

    def body(lg_ref, q_ref, k_ref, v_ref, o_ref, st_ref, state_sc):
        h, g = pl.program_id(0), pl.program_id(1)

        @pl.when(g == 0)
        def _():
            state_sc[...] = jnp.zeros_like(state_sc)

        hm, din, _, qd, kd, cd, _, _ = _ret_consts(lg_ref[h], h, rev)
        for cc in (reversed(range(cb)) if rev else range(cb)):
            rows = pl.ds(cc * C, C)
            q, k, v = q_ref[rows, :] * hm, k_ref[rows, :] * hm, v_ref[rows, :].astype(MXU)
            st = state_sc[...]
            st_ref[0, cc] = st
            a = _dot(q.astype(MXU), k.astype(MXU), NT) * din
            inner = _dot(a.astype(MXU), v, NN)
            cross = _dot((q * qd).astype(MXU), st.astype(MXU), NN)
            o_ref[rows, :] = inner + cross
            state_sc[...] = st * cd + _dot((k * kd).astype(MXU), v, TN)

    return pl.pallas_call(
        body, name=name, grid=(HEADS, nb),
        in_specs=[pl.BlockSpec(memory_space=pltpu.SMEM),
                  pl.BlockSpec((TB, LANES), lambda h, g: (blk(g), h // 2)),
                  pl.BlockSpec((TB, LANES), lambda h, g: (blk(g), h // 2)),
                  pl.BlockSpec((TB, LANES), lambda h, g: (blk(g), P_VR // LANES + h))],
        out_specs=[pl.BlockSpec((TB, LANES), lambda h, g: (blk(g), h)),
                   pl.BlockSpec((1, cb, LANES, LANES), lambda h, g: (h, blk(g), 0, 0))],
        out_shape=[jax.ShapeDtypeStruct((S, HEADS * LANES), F32), jax.ShapeDtypeStruct((HEADS, S // C, LANES, LANES), F32)],
        scratch_shapes=[pltpu.VMEM((LANES, LANES), F32)],
        compiler_params=pltpu.CompilerParams(dimension_semantics=("parallel", "arbitrary"), vmem_limit_bytes=VMEM_LIMIT),
    )(lg, qt, kt, proj)


def _ret_bwd(name, qt, kt, proj, dret, states, lg, rev):
    S = qt.shape[0]
    C = CHUNK
    TB = _pick(S, (512, 256, 128))
    cb, nb = TB // C, S // TB
    blk = (lambda g: g) if rev else (lambda g: nb - 1 - g)

    def body(lg_ref, q_ref, k_ref, v_ref, do_ref, st_ref, dq_ref, dk_ref, dv_ref, dlg_ref, ds_sc, acc_cc, acc_q, acc_k, acc_s):
        h, g = pl.program_id(0), pl.program_id(1)

        @pl.when(g == 0)
        def _():
            ds_sc[...] = jnp.zeros_like(ds_sc)
            acc_cc[...] = jnp.zeros_like(acc_cc)
            acc_q[...] = jnp.zeros_like(acc_q)
            acc_k[...] = jnp.zeros_like(acc_k)
            acc_s[...] = jnp.zeros_like(acc_s)

        lgh = lg_ref[h]
        hm, din, dpos, qd, kd, cd, eq, ek = _ret_consts(lgh, h, rev)
        for cc in (range(cb) if rev else reversed(range(cb))):
            rows = pl.ds(cc * C, C)
            q, k = q_ref[rows, :] * hm, k_ref[rows, :] * hm
            qb, kb, vb = q.astype(MXU), k.astype(MXU), v_ref[rows, :].astype(MXU)
            dob = do_ref[rows, :].astype(MXU)
            st = st_ref[0, cc]
            dsn = ds_sc[...]
            dsnb = dsn.astype(MXU)
            a = _dot(qb, kb, NT)
            dp = _dot(dob, vb, NT)
            da = (dp * din).astype(MXU)
            dqs = _dot(dob, st.astype(MXU), NT)
            vds = _dot(vb, dsnb, NT)
            dq_ref[rows, :] = (_dot(da, kb, NN) + dqs * qd) * hm
            dk_ref[rows, :] = (_dot(da, qb, TN) + vds * kd) * hm
            dv_ref[rows, :] = _dot((a * din).astype(MXU), dob, TN) + _dot((k * kd).astype(MXU), dsnb, NN)
            ds_sc[...] = dsn * cd + _dot((q * qd).astype(MXU), dob, TN)
            acc_cc[...] += dp * a * din * dpos
            acc_q[...] += dqs * q * (qd * eq)
            acc_k[...] += vds * k * (kd * ek)
            acc_s[...] += dsn * st * (cd * float(C))

        @pl.when(g == nb - 1)
        def _():
            tot = (jnp.sum(acc_cc[...], keepdims=True) + jnp.sum(acc_q[...], keepdims=True)
                   + jnp.sum(acc_k[...], keepdims=True) + jnp.sum(acc_s[...], keepdims=True))
            dlg_ref[0] = jnp.broadcast_to(tot * lgh, (8, LANES))

    full = jax.ShapeDtypeStruct((S, HEADS * LANES), F32)
    hspec = pl.BlockSpec((TB, LANES), lambda h, g: (blk(g), h))
    return pl.pallas_call(
        body, name=name, grid=(HEADS, nb),
        in_specs=[pl.BlockSpec(memory_space=pltpu.SMEM),
                  pl.BlockSpec((TB, LANES), lambda h, g: (blk(g), h // 2)),
                  pl.BlockSpec((TB, LANES), lambda h, g: (blk(g), h // 2)),
                  pl.BlockSpec((TB, LANES), lambda h, g: (blk(g), P_VR // LANES + h)),
                  hspec,
                  pl.BlockSpec((1, cb, LANES, LANES), lambda h, g: (h, blk(g), 0, 0))],
        out_specs=[hspec, hspec, hspec, pl.BlockSpec((1, 8, LANES), lambda h, g: (h, 0, 0))],
        out_shape=[full, full, full, jax.ShapeDtypeStruct((HEADS, 8, LANES), F32)],
        scratch_shapes=[pltpu.VMEM((LANES, LANES), F32), pltpu.VMEM((C, C), F32), pltpu.VMEM((C, LANES), F32),
                        pltpu.VMEM((C, LANES), F32), pltpu.VMEM((LANES, LANES), F32)],
        compiler_params=pltpu.CompilerParams(dimension_semantics=("parallel", "arbitrary"), vmem_limit_bytes=VMEM_LIMIT),
    )(lg, qt, kt, proj, dret, states)


def _rope_consts():
    inv16 = THETA ** (-jnp.arange(16, dtype=F32) / 16)
    inv32 = THETA ** (-jnp.arange(32, dtype=F32) / 32)
    lane = np.arange(LANES)
    z48 = jnp.zeros((48,), F32)
    inv_m = jnp.concatenate([inv16, z48, inv16, z48])[None, :]
    sgn_m = jnp.asarray(np.where(lane < 16, -1.0, np.where((lane >= 64) & (lane < 80), 1.0, 0.0)), F32)[None, :]
    inv_r = jnp.concatenate([inv32] * 4)[None, :]
    sgn_r = jnp.asarray(np.where(lane < 64, -1.0, 1.0), F32)[None, :]
    return inv_m, sgn_m, inv_r, sgn_r


def _local_step(x, pos, tgt, gains, W):
    S = x.shape[0]
    ts = _pick(S, (256, 128))
    ts_wide = _pick(S, (128,))
    R = lambda a, w=None, c=0: (a, ((a.shape[1] if w is None else w), c))
    W_ = lambda a: (a, None)

    win = _win_pad(W["w_in"])
    wq = _wq_pad(W["w_q_b"])
    wk, wv = _wkv_pad(W["w_kv_b"])
    wmla = _wmla_pad(W["w_mla_out"])
    wret, wout, wgu, wdown = W["w_ret_out"], W["w_out"], W["w_gate_up"], W["w_down"]
    gqn, gkn = _qk_pad(gains["g_qn"]), _qk_pad(gains["g_kn"])
    g_mix, g_q_a, g_kv_a, g_ffn = gains["g_mix"], gains["g_q_a"], gains["g_kv_a"], gains["g_ffn"]
    lg_f = -jnp.exp(gains["ret_decay_fwd"][0])
    lg_b = -jnp.exp(gains["ret_decay_bwd"][0])

    consts = list(_rope_consts())
    cosm, sinm, cosr, sinr = _rowwise("rope_tables", _tables_fn, S, ts, [R(pos)] + [W_(c) for c in consts],
                                      [(LANES, F32, LANES, 0)] * 4)

    (h,) = _rowwise("rms_mix", _rmsg_fn, S, ts, [R(x), W_(g_mix)], [(D_MODEL, MXU, D_MODEL, 0)])
    proj = _mm("in_proj", h, win, "nn")
    seg = lambda off, w: (proj, (w, off // w))
    mla_ins = [seg(P_CQ, 256), seg(P_CKV, 128), seg(P_KROPE, 128), R(cosm), R(sinm),
               W_(g_q_a), W_(g_kv_a), W_(gqn), W_(gkn), W_(wq), W_(wk), W_(wv)]
    q, k, v = _rowwise("mla_prep", _mla_prep_fn, S, ts, mla_ins, [(HEADS * LANES, MXU, HEADS * LANES, 0)] * 3)
    o, o_bf, lse = _flash_fwd(q, k, v)
    y_a = _mm("mla_out", o_bf, wmla, "nn")

    ret_ins = [seg(P_QR, 512), seg(P_KR, 512), R(cosr), R(sinr)]
    qt, kt = _rowwise("ret_prep", _ret_prep_fn, S, ts, ret_ins, [(512, F32, 512, 0)] * 2)
    ret_f, st_f = _ret_fwd("ret_fwd_f", qt, kt, proj, lg_f, False)
    ret_b, st_b = _ret_fwd("ret_fwd_b", qt, kt, proj, lg_b, True)
    post_ins = [R(ret_f), R(ret_b), seg(P_GR, 1024)]
    (o_b,) = _rowwise("ret_post", _ret_post_fn, S, ts, post_ins, [(1024, MXU, 1024, 0)])
    y_b = _mm("ret_out", o_b, wret, "nn")

    merge_ins = [seg(P_GATES, 1024), (proj, (1024, 1)), R(y_a), R(y_b)]
    (merged,) = _rowwise("merge", _merge_fn, S, ts, merge_ins, [(D_MODEL, MXU, D_MODEL, 0)])
    x1 = _mm("out_proj", merged, wout, "nn", add=x)
    (h2,) = _rowwise("rms_ffn", _rmsg_fn, S, ts, [R(x1), W_(g_ffn)], [(D_MODEL, MXU, D_MODEL, 0)])
    gu = _mm("gate_up", h2, wgu, "nn")
    (act,) = _rowwise("swiglu", lambda t: _swiglu_fn(t[:, :FFN], t[:, FFN:]), S, ts_wide, [R(gu)], [(FFN, MXU, FFN, 0)])
    x2 = _mm("down_proj", act, wdown, "nn", add=x1)
    dx2, dx2_bf, loss_rows = _rowwise("loss", lambda a, b: (lambda d, l: (d, d, l))(*_loss_fn(a, b)), S, ts, [R(x2), R(tgt)],
                                      [(D_MODEL, F32, D_MODEL, 0), (D_MODEL, MXU, D_MODEL, 0)], accs=[(1, D_MODEL)])

    gW = {}
    gW["w_down"] = _mm("d_w_down", act, dx2_bf, "tn")
    dact = _mm("d_act", dx2_bf, wdown, "nt")

    def glu_bwd(t, da):
        _, vjp = jax.vjp(_swiglu_fn, t[:, :FFN], t[:, FFN:])
        return jnp.concatenate(vjp(da), axis=1)

    (dgu,) = _rowwise("swiglu_bwd", glu_bwd, S, ts_wide, [R(gu), R(dact)], [(2 * FFN, MXU, 2 * FFN, 0)])
    gW["w_gate_up"] = _mm("d_w_gate_up", h2, dgu, "tn")
    dh2 = _mm("d_h2", dgu, wgu, "nt")

    def rms_bwd(xx, g, dh, dres):
        _, vjp = jax.vjp(_rmsg_fn, xx, g)
        dx, dg = vjp(dh)
        dx = dx + dres
        return dx, dx, dg

    dx1, dx1_bf, dg_ffn = _rowwise("rms_ffn_bwd", rms_bwd, S, ts, [R(x1), W_(g_ffn), R(dh2), R(dx2)],
                                   [(D_MODEL, F32, D_MODEL, 0), (D_MODEL, MXU, D_MODEL, 0)], accs=[(1, D_MODEL)])
    gW["w_out"] = _mm("d_w_out", merged, dx1_bf, "tn")
    dmerged = _mm("d_merged", dx1_bf, wout, "nt")

    def merge_bwd(ga, gb, ya, yb, dm):
        _, vjp = jax.vjp(_merge_fn, ga, gb, ya, yb)
        return vjp(dm)

    dga, dgb, dy_a, dy_b = _rowwise("merge_bwd", merge_bwd, S, ts, merge_ins + [R(dmerged)], [(D_MODEL, MXU, D_MODEL, 0)] * 4)
    gW["w_ret_out"] = _mm("d_w_ret_out", o_b, dy_b, "tn")
    do_b = _mm("d_o_b", dy_b, wret, "nt")

    def post_bwd(rf, rb, gr, dob):
        _, vjp = jax.vjp(_ret_post_fn, rf, rb, gr)
        drf, _, dgr = vjp(dob)
        return drf, dgr

    dret, dg_r = _rowwise("ret_post_bwd", post_bwd, S, ts, post_ins + [R(do_b)], [(1024, F32, 1024, 0), (1024, MXU, 1024, 0)])
    dq_f, dk_f, dv_f, dlg_f = _ret_bwd("ret_bwd_f", qt, kt, proj, dret, st_f, lg_f, False)
    dq_b, dk_b, dv_b, dlg_b = _ret_bwd("ret_bwd_b", qt, kt, proj, dret, st_b, lg_b, True)

    def ret_prep_bwd(qr, kr, cosr_, sinr_, dqf, dqb, dkf, dkb, dvf, dvb):
        _, vjp = jax.vjp(lambda a, b: _ret_prep_fn(a, b, cosr_, sinr_), qr, kr)
        pair = lambda t: jnp.concatenate([t[:, 256 * j:256 * j + 128] + t[:, 256 * j + 128:256 * j + 256] for j in range(4)], axis=1)
        dqr, dkr = vjp((pair(dqf + dqb), pair(dkf + dkb)))
        return dqr, dkr, dvf + dvb

    dq_r, dk_r, dv_r = _rowwise("ret_prep_bwd", ret_prep_bwd, S, ts, ret_ins + [R(t) for t in (dq_f, dq_b, dk_f, dk_b, dv_f, dv_b)],
                                [(512, MXU, 512, 0), (512, MXU, 512, 0), (1024, MXU, 1024, 0)])

    gW_mla_p = _mm("d_w_mla_out", o_bf, dy_a, "tn")
    do = _mm("d_o", dy_a, wmla, "nt")
    do_bf, delta = _rowwise("attn_delta", _delta_fn, S, ts, [R(o), R(do)], [(HEADS * LANES, MXU, HEADS * LANES, 0), (HEADS * LANES, F32, HEADS * LANES, 0)])
    dq, dk, dv = _flash_bwd(q, k, v, do_bf, lse, delta)

    def mla_prep_bwd(cq, ckv, kr, cosm_, sinm_, gqa, gkva, gqn_, gkn_, wq_, wk_, wv_, dq_, dk_, dv_):
        f = lambda cq, ckv, kr, gqa, gkva, gqn_, gkn_, wq_, wk_, wv_: _mla_prep_fn(cq, ckv, kr, cosm_, sinm_, gqa, gkva, gqn_, gkn_, wq_, wk_, wv_)
        _, vjp = jax.vjp(f, cq, ckv, kr, gqa, gkva, gqn_, gkn_, wq_.astype(F32), wk_.astype(F32), wv_.astype(F32))
        return vjp((dq_, dk_, dv_))

    mb = _rowwise("mla_prep_bwd", mla_prep_bwd, S, ts_wide, mla_ins + [R(dq), R(dk), R(dv)],
                  [(256, MXU, 256, 0), (128, MXU, 128, 0), (128, MXU, 128, 0)],
                  accs=[(1, 256), (1, 128), (1, LANES), (1, LANES), (256, HEADS * LANES), (128, HEADS * LANES), (128, HEADS * LANES)])
    dc_q, dc_kv, dk_rope, dg_q_a, dg_kv_a, dgqn_p, dgkn_p, dwq_p, dwk_p, dwv_p = mb

    dproj = jnp.concatenate([dga, dgb, dv_r, dg_r, dq_r, dk_r, dc_q, dc_kv, dk_rope], axis=1)
    gwin_p = _mm("d_w_in", h, dproj, "tn")
    dh = _mm("d_h", dproj, win, "nt")
    grad_x, _, dg_mix = _rowwise("rms_mix_bwd", rms_bwd, S, ts, [R(x), W_(g_mix), R(dh), R(dx1)],
                                 [(D_MODEL, F32, D_MODEL, 0), (D_MODEL, MXU, D_MODEL, 0)], accs=[(1, D_MODEL)])

    gW["w_in"] = _win_unpad(gwin_p)
    gW["w_q_b"] = _wq_unpad(dwq_p)
    gW["w_kv_b"] = _wkv_unpad(dwk_p, dwv_p)
    gW["w_mla_out"] = _wmla_unpad(gW_mla_p)
    gG = {"g_mix": dg_mix, "g_q_a": dg_q_a, "g_kv_a": dg_kv_a, "g_qn": _qk_unpad(dgqn_p),
          "g_kn": _qk_unpad(dgkn_p), "ret_decay_fwd": dlg_f[:, 0, 0][None, :], "ret_decay_bwd": dlg_b[:, 0, 0][None, :],
          "g_ffn": dg_ffn}
    return loss_rows, grad_x, gG, gW


MATS = [("w_in", (1024, 5536), 1), ("w_q_b", (256, 768), 1), ("w_kv_b", (128, 1024), 1), ("w_mla_out", (512, 1024), 1),
        ("w_ret_out", (1024, 1024), 0), ("w_out", (1024, 1024), 0), ("w_gate_up", (1024, 5632), 1), ("w_down", (2816, 1024), 0)]
GAINS = [("g_mix", 1024), ("g_q_a", 256), ("g_kv_a", 128), ("g_qn", 96), ("g_kn", 96), ("ret_decay_fwd", 8), ("ret_decay_bwd", 8),
         ("g_ffn", 1024)]
ORDER = ["g_mix", "w_in", "g_q_a", "w_q_b", "g_kv_a", "w_kv_b", "g_qn", "g_kn", "w_mla_out", "ret_decay_fwd", "ret_decay_bwd",
         "w_ret_out", "w_out", "g_ffn", "w_gate_up", "w_down"]
GAIN_LEN = sum(n for _, n in GAINS)
GAIN_PAD = -(-GAIN_LEN // LANES) * LANES


def _pack_gains(d):
    row = jnp.concatenate([d[n].reshape(1, ln).astype(F32) for n, ln in GAINS], axis=1)
    return jnp.pad(row, ((0, 0), (0, GAIN_PAD - GAIN_LEN)))


def _unpack_gains(row):
    out, off = {}, 0
    for n, ln in GAINS:
        out[n] = row[0, off:off + ln]
        off += ln
    return out


def _unshard(pieces, axis):
    if axis == 0:
        return pieces.reshape((N_DEV * pieces.shape[1], pieces.shape[2]))
    return jnp.concatenate([pieces[p] for p in range(N_DEV)], axis=1)


def _reshard(full, axis):
    if axis == 0:
        return full.reshape((N_DEV, full.shape[0] // N_DEV, full.shape[1]))
    c = full.shape[1] // N_DEV
    return jnp.stack([full[:, c * p:c * (p + 1)] for p in range(N_DEV)])


def _all_gather(shards):
    n = len(shards)

    def body(*refs):
        x_refs, out_refs = refs[:n], refs[n:2 * n]
        send_sems, recv_sems, local_sems = refs[2 * n:]
        x, y, c = lax.axis_index("x"), lax.axis_index("y"), lax.axis_index("c")
        me, sibling = (x, y, c), (x, y, 1 - c)
        chips = [(1 - x, y), (x, 1 - y), (1 - x, 1 - y)]

        def slot(a, px, py, pc):
            return out_refs[a].at[4 * px + 2 * py + pc]

        def copy(a, k, block, to, from_input=False):
            return pltpu.make_async_remote_copy(
                src_ref=x_refs[a] if from_input else slot(a, *block), dst_ref=slot(a, *block),
                send_sem=send_sems.at[a, k], recv_sem=recv_sems.at[a, k], device_id=to, device_id_type=pl.DeviceIdType.MESH)

        mine = [pltpu.make_async_copy(x_refs[a], slot(a, *me), local_sems.at[a]) for a in range(n)]
        first = [copy(a, 0, me, sibling, True) for a in range(n)]
        first += [copy(a, 1 + j, me, (*chip, c), True) for j, chip in enumerate(chips) for a in range(n)]
        for cp in mine + first:
            cp.start()
        passed = []
        for j, chip in enumerate(chips):
            for a in range(n):
                copy(a, 1 + j, (*chip, c), me).wait_recv()
                passed.append(copy(a, 4 + j, (*chip, c), sibling))
                passed[-1].start()
        for a in range(n):
            copy(a, 0, sibling, me).wait_recv()
        for j, chip in enumerate(chips):
            for a in range(n):
                copy(a, 4 + j, (*chip, 1 - c), me).wait_recv()
        for cp in first + passed:
            cp.wait_send()
        for cp in mine:
            cp.wait()

    any_spec = pl.BlockSpec(memory_space=pl.ANY)
    return pl.pallas_call(
        body, name="all_gather_weights", out_shape=[jax.ShapeDtypeStruct((N_DEV,) + s.shape, s.dtype) for s in shards],
        in_specs=[any_spec] * n, out_specs=[any_spec] * n,
        scratch_shapes=[pltpu.SemaphoreType.DMA((n, 7)), pltpu.SemaphoreType.DMA((n, 7)), pltpu.SemaphoreType.DMA((n,))],
    )(*shards)


def _all_to_all(pieces):
    n = len(pieces)

    def body(*refs):
        in_refs, out_refs = refs[:n], refs[n:2 * n]
        send_sems, recv_sems, local_sems = refs[2 * n:]
        x, y, c = lax.axis_index("x"), lax.axis_index("y"), lax.axis_index("c")
        my_id = 4 * x + 2 * y + c
        flips = [(fx, fy, fc) for fx in (0, 1) for fy in (0, 1) for fc in (0, 1)][1:]

        def copy(a, kk, f):
            p = (x ^ f[0], y ^ f[1], c ^ f[2])
            return pltpu.make_async_remote_copy(
                src_ref=in_refs[a].at[4 * p[0] + 2 * p[1] + p[2]], dst_ref=out_refs[a].at[my_id],
                send_sem=send_sems.at[a, kk], recv_sem=recv_sems.at[a, kk], device_id=p, device_id_type=pl.DeviceIdType.MESH)

        mine = [pltpu.make_async_copy(in_refs[a].at[my_id], out_refs[a].at[my_id], local_sems.at[a]) for a in range(n)]
        copies = [copy(a, kk, f) for kk, f in enumerate(flips) for a in range(n)]
        for cp in mine + copies:
            cp.start()
        for cp in copies:
            cp.wait_recv()
        for cp in copies:
            cp.wait_send()
        for cp in mine:
            cp.wait()

    any_spec = pl.BlockSpec(memory_space=pl.ANY)
    return pl.pallas_call(
        body, name="all_to_all_grads", out_shape=[jax.ShapeDtypeStruct(p.shape, p.dtype) for p in pieces],
        in_specs=[any_spec] * n, out_specs=[any_spec] * n,
        scratch_shapes=[pltpu.SemaphoreType.DMA((n, 7)), pltpu.SemaphoreType.DMA((n, 7)), pltpu.SemaphoreType.DMA((n,))],
    )(*pieces)


def _adamw(name, parts, w, m, v):
    rows, cols = w.shape
    tr = _pick(rows, (128, 64, 32, 16, 8))
    pspec = pl.BlockSpec((N_DEV, tr, cols), lambda i: (0, i, 0))
    rspec = pl.BlockSpec((tr, cols), lambda i: (i, 0))

    def body(p_ref, w_ref, m_ref, v_ref, g_ref, d_ref, m2_ref, v2_ref):
        g, d, m2, v2 = _adamw_fn([p_ref[s] for s in range(N_DEV)], w_ref[...], m_ref[...], v_ref[...])
        g_ref[...], d_ref[...], m2_ref[...], v2_ref[...] = g, d, m2, v2

    return pl.pallas_call(
        body, name=name, grid=(rows // tr,), in_specs=[pspec, rspec, rspec, rspec], out_specs=[rspec] * 4,
        out_shape=[jax.ShapeDtypeStruct((rows, cols), F32)] * 4,
        compiler_params=pltpu.CompilerParams(dimension_semantics=("parallel",), vmem_limit_bytes=VMEM_LIMIT),
    )(parts, w, m, v)


def kernel(x, positions, g_mix, w_in, g_q_a, w_q_b, g_kv_a, w_kv_b, g_qn, g_kn, w_mla_out, ret_decay_fwd, ret_decay_bwd, w_ret_out, w_out, g_ffn, w_gate_up, w_down, loss_target, m_g_mix, m_w_in, m_g_q_a, m_w_q_b, m_g_kv_a, m_w_kv_b, m_g_qn, m_g_kn, m_w_mla_out, m_ret_decay_fwd, m_ret_decay_bwd, m_w_ret_out, m_w_out, m_g_ffn, m_w_gate_up, m_w_down, v_g_mix, v_w_in, v_g_q_a, v_w_q_b, v_g_kv_a, v_w_kv_b, v_g_qn, v_g_kn, v_w_mla_out, v_ret_decay_fwd, v_ret_decay_bwd, v_w_ret_out, v_w_out, v_g_ffn, v_w_gate_up, v_w_down):
    w = dict(g_mix=g_mix, w_in=w_in, g_q_a=g_q_a, w_q_b=w_q_b, g_kv_a=g_kv_a, w_kv_b=w_kv_b, g_qn=g_qn, g_kn=g_kn, w_mla_out=w_mla_out,
             ret_decay_fwd=ret_decay_fwd, ret_decay_bwd=ret_decay_bwd, w_ret_out=w_ret_out, w_out=w_out, g_ffn=g_ffn,
             w_gate_up=w_gate_up, w_down=w_down)
    m = dict(g_mix=m_g_mix, w_in=m_w_in, g_q_a=m_g_q_a, w_q_b=m_w_q_b, g_kv_a=m_g_kv_a, w_kv_b=m_w_kv_b, g_qn=m_g_qn, g_kn=m_g_kn,
             w_mla_out=m_w_mla_out, ret_decay_fwd=m_ret_decay_fwd, ret_decay_bwd=m_ret_decay_bwd, w_ret_out=m_w_ret_out, w_out=m_w_out,
             g_ffn=m_g_ffn, w_gate_up=m_w_gate_up, w_down=m_w_down)
    v = dict(g_mix=v_g_mix, w_in=v_w_in, g_q_a=v_g_q_a, w_q_b=v_w_q_b, g_kv_a=v_g_kv_a, w_kv_b=v_w_kv_b, g_qn=v_g_qn, g_kn=v_g_kn,
             w_mla_out=v_w_mla_out, ret_decay_fwd=v_ret_decay_fwd, ret_decay_bwd=v_ret_decay_bwd, w_ret_out=v_w_ret_out, w_out=v_w_out,
             g_ffn=v_g_ffn, w_gate_up=v_w_gate_up, w_down=v_w_down)
    gains = {n: w[n].reshape(1, ln) for n, ln in GAINS}

    gathered = _all_gather([w[n].astype(WIRE) for n, _, _ in MATS])
    W = {n: _unshard(g, axis) for (n, _, axis), g in zip(MATS, gathered)}
    S = x.shape[1]
    pos = positions.reshape(S, 1).astype(F32)
    loss_rows, grad_x, gG, gW = _local_step(x.reshape(S, D_MODEL), pos, loss_target.reshape(S, D_MODEL), gains, W)
    loss = lax.psum(jnp.sum(loss_rows), ("x", "y", "c"))

    pieces = [_reshard(gW[n], axis).astype(GWIRE) for n, _, axis in MATS]
    pieces.append(jnp.broadcast_to(_pack_gains(gG)[None], (N_DEV, 1, GAIN_PAD)))
    parts = _all_to_all(pieces)
    out = [dict() for _ in range(4)]
    for (n, _, _), p in zip(MATS, parts):
        for o, r in zip(out, _adamw("adamw_" + n, p, w[n], m[n], v[n])):
            o[n] = r
    for o, r in zip(out, _adamw("adamw_gains", parts[-1], _pack_gains(w), _pack_gains(m), _pack_gains(v))):
        o.update(_unpack_gains(r))
    return (loss, grad_x.reshape(x.shape), *[o[n] for o in out for n in ORDER])
```

```python
import functools

import numpy as np
import jax
import jax.numpy as jnp
from jax import lax
from jax.experimental import pallas as pl
from jax.experimental.pallas import tpu as pltpu

F32 = jnp.float32
MXU = jnp.bfloat16
WIRE = jnp.bfloat16
GWIRE = jnp.bfloat16

N_DEV = 8
D_MODEL = 1024
HEADS = 8
LANES = 128
Q_RANK, KV_RANK = 256, 128
NOPE, ROPE_M, V_M = 64, 32, 64
QK_M = NOPE + ROPE_M
RQK, RV = 64, 128
CHUNK = 128
FFN = 2816
IN_WIDTH = 5536
THETA = 10000.0
EPS = 1e-6
LR, B1, B2, AEPS, WD, STEP = 0.001, 0.9, 0.999, 1e-08, 0.01, 10
VMEM_LIMIT = 56 * 1024 * 1024

NN = ((1,), (0,))
NT = ((1,), (1,))
TN = ((0,), (0,))

P_GATES, P_VR, P_GR, P_QR, P_KR, P_CQ, P_CKV, P_KROPE, P_WIDTH = 0, 2048, 3072, 4096, 4608, 5120, 5376, 5504, 5632
O_CQ, O_CKV, O_KROPE, O_QR, O_KR, O_VR, O_GR, O_GATES = 0, 256, 384, 416, 928, 1440, 2464, 3488


def _dot(a, b, dims):
    return lax.dot_general(a, b, (dims, ((), ())), preferred_element_type=F32)


def _pick(dim, cands):
    for c in cands:
        if dim % c == 0:
            return c
    return dim


def _pairs(t):
    return t.reshape(t.shape[0], 4, 2, 2, 32).transpose(0, 1, 3, 2, 4).reshape(t.shape[0], 512)


def _win_pad(w):
    z = jnp.zeros((w.shape[0], 48), w.dtype)
    kr = w[:, O_KROPE:O_KROPE + 32]
    return jnp.concatenate([w[:, O_GATES:], w[:, O_VR:O_VR + 1024], w[:, O_GR:O_GR + 1024], _pairs(w[:, O_QR:O_QR + 512]),
                            _pairs(w[:, O_KR:O_KR + 512]), w[:, :O_CKV], w[:, O_CKV:O_KROPE], kr[:, :16], z, kr[:, 16:], z], axis=1)


def _win_unpad(g):
    return jnp.concatenate([g[:, P_CQ:P_CQ + 256], g[:, P_CKV:P_CKV + 128], g[:, P_KROPE:P_KROPE + 16], g[:, P_KROPE + 64:P_KROPE + 80],
                            _pairs(g[:, P_QR:P_QR + 512]), _pairs(g[:, P_KR:P_KR + 512]), g[:, P_VR:P_VR + 1024],
                            g[:, P_GR:P_GR + 1024], g[:, P_GATES:P_GATES + 2048]], axis=1)


def _qk_pad(t):
    z = jnp.zeros(t.shape[:-1] + (32,), t.dtype)
    return jnp.concatenate([t[..., 64:80], t[..., 0:48], t[..., 80:96], t[..., 48:64], z], axis=-1)


def _qk_unpad(p):
    return jnp.concatenate([p[..., 16:64], p[..., 80:96], p[..., 0:16], p[..., 64:80]], axis=-1)


def _wq_pad(w):
    return _qk_pad(w.reshape(Q_RANK, HEADS, QK_M)).reshape(Q_RANK, HEADS * LANES)


def _wq_unpad(g):
    return _qk_unpad(g.reshape(Q_RANK, HEADS, LANES)).reshape(Q_RANK, HEADS * QK_M)


def _wkv_pad(w):
    t = w.reshape(KV_RANK, HEADS, NOPE + V_M)
    z = lambda n: jnp.zeros((KV_RANK, HEADS, n), w.dtype)
    wk = jnp.concatenate([z(16), t[..., 0:48], z(16), t[..., 48:64], z(32)], axis=-1)
    wv = jnp.concatenate([t[..., 64:128], z(64)], axis=-1)
    return wk.reshape(KV_RANK, HEADS * LANES), wv.reshape(KV_RANK, HEADS * LANES)


def _wkv_unpad(dwk, dwv):
    k, v = dwk.reshape(KV_RANK, HEADS, LANES), dwv.reshape(KV_RANK, HEADS, LANES)
    return jnp.concatenate([k[..., 16:64], k[..., 80:96], v[..., 0:64]], axis=-1).reshape(KV_RANK, HEADS * (NOPE + V_M))


def _wmla_pad(w):
    t = w.reshape(HEADS, V_M, D_MODEL)
    return jnp.concatenate([t, jnp.zeros_like(t)], axis=1).reshape(HEADS * LANES, D_MODEL)


def _wmla_unpad(g):
    return g.reshape(HEADS, LANES, D_MODEL)[:, :V_M].reshape(HEADS * V_M, D_MODEL)


def _rowwise(name, fn, rows, ts, ins, outs, accs=(), ncol=1):
    n_in, n_out, n_acc = len(ins), len(outs), len(accs)

    def colmap(col):
        if callable(col):
            return lambda i, j: (i, col(j))
        return lambda i, j: (i, col)

    arrays, in_specs = [], []
    for arr, spec in ins:
        arrays.append(arr)
        if spec is None:
            in_specs.append(pl.BlockSpec(arr.shape, functools.partial(lambda i, j, nd: (0,) * nd, nd=arr.ndim)))
        else:
            in_specs.append(pl.BlockSpec((ts, spec[0]), colmap(spec[1])))
    out_shape, out_specs = [], []
    for total, dtype, width, col in outs:
        out_shape.append(jax.ShapeDtypeStruct((rows, total), dtype))
        out_specs.append(pl.BlockSpec((ts, width), colmap(col)))
    for shp in accs:
        out_shape.append(jax.ShapeDtypeStruct(shp, F32))
        out_specs.append(pl.BlockSpec(shp, functools.partial(lambda i, j, nd: (0,) * nd, nd=len(shp))))

    def body(*refs):
        vals = [r[...] for r in refs[:n_in]]
        res = fn(*vals)
        if not isinstance(res, (tuple, list)):
            res = (res,)
        for r, v in zip(refs[n_in:n_in + n_out], res[:n_out]):
            r[...] = v.astype(r.dtype)
        if n_acc:
            first = jnp.logical_and(pl.program_id(0) == 0, pl.program_id(1) == 0)
            for r, v in zip(refs[n_in + n_out:], res[n_out:]):
                @pl.when(first)
                def _(r=r):
                    r[...] = jnp.zeros_like(r)
                r[...] += v.astype(F32)

    res = pl.pallas_call(
        body, name=name, grid=(rows // ts, ncol), in_specs=in_specs, out_specs=out_specs, out_shape=out_shape,
        compiler_params=pltpu.CompilerParams(dimension_semantics=("arbitrary", "arbitrary"), vmem_limit_bytes=VMEM_LIMIT),
    )(*arrays)
    return res


MM_OPERAND_BYTES = 24 * 1024 * 1024


def _mm(name, a, b, mode, add=None):
    if mode == "nn":
        (M, K), N = a.shape, b.shape[1]
    elif mode == "nt":
        (M, K), N = a.shape, b.shape[0]
    else:
        (K, M), N = a.shape, b.shape[1]
    tm = _pick(M, (512, 256, 128)) if mode == "tn" else _pick(M, (1024, 512, 256, 128))
    tn = _pick(N, (512, 256, 128))
    fits = lambda t: 2 * (tm + tn) * t * a.dtype.itemsize <= MM_OPERAND_BYTES
    tk = next(t for t in (K, 4096, 2816, 2048, 1408, 1024, 512, 256, 128) if K % t == 0 and (fits(t) or t == 128))
    nk = K // tk
    dims = {"nn": NN, "nt": NT, "tn": TN}[mode]
    a_spec = pl.BlockSpec((tk, tm), lambda i, j, k: (k, i)) if mode == "tn" else pl.BlockSpec((tm, tk), lambda i, j, k: (i, k))
    b_spec = pl.BlockSpec((tn, tk), lambda i, j, k: (j, k)) if mode == "nt" else pl.BlockSpec((tk, tn), lambda i, j, k: (k, j))
    o_spec = pl.BlockSpec((tm, tn), lambda i, j, k: (i, j))
    has_add = add is not None

    def body(*refs):
        a_ref, b_ref, o_ref = refs[0], refs[1], refs[-1]
        d = _dot(a_ref[...], b_ref[...], dims)
        first = (d + refs[2][...]) if has_add else d
        if nk == 1:
            o_ref[...] = first
        else:
            k = pl.program_id(2)

            @pl.when(k == 0)
            def _():
                o_ref[...] = first

            @pl.when(k > 0)
            def _():
                o_ref[...] += d

    args = [a, b] + ([add] if has_add else [])
    specs = [a_spec, b_spec] + ([o_spec] if has_add else [])
    return pl.pallas_call(
        body, name=name, grid=(M // tm, N // tn, nk), in_specs=specs, out_specs=o_spec,
        out_shape=jax.ShapeDtypeStruct((M, N), F32),
        compiler_params=pltpu.CompilerParams(dimension_semantics=("parallel", "parallel", "arbitrary"), vmem_limit_bytes=VMEM_LIMIT),
    )(*args)


@jax.custom_vjp
def _swap64(x):
    return pltpu.roll(x, 64, 1)


_swap64.defvjp(lambda x: (_swap64(x), None), lambda _, g: (_swap64(g),))


@jax.custom_vjp
def _mxdot(a, b):
    return _dot(a.astype(MXU), b.astype(MXU), NN)


def _mxdot_bwd(res, g):
    a, b = res
    gb = g.astype(MXU)
    return _dot(gb, b.astype(MXU), NT), _dot(a.astype(MXU), gb, TN)


_mxdot.defvjp(lambda a, b: (_mxdot(a, b), (a, b)), _mxdot_bwd)


def _rms(x):
    return x * lax.rsqrt(jnp.mean(x * x, axis=-1, keepdims=True) + EPS)


def _rmsg_fn(x, g):
    return _rms(x) * g


def _silu(x):
    return x * jax.nn.sigmoid(x)


def _tables_fn(pos, inv_m, sgn_m, inv_r, sgn_r):
    am, ar = pos * inv_m, pos * inv_r
    return jnp.cos(am), jnp.sin(am) * sgn_m, jnp.cos(ar), jnp.sin(ar) * sgn_r


def _head_blocks(t):
    return [t[:, LANES * h:LANES * (h + 1)] for h in range(t.shape[1] // LANES)]


def _mla_prep_fn(cq, ckv, kr, cosm, sinm, gqa, gkva, gqn, gkn, wq, wk, wv):
    cqn = _rms(cq) * gqa
    ckvn = _rms(ckv) * gkva
    q_raw = _mxdot(cqn, wq)
    k_raw = _mxdot(ckvn, wk)
    lane = lax.broadcasted_iota(jnp.int32, (1, HEADS * LANES), 1)
    v = _mxdot(ckvn, wv) + (lane % LANES == V_M).astype(F32)

    def norm_rope(blocks, g, extra):
        outs = []
        for b in blocks:
            if extra is not None:
                b = b + extra
            n = b * lax.rsqrt(jnp.sum(b * b, axis=-1, keepdims=True) * (1.0 / QK_M) + EPS) * g
            outs.append(n * cosm + _swap64(n) * sinm)
        return jnp.concatenate(outs, axis=1)

    q = norm_rope(_head_blocks(q_raw), gqn, None)
    k = norm_rope(_head_blocks(k_raw), gkn, kr)
    return q, k, v


def _ret_prep_fn(qr, kr, cosr, sinr):
    def rope(t, scale):
        return jnp.concatenate([(b * cosr + _swap64(b) * sinr) * scale for b in _head_blocks(t)], axis=1)
    return rope(qr, 1.0), rope(kr, RQK ** -0.5)


def _ret_post_fn(rf, rb, gr):
    ret = rf + rb
    outs = []
    for b, g in zip(_head_blocks(ret), _head_blocks(gr)):
        outs.append(_silu(g) * _rms(b))
    return jnp.concatenate(outs, axis=1)


def _merge_fn(ga, gb, ya, yb):
    return jax.nn.sigmoid(ga) * ya + jax.nn.sigmoid(gb) * yb


def _swiglu_fn(gate, up):
    return _silu(gate) * up


def _loss_fn(x2, tgt):
    d = x2 - tgt
    return d * (1.0 / D_MODEL), 0.5 * jnp.sum(d * d, axis=0, keepdims=True) * (1.0 / D_MODEL)


def _adamw_fn(parts, w, m, v):
    g = parts[0].astype(F32)
    for p in range(1, N_DEV):
        g = g + parts[p].astype(F32)
    m2 = B1 * m + (1.0 - B1) * g
    v2 = B2 * v + (1.0 - B2) * jnp.square(g)
    m_hat = m2 / (1.0 - B1 ** STEP)
    v_hat = v2 / (1.0 - B2 ** STEP)
    delta = -LR * (m_hat / (jnp.sqrt(v_hat) + AEPS) + WD * w)
    return g, delta, m2, v2


SCALE = QK_M ** -0.5
LOG2E = 1.4426950408889634
FLASH_ROWS = 32


def _flash_fwd(q, k, v):
    S = q.shape[0]
    tq = tk = _pick(S, (512, 256, 128))
    ncb = tk // LANES
    nkv = S // tk
    assert nkv % 2 == 0, "kv tiles are processed in pairs"
    mrows = 64
    c = SCALE * LOG2E

    def body(q_ref, k_ref, v_ref, o_ref, obf_ref, lse_ref, s_a, p_a, s_b, p_b, m_sc, a_sc, acc_sc):
        m_sc[...] = jnp.full_like(m_sc, -jnp.inf)
        acc_sc[...] = jnp.zeros_like(acc_sc)
        qb = q_ref[...]

        def scores(j, s_buf):
            s_buf[...] = _dot(qb, k_ref[pl.ds(pl.multiple_of(j * tk, tk), tk), :], NT)

        def stage(j, s_buf, p_buf, s_next):
            scores(jnp.minimum(j + 1, nkv - 1), s_next)
            for r in range(tq // mrows):
                rows = slice(r * mrows, (r + 1) * mrows)
                cols = [s_buf[rows, LANES * cb:LANES * (cb + 1)] for cb in range(ncb)]
                m_prev = m_sc[rows, :]
                row_max = jnp.max(functools.reduce(jnp.maximum, cols), axis=-1, keepdims=True)
                m_new = jnp.maximum(m_prev, jnp.broadcast_to(row_max, (mrows, LANES)))
                a_sc[rows, :] = jnp.exp2((m_prev - m_new) * c)
                m_sc[rows, :] = m_new
                for cb in range(ncb):
                    p_buf[rows, LANES * cb:LANES * (cb + 1)] = jnp.exp2((cols[cb] - m_new) * c).astype(p_buf.dtype)
            acc_sc[...] = a_sc[...] * acc_sc[...] + _dot(p_buf[...], v_ref[pl.ds(pl.multiple_of(j * tk, tk), tk), :], NN)

        scores(0, s_a)

        def pair_step(t, carry):
            stage(2 * t, s_a, p_a, s_b)
            stage(2 * t + 1, s_b, p_b, s_a)
            return carry

        lax.fori_loop(0, nkv // 2, pair_step, 0)
        acc = acc_sc[...]
        lane = lax.broadcasted_iota(jnp.int32, (1, LANES), 1)
        l = jnp.sum(jnp.where(lane == V_M, acc, 0.0), axis=-1, keepdims=True)
        o = acc / l
        o_ref[...] = o
        obf_ref[...] = o.astype(obf_ref.dtype)
        lse_ref[...] = m_sc[...] * c + jnp.log2(jnp.broadcast_to(l, (tq, LANES)))

    qspec = pl.BlockSpec((tq, LANES), lambda h, i: (i, h))
    kspec = pl.BlockSpec((S, LANES), lambda h, i: (0, h))
    full = jax.ShapeDtypeStruct((S, HEADS * LANES), F32)
    return pl.pallas_call(
        body, name="flash_fwd", grid=(HEADS, S // tq), in_specs=[qspec, kspec, kspec], out_specs=[qspec, qspec, qspec],
        out_shape=[full, jax.ShapeDtypeStruct((S, HEADS * LANES), MXU), full],
        scratch_shapes=[pltpu.VMEM((tq, tk), F32), pltpu.VMEM((tq, tk), MXU)] * 2 + [pltpu.VMEM((tq, LANES), F32)] * 3,
        compiler_params=pltpu.CompilerParams(dimension_semantics=("parallel", "arbitrary"), vmem_limit_bytes=VMEM_LIMIT),
    )(q, k, v)


def _delta_fn(o, do):
    outs = [jnp.broadcast_to(jnp.sum(a * b, axis=-1, keepdims=True), a.shape) for a, b in zip(_head_blocks(o), _head_blocks(do))]
    return do, jnp.concatenate(outs, axis=1)


def _flash_bwd(q, k, v, do, lse, delta):
    S = q.shape[0]
    tq = tk = _pick(S, (512, 256, 128))
    ncb = tk // LANES
    c = SCALE * LOG2E

    nq = S // tq
    assert nq % 2 == 0, "q tiles are processed in pairs"

    def body(q_ref, k_ref, v_ref, do_ref, lse_ref, dl_ref, dq_ref, dk_ref, dv_ref, s_a, dp_a, p_a, ds_a, s_b, dp_b, p_b, ds_b, dk_sc, dv_sc):
        @pl.when(pl.program_id(1) == 0)
        def _():
            dq_ref[...] = jnp.zeros_like(dq_ref)

        dk_sc[...] = jnp.zeros_like(dk_sc)
        dv_sc[...] = jnp.zeros_like(dv_sc)
        kb, vb = k_ref[...], v_ref[...]

        def scores(i, s_buf, dp_buf):
            q_rows = pl.ds(pl.multiple_of(i * tq, tq), tq)
            s_buf[...] = _dot(q_ref[q_rows, :], kb, NT)
            dp_buf[...] = _dot(do_ref[q_rows, :], vb, NT)

        def stage(i, s_buf, dp_buf, p_buf, ds_buf, s_next, dp_next):
            scores(jnp.minimum(i + 1, nq - 1), s_next, dp_next)
            for r in range(tq // FLASH_ROWS):
                rows = slice(r * FLASH_ROWS, (r + 1) * FLASH_ROWS)
                grows = pl.ds(pl.multiple_of(i * tq + r * FLASH_ROWS, FLASH_ROWS), FLASH_ROWS)
                lse_b, dl_b = lse_ref[grows, :], dl_ref[grows, :]
                for cb in range(ncb):
                    sl = slice(LANES * cb, LANES * (cb + 1))
                    p = jnp.exp2(s_buf[rows, sl] * c - lse_b)
                    p_buf[rows, sl] = p.astype(p_buf.dtype)
                    ds_buf[rows, sl] = (p * (dp_buf[rows, sl] - dl_b) * SCALE).astype(ds_buf.dtype)
            q_rows = pl.ds(pl.multiple_of(i * tq, tq), tq)
            dv_sc[...] += _dot(p_buf[...], do_ref[q_rows, :], TN)
            dk_sc[...] += _dot(ds_buf[...], q_ref[q_rows, :], TN)
            dq_ref[q_rows, :] += _dot(ds_buf[...], kb, NN)

        scores(0, s_a, dp_a)

        def pair_step(t, carry):
            stage(2 * t, s_a, dp_a, p_a, ds_a, s_b, dp_b)
            stage(2 * t + 1, s_b, dp_b, p_b, ds_b, s_a, dp_a)
            return carry

        lax.fori_loop(0, nq // 2, pair_step, 0)
        dk_ref[...] = dk_sc[...]
        dv_ref[...] = dv_sc[...]

    hspec = pl.BlockSpec((S, LANES), lambda h, j: (0, h))
    kspec = pl.BlockSpec((tk, LANES), lambda h, j: (j, h))
    full = jax.ShapeDtypeStruct((S, HEADS * LANES), F32)
    tile_bufs = [pltpu.VMEM((tq, tk), F32), pltpu.VMEM((tq, tk), F32), pltpu.VMEM((tq, tk), MXU), pltpu.VMEM((tq, tk), MXU)]
    return pl.pallas_call(
        body, name="flash_bwd", grid=(HEADS, S // tk), in_specs=[hspec, kspec, kspec, hspec, hspec, hspec],
        out_specs=[hspec, kspec, kspec], out_shape=[full, full, full],
        scratch_shapes=tile_bufs + tile_bufs + [pltpu.VMEM((tk, LANES), F32), pltpu.VMEM((tk, LANES), F32)],
        compiler_params=pltpu.CompilerParams(dimension_semantics=("parallel", "arbitrary"), vmem_limit_bytes=VMEM_LIMIT),
    )(q, k, v, do, lse, delta)


def _ret_consts(lgh, head, rev):
    C = CHUNK
    lane = lax.broadcasted_iota(jnp.int32, (1, LANES), 1)
    hm = ((lane // 32) % 2 == head % 2).astype(F32)
    r = lax.broadcasted_iota(jnp.int32, (C, C), 0)
    c = lax.broadcasted_iota(jnp.int32, (C, C), 1)
    diff = ((c - r) if rev else (r - c)).astype(F32)
    mask = (diff > 0) if rev else (diff >= 0)
    dpos = jnp.maximum(diff, 0.0)
    din = jnp.where(mask, jnp.exp(lgh * dpos), 0.0)
    idx = lax.broadcasted_iota(jnp.int32, (C, 1), 0).astype(F32)
    eq = (C - idx) if rev else (idx + 1.0)
    ek = idx if rev else (C - 1.0 - idx)
    qd, kd = jnp.exp(lgh * eq), jnp.exp(lgh * ek)
    cd = jnp.exp(lgh * jnp.full((1, 1), float(C), F32))
    return hm, din, dpos, qd, kd, cd, eq, ek


RET_HEADS_PER_STEP = 4


def _ret_fwd(name, qt, kt, proj, lg, rev):
    S = qt.shape[0]
    C = CHUNK
    TB = _pick(S, (512, 256, 128))
    cb, nb = TB // C, S // TB
    hps = RET_HEADS_PER_STEP
    blk = (lambda g: nb - 1 - g) if rev else (lambda g: g)

    def body(lg_ref, q_ref, k_ref, v_ref, o_ref, st_ref, state_sc):
        hg, g = pl.program_id(0), pl.program_id(1)

        @pl.when(g == 0)
        def _():
            state_sc[...] = jnp.zeros_like(state_sc)

        consts = [_ret_consts(lg_ref[hg * hps + u], u, rev) for u in range(hps)]
        order = list(reversed(range(cb))) if rev else list(range(cb))
        units = [(cc, u) for cc in order for u in range(hps)]

        def operands(cc, u):
            rows = pl.ds(cc * C, C)
            pair = slice(LANES * (u // 2), LANES * (u // 2 + 1))
            hm = consts[u][0]
            return q_ref[rows, pair] * hm, k_ref[rows, pair] * hm, v_ref[rows, LANES * u:LANES * (u + 1)].astype(MXU)

        a, inc = {}, {}
        for cc, u in units:
            q, k, v = operands(cc, u)
            a[cc, u] = _dot(q.astype(MXU), k.astype(MXU), NT) * consts[u][1]
            inc[cc, u] = _dot((k * consts[u][4]).astype(MXU), v, TN)
        for u in range(hps):
            st = state_sc[u]
            for cc in order:
                st_ref[u, cc] = st
                st = st * consts[u][5] + inc[cc, u]
            state_sc[u] = st
        for cc, u in units:
            q, _, v = operands(cc, u)
            cross = _dot((q * consts[u][3]).astype(MXU), st_ref[u, cc].astype(MXU), NN)
            o_ref[pl.ds(cc * C, C), LANES * u:LANES * (u + 1)] = _dot(a[cc, u].astype(MXU), v, NN) + cross

    qk_spec = pl.BlockSpec((TB, LANES * hps // 2), lambda h, g: (blk(g), h))
    return pl.pallas_call(
        body, name=name, grid=(HEADS // hps, nb),
        in_specs=[pl.BlockSpec(memory_space=pltpu.SMEM), qk_spec, qk_spec,
                  pl.BlockSpec((TB, LANES * hps), lambda h, g: (blk(g), P_VR // (LANES * hps) + h))],
        out_specs=[pl.BlockSpec((TB, LANES * hps), lambda h, g: (blk(g), h)),
                   pl.BlockSpec((hps, cb, LANES, LANES), lambda h, g: (h, blk(g), 0, 0))],
        out_shape=[jax.ShapeDtypeStruct((S, HEADS * LANES), F32), jax.ShapeDtypeStruct((HEADS, S // C, LANES, LANES), F32)],
        scratch_shapes=[pltpu.VMEM((hps, LANES, LANES), F32)],
        compiler_params=pltpu.CompilerParams(dimension_semantics=("parallel", "arbitrary"), vmem_limit_bytes=VMEM_LIMIT),
    )(lg, qt, kt, proj)


def _ret_bwd(name, qt, kt, proj, dret, states, lg, rev):
    S = qt.shape[0]
    C = CHUNK
    TB = _pick(S, (512, 256, 128))
    cb, nb = TB // C, S // TB
    hps = RET_HEADS_PER_STEP
    blk = (lambda g: g) if rev else (lambda g: nb - 1 - g)

    def body(lg_ref, q_ref, k_ref, v_ref, do_ref, st_ref, dq_ref, dk_ref, dv_ref, dlg_ref, ds_sc, acc_cc, acc_q, acc_k, acc_s):
        hg, g = pl.program_id(0), pl.program_id(1)

        @pl.when(g == 0)
        def _():
            ds_sc[...] = jnp.zeros_like(ds_sc)
            acc_cc[...] = jnp.zeros_like(acc_cc)
            acc_q[...] = jnp.zeros_like(acc_q)
            acc_k[...] = jnp.zeros_like(acc_k)
            acc_s[...] = jnp.zeros_like(acc_s)

        lgs = [lg_ref[hg * hps + u] for u in range(hps)]
        consts = [_ret_consts(lgs[u], u, rev) for u in range(hps)]
        order = list(range(cb)) if rev else list(reversed(range(cb)))
        units = [(cc, u) for cc in order for u in range(hps)]

        def operands(cc, u):
            rows = pl.ds(cc * C, C)
            pair = slice(LANES * (u // 2), LANES * (u // 2 + 1))
            head = slice(LANES * u, LANES * (u + 1))
            hm = consts[u][0]
            return q_ref[rows, pair] * hm, k_ref[rows, pair] * hm, v_ref[rows, head].astype(MXU), do_ref[rows, head].astype(MXU)

        a, dp, dqs, inc = {}, {}, {}, {}
        for cc, u in units:
            q, k, vb, dob = operands(cc, u)
            a[cc, u] = _dot(q.astype(MXU), k.astype(MXU), NT)
            dp[cc, u] = _dot(dob, vb, NT)
            dqs[cc, u] = _dot(dob, st_ref[u, cc].astype(MXU), NT)
            inc[cc, u] = _dot((q * consts[u][3]).astype(MXU), dob, TN)
        dsn = {}
        for u in range(hps):
            ds = ds_sc[u]
            for cc in order:
                dsn[cc, u] = ds
                ds = ds * consts[u][5] + inc[cc, u]
            ds_sc[u] = ds
        for cc, u in units:
            hm, din, dpos, qd, kd, cd, eq, ek = consts[u]
            rows, head = pl.ds(cc * C, C), slice(LANES * u, LANES * (u + 1))
            q, k, vb, dob = operands(cc, u)
            qb, kb = q.astype(MXU), k.astype(MXU)
            dsnb = dsn[cc, u].astype(MXU)
            da = (dp[cc, u] * din).astype(MXU)
            vds = _dot(vb, dsnb, NT)
            dq_ref[rows, head] = (_dot(da, kb, NN) + dqs[cc, u] * qd) * hm
            dk_ref[rows, head] = (_dot(da, qb, TN) + vds * kd) * hm
            dv_ref[rows, head] = _dot((a[cc, u] * din).astype(MXU), dob, TN) + _dot((k * kd).astype(MXU), dsnb, NN)
            acc_cc[u] += dp[cc, u] * a[cc, u] * din * dpos
            acc_q[u] += dqs[cc, u] * q * (qd * eq)
            acc_k[u] += vds * k * (kd * ek)
            acc_s[u] += dsn[cc, u] * st_ref[u, cc] * (cd * float(C))

        @pl.when(g == nb - 1)
        def _():
            for u in range(hps):
                tot = (jnp.sum(acc_cc[u], keepdims=True) + jnp.sum(acc_q[u], keepdims=True)
                       + jnp.sum(acc_k[u], keepdims=True) + jnp.sum(acc_s[u], keepdims=True))
                dlg_ref[u] = jnp.broadcast_to(tot * lgs[u], (8, LANES))

    full = jax.ShapeDtypeStruct((S, HEADS * LANES), F32)
    hspec = pl.BlockSpec((TB, LANES * hps), lambda h, g: (blk(g), h))
    qk_spec = pl.BlockSpec((TB, LANES * hps // 2), lambda h, g: (blk(g), h))
    return pl.pallas_call(
        body, name=name, grid=(HEADS // hps, nb),
        in_specs=[pl.BlockSpec(memory_space=pltpu.SMEM), qk_spec, qk_spec,
                  pl.BlockSpec((TB, LANES * hps), lambda h, g: (blk(g), P_VR // (LANES * hps) + h)),
                  hspec,
                  pl.BlockSpec((hps, cb, LANES, LANES), lambda h, g: (h, blk(g), 0, 0))],
        out_specs=[hspec, hspec, hspec, pl.BlockSpec((hps, 8, LANES), lambda h, g: (h, 0, 0))],
        out_shape=[full, full, full, jax.ShapeDtypeStruct((HEADS, 8, LANES), F32)],
        scratch_shapes=[pltpu.VMEM((hps, LANES, LANES), F32), pltpu.VMEM((hps, C, C), F32), pltpu.VMEM((hps, C, LANES), F32),
                        pltpu.VMEM((hps, C, LANES), F32), pltpu.VMEM((hps, LANES, LANES), F32)],
        compiler_params=pltpu.CompilerParams(dimension_semantics=("parallel", "arbitrary"), vmem_limit_bytes=VMEM_LIMIT),
    )(lg, qt, kt, proj, dret, states)


def _rope_consts():
    inv16 = THETA ** (-jnp.arange(16, dtype=F32) / 16)
    inv32 = THETA ** (-jnp.arange(32, dtype=F32) / 32)
    lane = np.arange(LANES)
    z48 = jnp.zeros((48,), F32)
    inv_m = jnp.concatenate([inv16, z48, inv16, z48])[None, :]
    sgn_m = jnp.asarray(np.where(lane < 16, -1.0, np.where((lane >= 64) & (lane < 80), 1.0, 0.0)), F32)[None, :]
    inv_r = jnp.concatenate([inv32] * 4)[None, :]
    sgn_r = jnp.asarray(np.where(lane < 64, -1.0, 1.0), F32)[None, :]
    return inv_m, sgn_m, inv_r, sgn_r


def _local_step(x, pos, tgt, gains, W):
    S = x.shape[0]
    ts = _pick(S, (256, 128))
    ts_wide = _pick(S, (128,))
    R = lambda a, w=None, c=0: (a, ((a.shape[1] if w is None else w), c))
    W_ = lambda a: (a, None)

    win = _win_pad(W["w_in"])
    wq = _wq_pad(W["w_q_b"])
    wk, wv = _wkv_pad(W["w_kv_b"])
    wmla = _wmla_pad(W["w_mla_out"])
    wret, wout, wgu, wdown = W["w_ret_out"], W["w_out"], W["w_gate_up"], W["w_down"]
    gqn, gkn = _qk_pad(gains["g_qn"]), _qk_pad(gains["g_kn"])
    g_mix, g_q_a, g_kv_a, g_ffn = gains["g_mix"], gains["g_q_a"], gains["g_kv_a"], gains["g_ffn"]
    lg_f = -jnp.exp(gains["ret_decay_fwd"][0])
    lg_b = -jnp.exp(gains["ret_decay_bwd"][0])

    consts = list(_rope_consts())
    cosm, sinm, cosr, sinr = _rowwise("rope_tables", _tables_fn, S, ts, [R(pos)] + [W_(c) for c in consts],
                                      [(LANES, F32, LANES, 0)] * 4)

    (h,) = _rowwise("rms_mix", _rmsg_fn, S, ts, [R(x), W_(g_mix)], [(D_MODEL, MXU, D_MODEL, 0)])
    proj = _mm("in_proj", h, win, "nn")
    seg = lambda off, w: (proj, (w, off // w))
    mla_ins = [seg(P_CQ, 256), seg(P_CKV, 128), seg(P_KROPE, 128), R(cosm), R(sinm),
               W_(g_q_a), W_(g_kv_a), W_(gqn), W_(gkn), W_(wq), W_(wk), W_(wv)]
    q, k, v = _rowwise("mla_prep", _mla_prep_fn, S, ts, mla_ins, [(HEADS * LANES, MXU, HEADS * LANES, 0)] * 3)
    o, o_bf, lse = _flash_fwd(q, k, v)
    y_a = _mm("mla_out", o_bf, wmla, "nn")

    ret_ins = [seg(P_QR, 512), seg(P_KR, 512), R(cosr), R(sinr)]
    qt, kt = _rowwise("ret_prep", _ret_prep_fn, S, ts, ret_ins, [(512, F32, 512, 0)] * 2)
    ret_f, st_f = _ret_fwd("ret_fwd_f", qt, kt, proj, lg_f, False)
    ret_b, st_b = _ret_fwd("ret_fwd_b", qt, kt, proj, lg_b, True)
    post_ins = [R(ret_f), R(ret_b), seg(P_GR, 1024)]
    (o_b,) = _rowwise("ret_post", _ret_post_fn, S, ts, post_ins, [(1024, MXU, 1024, 0)])
    y_b = _mm("ret_out", o_b, wret, "nn")

    merge_ins = [seg(P_GATES, 1024), (proj, (1024, 1)), R(y_a), R(y_b)]
    (merged,) = _rowwise("merge", _merge_fn, S, ts, merge_ins, [(D_MODEL, MXU, D_MODEL, 0)])
    x1 = _mm("out_proj", merged, wout, "nn", add=x)
    (h2,) = _rowwise("rms_ffn", _rmsg_fn, S, ts, [R(x1), W_(g_ffn)], [(D_MODEL, MXU, D_MODEL, 0)])
    gu = _mm("gate_up", h2, wgu, "nn")
    (act,) = _rowwise("swiglu", lambda t: _swiglu_fn(t[:, :FFN], t[:, FFN:]), S, ts_wide, [R(gu)], [(FFN, MXU, FFN, 0)])
    x2 = _mm("down_proj", act, wdown, "nn", add=x1)
    dx2, dx2_bf, loss_rows = _rowwise("loss", lambda a, b: (lambda d, l: (d, d, l))(*_loss_fn(a, b)), S, ts, [R(x2), R(tgt)],
                                      [(D_MODEL, F32, D_MODEL, 0), (D_MODEL, MXU, D_MODEL, 0)], accs=[(1, D_MODEL)])

    gW = {}
    gW["w_down"] = _mm("d_w_down", act, dx2_bf, "tn")
    dact = _mm("d_act", dx2_bf, wdown, "nt")

    def glu_bwd(t, da):
        _, vjp = jax.vjp(_swiglu_fn, t[:, :FFN], t[:, FFN:])
        return jnp.concatenate(vjp(da), axis=1)

    (dgu,) = _rowwise("swiglu_bwd", glu_bwd, S, ts_wide, [R(gu), R(dact)], [(2 * FFN, MXU, 2 * FFN, 0)])
    gW["w_gate_up"] = _mm("d_w_gate_up", h2, dgu, "tn")
    dh2 = _mm("d_h2", dgu, wgu, "nt")

    def rms_bwd(xx, g, dh, dres):
        _, vjp = jax.vjp(_rmsg_fn, xx, g)
        dx, dg = vjp(dh)
        dx = dx + dres
        return dx, dx, dg

    dx1, dx1_bf, dg_ffn = _rowwise("rms_ffn_bwd", rms_bwd, S, ts, [R(x1), W_(g_ffn), R(dh2), R(dx2)],
                                   [(D_MODEL, F32, D_MODEL, 0), (D_MODEL, MXU, D_MODEL, 0)], accs=[(1, D_MODEL)])
    gW["w_out"] = _mm("d_w_out", merged, dx1_bf, "tn")
    dmerged = _mm("d_merged", dx1_bf, wout, "nt")

    def merge_bwd(ga, gb, ya, yb, dm):
        _, vjp = jax.vjp(_merge_fn, ga, gb, ya, yb)
        return vjp(dm)

    dga, dgb, dy_a, dy_b = _rowwise("merge_bwd", merge_bwd, S, ts, merge_ins + [R(dmerged)], [(D_MODEL, MXU, D_MODEL, 0)] * 4)
    gW["w_ret_out"] = _mm("d_w_ret_out", o_b, dy_b, "tn")
    do_b = _mm("d_o_b", dy_b, wret, "nt")

    def post_bwd(rf, rb, gr, dob):
        _, vjp = jax.vjp(_ret_post_fn, rf, rb, gr)
        drf, _, dgr = vjp(dob)
        return drf, dgr

    dret, dg_r = _rowwise("ret_post_bwd", post_bwd, S, ts, post_ins + [R(do_b)], [(1024, F32, 1024, 0), (1024, MXU, 1024, 0)])
    dq_f, dk_f, dv_f, dlg_f = _ret_bwd("ret_bwd_f", qt, kt, proj, dret, st_f, lg_f, False)
    dq_b, dk_b, dv_b, dlg_b = _ret_bwd("ret_bwd_b", qt, kt, proj, dret, st_b, lg_b, True)

    def ret_prep_bwd(qr, kr, cosr_, sinr_, dqf, dqb, dkf, dkb, dvf, dvb):
        _, vjp = jax.vjp(lambda a, b: _ret_prep_fn(a, b, cosr_, sinr_), qr, kr)
        pair = lambda t: jnp.concatenate([t[:, 256 * j:256 * j + 128] + t[:, 256 * j + 128:256 * j + 256] for j in range(4)], axis=1)
        dqr, dkr = vjp((pair(dqf + dqb), pair(dkf + dkb)))
        return dqr, dkr, dvf + dvb

    dq_r, dk_r, dv_r = _rowwise("ret_prep_bwd", ret_prep_bwd, S, ts, ret_ins + [R(t) for t in (dq_f, dq_b, dk_f, dk_b, dv_f, dv_b)],
                                [(512, MXU, 512, 0), (512, MXU, 512, 0), (1024, MXU, 1024, 0)])

    gW_mla_p = _mm("d_w_mla_out", o_bf, dy_a, "tn")
    do = _mm("d_o", dy_a, wmla, "nt")
    do_bf, delta = _rowwise("attn_delta", _delta_fn, S, ts, [R(o), R(do)], [(HEADS * LANES, MXU, HEADS * LANES, 0), (HEADS * LANES, F32, HEADS * LANES, 0)])
    dq, dk, dv = _flash_bwd(q, k, v, do_bf, lse, delta)

    def mla_prep_bwd(cq, ckv, kr, cosm_, sinm_, gqa, gkva, gqn_, gkn_, wq_, wk_, wv_, dq_, dk_, dv_):
        f = lambda cq, ckv, kr, gqa, gkva, gqn_, gkn_, wq_, wk_, wv_: _mla_prep_fn(cq, ckv, kr, cosm_, sinm_, gqa, gkva, gqn_, gkn_, wq_, wk_, wv_)
        _, vjp = jax.vjp(f, cq, ckv, kr, gqa, gkva, gqn_, gkn_, wq_.astype(F32), wk_.astype(F32), wv_.astype(F32))
        return vjp((dq_, dk_, dv_))

    mb = _rowwise("mla_prep_bwd", mla_prep_bwd, S, ts, mla_ins + [R(dq), R(dk), R(dv)],
                  [(256, MXU, 256, 0), (128, MXU, 128, 0), (128, MXU, 128, 0)],
                  accs=[(1, 256), (1, 128), (1, LANES), (1, LANES), (256, HEADS * LANES), (128, HEADS * LANES), (128, HEADS * LANES)])
    dc_q, dc_kv, dk_rope, dg_q_a, dg_kv_a, dgqn_p, dgkn_p, dwq_p, dwk_p, dwv_p = mb

    dproj = jnp.concatenate([dga, dgb, dv_r, dg_r, dq_r, dk_r, dc_q, dc_kv, dk_rope], axis=1)
    gwin_p = _mm("d_w_in", h, dproj, "tn")
    dh = _mm("d_h", dproj, win, "nt")
    grad_x, _, dg_mix = _rowwise("rms_mix_bwd", rms_bwd, S, ts, [R(x), W_(g_mix), R(dh), R(dx1)],
                                 [(D_MODEL, F32, D_MODEL, 0), (D_MODEL, MXU, D_MODEL, 0)], accs=[(1, D_MODEL)])

    gW["w_in"] = _win_unpad(gwin_p)
    gW["w_q_b"] = _wq_unpad(dwq_p)
    gW["w_kv_b"] = _wkv_unpad(dwk_p, dwv_p)
    gW["w_mla_out"] = _wmla_unpad(gW_mla_p)
    gG = {"g_mix": dg_mix, "g_q_a": dg_q_a, "g_kv_a": dg_kv_a, "g_qn": _qk_unpad(dgqn_p),
          "g_kn": _qk_unpad(dgkn_p), "ret_decay_fwd": dlg_f[:, 0, 0][None, :], "ret_decay_bwd": dlg_b[:, 0, 0][None, :],
          "g_ffn": dg_ffn}
    return loss_rows, grad_x, gG, gW


MATS = [("w_in", (1024, 5536), 1), ("w_q_b", (256, 768), 1), ("w_kv_b", (128, 1024), 1), ("w_mla_out", (512, 1024), 1),
        ("w_ret_out", (1024, 1024), 0), ("w_out", (1024, 1024), 0), ("w_gate_up", (1024, 5632), 1), ("w_down", (2816, 1024), 0)]
GAINS = [("g_mix", 1024), ("g_q_a", 256), ("g_kv_a", 128), ("g_qn", 96), ("g_kn", 96), ("ret_decay_fwd", 8), ("ret_decay_bwd", 8),
         ("g_ffn", 1024)]
ORDER = ["g_mix", "w_in", "g_q_a", "w_q_b", "g_kv_a", "w_kv_b", "g_qn", "g_kn", "w_mla_out", "ret_decay_fwd", "ret_decay_bwd",
         "w_ret_out", "w_out", "g_ffn", "w_gate_up", "w_down"]
GAIN_LEN = sum(n for _, n in GAINS)
GAIN_PAD = -(-GAIN_LEN // LANES) * LANES


def _pack_gains(d):
    row = jnp.concatenate([d[n].reshape(1, ln).astype(F32) for n, ln in GAINS], axis=1)
    return jnp.pad(row, ((0, 0), (0, GAIN_PAD - GAIN_LEN)))


def _unpack_gains(row):
    out, off = {}, 0
    for n, ln in GAINS:
        out[n] = row[0, off:off + ln]
        off += ln
    return out


def _unshard(pieces, axis):
    if axis == 0:
        return pieces.reshape((N_DEV * pieces.shape[1], pieces.shape[2]))
    return jnp.concatenate([pieces[p] for p in range(N_DEV)], axis=1)


def _reshard(full, axis):
    if axis == 0:
        return full.reshape((N_DEV, full.shape[0] // N_DEV, full.shape[1]))
    c = full.shape[1] // N_DEV
    return jnp.stack([full[:, c * p:c * (p + 1)] for p in range(N_DEV)])


def _all_gather(shards):
    n = len(shards)

    def body(*refs):
        x_refs, out_refs = refs[:n], refs[n:2 * n]
        send_sems, recv_sems, local_sems = refs[2 * n:]
        x, y, c = lax.axis_index("x"), lax.axis_index("y"), lax.axis_index("c")
        me, sibling = (x, y, c), (x, y, 1 - c)
        chips = [(1 - x, y), (x, 1 - y), (1 - x, 1 - y)]

        def slot(a, px, py, pc):
            return out_refs[a].at[4 * px + 2 * py + pc]

        def copy(a, k, block, to, from_input=False):
            return pltpu.make_async_remote_copy(
                src_ref=x_refs[a] if from_input else slot(a, *block), dst_ref=slot(a, *block),
                send_sem=send_sems.at[a, k], recv_sem=recv_sems.at[a, k], device_id=to, device_id_type=pl.DeviceIdType.MESH)

        mine = [pltpu.make_async_copy(x_refs[a], slot(a, *me), local_sems.at[a]) for a in range(n)]
        first = [copy(a, 0, me, sibling, True) for a in range(n)]
        first += [copy(a, 1 + j, me, (*chip, c), True) for j, chip in enumerate(chips) for a in range(n)]
        for cp in mine + first:
            cp.start()
        passed = []
        for j, chip in enumerate(chips):
            for a in range(n):
                copy(a, 1 + j, (*chip, c), me).wait_recv()
                passed.append(copy(a, 4 + j, (*chip, c), sibling))
                passed[-1].start()
        for a in range(n):
            copy(a, 0, sibling, me).wait_recv()
        for j, chip in enumerate(chips):
            for a in range(n):
                copy(a, 4 + j, (*chip, 1 - c), me).wait_recv()
        for cp in first + passed:
            cp.wait_send()
        for cp in mine:
            cp.wait()

    any_spec = pl.BlockSpec(memory_space=pl.ANY)
    return pl.pallas_call(
        body, name="all_gather_weights", out_shape=[jax.ShapeDtypeStruct((N_DEV,) + s.shape, s.dtype) for s in shards],
        in_specs=[any_spec] * n, out_specs=[any_spec] * n,
        scratch_shapes=[pltpu.SemaphoreType.DMA((n, 7)), pltpu.SemaphoreType.DMA((n, 7)), pltpu.SemaphoreType.DMA((n,))],
    )(*shards)


def _all_to_all(pieces):
    n = len(pieces)

    def body(*refs):
        in_refs, out_refs = refs[:n], refs[n:2 * n]
        send_sems, recv_sems, local_sems = refs[2 * n:]
        x, y, c = lax.axis_index("x"), lax.axis_index("y"), lax.axis_index("c")
        my_id = 4 * x + 2 * y + c
        flips = [(fx, fy, fc) for fx in (0, 1) for fy in (0, 1) for fc in (0, 1)][1:]

        def copy(a, kk, f):
            p = (x ^ f[0], y ^ f[1], c ^ f[2])
            return pltpu.make_async_remote_copy(
                src_ref=in_refs[a].at[4 * p[0] + 2 * p[1] + p[2]], dst_ref=out_refs[a].at[my_id],
                send_sem=send_sems.at[a, kk], recv_sem=recv_sems.at[a, kk], device_id=p, device_id_type=pl.DeviceIdType.MESH)

        mine = [pltpu.make_async_copy(in_refs[a].at[my_id], out_refs[a].at[my_id], local_sems.at[a]) for a in range(n)]
        copies = [copy(a, kk, f) for kk, f in enumerate(flips) for a in range(n)]
        for cp in mine + copies:
            cp.start()
        for cp in copies:
            cp.wait_recv()
        for cp in copies:
            cp.wait_send()
        for cp in mine:
            cp.wait()

    any_spec = pl.BlockSpec(memory_space=pl.ANY)
    return pl.pallas_call(
        body, name="all_to_all_grads", out_shape=[jax.ShapeDtypeStruct(p.shape, p.dtype) for p in pieces],
        in_specs=[any_spec] * n, out_specs=[any_spec] * n,
        scratch_shapes=[pltpu.SemaphoreType.DMA((n, 7)), pltpu.SemaphoreType.DMA((n, 7)), pltpu.SemaphoreType.DMA((n,))],
    )(*pieces)


def _adamw(name, parts, w, m, v):
    rows, cols = w.shape
    tr = _pick(rows, (128, 64, 32, 16, 8))
    pspec = pl.BlockSpec((N_DEV, tr, cols), lambda i: (0, i, 0))
    rspec = pl.BlockSpec((tr, cols), lambda i: (i, 0))

    def body(p_ref, w_ref, m_ref, v_ref, g_ref, d_ref, m2_ref, v2_ref):
        g, d, m2, v2 = _adamw_fn([p_ref[s] for s in range(N_DEV)], w_ref[...], m_ref[...], v_ref[...])
        g_ref[...], d_ref[...], m2_ref[...], v2_ref[...] = g, d, m2, v2

    return pl.pallas_call(
        body, name=name, grid=(rows // tr,), in_specs=[pspec, rspec, rspec, rspec], out_specs=[rspec] * 4,
        out_shape=[jax.ShapeDtypeStruct((rows, cols), F32)] * 4,
        compiler_params=pltpu.CompilerParams(dimension_semantics=("parallel",), vmem_limit_bytes=VMEM_LIMIT),
    )(parts, w, m, v)


def kernel(x, positions, g_mix, w_in, g_q_a, w_q_b, g_kv_a, w_kv_b, g_qn, g_kn, w_mla_out, ret_decay_fwd, ret_decay_bwd, w_ret_out, w_out, g_ffn, w_gate_up, w_down, loss_target, m_g_mix, m_w_in, m_g_q_a, m_w_q_b, m_g_kv_a, m_w_kv_b, m_g_qn, m_g_kn, m_w_mla_out, m_ret_decay_fwd, m_ret_decay_bwd, m_w_ret_out, m_w_out, m_g_ffn, m_w_gate_up, m_w_down, v_g_mix, v_w_in, v_g_q_a, v_w_q_b, v_g_kv_a, v_w_kv_b, v_g_qn, v_g_kn, v_w_mla_out, v_ret_decay_fwd, v_ret_decay_bwd, v_w_ret_out, v_w_out, v_g_ffn, v_w_gate_up, v_w_down):
    w = dict(g_mix=g_mix, w_in=w_in, g_q_a=g_q_a, w_q_b=w_q_b, g_kv_a=g_kv_a, w_kv_b=w_kv_b, g_qn=g_qn, g_kn=g_kn, w_mla_out=w_mla_out,
             ret_decay_fwd=ret_decay_fwd, ret_decay_bwd=ret_decay_bwd, w_ret_out=w_ret_out, w_out=w_out, g_ffn=g_ffn,
             w_gate_up=w_gate_up, w_down=w_down)
    m = dict(g_mix=m_g_mix, w_in=m_w_in, g_q_a=m_g_q_a, w_q_b=m_w_q_b, g_kv_a=m_g_kv_a, w_kv_b=m_w_kv_b, g_qn=m_g_qn, g_kn=m_g_kn,
             w_mla_out=m_w_mla_out, ret_decay_fwd=m_ret_decay_fwd, ret_decay_bwd=m_ret_decay_bwd, w_ret_out=m_w_ret_out, w_out=m_w_out,
             g_ffn=m_g_ffn, w_gate_up=m_w_gate_up, w_down=m_w_down)
    v = dict(g_mix=v_g_mix, w_in=v_w_in, g_q_a=v_g_q_a, w_q_b=v_w_q_b, g_kv_a=v_g_kv_a, w_kv_b=v_w_kv_b, g_qn=v_g_qn, g_kn=v_g_kn,
             w_mla_out=v_w_mla_out, ret_decay_fwd=v_ret_decay_fwd, ret_decay_bwd=v_ret_decay_bwd, w_ret_out=v_w_ret_out, w_out=v_w_out,
             g_ffn=v_g_ffn, w_gate_up=v_w_gate_up, w_down=v_w_down)
    gains = {n: w[n].reshape(1, ln) for n, ln in GAINS}

    gathered = _all_gather([w[n].astype(WIRE) for n, _, _ in MATS])
    W = {n: _unshard(g, axis) for (n, _, axis), g in zip(MATS, gathered)}
    S = x.shape[1]
    pos = positions.reshape(S, 1).astype(F32)
    loss_rows, grad_x, gG, gW = _local_step(x.reshape(S, D_MODEL), pos, loss_target.reshape(S, D_MODEL), gains, W)
    loss = lax.psum(jnp.sum(loss_rows), ("x", "y", "c"))

    pieces = [_reshard(gW[n], axis).astype(GWIRE) for n, _, axis in MATS]
    pieces.append(jnp.broadcast_to(_pack_gains(gG)[None], (N_DEV, 1, GAIN_PAD)))
    parts = _all_to_all(pieces)
    out = [dict() for _ in range(4)]
    for (n, _, _), p in zip(MATS, parts):
        for o, r in zip(out, _adamw("adamw_" + n, p, w[n], m[n], v[n])):
            o[n] = r
    for o, r in zip(out, _adamw("adamw_gains", parts[-1], _pack_gains(w), _pack_gains(m), _pack_gains(v))):
        o.update(_unpack_gains(r))
    return (loss, grad_x.reshape(x.shape), *[o[n] for o in out for n in ORDER])
```

```python
import functools

import numpy as np
import jax
import jax.numpy as jnp
from jax import lax
from jax.experimental import pallas as pl
from jax.experimental.pallas import tpu as pltpu

F32 = jnp.float32
MXU = jnp.bfloat16
WIRE = jnp.bfloat16
GWIRE = jnp.bfloat16

N_DEV = 8
D_MODEL = 1024
HEADS = 8
LANES = 128
Q_RANK, KV_RANK = 256, 128
NOPE, ROPE_M, V_M = 64, 32, 64
QK_M = NOPE + ROPE_M
RQK, RV = 64, 128
CHUNK = 128
FFN = 2816
IN_WIDTH = 5536
THETA = 10000.0
EPS = 1e-6
LR, B1, B2, AEPS, WD, STEP = 0.001, 0.9, 0.999, 1e-08, 0.01, 10
VMEM_LIMIT = 56 * 1024 * 1024

NN = ((1,), (0,))
NT = ((1,), (1,))
TN = ((0,), (0,))

P_GATES, P_VR, P_GR, P_QR, P_KR, P_CQ, P_CKV, P_KROPE, P_WIDTH = 0, 2048, 3072, 4096, 4608, 5120, 5376, 5504, 5632
O_CQ, O_CKV, O_KROPE, O_QR, O_KR, O_VR, O_GR, O_GATES = 0, 256, 384, 416, 928, 1440, 2464, 3488


def _dot(a, b, dims):
    return lax.dot_general(a, b, (dims, ((), ())), preferred_element_type=F32)


def _pick(dim, cands):
    for c in cands:
        if dim % c == 0:
            return c
    return dim


def _pairs(t):
    return t.reshape(t.shape[0], 4, 2, 2, 32).transpose(0, 1, 3, 2, 4).reshape(t.shape[0], 512)


def _win_pad(w):
    z = jnp.zeros((w.shape[0], 48), w.dtype)
    kr = w[:, O_KROPE:O_KROPE + 32]
    return jnp.concatenate([w[:, O_GATES:], w[:, O_VR:O_VR + 1024], w[:, O_GR:O_GR + 1024], _pairs(w[:, O_QR:O_QR + 512]),
                            _pairs(w[:, O_KR:O_KR + 512]), w[:, :O_CKV], w[:, O_CKV:O_KROPE], kr[:, :16], z, kr[:, 16:], z], axis=1)


def _win_unpad(g):
    return jnp.concatenate([g[:, P_CQ:P_CQ + 256], g[:, P_CKV:P_CKV + 128], g[:, P_KROPE:P_KROPE + 16], g[:, P_KROPE + 64:P_KROPE + 80],
                            _pairs(g[:, P_QR:P_QR + 512]), _pairs(g[:, P_KR:P_KR + 512]), g[:, P_VR:P_VR + 1024],
                            g[:, P_GR:P_GR + 1024], g[:, P_GATES:P_GATES + 2048]], axis=1)


def _qk_pad(t):
    z = jnp.zeros(t.shape[:-1] + (32,), t.dtype)
    return jnp.concatenate([t[..., 64:80], t[..., 0:48], t[..., 80:96], t[..., 48:64], z], axis=-1)


def _qk_unpad(p):
    return jnp.concatenate([p[..., 16:64], p[..., 80:96], p[..., 0:16], p[..., 64:80]], axis=-1)


def _wq_pad(w):
    return _qk_pad(w.reshape(Q_RANK, HEADS, QK_M)).reshape(Q_RANK, HEADS * LANES)


def _wq_unpad(g):
    return _qk_unpad(g.reshape(Q_RANK, HEADS, LANES)).reshape(Q_RANK, HEADS * QK_M)


def _wkv_pad(w):
    t = w.reshape(KV_RANK, HEADS, NOPE + V_M)
    z = lambda n: jnp.zeros((KV_RANK, HEADS, n), w.dtype)
    wk = jnp.concatenate([z(16), t[..., 0:48], z(16), t[..., 48:64], z(32)], axis=-1)
    wv = jnp.concatenate([t[..., 64:128], z(64)], axis=-1)
    return wk.reshape(KV_RANK, HEADS * LANES), wv.reshape(KV_RANK, HEADS * LANES)


def _wkv_unpad(dwk, dwv):
    k, v = dwk.reshape(KV_RANK, HEADS, LANES), dwv.reshape(KV_RANK, HEADS, LANES)
    return jnp.concatenate([k[..., 16:64], k[..., 80:96], v[..., 0:64]], axis=-1).reshape(KV_RANK, HEADS * (NOPE + V_M))


def _wmla_pad(w):
    t = w.reshape(HEADS, V_M, D_MODEL)
    return jnp.concatenate([t, jnp.zeros_like(t)], axis=1).reshape(HEADS * LANES, D_MODEL)


def _wmla_unpad(g):
    return g.reshape(HEADS, LANES, D_MODEL)[:, :V_M].reshape(HEADS * V_M, D_MODEL)


def _rowwise(name, fn, rows, ts, ins, outs, accs=(), ncol=1):
    n_in, n_out, n_acc = len(ins), len(outs), len(accs)

    def colmap(col):
        if callable(col):
            return lambda i, j: (i, col(j))
        return lambda i, j: (i, col)

    arrays, in_specs = [], []
    for arr, spec in ins:
        arrays.append(arr)
        if spec is None:
            in_specs.append(pl.BlockSpec(arr.shape, functools.partial(lambda i, j, nd: (0,) * nd, nd=arr.ndim)))
        else:
            in_specs.append(pl.BlockSpec((ts, spec[0]), colmap(spec[1])))
    out_shape, out_specs = [], []
    for total, dtype, width, col in outs:
        out_shape.append(jax.ShapeDtypeStruct((rows, total), dtype))
        out_specs.append(pl.BlockSpec((ts, width), colmap(col)))
    for shp in accs:
        out_shape.append(jax.ShapeDtypeStruct(shp, F32))
        out_specs.append(pl.BlockSpec(shp, functools.partial(lambda i, j, nd: (0,) * nd, nd=len(shp))))

    def body(*refs):
        vals = [r[...] for r in refs[:n_in]]
        res = fn(*vals)
        if not isinstance(res, (tuple, list)):
            res = (res,)
        for r, v in zip(refs[n_in:n_in + n_out], res[:n_out]):
            r[...] = v.astype(r.dtype)
        if n_acc:
            first = jnp.logical_and(pl.program_id(0) == 0, pl.program_id(1) == 0)
            for r, v in zip(refs[n_in + n_out:], res[n_out:]):
                @pl.when(first)
                def _(r=r):
                    r[...] = jnp.zeros_like(r)
                r[...] += v.astype(F32)

    res = pl.pallas_call(
        body, name=name, grid=(rows // ts, ncol), in_specs=in_specs, out_specs=out_specs, out_shape=out_shape,
        compiler_params=pltpu.CompilerParams(dimension_semantics=("arbitrary", "arbitrary"), vmem_limit_bytes=VMEM_LIMIT),
    )(*arrays)
    return res


MM_OPERAND_BYTES = 24 * 1024 * 1024


def _mm(name, a, b, mode, add=None):
    if mode == "nn":
        (M, K), N = a.shape, b.shape[1]
    elif mode == "nt":
        (M, K), N = a.shape, b.shape[0]
    else:
        (K, M), N = a.shape, b.shape[1]
    tm = _pick(M, (512, 256, 128)) if mode == "tn" else _pick(M, (1024, 512, 256, 128))
    tn = _pick(N, (512, 256, 128))
    fits = lambda t: 2 * (tm + tn) * t * a.dtype.itemsize <= MM_OPERAND_BYTES
    tk = next(t for t in (K, 4096, 2816, 2048, 1408, 1024, 512, 256, 128) if K % t == 0 and (fits(t) or t == 128))
    nk = K // tk
    dims = {"nn": NN, "nt": NT, "tn": TN}[mode]
    a_spec = pl.BlockSpec((tk, tm), lambda i, j, k: (k, i)) if mode == "tn" else pl.BlockSpec((tm, tk), lambda i, j, k: (i, k))
    b_spec = pl.BlockSpec((tn, tk), lambda i, j, k: (j, k)) if mode == "nt" else pl.BlockSpec((tk, tn), lambda i, j, k: (k, j))
    o_spec = pl.BlockSpec((tm, tn), lambda i, j, k: (i, j))
    has_add = add is not None

    def body(*refs):
        a_ref, b_ref, o_ref = refs[0], refs[1], refs[-1]
        d = _dot(a_ref[...], b_ref[...], dims)
        first = (d + refs[2][...]) if has_add else d
        if nk == 1:
            o_ref[...] = first
        else:
            k = pl.program_id(2)

            @pl.when(k == 0)
            def _():
                o_ref[...] = first

            @pl.when(k > 0)
            def _():
                o_ref[...] += d

    args = [a, b] + ([add] if has_add else [])
    specs = [a_spec, b_spec] + ([o_spec] if has_add else [])
    return pl.pallas_call(
        body, name=name, grid=(M // tm, N // tn, nk), in_specs=specs, out_specs=o_spec,
        out_shape=jax.ShapeDtypeStruct((M, N), F32),
        compiler_params=pltpu.CompilerParams(dimension_semantics=("parallel", "parallel", "arbitrary"), vmem_limit_bytes=VMEM_LIMIT),
    )(*args)


@jax.custom_vjp
def _swap64(x):
    return pltpu.roll(x, 64, 1)


_swap64.defvjp(lambda x: (_swap64(x), None), lambda _, g: (_swap64(g),))


@jax.custom_vjp
def _mxdot(a, b):
    return _dot(a.astype(MXU), b.astype(MXU), NN)


def _mxdot_bwd(res, g):
    a, b = res
    gb = g.astype(MXU)
    return _dot(gb, b.astype(MXU), NT), _dot(a.astype(MXU), gb, TN)


_mxdot.defvjp(lambda a, b: (_mxdot(a, b), (a, b)), _mxdot_bwd)


def _rms(x):
    return x * lax.rsqrt(jnp.mean(x * x, axis=-1, keepdims=True) + EPS)


def _rmsg_fn(x, g):
    return _rms(x) * g


def _silu(x):
    return x * jax.nn.sigmoid(x)


def _tables_fn(pos, inv_m, sgn_m, inv_r, sgn_r):
    am, ar = pos * inv_m, pos * inv_r
    return jnp.cos(am), jnp.sin(am) * sgn_m, jnp.cos(ar), jnp.sin(ar) * sgn_r


def _head_blocks(t):
    return [t[:, LANES * h:LANES * (h + 1)] for h in range(t.shape[1] // LANES)]


def _mla_prep_fn(cq, ckv, kr, cosm, sinm, gqa, gkva, gqn, gkn, wq, wk, wv):
    cqn = _rms(cq) * gqa
    ckvn = _rms(ckv) * gkva
    q_raw = _mxdot(cqn, wq)
    k_raw = _mxdot(ckvn, wk)
    lane = lax.broadcasted_iota(jnp.int32, (1, HEADS * LANES), 1)
    v = _mxdot(ckvn, wv) + (lane % LANES == V_M).astype(F32)

    def norm_rope(blocks, g, extra):
        outs = []
        for b in blocks:
            if extra is not None:
                b = b + extra
            n = b * lax.rsqrt(jnp.sum(b * b, axis=-1, keepdims=True) * (1.0 / QK_M) + EPS) * g
            outs.append(n * cosm + _swap64(n) * sinm)
        return jnp.concatenate(outs, axis=1)

    q = norm_rope(_head_blocks(q_raw), gqn, None)
    k = norm_rope(_head_blocks(k_raw), gkn, kr)
    return q, k, v


def _ret_prep_fn(qr, kr, cosr, sinr):
    def rope(t, scale):
        return jnp.concatenate([(b * cosr + _swap64(b) * sinr) * scale for b in _head_blocks(t)], axis=1)
    return rope(qr, 1.0), rope(kr, RQK ** -0.5)


def _ret_post_fn(rf, rb, gr):
    ret = rf + rb
    outs = []
    for b, g in zip(_head_blocks(ret), _head_blocks(gr)):
        outs.append(_silu(g) * _rms(b))
    return jnp.concatenate(outs, axis=1)


def _merge_fn(ga, gb, ya, yb):
    return jax.nn.sigmoid(ga) * ya + jax.nn.sigmoid(gb) * yb


def _swiglu_fn(gate, up):
    return _silu(gate) * up


def _loss_fn(x2, tgt):
    d = x2 - tgt
    return d * (1.0 / D_MODEL), 0.5 * jnp.sum(d * d, axis=0, keepdims=True) * (1.0 / D_MODEL)


def _adamw_fn(parts, w, m, v):
    g = parts[0].astype(F32)
    for p in range(1, N_DEV):
        g = g + parts[p].astype(F32)
    m2 = B1 * m + (1.0 - B1) * g
    v2 = B2 * v + (1.0 - B2) * jnp.square(g)
    m_hat = m2 / (1.0 - B1 ** STEP)
    v_hat = v2 / (1.0 - B2 ** STEP)
    delta = -LR * (m_hat / (jnp.sqrt(v_hat) + AEPS) + WD * w)
    return g, delta, m2, v2


SCALE = QK_M ** -0.5
LOG2E = 1.4426950408889634
FLASH_ROWS = 32


def _flash_fwd(q, k, v):
    S = q.shape[0]
    tq = tk = _pick(S, (512, 256, 128))
    ncb = tk // LANES
    nkv = S // tk
    assert nkv % 2 == 0, "kv tiles are processed in pairs"
    mrows = 64
    c = SCALE * LOG2E

    def body(q_ref, k_ref, v_ref, o_ref, obf_ref, lse_ref, s_a, p_a, s_b, p_b, m_sc, a_sc, acc_sc):
        m_sc[...] = jnp.full_like(m_sc, -jnp.inf)
        acc_sc[...] = jnp.zeros_like(acc_sc)
        qb = q_ref[...]

        def scores(j, s_buf):
            s_buf[...] = _dot(qb, k_ref[pl.ds(pl.multiple_of(j * tk, tk), tk), :], NT)

        def stage(j, s_buf, p_buf, s_next):
            scores(jnp.minimum(j + 1, nkv - 1), s_next)
            for r in range(tq // mrows):
                rows = slice(r * mrows, (r + 1) * mrows)
                cols = [s_buf[rows, LANES * cb:LANES * (cb + 1)] for cb in range(ncb)]
                m_prev = m_sc[rows, :]
                row_max = jnp.max(functools.reduce(jnp.maximum, cols), axis=-1, keepdims=True)
                m_new = jnp.maximum(m_prev, jnp.broadcast_to(row_max, (mrows, LANES)))
                a_sc[rows, :] = jnp.exp2((m_prev - m_new) * c)
                m_sc[rows, :] = m_new
                for cb in range(ncb):
                    p_buf[rows, LANES * cb:LANES * (cb + 1)] = jnp.exp2((cols[cb] - m_new) * c).astype(p_buf.dtype)
            acc_sc[...] = a_sc[...] * acc_sc[...] + _dot(p_buf[...], v_ref[pl.ds(pl.multiple_of(j * tk, tk), tk), :], NN)

        scores(0, s_a)

        def pair_step(t, carry):
            stage(2 * t, s_a, p_a, s_b)
            stage(2 * t + 1, s_b, p_b, s_a)
            return carry

        lax.fori_loop(0, nkv // 2, pair_step, 0, unroll=4)
        acc = acc_sc[...]
        lane = lax.broadcasted_iota(jnp.int32, (1, LANES), 1)
        l = jnp.sum(jnp.where(lane == V_M, acc, 0.0), axis=-1, keepdims=True)
        o = acc / l
        o_ref[...] = o
        obf_ref[...] = o.astype(obf_ref.dtype)
        lse_ref[...] = m_sc[...] * c + jnp.log2(jnp.broadcast_to(l, (tq, LANES)))

    qspec = pl.BlockSpec((tq, LANES), lambda h, i: (i, h))
    kspec = pl.BlockSpec((S, LANES), lambda h, i: (0, h))
    full = jax.ShapeDtypeStruct((S, HEADS * LANES), F32)
    return pl.pallas_call(
        body, name="flash_fwd", grid=(HEADS, S // tq), in_specs=[qspec, kspec, kspec], out_specs=[qspec, qspec, qspec],
        out_shape=[full, jax.ShapeDtypeStruct((S, HEADS * LANES), MXU), full],
        scratch_shapes=[pltpu.VMEM((tq, tk), F32), pltpu.VMEM((tq, tk), MXU)] * 2 + [pltpu.VMEM((tq, LANES), F32)] * 3,
        compiler_params=pltpu.CompilerParams(dimension_semantics=("parallel", "arbitrary"), vmem_limit_bytes=VMEM_LIMIT),
    )(q, k, v)


def _delta_fn(o, do):
    outs = [jnp.broadcast_to(jnp.sum(a * b, axis=-1, keepdims=True), a.shape) for a, b in zip(_head_blocks(o), _head_blocks(do))]
    return do, jnp.concatenate(outs, axis=1)


def _flash_bwd(q, k, v, do, lse, delta):
    S = q.shape[0]
    tq = tk = _pick(S, (512, 256, 128))
    ncb = tk // LANES
    c = SCALE * LOG2E

    nq = S // tq
    assert nq % 2 == 0, "q tiles are processed in pairs"

    def body(q_ref, k_ref, v_ref, do_ref, lse_ref, dl_ref, dq_ref, dk_ref, dv_ref, s_a, dp_a, p_a, ds_a, s_b, dp_b, p_b, ds_b, dk_sc, dv_sc):
        @pl.when(pl.program_id(1) == 0)
        def _():
            dq_ref[...] = jnp.zeros_like(dq_ref)

        dk_sc[...] = jnp.zeros_like(dk_sc)
        dv_sc[...] = jnp.zeros_like(dv_sc)
        kb, vb = k_ref[...], v_ref[...]

        def scores(i, s_buf, dp_buf):
            q_rows = pl.ds(pl.multiple_of(i * tq, tq), tq)
            s_buf[...] = _dot(q_ref[q_rows, :], kb, NT)
            dp_buf[...] = _dot(do_ref[q_rows, :], vb, NT)

        def stage(i, s_buf, dp_buf, p_buf, ds_buf, s_next, dp_next):
            scores(jnp.minimum(i + 1, nq - 1), s_next, dp_next)
            for r in range(tq // FLASH_ROWS):
                rows = slice(r * FLASH_ROWS, (r + 1) * FLASH_ROWS)
                grows = pl.ds(pl.multiple_of(i * tq + r * FLASH_ROWS, FLASH_ROWS), FLASH_ROWS)
                lse_b, dl_b = lse_ref[grows, :], dl_ref[grows, :]
                for cb in range(ncb):
                    sl = slice(LANES * cb, LANES * (cb + 1))
                    p = jnp.exp2(s_buf[rows, sl] * c - lse_b)
                    p_buf[rows, sl] = p.astype(p_buf.dtype)
                    ds_buf[rows, sl] = (p * (dp_buf[rows, sl] - dl_b) * SCALE).astype(ds_buf.dtype)
            q_rows = pl.ds(pl.multiple_of(i * tq, tq), tq)
            dv_sc[...] += _dot(p_buf[...], do_ref[q_rows, :], TN)
            dk_sc[...] += _dot(ds_buf[...], q_ref[q_rows, :], TN)
            dq_ref[q_rows, :] += _dot(ds_buf[...], kb, NN)

        scores(0, s_a, dp_a)

        def pair_step(t, carry):
            stage(2 * t, s_a, dp_a, p_a, ds_a, s_b, dp_b)
            stage(2 * t + 1, s_b, dp_b, p_b, ds_b, s_a, dp_a)
            return carry

        lax.fori_loop(0, nq // 2, pair_step, 0, unroll=2)
        dk_ref[...] = dk_sc[...]
        dv_ref[...] = dv_sc[...]

    hspec = pl.BlockSpec((S, LANES), lambda h, j: (0, h))
    kspec = pl.BlockSpec((tk, LANES), lambda h, j: (j, h))
    full = jax.ShapeDtypeStruct((S, HEADS * LANES), F32)
    tile_bufs = [pltpu.VMEM((tq, tk), F32), pltpu.VMEM((tq, tk), F32), pltpu.VMEM((tq, tk), MXU), pltpu.VMEM((tq, tk), MXU)]
    return pl.pallas_call(
        body, name="flash_bwd", grid=(HEADS, S // tk), in_specs=[hspec, kspec, kspec, hspec, hspec, hspec],
        out_specs=[hspec, kspec, kspec], out_shape=[full, full, full],
        scratch_shapes=tile_bufs + tile_bufs + [pltpu.VMEM((tk, LANES), F32), pltpu.VMEM((tk, LANES), F32)],
        compiler_params=pltpu.CompilerParams(dimension_semantics=("parallel", "arbitrary"), vmem_limit_bytes=VMEM_LIMIT),
    )(q, k, v, do, lse, delta)


def _ret_consts(lgh, head, rev):
    C = CHUNK
    lane = lax.broadcasted_iota(jnp.int32, (1, LANES), 1)
    hm = ((lane // 32) % 2 == head % 2).astype(F32)
    r = lax.broadcasted_iota(jnp.int32, (C, C), 0)
    c = lax.broadcasted_iota(jnp.int32, (C, C), 1)
    diff = ((c - r) if rev else (r - c)).astype(F32)
    mask = (diff > 0) if rev else (diff >= 0)
    dpos = jnp.maximum(diff, 0.0)
    din = jnp.where(mask, jnp.exp(lgh * dpos), 0.0)
    idx = lax.broadcasted_iota(jnp.int32, (C, 1), 0).astype(F32)
    eq = (C - idx) if rev else (idx + 1.0)
    ek = idx if rev else (C - 1.0 - idx)
    qd, kd = jnp.exp(lgh * eq), jnp.exp(lgh * ek)
    cd = jnp.exp(lgh * jnp.full((1, 1), float(C), F32))
    return hm, din, dpos, qd, kd, cd, eq, ek


RET_HEADS_PER_STEP = 4


def _ret_fwd(name, qt, kt, proj, lg, rev):
    S = qt.shape[0]
    C = CHUNK
    TB = _pick(S, (512, 256, 128))
    cb, nb = TB // C, S // TB
    hps = RET_HEADS_PER_STEP
    blk = (lambda g: nb - 1 - g) if rev else (lambda g: g)

    def body(lg_ref, q_ref, k_ref, v_ref, o_ref, st_ref, state_sc):
        hg, g = pl.program_id(0), pl.program_id(1)

        @pl.when(g == 0)
        def _():
            state_sc[...] = jnp.zeros_like(state_sc)

        consts = [_ret_consts(lg_ref[hg * hps + u], u, rev) for u in range(hps)]
        order = list(reversed(range(cb))) if rev else list(range(cb))
        units = [(cc, u) for cc in order for u in range(hps)]

        def operands(cc, u):
            rows = pl.ds(cc * C, C)
            pair = slice(LANES * (u // 2), LANES * (u // 2 + 1))
            hm = consts[u][0]
            return q_ref[rows, pair] * hm, k_ref[rows, pair] * hm, v_ref[rows, LANES * u:LANES * (u + 1)].astype(MXU)

        a, inc = {}, {}
        for cc, u in units:
            q, k, v = operands(cc, u)
            a[cc, u] = _dot(q.astype(MXU), k.astype(MXU), NT) * consts[u][1]
            inc[cc, u] = _dot((k * consts[u][4]).astype(MXU), v, TN)
        for u in range(hps):
            st = state_sc[u]
            for cc in order:
                st_ref[u, cc] = st
                st = st * consts[u][5] + inc[cc, u]
            state_sc[u] = st
        for cc, u in units:
            q, _, v = operands(cc, u)
            cross = _dot((q * consts[u][3]).astype(MXU), st_ref[u, cc].astype(MXU), NN)
            o_ref[pl.ds(cc * C, C), LANES * u:LANES * (u + 1)] = _dot(a[cc, u].astype(MXU), v, NN) + cross

    qk_spec = pl.BlockSpec((TB, LANES * hps // 2), lambda h, g: (blk(g), h))
    return pl.pallas_call(
        body, name=name, grid=(HEADS // hps, nb),
        in_specs=[pl.BlockSpec(memory_space=pltpu.SMEM), qk_spec, qk_spec,
                  pl.BlockSpec((TB, LANES * hps), lambda h, g: (blk(g), P_VR // (LANES * hps) + h))],
        out_specs=[pl.BlockSpec((TB, LANES * hps), lambda h, g: (blk(g), h)),
                   pl.BlockSpec((hps, cb, LANES, LANES), lambda h, g: (h, blk(g), 0, 0))],
        out_shape=[jax.ShapeDtypeStruct((S, HEADS * LANES), F32), jax.ShapeDtypeStruct((HEADS, S // C, LANES, LANES), F32)],
        scratch_shapes=[pltpu.VMEM((hps, LANES, LANES), F32)],
        compiler_params=pltpu.CompilerParams(dimension_semantics=("parallel", "arbitrary"), vmem_limit_bytes=VMEM_LIMIT),
    )(lg, qt, kt, proj)


def _ret_bwd(name, qt, kt, proj, dret, states, lg, rev):
    S = qt.shape[0]
    C = CHUNK
    TB = _pick(S, (512, 256, 128))
    cb, nb = TB // C, S // TB
    hps = RET_HEADS_PER_STEP
    blk = (lambda g: g) if rev else (lambda g: nb - 1 - g)

    def body(lg_ref, q_ref, k_ref, v_ref, do_ref, st_ref, dq_ref, dk_ref, dv_ref, dlg_ref, ds_sc, acc_cc, acc_q, acc_k, acc_s):
        hg, g = pl.program_id(0), pl.program_id(1)

        @pl.when(g == 0)
        def _():
            ds_sc[...] = jnp.zeros_like(ds_sc)
            acc_cc[...] = jnp.zeros_like(acc_cc)
            acc_q[...] = jnp.zeros_like(acc_q)
            acc_k[...] = jnp.zeros_like(acc_k)
            acc_s[...] = jnp.zeros_like(acc_s)

        lgs = [lg_ref[hg * hps + u] for u in range(hps)]
        consts = [_ret_consts(lgs[u], u, rev) for u in range(hps)]
        order = list(range(cb)) if rev else list(reversed(range(cb)))
        units = [(cc, u) for cc in order for u in range(hps)]

        def operands(cc, u):
            rows = pl.ds(cc * C, C)
            pair = slice(LANES * (u // 2), LANES * (u // 2 + 1))
            head = slice(LANES * u, LANES * (u + 1))
            hm = consts[u][0]
            return q_ref[rows, pair] * hm, k_ref[rows, pair] * hm, v_ref[rows, head].astype(MXU), do_ref[rows, head].astype(MXU)

        a, dp, dqs, inc = {}, {}, {}, {}
        for cc, u in units:
            q, k, vb, dob = operands(cc, u)
            a[cc, u] = _dot(q.astype(MXU), k.astype(MXU), NT)
            dp[cc, u] = _dot(dob, vb, NT)
            dqs[cc, u] = _dot(dob, st_ref[u, cc].astype(MXU), NT)
            inc[cc, u] = _dot((q * consts[u][3]).astype(MXU), dob, TN)
        dsn = {}
        for u in range(hps):
            ds = ds_sc[u]
            for cc in order:
                dsn[cc, u] = ds
                ds = ds * consts[u][5] + inc[cc, u]
            ds_sc[u] = ds
        for cc, u in units:
            hm, din, dpos, qd, kd, cd, eq, ek = consts[u]
            rows, head = pl.ds(cc * C, C), slice(LANES * u, LANES * (u + 1))
            q, k, vb, dob = operands(cc, u)
            qb, kb = q.astype(MXU), k.astype(MXU)
            dsnb = dsn[cc, u].astype(MXU)
            da = (dp[cc, u] * din).astype(MXU)
            vds = _dot(vb, dsnb, NT)
            dq_ref[rows, head] = (_dot(da, kb, NN) + dqs[cc, u] * qd) * hm
            dk_ref[rows, head] = (_dot(da, qb, TN) + vds * kd) * hm
            dv_ref[rows, head] = _dot((a[cc, u] * din).astype(MXU), dob, TN) + _dot((k * kd).astype(MXU), dsnb, NN)
            acc_cc[u] += dp[cc, u] * a[cc, u] * din * dpos
            acc_q[u] += dqs[cc, u] * q * (qd * eq)
            acc_k[u] += vds * k * (kd * ek)
            acc_s[u] += dsn[cc, u] * st_ref[u, cc] * (cd * float(C))

        @pl.when(g == nb - 1)
        def _():
            for u in range(hps):
                tot = (jnp.sum(acc_cc[u], keepdims=True) + jnp.sum(acc_q[u], keepdims=True)
                       + jnp.sum(acc_k[u], keepdims=True) + jnp.sum(acc_s[u], keepdims=True))
                dlg_ref[u] = jnp.broadcast_to(tot * lgs[u], (8, LANES))

    full = jax.ShapeDtypeStruct((S, HEADS * LANES), F32)
    hspec = pl.BlockSpec((TB, LANES * hps), lambda h, g: (blk(g), h))
    qk_spec = pl.BlockSpec((TB, LANES * hps // 2), lambda h, g: (blk(g), h))
    return pl.pallas_call(
        body, name=name, grid=(HEADS // hps, nb),
        in_specs=[pl.BlockSpec(memory_space=pltpu.SMEM), qk_spec, qk_spec,
                  pl.BlockSpec((TB, LANES * hps), lambda h, g: (blk(g), P_VR // (LANES * hps) + h)),
                  hspec,
                  pl.BlockSpec((hps, cb, LANES, LANES), lambda h, g: (h, blk(g), 0, 0))],
        out_specs=[hspec, hspec, hspec, pl.BlockSpec((hps, 8, LANES), lambda h, g: (h, 0, 0))],
        out_shape=[full, full, full, jax.ShapeDtypeStruct((HEADS, 8, LANES), F32)],
        scratch_shapes=[pltpu.VMEM((hps, LANES, LANES), F32), pltpu.VMEM((hps, C, C), F32), pltpu.VMEM((hps, C, LANES), F32),
                        pltpu.VMEM((hps, C, LANES), F32), pltpu.VMEM((hps, LANES, LANES), F32)],
        compiler_params=pltpu.CompilerParams(dimension_semantics=("parallel", "arbitrary"), vmem_limit_bytes=VMEM_LIMIT),
    )(lg, qt, kt, proj, dret, states)


def _rope_consts():
    inv16 = THETA ** (-jnp.arange(16, dtype=F32) / 16)
    inv32 = THETA ** (-jnp.arange(32, dtype=F32) / 32)
    lane = np.arange(LANES)
    z48 = jnp.zeros((48,), F32)
    inv_m = jnp.concatenate([inv16, z48, inv16, z48])[None, :]
    sgn_m = jnp.asarray(np.where(lane < 16, -1.0, np.where((lane >= 64) & (lane < 80), 1.0, 0.0)), F32)[None, :]
    inv_r = jnp.concatenate([inv32] * 4)[None, :]
    sgn_r = jnp.asarray(np.where(lane < 64, -1.0, 1.0), F32)[None, :]
    return inv_m, sgn_m, inv_r, sgn_r


def _local_step(x, pos, tgt, gains, W):
    S = x.shape[0]
    ts = _pick(S, (256, 128))
    ts_wide = _pick(S, (128,))
    R = lambda a, w=None, c=0: (a, ((a.shape[1] if w is None else w), c))
    W_ = lambda a: (a, None)

    win = _win_pad(W["w_in"])
    wq = _wq_pad(W["w_q_b"])
    wk, wv = _wkv_pad(W["w_kv_b"])
    wmla = _wmla_pad(W["w_mla_out"])
    wret, wout, wgu, wdown = W["w_ret_out"], W["w_out"], W["w_gate_up"], W["w_down"]
    gqn, gkn = _qk_pad(gains["g_qn"]), _qk_pad(gains["g_kn"])
    g_mix, g_q_a, g_kv_a, g_ffn = gains["g_mix"], gains["g_q_a"], gains["g_kv_a"], gains["g_ffn"]
    lg_f = -jnp.exp(gains["ret_decay_fwd"][0])
    lg_b = -jnp.exp(gains["ret_decay_bwd"][0])

    consts = list(_rope_consts())
    cosm, sinm, cosr, sinr = _rowwise("rope_tables", _tables_fn, S, ts, [R(pos)] + [W_(c) for c in consts],
                                      [(LANES, F32, LANES, 0)] * 4)

    (h,) = _rowwise("rms_mix", _rmsg_fn, S, ts, [R(x), W_(g_mix)], [(D_MODEL, MXU, D_MODEL, 0)])
    proj = _mm("in_proj", h, win, "nn")
    seg = lambda off, w: (proj, (w, off // w))
    mla_ins = [seg(P_CQ, 256), seg(P_CKV, 128), seg(P_KROPE, 128), R(cosm), R(sinm),
               W_(g_q_a), W_(g_kv_a), W_(gqn), W_(gkn), W_(wq), W_(wk), W_(wv)]
    q, k, v = _rowwise("mla_prep", _mla_prep_fn, S, ts, mla_ins, [(HEADS * LANES, MXU, HEADS * LANES, 0)] * 3)
    o, o_bf, lse = _flash_fwd(q, k, v)
    y_a = _mm("mla_out", o_bf, wmla, "nn")

    ret_ins = [seg(P_QR, 512), seg(P_KR, 512), R(cosr), R(sinr)]
    qt, kt = _rowwise("ret_prep", _ret_prep_fn, S, ts, ret_ins, [(512, F32, 512, 0)] * 2)
    ret_f, st_f = _ret_fwd("ret_fwd_f", qt, kt, proj, lg_f, False)
    ret_b, st_b = _ret_fwd("ret_fwd_b", qt, kt, proj, lg_b, True)
    post_ins = [R(ret_f), R(ret_b), seg(P_GR, 1024)]
    (o_b,) = _rowwise("ret_post", _ret_post_fn, S, ts, post_ins, [(1024, MXU, 1024, 0)])
    y_b = _mm("ret_out", o_b, wret, "nn")

    merge_ins = [seg(P_GATES, 1024), (proj, (1024, 1)), R(y_a), R(y_b)]
    (merged,) = _rowwise("merge", _merge_fn, S, ts, merge_ins, [(D_MODEL, MXU, D_MODEL, 0)])
    x1 = _mm("out_proj", merged, wout, "nn", add=x)
    (h2,) = _rowwise("rms_ffn", _rmsg_fn, S, ts, [R(x1), W_(g_ffn)], [(D_MODEL, MXU, D_MODEL, 0)])
    gu = _mm("gate_up", h2, wgu, "nn")
    (act,) = _rowwise("swiglu", lambda t: _swiglu_fn(t[:, :FFN], t[:, FFN:]), S, ts_wide, [R(gu)], [(FFN, MXU, FFN, 0)])
    x2 = _mm("down_proj", act, wdown, "nn", add=x1)
    dx2, dx2_bf, loss_rows = _rowwise("loss", lambda a, b: (lambda d, l: (d, d, l))(*_loss_fn(a, b)), S, ts, [R(x2), R(tgt)],
                                      [(D_MODEL, F32, D_MODEL, 0), (D_MODEL, MXU, D_MODEL, 0)], accs=[(1, D_MODEL)])

    gW = {}
    gW["w_down"] = _mm("d_w_down", act, dx2_bf, "tn")
    dact = _mm("d_act", dx2_bf, wdown, "nt")

    def glu_bwd(t, da):
        _, vjp = jax.vjp(_swiglu_fn, t[:, :FFN], t[:, FFN:])
        return jnp.concatenate(vjp(da), axis=1)

    (dgu,) = _rowwise("swiglu_bwd", glu_bwd, S, ts_wide, [R(gu), R(dact)], [(2 * FFN, MXU, 2 * FFN, 0)])
    gW["w_gate_up"] = _mm("d_w_gate_up", h2, dgu, "tn")
    dh2 = _mm("d_h2", dgu, wgu, "nt")

    def rms_bwd(xx, g, dh, dres):
        _, vjp = jax.vjp(_rmsg_fn, xx, g)
        dx, dg = vjp(dh)
        dx = dx + dres
        return dx, dx, dg

    dx1, dx1_bf, dg_ffn = _rowwise("rms_ffn_bwd", rms_bwd, S, ts, [R(x1), W_(g_ffn), R(dh2), R(dx2)],
                                   [(D_MODEL, F32, D_MODEL, 0), (D_MODEL, MXU, D_MODEL, 0)], accs=[(1, D_MODEL)])
    gW["w_out"] = _mm("d_w_out", merged, dx1_bf, "tn")
    dmerged = _mm("d_merged", dx1_bf, wout, "nt")

    def merge_bwd(ga, gb, ya, yb, dm):
        _, vjp = jax.vjp(_merge_fn, ga, gb, ya, yb)
        return vjp(dm)

    dga, dgb, dy_a, dy_b = _rowwise("merge_bwd", merge_bwd, S, ts, merge_ins + [R(dmerged)], [(D_MODEL, MXU, D_MODEL, 0)] * 4)
    gW["w_ret_out"] = _mm("d_w_ret_out", o_b, dy_b, "tn")
    do_b = _mm("d_o_b", dy_b, wret, "nt")

    def post_bwd(rf, rb, gr, dob):
        _, vjp = jax.vjp(_ret_post_fn, rf, rb, gr)
        drf, _, dgr = vjp(dob)
        return drf, dgr

    dret, dg_r = _rowwise("ret_post_bwd", post_bwd, S, ts, post_ins + [R(do_b)], [(1024, F32, 1024, 0), (1024, MXU, 1024, 0)])
    dq_f, dk_f, dv_f, dlg_f = _ret_bwd("ret_bwd_f", qt, kt, proj, dret, st_f, lg_f, False)
    dq_b, dk_b, dv_b, dlg_b = _ret_bwd("ret_bwd_b", qt, kt, proj, dret, st_b, lg_b, True)

    def ret_prep_bwd(qr, kr, cosr_, sinr_, dqf, dqb, dkf, dkb, dvf, dvb):
        _, vjp = jax.vjp(lambda a, b: _ret_prep_fn(a, b, cosr_, sinr_), qr, kr)
        pair = lambda t: jnp.concatenate([t[:, 256 * j:256 * j + 128] + t[:, 256 * j + 128:256 * j + 256] for j in range(4)], axis=1)
        dqr, dkr = vjp((pair(dqf + dqb), pair(dkf + dkb)))
        return dqr, dkr, dvf + dvb

    dq_r, dk_r, dv_r = _rowwise("ret_prep_bwd", ret_prep_bwd, S, ts, ret_ins + [R(t) for t in (dq_f, dq_b, dk_f, dk_b, dv_f, dv_b)],
                                [(512, MXU, 512, 0), (512, MXU, 512, 0), (1024, MXU, 1024, 0)])

    gW_mla_p = _mm("d_w_mla_out", o_bf, dy_a, "tn")
    do = _mm("d_o", dy_a, wmla, "nt")
    do_bf, delta = _rowwise("attn_delta", _delta_fn, S, ts, [R(o), R(do)], [(HEADS * LANES, MXU, HEADS * LANES, 0), (HEADS * LANES, F32, HEADS * LANES, 0)])
    dq, dk, dv = _flash_bwd(q, k, v, do_bf, lse, delta)

    def mla_prep_bwd(cq, ckv, kr, cosm_, sinm_, gqa, gkva, gqn_, gkn_, wq_, wk_, wv_, dq_, dk_, dv_):
        f = lambda cq, ckv, kr, gqa, gkva, gqn_, gkn_, wq_, wk_, wv_: _mla_prep_fn(cq, ckv, kr, cosm_, sinm_, gqa, gkva, gqn_, gkn_, wq_, wk_, wv_)
        _, vjp = jax.vjp(f, cq, ckv, kr, gqa, gkva, gqn_, gkn_, wq_.astype(F32), wk_.astype(F32), wv_.astype(F32))
        return vjp((dq_, dk_, dv_))

    mb = _rowwise("mla_prep_bwd", mla_prep_bwd, S, ts, mla_ins + [R(dq), R(dk), R(dv)],
                  [(256, MXU, 256, 0), (128, MXU, 128, 0), (128, MXU, 128, 0)],
                  accs=[(1, 256), (1, 128), (1, LANES), (1, LANES), (256, HEADS * LANES), (128, HEADS * LANES), (128, HEADS * LANES)])
    dc_q, dc_kv, dk_rope, dg_q_a, dg_kv_a, dgqn_p, dgkn_p, dwq_p, dwk_p, dwv_p = mb

    dproj = jnp.concatenate([dga, dgb, dv_r, dg_r, dq_r, dk_r, dc_q, dc_kv, dk_rope], axis=1)
    gwin_p = _mm("d_w_in", h, dproj, "tn")
    dh = _mm("d_h", dproj, win, "nt")
    grad_x, _, dg_mix = _rowwise("rms_mix_bwd", rms_bwd, S, ts, [R(x), W_(g_mix), R(dh), R(dx1)],
                                 [(D_MODEL, F32, D_MODEL, 0), (D_MODEL, MXU, D_MODEL, 0)], accs=[(1, D_MODEL)])

    gW["w_in"] = _win_unpad(gwin_p)
    gW["w_q_b"] = _wq_unpad(dwq_p)
    gW["w_kv_b"] = _wkv_unpad(dwk_p, dwv_p)
    gW["w_mla_out"] = _wmla_unpad(gW_mla_p)
    gG = {"g_mix": dg_mix, "g_q_a": dg_q_a, "g_kv_a": dg_kv_a, "g_qn": _qk_unpad(dgqn_p),
          "g_kn": _qk_unpad(dgkn_p), "ret_decay_fwd": dlg_f[:, 0, 0][None, :], "ret_decay_bwd": dlg_b[:, 0, 0][None, :],
          "g_ffn": dg_ffn}
    return loss_rows, grad_x, gG, gW


MATS = [("w_in", (1024, 5536), 1), ("w_q_b", (256, 768), 1), ("w_kv_b", (128, 1024), 1), ("w_mla_out", (512, 1024), 1),
        ("w_ret_out", (1024, 1024), 0), ("w_out", (1024, 1024), 0), ("w_gate_up", (1024, 5632), 1), ("w_down", (2816, 1024), 0)]
GAINS = [("g_mix", 1024), ("g_q_a", 256), ("g_kv_a", 128), ("g_qn", 96), ("g_kn", 96), ("ret_decay_fwd", 8), ("ret_decay_bwd", 8),
         ("g_ffn", 1024)]
ORDER = ["g_mix", "w_in", "g_q_a", "w_q_b", "g_kv_a", "w_kv_b", "g_qn", "g_kn", "w_mla_out", "ret_decay_fwd", "ret_decay_bwd",
         "w_ret_out", "w_out", "g_ffn", "w_gate_up", "w_down"]
GAIN_LEN = sum(n for _, n in GAINS)
GAIN_PAD = -(-GAIN_LEN // LANES) * LANES


def _pack_gains(d):
    row = jnp.concatenate([d[n].reshape(1, ln).astype(F32) for n, ln in GAINS], axis=1)
    return jnp.pad(row, ((0, 0), (0, GAIN_PAD - GAIN_LEN)))


def _unpack_gains(row):
    out, off = {}, 0
    for n, ln in GAINS:
        out[n] = row[0, off:off + ln]
        off += ln
    return out


def _unshard(pieces, axis):
    if axis == 0:
        return pieces.reshape((N_DEV * pieces.shape[1], pieces.shape[2]))
    return jnp.concatenate([pieces[p] for p in range(N_DEV)], axis=1)


def _reshard(full, axis):
    if axis == 0:
        return full.reshape((N_DEV, full.shape[0] // N_DEV, full.shape[1]))
    c = full.shape[1] // N_DEV
    return jnp.stack([full[:, c * p:c * (p + 1)] for p in range(N_DEV)])


def _all_gather(shards):
    n = len(shards)

    def body(*refs):
        x_refs, out_refs = refs[:n], refs[n:2 * n]
        send_sems, recv_sems, local_sems = refs[2 * n:]
        x, y, c = lax.axis_index("x"), lax.axis_index("y"), lax.axis_index("c")
        me, sibling = (x, y, c), (x, y, 1 - c)
        chips = [(1 - x, y), (x, 1 - y), (1 - x, 1 - y)]

        def slot(a, px, py, pc):
            return out_refs[a].at[4 * px + 2 * py + pc]

        def copy(a, k, block, to, from_input=False):
            return pltpu.make_async_remote_copy(
                src_ref=x_refs[a] if from_input else slot(a, *block), dst_ref=slot(a, *block),
                send_sem=send_sems.at[a, k], recv_sem=recv_sems.at[a, k], device_id=to, device_id_type=pl.DeviceIdType.MESH)

        mine = [pltpu.make_async_copy(x_refs[a], slot(a, *me), local_sems.at[a]) for a in range(n)]
        first = [copy(a, 0, me, sibling, True) for a in range(n)]
        first += [copy(a, 1 + j, me, (*chip, c), True) for j, chip in enumerate(chips) for a in range(n)]
        for cp in mine + first:
            cp.start()
        passed = []
        for j, chip in enumerate(chips):
            for a in range(n):
                copy(a, 1 + j, (*chip, c), me).wait_recv()
                passed.append(copy(a, 4 + j, (*chip, c), sibling))
                passed[-1].start()
        for a in range(n):
            copy(a, 0, sibling, me).wait_recv()
        for j, chip in enumerate(chips):
            for a in range(n):
                copy(a, 4 + j, (*chip, 1 - c), me).wait_recv()
        for cp in first + passed:
            cp.wait_send()
        for cp in mine:
            cp.wait()

    any_spec = pl.BlockSpec(memory_space=pl.ANY)
    return pl.pallas_call(
        body, name="all_gather_weights", out_shape=[jax.ShapeDtypeStruct((N_DEV,) + s.shape, s.dtype) for s in shards],
        in_specs=[any_spec] * n, out_specs=[any_spec] * n,
        scratch_shapes=[pltpu.SemaphoreType.DMA((n, 7)), pltpu.SemaphoreType.DMA((n, 7)), pltpu.SemaphoreType.DMA((n,))],
    )(*shards)


def _all_to_all(pieces):
    n = len(pieces)

    def body(*refs):
        in_refs, out_refs = refs[:n], refs[n:2 * n]
        send_sems, recv_sems, local_sems = refs[2 * n:]
        x, y, c = lax.axis_index("x"), lax.axis_index("y"), lax.axis_index("c")
        my_id = 4 * x + 2 * y + c
        flips = [(fx, fy, fc) for fx in (0, 1) for fy in (0, 1) for fc in (0, 1)][1:]

        def copy(a, kk, f):
            p = (x ^ f[0], y ^ f[1], c ^ f[2])
            return pltpu.make_async_remote_copy(
                src_ref=in_refs[a].at[4 * p[0] + 2 * p[1] + p[2]], dst_ref=out_refs[a].at[my_id],
                send_sem=send_sems.at[a, kk], recv_sem=recv_sems.at[a, kk], device_id=p, device_id_type=pl.DeviceIdType.MESH)

        mine = [pltpu.make_async_copy(in_refs[a].at[my_id], out_refs[a].at[my_id], local_sems.at[a]) for a in range(n)]
        copies = [copy(a, kk, f) for kk, f in enumerate(flips) for a in range(n)]
        for cp in mine + copies:
            cp.start()
        for cp in copies:
            cp.wait_recv()
        for cp in copies:
            cp.wait_send()
        for cp in mine:
            cp.wait()

    any_spec = pl.BlockSpec(memory_space=pl.ANY)
    return pl.pallas_call(
        body, name="all_to_all_grads", out_shape=[jax.ShapeDtypeStruct(p.shape, p.dtype) for p in pieces],
        in_specs=[any_spec] * n, out_specs=[any_spec] * n,
        scratch_shapes=[pltpu.SemaphoreType.DMA((n, 7)), pltpu.SemaphoreType.DMA((n, 7)), pltpu.SemaphoreType.DMA((n,))],
    )(*pieces)


def _adamw(name, parts, w, m, v):
    rows, cols = w.shape
    tr = _pick(rows, (128, 64, 32, 16, 8))
    pspec = pl.BlockSpec((N_DEV, tr, cols), lambda i: (0, i, 0))
    rspec = pl.BlockSpec((tr, cols), lambda i: (i, 0))

    def body(p_ref, w_ref, m_ref, v_ref, g_ref, d_ref, m2_ref, v2_ref):
        g, d, m2, v2 = _adamw_fn([p_ref[s] for s in range(N_DEV)], w_ref[...], m_ref[...], v_ref[...])
        g_ref[...], d_ref[...], m2_ref[...], v2_ref[...] = g, d, m2, v2

    return pl.pallas_call(
        body, name=name, grid=(rows // tr,), in_specs=[pspec, rspec, rspec, rspec], out_specs=[rspec] * 4,
        out_shape=[jax.ShapeDtypeStruct((rows, cols), F32)] * 4,
        compiler_params=pltpu.CompilerParams(dimension_semantics=("parallel",), vmem_limit_bytes=VMEM_LIMIT),
    )(parts, w, m, v)


def kernel(x, positions, g_mix, w_in, g_q_a, w_q_b, g_kv_a, w_kv_b, g_qn, g_kn, w_mla_out, ret_decay_fwd, ret_decay_bwd, w_ret_out, w_out, g_ffn, w_gate_up, w_down, loss_target, m_g_mix, m_w_in, m_g_q_a, m_w_q_b, m_g_kv_a, m_w_kv_b, m_g_qn, m_g_kn, m_w_mla_out, m_ret_decay_fwd, m_ret_decay_bwd, m_w_ret_out, m_w_out, m_g_ffn, m_w_gate_up, m_w_down, v_g_mix, v_w_in, v_g_q_a, v_w_q_b, v_g_kv_a, v_w_kv_b, v_g_qn, v_g_kn, v_w_mla_out, v_ret_decay_fwd, v_ret_decay_bwd, v_w_ret_out, v_w_out, v_g_ffn, v_w_gate_up, v_w_down):
    w = dict(g_mix=g_mix, w_in=w_in, g_q_a=g_q_a, w_q_b=w_q_b, g_kv_a=g_kv_a, w_kv_b=w_kv_b, g_qn=g_qn, g_kn=g_kn, w_mla_out=w_mla_out,
             ret_decay_fwd=ret_decay_fwd, ret_decay_bwd=ret_decay_bwd, w_ret_out=w_ret_out, w_out=w_out, g_ffn=g_ffn,
             w_gate_up=w_gate_up, w_down=w_down)
    m = dict(g_mix=m_g_mix, w_in=m_w_in, g_q_a=m_g_q_a, w_q_b=m_w_q_b, g_kv_a=m_g_kv_a, w_kv_b=m_w_kv_b, g_qn=m_g_qn, g_kn=m_g_kn,
             w_mla_out=m_w_mla_out, ret_decay_fwd=m_ret_decay_fwd, ret_decay_bwd=m_ret_decay_bwd, w_ret_out=m_w_ret_out, w_out=m_w_out,
             g_ffn=m_g_ffn, w_gate_up=m_w_gate_up, w_down=m_w_down)
    v = dict(g_mix=v_g_mix, w_in=v_w_in, g_q_a=v_g_q_a, w_q_b=v_w_q_b, g_kv_a=v_g_kv_a, w_kv_b=v_w_kv_b, g_qn=v_g_qn, g_kn=v_g_kn,
             w_mla_out=v_w_mla_out, ret_decay_fwd=v_ret_decay_fwd, ret_decay_bwd=v_ret_decay_bwd, w_ret_out=v_w_ret_out, w_out=v_w_out,
             g_ffn=v_g_ffn, w_gate_up=v_w_gate_up, w_down=v_w_down)
    gains = {n: w[n].reshape(1, ln) for n, ln in GAINS}

    gathered = _all_gather([w[n].astype(WIRE) for n, _, _ in MATS])
    W = {n: _unshard(g, axis) for (n, _, axis), g in zip(MATS, gathered)}
    S = x.shape[1]
    pos = positions.reshape(S, 1).astype(F32)
    loss_rows, grad_x, gG, gW = _local_step(x.reshape(S, D_MODEL), pos, loss_target.reshape(S, D_MODEL), gains, W)
    loss = lax.psum(jnp.sum(loss_rows), ("x", "y", "c"))

    pieces = [_reshard(gW[n], axis).astype(GWIRE) for n, _, axis in MATS]
    pieces.append(jnp.broadcast_to(_pack_gains(gG)[None], (N_DEV, 1, GAIN_PAD)))
    parts = _all_to_all(pieces)
    out = [dict() for _ in range(4)]
    for (n, _, _), p in zip(MATS, parts):
        for o, r in zip(out, _adamw("adamw_" + n, p, w[n], m[n], v[n])):
            o[n] = r
    for o, r in zip(out, _adamw("adamw_gains", parts[-1], _pack_gains(w), _pack_gains(m), _pack_gains(v))):
        o.update(_unpack_gains(r))
    return (loss, grad_x.reshape(x.shape), *[o[n] for o in out for n in ORDER])
```

```python
import functools

import numpy as np
import jax
import jax.numpy as jnp
from jax import lax
from jax.experimental import pallas as pl
from jax.experimental.pallas import tpu as pltpu

F32 = jnp.float32
MXU = jnp.bfloat16
WIRE = jnp.bfloat16
GWIRE = jnp.bfloat16

N_DEV = 8
D_MODEL = 1024
HEADS = 8
LANES = 128
Q_RANK, KV_RANK = 256, 128
NOPE, ROPE_M, V_M = 64, 32, 64
QK_M = NOPE + ROPE_M
RQK, RV = 64, 128
CHUNK = 128
FFN = 2816
IN_WIDTH = 5536
THETA = 10000.0
EPS = 1e-6
LR, B1, B2, AEPS, WD, STEP = 0.001, 0.9, 0.999, 1e-08, 0.01, 10
VMEM_LIMIT = 56 * 1024 * 1024

NN = ((1,), (0,))
NT = ((1,), (1,))
TN = ((0,), (0,))

P_GATES, P_VR, P_GR, P_QR, P_KR, P_CQ, P_CKV, P_KROPE, P_WIDTH = 0, 2048, 3072, 4096, 4608, 5120, 5376, 5504, 5632
O_CQ, O_CKV, O_KROPE, O_QR, O_KR, O_VR, O_GR, O_GATES = 0, 256, 384, 416, 928, 1440, 2464, 3488


def _dot(a, b, dims):
    return lax.dot_general(a, b, (dims, ((), ())), preferred_element_type=F32)


def _pick(dim, cands):
    for c in cands:
        if dim % c == 0:
            return c
    return dim


def _pairs(t):
    return t.reshape(t.shape[0], 4, 2, 2, 32).transpose(0, 1, 3, 2, 4).reshape(t.shape[0], 512)


def _win_pad(w):
    z = jnp.zeros((w.shape[0], 48), w.dtype)
    kr = w[:, O_KROPE:O_KROPE + 32]
    return jnp.concatenate([w[:, O_GATES:], w[:, O_VR:O_VR + 1024], w[:, O_GR:O_GR + 1024], _pairs(w[:, O_QR:O_QR + 512]),
                            _pairs(w[:, O_KR:O_KR + 512]), w[:, :O_CKV], w[:, O_CKV:O_KROPE], kr[:, :16], z, kr[:, 16:], z], axis=1)


def _win_unpad(g):
    return jnp.concatenate([g[:, P_CQ:P_CQ + 256], g[:, P_CKV:P_CKV + 128], g[:, P_KROPE:P_KROPE + 16], g[:, P_KROPE + 64:P_KROPE + 80],
                            _pairs(g[:, P_QR:P_QR + 512]), _pairs(g[:, P_KR:P_KR + 512]), g[:, P_VR:P_VR + 1024],
                            g[:, P_GR:P_GR + 1024], g[:, P_GATES:P_GATES + 2048]], axis=1)


def _qk_pad(t):
    z = jnp.zeros(t.shape[:-1] + (32,), t.dtype)
    return jnp.concatenate([t[..., 64:80], t[..., 0:48], t[..., 80:96], t[..., 48:64], z], axis=-1)


def _qk_unpad(p):
    return jnp.concatenate([p[..., 16:64], p[..., 80:96], p[..., 0:16], p[..., 64:80]], axis=-1)


def _wq_pad(w):
    return _qk_pad(w.reshape(Q_RANK, HEADS, QK_M)).reshape(Q_RANK, HEADS * LANES)


def _wq_unpad(g):
    return _qk_unpad(g.reshape(Q_RANK, HEADS, LANES)).reshape(Q_RANK, HEADS * QK_M)


def _wkv_pad(w):
    t = w.reshape(KV_RANK, HEADS, NOPE + V_M)
    z = lambda n: jnp.zeros((KV_RANK, HEADS, n), w.dtype)
    wk = jnp.concatenate([z(16), t[..., 0:48], z(16), t[..., 48:64], z(32)], axis=-1)
    wv = jnp.concatenate([t[..., 64:128], z(64)], axis=-1)
    return wk.reshape(KV_RANK, HEADS * LANES), wv.reshape(KV_RANK, HEADS * LANES)


def _wkv_unpad(dwk, dwv):
    k, v = dwk.reshape(KV_RANK, HEADS, LANES), dwv.reshape(KV_RANK, HEADS, LANES)
    return jnp.concatenate([k[..., 16:64], k[..., 80:96], v[..., 0:64]], axis=-1).reshape(KV_RANK, HEADS * (NOPE + V_M))


def _wmla_pad(w):
    t = w.reshape(HEADS, V_M, D_MODEL)
    return jnp.concatenate([t, jnp.zeros_like(t)], axis=1).reshape(HEADS * LANES, D_MODEL)


def _wmla_unpad(g):
    return g.reshape(HEADS, LANES, D_MODEL)[:, :V_M].reshape(HEADS * V_M, D_MODEL)


def _rowwise(name, fn, rows, ts, ins, outs, accs=(), ncol=1):
    n_in, n_out, n_acc = len(ins), len(outs), len(accs)

    def colmap(col):
        if callable(col):
            return lambda i, j: (i, col(j))
        return lambda i, j: (i, col)

    arrays, in_specs = [], []
    for arr, spec in ins:
        arrays.append(arr)
        if spec is None:
            in_specs.append(pl.BlockSpec(arr.shape, functools.partial(lambda i, j, nd: (0,) * nd, nd=arr.ndim)))
        else:
            in_specs.append(pl.BlockSpec((ts, spec[0]), colmap(spec[1])))
    out_shape, out_specs = [], []
    for total, dtype, width, col in outs:
        out_shape.append(jax.ShapeDtypeStruct((rows, total), dtype))
        out_specs.append(pl.BlockSpec((ts, width), colmap(col)))
    for shp in accs:
        out_shape.append(jax.ShapeDtypeStruct(shp, F32))
        out_specs.append(pl.BlockSpec(shp, functools.partial(lambda i, j, nd: (0,) * nd, nd=len(shp))))

    def body(*refs):
        vals = [r[...] for r in refs[:n_in]]
        res = fn(*vals)
        if not isinstance(res, (tuple, list)):
            res = (res,)
        for r, v in zip(refs[n_in:n_in + n_out], res[:n_out]):
            r[...] = v.astype(r.dtype)
        if n_acc:
            first = jnp.logical_and(pl.program_id(0) == 0, pl.program_id(1) == 0)
            for r, v in zip(refs[n_in + n_out:], res[n_out:]):
                @pl.when(first)
                def _(r=r):
                    r[...] = jnp.zeros_like(r)
                r[...] += v.astype(F32)

    res = pl.pallas_call(
        body, name=name, grid=(rows // ts, ncol), in_specs=in_specs, out_specs=out_specs, out_shape=out_shape,
        compiler_params=pltpu.CompilerParams(dimension_semantics=("arbitrary", "arbitrary"), vmem_limit_bytes=VMEM_LIMIT),
    )(*arrays)
    return res


MM_OPERAND_BYTES = 24 * 1024 * 1024


def _mm(name, a, b, mode, add=None):
    if mode == "nn":
        (M, K), N = a.shape, b.shape[1]
    elif mode == "nt":
        (M, K), N = a.shape, b.shape[0]
    else:
        (K, M), N = a.shape, b.shape[1]
    tm = _pick(M, (512, 256, 128)) if mode == "tn" else _pick(M, (1024, 512, 256, 128))
    tn = _pick(N, (512, 256, 128))
    fits = lambda t: 2 * (tm + tn) * t * a.dtype.itemsize <= MM_OPERAND_BYTES
    tk = next(t for t in (K, 4096, 2816, 2048, 1408, 1024, 512, 256, 128) if K % t == 0 and (fits(t) or t == 128))
    nk = K // tk
    dims = {"nn": NN, "nt": NT, "tn": TN}[mode]
    a_spec = pl.BlockSpec((tk, tm), lambda i, j, k: (k, i)) if mode == "tn" else pl.BlockSpec((tm, tk), lambda i, j, k: (i, k))
    b_spec = pl.BlockSpec((tn, tk), lambda i, j, k: (j, k)) if mode == "nt" else pl.BlockSpec((tk, tn), lambda i, j, k: (k, j))
    o_spec = pl.BlockSpec((tm, tn), lambda i, j, k: (i, j))
    has_add = add is not None

    def body(*refs):
        a_ref, b_ref, o_ref = refs[0], refs[1], refs[-1]
        d = _dot(a_ref[...], b_ref[...], dims)
        first = (d + refs[2][...]) if has_add else d
        if nk == 1:
            o_ref[...] = first
        else:
            k = pl.program_id(2)

            @pl.when(k == 0)
            def _():
                o_ref[...] = first

            @pl.when(k > 0)
            def _():
                o_ref[...] += d

    args = [a, b] + ([add] if has_add else [])
    specs = [a_spec, b_spec] + ([o_spec] if has_add else [])
    return pl.pallas_call(
        body, name=name, grid=(M // tm, N // tn, nk), in_specs=specs, out_specs=o_spec,
        out_shape=jax.ShapeDtypeStruct((M, N), F32),
        compiler_params=pltpu.CompilerParams(dimension_semantics=("parallel", "parallel", "arbitrary"), vmem_limit_bytes=VMEM_LIMIT),
    )(*args)


@jax.custom_vjp
def _swap64(x):
    return pltpu.roll(x, 64, 1)


_swap64.defvjp(lambda x: (_swap64(x), None), lambda _, g: (_swap64(g),))


@jax.custom_vjp
def _mxdot(a, b):
    return _dot(a.astype(MXU), b.astype(MXU), NN)


def _mxdot_bwd(res, g):
    a, b = res
    gb = g.astype(MXU)
    return _dot(gb, b.astype(MXU), NT), _dot(a.astype(MXU), gb, TN)


_mxdot.defvjp(lambda a, b: (_mxdot(a, b), (a, b)), _mxdot_bwd)


def _rms(x):
    return x * lax.rsqrt(jnp.mean(x * x, axis=-1, keepdims=True) + EPS)


def _rmsg_fn(x, g):
    return _rms(x) * g


def _silu(x):
    return x * jax.nn.sigmoid(x)


def _tables_fn(pos, inv_m, sgn_m, inv_r, sgn_r):
    am, ar = pos * inv_m, pos * inv_r
    return jnp.cos(am), jnp.sin(am) * sgn_m, jnp.cos(ar), jnp.sin(ar) * sgn_r


def _head_blocks(t):
    return [t[:, LANES * h:LANES * (h + 1)] for h in range(t.shape[1] // LANES)]


def _mla_prep_fn(cq, ckv, kr, cosm, sinm, gqa, gkva, gqn, gkn, wq, wk, wv):
    cqn = _rms(cq) * gqa
    ckvn = _rms(ckv) * gkva
    q_raw = _mxdot(cqn, wq)
    k_raw = _mxdot(ckvn, wk)
    lane = lax.broadcasted_iota(jnp.int32, (1, HEADS * LANES), 1)
    v = _mxdot(ckvn, wv) + (lane % LANES == V_M).astype(F32)

    def norm_rope(blocks, g, extra):
        outs = []
        for b in blocks:
            if extra is not None:
                b = b + extra
            n = b * lax.rsqrt(jnp.sum(b * b, axis=-1, keepdims=True) * (1.0 / QK_M) + EPS) * g
            outs.append(n * cosm + _swap64(n) * sinm)
        return jnp.concatenate(outs, axis=1)

    q = norm_rope(_head_blocks(q_raw), gqn, None)
    k = norm_rope(_head_blocks(k_raw), gkn, kr)
    return q, k, v


def _ret_prep_fn(qr, kr, cosr, sinr):
    def rope(t, scale):
        return jnp.concatenate([(b * cosr + _swap64(b) * sinr) * scale for b in _head_blocks(t)], axis=1)
    return rope(qr, 1.0), rope(kr, RQK ** -0.5)


def _ret_post_fn(rf, rb, gr):
    ret = rf + rb
    outs = []
    for b, g in zip(_head_blocks(ret), _head_blocks(gr)):
        outs.append(_silu(g) * _rms(b))
    return jnp.concatenate(outs, axis=1)


def _merge_fn(ga, gb, ya, yb):
    return jax.nn.sigmoid(ga) * ya + jax.nn.sigmoid(gb) * yb


def _swiglu_fn(gate, up):
    return _silu(gate) * up


def _loss_fn(x2, tgt):
    d = x2 - tgt
    return d * (1.0 / D_MODEL), 0.5 * jnp.sum(d * d, axis=0, keepdims=True) * (1.0 / D_MODEL)


def _adamw_fn(parts, w, m, v):
    g = parts[0].astype(F32)
    for p in range(1, N_DEV):
        g = g + parts[p].astype(F32)
    m2 = B1 * m + (1.0 - B1) * g
    v2 = B2 * v + (1.0 - B2) * jnp.square(g)
    m_hat = m2 / (1.0 - B1 ** STEP)
    v_hat = v2 / (1.0 - B2 ** STEP)
    delta = -LR * (m_hat / (jnp.sqrt(v_hat) + AEPS) + WD * w)
    return g, delta, m2, v2


SCALE = QK_M ** -0.5
LOG2E = 1.4426950408889634
FLASH_ROWS = 32


def _flash_fwd(q, k, v):
    S = q.shape[0]
    tq = tk = _pick(S, (512, 256, 128))
    ncb = tk // LANES
    nkv = S // tk
    assert nkv % 2 == 0, "kv tiles are processed in pairs"
    mrows = 64
    c = SCALE * LOG2E

    def body(q_ref, k_ref, v_ref, o_ref, obf_ref, lse_ref, s_a, p_a, s_b, p_b, m_sc, a_sc, acc_sc):
        m_sc[...] = jnp.full_like(m_sc, -jnp.inf)
        acc_sc[...] = jnp.zeros_like(acc_sc)
        qb = q_ref[...]

        def scores(j, s_buf):
            s_buf[...] = _dot(qb, k_ref[pl.ds(pl.multiple_of(j * tk, tk), tk), :], NT)

        def stage(j, s_buf, p_buf, s_next):
            scores(jnp.minimum(j + 1, nkv - 1), s_next)
            for r in range(tq // mrows):
                rows = slice(r * mrows, (r + 1) * mrows)
                cols = [s_buf[rows, LANES * cb:LANES * (cb + 1)] for cb in range(ncb)]
                m_prev = m_sc[rows, :]
                row_max = jnp.max(functools.reduce(jnp.maximum, cols), axis=-1, keepdims=True)
                m_new = jnp.maximum(m_prev, jnp.broadcast_to(row_max, (mrows, LANES)))
                a_sc[rows, :] = jnp.exp2((m_prev - m_new) * c)
                m_sc[rows, :] = m_new
                for cb in range(ncb):
                    p_buf[rows, LANES * cb:LANES * (cb + 1)] = jnp.exp2((cols[cb] - m_new) * c).astype(p_buf.dtype)
            acc_sc[...] = a_sc[...] * acc_sc[...] + _dot(p_buf[...], v_ref[pl.ds(pl.multiple_of(j * tk, tk), tk), :], NN)

        scores(0, s_a)

        def pair_step(t, carry):
            stage(2 * t, s_a, p_a, s_b)
            stage(2 * t + 1, s_b, p_b, s_a)
            return carry

        lax.fori_loop(0, nkv // 2, pair_step, 0, unroll=4)
        acc = acc_sc[...]
        lane = lax.broadcasted_iota(jnp.int32, (1, LANES), 1)
        l = jnp.sum(jnp.where(lane == V_M, acc, 0.0), axis=-1, keepdims=True)
        o = acc / l
        o_ref[...] = o
        obf_ref[...] = o.astype(obf_ref.dtype)
        lse_ref[...] = m_sc[...] * c + jnp.log2(jnp.broadcast_to(l, (tq, LANES)))

    qspec = pl.BlockSpec((tq, LANES), lambda h, i: (i, h))
    kspec = pl.BlockSpec((S, LANES), lambda h, i: (0, h))
    full = jax.ShapeDtypeStruct((S, HEADS * LANES), F32)
    return pl.pallas_call(
        body, name="flash_fwd", grid=(HEADS, S // tq), in_specs=[qspec, kspec, kspec], out_specs=[qspec, qspec, qspec],
        out_shape=[full, jax.ShapeDtypeStruct((S, HEADS * LANES), MXU), full],
        scratch_shapes=[pltpu.VMEM((tq, tk), F32), pltpu.VMEM((tq, tk), MXU)] * 2 + [pltpu.VMEM((tq, LANES), F32)] * 3,
        compiler_params=pltpu.CompilerParams(dimension_semantics=("parallel", "arbitrary"), vmem_limit_bytes=VMEM_LIMIT),
    )(q, k, v)


def _delta_fn(o, do):
    outs = [jnp.broadcast_to(jnp.sum(a * b, axis=-1, keepdims=True), a.shape) for a, b in zip(_head_blocks(o), _head_blocks(do))]
    return do, jnp.concatenate(outs, axis=1)


def _flash_bwd(q, k, v, do, lse, delta):
    S = q.shape[0]
    tq = tk = _pick(S, (512, 256, 128))
    ncb = tk // LANES
    c = SCALE * LOG2E

    nq = S // tq
    assert nq % 2 == 0, "q tiles are processed in pairs"

    def body(q_ref, k_ref, v_ref, do_ref, lse_ref, dl_ref, dq_ref, dk_ref, dv_ref, s_a, dp_a, p_a, ds_a, s_b, dp_b, p_b, ds_b, dk_sc, dv_sc):
        @pl.when(pl.program_id(1) == 0)
        def _():
            dq_ref[...] = jnp.zeros_like(dq_ref)

        dk_sc[...] = jnp.zeros_like(dk_sc)
        dv_sc[...] = jnp.zeros_like(dv_sc)
        kb, vb = k_ref[...], v_ref[...]

        def scores(i, s_buf, dp_buf):
            q_rows = pl.ds(pl.multiple_of(i * tq, tq), tq)
            s_buf[...] = _dot(q_ref[q_rows, :], kb, NT)
            dp_buf[...] = _dot(do_ref[q_rows, :], vb, NT)

        def stage(i, s_buf, dp_buf, p_buf, ds_buf, s_next, dp_next):
            scores(jnp.minimum(i + 1, nq - 1), s_next, dp_next)
            for r in range(tq // FLASH_ROWS):
                rows = slice(r * FLASH_ROWS, (r + 1) * FLASH_ROWS)
                grows = pl.ds(pl.multiple_of(i * tq + r * FLASH_ROWS, FLASH_ROWS), FLASH_ROWS)
                lse_b, dl_b = lse_ref[grows, :], dl_ref[grows, :]
                for cb in range(ncb):
                    sl = slice(LANES * cb, LANES * (cb + 1))
                    p = jnp.exp2(s_buf[rows, sl] * c - lse_b)
                    p_buf[rows, sl] = p.astype(p_buf.dtype)
                    ds_buf[rows, sl] = (p * (dp_buf[rows, sl] - dl_b) * SCALE).astype(ds_buf.dtype)
            q_rows = pl.ds(pl.multiple_of(i * tq, tq), tq)
            dv_sc[...] += _dot(p_buf[...], do_ref[q_rows, :], TN)
            dk_sc[...] += _dot(ds_buf[...], q_ref[q_rows, :], TN)
            dq_ref[q_rows, :] += _dot(ds_buf[...], kb, NN)

        scores(0, s_a, dp_a)

        def pair_step(t, carry):
            stage(2 * t, s_a, dp_a, p_a, ds_a, s_b, dp_b)
            stage(2 * t + 1, s_b, dp_b, p_b, ds_b, s_a, dp_a)
            return carry

        lax.fori_loop(0, nq // 2, pair_step, 0, unroll=2)
        dk_ref[...] = dk_sc[...]
        dv_ref[...] = dv_sc[...]

    hspec = pl.BlockSpec((S, LANES), lambda h, j: (0, h))
    kspec = pl.BlockSpec((tk, LANES), lambda h, j: (j, h))
    full = jax.ShapeDtypeStruct((S, HEADS * LANES), F32)
    tile_bufs = [pltpu.VMEM((tq, tk), F32), pltpu.VMEM((tq, tk), F32), pltpu.VMEM((tq, tk), MXU), pltpu.VMEM((tq, tk), MXU)]
    return pl.pallas_call(
        body, name="flash_bwd", grid=(HEADS, S // tk), in_specs=[hspec, kspec, kspec, hspec, hspec, hspec],
        out_specs=[hspec, kspec, kspec], out_shape=[full, full, full],
        scratch_shapes=tile_bufs + tile_bufs + [pltpu.VMEM((tk, LANES), F32), pltpu.VMEM((tk, LANES), F32)],
        compiler_params=pltpu.CompilerParams(dimension_semantics=("parallel", "arbitrary"), vmem_limit_bytes=VMEM_LIMIT),
    )(q, k, v, do, lse, delta)


def _ret_consts(lgh, head, rev):
    C = CHUNK
    lane = lax.broadcasted_iota(jnp.int32, (1, LANES), 1)
    hm = ((lane // 32) % 2 == head % 2).astype(F32)
    r = lax.broadcasted_iota(jnp.int32, (C, C), 0)
    c = lax.broadcasted_iota(jnp.int32, (C, C), 1)
    diff = ((c - r) if rev else (r - c)).astype(F32)
    mask = (diff > 0) if rev else (diff >= 0)
    dpos = jnp.maximum(diff, 0.0)
    din = jnp.where(mask, jnp.exp(lgh * dpos), 0.0)
    idx = lax.broadcasted_iota(jnp.int32, (C, 1), 0).astype(F32)
    eq = (C - idx) if rev else (idx + 1.0)
    ek = idx if rev else (C - 1.0 - idx)
    qd, kd = jnp.exp(lgh * eq), jnp.exp(lgh * ek)
    cd = jnp.exp(lgh * jnp.full((1, 1), float(C), F32))
    return hm, din, dpos, qd, kd, cd, eq, ek


RET_HEADS_PER_STEP = 4


def _ret_fwd(name, qt, kt, proj, lg, rev):
    S = qt.shape[0]
    C = CHUNK
    TB = _pick(S, (512, 256, 128))
    cb, nb = TB // C, S // TB
    hps = RET_HEADS_PER_STEP
    blk = (lambda g: nb - 1 - g) if rev else (lambda g: g)

    def body(lg_ref, q_ref, k_ref, v_ref, o_ref, st_ref, state_sc):
        hg, g = pl.program_id(0), pl.program_id(1)

        @pl.when(g == 0)
        def _():
            state_sc[...] = jnp.zeros_like(state_sc)

        consts = [_ret_consts(lg_ref[hg * hps + u], u, rev) for u in range(hps)]
        order = list(reversed(range(cb))) if rev else list(range(cb))
        units = [(cc, u) for cc in order for u in range(hps)]

        def operands(cc, u):
            rows = pl.ds(cc * C, C)
            pair = slice(LANES * (u // 2), LANES * (u // 2 + 1))
            hm = consts[u][0]
            return q_ref[rows, pair] * hm, k_ref[rows, pair] * hm, v_ref[rows, LANES * u:LANES * (u + 1)].astype(MXU)

        a, inc = {}, {}
        for cc, u in units:
            q, k, v = operands(cc, u)
            a[cc, u] = _dot(q.astype(MXU), k.astype(MXU), NT) * consts[u][1]
            inc[cc, u] = _dot((k * consts[u][4]).astype(MXU), v, TN)
        for u in range(hps):
            st = state_sc[u]
            for cc in order:
                st_ref[u, cc] = st
                st = st * consts[u][5] + inc[cc, u]
            state_sc[u] = st
        for cc, u in units:
            q, _, v = operands(cc, u)
            cross = _dot((q * consts[u][3]).astype(MXU), st_ref[u, cc].astype(MXU), NN)
            o_ref[pl.ds(cc * C, C), LANES * u:LANES * (u + 1)] = _dot(a[cc, u].astype(MXU), v, NN) + cross

    qk_spec = pl.BlockSpec((TB, LANES * hps // 2), lambda h, g: (blk(g), h))
    return pl.pallas_call(
        body, name=name, grid=(HEADS // hps, nb),
        in_specs=[pl.BlockSpec(memory_space=pltpu.SMEM), qk_spec, qk_spec,
                  pl.BlockSpec((TB, LANES * hps), lambda h, g: (blk(g), P_VR // (LANES * hps) + h))],
        out_specs=[pl.BlockSpec((TB, LANES * hps), lambda h, g: (blk(g), h)),
                   pl.BlockSpec((hps, cb, LANES, LANES), lambda h, g: (h, blk(g), 0, 0))],
        out_shape=[jax.ShapeDtypeStruct((S, HEADS * LANES), F32), jax.ShapeDtypeStruct((HEADS, S // C, LANES, LANES), F32)],
        scratch_shapes=[pltpu.VMEM((hps, LANES, LANES), F32)],
        compiler_params=pltpu.CompilerParams(dimension_semantics=("parallel", "arbitrary"), vmem_limit_bytes=VMEM_LIMIT),
    )(lg, qt, kt, proj)


def _ret_bwd(name, qt, kt, proj, dret, states, lg, rev):
    S = qt.shape[0]
    C = CHUNK
    TB = _pick(S, (512, 256, 128))
    cb, nb = TB // C, S // TB
    hps = RET_HEADS_PER_STEP
    blk = (lambda g: g) if rev else (lambda g: nb - 1 - g)

    def body(lg_ref, q_ref, k_ref, v_ref, do_ref, st_ref, dq_ref, dk_ref, dv_ref, dlg_ref, ds_sc, acc_cc, acc_q, acc_k, acc_s):
        hg, g = pl.program_id(0), pl.program_id(1)

        @pl.when(g == 0)
        def _():
            ds_sc[...] = jnp.zeros_like(ds_sc)
            acc_cc[...] = jnp.zeros_like(acc_cc)
            acc_q[...] = jnp.zeros_like(acc_q)
            acc_k[...] = jnp.zeros_like(acc_k)
            acc_s[...] = jnp.zeros_like(acc_s)

        lgs = [lg_ref[hg * hps + u] for u in range(hps)]
        consts = [_ret_consts(lgs[u], u, rev) for u in range(hps)]
        order = list(range(cb)) if rev else list(reversed(range(cb)))
        units = [(cc, u) for cc in order for u in range(hps)]

        def operands(cc, u):
            rows = pl.ds(cc * C, C)
            pair = slice(LANES * (u // 2), LANES * (u // 2 + 1))
            head = slice(LANES * u, LANES * (u + 1))
            hm = consts[u][0]
            return q_ref[rows, pair] * hm, k_ref[rows, pair] * hm, v_ref[rows, head].astype(MXU), do_ref[rows, head].astype(MXU)

        a, dp, dqs, inc = {}, {}, {}, {}
        for cc, u in units:
            q, k, vb, dob = operands(cc, u)
            a[cc, u] = _dot(q.astype(MXU), k.astype(MXU), NT)
            dp[cc, u] = _dot(dob, vb, NT)
            dqs[cc, u] = _dot(dob, st_ref[u, cc].astype(MXU), NT)
            inc[cc, u] = _dot((q * consts[u][3]).astype(MXU), dob, TN)
        dsn = {}
        for u in range(hps):
            ds = ds_sc[u]
            for cc in order:
                dsn[cc, u] = ds
                ds = ds * consts[u][5] + inc[cc, u]
            ds_sc[u] = ds
        for cc, u in units:
            hm, din, dpos, qd, kd, cd, eq, ek = consts[u]
            rows, head = pl.ds(cc * C, C), slice(LANES * u, LANES * (u + 1))
            q, k, vb, dob = operands(cc, u)
            qb, kb = q.astype(MXU), k.astype(MXU)
            dsnb = dsn[cc, u].astype(MXU)
            da = (dp[cc, u] * din).astype(MXU)
            vds = _dot(vb, dsnb, NT)
            dq_ref[rows, head] = (_dot(da, kb, NN) + dqs[cc, u] * qd) * hm
            dk_ref[rows, head] = (_dot(da, qb, TN) + vds * kd) * hm
            dv_ref[rows, head] = _dot((a[cc, u] * din).astype(MXU), dob, TN) + _dot((k * kd).astype(MXU), dsnb, NN)
            acc_cc[u] += dp[cc, u] * a[cc, u] * din * dpos
            acc_q[u] += dqs[cc, u] * q * (qd * eq)
            acc_k[u] += vds * k * (kd * ek)
            acc_s[u] += dsn[cc, u] * st_ref[u, cc] * (cd * float(C))

        @pl.when(g == nb - 1)
        def _():
            for u in range(hps):
                tot = (jnp.sum(acc_cc[u], keepdims=True) + jnp.sum(acc_q[u], keepdims=True)
                       + jnp.sum(acc_k[u], keepdims=True) + jnp.sum(acc_s[u], keepdims=True))
                dlg_ref[u] = jnp.broadcast_to(tot * lgs[u], (8, LANES))

    full = jax.ShapeDtypeStruct((S, HEADS * LANES), F32)
    hspec = pl.BlockSpec((TB, LANES * hps), lambda h, g: (blk(g), h))
    qk_spec = pl.BlockSpec((TB, LANES * hps // 2), lambda h, g: (blk(g), h))
    return pl.pallas_call(
        body, name=name, grid=(HEADS // hps, nb),
        in_specs=[pl.BlockSpec(memory_space=pltpu.SMEM), qk_spec, qk_spec,
                  pl.BlockSpec((TB, LANES * hps), lambda h, g: (blk(g), P_VR // (LANES * hps) + h)),
                  hspec,
                  pl.BlockSpec((hps, cb, LANES, LANES), lambda h, g: (h, blk(g), 0, 0))],
        out_specs=[hspec, hspec, hspec, pl.BlockSpec((hps, 8, LANES), lambda h, g: (h, 0, 0))],
        out_shape=[full, full, full, jax.ShapeDtypeStruct((HEADS, 8, LANES), F32)],
        scratch_shapes=[pltpu.VMEM((hps, LANES, LANES), F32), pltpu.VMEM((hps, C, C), F32), pltpu.VMEM((hps, C, LANES), F32),
                        pltpu.VMEM((hps, C, LANES), F32), pltpu.VMEM((hps, LANES, LANES), F32)],
        compiler_params=pltpu.CompilerParams(dimension_semantics=("parallel", "arbitrary"), vmem_limit_bytes=VMEM_LIMIT),
    )(lg, qt, kt, proj, dret, states)


def _rope_consts():
    inv16 = THETA ** (-jnp.arange(16, dtype=F32) / 16)
    inv32 = THETA ** (-jnp.arange(32, dtype=F32) / 32)
    lane = np.arange(LANES)
    z48 = jnp.zeros((48,), F32)
    inv_m = jnp.concatenate([inv16, z48, inv16, z48])[None, :]
    sgn_m = jnp.asarray(np.where(lane < 16, -1.0, np.where((lane >= 64) & (lane < 80), 1.0, 0.0)), F32)[None, :]
    inv_r = jnp.concatenate([inv32] * 4)[None, :]
    sgn_r = jnp.asarray(np.where(lane < 64, -1.0, 1.0), F32)[None, :]
    return inv_m, sgn_m, inv_r, sgn_r


EARLY_GRADS = ("w_down", "w_gate_up", "w_out", "w_ret_out")


def _local_step(x, pos, tgt, gains, W, early_hook=None):
    S = x.shape[0]
    ts = _pick(S, (256, 128))
    ts_wide = _pick(S, (128,))
    R = lambda a, w=None, c=0: (a, ((a.shape[1] if w is None else w), c))
    W_ = lambda a: (a, None)

    win = _win_pad(W["w_in"])
    wq = _wq_pad(W["w_q_b"])
    wk, wv = _wkv_pad(W["w_kv_b"])
    wmla = _wmla_pad(W["w_mla_out"])
    wret, wout, wgu, wdown = W["w_ret_out"], W["w_out"], W["w_gate_up"], W["w_down"]
    gqn, gkn = _qk_pad(gains["g_qn"]), _qk_pad(gains["g_kn"])
    g_mix, g_q_a, g_kv_a, g_ffn = gains["g_mix"], gains["g_q_a"], gains["g_kv_a"], gains["g_ffn"]
    lg_f = -jnp.exp(gains["ret_decay_fwd"][0])
    lg_b = -jnp.exp(gains["ret_decay_bwd"][0])

    consts = list(_rope_consts())
    cosm, sinm, cosr, sinr = _rowwise("rope_tables", _tables_fn, S, ts, [R(pos)] + [W_(c) for c in consts],
                                      [(LANES, F32, LANES, 0)] * 4)

    (h,) = _rowwise("rms_mix", _rmsg_fn, S, ts, [R(x), W_(g_mix)], [(D_MODEL, MXU, D_MODEL, 0)])
    proj = _mm("in_proj", h, win, "nn")
    seg = lambda off, w: (proj, (w, off // w))
    mla_ins = [seg(P_CQ, 256), seg(P_CKV, 128), seg(P_KROPE, 128), R(cosm), R(sinm),
               W_(g_q_a), W_(g_kv_a), W_(gqn), W_(gkn), W_(wq), W_(wk), W_(wv)]
    q, k, v = _rowwise("mla_prep", _mla_prep_fn, S, ts, mla_ins, [(HEADS * LANES, MXU, HEADS * LANES, 0)] * 3)
    o, o_bf, lse = _flash_fwd(q, k, v)
    y_a = _mm("mla_out", o_bf, wmla, "nn")

    ret_ins = [seg(P_QR, 512), seg(P_KR, 512), R(cosr), R(sinr)]
    qt, kt = _rowwise("ret_prep", _ret_prep_fn, S, ts, ret_ins, [(512, F32, 512, 0)] * 2)
    ret_f, st_f = _ret_fwd("ret_fwd_f", qt, kt, proj, lg_f, False)
    ret_b, st_b = _ret_fwd("ret_fwd_b", qt, kt, proj, lg_b, True)
    post_ins = [R(ret_f), R(ret_b), seg(P_GR, 1024)]
    (o_b,) = _rowwise("ret_post", _ret_post_fn, S, ts, post_ins, [(1024, MXU, 1024, 0)])
    y_b = _mm("ret_out", o_b, wret, "nn")

    merge_ins = [seg(P_GATES, 1024), (proj, (1024, 1)), R(y_a), R(y_b)]
    (merged,) = _rowwise("merge", _merge_fn, S, ts, merge_ins, [(D_MODEL, MXU, D_MODEL, 0)])
    x1 = _mm("out_proj", merged, wout, "nn", add=x)
    (h2,) = _rowwise("rms_ffn", _rmsg_fn, S, ts, [R(x1), W_(g_ffn)], [(D_MODEL, MXU, D_MODEL, 0)])
    gu = _mm("gate_up", h2, wgu, "nn")
    (act,) = _rowwise("swiglu", lambda t: _swiglu_fn(t[:, :FFN], t[:, FFN:]), S, ts_wide, [R(gu)], [(FFN, MXU, FFN, 0)])
    x2 = _mm("down_proj", act, wdown, "nn", add=x1)
    dx2, dx2_bf, loss_rows = _rowwise("loss", lambda a, b: (lambda d, l: (d, d, l))(*_loss_fn(a, b)), S, ts, [R(x2), R(tgt)],
                                      [(D_MODEL, F32, D_MODEL, 0), (D_MODEL, MXU, D_MODEL, 0)], accs=[(1, D_MODEL)])

    gW = {}
    gW["w_down"] = _mm("d_w_down", act, dx2_bf, "tn")
    dact = _mm("d_act", dx2_bf, wdown, "nt")

    def glu_bwd(t, da):
        _, vjp = jax.vjp(_swiglu_fn, t[:, :FFN], t[:, FFN:])
        return jnp.concatenate(vjp(da), axis=1)

    (dgu,) = _rowwise("swiglu_bwd", glu_bwd, S, ts_wide, [R(gu), R(dact)], [(2 * FFN, MXU, 2 * FFN, 0)])
    gW["w_gate_up"] = _mm("d_w_gate_up", h2, dgu, "tn")
    dh2 = _mm("d_h2", dgu, wgu, "nt")

    def rms_bwd(xx, g, dh, dres):
        _, vjp = jax.vjp(_rmsg_fn, xx, g)
        dx, dg = vjp(dh)
        dx = dx + dres
        return dx, dx, dg

    dx1, dx1_bf, dg_ffn = _rowwise("rms_ffn_bwd", rms_bwd, S, ts, [R(x1), W_(g_ffn), R(dh2), R(dx2)],
                                   [(D_MODEL, F32, D_MODEL, 0), (D_MODEL, MXU, D_MODEL, 0)], accs=[(1, D_MODEL)])
    gW["w_out"] = _mm("d_w_out", merged, dx1_bf, "tn")
    dmerged = _mm("d_merged", dx1_bf, wout, "nt")

    def merge_bwd(ga, gb, ya, yb, dm):
        _, vjp = jax.vjp(_merge_fn, ga, gb, ya, yb)
        return vjp(dm)

    dga, dgb, dy_a, dy_b = _rowwise("merge_bwd", merge_bwd, S, ts, merge_ins + [R(dmerged)], [(D_MODEL, MXU, D_MODEL, 0)] * 4)
    gW["w_ret_out"] = _mm("d_w_ret_out", o_b, dy_b, "tn")
    after_early = [] if early_hook is None else [W_(early_hook({n: gW[n] for n in EARLY_GRADS}))]
    do_b = _mm("d_o_b", dy_b, wret, "nt")

    def post_bwd(rf, rb, gr, dob, *_):
        _, vjp = jax.vjp(_ret_post_fn, rf, rb, gr)
        drf, _, dgr = vjp(dob)
        return drf, dgr

    dret, dg_r = _rowwise("ret_post_bwd", post_bwd, S, ts, post_ins + [R(do_b)] + after_early, [(1024, F32, 1024, 0), (1024, MXU, 1024, 0)])
    dq_f, dk_f, dv_f, dlg_f = _ret_bwd("ret_bwd_f", qt, kt, proj, dret, st_f, lg_f, False)
    dq_b, dk_b, dv_b, dlg_b = _ret_bwd("ret_bwd_b", qt, kt, proj, dret, st_b, lg_b, True)

    def ret_prep_bwd(qr, kr, cosr_, sinr_, dqf, dqb, dkf, dkb, dvf, dvb):
        _, vjp = jax.vjp(lambda a, b: _ret_prep_fn(a, b, cosr_, sinr_), qr, kr)
        pair = lambda t: jnp.concatenate([t[:, 256 * j:256 * j + 128] + t[:, 256 * j + 128:256 * j + 256] for j in range(4)], axis=1)
        dqr, dkr = vjp((pair(dqf + dqb), pair(dkf + dkb)))
        return dqr, dkr, dvf + dvb

    dq_r, dk_r, dv_r = _rowwise("ret_prep_bwd", ret_prep_bwd, S, ts, ret_ins + [R(t) for t in (dq_f, dq_b, dk_f, dk_b, dv_f, dv_b)],
                                [(512, MXU, 512, 0), (512, MXU, 512, 0), (1024, MXU, 1024, 0)])

    gW_mla_p = _mm("d_w_mla_out", o_bf, dy_a, "tn")
    do = _mm("d_o", dy_a, wmla, "nt")
    do_bf, delta = _rowwise("attn_delta", lambda a, b, *_: _delta_fn(a, b), S, ts, [R(o), R(do)] + after_early,
                            [(HEADS * LANES, MXU, HEADS * LANES, 0), (HEADS * LANES, F32, HEADS * LANES, 0)])
    dq, dk, dv = _flash_bwd(q, k, v, do_bf, lse, delta)

    def mla_prep_bwd(cq, ckv, kr, cosm_, sinm_, gqa, gkva, gqn_, gkn_, wq_, wk_, wv_, dq_, dk_, dv_):
        f = lambda cq, ckv, kr, gqa, gkva, gqn_, gkn_, wq_, wk_, wv_: _mla_prep_fn(cq, ckv, kr, cosm_, sinm_, gqa, gkva, gqn_, gkn_, wq_, wk_, wv_)
        _, vjp = jax.vjp(f, cq, ckv, kr, gqa, gkva, gqn_, gkn_, wq_.astype(F32), wk_.astype(F32), wv_.astype(F32))
        return vjp((dq_, dk_, dv_))

    mb = _rowwise("mla_prep_bwd", mla_prep_bwd, S, ts, mla_ins + [R(dq), R(dk), R(dv)],
                  [(256, MXU, 256, 0), (128, MXU, 128, 0), (128, MXU, 128, 0)],
                  accs=[(1, 256), (1, 128), (1, LANES), (1, LANES), (256, HEADS * LANES), (128, HEADS * LANES), (128, HEADS * LANES)])
    dc_q, dc_kv, dk_rope, dg_q_a, dg_kv_a, dgqn_p, dgkn_p, dwq_p, dwk_p, dwv_p = mb

    dproj = jnp.concatenate([dga, dgb, dv_r, dg_r, dq_r, dk_r, dc_q, dc_kv, dk_rope], axis=1)
    gwin_p = _mm("d_w_in", h, dproj, "tn")
    dh = _mm("d_h", dproj, win, "nt")
    grad_x, _, dg_mix = _rowwise("rms_mix_bwd", rms_bwd, S, ts, [R(x), W_(g_mix), R(dh), R(dx1)],
                                 [(D_MODEL, F32, D_MODEL, 0), (D_MODEL, MXU, D_MODEL, 0)], accs=[(1, D_MODEL)])

    gW["w_in"] = _win_unpad(gwin_p)
    gW["w_q_b"] = _wq_unpad(dwq_p)
    gW["w_kv_b"] = _wkv_unpad(dwk_p, dwv_p)
    gW["w_mla_out"] = _wmla_unpad(gW_mla_p)
    gG = {"g_mix": dg_mix, "g_q_a": dg_q_a, "g_kv_a": dg_kv_a, "g_qn": _qk_unpad(dgqn_p),
          "g_kn": _qk_unpad(dgkn_p), "ret_decay_fwd": dlg_f[:, 0, 0][None, :], "ret_decay_bwd": dlg_b[:, 0, 0][None, :],
          "g_ffn": dg_ffn}
    return loss_rows, grad_x, gG, gW


MATS = [("w_in", (1024, 5536), 1), ("w_q_b", (256, 768), 1), ("w_kv_b", (128, 1024), 1), ("w_mla_out", (512, 1024), 1),
        ("w_ret_out", (1024, 1024), 0), ("w_out", (1024, 1024), 0), ("w_gate_up", (1024, 5632), 1), ("w_down", (2816, 1024), 0)]
GAINS = [("g_mix", 1024), ("g_q_a", 256), ("g_kv_a", 128), ("g_qn", 96), ("g_kn", 96), ("ret_decay_fwd", 8), ("ret_decay_bwd", 8),
         ("g_ffn", 1024)]
ORDER = ["g_mix", "w_in", "g_q_a", "w_q_b", "g_kv_a", "w_kv_b", "g_qn", "g_kn", "w_mla_out", "ret_decay_fwd", "ret_decay_bwd",
         "w_ret_out", "w_out", "g_ffn", "w_gate_up", "w_down"]
GAIN_LEN = sum(n for _, n in GAINS)
GAIN_PAD = -(-GAIN_LEN // LANES) * LANES


def _pack_gains(d):
    row = jnp.concatenate([d[n].reshape(1, ln).astype(F32) for n, ln in GAINS], axis=1)
    return jnp.pad(row, ((0, 0), (0, GAIN_PAD - GAIN_LEN)))


def _unpack_gains(row):
    out, off = {}, 0
    for n, ln in GAINS:
        out[n] = row[0, off:off + ln]
        off += ln
    return out


def _unshard(pieces, axis):
    if axis == 0:
        return pieces.reshape((N_DEV * pieces.shape[1], pieces.shape[2]))
    return jnp.concatenate([pieces[p] for p in range(N_DEV)], axis=1)


def _reshard(full, axis):
    if axis == 0:
        return full.reshape((N_DEV, full.shape[0] // N_DEV, full.shape[1]))
    c = full.shape[1] // N_DEV
    return jnp.stack([full[:, c * p:c * (p + 1)] for p in range(N_DEV)])


def _all_gather(shards):
    n = len(shards)

    def body(*refs):
        x_refs, out_refs = refs[:n], refs[n:2 * n]
        send_sems, recv_sems, local_sems = refs[2 * n:]
        x, y, c = lax.axis_index("x"), lax.axis_index("y"), lax.axis_index("c")
        me, sibling = (x, y, c), (x, y, 1 - c)
        chips = [(1 - x, y), (x, 1 - y), (1 - x, 1 - y)]

        def slot(a, px, py, pc):
            return out_refs[a].at[4 * px + 2 * py + pc]

        def copy(a, k, block, to, from_input=False):
            return pltpu.make_async_remote_copy(
                src_ref=x_refs[a] if from_input else slot(a, *block), dst_ref=slot(a, *block),
                send_sem=send_sems.at[a, k], recv_sem=recv_sems.at[a, k], device_id=to, device_id_type=pl.DeviceIdType.MESH)

        mine = [pltpu.make_async_copy(x_refs[a], slot(a, *me), local_sems.at[a]) for a in range(n)]
        first = [copy(a, 0, me, sibling, True) for a in range(n)]
        first += [copy(a, 1 + j, me, (*chip, c), True) for j, chip in enumerate(chips) for a in range(n)]
        for cp in mine + first:
            cp.start()
        passed = []
        for j, chip in enumerate(chips):
            for a in range(n):
                copy(a, 1 + j, (*chip, c), me).wait_recv()
                passed.append(copy(a, 4 + j, (*chip, c), sibling))
                passed[-1].start()
        for a in range(n):
            copy(a, 0, sibling, me).wait_recv()
        for j, chip in enumerate(chips):
            for a in range(n):
                copy(a, 4 + j, (*chip, 1 - c), me).wait_recv()
        for cp in first + passed:
            cp.wait_send()
        for cp in mine:
            cp.wait()

    any_spec = pl.BlockSpec(memory_space=pl.ANY)
    return pl.pallas_call(
        body, name="all_gather_weights", out_shape=[jax.ShapeDtypeStruct((N_DEV,) + s.shape, s.dtype) for s in shards],
        in_specs=[any_spec] * n, out_specs=[any_spec] * n,
        scratch_shapes=[pltpu.SemaphoreType.DMA((n, 7)), pltpu.SemaphoreType.DMA((n, 7)), pltpu.SemaphoreType.DMA((n,))],
    )(*shards)


def _all_to_all(pieces):
    n = len(pieces)

    def body(*refs):
        in_refs, out_refs = refs[:n], refs[n:2 * n]
        send_sems, recv_sems, local_sems = refs[2 * n:]
        x, y, c = lax.axis_index("x"), lax.axis_index("y"), lax.axis_index("c")
        my_id = 4 * x + 2 * y + c
        flips = [(fx, fy, fc) for fx in (0, 1) for fy in (0, 1) for fc in (0, 1)][1:]

        def copy(a, kk, f):
            p = (x ^ f[0], y ^ f[1], c ^ f[2])
            return pltpu.make_async_remote_copy(
                src_ref=in_refs[a].at[4 * p[0] + 2 * p[1] + p[2]], dst_ref=out_refs[a].at[my_id],
                send_sem=send_sems.at[a, kk], recv_sem=recv_sems.at[a, kk], device_id=p, device_id_type=pl.DeviceIdType.MESH)

        mine = [pltpu.make_async_copy(in_refs[a].at[my_id], out_refs[a].at[my_id], local_sems.at[a]) for a in range(n)]
        copies = [copy(a, kk, f) for kk, f in enumerate(flips) for a in range(n)]
        for cp in mine + copies:
            cp.start()
        for cp in copies:
            cp.wait_recv()
        for cp in copies:
            cp.wait_send()
        for cp in mine:
            cp.wait()

    any_spec = pl.BlockSpec(memory_space=pl.ANY)
    return pl.pallas_call(
        body, name="all_to_all_grads", out_shape=[jax.ShapeDtypeStruct(p.shape, p.dtype) for p in pieces],
        in_specs=[any_spec] * n, out_specs=[any_spec] * n,
        scratch_shapes=[pltpu.SemaphoreType.DMA((n, 7)), pltpu.SemaphoreType.DMA((n, 7)), pltpu.SemaphoreType.DMA((n,))],
    )(*pieces)


def _flip_peers(x, y, c):
    flips = [(fx, fy, fc) for fx in (0, 1) for fy in (0, 1) for fc in (0, 1)][1:]
    return [(x ^ fx, y ^ fy, c ^ fc) for fx, fy, fc in flips]


def _all_to_all_start(pieces):
    n = len(pieces)

    def body(*refs):
        in_refs, land_refs = refs[:n], refs[n:2 * n]
        send_sems, recv_sems = refs[2 * n], refs[2 * n + 1]
        token = refs[-1]
        x, y, c = lax.axis_index("x"), lax.axis_index("y"), lax.axis_index("c")
        my_id = 4 * x + 2 * y + c
        for kk, p in enumerate(_flip_peers(x, y, c)):
            for a in range(n):
                pltpu.make_async_remote_copy(
                    src_ref=in_refs[a].at[4 * p[0] + 2 * p[1] + p[2]], dst_ref=land_refs[a].at[my_id],
                    send_sem=send_sems.at[a * 7 + kk], recv_sem=recv_sems.at[a * 7 + kk], device_id=p,
                    device_id_type=pl.DeviceIdType.MESH).start()
        token[...] = jnp.zeros_like(token)

    hbm, sem = pl.BlockSpec(memory_space=pltpu.HBM), pl.BlockSpec(memory_space=pltpu.SEMAPHORE)
    lands = [pltpu.with_memory_space_constraint(lax.empty(p.shape, p.dtype), pltpu.HBM) for p in pieces]
    srcs = [pltpu.with_memory_space_constraint(p, pltpu.HBM) for p in pieces]
    res = pl.pallas_call(
        body, name="all_to_all_early_start",
        out_shape=[pltpu.SemaphoreType.DMA((7 * n,)), pltpu.SemaphoreType.DMA((7 * n,))]
        + [pltpu.HBM(p.shape, p.dtype) for p in pieces] * 2 + [jax.ShapeDtypeStruct((8, LANES), F32)],
        in_specs=[hbm] * (2 * n), out_specs=[sem, sem] + [hbm] * (2 * n) + [pl.BlockSpec(memory_space=pltpu.VMEM)],
        input_output_aliases={i: 2 + i for i in range(2 * n)},
        compiler_params=pltpu.CompilerParams(has_side_effects=pltpu.SideEffectType.DATAFLOW_SIDE_EFFECTING),
    )(*srcs, *lands)
    return res[0], res[1], res[2:2 + n], res[2 + n:2 + 2 * n], res[-1]


def _all_to_all_wait(send_sems, recv_sems, srcs, lands, after):
    n = len(srcs)

    def body(*refs):
        in_refs, land_refs = refs[:n], refs[n:2 * n]
        send_sems, recv_sems = refs[2 * n], refs[2 * n + 1]
        x, y, c = lax.axis_index("x"), lax.axis_index("y"), lax.axis_index("c")
        my_id = 4 * x + 2 * y + c
        for kk, p in enumerate(_flip_peers(x, y, c)):
            for a in range(n):
                cp = pltpu.make_async_remote_copy(
                    src_ref=in_refs[a].at[4 * p[0] + 2 * p[1] + p[2]], dst_ref=land_refs[a].at[my_id],
                    send_sem=send_sems.at[a * 7 + kk], recv_sem=recv_sems.at[a * 7 + kk], device_id=p,
                    device_id_type=pl.DeviceIdType.MESH)
                cp.wait_send()
                cp.wait_recv()

    hbm, sem = pl.BlockSpec(memory_space=pltpu.HBM), pl.BlockSpec(memory_space=pltpu.SEMAPHORE)
    res = pl.pallas_call(
        body, name="all_to_all_early_wait", out_shape=[pltpu.HBM(p.shape, p.dtype) for p in srcs] * 2,
        in_specs=[hbm] * (2 * n) + [sem, sem, pl.BlockSpec(memory_space=pl.ANY)], out_specs=[hbm] * (2 * n),
        input_output_aliases={i: i for i in range(2 * n)},
        compiler_params=pltpu.CompilerParams(has_side_effects=pltpu.SideEffectType.DATAFLOW_SIDE_EFFECTING),
    )(*srcs, *lands, send_sems, recv_sems, after)
    return res[:n], res[n:]


def _adamw(name, parts, w, m, v):
    rows, cols = w.shape
    tr = _pick(rows, (128, 64, 32, 16, 8))
    pspec = pl.BlockSpec((N_DEV, tr, cols), lambda i: (0, i, 0))
    rspec = pl.BlockSpec((tr, cols), lambda i: (i, 0))

    def body(p_ref, w_ref, m_ref, v_ref, g_ref, d_ref, m2_ref, v2_ref):
        g, d, m2, v2 = _adamw_fn([p_ref[s] for s in range(N_DEV)], w_ref[...], m_ref[...], v_ref[...])
        g_ref[...], d_ref[...], m2_ref[...], v2_ref[...] = g, d, m2, v2

    return pl.pallas_call(
        body, name=name, grid=(rows // tr,), in_specs=[pspec, rspec, rspec, rspec], out_specs=[rspec] * 4,
        out_shape=[jax.ShapeDtypeStruct((rows, cols), F32)] * 4,
        compiler_params=pltpu.CompilerParams(dimension_semantics=("parallel",), vmem_limit_bytes=VMEM_LIMIT),
    )(parts, w, m, v)


def kernel(x, positions, g_mix, w_in, g_q_a, w_q_b, g_kv_a, w_kv_b, g_qn, g_kn, w_mla_out, ret_decay_fwd, ret_decay_bwd, w_ret_out, w_out, g_ffn, w_gate_up, w_down, loss_target, m_g_mix, m_w_in, m_g_q_a, m_w_q_b, m_g_kv_a, m_w_kv_b, m_g_qn, m_g_kn, m_w_mla_out, m_ret_decay_fwd, m_ret_decay_bwd, m_w_ret_out, m_w_out, m_g_ffn, m_w_gate_up, m_w_down, v_g_mix, v_w_in, v_g_q_a, v_w_q_b, v_g_kv_a, v_w_kv_b, v_g_qn, v_g_kn, v_w_mla_out, v_ret_decay_fwd, v_ret_decay_bwd, v_w_ret_out, v_w_out, v_g_ffn, v_w_gate_up, v_w_down):
    w = dict(g_mix=g_mix, w_in=w_in, g_q_a=g_q_a, w_q_b=w_q_b, g_kv_a=g_kv_a, w_kv_b=w_kv_b, g_qn=g_qn, g_kn=g_kn, w_mla_out=w_mla_out,
             ret_decay_fwd=ret_decay_fwd, ret_decay_bwd=ret_decay_bwd, w_ret_out=w_ret_out, w_out=w_out, g_ffn=g_ffn,
             w_gate_up=w_gate_up, w_down=w_down)
    m = dict(g_mix=m_g_mix, w_in=m_w_in, g_q_a=m_g_q_a, w_q_b=m_w_q_b, g_kv_a=m_g_kv_a, w_kv_b=m_w_kv_b, g_qn=m_g_qn, g_kn=m_g_kn,
             w_mla_out=m_w_mla_out, ret_decay_fwd=m_ret_decay_fwd, ret_decay_bwd=m_ret_decay_bwd, w_ret_out=m_w_ret_out, w_out=m_w_out,
             g_ffn=m_g_ffn, w_gate_up=m_w_gate_up, w_down=m_w_down)
    v = dict(g_mix=v_g_mix, w_in=v_w_in, g_q_a=v_g_q_a, w_q_b=v_w_q_b, g_kv_a=v_g_kv_a, w_kv_b=v_w_kv_b, g_qn=v_g_qn, g_kn=v_g_kn,
             w_mla_out=v_w_mla_out, ret_decay_fwd=v_ret_decay_fwd, ret_decay_bwd=v_ret_decay_bwd, w_ret_out=v_w_ret_out, w_out=v_w_out,
             g_ffn=v_g_ffn, w_gate_up=v_w_gate_up, w_down=v_w_down)
    gains = {n: w[n].reshape(1, ln) for n, ln in GAINS}

    gathered = _all_gather([w[n].astype(WIRE) for n, _, _ in MATS])
    W = {n: _unshard(g, axis) for (n, _, axis), g in zip(MATS, gathered)}
    S = x.shape[1]
    pos = positions.reshape(S, 1).astype(F32)
    axis_of = {n: axis for n, _, axis in MATS}
    early = {}

    def start_early(g):
        early["handles"] = _all_to_all_start([_reshard(g[n], axis_of[n]).astype(GWIRE) for n in EARLY_GRADS])
        return early["handles"][4]

    loss_rows, grad_x, gG, gW = _local_step(x.reshape(S, D_MODEL), pos, loss_target.reshape(S, D_MODEL), gains, W, start_early)
    loss = lax.psum(jnp.sum(loss_rows), ("x", "y", "c"))

    late = [n for n, _, _ in MATS if n not in EARLY_GRADS]
    pieces = [_reshard(gW[n], axis_of[n]).astype(GWIRE) for n in late]
    pieces.append(jnp.broadcast_to(_pack_gains(gG)[None], (N_DEV, 1, GAIN_PAD)))
    late_parts = _all_to_all(pieces)
    send_sems, recv_sems, srcs, lands, _ = early["handles"]
    srcs, lands = _all_to_all_wait(send_sems, recv_sems, srcs, lands, late_parts[-1])
    my_id = 4 * lax.axis_index("x") + 2 * lax.axis_index("y") + lax.axis_index("c")
    parts = {n: lax.dynamic_update_index_in_dim(land, lax.dynamic_index_in_dim(src, my_id, 0, keepdims=False), my_id, 0)
             for n, src, land in zip(EARLY_GRADS, srcs, lands)}
    parts.update(zip(late, late_parts))
    out = [dict() for _ in range(4)]
    for n, _, _ in MATS:
        for o, r in zip(out, _adamw("adamw_" + n, parts[n], w[n], m[n], v[n])):
            o[n] = r
    for o, r in zip(out, _adamw("adamw_gains", late_parts[-1], _pack_gains(w), _pack_gains(m), _pack_gains(v))):
        o.update(_unpack_gains(r))
    return (loss, grad_x.reshape(x.shape), *[o[n] for o in out for n in ORDER])
```

```python
import functools

import numpy as np
import jax
import jax.numpy as jnp
from jax import lax
from jax.experimental import pallas as pl
from jax.experimental.pallas import tpu as pltpu

F32 = jnp.float32
MXU = jnp.bfloat16
WIRE = jnp.bfloat16
GWIRE = jnp.bfloat16

N_DEV = 8
D_MODEL = 1024
HEADS = 8
LANES = 128
Q_RANK, KV_RANK = 256, 128
NOPE, ROPE_M, V_M = 64, 32, 64
QK_M = NOPE + ROPE_M
RQK, RV = 64, 128
CHUNK = 128
FFN = 2816
IN_WIDTH = 5536
THETA = 10000.0
EPS = 1e-6
LR, B1, B2, AEPS, WD, STEP = 0.001, 0.9, 0.999, 1e-08, 0.01, 10
VMEM_LIMIT = 56 * 1024 * 1024

NN = ((1,), (0,))
NT = ((1,), (1,))
TN = ((0,), (0,))

P_GATES, P_VR, P_GR, P_QR, P_KR, P_CQ, P_CKV, P_KROPE, P_WIDTH = 0, 2048, 3072, 4096, 4608, 5120, 5376, 5504, 5632
O_CQ, O_CKV, O_KROPE, O_QR, O_KR, O_VR, O_GR, O_GATES = 0, 256, 384, 416, 928, 1440, 2464, 3488


def _dot(a, b, dims):
    return lax.dot_general(a, b, (dims, ((), ())), preferred_element_type=F32)


def _pick(dim, cands):
    for c in cands:
        if dim % c == 0:
            return c
    return dim


def _pairs(t):
    return t.reshape(t.shape[0], 4, 2, 2, 32).transpose(0, 1, 3, 2, 4).reshape(t.shape[0], 512)


def _win_pad(w):
    z = jnp.zeros((w.shape[0], 48), w.dtype)
    kr = w[:, O_KROPE:O_KROPE + 32]
    return jnp.concatenate([w[:, O_GATES:], w[:, O_VR:O_VR + 1024], w[:, O_GR:O_GR + 1024], _pairs(w[:, O_QR:O_QR + 512]),
                            _pairs(w[:, O_KR:O_KR + 512]), w[:, :O_CKV], w[:, O_CKV:O_KROPE], kr[:, :16], z, kr[:, 16:], z], axis=1)


def _win_unpad(g):
    return jnp.concatenate([g[:, P_CQ:P_CQ + 256], g[:, P_CKV:P_CKV + 128], g[:, P_KROPE:P_KROPE + 16], g[:, P_KROPE + 64:P_KROPE + 80],
                            _pairs(g[:, P_QR:P_QR + 512]), _pairs(g[:, P_KR:P_KR + 512]), g[:, P_VR:P_VR + 1024],
                            g[:, P_GR:P_GR + 1024], g[:, P_GATES:P_GATES + 2048]], axis=1)


def _qk_pad(t):
    z = jnp.zeros(t.shape[:-1] + (32,), t.dtype)
    return jnp.concatenate([t[..., 64:80], t[..., 0:48], t[..., 80:96], t[..., 48:64], z], axis=-1)


def _qk_unpad(p):
    return jnp.concatenate([p[..., 16:64], p[..., 80:96], p[..., 0:16], p[..., 64:80]], axis=-1)


def _wq_pad(w):
    return _qk_pad(w.reshape(Q_RANK, HEADS, QK_M)).reshape(Q_RANK, HEADS * LANES)


def _wq_unpad(g):
    return _qk_unpad(g.reshape(Q_RANK, HEADS, LANES)).reshape(Q_RANK, HEADS * QK_M)


def _wkv_pad(w):
    t = w.reshape(KV_RANK, HEADS, NOPE + V_M)
    z = lambda n: jnp.zeros((KV_RANK, HEADS, n), w.dtype)
    wk = jnp.concatenate([z(16), t[..., 0:48], z(16), t[..., 48:64], z(32)], axis=-1)
    wv = jnp.concatenate([t[..., 64:128], z(64)], axis=-1)
    return wk.reshape(KV_RANK, HEADS * LANES), wv.reshape(KV_RANK, HEADS * LANES)


def _wkv_unpad(dwk, dwv):
    k, v = dwk.reshape(KV_RANK, HEADS, LANES), dwv.reshape(KV_RANK, HEADS, LANES)
    return jnp.concatenate([k[..., 16:64], k[..., 80:96], v[..., 0:64]], axis=-1).reshape(KV_RANK, HEADS * (NOPE + V_M))


def _wmla_pad(w):
    t = w.reshape(HEADS, V_M, D_MODEL)
    return jnp.concatenate([t, jnp.zeros_like(t)], axis=1).reshape(HEADS * LANES, D_MODEL)


def _wmla_unpad(g):
    return g.reshape(HEADS, LANES, D_MODEL)[:, :V_M].reshape(HEADS * V_M, D_MODEL)


def _rowwise(name, fn, rows, ts, ins, outs, accs=(), ncol=1):
    n_in, n_out, n_acc = len(ins), len(outs), len(accs)

    def colmap(col):
        if callable(col):
            return lambda i, j: (i, col(j))
        return lambda i, j: (i, col)

    arrays, in_specs = [], []
    for arr, spec in ins:
        arrays.append(arr)
        if spec is None:
            in_specs.append(pl.BlockSpec(arr.shape, functools.partial(lambda i, j, nd: (0,) * nd, nd=arr.ndim)))
        else:
            in_specs.append(pl.BlockSpec((ts, spec[0]), colmap(spec[1])))
    out_shape, out_specs = [], []
    for total, dtype, width, col in outs:
        out_shape.append(jax.ShapeDtypeStruct((rows, total), dtype))
        out_specs.append(pl.BlockSpec((ts, width), colmap(col)))
    for shp in accs:
        out_shape.append(jax.ShapeDtypeStruct(shp, F32))
        out_specs.append(pl.BlockSpec(shp, functools.partial(lambda i, j, nd: (0,) * nd, nd=len(shp))))

    def body(*refs):
        vals = [r[...] for r in refs[:n_in]]
        res = fn(*vals)
        if not isinstance(res, (tuple, list)):
            res = (res,)
        for r, v in zip(refs[n_in:n_in + n_out], res[:n_out]):
            r[...] = v.astype(r.dtype)
        if n_acc:
            first = jnp.logical_and(pl.program_id(0) == 0, pl.program_id(1) == 0)
            for r, v in zip(refs[n_in + n_out:], res[n_out:]):
                @pl.when(first)
                def _(r=r):
                    r[...] = jnp.zeros_like(r)
                r[...] += v.astype(F32)

    res = pl.pallas_call(
        body, name=name, grid=(rows // ts, ncol), in_specs=in_specs, out_specs=out_specs, out_shape=out_shape,
        compiler_params=pltpu.CompilerParams(dimension_semantics=("arbitrary", "arbitrary"), vmem_limit_bytes=VMEM_LIMIT),
    )(*arrays)
    return res


MM_OPERAND_BYTES = 24 * 1024 * 1024


def _mm(name, a, b, mode, add=None, after=None):
    if mode == "nn":
        (M, K), N = a.shape, b.shape[1]
    elif mode == "nt":
        (M, K), N = a.shape, b.shape[0]
    else:
        (K, M), N = a.shape, b.shape[1]
    tm = _pick(M, (512, 256, 128)) if mode == "tn" else _pick(M, (1024, 512, 256, 128))
    tn = _pick(N, (512, 256, 128))
    fits = lambda t: 2 * (tm + tn) * t * a.dtype.itemsize <= MM_OPERAND_BYTES
    tk = next(t for t in (K, 4096, 2816, 2048, 1408, 1024, 512, 256, 128) if K % t == 0 and (fits(t) or t == 128))
    nk = K // tk
    dims = {"nn": NN, "nt": NT, "tn": TN}[mode]
    a_spec = pl.BlockSpec((tk, tm), lambda i, j, k: (k, i)) if mode == "tn" else pl.BlockSpec((tm, tk), lambda i, j, k: (i, k))
    b_spec = pl.BlockSpec((tn, tk), lambda i, j, k: (j, k)) if mode == "nt" else pl.BlockSpec((tk, tn), lambda i, j, k: (k, j))
    o_spec = pl.BlockSpec((tm, tn), lambda i, j, k: (i, j))
    has_add = add is not None

    def body(*refs):
        a_ref, b_ref, o_ref = refs[0], refs[1], refs[-1]
        d = _dot(a_ref[...], b_ref[...], dims)
        first = (d + refs[2][...]) if has_add else d
        if nk == 1:
            o_ref[...] = first
        else:
            k = pl.program_id(2)

            @pl.when(k == 0)
            def _():
                o_ref[...] = first

            @pl.when(k > 0)
            def _():
                o_ref[...] += d

    args = [a, b] + ([add] if has_add else []) + ([] if after is None else [after])
    specs = [a_spec, b_spec] + ([o_spec] if has_add else []) + ([] if after is None else [pl.BlockSpec(memory_space=pl.ANY)])
    return pl.pallas_call(
        body, name=name, grid=(M // tm, N // tn, nk), in_specs=specs, out_specs=o_spec,
        out_shape=jax.ShapeDtypeStruct((M, N), F32),
        compiler_params=pltpu.CompilerParams(dimension_semantics=("parallel", "parallel", "arbitrary"), vmem_limit_bytes=VMEM_LIMIT),
    )(*args)


@jax.custom_vjp
def _swap64(x):
    return pltpu.roll(x, 64, 1)


_swap64.defvjp(lambda x: (_swap64(x), None), lambda _, g: (_swap64(g),))


@jax.custom_vjp
def _mxdot(a, b):
    return _dot(a.astype(MXU), b.astype(MXU), NN)


def _mxdot_bwd(res, g):
    a, b = res
    gb = g.astype(MXU)
    return _dot(gb, b.astype(MXU), NT), _dot(a.astype(MXU), gb, TN)


_mxdot.defvjp(lambda a, b: (_mxdot(a, b), (a, b)), _mxdot_bwd)


def _rms(x):
    return x * lax.rsqrt(jnp.mean(x * x, axis=-1, keepdims=True) + EPS)


def _rmsg_fn(x, g):
    return _rms(x) * g


def _silu(x):
    return x * jax.nn.sigmoid(x)


def _tables_fn(pos, inv_m, sgn_m, inv_r, sgn_r):
    am, ar = pos * inv_m, pos * inv_r
    return jnp.cos(am), jnp.sin(am) * sgn_m, jnp.cos(ar), jnp.sin(ar) * sgn_r


def _head_blocks(t):
    return [t[:, LANES * h:LANES * (h + 1)] for h in range(t.shape[1] // LANES)]


def _mla_prep_fn(cq, ckv, kr, cosm, sinm, gqa, gkva, gqn, gkn, wq, wk, wv):
    cqn = _rms(cq) * gqa
    ckvn = _rms(ckv) * gkva
    q_raw = _mxdot(cqn, wq)
    k_raw = _mxdot(ckvn, wk)
    lane = lax.broadcasted_iota(jnp.int32, (1, HEADS * LANES), 1)
    v = _mxdot(ckvn, wv) + (lane % LANES == V_M).astype(F32)

    def norm_rope(blocks, g, extra):
        outs = []
        for b in blocks:
            if extra is not None:
                b = b + extra
            n = b * lax.rsqrt(jnp.sum(b * b, axis=-1, keepdims=True) * (1.0 / QK_M) + EPS) * g
            outs.append(n * cosm + _swap64(n) * sinm)
        return jnp.concatenate(outs, axis=1)

    q = norm_rope(_head_blocks(q_raw), gqn, None)
    k = norm_rope(_head_blocks(k_raw), gkn, kr)
    return q, k, v


def _ret_prep_fn(qr, kr, cosr, sinr):
    def rope(t, scale):
        return jnp.concatenate([(b * cosr + _swap64(b) * sinr) * scale for b in _head_blocks(t)], axis=1)
    return rope(qr, 1.0), rope(kr, RQK ** -0.5)


def _ret_post_fn(rf, rb, gr):
    ret = rf + rb
    outs = []
    for b, g in zip(_head_blocks(ret), _head_blocks(gr)):
        outs.append(_silu(g) * _rms(b))
    return jnp.concatenate(outs, axis=1)


def _merge_fn(ga, gb, ya, yb):
    return jax.nn.sigmoid(ga) * ya + jax.nn.sigmoid(gb) * yb


def _swiglu_fn(gate, up):
    return _silu(gate) * up


def _loss_fn(x2, tgt):
    d = x2 - tgt
    return d * (1.0 / D_MODEL), 0.5 * jnp.sum(d * d, axis=0, keepdims=True) * (1.0 / D_MODEL)


def _adamw_fn(parts, w, m, v):
    g = parts[0].astype(F32)
    for p in range(1, N_DEV):
        g = g + parts[p].astype(F32)
    m2 = B1 * m + (1.0 - B1) * g
    v2 = B2 * v + (1.0 - B2) * jnp.square(g)
    m_hat = m2 / (1.0 - B1 ** STEP)
    v_hat = v2 / (1.0 - B2 ** STEP)
    delta = -LR * (m_hat / (jnp.sqrt(v_hat) + AEPS) + WD * w)
    return g, delta, m2, v2


SCALE = QK_M ** -0.5
LOG2E = 1.4426950408889634
FLASH_ROWS = 32


def _flash_fwd(q, k, v):
    S = q.shape[0]
    tq = tk = _pick(S, (512, 256, 128))
    ncb = tk // LANES
    nkv = S // tk
    assert nkv % 2 == 0, "kv tiles are processed in pairs"
    mrows = 64
    c = SCALE * LOG2E

    def body(q_ref, k_ref, v_ref, o_ref, obf_ref, lse_ref, s_a, p_a, s_b, p_b, m_sc, a_sc, acc_sc):
        m_sc[...] = jnp.full_like(m_sc, -jnp.inf)
        acc_sc[...] = jnp.zeros_like(acc_sc)
        qb = q_ref[...]

        def scores(j, s_buf):
            s_buf[...] = _dot(qb, k_ref[pl.ds(pl.multiple_of(j * tk, tk), tk), :], NT)

        def stage(j, s_buf, p_buf, s_next):
            scores(jnp.minimum(j + 1, nkv - 1), s_next)
            for r in range(tq // mrows):
                rows = slice(r * mrows, (r + 1) * mrows)
                cols = [s_buf[rows, LANES * cb:LANES * (cb + 1)] for cb in range(ncb)]
                m_prev = m_sc[rows, :]
                row_max = jnp.max(functools.reduce(jnp.maximum, cols), axis=-1, keepdims=True)
                m_new = jnp.maximum(m_prev, jnp.broadcast_to(row_max, (mrows, LANES)))
                a_sc[rows, :] = jnp.exp2((m_prev - m_new) * c)
                m_sc[rows, :] = m_new
                for cb in range(ncb):
                    p_buf[rows, LANES * cb:LANES * (cb + 1)] = jnp.exp2((cols[cb] - m_new) * c).astype(p_buf.dtype)
            acc_sc[...] = a_sc[...] * acc_sc[...] + _dot(p_buf[...], v_ref[pl.ds(pl.multiple_of(j * tk, tk), tk), :], NN)

        scores(0, s_a)

        def pair_step(t, carry):
            stage(2 * t, s_a, p_a, s_b)
            stage(2 * t + 1, s_b, p_b, s_a)
            return carry

        lax.fori_loop(0, nkv // 2, pair_step, 0, unroll=4)
        acc = acc_sc[...]
        lane = lax.broadcasted_iota(jnp.int32, (1, LANES), 1)
        l = jnp.sum(jnp.where(lane == V_M, acc, 0.0), axis=-1, keepdims=True)
        o = acc / l
        o_ref[...] = o
        obf_ref[...] = o.astype(obf_ref.dtype)
        lse_ref[...] = m_sc[...] * c + jnp.log2(jnp.broadcast_to(l, (tq, LANES)))

    qspec = pl.BlockSpec((tq, LANES), lambda h, i: (i, h))
    kspec = pl.BlockSpec((S, LANES), lambda h, i: (0, h))
    full = jax.ShapeDtypeStruct((S, HEADS * LANES), F32)
    return pl.pallas_call(
        body, name="flash_fwd", grid=(HEADS, S // tq), in_specs=[qspec, kspec, kspec], out_specs=[qspec, qspec, qspec],
        out_shape=[full, jax.ShapeDtypeStruct((S, HEADS * LANES), MXU), full],
        scratch_shapes=[pltpu.VMEM((tq, tk), F32), pltpu.VMEM((tq, tk), MXU)] * 2 + [pltpu.VMEM((tq, LANES), F32)] * 3,
        compiler_params=pltpu.CompilerParams(dimension_semantics=("parallel", "arbitrary"), vmem_limit_bytes=VMEM_LIMIT),
    )(q, k, v)


def _delta_fn(o, do):
    outs = [jnp.broadcast_to(jnp.sum(a * b, axis=-1, keepdims=True), a.shape) for a, b in zip(_head_blocks(o), _head_blocks(do))]
    return do, jnp.concatenate(outs, axis=1)


def _flash_bwd(q, k, v, do, lse, delta):
    S = q.shape[0]
    tq = tk = _pick(S, (512, 256, 128))
    ncb = tk // LANES
    c = SCALE * LOG2E

    nq = S // tq
    assert nq % 2 == 0, "q tiles are processed in pairs"

    def body(q_ref, k_ref, v_ref, do_ref, lse_ref, dl_ref, dq_ref, dk_ref, dv_ref, s_a, dp_a, p_a, ds_a, s_b, dp_b, p_b, ds_b, dk_sc, dv_sc):
        @pl.when(pl.program_id(1) == 0)
        def _():
            dq_ref[...] = jnp.zeros_like(dq_ref)

        dk_sc[...] = jnp.zeros_like(dk_sc)
        dv_sc[...] = jnp.zeros_like(dv_sc)
        kb, vb = k_ref[...], v_ref[...]

        def scores(i, s_buf, dp_buf):
            q_rows = pl.ds(pl.multiple_of(i * tq, tq), tq)
            s_buf[...] = _dot(q_ref[q_rows, :], kb, NT)
            dp_buf[...] = _dot(do_ref[q_rows, :], vb, NT)

        def stage(i, s_buf, dp_buf, p_buf, ds_buf, s_next, dp_next):
            scores(jnp.minimum(i + 1, nq - 1), s_next, dp_next)
            for r in range(tq // FLASH_ROWS):
                rows = slice(r * FLASH_ROWS, (r + 1) * FLASH_ROWS)
                grows = pl.ds(pl.multiple_of(i * tq + r * FLASH_ROWS, FLASH_ROWS), FLASH_ROWS)
                lse_b, dl_b = lse_ref[grows, :], dl_ref[grows, :]
                for cb in range(ncb):
                    sl = slice(LANES * cb, LANES * (cb + 1))
                    p = jnp.exp2(s_buf[rows, sl] * c - lse_b)
                    p_buf[rows, sl] = p.astype(p_buf.dtype)
                    ds_buf[rows, sl] = (p * (dp_buf[rows, sl] - dl_b) * SCALE).astype(ds_buf.dtype)
            q_rows = pl.ds(pl.multiple_of(i * tq, tq), tq)
            dv_sc[...] += _dot(p_buf[...], do_ref[q_rows, :], TN)
            dk_sc[...] += _dot(ds_buf[...], q_ref[q_rows, :], TN)
            dq_ref[q_rows, :] += _dot(ds_buf[...], kb, NN)

        scores(0, s_a, dp_a)

        def pair_step(t, carry):
            stage(2 * t, s_a, dp_a, p_a, ds_a, s_b, dp_b)
            stage(2 * t + 1, s_b, dp_b, p_b, ds_b, s_a, dp_a)
            return carry

        lax.fori_loop(0, nq // 2, pair_step, 0, unroll=2)
        dk_ref[...] = dk_sc[...]
        dv_ref[...] = dv_sc[...]

    hspec = pl.BlockSpec((S, LANES), lambda h, j: (0, h))
    kspec = pl.BlockSpec((tk, LANES), lambda h, j: (j, h))
    full = jax.ShapeDtypeStruct((S, HEADS * LANES), F32)
    tile_bufs = [pltpu.VMEM((tq, tk), F32), pltpu.VMEM((tq, tk), F32), pltpu.VMEM((tq, tk), MXU), pltpu.VMEM((tq, tk), MXU)]
    return pl.pallas_call(
        body, name="flash_bwd", grid=(HEADS, S // tk), in_specs=[hspec, kspec, kspec, hspec, hspec, hspec],
        out_specs=[hspec, kspec, kspec], out_shape=[full, full, full],
        scratch_shapes=tile_bufs + tile_bufs + [pltpu.VMEM((tk, LANES), F32), pltpu.VMEM((tk, LANES), F32)],
        compiler_params=pltpu.CompilerParams(dimension_semantics=("parallel", "arbitrary"), vmem_limit_bytes=VMEM_LIMIT),
    )(q, k, v, do, lse, delta)


def _ret_consts(lgh, head, rev):
    C = CHUNK
    lane = lax.broadcasted_iota(jnp.int32, (1, LANES), 1)
    hm = ((lane // 32) % 2 == head % 2).astype(F32)
    r = lax.broadcasted_iota(jnp.int32, (C, C), 0)
    c = lax.broadcasted_iota(jnp.int32, (C, C), 1)
    diff = ((c - r) if rev else (r - c)).astype(F32)
    mask = (diff > 0) if rev else (diff >= 0)
    dpos = jnp.maximum(diff, 0.0)
    din = jnp.where(mask, jnp.exp(lgh * dpos), 0.0)
    idx = lax.broadcasted_iota(jnp.int32, (C, 1), 0).astype(F32)
    eq = (C - idx) if rev else (idx + 1.0)
    ek = idx if rev else (C - 1.0 - idx)
    qd, kd = jnp.exp(lgh * eq), jnp.exp(lgh * ek)
    cd = jnp.exp(lgh * jnp.full((1, 1), float(C), F32))
    return hm, din, dpos, qd, kd, cd, eq, ek


RET_HEADS_PER_STEP = 4


def _ret_fwd(name, qt, kt, proj, lg, rev):
    S = qt.shape[0]
    C = CHUNK
    TB = _pick(S, (512, 256, 128))
    cb, nb = TB // C, S // TB
    hps = RET_HEADS_PER_STEP
    blk = (lambda g: nb - 1 - g) if rev else (lambda g: g)

    def body(lg_ref, q_ref, k_ref, v_ref, o_ref, st_ref, state_sc):
        hg, g = pl.program_id(0), pl.program_id(1)

        @pl.when(g == 0)
        def _():
            state_sc[...] = jnp.zeros_like(state_sc)

        consts = [_ret_consts(lg_ref[hg * hps + u], u, rev) for u in range(hps)]
        order = list(reversed(range(cb))) if rev else list(range(cb))
        units = [(cc, u) for cc in order for u in range(hps)]

        def operands(cc, u):
            rows = pl.ds(cc * C, C)
            pair = slice(LANES * (u // 2), LANES * (u // 2 + 1))
            hm = consts[u][0]
            return q_ref[rows, pair] * hm, k_ref[rows, pair] * hm, v_ref[rows, LANES * u:LANES * (u + 1)].astype(MXU)

        a, inc = {}, {}
        for cc, u in units:
            q, k, v = operands(cc, u)
            a[cc, u] = _dot(q.astype(MXU), k.astype(MXU), NT) * consts[u][1]
            inc[cc, u] = _dot((k * consts[u][4]).astype(MXU), v, TN)
        for u in range(hps):
            st = state_sc[u]
            for cc in order:
                st_ref[u, cc] = st
                st = st * consts[u][5] + inc[cc, u]
            state_sc[u] = st
        for cc, u in units:
            q, _, v = operands(cc, u)
            cross = _dot((q * consts[u][3]).astype(MXU), st_ref[u, cc].astype(MXU), NN)
            o_ref[pl.ds(cc * C, C), LANES * u:LANES * (u + 1)] = _dot(a[cc, u].astype(MXU), v, NN) + cross

    qk_spec = pl.BlockSpec((TB, LANES * hps // 2), lambda h, g: (blk(g), h))
    return pl.pallas_call(
        body, name=name, grid=(HEADS // hps, nb),
        in_specs=[pl.BlockSpec(memory_space=pltpu.SMEM), qk_spec, qk_spec,
                  pl.BlockSpec((TB, LANES * hps), lambda h, g: (blk(g), P_VR // (LANES * hps) + h))],
        out_specs=[pl.BlockSpec((TB, LANES * hps), lambda h, g: (blk(g), h)),
                   pl.BlockSpec((hps, cb, LANES, LANES), lambda h, g: (h, blk(g), 0, 0))],
        out_shape=[jax.ShapeDtypeStruct((S, HEADS * LANES), F32), jax.ShapeDtypeStruct((HEADS, S // C, LANES, LANES), F32)],
        scratch_shapes=[pltpu.VMEM((hps, LANES, LANES), F32)],
        compiler_params=pltpu.CompilerParams(dimension_semantics=("parallel", "arbitrary"), vmem_limit_bytes=VMEM_LIMIT),
    )(lg, qt, kt, proj)


def _ret_bwd(name, qt, kt, proj, dret, states, lg, rev):
    S = qt.shape[0]
    C = CHUNK
    TB = _pick(S, (512, 256, 128))
    cb, nb = TB // C, S // TB
    hps = RET_HEADS_PER_STEP
    blk = (lambda g: g) if rev else (lambda g: nb - 1 - g)

    def body(lg_ref, q_ref, k_ref, v_ref, do_ref, st_ref, dq_ref, dk_ref, dv_ref, dlg_ref, ds_sc, acc_cc, acc_q, acc_k, acc_s):
        hg, g = pl.program_id(0), pl.program_id(1)

        @pl.when(g == 0)
        def _():
            ds_sc[...] = jnp.zeros_like(ds_sc)
            acc_cc[...] = jnp.zeros_like(acc_cc)
            acc_q[...] = jnp.zeros_like(acc_q)
            acc_k[...] = jnp.zeros_like(acc_k)
            acc_s[...] = jnp.zeros_like(acc_s)

        lgs = [lg_ref[hg * hps + u] for u in range(hps)]
        consts = [_ret_consts(lgs[u], u, rev) for u in range(hps)]
        order = list(range(cb)) if rev else list(reversed(range(cb)))
        units = [(cc, u) for cc in order for u in range(hps)]

        def operands(cc, u):
            rows = pl.ds(cc * C, C)
            pair = slice(LANES * (u // 2), LANES * (u // 2 + 1))
            head = slice(LANES * u, LANES * (u + 1))
            hm = consts[u][0]
            return q_ref[rows, pair] * hm, k_ref[rows, pair] * hm, v_ref[rows, head].astype(MXU), do_ref[rows, head].astype(MXU)

        a, dp, dqs, inc = {}, {}, {}, {}
        for cc, u in units:
            q, k, vb, dob = operands(cc, u)
            a[cc, u] = _dot(q.astype(MXU), k.astype(MXU), NT)
            dp[cc, u] = _dot(dob, vb, NT)
            dqs[cc, u] = _dot(dob, st_ref[u, cc].astype(MXU), NT)
            inc[cc, u] = _dot((q * consts[u][3]).astype(MXU), dob, TN)
        dsn = {}
        for u in range(hps):
            ds = ds_sc[u]
            for cc in order:
                dsn[cc, u] = ds
                ds = ds * consts[u][5] + inc[cc, u]
            ds_sc[u] = ds
        for cc, u in units:
            hm, din, dpos, qd, kd, cd, eq, ek = consts[u]
            rows, head = pl.ds(cc * C, C), slice(LANES * u, LANES * (u + 1))
            q, k, vb, dob = operands(cc, u)
            qb, kb = q.astype(MXU), k.astype(MXU)
            dsnb = dsn[cc, u].astype(MXU)
            da = (dp[cc, u] * din).astype(MXU)
            vds = _dot(vb, dsnb, NT)
            dq_ref[rows, head] = (_dot(da, kb, NN) + dqs[cc, u] * qd) * hm
            dk_ref[rows, head] = (_dot(da, qb, TN) + vds * kd) * hm
            dv_ref[rows, head] = _dot((a[cc, u] * din).astype(MXU), dob, TN) + _dot((k * kd).astype(MXU), dsnb, NN)
            acc_cc[u] += dp[cc, u] * a[cc, u] * din * dpos
            acc_q[u] += dqs[cc, u] * q * (qd * eq)
            acc_k[u] += vds * k * (kd * ek)
            acc_s[u] += dsn[cc, u] * st_ref[u, cc] * (cd * float(C))

        @pl.when(g == nb - 1)
        def _():
            for u in range(hps):
                tot = (jnp.sum(acc_cc[u], keepdims=True) + jnp.sum(acc_q[u], keepdims=True)
                       + jnp.sum(acc_k[u], keepdims=True) + jnp.sum(acc_s[u], keepdims=True))
                dlg_ref[u] = jnp.broadcast_to(tot * lgs[u], (8, LANES))

    full = jax.ShapeDtypeStruct((S, HEADS * LANES), F32)
    hspec = pl.BlockSpec((TB, LANES * hps), lambda h, g: (blk(g), h))
    qk_spec = pl.BlockSpec((TB, LANES * hps // 2), lambda h, g: (blk(g), h))
    return pl.pallas_call(
        body, name=name, grid=(HEADS // hps, nb),
        in_specs=[pl.BlockSpec(memory_space=pltpu.SMEM), qk_spec, qk_spec,
                  pl.BlockSpec((TB, LANES * hps), lambda h, g: (blk(g), P_VR // (LANES * hps) + h)),
                  hspec,
                  pl.BlockSpec((hps, cb, LANES, LANES), lambda h, g: (h, blk(g), 0, 0))],
        out_specs=[hspec, hspec, hspec, pl.BlockSpec((hps, 8, LANES), lambda h, g: (h, 0, 0))],
        out_shape=[full, full, full, jax.ShapeDtypeStruct((HEADS, 8, LANES), F32)],
        scratch_shapes=[pltpu.VMEM((hps, LANES, LANES), F32), pltpu.VMEM((hps, C, C), F32), pltpu.VMEM((hps, C, LANES), F32),
                        pltpu.VMEM((hps, C, LANES), F32), pltpu.VMEM((hps, LANES, LANES), F32)],
        compiler_params=pltpu.CompilerParams(dimension_semantics=("parallel", "arbitrary"), vmem_limit_bytes=VMEM_LIMIT),
    )(lg, qt, kt, proj, dret, states)


def _rope_consts():
    inv16 = THETA ** (-jnp.arange(16, dtype=F32) / 16)
    inv32 = THETA ** (-jnp.arange(32, dtype=F32) / 32)
    lane = np.arange(LANES)
    z48 = jnp.zeros((48,), F32)
    inv_m = jnp.concatenate([inv16, z48, inv16, z48])[None, :]
    sgn_m = jnp.asarray(np.where(lane < 16, -1.0, np.where((lane >= 64) & (lane < 80), 1.0, 0.0)), F32)[None, :]
    inv_r = jnp.concatenate([inv32] * 4)[None, :]
    sgn_r = jnp.asarray(np.where(lane < 64, -1.0, 1.0), F32)[None, :]
    return inv_m, sgn_m, inv_r, sgn_r


FIRST_WEIGHTS = ("w_in", "w_q_b", "w_kv_b")
EARLY_GRADS = ("w_down", "w_gate_up", "w_out", "w_ret_out")
MID_GRADS = ("w_mla_out", "w_in")


def _local_step(x, pos, tgt, gains, W, late_weights=None, grad_hook=None):
    S = x.shape[0]
    ts = _pick(S, (256, 128))
    ts_wide = _pick(S, (128,))
    R = lambda a, w=None, c=0: (a, ((a.shape[1] if w is None else w), c))
    W_ = lambda a: (a, None)

    win = _win_pad(W["w_in"])
    wq = _wq_pad(W["w_q_b"])
    wk, wv = _wkv_pad(W["w_kv_b"])
    gqn, gkn = _qk_pad(gains["g_qn"]), _qk_pad(gains["g_kn"])
    g_mix, g_q_a, g_kv_a, g_ffn = gains["g_mix"], gains["g_q_a"], gains["g_kv_a"], gains["g_ffn"]
    lg_f = -jnp.exp(gains["ret_decay_fwd"][0])
    lg_b = -jnp.exp(gains["ret_decay_bwd"][0])

    consts = list(_rope_consts())
    cosm, sinm, cosr, sinr = _rowwise("rope_tables", _tables_fn, S, ts, [R(pos)] + [W_(c) for c in consts],
                                      [(LANES, F32, LANES, 0)] * 4)

    (h,) = _rowwise("rms_mix", _rmsg_fn, S, ts, [R(x), W_(g_mix)], [(D_MODEL, MXU, D_MODEL, 0)])
    proj = _mm("in_proj", h, win, "nn")
    seg = lambda off, w: (proj, (w, off // w))
    mla_ins = [seg(P_CQ, 256), seg(P_CKV, 128), seg(P_KROPE, 128), R(cosm), R(sinm),
               W_(g_q_a), W_(g_kv_a), W_(gqn), W_(gkn), W_(wq), W_(wk), W_(wv)]
    q, k, v = _rowwise("mla_prep", _mla_prep_fn, S, ts, mla_ins, [(HEADS * LANES, MXU, HEADS * LANES, 0)] * 3)
    o, o_bf, lse = _flash_fwd(q, k, v)
    if late_weights is not None:
        W = {**W, **late_weights(lse)}
    wmla = _wmla_pad(W["w_mla_out"])
    wret, wout, wgu, wdown = W["w_ret_out"], W["w_out"], W["w_gate_up"], W["w_down"]
    y_a = _mm("mla_out", o_bf, wmla, "nn")

    ret_ins = [seg(P_QR, 512), seg(P_KR, 512), R(cosr), R(sinr)]
    qt, kt = _rowwise("ret_prep", _ret_prep_fn, S, ts, ret_ins, [(512, F32, 512, 0)] * 2)
    ret_f, st_f = _ret_fwd("ret_fwd_f", qt, kt, proj, lg_f, False)
    ret_b, st_b = _ret_fwd("ret_fwd_b", qt, kt, proj, lg_b, True)
    post_ins = [R(ret_f), R(ret_b), seg(P_GR, 1024)]
    (o_b,) = _rowwise("ret_post", _ret_post_fn, S, ts, post_ins, [(1024, MXU, 1024, 0)])
    y_b = _mm("ret_out", o_b, wret, "nn")

    merge_ins = [seg(P_GATES, 1024), (proj, (1024, 1)), R(y_a), R(y_b)]
    (merged,) = _rowwise("merge", _merge_fn, S, ts, merge_ins, [(D_MODEL, MXU, D_MODEL, 0)])
    x1 = _mm("out_proj", merged, wout, "nn", add=x)
    (h2,) = _rowwise("rms_ffn", _rmsg_fn, S, ts, [R(x1), W_(g_ffn)], [(D_MODEL, MXU, D_MODEL, 0)])
    gu = _mm("gate_up", h2, wgu, "nn")
    (act,) = _rowwise("swiglu", lambda t: _swiglu_fn(t[:, :FFN], t[:, FFN:]), S, ts_wide, [R(gu)], [(FFN, MXU, FFN, 0)])
    x2 = _mm("down_proj", act, wdown, "nn", add=x1)
    dx2, dx2_bf, loss_rows = _rowwise("loss", lambda a, b: (lambda d, l: (d, d, l))(*_loss_fn(a, b)), S, ts, [R(x2), R(tgt)],
                                      [(D_MODEL, F32, D_MODEL, 0), (D_MODEL, MXU, D_MODEL, 0)], accs=[(1, D_MODEL)])

    gW = {}
    gW["w_down"] = _mm("d_w_down", act, dx2_bf, "tn")
    dact = _mm("d_act", dx2_bf, wdown, "nt")

    def glu_bwd(t, da):
        _, vjp = jax.vjp(_swiglu_fn, t[:, :FFN], t[:, FFN:])
        return jnp.concatenate(vjp(da), axis=1)

    (dgu,) = _rowwise("swiglu_bwd", glu_bwd, S, ts_wide, [R(gu), R(dact)], [(2 * FFN, MXU, 2 * FFN, 0)])
    gW["w_gate_up"] = _mm("d_w_gate_up", h2, dgu, "tn")
    dh2 = _mm("d_h2", dgu, wgu, "nt")

    def rms_bwd(xx, g, dh, dres):
        _, vjp = jax.vjp(_rmsg_fn, xx, g)
        dx, dg = vjp(dh)
        dx = dx + dres
        return dx, dx, dg

    dx1, dx1_bf, dg_ffn = _rowwise("rms_ffn_bwd", rms_bwd, S, ts, [R(x1), W_(g_ffn), R(dh2), R(dx2)],
                                   [(D_MODEL, F32, D_MODEL, 0), (D_MODEL, MXU, D_MODEL, 0)], accs=[(1, D_MODEL)])
    gW["w_out"] = _mm("d_w_out", merged, dx1_bf, "tn")
    dmerged = _mm("d_merged", dx1_bf, wout, "nt")

    def merge_bwd(ga, gb, ya, yb, dm):
        _, vjp = jax.vjp(_merge_fn, ga, gb, ya, yb)
        return vjp(dm)

    dga, dgb, dy_a, dy_b = _rowwise("merge_bwd", merge_bwd, S, ts, merge_ins + [R(dmerged)], [(D_MODEL, MXU, D_MODEL, 0)] * 4)
    gW["w_ret_out"] = _mm("d_w_ret_out", o_b, dy_b, "tn")
    after_early = [] if grad_hook is None else [W_(grad_hook({n: gW[n] for n in EARLY_GRADS}))]
    do_b = _mm("d_o_b", dy_b, wret, "nt")

    def post_bwd(rf, rb, gr, dob, *_):
        _, vjp = jax.vjp(_ret_post_fn, rf, rb, gr)
        drf, _, dgr = vjp(dob)
        return drf, dgr

    dret, dg_r = _rowwise("ret_post_bwd", post_bwd, S, ts, post_ins + [R(do_b)] + after_early, [(1024, F32, 1024, 0), (1024, MXU, 1024, 0)])
    dq_f, dk_f, dv_f, dlg_f = _ret_bwd("ret_bwd_f", qt, kt, proj, dret, st_f, lg_f, False)
    dq_b, dk_b, dv_b, dlg_b = _ret_bwd("ret_bwd_b", qt, kt, proj, dret, st_b, lg_b, True)

    def ret_prep_bwd(qr, kr, cosr_, sinr_, dqf, dqb, dkf, dkb, dvf, dvb):
        _, vjp = jax.vjp(lambda a, b: _ret_prep_fn(a, b, cosr_, sinr_), qr, kr)
        pair = lambda t: jnp.concatenate([t[:, 256 * j:256 * j + 128] + t[:, 256 * j + 128:256 * j + 256] for j in range(4)], axis=1)
        dqr, dkr = vjp((pair(dqf + dqb), pair(dkf + dkb)))
        return dqr, dkr, dvf + dvb

    dq_r, dk_r, dv_r = _rowwise("ret_prep_bwd", ret_prep_bwd, S, ts, ret_ins + [R(t) for t in (dq_f, dq_b, dk_f, dk_b, dv_f, dv_b)],
                                [(512, MXU, 512, 0), (512, MXU, 512, 0), (1024, MXU, 1024, 0)])

    gW_mla_p = _mm("d_w_mla_out", o_bf, dy_a, "tn")
    do = _mm("d_o", dy_a, wmla, "nt")
    do_bf, delta = _rowwise("attn_delta", lambda a, b, *_: _delta_fn(a, b), S, ts, [R(o), R(do)] + after_early,
                            [(HEADS * LANES, MXU, HEADS * LANES, 0), (HEADS * LANES, F32, HEADS * LANES, 0)])
    dq, dk, dv = _flash_bwd(q, k, v, do_bf, lse, delta)

    def mla_prep_bwd(cq, ckv, kr, cosm_, sinm_, gqa, gkva, gqn_, gkn_, wq_, wk_, wv_, dq_, dk_, dv_):
        f = lambda cq, ckv, kr, gqa, gkva, gqn_, gkn_, wq_, wk_, wv_: _mla_prep_fn(cq, ckv, kr, cosm_, sinm_, gqa, gkva, gqn_, gkn_, wq_, wk_, wv_)
        _, vjp = jax.vjp(f, cq, ckv, kr, gqa, gkva, gqn_, gkn_, wq_.astype(F32), wk_.astype(F32), wv_.astype(F32))
        return vjp((dq_, dk_, dv_))

    mb = _rowwise("mla_prep_bwd", mla_prep_bwd, S, ts, mla_ins + [R(dq), R(dk), R(dv)],
                  [(256, MXU, 256, 0), (128, MXU, 128, 0), (128, MXU, 128, 0)],
                  accs=[(1, 256), (1, 128), (1, LANES), (1, LANES), (256, HEADS * LANES), (128, HEADS * LANES), (128, HEADS * LANES)])
    dc_q, dc_kv, dk_rope, dg_q_a, dg_kv_a, dgqn_p, dgkn_p, dwq_p, dwk_p, dwv_p = mb

    dproj = jnp.concatenate([dga, dgb, dv_r, dg_r, dq_r, dk_r, dc_q, dc_kv, dk_rope], axis=1)
    gW["w_in"] = _win_unpad(_mm("d_w_in", h, dproj, "tn"))
    gW["w_mla_out"] = _wmla_unpad(gW_mla_p)
    after_mid = None if grad_hook is None else grad_hook({n: gW[n] for n in MID_GRADS})
    dh = _mm("d_h", dproj, win, "nt", after=after_mid)
    grad_x, _, dg_mix = _rowwise("rms_mix_bwd", lambda a, b, c, d, *_: rms_bwd(a, b, c, d), S, ts,
                                 [R(x), W_(g_mix), R(dh), R(dx1)] + ([] if after_mid is None else [W_(after_mid)]),
                                 [(D_MODEL, F32, D_MODEL, 0), (D_MODEL, MXU, D_MODEL, 0)], accs=[(1, D_MODEL)])
    gW["w_q_b"] = _wq_unpad(dwq_p)
    gW["w_kv_b"] = _wkv_unpad(dwk_p, dwv_p)
    gG = {"g_mix": dg_mix, "g_q_a": dg_q_a, "g_kv_a": dg_kv_a, "g_qn": _qk_unpad(dgqn_p),
          "g_kn": _qk_unpad(dgkn_p), "ret_decay_fwd": dlg_f[:, 0, 0][None, :], "ret_decay_bwd": dlg_b[:, 0, 0][None, :],
          "g_ffn": dg_ffn}
    return loss_rows, grad_x, gG, gW


MATS = [("w_in", (1024, 5536), 1), ("w_q_b", (256, 768), 1), ("w_kv_b", (128, 1024), 1), ("w_mla_out", (512, 1024), 1),
        ("w_ret_out", (1024, 1024), 0), ("w_out", (1024, 1024), 0), ("w_gate_up", (1024, 5632), 1), ("w_down", (2816, 1024), 0)]
GAINS = [("g_mix", 1024), ("g_q_a", 256), ("g_kv_a", 128), ("g_qn", 96), ("g_kn", 96), ("ret_decay_fwd", 8), ("ret_decay_bwd", 8),
         ("g_ffn", 1024)]
ORDER = ["g_mix", "w_in", "g_q_a", "w_q_b", "g_kv_a", "w_kv_b", "g_qn", "g_kn", "w_mla_out", "ret_decay_fwd", "ret_decay_bwd",
         "w_ret_out", "w_out", "g_ffn", "w_gate_up", "w_down"]
GAIN_LEN = sum(n for _, n in GAINS)
GAIN_PAD = -(-GAIN_LEN // LANES) * LANES


def _pack_gains(d):
    row = jnp.concatenate([d[n].reshape(1, ln).astype(F32) for n, ln in GAINS], axis=1)
    return jnp.pad(row, ((0, 0), (0, GAIN_PAD - GAIN_LEN)))


def _unpack_gains(row):
    out, off = {}, 0
    for n, ln in GAINS:
        out[n] = row[0, off:off + ln]
        off += ln
    return out


def _unshard(pieces, axis):
    if axis == 0:
        return pieces.reshape((N_DEV * pieces.shape[1], pieces.shape[2]))
    return jnp.concatenate([pieces[p] for p in range(N_DEV)], axis=1)


def _reshard(full, axis):
    if axis == 0:
        return full.reshape((N_DEV, full.shape[0] // N_DEV, full.shape[1]))
    c = full.shape[1] // N_DEV
    return jnp.stack([full[:, c * p:c * (p + 1)] for p in range(N_DEV)])


def _all_gather(shards):
    n = len(shards)

    def body(*refs):
        x_refs, out_refs = refs[:n], refs[n:2 * n]
        send_sems, recv_sems, local_sems = refs[2 * n:]
        x, y, c = lax.axis_index("x"), lax.axis_index("y"), lax.axis_index("c")
        me, sibling = (x, y, c), (x, y, 1 - c)
        chips = [(1 - x, y), (x, 1 - y), (1 - x, 1 - y)]

        def slot(a, px, py, pc):
            return out_refs[a].at[4 * px + 2 * py + pc]

        def copy(a, k, block, to, from_input=False):
            return pltpu.make_async_remote_copy(
                src_ref=x_refs[a] if from_input else slot(a, *block), dst_ref=slot(a, *block),
                send_sem=send_sems.at[a, k], recv_sem=recv_sems.at[a, k], device_id=to, device_id_type=pl.DeviceIdType.MESH)

        mine = [pltpu.make_async_copy(x_refs[a], slot(a, *me), local_sems.at[a]) for a in range(n)]
        first = [copy(a, 0, me, sibling, True) for a in range(n)]
        first += [copy(a, 1 + j, me, (*chip, c), True) for j, chip in enumerate(chips) for a in range(n)]
        for cp in mine + first:
            cp.start()
        passed = []
        for j, chip in enumerate(chips):
            for a in range(n):
                copy(a, 1 + j, (*chip, c), me).wait_recv()
                passed.append(copy(a, 4 + j, (*chip, c), sibling))
                passed[-1].start()
        for a in range(n):
            copy(a, 0, sibling, me).wait_recv()
        for j, chip in enumerate(chips):
            for a in range(n):
                copy(a, 4 + j, (*chip, 1 - c), me).wait_recv()
        for cp in first + passed:
            cp.wait_send()
        for cp in mine:
            cp.wait()

    any_spec = pl.BlockSpec(memory_space=pl.ANY)
    return pl.pallas_call(
        body, name="all_gather_weights", out_shape=[jax.ShapeDtypeStruct((N_DEV,) + s.shape, s.dtype) for s in shards],
        in_specs=[any_spec] * n, out_specs=[any_spec] * n,
        scratch_shapes=[pltpu.SemaphoreType.DMA((n, 7)), pltpu.SemaphoreType.DMA((n, 7)), pltpu.SemaphoreType.DMA((n,))],
    )(*shards)


def _all_to_all(pieces):
    n = len(pieces)

    def body(*refs):
        in_refs, out_refs = refs[:n], refs[n:2 * n]
        send_sems, recv_sems, local_sems = refs[2 * n:]
        x, y, c = lax.axis_index("x"), lax.axis_index("y"), lax.axis_index("c")
        my_id = 4 * x + 2 * y + c
        flips = [(fx, fy, fc) for fx in (0, 1) for fy in (0, 1) for fc in (0, 1)][1:]

        def copy(a, kk, f):
            p = (x ^ f[0], y ^ f[1], c ^ f[2])
            return pltpu.make_async_remote_copy(
                src_ref=in_refs[a].at[4 * p[0] + 2 * p[1] + p[2]], dst_ref=out_refs[a].at[my_id],
                send_sem=send_sems.at[a, kk], recv_sem=recv_sems.at[a, kk], device_id=p, device_id_type=pl.DeviceIdType.MESH)

        mine = [pltpu.make_async_copy(in_refs[a].at[my_id], out_refs[a].at[my_id], local_sems.at[a]) for a in range(n)]
        copies = [copy(a, kk, f) for kk, f in enumerate(flips) for a in range(n)]
        for cp in mine + copies:
            cp.start()
        for cp in copies:
            cp.wait_recv()
        for cp in copies:
            cp.wait_send()
        for cp in mine:
            cp.wait()

    any_spec = pl.BlockSpec(memory_space=pl.ANY)
    return pl.pallas_call(
        body, name="all_to_all_grads", out_shape=[jax.ShapeDtypeStruct(p.shape, p.dtype) for p in pieces],
        in_specs=[any_spec] * n, out_specs=[any_spec] * n,
        scratch_shapes=[pltpu.SemaphoreType.DMA((n, 7)), pltpu.SemaphoreType.DMA((n, 7)), pltpu.SemaphoreType.DMA((n,))],
    )(*pieces)


def _flip_peers(x, y, c):
    flips = [(fx, fy, fc) for fx in (0, 1) for fy in (0, 1) for fc in (0, 1)][1:]
    return [(x ^ fx, y ^ fy, c ^ fc) for fx, fy, fc in flips]


def _split_copies(in_refs, land_refs, send_sems, recv_sems, gather):
    x, y, c = lax.axis_index("x"), lax.axis_index("y"), lax.axis_index("c")
    my_id = 4 * x + 2 * y + c
    copies = []
    for kk, p in enumerate(_flip_peers(x, y, c)):
        for a in range(len(in_refs)):
            src = in_refs[a] if gather else in_refs[a].at[4 * p[0] + 2 * p[1] + p[2]]
            copies.append(pltpu.make_async_remote_copy(
                src_ref=src, dst_ref=land_refs[a].at[my_id], send_sem=send_sems.at[a * 7 + kk], recv_sem=recv_sems.at[a * 7 + kk],
                device_id=p, device_id_type=pl.DeviceIdType.MESH))
    return copies


def _exchange_start(name, srcs, gather, after=None):
    n = len(srcs)
    first_out = 2 * n + (0 if after is None else 1)

    def body(*refs):
        for cp in _split_copies(refs[:n], refs[n:2 * n], refs[first_out], refs[first_out + 1], gather):
            cp.start()
        refs[-1][...] = jnp.zeros_like(refs[-1])

    hbm, sem = pl.BlockSpec(memory_space=pltpu.HBM), pl.BlockSpec(memory_space=pltpu.SEMAPHORE)
    land_shapes = [((N_DEV,) + s.shape if gather else s.shape, s.dtype) for s in srcs]
    lands = [pltpu.with_memory_space_constraint(lax.empty(shp, dt), pltpu.HBM) for shp, dt in land_shapes]
    srcs = [pltpu.with_memory_space_constraint(s, pltpu.HBM) for s in srcs]
    res = pl.pallas_call(
        body, name=name,
        out_shape=[pltpu.SemaphoreType.DMA((7 * n,)), pltpu.SemaphoreType.DMA((7 * n,))] + [pltpu.HBM(s.shape, s.dtype) for s in srcs]
        + [pltpu.HBM(shp, dt) for shp, dt in land_shapes] + [jax.ShapeDtypeStruct((8, LANES), F32)],
        in_specs=[hbm] * (2 * n) + ([] if after is None else [pl.BlockSpec(memory_space=pl.ANY)]),
        out_specs=[sem, sem] + [hbm] * (2 * n) + [pl.BlockSpec(memory_space=pltpu.VMEM)],
        input_output_aliases={i: 2 + i for i in range(2 * n)},
        compiler_params=pltpu.CompilerParams(has_side_effects=pltpu.SideEffectType.DATAFLOW_SIDE_EFFECTING),
    )(*srcs, *lands, *([] if after is None else [after]))
    return res[0], res[1], res[2:2 + n], res[2 + n:2 + 2 * n], res[-1]


def _exchange_wait(name, handles, after, gather):
    send_sems, recv_sems, srcs, lands, _ = handles
    n = len(srcs)

    def body(*refs):
        for cp in _split_copies(refs[:n], refs[n:2 * n], refs[2 * n], refs[2 * n + 1], gather):
            cp.wait_send()
            cp.wait_recv()

    hbm, sem = pl.BlockSpec(memory_space=pltpu.HBM), pl.BlockSpec(memory_space=pltpu.SEMAPHORE)
    res = pl.pallas_call(
        body, name=name, out_shape=[pltpu.HBM(t.shape, t.dtype) for t in list(srcs) + list(lands)],
        in_specs=[hbm] * (2 * n) + [sem, sem, pl.BlockSpec(memory_space=pl.ANY)], out_specs=[hbm] * (2 * n),
        input_output_aliases={i: i for i in range(2 * n)},
        compiler_params=pltpu.CompilerParams(has_side_effects=pltpu.SideEffectType.DATAFLOW_SIDE_EFFECTING),
    )(*srcs, *lands, send_sems, recv_sems, after)
    my_id = 4 * lax.axis_index("x") + 2 * lax.axis_index("y") + lax.axis_index("c")
    own = [s if gather else lax.dynamic_index_in_dim(s, my_id, 0, keepdims=False) for s in res[:n]]
    return [lax.dynamic_update_index_in_dim(land, o, my_id, 0) for land, o in zip(res[n:], own)]


def _adamw(name, parts, w, m, v):
    rows, cols = w.shape
    tr = _pick(rows, (128, 64, 32, 16, 8))
    pspec = pl.BlockSpec((N_DEV, tr, cols), lambda i: (0, i, 0))
    rspec = pl.BlockSpec((tr, cols), lambda i: (i, 0))

    def body(p_ref, w_ref, m_ref, v_ref, g_ref, d_ref, m2_ref, v2_ref):
        g, d, m2, v2 = _adamw_fn([p_ref[s] for s in range(N_DEV)], w_ref[...], m_ref[...], v_ref[...])
        g_ref[...], d_ref[...], m2_ref[...], v2_ref[...] = g, d, m2, v2

    return pl.pallas_call(
        body, name=name, grid=(rows // tr,), in_specs=[pspec, rspec, rspec, rspec], out_specs=[rspec] * 4,
        out_shape=[jax.ShapeDtypeStruct((rows, cols), F32)] * 4,
        compiler_params=pltpu.CompilerParams(dimension_semantics=("parallel",), vmem_limit_bytes=VMEM_LIMIT),
    )(parts, w, m, v)


def kernel(x, positions, g_mix, w_in, g_q_a, w_q_b, g_kv_a, w_kv_b, g_qn, g_kn, w_mla_out, ret_decay_fwd, ret_decay_bwd, w_ret_out, w_out, g_ffn, w_gate_up, w_down, loss_target, m_g_mix, m_w_in, m_g_q_a, m_w_q_b, m_g_kv_a, m_w_kv_b, m_g_qn, m_g_kn, m_w_mla_out, m_ret_decay_fwd, m_ret_decay_bwd, m_w_ret_out, m_w_out, m_g_ffn, m_w_gate_up, m_w_down, v_g_mix, v_w_in, v_g_q_a, v_w_q_b, v_g_kv_a, v_w_kv_b, v_g_qn, v_g_kn, v_w_mla_out, v_ret_decay_fwd, v_ret_decay_bwd, v_w_ret_out, v_w_out, v_g_ffn, v_w_gate_up, v_w_down):
    w = dict(g_mix=g_mix, w_in=w_in, g_q_a=g_q_a, w_q_b=w_q_b, g_kv_a=g_kv_a, w_kv_b=w_kv_b, g_qn=g_qn, g_kn=g_kn, w_mla_out=w_mla_out,
             ret_decay_fwd=ret_decay_fwd, ret_decay_bwd=ret_decay_bwd, w_ret_out=w_ret_out, w_out=w_out, g_ffn=g_ffn,
             w_gate_up=w_gate_up, w_down=w_down)
    m = dict(g_mix=m_g_mix, w_in=m_w_in, g_q_a=m_g_q_a, w_q_b=m_w_q_b, g_kv_a=m_g_kv_a, w_kv_b=m_w_kv_b, g_qn=m_g_qn, g_kn=m_g_kn,
             w_mla_out=m_w_mla_out, ret_decay_fwd=m_ret_decay_fwd, ret_decay_bwd=m_ret_decay_bwd, w_ret_out=m_w_ret_out, w_out=m_w_out,
             g_ffn=m_g_ffn, w_gate_up=m_w_gate_up, w_down=m_w_down)
    v = dict(g_mix=v_g_mix, w_in=v_w_in, g_q_a=v_g_q_a, w_q_b=v_w_q_b, g_kv_a=v_g_kv_a, w_kv_b=v_w_kv_b, g_qn=v_g_qn, g_kn=v_g_kn,
             w_mla_out=v_w_mla_out, ret_decay_fwd=v_ret_decay_fwd, ret_decay_bwd=v_ret_decay_bwd, w_ret_out=v_w_ret_out, w_out=v_w_out,
             g_ffn=v_g_ffn, w_gate_up=v_w_gate_up, w_down=v_w_down)
    gains = {n: w[n].reshape(1, ln) for n, ln in GAINS}

    axis_of = {n: axis for n, _, axis in MATS}
    later = [n for n, _, _ in MATS if n not in FIRST_WEIGHTS]
    gathered = _all_gather([w[n].astype(WIRE) for n in FIRST_WEIGHTS])
    W = {n: _unshard(g, axis_of[n]) for n, g in zip(FIRST_WEIGHTS, gathered)}
    later_handles = _exchange_start("gather_later_start", [w[n].astype(WIRE) for n in later], True, after=gathered[0])

    def late_weights(after):
        lands = _exchange_wait("gather_later_wait", later_handles, after, True)
        return {n: _unshard(g, axis_of[n]) for n, g in zip(later, lands)}

    grad_groups = []

    def grad_hook(g):
        names = tuple(g)
        handles = _exchange_start("grads_start_%d" % len(grad_groups), [_reshard(g[n], axis_of[n]).astype(GWIRE) for n in names], False)
        grad_groups.append((names, handles))
        return handles[4]

    S = x.shape[1]
    pos = positions.reshape(S, 1).astype(F32)
    loss_rows, grad_x, gG, gW = _local_step(x.reshape(S, D_MODEL), pos, loss_target.reshape(S, D_MODEL), gains, W, late_weights, grad_hook)
    loss = lax.psum(jnp.sum(loss_rows), ("x", "y", "c"))

    last = [n for n, _, _ in MATS if n not in EARLY_GRADS + MID_GRADS]
    pieces = [_reshard(gW[n], axis_of[n]).astype(GWIRE) for n in last]
    pieces.append(jnp.broadcast_to(_pack_gains(gG)[None], (N_DEV, 1, GAIN_PAD)))
    late_parts = _all_to_all(pieces)
    parts = dict(zip(last, late_parts))
    for i, (names, handles) in enumerate(grad_groups):
        parts.update(zip(names, _exchange_wait("grads_wait_%d" % i, handles, late_parts[-1], False)))
    out = [dict() for _ in range(4)]
    for n, _, _ in MATS:
        for o, r in zip(out, _adamw("adamw_" + n, parts[n], w[n], m[n], v[n])):
            o[n] = r
    for o, r in zip(out, _adamw("adamw_gains", late_parts[-1], _pack_gains(w), _pack_gains(m), _pack_gains(v))):
        o.update(_unpack_gains(r))
    return (loss, grad_x.reshape(x.shape), *[o[n] for o in out for n in ORDER])
```

```python
import functools

import numpy as np
import jax
import jax.numpy as jnp
from jax import lax
from jax.experimental import pallas as pl
from jax.experimental.pallas import tpu as pltpu

F32 = jnp.float32
MXU = jnp.bfloat16
WIRE = jnp.bfloat16
GWIRE = jnp.bfloat16

N_DEV = 8
D_MODEL = 1024
HEADS = 8
LANES = 128
Q_RANK, KV_RANK = 256, 128
NOPE, ROPE_M, V_M = 64, 32, 64
QK_M = NOPE + ROPE_M
RQK, RV = 64, 128
CHUNK = 128
FFN = 2816
IN_WIDTH = 5536
THETA = 10000.0
EPS = 1e-6
LR, B1, B2, AEPS, WD, STEP = 0.001, 0.9, 0.999, 1e-08, 0.01, 10
VMEM_LIMIT = 56 * 1024 * 1024

NN = ((1,), (0,))
NT = ((1,), (1,))
TN = ((0,), (0,))

P_GATES, P_VR, P_GR, P_QR, P_KR, P_CQ, P_CKV, P_KROPE, P_WIDTH = 0, 2048, 3072, 4096, 4608, 5120, 5376, 5504, 5632
O_CQ, O_CKV, O_KROPE, O_QR, O_KR, O_VR, O_GR, O_GATES = 0, 256, 384, 416, 928, 1440, 2464, 3488


def _dot(a, b, dims):
    return lax.dot_general(a, b, (dims, ((), ())), preferred_element_type=F32)


def _pick(dim, cands):
    for c in cands:
        if dim % c == 0:
            return c
    return dim


def _pairs(t):
    return t.reshape(t.shape[0], 4, 2, 2, 32).transpose(0, 1, 3, 2, 4).reshape(t.shape[0], 512)


def _win_pad(w):
    z = jnp.zeros((w.shape[0], 48), w.dtype)
    kr = w[:, O_KROPE:O_KROPE + 32]
    return jnp.concatenate([w[:, O_GATES:], w[:, O_VR:O_VR + 1024], w[:, O_GR:O_GR + 1024], _pairs(w[:, O_QR:O_QR + 512]),
                            _pairs(w[:, O_KR:O_KR + 512]), w[:, :O_CKV], w[:, O_CKV:O_KROPE], kr[:, :16], z, kr[:, 16:], z], axis=1)


def _win_unpad(g):
    return jnp.concatenate([g[:, P_CQ:P_CQ + 256], g[:, P_CKV:P_CKV + 128], g[:, P_KROPE:P_KROPE + 16], g[:, P_KROPE + 64:P_KROPE + 80],
                            _pairs(g[:, P_QR:P_QR + 512]), _pairs(g[:, P_KR:P_KR + 512]), g[:, P_VR:P_VR + 1024],
                            g[:, P_GR:P_GR + 1024], g[:, P_GATES:P_GATES + 2048]], axis=1)


def _qk_pad(t):
    z = jnp.zeros(t.shape[:-1] + (32,), t.dtype)
    return jnp.concatenate([t[..., 64:80], t[..., 0:48], t[..., 80:96], t[..., 48:64], z], axis=-1)


def _qk_unpad(p):
    return jnp.concatenate([p[..., 16:64], p[..., 80:96], p[..., 0:16], p[..., 64:80]], axis=-1)


def _wq_pad(w):
    return _qk_pad(w.reshape(Q_RANK, HEADS, QK_M)).reshape(Q_RANK, HEADS * LANES)


def _wq_unpad(g):
    return _qk_unpad(g.reshape(Q_RANK, HEADS, LANES)).reshape(Q_RANK, HEADS * QK_M)


def _wkv_pad(w):
    t = w.reshape(KV_RANK, HEADS, NOPE + V_M)
    z = lambda n: jnp.zeros((KV_RANK, HEADS, n), w.dtype)
    wk = jnp.concatenate([z(16), t[..., 0:48], z(16), t[..., 48:64], z(32)], axis=-1)
    wv = jnp.concatenate([t[..., 64:128], z(64)], axis=-1)
    return wk.reshape(KV_RANK, HEADS * LANES), wv.reshape(KV_RANK, HEADS * LANES)


def _wkv_unpad(dwk, dwv):
    k, v = dwk.reshape(KV_RANK, HEADS, LANES), dwv.reshape(KV_RANK, HEADS, LANES)
    return jnp.concatenate([k[..., 16:64], k[..., 80:96], v[..., 0:64]], axis=-1).reshape(KV_RANK, HEADS * (NOPE + V_M))


def _wmla_pad(w):
    t = w.reshape(HEADS, V_M, D_MODEL)
    return jnp.concatenate([t, jnp.zeros_like(t)], axis=1).reshape(HEADS * LANES, D_MODEL)


def _wmla_unpad(g):
    return g.reshape(HEADS, LANES, D_MODEL)[:, :V_M].reshape(HEADS * V_M, D_MODEL)


def _rowwise(name, fn, rows, ts, ins, outs, accs=(), ncol=1):
    n_in, n_out, n_acc = len(ins), len(outs), len(accs)

    def colmap(col):
        if callable(col):
            return lambda i, j: (i, col(j))
        return lambda i, j: (i, col)

    arrays, in_specs = [], []
    for arr, spec in ins:
        arrays.append(arr)
        if spec is None:
            in_specs.append(pl.BlockSpec(arr.shape, functools.partial(lambda i, j, nd: (0,) * nd, nd=arr.ndim)))
        else:
            in_specs.append(pl.BlockSpec((ts, spec[0]), colmap(spec[1])))
    out_shape, out_specs = [], []
    for total, dtype, width, col in outs:
        out_shape.append(jax.ShapeDtypeStruct((rows, total), dtype))
        out_specs.append(pl.BlockSpec((ts, width), colmap(col)))
    for shp in accs:
        out_shape.append(jax.ShapeDtypeStruct(shp, F32))
        out_specs.append(pl.BlockSpec(shp, functools.partial(lambda i, j, nd: (0,) * nd, nd=len(shp))))

    def body(*refs):
        vals = [r[...] for r in refs[:n_in]]
        res = fn(*vals)
        if not isinstance(res, (tuple, list)):
            res = (res,)
        for r, v in zip(refs[n_in:n_in + n_out], res[:n_out]):
            r[...] = v.astype(r.dtype)
        if n_acc:
            first = jnp.logical_and(pl.program_id(0) == 0, pl.program_id(1) == 0)
            for r, v in zip(refs[n_in + n_out:], res[n_out:]):
                @pl.when(first)
                def _(r=r):
                    r[...] = jnp.zeros_like(r)
                r[...] += v.astype(F32)

    res = pl.pallas_call(
        body, name=name, grid=(rows // ts, ncol), in_specs=in_specs, out_specs=out_specs, out_shape=out_shape,
        compiler_params=pltpu.CompilerParams(dimension_semantics=("arbitrary", "arbitrary"), vmem_limit_bytes=VMEM_LIMIT),
    )(*arrays)
    return res


MM_OPERAND_BYTES = 24 * 1024 * 1024


def _mm(name, a, b, mode, add=None, after=None):
    if mode == "nn":
        (M, K), N = a.shape, b.shape[1]
    elif mode == "nt":
        (M, K), N = a.shape, b.shape[0]
    else:
        (K, M), N = a.shape, b.shape[1]
    tm = _pick(M, (512, 1408, 256, 128)) if mode == "tn" else _pick(M, (1024, 512, 256, 128))
    tn = _pick(N, (512, 256, 128)) if mode == "tn" else _pick(N, (1408, 1024, 512, 256, 128))
    fits = lambda t: 2 * (tm + tn) * t * a.dtype.itemsize <= MM_OPERAND_BYTES
    tk = next(t for t in (K, 4096, 2816, 2048, 1408, 1024, 512, 256, 128) if K % t == 0 and (fits(t) or t == 128))
    nk = K // tk
    dims = {"nn": NN, "nt": NT, "tn": TN}[mode]
    a_spec = pl.BlockSpec((tk, tm), lambda i, j, k: (k, i)) if mode == "tn" else pl.BlockSpec((tm, tk), lambda i, j, k: (i, k))
    b_spec = pl.BlockSpec((tn, tk), lambda i, j, k: (j, k)) if mode == "nt" else pl.BlockSpec((tk, tn), lambda i, j, k: (k, j))
    o_spec = pl.BlockSpec((tm, tn), lambda i, j, k: (i, j))
    has_add = add is not None

    def body(*refs):
        a_ref, b_ref, o_ref = refs[0], refs[1], refs[-1]
        d = _dot(a_ref[...], b_ref[...], dims)
        first = (d + refs[2][...]) if has_add else d
        if nk == 1:
            o_ref[...] = first
        else:
            k = pl.program_id(2)

            @pl.when(k == 0)
            def _():
                o_ref[...] = first

            @pl.when(k > 0)
            def _():
                o_ref[...] += d

    args = [a, b] + ([add] if has_add else []) + ([] if after is None else [after])
    specs = [a_spec, b_spec] + ([o_spec] if has_add else []) + ([] if after is None else [pl.BlockSpec(memory_space=pl.ANY)])
    return pl.pallas_call(
        body, name=name, grid=(M // tm, N // tn, nk), in_specs=specs, out_specs=o_spec,
        out_shape=jax.ShapeDtypeStruct((M, N), F32),
        compiler_params=pltpu.CompilerParams(dimension_semantics=("parallel", "parallel", "arbitrary"), vmem_limit_bytes=VMEM_LIMIT),
    )(*args)


@jax.custom_vjp
def _swap64(x):
    return pltpu.roll(x, 64, 1)


_swap64.defvjp(lambda x: (_swap64(x), None), lambda _, g: (_swap64(g),))


@jax.custom_vjp
def _mxdot(a, b):
    return _dot(a.astype(MXU), b.astype(MXU), NN)


def _mxdot_bwd(res, g):
    a, b = res
    gb = g.astype(MXU)
    return _dot(gb, b.astype(MXU), NT), _dot(a.astype(MXU), gb, TN)


_mxdot.defvjp(lambda a, b: (_mxdot(a, b), (a, b)), _mxdot_bwd)


def _rms(x):
    return x * lax.rsqrt(jnp.mean(x * x, axis=-1, keepdims=True) + EPS)


def _rmsg_fn(x, g):
    return _rms(x) * g


def _silu(x):
    return x * jax.nn.sigmoid(x)


def _tables_fn(pos, inv_m, sgn_m, inv_r, sgn_r):
    am, ar = pos * inv_m, pos * inv_r
    return jnp.cos(am), jnp.sin(am) * sgn_m, jnp.cos(ar), jnp.sin(ar) * sgn_r


def _head_blocks(t):
    return [t[:, LANES * h:LANES * (h + 1)] for h in range(t.shape[1] // LANES)]


def _mla_prep_fn(cq, ckv, kr, cosm, sinm, gqa, gkva, gqn, gkn, wq, wk, wv):
    cqn = _rms(cq) * gqa
    ckvn = _rms(ckv) * gkva
    q_raw = _mxdot(cqn, wq)
    k_raw = _mxdot(ckvn, wk)
    lane = lax.broadcasted_iota(jnp.int32, (1, HEADS * LANES), 1)
    v = _mxdot(ckvn, wv) + (lane % LANES == V_M).astype(F32)

    def norm_rope(blocks, g, extra):
        outs = []
        for b in blocks:
            if extra is not None:
                b = b + extra
            n = b * lax.rsqrt(jnp.sum(b * b, axis=-1, keepdims=True) * (1.0 / QK_M) + EPS) * g
            outs.append(n * cosm + _swap64(n) * sinm)
        return jnp.concatenate(outs, axis=1)

    q = norm_rope(_head_blocks(q_raw), gqn, None)
    k = norm_rope(_head_blocks(k_raw), gkn, kr)
    return q, k, v


def _ret_prep_fn(qr, kr, cosr, sinr):
    def rope(t, scale):
        return jnp.concatenate([(b * cosr + _swap64(b) * sinr) * scale for b in _head_blocks(t)], axis=1)
    return rope(qr, 1.0), rope(kr, RQK ** -0.5)


def _ret_post_fn(rf, rb, gr):
    ret = rf + rb
    outs = []
    for b, g in zip(_head_blocks(ret), _head_blocks(gr)):
        outs.append(_silu(g) * _rms(b))
    return jnp.concatenate(outs, axis=1)


def _merge_fn(ga, gb, ya, yb):
    return jax.nn.sigmoid(ga) * ya + jax.nn.sigmoid(gb) * yb


def _swiglu_fn(gate, up):
    return _silu(gate) * up


def _loss_fn(x2, tgt):
    d = x2 - tgt
    return d * (1.0 / D_MODEL), 0.5 * jnp.sum(d * d, axis=0, keepdims=True) * (1.0 / D_MODEL)


def _adamw_fn(parts, w, m, v):
    g = parts[0].astype(F32)
    for p in range(1, N_DEV):
        g = g + parts[p].astype(F32)
    m2 = B1 * m + (1.0 - B1) * g
    v2 = B2 * v + (1.0 - B2) * jnp.square(g)
    m_hat = m2 / (1.0 - B1 ** STEP)
    v_hat = v2 / (1.0 - B2 ** STEP)
    delta = -LR * (m_hat / (jnp.sqrt(v_hat) + AEPS) + WD * w)
    return g, delta, m2, v2


SCALE = QK_M ** -0.5
LOG2E = 1.4426950408889634
FLASH_ROWS = 32


def _flash_fwd(q, k, v):
    S = q.shape[0]
    tq = tk = _pick(S, (512, 256, 128))
    ncb = tk // LANES
    nkv = S // tk
    assert nkv % 2 == 0, "kv tiles are processed in pairs"
    mrows = 64
    c = SCALE * LOG2E

    def body(q_ref, k_ref, v_ref, o_ref, obf_ref, lse_ref, s_a, p_a, s_b, p_b, m_sc, a_sc, acc_sc):
        m_sc[...] = jnp.full_like(m_sc, -jnp.inf)
        acc_sc[...] = jnp.zeros_like(acc_sc)
        qb = q_ref[...]

        def scores(j, s_buf):
            s_buf[...] = _dot(qb, k_ref[pl.ds(pl.multiple_of(j * tk, tk), tk), :], NT)

        def stage(j, s_buf, p_buf, s_next):
            scores(jnp.minimum(j + 1, nkv - 1), s_next)
            for r in range(tq // mrows):
                rows = slice(r * mrows, (r + 1) * mrows)
                cols = [s_buf[rows, LANES * cb:LANES * (cb + 1)] for cb in range(ncb)]
                m_prev = m_sc[rows, :]
                row_max = jnp.max(functools.reduce(jnp.maximum, cols), axis=-1, keepdims=True)
                m_new = jnp.maximum(m_prev, jnp.broadcast_to(row_max, (mrows, LANES)))
                a_sc[rows, :] = jnp.exp2((m_prev - m_new) * c)
                m_sc[rows, :] = m_new
                for cb in range(ncb):
                    p_buf[rows, LANES * cb:LANES * (cb + 1)] = jnp.exp2((cols[cb] - m_new) * c).astype(p_buf.dtype)
            acc_sc[...] = a_sc[...] * acc_sc[...] + _dot(p_buf[...], v_ref[pl.ds(pl.multiple_of(j * tk, tk), tk), :], NN)

        scores(0, s_a)

        def pair_step(t, carry):
            stage(2 * t, s_a, p_a, s_b)
            stage(2 * t + 1, s_b, p_b, s_a)
            return carry

        lax.fori_loop(0, nkv // 2, pair_step, 0, unroll=4)
        acc = acc_sc[...]
        lane = lax.broadcasted_iota(jnp.int32, (1, LANES), 1)
        l = jnp.sum(jnp.where(lane == V_M, acc, 0.0), axis=-1, keepdims=True)
        o = acc / l
        o_ref[...] = o
        obf_ref[...] = o.astype(obf_ref.dtype)
        lse_ref[...] = m_sc[...] * c + jnp.log2(jnp.broadcast_to(l, (tq, LANES)))

    qspec = pl.BlockSpec((tq, LANES), lambda h, i: (i, h))
    kspec = pl.BlockSpec((S, LANES), lambda h, i: (0, h))
    full = jax.ShapeDtypeStruct((S, HEADS * LANES), F32)
    return pl.pallas_call(
        body, name="flash_fwd", grid=(HEADS, S // tq), in_specs=[qspec, kspec, kspec], out_specs=[qspec, qspec, qspec],
        out_shape=[full, jax.ShapeDtypeStruct((S, HEADS * LANES), MXU), full],
        scratch_shapes=[pltpu.VMEM((tq, tk), F32), pltpu.VMEM((tq, tk), MXU)] * 2 + [pltpu.VMEM((tq, LANES), F32)] * 3,
        compiler_params=pltpu.CompilerParams(dimension_semantics=("parallel", "arbitrary"), vmem_limit_bytes=VMEM_LIMIT),
    )(q, k, v)


def _delta_fn(o, do):
    outs = [jnp.broadcast_to(jnp.sum(a * b, axis=-1, keepdims=True), a.shape) for a, b in zip(_head_blocks(o), _head_blocks(do))]
    return do, jnp.concatenate(outs, axis=1)


def _flash_bwd(q, k, v, do, lse, delta):
    S = q.shape[0]
    tq = tk = _pick(S, (512, 256, 128))
    ncb = tk // LANES
    c = SCALE * LOG2E

    nq = S // tq
    assert nq % 2 == 0, "q tiles are processed in pairs"

    def body(q_ref, k_ref, v_ref, do_ref, lse_ref, dl_ref, dq_ref, dk_ref, dv_ref, s_a, dp_a, p_a, ds_a, s_b, dp_b, p_b, ds_b, dk_sc, dv_sc):
        @pl.when(pl.program_id(1) == 0)
        def _():
            dq_ref[...] = jnp.zeros_like(dq_ref)

        dk_sc[...] = jnp.zeros_like(dk_sc)
        dv_sc[...] = jnp.zeros_like(dv_sc)
        kb, vb = k_ref[...], v_ref[...]

        def scores(i, s_buf, dp_buf):
            q_rows = pl.ds(pl.multiple_of(i * tq, tq), tq)
            s_buf[...] = _dot(q_ref[q_rows, :], kb, NT)
            dp_buf[...] = _dot(do_ref[q_rows, :], vb, NT)

        def stage(i, s_buf, dp_buf, p_buf, ds_buf, s_next, dp_next):
            scores(jnp.minimum(i + 1, nq - 1), s_next, dp_next)
            for r in range(tq // FLASH_ROWS):
                rows = slice(r * FLASH_ROWS, (r + 1) * FLASH_ROWS)
                grows = pl.ds(pl.multiple_of(i * tq + r * FLASH_ROWS, FLASH_ROWS), FLASH_ROWS)
                lse_b, dl_b = lse_ref[grows, :], dl_ref[grows, :]
                for cb in range(ncb):
                    sl = slice(LANES * cb, LANES * (cb + 1))
                    p = jnp.exp2(s_buf[rows, sl] * c - lse_b)
                    p_buf[rows, sl] = p.astype(p_buf.dtype)
                    ds_buf[rows, sl] = (p * (dp_buf[rows, sl] - dl_b) * SCALE).astype(ds_buf.dtype)
            q_rows = pl.ds(pl.multiple_of(i * tq, tq), tq)
            dv_sc[...] += _dot(p_buf[...], do_ref[q_rows, :], TN)
            dk_sc[...] += _dot(ds_buf[...], q_ref[q_rows, :], TN)
            dq_ref[q_rows, :] += _dot(ds_buf[...], kb, NN)

        scores(0, s_a, dp_a)

        def pair_step(t, carry):
            stage(2 * t, s_a, dp_a, p_a, ds_a, s_b, dp_b)
            stage(2 * t + 1, s_b, dp_b, p_b, ds_b, s_a, dp_a)
            return carry

        lax.fori_loop(0, nq // 2, pair_step, 0, unroll=2)
        dk_ref[...] = dk_sc[...]
        dv_ref[...] = dv_sc[...]

    hspec = pl.BlockSpec((S, LANES), lambda h, j: (0, h))
    kspec = pl.BlockSpec((tk, LANES), lambda h, j: (j, h))
    full = jax.ShapeDtypeStruct((S, HEADS * LANES), F32)
    tile_bufs = [pltpu.VMEM((tq, tk), F32), pltpu.VMEM((tq, tk), F32), pltpu.VMEM((tq, tk), MXU), pltpu.VMEM((tq, tk), MXU)]
    return pl.pallas_call(
        body, name="flash_bwd", grid=(HEADS, S // tk), in_specs=[hspec, kspec, kspec, hspec, hspec, hspec],
        out_specs=[hspec, kspec, kspec], out_shape=[full, full, full],
        scratch_shapes=tile_bufs + tile_bufs + [pltpu.VMEM((tk, LANES), F32), pltpu.VMEM((tk, LANES), F32)],
        compiler_params=pltpu.CompilerParams(dimension_semantics=("parallel", "arbitrary"), vmem_limit_bytes=VMEM_LIMIT),
    )(q, k, v, do, lse, delta)


def _ret_consts(lgh, head, rev):
    C = CHUNK
    lane = lax.broadcasted_iota(jnp.int32, (1, LANES), 1)
    hm = ((lane // 32) % 2 == head % 2).astype(F32)
    r = lax.broadcasted_iota(jnp.int32, (C, C), 0)
    c = lax.broadcasted_iota(jnp.int32, (C, C), 1)
    diff = ((c - r) if rev else (r - c)).astype(F32)
    mask = (diff > 0) if rev else (diff >= 0)
    dpos = jnp.maximum(diff, 0.0)
    din = jnp.where(mask, jnp.exp(lgh * dpos), 0.0)
    idx = lax.broadcasted_iota(jnp.int32, (C, 1), 0).astype(F32)
    eq = (C - idx) if rev else (idx + 1.0)
    ek = idx if rev else (C - 1.0 - idx)
    qd, kd = jnp.exp(lgh * eq), jnp.exp(lgh * ek)
    cd = jnp.exp(lgh * jnp.full((1, 1), float(C), F32))
    return hm, din, dpos, qd, kd, cd, eq, ek


RET_HEADS_PER_STEP = 4


def _ret_fwd(name, qt, kt, proj, lg, rev):
    S = qt.shape[0]
    C = CHUNK
    TB = _pick(S, (512, 256, 128))
    cb, nb = TB // C, S // TB
    hps = RET_HEADS_PER_STEP
    blk = (lambda g: nb - 1 - g) if rev else (lambda g: g)

    def body(lg_ref, q_ref, k_ref, v_ref, o_ref, st_ref, state_sc):
        hg, g = pl.program_id(0), pl.program_id(1)

        @pl.when(g == 0)
        def _():
            state_sc[...] = jnp.zeros_like(state_sc)

        consts = [_ret_consts(lg_ref[hg * hps + u], u, rev) for u in range(hps)]
        order = list(reversed(range(cb))) if rev else list(range(cb))
        units = [(cc, u) for cc in order for u in range(hps)]

        def operands(cc, u):
            rows = pl.ds(cc * C, C)
            pair = slice(LANES * (u // 2), LANES * (u // 2 + 1))
            hm = consts[u][0]
            return q_ref[rows, pair] * hm, k_ref[rows, pair] * hm, v_ref[rows, LANES * u:LANES * (u + 1)].astype(MXU)

        a, inc = {}, {}
        for cc, u in units:
            q, k, v = operands(cc, u)
            a[cc, u] = _dot(q.astype(MXU), k.astype(MXU), NT) * consts[u][1]
            inc[cc, u] = _dot((k * consts[u][4]).astype(MXU), v, TN)
        for u in range(hps):
            st = state_sc[u]
            for cc in order:
                st_ref[u, cc] = st
                st = st * consts[u][5] + inc[cc, u]
            state_sc[u] = st
        for cc, u in units:
            q, _, v = operands(cc, u)
            cross = _dot((q * consts[u][3]).astype(MXU), st_ref[u, cc].astype(MXU), NN)
            o_ref[pl.ds(cc * C, C), LANES * u:LANES * (u + 1)] = _dot(a[cc, u].astype(MXU), v, NN) + cross

    qk_spec = pl.BlockSpec((TB, LANES * hps // 2), lambda h, g: (blk(g), h))
    return pl.pallas_call(
        body, name=name, grid=(HEADS // hps, nb),
        in_specs=[pl.BlockSpec(memory_space=pltpu.SMEM), qk_spec, qk_spec,
                  pl.BlockSpec((TB, LANES * hps), lambda h, g: (blk(g), P_VR // (LANES * hps) + h))],
        out_specs=[pl.BlockSpec((TB, LANES * hps), lambda h, g: (blk(g), h)),
                   pl.BlockSpec((hps, cb, LANES, LANES), lambda h, g: (h, blk(g), 0, 0))],
        out_shape=[jax.ShapeDtypeStruct((S, HEADS * LANES), F32), jax.ShapeDtypeStruct((HEADS, S // C, LANES, LANES), F32)],
        scratch_shapes=[pltpu.VMEM((hps, LANES, LANES), F32)],
        compiler_params=pltpu.CompilerParams(dimension_semantics=("parallel", "arbitrary"), vmem_limit_bytes=VMEM_LIMIT),
    )(lg, qt, kt, proj)


def _ret_bwd(name, qt, kt, proj, dret, states, lg, rev):
    S = qt.shape[0]
    C = CHUNK
    TB = _pick(S, (512, 256, 128))
    cb, nb = TB // C, S // TB
    hps = RET_HEADS_PER_STEP
    blk = (lambda g: g) if rev else (lambda g: nb - 1 - g)

    def body(lg_ref, q_ref, k_ref, v_ref, do_ref, st_ref, dq_ref, dk_ref, dv_ref, dlg_ref, ds_sc, acc_cc, acc_q, acc_k, acc_s):
        hg, g = pl.program_id(0), pl.program_id(1)

        @pl.when(g == 0)
        def _():
            ds_sc[...] = jnp.zeros_like(ds_sc)
            acc_cc[...] = jnp.zeros_like(acc_cc)
            acc_q[...] = jnp.zeros_like(acc_q)
            acc_k[...] = jnp.zeros_like(acc_k)
            acc_s[...] = jnp.zeros_like(acc_s)

        lgs = [lg_ref[hg * hps + u] for u in range(hps)]
        consts = [_ret_consts(lgs[u], u, rev) for u in range(hps)]
        order = list(range(cb)) if rev else list(reversed(range(cb)))
        units = [(cc, u) for cc in order for u in range(hps)]

        def operands(cc, u):
            rows = pl.ds(cc * C, C)
            pair = slice(LANES * (u // 2), LANES * (u // 2 + 1))
            head = slice(LANES * u, LANES * (u + 1))
            hm = consts[u][0]
            return q_ref[rows, pair] * hm, k_ref[rows, pair] * hm, v_ref[rows, head].astype(MXU), do_ref[rows, head].astype(MXU)

        a, dp, dqs, inc = {}, {}, {}, {}
        for cc, u in units:
            q, k, vb, dob = operands(cc, u)
            a[cc, u] = _dot(q.astype(MXU), k.astype(MXU), NT)
            dp[cc, u] = _dot(dob, vb, NT)
            dqs[cc, u] = _dot(dob, st_ref[u, cc].astype(MXU), NT)
            inc[cc, u] = _dot((q * consts[u][3]).astype(MXU), dob, TN)
        dsn = {}
        for u in range(hps):
            ds = ds_sc[u]
            for cc in order:
                dsn[cc, u] = ds
                ds = ds * consts[u][5] + inc[cc, u]
            ds_sc[u] = ds
        even = {}
        for cc, u in units:
            hm, din, dpos, qd, kd, cd, eq, ek = consts[u]
            rows, head = pl.ds(cc * C, C), slice(LANES * u, LANES * (u + 1))
            q, k, vb, dob = operands(cc, u)
            qb, kb = q.astype(MXU), k.astype(MXU)
            dsnb = dsn[cc, u].astype(MXU)
            da = (dp[cc, u] * din).astype(MXU)
            vds = _dot(vb, dsnb, NT)
            dq_u = (_dot(da, kb, NN) + dqs[cc, u] * qd) * hm
            dk_u = (_dot(da, qb, TN) + vds * kd) * hm
            if u % 2 == 0:
                even[cc] = (dq_u, dk_u)
            else:
                pair = slice(LANES * (u // 2), LANES * (u // 2 + 1))
                dq_ref[rows, pair] = even[cc][0] + dq_u
                dk_ref[rows, pair] = even[cc][1] + dk_u
            dv_ref[rows, head] = _dot((a[cc, u] * din).astype(MXU), dob, TN) + _dot((k * kd).astype(MXU), dsnb, NN)
            acc_cc[u] += dp[cc, u] * a[cc, u] * din * dpos
            acc_q[u] += dqs[cc, u] * q * (qd * eq)
            acc_k[u] += vds * k * (kd * ek)
            acc_s[u] += dsn[cc, u] * st_ref[u, cc] * (cd * float(C))

        @pl.when(g == nb - 1)
        def _():
            for u in range(hps):
                tot = (jnp.sum(acc_cc[u], keepdims=True) + jnp.sum(acc_q[u], keepdims=True)
                       + jnp.sum(acc_k[u], keepdims=True) + jnp.sum(acc_s[u], keepdims=True))
                dlg_ref[u] = jnp.broadcast_to(tot * lgs[u], (8, LANES))

    full = jax.ShapeDtypeStruct((S, HEADS * LANES), F32)
    hspec = pl.BlockSpec((TB, LANES * hps), lambda h, g: (blk(g), h))
    qk_spec = pl.BlockSpec((TB, LANES * hps // 2), lambda h, g: (blk(g), h))
    return pl.pallas_call(
        body, name=name, grid=(HEADS // hps, nb),
        in_specs=[pl.BlockSpec(memory_space=pltpu.SMEM), qk_spec, qk_spec,
                  pl.BlockSpec((TB, LANES * hps), lambda h, g: (blk(g), P_VR // (LANES * hps) + h)),
                  hspec,
                  pl.BlockSpec((hps, cb, LANES, LANES), lambda h, g: (h, blk(g), 0, 0))],
        out_specs=[qk_spec, qk_spec, hspec, pl.BlockSpec((hps, 8, LANES), lambda h, g: (h, 0, 0))],
        out_shape=[jax.ShapeDtypeStruct(qt.shape, F32), jax.ShapeDtypeStruct(kt.shape, F32), full,
                   jax.ShapeDtypeStruct((HEADS, 8, LANES), F32)],
        scratch_shapes=[pltpu.VMEM((hps, LANES, LANES), F32), pltpu.VMEM((hps, C, C), F32), pltpu.VMEM((hps, C, LANES), F32),
                        pltpu.VMEM((hps, C, LANES), F32), pltpu.VMEM((hps, LANES, LANES), F32)],
        compiler_params=pltpu.CompilerParams(dimension_semantics=("parallel", "arbitrary"), vmem_limit_bytes=VMEM_LIMIT),
    )(lg, qt, kt, proj, dret, states)


def _rope_consts():
    inv16 = THETA ** (-jnp.arange(16, dtype=F32) / 16)
    inv32 = THETA ** (-jnp.arange(32, dtype=F32) / 32)
    lane = np.arange(LANES)
    z48 = jnp.zeros((48,), F32)
    inv_m = jnp.concatenate([inv16, z48, inv16, z48])[None, :]
    sgn_m = jnp.asarray(np.where(lane < 16, -1.0, np.where((lane >= 64) & (lane < 80), 1.0, 0.0)), F32)[None, :]
    inv_r = jnp.concatenate([inv32] * 4)[None, :]
    sgn_r = jnp.asarray(np.where(lane < 64, -1.0, 1.0), F32)[None, :]
    return inv_m, sgn_m, inv_r, sgn_r


FIRST_WEIGHTS = ("w_in", "w_q_b", "w_kv_b")
EARLY_GRADS = ("w_down", "w_gate_up", "w_out", "w_ret_out")
MID_GRADS = ("w_mla_out", "w_in")


def _local_step(x, pos, tgt, gains, W, late_weights=None, grad_hook=None):
    S = x.shape[0]
    ts = _pick(S, (256, 128))
    ts_wide = _pick(S, (128,))
    R = lambda a, w=None, c=0: (a, ((a.shape[1] if w is None else w), c))
    W_ = lambda a: (a, None)

    win = _win_pad(W["w_in"])
    wq = _wq_pad(W["w_q_b"])
    wk, wv = _wkv_pad(W["w_kv_b"])
    gqn, gkn = _qk_pad(gains["g_qn"]), _qk_pad(gains["g_kn"])
    g_mix, g_q_a, g_kv_a, g_ffn = gains["g_mix"], gains["g_q_a"], gains["g_kv_a"], gains["g_ffn"]
    lg_f = -jnp.exp(gains["ret_decay_fwd"][0])
    lg_b = -jnp.exp(gains["ret_decay_bwd"][0])

    consts = list(_rope_consts())
    cosm, sinm, cosr, sinr = _rowwise("rope_tables", _tables_fn, S, ts, [R(pos)] + [W_(c) for c in consts],
                                      [(LANES, F32, LANES, 0)] * 4)

    (h,) = _rowwise("rms_mix", _rmsg_fn, S, ts, [R(x), W_(g_mix)], [(D_MODEL, MXU, D_MODEL, 0)])
    proj = _mm("in_proj", h, win, "nn")
    seg = lambda off, w: (proj, (w, off // w))
    mla_ins = [seg(P_CQ, 256), seg(P_CKV, 128), seg(P_KROPE, 128), R(cosm), R(sinm),
               W_(g_q_a), W_(g_kv_a), W_(gqn), W_(gkn), W_(wq), W_(wk), W_(wv)]
    q, k, v = _rowwise("mla_prep", _mla_prep_fn, S, ts, mla_ins, [(HEADS * LANES, MXU, HEADS * LANES, 0)] * 3)
    o, o_bf, lse = _flash_fwd(q, k, v)
    if late_weights is not None:
        W = {**W, **late_weights(lse)}
    wmla = _wmla_pad(W["w_mla_out"])
    wret, wout, wgu, wdown = W["w_ret_out"], W["w_out"], W["w_gate_up"], W["w_down"]
    y_a = _mm("mla_out", o_bf, wmla, "nn")

    ret_ins = [seg(P_QR, 512), seg(P_KR, 512), R(cosr), R(sinr)]
    qt, kt = _rowwise("ret_prep", _ret_prep_fn, S, ts, ret_ins, [(512, F32, 512, 0)] * 2)
    ret_f, st_f = _ret_fwd("ret_fwd_f", qt, kt, proj, lg_f, False)
    ret_b, st_b = _ret_fwd("ret_fwd_b", qt, kt, proj, lg_b, True)
    post_ins = [R(ret_f), R(ret_b), seg(P_GR, 1024)]
    (o_b,) = _rowwise("ret_post", _ret_post_fn, S, ts, post_ins, [(1024, MXU, 1024, 0)])
    y_b = _mm("ret_out", o_b, wret, "nn")

    merge_ins = [seg(P_GATES, 1024), (proj, (1024, 1)), R(y_a), R(y_b)]
    (merged,) = _rowwise("merge", _merge_fn, S, ts, merge_ins, [(D_MODEL, MXU, D_MODEL, 0)])
    x1 = _mm("out_proj", merged, wout, "nn", add=x)
    (h2,) = _rowwise("rms_ffn", _rmsg_fn, S, ts, [R(x1), W_(g_ffn)], [(D_MODEL, MXU, D_MODEL, 0)])
    gu = _mm("gate_up", h2, wgu, "nn")
    (act,) = _rowwise("swiglu", lambda t: _swiglu_fn(t[:, :FFN], t[:, FFN:]), S, ts_wide, [R(gu)], [(FFN, MXU, FFN, 0)])
    x2 = _mm("down_proj", act, wdown, "nn", add=x1)
    dx2, dx2_bf, loss_rows = _rowwise("loss", lambda a, b: (lambda d, l: (d, d, l))(*_loss_fn(a, b)), S, ts, [R(x2), R(tgt)],
                                      [(D_MODEL, F32, D_MODEL, 0), (D_MODEL, MXU, D_MODEL, 0)], accs=[(1, D_MODEL)])

    gW = {}
    gW["w_down"] = _mm("d_w_down", act, dx2_bf, "tn")
    dact = _mm("d_act", dx2_bf, wdown, "nt")

    def glu_bwd(t, da):
        _, vjp = jax.vjp(_swiglu_fn, t[:, :FFN], t[:, FFN:])
        return jnp.concatenate(vjp(da), axis=1)

    (dgu,) = _rowwise("swiglu_bwd", glu_bwd, S, ts_wide, [R(gu), R(dact)], [(2 * FFN, MXU, 2 * FFN, 0)])
    gW["w_gate_up"] = _mm("d_w_gate_up", h2, dgu, "tn")
    dh2 = _mm("d_h2", dgu, wgu, "nt")

    def rms_bwd(xx, g, dh, dres):
        _, vjp = jax.vjp(_rmsg_fn, xx, g)
        dx, dg = vjp(dh)
        dx = dx + dres
        return dx, dx, dg

    dx1, dx1_bf, dg_ffn = _rowwise("rms_ffn_bwd", rms_bwd, S, ts, [R(x1), W_(g_ffn), R(dh2), R(dx2)],
                                   [(D_MODEL, F32, D_MODEL, 0), (D_MODEL, MXU, D_MODEL, 0)], accs=[(1, D_MODEL)])
    gW["w_out"] = _mm("d_w_out", merged, dx1_bf, "tn")
    dmerged = _mm("d_merged", dx1_bf, wout, "nt")

    def merge_bwd(ga, gb, ya, yb, dm):
        _, vjp = jax.vjp(_merge_fn, ga, gb, ya, yb)
        return vjp(dm)

    dga, dgb, dy_a, dy_b = _rowwise("merge_bwd", merge_bwd, S, ts, merge_ins + [R(dmerged)], [(D_MODEL, MXU, D_MODEL, 0)] * 4)
    gW["w_ret_out"] = _mm("d_w_ret_out", o_b, dy_b, "tn")
    after_early = [] if grad_hook is None else [W_(grad_hook({n: gW[n] for n in EARLY_GRADS}))]
    do_b = _mm("d_o_b", dy_b, wret, "nt")

    def post_bwd(rf, rb, gr, dob, *_):
        _, vjp = jax.vjp(_ret_post_fn, rf, rb, gr)
        drf, _, dgr = vjp(dob)
        return drf, dgr

    dret, dg_r = _rowwise("ret_post_bwd", post_bwd, S, ts, post_ins + [R(do_b)] + after_early, [(1024, F32, 1024, 0), (1024, MXU, 1024, 0)])
    dq_f, dk_f, dv_f, dlg_f = _ret_bwd("ret_bwd_f", qt, kt, proj, dret, st_f, lg_f, False)
    dq_b, dk_b, dv_b, dlg_b = _ret_bwd("ret_bwd_b", qt, kt, proj, dret, st_b, lg_b, True)

    def ret_prep_bwd(qr, kr, cosr_, sinr_, dqf, dqb, dkf, dkb, dvf, dvb):
        _, vjp = jax.vjp(lambda a, b: _ret_prep_fn(a, b, cosr_, sinr_), qr, kr)
        dqr, dkr = vjp((dqf + dqb, dkf + dkb))
        return dqr, dkr, dvf + dvb

    dq_r, dk_r, dv_r = _rowwise("ret_prep_bwd", ret_prep_bwd, S, ts, ret_ins + [R(t) for t in (dq_f, dq_b, dk_f, dk_b, dv_f, dv_b)],
                                [(512, MXU, 512, 0), (512, MXU, 512, 0), (1024, MXU, 1024, 0)])

    gW_mla_p = _mm("d_w_mla_out", o_bf, dy_a, "tn")
    do = _mm("d_o", dy_a, wmla, "nt")
    do_bf, delta = _rowwise("attn_delta", lambda a, b, *_: _delta_fn(a, b), S, ts, [R(o), R(do)] + after_early,
                            [(HEADS * LANES, MXU, HEADS * LANES, 0), (HEADS * LANES, F32, HEADS * LANES, 0)])
    dq, dk, dv = _flash_bwd(q, k, v, do_bf, lse, delta)

    def mla_prep_bwd(cq, ckv, kr, cosm_, sinm_, gqa, gkva, gqn_, gkn_, wq_, wk_, wv_, dq_, dk_, dv_):
        f = lambda cq, ckv, kr, gqa, gkva, gqn_, gkn_, wq_, wk_, wv_: _mla_prep_fn(cq, ckv, kr, cosm_, sinm_, gqa, gkva, gqn_, gkn_, wq_, wk_, wv_)
        _, vjp = jax.vjp(f, cq, ckv, kr, gqa, gkva, gqn_, gkn_, wq_.astype(F32), wk_.astype(F32), wv_.astype(F32))
        return vjp((dq_, dk_, dv_))

    mb = _rowwise("mla_prep_bwd", mla_prep_bwd, S, ts, mla_ins + [R(dq), R(dk), R(dv)],
                  [(256, MXU, 256, 0), (128, MXU, 128, 0), (128, MXU, 128, 0)],
                  accs=[(1, 256), (1, 128), (1, LANES), (1, LANES), (256, HEADS * LANES), (128, HEADS * LANES), (128, HEADS * LANES)])
    dc_q, dc_kv, dk_rope, dg_q_a, dg_kv_a, dgqn_p, dgkn_p, dwq_p, dwk_p, dwv_p = mb

    dproj = jnp.concatenate([dga, dgb, dv_r, dg_r, dq_r, dk_r, dc_q, dc_kv, dk_rope], axis=1)
    gW["w_in"] = _win_unpad(_mm("d_w_in", h, dproj, "tn"))
    gW["w_mla_out"] = _wmla_unpad(gW_mla_p)
    after_mid = None if grad_hook is None else grad_hook({n: gW[n] for n in MID_GRADS})
    dh = _mm("d_h", dproj, win, "nt", after=after_mid)
    grad_x, _, dg_mix = _rowwise("rms_mix_bwd", lambda a, b, c, d, *_: rms_bwd(a, b, c, d), S, ts,
                                 [R(x), W_(g_mix), R(dh), R(dx1)] + ([] if after_mid is None else [W_(after_mid)]),
                                 [(D_MODEL, F32, D_MODEL, 0), (D_MODEL, MXU, D_MODEL, 0)], accs=[(1, D_MODEL)])
    gW["w_q_b"] = _wq_unpad(dwq_p)
    gW["w_kv_b"] = _wkv_unpad(dwk_p, dwv_p)
    gG = {"g_mix": dg_mix, "g_q_a": dg_q_a, "g_kv_a": dg_kv_a, "g_qn": _qk_unpad(dgqn_p),
          "g_kn": _qk_unpad(dgkn_p), "ret_decay_fwd": dlg_f[:, 0, 0][None, :], "ret_decay_bwd": dlg_b[:, 0, 0][None, :],
          "g_ffn": dg_ffn}
    return loss_rows, grad_x, gG, gW


MATS = [("w_in", (1024, 5536), 1), ("w_q_b", (256, 768), 1), ("w_kv_b", (128, 1024), 1), ("w_mla_out", (512, 1024), 1),
        ("w_ret_out", (1024, 1024), 0), ("w_out", (1024, 1024), 0), ("w_gate_up", (1024, 5632), 1), ("w_down", (2816, 1024), 0)]
GAINS = [("g_mix", 1024), ("g_q_a", 256), ("g_kv_a", 128), ("g_qn", 96), ("g_kn", 96), ("ret_decay_fwd", 8), ("ret_decay_bwd", 8),
         ("g_ffn", 1024)]
ORDER = ["g_mix", "w_in", "g_q_a", "w_q_b", "g_kv_a", "w_kv_b", "g_qn", "g_kn", "w_mla_out", "ret_decay_fwd", "ret_decay_bwd",
         "w_ret_out", "w_out", "g_ffn", "w_gate_up", "w_down"]
GAIN_LEN = sum(n for _, n in GAINS)
GAIN_PAD = -(-GAIN_LEN // LANES) * LANES


def _pack_gains(d):
    row = jnp.concatenate([d[n].reshape(1, ln).astype(F32) for n, ln in GAINS], axis=1)
    return jnp.pad(row, ((0, 0), (0, GAIN_PAD - GAIN_LEN)))


def _unpack_gains(row):
    out, off = {}, 0
    for n, ln in GAINS:
        out[n] = row[0, off:off + ln]
        off += ln
    return out


def _unshard(pieces, axis):
    if axis == 0:
        return pieces.reshape((N_DEV * pieces.shape[1], pieces.shape[2]))
    return jnp.concatenate([pieces[p] for p in range(N_DEV)], axis=1)


def _reshard(full, axis):
    if axis == 0:
        return full.reshape((N_DEV, full.shape[0] // N_DEV, full.shape[1]))
    c = full.shape[1] // N_DEV
    return jnp.stack([full[:, c * p:c * (p + 1)] for p in range(N_DEV)])


def _all_gather(shards):
    n = len(shards)

    def body(*refs):
        x_refs, out_refs = refs[:n], refs[n:2 * n]
        send_sems, recv_sems, local_sems = refs[2 * n:]
        x, y, c = lax.axis_index("x"), lax.axis_index("y"), lax.axis_index("c")
        me, sibling = (x, y, c), (x, y, 1 - c)
        chips = [(1 - x, y), (x, 1 - y), (1 - x, 1 - y)]

        def slot(a, px, py, pc):
            return out_refs[a].at[4 * px + 2 * py + pc]

        def copy(a, k, block, to, from_input=False):
            return pltpu.make_async_remote_copy(
                src_ref=x_refs[a] if from_input else slot(a, *block), dst_ref=slot(a, *block),
                send_sem=send_sems.at[a, k], recv_sem=recv_sems.at[a, k], device_id=to, device_id_type=pl.DeviceIdType.MESH)

        mine = [pltpu.make_async_copy(x_refs[a], slot(a, *me), local_sems.at[a]) for a in range(n)]
        first = [copy(a, 0, me, sibling, True) for a in range(n)]
        first += [copy(a, 1 + j, me, (*chip, c), True) for j, chip in enumerate(chips) for a in range(n)]
        for cp in mine + first:
            cp.start()
        passed = []
        for j, chip in enumerate(chips):
            for a in range(n):
                copy(a, 1 + j, (*chip, c), me).wait_recv()
                passed.append(copy(a, 4 + j, (*chip, c), sibling))
                passed[-1].start()
        for a in range(n):
            copy(a, 0, sibling, me).wait_recv()
        for j, chip in enumerate(chips):
            for a in range(n):
                copy(a, 4 + j, (*chip, 1 - c), me).wait_recv()
        for cp in first + passed:
            cp.wait_send()
        for cp in mine:
            cp.wait()

    any_spec = pl.BlockSpec(memory_space=pl.ANY)
    return pl.pallas_call(
        body, name="all_gather_weights", out_shape=[jax.ShapeDtypeStruct((N_DEV,) + s.shape, s.dtype) for s in shards],
        in_specs=[any_spec] * n, out_specs=[any_spec] * n,
        scratch_shapes=[pltpu.SemaphoreType.DMA((n, 7)), pltpu.SemaphoreType.DMA((n, 7)), pltpu.SemaphoreType.DMA((n,))],
    )(*shards)


def _all_to_all(pieces):
    n = len(pieces)

    def body(*refs):
        in_refs, out_refs = refs[:n], refs[n:2 * n]
        send_sems, recv_sems, local_sems = refs[2 * n:]
        x, y, c = lax.axis_index("x"), lax.axis_index("y"), lax.axis_index("c")
        my_id = 4 * x + 2 * y + c
        flips = [(fx, fy, fc) for fx in (0, 1) for fy in (0, 1) for fc in (0, 1)][1:]

        def copy(a, kk, f):
            p = (x ^ f[0], y ^ f[1], c ^ f[2])
            return pltpu.make_async_remote_copy(
                src_ref=in_refs[a].at[4 * p[0] + 2 * p[1] + p[2]], dst_ref=out_refs[a].at[my_id],
                send_sem=send_sems.at[a, kk], recv_sem=recv_sems.at[a, kk], device_id=p, device_id_type=pl.DeviceIdType.MESH)

        mine = [pltpu.make_async_copy(in_refs[a].at[my_id], out_refs[a].at[my_id], local_sems.at[a]) for a in range(n)]
        copies = [copy(a, kk, f) for kk, f in enumerate(flips) for a in range(n)]
        for cp in mine + copies:
            cp.start()
        for cp in copies:
            cp.wait_recv()
        for cp in copies:
            cp.wait_send()
        for cp in mine:
            cp.wait()

    any_spec = pl.BlockSpec(memory_space=pl.ANY)
    return pl.pallas_call(
        body, name="all_to_all_grads", out_shape=[jax.ShapeDtypeStruct(p.shape, p.dtype) for p in pieces],
        in_specs=[any_spec] * n, out_specs=[any_spec] * n,
        scratch_shapes=[pltpu.SemaphoreType.DMA((n, 7)), pltpu.SemaphoreType.DMA((n, 7)), pltpu.SemaphoreType.DMA((n,))],
    )(*pieces)


def _flip_peers(x, y, c):
    flips = [(fx, fy, fc) for fx in (0, 1) for fy in (0, 1) for fc in (0, 1)][1:]
    return [(x ^ fx, y ^ fy, c ^ fc) for fx, fy, fc in flips]


def _split_copies(in_refs, land_refs, send_sems, recv_sems, gather):
    x, y, c = lax.axis_index("x"), lax.axis_index("y"), lax.axis_index("c")
    my_id = 4 * x + 2 * y + c
    copies = []
    for kk, p in enumerate(_flip_peers(x, y, c)):
        for a in range(len(in_refs)):
            src = in_refs[a] if gather else in_refs[a].at[4 * p[0] + 2 * p[1] + p[2]]
            copies.append(pltpu.make_async_remote_copy(
                src_ref=src, dst_ref=land_refs[a].at[my_id], send_sem=send_sems.at[a * 7 + kk], recv_sem=recv_sems.at[a * 7 + kk],
                device_id=p, device_id_type=pl.DeviceIdType.MESH))
    return copies


def _exchange_start(name, srcs, gather, after=None):
    n = len(srcs)
    first_out = 2 * n + (0 if after is None else 1)

    def body(*refs):
        for cp in _split_copies(refs[:n], refs[n:2 * n], refs[first_out], refs[first_out + 1], gather):
            cp.start()
        refs[-1][...] = jnp.zeros_like(refs[-1])

    hbm, sem = pl.BlockSpec(memory_space=pltpu.HBM), pl.BlockSpec(memory_space=pltpu.SEMAPHORE)
    land_shapes = [((N_DEV,) + s.shape if gather else s.shape, s.dtype) for s in srcs]
    lands = [pltpu.with_memory_space_constraint(lax.empty(shp, dt), pltpu.HBM) for shp, dt in land_shapes]
    srcs = [pltpu.with_memory_space_constraint(s, pltpu.HBM) for s in srcs]
    res = pl.pallas_call(
        body, name=name,
        out_shape=[pltpu.SemaphoreType.DMA((7 * n,)), pltpu.SemaphoreType.DMA((7 * n,))] + [pltpu.HBM(s.shape, s.dtype) for s in srcs]
        + [pltpu.HBM(shp, dt) for shp, dt in land_shapes] + [jax.ShapeDtypeStruct((8, LANES), F32)],
        in_specs=[hbm] * (2 * n) + ([] if after is None else [pl.BlockSpec(memory_space=pl.ANY)]),
        out_specs=[sem, sem] + [hbm] * (2 * n) + [pl.BlockSpec(memory_space=pltpu.VMEM)],
        input_output_aliases={i: 2 + i for i in range(2 * n)},
        compiler_params=pltpu.CompilerParams(has_side_effects=pltpu.SideEffectType.DATAFLOW_SIDE_EFFECTING),
    )(*srcs, *lands, *([] if after is None else [after]))
    return res[0], res[1], res[2:2 + n], res[2 + n:2 + 2 * n], res[-1]


def _exchange_wait(name, handles, after, gather):
    send_sems, recv_sems, srcs, lands, _ = handles
    n = len(srcs)

    def body(*refs):
        for cp in _split_copies(refs[:n], refs[n:2 * n], refs[2 * n], refs[2 * n + 1], gather):
            cp.wait_send()
            cp.wait_recv()

    hbm, sem = pl.BlockSpec(memory_space=pltpu.HBM), pl.BlockSpec(memory_space=pltpu.SEMAPHORE)
    res = pl.pallas_call(
        body, name=name, out_shape=[pltpu.HBM(t.shape, t.dtype) for t in list(srcs) + list(lands)],
        in_specs=[hbm] * (2 * n) + [sem, sem, pl.BlockSpec(memory_space=pl.ANY)], out_specs=[hbm] * (2 * n),
        input_output_aliases={i: i for i in range(2 * n)},
        compiler_params=pltpu.CompilerParams(has_side_effects=pltpu.SideEffectType.DATAFLOW_SIDE_EFFECTING),
    )(*srcs, *lands, send_sems, recv_sems, after)
    my_id = 4 * lax.axis_index("x") + 2 * lax.axis_index("y") + lax.axis_index("c")
    own = [s if gather else lax.dynamic_index_in_dim(s, my_id, 0, keepdims=False) for s in res[:n]]
    return [lax.dynamic_update_index_in_dim(land, o, my_id, 0) for land, o in zip(res[n:], own)]


def _adamw(name, parts, w, m, v):
    rows, cols = w.shape
    tr = _pick(rows, (128, 64, 32, 16, 8))
    pspec = pl.BlockSpec((N_DEV, tr, cols), lambda i: (0, i, 0))
    rspec = pl.BlockSpec((tr, cols), lambda i: (i, 0))

    def body(p_ref, w_ref, m_ref, v_ref, g_ref, d_ref, m2_ref, v2_ref):
        g, d, m2, v2 = _adamw_fn([p_ref[s] for s in range(N_DEV)], w_ref[...], m_ref[...], v_ref[...])
        g_ref[...], d_ref[...], m2_ref[...], v2_ref[...] = g, d, m2, v2

    return pl.pallas_call(
        body, name=name, grid=(rows // tr,), in_specs=[pspec, rspec, rspec, rspec], out_specs=[rspec] * 4,
        out_shape=[jax.ShapeDtypeStruct((rows, cols), F32)] * 4,
        compiler_params=pltpu.CompilerParams(dimension_semantics=("parallel",), vmem_limit_bytes=VMEM_LIMIT),
    )(parts, w, m, v)


def kernel(x, positions, g_mix, w_in, g_q_a, w_q_b, g_kv_a, w_kv_b, g_qn, g_kn, w_mla_out, ret_decay_fwd, ret_decay_bwd, w_ret_out, w_out, g_ffn, w_gate_up, w_down, loss_target, m_g_mix, m_w_in, m_g_q_a, m_w_q_b, m_g_kv_a, m_w_kv_b, m_g_qn, m_g_kn, m_w_mla_out, m_ret_decay_fwd, m_ret_decay_bwd, m_w_ret_out, m_w_out, m_g_ffn, m_w_gate_up, m_w_down, v_g_mix, v_w_in, v_g_q_a, v_w_q_b, v_g_kv_a, v_w_kv_b, v_g_qn, v_g_kn, v_w_mla_out, v_ret_decay_fwd, v_ret_decay_bwd, v_w_ret_out, v_w_out, v_g_ffn, v_w_gate_up, v_w_down):
    w = dict(g_mix=g_mix, w_in=w_in, g_q_a=g_q_a, w_q_b=w_q_b, g_kv_a=g_kv_a, w_kv_b=w_kv_b, g_qn=g_qn, g_kn=g_kn, w_mla_out=w_mla_out,
             ret_decay_fwd=ret_decay_fwd, ret_decay_bwd=ret_decay_bwd, w_ret_out=w_ret_out, w_out=w_out, g_ffn=g_ffn,
             w_gate_up=w_gate_up, w_down=w_down)
    m = dict(g_mix=m_g_mix, w_in=m_w_in, g_q_a=m_g_q_a, w_q_b=m_w_q_b, g_kv_a=m_g_kv_a, w_kv_b=m_w_kv_b, g_qn=m_g_qn, g_kn=m_g_kn,
             w_mla_out=m_w_mla_out, ret_decay_fwd=m_ret_decay_fwd, ret_decay_bwd=m_ret_decay_bwd, w_ret_out=m_w_ret_out, w_out=m_w_out,
             g_ffn=m_g_ffn, w_gate_up=m_w_gate_up, w_down=m_w_down)
    v = dict(g_mix=v_g_mix, w_in=v_w_in, g_q_a=v_g_q_a, w_q_b=v_w_q_b, g_kv_a=v_g_kv_a, w_kv_b=v_w_kv_b, g_qn=v_g_qn, g_kn=v_g_kn,
             w_mla_out=v_w_mla_out, ret_decay_fwd=v_ret_decay_fwd, ret_decay_bwd=v_ret_decay_bwd, w_ret_out=v_w_ret_out, w_out=v_w_out,
             g_ffn=v_g_ffn, w_gate_up=v_w_gate_up, w_down=v_w_down)
    gains = {n: w[n].reshape(1, ln) for n, ln in GAINS}

    axis_of = {n: axis for n, _, axis in MATS}
    later = [n for n, _, _ in MATS if n not in FIRST_WEIGHTS]
    gathered = _all_gather([w[n].astype(WIRE) for n in FIRST_WEIGHTS])
    W = {n: _unshard(g, axis_of[n]) for n, g in zip(FIRST_WEIGHTS, gathered)}
    later_handles = _exchange_start("gather_later_start", [w[n].astype(WIRE) for n in later], True, after=gathered[0])

    def late_weights(after):
        lands = _exchange_wait("gather_later_wait", later_handles, after, True)
        return {n: _unshard(g, axis_of[n]) for n, g in zip(later, lands)}

    grad_groups = []

    def grad_hook(g):
        names = tuple(g)
        handles = _exchange_start("grads_start_%d" % len(grad_groups), [_reshard(g[n], axis_of[n]).astype(GWIRE) for n in names], False)
        grad_groups.append((names, handles))
        return handles[4]

    S = x.shape[1]
    pos = positions.reshape(S, 1).astype(F32)
    loss_rows, grad_x, gG, gW = _local_step(x.reshape(S, D_MODEL), pos, loss_target.reshape(S, D_MODEL), gains, W, late_weights, grad_hook)
    loss = lax.psum(jnp.sum(loss_rows), ("x", "y", "c"))

    last = [n for n, _, _ in MATS if n not in EARLY_GRADS + MID_GRADS]
    pieces = [_reshard(gW[n], axis_of[n]).astype(GWIRE) for n in last]
    pieces.append(jnp.broadcast_to(_pack_gains(gG)[None], (N_DEV, 1, GAIN_PAD)))
    late_parts = _all_to_all(pieces)
    parts = dict(zip(last, late_parts))
    for i, (names, handles) in enumerate(grad_groups):
        parts.update(zip(names, _exchange_wait("grads_wait_%d" % i, handles, late_parts[-1], False)))
    out = [dict() for _ in range(4)]
    for n, _, _ in MATS:
        for o, r in zip(out, _adamw("adamw_" + n, parts[n], w[n], m[n], v[n])):
            o[n] = r
    for o, r in zip(out, _adamw("adamw_gains", late_parts[-1], _pack_gains(w), _pack_gains(m), _pack_gains(v))):
        o.update(_unpack_gains(r))
    return (loss, grad_x.reshape(x.shape), *[o[n] for o in out for n in ORDER])
```

```python
import functools

import numpy as np
import jax
import jax.numpy as jnp
from jax import lax
from jax.experimental import pallas as pl
from jax.experimental.pallas import tpu as pltpu

F32 = jnp.float32
MXU = jnp.bfloat16
WIRE = jnp.bfloat16
GWIRE = jnp.bfloat16

N_DEV = 8
D_MODEL = 1024
HEADS = 8
LANES = 128
Q_RANK, KV_RANK = 256, 128
NOPE, ROPE_M, V_M = 64, 32, 64
QK_M = NOPE + ROPE_M
RQK, RV = 64, 128
CHUNK = 128
FFN = 2816
IN_WIDTH = 5536
THETA = 10000.0
EPS = 1e-6
LR, B1, B2, AEPS, WD, STEP = 0.001, 0.9, 0.999, 1e-08, 0.01, 10
VMEM_LIMIT = 56 * 1024 * 1024

NN = ((1,), (0,))
NT = ((1,), (1,))
TN = ((0,), (0,))

P_GATES, P_VR, P_GR, P_QR, P_KR, P_CQ, P_CKV, P_KROPE, P_WIDTH = 0, 2048, 3072, 4096, 4608, 5120, 5376, 5504, 5632
O_CQ, O_CKV, O_KROPE, O_QR, O_KR, O_VR, O_GR, O_GATES = 0, 256, 384, 416, 928, 1440, 2464, 3488


def _dot(a, b, dims):
    return lax.dot_general(a, b, (dims, ((), ())), preferred_element_type=F32)


def _pick(dim, cands):
    for c in cands:
        if dim % c == 0:
            return c
    return dim


def _pairs(t):
    return t.reshape(t.shape[0], 4, 2, 2, 32).transpose(0, 1, 3, 2, 4).reshape(t.shape[0], 512)


def _win_pad(w):
    z = jnp.zeros((w.shape[0], 48), w.dtype)
    kr = w[:, O_KROPE:O_KROPE + 32]
    return jnp.concatenate([w[:, O_GATES:], w[:, O_VR:O_VR + 1024], w[:, O_GR:O_GR + 1024], _pairs(w[:, O_QR:O_QR + 512]),
                            _pairs(w[:, O_KR:O_KR + 512]), w[:, :O_CKV], w[:, O_CKV:O_KROPE], kr[:, :16], z, kr[:, 16:], z], axis=1)


def _win_unpad(g):
    return jnp.concatenate([g[:, P_CQ:P_CQ + 256], g[:, P_CKV:P_CKV + 128], g[:, P_KROPE:P_KROPE + 16], g[:, P_KROPE + 64:P_KROPE + 80],
                            _pairs(g[:, P_QR:P_QR + 512]), _pairs(g[:, P_KR:P_KR + 512]), g[:, P_VR:P_VR + 1024],
                            g[:, P_GR:P_GR + 1024], g[:, P_GATES:P_GATES + 2048]], axis=1)


def _qk_pad(t):
    z = jnp.zeros(t.shape[:-1] + (32,), t.dtype)
    return jnp.concatenate([t[..., 64:80], t[..., 0:48], t[..., 80:96], t[..., 48:64], z], axis=-1)


def _qk_unpad(p):
    return jnp.concatenate([p[..., 16:64], p[..., 80:96], p[..., 0:16], p[..., 64:80]], axis=-1)


def _wq_pad(w):
    return _qk_pad(w.reshape(Q_RANK, HEADS, QK_M)).reshape(Q_RANK, HEADS * LANES)


def _wq_unpad(g):
    return _qk_unpad(g.reshape(Q_RANK, HEADS, LANES)).reshape(Q_RANK, HEADS * QK_M)


def _wkv_pad(w):
    t = w.reshape(KV_RANK, HEADS, NOPE + V_M)
    z = lambda n: jnp.zeros((KV_RANK, HEADS, n), w.dtype)
    wk = jnp.concatenate([z(16), t[..., 0:48], z(16), t[..., 48:64], z(32)], axis=-1)
    wv = jnp.concatenate([t[..., 64:128], z(64)], axis=-1)
    return wk.reshape(KV_RANK, HEADS * LANES), wv.reshape(KV_RANK, HEADS * LANES)


def _wkv_unpad(dwk, dwv):
    k, v = dwk.reshape(KV_RANK, HEADS, LANES), dwv.reshape(KV_RANK, HEADS, LANES)
    return jnp.concatenate([k[..., 16:64], k[..., 80:96], v[..., 0:64]], axis=-1).reshape(KV_RANK, HEADS * (NOPE + V_M))


def _wmla_pad(w):
    t = w.reshape(HEADS, V_M, D_MODEL)
    return jnp.concatenate([t, jnp.zeros_like(t)], axis=1).reshape(HEADS * LANES, D_MODEL)


def _wmla_unpad(g):
    return g.reshape(HEADS, LANES, D_MODEL)[:, :V_M].reshape(HEADS * V_M, D_MODEL)


def _rowwise(name, fn, rows, ts, ins, outs, accs=(), ncol=1):
    n_in, n_out, n_acc = len(ins), len(outs), len(accs)

    def colmap(col):
        if callable(col):
            return lambda i, j: (i, col(j))
        return lambda i, j: (i, col)

    arrays, in_specs = [], []
    for arr, spec in ins:
        arrays.append(arr)
        if spec is None:
            in_specs.append(pl.BlockSpec(arr.shape, functools.partial(lambda i, j, nd: (0,) * nd, nd=arr.ndim)))
        else:
            in_specs.append(pl.BlockSpec((ts, spec[0]), colmap(spec[1])))
    out_shape, out_specs = [], []
    for total, dtype, width, col in outs:
        out_shape.append(jax.ShapeDtypeStruct((rows, total), dtype))
        out_specs.append(pl.BlockSpec((ts, width), colmap(col)))
    for shp in accs:
        out_shape.append(jax.ShapeDtypeStruct(shp, F32))
        out_specs.append(pl.BlockSpec(shp, functools.partial(lambda i, j, nd: (0,) * nd, nd=len(shp))))

    def body(*refs):
        vals = [r[...] for r in refs[:n_in]]
        res = fn(*vals)
        if not isinstance(res, (tuple, list)):
            res = (res,)
        for r, v in zip(refs[n_in:n_in + n_out], res[:n_out]):
            r[...] = v.astype(r.dtype)
        if n_acc:
            first = jnp.logical_and(pl.program_id(0) == 0, pl.program_id(1) == 0)
            for r, v in zip(refs[n_in + n_out:], res[n_out:]):
                @pl.when(first)
                def _(r=r):
                    r[...] = jnp.zeros_like(r)
                r[...] += v.astype(F32)

    res = pl.pallas_call(
        body, name=name, grid=(rows // ts, ncol), in_specs=in_specs, out_specs=out_specs, out_shape=out_shape,
        compiler_params=pltpu.CompilerParams(dimension_semantics=("arbitrary", "arbitrary"), vmem_limit_bytes=VMEM_LIMIT),
    )(*arrays)
    return res


MM_OPERAND_BYTES = 24 * 1024 * 1024


def _mm(name, a, b, mode, add=None, after=None):
    a_halves, b_halves = a.ndim == 3, b.ndim == 3
    assert not a_halves or mode == "nt"
    assert not b_halves or mode == "tn"
    if mode == "nn":
        (M, K), N = a.shape, b.shape[1]
    elif mode == "nt":
        M, K, N = a.shape[-2], a.shape[-1] * (2 if a_halves else 1), b.shape[0]
    else:
        (K, M), N = a.shape, b.shape[-1] * (2 if b_halves else 1)
    tm = _pick(M, (512, 1408, 256, 128)) if mode == "tn" else _pick(M, (1024, 512, 256, 128))
    tn = _pick(N // 2 if b_halves else N, (1408, 1024, 512, 256, 128))
    fits = lambda t: 2 * (tm + tn) * t * a.dtype.itemsize <= MM_OPERAND_BYTES
    kdiv = K // 2 if a_halves else K
    tk = next(t for t in (K, 4096, 2816, 2048, 1408, 1024, 512, 256, 128) if kdiv % t == 0 and (fits(t) or t == 128))
    nk = K // tk
    dims = {"nn": NN, "nt": NT, "tn": TN}[mode]
    if a_halves:
        per = kdiv // tk
        a_spec = pl.BlockSpec((None, tm, tk), lambda i, j, k: (k // per, i, k % per))
    else:
        a_spec = pl.BlockSpec((tk, tm), lambda i, j, k: (k, i)) if mode == "tn" else pl.BlockSpec((tm, tk), lambda i, j, k: (i, k))
    if b_halves:
        perj = (N // 2) // tn
        b_spec = pl.BlockSpec((None, tk, tn), lambda i, j, k: (j // perj, k, j % perj))
    else:
        b_spec = pl.BlockSpec((tn, tk), lambda i, j, k: (j, k)) if mode == "nt" else pl.BlockSpec((tk, tn), lambda i, j, k: (k, j))
    o_spec = pl.BlockSpec((tm, tn), lambda i, j, k: (i, j))
    has_add = add is not None

    def body(*refs):
        a_ref, b_ref, o_ref = refs[0], refs[1], refs[-1]
        d = _dot(a_ref[...], b_ref[...], dims)
        first = (d + refs[2][...]) if has_add else d
        if nk == 1:
            o_ref[...] = first
        else:
            k = pl.program_id(2)

            @pl.when(k == 0)
            def _():
                o_ref[...] = first

            @pl.when(k > 0)
            def _():
                o_ref[...] += d

    args = [a, b] + ([add] if has_add else []) + ([] if after is None else [after])
    specs = [a_spec, b_spec] + ([o_spec] if has_add else []) + ([] if after is None else [pl.BlockSpec(memory_space=pl.ANY)])
    return pl.pallas_call(
        body, name=name, grid=(M // tm, N // tn, nk), in_specs=specs, out_specs=o_spec,
        out_shape=jax.ShapeDtypeStruct((M, N), F32),
        compiler_params=pltpu.CompilerParams(dimension_semantics=("parallel", "parallel", "arbitrary"), vmem_limit_bytes=VMEM_LIMIT),
    )(*args)


def _ffn_tiles(S):
    return _pick(S, (512, 256, 128)), _pick(FFN, (1408, 704, 256, 128))


def _gate_up_swiglu(h2, wgu):
    S, K = h2.shape
    tm, tn = _ffn_tiles(S)
    nj = FFN // tn

    def body(a_ref, bg_ref, bu_ref, gu_ref, act_ref):
        a = a_ref[...]
        g, u = _dot(a, bg_ref[...], NN), _dot(a, bu_ref[...], NN)
        gu_ref[0], gu_ref[1] = g, u
        act_ref[...] = _swiglu_fn(g, u).astype(act_ref.dtype)

    return pl.pallas_call(
        body, name="gate_up_swiglu", grid=(S // tm, nj),
        in_specs=[pl.BlockSpec((tm, K), lambda i, j: (i, 0)), pl.BlockSpec((K, tn), lambda i, j: (0, j)),
                  pl.BlockSpec((K, tn), lambda i, j: (0, nj + j))],
        out_specs=[pl.BlockSpec((2, tm, tn), lambda i, j: (0, i, j)), pl.BlockSpec((tm, tn), lambda i, j: (i, j))],
        out_shape=[jax.ShapeDtypeStruct((2, S, FFN), F32), jax.ShapeDtypeStruct((S, FFN), MXU)],
        compiler_params=pltpu.CompilerParams(dimension_semantics=("parallel", "parallel"), vmem_limit_bytes=VMEM_LIMIT),
    )(h2, wgu, wgu)


def _d_act_swiglu(dx2, wdown, gu):
    S, K = dx2.shape
    tm, tn = _ffn_tiles(S)

    def body(a_ref, b_ref, gu_ref, o_ref):
        dact = _dot(a_ref[...], b_ref[...], NT)
        _, vjp = jax.vjp(_swiglu_fn, gu_ref[0], gu_ref[1])
        dg, du = vjp(dact)
        o_ref[0], o_ref[1] = dg.astype(o_ref.dtype), du.astype(o_ref.dtype)

    stacked = pl.BlockSpec((2, tm, tn), lambda i, j: (0, i, j))
    return pl.pallas_call(
        body, name="d_act_swiglu", grid=(S // tm, FFN // tn),
        in_specs=[pl.BlockSpec((tm, K), lambda i, j: (i, 0)), pl.BlockSpec((tn, K), lambda i, j: (j, 0)), stacked],
        out_specs=stacked, out_shape=jax.ShapeDtypeStruct((2, S, FFN), MXU),
        compiler_params=pltpu.CompilerParams(dimension_semantics=("parallel", "parallel"), vmem_limit_bytes=VMEM_LIMIT),
    )(dx2, wdown, gu)


@jax.custom_vjp
def _swap64(x):
    return pltpu.roll(x, 64, 1)


_swap64.defvjp(lambda x: (_swap64(x), None), lambda _, g: (_swap64(g),))


@jax.custom_vjp
def _mxdot(a, b):
    return _dot(a.astype(MXU), b.astype(MXU), NN)


def _mxdot_bwd(res, g):
    a, b = res
    gb = g.astype(MXU)
    return _dot(gb, b.astype(MXU), NT), _dot(a.astype(MXU), gb, TN)


_mxdot.defvjp(lambda a, b: (_mxdot(a, b), (a, b)), _mxdot_bwd)


def _rms(x):
    return x * lax.rsqrt(jnp.mean(x * x, axis=-1, keepdims=True) + EPS)


def _rmsg_fn(x, g):
    return _rms(x) * g


def _silu(x):
    return x * jax.nn.sigmoid(x)


def _tables_fn(pos, inv_m, sgn_m, inv_r, sgn_r):
    am, ar = pos * inv_m, pos * inv_r
    return jnp.cos(am), jnp.sin(am) * sgn_m, jnp.cos(ar), jnp.sin(ar) * sgn_r


def _head_blocks(t):
    return [t[:, LANES * h:LANES * (h + 1)] for h in range(t.shape[1] // LANES)]


def _mla_prep_fn(cq, ckv, kr, cosm, sinm, gqa, gkva, gqn, gkn, wq, wk, wv):
    cqn = _rms(cq) * gqa
    ckvn = _rms(ckv) * gkva
    q_raw = _mxdot(cqn, wq)
    k_raw = _mxdot(ckvn, wk)
    lane = lax.broadcasted_iota(jnp.int32, (1, HEADS * LANES), 1)
    v = _mxdot(ckvn, wv) + (lane % LANES == V_M).astype(F32)

    def norm_rope(blocks, g, extra):
        outs = []
        for b in blocks:
            if extra is not None:
                b = b + extra
            n = b * lax.rsqrt(jnp.sum(b * b, axis=-1, keepdims=True) * (1.0 / QK_M) + EPS) * g
            outs.append(n * cosm + _swap64(n) * sinm)
        return jnp.concatenate(outs, axis=1)

    q = norm_rope(_head_blocks(q_raw), gqn, None)
    k = norm_rope(_head_blocks(k_raw), gkn, kr)
    return q, k, v


def _ret_prep_fn(qr, kr, cosr, sinr):
    def rope(t, scale):
        return jnp.concatenate([(b * cosr + _swap64(b) * sinr) * scale for b in _head_blocks(t)], axis=1)
    return rope(qr, 1.0), rope(kr, RQK ** -0.5)


def _ret_post_fn(rf, rb, gr):
    ret = rf + rb
    outs = []
    for b, g in zip(_head_blocks(ret), _head_blocks(gr)):
        outs.append(_silu(g) * _rms(b))
    return jnp.concatenate(outs, axis=1)


def _merge_fn(ga, gb, ya, yb):
    return jax.nn.sigmoid(ga) * ya + jax.nn.sigmoid(gb) * yb


def _swiglu_fn(gate, up):
    return _silu(gate) * up


def _loss_fn(x2, tgt):
    d = x2 - tgt
    return d * (1.0 / D_MODEL), 0.5 * jnp.sum(d * d, axis=0, keepdims=True) * (1.0 / D_MODEL)


def _adamw_fn(parts, w, m, v):
    g = parts[0].astype(F32)
    for p in range(1, N_DEV):
        g = g + parts[p].astype(F32)
    m2 = B1 * m + (1.0 - B1) * g
    v2 = B2 * v + (1.0 - B2) * jnp.square(g)
    m_hat = m2 / (1.0 - B1 ** STEP)
    v_hat = v2 / (1.0 - B2 ** STEP)
    delta = -LR * (m_hat / (jnp.sqrt(v_hat) + AEPS) + WD * w)
    return g, delta, m2, v2


SCALE = QK_M ** -0.5
LOG2E = 1.4426950408889634
FLASH_ROWS = 32


def _flash_fwd(q, k, v):
    S = q.shape[0]
    tq = tk = _pick(S, (512, 256, 128))
    ncb = tk // LANES
    nkv = S // tk
    assert nkv % 2 == 0, "kv tiles are processed in pairs"
    mrows = 64
    c = SCALE * LOG2E

    def body(q_ref, k_ref, v_ref, o_ref, obf_ref, lse_ref, s_a, p_a, s_b, p_b, m_sc, a_sc, acc_sc):
        m_sc[...] = jnp.full_like(m_sc, -jnp.inf)
        acc_sc[...] = jnp.zeros_like(acc_sc)
        qb = q_ref[...]

        def scores(j, s_buf):
            s_buf[...] = _dot(qb, k_ref[pl.ds(pl.multiple_of(j * tk, tk), tk), :], NT)

        def stage(j, s_buf, p_buf, s_next):
            scores(jnp.minimum(j + 1, nkv - 1), s_next)
            for r in range(tq // mrows):
                rows = slice(r * mrows, (r + 1) * mrows)
                cols = [s_buf[rows, LANES * cb:LANES * (cb + 1)] for cb in range(ncb)]
                m_prev = m_sc[rows, :]
                row_max = jnp.max(functools.reduce(jnp.maximum, cols), axis=-1, keepdims=True)
                m_new = jnp.maximum(m_prev, jnp.broadcast_to(row_max, (mrows, LANES)))
                a_sc[rows, :] = jnp.exp2((m_prev - m_new) * c)
                m_sc[rows, :] = m_new
                for cb in range(ncb):
                    p_buf[rows, LANES * cb:LANES * (cb + 1)] = jnp.exp2((cols[cb] - m_new) * c).astype(p_buf.dtype)
            acc_sc[...] = a_sc[...] * acc_sc[...] + _dot(p_buf[...], v_ref[pl.ds(pl.multiple_of(j * tk, tk), tk), :], NN)

        scores(0, s_a)

        def pair_step(t, carry):
            stage(2 * t, s_a, p_a, s_b)
            stage(2 * t + 1, s_b, p_b, s_a)
            return carry

        lax.fori_loop(0, nkv // 2, pair_step, 0, unroll=4)
        acc = acc_sc[...]
        lane = lax.broadcasted_iota(jnp.int32, (1, LANES), 1)
        l = jnp.sum(jnp.where(lane == V_M, acc, 0.0), axis=-1, keepdims=True)
        o = acc / l
        o_ref[...] = o
        obf_ref[...] = o.astype(obf_ref.dtype)
        lse_ref[...] = m_sc[...] * c + jnp.log2(jnp.broadcast_to(l, (tq, LANES)))

    qspec = pl.BlockSpec((tq, LANES), lambda h, i: (i, h))
    kspec = pl.BlockSpec((S, LANES), lambda h, i: (0, h))
    full = jax.ShapeDtypeStruct((S, HEADS * LANES), F32)
    return pl.pallas_call(
        body, name="flash_fwd", grid=(HEADS, S // tq), in_specs=[qspec, kspec, kspec], out_specs=[qspec, qspec, qspec],
        out_shape=[full, jax.ShapeDtypeStruct((S, HEADS * LANES), MXU), full],
        scratch_shapes=[pltpu.VMEM((tq, tk), F32), pltpu.VMEM((tq, tk), MXU)] * 2 + [pltpu.VMEM((tq, LANES), F32)] * 3,
        compiler_params=pltpu.CompilerParams(dimension_semantics=("parallel", "arbitrary"), vmem_limit_bytes=VMEM_LIMIT),
    )(q, k, v)


def _delta_fn(o, do):
    outs = [jnp.broadcast_to(jnp.sum(a * b, axis=-1, keepdims=True), a.shape) for a, b in zip(_head_blocks(o), _head_blocks(do))]
    return do, jnp.concatenate(outs, axis=1)


def _flash_bwd(q, k, v, do, lse, delta):
    S = q.shape[0]
    tq = tk = _pick(S, (512, 256, 128))
    ncb = tk // LANES
    c = SCALE * LOG2E

    nq = S // tq
    assert nq % 2 == 0, "q tiles are processed in pairs"

    def body(q_ref, k_ref, v_ref, do_ref, lse_ref, dl_ref, dq_ref, dk_ref, dv_ref, s_a, dp_a, p_a, ds_a, s_b, dp_b, p_b, ds_b, dk_sc, dv_sc):
        @pl.when(pl.program_id(1) == 0)
        def _():
            dq_ref[...] = jnp.zeros_like(dq_ref)

        dk_sc[...] = jnp.zeros_like(dk_sc)
        dv_sc[...] = jnp.zeros_like(dv_sc)
        kb, vb = k_ref[...], v_ref[...]

        def scores(i, s_buf, dp_buf):
            q_rows = pl.ds(pl.multiple_of(i * tq, tq), tq)
            s_buf[...] = _dot(q_ref[q_rows, :], kb, NT)
            dp_buf[...] = _dot(do_ref[q_rows, :], vb, NT)

        def stage(i, s_buf, dp_buf, p_buf, ds_buf, s_next, dp_next):
            scores(jnp.minimum(i + 1, nq - 1), s_next, dp_next)
            for r in range(tq // FLASH_ROWS):
                rows = slice(r * FLASH_ROWS, (r + 1) * FLASH_ROWS)
                grows = pl.ds(pl.multiple_of(i * tq + r * FLASH_ROWS, FLASH_ROWS), FLASH_ROWS)
                lse_b, dl_b = lse_ref[grows, :], dl_ref[grows, :]
                for cb in range(ncb):
                    sl = slice(LANES * cb, LANES * (cb + 1))
                    p = jnp.exp2(s_buf[rows, sl] * c - lse_b)
                    p_buf[rows, sl] = p.astype(p_buf.dtype)
                    ds_buf[rows, sl] = (p * (dp_buf[rows, sl] - dl_b) * SCALE).astype(ds_buf.dtype)
            q_rows = pl.ds(pl.multiple_of(i * tq, tq), tq)
            dv_sc[...] += _dot(p_buf[...], do_ref[q_rows, :], TN)
            dk_sc[...] += _dot(ds_buf[...], q_ref[q_rows, :], TN)
            dq_ref[q_rows, :] += _dot(ds_buf[...], kb, NN)

        scores(0, s_a, dp_a)

        def pair_step(t, carry):
            stage(2 * t, s_a, dp_a, p_a, ds_a, s_b, dp_b)
            stage(2 * t + 1, s_b, dp_b, p_b, ds_b, s_a, dp_a)
            return carry

        lax.fori_loop(0, nq // 2, pair_step, 0, unroll=2)
        dk_ref[...] = dk_sc[...]
        dv_ref[...] = dv_sc[...]

    hspec = pl.BlockSpec((S, LANES), lambda h, j: (0, h))
    kspec = pl.BlockSpec((tk, LANES), lambda h, j: (j, h))
    full = jax.ShapeDtypeStruct((S, HEADS * LANES), F32)
    tile_bufs = [pltpu.VMEM((tq, tk), F32), pltpu.VMEM((tq, tk), F32), pltpu.VMEM((tq, tk), MXU), pltpu.VMEM((tq, tk), MXU)]
    return pl.pallas_call(
        body, name="flash_bwd", grid=(HEADS, S // tk), in_specs=[hspec, kspec, kspec, hspec, hspec, hspec],
        out_specs=[hspec, kspec, kspec], out_shape=[full, full, full],
        scratch_shapes=tile_bufs + tile_bufs + [pltpu.VMEM((tk, LANES), F32), pltpu.VMEM((tk, LANES), F32)],
        compiler_params=pltpu.CompilerParams(dimension_semantics=("parallel", "arbitrary"), vmem_limit_bytes=VMEM_LIMIT),
    )(q, k, v, do, lse, delta)


def _ret_consts(lgh, head, rev):
    C = CHUNK
    lane = lax.broadcasted_iota(jnp.int32, (1, LANES), 1)
    hm = ((lane // 32) % 2 == head % 2).astype(F32)
    r = lax.broadcasted_iota(jnp.int32, (C, C), 0)
    c = lax.broadcasted_iota(jnp.int32, (C, C), 1)
    diff = ((c - r) if rev else (r - c)).astype(F32)
    mask = (diff > 0) if rev else (diff >= 0)
    dpos = jnp.maximum(diff, 0.0)
    din = jnp.where(mask, jnp.exp(lgh * dpos), 0.0)
    idx = lax.broadcasted_iota(jnp.int32, (C, 1), 0).astype(F32)
    eq = (C - idx) if rev else (idx + 1.0)
    ek = idx if rev else (C - 1.0 - idx)
    qd, kd = jnp.exp(lgh * eq), jnp.exp(lgh * ek)
    cd = jnp.exp(lgh * jnp.full((1, 1), float(C), F32))
    return hm, din, dpos, qd, kd, cd, eq, ek


RET_HEADS_PER_STEP = 4


def _ret_fwd(name, qt, kt, proj, lg, rev):
    S = qt.shape[0]
    C = CHUNK
    TB = _pick(S, (512, 256, 128))
    cb, nb = TB // C, S // TB
    hps = RET_HEADS_PER_STEP
    blk = (lambda g: nb - 1 - g) if rev else (lambda g: g)

    def body(lg_ref, q_ref, k_ref, v_ref, o_ref, st_ref, state_sc):
        hg, g = pl.program_id(0), pl.program_id(1)

        @pl.when(g == 0)
        def _():
            state_sc[...] = jnp.zeros_like(state_sc)

        consts = [_ret_consts(lg_ref[hg * hps + u], u, rev) for u in range(hps)]
        order = list(reversed(range(cb))) if rev else list(range(cb))
        units = [(cc, u) for cc in order for u in range(hps)]

        def operands(cc, u):
            rows = pl.ds(cc * C, C)
            pair = slice(LANES * (u // 2), LANES * (u // 2 + 1))
            hm = consts[u][0]
            return q_ref[rows, pair] * hm, k_ref[rows, pair] * hm, v_ref[rows, LANES * u:LANES * (u + 1)].astype(MXU)

        a, inc = {}, {}
        for cc, u in units:
            q, k, v = operands(cc, u)
            a[cc, u] = _dot(q.astype(MXU), k.astype(MXU), NT) * consts[u][1]
            inc[cc, u] = _dot((k * consts[u][4]).astype(MXU), v, TN)
        for u in range(hps):
            st = state_sc[u]
            for cc in order:
                st_ref[u, cc] = st
                st = st * consts[u][5] + inc[cc, u]
            state_sc[u] = st
        for cc, u in units:
            q, _, v = operands(cc, u)
            cross = _dot((q * consts[u][3]).astype(MXU), st_ref[u, cc].astype(MXU), NN)
            o_ref[pl.ds(cc * C, C), LANES * u:LANES * (u + 1)] = _dot(a[cc, u].astype(MXU), v, NN) + cross

    qk_spec = pl.BlockSpec((TB, LANES * hps // 2), lambda h, g: (blk(g), h))
    return pl.pallas_call(
        body, name=name, grid=(HEADS // hps, nb),
        in_specs=[pl.BlockSpec(memory_space=pltpu.SMEM), qk_spec, qk_spec,
                  pl.BlockSpec((TB, LANES * hps), lambda h, g: (blk(g), P_VR // (LANES * hps) + h))],
        out_specs=[pl.BlockSpec((TB, LANES * hps), lambda h, g: (blk(g), h)),
                   pl.BlockSpec((hps, cb, LANES, LANES), lambda h, g: (h, blk(g), 0, 0))],
        out_shape=[jax.ShapeDtypeStruct((S, HEADS * LANES), F32), jax.ShapeDtypeStruct((HEADS, S // C, LANES, LANES), F32)],
        scratch_shapes=[pltpu.VMEM((hps, LANES, LANES), F32)],
        compiler_params=pltpu.CompilerParams(dimension_semantics=("parallel", "arbitrary"), vmem_limit_bytes=VMEM_LIMIT),
    )(lg, qt, kt, proj)


def _ret_bwd(name, qt, kt, proj, dret, states, lg, rev):
    S = qt.shape[0]
    C = CHUNK
    TB = _pick(S, (512, 256, 128))
    cb, nb = TB // C, S // TB
    hps = RET_HEADS_PER_STEP
    blk = (lambda g: g) if rev else (lambda g: nb - 1 - g)

    def body(lg_ref, q_ref, k_ref, v_ref, do_ref, st_ref, dq_ref, dk_ref, dv_ref, dlg_ref, ds_sc, acc_cc, acc_q, acc_k, acc_s):
        hg, g = pl.program_id(0), pl.program_id(1)

        @pl.when(g == 0)
        def _():
            ds_sc[...] = jnp.zeros_like(ds_sc)
            acc_cc[...] = jnp.zeros_like(acc_cc)
            acc_q[...] = jnp.zeros_like(acc_q)
            acc_k[...] = jnp.zeros_like(acc_k)
            acc_s[...] = jnp.zeros_like(acc_s)

        lgs = [lg_ref[hg * hps + u] for u in range(hps)]
        consts = [_ret_consts(lgs[u], u, rev) for u in range(hps)]
        order = list(range(cb)) if rev else list(reversed(range(cb)))
        units = [(cc, u) for cc in order for u in range(hps)]

        def operands(cc, u):
            rows = pl.ds(cc * C, C)
            pair = slice(LANES * (u // 2), LANES * (u // 2 + 1))
            head = slice(LANES * u, LANES * (u + 1))
            hm = consts[u][0]
            return q_ref[rows, pair] * hm, k_ref[rows, pair] * hm, v_ref[rows, head].astype(MXU), do_ref[rows, head].astype(MXU)

        a, dp, dqs, inc = {}, {}, {}, {}
        for cc, u in units:
            q, k, vb, dob = operands(cc, u)
            a[cc, u] = _dot(q.astype(MXU), k.astype(MXU), NT)
            dp[cc, u] = _dot(dob, vb, NT)
            dqs[cc, u] = _dot(dob, st_ref[u, cc].astype(MXU), NT)
            inc[cc, u] = _dot((q * consts[u][3]).astype(MXU), dob, TN)
        dsn = {}
        for u in range(hps):
            ds = ds_sc[u]
            for cc in order:
                dsn[cc, u] = ds
                ds = ds * consts[u][5] + inc[cc, u]
            ds_sc[u] = ds
        even = {}
        for cc, u in units:
            hm, din, dpos, qd, kd, cd, eq, ek = consts[u]
            rows, head = pl.ds(cc * C, C), slice(LANES * u, LANES * (u + 1))
            q, k, vb, dob = operands(cc, u)
            qb, kb = q.astype(MXU), k.astype(MXU)
            dsnb = dsn[cc, u].astype(MXU)
            da = (dp[cc, u] * din).astype(MXU)
            vds = _dot(vb, dsnb, NT)
            dq_u = (_dot(da, kb, NN) + dqs[cc, u] * qd) * hm
            dk_u = (_dot(da, qb, TN) + vds * kd) * hm
            if u % 2 == 0:
                even[cc] = (dq_u, dk_u)
            else:
                pair = slice(LANES * (u // 2), LANES * (u // 2 + 1))
                dq_ref[rows, pair] = even[cc][0] + dq_u
                dk_ref[rows, pair] = even[cc][1] + dk_u
            dv_ref[rows, head] = _dot((a[cc, u] * din).astype(MXU), dob, TN) + _dot((k * kd).astype(MXU), dsnb, NN)
            acc_cc[u] += dp[cc, u] * a[cc, u] * din * dpos
            acc_q[u] += dqs[cc, u] * q * (qd * eq)
            acc_k[u] += vds * k * (kd * ek)
            acc_s[u] += dsn[cc, u] * st_ref[u, cc] * (cd * float(C))

        @pl.when(g == nb - 1)
        def _():
            for u in range(hps):
                tot = (jnp.sum(acc_cc[u], keepdims=True) + jnp.sum(acc_q[u], keepdims=True)
                       + jnp.sum(acc_k[u], keepdims=True) + jnp.sum(acc_s[u], keepdims=True))
                dlg_ref[u] = jnp.broadcast_to(tot * lgs[u], (8, LANES))

    full = jax.ShapeDtypeStruct((S, HEADS * LANES), F32)
    hspec = pl.BlockSpec((TB, LANES * hps), lambda h, g: (blk(g), h))
    qk_spec = pl.BlockSpec((TB, LANES * hps // 2), lambda h, g: (blk(g), h))
    return pl.pallas_call(
        body, name=name, grid=(HEADS // hps, nb),
        in_specs=[pl.BlockSpec(memory_space=pltpu.SMEM), qk_spec, qk_spec,
                  pl.BlockSpec((TB, LANES * hps), lambda h, g: (blk(g), P_VR // (LANES * hps) + h)),
                  hspec,
                  pl.BlockSpec((hps, cb, LANES, LANES), lambda h, g: (h, blk(g), 0, 0))],
        out_specs=[qk_spec, qk_spec, hspec, pl.BlockSpec((hps, 8, LANES), lambda h, g: (h, 0, 0))],
        out_shape=[jax.ShapeDtypeStruct(qt.shape, F32), jax.ShapeDtypeStruct(kt.shape, F32), full,
                   jax.ShapeDtypeStruct((HEADS, 8, LANES), F32)],
        scratch_shapes=[pltpu.VMEM((hps, LANES, LANES), F32), pltpu.VMEM((hps, C, C), F32), pltpu.VMEM((hps, C, LANES), F32),
                        pltpu.VMEM((hps, C, LANES), F32), pltpu.VMEM((hps, LANES, LANES), F32)],
        compiler_params=pltpu.CompilerParams(dimension_semantics=("parallel", "arbitrary"), vmem_limit_bytes=VMEM_LIMIT),
    )(lg, qt, kt, proj, dret, states)


def _rope_consts():
    inv16 = THETA ** (-jnp.arange(16, dtype=F32) / 16)
    inv32 = THETA ** (-jnp.arange(32, dtype=F32) / 32)
    lane = np.arange(LANES)
    z48 = jnp.zeros((48,), F32)
    inv_m = jnp.concatenate([inv16, z48, inv16, z48])[None, :]
    sgn_m = jnp.asarray(np.where(lane < 16, -1.0, np.where((lane >= 64) & (lane < 80), 1.0, 0.0)), F32)[None, :]
    inv_r = jnp.concatenate([inv32] * 4)[None, :]
    sgn_r = jnp.asarray(np.where(lane < 64, -1.0, 1.0), F32)[None, :]
    return inv_m, sgn_m, inv_r, sgn_r


FIRST_WEIGHTS = ("w_in", "w_q_b", "w_kv_b")
EARLY_GRADS = ("w_down", "w_gate_up", "w_out", "w_ret_out")
MID_GRADS = ("w_mla_out", "w_in")


def _local_step(x, pos, tgt, gains, W, late_weights=None, grad_hook=None):
    S = x.shape[0]
    ts = _pick(S, (256, 128))
    R = lambda a, w=None, c=0: (a, ((a.shape[1] if w is None else w), c))
    W_ = lambda a: (a, None)

    win = _win_pad(W["w_in"])
    wq = _wq_pad(W["w_q_b"])
    wk, wv = _wkv_pad(W["w_kv_b"])
    gqn, gkn = _qk_pad(gains["g_qn"]), _qk_pad(gains["g_kn"])
    g_mix, g_q_a, g_kv_a, g_ffn = gains["g_mix"], gains["g_q_a"], gains["g_kv_a"], gains["g_ffn"]
    lg_f = -jnp.exp(gains["ret_decay_fwd"][0])
    lg_b = -jnp.exp(gains["ret_decay_bwd"][0])

    consts = list(_rope_consts())
    cosm, sinm, cosr, sinr = _rowwise("rope_tables", _tables_fn, S, ts, [R(pos)] + [W_(c) for c in consts],
                                      [(LANES, F32, LANES, 0)] * 4)

    (h,) = _rowwise("rms_mix", _rmsg_fn, S, ts, [R(x), W_(g_mix)], [(D_MODEL, MXU, D_MODEL, 0)])
    proj = _mm("in_proj", h, win, "nn")
    seg = lambda off, w: (proj, (w, off // w))
    mla_ins = [seg(P_CQ, 256), seg(P_CKV, 128), seg(P_KROPE, 128), R(cosm), R(sinm),
               W_(g_q_a), W_(g_kv_a), W_(gqn), W_(gkn), W_(wq), W_(wk), W_(wv)]
    q, k, v = _rowwise("mla_prep", _mla_prep_fn, S, ts, mla_ins, [(HEADS * LANES, MXU, HEADS * LANES, 0)] * 3)
    o, o_bf, lse = _flash_fwd(q, k, v)
    if late_weights is not None:
        W = {**W, **late_weights(lse)}
    wmla = _wmla_pad(W["w_mla_out"])
    wret, wout, wgu, wdown = W["w_ret_out"], W["w_out"], W["w_gate_up"], W["w_down"]
    y_a = _mm("mla_out", o_bf, wmla, "nn")

    ret_ins = [seg(P_QR, 512), seg(P_KR, 512), R(cosr), R(sinr)]
    qt, kt = _rowwise("ret_prep", _ret_prep_fn, S, ts, ret_ins, [(512, F32, 512, 0)] * 2)
    ret_f, st_f = _ret_fwd("ret_fwd_f", qt, kt, proj, lg_f, False)
    ret_b, st_b = _ret_fwd("ret_fwd_b", qt, kt, proj, lg_b, True)
    post_ins = [R(ret_f), R(ret_b), seg(P_GR, 1024)]
    (o_b,) = _rowwise("ret_post", _ret_post_fn, S, ts, post_ins, [(1024, MXU, 1024, 0)])
    y_b = _mm("ret_out", o_b, wret, "nn")

    merge_ins = [seg(P_GATES, 1024), (proj, (1024, 1)), R(y_a), R(y_b)]
    (merged,) = _rowwise("merge", _merge_fn, S, ts, merge_ins, [(D_MODEL, MXU, D_MODEL, 0)])
    x1 = _mm("out_proj", merged, wout, "nn", add=x)
    (h2,) = _rowwise("rms_ffn", _rmsg_fn, S, ts, [R(x1), W_(g_ffn)], [(D_MODEL, MXU, D_MODEL, 0)])
    gu, act = _gate_up_swiglu(h2, wgu)
    x2 = _mm("down_proj", act, wdown, "nn", add=x1)
    dx2, dx2_bf, loss_rows = _rowwise("loss", lambda a, b: (lambda d, l: (d, d, l))(*_loss_fn(a, b)), S, ts, [R(x2), R(tgt)],
                                      [(D_MODEL, F32, D_MODEL, 0), (D_MODEL, MXU, D_MODEL, 0)], accs=[(1, D_MODEL)])

    gW = {}
    gW["w_down"] = _mm("d_w_down", act, dx2_bf, "tn")
    dgu = _d_act_swiglu(dx2_bf, wdown, gu)
    gW["w_gate_up"] = _mm("d_w_gate_up", h2, dgu, "tn")
    dh2 = _mm("d_h2", dgu, wgu, "nt")

    def rms_bwd(xx, g, dh, dres):
        _, vjp = jax.vjp(_rmsg_fn, xx, g)
        dx, dg = vjp(dh)
        dx = dx + dres
        return dx, dx, dg

    dx1, dx1_bf, dg_ffn = _rowwise("rms_ffn_bwd", rms_bwd, S, ts, [R(x1), W_(g_ffn), R(dh2), R(dx2)],
                                   [(D_MODEL, F32, D_MODEL, 0), (D_MODEL, MXU, D_MODEL, 0)], accs=[(1, D_MODEL)])
    gW["w_out"] = _mm("d_w_out", merged, dx1_bf, "tn")
    dmerged = _mm("d_merged", dx1_bf, wout, "nt")

    def merge_bwd(ga, gb, ya, yb, dm):
        _, vjp = jax.vjp(_merge_fn, ga, gb, ya, yb)
        return vjp(dm)

    dga, dgb, dy_a, dy_b = _rowwise("merge_bwd", merge_bwd, S, ts, merge_ins + [R(dmerged)], [(D_MODEL, MXU, D_MODEL, 0)] * 4)
    gW["w_ret_out"] = _mm("d_w_ret_out", o_b, dy_b, "tn")
    after_early = [] if grad_hook is None else [W_(grad_hook({n: gW[n] for n in EARLY_GRADS}))]
    do_b = _mm("d_o_b", dy_b, wret, "nt")

    def post_bwd(rf, rb, gr, dob, *_):
        _, vjp = jax.vjp(_ret_post_fn, rf, rb, gr)
        drf, _, dgr = vjp(dob)
        return drf, dgr

    dret, dg_r = _rowwise("ret_post_bwd", post_bwd, S, ts, post_ins + [R(do_b)] + after_early, [(1024, F32, 1024, 0), (1024, MXU, 1024, 0)])
    dq_f, dk_f, dv_f, dlg_f = _ret_bwd("ret_bwd_f", qt, kt, proj, dret, st_f, lg_f, False)
    dq_b, dk_b, dv_b, dlg_b = _ret_bwd("ret_bwd_b", qt, kt, proj, dret, st_b, lg_b, True)

    def ret_prep_bwd(qr, kr, cosr_, sinr_, dqf, dqb, dkf, dkb, dvf, dvb):
        _, vjp = jax.vjp(lambda a, b: _ret_prep_fn(a, b, cosr_, sinr_), qr, kr)
        dqr, dkr = vjp((dqf + dqb, dkf + dkb))
        return dqr, dkr, dvf + dvb

    dq_r, dk_r, dv_r = _rowwise("ret_prep_bwd", ret_prep_bwd, S, ts, ret_ins + [R(t) for t in (dq_f, dq_b, dk_f, dk_b, dv_f, dv_b)],
                                [(512, MXU, 512, 0), (512, MXU, 512, 0), (1024, MXU, 1024, 0)])

    gW_mla_p = _mm("d_w_mla_out", o_bf, dy_a, "tn")
    do = _mm("d_o", dy_a, wmla, "nt")
    do_bf, delta = _rowwise("attn_delta", lambda a, b, *_: _delta_fn(a, b), S, ts, [R(o), R(do)] + after_early,
                            [(HEADS * LANES, MXU, HEADS * LANES, 0), (HEADS * LANES, F32, HEADS * LANES, 0)])
    dq, dk, dv = _flash_bwd(q, k, v, do_bf, lse, delta)

    def mla_prep_bwd(cq, ckv, kr, cosm_, sinm_, gqa, gkva, gqn_, gkn_, wq_, wk_, wv_, dq_, dk_, dv_):
        f = lambda cq, ckv, kr, gqa, gkva, gqn_, gkn_, wq_, wk_, wv_: _mla_prep_fn(cq, ckv, kr, cosm_, sinm_, gqa, gkva, gqn_, gkn_, wq_, wk_, wv_)
        _, vjp = jax.vjp(f, cq, ckv, kr, gqa, gkva, gqn_, gkn_, wq_.astype(F32), wk_.astype(F32), wv_.astype(F32))
        return vjp((dq_, dk_, dv_))

    mb = _rowwise("mla_prep_bwd", mla_prep_bwd, S, ts, mla_ins + [R(dq), R(dk), R(dv)],
                  [(256, MXU, 256, 0), (128, MXU, 128, 0), (128, MXU, 128, 0)],
                  accs=[(1, 256), (1, 128), (1, LANES), (1, LANES), (256, HEADS * LANES), (128, HEADS * LANES), (128, HEADS * LANES)])
    dc_q, dc_kv, dk_rope, dg_q_a, dg_kv_a, dgqn_p, dgkn_p, dwq_p, dwk_p, dwv_p = mb

    dproj = jnp.concatenate([dga, dgb, dv_r, dg_r, dq_r, dk_r, dc_q, dc_kv, dk_rope], axis=1)
    gW["w_in"] = _win_unpad(_mm("d_w_in", h, dproj, "tn"))
    gW["w_mla_out"] = _wmla_unpad(gW_mla_p)
    after_mid = None if grad_hook is None else grad_hook({n: gW[n] for n in MID_GRADS})
    dh = _mm("d_h", dproj, win, "nt", after=after_mid)
    grad_x, _, dg_mix = _rowwise("rms_mix_bwd", lambda a, b, c, d, *_: rms_bwd(a, b, c, d), S, ts,
                                 [R(x), W_(g_mix), R(dh), R(dx1)] + ([] if after_mid is None else [W_(after_mid)]),
                                 [(D_MODEL, F32, D_MODEL, 0), (D_MODEL, MXU, D_MODEL, 0)], accs=[(1, D_MODEL)])
    gW["w_q_b"] = _wq_unpad(dwq_p)
    gW["w_kv_b"] = _wkv_unpad(dwk_p, dwv_p)
    gG = {"g_mix": dg_mix, "g_q_a": dg_q_a, "g_kv_a": dg_kv_a, "g_qn": _qk_unpad(dgqn_p),
          "g_kn": _qk_unpad(dgkn_p), "ret_decay_fwd": dlg_f[:, 0, 0][None, :], "ret_decay_bwd": dlg_b[:, 0, 0][None, :],
          "g_ffn": dg_ffn}
    return loss_rows, grad_x, gG, gW


MATS = [("w_in", (1024, 5536), 1), ("w_q_b", (256, 768), 1), ("w_kv_b", (128, 1024), 1), ("w_mla_out", (512, 1024), 1),
        ("w_ret_out", (1024, 1024), 0), ("w_out", (1024, 1024), 0), ("w_gate_up", (1024, 5632), 1), ("w_down", (2816, 1024), 0)]
GAINS = [("g_mix", 1024), ("g_q_a", 256), ("g_kv_a", 128), ("g_qn", 96), ("g_kn", 96), ("ret_decay_fwd", 8), ("ret_decay_bwd", 8),
         ("g_ffn", 1024)]
ORDER = ["g_mix", "w_in", "g_q_a", "w_q_b", "g_kv_a", "w_kv_b", "g_qn", "g_kn", "w_mla_out", "ret_decay_fwd", "ret_decay_bwd",
         "w_ret_out", "w_out", "g_ffn", "w_gate_up", "w_down"]
GAIN_LEN = sum(n for _, n in GAINS)
GAIN_PAD = -(-GAIN_LEN // LANES) * LANES


def _pack_gains(d):
    row = jnp.concatenate([d[n].reshape(1, ln).astype(F32) for n, ln in GAINS], axis=1)
    return jnp.pad(row, ((0, 0), (0, GAIN_PAD - GAIN_LEN)))


def _unpack_gains(row):
    out, off = {}, 0
    for n, ln in GAINS:
        out[n] = row[0, off:off + ln]
        off += ln
    return out


def _unshard(pieces, axis):
    if axis == 0:
        return pieces.reshape((N_DEV * pieces.shape[1], pieces.shape[2]))
    return jnp.concatenate([pieces[p] for p in range(N_DEV)], axis=1)


def _reshard(full, axis):
    if axis == 0:
        return full.reshape((N_DEV, full.shape[0] // N_DEV, full.shape[1]))
    c = full.shape[1] // N_DEV
    return jnp.stack([full[:, c * p:c * (p + 1)] for p in range(N_DEV)])


def _all_gather(shards):
    n = len(shards)

    def body(*refs):
        x_refs, out_refs = refs[:n], refs[n:2 * n]
        send_sems, recv_sems, local_sems = refs[2 * n:]
        x, y, c = lax.axis_index("x"), lax.axis_index("y"), lax.axis_index("c")
        me, sibling = (x, y, c), (x, y, 1 - c)
        chips = [(1 - x, y), (x, 1 - y), (1 - x, 1 - y)]

        def slot(a, px, py, pc):
            return out_refs[a].at[4 * px + 2 * py + pc]

        def copy(a, k, block, to, from_input=False):
            return pltpu.make_async_remote_copy(
                src_ref=x_refs[a] if from_input else slot(a, *block), dst_ref=slot(a, *block),
                send_sem=send_sems.at[a, k], recv_sem=recv_sems.at[a, k], device_id=to, device_id_type=pl.DeviceIdType.MESH)

        mine = [pltpu.make_async_copy(x_refs[a], slot(a, *me), local_sems.at[a]) for a in range(n)]
        first = [copy(a, 0, me, sibling, True) for a in range(n)]
        first += [copy(a, 1 + j, me, (*chip, c), True) for j, chip in enumerate(chips) for a in range(n)]
        for cp in mine + first:
            cp.start()
        passed = []
        for j, chip in enumerate(chips):
            for a in range(n):
                copy(a, 1 + j, (*chip, c), me).wait_recv()
                passed.append(copy(a, 4 + j, (*chip, c), sibling))
                passed[-1].start()
        for a in range(n):
            copy(a, 0, sibling, me).wait_recv()
        for j, chip in enumerate(chips):
            for a in range(n):
                copy(a, 4 + j, (*chip, 1 - c), me).wait_recv()
        for cp in first + passed:
            cp.wait_send()
        for cp in mine:
            cp.wait()

    any_spec = pl.BlockSpec(memory_space=pl.ANY)
    return pl.pallas_call(
        body, name="all_gather_weights", out_shape=[jax.ShapeDtypeStruct((N_DEV,) + s.shape, s.dtype) for s in shards],
        in_specs=[any_spec] * n, out_specs=[any_spec] * n,
        scratch_shapes=[pltpu.SemaphoreType.DMA((n, 7)), pltpu.SemaphoreType.DMA((n, 7)), pltpu.SemaphoreType.DMA((n,))],
    )(*shards)


def _all_to_all(pieces):
    n = len(pieces)

    def body(*refs):
        in_refs, out_refs = refs[:n], refs[n:2 * n]
        send_sems, recv_sems, local_sems = refs[2 * n:]
        x, y, c = lax.axis_index("x"), lax.axis_index("y"), lax.axis_index("c")
        my_id = 4 * x + 2 * y + c
        flips = [(fx, fy, fc) for fx in (0, 1) for fy in (0, 1) for fc in (0, 1)][1:]

        def copy(a, kk, f):
            p = (x ^ f[0], y ^ f[1], c ^ f[2])
            return pltpu.make_async_remote_copy(
                src_ref=in_refs[a].at[4 * p[0] + 2 * p[1] + p[2]], dst_ref=out_refs[a].at[my_id],
                send_sem=send_sems.at[a, kk], recv_sem=recv_sems.at[a, kk], device_id=p, device_id_type=pl.DeviceIdType.MESH)

        mine = [pltpu.make_async_copy(in_refs[a].at[my_id], out_refs[a].at[my_id], local_sems.at[a]) for a in range(n)]
        copies = [copy(a, kk, f) for kk, f in enumerate(flips) for a in range(n)]
        for cp in mine + copies:
            cp.start()
        for cp in copies:
            cp.wait_recv()
        for cp in copies:
            cp.wait_send()
        for cp in mine:
            cp.wait()

    any_spec = pl.BlockSpec(memory_space=pl.ANY)
    return pl.pallas_call(
        body, name="all_to_all_grads", out_shape=[jax.ShapeDtypeStruct(p.shape, p.dtype) for p in pieces],
        in_specs=[any_spec] * n, out_specs=[any_spec] * n,
        scratch_shapes=[pltpu.SemaphoreType.DMA((n, 7)), pltpu.SemaphoreType.DMA((n, 7)), pltpu.SemaphoreType.DMA((n,))],
    )(*pieces)


def _flip_peers(x, y, c):
    flips = [(fx, fy, fc) for fx in (0, 1) for fy in (0, 1) for fc in (0, 1)][1:]
    return [(x ^ fx, y ^ fy, c ^ fc) for fx, fy, fc in flips]


def _split_copies(in_refs, land_refs, send_sems, recv_sems, gather):
    x, y, c = lax.axis_index("x"), lax.axis_index("y"), lax.axis_index("c")
    my_id = 4 * x + 2 * y + c
    copies = []
    for kk, p in enumerate(_flip_peers(x, y, c)):
        for a in range(len(in_refs)):
            src = in_refs[a] if gather else in_refs[a].at[4 * p[0] + 2 * p[1] + p[2]]
            copies.append(pltpu.make_async_remote_copy(
                src_ref=src, dst_ref=land_refs[a].at[my_id], send_sem=send_sems.at[a * 7 + kk], recv_sem=recv_sems.at[a * 7 + kk],
                device_id=p, device_id_type=pl.DeviceIdType.MESH))
    return copies


def _exchange_start(name, srcs, gather, after=None):
    n = len(srcs)
    first_out = 2 * n + (0 if after is None else 1)

    def body(*refs):
        for cp in _split_copies(refs[:n], refs[n:2 * n], refs[first_out], refs[first_out + 1], gather):
            cp.start()
        refs[-1][...] = jnp.zeros_like(refs[-1])

    hbm, sem = pl.BlockSpec(memory_space=pltpu.HBM), pl.BlockSpec(memory_space=pltpu.SEMAPHORE)
    land_shapes = [((N_DEV,) + s.shape if gather else s.shape, s.dtype) for s in srcs]
    lands = [pltpu.with_memory_space_constraint(lax.empty(shp, dt), pltpu.HBM) for shp, dt in land_shapes]
    srcs = [pltpu.with_memory_space_constraint(s, pltpu.HBM) for s in srcs]
    res = pl.pallas_call(
        body, name=name,
        out_shape=[pltpu.SemaphoreType.DMA((7 * n,)), pltpu.SemaphoreType.DMA((7 * n,))] + [pltpu.HBM(s.shape, s.dtype) for s in srcs]
        + [pltpu.HBM(shp, dt) for shp, dt in land_shapes] + [jax.ShapeDtypeStruct((8, LANES), F32)],
        in_specs=[hbm] * (2 * n) + ([] if after is None else [pl.BlockSpec(memory_space=pl.ANY)]),
        out_specs=[sem, sem] + [hbm] * (2 * n) + [pl.BlockSpec(memory_space=pltpu.VMEM)],
        input_output_aliases={i: 2 + i for i in range(2 * n)},
        compiler_params=pltpu.CompilerParams(has_side_effects=pltpu.SideEffectType.DATAFLOW_SIDE_EFFECTING),
    )(*srcs, *lands, *([] if after is None else [after]))
    return res[0], res[1], res[2:2 + n], res[2 + n:2 + 2 * n], res[-1]


def _exchange_wait(name, handles, after, gather):
    send_sems, recv_sems, srcs, lands, _ = handles
    n = len(srcs)

    def body(*refs):
        for cp in _split_copies(refs[:n], refs[n:2 * n], refs[2 * n], refs[2 * n + 1], gather):
            cp.wait_send()
            cp.wait_recv()

    hbm, sem = pl.BlockSpec(memory_space=pltpu.HBM), pl.BlockSpec(memory_space=pltpu.SEMAPHORE)
    res = pl.pallas_call(
        body, name=name, out_shape=[pltpu.HBM(t.shape, t.dtype) for t in list(srcs) + list(lands)],
        in_specs=[hbm] * (2 * n) + [sem, sem, pl.BlockSpec(memory_space=pl.ANY)], out_specs=[hbm] * (2 * n),
        input_output_aliases={i: i for i in range(2 * n)},
        compiler_params=pltpu.CompilerParams(has_side_effects=pltpu.SideEffectType.DATAFLOW_SIDE_EFFECTING),
    )(*srcs, *lands, send_sems, recv_sems, after)
    my_id = 4 * lax.axis_index("x") + 2 * lax.axis_index("y") + lax.axis_index("c")
    own = [s if gather else lax.dynamic_index_in_dim(s, my_id, 0, keepdims=False) for s in res[:n]]
    return [lax.dynamic_update_index_in_dim(land, o, my_id, 0) for land, o in zip(res[n:], own)]


def _adamw(name, parts, w, m, v):
    rows, cols = w.shape
    tr = _pick(rows, (128, 64, 32, 16, 8))
    pspec = pl.BlockSpec((N_DEV, tr, cols), lambda i: (0, i, 0))
    rspec = pl.BlockSpec((tr, cols), lambda i: (i, 0))

    def body(p_ref, w_ref, m_ref, v_ref, g_ref, d_ref, m2_ref, v2_ref):
        g, d, m2, v2 = _adamw_fn([p_ref[s] for s in range(N_DEV)], w_ref[...], m_ref[...], v_ref[...])
        g_ref[...], d_ref[...], m2_ref[...], v2_ref[...] = g, d, m2, v2

    return pl.pallas_call(
        body, name=name, grid=(rows // tr,), in_specs=[pspec, rspec, rspec, rspec], out_specs=[rspec] * 4,
        out_shape=[jax.ShapeDtypeStruct((rows, cols), F32)] * 4,
        compiler_params=pltpu.CompilerParams(dimension_semantics=("parallel",), vmem_limit_bytes=VMEM_LIMIT),
    )(parts, w, m, v)


def kernel(x, positions, g_mix, w_in, g_q_a, w_q_b, g_kv_a, w_kv_b, g_qn, g_kn, w_mla_out, ret_decay_fwd, ret_decay_bwd, w_ret_out, w_out, g_ffn, w_gate_up, w_down, loss_target, m_g_mix, m_w_in, m_g_q_a, m_w_q_b, m_g_kv_a, m_w_kv_b, m_g_qn, m_g_kn, m_w_mla_out, m_ret_decay_fwd, m_ret_decay_bwd, m_w_ret_out, m_w_out, m_g_ffn, m_w_gate_up, m_w_down, v_g_mix, v_w_in, v_g_q_a, v_w_q_b, v_g_kv_a, v_w_kv_b, v_g_qn, v_g_kn, v_w_mla_out, v_ret_decay_fwd, v_ret_decay_bwd, v_w_ret_out, v_w_out, v_g_ffn, v_w_gate_up, v_w_down):
    w = dict(g_mix=g_mix, w_in=w_in, g_q_a=g_q_a, w_q_b=w_q_b, g_kv_a=g_kv_a, w_kv_b=w_kv_b, g_qn=g_qn, g_kn=g_kn, w_mla_out=w_mla_out,
             ret_decay_fwd=ret_decay_fwd, ret_decay_bwd=ret_decay_bwd, w_ret_out=w_ret_out, w_out=w_out, g_ffn=g_ffn,
             w_gate_up=w_gate_up, w_down=w_down)
    m = dict(g_mix=m_g_mix, w_in=m_w_in, g_q_a=m_g_q_a, w_q_b=m_w_q_b, g_kv_a=m_g_kv_a, w_kv_b=m_w_kv_b, g_qn=m_g_qn, g_kn=m_g_kn,
             w_mla_out=m_w_mla_out, ret_decay_fwd=m_ret_decay_fwd, ret_decay_bwd=m_ret_decay_bwd, w_ret_out=m_w_ret_out, w_out=m_w_out,
             g_ffn=m_g_ffn, w_gate_up=m_w_gate_up, w_down=m_w_down)
    v = dict(g_mix=v_g_mix, w_in=v_w_in, g_q_a=v_g_q_a, w_q_b=v_w_q_b, g_kv_a=v_g_kv_a, w_kv_b=v_w_kv_b, g_qn=v_g_qn, g_kn=v_g_kn,
             w_mla_out=v_w_mla_out, ret_decay_fwd=v_ret_decay_fwd, ret_decay_bwd=v_ret_decay_bwd, w_ret_out=v_w_ret_out, w_out=v_w_out,
             g_ffn=v_g_ffn, w_gate_up=v_w_gate_up, w_down=v_w_down)
    gains = {n: w[n].reshape(1, ln) for n, ln in GAINS}

    axis_of = {n: axis for n, _, axis in MATS}
    later = [n for n, _, _ in MATS if n not in FIRST_WEIGHTS]
    gathered = _all_gather([w[n].astype(WIRE) for n in FIRST_WEIGHTS])
    W = {n: _unshard(g, axis_of[n]) for n, g in zip(FIRST_WEIGHTS, gathered)}
    later_handles = _exchange_start("gather_later_start", [w[n].astype(WIRE) for n in later], True, after=gathered[0])

    def late_weights(after):
        lands = _exchange_wait("gather_later_wait", later_handles, after, True)
        return {n: _unshard(g, axis_of[n]) for n, g in zip(later, lands)}

    grad_groups = []

    def grad_hook(g):
        names = tuple(g)
        handles = _exchange_start("grads_start_%d" % len(grad_groups), [_reshard(g[n], axis_of[n]).astype(GWIRE) for n in names], False)
        grad_groups.append((names, handles))
        return handles[4]

    S = x.shape[1]
    pos = positions.reshape(S, 1).astype(F32)
    loss_rows, grad_x, gG, gW = _local_step(x.reshape(S, D_MODEL), pos, loss_target.reshape(S, D_MODEL), gains, W, late_weights, grad_hook)
    loss = lax.psum(jnp.sum(loss_rows), ("x", "y", "c"))

    last = [n for n, _, _ in MATS if n not in EARLY_GRADS + MID_GRADS]
    pieces = [_reshard(gW[n], axis_of[n]).astype(GWIRE) for n in last]
    pieces.append(jnp.broadcast_to(_pack_gains(gG)[None], (N_DEV, 1, GAIN_PAD)))
    late_parts = _all_to_all(pieces)
    parts = dict(zip(last, late_parts))
    for i, (names, handles) in enumerate(grad_groups):
        parts.update(zip(names, _exchange_wait("grads_wait_%d" % i, handles, late_parts[-1], False)))
    out = [dict() for _ in range(4)]
    for n, _, _ in MATS:
        for o, r in zip(out, _adamw("adamw_" + n, parts[n], w[n], m[n], v[n])):
            o[n] = r
    for o, r in zip(out, _adamw("adamw_gains", late_parts[-1], _pack_gains(w), _pack_gains(m), _pack_gains(v))):
        o.update(_unpack_gains(r))
    return (loss, grad_x.reshape(x.shape), *[o[n] for o in out for n in ORDER])
```

```python
import functools

import numpy as np
import jax
import jax.numpy as jnp
from jax import lax
from jax.experimental import pallas as pl
from jax.experimental.pallas import tpu as pltpu

F32 = jnp.float32
MXU = jnp.bfloat16
WIRE = jnp.bfloat16
GWIRE = jnp.bfloat16

N_DEV = 8
D_MODEL = 1024
HEADS = 8
LANES = 128
Q_RANK, KV_RANK = 256, 128
NOPE, ROPE_M, V_M = 64, 32, 64
QK_M = NOPE + ROPE_M
RQK, RV = 64, 128
CHUNK = 128
FFN = 2816
IN_WIDTH = 5536
THETA = 10000.0
EPS = 1e-6
LR, B1, B2, AEPS, WD, STEP = 0.001, 0.9, 0.999, 1e-08, 0.01, 10
VMEM_LIMIT = 56 * 1024 * 1024

NN = ((1,), (0,))
NT = ((1,), (1,))
TN = ((0,), (0,))

P_GATES, P_VR, P_GR, P_QR, P_KR, P_CQ, P_CKV, P_KROPE, P_WIDTH = 0, 2048, 3072, 4096, 4608, 5120, 5376, 5504, 5632
O_CQ, O_CKV, O_KROPE, O_QR, O_KR, O_VR, O_GR, O_GATES = 0, 256, 384, 416, 928, 1440, 2464, 3488


def _dot(a, b, dims):
    return lax.dot_general(a, b, (dims, ((), ())), preferred_element_type=F32)


def _pick(dim, cands):
    for c in cands:
        if dim % c == 0:
            return c
    return dim


def _pairs(t):
    return t.reshape(t.shape[0], 4, 2, 2, 32).transpose(0, 1, 3, 2, 4).reshape(t.shape[0], 512)


def _win_pad(w):
    z = jnp.zeros((w.shape[0], 48), w.dtype)
    kr = w[:, O_KROPE:O_KROPE + 32]
    return jnp.concatenate([w[:, O_GATES:], w[:, O_VR:O_VR + 1024], w[:, O_GR:O_GR + 1024], _pairs(w[:, O_QR:O_QR + 512]),
                            _pairs(w[:, O_KR:O_KR + 512]), w[:, :O_CKV], w[:, O_CKV:O_KROPE], kr[:, :16], z, kr[:, 16:], z], axis=1)


def _win_unpad(g):
    return jnp.concatenate([g[:, P_CQ:P_CQ + 256], g[:, P_CKV:P_CKV + 128], g[:, P_KROPE:P_KROPE + 16], g[:, P_KROPE + 64:P_KROPE + 80],
                            _pairs(g[:, P_QR:P_QR + 512]), _pairs(g[:, P_KR:P_KR + 512]), g[:, P_VR:P_VR + 1024],
                            g[:, P_GR:P_GR + 1024], g[:, P_GATES:P_GATES + 2048]], axis=1)


def _qk_pad(t):
    z = jnp.zeros(t.shape[:-1] + (32,), t.dtype)
    return jnp.concatenate([t[..., 64:80], t[..., 0:48], t[..., 80:96], t[..., 48:64], z], axis=-1)


def _qk_unpad(p):
    return jnp.concatenate([p[..., 16:64], p[..., 80:96], p[..., 0:16], p[..., 64:80]], axis=-1)


def _wq_pad(w):
    return _qk_pad(w.reshape(Q_RANK, HEADS, QK_M)).reshape(Q_RANK, HEADS * LANES)


def _wq_unpad(g):
    return _qk_unpad(g.reshape(Q_RANK, HEADS, LANES)).reshape(Q_RANK, HEADS * QK_M)


def _wkv_pad(w):
    t = w.reshape(KV_RANK, HEADS, NOPE + V_M)
    z = lambda n: jnp.zeros((KV_RANK, HEADS, n), w.dtype)
    wk = jnp.concatenate([z(16), t[..., 0:48], z(16), t[..., 48:64], z(32)], axis=-1)
    wv = jnp.concatenate([t[..., 64:128], z(64)], axis=-1)
    return wk.reshape(KV_RANK, HEADS * LANES), wv.reshape(KV_RANK, HEADS * LANES)


def _wkv_unpad(dwk, dwv):
    k, v = dwk.reshape(KV_RANK, HEADS, LANES), dwv.reshape(KV_RANK, HEADS, LANES)
    return jnp.concatenate([k[..., 16:64], k[..., 80:96], v[..., 0:64]], axis=-1).reshape(KV_RANK, HEADS * (NOPE + V_M))


def _wmla_pad(w):
    t = w.reshape(HEADS, V_M, D_MODEL)
    return jnp.concatenate([t, jnp.zeros_like(t)], axis=1).reshape(HEADS * LANES, D_MODEL)


def _wmla_unpad(g):
    return g.reshape(HEADS, LANES, D_MODEL)[:, :V_M].reshape(HEADS * V_M, D_MODEL)


def _rowwise(name, fn, rows, ts, ins, outs, accs=(), ncol=1):
    n_in, n_out, n_acc = len(ins), len(outs), len(accs)

    def colmap(col):
        if callable(col):
            return lambda i, j: (i, col(j))
        return lambda i, j: (i, col)

    arrays, in_specs = [], []
    for arr, spec in ins:
        arrays.append(arr)
        if spec is None:
            in_specs.append(pl.BlockSpec(arr.shape, functools.partial(lambda i, j, nd: (0,) * nd, nd=arr.ndim)))
        else:
            in_specs.append(pl.BlockSpec((ts, spec[0]), colmap(spec[1])))
    out_shape, out_specs = [], []
    for total, dtype, width, col in outs:
        out_shape.append(jax.ShapeDtypeStruct((rows, total), dtype))
        out_specs.append(pl.BlockSpec((ts, width), colmap(col)))
    for shp in accs:
        out_shape.append(jax.ShapeDtypeStruct(shp, F32))
        out_specs.append(pl.BlockSpec(shp, functools.partial(lambda i, j, nd: (0,) * nd, nd=len(shp))))

    def body(*refs):
        vals = [r[...] for r in refs[:n_in]]
        res = fn(*vals)
        if not isinstance(res, (tuple, list)):
            res = (res,)
        for r, v in zip(refs[n_in:n_in + n_out], res[:n_out]):
            r[...] = v.astype(r.dtype)
        if n_acc:
            first = jnp.logical_and(pl.program_id(0) == 0, pl.program_id(1) == 0)
            for r, v in zip(refs[n_in + n_out:], res[n_out:]):
                @pl.when(first)
                def _(r=r):
                    r[...] = jnp.zeros_like(r)
                r[...] += v.astype(F32)

    res = pl.pallas_call(
        body, name=name, grid=(rows // ts, ncol), in_specs=in_specs, out_specs=out_specs, out_shape=out_shape,
        compiler_params=pltpu.CompilerParams(dimension_semantics=("arbitrary", "arbitrary"), vmem_limit_bytes=VMEM_LIMIT),
    )(*arrays)
    return res


MM_OPERAND_BYTES = 24 * 1024 * 1024


def _mm(name, a, b, mode, add=None, after=None):
    a_halves, b_halves = a.ndim == 3, b.ndim == 3
    assert not a_halves or mode == "nt"
    assert not b_halves or mode == "tn"
    if mode == "nn":
        (M, K), N = a.shape, b.shape[1]
    elif mode == "nt":
        M, K, N = a.shape[-2], a.shape[-1] * (2 if a_halves else 1), b.shape[0]
    else:
        (K, M), N = a.shape, b.shape[-1] * (2 if b_halves else 1)
    tm = _pick(M, (512, 1408, 256, 128)) if mode == "tn" else _pick(M, (1024, 512, 256, 128))
    tn = _pick(N // 2 if b_halves else N, (1408, 1024, 512, 256, 128))
    fits = lambda t: 2 * (tm + tn) * t * a.dtype.itemsize <= MM_OPERAND_BYTES
    kdiv = K // 2 if a_halves else K
    tk = next(t for t in (K, 4096, 2816, 2048, 1408, 1024, 512, 256, 128) if kdiv % t == 0 and (fits(t) or t == 128))
    nk = K // tk
    dims = {"nn": NN, "nt": NT, "tn": TN}[mode]
    if a_halves:
        per = kdiv // tk
        a_spec = pl.BlockSpec((None, tm, tk), lambda i, j, k: (k // per, i, k % per))
    else:
        a_spec = pl.BlockSpec((tk, tm), lambda i, j, k: (k, i)) if mode == "tn" else pl.BlockSpec((tm, tk), lambda i, j, k: (i, k))
    if b_halves:
        perj = (N // 2) // tn
        b_spec = pl.BlockSpec((None, tk, tn), lambda i, j, k: (j // perj, k, j % perj))
    else:
        b_spec = pl.BlockSpec((tn, tk), lambda i, j, k: (j, k)) if mode == "nt" else pl.BlockSpec((tk, tn), lambda i, j, k: (k, j))
    o_spec = pl.BlockSpec((tm, tn), lambda i, j, k: (i, j))
    has_add = add is not None

    def body(*refs):
        a_ref, b_ref, o_ref = refs[0], refs[1], refs[-1]
        d = _dot(a_ref[...], b_ref[...], dims)
        first = (d + refs[2][...]) if has_add else d
        if nk == 1:
            o_ref[...] = first
        else:
            k = pl.program_id(2)

            @pl.when(k == 0)
            def _():
                o_ref[...] = first

            @pl.when(k > 0)
            def _():
                o_ref[...] += d

    args = [a, b] + ([add] if has_add else []) + ([] if after is None else [after])
    specs = [a_spec, b_spec] + ([o_spec] if has_add else []) + ([] if after is None else [pl.BlockSpec(memory_space=pl.ANY)])
    return pl.pallas_call(
        body, name=name, grid=(M // tm, N // tn, nk), in_specs=specs, out_specs=o_spec,
        out_shape=jax.ShapeDtypeStruct((M, N), F32),
        compiler_params=pltpu.CompilerParams(dimension_semantics=("parallel", "parallel", "arbitrary"), vmem_limit_bytes=VMEM_LIMIT),
    )(*args)


def _mm_rows(name, a, b, fn, row_ins, whole_ins, outs, accs=()):
    (M, K), N = a.shape, b.shape[1]
    tm = _pick(M, (512, 256, 128))
    n_in, n_out = 2 + len(row_ins) + len(whole_ins), len(outs)

    def body(*refs):
        d = _dot(refs[0][...], refs[1][...], NN)
        res = fn(d, *[r[...] for r in refs[2:n_in]])
        for r, v in zip(refs[n_in:n_in + n_out], res[:n_out]):
            r[...] = v.astype(r.dtype)
        for r, v in zip(refs[n_in + n_out:], res[n_out:]):
            @pl.when(pl.program_id(0) == 0)
            def _(r=r):
                r[...] = jnp.zeros_like(r)
            r[...] += v

    row = pl.BlockSpec((tm, N), lambda i: (i, 0))
    whole = lambda t: pl.BlockSpec(t.shape, functools.partial(lambda i, nd: (0,) * nd, nd=t.ndim))
    return pl.pallas_call(
        body, name=name, grid=(M // tm,),
        in_specs=[pl.BlockSpec((tm, K), lambda i: (i, 0)), whole(b)] + [row] * len(row_ins) + [whole(t) for t in whole_ins],
        out_specs=[row] * n_out + [pl.BlockSpec(s, functools.partial(lambda i, nd: (0,) * nd, nd=len(s))) for s in accs],
        out_shape=[jax.ShapeDtypeStruct((M, N), dt) for dt in outs] + [jax.ShapeDtypeStruct(s, F32) for s in accs],
        compiler_params=pltpu.CompilerParams(dimension_semantics=("arbitrary",), vmem_limit_bytes=VMEM_LIMIT),
    )(a, b, *row_ins, *whole_ins)


def _ffn_tiles(S):
    return _pick(S, (512, 256, 128)), _pick(FFN, (1408, 704, 256, 128))


def _gate_up_swiglu(h2, wgu):
    S, K = h2.shape
    tm, tn = _ffn_tiles(S)
    nj = FFN // tn

    def body(a_ref, bg_ref, bu_ref, gu_ref, act_ref):
        a = a_ref[...]
        g, u = _dot(a, bg_ref[...], NN), _dot(a, bu_ref[...], NN)
        gu_ref[0], gu_ref[1] = g, u
        act_ref[...] = _swiglu_fn(g, u).astype(act_ref.dtype)

    return pl.pallas_call(
        body, name="gate_up_swiglu", grid=(S // tm, nj),
        in_specs=[pl.BlockSpec((tm, K), lambda i, j: (i, 0)), pl.BlockSpec((K, tn), lambda i, j: (0, j)),
                  pl.BlockSpec((K, tn), lambda i, j: (0, nj + j))],
        out_specs=[pl.BlockSpec((2, tm, tn), lambda i, j: (0, i, j)), pl.BlockSpec((tm, tn), lambda i, j: (i, j))],
        out_shape=[jax.ShapeDtypeStruct((2, S, FFN), F32), jax.ShapeDtypeStruct((S, FFN), MXU)],
        compiler_params=pltpu.CompilerParams(dimension_semantics=("parallel", "parallel"), vmem_limit_bytes=VMEM_LIMIT),
    )(h2, wgu, wgu)


def _d_act_swiglu(dx2, wdown, gu):
    S, K = dx2.shape
    tm, tn = _ffn_tiles(S)

    def body(a_ref, b_ref, gu_ref, o_ref):
        dact = _dot(a_ref[...], b_ref[...], NT)
        _, vjp = jax.vjp(_swiglu_fn, gu_ref[0], gu_ref[1])
        dg, du = vjp(dact)
        o_ref[0], o_ref[1] = dg.astype(o_ref.dtype), du.astype(o_ref.dtype)

    stacked = pl.BlockSpec((2, tm, tn), lambda i, j: (0, i, j))
    return pl.pallas_call(
        body, name="d_act_swiglu", grid=(S // tm, FFN // tn),
        in_specs=[pl.BlockSpec((tm, K), lambda i, j: (i, 0)), pl.BlockSpec((tn, K), lambda i, j: (j, 0)), stacked],
        out_specs=stacked, out_shape=jax.ShapeDtypeStruct((2, S, FFN), MXU),
        compiler_params=pltpu.CompilerParams(dimension_semantics=("parallel", "parallel"), vmem_limit_bytes=VMEM_LIMIT),
    )(dx2, wdown, gu)


@jax.custom_vjp
def _swap64(x):
    return pltpu.roll(x, 64, 1)


_swap64.defvjp(lambda x: (_swap64(x), None), lambda _, g: (_swap64(g),))


@jax.custom_vjp
def _mxdot(a, b):
    return _dot(a.astype(MXU), b.astype(MXU), NN)


def _mxdot_bwd(res, g):
    a, b = res
    gb = g.astype(MXU)
    return _dot(gb, b.astype(MXU), NT), _dot(a.astype(MXU), gb, TN)


_mxdot.defvjp(lambda a, b: (_mxdot(a, b), (a, b)), _mxdot_bwd)


def _rms(x):
    return x * lax.rsqrt(jnp.mean(x * x, axis=-1, keepdims=True) + EPS)


def _rmsg_fn(x, g):
    return _rms(x) * g


def _silu(x):
    return x * jax.nn.sigmoid(x)


def _tables_fn(pos, inv_m, sgn_m, inv_r, sgn_r):
    am, ar = pos * inv_m, pos * inv_r
    return jnp.cos(am), jnp.sin(am) * sgn_m, jnp.cos(ar), jnp.sin(ar) * sgn_r


def _head_blocks(t):
    return [t[:, LANES * h:LANES * (h + 1)] for h in range(t.shape[1] // LANES)]


def _mla_prep_fn(cq, ckv, kr, cosm, sinm, gqa, gkva, gqn, gkn, wq, wk, wv):
    cqn = _rms(cq) * gqa
    ckvn = _rms(ckv) * gkva
    q_raw = _mxdot(cqn, wq)
    k_raw = _mxdot(ckvn, wk)
    lane = lax.broadcasted_iota(jnp.int32, (1, HEADS * LANES), 1)
    v = _mxdot(ckvn, wv) + (lane % LANES == V_M).astype(F32)

    def norm_rope(blocks, g, extra):
        outs = []
        for b in blocks:
            if extra is not None:
                b = b + extra
            n = b * lax.rsqrt(jnp.sum(b * b, axis=-1, keepdims=True) * (1.0 / QK_M) + EPS) * g
            outs.append(n * cosm + _swap64(n) * sinm)
        return jnp.concatenate(outs, axis=1)

    q = norm_rope(_head_blocks(q_raw), gqn, None)
    k = norm_rope(_head_blocks(k_raw), gkn, kr)
    return q, k, v


def _ret_prep_fn(qr, kr, cosr, sinr):
    def rope(t, scale):
        return jnp.concatenate([(b * cosr + _swap64(b) * sinr) * scale for b in _head_blocks(t)], axis=1)
    return rope(qr, 1.0), rope(kr, RQK ** -0.5)


def _ret_post_fn(rf, rb, gr):
    ret = rf + rb
    outs = []
    for b, g in zip(_head_blocks(ret), _head_blocks(gr)):
        outs.append(_silu(g) * _rms(b))
    return jnp.concatenate(outs, axis=1)


def _merge_fn(ga, gb, ya, yb):
    return jax.nn.sigmoid(ga) * ya + jax.nn.sigmoid(gb) * yb


def _swiglu_fn(gate, up):
    return _silu(gate) * up


def _loss_fn(x2, tgt):
    d = x2 - tgt
    return d * (1.0 / D_MODEL), 0.5 * jnp.sum(d * d, axis=0, keepdims=True) * (1.0 / D_MODEL)


def _adamw_fn(parts, w, m, v):
    g = parts[0].astype(F32)
    for p in range(1, N_DEV):
        g = g + parts[p].astype(F32)
    m2 = B1 * m + (1.0 - B1) * g
    v2 = B2 * v + (1.0 - B2) * jnp.square(g)
    m_hat = m2 / (1.0 - B1 ** STEP)
    v_hat = v2 / (1.0 - B2 ** STEP)
    delta = -LR * (m_hat / (jnp.sqrt(v_hat) + AEPS) + WD * w)
    return g, delta, m2, v2


SCALE = QK_M ** -0.5
LOG2E = 1.4426950408889634
FLASH_ROWS = 32


def _flash_fwd(q, k, v):
    S = q.shape[0]
    tq = tk = _pick(S, (512, 256, 128))
    ncb = tk // LANES
    nkv = S // tk
    assert nkv % 2 == 0, "kv tiles are processed in pairs"
    mrows = 64
    c = SCALE * LOG2E

    def body(q_ref, k_ref, v_ref, o_ref, obf_ref, lse_ref, s_a, p_a, s_b, p_b, m_sc, a_sc, acc_sc):
        m_sc[...] = jnp.full_like(m_sc, -jnp.inf)
        acc_sc[...] = jnp.zeros_like(acc_sc)
        qb = q_ref[...]

        def scores(j, s_buf):
            s_buf[...] = _dot(qb, k_ref[pl.ds(pl.multiple_of(j * tk, tk), tk), :], NT)

        def stage(j, s_buf, p_buf, s_next):
            scores(jnp.minimum(j + 1, nkv - 1), s_next)
            for r in range(tq // mrows):
                rows = slice(r * mrows, (r + 1) * mrows)
                cols = [s_buf[rows, LANES * cb:LANES * (cb + 1)] for cb in range(ncb)]
                m_prev = m_sc[rows, :]
                row_max = jnp.max(functools.reduce(jnp.maximum, cols), axis=-1, keepdims=True)
                m_new = jnp.maximum(m_prev, jnp.broadcast_to(row_max, (mrows, LANES)))
                a_sc[rows, :] = jnp.exp2((m_prev - m_new) * c)
                m_sc[rows, :] = m_new
                for cb in range(ncb):
                    p_buf[rows, LANES * cb:LANES * (cb + 1)] = jnp.exp2((cols[cb] - m_new) * c).astype(p_buf.dtype)
            acc_sc[...] = a_sc[...] * acc_sc[...] + _dot(p_buf[...], v_ref[pl.ds(pl.multiple_of(j * tk, tk), tk), :], NN)

        scores(0, s_a)

        def pair_step(t, carry):
            stage(2 * t, s_a, p_a, s_b)
            stage(2 * t + 1, s_b, p_b, s_a)
            return carry

        lax.fori_loop(0, nkv // 2, pair_step, 0, unroll=4)
        acc = acc_sc[...]
        lane = lax.broadcasted_iota(jnp.int32, (1, LANES), 1)
        l = jnp.sum(jnp.where(lane == V_M, acc, 0.0), axis=-1, keepdims=True)
        o = acc / l
        o_ref[...] = o
        obf_ref[...] = o.astype(obf_ref.dtype)
        lse_ref[...] = m_sc[...] * c + jnp.log2(jnp.broadcast_to(l, (tq, LANES)))

    qspec = pl.BlockSpec((tq, LANES), lambda h, i: (i, h))
    kspec = pl.BlockSpec((S, LANES), lambda h, i: (0, h))
    full = jax.ShapeDtypeStruct((S, HEADS * LANES), F32)
    return pl.pallas_call(
        body, name="flash_fwd", grid=(HEADS, S // tq), in_specs=[qspec, kspec, kspec], out_specs=[qspec, qspec, qspec],
        out_shape=[full, jax.ShapeDtypeStruct((S, HEADS * LANES), MXU), full],
        scratch_shapes=[pltpu.VMEM((tq, tk), F32), pltpu.VMEM((tq, tk), MXU)] * 2 + [pltpu.VMEM((tq, LANES), F32)] * 3,
        compiler_params=pltpu.CompilerParams(dimension_semantics=("parallel", "arbitrary"), vmem_limit_bytes=VMEM_LIMIT),
    )(q, k, v)


def _delta_fn(o, do):
    outs = [jnp.broadcast_to(jnp.sum(a * b, axis=-1, keepdims=True), a.shape) for a, b in zip(_head_blocks(o), _head_blocks(do))]
    return do, jnp.concatenate(outs, axis=1)


def _flash_bwd(q, k, v, do, lse, delta):
    S = q.shape[0]
    tq = tk = _pick(S, (512, 256, 128))
    ncb = tk // LANES
    c = SCALE * LOG2E

    nq = S // tq
    assert nq % 2 == 0, "q tiles are processed in pairs"

    def body(q_ref, k_ref, v_ref, do_ref, lse_ref, dl_ref, dq_ref, dk_ref, dv_ref, s_a, dp_a, p_a, ds_a, s_b, dp_b, p_b, ds_b, dk_sc, dv_sc):
        @pl.when(pl.program_id(1) == 0)
        def _():
            dq_ref[...] = jnp.zeros_like(dq_ref)

        dk_sc[...] = jnp.zeros_like(dk_sc)
        dv_sc[...] = jnp.zeros_like(dv_sc)
        kb, vb = k_ref[...], v_ref[...]

        def scores(i, s_buf, dp_buf):
            q_rows = pl.ds(pl.multiple_of(i * tq, tq), tq)
            s_buf[...] = _dot(q_ref[q_rows, :], kb, NT)
            dp_buf[...] = _dot(do_ref[q_rows, :], vb, NT)

        def stage(i, s_buf, dp_buf, p_buf, ds_buf, s_next, dp_next):
            scores(jnp.minimum(i + 1, nq - 1), s_next, dp_next)
            for r in range(tq // FLASH_ROWS):
                rows = slice(r * FLASH_ROWS, (r + 1) * FLASH_ROWS)
                grows = pl.ds(pl.multiple_of(i * tq + r * FLASH_ROWS, FLASH_ROWS), FLASH_ROWS)
                lse_b, dl_b = lse_ref[grows, :], dl_ref[grows, :]
                for cb in range(ncb):
                    sl = slice(LANES * cb, LANES * (cb + 1))
                    p = jnp.exp2(s_buf[rows, sl] * c - lse_b)
                    p_buf[rows, sl] = p.astype(p_buf.dtype)
                    ds_buf[rows, sl] = (p * (dp_buf[rows, sl] - dl_b) * SCALE).astype(ds_buf.dtype)
            q_rows = pl.ds(pl.multiple_of(i * tq, tq), tq)
            dv_sc[...] += _dot(p_buf[...], do_ref[q_rows, :], TN)
            dk_sc[...] += _dot(ds_buf[...], q_ref[q_rows, :], TN)
            dq_ref[q_rows, :] += _dot(ds_buf[...], kb, NN)

        scores(0, s_a, dp_a)

        def pair_step(t, carry):
            stage(2 * t, s_a, dp_a, p_a, ds_a, s_b, dp_b)
            stage(2 * t + 1, s_b, dp_b, p_b, ds_b, s_a, dp_a)
            return carry

        lax.fori_loop(0, nq // 2, pair_step, 0, unroll=2)
        dk_ref[...] = dk_sc[...]
        dv_ref[...] = dv_sc[...]

    hspec = pl.BlockSpec((S, LANES), lambda h, j: (0, h))
    kspec = pl.BlockSpec((tk, LANES), lambda h, j: (j, h))
    full = jax.ShapeDtypeStruct((S, HEADS * LANES), F32)
    tile_bufs = [pltpu.VMEM((tq, tk), F32), pltpu.VMEM((tq, tk), F32), pltpu.VMEM((tq, tk), MXU), pltpu.VMEM((tq, tk), MXU)]
    return pl.pallas_call(
        body, name="flash_bwd", grid=(HEADS, S // tk), in_specs=[hspec, kspec, kspec, hspec, hspec, hspec],
        out_specs=[hspec, kspec, kspec], out_shape=[full, full, full],
        scratch_shapes=tile_bufs + tile_bufs + [pltpu.VMEM((tk, LANES), F32), pltpu.VMEM((tk, LANES), F32)],
        compiler_params=pltpu.CompilerParams(dimension_semantics=("parallel", "arbitrary"), vmem_limit_bytes=VMEM_LIMIT),
    )(q, k, v, do, lse, delta)


def _ret_consts(lgh, head, rev):
    C = CHUNK
    lane = lax.broadcasted_iota(jnp.int32, (1, LANES), 1)
    hm = ((lane // 32) % 2 == head % 2).astype(F32)
    r = lax.broadcasted_iota(jnp.int32, (C, C), 0)
    c = lax.broadcasted_iota(jnp.int32, (C, C), 1)
    diff = ((c - r) if rev else (r - c)).astype(F32)
    mask = (diff > 0) if rev else (diff >= 0)
    dpos = jnp.maximum(diff, 0.0)
    din = jnp.where(mask, jnp.exp(lgh * dpos), 0.0)
    idx = lax.broadcasted_iota(jnp.int32, (C, 1), 0).astype(F32)
    eq = (C - idx) if rev else (idx + 1.0)
    ek = idx if rev else (C - 1.0 - idx)
    qd, kd = jnp.exp(lgh * eq), jnp.exp(lgh * ek)
    cd = jnp.exp(lgh * jnp.full((1, 1), float(C), F32))
    return hm, din, dpos, qd, kd, cd, eq, ek


RET_HEADS_PER_STEP = 4


def _ret_fwd(name, qt, kt, proj, lg, rev):
    S = qt.shape[0]
    C = CHUNK
    TB = _pick(S, (512, 256, 128))
    cb, nb = TB // C, S // TB
    hps = RET_HEADS_PER_STEP
    blk = (lambda g: nb - 1 - g) if rev else (lambda g: g)

    def body(lg_ref, q_ref, k_ref, v_ref, o_ref, st_ref, state_sc):
        hg, g = pl.program_id(0), pl.program_id(1)

        @pl.when(g == 0)
        def _():
            state_sc[...] = jnp.zeros_like(state_sc)

        consts = [_ret_consts(lg_ref[hg * hps + u], u, rev) for u in range(hps)]
        order = list(reversed(range(cb))) if rev else list(range(cb))
        units = [(cc, u) for cc in order for u in range(hps)]

        def operands(cc, u):
            rows = pl.ds(cc * C, C)
            pair = slice(LANES * (u // 2), LANES * (u // 2 + 1))
            hm = consts[u][0]
            return q_ref[rows, pair] * hm, k_ref[rows, pair] * hm, v_ref[rows, LANES * u:LANES * (u + 1)].astype(MXU)

        a, inc = {}, {}
        for cc, u in units:
            q, k, v = operands(cc, u)
            a[cc, u] = _dot(q.astype(MXU), k.astype(MXU), NT) * consts[u][1]
            inc[cc, u] = _dot((k * consts[u][4]).astype(MXU), v, TN)
        for u in range(hps):
            st = state_sc[u]
            for cc in order:
                st_ref[u, cc] = st
                st = st * consts[u][5] + inc[cc, u]
            state_sc[u] = st
        for cc, u in units:
            q, _, v = operands(cc, u)
            cross = _dot((q * consts[u][3]).astype(MXU), st_ref[u, cc].astype(MXU), NN)
            o_ref[pl.ds(cc * C, C), LANES * u:LANES * (u + 1)] = _dot(a[cc, u].astype(MXU), v, NN) + cross

    qk_spec = pl.BlockSpec((TB, LANES * hps // 2), lambda h, g: (blk(g), h))
    return pl.pallas_call(
        body, name=name, grid=(HEADS // hps, nb),
        in_specs=[pl.BlockSpec(memory_space=pltpu.SMEM), qk_spec, qk_spec,
                  pl.BlockSpec((TB, LANES * hps), lambda h, g: (blk(g), P_VR // (LANES * hps) + h))],
        out_specs=[pl.BlockSpec((TB, LANES * hps), lambda h, g: (blk(g), h)),
                   pl.BlockSpec((hps, cb, LANES, LANES), lambda h, g: (h, blk(g), 0, 0))],
        out_shape=[jax.ShapeDtypeStruct((S, HEADS * LANES), F32), jax.ShapeDtypeStruct((HEADS, S // C, LANES, LANES), F32)],
        scratch_shapes=[pltpu.VMEM((hps, LANES, LANES), F32)],
        compiler_params=pltpu.CompilerParams(dimension_semantics=("parallel", "arbitrary"), vmem_limit_bytes=VMEM_LIMIT),
    )(lg, qt, kt, proj)


def _ret_bwd(name, qt, kt, proj, dret, states, lg, rev):
    S = qt.shape[0]
    C = CHUNK
    TB = _pick(S, (512, 256, 128))
    cb, nb = TB // C, S // TB
    hps = RET_HEADS_PER_STEP
    blk = (lambda g: g) if rev else (lambda g: nb - 1 - g)

    def body(lg_ref, q_ref, k_ref, v_ref, do_ref, st_ref, dq_ref, dk_ref, dv_ref, dlg_ref, ds_sc, acc_cc, acc_q, acc_k, acc_s):
        hg, g = pl.program_id(0), pl.program_id(1)

        @pl.when(g == 0)
        def _():
            ds_sc[...] = jnp.zeros_like(ds_sc)
            acc_cc[...] = jnp.zeros_like(acc_cc)
            acc_q[...] = jnp.zeros_like(acc_q)
            acc_k[...] = jnp.zeros_like(acc_k)
            acc_s[...] = jnp.zeros_like(acc_s)

        lgs = [lg_ref[hg * hps + u] for u in range(hps)]
        consts = [_ret_consts(lgs[u], u, rev) for u in range(hps)]
        order = list(range(cb)) if rev else list(reversed(range(cb)))
        units = [(cc, u) for cc in order for u in range(hps)]

        def operands(cc, u):
            rows = pl.ds(cc * C, C)
            pair = slice(LANES * (u // 2), LANES * (u // 2 + 1))
            head = slice(LANES * u, LANES * (u + 1))
            hm = consts[u][0]
            return q_ref[rows, pair] * hm, k_ref[rows, pair] * hm, v_ref[rows, head].astype(MXU), do_ref[rows, head].astype(MXU)

        a, dp, dqs, inc = {}, {}, {}, {}
        for cc, u in units:
            q, k, vb, dob = operands(cc, u)
            a[cc, u] = _dot(q.astype(MXU), k.astype(MXU), NT)
            dp[cc, u] = _dot(dob, vb, NT)
            dqs[cc, u] = _dot(dob, st_ref[u, cc].astype(MXU), NT)
            inc[cc, u] = _dot((q * consts[u][3]).astype(MXU), dob, TN)
        dsn = {}
        for u in range(hps):
            ds = ds_sc[u]
            for cc in order:
                dsn[cc, u] = ds
                ds = ds * consts[u][5] + inc[cc, u]
            ds_sc[u] = ds
        even = {}
        for cc, u in units:
            hm, din, dpos, qd, kd, cd, eq, ek = consts[u]
            rows, head = pl.ds(cc * C, C), slice(LANES * u, LANES * (u + 1))
            q, k, vb, dob = operands(cc, u)
            qb, kb = q.astype(MXU), k.astype(MXU)
            dsnb = dsn[cc, u].astype(MXU)
            da = (dp[cc, u] * din).astype(MXU)
            vds = _dot(vb, dsnb, NT)
            dq_u = (_dot(da, kb, NN) + dqs[cc, u] * qd) * hm
            dk_u = (_dot(da, qb, TN) + vds * kd) * hm
            if u % 2 == 0:
                even[cc] = (dq_u, dk_u)
            else:
                pair = slice(LANES * (u // 2), LANES * (u // 2 + 1))
                dq_ref[rows, pair] = even[cc][0] + dq_u
                dk_ref[rows, pair] = even[cc][1] + dk_u
            dv_ref[rows, head] = _dot((a[cc, u] * din).astype(MXU), dob, TN) + _dot((k * kd).astype(MXU), dsnb, NN)
            acc_cc[u] += dp[cc, u] * a[cc, u] * din * dpos
            acc_q[u] += dqs[cc, u] * q * (qd * eq)
            acc_k[u] += vds * k * (kd * ek)
            acc_s[u] += dsn[cc, u] * st_ref[u, cc] * (cd * float(C))

        @pl.when(g == nb - 1)
        def _():
            for u in range(hps):
                tot = (jnp.sum(acc_cc[u], keepdims=True) + jnp.sum(acc_q[u], keepdims=True)
                       + jnp.sum(acc_k[u], keepdims=True) + jnp.sum(acc_s[u], keepdims=True))
                dlg_ref[u] = jnp.broadcast_to(tot * lgs[u], (8, LANES))

    full = jax.ShapeDtypeStruct((S, HEADS * LANES), F32)
    hspec = pl.BlockSpec((TB, LANES * hps), lambda h, g: (blk(g), h))
    qk_spec = pl.BlockSpec((TB, LANES * hps // 2), lambda h, g: (blk(g), h))
    return pl.pallas_call(
        body, name=name, grid=(HEADS // hps, nb),
        in_specs=[pl.BlockSpec(memory_space=pltpu.SMEM), qk_spec, qk_spec,
                  pl.BlockSpec((TB, LANES * hps), lambda h, g: (blk(g), P_VR // (LANES * hps) + h)),
                  hspec,
                  pl.BlockSpec((hps, cb, LANES, LANES), lambda h, g: (h, blk(g), 0, 0))],
        out_specs=[qk_spec, qk_spec, hspec, pl.BlockSpec((hps, 8, LANES), lambda h, g: (h, 0, 0))],
        out_shape=[jax.ShapeDtypeStruct(qt.shape, F32), jax.ShapeDtypeStruct(kt.shape, F32), full,
                   jax.ShapeDtypeStruct((HEADS, 8, LANES), F32)],
        scratch_shapes=[pltpu.VMEM((hps, LANES, LANES), F32), pltpu.VMEM((hps, C, C), F32), pltpu.VMEM((hps, C, LANES), F32),
                        pltpu.VMEM((hps, C, LANES), F32), pltpu.VMEM((hps, LANES, LANES), F32)],
        compiler_params=pltpu.CompilerParams(dimension_semantics=("parallel", "arbitrary"), vmem_limit_bytes=VMEM_LIMIT),
    )(lg, qt, kt, proj, dret, states)


def _rope_consts():
    inv16 = THETA ** (-jnp.arange(16, dtype=F32) / 16)
    inv32 = THETA ** (-jnp.arange(32, dtype=F32) / 32)
    lane = np.arange(LANES)
    z48 = jnp.zeros((48,), F32)
    inv_m = jnp.concatenate([inv16, z48, inv16, z48])[None, :]
    sgn_m = jnp.asarray(np.where(lane < 16, -1.0, np.where((lane >= 64) & (lane < 80), 1.0, 0.0)), F32)[None, :]
    inv_r = jnp.concatenate([inv32] * 4)[None, :]
    sgn_r = jnp.asarray(np.where(lane < 64, -1.0, 1.0), F32)[None, :]
    return inv_m, sgn_m, inv_r, sgn_r


FIRST_WEIGHTS = ("w_in", "w_q_b", "w_kv_b")
EARLY_GRADS = ("w_down", "w_gate_up", "w_out", "w_ret_out")
MID_GRADS = ("w_mla_out", "w_in")


def _local_step(x, pos, tgt, gains, W, late_weights=None, grad_hook=None, start_after=None):
    S = x.shape[0]
    ts = _pick(S, (256, 128))
    R = lambda a, w=None, c=0: (a, ((a.shape[1] if w is None else w), c))
    W_ = lambda a: (a, None)

    win = _win_pad(W["w_in"])
    wq = _wq_pad(W["w_q_b"])
    wk, wv = _wkv_pad(W["w_kv_b"])
    gqn, gkn = _qk_pad(gains["g_qn"]), _qk_pad(gains["g_kn"])
    g_mix, g_q_a, g_kv_a, g_ffn = gains["g_mix"], gains["g_q_a"], gains["g_kv_a"], gains["g_ffn"]
    lg_f = -jnp.exp(gains["ret_decay_fwd"][0])
    lg_b = -jnp.exp(gains["ret_decay_bwd"][0])

    consts = list(_rope_consts())
    cosm, sinm, cosr, sinr = _rowwise("rope_tables", _tables_fn, S, ts, [R(pos)] + [W_(c) for c in consts],
                                      [(LANES, F32, LANES, 0)] * 4)

    (h,) = _rowwise("rms_mix", _rmsg_fn, S, ts, [R(x), W_(g_mix)], [(D_MODEL, MXU, D_MODEL, 0)])
    proj = _mm("in_proj", h, win, "nn", after=start_after)
    seg = lambda off, w: (proj, (w, off // w))
    mla_ins = [seg(P_CQ, 256), seg(P_CKV, 128), seg(P_KROPE, 128), R(cosm), R(sinm),
               W_(g_q_a), W_(g_kv_a), W_(gqn), W_(gkn), W_(wq), W_(wk), W_(wv)]
    q, k, v = _rowwise("mla_prep", _mla_prep_fn, S, ts, mla_ins, [(HEADS * LANES, MXU, HEADS * LANES, 0)] * 3)
    o, o_bf, lse = _flash_fwd(q, k, v)
    if late_weights is not None:
        W = {**W, **late_weights(lse)}
    wmla = _wmla_pad(W["w_mla_out"])
    wret, wout, wgu, wdown = W["w_ret_out"], W["w_out"], W["w_gate_up"], W["w_down"]
    y_a = _mm("mla_out", o_bf, wmla, "nn")

    ret_ins = [seg(P_QR, 512), seg(P_KR, 512), R(cosr), R(sinr)]
    qt, kt = _rowwise("ret_prep", _ret_prep_fn, S, ts, ret_ins, [(512, F32, 512, 0)] * 2)
    ret_f, st_f = _ret_fwd("ret_fwd_f", qt, kt, proj, lg_f, False)
    ret_b, st_b = _ret_fwd("ret_fwd_b", qt, kt, proj, lg_b, True)
    post_ins = [R(ret_f), R(ret_b), seg(P_GR, 1024)]
    (o_b,) = _rowwise("ret_post", _ret_post_fn, S, ts, post_ins, [(1024, MXU, 1024, 0)])
    y_b = _mm("ret_out", o_b, wret, "nn")

    merge_ins = [seg(P_GATES, 1024), (proj, (1024, 1)), R(y_a), R(y_b)]
    (merged,) = _rowwise("merge", _merge_fn, S, ts, merge_ins, [(D_MODEL, MXU, D_MODEL, 0)])
    def residual_rms(d, xx, g):
        r = d + xx
        return r, _rmsg_fn(r, g)

    x1, h2 = _mm_rows("out_proj_rms_ffn", merged, wout, residual_rms, [x], [g_ffn], [F32, MXU])
    gu, act = _gate_up_swiglu(h2, wgu)

    def residual_loss(d, xx, t):
        dx, rows = _loss_fn(d + xx, t)
        return dx, dx, rows

    dx2, dx2_bf, loss_rows = _mm_rows("down_proj_loss", act, wdown, residual_loss, [x1, tgt], [], [F32, MXU], accs=[(1, D_MODEL)])

    gW = {}
    gW["w_down"] = _mm("d_w_down", act, dx2_bf, "tn")
    dgu = _d_act_swiglu(dx2_bf, wdown, gu)
    gW["w_gate_up"] = _mm("d_w_gate_up", h2, dgu, "tn")
    dh2 = _mm("d_h2", dgu, wgu, "nt")

    def rms_bwd(xx, g, dh, dres):
        _, vjp = jax.vjp(_rmsg_fn, xx, g)
        dx, dg = vjp(dh)
        dx = dx + dres
        return dx, dx, dg

    dx1, dx1_bf, dg_ffn = _rowwise("rms_ffn_bwd", rms_bwd, S, ts, [R(x1), W_(g_ffn), R(dh2), R(dx2)],
                                   [(D_MODEL, F32, D_MODEL, 0), (D_MODEL, MXU, D_MODEL, 0)], accs=[(1, D_MODEL)])
    gW["w_out"] = _mm("d_w_out", merged, dx1_bf, "tn")
    dmerged = _mm("d_merged", dx1_bf, wout, "nt")

    def merge_bwd(ga, gb, ya, yb, dm):
        _, vjp = jax.vjp(_merge_fn, ga, gb, ya, yb)
        return vjp(dm)

    dga, dgb, dy_a, dy_b = _rowwise("merge_bwd", merge_bwd, S, ts, merge_ins + [R(dmerged)], [(D_MODEL, MXU, D_MODEL, 0)] * 4)
    gW["w_ret_out"] = _mm("d_w_ret_out", o_b, dy_b, "tn")
    after_early = [] if grad_hook is None else [W_(grad_hook({n: gW[n] for n in EARLY_GRADS}))]
    do_b = _mm("d_o_b", dy_b, wret, "nt")

    def post_bwd(rf, rb, gr, dob, *_):
        _, vjp = jax.vjp(_ret_post_fn, rf, rb, gr)
        drf, _, dgr = vjp(dob)
        return drf, dgr

    dret, dg_r = _rowwise("ret_post_bwd", post_bwd, S, ts, post_ins + [R(do_b)] + after_early, [(1024, F32, 1024, 0), (1024, MXU, 1024, 0)])
    dq_f, dk_f, dv_f, dlg_f = _ret_bwd("ret_bwd_f", qt, kt, proj, dret, st_f, lg_f, False)
    dq_b, dk_b, dv_b, dlg_b = _ret_bwd("ret_bwd_b", qt, kt, proj, dret, st_b, lg_b, True)

    def ret_prep_bwd(qr, kr, cosr_, sinr_, dqf, dqb, dkf, dkb, dvf, dvb):
        _, vjp = jax.vjp(lambda a, b: _ret_prep_fn(a, b, cosr_, sinr_), qr, kr)
        dqr, dkr = vjp((dqf + dqb, dkf + dkb))
        return dqr, dkr, dvf + dvb

    dq_r, dk_r, dv_r = _rowwise("ret_prep_bwd", ret_prep_bwd, S, ts, ret_ins + [R(t) for t in (dq_f, dq_b, dk_f, dk_b, dv_f, dv_b)],
                                [(512, MXU, 512, 0), (512, MXU, 512, 0), (1024, MXU, 1024, 0)])

    gW_mla_p = _mm("d_w_mla_out", o_bf, dy_a, "tn")
    do = _mm("d_o", dy_a, wmla, "nt")
    do_bf, delta = _rowwise("attn_delta", lambda a, b, *_: _delta_fn(a, b), S, ts, [R(o), R(do)] + after_early,
                            [(HEADS * LANES, MXU, HEADS * LANES, 0), (HEADS * LANES, F32, HEADS * LANES, 0)])
    dq, dk, dv = _flash_bwd(q, k, v, do_bf, lse, delta)

    def mla_prep_bwd(cq, ckv, kr, cosm_, sinm_, gqa, gkva, gqn_, gkn_, wq_, wk_, wv_, dq_, dk_, dv_):
        f = lambda cq, ckv, kr, gqa, gkva, gqn_, gkn_, wq_, wk_, wv_: _mla_prep_fn(cq, ckv, kr, cosm_, sinm_, gqa, gkva, gqn_, gkn_, wq_, wk_, wv_)
        _, vjp = jax.vjp(f, cq, ckv, kr, gqa, gkva, gqn_, gkn_, wq_.astype(F32), wk_.astype(F32), wv_.astype(F32))
        return vjp((dq_, dk_, dv_))

    mb = _rowwise("mla_prep_bwd", mla_prep_bwd, S, ts, mla_ins + [R(dq), R(dk), R(dv)],
                  [(256, MXU, 256, 0), (128, MXU, 128, 0), (128, MXU, 128, 0)],
                  accs=[(1, 256), (1, 128), (1, LANES), (1, LANES), (256, HEADS * LANES), (128, HEADS * LANES), (128, HEADS * LANES)])
    dc_q, dc_kv, dk_rope, dg_q_a, dg_kv_a, dgqn_p, dgkn_p, dwq_p, dwk_p, dwv_p = mb

    dproj = jnp.concatenate([dga, dgb, dv_r, dg_r, dq_r, dk_r, dc_q, dc_kv, dk_rope], axis=1)
    gW["w_in"] = _win_unpad(_mm("d_w_in", h, dproj, "tn"))
    gW["w_mla_out"] = _wmla_unpad(gW_mla_p)
    after_mid = None if grad_hook is None else grad_hook({n: gW[n] for n in MID_GRADS})
    dh = _mm("d_h", dproj, win, "nt", after=after_mid)
    grad_x, _, dg_mix = _rowwise("rms_mix_bwd", lambda a, b, c, d, *_: rms_bwd(a, b, c, d), S, ts,
                                 [R(x), W_(g_mix), R(dh), R(dx1)] + ([] if after_mid is None else [W_(after_mid)]),
                                 [(D_MODEL, F32, D_MODEL, 0), (D_MODEL, MXU, D_MODEL, 0)], accs=[(1, D_MODEL)])
    gW["w_q_b"] = _wq_unpad(dwq_p)
    gW["w_kv_b"] = _wkv_unpad(dwk_p, dwv_p)
    gG = {"g_mix": dg_mix, "g_q_a": dg_q_a, "g_kv_a": dg_kv_a, "g_qn": _qk_unpad(dgqn_p),
          "g_kn": _qk_unpad(dgkn_p), "ret_decay_fwd": dlg_f[:, 0, 0][None, :], "ret_decay_bwd": dlg_b[:, 0, 0][None, :],
          "g_ffn": dg_ffn}
    return loss_rows, grad_x, gG, gW


MATS = [("w_in", (1024, 5536), 1), ("w_q_b", (256, 768), 1), ("w_kv_b", (128, 1024), 1), ("w_mla_out", (512, 1024), 1),
        ("w_ret_out", (1024, 1024), 0), ("w_out", (1024, 1024), 0), ("w_gate_up", (1024, 5632), 1), ("w_down", (2816, 1024), 0)]
GAINS = [("g_mix", 1024), ("g_q_a", 256), ("g_kv_a", 128), ("g_qn", 96), ("g_kn", 96), ("ret_decay_fwd", 8), ("ret_decay_bwd", 8),
         ("g_ffn", 1024)]
ORDER = ["g_mix", "w_in", "g_q_a", "w_q_b", "g_kv_a", "w_kv_b", "g_qn", "g_kn", "w_mla_out", "ret_decay_fwd", "ret_decay_bwd",
         "w_ret_out", "w_out", "g_ffn", "w_gate_up", "w_down"]
GAIN_LEN = sum(n for _, n in GAINS)
GAIN_PAD = -(-GAIN_LEN // LANES) * LANES


def _pack_gains(d):
    row = jnp.concatenate([d[n].reshape(1, ln).astype(F32) for n, ln in GAINS], axis=1)
    return jnp.pad(row, ((0, 0), (0, GAIN_PAD - GAIN_LEN)))


def _unpack_gains(row):
    out, off = {}, 0
    for n, ln in GAINS:
        out[n] = row[0, off:off + ln]
        off += ln
    return out


def _unshard(pieces, axis):
    if axis == 0:
        return pieces.reshape((N_DEV * pieces.shape[1], pieces.shape[2]))
    return jnp.concatenate([pieces[p] for p in range(N_DEV)], axis=1)


def _reshard(full, axis):
    if axis == 0:
        return full.reshape((N_DEV, full.shape[0] // N_DEV, full.shape[1]))
    c = full.shape[1] // N_DEV
    return jnp.stack([full[:, c * p:c * (p + 1)] for p in range(N_DEV)])


def _all_gather(shards):
    n = len(shards)

    def body(*refs):
        x_refs, out_refs = refs[:n], refs[n:2 * n]
        send_sems, recv_sems, local_sems = refs[2 * n:]
        x, y, c = lax.axis_index("x"), lax.axis_index("y"), lax.axis_index("c")
        me, sibling = (x, y, c), (x, y, 1 - c)
        chips = [(1 - x, y), (x, 1 - y), (1 - x, 1 - y)]

        def slot(a, px, py, pc):
            return out_refs[a].at[4 * px + 2 * py + pc]

        def copy(a, k, block, to, from_input=False):
            return pltpu.make_async_remote_copy(
                src_ref=x_refs[a] if from_input else slot(a, *block), dst_ref=slot(a, *block),
                send_sem=send_sems.at[a, k], recv_sem=recv_sems.at[a, k], device_id=to, device_id_type=pl.DeviceIdType.MESH)

        mine = [pltpu.make_async_copy(x_refs[a], slot(a, *me), local_sems.at[a]) for a in range(n)]
        first = [copy(a, 0, me, sibling, True) for a in range(n)]
        first += [copy(a, 1 + j, me, (*chip, c), True) for j, chip in enumerate(chips) for a in range(n)]
        for cp in mine + first:
            cp.start()
        passed = []
        for j, chip in enumerate(chips):
            for a in range(n):
                copy(a, 1 + j, (*chip, c), me).wait_recv()
                passed.append(copy(a, 4 + j, (*chip, c), sibling))
                passed[-1].start()
        for a in range(n):
            copy(a, 0, sibling, me).wait_recv()
        for j, chip in enumerate(chips):
            for a in range(n):
                copy(a, 4 + j, (*chip, 1 - c), me).wait_recv()
        for cp in first + passed:
            cp.wait_send()
        for cp in mine:
            cp.wait()

    any_spec = pl.BlockSpec(memory_space=pl.ANY)
    return pl.pallas_call(
        body, name="all_gather_weights", out_shape=[jax.ShapeDtypeStruct((N_DEV,) + s.shape, s.dtype) for s in shards],
        in_specs=[any_spec] * n, out_specs=[any_spec] * n,
        scratch_shapes=[pltpu.SemaphoreType.DMA((n, 7)), pltpu.SemaphoreType.DMA((n, 7)), pltpu.SemaphoreType.DMA((n,))],
    )(*shards)


def _all_to_all(pieces):
    n = len(pieces)

    def body(*refs):
        in_refs, out_refs = refs[:n], refs[n:2 * n]
        send_sems, recv_sems, local_sems = refs[2 * n:]
        x, y, c = lax.axis_index("x"), lax.axis_index("y"), lax.axis_index("c")
        my_id = 4 * x + 2 * y + c
        flips = [(fx, fy, fc) for fx in (0, 1) for fy in (0, 1) for fc in (0, 1)][1:]

        def copy(a, kk, f):
            p = (x ^ f[0], y ^ f[1], c ^ f[2])
            return pltpu.make_async_remote_copy(
                src_ref=in_refs[a].at[4 * p[0] + 2 * p[1] + p[2]], dst_ref=out_refs[a].at[my_id],
                send_sem=send_sems.at[a, kk], recv_sem=recv_sems.at[a, kk], device_id=p, device_id_type=pl.DeviceIdType.MESH)

        mine = [pltpu.make_async_copy(in_refs[a].at[my_id], out_refs[a].at[my_id], local_sems.at[a]) for a in range(n)]
        copies = [copy(a, kk, f) for kk, f in enumerate(flips) for a in range(n)]
        for cp in mine + copies:
            cp.start()
        for cp in copies:
            cp.wait_recv()
        for cp in copies:
            cp.wait_send()
        for cp in mine:
            cp.wait()

    any_spec = pl.BlockSpec(memory_space=pl.ANY)
    return pl.pallas_call(
        body, name="all_to_all_grads", out_shape=[jax.ShapeDtypeStruct(p.shape, p.dtype) for p in pieces],
        in_specs=[any_spec] * n, out_specs=[any_spec] * n,
        scratch_shapes=[pltpu.SemaphoreType.DMA((n, 7)), pltpu.SemaphoreType.DMA((n, 7)), pltpu.SemaphoreType.DMA((n,))],
    )(*pieces)


def _flip_peers(x, y, c):
    flips = [(fx, fy, fc) for fx in (0, 1) for fy in (0, 1) for fc in (0, 1)][1:]
    return [(x ^ fx, y ^ fy, c ^ fc) for fx, fy, fc in flips]


def _split_copies(in_refs, land_refs, send_sems, recv_sems, gather):
    x, y, c = lax.axis_index("x"), lax.axis_index("y"), lax.axis_index("c")
    my_id = 4 * x + 2 * y + c
    copies = []
    for kk, p in enumerate(_flip_peers(x, y, c)):
        for a in range(len(in_refs)):
            src = in_refs[a] if gather else in_refs[a].at[4 * p[0] + 2 * p[1] + p[2]]
            copies.append(pltpu.make_async_remote_copy(
                src_ref=src, dst_ref=land_refs[a].at[my_id], send_sem=send_sems.at[a * 7 + kk], recv_sem=recv_sems.at[a * 7 + kk],
                device_id=p, device_id_type=pl.DeviceIdType.MESH))
    return copies


def _exchange_start(name, srcs, gather, after=None):
    n = len(srcs)
    first_out = 2 * n + (0 if after is None else 1)

    def body(*refs):
        for cp in _split_copies(refs[:n], refs[n:2 * n], refs[first_out], refs[first_out + 1], gather):
            cp.start()
        refs[-1][...] = jnp.zeros_like(refs[-1])

    hbm, sem = pl.BlockSpec(memory_space=pltpu.HBM), pl.BlockSpec(memory_space=pltpu.SEMAPHORE)
    land_shapes = [((N_DEV,) + s.shape if gather else s.shape, s.dtype) for s in srcs]
    lands = [pltpu.with_memory_space_constraint(lax.empty(shp, dt), pltpu.HBM) for shp, dt in land_shapes]
    srcs = [pltpu.with_memory_space_constraint(s, pltpu.HBM) for s in srcs]
    res = pl.pallas_call(
        body, name=name,
        out_shape=[pltpu.SemaphoreType.DMA((7 * n,)), pltpu.SemaphoreType.DMA((7 * n,))] + [pltpu.HBM(s.shape, s.dtype) for s in srcs]
        + [pltpu.HBM(shp, dt) for shp, dt in land_shapes] + [jax.ShapeDtypeStruct((8, LANES), F32)],
        in_specs=[hbm] * (2 * n) + ([] if after is None else [pl.BlockSpec(memory_space=pl.ANY)]),
        out_specs=[sem, sem] + [hbm] * (2 * n) + [pl.BlockSpec(memory_space=pltpu.VMEM)],
        input_output_aliases={i: 2 + i for i in range(2 * n)},
        compiler_params=pltpu.CompilerParams(has_side_effects=pltpu.SideEffectType.DATAFLOW_SIDE_EFFECTING),
    )(*srcs, *lands, *([] if after is None else [after]))
    return res[0], res[1], res[2:2 + n], res[2 + n:2 + 2 * n], res[-1]


def _exchange_wait(name, handles, after, gather):
    send_sems, recv_sems, srcs, lands, _ = handles
    n = len(srcs)

    def body(*refs):
        for cp in _split_copies(refs[:n], refs[n:2 * n], refs[2 * n], refs[2 * n + 1], gather):
            cp.wait_send()
            cp.wait_recv()

    hbm, sem = pl.BlockSpec(memory_space=pltpu.HBM), pl.BlockSpec(memory_space=pltpu.SEMAPHORE)
    res = pl.pallas_call(
        body, name=name, out_shape=[pltpu.HBM(t.shape, t.dtype) for t in list(srcs) + list(lands)],
        in_specs=[hbm] * (2 * n) + [sem, sem, pl.BlockSpec(memory_space=pl.ANY)], out_specs=[hbm] * (2 * n),
        input_output_aliases={i: i for i in range(2 * n)},
        compiler_params=pltpu.CompilerParams(has_side_effects=pltpu.SideEffectType.DATAFLOW_SIDE_EFFECTING),
    )(*srcs, *lands, send_sems, recv_sems, after)
    my_id = 4 * lax.axis_index("x") + 2 * lax.axis_index("y") + lax.axis_index("c")
    own = [s if gather else lax.dynamic_index_in_dim(s, my_id, 0, keepdims=False) for s in res[:n]]
    return [lax.dynamic_update_index_in_dim(land, o, my_id, 0) for land, o in zip(res[n:], own)]


def _adamw(name, parts, w, m, v):
    rows, cols = w.shape
    tr = _pick(rows, (128, 64, 32, 16, 8))
    pspec = pl.BlockSpec((N_DEV, tr, cols), lambda i: (0, i, 0))
    rspec = pl.BlockSpec((tr, cols), lambda i: (i, 0))

    def body(p_ref, w_ref, m_ref, v_ref, g_ref, d_ref, m2_ref, v2_ref):
        g, d, m2, v2 = _adamw_fn([p_ref[s] for s in range(N_DEV)], w_ref[...], m_ref[...], v_ref[...])
        g_ref[...], d_ref[...], m2_ref[...], v2_ref[...] = g, d, m2, v2

    return pl.pallas_call(
        body, name=name, grid=(rows // tr,), in_specs=[pspec, rspec, rspec, rspec], out_specs=[rspec] * 4,
        out_shape=[jax.ShapeDtypeStruct((rows, cols), F32)] * 4,
        compiler_params=pltpu.CompilerParams(dimension_semantics=("parallel",), vmem_limit_bytes=VMEM_LIMIT),
    )(parts, w, m, v)


def kernel(x, positions, g_mix, w_in, g_q_a, w_q_b, g_kv_a, w_kv_b, g_qn, g_kn, w_mla_out, ret_decay_fwd, ret_decay_bwd, w_ret_out, w_out, g_ffn, w_gate_up, w_down, loss_target, m_g_mix, m_w_in, m_g_q_a, m_w_q_b, m_g_kv_a, m_w_kv_b, m_g_qn, m_g_kn, m_w_mla_out, m_ret_decay_fwd, m_ret_decay_bwd, m_w_ret_out, m_w_out, m_g_ffn, m_w_gate_up, m_w_down, v_g_mix, v_w_in, v_g_q_a, v_w_q_b, v_g_kv_a, v_w_kv_b, v_g_qn, v_g_kn, v_w_mla_out, v_ret_decay_fwd, v_ret_decay_bwd, v_w_ret_out, v_w_out, v_g_ffn, v_w_gate_up, v_w_down):
    w = dict(g_mix=g_mix, w_in=w_in, g_q_a=g_q_a, w_q_b=w_q_b, g_kv_a=g_kv_a, w_kv_b=w_kv_b, g_qn=g_qn, g_kn=g_kn, w_mla_out=w_mla_out,
             ret_decay_fwd=ret_decay_fwd, ret_decay_bwd=ret_decay_bwd, w_ret_out=w_ret_out, w_out=w_out, g_ffn=g_ffn,
             w_gate_up=w_gate_up, w_down=w_down)
    m = dict(g_mix=m_g_mix, w_in=m_w_in, g_q_a=m_g_q_a, w_q_b=m_w_q_b, g_kv_a=m_g_kv_a, w_kv_b=m_w_kv_b, g_qn=m_g_qn, g_kn=m_g_kn,
             w_mla_out=m_w_mla_out, ret_decay_fwd=m_ret_decay_fwd, ret_decay_bwd=m_ret_decay_bwd, w_ret_out=m_w_ret_out, w_out=m_w_out,
             g_ffn=m_g_ffn, w_gate_up=m_w_gate_up, w_down=m_w_down)
    v = dict(g_mix=v_g_mix, w_in=v_w_in, g_q_a=v_g_q_a, w_q_b=v_w_q_b, g_kv_a=v_g_kv_a, w_kv_b=v_w_kv_b, g_qn=v_g_qn, g_kn=v_g_kn,
             w_mla_out=v_w_mla_out, ret_decay_fwd=v_ret_decay_fwd, ret_decay_bwd=v_ret_decay_bwd, w_ret_out=v_w_ret_out, w_out=v_w_out,
             g_ffn=v_g_ffn, w_gate_up=v_w_gate_up, w_down=v_w_down)
    gains = {n: w[n].reshape(1, ln) for n, ln in GAINS}

    axis_of = {n: axis for n, _, axis in MATS}
    later = [n for n, _, _ in MATS if n not in FIRST_WEIGHTS]
    gathered = _all_gather([w[n].astype(WIRE) for n in FIRST_WEIGHTS])
    W = {n: _unshard(g, axis_of[n]) for n, g in zip(FIRST_WEIGHTS, gathered)}
    later_handles = _exchange_start("gather_later_start", [w[n].astype(WIRE) for n in later], True, after=gathered[0])

    def late_weights(after):
        lands = _exchange_wait("gather_later_wait", later_handles, after, True)
        return {n: _unshard(g, axis_of[n]) for n, g in zip(later, lands)}

    grad_groups = []

    def grad_hook(g):
        names = tuple(g)
        handles = _exchange_start("grads_start_%d" % len(grad_groups), [_reshard(g[n], axis_of[n]).astype(GWIRE) for n in names], False)
        grad_groups.append((names, handles))
        return handles[4]

    S = x.shape[1]
    pos = positions.reshape(S, 1).astype(F32)
    loss_rows, grad_x, gG, gW = _local_step(x.reshape(S, D_MODEL), pos, loss_target.reshape(S, D_MODEL), gains, W, late_weights, grad_hook,
                                            start_after=later_handles[4])
    loss = lax.psum(jnp.sum(loss_rows), ("x", "y", "c"))

    last = [n for n, _, _ in MATS if n not in EARLY_GRADS + MID_GRADS]
    pieces = [_reshard(gW[n], axis_of[n]).astype(GWIRE) for n in last]
    pieces.append(jnp.broadcast_to(_pack_gains(gG)[None], (N_DEV, 1, GAIN_PAD)))
    late_parts = _all_to_all(pieces)
    parts = dict(zip(last, late_parts))
    for i, (names, handles) in enumerate(grad_groups):
        parts.update(zip(names, _exchange_wait("grads_wait_%d" % i, handles, late_parts[-1], False)))
    out = [dict() for _ in range(4)]
    for n, _, _ in MATS:
        for o, r in zip(out, _adamw("adamw_" + n, parts[n], w[n], m[n], v[n])):
            o[n] = r
    for o, r in zip(out, _adamw("adamw_gains", late_parts[-1], _pack_gains(w), _pack_gains(m), _pack_gains(v))):
        o.update(_unpack_gains(r))
    return (loss, grad_x.reshape(x.shape), *[o[n] for o in out for n in ORDER])
```

```python
import functools

import numpy as np
import jax
import jax.numpy as jnp
from jax import lax
from jax.experimental import pallas as pl
from jax.experimental.pallas import tpu as pltpu

F32 = jnp.float32
MXU = jnp.bfloat16
WIRE = jnp.bfloat16
GWIRE = jnp.bfloat16

N_DEV = 8
D_MODEL = 1024
HEADS = 8
LANES = 128
Q_RANK, KV_RANK = 256, 128
NOPE, ROPE_M, V_M = 64, 32, 64
QK_M = NOPE + ROPE_M
RQK, RV = 64, 128
CHUNK = 128
FFN = 2816
IN_WIDTH = 5536
THETA = 10000.0
EPS = 1e-6
LR, B1, B2, AEPS, WD, STEP = 0.001, 0.9, 0.999, 1e-08, 0.01, 10
VMEM_LIMIT = 56 * 1024 * 1024

NN = ((1,), (0,))
NT = ((1,), (1,))
TN = ((0,), (0,))

P_GATES, P_VR, P_GR, P_QR, P_KR, P_CQ, P_CKV, P_KROPE, P_WIDTH = 0, 2048, 3072, 4096, 4608, 5120, 5376, 5504, 5632
O_CQ, O_CKV, O_KROPE, O_QR, O_KR, O_VR, O_GR, O_GATES = 0, 256, 384, 416, 928, 1440, 2464, 3488


def _dot(a, b, dims):
    return lax.dot_general(a, b, (dims, ((), ())), preferred_element_type=F32)


def _pick(dim, cands):
    for c in cands:
        if dim % c == 0:
            return c
    return dim


def _pairs(t):
    return t.reshape(t.shape[0], 4, 2, 2, 32).transpose(0, 1, 3, 2, 4).reshape(t.shape[0], 512)


def _win_pad(w):
    z = jnp.zeros((w.shape[0], 48), w.dtype)
    kr = w[:, O_KROPE:O_KROPE + 32]
    return jnp.concatenate([w[:, O_GATES:], w[:, O_VR:O_VR + 1024], w[:, O_GR:O_GR + 1024], _pairs(w[:, O_QR:O_QR + 512]),
                            _pairs(w[:, O_KR:O_KR + 512]), w[:, :O_CKV], w[:, O_CKV:O_KROPE], kr[:, :16], z, kr[:, 16:], z], axis=1)


def _win_unpad(g):
    return jnp.concatenate([g[:, P_CQ:P_CQ + 256], g[:, P_CKV:P_CKV + 128], g[:, P_KROPE:P_KROPE + 16], g[:, P_KROPE + 64:P_KROPE + 80],
                            _pairs(g[:, P_QR:P_QR + 512]), _pairs(g[:, P_KR:P_KR + 512]), g[:, P_VR:P_VR + 1024],
                            g[:, P_GR:P_GR + 1024], g[:, P_GATES:P_GATES + 2048]], axis=1)


def _qk_pad(t):
    z = jnp.zeros(t.shape[:-1] + (32,), t.dtype)
    return jnp.concatenate([t[..., 64:80], t[..., 0:48], t[..., 80:96], t[..., 48:64], z], axis=-1)


def _qk_unpad(p):
    return jnp.concatenate([p[..., 16:64], p[..., 80:96], p[..., 0:16], p[..., 64:80]], axis=-1)


def _wq_pad(w):
    return _qk_pad(w.reshape(Q_RANK, HEADS, QK_M)).reshape(Q_RANK, HEADS * LANES)


def _wq_unpad(g):
    return _qk_unpad(g.reshape(Q_RANK, HEADS, LANES)).reshape(Q_RANK, HEADS * QK_M)


def _wkv_pad(w):
    t = w.reshape(KV_RANK, HEADS, NOPE + V_M)
    z = lambda n: jnp.zeros((KV_RANK, HEADS, n), w.dtype)
    wk = jnp.concatenate([z(16), t[..., 0:48], z(16), t[..., 48:64], z(32)], axis=-1)
    wv = jnp.concatenate([t[..., 64:128], z(64)], axis=-1)
    return wk.reshape(KV_RANK, HEADS * LANES), wv.reshape(KV_RANK, HEADS * LANES)


def _wkv_unpad(dwk, dwv):
    k, v = dwk.reshape(KV_RANK, HEADS, LANES), dwv.reshape(KV_RANK, HEADS, LANES)
    return jnp.concatenate([k[..., 16:64], k[..., 80:96], v[..., 0:64]], axis=-1).reshape(KV_RANK, HEADS * (NOPE + V_M))


def _wmla_pad(w):
    t = w.reshape(HEADS, V_M, D_MODEL)
    return jnp.concatenate([t, jnp.zeros_like(t)], axis=1).reshape(HEADS * LANES, D_MODEL)


def _wmla_unpad(g):
    return g.reshape(HEADS, LANES, D_MODEL)[:, :V_M].reshape(HEADS * V_M, D_MODEL)


def _rowwise(name, fn, rows, ts, ins, outs, accs=(), ncol=1):
    n_in, n_out, n_acc = len(ins), len(outs), len(accs)

    def colmap(col):
        if callable(col):
            return lambda i, j: (i, col(j))
        return lambda i, j: (i, col)

    arrays, in_specs = [], []
    for arr, spec in ins:
        arrays.append(arr)
        if spec is None:
            in_specs.append(pl.BlockSpec(arr.shape, functools.partial(lambda i, j, nd: (0,) * nd, nd=arr.ndim)))
        else:
            in_specs.append(pl.BlockSpec((ts, spec[0]), colmap(spec[1])))
    out_shape, out_specs = [], []
    for total, dtype, width, col in outs:
        out_shape.append(jax.ShapeDtypeStruct((rows, total), dtype))
        out_specs.append(pl.BlockSpec((ts, width), colmap(col)))
    for shp in accs:
        out_shape.append(jax.ShapeDtypeStruct(shp, F32))
        out_specs.append(pl.BlockSpec(shp, functools.partial(lambda i, j, nd: (0,) * nd, nd=len(shp))))

    def body(*refs):
        vals = [r[...] for r in refs[:n_in]]
        res = fn(*vals)
        if not isinstance(res, (tuple, list)):
            res = (res,)
        for r, v in zip(refs[n_in:n_in + n_out], res[:n_out]):
            r[...] = v.astype(r.dtype)
        if n_acc:
            first = jnp.logical_and(pl.program_id(0) == 0, pl.program_id(1) == 0)
            for r, v in zip(refs[n_in + n_out:], res[n_out:]):
                @pl.when(first)
                def _(r=r):
                    r[...] = jnp.zeros_like(r)
                r[...] += v.astype(F32)

    res = pl.pallas_call(
        body, name=name, grid=(rows // ts, ncol), in_specs=in_specs, out_specs=out_specs, out_shape=out_shape,
        compiler_params=pltpu.CompilerParams(dimension_semantics=("arbitrary", "arbitrary"), vmem_limit_bytes=VMEM_LIMIT),
    )(*arrays)
    return res


MM_OPERAND_BYTES = 24 * 1024 * 1024


def _mm(name, a, b, mode, add=None, after=None):
    a_halves, b_halves = a.ndim == 3, b.ndim == 3
    assert not a_halves or mode == "nt"
    assert not b_halves or mode == "tn"
    if mode == "nn":
        (M, K), N = a.shape, b.shape[1]
    elif mode == "nt":
        M, K, N = a.shape[-2], a.shape[-1] * (2 if a_halves else 1), b.shape[0]
    else:
        (K, M), N = a.shape, b.shape[-1] * (2 if b_halves else 1)
    tm = _pick(M, (512, 1408, 256, 128)) if mode == "tn" else _pick(M, (1024, 512, 256, 128))
    tn = _pick(N // 2 if b_halves else N, (1408, 1024, 512, 256, 128))
    fits = lambda t: 2 * (tm + tn) * t * a.dtype.itemsize <= MM_OPERAND_BYTES
    kdiv = K // 2 if a_halves else K
    tk = next(t for t in (K, 4096, 2816, 2048, 1408, 1024, 512, 256, 128) if kdiv % t == 0 and (fits(t) or t == 128))
    nk = K // tk
    dims = {"nn": NN, "nt": NT, "tn": TN}[mode]
    if a_halves:
        per = kdiv // tk
        a_spec = pl.BlockSpec((None, tm, tk), lambda i, j, k: (k // per, i, k % per))
    else:
        a_spec = pl.BlockSpec((tk, tm), lambda i, j, k: (k, i)) if mode == "tn" else pl.BlockSpec((tm, tk), lambda i, j, k: (i, k))
    if b_halves:
        perj = (N // 2) // tn
        b_spec = pl.BlockSpec((None, tk, tn), lambda i, j, k: (j // perj, k, j % perj))
    else:
        b_spec = pl.BlockSpec((tn, tk), lambda i, j, k: (j, k)) if mode == "nt" else pl.BlockSpec((tk, tn), lambda i, j, k: (k, j))
    o_spec = pl.BlockSpec((tm, tn), lambda i, j, k: (i, j))
    has_add = add is not None

    def body(*refs):
        a_ref, b_ref, o_ref = refs[0], refs[1], refs[-1]
        d = _dot(a_ref[...], b_ref[...], dims)
        first = (d + refs[2][...]) if has_add else d
        if nk == 1:
            o_ref[...] = first
        else:
            k = pl.program_id(2)

            @pl.when(k == 0)
            def _():
                o_ref[...] = first

            @pl.when(k > 0)
            def _():
                o_ref[...] += d

    args = [a, b] + ([add] if has_add else []) + ([] if after is None else [after])
    specs = [a_spec, b_spec] + ([o_spec] if has_add else []) + ([] if after is None else [pl.BlockSpec(memory_space=pl.ANY)])
    return pl.pallas_call(
        body, name=name, grid=(M // tm, N // tn, nk), in_specs=specs, out_specs=o_spec,
        out_shape=jax.ShapeDtypeStruct((M, N), F32),
        compiler_params=pltpu.CompilerParams(dimension_semantics=("parallel", "parallel", "arbitrary"), vmem_limit_bytes=VMEM_LIMIT),
    )(*args)


def _mm_rows(name, a, b, fn, row_ins, whole_ins, outs, accs=(), mode="nn"):
    (M, K), N = a.shape, b.shape[1 if mode == "nn" else 0]
    tm = _pick(M, (512, 256, 128))
    n_in, n_out = 2 + len(row_ins) + len(whole_ins), len(outs)
    windows = [t if isinstance(t, tuple) else (t, (t.shape[1], 0)) for t in row_ins]
    row_ins = [t for t, _ in windows]
    row_specs = [pl.BlockSpec((tm, w), functools.partial(lambda i, col: (i, col), col=col)) for _, (w, col) in windows]

    def body(*refs):
        d = _dot(refs[0][...], refs[1][...], NN if mode == "nn" else NT)
        res = fn(d, *[r[...] for r in refs[2:n_in]])
        for r, v in zip(refs[n_in:n_in + n_out], res[:n_out]):
            r[...] = v.astype(r.dtype)
        for r, v in zip(refs[n_in + n_out:], res[n_out:]):
            @pl.when(pl.program_id(0) == 0)
            def _(r=r):
                r[...] = jnp.zeros_like(r)
            r[...] += v

    row = pl.BlockSpec((tm, N), lambda i: (i, 0))
    whole = lambda t: pl.BlockSpec(t.shape, functools.partial(lambda i, nd: (0,) * nd, nd=t.ndim))
    return pl.pallas_call(
        body, name=name, grid=(M // tm,),
        in_specs=[pl.BlockSpec((tm, K), lambda i: (i, 0)), whole(b)] + row_specs + [whole(t) for t in whole_ins],
        out_specs=[row] * n_out + [pl.BlockSpec(s, functools.partial(lambda i, nd: (0,) * nd, nd=len(s))) for s in accs],
        out_shape=[jax.ShapeDtypeStruct((M, N), dt) for dt in outs] + [jax.ShapeDtypeStruct(s, F32) for s in accs],
        compiler_params=pltpu.CompilerParams(dimension_semantics=("arbitrary",), vmem_limit_bytes=VMEM_LIMIT),
    )(a, b, *row_ins, *whole_ins)


def _ffn_tiles(S):
    return _pick(S, (512, 256, 128)), _pick(FFN, (1408, 704, 256, 128))


def _gate_up_swiglu(h2, wgu):
    S, K = h2.shape
    tm, tn = _ffn_tiles(S)
    nj = FFN // tn

    def body(a_ref, bg_ref, bu_ref, gu_ref, act_ref):
        a = a_ref[...]
        g, u = _dot(a, bg_ref[...], NN), _dot(a, bu_ref[...], NN)
        gu_ref[0], gu_ref[1] = g, u
        act_ref[...] = _swiglu_fn(g, u).astype(act_ref.dtype)

    return pl.pallas_call(
        body, name="gate_up_swiglu", grid=(S // tm, nj),
        in_specs=[pl.BlockSpec((tm, K), lambda i, j: (i, 0)), pl.BlockSpec((K, tn), lambda i, j: (0, j)),
                  pl.BlockSpec((K, tn), lambda i, j: (0, nj + j))],
        out_specs=[pl.BlockSpec((2, tm, tn), lambda i, j: (0, i, j)), pl.BlockSpec((tm, tn), lambda i, j: (i, j))],
        out_shape=[jax.ShapeDtypeStruct((2, S, FFN), F32), jax.ShapeDtypeStruct((S, FFN), MXU)],
        compiler_params=pltpu.CompilerParams(dimension_semantics=("parallel", "parallel"), vmem_limit_bytes=VMEM_LIMIT),
    )(h2, wgu, wgu)


def _d_act_swiglu(dx2, wdown, gu):
    S, K = dx2.shape
    tm, tn = _ffn_tiles(S)

    def body(a_ref, b_ref, gu_ref, o_ref):
        dact = _dot(a_ref[...], b_ref[...], NT)
        _, vjp = jax.vjp(_swiglu_fn, gu_ref[0], gu_ref[1])
        dg, du = vjp(dact)
        o_ref[0], o_ref[1] = dg.astype(o_ref.dtype), du.astype(o_ref.dtype)

    stacked = pl.BlockSpec((2, tm, tn), lambda i, j: (0, i, j))
    return pl.pallas_call(
        body, name="d_act_swiglu", grid=(S // tm, FFN // tn),
        in_specs=[pl.BlockSpec((tm, K), lambda i, j: (i, 0)), pl.BlockSpec((tn, K), lambda i, j: (j, 0)), stacked],
        out_specs=stacked, out_shape=jax.ShapeDtypeStruct((2, S, FFN), MXU),
        compiler_params=pltpu.CompilerParams(dimension_semantics=("parallel", "parallel"), vmem_limit_bytes=VMEM_LIMIT),
    )(dx2, wdown, gu)


@jax.custom_vjp
def _swap64(x):
    return pltpu.roll(x, 64, 1)


_swap64.defvjp(lambda x: (_swap64(x), None), lambda _, g: (_swap64(g),))


@jax.custom_vjp
def _mxdot(a, b):
    return _dot(a.astype(MXU), b.astype(MXU), NN)


def _mxdot_bwd(res, g):
    a, b = res
    gb = g.astype(MXU)
    return _dot(gb, b.astype(MXU), NT), _dot(a.astype(MXU), gb, TN)


_mxdot.defvjp(lambda a, b: (_mxdot(a, b), (a, b)), _mxdot_bwd)


def _rms(x):
    return x * lax.rsqrt(jnp.mean(x * x, axis=-1, keepdims=True) + EPS)


def _rmsg_fn(x, g):
    return _rms(x) * g


def _silu(x):
    return x * jax.nn.sigmoid(x)


def _tables_fn(pos, inv_m, sgn_m, inv_r, sgn_r):
    am, ar = pos * inv_m, pos * inv_r
    return jnp.cos(am), jnp.sin(am) * sgn_m, jnp.cos(ar), jnp.sin(ar) * sgn_r


def _head_blocks(t):
    return [t[:, LANES * h:LANES * (h + 1)] for h in range(t.shape[1] // LANES)]


def _mla_prep_fn(cq, ckv, kr, cosm, sinm, gqa, gkva, gqn, gkn, wq, wk, wv):
    cqn = _rms(cq) * gqa
    ckvn = _rms(ckv) * gkva
    q_raw = _mxdot(cqn, wq)
    k_raw = _mxdot(ckvn, wk)
    lane = lax.broadcasted_iota(jnp.int32, (1, HEADS * LANES), 1)
    v = _mxdot(ckvn, wv) + (lane % LANES == V_M).astype(F32)

    def norm_rope(blocks, g, extra):
        outs = []
        for b in blocks:
            if extra is not None:
                b = b + extra
            n = b * lax.rsqrt(jnp.sum(b * b, axis=-1, keepdims=True) * (1.0 / QK_M) + EPS) * g
            outs.append(n * cosm + _swap64(n) * sinm)
        return jnp.concatenate(outs, axis=1)

    q = norm_rope(_head_blocks(q_raw), gqn, None)
    k = norm_rope(_head_blocks(k_raw), gkn, kr)
    return q, k, v


def _ret_prep_fn(qr, kr, cosr, sinr):
    def rope(t, scale):
        return jnp.concatenate([(b * cosr + _swap64(b) * sinr) * scale for b in _head_blocks(t)], axis=1)
    return rope(qr, 1.0), rope(kr, RQK ** -0.5)


def _ret_post_fn(rf, rb, gr):
    ret = rf + rb
    outs = []
    for b, g in zip(_head_blocks(ret), _head_blocks(gr)):
        outs.append(_silu(g) * _rms(b))
    return jnp.concatenate(outs, axis=1)


def _merge_fn(ga, gb, ya, yb):
    return jax.nn.sigmoid(ga) * ya + jax.nn.sigmoid(gb) * yb


def _swiglu_fn(gate, up):
    return _silu(gate) * up


def _loss_fn(x2, tgt):
    d = x2 - tgt
    return d * (1.0 / D_MODEL), 0.5 * jnp.sum(d * d, axis=0, keepdims=True) * (1.0 / D_MODEL)


def _adamw_fn(parts, w, m, v):
    g = parts[0].astype(F32)
    for p in range(1, N_DEV):
        g = g + parts[p].astype(F32)
    m2 = B1 * m + (1.0 - B1) * g
    v2 = B2 * v + (1.0 - B2) * jnp.square(g)
    m_hat = m2 / (1.0 - B1 ** STEP)
    v_hat = v2 / (1.0 - B2 ** STEP)
    delta = -LR * (m_hat / (jnp.sqrt(v_hat) + AEPS) + WD * w)
    return g, delta, m2, v2


SCALE = QK_M ** -0.5
LOG2E = 1.4426950408889634
FLASH_ROWS = 32


def _flash_fwd(q, k, v):
    S = q.shape[0]
    tq = tk = _pick(S, (512, 256, 128))
    ncb = tk // LANES
    nkv = S // tk
    assert nkv % 2 == 0, "kv tiles are processed in pairs"
    mrows = 64
    c = SCALE * LOG2E

    def body(q_ref, k_ref, v_ref, o_ref, obf_ref, lse_ref, s_a, p_a, s_b, p_b, m_sc, a_sc, acc_sc):
        m_sc[...] = jnp.full_like(m_sc, -jnp.inf)
        acc_sc[...] = jnp.zeros_like(acc_sc)
        qb = q_ref[...]

        def scores(j, s_buf):
            s_buf[...] = _dot(qb, k_ref[pl.ds(pl.multiple_of(j * tk, tk), tk), :], NT)

        def stage(j, s_buf, p_buf, s_next):
            scores(jnp.minimum(j + 1, nkv - 1), s_next)
            for r in range(tq // mrows):
                rows = slice(r * mrows, (r + 1) * mrows)
                cols = [s_buf[rows, LANES * cb:LANES * (cb + 1)] for cb in range(ncb)]
                m_prev = m_sc[rows, :]
                row_max = jnp.max(functools.reduce(jnp.maximum, cols), axis=-1, keepdims=True)
                m_new = jnp.maximum(m_prev, jnp.broadcast_to(row_max, (mrows, LANES)))
                a_sc[rows, :] = jnp.exp2((m_prev - m_new) * c)
                m_sc[rows, :] = m_new
                for cb in range(ncb):
                    p_buf[rows, LANES * cb:LANES * (cb + 1)] = jnp.exp2((cols[cb] - m_new) * c).astype(p_buf.dtype)
            acc_sc[...] = a_sc[...] * acc_sc[...] + _dot(p_buf[...], v_ref[pl.ds(pl.multiple_of(j * tk, tk), tk), :], NN)

        scores(0, s_a)

        def pair_step(t, carry):
            stage(2 * t, s_a, p_a, s_b)
            stage(2 * t + 1, s_b, p_b, s_a)
            return carry

        lax.fori_loop(0, nkv // 2, pair_step, 0, unroll=4)
        acc = acc_sc[...]
        lane = lax.broadcasted_iota(jnp.int32, (1, LANES), 1)
        l = jnp.sum(jnp.where(lane == V_M, acc, 0.0), axis=-1, keepdims=True)
        o = acc / l
        o_ref[...] = o
        obf_ref[...] = o.astype(obf_ref.dtype)
        lse_ref[...] = m_sc[...] * c + jnp.log2(jnp.broadcast_to(l, (tq, LANES)))

    qspec = pl.BlockSpec((tq, LANES), lambda h, i: (i, h))
    kspec = pl.BlockSpec((S, LANES), lambda h, i: (0, h))
    full = jax.ShapeDtypeStruct((S, HEADS * LANES), F32)
    return pl.pallas_call(
        body, name="flash_fwd", grid=(HEADS, S // tq), in_specs=[qspec, kspec, kspec], out_specs=[qspec, qspec, qspec],
        out_shape=[full, jax.ShapeDtypeStruct((S, HEADS * LANES), MXU), full],
        scratch_shapes=[pltpu.VMEM((tq, tk), F32), pltpu.VMEM((tq, tk), MXU)] * 2 + [pltpu.VMEM((tq, LANES), F32)] * 3,
        compiler_params=pltpu.CompilerParams(dimension_semantics=("parallel", "arbitrary"), vmem_limit_bytes=VMEM_LIMIT),
    )(q, k, v)


def _delta_fn(o, do):
    outs = [jnp.broadcast_to(jnp.sum(a * b, axis=-1, keepdims=True), a.shape) for a, b in zip(_head_blocks(o), _head_blocks(do))]
    return do, jnp.concatenate(outs, axis=1)


def _flash_bwd(q, k, v, do, lse, delta):
    S = q.shape[0]
    tq = tk = _pick(S, (512, 256, 128))
    ncb = tk // LANES
    c = SCALE * LOG2E

    nq = S // tq
    assert nq % 2 == 0, "q tiles are processed in pairs"

    def body(q_ref, k_ref, v_ref, do_ref, lse_ref, dl_ref, dq_ref, dk_ref, dv_ref, s_a, dp_a, p_a, ds_a, s_b, dp_b, p_b, ds_b, dk_sc, dv_sc):
        @pl.when(pl.program_id(1) == 0)
        def _():
            dq_ref[...] = jnp.zeros_like(dq_ref)

        dk_sc[...] = jnp.zeros_like(dk_sc)
        dv_sc[...] = jnp.zeros_like(dv_sc)
        kb, vb = k_ref[...], v_ref[...]

        def scores(i, s_buf, dp_buf):
            q_rows = pl.ds(pl.multiple_of(i * tq, tq), tq)
            s_buf[...] = _dot(q_ref[q_rows, :], kb, NT)
            dp_buf[...] = _dot(do_ref[q_rows, :], vb, NT)

        def stage(i, s_buf, dp_buf, p_buf, ds_buf, s_next, dp_next):
            scores(jnp.minimum(i + 1, nq - 1), s_next, dp_next)
            for r in range(tq // FLASH_ROWS):
                rows = slice(r * FLASH_ROWS, (r + 1) * FLASH_ROWS)
                grows = pl.ds(pl.multiple_of(i * tq + r * FLASH_ROWS, FLASH_ROWS), FLASH_ROWS)
                lse_b, dl_b = lse_ref[grows, :], dl_ref[grows, :]
                for cb in range(ncb):
                    sl = slice(LANES * cb, LANES * (cb + 1))
                    p = jnp.exp2(s_buf[rows, sl] * c - lse_b)
                    p_buf[rows, sl] = p.astype(p_buf.dtype)
                    ds_buf[rows, sl] = (p * (dp_buf[rows, sl] - dl_b) * SCALE).astype(ds_buf.dtype)
            q_rows = pl.ds(pl.multiple_of(i * tq, tq), tq)
            dv_sc[...] += _dot(p_buf[...], do_ref[q_rows, :], TN)
            dk_sc[...] += _dot(ds_buf[...], q_ref[q_rows, :], TN)
            dq_ref[q_rows, :] += _dot(ds_buf[...], kb, NN)

        scores(0, s_a, dp_a)

        def pair_step(t, carry):
            stage(2 * t, s_a, dp_a, p_a, ds_a, s_b, dp_b)
            stage(2 * t + 1, s_b, dp_b, p_b, ds_b, s_a, dp_a)
            return carry

        lax.fori_loop(0, nq // 2, pair_step, 0, unroll=2)
        dk_ref[...] = dk_sc[...]
        dv_ref[...] = dv_sc[...]

    hspec = pl.BlockSpec((S, LANES), lambda h, j: (0, h))
    kspec = pl.BlockSpec((tk, LANES), lambda h, j: (j, h))
    full = jax.ShapeDtypeStruct((S, HEADS * LANES), F32)
    tile_bufs = [pltpu.VMEM((tq, tk), F32), pltpu.VMEM((tq, tk), F32), pltpu.VMEM((tq, tk), MXU), pltpu.VMEM((tq, tk), MXU)]
    return pl.pallas_call(
        body, name="flash_bwd", grid=(HEADS, S // tk), in_specs=[hspec, kspec, kspec, hspec, hspec, hspec],
        out_specs=[hspec, kspec, kspec], out_shape=[full, full, full],
        scratch_shapes=tile_bufs + tile_bufs + [pltpu.VMEM((tk, LANES), F32), pltpu.VMEM((tk, LANES), F32)],
        compiler_params=pltpu.CompilerParams(dimension_semantics=("parallel", "arbitrary"), vmem_limit_bytes=VMEM_LIMIT),
    )(q, k, v, do, lse, delta)


def _ret_consts(lgh, head, rev):
    C = CHUNK
    lane = lax.broadcasted_iota(jnp.int32, (1, LANES), 1)
    hm = ((lane // 32) % 2 == head % 2).astype(F32)
    r = lax.broadcasted_iota(jnp.int32, (C, C), 0)
    c = lax.broadcasted_iota(jnp.int32, (C, C), 1)
    diff = ((c - r) if rev else (r - c)).astype(F32)
    mask = (diff > 0) if rev else (diff >= 0)
    dpos = jnp.maximum(diff, 0.0)
    din = jnp.where(mask, jnp.exp(lgh * dpos), 0.0)
    idx = lax.broadcasted_iota(jnp.int32, (C, 1), 0).astype(F32)
    eq = (C - idx) if rev else (idx + 1.0)
    ek = idx if rev else (C - 1.0 - idx)
    qd, kd = jnp.exp(lgh * eq), jnp.exp(lgh * ek)
    cd = jnp.exp(lgh * jnp.full((1, 1), float(C), F32))
    return hm, din, dpos, qd, kd, cd, eq, ek


RET_HEADS_PER_STEP = 4


def _ret_fwd(name, qt, kt, proj, lg, rev):
    S = qt.shape[0]
    C = CHUNK
    TB = _pick(S, (512, 256, 128))
    cb, nb = TB // C, S // TB
    hps = RET_HEADS_PER_STEP
    blk = (lambda g: nb - 1 - g) if rev else (lambda g: g)

    def body(lg_ref, q_ref, k_ref, v_ref, o_ref, st_ref, state_sc):
        hg, g = pl.program_id(0), pl.program_id(1)

        @pl.when(g == 0)
        def _():
            state_sc[...] = jnp.zeros_like(state_sc)

        consts = [_ret_consts(lg_ref[hg * hps + u], u, rev) for u in range(hps)]
        order = list(reversed(range(cb))) if rev else list(range(cb))
        units = [(cc, u) for cc in order for u in range(hps)]

        def operands(cc, u):
            rows = pl.ds(cc * C, C)
            pair = slice(LANES * (u // 2), LANES * (u // 2 + 1))
            hm = consts[u][0]
            return q_ref[rows, pair] * hm, k_ref[rows, pair] * hm, v_ref[rows, LANES * u:LANES * (u + 1)].astype(MXU)

        a, inc = {}, {}
        for cc, u in units:
            q, k, v = operands(cc, u)
            a[cc, u] = _dot(q.astype(MXU), k.astype(MXU), NT) * consts[u][1]
            inc[cc, u] = _dot((k * consts[u][4]).astype(MXU), v, TN)
        for u in range(hps):
            st = state_sc[u]
            for cc in order:
                st_ref[u, cc] = st
                st = st * consts[u][5] + inc[cc, u]
            state_sc[u] = st
        for cc, u in units:
            q, _, v = operands(cc, u)
            cross = _dot((q * consts[u][3]).astype(MXU), st_ref[u, cc].astype(MXU), NN)
            o_ref[pl.ds(cc * C, C), LANES * u:LANES * (u + 1)] = _dot(a[cc, u].astype(MXU), v, NN) + cross

    qk_spec = pl.BlockSpec((TB, LANES * hps // 2), lambda h, g: (blk(g), h))
    return pl.pallas_call(
        body, name=name, grid=(HEADS // hps, nb),
        in_specs=[pl.BlockSpec(memory_space=pltpu.SMEM), qk_spec, qk_spec,
                  pl.BlockSpec((TB, LANES * hps), lambda h, g: (blk(g), P_VR // (LANES * hps) + h))],
        out_specs=[pl.BlockSpec((TB, LANES * hps), lambda h, g: (blk(g), h)),
                   pl.BlockSpec((hps, cb, LANES, LANES), lambda h, g: (h, blk(g), 0, 0))],
        out_shape=[jax.ShapeDtypeStruct((S, HEADS * LANES), F32), jax.ShapeDtypeStruct((HEADS, S // C, LANES, LANES), F32)],
        scratch_shapes=[pltpu.VMEM((hps, LANES, LANES), F32)],
        compiler_params=pltpu.CompilerParams(dimension_semantics=("parallel", "arbitrary"), vmem_limit_bytes=VMEM_LIMIT),
    )(lg, qt, kt, proj)


def _ret_bwd(name, qt, kt, proj, dret, states, lg, rev):
    S = qt.shape[0]
    C = CHUNK
    TB = _pick(S, (512, 256, 128))
    cb, nb = TB // C, S // TB
    hps = RET_HEADS_PER_STEP
    blk = (lambda g: g) if rev else (lambda g: nb - 1 - g)

    def body(lg_ref, q_ref, k_ref, v_ref, do_ref, st_ref, dq_ref, dk_ref, dv_ref, dlg_ref, ds_sc, acc_cc, acc_q, acc_k, acc_s):
        hg, g = pl.program_id(0), pl.program_id(1)

        @pl.when(g == 0)
        def _():
            ds_sc[...] = jnp.zeros_like(ds_sc)
            acc_cc[...] = jnp.zeros_like(acc_cc)
            acc_q[...] = jnp.zeros_like(acc_q)
            acc_k[...] = jnp.zeros_like(acc_k)
            acc_s[...] = jnp.zeros_like(acc_s)

        lgs = [lg_ref[hg * hps + u] for u in range(hps)]
        consts = [_ret_consts(lgs[u], u, rev) for u in range(hps)]
        order = list(range(cb)) if rev else list(reversed(range(cb)))
        units = [(cc, u) for cc in order for u in range(hps)]

        def operands(cc, u):
            rows = pl.ds(cc * C, C)
            pair = slice(LANES * (u // 2), LANES * (u // 2 + 1))
            head = slice(LANES * u, LANES * (u + 1))
            hm = consts[u][0]
            return q_ref[rows, pair] * hm, k_ref[rows, pair] * hm, v_ref[rows, head].astype(MXU), do_ref[rows, head].astype(MXU)

        a, dp, dqs, inc = {}, {}, {}, {}
        for cc, u in units:
            q, k, vb, dob = operands(cc, u)
            a[cc, u] = _dot(q.astype(MXU), k.astype(MXU), NT)
            dp[cc, u] = _dot(dob, vb, NT)
            dqs[cc, u] = _dot(dob, st_ref[u, cc].astype(MXU), NT)
            inc[cc, u] = _dot((q * consts[u][3]).astype(MXU), dob, TN)
        dsn = {}
        for u in range(hps):
            ds = ds_sc[u]
            for cc in order:
                dsn[cc, u] = ds
                ds = ds * consts[u][5] + inc[cc, u]
            ds_sc[u] = ds
        even = {}
        for cc, u in units:
            hm, din, dpos, qd, kd, cd, eq, ek = consts[u]
            rows, head = pl.ds(cc * C, C), slice(LANES * u, LANES * (u + 1))
            q, k, vb, dob = operands(cc, u)
            qb, kb = q.astype(MXU), k.astype(MXU)
            dsnb = dsn[cc, u].astype(MXU)
            da = (dp[cc, u] * din).astype(MXU)
            vds = _dot(vb, dsnb, NT)
            dq_u = (_dot(da, kb, NN) + dqs[cc, u] * qd) * hm
            dk_u = (_dot(da, qb, TN) + vds * kd) * hm
            if u % 2 == 0:
                even[cc] = (dq_u, dk_u)
            else:
                pair = slice(LANES * (u // 2), LANES * (u // 2 + 1))
                dq_ref[rows, pair] = even[cc][0] + dq_u
                dk_ref[rows, pair] = even[cc][1] + dk_u
            dv_ref[rows, head] = _dot((a[cc, u] * din).astype(MXU), dob, TN) + _dot((k * kd).astype(MXU), dsnb, NN)
            acc_cc[u] += dp[cc, u] * a[cc, u] * din * dpos
            acc_q[u] += dqs[cc, u] * q * (qd * eq)
            acc_k[u] += vds * k * (kd * ek)
            acc_s[u] += dsn[cc, u] * st_ref[u, cc] * (cd * float(C))

        @pl.when(g == nb - 1)
        def _():
            for u in range(hps):
                tot = (jnp.sum(acc_cc[u], keepdims=True) + jnp.sum(acc_q[u], keepdims=True)
                       + jnp.sum(acc_k[u], keepdims=True) + jnp.sum(acc_s[u], keepdims=True))
                dlg_ref[u] = jnp.broadcast_to(tot * lgs[u], (8, LANES))

    full = jax.ShapeDtypeStruct((S, HEADS * LANES), F32)
    hspec = pl.BlockSpec((TB, LANES * hps), lambda h, g: (blk(g), h))
    qk_spec = pl.BlockSpec((TB, LANES * hps // 2), lambda h, g: (blk(g), h))
    return pl.pallas_call(
        body, name=name, grid=(HEADS // hps, nb),
        in_specs=[pl.BlockSpec(memory_space=pltpu.SMEM), qk_spec, qk_spec,
                  pl.BlockSpec((TB, LANES * hps), lambda h, g: (blk(g), P_VR // (LANES * hps) + h)),
                  hspec,
                  pl.BlockSpec((hps, cb, LANES, LANES), lambda h, g: (h, blk(g), 0, 0))],
        out_specs=[qk_spec, qk_spec, hspec, pl.BlockSpec((hps, 8, LANES), lambda h, g: (h, 0, 0))],
        out_shape=[jax.ShapeDtypeStruct(qt.shape, F32), jax.ShapeDtypeStruct(kt.shape, F32), full,
                   jax.ShapeDtypeStruct((HEADS, 8, LANES), F32)],
        scratch_shapes=[pltpu.VMEM((hps, LANES, LANES), F32), pltpu.VMEM((hps, C, C), F32), pltpu.VMEM((hps, C, LANES), F32),
                        pltpu.VMEM((hps, C, LANES), F32), pltpu.VMEM((hps, LANES, LANES), F32)],
        compiler_params=pltpu.CompilerParams(dimension_semantics=("parallel", "arbitrary"), vmem_limit_bytes=VMEM_LIMIT),
    )(lg, qt, kt, proj, dret, states)


def _rope_consts():
    inv16 = THETA ** (-jnp.arange(16, dtype=F32) / 16)
    inv32 = THETA ** (-jnp.arange(32, dtype=F32) / 32)
    lane = np.arange(LANES)
    z48 = jnp.zeros((48,), F32)
    inv_m = jnp.concatenate([inv16, z48, inv16, z48])[None, :]
    sgn_m = jnp.asarray(np.where(lane < 16, -1.0, np.where((lane >= 64) & (lane < 80), 1.0, 0.0)), F32)[None, :]
    inv_r = jnp.concatenate([inv32] * 4)[None, :]
    sgn_r = jnp.asarray(np.where(lane < 64, -1.0, 1.0), F32)[None, :]
    return inv_m, sgn_m, inv_r, sgn_r


FIRST_WEIGHTS = ("w_in", "w_q_b", "w_kv_b")
EARLY_GRADS = ("w_down", "w_gate_up", "w_out", "w_ret_out")
MID_GRADS = ("w_mla_out", "w_in")


def _local_step(x, pos, tgt, gains, W, late_weights=None, grad_hook=None, start_after=None):
    S = x.shape[0]
    ts = _pick(S, (256, 128))
    R = lambda a, w=None, c=0: (a, ((a.shape[1] if w is None else w), c))
    W_ = lambda a: (a, None)

    win = _win_pad(W["w_in"])
    wq = _wq_pad(W["w_q_b"])
    wk, wv = _wkv_pad(W["w_kv_b"])
    gqn, gkn = _qk_pad(gains["g_qn"]), _qk_pad(gains["g_kn"])
    g_mix, g_q_a, g_kv_a, g_ffn = gains["g_mix"], gains["g_q_a"], gains["g_kv_a"], gains["g_ffn"]
    lg_f = -jnp.exp(gains["ret_decay_fwd"][0])
    lg_b = -jnp.exp(gains["ret_decay_bwd"][0])

    consts = list(_rope_consts())
    cosm, sinm, cosr, sinr = _rowwise("rope_tables", _tables_fn, S, ts, [R(pos)] + [W_(c) for c in consts],
                                      [(LANES, F32, LANES, 0)] * 4)

    (h,) = _rowwise("rms_mix", _rmsg_fn, S, ts, [R(x), W_(g_mix)], [(D_MODEL, MXU, D_MODEL, 0)])
    proj = _mm("in_proj", h, win, "nn", after=start_after)
    seg = lambda off, w: (proj, (w, off // w))
    mla_ins = [seg(P_CQ, 256), seg(P_CKV, 128), seg(P_KROPE, 128), R(cosm), R(sinm),
               W_(g_q_a), W_(g_kv_a), W_(gqn), W_(gkn), W_(wq), W_(wk), W_(wv)]
    q, k, v = _rowwise("mla_prep", _mla_prep_fn, S, ts, mla_ins, [(HEADS * LANES, MXU, HEADS * LANES, 0)] * 3)
    o, o_bf, lse = _flash_fwd(q, k, v)
    if late_weights is not None:
        W = {**W, **late_weights(lse)}
    wmla = _wmla_pad(W["w_mla_out"])
    wret, wout, wgu, wdown = W["w_ret_out"], W["w_out"], W["w_gate_up"], W["w_down"]
    y_a = _mm("mla_out", o_bf, wmla, "nn")

    ret_ins = [seg(P_QR, 512), seg(P_KR, 512), R(cosr), R(sinr)]
    qt, kt = _rowwise("ret_prep", _ret_prep_fn, S, ts, ret_ins, [(512, F32, 512, 0)] * 2)
    ret_f, st_f = _ret_fwd("ret_fwd_f", qt, kt, proj, lg_f, False)
    ret_b, st_b = _ret_fwd("ret_fwd_b", qt, kt, proj, lg_b, True)
    post_ins = [R(ret_f), R(ret_b), seg(P_GR, 1024)]
    (o_b,) = _rowwise("ret_post", _ret_post_fn, S, ts, post_ins, [(1024, MXU, 1024, 0)])
    y_b, merged = _mm_rows("ret_out_merge", o_b, wret, lambda yb, ga, gb, ya: (yb, _merge_fn(ga, gb, ya, yb)),
                           [seg(P_GATES, 1024), (proj, (1024, 1)), R(y_a)], [], [F32, MXU])
    merge_ins = [seg(P_GATES, 1024), (proj, (1024, 1)), R(y_a), R(y_b)]
    def residual_rms(d, xx, g):
        r = d + xx
        return r, _rmsg_fn(r, g)

    x1, h2 = _mm_rows("out_proj_rms_ffn", merged, wout, residual_rms, [x], [g_ffn], [F32, MXU])
    gu, act = _gate_up_swiglu(h2, wgu)

    def residual_loss(d, xx, t):
        dx, rows = _loss_fn(d + xx, t)
        return dx, dx, rows

    dx2, dx2_bf, loss_rows = _mm_rows("down_proj_loss", act, wdown, residual_loss, [x1, tgt], [], [F32, MXU], accs=[(1, D_MODEL)])

    gW = {}
    gW["w_down"] = _mm("d_w_down", act, dx2_bf, "tn")
    dgu = _d_act_swiglu(dx2_bf, wdown, gu)
    gW["w_gate_up"] = _mm("d_w_gate_up", h2, dgu, "tn")
    dh2 = _mm("d_h2", dgu, wgu, "nt")

    def rms_bwd(xx, g, dh, dres):
        _, vjp = jax.vjp(_rmsg_fn, xx, g)
        dx, dg = vjp(dh)
        dx = dx + dres
        return dx, dx, dg

    dx1, dx1_bf, dg_ffn = _rowwise("rms_ffn_bwd", rms_bwd, S, ts, [R(x1), W_(g_ffn), R(dh2), R(dx2)],
                                   [(D_MODEL, F32, D_MODEL, 0), (D_MODEL, MXU, D_MODEL, 0)], accs=[(1, D_MODEL)])
    gW["w_out"] = _mm("d_w_out", merged, dx1_bf, "tn")
    def merge_bwd(dm, ga, gb, ya, yb):
        _, vjp = jax.vjp(_merge_fn, ga, gb, ya, yb)
        return vjp(dm)

    dga, dgb, dy_a, dy_b = _mm_rows("d_merged_merge_bwd", dx1_bf, wout, merge_bwd, merge_ins, [], [MXU] * 4, mode="nt")
    gW["w_ret_out"] = _mm("d_w_ret_out", o_b, dy_b, "tn")
    after_early = [] if grad_hook is None else [grad_hook({n: gW[n] for n in EARLY_GRADS})]

    def post_bwd(dob, rf, rb, gr, *_):
        _, vjp = jax.vjp(_ret_post_fn, rf, rb, gr)
        drf, _, dgr = vjp(dob)
        return drf, dgr

    dret, dg_r = _mm_rows("d_o_b_ret_post_bwd", dy_b, wret, post_bwd, post_ins, after_early, [F32, MXU], mode="nt")
    dq_f, dk_f, dv_f, dlg_f = _ret_bwd("ret_bwd_f", qt, kt, proj, dret, st_f, lg_f, False)
    dq_b, dk_b, dv_b, dlg_b = _ret_bwd("ret_bwd_b", qt, kt, proj, dret, st_b, lg_b, True)

    def ret_prep_bwd(qr, kr, cosr_, sinr_, dqf, dqb, dkf, dkb, dvf, dvb):
        _, vjp = jax.vjp(lambda a, b: _ret_prep_fn(a, b, cosr_, sinr_), qr, kr)
        dqr, dkr = vjp((dqf + dqb, dkf + dkb))
        return dqr, dkr, dvf + dvb

    dq_r, dk_r, dv_r = _rowwise("ret_prep_bwd", ret_prep_bwd, S, ts, ret_ins + [R(t) for t in (dq_f, dq_b, dk_f, dk_b, dv_f, dv_b)],
                                [(512, MXU, 512, 0), (512, MXU, 512, 0), (1024, MXU, 1024, 0)])

    gW_mla_p = _mm("d_w_mla_out", o_bf, dy_a, "tn")
    do_bf, delta = _mm_rows("d_o_attn_delta", dy_a, wmla, lambda d, oo, *_: _delta_fn(oo, d), [o], after_early, [MXU, F32], mode="nt")
    dq, dk, dv = _flash_bwd(q, k, v, do_bf, lse, delta)

    def mla_prep_bwd(cq, ckv, kr, cosm_, sinm_, gqa, gkva, gqn_, gkn_, wq_, wk_, wv_, dq_, dk_, dv_):
        f = lambda cq, ckv, kr, gqa, gkva, gqn_, gkn_, wq_, wk_, wv_: _mla_prep_fn(cq, ckv, kr, cosm_, sinm_, gqa, gkva, gqn_, gkn_, wq_, wk_, wv_)
        _, vjp = jax.vjp(f, cq, ckv, kr, gqa, gkva, gqn_, gkn_, wq_.astype(F32), wk_.astype(F32), wv_.astype(F32))
        return vjp((dq_, dk_, dv_))

    mb = _rowwise("mla_prep_bwd", mla_prep_bwd, S, ts, mla_ins + [R(dq), R(dk), R(dv)],
                  [(256, MXU, 256, 0), (128, MXU, 128, 0), (128, MXU, 128, 0)],
                  accs=[(1, 256), (1, 128), (1, LANES), (1, LANES), (256, HEADS * LANES), (128, HEADS * LANES), (128, HEADS * LANES)])
    dc_q, dc_kv, dk_rope, dg_q_a, dg_kv_a, dgqn_p, dgkn_p, dwq_p, dwk_p, dwv_p = mb

    dproj = jnp.concatenate([dga, dgb, dv_r, dg_r, dq_r, dk_r, dc_q, dc_kv, dk_rope], axis=1)
    gW["w_in"] = _win_unpad(_mm("d_w_in", h, dproj, "tn"))
    gW["w_mla_out"] = _wmla_unpad(gW_mla_p)
    after_mid = None if grad_hook is None else grad_hook({n: gW[n] for n in MID_GRADS})
    dh = _mm("d_h", dproj, win, "nt", after=after_mid)
    grad_x, _, dg_mix = _rowwise("rms_mix_bwd", lambda a, b, c, d, *_: rms_bwd(a, b, c, d), S, ts,
                                 [R(x), W_(g_mix), R(dh), R(dx1)] + ([] if after_mid is None else [W_(after_mid)]),
                                 [(D_MODEL, F32, D_MODEL, 0), (D_MODEL, MXU, D_MODEL, 0)], accs=[(1, D_MODEL)])
    gW["w_q_b"] = _wq_unpad(dwq_p)
    gW["w_kv_b"] = _wkv_unpad(dwk_p, dwv_p)
    gG = {"g_mix": dg_mix, "g_q_a": dg_q_a, "g_kv_a": dg_kv_a, "g_qn": _qk_unpad(dgqn_p),
          "g_kn": _qk_unpad(dgkn_p), "ret_decay_fwd": dlg_f[:, 0, 0][None, :], "ret_decay_bwd": dlg_b[:, 0, 0][None, :],
          "g_ffn": dg_ffn}
    return loss_rows, grad_x, gG, gW


MATS = [("w_in", (1024, 5536), 1), ("w_q_b", (256, 768), 1), ("w_kv_b", (128, 1024), 1), ("w_mla_out", (512, 1024), 1),
        ("w_ret_out", (1024, 1024), 0), ("w_out", (1024, 1024), 0), ("w_gate_up", (1024, 5632), 1), ("w_down", (2816, 1024), 0)]
GAINS = [("g_mix", 1024), ("g_q_a", 256), ("g_kv_a", 128), ("g_qn", 96), ("g_kn", 96), ("ret_decay_fwd", 8), ("ret_decay_bwd", 8),
         ("g_ffn", 1024)]
ORDER = ["g_mix", "w_in", "g_q_a", "w_q_b", "g_kv_a", "w_kv_b", "g_qn", "g_kn", "w_mla_out", "ret_decay_fwd", "ret_decay_bwd",
         "w_ret_out", "w_out", "g_ffn", "w_gate_up", "w_down"]
GAIN_LEN = sum(n for _, n in GAINS)
GAIN_PAD = -(-GAIN_LEN // LANES) * LANES


def _pack_gains(d):
    row = jnp.concatenate([d[n].reshape(1, ln).astype(F32) for n, ln in GAINS], axis=1)
    return jnp.pad(row, ((0, 0), (0, GAIN_PAD - GAIN_LEN)))


def _unpack_gains(row):
    out, off = {}, 0
    for n, ln in GAINS:
        out[n] = row[0, off:off + ln]
        off += ln
    return out


def _unshard(pieces, axis):
    if axis == 0:
        return pieces.reshape((N_DEV * pieces.shape[1], pieces.shape[2]))
    return jnp.concatenate([pieces[p] for p in range(N_DEV)], axis=1)


def _reshard(full, axis):
    if axis == 0:
        return full.reshape((N_DEV, full.shape[0] // N_DEV, full.shape[1]))
    c = full.shape[1] // N_DEV
    return jnp.stack([full[:, c * p:c * (p + 1)] for p in range(N_DEV)])


def _all_gather(shards):
    n = len(shards)

    def body(*refs):
        x_refs, out_refs = refs[:n], refs[n:2 * n]
        send_sems, recv_sems, local_sems = refs[2 * n:]
        x, y, c = lax.axis_index("x"), lax.axis_index("y"), lax.axis_index("c")
        me, sibling = (x, y, c), (x, y, 1 - c)
        chips = [(1 - x, y), (x, 1 - y), (1 - x, 1 - y)]

        def slot(a, px, py, pc):
            return out_refs[a].at[4 * px + 2 * py + pc]

        def copy(a, k, block, to, from_input=False):
            return pltpu.make_async_remote_copy(
                src_ref=x_refs[a] if from_input else slot(a, *block), dst_ref=slot(a, *block),
                send_sem=send_sems.at[a, k], recv_sem=recv_sems.at[a, k], device_id=to, device_id_type=pl.DeviceIdType.MESH)

        mine = [pltpu.make_async_copy(x_refs[a], slot(a, *me), local_sems.at[a]) for a in range(n)]
        first = [copy(a, 0, me, sibling, True) for a in range(n)]
        first += [copy(a, 1 + j, me, (*chip, c), True) for j, chip in enumerate(chips) for a in range(n)]
        for cp in mine + first:
            cp.start()
        passed = []
        for j, chip in enumerate(chips):
            for a in range(n):
                copy(a, 1 + j, (*chip, c), me).wait_recv()
                passed.append(copy(a, 4 + j, (*chip, c), sibling))
                passed[-1].start()
        for a in range(n):
            copy(a, 0, sibling, me).wait_recv()
        for j, chip in enumerate(chips):
            for a in range(n):
                copy(a, 4 + j, (*chip, 1 - c), me).wait_recv()
        for cp in first + passed:
            cp.wait_send()
        for cp in mine:
            cp.wait()

    any_spec = pl.BlockSpec(memory_space=pl.ANY)
    return pl.pallas_call(
        body, name="all_gather_weights", out_shape=[jax.ShapeDtypeStruct((N_DEV,) + s.shape, s.dtype) for s in shards],
        in_specs=[any_spec] * n, out_specs=[any_spec] * n,
        scratch_shapes=[pltpu.SemaphoreType.DMA((n, 7)), pltpu.SemaphoreType.DMA((n, 7)), pltpu.SemaphoreType.DMA((n,))],
    )(*shards)


def _all_to_all(pieces):
    n = len(pieces)

    def body(*refs):
        in_refs, out_refs = refs[:n], refs[n:2 * n]
        send_sems, recv_sems, local_sems = refs[2 * n:]
        x, y, c = lax.axis_index("x"), lax.axis_index("y"), lax.axis_index("c")
        my_id = 4 * x + 2 * y + c
        flips = [(fx, fy, fc) for fx in (0, 1) for fy in (0, 1) for fc in (0, 1)][1:]

        def copy(a, kk, f):
            p = (x ^ f[0], y ^ f[1], c ^ f[2])
            return pltpu.make_async_remote_copy(
                src_ref=in_refs[a].at[4 * p[0] + 2 * p[1] + p[2]], dst_ref=out_refs[a].at[my_id],
                send_sem=send_sems.at[a, kk], recv_sem=recv_sems.at[a, kk], device_id=p, device_id_type=pl.DeviceIdType.MESH)

        mine = [pltpu.make_async_copy(in_refs[a].at[my_id], out_refs[a].at[my_id], local_sems.at[a]) for a in range(n)]
        copies = [copy(a, kk, f) for kk, f in enumerate(flips) for a in range(n)]
        for cp in mine + copies:
            cp.start()
        for cp in copies:
            cp.wait_recv()
        for cp in copies:
            cp.wait_send()
        for cp in mine:
            cp.wait()

    any_spec = pl.BlockSpec(memory_space=pl.ANY)
    return pl.pallas_call(
        body, name="all_to_all_grads", out_shape=[jax.ShapeDtypeStruct(p.shape, p.dtype) for p in pieces],
        in_specs=[any_spec] * n, out_specs=[any_spec] * n,
        scratch_shapes=[pltpu.SemaphoreType.DMA((n, 7)), pltpu.SemaphoreType.DMA((n, 7)), pltpu.SemaphoreType.DMA((n,))],
    )(*pieces)


def _flip_peers(x, y, c):
    flips = [(fx, fy, fc) for fx in (0, 1) for fy in (0, 1) for fc in (0, 1)][1:]
    return [(x ^ fx, y ^ fy, c ^ fc) for fx, fy, fc in flips]


def _split_copies(in_refs, land_refs, send_sems, recv_sems, gather):
    x, y, c = lax.axis_index("x"), lax.axis_index("y"), lax.axis_index("c")
    my_id = 4 * x + 2 * y + c
    copies = []
    for kk, p in enumerate(_flip_peers(x, y, c)):
        for a in range(len(in_refs)):
            src = in_refs[a] if gather else in_refs[a].at[4 * p[0] + 2 * p[1] + p[2]]
            copies.append(pltpu.make_async_remote_copy(
                src_ref=src, dst_ref=land_refs[a].at[my_id], send_sem=send_sems.at[a * 7 + kk], recv_sem=recv_sems.at[a * 7 + kk],
                device_id=p, device_id_type=pl.DeviceIdType.MESH))
    return copies


def _exchange_start(name, srcs, gather, after=None):
    n = len(srcs)
    first_out = 2 * n + (0 if after is None else 1)

    def body(*refs):
        for cp in _split_copies(refs[:n], refs[n:2 * n], refs[first_out], refs[first_out + 1], gather):
            cp.start()
        refs[-1][...] = jnp.zeros_like(refs[-1])

    hbm, sem = pl.BlockSpec(memory_space=pltpu.HBM), pl.BlockSpec(memory_space=pltpu.SEMAPHORE)
    land_shapes = [((N_DEV,) + s.shape if gather else s.shape, s.dtype) for s in srcs]
    lands = [pltpu.with_memory_space_constraint(lax.empty(shp, dt), pltpu.HBM) for shp, dt in land_shapes]
    srcs = [pltpu.with_memory_space_constraint(s, pltpu.HBM) for s in srcs]
    res = pl.pallas_call(
        body, name=name,
        out_shape=[pltpu.SemaphoreType.DMA((7 * n,)), pltpu.SemaphoreType.DMA((7 * n,))] + [pltpu.HBM(s.shape, s.dtype) for s in srcs]
        + [pltpu.HBM(shp, dt) for shp, dt in land_shapes] + [jax.ShapeDtypeStruct((8, LANES), F32)],
        in_specs=[hbm] * (2 * n) + ([] if after is None else [pl.BlockSpec(memory_space=pl.ANY)]),
        out_specs=[sem, sem] + [hbm] * (2 * n) + [pl.BlockSpec(memory_space=pltpu.VMEM)],
        input_output_aliases={i: 2 + i for i in range(2 * n)},
        compiler_params=pltpu.CompilerParams(has_side_effects=pltpu.SideEffectType.DATAFLOW_SIDE_EFFECTING),
    )(*srcs, *lands, *([] if after is None else [after]))
    return res[0], res[1], res[2:2 + n], res[2 + n:2 + 2 * n], res[-1]


def _exchange_wait(name, handles, after, gather):
    send_sems, recv_sems, srcs, lands, _ = handles
    n = len(srcs)

    def body(*refs):
        for cp in _split_copies(refs[:n], refs[n:2 * n], refs[2 * n], refs[2 * n + 1], gather):
            cp.wait_send()
            cp.wait_recv()

    hbm, sem = pl.BlockSpec(memory_space=pltpu.HBM), pl.BlockSpec(memory_space=pltpu.SEMAPHORE)
    res = pl.pallas_call(
        body, name=name, out_shape=[pltpu.HBM(t.shape, t.dtype) for t in list(srcs) + list(lands)],
        in_specs=[hbm] * (2 * n) + [sem, sem, pl.BlockSpec(memory_space=pl.ANY)], out_specs=[hbm] * (2 * n),
        input_output_aliases={i: i for i in range(2 * n)},
        compiler_params=pltpu.CompilerParams(has_side_effects=pltpu.SideEffectType.DATAFLOW_SIDE_EFFECTING),
    )(*srcs, *lands, send_sems, recv_sems, after)
    my_id = 4 * lax.axis_index("x") + 2 * lax.axis_index("y") + lax.axis_index("c")
    own = [s if gather else lax.dynamic_index_in_dim(s, my_id, 0, keepdims=False) for s in res[:n]]
    return [lax.dynamic_update_index_in_dim(land, o, my_id, 0) for land, o in zip(res[n:], own)]


def _adamw(name, parts, w, m, v):
    rows, cols = w.shape
    tr = _pick(rows, (128, 64, 32, 16, 8))
    pspec = pl.BlockSpec((N_DEV, tr, cols), lambda i: (0, i, 0))
    rspec = pl.BlockSpec((tr, cols), lambda i: (i, 0))

    def body(p_ref, w_ref, m_ref, v_ref, g_ref, d_ref, m2_ref, v2_ref):
        g, d, m2, v2 = _adamw_fn([p_ref[s] for s in range(N_DEV)], w_ref[...], m_ref[...], v_ref[...])
        g_ref[...], d_ref[...], m2_ref[...], v2_ref[...] = g, d, m2, v2

    return pl.pallas_call(
        body, name=name, grid=(rows // tr,), in_specs=[pspec, rspec, rspec, rspec], out_specs=[rspec] * 4,
        out_shape=[jax.ShapeDtypeStruct((rows, cols), F32)] * 4,
        compiler_params=pltpu.CompilerParams(dimension_semantics=("parallel",), vmem_limit_bytes=VMEM_LIMIT),
    )(parts, w, m, v)


def kernel(x, positions, g_mix, w_in, g_q_a, w_q_b, g_kv_a, w_kv_b, g_qn, g_kn, w_mla_out, ret_decay_fwd, ret_decay_bwd, w_ret_out, w_out, g_ffn, w_gate_up, w_down, loss_target, m_g_mix, m_w_in, m_g_q_a, m_w_q_b, m_g_kv_a, m_w_kv_b, m_g_qn, m_g_kn, m_w_mla_out, m_ret_decay_fwd, m_ret_decay_bwd, m_w_ret_out, m_w_out, m_g_ffn, m_w_gate_up, m_w_down, v_g_mix, v_w_in, v_g_q_a, v_w_q_b, v_g_kv_a, v_w_kv_b, v_g_qn, v_g_kn, v_w_mla_out, v_ret_decay_fwd, v_ret_decay_bwd, v_w_ret_out, v_w_out, v_g_ffn, v_w_gate_up, v_w_down):
    w = dict(g_mix=g_mix, w_in=w_in, g_q_a=g_q_a, w_q_b=w_q_b, g_kv_a=g_kv_a, w_kv_b=w_kv_b, g_qn=g_qn, g_kn=g_kn, w_mla_out=w_mla_out,
             ret_decay_fwd=ret_decay_fwd, ret_decay_bwd=ret_decay_bwd, w_ret_out=w_ret_out, w_out=w_out, g_ffn=g_ffn,
             w_gate_up=w_gate_up, w_down=w_down)
    m = dict(g_mix=m_g_mix, w_in=m_w_in, g_q_a=m_g_q_a, w_q_b=m_w_q_b, g_kv_a=m_g_kv_a, w_kv_b=m_w_kv_b, g_qn=m_g_qn, g_kn=m_g_kn,
             w_mla_out=m_w_mla_out, ret_decay_fwd=m_ret_decay_fwd, ret_decay_bwd=m_ret_decay_bwd, w_ret_out=m_w_ret_out, w_out=m_w_out,
             g_ffn=m_g_ffn, w_gate_up=m_w_gate_up, w_down=m_w_down)
    v = dict(g_mix=v_g_mix, w_in=v_w_in, g_q_a=v_g_q_a, w_q_b=v_w_q_b, g_kv_a=v_g_kv_a, w_kv_b=v_w_kv_b, g_qn=v_g_qn, g_kn=v_g_kn,
             w_mla_out=v_w_mla_out, ret_decay_fwd=v_ret_decay_fwd, ret_decay_bwd=v_ret_decay_bwd, w_ret_out=v_w_ret_out, w_out=v_w_out,
             g_ffn=v_g_ffn, w_gate_up=v_w_gate_up, w_down=v_w_down)
    gains = {n: w[n].reshape(1, ln) for n, ln in GAINS}

    axis_of = {n: axis for n, _, axis in MATS}
    later = [n for n, _, _ in MATS if n not in FIRST_WEIGHTS]
    gathered = _all_gather([w[n].astype(WIRE) for n in FIRST_WEIGHTS])
    W = {n: _unshard(g, axis_of[n]) for n, g in zip(FIRST_WEIGHTS, gathered)}
    later_handles = _exchange_start("gather_later_start", [w[n].astype(WIRE) for n in later], True, after=gathered[0])

    def late_weights(after):
        lands = _exchange_wait("gather_later_wait", later_handles, after, True)
        return {n: _unshard(g, axis_of[n]) for n, g in zip(later, lands)}

    grad_groups = []

    def grad_hook(g):
        names = tuple(g)
        handles = _exchange_start("grads_start_%d" % len(grad_groups), [_reshard(g[n], axis_of[n]).astype(GWIRE) for n in names], False)
        grad_groups.append((names, handles))
        return handles[4]

    S = x.shape[1]
    pos = positions.reshape(S, 1).astype(F32)
    loss_rows, grad_x, gG, gW = _local_step(x.reshape(S, D_MODEL), pos, loss_target.reshape(S, D_MODEL), gains, W, late_weights, grad_hook,
                                            start_after=later_handles[4])
    loss = lax.psum(jnp.sum(loss_rows), ("x", "y", "c"))

    last = [n for n, _, _ in MATS if n not in EARLY_GRADS + MID_GRADS]
    pieces = [_reshard(gW[n], axis_of[n]).astype(GWIRE) for n in last]
    pieces.append(jnp.broadcast_to(_pack_gains(gG)[None], (N_DEV, 1, GAIN_PAD)))
    late_parts = _all_to_all(pieces)
    parts = dict(zip(last, late_parts))
    for i, (names, handles) in enumerate(grad_groups):
        parts.update(zip(names, _exchange_wait("grads_wait_%d" % i, handles, late_parts[-1], False)))
    out = [dict() for _ in range(4)]
    for n, _, _ in MATS:
        for o, r in zip(out, _adamw("adamw_" + n, parts[n], w[n], m[n], v[n])):
            o[n] = r
    for o, r in zip(out, _adamw("adamw_gains", late_parts[-1], _pack_gains(w), _pack_gains(m), _pack_gains(v))):
        o.update(_unpack_gains(r))
    return (loss, grad_x.reshape(x.shape), *[o[n] for o in out for n in ORDER])
```

```python
import functools

import numpy as np
import jax
import jax.numpy as jnp
from jax import lax
from jax.experimental import pallas as pl
from jax.experimental.pallas import tpu as pltpu

F32 = jnp.float32
MXU = jnp.bfloat16
WIRE = jnp.bfloat16
GWIRE = jnp.bfloat16

N_DEV = 8
D_MODEL = 1024
HEADS = 8
LANES = 128
Q_RANK, KV_RANK = 256, 128
NOPE, ROPE_M, V_M = 64, 32, 64
QK_M = NOPE + ROPE_M
RQK, RV = 64, 128
CHUNK = 128
FFN = 2816
IN_WIDTH = 5536
THETA = 10000.0
EPS = 1e-6
LR, B1, B2, AEPS, WD, STEP = 0.001, 0.9, 0.999, 1e-08, 0.01, 10
VMEM_LIMIT = 56 * 1024 * 1024

NN = ((1,), (0,))
NT = ((1,), (1,))
TN = ((0,), (0,))

P_GATES, P_VR, P_GR, P_QR, P_KR, P_CQ, P_CKV, P_KROPE, P_WIDTH = 0, 2048, 3072, 4096, 4608, 5120, 5376, 5504, 5632
O_CQ, O_CKV, O_KROPE, O_QR, O_KR, O_VR, O_GR, O_GATES = 0, 256, 384, 416, 928, 1440, 2464, 3488


def _dot(a, b, dims):
    return lax.dot_general(a, b, (dims, ((), ())), preferred_element_type=F32)


def _pick(dim, cands):
    for c in cands:
        if dim % c == 0:
            return c
    return dim


def _pairs(t):
    return t.reshape(t.shape[0], 4, 2, 2, 32).transpose(0, 1, 3, 2, 4).reshape(t.shape[0], 512)


def _win_pad(w):
    z = jnp.zeros((w.shape[0], 48), w.dtype)
    kr = w[:, O_KROPE:O_KROPE + 32]
    return jnp.concatenate([w[:, O_GATES:], w[:, O_VR:O_VR + 1024], w[:, O_GR:O_GR + 1024], _pairs(w[:, O_QR:O_QR + 512]),
                            _pairs(w[:, O_KR:O_KR + 512]), w[:, :O_CKV], w[:, O_CKV:O_KROPE], kr[:, :16], z, kr[:, 16:], z], axis=1)


def _win_unpad(g):
    return jnp.concatenate([g[:, P_CQ:P_CQ + 256], g[:, P_CKV:P_CKV + 128], g[:, P_KROPE:P_KROPE + 16], g[:, P_KROPE + 64:P_KROPE + 80],
                            _pairs(g[:, P_QR:P_QR + 512]), _pairs(g[:, P_KR:P_KR + 512]), g[:, P_VR:P_VR + 1024],
                            g[:, P_GR:P_GR + 1024], g[:, P_GATES:P_GATES + 2048]], axis=1)


def _qk_pad(t):
    z = jnp.zeros(t.shape[:-1] + (32,), t.dtype)
    return jnp.concatenate([t[..., 64:80], t[..., 0:48], t[..., 80:96], t[..., 48:64], z], axis=-1)


def _qk_unpad(p):
    return jnp.concatenate([p[..., 16:64], p[..., 80:96], p[..., 0:16], p[..., 64:80]], axis=-1)


def _wq_pad(w):
    return _qk_pad(w.reshape(Q_RANK, HEADS, QK_M)).reshape(Q_RANK, HEADS * LANES)


def _wq_unpad(g):
    return _qk_unpad(g.reshape(Q_RANK, HEADS, LANES)).reshape(Q_RANK, HEADS * QK_M)


def _wkv_pad(w):
    t = w.reshape(KV_RANK, HEADS, NOPE + V_M)
    z = lambda n: jnp.zeros((KV_RANK, HEADS, n), w.dtype)
    wk = jnp.concatenate([z(16), t[..., 0:48], z(16), t[..., 48:64], z(32)], axis=-1)
    wv = jnp.concatenate([t[..., 64:128], z(64)], axis=-1)
    return wk.reshape(KV_RANK, HEADS * LANES), wv.reshape(KV_RANK, HEADS * LANES)


def _wkv_unpad(dwk, dwv):
    k, v = dwk.reshape(KV_RANK, HEADS, LANES), dwv.reshape(KV_RANK, HEADS, LANES)
    return jnp.concatenate([k[..., 16:64], k[..., 80:96], v[..., 0:64]], axis=-1).reshape(KV_RANK, HEADS * (NOPE + V_M))


def _wmla_pad(w):
    t = w.reshape(HEADS, V_M, D_MODEL)
    return jnp.concatenate([t, jnp.zeros_like(t)], axis=1).reshape(HEADS * LANES, D_MODEL)


def _wmla_unpad(g):
    return g.reshape(HEADS, LANES, D_MODEL)[:, :V_M].reshape(HEADS * V_M, D_MODEL)


def _rowwise(name, fn, rows, ts, ins, outs, accs=(), ncol=1):
    n_in, n_out, n_acc = len(ins), len(outs), len(accs)

    def colmap(col):
        if callable(col):
            return lambda i, j: (i, col(j))
        return lambda i, j: (i, col)

    arrays, in_specs = [], []
    for arr, spec in ins:
        arrays.append(arr)
        if spec is None:
            in_specs.append(pl.BlockSpec(arr.shape, functools.partial(lambda i, j, nd: (0,) * nd, nd=arr.ndim)))
        else:
            in_specs.append(pl.BlockSpec((ts, spec[0]), colmap(spec[1])))
    out_shape, out_specs = [], []
    for total, dtype, width, col in outs:
        out_shape.append(jax.ShapeDtypeStruct((rows, total), dtype))
        out_specs.append(pl.BlockSpec((ts, width), colmap(col)))
    for shp in accs:
        out_shape.append(jax.ShapeDtypeStruct(shp, F32))
        out_specs.append(pl.BlockSpec(shp, functools.partial(lambda i, j, nd: (0,) * nd, nd=len(shp))))

    def body(*refs):
        vals = [r[...] for r in refs[:n_in]]
        res = fn(*vals)
        if not isinstance(res, (tuple, list)):
            res = (res,)
        for r, v in zip(refs[n_in:n_in + n_out], res[:n_out]):
            r[...] = v.astype(r.dtype)
        if n_acc:
            first = jnp.logical_and(pl.program_id(0) == 0, pl.program_id(1) == 0)
            for r, v in zip(refs[n_in + n_out:], res[n_out:]):
                @pl.when(first)
                def _(r=r):
                    r[...] = jnp.zeros_like(r)
                r[...] += v.astype(F32)

    res = pl.pallas_call(
        body, name=name, grid=(rows // ts, ncol), in_specs=in_specs, out_specs=out_specs, out_shape=out_shape,
        compiler_params=pltpu.CompilerParams(dimension_semantics=("arbitrary", "arbitrary"), vmem_limit_bytes=VMEM_LIMIT),
    )(*arrays)
    return res


MM_OPERAND_BYTES = 24 * 1024 * 1024


def _mm(name, a, b, mode, add=None, after=None):
    a_halves, b_halves = a.ndim == 3, b.ndim == 3
    assert not a_halves or mode == "nt"
    assert not b_halves or mode == "tn"
    if mode == "nn":
        (M, K), N = a.shape, b.shape[1]
    elif mode == "nt":
        M, K, N = a.shape[-2], a.shape[-1] * (2 if a_halves else 1), b.shape[0]
    else:
        (K, M), N = a.shape, b.shape[-1] * (2 if b_halves else 1)
    tm = _pick(M, (512, 1408, 256, 128)) if mode == "tn" else _pick(M, (1024, 512, 256, 128))
    tn = _pick(N // 2 if b_halves else N, (1408, 1024, 512, 256, 128))
    fits = lambda t: 2 * (tm + tn) * t * a.dtype.itemsize <= MM_OPERAND_BYTES
    kdiv = K // 2 if a_halves else K
    tk = next(t for t in (K, 4096, 2816, 2048, 1408, 1024, 512, 256, 128) if kdiv % t == 0 and (fits(t) or t == 128))
    nk = K // tk
    dims = {"nn": NN, "nt": NT, "tn": TN}[mode]
    if a_halves:
        per = kdiv // tk
        a_spec = pl.BlockSpec((None, tm, tk), lambda i, j, k: (k // per, i, k % per))
    else:
        a_spec = pl.BlockSpec((tk, tm), lambda i, j, k: (k, i)) if mode == "tn" else pl.BlockSpec((tm, tk), lambda i, j, k: (i, k))
    if b_halves:
        perj = (N // 2) // tn
        b_spec = pl.BlockSpec((None, tk, tn), lambda i, j, k: (j // perj, k, j % perj))
    else:
        b_spec = pl.BlockSpec((tn, tk), lambda i, j, k: (j, k)) if mode == "nt" else pl.BlockSpec((tk, tn), lambda i, j, k: (k, j))
    o_spec = pl.BlockSpec((tm, tn), lambda i, j, k: (i, j))
    has_add = add is not None

    def body(*refs):
        a_ref, b_ref, o_ref = refs[0], refs[1], refs[-1]
        d = _dot(a_ref[...], b_ref[...], dims)
        first = (d + refs[2][...]) if has_add else d
        if nk == 1:
            o_ref[...] = first
        else:
            k = pl.program_id(2)

            @pl.when(k == 0)
            def _():
                o_ref[...] = first

            @pl.when(k > 0)
            def _():
                o_ref[...] += d

    args = [a, b] + ([add] if has_add else []) + ([] if after is None else [after])
    specs = [a_spec, b_spec] + ([o_spec] if has_add else []) + ([] if after is None else [pl.BlockSpec(memory_space=pl.ANY)])
    return pl.pallas_call(
        body, name=name, grid=(M // tm, N // tn, nk), in_specs=specs, out_specs=o_spec,
        out_shape=jax.ShapeDtypeStruct((M, N), F32),
        compiler_params=pltpu.CompilerParams(dimension_semantics=("parallel", "parallel", "arbitrary"), vmem_limit_bytes=VMEM_LIMIT),
    )(*args)


def _mm_rows(name, a, b, fn, row_ins, whole_ins, outs, accs=(), mode="nn"):
    (M, K), N = a.shape, b.shape[1 if mode == "nn" else 0]
    tm = _pick(M, (512, 256, 128))
    n_in, n_out = 2 + len(row_ins) + len(whole_ins), len(outs)
    windows = [t if isinstance(t, tuple) else (t, (t.shape[1], 0)) for t in row_ins]
    row_ins = [t for t, _ in windows]
    row_specs = [pl.BlockSpec((tm, w), functools.partial(lambda i, col: (i, col), col=col)) for _, (w, col) in windows]

    def body(*refs):
        d = _dot(refs[0][...], refs[1][...], NN if mode == "nn" else NT)
        res = fn(d, *[r[...] for r in refs[2:n_in]])
        for r, v in zip(refs[n_in:n_in + n_out], res[:n_out]):
            r[...] = v.astype(r.dtype)
        for r, v in zip(refs[n_in + n_out:], res[n_out:]):
            @pl.when(pl.program_id(0) == 0)
            def _(r=r):
                r[...] = jnp.zeros_like(r)
            r[...] += v

    row = pl.BlockSpec((tm, N), lambda i: (i, 0))
    whole = lambda t: pl.BlockSpec(t.shape, functools.partial(lambda i, nd: (0,) * nd, nd=t.ndim))
    return pl.pallas_call(
        body, name=name, grid=(M // tm,),
        in_specs=[pl.BlockSpec((tm, K), lambda i: (i, 0)), whole(b)] + row_specs + [whole(t) for t in whole_ins],
        out_specs=[row] * n_out + [pl.BlockSpec(s, functools.partial(lambda i, nd: (0,) * nd, nd=len(s))) for s in accs],
        out_shape=[jax.ShapeDtypeStruct((M, N), dt) for dt in outs] + [jax.ShapeDtypeStruct(s, F32) for s in accs],
        compiler_params=pltpu.CompilerParams(dimension_semantics=("arbitrary",), vmem_limit_bytes=VMEM_LIMIT),
    )(a, b, *row_ins, *whole_ins)


def _ffn_tiles(S):
    return _pick(S, (512, 256, 128)), _pick(FFN, (1408, 704, 256, 128))


def _gate_up_swiglu(h2, wgu):
    S, K = h2.shape
    tm, tn = _ffn_tiles(S)
    nj = FFN // tn

    def body(a_ref, bg_ref, bu_ref, gu_ref, act_ref):
        a = a_ref[...]
        g, u = _dot(a, bg_ref[...], NN), _dot(a, bu_ref[...], NN)
        gu_ref[0], gu_ref[1] = g, u
        act_ref[...] = _swiglu_fn(g, u).astype(act_ref.dtype)

    return pl.pallas_call(
        body, name="gate_up_swiglu", grid=(S // tm, nj),
        in_specs=[pl.BlockSpec((tm, K), lambda i, j: (i, 0)), pl.BlockSpec((K, tn), lambda i, j: (0, j)),
                  pl.BlockSpec((K, tn), lambda i, j: (0, nj + j))],
        out_specs=[pl.BlockSpec((2, tm, tn), lambda i, j: (0, i, j)), pl.BlockSpec((tm, tn), lambda i, j: (i, j))],
        out_shape=[jax.ShapeDtypeStruct((2, S, FFN), F32), jax.ShapeDtypeStruct((S, FFN), MXU)],
        compiler_params=pltpu.CompilerParams(dimension_semantics=("parallel", "parallel"), vmem_limit_bytes=VMEM_LIMIT),
    )(h2, wgu, wgu)


def _d_act_swiglu(dx2, wdown, gu):
    S, K = dx2.shape
    tm, tn = _ffn_tiles(S)

    def body(a_ref, b_ref, gu_ref, o_ref):
        dact = _dot(a_ref[...], b_ref[...], NT)
        _, vjp = jax.vjp(_swiglu_fn, gu_ref[0], gu_ref[1])
        dg, du = vjp(dact)
        o_ref[0], o_ref[1] = dg.astype(o_ref.dtype), du.astype(o_ref.dtype)

    stacked = pl.BlockSpec((2, tm, tn), lambda i, j: (0, i, j))
    return pl.pallas_call(
        body, name="d_act_swiglu", grid=(S // tm, FFN // tn),
        in_specs=[pl.BlockSpec((tm, K), lambda i, j: (i, 0)), pl.BlockSpec((tn, K), lambda i, j: (j, 0)), stacked],
        out_specs=stacked, out_shape=jax.ShapeDtypeStruct((2, S, FFN), MXU),
        compiler_params=pltpu.CompilerParams(dimension_semantics=("parallel", "parallel"), vmem_limit_bytes=VMEM_LIMIT),
    )(dx2, wdown, gu)


@jax.custom_vjp
def _swap64(x):
    return pltpu.roll(x, 64, 1)


_swap64.defvjp(lambda x: (_swap64(x), None), lambda _, g: (_swap64(g),))


@jax.custom_vjp
def _mxdot(a, b):
    return _dot(a.astype(MXU), b.astype(MXU), NN)


def _mxdot_bwd(res, g):
    a, b = res
    gb = g.astype(MXU)
    return _dot(gb, b.astype(MXU), NT), _dot(a.astype(MXU), gb, TN)


_mxdot.defvjp(lambda a, b: (_mxdot(a, b), (a, b)), _mxdot_bwd)


def _row_sum(t):
    if t.shape[-1] == LANES:
        return lax.dot_general(t, jnp.ones((LANES, LANES), F32), ((NN), ((), ())), precision=lax.Precision.HIGH,
                               preferred_element_type=F32)
    return jnp.sum(t, axis=-1, keepdims=True)


@functools.partial(jax.custom_vjp, nondiff_argnums=(1,))
def _unit_rms(x, n):
    return x * lax.rsqrt(_row_sum(x * x) * (1.0 / n) + EPS)


def _unit_rms_fwd(x, n):
    r = lax.rsqrt(_row_sum(x * x) * (1.0 / n) + EPS)
    y = x * r
    return y, (y, r)


def _unit_rms_bwd(n, res, g):
    y, r = res
    return (r * (g - y * (_row_sum(g * y) * (1.0 / n))),)


_unit_rms.defvjp(_unit_rms_fwd, _unit_rms_bwd)


def _rms(x):
    return _unit_rms(x, x.shape[-1])


def _rmsg_fn(x, g):
    return _rms(x) * g


def _silu(x):
    return x * jax.nn.sigmoid(x)


def _tables_fn(pos, inv_m, sgn_m, inv_r, sgn_r):
    am, ar = pos * inv_m, pos * inv_r
    return jnp.cos(am), jnp.sin(am) * sgn_m, jnp.cos(ar), jnp.sin(ar) * sgn_r


def _head_blocks(t):
    return [t[:, LANES * h:LANES * (h + 1)] for h in range(t.shape[1] // LANES)]


def _mla_prep_fn(cq, ckv, kr, cosm, sinm, gqa, gkva, gqn, gkn, wq, wk, wv):
    cqn = _rms(cq) * gqa
    ckvn = _rms(ckv) * gkva
    q_raw = _mxdot(cqn, wq)
    k_raw = _mxdot(ckvn, wk)
    lane = lax.broadcasted_iota(jnp.int32, (1, HEADS * LANES), 1)
    v = _mxdot(ckvn, wv) + (lane % LANES == V_M).astype(F32)

    def norm_rope(blocks, g, extra):
        outs = []
        for b in blocks:
            if extra is not None:
                b = b + extra
            n = _unit_rms(b, QK_M) * g
            outs.append(n * cosm + _swap64(n) * sinm)
        return jnp.concatenate(outs, axis=1)

    q = norm_rope(_head_blocks(q_raw), gqn, None)
    k = norm_rope(_head_blocks(k_raw), gkn, kr)
    return q, k, v


def _ret_prep_fn(qr, kr, cosr, sinr):
    def rope(t, scale):
        return jnp.concatenate([(b * cosr + _swap64(b) * sinr) * scale for b in _head_blocks(t)], axis=1)
    return rope(qr, 1.0), rope(kr, RQK ** -0.5)


def _ret_post_fn(rf, rb, gr):
    ret = rf + rb
    outs = []
    for b, g in zip(_head_blocks(ret), _head_blocks(gr)):
        outs.append(_silu(g) * _rms(b))
    return jnp.concatenate(outs, axis=1)


def _merge_fn(ga, gb, ya, yb):
    return jax.nn.sigmoid(ga) * ya + jax.nn.sigmoid(gb) * yb


def _swiglu_fn(gate, up):
    return _silu(gate) * up


def _loss_fn(x2, tgt):
    d = x2 - tgt
    return d * (1.0 / D_MODEL), 0.5 * jnp.sum(d * d, axis=0, keepdims=True) * (1.0 / D_MODEL)


def _adamw_fn(parts, w, m, v):
    g = parts[0].astype(F32)
    for p in range(1, N_DEV):
        g = g + parts[p].astype(F32)
    m2 = B1 * m + (1.0 - B1) * g
    v2 = B2 * v + (1.0 - B2) * jnp.square(g)
    m_hat = m2 / (1.0 - B1 ** STEP)
    v_hat = v2 / (1.0 - B2 ** STEP)
    delta = -LR * (m_hat / (jnp.sqrt(v_hat) + AEPS) + WD * w)
    return g, delta, m2, v2


SCALE = QK_M ** -0.5
LOG2E = 1.4426950408889634
FLASH_ROWS = 32


def _flash_fwd(q, k, v):
    S = q.shape[0]
    tq = tk = _pick(S, (512, 256, 128))
    ncb = tk // LANES
    nkv = S // tk
    assert nkv % 2 == 0, "kv tiles are processed in pairs"
    mrows = 64
    c = SCALE * LOG2E

    def body(q_ref, k_ref, v_ref, o_ref, obf_ref, lse_ref, s_a, p_a, s_b, p_b, m_sc, a_sc, acc_sc):
        m_sc[...] = jnp.full_like(m_sc, -jnp.inf)
        acc_sc[...] = jnp.zeros_like(acc_sc)
        qb = q_ref[...]

        def scores(j, s_buf):
            s_buf[...] = _dot(qb, k_ref[pl.ds(pl.multiple_of(j * tk, tk), tk), :], NT)

        def stage(j, s_buf, p_buf, s_next):
            scores(jnp.minimum(j + 1, nkv - 1), s_next)
            for r in range(tq // mrows):
                rows = slice(r * mrows, (r + 1) * mrows)
                cols = [s_buf[rows, LANES * cb:LANES * (cb + 1)] for cb in range(ncb)]
                m_prev = m_sc[rows, :]
                row_max = jnp.max(functools.reduce(jnp.maximum, cols), axis=-1, keepdims=True)
                m_new = jnp.maximum(m_prev, jnp.broadcast_to(row_max, (mrows, LANES)))
                a_sc[rows, :] = jnp.exp2((m_prev - m_new) * c)
                m_sc[rows, :] = m_new
                for cb in range(ncb):
                    p_buf[rows, LANES * cb:LANES * (cb + 1)] = jnp.exp2((cols[cb] - m_new) * c).astype(p_buf.dtype)
            acc_sc[...] = a_sc[...] * acc_sc[...] + _dot(p_buf[...], v_ref[pl.ds(pl.multiple_of(j * tk, tk), tk), :], NN)

        scores(0, s_a)

        def pair_step(t, carry):
            stage(2 * t, s_a, p_a, s_b)
            stage(2 * t + 1, s_b, p_b, s_a)
            return carry

        lax.fori_loop(0, nkv // 2, pair_step, 0, unroll=4)
        acc = acc_sc[...]
        lane = lax.broadcasted_iota(jnp.int32, (1, LANES), 1)
        l = jnp.sum(jnp.where(lane == V_M, acc, 0.0), axis=-1, keepdims=True)
        o = acc / l
        o_ref[...] = o
        obf_ref[...] = o.astype(obf_ref.dtype)
        lse_ref[...] = m_sc[...] * c + jnp.log2(jnp.broadcast_to(l, (tq, LANES)))

    qspec = pl.BlockSpec((tq, LANES), lambda h, i: (i, h))
    kspec = pl.BlockSpec((S, LANES), lambda h, i: (0, h))
    full = jax.ShapeDtypeStruct((S, HEADS * LANES), F32)
    return pl.pallas_call(
        body, name="flash_fwd", grid=(HEADS, S // tq), in_specs=[qspec, kspec, kspec], out_specs=[qspec, qspec, qspec],
        out_shape=[full, jax.ShapeDtypeStruct((S, HEADS * LANES), MXU), full],
        scratch_shapes=[pltpu.VMEM((tq, tk), F32), pltpu.VMEM((tq, tk), MXU)] * 2 + [pltpu.VMEM((tq, LANES), F32)] * 3,
        compiler_params=pltpu.CompilerParams(dimension_semantics=("parallel", "arbitrary"), vmem_limit_bytes=VMEM_LIMIT),
    )(q, k, v)


def _delta_fn(o, do):
    outs = [jnp.broadcast_to(jnp.sum(a * b, axis=-1, keepdims=True), a.shape) for a, b in zip(_head_blocks(o), _head_blocks(do))]
    return do, jnp.concatenate(outs, axis=1)


def _flash_bwd(q, k, v, do, lse, delta):
    S = q.shape[0]
    tq = tk = _pick(S, (512, 256, 128))
    ncb = tk // LANES
    c = SCALE * LOG2E

    nq = S // tq
    assert nq % 2 == 0, "q tiles are processed in pairs"

    def body(q_ref, k_ref, v_ref, do_ref, lse_ref, dl_ref, dq_ref, dk_ref, dv_ref, s_a, dp_a, p_a, ds_a, s_b, dp_b, p_b, ds_b, dk_sc, dv_sc):
        @pl.when(pl.program_id(1) == 0)
        def _():
            dq_ref[...] = jnp.zeros_like(dq_ref)

        dk_sc[...] = jnp.zeros_like(dk_sc)
        dv_sc[...] = jnp.zeros_like(dv_sc)
        kb, vb = k_ref[...], v_ref[...]

        def scores(i, s_buf, dp_buf):
            q_rows = pl.ds(pl.multiple_of(i * tq, tq), tq)
            s_buf[...] = _dot(q_ref[q_rows, :], kb, NT)
            dp_buf[...] = _dot(do_ref[q_rows, :], vb, NT)

        def stage(i, s_buf, dp_buf, p_buf, ds_buf, s_next, dp_next):
            scores(jnp.minimum(i + 1, nq - 1), s_next, dp_next)
            for r in range(tq // FLASH_ROWS):
                rows = slice(r * FLASH_ROWS, (r + 1) * FLASH_ROWS)
                grows = pl.ds(pl.multiple_of(i * tq + r * FLASH_ROWS, FLASH_ROWS), FLASH_ROWS)
                lse_b, dl_b = lse_ref[grows, :], dl_ref[grows, :]
                for cb in range(ncb):
                    sl = slice(LANES * cb, LANES * (cb + 1))
                    p = jnp.exp2(s_buf[rows, sl] * c - lse_b)
                    p_buf[rows, sl] = p.astype(p_buf.dtype)
                    ds_buf[rows, sl] = (p * (dp_buf[rows, sl] - dl_b) * SCALE).astype(ds_buf.dtype)
            q_rows = pl.ds(pl.multiple_of(i * tq, tq), tq)
            dv_sc[...] += _dot(p_buf[...], do_ref[q_rows, :], TN)
            dk_sc[...] += _dot(ds_buf[...], q_ref[q_rows, :], TN)
            dq_ref[q_rows, :] += _dot(ds_buf[...], kb, NN)

        scores(0, s_a, dp_a)

        def pair_step(t, carry):
            stage(2 * t, s_a, dp_a, p_a, ds_a, s_b, dp_b)
            stage(2 * t + 1, s_b, dp_b, p_b, ds_b, s_a, dp_a)
            return carry

        lax.fori_loop(0, nq // 2, pair_step, 0, unroll=2)
        dk_ref[...] = dk_sc[...]
        dv_ref[...] = dv_sc[...]

    hspec = pl.BlockSpec((S, LANES), lambda h, j: (0, h))
    kspec = pl.BlockSpec((tk, LANES), lambda h, j: (j, h))
    full = jax.ShapeDtypeStruct((S, HEADS * LANES), F32)
    tile_bufs = [pltpu.VMEM((tq, tk), F32), pltpu.VMEM((tq, tk), F32), pltpu.VMEM((tq, tk), MXU), pltpu.VMEM((tq, tk), MXU)]
    return pl.pallas_call(
        body, name="flash_bwd", grid=(HEADS, S // tk), in_specs=[hspec, kspec, kspec, hspec, hspec, hspec],
        out_specs=[hspec, kspec, kspec], out_shape=[full, full, full],
        scratch_shapes=tile_bufs + tile_bufs + [pltpu.VMEM((tk, LANES), F32), pltpu.VMEM((tk, LANES), F32)],
        compiler_params=pltpu.CompilerParams(dimension_semantics=("parallel", "arbitrary"), vmem_limit_bytes=VMEM_LIMIT),
    )(q, k, v, do, lse, delta)


def _ret_consts(lgh, head, rev):
    C = CHUNK
    lane = lax.broadcasted_iota(jnp.int32, (1, LANES), 1)
    hm = ((lane // 32) % 2 == head % 2).astype(F32)
    r = lax.broadcasted_iota(jnp.int32, (C, C), 0)
    c = lax.broadcasted_iota(jnp.int32, (C, C), 1)
    diff = ((c - r) if rev else (r - c)).astype(F32)
    mask = (diff > 0) if rev else (diff >= 0)
    dpos = jnp.maximum(diff, 0.0)
    din = jnp.where(mask, jnp.exp(lgh * dpos), 0.0)
    idx = lax.broadcasted_iota(jnp.int32, (C, 1), 0).astype(F32)
    eq = (C - idx) if rev else (idx + 1.0)
    ek = idx if rev else (C - 1.0 - idx)
    qd, kd = jnp.exp(lgh * eq), jnp.exp(lgh * ek)
    cd = jnp.exp(lgh * jnp.full((1, 1), float(C), F32))
    return hm, din, dpos, qd, kd, cd, eq, ek


RET_HEADS_PER_STEP = 4


def _ret_fwd(name, qt, kt, proj, lg, rev):
    S = qt.shape[0]
    C = CHUNK
    TB = _pick(S, (512, 256, 128))
    cb, nb = TB // C, S // TB
    hps = RET_HEADS_PER_STEP
    blk = (lambda g: nb - 1 - g) if rev else (lambda g: g)

    def body(lg_ref, q_ref, k_ref, v_ref, o_ref, st_ref, state_sc):
        hg, g = pl.program_id(0), pl.program_id(1)

        @pl.when(g == 0)
        def _():
            state_sc[...] = jnp.zeros_like(state_sc)

        consts = [_ret_consts(lg_ref[hg * hps + u], u, rev) for u in range(hps)]
        order = list(reversed(range(cb))) if rev else list(range(cb))
        units = [(cc, u) for cc in order for u in range(hps)]

        def operands(cc, u):
            rows = pl.ds(cc * C, C)
            pair = slice(LANES * (u // 2), LANES * (u // 2 + 1))
            hm = consts[u][0]
            return q_ref[rows, pair] * hm, k_ref[rows, pair] * hm, v_ref[rows, LANES * u:LANES * (u + 1)].astype(MXU)

        a, inc = {}, {}
        for cc, u in units:
            q, k, v = operands(cc, u)
            a[cc, u] = _dot(q.astype(MXU), k.astype(MXU), NT) * consts[u][1]
            inc[cc, u] = _dot((k * consts[u][4]).astype(MXU), v, TN)
        for u in range(hps):
            st = state_sc[u]
            for cc in order:
                st_ref[u, cc] = st
                st = st * consts[u][5] + inc[cc, u]
            state_sc[u] = st
        for cc, u in units:
            q, _, v = operands(cc, u)
            cross = _dot((q * consts[u][3]).astype(MXU), st_ref[u, cc].astype(MXU), NN)
            o_ref[pl.ds(cc * C, C), LANES * u:LANES * (u + 1)] = _dot(a[cc, u].astype(MXU), v, NN) + cross

    qk_spec = pl.BlockSpec((TB, LANES * hps // 2), lambda h, g: (blk(g), h))
    return pl.pallas_call(
        body, name=name, grid=(HEADS // hps, nb),
        in_specs=[pl.BlockSpec(memory_space=pltpu.SMEM), qk_spec, qk_spec,
                  pl.BlockSpec((TB, LANES * hps), lambda h, g: (blk(g), P_VR // (LANES * hps) + h))],
        out_specs=[pl.BlockSpec((TB, LANES * hps), lambda h, g: (blk(g), h)),
                   pl.BlockSpec((hps, cb, LANES, LANES), lambda h, g: (h, blk(g), 0, 0))],
        out_shape=[jax.ShapeDtypeStruct((S, HEADS * LANES), F32), jax.ShapeDtypeStruct((HEADS, S // C, LANES, LANES), F32)],
        scratch_shapes=[pltpu.VMEM((hps, LANES, LANES), F32)],
        compiler_params=pltpu.CompilerParams(dimension_semantics=("parallel", "arbitrary"), vmem_limit_bytes=VMEM_LIMIT),
    )(lg, qt, kt, proj)


def _ret_bwd(name, qt, kt, proj, dret, states, lg, rev):
    S = qt.shape[0]
    C = CHUNK
    TB = _pick(S, (512, 256, 128))
    cb, nb = TB // C, S // TB
    hps = RET_HEADS_PER_STEP
    blk = (lambda g: g) if rev else (lambda g: nb - 1 - g)

    def body(lg_ref, q_ref, k_ref, v_ref, do_ref, st_ref, dq_ref, dk_ref, dv_ref, dlg_ref, ds_sc, acc_cc, acc_q, acc_k, acc_s):
        hg, g = pl.program_id(0), pl.program_id(1)

        @pl.when(g == 0)
        def _():
            ds_sc[...] = jnp.zeros_like(ds_sc)
            acc_cc[...] = jnp.zeros_like(acc_cc)
            acc_q[...] = jnp.zeros_like(acc_q)
            acc_k[...] = jnp.zeros_like(acc_k)
            acc_s[...] = jnp.zeros_like(acc_s)

        lgs = [lg_ref[hg * hps + u] for u in range(hps)]
        consts = [_ret_consts(lgs[u], u, rev) for u in range(hps)]
        order = list(range(cb)) if rev else list(reversed(range(cb)))
        units = [(cc, u) for cc in order for u in range(hps)]

        def operands(cc, u):
            rows = pl.ds(cc * C, C)
            pair = slice(LANES * (u // 2), LANES * (u // 2 + 1))
            head = slice(LANES * u, LANES * (u + 1))
            hm = consts[u][0]
            return q_ref[rows, pair] * hm, k_ref[rows, pair] * hm, v_ref[rows, head].astype(MXU), do_ref[rows, head].astype(MXU)

        a, dp, dqs, inc = {}, {}, {}, {}
        for cc, u in units:
            q, k, vb, dob = operands(cc, u)
            a[cc, u] = _dot(q.astype(MXU), k.astype(MXU), NT)
            dp[cc, u] = _dot(dob, vb, NT)
            dqs[cc, u] = _dot(dob, st_ref[u, cc].astype(MXU), NT)
            inc[cc, u] = _dot((q * consts[u][3]).astype(MXU), dob, TN)
        dsn = {}
        for u in range(hps):
            ds = ds_sc[u]
            for cc in order:
                dsn[cc, u] = ds
                ds = ds * consts[u][5] + inc[cc, u]
            ds_sc[u] = ds
        even = {}
        for cc, u in units:
            hm, din, dpos, qd, kd, cd, eq, ek = consts[u]
            rows, head = pl.ds(cc * C, C), slice(LANES * u, LANES * (u + 1))
            q, k, vb, dob = operands(cc, u)
            qb, kb = q.astype(MXU), k.astype(MXU)
            dsnb = dsn[cc, u].astype(MXU)
            da = (dp[cc, u] * din).astype(MXU)
            vds = _dot(vb, dsnb, NT)
            dq_u = (_dot(da, kb, NN) + dqs[cc, u] * qd) * hm
            dk_u = (_dot(da, qb, TN) + vds * kd) * hm
            if u % 2 == 0:
                even[cc] = (dq_u, dk_u)
            else:
                pair = slice(LANES * (u // 2), LANES * (u // 2 + 1))
                dq_ref[rows, pair] = even[cc][0] + dq_u
                dk_ref[rows, pair] = even[cc][1] + dk_u
            dv_ref[rows, head] = _dot((a[cc, u] * din).astype(MXU), dob, TN) + _dot((k * kd).astype(MXU), dsnb, NN)
            acc_cc[u] += dp[cc, u] * a[cc, u] * din * dpos
            acc_q[u] += dqs[cc, u] * q * (qd * eq)
            acc_k[u] += vds * k * (kd * ek)
            acc_s[u] += dsn[cc, u] * st_ref[u, cc] * (cd * float(C))

        @pl.when(g == nb - 1)
        def _():
            for u in range(hps):
                tot = (jnp.sum(acc_cc[u], keepdims=True) + jnp.sum(acc_q[u], keepdims=True)
                       + jnp.sum(acc_k[u], keepdims=True) + jnp.sum(acc_s[u], keepdims=True))
                dlg_ref[u] = jnp.broadcast_to(tot * lgs[u], (8, LANES))

    full = jax.ShapeDtypeStruct((S, HEADS * LANES), F32)
    hspec = pl.BlockSpec((TB, LANES * hps), lambda h, g: (blk(g), h))
    qk_spec = pl.BlockSpec((TB, LANES * hps // 2), lambda h, g: (blk(g), h))
    return pl.pallas_call(
        body, name=name, grid=(HEADS // hps, nb),
        in_specs=[pl.BlockSpec(memory_space=pltpu.SMEM), qk_spec, qk_spec,
                  pl.BlockSpec((TB, LANES * hps), lambda h, g: (blk(g), P_VR // (LANES * hps) + h)),
                  hspec,
                  pl.BlockSpec((hps, cb, LANES, LANES), lambda h, g: (h, blk(g), 0, 0))],
        out_specs=[qk_spec, qk_spec, hspec, pl.BlockSpec((hps, 8, LANES), lambda h, g: (h, 0, 0))],
        out_shape=[jax.ShapeDtypeStruct(qt.shape, F32), jax.ShapeDtypeStruct(kt.shape, F32), full,
                   jax.ShapeDtypeStruct((HEADS, 8, LANES), F32)],
        scratch_shapes=[pltpu.VMEM((hps, LANES, LANES), F32), pltpu.VMEM((hps, C, C), F32), pltpu.VMEM((hps, C, LANES), F32),
                        pltpu.VMEM((hps, C, LANES), F32), pltpu.VMEM((hps, LANES, LANES), F32)],
        compiler_params=pltpu.CompilerParams(dimension_semantics=("parallel", "arbitrary"), vmem_limit_bytes=VMEM_LIMIT),
    )(lg, qt, kt, proj, dret, states)


def _rope_consts():
    inv16 = THETA ** (-jnp.arange(16, dtype=F32) / 16)
    inv32 = THETA ** (-jnp.arange(32, dtype=F32) / 32)
    lane = np.arange(LANES)
    z48 = jnp.zeros((48,), F32)
    inv_m = jnp.concatenate([inv16, z48, inv16, z48])[None, :]
    sgn_m = jnp.asarray(np.where(lane < 16, -1.0, np.where((lane >= 64) & (lane < 80), 1.0, 0.0)), F32)[None, :]
    inv_r = jnp.concatenate([inv32] * 4)[None, :]
    sgn_r = jnp.asarray(np.where(lane < 64, -1.0, 1.0), F32)[None, :]
    return inv_m, sgn_m, inv_r, sgn_r


FIRST_WEIGHTS = ("w_in", "w_q_b", "w_kv_b")
EARLY_GRADS = ("w_down", "w_gate_up", "w_out", "w_ret_out")
MID_GRADS = ("w_mla_out", "w_in")


def _local_step(x, pos, tgt, gains, W, late_weights=None, grad_hook=None, start_after=None):
    S = x.shape[0]
    ts = _pick(S, (256, 128))
    R = lambda a, w=None, c=0: (a, ((a.shape[1] if w is None else w), c))
    W_ = lambda a: (a, None)

    win = _win_pad(W["w_in"])
    wq = _wq_pad(W["w_q_b"])
    wk, wv = _wkv_pad(W["w_kv_b"])
    gqn, gkn = _qk_pad(gains["g_qn"]), _qk_pad(gains["g_kn"])
    g_mix, g_q_a, g_kv_a, g_ffn = gains["g_mix"], gains["g_q_a"], gains["g_kv_a"], gains["g_ffn"]
    lg_f = -jnp.exp(gains["ret_decay_fwd"][0])
    lg_b = -jnp.exp(gains["ret_decay_bwd"][0])

    consts = list(_rope_consts())
    cosm, sinm, cosr, sinr = _rowwise("rope_tables", _tables_fn, S, ts, [R(pos)] + [W_(c) for c in consts],
                                      [(LANES, F32, LANES, 0)] * 4)

    (h,) = _rowwise("rms_mix", _rmsg_fn, S, ts, [R(x), W_(g_mix)], [(D_MODEL, MXU, D_MODEL, 0)])
    proj = _mm("in_proj", h, win, "nn", after=start_after)
    seg = lambda off, w: (proj, (w, off // w))
    mla_ins = [seg(P_CQ, 256), seg(P_CKV, 128), seg(P_KROPE, 128), R(cosm), R(sinm),
               W_(g_q_a), W_(g_kv_a), W_(gqn), W_(gkn), W_(wq), W_(wk), W_(wv)]
    q, k, v = _rowwise("mla_prep", _mla_prep_fn, S, ts, mla_ins, [(HEADS * LANES, MXU, HEADS * LANES, 0)] * 3)
    o, o_bf, lse = _flash_fwd(q, k, v)
    if late_weights is not None:
        W = {**W, **late_weights(lse)}
    wmla = _wmla_pad(W["w_mla_out"])
    wret, wout, wgu, wdown = W["w_ret_out"], W["w_out"], W["w_gate_up"], W["w_down"]
    y_a = _mm("mla_out", o_bf, wmla, "nn")

    ret_ins = [seg(P_QR, 512), seg(P_KR, 512), R(cosr), R(sinr)]
    qt, kt = _rowwise("ret_prep", _ret_prep_fn, S, ts, ret_ins, [(512, F32, 512, 0)] * 2)
    ret_f, st_f = _ret_fwd("ret_fwd_f", qt, kt, proj, lg_f, False)
    ret_b, st_b = _ret_fwd("ret_fwd_b", qt, kt, proj, lg_b, True)
    post_ins = [R(ret_f), R(ret_b), seg(P_GR, 1024)]
    (o_b,) = _rowwise("ret_post", _ret_post_fn, S, ts, post_ins, [(1024, MXU, 1024, 0)])
    y_b, merged = _mm_rows("ret_out_merge", o_b, wret, lambda yb, ga, gb, ya: (yb, _merge_fn(ga, gb, ya, yb)),
                           [seg(P_GATES, 1024), (proj, (1024, 1)), R(y_a)], [], [F32, MXU])
    merge_ins = [seg(P_GATES, 1024), (proj, (1024, 1)), R(y_a), R(y_b)]
    def residual_rms(d, xx, g):
        r = d + xx
        return r, _rmsg_fn(r, g)

    x1, h2 = _mm_rows("out_proj_rms_ffn", merged, wout, residual_rms, [x], [g_ffn], [F32, MXU])
    gu, act = _gate_up_swiglu(h2, wgu)

    def residual_loss(d, xx, t):
        dx, rows = _loss_fn(d + xx, t)
        return dx, dx, rows

    dx2, dx2_bf, loss_rows = _mm_rows("down_proj_loss", act, wdown, residual_loss, [x1, tgt], [], [F32, MXU], accs=[(1, D_MODEL)])

    gW = {}
    gW["w_down"] = _mm("d_w_down", act, dx2_bf, "tn")
    dgu = _d_act_swiglu(dx2_bf, wdown, gu)
    gW["w_gate_up"] = _mm("d_w_gate_up", h2, dgu, "tn")
    dh2 = _mm("d_h2", dgu, wgu, "nt")

    def rms_bwd(xx, g, dh, dres):
        _, vjp = jax.vjp(_rmsg_fn, xx, g)
        dx, dg = vjp(dh)
        dx = dx + dres
        return dx, dx, dg

    dx1, dx1_bf, dg_ffn = _rowwise("rms_ffn_bwd", rms_bwd, S, ts, [R(x1), W_(g_ffn), R(dh2), R(dx2)],
                                   [(D_MODEL, F32, D_MODEL, 0), (D_MODEL, MXU, D_MODEL, 0)], accs=[(1, D_MODEL)])
    gW["w_out"] = _mm("d_w_out", merged, dx1_bf, "tn")
    def merge_bwd(dm, ga, gb, ya, yb):
        _, vjp = jax.vjp(_merge_fn, ga, gb, ya, yb)
        return vjp(dm)

    dga, dgb, dy_a, dy_b = _mm_rows("d_merged_merge_bwd", dx1_bf, wout, merge_bwd, merge_ins, [], [MXU] * 4, mode="nt")
    gW["w_ret_out"] = _mm("d_w_ret_out", o_b, dy_b, "tn")
    after_early = [] if grad_hook is None else [grad_hook({n: gW[n] for n in EARLY_GRADS})]

    def post_bwd(dob, rf, rb, gr, *_):
        _, vjp = jax.vjp(_ret_post_fn, rf, rb, gr)
        drf, _, dgr = vjp(dob)
        return drf, dgr

    dret, dg_r = _mm_rows("d_o_b_ret_post_bwd", dy_b, wret, post_bwd, post_ins, after_early, [F32, MXU], mode="nt")
    dq_f, dk_f, dv_f, dlg_f = _ret_bwd("ret_bwd_f", qt, kt, proj, dret, st_f, lg_f, False)
    dq_b, dk_b, dv_b, dlg_b = _ret_bwd("ret_bwd_b", qt, kt, proj, dret, st_b, lg_b, True)

    def ret_prep_bwd(qr, kr, cosr_, sinr_, dqf, dqb, dkf, dkb, dvf, dvb):
        _, vjp = jax.vjp(lambda a, b: _ret_prep_fn(a, b, cosr_, sinr_), qr, kr)
        dqr, dkr = vjp((dqf + dqb, dkf + dkb))
        return dqr, dkr, dvf + dvb

    dq_r, dk_r, dv_r = _rowwise("ret_prep_bwd", ret_prep_bwd, S, ts, ret_ins + [R(t) for t in (dq_f, dq_b, dk_f, dk_b, dv_f, dv_b)],
                                [(512, MXU, 512, 0), (512, MXU, 512, 0), (1024, MXU, 1024, 0)])

    gW_mla_p = _mm("d_w_mla_out", o_bf, dy_a, "tn")
    do_bf, delta = _mm_rows("d_o_attn_delta", dy_a, wmla, lambda d, oo, *_: _delta_fn(oo, d), [o], after_early, [MXU, F32], mode="nt")
    dq, dk, dv = _flash_bwd(q, k, v, do_bf, lse, delta)

    def mla_prep_bwd(cq, ckv, kr, cosm_, sinm_, gqa, gkva, gqn_, gkn_, wq_, wk_, wv_, dq_, dk_, dv_):
        f = lambda cq, ckv, kr, gqa, gkva, gqn_, gkn_, wq_, wk_, wv_: _mla_prep_fn(cq, ckv, kr, cosm_, sinm_, gqa, gkva, gqn_, gkn_, wq_, wk_, wv_)
        _, vjp = jax.vjp(f, cq, ckv, kr, gqa, gkva, gqn_, gkn_, wq_.astype(F32), wk_.astype(F32), wv_.astype(F32))
        return vjp((dq_, dk_, dv_))

    mb = _rowwise("mla_prep_bwd", mla_prep_bwd, S, ts, mla_ins + [R(dq), R(dk), R(dv)],
                  [(256, MXU, 256, 0), (128, MXU, 128, 0), (128, MXU, 128, 0)],
                  accs=[(1, 256), (1, 128), (1, LANES), (1, LANES), (256, HEADS * LANES), (128, HEADS * LANES), (128, HEADS * LANES)])
    dc_q, dc_kv, dk_rope, dg_q_a, dg_kv_a, dgqn_p, dgkn_p, dwq_p, dwk_p, dwv_p = mb

    dproj = jnp.concatenate([dga, dgb, dv_r, dg_r, dq_r, dk_r, dc_q, dc_kv, dk_rope], axis=1)
    gW["w_in"] = _win_unpad(_mm("d_w_in", h, dproj, "tn"))
    gW["w_mla_out"] = _wmla_unpad(gW_mla_p)
    after_mid = None if grad_hook is None else grad_hook({n: gW[n] for n in MID_GRADS})
    dh = _mm("d_h", dproj, win, "nt", after=after_mid)
    grad_x, _, dg_mix = _rowwise("rms_mix_bwd", lambda a, b, c, d, *_: rms_bwd(a, b, c, d), S, ts,
                                 [R(x), W_(g_mix), R(dh), R(dx1)] + ([] if after_mid is None else [W_(after_mid)]),
                                 [(D_MODEL, F32, D_MODEL, 0), (D_MODEL, MXU, D_MODEL, 0)], accs=[(1, D_MODEL)])
    gW["w_q_b"] = _wq_unpad(dwq_p)
    gW["w_kv_b"] = _wkv_unpad(dwk_p, dwv_p)
    gG = {"g_mix": dg_mix, "g_q_a": dg_q_a, "g_kv_a": dg_kv_a, "g_qn": _qk_unpad(dgqn_p),
          "g_kn": _qk_unpad(dgkn_p), "ret_decay_fwd": dlg_f[:, 0, 0][None, :], "ret_decay_bwd": dlg_b[:, 0, 0][None, :],
          "g_ffn": dg_ffn}
    return loss_rows, grad_x, gG, gW


MATS = [("w_in", (1024, 5536), 1), ("w_q_b", (256, 768), 1), ("w_kv_b", (128, 1024), 1), ("w_mla_out", (512, 1024), 1),
        ("w_ret_out", (1024, 1024), 0), ("w_out", (1024, 1024), 0), ("w_gate_up", (1024, 5632), 1), ("w_down", (2816, 1024), 0)]
GAINS = [("g_mix", 1024), ("g_q_a", 256), ("g_kv_a", 128), ("g_qn", 96), ("g_kn", 96), ("ret_decay_fwd", 8), ("ret_decay_bwd", 8),
         ("g_ffn", 1024)]
ORDER = ["g_mix", "w_in", "g_q_a", "w_q_b", "g_kv_a", "w_kv_b", "g_qn", "g_kn", "w_mla_out", "ret_decay_fwd", "ret_decay_bwd",
         "w_ret_out", "w_out", "g_ffn", "w_gate_up", "w_down"]
GAIN_LEN = sum(n for _, n in GAINS)
GAIN_PAD = -(-GAIN_LEN // LANES) * LANES


def _pack_gains(d):
    row = jnp.concatenate([d[n].reshape(1, ln).astype(F32) for n, ln in GAINS], axis=1)
    return jnp.pad(row, ((0, 0), (0, GAIN_PAD - GAIN_LEN)))


def _unpack_gains(row):
    out, off = {}, 0
    for n, ln in GAINS:
        out[n] = row[0, off:off + ln]
        off += ln
    return out


def _unshard(pieces, axis):
    if axis == 0:
        return pieces.reshape((N_DEV * pieces.shape[1], pieces.shape[2]))
    return jnp.concatenate([pieces[p] for p in range(N_DEV)], axis=1)


def _reshard(full, axis):
    if axis == 0:
        return full.reshape((N_DEV, full.shape[0] // N_DEV, full.shape[1]))
    c = full.shape[1] // N_DEV
    return jnp.stack([full[:, c * p:c * (p + 1)] for p in range(N_DEV)])


def _all_gather(shards):
    n = len(shards)

    def body(*refs):
        x_refs, out_refs = refs[:n], refs[n:2 * n]
        send_sems, recv_sems, local_sems = refs[2 * n:]
        x, y, c = lax.axis_index("x"), lax.axis_index("y"), lax.axis_index("c")
        me, sibling = (x, y, c), (x, y, 1 - c)
        chips = [(1 - x, y), (x, 1 - y), (1 - x, 1 - y)]

        def slot(a, px, py, pc):
            return out_refs[a].at[4 * px + 2 * py + pc]

        def copy(a, k, block, to, from_input=False):
            return pltpu.make_async_remote_copy(
                src_ref=x_refs[a] if from_input else slot(a, *block), dst_ref=slot(a, *block),
                send_sem=send_sems.at[a, k], recv_sem=recv_sems.at[a, k], device_id=to, device_id_type=pl.DeviceIdType.MESH)

        mine = [pltpu.make_async_copy(x_refs[a], slot(a, *me), local_sems.at[a]) for a in range(n)]
        first = [copy(a, 0, me, sibling, True) for a in range(n)]
        first += [copy(a, 1 + j, me, (*chip, c), True) for j, chip in enumerate(chips) for a in range(n)]
        for cp in mine + first:
            cp.start()
        passed = []
        for j, chip in enumerate(chips):
            for a in range(n):
                copy(a, 1 + j, (*chip, c), me).wait_recv()
                passed.append(copy(a, 4 + j, (*chip, c), sibling))
                passed[-1].start()
        for a in range(n):
            copy(a, 0, sibling, me).wait_recv()
        for j, chip in enumerate(chips):
            for a in range(n):
                copy(a, 4 + j, (*chip, 1 - c), me).wait_recv()
        for cp in first + passed:
            cp.wait_send()
        for cp in mine:
            cp.wait()

    any_spec = pl.BlockSpec(memory_space=pl.ANY)
    return pl.pallas_call(
        body, name="all_gather_weights", out_shape=[jax.ShapeDtypeStruct((N_DEV,) + s.shape, s.dtype) for s in shards],
        in_specs=[any_spec] * n, out_specs=[any_spec] * n,
        scratch_shapes=[pltpu.SemaphoreType.DMA((n, 7)), pltpu.SemaphoreType.DMA((n, 7)), pltpu.SemaphoreType.DMA((n,))],
    )(*shards)


def _all_to_all(pieces):
    n = len(pieces)

    def body(*refs):
        in_refs, out_refs = refs[:n], refs[n:2 * n]
        send_sems, recv_sems, local_sems = refs[2 * n:]
        x, y, c = lax.axis_index("x"), lax.axis_index("y"), lax.axis_index("c")
        my_id = 4 * x + 2 * y + c
        flips = [(fx, fy, fc) for fx in (0, 1) for fy in (0, 1) for fc in (0, 1)][1:]

        def copy(a, kk, f):
            p = (x ^ f[0], y ^ f[1], c ^ f[2])
            return pltpu.make_async_remote_copy(
                src_ref=in_refs[a].at[4 * p[0] + 2 * p[1] + p[2]], dst_ref=out_refs[a].at[my_id],
                send_sem=send_sems.at[a, kk], recv_sem=recv_sems.at[a, kk], device_id=p, device_id_type=pl.DeviceIdType.MESH)

        mine = [pltpu.make_async_copy(in_refs[a].at[my_id], out_refs[a].at[my_id], local_sems.at[a]) for a in range(n)]
        copies = [copy(a, kk, f) for kk, f in enumerate(flips) for a in range(n)]
        for cp in mine + copies:
            cp.start()
        for cp in copies:
            cp.wait_recv()
        for cp in copies:
            cp.wait_send()
        for cp in mine:
            cp.wait()

    any_spec = pl.BlockSpec(memory_space=pl.ANY)
    return pl.pallas_call(
        body, name="all_to_all_grads", out_shape=[jax.ShapeDtypeStruct(p.shape, p.dtype) for p in pieces],
        in_specs=[any_spec] * n, out_specs=[any_spec] * n,
        scratch_shapes=[pltpu.SemaphoreType.DMA((n, 7)), pltpu.SemaphoreType.DMA((n, 7)), pltpu.SemaphoreType.DMA((n,))],
    )(*pieces)


def _flip_peers(x, y, c):
    flips = [(fx, fy, fc) for fx in (0, 1) for fy in (0, 1) for fc in (0, 1)][1:]
    return [(x ^ fx, y ^ fy, c ^ fc) for fx, fy, fc in flips]


def _split_copies(in_refs, land_refs, send_sems, recv_sems, gather):
    x, y, c = lax.axis_index("x"), lax.axis_index("y"), lax.axis_index("c")
    my_id = 4 * x + 2 * y + c
    copies = []
    for kk, p in enumerate(_flip_peers(x, y, c)):
        for a in range(len(in_refs)):
            src = in_refs[a] if gather else in_refs[a].at[4 * p[0] + 2 * p[1] + p[2]]
            copies.append(pltpu.make_async_remote_copy(
                src_ref=src, dst_ref=land_refs[a].at[my_id], send_sem=send_sems.at[a * 7 + kk], recv_sem=recv_sems.at[a * 7 + kk],
                device_id=p, device_id_type=pl.DeviceIdType.MESH))
    return copies


def _exchange_start(name, srcs, gather, after=None):
    n = len(srcs)
    first_out = 2 * n + (0 if after is None else 1)

    def body(*refs):
        for cp in _split_copies(refs[:n], refs[n:2 * n], refs[first_out], refs[first_out + 1], gather):
            cp.start()
        refs[-1][...] = jnp.zeros_like(refs[-1])

    hbm, sem = pl.BlockSpec(memory_space=pltpu.HBM), pl.BlockSpec(memory_space=pltpu.SEMAPHORE)
    land_shapes = [((N_DEV,) + s.shape if gather else s.shape, s.dtype) for s in srcs]
    lands = [pltpu.with_memory_space_constraint(lax.empty(shp, dt), pltpu.HBM) for shp, dt in land_shapes]
    srcs = [pltpu.with_memory_space_constraint(s, pltpu.HBM) for s in srcs]
    res = pl.pallas_call(
        body, name=name,
        out_shape=[pltpu.SemaphoreType.DMA((7 * n,)), pltpu.SemaphoreType.DMA((7 * n,))] + [pltpu.HBM(s.shape, s.dtype) for s in srcs]
        + [pltpu.HBM(shp, dt) for shp, dt in land_shapes] + [jax.ShapeDtypeStruct((8, LANES), F32)],
        in_specs=[hbm] * (2 * n) + ([] if after is None else [pl.BlockSpec(memory_space=pl.ANY)]),
        out_specs=[sem, sem] + [hbm] * (2 * n) + [pl.BlockSpec(memory_space=pltpu.VMEM)],
        input_output_aliases={i: 2 + i for i in range(2 * n)},
        compiler_params=pltpu.CompilerParams(has_side_effects=pltpu.SideEffectType.DATAFLOW_SIDE_EFFECTING),
    )(*srcs, *lands, *([] if after is None else [after]))
    return res[0], res[1], res[2:2 + n], res[2 + n:2 + 2 * n], res[-1]


def _exchange_wait(name, handles, after, gather):
    send_sems, recv_sems, srcs, lands, _ = handles
    n = len(srcs)

    def body(*refs):
        for cp in _split_copies(refs[:n], refs[n:2 * n], refs[2 * n], refs[2 * n + 1], gather):
            cp.wait_send()
            cp.wait_recv()

    hbm, sem = pl.BlockSpec(memory_space=pltpu.HBM), pl.BlockSpec(memory_space=pltpu.SEMAPHORE)
    res = pl.pallas_call(
        body, name=name, out_shape=[pltpu.HBM(t.shape, t.dtype) for t in list(srcs) + list(lands)],
        in_specs=[hbm] * (2 * n) + [sem, sem, pl.BlockSpec(memory_space=pl.ANY)], out_specs=[hbm] * (2 * n),
        input_output_aliases={i: i for i in range(2 * n)},
        compiler_params=pltpu.CompilerParams(has_side_effects=pltpu.SideEffectType.DATAFLOW_SIDE_EFFECTING),
    )(*srcs, *lands, send_sems, recv_sems, after)
    my_id = 4 * lax.axis_index("x") + 2 * lax.axis_index("y") + lax.axis_index("c")
    own = [s if gather else lax.dynamic_index_in_dim(s, my_id, 0, keepdims=False) for s in res[:n]]
    return [lax.dynamic_update_index_in_dim(land, o, my_id, 0) for land, o in zip(res[n:], own)]


def _adamw(name, parts, w, m, v):
    rows, cols = w.shape
    tr = _pick(rows, (128, 64, 32, 16, 8))
    pspec = pl.BlockSpec((N_DEV, tr, cols), lambda i: (0, i, 0))
    rspec = pl.BlockSpec((tr, cols), lambda i: (i, 0))

    def body(p_ref, w_ref, m_ref, v_ref, g_ref, d_ref, m2_ref, v2_ref):
        g, d, m2, v2 = _adamw_fn([p_ref[s] for s in range(N_DEV)], w_ref[...], m_ref[...], v_ref[...])
        g_ref[...], d_ref[...], m2_ref[...], v2_ref[...] = g, d, m2, v2

    return pl.pallas_call(
        body, name=name, grid=(rows // tr,), in_specs=[pspec, rspec, rspec, rspec], out_specs=[rspec] * 4,
        out_shape=[jax.ShapeDtypeStruct((rows, cols), F32)] * 4,
        compiler_params=pltpu.CompilerParams(dimension_semantics=("parallel",), vmem_limit_bytes=VMEM_LIMIT),
    )(parts, w, m, v)


def kernel(x, positions, g_mix, w_in, g_q_a, w_q_b, g_kv_a, w_kv_b, g_qn, g_kn, w_mla_out, ret_decay_fwd, ret_decay_bwd, w_ret_out, w_out, g_ffn, w_gate_up, w_down, loss_target, m_g_mix, m_w_in, m_g_q_a, m_w_q_b, m_g_kv_a, m_w_kv_b, m_g_qn, m_g_kn, m_w_mla_out, m_ret_decay_fwd, m_ret_decay_bwd, m_w_ret_out, m_w_out, m_g_ffn, m_w_gate_up, m_w_down, v_g_mix, v_w_in, v_g_q_a, v_w_q_b, v_g_kv_a, v_w_kv_b, v_g_qn, v_g_kn, v_w_mla_out, v_ret_decay_fwd, v_ret_decay_bwd, v_w_ret_out, v_w_out, v_g_ffn, v_w_gate_up, v_w_down):
    w = dict(g_mix=g_mix, w_in=w_in, g_q_a=g_q_a, w_q_b=w_q_b, g_kv_a=g_kv_a, w_kv_b=w_kv_b, g_qn=g_qn, g_kn=g_kn, w_mla_out=w_mla_out,
             ret_decay_fwd=ret_decay_fwd, ret_decay_bwd=ret_decay_bwd, w_ret_out=w_ret_out, w_out=w_out, g_ffn=g_ffn,
             w_gate_up=w_gate_up, w_down=w_down)
    m = dict(g_mix=m_g_mix, w_in=m_w_in, g_q_a=m_g_q_a, w_q_b=m_w_q_b, g_kv_a=m_g_kv_a, w_kv_b=m_w_kv_b, g_qn=m_g_qn, g_kn=m_g_kn,
             w_mla_out=m_w_mla_out, ret_decay_fwd=m_ret_decay_fwd, ret_decay_bwd=m_ret_decay_bwd, w_ret_out=m_w_ret_out, w_out=m_w_out,
             g_ffn=m_g_ffn, w_gate_up=m_w_gate_up, w_down=m_w_down)
    v = dict(g_mix=v_g_mix, w_in=v_w_in, g_q_a=v_g_q_a, w_q_b=v_w_q_b, g_kv_a=v_g_kv_a, w_kv_b=v_w_kv_b, g_qn=v_g_qn, g_kn=v_g_kn,
             w_mla_out=v_w_mla_out, ret_decay_fwd=v_ret_decay_fwd, ret_decay_bwd=v_ret_decay_bwd, w_ret_out=v_w_ret_out, w_out=v_w_out,
             g_ffn=v_g_ffn, w_gate_up=v_w_gate_up, w_down=v_w_down)
    gains = {n: w[n].reshape(1, ln) for n, ln in GAINS}

    axis_of = {n: axis for n, _, axis in MATS}
    later = [n for n, _, _ in MATS if n not in FIRST_WEIGHTS]
    gathered = _all_gather([w[n].astype(WIRE) for n in FIRST_WEIGHTS])
    W = {n: _unshard(g, axis_of[n]) for n, g in zip(FIRST_WEIGHTS, gathered)}
    later_handles = _exchange_start("gather_later_start", [w[n].astype(WIRE) for n in later], True, after=gathered[0])

    def late_weights(after):
        lands = _exchange_wait("gather_later_wait", later_handles, after, True)
        return {n: _unshard(g, axis_of[n]) for n, g in zip(later, lands)}

    grad_groups = []

    def grad_hook(g):
        names = tuple(g)
        handles = _exchange_start("grads_start_%d" % len(grad_groups), [_reshard(g[n], axis_of[n]).astype(GWIRE) for n in names], False)
        grad_groups.append((names, handles))
        return handles[4]

    S = x.shape[1]
    pos = positions.reshape(S, 1).astype(F32)
    loss_rows, grad_x, gG, gW = _local_step(x.reshape(S, D_MODEL), pos, loss_target.reshape(S, D_MODEL), gains, W, late_weights, grad_hook,
                                            start_after=later_handles[4])
    loss = lax.psum(jnp.sum(loss_rows), ("x", "y", "c"))

    last = [n for n, _, _ in MATS if n not in EARLY_GRADS + MID_GRADS]
    pieces = [_reshard(gW[n], axis_of[n]).astype(GWIRE) for n in last]
    pieces.append(jnp.broadcast_to(_pack_gains(gG)[None], (N_DEV, 1, GAIN_PAD)))
    late_parts = _all_to_all(pieces)
    parts = dict(zip(last, late_parts))
    for i, (names, handles) in enumerate(grad_groups):
        parts.update(zip(names, _exchange_wait("grads_wait_%d" % i, handles, late_parts[-1], False)))
    out = [dict() for _ in range(4)]
    for n, _, _ in MATS:
        for o, r in zip(out, _adamw("adamw_" + n, parts[n], w[n], m[n], v[n])):
            o[n] = r
    for o, r in zip(out, _adamw("adamw_gains", late_parts[-1], _pack_gains(w), _pack_gains(m), _pack_gains(v))):
        o.update(_unpack_gains(r))
    return (loss, grad_x.reshape(x.shape), *[o[n] for o in out for n in ORDER])
```

```python
import functools

import numpy as np
import jax
import jax.numpy as jnp
from jax import lax
from jax.experimental import pallas as pl
from jax.experimental.pallas import tpu as pltpu

F32 = jnp.float32
MXU = jnp.bfloat16
WIRE = jnp.bfloat16
GWIRE = jnp.bfloat16

N_DEV = 8
D_MODEL = 1024
HEADS = 8
LANES = 128
Q_RANK, KV_RANK = 256, 128
NOPE, ROPE_M, V_M = 64, 32, 64
QK_M = NOPE + ROPE_M
RQK, RV = 64, 128
CHUNK = 128
FFN = 2816
IN_WIDTH = 5536
THETA = 10000.0
EPS = 1e-6
LR, B1, B2, AEPS, WD, STEP = 0.001, 0.9, 0.999, 1e-08, 0.01, 10
VMEM_LIMIT = 56 * 1024 * 1024

NN = ((1,), (0,))
NT = ((1,), (1,))
TN = ((0,), (0,))

P_GATES, P_VR, P_GR, P_QR, P_KR, P_CQ, P_CKV, P_KROPE, P_WIDTH = 0, 2048, 3072, 4096, 4608, 5120, 5376, 5504, 5632
O_CQ, O_CKV, O_KROPE, O_QR, O_KR, O_VR, O_GR, O_GATES = 0, 256, 384, 416, 928, 1440, 2464, 3488


def _dot(a, b, dims):
    return lax.dot_general(a, b, (dims, ((), ())), preferred_element_type=F32)


def _pick(dim, cands):
    for c in cands:
        if dim % c == 0:
            return c
    return dim


def _pairs(t):
    return t.reshape(t.shape[0], 4, 2, 2, 32).transpose(0, 1, 3, 2, 4).reshape(t.shape[0], 512)


def _win_pad(w):
    z = jnp.zeros((w.shape[0], 48), w.dtype)
    kr = w[:, O_KROPE:O_KROPE + 32]
    return jnp.concatenate([w[:, O_GATES:], w[:, O_VR:O_VR + 1024], w[:, O_GR:O_GR + 1024], _pairs(w[:, O_QR:O_QR + 512]),
                            _pairs(w[:, O_KR:O_KR + 512]), w[:, :O_CKV], w[:, O_CKV:O_KROPE], kr[:, :16], z, kr[:, 16:], z], axis=1)


def _win_unpad(g):
    return jnp.concatenate([g[:, P_CQ:P_CQ + 256], g[:, P_CKV:P_CKV + 128], g[:, P_KROPE:P_KROPE + 16], g[:, P_KROPE + 64:P_KROPE + 80],
                            _pairs(g[:, P_QR:P_QR + 512]), _pairs(g[:, P_KR:P_KR + 512]), g[:, P_VR:P_VR + 1024],
                            g[:, P_GR:P_GR + 1024], g[:, P_GATES:P_GATES + 2048]], axis=1)


def _qk_pad(t):
    z = jnp.zeros(t.shape[:-1] + (32,), t.dtype)
    return jnp.concatenate([t[..., 64:80], t[..., 0:48], t[..., 80:96], t[..., 48:64], z], axis=-1)


def _qk_unpad(p):
    return jnp.concatenate([p[..., 16:64], p[..., 80:96], p[..., 0:16], p[..., 64:80]], axis=-1)


def _wq_pad(w):
    return _qk_pad(w.reshape(Q_RANK, HEADS, QK_M)).reshape(Q_RANK, HEADS * LANES)


def _wq_unpad(g):
    return _qk_unpad(g.reshape(Q_RANK, HEADS, LANES)).reshape(Q_RANK, HEADS * QK_M)


def _wkv_pad(w):
    t = w.reshape(KV_RANK, HEADS, NOPE + V_M)
    z = lambda n: jnp.zeros((KV_RANK, HEADS, n), w.dtype)
    wk = jnp.concatenate([z(16), t[..., 0:48], z(16), t[..., 48:64], z(32)], axis=-1)
    wv = jnp.concatenate([t[..., 64:128], z(64)], axis=-1)
    return wk.reshape(KV_RANK, HEADS * LANES), wv.reshape(KV_RANK, HEADS * LANES)


def _wkv_unpad(dwk, dwv):
    k, v = dwk.reshape(KV_RANK, HEADS, LANES), dwv.reshape(KV_RANK, HEADS, LANES)
    return jnp.concatenate([k[..., 16:64], k[..., 80:96], v[..., 0:64]], axis=-1).reshape(KV_RANK, HEADS * (NOPE + V_M))


def _wmla_pad(w):
    t = w.reshape(HEADS, V_M, D_MODEL)
    return jnp.concatenate([t, jnp.zeros_like(t)], axis=1).reshape(HEADS * LANES, D_MODEL)


def _wmla_unpad(g):
    return g.reshape(HEADS, LANES, D_MODEL)[:, :V_M].reshape(HEADS * V_M, D_MODEL)


def _rowwise(name, fn, rows, ts, ins, outs, accs=(), ncol=1):
    n_in, n_out, n_acc = len(ins), len(outs), len(accs)

    def colmap(col):
        if callable(col):
            return lambda i, j: (i, col(j))
        return lambda i, j: (i, col)

    arrays, in_specs = [], []
    for arr, spec in ins:
        arrays.append(arr)
        if spec is None:
            in_specs.append(pl.BlockSpec(arr.shape, functools.partial(lambda i, j, nd: (0,) * nd, nd=arr.ndim)))
        else:
            in_specs.append(pl.BlockSpec((ts, spec[0]), colmap(spec[1])))
    out_shape, out_specs = [], []
    for total, dtype, width, col in outs:
        out_shape.append(jax.ShapeDtypeStruct((rows, total), dtype))
        out_specs.append(pl.BlockSpec((ts, width), colmap(col)))
    for shp in accs:
        out_shape.append(jax.ShapeDtypeStruct(shp, F32))
        out_specs.append(pl.BlockSpec(shp, functools.partial(lambda i, j, nd: (0,) * nd, nd=len(shp))))

    def body(*refs):
        vals = [r[...] for r in refs[:n_in]]
        res = fn(*vals)
        if not isinstance(res, (tuple, list)):
            res = (res,)
        for r, v in zip(refs[n_in:n_in + n_out], res[:n_out]):
            r[...] = v.astype(r.dtype)
        if n_acc:
            first = jnp.logical_and(pl.program_id(0) == 0, pl.program_id(1) == 0)
            for r, v in zip(refs[n_in + n_out:], res[n_out:]):
                @pl.when(first)
                def _(r=r):
                    r[...] = jnp.zeros_like(r)
                r[...] += v.astype(F32)

    res = pl.pallas_call(
        body, name=name, grid=(rows // ts, ncol), in_specs=in_specs, out_specs=out_specs, out_shape=out_shape,
        compiler_params=pltpu.CompilerParams(dimension_semantics=("arbitrary", "arbitrary"), vmem_limit_bytes=VMEM_LIMIT),
    )(*arrays)
    return res


MM_OPERAND_BYTES = 24 * 1024 * 1024


def _mm(name, a, b, mode, add=None, after=None):
    a_halves, b_halves = a.ndim == 3, b.ndim == 3
    assert not a_halves or mode == "nt"
    assert not b_halves or mode == "tn"
    if mode == "nn":
        (M, K), N = a.shape, b.shape[1]
    elif mode == "nt":
        M, K, N = a.shape[-2], a.shape[-1] * (2 if a_halves else 1), b.shape[0]
    else:
        (K, M), N = a.shape, b.shape[-1] * (2 if b_halves else 1)
    tm = _pick(M, (512, 1408, 256, 128)) if mode == "tn" else _pick(M, (1024, 512, 256, 128))
    tn = _pick(N // 2 if b_halves else N, (1408, 1024, 512, 256, 128))
    fits = lambda t: 2 * (tm + tn) * t * a.dtype.itemsize <= MM_OPERAND_BYTES
    kdiv = K // 2 if a_halves else K
    tk = next(t for t in (K, 4096, 2816, 2048, 1408, 1024, 512, 256, 128) if kdiv % t == 0 and (fits(t) or t == 128))
    nk = K // tk
    dims = {"nn": NN, "nt": NT, "tn": TN}[mode]
    if a_halves:
        per = kdiv // tk
        a_spec = pl.BlockSpec((None, tm, tk), lambda i, j, k: (k // per, i, k % per))
    else:
        a_spec = pl.BlockSpec((tk, tm), lambda i, j, k: (k, i)) if mode == "tn" else pl.BlockSpec((tm, tk), lambda i, j, k: (i, k))
    if b_halves:
        perj = (N // 2) // tn
        b_spec = pl.BlockSpec((None, tk, tn), lambda i, j, k: (j // perj, k, j % perj))
    else:
        b_spec = pl.BlockSpec((tn, tk), lambda i, j, k: (j, k)) if mode == "nt" else pl.BlockSpec((tk, tn), lambda i, j, k: (k, j))
    o_spec = pl.BlockSpec((tm, tn), lambda i, j, k: (i, j))
    has_add = add is not None

    def body(*refs):
        a_ref, b_ref, o_ref = refs[0], refs[1], refs[-1]
        d = _dot(a_ref[...], b_ref[...], dims)
        first = (d + refs[2][...]) if has_add else d
        if nk == 1:
            o_ref[...] = first
        else:
            k = pl.program_id(2)

            @pl.when(k == 0)
            def _():
                o_ref[...] = first

            @pl.when(k > 0)
            def _():
                o_ref[...] += d

    args = [a, b] + ([add] if has_add else []) + ([] if after is None else [after])
    specs = [a_spec, b_spec] + ([o_spec] if has_add else []) + ([] if after is None else [pl.BlockSpec(memory_space=pl.ANY)])
    return pl.pallas_call(
        body, name=name, grid=(M // tm, N // tn, nk), in_specs=specs, out_specs=o_spec,
        out_shape=jax.ShapeDtypeStruct((M, N), F32),
        compiler_params=pltpu.CompilerParams(dimension_semantics=("parallel", "parallel", "arbitrary"), vmem_limit_bytes=VMEM_LIMIT),
    )(*args)


def _mm_rows(name, a, b, fn, row_ins, whole_ins, outs, accs=(), mode="nn"):
    (M, K), N = a.shape, b.shape[1 if mode == "nn" else 0]
    tm = _pick(M, (512, 256, 128))
    n_in, n_out = 2 + len(row_ins) + len(whole_ins), len(outs)
    windows = [t if isinstance(t, tuple) else (t, (t.shape[1], 0)) for t in row_ins]
    row_ins = [t for t, _ in windows]
    row_specs = [pl.BlockSpec((tm, w), functools.partial(lambda i, col: (i, col), col=col)) for _, (w, col) in windows]

    def body(*refs):
        d = _dot(refs[0][...], refs[1][...], NN if mode == "nn" else NT)
        res = fn(d, *[r[...] for r in refs[2:n_in]])
        for r, v in zip(refs[n_in:n_in + n_out], res[:n_out]):
            r[...] = v.astype(r.dtype)
        for r, v in zip(refs[n_in + n_out:], res[n_out:]):
            @pl.when(pl.program_id(0) == 0)
            def _(r=r):
                r[...] = jnp.zeros_like(r)
            r[...] += v

    row = pl.BlockSpec((tm, N), lambda i: (i, 0))
    whole = lambda t: pl.BlockSpec(t.shape, functools.partial(lambda i, nd: (0,) * nd, nd=t.ndim))
    return pl.pallas_call(
        body, name=name, grid=(M // tm,),
        in_specs=[pl.BlockSpec((tm, K), lambda i: (i, 0)), whole(b)] + row_specs + [whole(t) for t in whole_ins],
        out_specs=[row] * n_out + [pl.BlockSpec(s, functools.partial(lambda i, nd: (0,) * nd, nd=len(s))) for s in accs],
        out_shape=[jax.ShapeDtypeStruct((M, N), dt) for dt in outs] + [jax.ShapeDtypeStruct(s, F32) for s in accs],
        compiler_params=pltpu.CompilerParams(dimension_semantics=("arbitrary",), vmem_limit_bytes=VMEM_LIMIT),
    )(a, b, *row_ins, *whole_ins)


def _ffn_tiles(S):
    return _pick(S, (1024, 512, 256, 128)), _pick(FFN, (1408, 704, 256, 128))


def _gate_up_swiglu(h2, wgu):
    S, K = h2.shape
    tm, tn = _ffn_tiles(S)
    nj = FFN // tn

    def body(a_ref, bg_ref, bu_ref, gu_ref, act_ref):
        a = a_ref[...]
        g, u = _dot(a, bg_ref[...], NN), _dot(a, bu_ref[...], NN)
        gu_ref[0], gu_ref[1] = g.astype(gu_ref.dtype), u.astype(gu_ref.dtype)
        act_ref[...] = _swiglu_fn(g, u).astype(act_ref.dtype)

    return pl.pallas_call(
        body, name="gate_up_swiglu", grid=(S // tm, nj),
        in_specs=[pl.BlockSpec((tm, K), lambda i, j: (i, 0)), pl.BlockSpec((K, tn), lambda i, j: (0, j)),
                  pl.BlockSpec((K, tn), lambda i, j: (0, nj + j))],
        out_specs=[pl.BlockSpec((2, tm, tn), lambda i, j: (0, i, j)), pl.BlockSpec((tm, tn), lambda i, j: (i, j))],
        out_shape=[jax.ShapeDtypeStruct((2, S, FFN), MXU), jax.ShapeDtypeStruct((S, FFN), MXU)],
        compiler_params=pltpu.CompilerParams(dimension_semantics=("parallel", "parallel"), vmem_limit_bytes=VMEM_LIMIT),
    )(h2, wgu, wgu)


def _d_act_swiglu(dx2, wdown, gu):
    S, K = dx2.shape
    tm, tn = _ffn_tiles(S)

    def body(a_ref, b_ref, gu_ref, o_ref):
        dact = _dot(a_ref[...], b_ref[...], NT)
        _, vjp = jax.vjp(_swiglu_fn, gu_ref[0].astype(F32), gu_ref[1].astype(F32))
        dg, du = vjp(dact)
        o_ref[0], o_ref[1] = dg.astype(o_ref.dtype), du.astype(o_ref.dtype)

    stacked = pl.BlockSpec((2, tm, tn), lambda i, j: (0, i, j))
    return pl.pallas_call(
        body, name="d_act_swiglu", grid=(S // tm, FFN // tn),
        in_specs=[pl.BlockSpec((tm, K), lambda i, j: (i, 0)), pl.BlockSpec((tn, K), lambda i, j: (j, 0)), stacked],
        out_specs=stacked, out_shape=jax.ShapeDtypeStruct((2, S, FFN), MXU),
        compiler_params=pltpu.CompilerParams(dimension_semantics=("parallel", "parallel"), vmem_limit_bytes=VMEM_LIMIT),
    )(dx2, wdown, gu)


@jax.custom_vjp
def _swap64(x):
    return pltpu.roll(x, 64, 1)


_swap64.defvjp(lambda x: (_swap64(x), None), lambda _, g: (_swap64(g),))


@jax.custom_vjp
def _mxdot(a, b):
    return _dot(a.astype(MXU), b.astype(MXU), NN)


def _mxdot_bwd(res, g):
    a, b = res
    gb = g.astype(MXU)
    return _dot(gb, b.astype(MXU), NT), _dot(a.astype(MXU), gb, TN)


_mxdot.defvjp(lambda a, b: (_mxdot(a, b), (a, b)), _mxdot_bwd)


def _row_sum(t):
    if t.shape[-1] == LANES:
        return lax.dot_general(t, jnp.ones((LANES, LANES), F32), ((NN), ((), ())), precision=lax.Precision.HIGH,
                               preferred_element_type=F32)
    return jnp.sum(t, axis=-1, keepdims=True)


@functools.partial(jax.custom_vjp, nondiff_argnums=(1,))
def _unit_rms(x, n):
    return x * lax.rsqrt(_row_sum(x * x) * (1.0 / n) + EPS)


def _unit_rms_fwd(x, n):
    r = lax.rsqrt(_row_sum(x * x) * (1.0 / n) + EPS)
    y = x * r
    return y, (y, r)


def _unit_rms_bwd(n, res, g):
    y, r = res
    return (r * (g - y * (_row_sum(g * y) * (1.0 / n))),)


_unit_rms.defvjp(_unit_rms_fwd, _unit_rms_bwd)


def _rms(x):
    return _unit_rms(x, x.shape[-1])


def _rmsg_fn(x, g):
    return _rms(x) * g


def _silu(x):
    return x * jax.nn.sigmoid(x)


def _tables_fn(pos, inv_m, sgn_m, inv_r, sgn_r):
    am, ar = pos * inv_m, pos * inv_r
    return jnp.cos(am), jnp.sin(am) * sgn_m, jnp.cos(ar), jnp.sin(ar) * sgn_r


def _head_blocks(t):
    return [t[:, LANES * h:LANES * (h + 1)] for h in range(t.shape[1] // LANES)]


def _mla_prep_fn(cq, ckv, kr, cosm, sinm, gqa, gkva, gqn, gkn, wq, wk, wv):
    cqn = _rms(cq) * gqa
    ckvn = _rms(ckv) * gkva
    q_raw = _mxdot(cqn, wq)
    k_raw = _mxdot(ckvn, wk)
    lane = lax.broadcasted_iota(jnp.int32, (1, HEADS * LANES), 1)
    v = _mxdot(ckvn, wv) + (lane % LANES == V_M).astype(F32)

    def norm_rope(blocks, g, extra):
        outs = []
        for b in blocks:
            if extra is not None:
                b = b + extra
            n = _unit_rms(b, QK_M) * g
            outs.append(n * cosm + _swap64(n) * sinm)
        return jnp.concatenate(outs, axis=1)

    q = norm_rope(_head_blocks(q_raw), gqn, None)
    k = norm_rope(_head_blocks(k_raw), gkn, kr)
    return q, k, v


def _ret_prep_fn(qr, kr, cosr, sinr):
    def rope(t, scale):
        return jnp.concatenate([(b * cosr + _swap64(b) * sinr) * scale for b in _head_blocks(t)], axis=1)
    return rope(qr, 1.0), rope(kr, RQK ** -0.5)


def _ret_post_fn(rf, rb, gr):
    ret = rf + rb
    outs = []
    for b, g in zip(_head_blocks(ret), _head_blocks(gr)):
        outs.append(_silu(g) * _rms(b))
    return jnp.concatenate(outs, axis=1)


def _merge_fn(ga, gb, ya, yb):
    return jax.nn.sigmoid(ga) * ya + jax.nn.sigmoid(gb) * yb


def _swiglu_fn(gate, up):
    return _silu(gate) * up


def _loss_fn(x2, tgt):
    d = x2 - tgt
    return d * (1.0 / D_MODEL), 0.5 * jnp.sum(d * d, axis=0, keepdims=True) * (1.0 / D_MODEL)


def _adamw_fn(parts, w, m, v):
    g = parts[0].astype(F32)
    for p in range(1, N_DEV):
        g = g + parts[p].astype(F32)
    m2 = B1 * m + (1.0 - B1) * g
    v2 = B2 * v + (1.0 - B2) * jnp.square(g)
    m_hat = m2 / (1.0 - B1 ** STEP)
    v_hat = v2 / (1.0 - B2 ** STEP)
    delta = -LR * (m_hat / (jnp.sqrt(v_hat) + AEPS) + WD * w)
    return g, delta, m2, v2


SCALE = QK_M ** -0.5
LOG2E = 1.4426950408889634
FLASH_ROWS = 32


def _flash_fwd(q, k, v):
    S = q.shape[0]
    tk = _pick(S, (512, 256, 128))
    tq = _pick(S, (1024, 512, 256, 128))
    ncb = tk // LANES
    nkv = S // tk
    assert nkv % 2 == 0, "kv tiles are processed in pairs"
    mrows = 64
    c = SCALE * LOG2E

    def body(q_ref, k_ref, v_ref, o_ref, obf_ref, lse_ref, s_a, p_a, s_b, p_b, m_sc, a_sc, acc_sc):
        m_sc[...] = jnp.full_like(m_sc, -jnp.inf)
        acc_sc[...] = jnp.zeros_like(acc_sc)
        qb = q_ref[...]

        def scores(j, s_buf):
            s_buf[...] = _dot(qb, k_ref[pl.ds(pl.multiple_of(j * tk, tk), tk), :], NT)

        def stage(j, s_buf, p_buf, s_next):
            scores(jnp.minimum(j + 1, nkv - 1), s_next)
            for r in range(tq // mrows):
                rows = slice(r * mrows, (r + 1) * mrows)
                cols = [s_buf[rows, LANES * cb:LANES * (cb + 1)] for cb in range(ncb)]
                m_prev = m_sc[rows, :]
                row_max = jnp.max(functools.reduce(jnp.maximum, cols), axis=-1, keepdims=True)
                m_new = jnp.maximum(m_prev, jnp.broadcast_to(row_max, (mrows, LANES)))
                a_sc[rows, :] = jnp.exp2((m_prev - m_new) * c)
                m_sc[rows, :] = m_new
                for cb in range(ncb):
                    p_buf[rows, LANES * cb:LANES * (cb + 1)] = jnp.exp2((cols[cb] - m_new) * c).astype(p_buf.dtype)
            acc_sc[...] = a_sc[...] * acc_sc[...] + _dot(p_buf[...], v_ref[pl.ds(pl.multiple_of(j * tk, tk), tk), :], NN)

        scores(0, s_a)

        def pair_step(t, carry):
            stage(2 * t, s_a, p_a, s_b)
            stage(2 * t + 1, s_b, p_b, s_a)
            return carry

        lax.fori_loop(0, nkv // 2, pair_step, 0, unroll=4)
        acc = acc_sc[...]
        lane = lax.broadcasted_iota(jnp.int32, (1, LANES), 1)
        l = jnp.sum(jnp.where(lane == V_M, acc, 0.0), axis=-1, keepdims=True)
        o = acc / l
        o_ref[...] = o
        obf_ref[...] = o.astype(obf_ref.dtype)
        lse_ref[...] = m_sc[...] * c + jnp.log2(jnp.broadcast_to(l, (tq, LANES)))

    qspec = pl.BlockSpec((tq, LANES), lambda h, i: (i, h))
    kspec = pl.BlockSpec((S, LANES), lambda h, i: (0, h))
    full = jax.ShapeDtypeStruct((S, HEADS * LANES), F32)
    return pl.pallas_call(
        body, name="flash_fwd", grid=(HEADS, S // tq), in_specs=[qspec, kspec, kspec], out_specs=[qspec, qspec, qspec],
        out_shape=[full, jax.ShapeDtypeStruct((S, HEADS * LANES), MXU), full],
        scratch_shapes=[pltpu.VMEM((tq, tk), F32), pltpu.VMEM((tq, tk), MXU)] * 2 + [pltpu.VMEM((tq, LANES), F32)] * 3,
        compiler_params=pltpu.CompilerParams(dimension_semantics=("parallel", "arbitrary"), vmem_limit_bytes=VMEM_LIMIT),
    )(q, k, v)


def _delta_fn(o, do):
    outs = [jnp.broadcast_to(jnp.sum(a * b, axis=-1, keepdims=True), a.shape) for a, b in zip(_head_blocks(o), _head_blocks(do))]
    return do, jnp.concatenate(outs, axis=1)


def _flash_bwd(q, k, v, do, lse, delta):
    S = q.shape[0]
    tq = tk = _pick(S, (512, 256, 128))
    ncb = tk // LANES
    c = SCALE * LOG2E

    nq = S // tq
    assert nq % 2 == 0, "q tiles are processed in pairs"

    def body(q_ref, k_ref, v_ref, do_ref, lse_ref, dl_ref, dq_ref, dk_ref, dv_ref, s_a, dp_a, p_a, ds_a, s_b, dp_b, p_b, ds_b, dk_sc, dv_sc):
        @pl.when(pl.program_id(1) == 0)
        def _():
            dq_ref[...] = jnp.zeros_like(dq_ref)

        dk_sc[...] = jnp.zeros_like(dk_sc)
        dv_sc[...] = jnp.zeros_like(dv_sc)
        kb, vb = k_ref[...], v_ref[...]

        def scores(i, s_buf, dp_buf):
            q_rows = pl.ds(pl.multiple_of(i * tq, tq), tq)
            s_buf[...] = _dot(q_ref[q_rows, :], kb, NT)
            dp_buf[...] = _dot(do_ref[q_rows, :], vb, NT)

        def stage(i, s_buf, dp_buf, p_buf, ds_buf, s_next, dp_next):
            scores(jnp.minimum(i + 1, nq - 1), s_next, dp_next)
            for r in range(tq // FLASH_ROWS):
                rows = slice(r * FLASH_ROWS, (r + 1) * FLASH_ROWS)
                grows = pl.ds(pl.multiple_of(i * tq + r * FLASH_ROWS, FLASH_ROWS), FLASH_ROWS)
                lse_b, dl_b = lse_ref[grows, :], dl_ref[grows, :]
                for cb in range(ncb):
                    sl = slice(LANES * cb, LANES * (cb + 1))
                    p = jnp.exp2(s_buf[rows, sl] * c - lse_b)
                    p_buf[rows, sl] = p.astype(p_buf.dtype)
                    ds_buf[rows, sl] = (p * (dp_buf[rows, sl] - dl_b) * SCALE).astype(ds_buf.dtype)
            q_rows = pl.ds(pl.multiple_of(i * tq, tq), tq)
            dv_sc[...] += _dot(p_buf[...], do_ref[q_rows, :], TN)
            dk_sc[...] += _dot(ds_buf[...], q_ref[q_rows, :], TN)
            dq_ref[q_rows, :] += _dot(ds_buf[...], kb, NN)

        scores(0, s_a, dp_a)

        def pair_step(t, carry):
            stage(2 * t, s_a, dp_a, p_a, ds_a, s_b, dp_b)
            stage(2 * t + 1, s_b, dp_b, p_b, ds_b, s_a, dp_a)
            return carry

        lax.fori_loop(0, nq // 2, pair_step, 0, unroll=2)
        dk_ref[...] = dk_sc[...]
        dv_ref[...] = dv_sc[...]

    hspec = pl.BlockSpec((S, LANES), lambda h, j: (0, h))
    kspec = pl.BlockSpec((tk, LANES), lambda h, j: (j, h))
    full = jax.ShapeDtypeStruct((S, HEADS * LANES), F32)
    tile_bufs = [pltpu.VMEM((tq, tk), F32), pltpu.VMEM((tq, tk), F32), pltpu.VMEM((tq, tk), MXU), pltpu.VMEM((tq, tk), MXU)]
    return pl.pallas_call(
        body, name="flash_bwd", grid=(HEADS, S // tk), in_specs=[hspec, kspec, kspec, hspec, hspec, hspec],
        out_specs=[hspec, kspec, kspec], out_shape=[full, full, full],
        scratch_shapes=tile_bufs + tile_bufs + [pltpu.VMEM((tk, LANES), F32), pltpu.VMEM((tk, LANES), F32)],
        compiler_params=pltpu.CompilerParams(dimension_semantics=("parallel", "arbitrary"), vmem_limit_bytes=VMEM_LIMIT),
    )(q, k, v, do, lse, delta)


def _ret_consts(lgh, head, rev):
    C = CHUNK
    lane = lax.broadcasted_iota(jnp.int32, (1, LANES), 1)
    hm = ((lane // 32) % 2 == head % 2).astype(F32)
    r = lax.broadcasted_iota(jnp.int32, (C, C), 0)
    c = lax.broadcasted_iota(jnp.int32, (C, C), 1)
    diff = ((c - r) if rev else (r - c)).astype(F32)
    mask = (diff > 0) if rev else (diff >= 0)
    dpos = jnp.maximum(diff, 0.0)
    din = jnp.where(mask, jnp.exp(lgh * dpos), 0.0)
    idx = lax.broadcasted_iota(jnp.int32, (C, 1), 0).astype(F32)
    eq = (C - idx) if rev else (idx + 1.0)
    ek = idx if rev else (C - 1.0 - idx)
    qd, kd = jnp.exp(lgh * eq), jnp.exp(lgh * ek)
    cd = jnp.exp(lgh * jnp.full((1, 1), float(C), F32))
    return hm, din, dpos, qd, kd, cd, eq, ek


RET_HEADS_PER_STEP = 4


def _ret_fwd(name, qt, kt, proj, lg, rev):
    S = qt.shape[0]
    C = CHUNK
    TB = _pick(S, (512, 256, 128))
    cb, nb = TB // C, S // TB
    hps = RET_HEADS_PER_STEP
    blk = (lambda g: nb - 1 - g) if rev else (lambda g: g)

    def body(lg_ref, q_ref, k_ref, v_ref, o_ref, st_ref, state_sc):
        hg, g = pl.program_id(0), pl.program_id(1)

        @pl.when(g == 0)
        def _():
            state_sc[...] = jnp.zeros_like(state_sc)

        consts = [_ret_consts(lg_ref[hg * hps + u], u, rev) for u in range(hps)]
        order = list(reversed(range(cb))) if rev else list(range(cb))
        units = [(cc, u) for cc in order for u in range(hps)]

        def operands(cc, u):
            rows = pl.ds(cc * C, C)
            pair = slice(LANES * (u // 2), LANES * (u // 2 + 1))
            hm = consts[u][0]
            return q_ref[rows, pair] * hm, k_ref[rows, pair] * hm, v_ref[rows, LANES * u:LANES * (u + 1)].astype(MXU)

        a, inc = {}, {}
        for cc, u in units:
            q, k, v = operands(cc, u)
            a[cc, u] = _dot(q.astype(MXU), k.astype(MXU), NT) * consts[u][1]
            inc[cc, u] = _dot((k * consts[u][4]).astype(MXU), v, TN)
        for u in range(hps):
            st = state_sc[u]
            for cc in order:
                st_ref[u, cc] = st
                st = st * consts[u][5] + inc[cc, u]
            state_sc[u] = st
        for cc, u in units:
            q, _, v = operands(cc, u)
            cross = _dot((q * consts[u][3]).astype(MXU), st_ref[u, cc].astype(MXU), NN)
            o_ref[pl.ds(cc * C, C), LANES * u:LANES * (u + 1)] = _dot(a[cc, u].astype(MXU), v, NN) + cross

    qk_spec = pl.BlockSpec((TB, LANES * hps // 2), lambda h, g: (blk(g), h))
    return pl.pallas_call(
        body, name=name, grid=(HEADS // hps, nb),
        in_specs=[pl.BlockSpec(memory_space=pltpu.SMEM), qk_spec, qk_spec,
                  pl.BlockSpec((TB, LANES * hps), lambda h, g: (blk(g), P_VR // (LANES * hps) + h))],
        out_specs=[pl.BlockSpec((TB, LANES * hps), lambda h, g: (blk(g), h)),
                   pl.BlockSpec((hps, cb, LANES, LANES), lambda h, g: (h, blk(g), 0, 0))],
        out_shape=[jax.ShapeDtypeStruct((S, HEADS * LANES), F32), jax.ShapeDtypeStruct((HEADS, S // C, LANES, LANES), F32)],
        scratch_shapes=[pltpu.VMEM((hps, LANES, LANES), F32)],
        compiler_params=pltpu.CompilerParams(dimension_semantics=("parallel", "arbitrary"), vmem_limit_bytes=VMEM_LIMIT),
    )(lg, qt, kt, proj)


def _ret_bwd(name, qt, kt, proj, dret, states, lg, rev):
    S = qt.shape[0]
    C = CHUNK
    TB = _pick(S, (512, 256, 128))
    cb, nb = TB // C, S // TB
    hps = RET_HEADS_PER_STEP
    blk = (lambda g: g) if rev else (lambda g: nb - 1 - g)

    def body(lg_ref, q_ref, k_ref, v_ref, do_ref, st_ref, dq_ref, dk_ref, dv_ref, dlg_ref, ds_sc, acc_cc, acc_q, acc_k, acc_s):
        hg, g = pl.program_id(0), pl.program_id(1)

        @pl.when(g == 0)
        def _():
            ds_sc[...] = jnp.zeros_like(ds_sc)
            acc_cc[...] = jnp.zeros_like(acc_cc)
            acc_q[...] = jnp.zeros_like(acc_q)
            acc_k[...] = jnp.zeros_like(acc_k)
            acc_s[...] = jnp.zeros_like(acc_s)

        lgs = [lg_ref[hg * hps + u] for u in range(hps)]
        consts = [_ret_consts(lgs[u], u, rev) for u in range(hps)]
        order = list(range(cb)) if rev else list(reversed(range(cb)))
        units = [(cc, u) for cc in order for u in range(hps)]

        def operands(cc, u):
            rows = pl.ds(cc * C, C)
            pair = slice(LANES * (u // 2), LANES * (u // 2 + 1))
            head = slice(LANES * u, LANES * (u + 1))
            hm = consts[u][0]
            return q_ref[rows, pair] * hm, k_ref[rows, pair] * hm, v_ref[rows, head].astype(MXU), do_ref[rows, head].astype(MXU)

        a, dp, dqs, inc = {}, {}, {}, {}
        for cc, u in units:
            q, k, vb, dob = operands(cc, u)
            a[cc, u] = _dot(q.astype(MXU), k.astype(MXU), NT)
            dp[cc, u] = _dot(dob, vb, NT)
            dqs[cc, u] = _dot(dob, st_ref[u, cc].astype(MXU), NT)
            inc[cc, u] = _dot((q * consts[u][3]).astype(MXU), dob, TN)
        dsn = {}
        for u in range(hps):
            ds = ds_sc[u]
            for cc in order:
                dsn[cc, u] = ds
                ds = ds * consts[u][5] + inc[cc, u]
            ds_sc[u] = ds
        even = {}
        for cc, u in units:
            hm, din, dpos, qd, kd, cd, eq, ek = consts[u]
            rows, head = pl.ds(cc * C, C), slice(LANES * u, LANES * (u + 1))
            q, k, vb, dob = operands(cc, u)
            qb, kb = q.astype(MXU), k.astype(MXU)
            dsnb = dsn[cc, u].astype(MXU)
            da = (dp[cc, u] * din).astype(MXU)
            vds = _dot(vb, dsnb, NT)
            dq_u = (_dot(da, kb, NN) + dqs[cc, u] * qd) * hm
            dk_u = (_dot(da, qb, TN) + vds * kd) * hm
            if u % 2 == 0:
                even[cc] = (dq_u, dk_u)
            else:
                pair = slice(LANES * (u // 2), LANES * (u // 2 + 1))
                dq_ref[rows, pair] = even[cc][0] + dq_u
                dk_ref[rows, pair] = even[cc][1] + dk_u
            dv_ref[rows, head] = _dot((a[cc, u] * din).astype(MXU), dob, TN) + _dot((k * kd).astype(MXU), dsnb, NN)
            acc_cc[u] += dp[cc, u] * a[cc, u] * din * dpos
            acc_q[u] += dqs[cc, u] * q * (qd * eq)
            acc_k[u] += vds * k * (kd * ek)
            acc_s[u] += dsn[cc, u] * st_ref[u, cc] * (cd * float(C))

        @pl.when(g == nb - 1)
        def _():
            for u in range(hps):
                tot = (jnp.sum(acc_cc[u], keepdims=True) + jnp.sum(acc_q[u], keepdims=True)
                       + jnp.sum(acc_k[u], keepdims=True) + jnp.sum(acc_s[u], keepdims=True))
                dlg_ref[u] = jnp.broadcast_to(tot * lgs[u], (8, LANES))

    full = jax.ShapeDtypeStruct((S, HEADS * LANES), F32)
    hspec = pl.BlockSpec((TB, LANES * hps), lambda h, g: (blk(g), h))
    qk_spec = pl.BlockSpec((TB, LANES * hps // 2), lambda h, g: (blk(g), h))
    return pl.pallas_call(
        body, name=name, grid=(HEADS // hps, nb),
        in_specs=[pl.BlockSpec(memory_space=pltpu.SMEM), qk_spec, qk_spec,
                  pl.BlockSpec((TB, LANES * hps), lambda h, g: (blk(g), P_VR // (LANES * hps) + h)),
                  hspec,
                  pl.BlockSpec((hps, cb, LANES, LANES), lambda h, g: (h, blk(g), 0, 0))],
        out_specs=[qk_spec, qk_spec, hspec, pl.BlockSpec((hps, 8, LANES), lambda h, g: (h, 0, 0))],
        out_shape=[jax.ShapeDtypeStruct(qt.shape, F32), jax.ShapeDtypeStruct(kt.shape, F32), full,
                   jax.ShapeDtypeStruct((HEADS, 8, LANES), F32)],
        scratch_shapes=[pltpu.VMEM((hps, LANES, LANES), F32), pltpu.VMEM((hps, C, C), F32), pltpu.VMEM((hps, C, LANES), F32),
                        pltpu.VMEM((hps, C, LANES), F32), pltpu.VMEM((hps, LANES, LANES), F32)],
        compiler_params=pltpu.CompilerParams(dimension_semantics=("parallel", "arbitrary"), vmem_limit_bytes=VMEM_LIMIT),
    )(lg, qt, kt, proj, dret, states)


def _rope_consts():
    inv16 = THETA ** (-jnp.arange(16, dtype=F32) / 16)
    inv32 = THETA ** (-jnp.arange(32, dtype=F32) / 32)
    lane = np.arange(LANES)
    z48 = jnp.zeros((48,), F32)
    inv_m = jnp.concatenate([inv16, z48, inv16, z48])[None, :]
    sgn_m = jnp.asarray(np.where(lane < 16, -1.0, np.where((lane >= 64) & (lane < 80), 1.0, 0.0)), F32)[None, :]
    inv_r = jnp.concatenate([inv32] * 4)[None, :]
    sgn_r = jnp.asarray(np.where(lane < 64, -1.0, 1.0), F32)[None, :]
    return inv_m, sgn_m, inv_r, sgn_r


FIRST_WEIGHTS = ("w_in", "w_q_b", "w_kv_b")
EARLY_GRADS = ("w_down", "w_gate_up", "w_out", "w_ret_out")
MID_GRADS = ("w_mla_out", "w_in")


def _local_step(x, pos, tgt, gains, W, late_weights=None, grad_hook=None, start_after=None):
    S = x.shape[0]
    ts = _pick(S, (256, 128))
    R = lambda a, w=None, c=0: (a, ((a.shape[1] if w is None else w), c))
    W_ = lambda a: (a, None)

    win = _win_pad(W["w_in"])
    wq = _wq_pad(W["w_q_b"])
    wk, wv = _wkv_pad(W["w_kv_b"])
    gqn, gkn = _qk_pad(gains["g_qn"]), _qk_pad(gains["g_kn"])
    g_mix, g_q_a, g_kv_a, g_ffn = gains["g_mix"], gains["g_q_a"], gains["g_kv_a"], gains["g_ffn"]
    lg_f = -jnp.exp(gains["ret_decay_fwd"][0])
    lg_b = -jnp.exp(gains["ret_decay_bwd"][0])

    consts = list(_rope_consts())
    cosm, sinm, cosr, sinr = _rowwise("rope_tables", _tables_fn, S, ts, [R(pos)] + [W_(c) for c in consts],
                                      [(LANES, F32, LANES, 0)] * 4)

    (h,) = _rowwise("rms_mix", _rmsg_fn, S, ts, [R(x), W_(g_mix)], [(D_MODEL, MXU, D_MODEL, 0)])
    proj = _mm("in_proj", h, win, "nn", after=start_after)
    seg = lambda off, w: (proj, (w, off // w))
    mla_ins = [seg(P_CQ, 256), seg(P_CKV, 128), seg(P_KROPE, 128), R(cosm), R(sinm),
               W_(g_q_a), W_(g_kv_a), W_(gqn), W_(gkn), W_(wq), W_(wk), W_(wv)]
    q, k, v = _rowwise("mla_prep", _mla_prep_fn, S, ts, mla_ins, [(HEADS * LANES, MXU, HEADS * LANES, 0)] * 3)
    o, o_bf, lse = _flash_fwd(q, k, v)
    if late_weights is not None:
        W = {**W, **late_weights(lse)}
    wmla = _wmla_pad(W["w_mla_out"])
    wret, wout, wgu, wdown = W["w_ret_out"], W["w_out"], W["w_gate_up"], W["w_down"]
    y_a = _mm("mla_out", o_bf, wmla, "nn")

    ret_ins = [seg(P_QR, 512), seg(P_KR, 512), R(cosr), R(sinr)]
    qt, kt = _rowwise("ret_prep", _ret_prep_fn, S, ts, ret_ins, [(512, F32, 512, 0)] * 2)
    ret_f, st_f = _ret_fwd("ret_fwd_f", qt, kt, proj, lg_f, False)
    ret_b, st_b = _ret_fwd("ret_fwd_b", qt, kt, proj, lg_b, True)
    post_ins = [R(ret_f), R(ret_b), seg(P_GR, 1024)]
    (o_b,) = _rowwise("ret_post", _ret_post_fn, S, ts, post_ins, [(1024, MXU, 1024, 0)])
    y_b, merged = _mm_rows("ret_out_merge", o_b, wret, lambda yb, ga, gb, ya: (yb, _merge_fn(ga, gb, ya, yb)),
                           [seg(P_GATES, 1024), (proj, (1024, 1)), R(y_a)], [], [F32, MXU])
    merge_ins = [seg(P_GATES, 1024), (proj, (1024, 1)), R(y_a), R(y_b)]
    def residual_rms(d, xx, g):
        r = d + xx
        return r, _rmsg_fn(r, g)

    x1, h2 = _mm_rows("out_proj_rms_ffn", merged, wout, residual_rms, [x], [g_ffn], [F32, MXU])
    gu, act = _gate_up_swiglu(h2, wgu)

    def residual_loss(d, xx, t):
        dx, rows = _loss_fn(d + xx, t)
        return dx, dx, rows

    dx2, dx2_bf, loss_rows = _mm_rows("down_proj_loss", act, wdown, residual_loss, [x1, tgt], [], [F32, MXU], accs=[(1, D_MODEL)])

    gW = {}
    gW["w_down"] = _mm("d_w_down", act, dx2_bf, "tn")
    dgu = _d_act_swiglu(dx2_bf, wdown, gu)
    gW["w_gate_up"] = _mm("d_w_gate_up", h2, dgu, "tn")
    dh2 = _mm("d_h2", dgu, wgu, "nt")

    def rms_bwd(xx, g, dh, dres):
        _, vjp = jax.vjp(_rmsg_fn, xx, g)
        dx, dg = vjp(dh)
        dx = dx + dres
        return dx, dx, dg

    dx1, dx1_bf, dg_ffn = _rowwise("rms_ffn_bwd", rms_bwd, S, ts, [R(x1), W_(g_ffn), R(dh2), R(dx2)],
                                   [(D_MODEL, F32, D_MODEL, 0), (D_MODEL, MXU, D_MODEL, 0)], accs=[(1, D_MODEL)])
    gW["w_out"] = _mm("d_w_out", merged, dx1_bf, "tn")
    def merge_bwd(dm, ga, gb, ya, yb):
        _, vjp = jax.vjp(_merge_fn, ga, gb, ya, yb)
        return vjp(dm)

    dga, dgb, dy_a, dy_b = _mm_rows("d_merged_merge_bwd", dx1_bf, wout, merge_bwd, merge_ins, [], [MXU] * 4, mode="nt")
    gW["w_ret_out"] = _mm("d_w_ret_out", o_b, dy_b, "tn")
    after_early = [] if grad_hook is None else [grad_hook({n: gW[n] for n in EARLY_GRADS})]

    def post_bwd(dob, rf, rb, gr, *_):
        _, vjp = jax.vjp(_ret_post_fn, rf, rb, gr)
        drf, _, dgr = vjp(dob)
        return drf, dgr

    dret, dg_r = _mm_rows("d_o_b_ret_post_bwd", dy_b, wret, post_bwd, post_ins, after_early, [F32, MXU], mode="nt")
    dq_f, dk_f, dv_f, dlg_f = _ret_bwd("ret_bwd_f", qt, kt, proj, dret, st_f, lg_f, False)
    dq_b, dk_b, dv_b, dlg_b = _ret_bwd("ret_bwd_b", qt, kt, proj, dret, st_b, lg_b, True)

    def ret_prep_bwd(qr, kr, cosr_, sinr_, dqf, dqb, dkf, dkb, dvf, dvb):
        _, vjp = jax.vjp(lambda a, b: _ret_prep_fn(a, b, cosr_, sinr_), qr, kr)
        dqr, dkr = vjp((dqf + dqb, dkf + dkb))
        return dqr, dkr, dvf + dvb

    dq_r, dk_r, dv_r = _rowwise("ret_prep_bwd", ret_prep_bwd, S, ts, ret_ins + [R(t) for t in (dq_f, dq_b, dk_f, dk_b, dv_f, dv_b)],
                                [(512, MXU, 512, 0), (512, MXU, 512, 0), (1024, MXU, 1024, 0)])

    gW_mla_p = _mm("d_w_mla_out", o_bf, dy_a, "tn")
    do_bf, delta = _mm_rows("d_o_attn_delta", dy_a, wmla, lambda d, oo, *_: _delta_fn(oo, d), [o], after_early, [MXU, F32], mode="nt")
    dq, dk, dv = _flash_bwd(q, k, v, do_bf, lse, delta)

    def mla_prep_bwd(cq, ckv, kr, cosm_, sinm_, gqa, gkva, gqn_, gkn_, wq_, wk_, wv_, dq_, dk_, dv_):
        f = lambda cq, ckv, kr, gqa, gkva, gqn_, gkn_, wq_, wk_, wv_: _mla_prep_fn(cq, ckv, kr, cosm_, sinm_, gqa, gkva, gqn_, gkn_, wq_, wk_, wv_)
        _, vjp = jax.vjp(f, cq, ckv, kr, gqa, gkva, gqn_, gkn_, wq_.astype(F32), wk_.astype(F32), wv_.astype(F32))
        return vjp((dq_, dk_, dv_))

    mb = _rowwise("mla_prep_bwd", mla_prep_bwd, S, ts, mla_ins + [R(dq), R(dk), R(dv)],
                  [(256, MXU, 256, 0), (128, MXU, 128, 0), (128, MXU, 128, 0)],
                  accs=[(1, 256), (1, 128), (1, LANES), (1, LANES), (256, HEADS * LANES), (128, HEADS * LANES), (128, HEADS * LANES)])
    dc_q, dc_kv, dk_rope, dg_q_a, dg_kv_a, dgqn_p, dgkn_p, dwq_p, dwk_p, dwv_p = mb

    dproj = jnp.concatenate([dga, dgb, dv_r, dg_r, dq_r, dk_r, dc_q, dc_kv, dk_rope], axis=1)
    gW["w_in"] = _win_unpad(_mm("d_w_in", h, dproj, "tn"))
    gW["w_mla_out"] = _wmla_unpad(gW_mla_p)
    after_mid = None if grad_hook is None else grad_hook({n: gW[n] for n in MID_GRADS})
    dh = _mm("d_h", dproj, win, "nt", after=after_mid)
    grad_x, _, dg_mix = _rowwise("rms_mix_bwd", lambda a, b, c, d, *_: rms_bwd(a, b, c, d), S, ts,
                                 [R(x), W_(g_mix), R(dh), R(dx1)] + ([] if after_mid is None else [W_(after_mid)]),
                                 [(D_MODEL, F32, D_MODEL, 0), (D_MODEL, MXU, D_MODEL, 0)], accs=[(1, D_MODEL)])
    gW["w_q_b"] = _wq_unpad(dwq_p)
    gW["w_kv_b"] = _wkv_unpad(dwk_p, dwv_p)
    gG = {"g_mix": dg_mix, "g_q_a": dg_q_a, "g_kv_a": dg_kv_a, "g_qn": _qk_unpad(dgqn_p),
          "g_kn": _qk_unpad(dgkn_p), "ret_decay_fwd": dlg_f[:, 0, 0][None, :], "ret_decay_bwd": dlg_b[:, 0, 0][None, :],
          "g_ffn": dg_ffn}
    return loss_rows, grad_x, gG, gW


MATS = [("w_in", (1024, 5536), 1), ("w_q_b", (256, 768), 1), ("w_kv_b", (128, 1024), 1), ("w_mla_out", (512, 1024), 1),
        ("w_ret_out", (1024, 1024), 0), ("w_out", (1024, 1024), 0), ("w_gate_up", (1024, 5632), 1), ("w_down", (2816, 1024), 0)]
GAINS = [("g_mix", 1024), ("g_q_a", 256), ("g_kv_a", 128), ("g_qn", 96), ("g_kn", 96), ("ret_decay_fwd", 8), ("ret_decay_bwd", 8),
         ("g_ffn", 1024)]
ORDER = ["g_mix", "w_in", "g_q_a", "w_q_b", "g_kv_a", "w_kv_b", "g_qn", "g_kn", "w_mla_out", "ret_decay_fwd", "ret_decay_bwd",
         "w_ret_out", "w_out", "g_ffn", "w_gate_up", "w_down"]
GAIN_LEN = sum(n for _, n in GAINS)
GAIN_PAD = -(-GAIN_LEN // LANES) * LANES


def _pack_gains(d):
    row = jnp.concatenate([d[n].reshape(1, ln).astype(F32) for n, ln in GAINS], axis=1)
    return jnp.pad(row, ((0, 0), (0, GAIN_PAD - GAIN_LEN)))


def _unpack_gains(row):
    out, off = {}, 0
    for n, ln in GAINS:
        out[n] = row[0, off:off + ln]
        off += ln
    return out


def _unshard(pieces, axis):
    if axis == 0:
        return pieces.reshape((N_DEV * pieces.shape[1], pieces.shape[2]))
    return jnp.concatenate([pieces[p] for p in range(N_DEV)], axis=1)


def _reshard(full, axis):
    if axis == 0:
        return full.reshape((N_DEV, full.shape[0] // N_DEV, full.shape[1]))
    c = full.shape[1] // N_DEV
    return jnp.stack([full[:, c * p:c * (p + 1)] for p in range(N_DEV)])


def _all_gather(shards):
    n = len(shards)

    def body(*refs):
        x_refs, out_refs = refs[:n], refs[n:2 * n]
        send_sems, recv_sems, local_sems = refs[2 * n:]
        x, y, c = lax.axis_index("x"), lax.axis_index("y"), lax.axis_index("c")
        me, sibling = (x, y, c), (x, y, 1 - c)
        chips = [(1 - x, y), (x, 1 - y), (1 - x, 1 - y)]

        def slot(a, px, py, pc):
            return out_refs[a].at[4 * px + 2 * py + pc]

        def copy(a, k, block, to, from_input=False):
            return pltpu.make_async_remote_copy(
                src_ref=x_refs[a] if from_input else slot(a, *block), dst_ref=slot(a, *block),
                send_sem=send_sems.at[a, k], recv_sem=recv_sems.at[a, k], device_id=to, device_id_type=pl.DeviceIdType.MESH)

        mine = [pltpu.make_async_copy(x_refs[a], slot(a, *me), local_sems.at[a]) for a in range(n)]
        first = [copy(a, 0, me, sibling, True) for a in range(n)]
        first += [copy(a, 1 + j, me, (*chip, c), True) for j, chip in enumerate(chips) for a in range(n)]
        for cp in mine + first:
            cp.start()
        passed = []
        for j, chip in enumerate(chips):
            for a in range(n):
                copy(a, 1 + j, (*chip, c), me).wait_recv()
                passed.append(copy(a, 4 + j, (*chip, c), sibling))
                passed[-1].start()
        for a in range(n):
            copy(a, 0, sibling, me).wait_recv()
        for j, chip in enumerate(chips):
            for a in range(n):
                copy(a, 4 + j, (*chip, 1 - c), me).wait_recv()
        for cp in first + passed:
            cp.wait_send()
        for cp in mine:
            cp.wait()

    any_spec = pl.BlockSpec(memory_space=pl.ANY)
    return pl.pallas_call(
        body, name="all_gather_weights", out_shape=[jax.ShapeDtypeStruct((N_DEV,) + s.shape, s.dtype) for s in shards],
        in_specs=[any_spec] * n, out_specs=[any_spec] * n,
        scratch_shapes=[pltpu.SemaphoreType.DMA((n, 7)), pltpu.SemaphoreType.DMA((n, 7)), pltpu.SemaphoreType.DMA((n,))],
    )(*shards)


def _all_to_all(pieces):
    n = len(pieces)

    def body(*refs):
        in_refs, out_refs = refs[:n], refs[n:2 * n]
        send_sems, recv_sems, local_sems = refs[2 * n:]
        x, y, c = lax.axis_index("x"), lax.axis_index("y"), lax.axis_index("c")
        my_id = 4 * x + 2 * y + c
        flips = [(fx, fy, fc) for fx in (0, 1) for fy in (0, 1) for fc in (0, 1)][1:]

        def copy(a, kk, f):
            p = (x ^ f[0], y ^ f[1], c ^ f[2])
            return pltpu.make_async_remote_copy(
                src_ref=in_refs[a].at[4 * p[0] + 2 * p[1] + p[2]], dst_ref=out_refs[a].at[my_id],
                send_sem=send_sems.at[a, kk], recv_sem=recv_sems.at[a, kk], device_id=p, device_id_type=pl.DeviceIdType.MESH)

        mine = [pltpu.make_async_copy(in_refs[a].at[my_id], out_refs[a].at[my_id], local_sems.at[a]) for a in range(n)]
        copies = [copy(a, kk, f) for kk, f in enumerate(flips) for a in range(n)]
        for cp in mine + copies:
            cp.start()
        for cp in copies:
            cp.wait_recv()
        for cp in copies:
            cp.wait_send()
        for cp in mine:
            cp.wait()

    any_spec = pl.BlockSpec(memory_space=pl.ANY)
    return pl.pallas_call(
        body, name="all_to_all_grads", out_shape=[jax.ShapeDtypeStruct(p.shape, p.dtype) for p in pieces],
        in_specs=[any_spec] * n, out_specs=[any_spec] * n,
        scratch_shapes=[pltpu.SemaphoreType.DMA((n, 7)), pltpu.SemaphoreType.DMA((n, 7)), pltpu.SemaphoreType.DMA((n,))],
    )(*pieces)


def _flip_peers(x, y, c):
    flips = [(fx, fy, fc) for fx in (0, 1) for fy in (0, 1) for fc in (0, 1)][1:]
    return [(x ^ fx, y ^ fy, c ^ fc) for fx, fy, fc in flips]


def _split_copies(in_refs, land_refs, send_sems, recv_sems, gather):
    x, y, c = lax.axis_index("x"), lax.axis_index("y"), lax.axis_index("c")
    my_id = 4 * x + 2 * y + c
    copies = []
    for kk, p in enumerate(_flip_peers(x, y, c)):
        for a in range(len(in_refs)):
            src = in_refs[a] if gather else in_refs[a].at[4 * p[0] + 2 * p[1] + p[2]]
            copies.append(pltpu.make_async_remote_copy(
                src_ref=src, dst_ref=land_refs[a].at[my_id], send_sem=send_sems.at[a * 7 + kk], recv_sem=recv_sems.at[a * 7 + kk],
                device_id=p, device_id_type=pl.DeviceIdType.MESH))
    return copies


def _exchange_start(name, srcs, gather, after=None):
    n = len(srcs)
    first_out = 2 * n + (0 if after is None else 1)

    def body(*refs):
        for cp in _split_copies(refs[:n], refs[n:2 * n], refs[first_out], refs[first_out + 1], gather):
            cp.start()
        refs[-1][...] = jnp.zeros_like(refs[-1])

    hbm, sem = pl.BlockSpec(memory_space=pltpu.HBM), pl.BlockSpec(memory_space=pltpu.SEMAPHORE)
    land_shapes = [((N_DEV,) + s.shape if gather else s.shape, s.dtype) for s in srcs]
    lands = [pltpu.with_memory_space_constraint(lax.empty(shp, dt), pltpu.HBM) for shp, dt in land_shapes]
    srcs = [pltpu.with_memory_space_constraint(s, pltpu.HBM) for s in srcs]
    res = pl.pallas_call(
        body, name=name,
        out_shape=[pltpu.SemaphoreType.DMA((7 * n,)), pltpu.SemaphoreType.DMA((7 * n,))] + [pltpu.HBM(s.shape, s.dtype) for s in srcs]
        + [pltpu.HBM(shp, dt) for shp, dt in land_shapes] + [jax.ShapeDtypeStruct((8, LANES), F32)],
        in_specs=[hbm] * (2 * n) + ([] if after is None else [pl.BlockSpec(memory_space=pl.ANY)]),
        out_specs=[sem, sem] + [hbm] * (2 * n) + [pl.BlockSpec(memory_space=pltpu.VMEM)],
        input_output_aliases={i: 2 + i for i in range(2 * n)},
        compiler_params=pltpu.CompilerParams(has_side_effects=pltpu.SideEffectType.DATAFLOW_SIDE_EFFECTING),
    )(*srcs, *lands, *([] if after is None else [after]))
    return res[0], res[1], res[2:2 + n], res[2 + n:2 + 2 * n], res[-1]


def _exchange_wait(name, handles, after, gather):
    send_sems, recv_sems, srcs, lands, _ = handles
    n = len(srcs)

    def body(*refs):
        for cp in _split_copies(refs[:n], refs[n:2 * n], refs[2 * n], refs[2 * n + 1], gather):
            cp.wait_send()
            cp.wait_recv()

    hbm, sem = pl.BlockSpec(memory_space=pltpu.HBM), pl.BlockSpec(memory_space=pltpu.SEMAPHORE)
    res = pl.pallas_call(
        body, name=name, out_shape=[pltpu.HBM(t.shape, t.dtype) for t in list(srcs) + list(lands)],
        in_specs=[hbm] * (2 * n) + [sem, sem, pl.BlockSpec(memory_space=pl.ANY)], out_specs=[hbm] * (2 * n),
        input_output_aliases={i: i for i in range(2 * n)},
        compiler_params=pltpu.CompilerParams(has_side_effects=pltpu.SideEffectType.DATAFLOW_SIDE_EFFECTING),
    )(*srcs, *lands, send_sems, recv_sems, after)
    my_id = 4 * lax.axis_index("x") + 2 * lax.axis_index("y") + lax.axis_index("c")
    own = [s if gather else lax.dynamic_index_in_dim(s, my_id, 0, keepdims=False) for s in res[:n]]
    return [lax.dynamic_update_index_in_dim(land, o, my_id, 0) for land, o in zip(res[n:], own)]


def _adamw(name, parts, w, m, v):
    rows, cols = w.shape
    tr = _pick(rows, (128, 64, 32, 16, 8))
    pspec = pl.BlockSpec((N_DEV, tr, cols), lambda i: (0, i, 0))
    rspec = pl.BlockSpec((tr, cols), lambda i: (i, 0))

    def body(p_ref, w_ref, m_ref, v_ref, g_ref, d_ref, m2_ref, v2_ref):
        g, d, m2, v2 = _adamw_fn([p_ref[s] for s in range(N_DEV)], w_ref[...], m_ref[...], v_ref[...])
        g_ref[...], d_ref[...], m2_ref[...], v2_ref[...] = g, d, m2, v2

    return pl.pallas_call(
        body, name=name, grid=(rows // tr,), in_specs=[pspec, rspec, rspec, rspec], out_specs=[rspec] * 4,
        out_shape=[jax.ShapeDtypeStruct((rows, cols), F32)] * 4,
        compiler_params=pltpu.CompilerParams(dimension_semantics=("parallel",), vmem_limit_bytes=VMEM_LIMIT),
    )(parts, w, m, v)


def kernel(x, positions, g_mix, w_in, g_q_a, w_q_b, g_kv_a, w_kv_b, g_qn, g_kn, w_mla_out, ret_decay_fwd, ret_decay_bwd, w_ret_out, w_out, g_ffn, w_gate_up, w_down, loss_target, m_g_mix, m_w_in, m_g_q_a, m_w_q_b, m_g_kv_a, m_w_kv_b, m_g_qn, m_g_kn, m_w_mla_out, m_ret_decay_fwd, m_ret_decay_bwd, m_w_ret_out, m_w_out, m_g_ffn, m_w_gate_up, m_w_down, v_g_mix, v_w_in, v_g_q_a, v_w_q_b, v_g_kv_a, v_w_kv_b, v_g_qn, v_g_kn, v_w_mla_out, v_ret_decay_fwd, v_ret_decay_bwd, v_w_ret_out, v_w_out, v_g_ffn, v_w_gate_up, v_w_down):
    w = dict(g_mix=g_mix, w_in=w_in, g_q_a=g_q_a, w_q_b=w_q_b, g_kv_a=g_kv_a, w_kv_b=w_kv_b, g_qn=g_qn, g_kn=g_kn, w_mla_out=w_mla_out,
             ret_decay_fwd=ret_decay_fwd, ret_decay_bwd=ret_decay_bwd, w_ret_out=w_ret_out, w_out=w_out, g_ffn=g_ffn,
             w_gate_up=w_gate_up, w_down=w_down)
    m = dict(g_mix=m_g_mix, w_in=m_w_in, g_q_a=m_g_q_a, w_q_b=m_w_q_b, g_kv_a=m_g_kv_a, w_kv_b=m_w_kv_b, g_qn=m_g_qn, g_kn=m_g_kn,
             w_mla_out=m_w_mla_out, ret_decay_fwd=m_ret_decay_fwd, ret_decay_bwd=m_ret_decay_bwd, w_ret_out=m_w_ret_out, w_out=m_w_out,
             g_ffn=m_g_ffn, w_gate_up=m_w_gate_up, w_down=m_w_down)
    v = dict(g_mix=v_g_mix, w_in=v_w_in, g_q_a=v_g_q_a, w_q_b=v_w_q_b, g_kv_a=v_g_kv_a, w_kv_b=v_w_kv_b, g_qn=v_g_qn, g_kn=v_g_kn,
             w_mla_out=v_w_mla_out, ret_decay_fwd=v_ret_decay_fwd, ret_decay_bwd=v_ret_decay_bwd, w_ret_out=v_w_ret_out, w_out=v_w_out,
             g_ffn=v_g_ffn, w_gate_up=v_w_gate_up, w_down=v_w_down)
    gains = {n: w[n].reshape(1, ln) for n, ln in GAINS}

    axis_of = {n: axis for n, _, axis in MATS}
    later = [n for n, _, _ in MATS if n not in FIRST_WEIGHTS]
    gathered = _all_gather([w[n].astype(WIRE) for n in FIRST_WEIGHTS])
    W = {n: _unshard(g, axis_of[n]) for n, g in zip(FIRST_WEIGHTS, gathered)}
    later_handles = _exchange_start("gather_later_start", [w[n].astype(WIRE) for n in later], True, after=gathered[0])

    def late_weights(after):
        lands = _exchange_wait("gather_later_wait", later_handles, after, True)
        return {n: _unshard(g, axis_of[n]) for n, g in zip(later, lands)}

    grad_groups = []

    def grad_hook(g):
        names = tuple(g)
        handles = _exchange_start("grads_start_%d" % len(grad_groups), [_reshard(g[n], axis_of[n]).astype(GWIRE) for n in names], False)
        grad_groups.append((names, handles))
        return handles[4]

    S = x.shape[1]
    pos = positions.reshape(S, 1).astype(F32)
    loss_rows, grad_x, gG, gW = _local_step(x.reshape(S, D_MODEL), pos, loss_target.reshape(S, D_MODEL), gains, W, late_weights, grad_hook,
                                            start_after=later_handles[4])
    loss = lax.psum(jnp.sum(loss_rows), ("x", "y", "c"))

    last = [n for n, _, _ in MATS if n not in EARLY_GRADS + MID_GRADS]
    pieces = [_reshard(gW[n], axis_of[n]).astype(GWIRE) for n in last]
    pieces.append(jnp.broadcast_to(_pack_gains(gG)[None], (N_DEV, 1, GAIN_PAD)))
    late_parts = _all_to_all(pieces)
    parts = dict(zip(last, late_parts))
    for i, (names, handles) in enumerate(grad_groups):
        parts.update(zip(names, _exchange_wait("grads_wait_%d" % i, handles, late_parts[-1], False)))
    out = [dict() for _ in range(4)]
    for n, _, _ in MATS:
        for o, r in zip(out, _adamw("adamw_" + n, parts[n], w[n], m[n], v[n])):
            o[n] = r
    for o, r in zip(out, _adamw("adamw_gains", late_parts[-1], _pack_gains(w), _pack_gains(m), _pack_gains(v))):
        o.update(_unpack_gains(r))
    return (loss, grad_x.reshape(x.shape), *[o[n] for o in out for n in ORDER])
```

```python
import functools

import numpy as np
import jax
import jax.numpy as jnp
from jax import lax
from jax.experimental import pallas as pl
from jax.experimental.pallas import tpu as pltpu

F32 = jnp.float32
MXU = jnp.bfloat16
WIRE = jnp.bfloat16
GWIRE = jnp.bfloat16

N_DEV = 8
D_MODEL = 1024
HEADS = 8
LANES = 128
Q_RANK, KV_RANK = 256, 128
NOPE, ROPE_M, V_M = 64, 32, 64
QK_M = NOPE + ROPE_M
RQK, RV = 64, 128
CHUNK = 128
FFN = 2816
IN_WIDTH = 5536
THETA = 10000.0
EPS = 1e-6
LR, B1, B2, AEPS, WD, STEP = 0.001, 0.9, 0.999, 1e-08, 0.01, 10
VMEM_LIMIT = 56 * 1024 * 1024

NN = ((1,), (0,))
NT = ((1,), (1,))
TN = ((0,), (0,))

P_GATES, P_VR, P_GR, P_QR, P_KR, P_CQ, P_CKV, P_KROPE, P_WIDTH = 0, 2048, 3072, 4096, 4608, 5120, 5376, 5504, 5632
O_CQ, O_CKV, O_KROPE, O_QR, O_KR, O_VR, O_GR, O_GATES = 0, 256, 384, 416, 928, 1440, 2464, 3488


def _dot(a, b, dims):
    return lax.dot_general(a, b, (dims, ((), ())), preferred_element_type=F32)


def _pick(dim, cands):
    for c in cands:
        if dim % c == 0:
            return c
    return dim


def _pairs(t):
    return t.reshape(t.shape[0], 4, 2, 2, 32).transpose(0, 1, 3, 2, 4).reshape(t.shape[0], 512)


def _win_pad(w):
    z = jnp.zeros((w.shape[0], 48), w.dtype)
    kr = w[:, O_KROPE:O_KROPE + 32]
    return jnp.concatenate([w[:, O_GATES:], w[:, O_VR:O_VR + 1024], w[:, O_GR:O_GR + 1024], _pairs(w[:, O_QR:O_QR + 512]),
                            _pairs(w[:, O_KR:O_KR + 512]), w[:, :O_CKV], w[:, O_CKV:O_KROPE], kr[:, :16], z, kr[:, 16:], z], axis=1)


def _win_unpad(g):
    return jnp.concatenate([g[:, P_CQ:P_CQ + 256], g[:, P_CKV:P_CKV + 128], g[:, P_KROPE:P_KROPE + 16], g[:, P_KROPE + 64:P_KROPE + 80],
                            _pairs(g[:, P_QR:P_QR + 512]), _pairs(g[:, P_KR:P_KR + 512]), g[:, P_VR:P_VR + 1024],
                            g[:, P_GR:P_GR + 1024], g[:, P_GATES:P_GATES + 2048]], axis=1)


def _qk_pad(t):
    z = jnp.zeros(t.shape[:-1] + (32,), t.dtype)
    return jnp.concatenate([t[..., 64:80], t[..., 0:48], t[..., 80:96], t[..., 48:64], z], axis=-1)


def _qk_unpad(p):
    return jnp.concatenate([p[..., 16:64], p[..., 80:96], p[..., 0:16], p[..., 64:80]], axis=-1)


def _wq_pad(w):
    return _qk_pad(w.reshape(Q_RANK, HEADS, QK_M)).reshape(Q_RANK, HEADS * LANES)


def _wq_unpad(g):
    return _qk_unpad(g.reshape(Q_RANK, HEADS, LANES)).reshape(Q_RANK, HEADS * QK_M)


def _wkv_pad(w):
    t = w.reshape(KV_RANK, HEADS, NOPE + V_M)
    z = lambda n: jnp.zeros((KV_RANK, HEADS, n), w.dtype)
    wk = jnp.concatenate([z(16), t[..., 0:48], z(16), t[..., 48:64], z(32)], axis=-1)
    wv = jnp.concatenate([t[..., 64:128], z(64)], axis=-1)
    return wk.reshape(KV_RANK, HEADS * LANES), wv.reshape(KV_RANK, HEADS * LANES)


def _wkv_unpad(dwk, dwv):
    k, v = dwk.reshape(KV_RANK, HEADS, LANES), dwv.reshape(KV_RANK, HEADS, LANES)
    return jnp.concatenate([k[..., 16:64], k[..., 80:96], v[..., 0:64]], axis=-1).reshape(KV_RANK, HEADS * (NOPE + V_M))


def _wmla_pad(w):
    t = w.reshape(HEADS, V_M, D_MODEL)
    return jnp.concatenate([t, jnp.zeros_like(t)], axis=1).reshape(HEADS * LANES, D_MODEL)


def _wmla_unpad(g):
    return g.reshape(HEADS, LANES, D_MODEL)[:, :V_M].reshape(HEADS * V_M, D_MODEL)


def _rowwise(name, fn, rows, ts, ins, outs, accs=(), ncol=1):
    n_in, n_out, n_acc = len(ins), len(outs), len(accs)

    def colmap(col):
        if callable(col):
            return lambda i, j: (i, col(j))
        return lambda i, j: (i, col)

    arrays, in_specs = [], []
    for arr, spec in ins:
        arrays.append(arr)
        if spec is None:
            in_specs.append(pl.BlockSpec(arr.shape, functools.partial(lambda i, j, nd: (0,) * nd, nd=arr.ndim)))
        else:
            in_specs.append(pl.BlockSpec((ts, spec[0]), colmap(spec[1])))
    out_shape, out_specs = [], []
    for total, dtype, width, col in outs:
        out_shape.append(jax.ShapeDtypeStruct((rows, total), dtype))
        out_specs.append(pl.BlockSpec((ts, width), colmap(col)))
    for shp in accs:
        out_shape.append(jax.ShapeDtypeStruct(shp, F32))
        out_specs.append(pl.BlockSpec(shp, functools.partial(lambda i, j, nd: (0,) * nd, nd=len(shp))))

    def body(*refs):
        vals = [r[...] for r in refs[:n_in]]
        res = fn(*vals)
        if not isinstance(res, (tuple, list)):
            res = (res,)
        for r, v in zip(refs[n_in:n_in + n_out], res[:n_out]):
            r[...] = v.astype(r.dtype)
        if n_acc:
            first = jnp.logical_and(pl.program_id(0) == 0, pl.program_id(1) == 0)
            for r, v in zip(refs[n_in + n_out:], res[n_out:]):
                @pl.when(first)
                def _(r=r):
                    r[...] = jnp.zeros_like(r)
                r[...] += v.astype(F32)

    res = pl.pallas_call(
        body, name=name, grid=(rows // ts, ncol), in_specs=in_specs, out_specs=out_specs, out_shape=out_shape,
        compiler_params=pltpu.CompilerParams(dimension_semantics=("arbitrary", "arbitrary"), vmem_limit_bytes=VMEM_LIMIT),
    )(*arrays)
    return res


MM_OPERAND_BYTES = 24 * 1024 * 1024


def _mm(name, a, b, mode, add=None, after=None):
    a_halves, b_halves = a.ndim == 3, b.ndim == 3
    assert not a_halves or mode == "nt"
    assert not b_halves or mode == "tn"
    if mode == "nn":
        (M, K), N = a.shape, b.shape[1]
    elif mode == "nt":
        M, K, N = a.shape[-2], a.shape[-1] * (2 if a_halves else 1), b.shape[0]
    else:
        (K, M), N = a.shape, b.shape[-1] * (2 if b_halves else 1)
    tm = _pick(M, (512, 1408, 256, 128)) if mode == "tn" else _pick(M, (1024, 512, 256, 128))
    tn = _pick(N // 2 if b_halves else N, (1408, 1024, 512, 256, 128))
    fits = lambda t: 2 * (tm + tn) * t * a.dtype.itemsize <= MM_OPERAND_BYTES
    kdiv = K // 2 if a_halves else K
    tk = next(t for t in (K, 4096, 2816, 2048, 1408, 1024, 512, 256, 128) if kdiv % t == 0 and (fits(t) or t == 128))
    nk = K // tk
    dims = {"nn": NN, "nt": NT, "tn": TN}[mode]
    if a_halves:
        per = kdiv // tk
        a_spec = pl.BlockSpec((None, tm, tk), lambda i, j, k: (k // per, i, k % per))
    else:
        a_spec = pl.BlockSpec((tk, tm), lambda i, j, k: (k, i)) if mode == "tn" else pl.BlockSpec((tm, tk), lambda i, j, k: (i, k))
    if b_halves:
        perj = (N // 2) // tn
        b_spec = pl.BlockSpec((None, tk, tn), lambda i, j, k: (j // perj, k, j % perj))
    else:
        b_spec = pl.BlockSpec((tn, tk), lambda i, j, k: (j, k)) if mode == "nt" else pl.BlockSpec((tk, tn), lambda i, j, k: (k, j))
    o_spec = pl.BlockSpec((tm, tn), lambda i, j, k: (i, j))
    has_add = add is not None

    def body(*refs):
        a_ref, b_ref, o_ref = refs[0], refs[1], refs[-1]
        d = _dot(a_ref[...], b_ref[...], dims)
        first = (d + refs[2][...]) if has_add else d
        if nk == 1:
            o_ref[...] = first
        else:
            k = pl.program_id(2)

            @pl.when(k == 0)
            def _():
                o_ref[...] = first

            @pl.when(k > 0)
            def _():
                o_ref[...] += d

    args = [a, b] + ([add] if has_add else []) + ([] if after is None else [after])
    specs = [a_spec, b_spec] + ([o_spec] if has_add else []) + ([] if after is None else [pl.BlockSpec(memory_space=pl.ANY)])
    return pl.pallas_call(
        body, name=name, grid=(M // tm, N // tn, nk), in_specs=specs, out_specs=o_spec,
        out_shape=jax.ShapeDtypeStruct((M, N), F32),
        compiler_params=pltpu.CompilerParams(dimension_semantics=("parallel", "parallel", "arbitrary"), vmem_limit_bytes=VMEM_LIMIT),
    )(*args)


def _mm_rows(name, a, b, fn, row_ins, whole_ins, outs, accs=(), mode="nn"):
    (M, K), N = a.shape, b.shape[1 if mode == "nn" else 0]
    tm = _pick(M, (512, 256, 128))
    n_in, n_out = 2 + len(row_ins) + len(whole_ins), len(outs)
    windows = [t if isinstance(t, tuple) else (t, (t.shape[1], 0)) for t in row_ins]
    row_ins = [t for t, _ in windows]
    row_specs = [pl.BlockSpec((tm, w), functools.partial(lambda i, col: (i, col), col=col)) for _, (w, col) in windows]

    def body(*refs):
        d = _dot(refs[0][...], refs[1][...], NN if mode == "nn" else NT)
        res = fn(d, *[r[...] for r in refs[2:n_in]])
        for r, v in zip(refs[n_in:n_in + n_out], res[:n_out]):
            r[...] = v.astype(r.dtype)
        for r, v in zip(refs[n_in + n_out:], res[n_out:]):
            @pl.when(pl.program_id(0) == 0)
            def _(r=r):
                r[...] = jnp.zeros_like(r)
            r[...] += v

    row = pl.BlockSpec((tm, N), lambda i: (i, 0))
    whole = lambda t: pl.BlockSpec(t.shape, functools.partial(lambda i, nd: (0,) * nd, nd=t.ndim))
    return pl.pallas_call(
        body, name=name, grid=(M // tm,),
        in_specs=[pl.BlockSpec((tm, K), lambda i: (i, 0)), whole(b)] + row_specs + [whole(t) for t in whole_ins],
        out_specs=[row] * n_out + [pl.BlockSpec(s, functools.partial(lambda i, nd: (0,) * nd, nd=len(s))) for s in accs],
        out_shape=[jax.ShapeDtypeStruct((M, N), dt) for dt in outs] + [jax.ShapeDtypeStruct(s, F32) for s in accs],
        compiler_params=pltpu.CompilerParams(dimension_semantics=("arbitrary",), vmem_limit_bytes=VMEM_LIMIT),
    )(a, b, *row_ins, *whole_ins)


def _ffn_tiles(S):
    return _pick(S, (1024, 512, 256, 128)), _pick(FFN, (1408, 704, 256, 128))


def _gate_up_swiglu(h2, wgu):
    S, K = h2.shape
    tm, tn = _ffn_tiles(S)
    nj = FFN // tn

    def body(a_ref, bg_ref, bu_ref, gu_ref, act_ref):
        a = a_ref[...]
        g, u = _dot(a, bg_ref[...], NN), _dot(a, bu_ref[...], NN)
        gu_ref[0], gu_ref[1] = g.astype(gu_ref.dtype), u.astype(gu_ref.dtype)
        act_ref[...] = _swiglu_fn(g, u).astype(act_ref.dtype)

    return pl.pallas_call(
        body, name="gate_up_swiglu", grid=(S // tm, nj),
        in_specs=[pl.BlockSpec((tm, K), lambda i, j: (i, 0)), pl.BlockSpec((K, tn), lambda i, j: (0, j)),
                  pl.BlockSpec((K, tn), lambda i, j: (0, nj + j))],
        out_specs=[pl.BlockSpec((2, tm, tn), lambda i, j: (0, i, j)), pl.BlockSpec((tm, tn), lambda i, j: (i, j))],
        out_shape=[jax.ShapeDtypeStruct((2, S, FFN), MXU), jax.ShapeDtypeStruct((S, FFN), MXU)],
        compiler_params=pltpu.CompilerParams(dimension_semantics=("parallel", "parallel"), vmem_limit_bytes=VMEM_LIMIT),
    )(h2, wgu, wgu)


def _d_act_swiglu(dx2, wdown, gu):
    S, K = dx2.shape
    tm, tn = _ffn_tiles(S)

    def body(a_ref, b_ref, gu_ref, o_ref):
        dact = _dot(a_ref[...], b_ref[...], NT)
        _, vjp = jax.vjp(_swiglu_fn, gu_ref[0].astype(F32), gu_ref[1].astype(F32))
        dg, du = vjp(dact)
        o_ref[0], o_ref[1] = dg.astype(o_ref.dtype), du.astype(o_ref.dtype)

    stacked = pl.BlockSpec((2, tm, tn), lambda i, j: (0, i, j))
    return pl.pallas_call(
        body, name="d_act_swiglu", grid=(S // tm, FFN // tn),
        in_specs=[pl.BlockSpec((tm, K), lambda i, j: (i, 0)), pl.BlockSpec((tn, K), lambda i, j: (j, 0)), stacked],
        out_specs=stacked, out_shape=jax.ShapeDtypeStruct((2, S, FFN), MXU),
        compiler_params=pltpu.CompilerParams(dimension_semantics=("parallel", "parallel"), vmem_limit_bytes=VMEM_LIMIT),
    )(dx2, wdown, gu)


@jax.custom_vjp
def _swap64(x):
    return pltpu.roll(x, 64, 1)


_swap64.defvjp(lambda x: (_swap64(x), None), lambda _, g: (_swap64(g),))


@jax.custom_vjp
def _mxdot(a, b):
    return _dot(a.astype(MXU), b.astype(MXU), NN)


def _mxdot_bwd(res, g):
    a, b = res
    gb = g.astype(MXU)
    return _dot(gb, b.astype(MXU), NT), _dot(a.astype(MXU), gb, TN)


_mxdot.defvjp(lambda a, b: (_mxdot(a, b), (a, b)), _mxdot_bwd)


def _row_sum(t):
    if t.shape[-1] == LANES:
        return lax.dot_general(t, jnp.ones((LANES, LANES), F32), ((NN), ((), ())), precision=lax.Precision.HIGH,
                               preferred_element_type=F32)
    return jnp.sum(t, axis=-1, keepdims=True)


@functools.partial(jax.custom_vjp, nondiff_argnums=(1,))
def _unit_rms(x, n):
    return x * lax.rsqrt(_row_sum(x * x) * (1.0 / n) + EPS)


def _unit_rms_fwd(x, n):
    r = lax.rsqrt(_row_sum(x * x) * (1.0 / n) + EPS)
    y = x * r
    return y, (y, r)


def _unit_rms_bwd(n, res, g):
    y, r = res
    return (r * (g - y * (_row_sum(g * y) * (1.0 / n))),)


_unit_rms.defvjp(_unit_rms_fwd, _unit_rms_bwd)


def _rms(x):
    return _unit_rms(x, x.shape[-1])


def _rmsg_fn(x, g):
    return _rms(x) * g


def _silu(x):
    return x * jax.nn.sigmoid(x)


def _tables_fn(pos, inv_m, sgn_m, inv_r, sgn_r):
    am, ar = pos * inv_m, pos * inv_r
    return jnp.cos(am), jnp.sin(am) * sgn_m, jnp.cos(ar), jnp.sin(ar) * sgn_r


def _head_blocks(t):
    return [t[:, LANES * h:LANES * (h + 1)] for h in range(t.shape[1] // LANES)]


def _mla_prep_fn(cq, ckv, kr, cosm, sinm, gqa, gkva, gqn, gkn, wq, wk, wv):
    cqn = _rms(cq) * gqa
    ckvn = _rms(ckv) * gkva
    q_raw = _mxdot(cqn, wq)
    k_raw = _mxdot(ckvn, wk)
    lane = lax.broadcasted_iota(jnp.int32, (1, HEADS * LANES), 1)
    v = _mxdot(ckvn, wv) + (lane % LANES == V_M).astype(F32)

    def norm_rope(blocks, g, extra):
        outs = []
        for b in blocks:
            if extra is not None:
                b = b + extra
            n = _unit_rms(b, QK_M) * g
            outs.append(n * cosm + _swap64(n) * sinm)
        return jnp.concatenate(outs, axis=1)

    q = norm_rope(_head_blocks(q_raw), gqn, None)
    k = norm_rope(_head_blocks(k_raw), gkn, kr)
    return q, k, v


def _ret_prep_fn(qr, kr, cosr, sinr):
    def rope(t, scale):
        return jnp.concatenate([(b * cosr + _swap64(b) * sinr) * scale for b in _head_blocks(t)], axis=1)
    return rope(qr, 1.0), rope(kr, RQK ** -0.5)


def _ret_post_fn(rf, rb, gr):
    ret = rf + rb
    outs = []
    for b, g in zip(_head_blocks(ret), _head_blocks(gr)):
        outs.append(_silu(g) * _rms(b))
    return jnp.concatenate(outs, axis=1)


def _merge_fn(ga, gb, ya, yb):
    return jax.nn.sigmoid(ga) * ya + jax.nn.sigmoid(gb) * yb


def _swiglu_fn(gate, up):
    return _silu(gate) * up


def _loss_fn(x2, tgt):
    d = x2 - tgt
    return d * (1.0 / D_MODEL), 0.5 * jnp.sum(d * d, axis=0, keepdims=True) * (1.0 / D_MODEL)


def _adamw_fn(parts, w, m, v):
    g = parts[0].astype(F32)
    for p in range(1, N_DEV):
        g = g + parts[p].astype(F32)
    m2 = B1 * m + (1.0 - B1) * g
    v2 = B2 * v + (1.0 - B2) * jnp.square(g)
    m_hat = m2 / (1.0 - B1 ** STEP)
    v_hat = v2 / (1.0 - B2 ** STEP)
    delta = -LR * (m_hat / (jnp.sqrt(v_hat) + AEPS) + WD * w)
    return g, delta, m2, v2


SCALE = QK_M ** -0.5
LOG2E = 1.4426950408889634
FLASH_ROWS = 32


def _flash_fwd(q, k, v):
    S = q.shape[0]
    tk = _pick(S, (512, 256, 128))
    tq = _pick(S, (1024, 512, 256, 128))
    ncb = tk // LANES
    nkv = S // tk
    assert nkv % 2 == 0, "kv tiles are processed in pairs"
    mrows = 64
    c = SCALE * LOG2E

    def body(q_ref, k_ref, v_ref, o_ref, lse_ref, s_a, p_a, s_b, p_b, m_sc, a_sc, acc_sc):
        m_sc[...] = jnp.full_like(m_sc, -jnp.inf)
        acc_sc[...] = jnp.zeros_like(acc_sc)
        qb = q_ref[...]

        def scores(j, s_buf):
            s_buf[...] = _dot(qb, k_ref[pl.ds(pl.multiple_of(j * tk, tk), tk), :], NT)

        def stage(j, s_buf, p_buf, s_next):
            scores(jnp.minimum(j + 1, nkv - 1), s_next)
            for r in range(tq // mrows):
                rows = slice(r * mrows, (r + 1) * mrows)
                cols = [s_buf[rows, LANES * cb:LANES * (cb + 1)] for cb in range(ncb)]
                m_prev = m_sc[rows, :]
                row_max = jnp.max(functools.reduce(jnp.maximum, cols), axis=-1, keepdims=True)
                m_new = jnp.maximum(m_prev, jnp.broadcast_to(row_max, (mrows, LANES)))
                a_sc[rows, :] = jnp.exp2((m_prev - m_new) * c)
                m_sc[rows, :] = m_new
                for cb in range(ncb):
                    p_buf[rows, LANES * cb:LANES * (cb + 1)] = jnp.exp2((cols[cb] - m_new) * c).astype(p_buf.dtype)
            acc_sc[...] = a_sc[...] * acc_sc[...] + _dot(p_buf[...], v_ref[pl.ds(pl.multiple_of(j * tk, tk), tk), :], NN)

        scores(0, s_a)

        def pair_step(t, carry):
            stage(2 * t, s_a, p_a, s_b)
            stage(2 * t + 1, s_b, p_b, s_a)
            return carry

        lax.fori_loop(0, nkv // 2, pair_step, 0, unroll=4)
        acc = acc_sc[...]
        lane = lax.broadcasted_iota(jnp.int32, (1, LANES), 1)
        l = jnp.sum(jnp.where(lane == V_M, acc, 0.0), axis=-1, keepdims=True)
        o_ref[...] = (acc / l).astype(o_ref.dtype)
        lse_ref[...] = m_sc[...] * c + jnp.log2(jnp.broadcast_to(l, (tq, LANES)))

    qspec = pl.BlockSpec((tq, LANES), lambda h, i: (i, h))
    kspec = pl.BlockSpec((S, LANES), lambda h, i: (0, h))
    return pl.pallas_call(
        body, name="flash_fwd", grid=(HEADS, S // tq), in_specs=[qspec, kspec, kspec], out_specs=[qspec, qspec],
        out_shape=[jax.ShapeDtypeStruct((S, HEADS * LANES), MXU), jax.ShapeDtypeStruct((S, HEADS * LANES), F32)],
        scratch_shapes=[pltpu.VMEM((tq, tk), F32), pltpu.VMEM((tq, tk), MXU)] * 2 + [pltpu.VMEM((tq, LANES), F32)] * 3,
        compiler_params=pltpu.CompilerParams(dimension_semantics=("parallel", "arbitrary"), vmem_limit_bytes=VMEM_LIMIT),
    )(q, k, v)


def _delta_fn(o, do):
    outs = [jnp.broadcast_to(jnp.sum(a * b, axis=-1, keepdims=True), a.shape) for a, b in zip(_head_blocks(o), _head_blocks(do))]
    return do, jnp.concatenate(outs, axis=1)


def _flash_bwd(q, k, v, do, lse, delta):
    S = q.shape[0]
    tq = tk = _pick(S, (512, 256, 128))
    ncb = tk // LANES
    c = SCALE * LOG2E

    nq = S // tq
    assert nq % 2 == 0, "q tiles are processed in pairs"

    def body(q_ref, k_ref, v_ref, do_ref, lse_ref, dl_ref, dq_ref, dk_ref, dv_ref, s_a, dp_a, p_a, ds_a, s_b, dp_b, p_b, ds_b):
        @pl.when(pl.program_id(1) == 0)
        def _():
            dq_ref[...] = jnp.zeros_like(dq_ref)

        dk_ref[...] = jnp.zeros_like(dk_ref)
        dv_ref[...] = jnp.zeros_like(dv_ref)
        kb, vb = k_ref[...], v_ref[...]

        def scores(i, s_buf, dp_buf):
            q_rows = pl.ds(pl.multiple_of(i * tq, tq), tq)
            s_buf[...] = _dot(q_ref[q_rows, :], kb, NT)
            dp_buf[...] = _dot(do_ref[q_rows, :], vb, NT)

        def stage(i, s_buf, dp_buf, p_buf, ds_buf, s_next, dp_next):
            scores(jnp.minimum(i + 1, nq - 1), s_next, dp_next)
            for r in range(tq // FLASH_ROWS):
                rows = slice(r * FLASH_ROWS, (r + 1) * FLASH_ROWS)
                grows = pl.ds(pl.multiple_of(i * tq + r * FLASH_ROWS, FLASH_ROWS), FLASH_ROWS)
                lse_b, dl_b = lse_ref[grows, :], dl_ref[grows, :]
                for cb in range(ncb):
                    sl = slice(LANES * cb, LANES * (cb + 1))
                    p = jnp.exp2(s_buf[rows, sl] * c - lse_b)
                    p_buf[rows, sl] = p.astype(p_buf.dtype)
                    ds_buf[rows, sl] = (p * (dp_buf[rows, sl] - dl_b) * SCALE).astype(ds_buf.dtype)
            q_rows = pl.ds(pl.multiple_of(i * tq, tq), tq)
            dv_ref[...] += _dot(p_buf[...], do_ref[q_rows, :], TN)
            dk_ref[...] += _dot(ds_buf[...], q_ref[q_rows, :], TN)
            dq_ref[q_rows, :] += _dot(ds_buf[...], kb, NN)

        scores(0, s_a, dp_a)

        def pair_step(t, carry):
            stage(2 * t, s_a, dp_a, p_a, ds_a, s_b, dp_b)
            stage(2 * t + 1, s_b, dp_b, p_b, ds_b, s_a, dp_a)
            return carry

        lax.fori_loop(0, nq // 2, pair_step, 0, unroll=2)

    hspec = pl.BlockSpec((S, LANES), lambda h, j: (0, h))
    kspec = pl.BlockSpec((tk, LANES), lambda h, j: (j, h))
    full = jax.ShapeDtypeStruct((S, HEADS * LANES), F32)
    tile_bufs = [pltpu.VMEM((tq, tk), F32), pltpu.VMEM((tq, tk), F32), pltpu.VMEM((tq, tk), MXU), pltpu.VMEM((tq, tk), MXU)]
    return pl.pallas_call(
        body, name="flash_bwd", grid=(HEADS, S // tk), in_specs=[hspec, kspec, kspec, hspec, hspec, hspec],
        out_specs=[hspec, kspec, kspec], out_shape=[full, full, full],
        scratch_shapes=tile_bufs + tile_bufs,
        compiler_params=pltpu.CompilerParams(dimension_semantics=("parallel", "arbitrary"), vmem_limit_bytes=VMEM_LIMIT),
    )(q, k, v, do, lse, delta)


def _ret_consts(lgh, head, rev):
    C = CHUNK
    lane = lax.broadcasted_iota(jnp.int32, (1, LANES), 1)
    hm = ((lane // 32) % 2 == head % 2).astype(F32)
    r = lax.broadcasted_iota(jnp.int32, (C, C), 0)
    c = lax.broadcasted_iota(jnp.int32, (C, C), 1)
    diff = ((c - r) if rev else (r - c)).astype(F32)
    mask = (diff > 0) if rev else (diff >= 0)
    dpos = jnp.maximum(diff, 0.0)
    din = jnp.where(mask, jnp.exp(lgh * dpos), 0.0)
    idx = lax.broadcasted_iota(jnp.int32, (C, 1), 0).astype(F32)
    eq = (C - idx) if rev else (idx + 1.0)
    ek = idx if rev else (C - 1.0 - idx)
    qd, kd = jnp.exp(lgh * eq), jnp.exp(lgh * ek)
    cd = jnp.exp(lgh * jnp.full((1, 1), float(C), F32))
    return hm, din, dpos, qd, kd, cd, eq, ek


RET_HEADS_PER_STEP = 4


def _ret_fwd(name, qt, kt, proj, lg, rev):
    S = qt.shape[0]
    C = CHUNK
    TB = _pick(S, (512, 256, 128))
    cb, nb = TB // C, S // TB
    hps = RET_HEADS_PER_STEP
    blk = (lambda g: nb - 1 - g) if rev else (lambda g: g)

    def body(lg_ref, q_ref, k_ref, v_ref, o_ref, st_ref, state_sc):
        hg, g = pl.program_id(0), pl.program_id(1)

        @pl.when(g == 0)
        def _():
            state_sc[...] = jnp.zeros_like(state_sc)

        consts = [_ret_consts(lg_ref[hg * hps + u], u, rev) for u in range(hps)]
        order = list(reversed(range(cb))) if rev else list(range(cb))
        units = [(cc, u) for cc in order for u in range(hps)]

        def operands(cc, u):
            rows = pl.ds(cc * C, C)
            pair = slice(LANES * (u // 2), LANES * (u // 2 + 1))
            hm = consts[u][0]
            return q_ref[rows, pair] * hm, k_ref[rows, pair] * hm, v_ref[rows, LANES * u:LANES * (u + 1)].astype(MXU)

        a, inc = {}, {}
        for cc, u in units:
            q, k, v = operands(cc, u)
            a[cc, u] = _dot(q.astype(MXU), k.astype(MXU), NT) * consts[u][1]
            inc[cc, u] = _dot((k * consts[u][4]).astype(MXU), v, TN)
        for u in range(hps):
            st = state_sc[u]
            for cc in order:
                st_ref[u, cc] = st
                st = st * consts[u][5] + inc[cc, u]
            state_sc[u] = st
        for cc, u in units:
            q, _, v = operands(cc, u)
            cross = _dot((q * consts[u][3]).astype(MXU), st_ref[u, cc].astype(MXU), NN)
            o_ref[pl.ds(cc * C, C), LANES * u:LANES * (u + 1)] = _dot(a[cc, u].astype(MXU), v, NN) + cross

    qk_spec = pl.BlockSpec((TB, LANES * hps // 2), lambda h, g: (blk(g), h))
    return pl.pallas_call(
        body, name=name, grid=(HEADS // hps, nb),
        in_specs=[pl.BlockSpec(memory_space=pltpu.SMEM), qk_spec, qk_spec,
                  pl.BlockSpec((TB, LANES * hps), lambda h, g: (blk(g), P_VR // (LANES * hps) + h))],
        out_specs=[pl.BlockSpec((TB, LANES * hps), lambda h, g: (blk(g), h)),
                   pl.BlockSpec((hps, cb, LANES, LANES), lambda h, g: (h, blk(g), 0, 0))],
        out_shape=[jax.ShapeDtypeStruct((S, HEADS * LANES), F32), jax.ShapeDtypeStruct((HEADS, S // C, LANES, LANES), F32)],
        scratch_shapes=[pltpu.VMEM((hps, LANES, LANES), F32)],
        compiler_params=pltpu.CompilerParams(dimension_semantics=("parallel", "arbitrary"), vmem_limit_bytes=VMEM_LIMIT),
    )(lg, qt, kt, proj)


def _ret_bwd(name, qt, kt, proj, dret, states, lg, rev):
    S = qt.shape[0]
    C = CHUNK
    TB = _pick(S, (512, 256, 128))
    cb, nb = TB // C, S // TB
    hps = RET_HEADS_PER_STEP
    blk = (lambda g: g) if rev else (lambda g: nb - 1 - g)

    def body(lg_ref, q_ref, k_ref, v_ref, do_ref, st_ref, dq_ref, dk_ref, dv_ref, dlg_ref, ds_sc, acc_cc, acc_q, acc_k, acc_s):
        hg, g = pl.program_id(0), pl.program_id(1)

        @pl.when(g == 0)
        def _():
            ds_sc[...] = jnp.zeros_like(ds_sc)
            acc_cc[...] = jnp.zeros_like(acc_cc)
            acc_q[...] = jnp.zeros_like(acc_q)
            acc_k[...] = jnp.zeros_like(acc_k)
            acc_s[...] = jnp.zeros_like(acc_s)

        lgs = [lg_ref[hg * hps + u] for u in range(hps)]
        consts = [_ret_consts(lgs[u], u, rev) for u in range(hps)]
        order = list(range(cb)) if rev else list(reversed(range(cb)))
        units = [(cc, u) for cc in order for u in range(hps)]

        def operands(cc, u):
            rows = pl.ds(cc * C, C)
            pair = slice(LANES * (u // 2), LANES * (u // 2 + 1))
            head = slice(LANES * u, LANES * (u + 1))
            hm = consts[u][0]
            return q_ref[rows, pair] * hm, k_ref[rows, pair] * hm, v_ref[rows, head].astype(MXU), do_ref[rows, head].astype(MXU)

        a, dp, dqs, inc = {}, {}, {}, {}
        for cc, u in units:
            q, k, vb, dob = operands(cc, u)
            a[cc, u] = _dot(q.astype(MXU), k.astype(MXU), NT)
            dp[cc, u] = _dot(dob, vb, NT)
            dqs[cc, u] = _dot(dob, st_ref[u, cc].astype(MXU), NT)
            inc[cc, u] = _dot((q * consts[u][3]).astype(MXU), dob, TN)
        dsn = {}
        for u in range(hps):
            ds = ds_sc[u]
            for cc in order:
                dsn[cc, u] = ds
                ds = ds * consts[u][5] + inc[cc, u]
            ds_sc[u] = ds
        even = {}
        for cc, u in units:
            hm, din, dpos, qd, kd, cd, eq, ek = consts[u]
            rows, head = pl.ds(cc * C, C), slice(LANES * u, LANES * (u + 1))
            q, k, vb, dob = operands(cc, u)
            qb, kb = q.astype(MXU), k.astype(MXU)
            dsnb = dsn[cc, u].astype(MXU)
            da = (dp[cc, u] * din).astype(MXU)
            vds = _dot(vb, dsnb, NT)
            dq_u = (_dot(da, kb, NN) + dqs[cc, u] * qd) * hm
            dk_u = (_dot(da, qb, TN) + vds * kd) * hm
            if u % 2 == 0:
                even[cc] = (dq_u, dk_u)
            else:
                pair = slice(LANES * (u // 2), LANES * (u // 2 + 1))
                dq_ref[rows, pair] = even[cc][0] + dq_u
                dk_ref[rows, pair] = even[cc][1] + dk_u
            dv_ref[rows, head] = _dot((a[cc, u] * din).astype(MXU), dob, TN) + _dot((k * kd).astype(MXU), dsnb, NN)
            acc_cc[u] += dp[cc, u] * a[cc, u] * din * dpos
            acc_q[u] += dqs[cc, u] * q * (qd * eq)
            acc_k[u] += vds * k * (kd * ek)
            acc_s[u] += dsn[cc, u] * st_ref[u, cc] * (cd * float(C))

        @pl.when(g == nb - 1)
        def _():
            for u in range(hps):
                tot = (jnp.sum(acc_cc[u], keepdims=True) + jnp.sum(acc_q[u], keepdims=True)
                       + jnp.sum(acc_k[u], keepdims=True) + jnp.sum(acc_s[u], keepdims=True))
                dlg_ref[u] = jnp.broadcast_to(tot * lgs[u], (8, LANES))

    full = jax.ShapeDtypeStruct((S, HEADS * LANES), F32)
    hspec = pl.BlockSpec((TB, LANES * hps), lambda h, g: (blk(g), h))
    qk_spec = pl.BlockSpec((TB, LANES * hps // 2), lambda h, g: (blk(g), h))
    return pl.pallas_call(
        body, name=name, grid=(HEADS // hps, nb),
        in_specs=[pl.BlockSpec(memory_space=pltpu.SMEM), qk_spec, qk_spec,
                  pl.BlockSpec((TB, LANES * hps), lambda h, g: (blk(g), P_VR // (LANES * hps) + h)),
                  hspec,
                  pl.BlockSpec((hps, cb, LANES, LANES), lambda h, g: (h, blk(g), 0, 0))],
        out_specs=[qk_spec, qk_spec, hspec, pl.BlockSpec((hps, 8, LANES), lambda h, g: (h, 0, 0))],
        out_shape=[jax.ShapeDtypeStruct(qt.shape, F32), jax.ShapeDtypeStruct(kt.shape, F32), full,
                   jax.ShapeDtypeStruct((HEADS, 8, LANES), F32)],
        scratch_shapes=[pltpu.VMEM((hps, LANES, LANES), F32), pltpu.VMEM((hps, C, C), F32), pltpu.VMEM((hps, C, LANES), F32),
                        pltpu.VMEM((hps, C, LANES), F32), pltpu.VMEM((hps, LANES, LANES), F32)],
        compiler_params=pltpu.CompilerParams(dimension_semantics=("parallel", "arbitrary"), vmem_limit_bytes=VMEM_LIMIT),
    )(lg, qt, kt, proj, dret, states)


def _rope_consts():
    inv16 = THETA ** (-jnp.arange(16, dtype=F32) / 16)
    inv32 = THETA ** (-jnp.arange(32, dtype=F32) / 32)
    lane = np.arange(LANES)
    z48 = jnp.zeros((48,), F32)
    inv_m = jnp.concatenate([inv16, z48, inv16, z48])[None, :]
    sgn_m = jnp.asarray(np.where(lane < 16, -1.0, np.where((lane >= 64) & (lane < 80), 1.0, 0.0)), F32)[None, :]
    inv_r = jnp.concatenate([inv32] * 4)[None, :]
    sgn_r = jnp.asarray(np.where(lane < 64, -1.0, 1.0), F32)[None, :]
    return inv_m, sgn_m, inv_r, sgn_r


FIRST_WEIGHTS = ("w_in", "w_q_b", "w_kv_b")
EARLY_GRADS = ("w_down", "w_gate_up", "w_out", "w_ret_out")
MID_GRADS = ("w_mla_out", "w_in")


def _local_step(x, pos, tgt, gains, W, late_weights=None, grad_hook=None, start_after=None):
    S = x.shape[0]
    ts = _pick(S, (256, 128))
    R = lambda a, w=None, c=0: (a, ((a.shape[1] if w is None else w), c))
    W_ = lambda a: (a, None)

    win = _win_pad(W["w_in"])
    wq = _wq_pad(W["w_q_b"])
    wk, wv = _wkv_pad(W["w_kv_b"])
    gqn, gkn = _qk_pad(gains["g_qn"]), _qk_pad(gains["g_kn"])
    g_mix, g_q_a, g_kv_a, g_ffn = gains["g_mix"], gains["g_q_a"], gains["g_kv_a"], gains["g_ffn"]
    lg_f = -jnp.exp(gains["ret_decay_fwd"][0])
    lg_b = -jnp.exp(gains["ret_decay_bwd"][0])

    consts = list(_rope_consts())
    cosm, sinm, cosr, sinr = _rowwise("rope_tables", _tables_fn, S, ts, [R(pos)] + [W_(c) for c in consts],
                                      [(LANES, F32, LANES, 0)] * 4)

    (h,) = _rowwise("rms_mix", _rmsg_fn, S, ts, [R(x), W_(g_mix)], [(D_MODEL, MXU, D_MODEL, 0)])
    proj = _mm("in_proj", h, win, "nn", after=start_after)
    seg = lambda off, w: (proj, (w, off // w))
    mla_ins = [seg(P_CQ, 256), seg(P_CKV, 128), seg(P_KROPE, 128), R(cosm), R(sinm),
               W_(g_q_a), W_(g_kv_a), W_(gqn), W_(gkn), W_(wq), W_(wk), W_(wv)]
    q, k, v = _rowwise("mla_prep", _mla_prep_fn, S, ts, mla_ins, [(HEADS * LANES, MXU, HEADS * LANES, 0)] * 3)
    o_bf, lse = _flash_fwd(q, k, v)
    if late_weights is not None:
        W = {**W, **late_weights(lse)}
    wmla = _wmla_pad(W["w_mla_out"])
    wret, wout, wgu, wdown = W["w_ret_out"], W["w_out"], W["w_gate_up"], W["w_down"]
    y_a = _mm("mla_out", o_bf, wmla, "nn")

    ret_ins = [seg(P_QR, 512), seg(P_KR, 512), R(cosr), R(sinr)]
    qt, kt = _rowwise("ret_prep", _ret_prep_fn, S, ts, ret_ins, [(512, F32, 512, 0)] * 2)
    ret_f, st_f = _ret_fwd("ret_fwd_f", qt, kt, proj, lg_f, False)
    ret_b, st_b = _ret_fwd("ret_fwd_b", qt, kt, proj, lg_b, True)
    post_ins = [R(ret_f), R(ret_b), seg(P_GR, 1024)]
    (o_b,) = _rowwise("ret_post", _ret_post_fn, S, ts, post_ins, [(1024, MXU, 1024, 0)])
    y_b, merged = _mm_rows("ret_out_merge", o_b, wret, lambda yb, ga, gb, ya: (yb, _merge_fn(ga, gb, ya, yb)),
                           [seg(P_GATES, 1024), (proj, (1024, 1)), R(y_a)], [], [F32, MXU])
    merge_ins = [seg(P_GATES, 1024), (proj, (1024, 1)), R(y_a), R(y_b)]
    def residual_rms(d, xx, g):
        r = d + xx
        return r, _rmsg_fn(r, g)

    x1, h2 = _mm_rows("out_proj_rms_ffn", merged, wout, residual_rms, [x], [g_ffn], [F32, MXU])
    gu, act = _gate_up_swiglu(h2, wgu)

    def residual_loss(d, xx, t):
        dx, rows = _loss_fn(d + xx, t)
        return dx, dx, rows

    dx2, dx2_bf, loss_rows = _mm_rows("down_proj_loss", act, wdown, residual_loss, [x1, tgt], [], [F32, MXU], accs=[(1, D_MODEL)])

    gW = {}
    gW["w_down"] = _mm("d_w_down", act, dx2_bf, "tn")
    dgu = _d_act_swiglu(dx2_bf, wdown, gu)
    gW["w_gate_up"] = _mm("d_w_gate_up", h2, dgu, "tn")
    dh2 = _mm("d_h2", dgu, wgu, "nt")

    def rms_bwd(xx, g, dh, dres):
        _, vjp = jax.vjp(_rmsg_fn, xx, g)
        dx, dg = vjp(dh)
        dx = dx + dres
        return dx, dx, dg

    dx1, dx1_bf, dg_ffn = _rowwise("rms_ffn_bwd", rms_bwd, S, ts, [R(x1), W_(g_ffn), R(dh2), R(dx2)],
                                   [(D_MODEL, F32, D_MODEL, 0), (D_MODEL, MXU, D_MODEL, 0)], accs=[(1, D_MODEL)])
    gW["w_out"] = _mm("d_w_out", merged, dx1_bf, "tn")
    def merge_bwd(dm, ga, gb, ya, yb):
        _, vjp = jax.vjp(_merge_fn, ga, gb, ya, yb)
        return vjp(dm)

    dga, dgb, dy_a, dy_b = _mm_rows("d_merged_merge_bwd", dx1_bf, wout, merge_bwd, merge_ins, [], [MXU] * 4, mode="nt")
    gW["w_ret_out"] = _mm("d_w_ret_out", o_b, dy_b, "tn")
    after_early = [] if grad_hook is None else [grad_hook({n: gW[n] for n in EARLY_GRADS})]

    def post_bwd(dob, rf, rb, gr, *_):
        _, vjp = jax.vjp(_ret_post_fn, rf, rb, gr)
        drf, _, dgr = vjp(dob)
        return drf, dgr

    dret, dg_r = _mm_rows("d_o_b_ret_post_bwd", dy_b, wret, post_bwd, post_ins, after_early, [F32, MXU], mode="nt")
    dq_f, dk_f, dv_f, dlg_f = _ret_bwd("ret_bwd_f", qt, kt, proj, dret, st_f, lg_f, False)
    dq_b, dk_b, dv_b, dlg_b = _ret_bwd("ret_bwd_b", qt, kt, proj, dret, st_b, lg_b, True)

    def ret_prep_bwd(qr, kr, cosr_, sinr_, dqf, dqb, dkf, dkb, dvf, dvb):
        _, vjp = jax.vjp(lambda a, b: _ret_prep_fn(a, b, cosr_, sinr_), qr, kr)
        dqr, dkr = vjp((dqf + dqb, dkf + dkb))
        return dqr, dkr, dvf + dvb

    dq_r, dk_r, dv_r = _rowwise("ret_prep_bwd", ret_prep_bwd, S, ts, ret_ins + [R(t) for t in (dq_f, dq_b, dk_f, dk_b, dv_f, dv_b)],
                                [(512, MXU, 512, 0), (512, MXU, 512, 0), (1024, MXU, 1024, 0)])

    gW_mla_p = _mm("d_w_mla_out", o_bf, dy_a, "tn")
    do_bf, delta = _mm_rows("d_o_attn_delta", dy_a, wmla, lambda d, oo, *_: _delta_fn(oo.astype(F32), d), [o_bf], after_early, [MXU, F32], mode="nt")
    dq, dk, dv = _flash_bwd(q, k, v, do_bf, lse, delta)

    def mla_prep_bwd(cq, ckv, kr, cosm_, sinm_, gqa, gkva, gqn_, gkn_, wq_, wk_, wv_, dq_, dk_, dv_):
        f = lambda cq, ckv, kr, gqa, gkva, gqn_, gkn_, wq_, wk_, wv_: _mla_prep_fn(cq, ckv, kr, cosm_, sinm_, gqa, gkva, gqn_, gkn_, wq_, wk_, wv_)
        _, vjp = jax.vjp(f, cq, ckv, kr, gqa, gkva, gqn_, gkn_, wq_.astype(F32), wk_.astype(F32), wv_.astype(F32))
        return vjp((dq_, dk_, dv_))

    mb = _rowwise("mla_prep_bwd", mla_prep_bwd, S, ts, mla_ins + [R(dq), R(dk), R(dv)],
                  [(256, MXU, 256, 0), (128, MXU, 128, 0), (128, MXU, 128, 0)],
                  accs=[(1, 256), (1, 128), (1, LANES), (1, LANES), (256, HEADS * LANES), (128, HEADS * LANES), (128, HEADS * LANES)])
    dc_q, dc_kv, dk_rope, dg_q_a, dg_kv_a, dgqn_p, dgkn_p, dwq_p, dwk_p, dwv_p = mb

    dproj = jnp.concatenate([dga, dgb, dv_r, dg_r, dq_r, dk_r, dc_q, dc_kv, dk_rope], axis=1)
    gW["w_in"] = _win_unpad(_mm("d_w_in", h, dproj, "tn"))
    gW["w_mla_out"] = _wmla_unpad(gW_mla_p)
    after_mid = None if grad_hook is None else grad_hook({n: gW[n] for n in MID_GRADS})
    dh = _mm("d_h", dproj, win, "nt", after=after_mid)
    grad_x, _, dg_mix = _rowwise("rms_mix_bwd", lambda a, b, c, d, *_: rms_bwd(a, b, c, d), S, ts,
                                 [R(x), W_(g_mix), R(dh), R(dx1)] + ([] if after_mid is None else [W_(after_mid)]),
                                 [(D_MODEL, F32, D_MODEL, 0), (D_MODEL, MXU, D_MODEL, 0)], accs=[(1, D_MODEL)])
    gW["w_q_b"] = _wq_unpad(dwq_p)
    gW["w_kv_b"] = _wkv_unpad(dwk_p, dwv_p)
    gG = {"g_mix": dg_mix, "g_q_a": dg_q_a, "g_kv_a": dg_kv_a, "g_qn": _qk_unpad(dgqn_p),
          "g_kn": _qk_unpad(dgkn_p), "ret_decay_fwd": dlg_f[:, 0, 0][None, :], "ret_decay_bwd": dlg_b[:, 0, 0][None, :],
          "g_ffn": dg_ffn}
    return loss_rows, grad_x, gG, gW


MATS = [("w_in", (1024, 5536), 1), ("w_q_b", (256, 768), 1), ("w_kv_b", (128, 1024), 1), ("w_mla_out", (512, 1024), 1),
        ("w_ret_out", (1024, 1024), 0), ("w_out", (1024, 1024), 0), ("w_gate_up", (1024, 5632), 1), ("w_down", (2816, 1024), 0)]
GAINS = [("g_mix", 1024), ("g_q_a", 256), ("g_kv_a", 128), ("g_qn", 96), ("g_kn", 96), ("ret_decay_fwd", 8), ("ret_decay_bwd", 8),
         ("g_ffn", 1024)]
ORDER = ["g_mix", "w_in", "g_q_a", "w_q_b", "g_kv_a", "w_kv_b", "g_qn", "g_kn", "w_mla_out", "ret_decay_fwd", "ret_decay_bwd",
         "w_ret_out", "w_out", "g_ffn", "w_gate_up", "w_down"]
GAIN_LEN = sum(n for _, n in GAINS)
GAIN_PAD = -(-GAIN_LEN // LANES) * LANES


def _pack_gains(d):
    row = jnp.concatenate([d[n].reshape(1, ln).astype(F32) for n, ln in GAINS], axis=1)
    return jnp.pad(row, ((0, 0), (0, GAIN_PAD - GAIN_LEN)))


def _unpack_gains(row):
    out, off = {}, 0
    for n, ln in GAINS:
        out[n] = row[0, off:off + ln]
        off += ln
    return out


def _unshard(pieces, axis):
    if axis == 0:
        return pieces.reshape((N_DEV * pieces.shape[1], pieces.shape[2]))
    return jnp.concatenate([pieces[p] for p in range(N_DEV)], axis=1)


def _reshard(full, axis):
    if axis == 0:
        return full.reshape((N_DEV, full.shape[0] // N_DEV, full.shape[1]))
    c = full.shape[1] // N_DEV
    return jnp.stack([full[:, c * p:c * (p + 1)] for p in range(N_DEV)])


def _all_gather(shards):
    n = len(shards)

    def body(*refs):
        x_refs, out_refs = refs[:n], refs[n:2 * n]
        send_sems, recv_sems, local_sems = refs[2 * n:]
        x, y, c = lax.axis_index("x"), lax.axis_index("y"), lax.axis_index("c")
        me, sibling = (x, y, c), (x, y, 1 - c)
        chips = [(1 - x, y), (x, 1 - y), (1 - x, 1 - y)]

        def slot(a, px, py, pc):
            return out_refs[a].at[4 * px + 2 * py + pc]

        def copy(a, k, block, to, from_input=False):
            return pltpu.make_async_remote_copy(
                src_ref=x_refs[a] if from_input else slot(a, *block), dst_ref=slot(a, *block),
                send_sem=send_sems.at[a, k], recv_sem=recv_sems.at[a, k], device_id=to, device_id_type=pl.DeviceIdType.MESH)

        mine = [pltpu.make_async_copy(x_refs[a], slot(a, *me), local_sems.at[a]) for a in range(n)]
        first = [copy(a, 0, me, sibling, True) for a in range(n)]
        first += [copy(a, 1 + j, me, (*chip, c), True) for j, chip in enumerate(chips) for a in range(n)]
        for cp in mine + first:
            cp.start()
        passed = []
        for j, chip in enumerate(chips):
            for a in range(n):
                copy(a, 1 + j, (*chip, c), me).wait_recv()
                passed.append(copy(a, 4 + j, (*chip, c), sibling))
                passed[-1].start()
        for a in range(n):
            copy(a, 0, sibling, me).wait_recv()
        for j, chip in enumerate(chips):
            for a in range(n):
                copy(a, 4 + j, (*chip, 1 - c), me).wait_recv()
        for cp in first + passed:
            cp.wait_send()
        for cp in mine:
            cp.wait()

    any_spec = pl.BlockSpec(memory_space=pl.ANY)
    return pl.pallas_call(
        body, name="all_gather_weights", out_shape=[jax.ShapeDtypeStruct((N_DEV,) + s.shape, s.dtype) for s in shards],
        in_specs=[any_spec] * n, out_specs=[any_spec] * n,
        scratch_shapes=[pltpu.SemaphoreType.DMA((n, 7)), pltpu.SemaphoreType.DMA((n, 7)), pltpu.SemaphoreType.DMA((n,))],
    )(*shards)


def _all_to_all(pieces):
    n = len(pieces)

    def body(*refs):
        in_refs, out_refs = refs[:n], refs[n:2 * n]
        send_sems, recv_sems, local_sems = refs[2 * n:]
        x, y, c = lax.axis_index("x"), lax.axis_index("y"), lax.axis_index("c")
        my_id = 4 * x + 2 * y + c
        flips = [(fx, fy, fc) for fx in (0, 1) for fy in (0, 1) for fc in (0, 1)][1:]

        def copy(a, kk, f):
            p = (x ^ f[0], y ^ f[1], c ^ f[2])
            return pltpu.make_async_remote_copy(
                src_ref=in_refs[a].at[4 * p[0] + 2 * p[1] + p[2]], dst_ref=out_refs[a].at[my_id],
                send_sem=send_sems.at[a, kk], recv_sem=recv_sems.at[a, kk], device_id=p, device_id_type=pl.DeviceIdType.MESH)

        mine = [pltpu.make_async_copy(in_refs[a].at[my_id], out_refs[a].at[my_id], local_sems.at[a]) for a in range(n)]
        copies = [copy(a, kk, f) for kk, f in enumerate(flips) for a in range(n)]
        for cp in mine + copies:
            cp.start()
        for cp in copies:
            cp.wait_recv()
        for cp in copies:
            cp.wait_send()
        for cp in mine:
            cp.wait()

    any_spec = pl.BlockSpec(memory_space=pl.ANY)
    return pl.pallas_call(
        body, name="all_to_all_grads", out_shape=[jax.ShapeDtypeStruct(p.shape, p.dtype) for p in pieces],
        in_specs=[any_spec] * n, out_specs=[any_spec] * n,
        scratch_shapes=[pltpu.SemaphoreType.DMA((n, 7)), pltpu.SemaphoreType.DMA((n, 7)), pltpu.SemaphoreType.DMA((n,))],
    )(*pieces)


def _flip_peers(x, y, c):
    flips = [(fx, fy, fc) for fx in (0, 1) for fy in (0, 1) for fc in (0, 1)][1:]
    return [(x ^ fx, y ^ fy, c ^ fc) for fx, fy, fc in flips]


def _split_copies(in_refs, land_refs, send_sems, recv_sems, gather):
    x, y, c = lax.axis_index("x"), lax.axis_index("y"), lax.axis_index("c")
    my_id = 4 * x + 2 * y + c
    copies = []
    for kk, p in enumerate(_flip_peers(x, y, c)):
        for a in range(len(in_refs)):
            src = in_refs[a] if gather else in_refs[a].at[4 * p[0] + 2 * p[1] + p[2]]
            copies.append(pltpu.make_async_remote_copy(
                src_ref=src, dst_ref=land_refs[a].at[my_id], send_sem=send_sems.at[a * 7 + kk], recv_sem=recv_sems.at[a * 7 + kk],
                device_id=p, device_id_type=pl.DeviceIdType.MESH))
    return copies


def _exchange_start(name, srcs, gather, after=None):
    n = len(srcs)
    first_out = 2 * n + (0 if after is None else 1)

    def body(*refs):
        for cp in _split_copies(refs[:n], refs[n:2 * n], refs[first_out], refs[first_out + 1], gather):
            cp.start()
        refs[-1][...] = jnp.zeros_like(refs[-1])

    hbm, sem = pl.BlockSpec(memory_space=pltpu.HBM), pl.BlockSpec(memory_space=pltpu.SEMAPHORE)
    land_shapes = [((N_DEV,) + s.shape if gather else s.shape, s.dtype) for s in srcs]
    lands = [pltpu.with_memory_space_constraint(lax.empty(shp, dt), pltpu.HBM) for shp, dt in land_shapes]
    srcs = [pltpu.with_memory_space_constraint(s, pltpu.HBM) for s in srcs]
    res = pl.pallas_call(
        body, name=name,
        out_shape=[pltpu.SemaphoreType.DMA((7 * n,)), pltpu.SemaphoreType.DMA((7 * n,))] + [pltpu.HBM(s.shape, s.dtype) for s in srcs]
        + [pltpu.HBM(shp, dt) for shp, dt in land_shapes] + [jax.ShapeDtypeStruct((8, LANES), F32)],
        in_specs=[hbm] * (2 * n) + ([] if after is None else [pl.BlockSpec(memory_space=pl.ANY)]),
        out_specs=[sem, sem] + [hbm] * (2 * n) + [pl.BlockSpec(memory_space=pltpu.VMEM)],
        input_output_aliases={i: 2 + i for i in range(2 * n)},
        compiler_params=pltpu.CompilerParams(has_side_effects=pltpu.SideEffectType.DATAFLOW_SIDE_EFFECTING),
    )(*srcs, *lands, *([] if after is None else [after]))
    return res[0], res[1], res[2:2 + n], res[2 + n:2 + 2 * n], res[-1]


def _exchange_wait(name, handles, after, gather):
    send_sems, recv_sems, srcs, lands, _ = handles
    n = len(srcs)

    def body(*refs):
        for cp in _split_copies(refs[:n], refs[n:2 * n], refs[2 * n], refs[2 * n + 1], gather):
            cp.wait_send()
            cp.wait_recv()

    hbm, sem = pl.BlockSpec(memory_space=pltpu.HBM), pl.BlockSpec(memory_space=pltpu.SEMAPHORE)
    res = pl.pallas_call(
        body, name=name, out_shape=[pltpu.HBM(t.shape, t.dtype) for t in list(srcs) + list(lands)],
        in_specs=[hbm] * (2 * n) + [sem, sem, pl.BlockSpec(memory_space=pl.ANY)], out_specs=[hbm] * (2 * n),
        input_output_aliases={i: i for i in range(2 * n)},
        compiler_params=pltpu.CompilerParams(has_side_effects=pltpu.SideEffectType.DATAFLOW_SIDE_EFFECTING),
    )(*srcs, *lands, send_sems, recv_sems, after)
    my_id = 4 * lax.axis_index("x") + 2 * lax.axis_index("y") + lax.axis_index("c")
    own = [s if gather else lax.dynamic_index_in_dim(s, my_id, 0, keepdims=False) for s in res[:n]]
    return [lax.dynamic_update_index_in_dim(land, o, my_id, 0) for land, o in zip(res[n:], own)]


def _adamw(name, parts, w, m, v):
    rows, cols = w.shape
    tr = _pick(rows, (128, 64, 32, 16, 8))
    pspec = pl.BlockSpec((N_DEV, tr, cols), lambda i: (0, i, 0))
    rspec = pl.BlockSpec((tr, cols), lambda i: (i, 0))

    def body(p_ref, w_ref, m_ref, v_ref, g_ref, d_ref, m2_ref, v2_ref):
        g, d, m2, v2 = _adamw_fn([p_ref[s] for s in range(N_DEV)], w_ref[...], m_ref[...], v_ref[...])
        g_ref[...], d_ref[...], m2_ref[...], v2_ref[...] = g, d, m2, v2

    return pl.pallas_call(
        body, name=name, grid=(rows // tr,), in_specs=[pspec, rspec, rspec, rspec], out_specs=[rspec] * 4,
        out_shape=[jax.ShapeDtypeStruct((rows, cols), F32)] * 4,
        compiler_params=pltpu.CompilerParams(dimension_semantics=("parallel",), vmem_limit_bytes=VMEM_LIMIT),
    )(parts, w, m, v)


def kernel(x, positions, g_mix, w_in, g_q_a, w_q_b, g_kv_a, w_kv_b, g_qn, g_kn, w_mla_out, ret_decay_fwd, ret_decay_bwd, w_ret_out, w_out, g_ffn, w_gate_up, w_down, loss_target, m_g_mix, m_w_in, m_g_q_a, m_w_q_b, m_g_kv_a, m_w_kv_b, m_g_qn, m_g_kn, m_w_mla_out, m_ret_decay_fwd, m_ret_decay_bwd, m_w_ret_out, m_w_out, m_g_ffn, m_w_gate_up, m_w_down, v_g_mix, v_w_in, v_g_q_a, v_w_q_b, v_g_kv_a, v_w_kv_b, v_g_qn, v_g_kn, v_w_mla_out, v_ret_decay_fwd, v_ret_decay_bwd, v_w_ret_out, v_w_out, v_g_ffn, v_w_gate_up, v_w_down):
    w = dict(g_mix=g_mix, w_in=w_in, g_q_a=g_q_a, w_q_b=w_q_b, g_kv_a=g_kv_a, w_kv_b=w_kv_b, g_qn=g_qn, g_kn=g_kn, w_mla_out=w_mla_out,
             ret_decay_fwd=ret_decay_fwd, ret_decay_bwd=ret_decay_bwd, w_ret_out=w_ret_out, w_out=w_out, g_ffn=g_ffn,
             w_gate_up=w_gate_up, w_down=w_down)
    m = dict(g_mix=m_g_mix, w_in=m_w_in, g_q_a=m_g_q_a, w_q_b=m_w_q_b, g_kv_a=m_g_kv_a, w_kv_b=m_w_kv_b, g_qn=m_g_qn, g_kn=m_g_kn,
             w_mla_out=m_w_mla_out, ret_decay_fwd=m_ret_decay_fwd, ret_decay_bwd=m_ret_decay_bwd, w_ret_out=m_w_ret_out, w_out=m_w_out,
             g_ffn=m_g_ffn, w_gate_up=m_w_gate_up, w_down=m_w_down)
    v = dict(g_mix=v_g_mix, w_in=v_w_in, g_q_a=v_g_q_a, w_q_b=v_w_q_b, g_kv_a=v_g_kv_a, w_kv_b=v_w_kv_b, g_qn=v_g_qn, g_kn=v_g_kn,
             w_mla_out=v_w_mla_out, ret_decay_fwd=v_ret_decay_fwd, ret_decay_bwd=v_ret_decay_bwd, w_ret_out=v_w_ret_out, w_out=v_w_out,
             g_ffn=v_g_ffn, w_gate_up=v_w_gate_up, w_down=v_w_down)
    gains = {n: w[n].reshape(1, ln) for n, ln in GAINS}

    axis_of = {n: axis for n, _, axis in MATS}
    later = [n for n, _, _ in MATS if n not in FIRST_WEIGHTS]
    gathered = _all_gather([w[n].astype(WIRE) for n in FIRST_WEIGHTS])
    W = {n: _unshard(g, axis_of[n]) for n, g in zip(FIRST_WEIGHTS, gathered)}
    later_handles = _exchange_start("gather_later_start", [w[n].astype(WIRE) for n in later], True, after=gathered[0])

    def late_weights(after):
        lands = _exchange_wait("gather_later_wait", later_handles, after, True)
        return {n: _unshard(g, axis_of[n]) for n, g in zip(later, lands)}

    grad_groups = []

    def grad_hook(g):
        names = tuple(g)
        handles = _exchange_start("grads_start_%d" % len(grad_groups), [_reshard(g[n], axis_of[n]).astype(GWIRE) for n in names], False)
        grad_groups.append((names, handles))
        return handles[4]

    S = x.shape[1]
    pos = positions.reshape(S, 1).astype(F32)
    loss_rows, grad_x, gG, gW = _local_step(x.reshape(S, D_MODEL), pos, loss_target.reshape(S, D_MODEL), gains, W, late_weights, grad_hook,
                                            start_after=later_handles[4])
    loss = lax.psum(jnp.sum(loss_rows), ("x", "y", "c"))

    last = [n for n, _, _ in MATS if n not in EARLY_GRADS + MID_GRADS]
    pieces = [_reshard(gW[n], axis_of[n]).astype(GWIRE) for n in last]
    pieces.append(jnp.broadcast_to(_pack_gains(gG)[None], (N_DEV, 1, GAIN_PAD)))
    late_parts = _all_to_all(pieces)
    parts = dict(zip(last, late_parts))
    for i, (names, handles) in enumerate(grad_groups):
        parts.update(zip(names, _exchange_wait("grads_wait_%d" % i, handles, late_parts[-1], False)))
    out = [dict() for _ in range(4)]
    for n, _, _ in MATS:
        for o, r in zip(out, _adamw("adamw_" + n, parts[n], w[n], m[n], v[n])):
            o[n] = r
    for o, r in zip(out, _adamw("adamw_gains", late_parts[-1], _pack_gains(w), _pack_gains(m), _pack_gains(v))):
        o.update(_unpack_gains(r))
    return (loss, grad_x.reshape(x.shape), *[o[n] for o in out for n in ORDER])
```

```python
import functools

import numpy as np
import jax
import jax.numpy as jnp
from jax import lax
from jax.experimental import pallas as pl
from jax.experimental.pallas import tpu as pltpu

F32 = jnp.float32
MXU = jnp.bfloat16
WIRE = jnp.bfloat16
GWIRE = jnp.bfloat16

N_DEV = 8
D_MODEL = 1024
HEADS = 8
LANES = 128
Q_RANK, KV_RANK = 256, 128
NOPE, ROPE_M, V_M = 64, 32, 64
QK_M = NOPE + ROPE_M
RQK, RV = 64, 128
CHUNK = 128
FFN = 2816
IN_WIDTH = 5536
THETA = 10000.0
EPS = 1e-6
LR, B1, B2, AEPS, WD, STEP = 0.001, 0.9, 0.999, 1e-08, 0.01, 10
VMEM_LIMIT = 56 * 1024 * 1024

NN = ((1,), (0,))
NT = ((1,), (1,))
TN = ((0,), (0,))

P_GATES, P_VR, P_GR, P_QR, P_KR, P_CQ, P_CKV, P_KROPE, P_WIDTH = 0, 2048, 3072, 4096, 4608, 5120, 5376, 5504, 5632
O_CQ, O_CKV, O_KROPE, O_QR, O_KR, O_VR, O_GR, O_GATES = 0, 256, 384, 416, 928, 1440, 2464, 3488


def _dot(a, b, dims):
    return lax.dot_general(a, b, (dims, ((), ())), preferred_element_type=F32)


def _pick(dim, cands):
    for c in cands:
        if dim % c == 0:
            return c
    return dim


def _pairs(t):
    return t.reshape(t.shape[0], 4, 2, 2, 32).transpose(0, 1, 3, 2, 4).reshape(t.shape[0], 512)


def _win_pad(w):
    z = jnp.zeros((w.shape[0], 48), w.dtype)
    kr = w[:, O_KROPE:O_KROPE + 32]
    return jnp.concatenate([w[:, O_GATES:], w[:, O_VR:O_VR + 1024], w[:, O_GR:O_GR + 1024], _pairs(w[:, O_QR:O_QR + 512]),
                            _pairs(w[:, O_KR:O_KR + 512]), w[:, :O_CKV], w[:, O_CKV:O_KROPE], kr[:, :16], z, kr[:, 16:], z], axis=1)


def _win_unpad(g):
    return jnp.concatenate([g[:, P_CQ:P_CQ + 256], g[:, P_CKV:P_CKV + 128], g[:, P_KROPE:P_KROPE + 16], g[:, P_KROPE + 64:P_KROPE + 80],
                            _pairs(g[:, P_QR:P_QR + 512]), _pairs(g[:, P_KR:P_KR + 512]), g[:, P_VR:P_VR + 1024],
                            g[:, P_GR:P_GR + 1024], g[:, P_GATES:P_GATES + 2048]], axis=1)


def _qk_pad(t):
    z = jnp.zeros(t.shape[:-1] + (32,), t.dtype)
    return jnp.concatenate([t[..., 64:80], t[..., 0:48], t[..., 80:96], t[..., 48:64], z], axis=-1)


def _qk_unpad(p):
    return jnp.concatenate([p[..., 16:64], p[..., 80:96], p[..., 0:16], p[..., 64:80]], axis=-1)


def _wq_pad(w):
    return _qk_pad(w.reshape(Q_RANK, HEADS, QK_M)).reshape(Q_RANK, HEADS * LANES)


def _wq_unpad(g):
    return _qk_unpad(g.reshape(Q_RANK, HEADS, LANES)).reshape(Q_RANK, HEADS * QK_M)


def _wkv_pad(w):
    t = w.reshape(KV_RANK, HEADS, NOPE + V_M)
    z = lambda n: jnp.zeros((KV_RANK, HEADS, n), w.dtype)
    wk = jnp.concatenate([z(16), t[..., 0:48], z(16), t[..., 48:64], z(32)], axis=-1)
    wv = jnp.concatenate([t[..., 64:128], z(64)], axis=-1)
    return wk.reshape(KV_RANK, HEADS * LANES), wv.reshape(KV_RANK, HEADS * LANES)


def _wkv_unpad(dwk, dwv):
    k, v = dwk.reshape(KV_RANK, HEADS, LANES), dwv.reshape(KV_RANK, HEADS, LANES)
    return jnp.concatenate([k[..., 16:64], k[..., 80:96], v[..., 0:64]], axis=-1).reshape(KV_RANK, HEADS * (NOPE + V_M))


def _wmla_pad(w):
    t = w.reshape(HEADS, V_M, D_MODEL)
    return jnp.concatenate([t, jnp.zeros_like(t)], axis=1).reshape(HEADS * LANES, D_MODEL)


def _wmla_unpad(g):
    return g.reshape(HEADS, LANES, D_MODEL)[:, :V_M].reshape(HEADS * V_M, D_MODEL)


def _rowwise(name, fn, rows, ts, ins, outs, accs=(), ncol=1):
    n_in, n_out, n_acc = len(ins), len(outs), len(accs)

    def colmap(col):
        if callable(col):
            return lambda i, j: (i, col(j))
        return lambda i, j: (i, col)

    arrays, in_specs = [], []
    for arr, spec in ins:
        arrays.append(arr)
        if spec is None:
            in_specs.append(pl.BlockSpec(arr.shape, functools.partial(lambda i, j, nd: (0,) * nd, nd=arr.ndim)))
        else:
            in_specs.append(pl.BlockSpec((ts, spec[0]), colmap(spec[1])))
    out_shape, out_specs = [], []
    for total, dtype, width, col in outs:
        out_shape.append(jax.ShapeDtypeStruct((rows, total), dtype))
        out_specs.append(pl.BlockSpec((ts, width), colmap(col)))
    for shp in accs:
        out_shape.append(jax.ShapeDtypeStruct(shp, F32))
        out_specs.append(pl.BlockSpec(shp, functools.partial(lambda i, j, nd: (0,) * nd, nd=len(shp))))

    def body(*refs):
        vals = [r[...] for r in refs[:n_in]]
        res = fn(*vals)
        if not isinstance(res, (tuple, list)):
            res = (res,)
        for r, v in zip(refs[n_in:n_in + n_out], res[:n_out]):
            r[...] = v.astype(r.dtype)
        if n_acc:
            first = jnp.logical_and(pl.program_id(0) == 0, pl.program_id(1) == 0)
            for r, v in zip(refs[n_in + n_out:], res[n_out:]):
                @pl.when(first)
                def _(r=r):
                    r[...] = jnp.zeros_like(r)
                r[...] += v.astype(F32)

    res = pl.pallas_call(
        body, name=name, grid=(rows // ts, ncol), in_specs=in_specs, out_specs=out_specs, out_shape=out_shape,
        compiler_params=pltpu.CompilerParams(dimension_semantics=("arbitrary", "arbitrary"), vmem_limit_bytes=VMEM_LIMIT),
    )(*arrays)
    return res


MM_OPERAND_BYTES = 24 * 1024 * 1024


def _mm(name, a, b, mode, add=None, after=None):
    a_halves, b_halves = a.ndim == 3, b.ndim == 3
    assert not a_halves or mode == "nt"
    assert not b_halves or mode == "tn"
    if mode == "nn":
        (M, K), N = a.shape, b.shape[1]
    elif mode == "nt":
        M, K, N = a.shape[-2], a.shape[-1] * (2 if a_halves else 1), b.shape[0]
    else:
        (K, M), N = a.shape, b.shape[-1] * (2 if b_halves else 1)
    tm = _pick(M, (1024, 512, 1408, 256, 128))
    tn = _pick(N // 2 if b_halves else N, (1408, 1024, 512, 256, 128))
    fits = lambda t: 2 * (tm + tn) * t * a.dtype.itemsize <= MM_OPERAND_BYTES
    kdiv = K // 2 if a_halves else K
    tk = next(t for t in (K, 4096, 2816, 2048, 1408, 1024, 512, 256, 128) if kdiv % t == 0 and (fits(t) or t == 128))
    nk = K // tk
    dims = {"nn": NN, "nt": NT, "tn": TN}[mode]
    if a_halves:
        per = kdiv // tk
        a_spec = pl.BlockSpec((None, tm, tk), lambda i, j, k: (k // per, i, k % per))
    else:
        a_spec = pl.BlockSpec((tk, tm), lambda i, j, k: (k, i)) if mode == "tn" else pl.BlockSpec((tm, tk), lambda i, j, k: (i, k))
    if b_halves:
        perj = (N // 2) // tn
        b_spec = pl.BlockSpec((None, tk, tn), lambda i, j, k: (j // perj, k, j % perj))
    else:
        b_spec = pl.BlockSpec((tn, tk), lambda i, j, k: (j, k)) if mode == "nt" else pl.BlockSpec((tk, tn), lambda i, j, k: (k, j))
    o_spec = pl.BlockSpec((tm, tn), lambda i, j, k: (i, j))
    has_add = add is not None

    def body(*refs):
        a_ref, b_ref, o_ref = refs[0], refs[1], refs[-1]
        d = _dot(a_ref[...], b_ref[...], dims)
        first = (d + refs[2][...]) if has_add else d
        if nk == 1:
            o_ref[...] = first
        else:
            k = pl.program_id(2)

            @pl.when(k == 0)
            def _():
                o_ref[...] = first

            @pl.when(k > 0)
            def _():
                o_ref[...] += d

    args = [a, b] + ([add] if has_add else []) + ([] if after is None else [after])
    specs = [a_spec, b_spec] + ([o_spec] if has_add else []) + ([] if after is None else [pl.BlockSpec(memory_space=pl.ANY)])
    return pl.pallas_call(
        body, name=name, grid=(M // tm, N // tn, nk), in_specs=specs, out_specs=o_spec,
        out_shape=jax.ShapeDtypeStruct((M, N), F32),
        compiler_params=pltpu.CompilerParams(dimension_semantics=("parallel", "parallel", "arbitrary"), vmem_limit_bytes=VMEM_LIMIT),
    )(*args)


def _mm_rows(name, a, b, fn, row_ins, whole_ins, outs, accs=(), mode="nn"):
    (M, K), N = a.shape, b.shape[1 if mode == "nn" else 0]
    tm = _pick(M, (512, 256, 128))
    n_in, n_out = 2 + len(row_ins) + len(whole_ins), len(outs)
    windows = [t if isinstance(t, tuple) else (t, (t.shape[1], 0)) for t in row_ins]
    row_ins = [t for t, _ in windows]
    row_specs = [pl.BlockSpec((tm, w), functools.partial(lambda i, col: (i, col), col=col)) for _, (w, col) in windows]

    def body(*refs):
        d = _dot(refs[0][...], refs[1][...], NN if mode == "nn" else NT)
        res = fn(d, *[r[...] for r in refs[2:n_in]])
        for r, v in zip(refs[n_in:n_in + n_out], res[:n_out]):
            r[...] = v.astype(r.dtype)
        for r, v in zip(refs[n_in + n_out:], res[n_out:]):
            @pl.when(pl.program_id(0) == 0)
            def _(r=r):
                r[...] = jnp.zeros_like(r)
            r[...] += v

    row = pl.BlockSpec((tm, N), lambda i: (i, 0))
    whole = lambda t: pl.BlockSpec(t.shape, functools.partial(lambda i, nd: (0,) * nd, nd=t.ndim))
    return pl.pallas_call(
        body, name=name, grid=(M // tm,),
        in_specs=[pl.BlockSpec((tm, K), lambda i: (i, 0)), whole(b)] + row_specs + [whole(t) for t in whole_ins],
        out_specs=[row] * n_out + [pl.BlockSpec(s, functools.partial(lambda i, nd: (0,) * nd, nd=len(s))) for s in accs],
        out_shape=[jax.ShapeDtypeStruct((M, N), dt) for dt in outs] + [jax.ShapeDtypeStruct(s, F32) for s in accs],
        compiler_params=pltpu.CompilerParams(dimension_semantics=("arbitrary",), vmem_limit_bytes=VMEM_LIMIT),
    )(a, b, *row_ins, *whole_ins)


def _ffn_tiles(S):
    return _pick(S, (1024, 512, 256, 128)), _pick(FFN, (1408, 704, 256, 128))


def _gate_up_swiglu(h2, wgu):
    S, K = h2.shape
    tm, tn = _ffn_tiles(S)
    nj = FFN // tn

    def body(a_ref, bg_ref, bu_ref, gu_ref, act_ref):
        a = a_ref[...]
        g, u = _dot(a, bg_ref[...], NN), _dot(a, bu_ref[...], NN)
        gu_ref[0], gu_ref[1] = g.astype(gu_ref.dtype), u.astype(gu_ref.dtype)
        act_ref[...] = _swiglu_fn(g, u).astype(act_ref.dtype)

    return pl.pallas_call(
        body, name="gate_up_swiglu", grid=(S // tm, nj),
        in_specs=[pl.BlockSpec((tm, K), lambda i, j: (i, 0)), pl.BlockSpec((K, tn), lambda i, j: (0, j)),
                  pl.BlockSpec((K, tn), lambda i, j: (0, nj + j))],
        out_specs=[pl.BlockSpec((2, tm, tn), lambda i, j: (0, i, j)), pl.BlockSpec((tm, tn), lambda i, j: (i, j))],
        out_shape=[jax.ShapeDtypeStruct((2, S, FFN), MXU), jax.ShapeDtypeStruct((S, FFN), MXU)],
        compiler_params=pltpu.CompilerParams(dimension_semantics=("parallel", "parallel"), vmem_limit_bytes=VMEM_LIMIT),
    )(h2, wgu, wgu)


def _d_act_swiglu(dx2, wdown, gu):
    S, K = dx2.shape
    tm, tn = _ffn_tiles(S)

    def body(a_ref, b_ref, gu_ref, o_ref):
        dact = _dot(a_ref[...], b_ref[...], NT)
        _, vjp = jax.vjp(_swiglu_fn, gu_ref[0].astype(F32), gu_ref[1].astype(F32))
        dg, du = vjp(dact)
        o_ref[0], o_ref[1] = dg.astype(o_ref.dtype), du.astype(o_ref.dtype)

    stacked = pl.BlockSpec((2, tm, tn), lambda i, j: (0, i, j))
    return pl.pallas_call(
        body, name="d_act_swiglu", grid=(S // tm, FFN // tn),
        in_specs=[pl.BlockSpec((tm, K), lambda i, j: (i, 0)), pl.BlockSpec((tn, K), lambda i, j: (j, 0)), stacked],
        out_specs=stacked, out_shape=jax.ShapeDtypeStruct((2, S, FFN), MXU),
        compiler_params=pltpu.CompilerParams(dimension_semantics=("parallel", "parallel"), vmem_limit_bytes=VMEM_LIMIT),
    )(dx2, wdown, gu)


@jax.custom_vjp
def _swap64(x):
    return pltpu.roll(x, 64, 1)


_swap64.defvjp(lambda x: (_swap64(x), None), lambda _, g: (_swap64(g),))


@jax.custom_vjp
def _mxdot(a, b):
    return _dot(a.astype(MXU), b.astype(MXU), NN)


def _mxdot_bwd(res, g):
    a, b = res
    gb = g.astype(MXU)
    return _dot(gb, b.astype(MXU), NT), _dot(a.astype(MXU), gb, TN)


_mxdot.defvjp(lambda a, b: (_mxdot(a, b), (a, b)), _mxdot_bwd)


def _row_sum(t):
    if t.shape[-1] == LANES:
        return lax.dot_general(t, jnp.ones((LANES, LANES), F32), ((NN), ((), ())), precision=lax.Precision.HIGH,
                               preferred_element_type=F32)
    return jnp.sum(t, axis=-1, keepdims=True)


@functools.partial(jax.custom_vjp, nondiff_argnums=(1,))
def _unit_rms(x, n):
    return x * lax.rsqrt(_row_sum(x * x) * (1.0 / n) + EPS)


def _unit_rms_fwd(x, n):
    r = lax.rsqrt(_row_sum(x * x) * (1.0 / n) + EPS)
    y = x * r
    return y, (y, r)


def _unit_rms_bwd(n, res, g):
    y, r = res
    return (r * (g - y * (_row_sum(g * y) * (1.0 / n))),)


_unit_rms.defvjp(_unit_rms_fwd, _unit_rms_bwd)


def _rms(x):
    return _unit_rms(x, x.shape[-1])


def _rmsg_fn(x, g):
    return _rms(x) * g


def _silu(x):
    return x * jax.nn.sigmoid(x)


def _tables_fn(pos, inv_m, sgn_m, inv_r, sgn_r):
    am, ar = pos * inv_m, pos * inv_r
    return jnp.cos(am), jnp.sin(am) * sgn_m, jnp.cos(ar), jnp.sin(ar) * sgn_r


def _head_blocks(t):
    return [t[:, LANES * h:LANES * (h + 1)] for h in range(t.shape[1] // LANES)]


def _mla_prep_fn(cq, ckv, kr, cosm, sinm, gqa, gkva, gqn, gkn, wq, wk, wv):
    cqn = _rms(cq) * gqa
    ckvn = _rms(ckv) * gkva
    q_raw = _mxdot(cqn, wq)
    k_raw = _mxdot(ckvn, wk)
    lane = lax.broadcasted_iota(jnp.int32, (1, HEADS * LANES), 1)
    v = _mxdot(ckvn, wv) + (lane % LANES == V_M).astype(F32)

    def norm_rope(blocks, g, extra):
        outs = []
        for b in blocks:
            if extra is not None:
                b = b + extra
            n = _unit_rms(b, QK_M) * g
            outs.append(n * cosm + _swap64(n) * sinm)
        return jnp.concatenate(outs, axis=1)

    q = norm_rope(_head_blocks(q_raw), gqn, None)
    k = norm_rope(_head_blocks(k_raw), gkn, kr)
    return q, k, v


def _ret_prep_fn(qr, kr, cosr, sinr):
    def rope(t, scale):
        return jnp.concatenate([(b * cosr + _swap64(b) * sinr) * scale for b in _head_blocks(t)], axis=1)
    return rope(qr, 1.0), rope(kr, RQK ** -0.5)


def _ret_post_fn(rf, rb, gr):
    ret = rf + rb
    outs = []
    for b, g in zip(_head_blocks(ret), _head_blocks(gr)):
        outs.append(_silu(g) * _rms(b))
    return jnp.concatenate(outs, axis=1)


def _merge_fn(ga, gb, ya, yb):
    return jax.nn.sigmoid(ga) * ya + jax.nn.sigmoid(gb) * yb


def _swiglu_fn(gate, up):
    return _silu(gate) * up


def _loss_fn(x2, tgt):
    d = x2 - tgt
    return d * (1.0 / D_MODEL), 0.5 * jnp.sum(d * d, axis=0, keepdims=True) * (1.0 / D_MODEL)


def _adamw_fn(parts, w, m, v):
    g = parts[0].astype(F32)
    for p in range(1, N_DEV):
        g = g + parts[p].astype(F32)
    m2 = B1 * m + (1.0 - B1) * g
    v2 = B2 * v + (1.0 - B2) * jnp.square(g)
    m_hat = m2 / (1.0 - B1 ** STEP)
    v_hat = v2 / (1.0 - B2 ** STEP)
    delta = -LR * (m_hat / (jnp.sqrt(v_hat) + AEPS) + WD * w)
    return g, delta, m2, v2


SCALE = QK_M ** -0.5
LOG2E = 1.4426950408889634
FLASH_ROWS = 32


def _flash_fwd(q, k, v):
    S = q.shape[0]
    tk = _pick(S, (512, 256, 128))
    tq = _pick(S, (1024, 512, 256, 128))
    ncb = tk // LANES
    nkv = S // tk
    assert nkv % 2 == 0, "kv tiles are processed in pairs"
    mrows = 64
    c = SCALE * LOG2E

    def body(q_ref, k_ref, v_ref, o_ref, lse_ref, s_a, p_a, s_b, p_b, m_sc, a_sc, acc_sc):
        m_sc[...] = jnp.full_like(m_sc, -jnp.inf)
        acc_sc[...] = jnp.zeros_like(acc_sc)
        qb = q_ref[...]

        def scores(j, s_buf):
            s_buf[...] = _dot(qb, k_ref[pl.ds(pl.multiple_of(j * tk, tk), tk), :], NT)

        def stage(j, s_buf, p_buf, s_next):
            scores(jnp.minimum(j + 1, nkv - 1), s_next)
            for r in range(tq // mrows):
                rows = slice(r * mrows, (r + 1) * mrows)
                cols = [s_buf[rows, LANES * cb:LANES * (cb + 1)] for cb in range(ncb)]
                m_prev = m_sc[rows, :]
                row_max = jnp.max(functools.reduce(jnp.maximum, cols), axis=-1, keepdims=True)
                m_new = jnp.maximum(m_prev, jnp.broadcast_to(row_max, (mrows, LANES)))
                a_sc[rows, :] = jnp.exp2((m_prev - m_new) * c)
                m_sc[rows, :] = m_new
                for cb in range(ncb):
                    p_buf[rows, LANES * cb:LANES * (cb + 1)] = jnp.exp2((cols[cb] - m_new) * c).astype(p_buf.dtype)
            acc_sc[...] = a_sc[...] * acc_sc[...] + _dot(p_buf[...], v_ref[pl.ds(pl.multiple_of(j * tk, tk), tk), :], NN)

        scores(0, s_a)

        def pair_step(t, carry):
            stage(2 * t, s_a, p_a, s_b)
            stage(2 * t + 1, s_b, p_b, s_a)
            return carry

        lax.fori_loop(0, nkv // 2, pair_step, 0, unroll=4)
        acc = acc_sc[...]
        lane = lax.broadcasted_iota(jnp.int32, (1, LANES), 1)
        l = jnp.sum(jnp.where(lane == V_M, acc, 0.0), axis=-1, keepdims=True)
        o_ref[...] = (acc / l).astype(o_ref.dtype)
        lse_ref[...] = m_sc[...] * c + jnp.log2(jnp.broadcast_to(l, (tq, LANES)))

    qspec = pl.BlockSpec((tq, LANES), lambda h, i: (i, h))
    kspec = pl.BlockSpec((S, LANES), lambda h, i: (0, h))
    return pl.pallas_call(
        body, name="flash_fwd", grid=(HEADS, S // tq), in_specs=[qspec, kspec, kspec], out_specs=[qspec, qspec],
        out_shape=[jax.ShapeDtypeStruct((S, HEADS * LANES), MXU), jax.ShapeDtypeStruct((S, HEADS * LANES), F32)],
        scratch_shapes=[pltpu.VMEM((tq, tk), F32), pltpu.VMEM((tq, tk), MXU)] * 2 + [pltpu.VMEM((tq, LANES), F32)] * 3,
        compiler_params=pltpu.CompilerParams(dimension_semantics=("parallel", "arbitrary"), vmem_limit_bytes=VMEM_LIMIT),
    )(q, k, v)


def _delta_fn(o, do):
    outs = [jnp.broadcast_to(jnp.sum(a * b, axis=-1, keepdims=True), a.shape) for a, b in zip(_head_blocks(o), _head_blocks(do))]
    return do, jnp.concatenate(outs, axis=1)


def _flash_bwd(q, k, v, do, lse, delta):
    S = q.shape[0]
    tq = tk = _pick(S, (512, 256, 128))
    ncb = tk // LANES
    c = SCALE * LOG2E

    nq = S // tq
    assert nq % 2 == 0, "q tiles are processed in pairs"

    def body(q_ref, k_ref, v_ref, do_ref, lse_ref, dl_ref, dq_ref, dk_ref, dv_ref, s_a, dp_a, p_a, ds_a, s_b, dp_b, p_b, ds_b):
        @pl.when(pl.program_id(1) == 0)
        def _():
            dq_ref[...] = jnp.zeros_like(dq_ref)

        dk_ref[...] = jnp.zeros_like(dk_ref)
        dv_ref[...] = jnp.zeros_like(dv_ref)
        kb, vb = k_ref[...], v_ref[...]

        def scores(i, s_buf, dp_buf):
            q_rows = pl.ds(pl.multiple_of(i * tq, tq), tq)
            s_buf[...] = _dot(q_ref[q_rows, :], kb, NT)
            dp_buf[...] = _dot(do_ref[q_rows, :], vb, NT)

        def stage(i, s_buf, dp_buf, p_buf, ds_buf, s_next, dp_next):
            scores(jnp.minimum(i + 1, nq - 1), s_next, dp_next)
            for r in range(tq // FLASH_ROWS):
                rows = slice(r * FLASH_ROWS, (r + 1) * FLASH_ROWS)
                grows = pl.ds(pl.multiple_of(i * tq + r * FLASH_ROWS, FLASH_ROWS), FLASH_ROWS)
                lse_b, dl_b = lse_ref[grows, :], dl_ref[grows, :]
                for cb in range(ncb):
                    sl = slice(LANES * cb, LANES * (cb + 1))
                    p = jnp.exp2(s_buf[rows, sl] * c - lse_b)
                    p_buf[rows, sl] = p.astype(p_buf.dtype)
                    ds_buf[rows, sl] = (p * (dp_buf[rows, sl] - dl_b) * SCALE).astype(ds_buf.dtype)
            q_rows = pl.ds(pl.multiple_of(i * tq, tq), tq)
            dv_ref[...] += _dot(p_buf[...], do_ref[q_rows, :], TN)
            dk_ref[...] += _dot(ds_buf[...], q_ref[q_rows, :], TN)
            dq_ref[q_rows, :] += _dot(ds_buf[...], kb, NN)

        scores(0, s_a, dp_a)

        def pair_step(t, carry):
            stage(2 * t, s_a, dp_a, p_a, ds_a, s_b, dp_b)
            stage(2 * t + 1, s_b, dp_b, p_b, ds_b, s_a, dp_a)
            return carry

        lax.fori_loop(0, nq // 2, pair_step, 0, unroll=2)

    hspec = pl.BlockSpec((S, LANES), lambda h, j: (0, h))
    kspec = pl.BlockSpec((tk, LANES), lambda h, j: (j, h))
    full = jax.ShapeDtypeStruct((S, HEADS * LANES), F32)
    tile_bufs = [pltpu.VMEM((tq, tk), F32), pltpu.VMEM((tq, tk), F32), pltpu.VMEM((tq, tk), MXU), pltpu.VMEM((tq, tk), MXU)]
    return pl.pallas_call(
        body, name="flash_bwd", grid=(HEADS, S // tk), in_specs=[hspec, kspec, kspec, hspec, hspec, hspec],
        out_specs=[hspec, kspec, kspec], out_shape=[full, full, full],
        scratch_shapes=tile_bufs + tile_bufs,
        compiler_params=pltpu.CompilerParams(dimension_semantics=("parallel", "arbitrary"), vmem_limit_bytes=VMEM_LIMIT),
    )(q, k, v, do, lse, delta)


def _ret_consts(lgh, head, rev):
    C = CHUNK
    lane = lax.broadcasted_iota(jnp.int32, (1, LANES), 1)
    hm = ((lane // 32) % 2 == head % 2).astype(F32)
    r = lax.broadcasted_iota(jnp.int32, (C, C), 0)
    c = lax.broadcasted_iota(jnp.int32, (C, C), 1)
    diff = ((c - r) if rev else (r - c)).astype(F32)
    mask = (diff > 0) if rev else (diff >= 0)
    dpos = jnp.maximum(diff, 0.0)
    din = jnp.where(mask, jnp.exp(lgh * dpos), 0.0)
    idx = lax.broadcasted_iota(jnp.int32, (C, 1), 0).astype(F32)
    eq = (C - idx) if rev else (idx + 1.0)
    ek = idx if rev else (C - 1.0 - idx)
    qd, kd = jnp.exp(lgh * eq), jnp.exp(lgh * ek)
    cd = jnp.exp(lgh * jnp.full((1, 1), float(C), F32))
    return hm, din, dpos, qd, kd, cd, eq, ek


RET_HEADS_PER_STEP = 4


def _ret_fwd(name, qt, kt, proj, lg, rev):
    S = qt.shape[0]
    C = CHUNK
    TB = _pick(S, (512, 256, 128))
    cb, nb = TB // C, S // TB
    hps = RET_HEADS_PER_STEP
    blk = (lambda g: nb - 1 - g) if rev else (lambda g: g)

    def body(lg_ref, q_ref, k_ref, v_ref, o_ref, st_ref, state_sc):
        hg, g = pl.program_id(0), pl.program_id(1)

        @pl.when(g == 0)
        def _():
            state_sc[...] = jnp.zeros_like(state_sc)

        consts = [_ret_consts(lg_ref[hg * hps + u], u, rev) for u in range(hps)]
        order = list(reversed(range(cb))) if rev else list(range(cb))
        units = [(cc, u) for cc in order for u in range(hps)]

        def operands(cc, u):
            rows = pl.ds(cc * C, C)
            pair = slice(LANES * (u // 2), LANES * (u // 2 + 1))
            hm = consts[u][0]
            return q_ref[rows, pair] * hm, k_ref[rows, pair] * hm, v_ref[rows, LANES * u:LANES * (u + 1)].astype(MXU)

        a, inc = {}, {}
        for cc, u in units:
            q, k, v = operands(cc, u)
            a[cc, u] = _dot(q.astype(MXU), k.astype(MXU), NT) * consts[u][1]
            inc[cc, u] = _dot((k * consts[u][4]).astype(MXU), v, TN)
        for u in range(hps):
            st = state_sc[u]
            for cc in order:
                st_ref[u, cc] = st
                st = st * consts[u][5] + inc[cc, u]
            state_sc[u] = st
        for cc, u in units:
            q, _, v = operands(cc, u)
            cross = _dot((q * consts[u][3]).astype(MXU), st_ref[u, cc].astype(MXU), NN)
            o_ref[pl.ds(cc * C, C), LANES * u:LANES * (u + 1)] = _dot(a[cc, u].astype(MXU), v, NN) + cross

    qk_spec = pl.BlockSpec((TB, LANES * hps // 2), lambda h, g: (blk(g), h))
    return pl.pallas_call(
        body, name=name, grid=(HEADS // hps, nb),
        in_specs=[pl.BlockSpec(memory_space=pltpu.SMEM), qk_spec, qk_spec,
                  pl.BlockSpec((TB, LANES * hps), lambda h, g: (blk(g), P_VR // (LANES * hps) + h))],
        out_specs=[pl.BlockSpec((TB, LANES * hps), lambda h, g: (blk(g), h)),
                   pl.BlockSpec((hps, cb, LANES, LANES), lambda h, g: (h, blk(g), 0, 0))],
        out_shape=[jax.ShapeDtypeStruct((S, HEADS * LANES), F32), jax.ShapeDtypeStruct((HEADS, S // C, LANES, LANES), F32)],
        scratch_shapes=[pltpu.VMEM((hps, LANES, LANES), F32)],
        compiler_params=pltpu.CompilerParams(dimension_semantics=("parallel", "arbitrary"), vmem_limit_bytes=VMEM_LIMIT),
    )(lg, qt, kt, proj)


def _ret_bwd(name, qt, kt, proj, dret, states, lg, rev):
    S = qt.shape[0]
    C = CHUNK
    TB = _pick(S, (512, 256, 128))
    cb, nb = TB // C, S // TB
    hps = RET_HEADS_PER_STEP
    blk = (lambda g: g) if rev else (lambda g: nb - 1 - g)

    def body(lg_ref, q_ref, k_ref, v_ref, do_ref, st_ref, dq_ref, dk_ref, dv_ref, dlg_ref, ds_sc, acc_cc, acc_q, acc_k, acc_s):
        hg, g = pl.program_id(0), pl.program_id(1)

        @pl.when(g == 0)
        def _():
            ds_sc[...] = jnp.zeros_like(ds_sc)
            acc_cc[...] = jnp.zeros_like(acc_cc)
            acc_q[...] = jnp.zeros_like(acc_q)
            acc_k[...] = jnp.zeros_like(acc_k)
            acc_s[...] = jnp.zeros_like(acc_s)

        lgs = [lg_ref[hg * hps + u] for u in range(hps)]
        consts = [_ret_consts(lgs[u], u, rev) for u in range(hps)]
        order = list(range(cb)) if rev else list(reversed(range(cb)))
        units = [(cc, u) for cc in order for u in range(hps)]

        def operands(cc, u):
            rows = pl.ds(cc * C, C)
            pair = slice(LANES * (u // 2), LANES * (u // 2 + 1))
            head = slice(LANES * u, LANES * (u + 1))
            hm = consts[u][0]
            return q_ref[rows, pair] * hm, k_ref[rows, pair] * hm, v_ref[rows, head].astype(MXU), do_ref[rows, head].astype(MXU)

        a, dp, dqs, inc = {}, {}, {}, {}
        for cc, u in units:
            q, k, vb, dob = operands(cc, u)
            a[cc, u] = _dot(q.astype(MXU), k.astype(MXU), NT)
            dp[cc, u] = _dot(dob, vb, NT)
            dqs[cc, u] = _dot(dob, st_ref[u, cc].astype(MXU), NT)
            inc[cc, u] = _dot((q * consts[u][3]).astype(MXU), dob, TN)
        dsn = {}
        for u in range(hps):
            ds = ds_sc[u]
            for cc in order:
                dsn[cc, u] = ds
                ds = ds * consts[u][5] + inc[cc, u]
            ds_sc[u] = ds
        even = {}
        for cc, u in units:
            hm, din, dpos, qd, kd, cd, eq, ek = consts[u]
            rows, head = pl.ds(cc * C, C), slice(LANES * u, LANES * (u + 1))
            q, k, vb, dob = operands(cc, u)
            qb, kb = q.astype(MXU), k.astype(MXU)
            dsnb = dsn[cc, u].astype(MXU)
            da = (dp[cc, u] * din).astype(MXU)
            vds = _dot(vb, dsnb, NT)
            dq_u = (_dot(da, kb, NN) + dqs[cc, u] * qd) * hm
            dk_u = (_dot(da, qb, TN) + vds * kd) * hm
            if u % 2 == 0:
                even[cc] = (dq_u, dk_u)
            else:
                pair = slice(LANES * (u // 2), LANES * (u // 2 + 1))
                dq_ref[rows, pair] = even[cc][0] + dq_u
                dk_ref[rows, pair] = even[cc][1] + dk_u
            dv_ref[rows, head] = _dot((a[cc, u] * din).astype(MXU), dob, TN) + _dot((k * kd).astype(MXU), dsnb, NN)
            acc_cc[u] += dp[cc, u] * a[cc, u] * din * dpos
            acc_q[u] += dqs[cc, u] * q * (qd * eq)
            acc_k[u] += vds * k * (kd * ek)
            acc_s[u] += dsn[cc, u] * st_ref[u, cc] * (cd * float(C))

        @pl.when(g == nb - 1)
        def _():
            for u in range(hps):
                tot = (jnp.sum(acc_cc[u], keepdims=True) + jnp.sum(acc_q[u], keepdims=True)
                       + jnp.sum(acc_k[u], keepdims=True) + jnp.sum(acc_s[u], keepdims=True))
                dlg_ref[u] = jnp.broadcast_to(tot * lgs[u], (8, LANES))

    full = jax.ShapeDtypeStruct((S, HEADS * LANES), F32)
    hspec = pl.BlockSpec((TB, LANES * hps), lambda h, g: (blk(g), h))
    qk_spec = pl.BlockSpec((TB, LANES * hps // 2), lambda h, g: (blk(g), h))
    return pl.pallas_call(
        body, name=name, grid=(HEADS // hps, nb),
        in_specs=[pl.BlockSpec(memory_space=pltpu.SMEM), qk_spec, qk_spec,
                  pl.BlockSpec((TB, LANES * hps), lambda h, g: (blk(g), P_VR // (LANES * hps) + h)),
                  hspec,
                  pl.BlockSpec((hps, cb, LANES, LANES), lambda h, g: (h, blk(g), 0, 0))],
        out_specs=[qk_spec, qk_spec, hspec, pl.BlockSpec((hps, 8, LANES), lambda h, g: (h, 0, 0))],
        out_shape=[jax.ShapeDtypeStruct(qt.shape, F32), jax.ShapeDtypeStruct(kt.shape, F32), full,
                   jax.ShapeDtypeStruct((HEADS, 8, LANES), F32)],
        scratch_shapes=[pltpu.VMEM((hps, LANES, LANES), F32), pltpu.VMEM((hps, C, C), F32), pltpu.VMEM((hps, C, LANES), F32),
                        pltpu.VMEM((hps, C, LANES), F32), pltpu.VMEM((hps, LANES, LANES), F32)],
        compiler_params=pltpu.CompilerParams(dimension_semantics=("parallel", "arbitrary"), vmem_limit_bytes=VMEM_LIMIT),
    )(lg, qt, kt, proj, dret, states)


def _rope_consts():
    inv16 = THETA ** (-jnp.arange(16, dtype=F32) / 16)
    inv32 = THETA ** (-jnp.arange(32, dtype=F32) / 32)
    lane = np.arange(LANES)
    z48 = jnp.zeros((48,), F32)
    inv_m = jnp.concatenate([inv16, z48, inv16, z48])[None, :]
    sgn_m = jnp.asarray(np.where(lane < 16, -1.0, np.where((lane >= 64) & (lane < 80), 1.0, 0.0)), F32)[None, :]
    inv_r = jnp.concatenate([inv32] * 4)[None, :]
    sgn_r = jnp.asarray(np.where(lane < 64, -1.0, 1.0), F32)[None, :]
    return inv_m, sgn_m, inv_r, sgn_r


FIRST_WEIGHTS = ("w_in", "w_q_b", "w_kv_b")
EARLY_GRADS = ("w_down", "w_gate_up", "w_out", "w_ret_out")
MID_GRADS = ("w_mla_out", "w_in")


def _local_step(x, pos, tgt, gains, W, late_weights=None, grad_hook=None, start_after=None):
    S = x.shape[0]
    ts = _pick(S, (256, 128))
    ts_light = _pick(S, (512, 256, 128))
    R = lambda a, w=None, c=0: (a, ((a.shape[1] if w is None else w), c))
    W_ = lambda a: (a, None)

    win = _win_pad(W["w_in"])
    wq = _wq_pad(W["w_q_b"])
    wk, wv = _wkv_pad(W["w_kv_b"])
    gqn, gkn = _qk_pad(gains["g_qn"]), _qk_pad(gains["g_kn"])
    g_mix, g_q_a, g_kv_a, g_ffn = gains["g_mix"], gains["g_q_a"], gains["g_kv_a"], gains["g_ffn"]
    lg_f = -jnp.exp(gains["ret_decay_fwd"][0])
    lg_b = -jnp.exp(gains["ret_decay_bwd"][0])

    consts = list(_rope_consts())
    cosm, sinm, cosr, sinr = _rowwise("rope_tables", _tables_fn, S, ts_light,[R(pos)] + [W_(c) for c in consts],
                                      [(LANES, F32, LANES, 0)] * 4)

    (h,) = _rowwise("rms_mix", _rmsg_fn, S, ts_light,[R(x), W_(g_mix)], [(D_MODEL, MXU, D_MODEL, 0)])
    proj = _mm("in_proj", h, win, "nn", after=start_after)
    seg = lambda off, w: (proj, (w, off // w))
    mla_ins = [seg(P_CQ, 256), seg(P_CKV, 128), seg(P_KROPE, 128), R(cosm), R(sinm),
               W_(g_q_a), W_(g_kv_a), W_(gqn), W_(gkn), W_(wq), W_(wk), W_(wv)]
    q, k, v = _rowwise("mla_prep", _mla_prep_fn, S, ts, mla_ins, [(HEADS * LANES, MXU, HEADS * LANES, 0)] * 3)
    o_bf, lse = _flash_fwd(q, k, v)
    if late_weights is not None:
        W = {**W, **late_weights(lse)}
    wmla = _wmla_pad(W["w_mla_out"])
    wret, wout, wgu, wdown = W["w_ret_out"], W["w_out"], W["w_gate_up"], W["w_down"]
    y_a = _mm("mla_out", o_bf, wmla, "nn")

    ret_ins = [seg(P_QR, 512), seg(P_KR, 512), R(cosr), R(sinr)]
    qt, kt = _rowwise("ret_prep", _ret_prep_fn, S, ts_light,ret_ins, [(512, F32, 512, 0)] * 2)
    ret_f, st_f = _ret_fwd("ret_fwd_f", qt, kt, proj, lg_f, False)
    ret_b, st_b = _ret_fwd("ret_fwd_b", qt, kt, proj, lg_b, True)
    post_ins = [R(ret_f), R(ret_b), seg(P_GR, 1024)]
    (o_b,) = _rowwise("ret_post", _ret_post_fn, S, ts_light,post_ins, [(1024, MXU, 1024, 0)])
    y_b, merged = _mm_rows("ret_out_merge", o_b, wret, lambda yb, ga, gb, ya: (yb, _merge_fn(ga, gb, ya, yb)),
                           [seg(P_GATES, 1024), (proj, (1024, 1)), R(y_a)], [], [F32, MXU])
    merge_ins = [seg(P_GATES, 1024), (proj, (1024, 1)), R(y_a), R(y_b)]
    def residual_rms(d, xx, g):
        r = d + xx
        return r, _rmsg_fn(r, g)

    x1, h2 = _mm_rows("out_proj_rms_ffn", merged, wout, residual_rms, [x], [g_ffn], [F32, MXU])
    gu, act = _gate_up_swiglu(h2, wgu)

    def residual_loss(d, xx, t):
        dx, rows = _loss_fn(d + xx, t)
        return dx, dx, rows

    dx2, dx2_bf, loss_rows = _mm_rows("down_proj_loss", act, wdown, residual_loss, [x1, tgt], [], [F32, MXU], accs=[(1, D_MODEL)])

    gW = {}
    gW["w_down"] = _mm("d_w_down", act, dx2_bf, "tn")
    dgu = _d_act_swiglu(dx2_bf, wdown, gu)
    gW["w_gate_up"] = _mm("d_w_gate_up", h2, dgu, "tn")
    dh2 = _mm("d_h2", dgu, wgu, "nt")

    def rms_bwd(xx, g, dh, dres):
        _, vjp = jax.vjp(_rmsg_fn, xx, g)
        dx, dg = vjp(dh)
        dx = dx + dres
        return dx, dx, dg

    dx1, dx1_bf, dg_ffn = _rowwise("rms_ffn_bwd", rms_bwd, S, ts_light,[R(x1), W_(g_ffn), R(dh2), R(dx2)],
                                   [(D_MODEL, F32, D_MODEL, 0), (D_MODEL, MXU, D_MODEL, 0)], accs=[(1, D_MODEL)])
    gW["w_out"] = _mm("d_w_out", merged, dx1_bf, "tn")
    def merge_bwd(dm, ga, gb, ya, yb):
        _, vjp = jax.vjp(_merge_fn, ga, gb, ya, yb)
        return vjp(dm)

    dga, dgb, dy_a, dy_b = _mm_rows("d_merged_merge_bwd", dx1_bf, wout, merge_bwd, merge_ins, [], [MXU] * 4, mode="nt")
    gW["w_ret_out"] = _mm("d_w_ret_out", o_b, dy_b, "tn")
    after_early = [] if grad_hook is None else [grad_hook({n: gW[n] for n in EARLY_GRADS})]

    def post_bwd(dob, rf, rb, gr, *_):
        _, vjp = jax.vjp(_ret_post_fn, rf, rb, gr)
        drf, _, dgr = vjp(dob)
        return drf, dgr

    dret, dg_r = _mm_rows("d_o_b_ret_post_bwd", dy_b, wret, post_bwd, post_ins, after_early, [F32, MXU], mode="nt")
    dq_f, dk_f, dv_f, dlg_f = _ret_bwd("ret_bwd_f", qt, kt, proj, dret, st_f, lg_f, False)
    dq_b, dk_b, dv_b, dlg_b = _ret_bwd("ret_bwd_b", qt, kt, proj, dret, st_b, lg_b, True)

    def ret_prep_bwd(qr, kr, cosr_, sinr_, dqf, dqb, dkf, dkb, dvf, dvb):
        _, vjp = jax.vjp(lambda a, b: _ret_prep_fn(a, b, cosr_, sinr_), qr, kr)
        dqr, dkr = vjp((dqf + dqb, dkf + dkb))
        return dqr, dkr, dvf + dvb

    dq_r, dk_r, dv_r = _rowwise("ret_prep_bwd", ret_prep_bwd, S, ts_light,ret_ins + [R(t) for t in (dq_f, dq_b, dk_f, dk_b, dv_f, dv_b)],
                                [(512, MXU, 512, 0), (512, MXU, 512, 0), (1024, MXU, 1024, 0)])

    gW_mla_p = _mm("d_w_mla_out", o_bf, dy_a, "tn")
    do_bf, delta = _mm_rows("d_o_attn_delta", dy_a, wmla, lambda d, oo, *_: _delta_fn(oo.astype(F32), d), [o_bf], after_early, [MXU, F32], mode="nt")
    dq, dk, dv = _flash_bwd(q, k, v, do_bf, lse, delta)

    def mla_prep_bwd(cq, ckv, kr, cosm_, sinm_, gqa, gkva, gqn_, gkn_, wq_, wk_, wv_, dq_, dk_, dv_):
        f = lambda cq, ckv, kr, gqa, gkva, gqn_, gkn_, wq_, wk_, wv_: _mla_prep_fn(cq, ckv, kr, cosm_, sinm_, gqa, gkva, gqn_, gkn_, wq_, wk_, wv_)
        _, vjp = jax.vjp(f, cq, ckv, kr, gqa, gkva, gqn_, gkn_, wq_.astype(F32), wk_.astype(F32), wv_.astype(F32))
        return vjp((dq_, dk_, dv_))

    mb = _rowwise("mla_prep_bwd", mla_prep_bwd, S, ts, mla_ins + [R(dq), R(dk), R(dv)],
                  [(256, MXU, 256, 0), (128, MXU, 128, 0), (128, MXU, 128, 0)],
                  accs=[(1, 256), (1, 128), (1, LANES), (1, LANES), (256, HEADS * LANES), (128, HEADS * LANES), (128, HEADS * LANES)])
    dc_q, dc_kv, dk_rope, dg_q_a, dg_kv_a, dgqn_p, dgkn_p, dwq_p, dwk_p, dwv_p = mb

    dproj = jnp.concatenate([dga, dgb, dv_r, dg_r, dq_r, dk_r, dc_q, dc_kv, dk_rope], axis=1)
    gW["w_in"] = _win_unpad(_mm("d_w_in", h, dproj, "tn"))
    gW["w_mla_out"] = _wmla_unpad(gW_mla_p)
    after_mid = None if grad_hook is None else grad_hook({n: gW[n] for n in MID_GRADS})
    dh = _mm("d_h", dproj, win, "nt", after=after_mid)
    grad_x, dg_mix = _rowwise("rms_mix_bwd", lambda a, b, c, d, *_: rms_bwd(a, b, c, d)[1:], S, ts_light,
                              [R(x), W_(g_mix), R(dh), R(dx1)] + ([] if after_mid is None else [W_(after_mid)]),
                              [(D_MODEL, F32, D_MODEL, 0)], accs=[(1, D_MODEL)])
    gW["w_q_b"] = _wq_unpad(dwq_p)
    gW["w_kv_b"] = _wkv_unpad(dwk_p, dwv_p)
    gG = {"g_mix": dg_mix, "g_q_a": dg_q_a, "g_kv_a": dg_kv_a, "g_qn": _qk_unpad(dgqn_p),
          "g_kn": _qk_unpad(dgkn_p), "ret_decay_fwd": dlg_f[:, 0, 0][None, :], "ret_decay_bwd": dlg_b[:, 0, 0][None, :],
          "g_ffn": dg_ffn}
    return loss_rows, grad_x, gG, gW


MATS = [("w_in", (1024, 5536), 1), ("w_q_b", (256, 768), 1), ("w_kv_b", (128, 1024), 1), ("w_mla_out", (512, 1024), 1),
        ("w_ret_out", (1024, 1024), 0), ("w_out", (1024, 1024), 0), ("w_gate_up", (1024, 5632), 1), ("w_down", (2816, 1024), 0)]
GAINS = [("g_mix", 1024), ("g_q_a", 256), ("g_kv_a", 128), ("g_qn", 96), ("g_kn", 96), ("ret_decay_fwd", 8), ("ret_decay_bwd", 8),
         ("g_ffn", 1024)]
ORDER = ["g_mix", "w_in", "g_q_a", "w_q_b", "g_kv_a", "w_kv_b", "g_qn", "g_kn", "w_mla_out", "ret_decay_fwd", "ret_decay_bwd",
         "w_ret_out", "w_out", "g_ffn", "w_gate_up", "w_down"]
GAIN_LEN = sum(n for _, n in GAINS)
GAIN_PAD = -(-GAIN_LEN // LANES) * LANES


def _pack_gains(d):
    row = jnp.concatenate([d[n].reshape(1, ln).astype(F32) for n, ln in GAINS], axis=1)
    return jnp.pad(row, ((0, 0), (0, GAIN_PAD - GAIN_LEN)))


def _unpack_gains(row):
    out, off = {}, 0
    for n, ln in GAINS:
        out[n] = row[0, off:off + ln]
        off += ln
    return out


def _unshard(pieces, axis):
    if axis == 0:
        return pieces.reshape((N_DEV * pieces.shape[1], pieces.shape[2]))
    return jnp.concatenate([pieces[p] for p in range(N_DEV)], axis=1)


def _reshard(full, axis):
    if axis == 0:
        return full.reshape((N_DEV, full.shape[0] // N_DEV, full.shape[1]))
    c = full.shape[1] // N_DEV
    return jnp.stack([full[:, c * p:c * (p + 1)] for p in range(N_DEV)])


def _all_gather(shards):
    n = len(shards)

    def body(*refs):
        x_refs, out_refs = refs[:n], refs[n:2 * n]
        send_sems, recv_sems, local_sems = refs[2 * n:]
        x, y, c = lax.axis_index("x"), lax.axis_index("y"), lax.axis_index("c")
        me, sibling = (x, y, c), (x, y, 1 - c)
        chips = [(1 - x, y), (x, 1 - y), (1 - x, 1 - y)]

        def slot(a, px, py, pc):
            return out_refs[a].at[4 * px + 2 * py + pc]

        def copy(a, k, block, to, from_input=False):
            return pltpu.make_async_remote_copy(
                src_ref=x_refs[a] if from_input else slot(a, *block), dst_ref=slot(a, *block),
                send_sem=send_sems.at[a, k], recv_sem=recv_sems.at[a, k], device_id=to, device_id_type=pl.DeviceIdType.MESH)

        mine = [pltpu.make_async_copy(x_refs[a], slot(a, *me), local_sems.at[a]) for a in range(n)]
        first = [copy(a, 0, me, sibling, True) for a in range(n)]
        first += [copy(a, 1 + j, me, (*chip, c), True) for j, chip in enumerate(chips) for a in range(n)]
        for cp in mine + first:
            cp.start()
        passed = []
        for j, chip in enumerate(chips):
            for a in range(n):
                copy(a, 1 + j, (*chip, c), me).wait_recv()
                passed.append(copy(a, 4 + j, (*chip, c), sibling))
                passed[-1].start()
        for a in range(n):
            copy(a, 0, sibling, me).wait_recv()
        for j, chip in enumerate(chips):
            for a in range(n):
                copy(a, 4 + j, (*chip, 1 - c), me).wait_recv()
        for cp in first + passed:
            cp.wait_send()
        for cp in mine:
            cp.wait()

    any_spec = pl.BlockSpec(memory_space=pl.ANY)
    return pl.pallas_call(
        body, name="all_gather_weights", out_shape=[jax.ShapeDtypeStruct((N_DEV,) + s.shape, s.dtype) for s in shards],
        in_specs=[any_spec] * n, out_specs=[any_spec] * n,
        scratch_shapes=[pltpu.SemaphoreType.DMA((n, 7)), pltpu.SemaphoreType.DMA((n, 7)), pltpu.SemaphoreType.DMA((n,))],
    )(*shards)


def _all_to_all(pieces):
    n = len(pieces)

    def body(*refs):
        in_refs, out_refs = refs[:n], refs[n:2 * n]
        send_sems, recv_sems, local_sems = refs[2 * n:]
        x, y, c = lax.axis_index("x"), lax.axis_index("y"), lax.axis_index("c")
        my_id = 4 * x + 2 * y + c
        flips = [(fx, fy, fc) for fx in (0, 1) for fy in (0, 1) for fc in (0, 1)][1:]

        def copy(a, kk, f):
            p = (x ^ f[0], y ^ f[1], c ^ f[2])
            return pltpu.make_async_remote_copy(
                src_ref=in_refs[a].at[4 * p[0] + 2 * p[1] + p[2]], dst_ref=out_refs[a].at[my_id],
                send_sem=send_sems.at[a, kk], recv_sem=recv_sems.at[a, kk], device_id=p, device_id_type=pl.DeviceIdType.MESH)

        mine = [pltpu.make_async_copy(in_refs[a].at[my_id], out_refs[a].at[my_id], local_sems.at[a]) for a in range(n)]
        copies = [copy(a, kk, f) for kk, f in enumerate(flips) for a in range(n)]
        for cp in mine + copies:
            cp.start()
        for cp in copies:
            cp.wait_recv()
        for cp in copies:
            cp.wait_send()
        for cp in mine:
            cp.wait()

    any_spec = pl.BlockSpec(memory_space=pl.ANY)
    return pl.pallas_call(
        body, name="all_to_all_grads", out_shape=[jax.ShapeDtypeStruct(p.shape, p.dtype) for p in pieces],
        in_specs=[any_spec] * n, out_specs=[any_spec] * n,
        scratch_shapes=[pltpu.SemaphoreType.DMA((n, 7)), pltpu.SemaphoreType.DMA((n, 7)), pltpu.SemaphoreType.DMA((n,))],
    )(*pieces)


def _flip_peers(x, y, c):
    flips = [(fx, fy, fc) for fx in (0, 1) for fy in (0, 1) for fc in (0, 1)][1:]
    return [(x ^ fx, y ^ fy, c ^ fc) for fx, fy, fc in flips]


def _split_copies(in_refs, land_refs, send_sems, recv_sems, gather):
    x, y, c = lax.axis_index("x"), lax.axis_index("y"), lax.axis_index("c")
    my_id = 4 * x + 2 * y + c
    copies = []
    for kk, p in enumerate(_flip_peers(x, y, c)):
        for a in range(len(in_refs)):
            src = in_refs[a] if gather else in_refs[a].at[4 * p[0] + 2 * p[1] + p[2]]
            copies.append(pltpu.make_async_remote_copy(
                src_ref=src, dst_ref=land_refs[a].at[my_id], send_sem=send_sems.at[a * 7 + kk], recv_sem=recv_sems.at[a * 7 + kk],
                device_id=p, device_id_type=pl.DeviceIdType.MESH))
    return copies


def _exchange_start(name, srcs, gather, after=None):
    n = len(srcs)
    first_out = 2 * n + (0 if after is None else 1)

    def body(*refs):
        for cp in _split_copies(refs[:n], refs[n:2 * n], refs[first_out], refs[first_out + 1], gather):
            cp.start()
        refs[-1][...] = jnp.zeros_like(refs[-1])

    hbm, sem = pl.BlockSpec(memory_space=pltpu.HBM), pl.BlockSpec(memory_space=pltpu.SEMAPHORE)
    land_shapes = [((N_DEV,) + s.shape if gather else s.shape, s.dtype) for s in srcs]
    lands = [pltpu.with_memory_space_constraint(lax.empty(shp, dt), pltpu.HBM) for shp, dt in land_shapes]
    srcs = [pltpu.with_memory_space_constraint(s, pltpu.HBM) for s in srcs]
    res = pl.pallas_call(
        body, name=name,
        out_shape=[pltpu.SemaphoreType.DMA((7 * n,)), pltpu.SemaphoreType.DMA((7 * n,))] + [pltpu.HBM(s.shape, s.dtype) for s in srcs]
        + [pltpu.HBM(shp, dt) for shp, dt in land_shapes] + [jax.ShapeDtypeStruct((8, LANES), F32)],
        in_specs=[hbm] * (2 * n) + ([] if after is None else [pl.BlockSpec(memory_space=pl.ANY)]),
        out_specs=[sem, sem] + [hbm] * (2 * n) + [pl.BlockSpec(memory_space=pltpu.VMEM)],
        input_output_aliases={i: 2 + i for i in range(2 * n)},
        compiler_params=pltpu.CompilerParams(has_side_effects=pltpu.SideEffectType.DATAFLOW_SIDE_EFFECTING),
    )(*srcs, *lands, *([] if after is None else [after]))
    return res[0], res[1], res[2:2 + n], res[2 + n:2 + 2 * n], res[-1]


def _exchange_wait(name, handles, after, gather):
    send_sems, recv_sems, srcs, lands, _ = handles
    n = len(srcs)

    def body(*refs):
        for cp in _split_copies(refs[:n], refs[n:2 * n], refs[2 * n], refs[2 * n + 1], gather):
            cp.wait_send()
            cp.wait_recv()

    hbm, sem = pl.BlockSpec(memory_space=pltpu.HBM), pl.BlockSpec(memory_space=pltpu.SEMAPHORE)
    res = pl.pallas_call(
        body, name=name, out_shape=[pltpu.HBM(t.shape, t.dtype) for t in list(srcs) + list(lands)],
        in_specs=[hbm] * (2 * n) + [sem, sem, pl.BlockSpec(memory_space=pl.ANY)], out_specs=[hbm] * (2 * n),
        input_output_aliases={i: i for i in range(2 * n)},
        compiler_params=pltpu.CompilerParams(has_side_effects=pltpu.SideEffectType.DATAFLOW_SIDE_EFFECTING),
    )(*srcs, *lands, send_sems, recv_sems, after)
    my_id = 4 * lax.axis_index("x") + 2 * lax.axis_index("y") + lax.axis_index("c")
    own = [s if gather else lax.dynamic_index_in_dim(s, my_id, 0, keepdims=False) for s in res[:n]]
    return [lax.dynamic_update_index_in_dim(land, o, my_id, 0) for land, o in zip(res[n:], own)]


def _adamw(name, parts, w, m, v):
    rows, cols = w.shape
    tr = _pick(rows, (128, 64, 32, 16, 8))
    pspec = pl.BlockSpec((N_DEV, tr, cols), lambda i: (0, i, 0))
    rspec = pl.BlockSpec((tr, cols), lambda i: (i, 0))

    def body(p_ref, w_ref, m_ref, v_ref, g_ref, d_ref, m2_ref, v2_ref):
        g, d, m2, v2 = _adamw_fn([p_ref[s] for s in range(N_DEV)], w_ref[...], m_ref[...], v_ref[...])
        g_ref[...], d_ref[...], m2_ref[...], v2_ref[...] = g, d, m2, v2

    return pl.pallas_call(
        body, name=name, grid=(rows // tr,), in_specs=[pspec, rspec, rspec, rspec], out_specs=[rspec] * 4,
        out_shape=[jax.ShapeDtypeStruct((rows, cols), F32)] * 4,
        compiler_params=pltpu.CompilerParams(dimension_semantics=("parallel",), vmem_limit_bytes=VMEM_LIMIT),
    )(parts, w, m, v)


def kernel(x, positions, g_mix, w_in, g_q_a, w_q_b, g_kv_a, w_kv_b, g_qn, g_kn, w_mla_out, ret_decay_fwd, ret_decay_bwd, w_ret_out, w_out, g_ffn, w_gate_up, w_down, loss_target, m_g_mix, m_w_in, m_g_q_a, m_w_q_b, m_g_kv_a, m_w_kv_b, m_g_qn, m_g_kn, m_w_mla_out, m_ret_decay_fwd, m_ret_decay_bwd, m_w_ret_out, m_w_out, m_g_ffn, m_w_gate_up, m_w_down, v_g_mix, v_w_in, v_g_q_a, v_w_q_b, v_g_kv_a, v_w_kv_b, v_g_qn, v_g_kn, v_w_mla_out, v_ret_decay_fwd, v_ret_decay_bwd, v_w_ret_out, v_w_out, v_g_ffn, v_w_gate_up, v_w_down):
    w = dict(g_mix=g_mix, w_in=w_in, g_q_a=g_q_a, w_q_b=w_q_b, g_kv_a=g_kv_a, w_kv_b=w_kv_b, g_qn=g_qn, g_kn=g_kn, w_mla_out=w_mla_out,
             ret_decay_fwd=ret_decay_fwd, ret_decay_bwd=ret_decay_bwd, w_ret_out=w_ret_out, w_out=w_out, g_ffn=g_ffn,
             w_gate_up=w_gate_up, w_down=w_down)
    m = dict(g_mix=m_g_mix, w_in=m_w_in, g_q_a=m_g_q_a, w_q_b=m_w_q_b, g_kv_a=m_g_kv_a, w_kv_b=m_w_kv_b, g_qn=m_g_qn, g_kn=m_g_kn,
             w_mla_out=m_w_mla_out, ret_decay_fwd=m_ret_decay_fwd, ret_decay_bwd=m_ret_decay_bwd, w_ret_out=m_w_ret_out, w_out=m_w_out,
             g_ffn=m_g_ffn, w_gate_up=m_w_gate_up, w_down=m_w_down)
    v = dict(g_mix=v_g_mix, w_in=v_w_in, g_q_a=v_g_q_a, w_q_b=v_w_q_b, g_kv_a=v_g_kv_a, w_kv_b=v_w_kv_b, g_qn=v_g_qn, g_kn=v_g_kn,
             w_mla_out=v_w_mla_out, ret_decay_fwd=v_ret_decay_fwd, ret_decay_bwd=v_ret_decay_bwd, w_ret_out=v_w_ret_out, w_out=v_w_out,
             g_ffn=v_g_ffn, w_gate_up=v_w_gate_up, w_down=v_w_down)
    gains = {n: w[n].reshape(1, ln) for n, ln in GAINS}

    axis_of = {n: axis for n, _, axis in MATS}
    later = [n for n, _, _ in MATS if n not in FIRST_WEIGHTS]
    gathered = _all_gather([w[n].astype(WIRE) for n in FIRST_WEIGHTS])
    W = {n: _unshard(g, axis_of[n]) for n, g in zip(FIRST_WEIGHTS, gathered)}
    later_handles = _exchange_start("gather_later_start", [w[n].astype(WIRE) for n in later], True, after=gathered[0])

    def late_weights(after):
        lands = _exchange_wait("gather_later_wait", later_handles, after, True)
        return {n: _unshard(g, axis_of[n]) for n, g in zip(later, lands)}

    grad_groups = []

    def grad_hook(g):
        names = tuple(g)
        handles = _exchange_start("grads_start_%d" % len(grad_groups), [_reshard(g[n], axis_of[n]).astype(GWIRE) for n in names], False)
        grad_groups.append((names, handles))
        return handles[4]

    S = x.shape[1]
    pos = positions.reshape(S, 1).astype(F32)
    loss_rows, grad_x, gG, gW = _local_step(x.reshape(S, D_MODEL), pos, loss_target.reshape(S, D_MODEL), gains, W, late_weights, grad_hook,
                                            start_after=later_handles[4])
    loss = lax.psum(jnp.sum(loss_rows), ("x", "y", "c"))

    last = [n for n, _, _ in MATS if n not in EARLY_GRADS + MID_GRADS]
    pieces = [_reshard(gW[n], axis_of[n]).astype(GWIRE) for n in last]
    pieces.append(jnp.broadcast_to(_pack_gains(gG)[None], (N_DEV, 1, GAIN_PAD)))
    late_parts = _all_to_all(pieces)
    parts = dict(zip(last, late_parts))
    for i, (names, handles) in enumerate(grad_groups):
        parts.update(zip(names, _exchange_wait("grads_wait_%d" % i, handles, late_parts[-1], False)))
    out = [dict() for _ in range(4)]
    for n, _, _ in MATS:
        for o, r in zip(out, _adamw("adamw_" + n, parts[n], w[n], m[n], v[n])):
            o[n] = r
    for o, r in zip(out, _adamw("adamw_gains", late_parts[-1], _pack_gains(w), _pack_gains(m), _pack_gains(v))):
        o.update(_unpack_gains(r))
    return (loss, grad_x.reshape(x.shape), *[o[n] for o in out for n in ORDER])
```

```python
import functools

import numpy as np
import jax
import jax.numpy as jnp
from jax import lax
from jax.experimental import pallas as pl
from jax.experimental.pallas import tpu as pltpu

F32 = jnp.float32
MXU = jnp.bfloat16
WIRE = jnp.bfloat16
GWIRE = jnp.bfloat16

N_DEV = 8
D_MODEL = 1024
HEADS = 8
LANES = 128
Q_RANK, KV_RANK = 256, 128
NOPE, ROPE_M, V_M = 64, 32, 64
QK_M = NOPE + ROPE_M
RQK, RV = 64, 128
CHUNK = 128
FFN = 2816
IN_WIDTH = 5536
THETA = 10000.0
EPS = 1e-6
LR, B1, B2, AEPS, WD, STEP = 0.001, 0.9, 0.999, 1e-08, 0.01, 10
VMEM_LIMIT = 56 * 1024 * 1024

NN = ((1,), (0,))
NT = ((1,), (1,))
TN = ((0,), (0,))

P_GATES, P_VR, P_GR, P_QR, P_KR, P_CQ, P_CKV, P_KROPE, P_WIDTH = 0, 2048, 3072, 4096, 4608, 5120, 5376, 5504, 5632
O_CQ, O_CKV, O_KROPE, O_QR, O_KR, O_VR, O_GR, O_GATES = 0, 256, 384, 416, 928, 1440, 2464, 3488


def _dot(a, b, dims):
    return lax.dot_general(a, b, (dims, ((), ())), preferred_element_type=F32)


def _pick(dim, cands):
    for c in cands:
        if dim % c == 0:
            return c
    return dim


def _pairs(t):
    return t.reshape(t.shape[0], 4, 2, 2, 32).transpose(0, 1, 3, 2, 4).reshape(t.shape[0], 512)


def _win_pad_mla(w):
    z = jnp.zeros((w.shape[0], 48), w.dtype)
    kr = w[:, O_KROPE:O_KROPE + 32]
    return jnp.concatenate([w[:, :O_CKV], w[:, O_CKV:O_KROPE], kr[:, :16], z, kr[:, 16:], z], axis=1)


def _win_pad(w):
    return jnp.concatenate([w[:, O_GATES:], w[:, O_VR:O_VR + 1024], w[:, O_GR:O_GR + 1024], _pairs(w[:, O_QR:O_QR + 512]),
                            _pairs(w[:, O_KR:O_KR + 512]), _win_pad_mla(w[:, :O_QR])], axis=1)


def _win_unpad(g):
    return jnp.concatenate([g[:, P_CQ:P_CQ + 256], g[:, P_CKV:P_CKV + 128], g[:, P_KROPE:P_KROPE + 16], g[:, P_KROPE + 64:P_KROPE + 80],
                            _pairs(g[:, P_QR:P_QR + 512]), _pairs(g[:, P_KR:P_KR + 512]), g[:, P_VR:P_VR + 1024],
                            g[:, P_GR:P_GR + 1024], g[:, P_GATES:P_GATES + 2048]], axis=1)


def _qk_pad(t):
    z = jnp.zeros(t.shape[:-1] + (32,), t.dtype)
    return jnp.concatenate([t[..., 64:80], t[..., 0:48], t[..., 80:96], t[..., 48:64], z], axis=-1)


def _qk_unpad(p):
    return jnp.concatenate([p[..., 16:64], p[..., 80:96], p[..., 0:16], p[..., 64:80]], axis=-1)


def _wq_pad(w):
    return _qk_pad(w.reshape(Q_RANK, HEADS, QK_M)).reshape(Q_RANK, HEADS * LANES)


def _wq_unpad(g):
    return _qk_unpad(g.reshape(Q_RANK, HEADS, LANES)).reshape(Q_RANK, HEADS * QK_M)


def _wkv_pad(w):
    t = w.reshape(KV_RANK, HEADS, NOPE + V_M)
    z = lambda n: jnp.zeros((KV_RANK, HEADS, n), w.dtype)
    wk = jnp.concatenate([z(16), t[..., 0:48], z(16), t[..., 48:64], z(32)], axis=-1)
    wv = jnp.concatenate([t[..., 64:128], z(64)], axis=-1)
    return wk.reshape(KV_RANK, HEADS * LANES), wv.reshape(KV_RANK, HEADS * LANES)


def _wkv_unpad(dwk, dwv):
    k, v = dwk.reshape(KV_RANK, HEADS, LANES), dwv.reshape(KV_RANK, HEADS, LANES)
    return jnp.concatenate([k[..., 16:64], k[..., 80:96], v[..., 0:64]], axis=-1).reshape(KV_RANK, HEADS * (NOPE + V_M))


def _wmla_pad(w):
    t = w.reshape(HEADS, V_M, D_MODEL)
    return jnp.concatenate([t, jnp.zeros_like(t)], axis=1).reshape(HEADS * LANES, D_MODEL)


def _wmla_unpad(g):
    return g.reshape(HEADS, LANES, D_MODEL)[:, :V_M].reshape(HEADS * V_M, D_MODEL)


def _rowwise(name, fn, rows, ts, ins, outs, accs=(), ncol=1):
    n_in, n_out, n_acc = len(ins), len(outs), len(accs)

    def colmap(col):
        if callable(col):
            return lambda i, j: (i, col(j))
        return lambda i, j: (i, col)

    arrays, in_specs = [], []
    for arr, spec in ins:
        arrays.append(arr)
        if spec is None:
            in_specs.append(pl.BlockSpec(arr.shape, functools.partial(lambda i, j, nd: (0,) * nd, nd=arr.ndim)))
        else:
            in_specs.append(pl.BlockSpec((ts, spec[0]), colmap(spec[1])))
    out_shape, out_specs = [], []
    for total, dtype, width, col in outs:
        out_shape.append(jax.ShapeDtypeStruct((rows, total), dtype))
        out_specs.append(pl.BlockSpec((ts, width), colmap(col)))
    for shp in accs:
        out_shape.append(jax.ShapeDtypeStruct(shp, F32))
        out_specs.append(pl.BlockSpec(shp, functools.partial(lambda i, j, nd: (0,) * nd, nd=len(shp))))

    def body(*refs):
        vals = [r[...] for r in refs[:n_in]]
        res = fn(*vals)
        if not isinstance(res, (tuple, list)):
            res = (res,)
        for r, v in zip(refs[n_in:n_in + n_out], res[:n_out]):
            r[...] = v.astype(r.dtype)
        if n_acc:
            first = jnp.logical_and(pl.program_id(0) == 0, pl.program_id(1) == 0)
            for r, v in zip(refs[n_in + n_out:], res[n_out:]):
                @pl.when(first)
                def _(r=r):
                    r[...] = jnp.zeros_like(r)
                r[...] += v.astype(F32)

    res = pl.pallas_call(
        body, name=name, grid=(rows // ts, ncol), in_specs=in_specs, out_specs=out_specs, out_shape=out_shape,
        compiler_params=pltpu.CompilerParams(dimension_semantics=("arbitrary", "arbitrary"), vmem_limit_bytes=VMEM_LIMIT),
    )(*arrays)
    return res


MM_OPERAND_BYTES = 24 * 1024 * 1024


def _mm(name, a, b, mode, add=None, after=None):
    a_halves, b_halves = a.ndim == 3, b.ndim == 3
    assert not a_halves or mode == "nt"
    assert not b_halves or mode == "tn"
    if mode == "nn":
        (M, K), N = a.shape, b.shape[1]
    elif mode == "nt":
        M, K, N = a.shape[-2], a.shape[-1] * (2 if a_halves else 1), b.shape[0]
    else:
        (K, M), N = a.shape, b.shape[-1] * (2 if b_halves else 1)
    tm = _pick(M, (1024, 512, 1408, 256, 128))
    tn = _pick(N // 2 if b_halves else N, (1408, 1024, 512, 256, 128))
    fits = lambda t: 2 * (tm + tn) * t * a.dtype.itemsize <= MM_OPERAND_BYTES
    kdiv = K // 2 if a_halves else K
    tk = next(t for t in (K, 4096, 2816, 2048, 1408, 1024, 512, 256, 128) if kdiv % t == 0 and (fits(t) or t == 128))
    nk = K // tk
    dims = {"nn": NN, "nt": NT, "tn": TN}[mode]
    if a_halves:
        per = kdiv // tk
        a_spec = pl.BlockSpec((None, tm, tk), lambda i, j, k: (k // per, i, k % per))
    else:
        a_spec = pl.BlockSpec((tk, tm), lambda i, j, k: (k, i)) if mode == "tn" else pl.BlockSpec((tm, tk), lambda i, j, k: (i, k))
    if b_halves:
        perj = (N // 2) // tn
        b_spec = pl.BlockSpec((None, tk, tn), lambda i, j, k: (j // perj, k, j % perj))
    else:
        b_spec = pl.BlockSpec((tn, tk), lambda i, j, k: (j, k)) if mode == "nt" else pl.BlockSpec((tk, tn), lambda i, j, k: (k, j))
    o_spec = pl.BlockSpec((tm, tn), lambda i, j, k: (i, j))
    has_add = add is not None

    def body(*refs):
        a_ref, b_ref, o_ref = refs[0], refs[1], refs[-1]
        d = _dot(a_ref[...], b_ref[...], dims)
        first = (d + refs[2][...]) if has_add else d
        if nk == 1:
            o_ref[...] = first
        else:
            k = pl.program_id(2)

            @pl.when(k == 0)
            def _():
                o_ref[...] = first

            @pl.when(k > 0)
            def _():
                o_ref[...] += d

    args = [a, b] + ([add] if has_add else []) + ([] if after is None else [after])
    specs = [a_spec, b_spec] + ([o_spec] if has_add else []) + ([] if after is None else [pl.BlockSpec(memory_space=pl.ANY)])
    return pl.pallas_call(
        body, name=name, grid=(M // tm, N // tn, nk), in_specs=specs, out_specs=o_spec,
        out_shape=jax.ShapeDtypeStruct((M, N), F32),
        compiler_params=pltpu.CompilerParams(dimension_semantics=("parallel", "parallel", "arbitrary"), vmem_limit_bytes=VMEM_LIMIT),
    )(*args)


def _mm_rows(name, a, b, fn, row_ins, whole_ins, outs, accs=(), mode="nn"):
    (M, K), N = a.shape, b.shape[1 if mode == "nn" else 0]
    tm = _pick(M, (512, 256, 128))
    n_in, n_out = 2 + len(row_ins) + len(whole_ins), len(outs)
    windows = [t if isinstance(t, tuple) else (t, (t.shape[1], 0)) for t in row_ins]
    row_ins = [t for t, _ in windows]
    row_specs = [pl.BlockSpec((tm, w), functools.partial(lambda i, col: (i, col), col=col)) for _, (w, col) in windows]

    def body(*refs):
        d = _dot(refs[0][...], refs[1][...], NN if mode == "nn" else NT)
        res = fn(d, *[r[...] for r in refs[2:n_in]])
        for r, v in zip(refs[n_in:n_in + n_out], res[:n_out]):
            r[...] = v.astype(r.dtype)
        for r, v in zip(refs[n_in + n_out:], res[n_out:]):
            @pl.when(pl.program_id(0) == 0)
            def _(r=r):
                r[...] = jnp.zeros_like(r)
            r[...] += v

    row = pl.BlockSpec((tm, N), lambda i: (i, 0))
    whole = lambda t: pl.BlockSpec(t.shape, functools.partial(lambda i, nd: (0,) * nd, nd=t.ndim))
    return pl.pallas_call(
        body, name=name, grid=(M // tm,),
        in_specs=[pl.BlockSpec((tm, K), lambda i: (i, 0)), whole(b)] + row_specs + [whole(t) for t in whole_ins],
        out_specs=[row] * n_out + [pl.BlockSpec(s, functools.partial(lambda i, nd: (0,) * nd, nd=len(s))) for s in accs],
        out_shape=[jax.ShapeDtypeStruct((M, N), dt) for dt in outs] + [jax.ShapeDtypeStruct(s, F32) for s in accs],
        compiler_params=pltpu.CompilerParams(dimension_semantics=("arbitrary",), vmem_limit_bytes=VMEM_LIMIT),
    )(a, b, *row_ins, *whole_ins)


def _ffn_tiles(S):
    return _pick(S, (1024, 512, 256, 128)), _pick(FFN, (1408, 704, 256, 128))


def _gate_up_swiglu(h2, wgu):
    S, K = h2.shape
    tm, tn = _ffn_tiles(S)
    nj = FFN // tn

    def body(a_ref, bg_ref, bu_ref, gu_ref, act_ref):
        a = a_ref[...]
        g, u = _dot(a, bg_ref[...], NN), _dot(a, bu_ref[...], NN)
        gu_ref[0], gu_ref[1] = g.astype(gu_ref.dtype), u.astype(gu_ref.dtype)
        act_ref[...] = _swiglu_fn(g, u).astype(act_ref.dtype)

    return pl.pallas_call(
        body, name="gate_up_swiglu", grid=(S // tm, nj),
        in_specs=[pl.BlockSpec((tm, K), lambda i, j: (i, 0)), pl.BlockSpec((K, tn), lambda i, j: (0, j)),
                  pl.BlockSpec((K, tn), lambda i, j: (0, nj + j))],
        out_specs=[pl.BlockSpec((2, tm, tn), lambda i, j: (0, i, j)), pl.BlockSpec((tm, tn), lambda i, j: (i, j))],
        out_shape=[jax.ShapeDtypeStruct((2, S, FFN), MXU), jax.ShapeDtypeStruct((S, FFN), MXU)],
        compiler_params=pltpu.CompilerParams(dimension_semantics=("parallel", "parallel"), vmem_limit_bytes=VMEM_LIMIT),
    )(h2, wgu, wgu)


def _d_act_swiglu(dx2, wdown, gu):
    S, K = dx2.shape
    tm, tn = _ffn_tiles(S)

    def body(a_ref, b_ref, gu_ref, o_ref):
        dact = _dot(a_ref[...], b_ref[...], NT)
        _, vjp = jax.vjp(_swiglu_fn, gu_ref[0].astype(F32), gu_ref[1].astype(F32))
        dg, du = vjp(dact)
        o_ref[0], o_ref[1] = dg.astype(o_ref.dtype), du.astype(o_ref.dtype)

    stacked = pl.BlockSpec((2, tm, tn), lambda i, j: (0, i, j))
    return pl.pallas_call(
        body, name="d_act_swiglu", grid=(S // tm, FFN // tn),
        in_specs=[pl.BlockSpec((tm, K), lambda i, j: (i, 0)), pl.BlockSpec((tn, K), lambda i, j: (j, 0)), stacked],
        out_specs=stacked, out_shape=jax.ShapeDtypeStruct((2, S, FFN), MXU),
        compiler_params=pltpu.CompilerParams(dimension_semantics=("parallel", "parallel"), vmem_limit_bytes=VMEM_LIMIT),
    )(dx2, wdown, gu)


@jax.custom_vjp
def _swap64(x):
    return pltpu.roll(x, 64, 1)


_swap64.defvjp(lambda x: (_swap64(x), None), lambda _, g: (_swap64(g),))


@jax.custom_vjp
def _mxdot(a, b):
    return _dot(a.astype(MXU), b.astype(MXU), NN)


def _mxdot_bwd(res, g):
    a, b = res
    gb = g.astype(MXU)
    return _dot(gb, b.astype(MXU), NT), _dot(a.astype(MXU), gb, TN)


_mxdot.defvjp(lambda a, b: (_mxdot(a, b), (a, b)), _mxdot_bwd)


def _row_sum(t):
    if t.shape[-1] == LANES:
        return lax.dot_general(t, jnp.ones((LANES, LANES), F32), ((NN), ((), ())), precision=lax.Precision.HIGH,
                               preferred_element_type=F32)
    return jnp.sum(t, axis=-1, keepdims=True)


@functools.partial(jax.custom_vjp, nondiff_argnums=(1,))
def _unit_rms(x, n):
    return x * lax.rsqrt(_row_sum(x * x) * (1.0 / n) + EPS)


def _unit_rms_fwd(x, n):
    r = lax.rsqrt(_row_sum(x * x) * (1.0 / n) + EPS)
    y = x * r
    return y, (y, r)


def _unit_rms_bwd(n, res, g):
    y, r = res
    return (r * (g - y * (_row_sum(g * y) * (1.0 / n))),)


_unit_rms.defvjp(_unit_rms_fwd, _unit_rms_bwd)


def _rms(x):
    return _unit_rms(x, x.shape[-1])


def _rmsg_fn(x, g):
    return _rms(x) * g


def _silu(x):
    return x * jax.nn.sigmoid(x)


def _tables_fn(pos, inv_m, sgn_m, inv_r, sgn_r):
    am, ar = pos * inv_m, pos * inv_r
    return jnp.cos(am), jnp.sin(am) * sgn_m, jnp.cos(ar), jnp.sin(ar) * sgn_r


def _head_blocks(t):
    return [t[:, LANES * h:LANES * (h + 1)] for h in range(t.shape[1] // LANES)]


def _mla_prep_fn(cq, ckv, kr, cosm, sinm, gqa, gkva, gqn, gkn, wq, wk, wv):
    cqn = _rms(cq) * gqa
    ckvn = _rms(ckv) * gkva
    q_raw = _mxdot(cqn, wq)
    k_raw = _mxdot(ckvn, wk)
    lane = lax.broadcasted_iota(jnp.int32, (1, HEADS * LANES), 1)
    v = _mxdot(ckvn, wv) + (lane % LANES == V_M).astype(F32)

    def norm_rope(blocks, g, extra):
        outs = []
        for b in blocks:
            if extra is not None:
                b = b + extra
            n = _unit_rms(b, QK_M) * g
            outs.append(n * cosm + _swap64(n) * sinm)
        return jnp.concatenate(outs, axis=1)

    q = norm_rope(_head_blocks(q_raw), gqn, None)
    k = norm_rope(_head_blocks(k_raw), gkn, kr)
    return q, k, v


def _ret_prep_fn(qr, kr, cosr, sinr):
    def rope(t, scale):
        return jnp.concatenate([(b * cosr + _swap64(b) * sinr) * scale for b in _head_blocks(t)], axis=1)
    return rope(qr, 1.0), rope(kr, RQK ** -0.5)


def _ret_post_fn(rf, rb, gr):
    ret = rf + rb
    outs = []
    for b, g in zip(_head_blocks(ret), _head_blocks(gr)):
        outs.append(_silu(g) * _rms(b))
    return jnp.concatenate(outs, axis=1)


def _merge_fn(ga, gb, ya, yb):
    return jax.nn.sigmoid(ga) * ya + jax.nn.sigmoid(gb) * yb


def _swiglu_fn(gate, up):
    return _silu(gate) * up


def _loss_fn(x2, tgt):
    d = x2 - tgt
    return d * (1.0 / D_MODEL), 0.5 * jnp.sum(d * d, axis=0, keepdims=True) * (1.0 / D_MODEL)


def _adamw_fn(parts, w, m, v):
    g = parts[0].astype(F32)
    for p in range(1, N_DEV):
        g = g + parts[p].astype(F32)
    m2 = B1 * m + (1.0 - B1) * g
    v2 = B2 * v + (1.0 - B2) * jnp.square(g)
    m_hat = m2 / (1.0 - B1 ** STEP)
    v_hat = v2 / (1.0 - B2 ** STEP)
    delta = -LR * (m_hat / (jnp.sqrt(v_hat) + AEPS) + WD * w)
    return g, delta, m2, v2


SCALE = QK_M ** -0.5
LOG2E = 1.4426950408889634
FLASH_ROWS = 32


def _flash_fwd(q, k, v):
    S = q.shape[0]
    tk = _pick(S, (512, 256, 128))
    tq = _pick(S, (1024, 512, 256, 128))
    ncb = tk // LANES
    nkv = S // tk
    assert nkv % 2 == 0, "kv tiles are processed in pairs"
    mrows = 64
    c = SCALE * LOG2E

    def body(q_ref, k_ref, v_ref, o_ref, lse_ref, s_a, p_a, s_b, p_b, m_sc, a_sc, acc_sc):
        m_sc[...] = jnp.full_like(m_sc, -jnp.inf)
        acc_sc[...] = jnp.zeros_like(acc_sc)
        qb = q_ref[...]

        def scores(j, s_buf):
            s_buf[...] = _dot(qb, k_ref[pl.ds(pl.multiple_of(j * tk, tk), tk), :], NT)

        def stage(j, s_buf, p_buf, s_next):
            scores(jnp.minimum(j + 1, nkv - 1), s_next)
            for r in range(tq // mrows):
                rows = slice(r * mrows, (r + 1) * mrows)
                cols = [s_buf[rows, LANES * cb:LANES * (cb + 1)] for cb in range(ncb)]
                m_prev = m_sc[rows, :]
                row_max = jnp.max(functools.reduce(jnp.maximum, cols), axis=-1, keepdims=True)
                m_new = jnp.maximum(m_prev, jnp.broadcast_to(row_max, (mrows, LANES)))
                a_sc[rows, :] = jnp.exp2((m_prev - m_new) * c)
                m_sc[rows, :] = m_new
                for cb in range(ncb):
                    p_buf[rows, LANES * cb:LANES * (cb + 1)] = jnp.exp2((cols[cb] - m_new) * c).astype(p_buf.dtype)
            acc_sc[...] = a_sc[...] * acc_sc[...] + _dot(p_buf[...], v_ref[pl.ds(pl.multiple_of(j * tk, tk), tk), :], NN)

        scores(0, s_a)

        def pair_step(t, carry):
            stage(2 * t, s_a, p_a, s_b)
            stage(2 * t + 1, s_b, p_b, s_a)
            return carry

        lax.fori_loop(0, nkv // 2, pair_step, 0, unroll=4)
        acc = acc_sc[...]
        lane = lax.broadcasted_iota(jnp.int32, (1, LANES), 1)
        l = jnp.sum(jnp.where(lane == V_M, acc, 0.0), axis=-1, keepdims=True)
        o_ref[...] = (acc / l).astype(o_ref.dtype)
        lse_ref[...] = m_sc[...] * c + jnp.log2(jnp.broadcast_to(l, (tq, LANES)))

    qspec = pl.BlockSpec((tq, LANES), lambda h, i: (i, h))
    kspec = pl.BlockSpec((S, LANES), lambda h, i: (0, h))
    return pl.pallas_call(
        body, name="flash_fwd", grid=(HEADS, S // tq), in_specs=[qspec, kspec, kspec], out_specs=[qspec, qspec],
        out_shape=[jax.ShapeDtypeStruct((S, HEADS * LANES), MXU), jax.ShapeDtypeStruct((S, HEADS * LANES), F32)],
        scratch_shapes=[pltpu.VMEM((tq, tk), F32), pltpu.VMEM((tq, tk), MXU)] * 2 + [pltpu.VMEM((tq, LANES), F32)] * 3,
        compiler_params=pltpu.CompilerParams(dimension_semantics=("parallel", "arbitrary"), vmem_limit_bytes=VMEM_LIMIT),
    )(q, k, v)


def _delta_fn(o, do):
    outs = [jnp.broadcast_to(jnp.sum(a * b, axis=-1, keepdims=True), a.shape) for a, b in zip(_head_blocks(o), _head_blocks(do))]
    return do, jnp.concatenate(outs, axis=1)


def _flash_bwd(q, k, v, do, lse, delta):
    S = q.shape[0]
    tq = tk = _pick(S, (512, 256, 128))
    ncb = tk // LANES
    c = SCALE * LOG2E

    nq = S // tq
    assert nq % 2 == 0, "q tiles are processed in pairs"

    def body(q_ref, k_ref, v_ref, do_ref, lse_ref, dl_ref, dq_ref, dk_ref, dv_ref, s_a, dp_a, p_a, ds_a, s_b, dp_b, p_b, ds_b):
        @pl.when(pl.program_id(1) == 0)
        def _():
            dq_ref[...] = jnp.zeros_like(dq_ref)

        dk_ref[...] = jnp.zeros_like(dk_ref)
        dv_ref[...] = jnp.zeros_like(dv_ref)
        kb, vb = k_ref[...], v_ref[...]

        def scores(i, s_buf, dp_buf):
            q_rows = pl.ds(pl.multiple_of(i * tq, tq), tq)
            s_buf[...] = _dot(q_ref[q_rows, :], kb, NT)
            dp_buf[...] = _dot(do_ref[q_rows, :], vb, NT)

        def stage(i, s_buf, dp_buf, p_buf, ds_buf, s_next, dp_next):
            scores(jnp.minimum(i + 1, nq - 1), s_next, dp_next)
            for r in range(tq // FLASH_ROWS):
                rows = slice(r * FLASH_ROWS, (r + 1) * FLASH_ROWS)
                grows = pl.ds(pl.multiple_of(i * tq + r * FLASH_ROWS, FLASH_ROWS), FLASH_ROWS)
                lse_b, dl_b = lse_ref[grows, :], dl_ref[grows, :]
                for cb in range(ncb):
                    sl = slice(LANES * cb, LANES * (cb + 1))
                    p = jnp.exp2(s_buf[rows, sl] * c - lse_b)
                    p_buf[rows, sl] = p.astype(p_buf.dtype)
                    ds_buf[rows, sl] = (p * (dp_buf[rows, sl] - dl_b) * SCALE).astype(ds_buf.dtype)
            q_rows = pl.ds(pl.multiple_of(i * tq, tq), tq)
            dv_ref[...] += _dot(p_buf[...], do_ref[q_rows, :], TN)
            dk_ref[...] += _dot(ds_buf[...], q_ref[q_rows, :], TN)
            dq_ref[q_rows, :] += _dot(ds_buf[...], kb, NN)

        scores(0, s_a, dp_a)

        def pair_step(t, carry):
            stage(2 * t, s_a, dp_a, p_a, ds_a, s_b, dp_b)
            stage(2 * t + 1, s_b, dp_b, p_b, ds_b, s_a, dp_a)
            return carry

        lax.fori_loop(0, nq // 2, pair_step, 0, unroll=2)

    hspec = pl.BlockSpec((S, LANES), lambda h, j: (0, h))
    kspec = pl.BlockSpec((tk, LANES), lambda h, j: (j, h))
    full = jax.ShapeDtypeStruct((S, HEADS * LANES), F32)
    tile_bufs = [pltpu.VMEM((tq, tk), F32), pltpu.VMEM((tq, tk), F32), pltpu.VMEM((tq, tk), MXU), pltpu.VMEM((tq, tk), MXU)]
    return pl.pallas_call(
        body, name="flash_bwd", grid=(HEADS, S // tk), in_specs=[hspec, kspec, kspec, hspec, hspec, hspec],
        out_specs=[hspec, kspec, kspec], out_shape=[full, full, full],
        scratch_shapes=tile_bufs + tile_bufs,
        compiler_params=pltpu.CompilerParams(dimension_semantics=("parallel", "arbitrary"), vmem_limit_bytes=VMEM_LIMIT),
    )(q, k, v, do, lse, delta)


def _ret_consts(lgh, head, rev):
    C = CHUNK
    lane = lax.broadcasted_iota(jnp.int32, (1, LANES), 1)
    hm = ((lane // 32) % 2 == head % 2).astype(F32)
    r = lax.broadcasted_iota(jnp.int32, (C, C), 0)
    c = lax.broadcasted_iota(jnp.int32, (C, C), 1)
    diff = ((c - r) if rev else (r - c)).astype(F32)
    mask = (diff > 0) if rev else (diff >= 0)
    dpos = jnp.maximum(diff, 0.0)
    din = jnp.where(mask, jnp.exp(lgh * dpos), 0.0)
    idx = lax.broadcasted_iota(jnp.int32, (C, 1), 0).astype(F32)
    eq = (C - idx) if rev else (idx + 1.0)
    ek = idx if rev else (C - 1.0 - idx)
    qd, kd = jnp.exp(lgh * eq), jnp.exp(lgh * ek)
    cd = jnp.exp(lgh * jnp.full((1, 1), float(C), F32))
    return hm, din, dpos, qd, kd, cd, eq, ek


RET_HEADS_PER_STEP = 4


def _ret_fwd(name, qt, kt, proj, lg, rev):
    S = qt.shape[0]
    C = CHUNK
    TB = _pick(S, (512, 256, 128))
    cb, nb = TB // C, S // TB
    hps = RET_HEADS_PER_STEP
    blk = (lambda g: nb - 1 - g) if rev else (lambda g: g)

    def body(lg_ref, q_ref, k_ref, v_ref, o_ref, st_ref, state_sc):
        hg, g = pl.program_id(0), pl.program_id(1)

        @pl.when(g == 0)
        def _():
            state_sc[...] = jnp.zeros_like(state_sc)

        consts = [_ret_consts(lg_ref[hg * hps + u], u, rev) for u in range(hps)]
        order = list(reversed(range(cb))) if rev else list(range(cb))
        units = [(cc, u) for cc in order for u in range(hps)]

        def operands(cc, u):
            rows = pl.ds(cc * C, C)
            pair = slice(LANES * (u // 2), LANES * (u // 2 + 1))
            hm = consts[u][0]
            return q_ref[rows, pair] * hm, k_ref[rows, pair] * hm, v_ref[rows, LANES * u:LANES * (u + 1)].astype(MXU)

        a, inc = {}, {}
        for cc, u in units:
            q, k, v = operands(cc, u)
            a[cc, u] = _dot(q.astype(MXU), k.astype(MXU), NT) * consts[u][1]
            inc[cc, u] = _dot((k * consts[u][4]).astype(MXU), v, TN)
        for u in range(hps):
            st = state_sc[u]
            for cc in order:
                st_ref[u, cc] = st
                st = st * consts[u][5] + inc[cc, u]
            state_sc[u] = st
        for cc, u in units:
            q, _, v = operands(cc, u)
            cross = _dot((q * consts[u][3]).astype(MXU), st_ref[u, cc].astype(MXU), NN)
            o_ref[pl.ds(cc * C, C), LANES * u:LANES * (u + 1)] = _dot(a[cc, u].astype(MXU), v, NN) + cross

    qk_spec = pl.BlockSpec((TB, LANES * hps // 2), lambda h, g: (blk(g), h))
    return pl.pallas_call(
        body, name=name, grid=(HEADS // hps, nb),
        in_specs=[pl.BlockSpec(memory_space=pltpu.SMEM), qk_spec, qk_spec,
                  pl.BlockSpec((TB, LANES * hps), lambda h, g: (blk(g), P_VR // (LANES * hps) + h))],
        out_specs=[pl.BlockSpec((TB, LANES * hps), lambda h, g: (blk(g), h)),
                   pl.BlockSpec((hps, cb, LANES, LANES), lambda h, g: (h, blk(g), 0, 0))],
        out_shape=[jax.ShapeDtypeStruct((S, HEADS * LANES), F32), jax.ShapeDtypeStruct((HEADS, S // C, LANES, LANES), F32)],
        scratch_shapes=[pltpu.VMEM((hps, LANES, LANES), F32)],
        compiler_params=pltpu.CompilerParams(dimension_semantics=("parallel", "arbitrary"), vmem_limit_bytes=VMEM_LIMIT),
    )(lg, qt, kt, proj)


def _ret_bwd(name, qt, kt, proj, dret, states, lg, rev):
    S = qt.shape[0]
    C = CHUNK
    TB = _pick(S, (512, 256, 128))
    cb, nb = TB // C, S // TB
    hps = RET_HEADS_PER_STEP
    blk = (lambda g: g) if rev else (lambda g: nb - 1 - g)

    def body(lg_ref, q_ref, k_ref, v_ref, do_ref, st_ref, dq_ref, dk_ref, dv_ref, dlg_ref, ds_sc, acc_cc, acc_q, acc_k, acc_s):
        hg, g = pl.program_id(0), pl.program_id(1)

        @pl.when(g == 0)
        def _():
            ds_sc[...] = jnp.zeros_like(ds_sc)
            acc_cc[...] = jnp.zeros_like(acc_cc)
            acc_q[...] = jnp.zeros_like(acc_q)
            acc_k[...] = jnp.zeros_like(acc_k)
            acc_s[...] = jnp.zeros_like(acc_s)

        lgs = [lg_ref[hg * hps + u] for u in range(hps)]
        consts = [_ret_consts(lgs[u], u, rev) for u in range(hps)]
        order = list(range(cb)) if rev else list(reversed(range(cb)))
        units = [(cc, u) for cc in order for u in range(hps)]

        def operands(cc, u):
            rows = pl.ds(cc * C, C)
            pair = slice(LANES * (u // 2), LANES * (u // 2 + 1))
            head = slice(LANES * u, LANES * (u + 1))
            hm = consts[u][0]
            return q_ref[rows, pair] * hm, k_ref[rows, pair] * hm, v_ref[rows, head].astype(MXU), do_ref[rows, head].astype(MXU)

        a, dp, dqs, inc = {}, {}, {}, {}
        for cc, u in units:
            q, k, vb, dob = operands(cc, u)
            a[cc, u] = _dot(q.astype(MXU), k.astype(MXU), NT)
            dp[cc, u] = _dot(dob, vb, NT)
            dqs[cc, u] = _dot(dob, st_ref[u, cc].astype(MXU), NT)
            inc[cc, u] = _dot((q * consts[u][3]).astype(MXU), dob, TN)
        dsn = {}
        for u in range(hps):
            ds = ds_sc[u]
            for cc in order:
                dsn[cc, u] = ds
                ds = ds * consts[u][5] + inc[cc, u]
            ds_sc[u] = ds
        even = {}
        for cc, u in units:
            hm, din, dpos, qd, kd, cd, eq, ek = consts[u]
            rows, head = pl.ds(cc * C, C), slice(LANES * u, LANES * (u + 1))
            q, k, vb, dob = operands(cc, u)
            qb, kb = q.astype(MXU), k.astype(MXU)
            dsnb = dsn[cc, u].astype(MXU)
            da = (dp[cc, u] * din).astype(MXU)
            vds = _dot(vb, dsnb, NT)
            dq_u = (_dot(da, kb, NN) + dqs[cc, u] * qd) * hm
            dk_u = (_dot(da, qb, TN) + vds * kd) * hm
            if u % 2 == 0:
                even[cc] = (dq_u, dk_u)
            else:
                pair = slice(LANES * (u // 2), LANES * (u // 2 + 1))
                dq_ref[rows, pair] = even[cc][0] + dq_u
                dk_ref[rows, pair] = even[cc][1] + dk_u
            dv_ref[rows, head] = _dot((a[cc, u] * din).astype(MXU), dob, TN) + _dot((k * kd).astype(MXU), dsnb, NN)
            acc_cc[u] += dp[cc, u] * a[cc, u] * din * dpos
            acc_q[u] += dqs[cc, u] * q * (qd * eq)
            acc_k[u] += vds * k * (kd * ek)
            acc_s[u] += dsn[cc, u] * st_ref[u, cc] * (cd * float(C))

        @pl.when(g == nb - 1)
        def _():
            for u in range(hps):
                tot = (jnp.sum(acc_cc[u], keepdims=True) + jnp.sum(acc_q[u], keepdims=True)
                       + jnp.sum(acc_k[u], keepdims=True) + jnp.sum(acc_s[u], keepdims=True))
                dlg_ref[u] = jnp.broadcast_to(tot * lgs[u], (8, LANES))

    full = jax.ShapeDtypeStruct((S, HEADS * LANES), F32)
    hspec = pl.BlockSpec((TB, LANES * hps), lambda h, g: (blk(g), h))
    qk_spec = pl.BlockSpec((TB, LANES * hps // 2), lambda h, g: (blk(g), h))
    return pl.pallas_call(
        body, name=name, grid=(HEADS // hps, nb),
        in_specs=[pl.BlockSpec(memory_space=pltpu.SMEM), qk_spec, qk_spec,
                  pl.BlockSpec((TB, LANES * hps), lambda h, g: (blk(g), P_VR // (LANES * hps) + h)),
                  hspec,
                  pl.BlockSpec((hps, cb, LANES, LANES), lambda h, g: (h, blk(g), 0, 0))],
        out_specs=[qk_spec, qk_spec, hspec, pl.BlockSpec((hps, 8, LANES), lambda h, g: (h, 0, 0))],
        out_shape=[jax.ShapeDtypeStruct(qt.shape, F32), jax.ShapeDtypeStruct(kt.shape, F32), full,
                   jax.ShapeDtypeStruct((HEADS, 8, LANES), F32)],
        scratch_shapes=[pltpu.VMEM((hps, LANES, LANES), F32), pltpu.VMEM((hps, C, C), F32), pltpu.VMEM((hps, C, LANES), F32),
                        pltpu.VMEM((hps, C, LANES), F32), pltpu.VMEM((hps, LANES, LANES), F32)],
        compiler_params=pltpu.CompilerParams(dimension_semantics=("parallel", "arbitrary"), vmem_limit_bytes=VMEM_LIMIT),
    )(lg, qt, kt, proj, dret, states)


def _rope_consts():
    inv16 = THETA ** (-jnp.arange(16, dtype=F32) / 16)
    inv32 = THETA ** (-jnp.arange(32, dtype=F32) / 32)
    lane = np.arange(LANES)
    z48 = jnp.zeros((48,), F32)
    inv_m = jnp.concatenate([inv16, z48, inv16, z48])[None, :]
    sgn_m = jnp.asarray(np.where(lane < 16, -1.0, np.where((lane >= 64) & (lane < 80), 1.0, 0.0)), F32)[None, :]
    inv_r = jnp.concatenate([inv32] * 4)[None, :]
    sgn_r = jnp.asarray(np.where(lane < 64, -1.0, 1.0), F32)[None, :]
    return inv_m, sgn_m, inv_r, sgn_r


FIRST_WEIGHTS = ("w_q_b", "w_kv_b")
EARLY_GRADS = ("w_down", "w_gate_up", "w_out", "w_ret_out")
MID_GRADS = ("w_mla_out", "w_in")


def _local_step(x, pos, tgt, gains, W, late_weights=None, grad_hook=None, start_after=None):
    S = x.shape[0]
    ts = _pick(S, (256, 128))
    ts_light = _pick(S, (512, 256, 128))
    R = lambda a, w=None, c=0: (a, ((a.shape[1] if w is None else w), c))
    W_ = lambda a: (a, None)

    win_mla = _win_pad_mla(W["w_in_head"] if "w_in_head" in W else W["w_in"][:, :O_QR])
    wq = _wq_pad(W["w_q_b"])
    wk, wv = _wkv_pad(W["w_kv_b"])
    gqn, gkn = _qk_pad(gains["g_qn"]), _qk_pad(gains["g_kn"])
    g_mix, g_q_a, g_kv_a, g_ffn = gains["g_mix"], gains["g_q_a"], gains["g_kv_a"], gains["g_ffn"]
    lg_f = -jnp.exp(gains["ret_decay_fwd"][0])
    lg_b = -jnp.exp(gains["ret_decay_bwd"][0])

    consts = list(_rope_consts())
    cosm, sinm, cosr, sinr = _rowwise("rope_tables", _tables_fn, S, ts_light,[R(pos)] + [W_(c) for c in consts],
                                      [(LANES, F32, LANES, 0)] * 4)

    (h,) = _rowwise("rms_mix", _rmsg_fn, S, ts_light,[R(x), W_(g_mix)], [(D_MODEL, MXU, D_MODEL, 0)])
    proj_mla = _mm("in_proj_mla", h, win_mla, "nn", after=start_after)
    mla_seg = lambda off, w: (proj_mla, (w, (off - P_CQ) // w))
    mla_ins = [mla_seg(P_CQ, 256), mla_seg(P_CKV, 128), mla_seg(P_KROPE, 128), R(cosm), R(sinm),
               W_(g_q_a), W_(g_kv_a), W_(gqn), W_(gkn), W_(wq), W_(wk), W_(wv)]
    q, k, v = _rowwise("mla_prep", _mla_prep_fn, S, ts, mla_ins, [(HEADS * LANES, MXU, HEADS * LANES, 0)] * 3)
    o_bf, lse = _flash_fwd(q, k, v)
    if late_weights is not None:
        W = {**W, **late_weights(lse)}
    win = _win_pad(W["w_in"])
    proj = _mm("in_proj", h, win[:, :P_CQ], "nn")
    seg = lambda off, w: (proj, (w, off // w))
    wmla = _wmla_pad(W["w_mla_out"])
    wret, wout, wgu, wdown = W["w_ret_out"], W["w_out"], W["w_gate_up"], W["w_down"]
    y_a = _mm("mla_out", o_bf, wmla, "nn")

    ret_ins = [seg(P_QR, 512), seg(P_KR, 512), R(cosr), R(sinr)]
    qt, kt = _rowwise("ret_prep", _ret_prep_fn, S, ts_light,ret_ins, [(512, F32, 512, 0)] * 2)
    ret_f, st_f = _ret_fwd("ret_fwd_f", qt, kt, proj, lg_f, False)
    ret_b, st_b = _ret_fwd("ret_fwd_b", qt, kt, proj, lg_b, True)
    post_ins = [R(ret_f), R(ret_b), seg(P_GR, 1024)]
    (o_b,) = _rowwise("ret_post", _ret_post_fn, S, ts_light,post_ins, [(1024, MXU, 1024, 0)])
    y_b, merged = _mm_rows("ret_out_merge", o_b, wret, lambda yb, ga, gb, ya: (yb, _merge_fn(ga, gb, ya, yb)),
                           [seg(P_GATES, 1024), (proj, (1024, 1)), R(y_a)], [], [F32, MXU])
    merge_ins = [seg(P_GATES, 1024), (proj, (1024, 1)), R(y_a), R(y_b)]
    def residual_rms(d, xx, g):
        r = d + xx
        return r, _rmsg_fn(r, g)

    x1, h2 = _mm_rows("out_proj_rms_ffn", merged, wout, residual_rms, [x], [g_ffn], [F32, MXU])
    gu, act = _gate_up_swiglu(h2, wgu)

    def residual_loss(d, xx, t):
        dx, rows = _loss_fn(d + xx, t)
        return dx, dx, rows

    dx2, dx2_bf, loss_rows = _mm_rows("down_proj_loss", act, wdown, residual_loss, [x1, tgt], [], [F32, MXU], accs=[(1, D_MODEL)])

    gW = {}
    gW["w_down"] = _mm("d_w_down", act, dx2_bf, "tn")
    dgu = _d_act_swiglu(dx2_bf, wdown, gu)
    gW["w_gate_up"] = _mm("d_w_gate_up", h2, dgu, "tn")
    dh2 = _mm("d_h2", dgu, wgu, "nt")

    def rms_bwd(xx, g, dh, dres):
        _, vjp = jax.vjp(_rmsg_fn, xx, g)
        dx, dg = vjp(dh)
        dx = dx + dres
        return dx, dx, dg

    dx1, dx1_bf, dg_ffn = _rowwise("rms_ffn_bwd", rms_bwd, S, ts_light,[R(x1), W_(g_ffn), R(dh2), R(dx2)],
                                   [(D_MODEL, F32, D_MODEL, 0), (D_MODEL, MXU, D_MODEL, 0)], accs=[(1, D_MODEL)])
    gW["w_out"] = _mm("d_w_out", merged, dx1_bf, "tn")
    def merge_bwd(dm, ga, gb, ya, yb):
        _, vjp = jax.vjp(_merge_fn, ga, gb, ya, yb)
        return vjp(dm)

    dga, dgb, dy_a, dy_b = _mm_rows("d_merged_merge_bwd", dx1_bf, wout, merge_bwd, merge_ins, [], [MXU] * 4, mode="nt")
    gW["w_ret_out"] = _mm("d_w_ret_out", o_b, dy_b, "tn")
    after_early = [] if grad_hook is None else [grad_hook({n: gW[n] for n in EARLY_GRADS})]

    def post_bwd(dob, rf, rb, gr, *_):
        _, vjp = jax.vjp(_ret_post_fn, rf, rb, gr)
        drf, _, dgr = vjp(dob)
        return drf, dgr

    dret, dg_r = _mm_rows("d_o_b_ret_post_bwd", dy_b, wret, post_bwd, post_ins, after_early, [MXU, MXU], mode="nt")
    dq_f, dk_f, dv_f, dlg_f = _ret_bwd("ret_bwd_f", qt, kt, proj, dret, st_f, lg_f, False)
    dq_b, dk_b, dv_b, dlg_b = _ret_bwd("ret_bwd_b", qt, kt, proj, dret, st_b, lg_b, True)

    def ret_prep_bwd(qr, kr, cosr_, sinr_, dqf, dqb, dkf, dkb, dvf, dvb):
        _, vjp = jax.vjp(lambda a, b: _ret_prep_fn(a, b, cosr_, sinr_), qr, kr)
        dqr, dkr = vjp((dqf + dqb, dkf + dkb))
        return dqr, dkr, dvf + dvb

    dq_r, dk_r, dv_r = _rowwise("ret_prep_bwd", ret_prep_bwd, S, ts_light,ret_ins + [R(t) for t in (dq_f, dq_b, dk_f, dk_b, dv_f, dv_b)],
                                [(512, MXU, 512, 0), (512, MXU, 512, 0), (1024, MXU, 1024, 0)])

    gW_mla_p = _mm("d_w_mla_out", o_bf, dy_a, "tn")
    do_bf, delta = _mm_rows("d_o_attn_delta", dy_a, wmla, lambda d, oo, *_: _delta_fn(oo.astype(F32), d), [o_bf], after_early, [MXU, F32], mode="nt")
    dq, dk, dv = _flash_bwd(q, k, v, do_bf, lse, delta)

    def mla_prep_bwd(cq, ckv, kr, cosm_, sinm_, gqa, gkva, gqn_, gkn_, wq_, wk_, wv_, dq_, dk_, dv_):
        f = lambda cq, ckv, kr, gqa, gkva, gqn_, gkn_, wq_, wk_, wv_: _mla_prep_fn(cq, ckv, kr, cosm_, sinm_, gqa, gkva, gqn_, gkn_, wq_, wk_, wv_)
        _, vjp = jax.vjp(f, cq, ckv, kr, gqa, gkva, gqn_, gkn_, wq_.astype(F32), wk_.astype(F32), wv_.astype(F32))
        return vjp((dq_, dk_, dv_))

    mb = _rowwise("mla_prep_bwd", mla_prep_bwd, S, ts, mla_ins + [R(dq), R(dk), R(dv)],
                  [(256, MXU, 256, 0), (128, MXU, 128, 0), (128, MXU, 128, 0)],
                  accs=[(1, 256), (1, 128), (1, LANES), (1, LANES), (256, HEADS * LANES), (128, HEADS * LANES), (128, HEADS * LANES)])
    dc_q, dc_kv, dk_rope, dg_q_a, dg_kv_a, dgqn_p, dgkn_p, dwq_p, dwk_p, dwv_p = mb

    dproj = jnp.concatenate([dga, dgb, dv_r, dg_r, dq_r, dk_r, dc_q, dc_kv, dk_rope], axis=1)
    gW["w_in"] = _win_unpad(_mm("d_w_in", h, dproj, "tn"))
    gW["w_mla_out"] = _wmla_unpad(gW_mla_p)
    after_mid = None if grad_hook is None else grad_hook({n: gW[n] for n in MID_GRADS})
    dh = _mm("d_h", dproj, win, "nt", after=after_mid)
    grad_x, dg_mix = _rowwise("rms_mix_bwd", lambda a, b, c, d, *_: rms_bwd(a, b, c, d)[1:], S, ts_light,
                              [R(x), W_(g_mix), R(dh), R(dx1)] + ([] if after_mid is None else [W_(after_mid)]),
                              [(D_MODEL, F32, D_MODEL, 0)], accs=[(1, D_MODEL)])
    gW["w_q_b"] = _wq_unpad(dwq_p)
    gW["w_kv_b"] = _wkv_unpad(dwk_p, dwv_p)
    gG = {"g_mix": dg_mix, "g_q_a": dg_q_a, "g_kv_a": dg_kv_a, "g_qn": _qk_unpad(dgqn_p),
          "g_kn": _qk_unpad(dgkn_p), "ret_decay_fwd": dlg_f[:, 0, 0][None, :], "ret_decay_bwd": dlg_b[:, 0, 0][None, :],
          "g_ffn": dg_ffn}
    return loss_rows, grad_x, gG, gW


MATS = [("w_in", (1024, 5536), 1), ("w_q_b", (256, 768), 1), ("w_kv_b", (128, 1024), 1), ("w_mla_out", (512, 1024), 1),
        ("w_ret_out", (1024, 1024), 0), ("w_out", (1024, 1024), 0), ("w_gate_up", (1024, 5632), 1), ("w_down", (2816, 1024), 0)]
GAINS = [("g_mix", 1024), ("g_q_a", 256), ("g_kv_a", 128), ("g_qn", 96), ("g_kn", 96), ("ret_decay_fwd", 8), ("ret_decay_bwd", 8),
         ("g_ffn", 1024)]
ORDER = ["g_mix", "w_in", "g_q_a", "w_q_b", "g_kv_a", "w_kv_b", "g_qn", "g_kn", "w_mla_out", "ret_decay_fwd", "ret_decay_bwd",
         "w_ret_out", "w_out", "g_ffn", "w_gate_up", "w_down"]
GAIN_LEN = sum(n for _, n in GAINS)
GAIN_PAD = -(-GAIN_LEN // LANES) * LANES


def _pack_gains(d):
    row = jnp.concatenate([d[n].reshape(1, ln).astype(F32) for n, ln in GAINS], axis=1)
    return jnp.pad(row, ((0, 0), (0, GAIN_PAD - GAIN_LEN)))


def _unpack_gains(row):
    out, off = {}, 0
    for n, ln in GAINS:
        out[n] = row[0, off:off + ln]
        off += ln
    return out


def _unshard(pieces, axis):
    if axis == 0:
        return pieces.reshape((N_DEV * pieces.shape[1], pieces.shape[2]))
    return jnp.concatenate([pieces[p] for p in range(N_DEV)], axis=1)


def _reshard(full, axis):
    if axis == 0:
        return full.reshape((N_DEV, full.shape[0] // N_DEV, full.shape[1]))
    c = full.shape[1] // N_DEV
    return jnp.stack([full[:, c * p:c * (p + 1)] for p in range(N_DEV)])


def _all_gather(shards):
    n = len(shards)

    def body(*refs):
        x_refs, out_refs = refs[:n], refs[n:2 * n]
        send_sems, recv_sems, local_sems = refs[2 * n:]
        x, y, c = lax.axis_index("x"), lax.axis_index("y"), lax.axis_index("c")
        me, sibling = (x, y, c), (x, y, 1 - c)
        chips = [(1 - x, y), (x, 1 - y), (1 - x, 1 - y)]

        def slot(a, px, py, pc):
            return out_refs[a].at[4 * px + 2 * py + pc]

        def copy(a, k, block, to, from_input=False):
            return pltpu.make_async_remote_copy(
                src_ref=x_refs[a] if from_input else slot(a, *block), dst_ref=slot(a, *block),
                send_sem=send_sems.at[a, k], recv_sem=recv_sems.at[a, k], device_id=to, device_id_type=pl.DeviceIdType.MESH)

        mine = [pltpu.make_async_copy(x_refs[a], slot(a, *me), local_sems.at[a]) for a in range(n)]
        first = [copy(a, 0, me, sibling, True) for a in range(n)]
        first += [copy(a, 1 + j, me, (*chip, c), True) for j, chip in enumerate(chips) for a in range(n)]
        for cp in mine + first:
            cp.start()
        passed = []
        for j, chip in enumerate(chips):
            for a in range(n):
                copy(a, 1 + j, (*chip, c), me).wait_recv()
                passed.append(copy(a, 4 + j, (*chip, c), sibling))
                passed[-1].start()
        for a in range(n):
            copy(a, 0, sibling, me).wait_recv()
        for j, chip in enumerate(chips):
            for a in range(n):
                copy(a, 4 + j, (*chip, 1 - c), me).wait_recv()
        for cp in first + passed:
            cp.wait_send()
        for cp in mine:
            cp.wait()

    any_spec = pl.BlockSpec(memory_space=pl.ANY)
    return pl.pallas_call(
        body, name="all_gather_weights", out_shape=[jax.ShapeDtypeStruct((N_DEV,) + s.shape, s.dtype) for s in shards],
        in_specs=[any_spec] * n, out_specs=[any_spec] * n,
        scratch_shapes=[pltpu.SemaphoreType.DMA((n, 7)), pltpu.SemaphoreType.DMA((n, 7)), pltpu.SemaphoreType.DMA((n,))],
    )(*shards)


def _all_to_all(pieces):
    n = len(pieces)

    def body(*refs):
        in_refs, out_refs = refs[:n], refs[n:2 * n]
        send_sems, recv_sems, local_sems = refs[2 * n:]
        x, y, c = lax.axis_index("x"), lax.axis_index("y"), lax.axis_index("c")
        my_id = 4 * x + 2 * y + c
        flips = [(fx, fy, fc) for fx in (0, 1) for fy in (0, 1) for fc in (0, 1)][1:]

        def copy(a, kk, f):
            p = (x ^ f[0], y ^ f[1], c ^ f[2])
            return pltpu.make_async_remote_copy(
                src_ref=in_refs[a].at[4 * p[0] + 2 * p[1] + p[2]], dst_ref=out_refs[a].at[my_id],
                send_sem=send_sems.at[a, kk], recv_sem=recv_sems.at[a, kk], device_id=p, device_id_type=pl.DeviceIdType.MESH)

        mine = [pltpu.make_async_copy(in_refs[a].at[my_id], out_refs[a].at[my_id], local_sems.at[a]) for a in range(n)]
        copies = [copy(a, kk, f) for kk, f in enumerate(flips) for a in range(n)]
        for cp in mine + copies:
            cp.start()
        for cp in copies:
            cp.wait_recv()
        for cp in copies:
            cp.wait_send()
        for cp in mine:
            cp.wait()

    any_spec = pl.BlockSpec(memory_space=pl.ANY)
    return pl.pallas_call(
        body, name="all_to_all_grads", out_shape=[jax.ShapeDtypeStruct(p.shape, p.dtype) for p in pieces],
        in_specs=[any_spec] * n, out_specs=[any_spec] * n,
        scratch_shapes=[pltpu.SemaphoreType.DMA((n, 7)), pltpu.SemaphoreType.DMA((n, 7)), pltpu.SemaphoreType.DMA((n,))],
    )(*pieces)


def _flip_peers(x, y, c):
    flips = [(fx, fy, fc) for fx in (0, 1) for fy in (0, 1) for fc in (0, 1)][1:]
    return [(x ^ fx, y ^ fy, c ^ fc) for fx, fy, fc in flips]


def _split_copies(in_refs, land_refs, send_sems, recv_sems, gather):
    x, y, c = lax.axis_index("x"), lax.axis_index("y"), lax.axis_index("c")
    my_id = 4 * x + 2 * y + c
    copies = []
    for kk, p in enumerate(_flip_peers(x, y, c)):
        for a in range(len(in_refs)):
            src = in_refs[a] if gather else in_refs[a].at[4 * p[0] + 2 * p[1] + p[2]]
            copies.append(pltpu.make_async_remote_copy(
                src_ref=src, dst_ref=land_refs[a].at[my_id], send_sem=send_sems.at[a * 7 + kk], recv_sem=recv_sems.at[a * 7 + kk],
                device_id=p, device_id_type=pl.DeviceIdType.MESH))
    return copies


def _exchange_start(name, srcs, gather, after=None):
    n = len(srcs)
    first_out = 2 * n + (0 if after is None else 1)

    def body(*refs):
        for cp in _split_copies(refs[:n], refs[n:2 * n], refs[first_out], refs[first_out + 1], gather):
            cp.start()
        refs[-1][...] = jnp.zeros_like(refs[-1])

    hbm, sem = pl.BlockSpec(memory_space=pltpu.HBM), pl.BlockSpec(memory_space=pltpu.SEMAPHORE)
    land_shapes = [((N_DEV,) + s.shape if gather else s.shape, s.dtype) for s in srcs]
    lands = [pltpu.with_memory_space_constraint(lax.empty(shp, dt), pltpu.HBM) for shp, dt in land_shapes]
    srcs = [pltpu.with_memory_space_constraint(s, pltpu.HBM) for s in srcs]
    res = pl.pallas_call(
        body, name=name,
        out_shape=[pltpu.SemaphoreType.DMA((7 * n,)), pltpu.SemaphoreType.DMA((7 * n,))] + [pltpu.HBM(s.shape, s.dtype) for s in srcs]
        + [pltpu.HBM(shp, dt) for shp, dt in land_shapes] + [jax.ShapeDtypeStruct((8, LANES), F32)],
        in_specs=[hbm] * (2 * n) + ([] if after is None else [pl.BlockSpec(memory_space=pl.ANY)]),
        out_specs=[sem, sem] + [hbm] * (2 * n) + [pl.BlockSpec(memory_space=pltpu.VMEM)],
        input_output_aliases={i: 2 + i for i in range(2 * n)},
        compiler_params=pltpu.CompilerParams(has_side_effects=pltpu.SideEffectType.DATAFLOW_SIDE_EFFECTING),
    )(*srcs, *lands, *([] if after is None else [after]))
    return res[0], res[1], res[2:2 + n], res[2 + n:2 + 2 * n], res[-1]


def _exchange_wait(name, handles, after, gather):
    send_sems, recv_sems, srcs, lands, _ = handles
    n = len(srcs)

    def body(*refs):
        for cp in _split_copies(refs[:n], refs[n:2 * n], refs[2 * n], refs[2 * n + 1], gather):
            cp.wait_send()
            cp.wait_recv()

    hbm, sem = pl.BlockSpec(memory_space=pltpu.HBM), pl.BlockSpec(memory_space=pltpu.SEMAPHORE)
    res = pl.pallas_call(
        body, name=name, out_shape=[pltpu.HBM(t.shape, t.dtype) for t in list(srcs) + list(lands)],
        in_specs=[hbm] * (2 * n) + [sem, sem, pl.BlockSpec(memory_space=pl.ANY)], out_specs=[hbm] * (2 * n),
        input_output_aliases={i: i for i in range(2 * n)},
        compiler_params=pltpu.CompilerParams(has_side_effects=pltpu.SideEffectType.DATAFLOW_SIDE_EFFECTING),
    )(*srcs, *lands, send_sems, recv_sems, after)
    my_id = 4 * lax.axis_index("x") + 2 * lax.axis_index("y") + lax.axis_index("c")
    own = [s if gather else lax.dynamic_index_in_dim(s, my_id, 0, keepdims=False) for s in res[:n]]
    return [lax.dynamic_update_index_in_dim(land, o, my_id, 0) for land, o in zip(res[n:], own)]


def _adamw(name, parts, w, m, v):
    rows, cols = w.shape
    tr = _pick(rows, (128, 64, 32, 16, 8))
    pspec = pl.BlockSpec((N_DEV, tr, cols), lambda i: (0, i, 0))
    rspec = pl.BlockSpec((tr, cols), lambda i: (i, 0))

    def body(p_ref, w_ref, m_ref, v_ref, g_ref, d_ref, m2_ref, v2_ref):
        g, d, m2, v2 = _adamw_fn([p_ref[s] for s in range(N_DEV)], w_ref[...], m_ref[...], v_ref[...])
        g_ref[...], d_ref[...], m2_ref[...], v2_ref[...] = g, d, m2, v2

    return pl.pallas_call(
        body, name=name, grid=(rows // tr,), in_specs=[pspec, rspec, rspec, rspec], out_specs=[rspec] * 4,
        out_shape=[jax.ShapeDtypeStruct((rows, cols), F32)] * 4,
        compiler_params=pltpu.CompilerParams(dimension_semantics=("parallel",), vmem_limit_bytes=VMEM_LIMIT),
    )(parts, w, m, v)


def kernel(x, positions, g_mix, w_in, g_q_a, w_q_b, g_kv_a, w_kv_b, g_qn, g_kn, w_mla_out, ret_decay_fwd, ret_decay_bwd, w_ret_out, w_out, g_ffn, w_gate_up, w_down, loss_target, m_g_mix, m_w_in, m_g_q_a, m_w_q_b, m_g_kv_a, m_w_kv_b, m_g_qn, m_g_kn, m_w_mla_out, m_ret_decay_fwd, m_ret_decay_bwd, m_w_ret_out, m_w_out, m_g_ffn, m_w_gate_up, m_w_down, v_g_mix, v_w_in, v_g_q_a, v_w_q_b, v_g_kv_a, v_w_kv_b, v_g_qn, v_g_kn, v_w_mla_out, v_ret_decay_fwd, v_ret_decay_bwd, v_w_ret_out, v_w_out, v_g_ffn, v_w_gate_up, v_w_down):
    w = dict(g_mix=g_mix, w_in=w_in, g_q_a=g_q_a, w_q_b=w_q_b, g_kv_a=g_kv_a, w_kv_b=w_kv_b, g_qn=g_qn, g_kn=g_kn, w_mla_out=w_mla_out,
             ret_decay_fwd=ret_decay_fwd, ret_decay_bwd=ret_decay_bwd, w_ret_out=w_ret_out, w_out=w_out, g_ffn=g_ffn,
             w_gate_up=w_gate_up, w_down=w_down)
    m = dict(g_mix=m_g_mix, w_in=m_w_in, g_q_a=m_g_q_a, w_q_b=m_w_q_b, g_kv_a=m_g_kv_a, w_kv_b=m_w_kv_b, g_qn=m_g_qn, g_kn=m_g_kn,
             w_mla_out=m_w_mla_out, ret_decay_fwd=m_ret_decay_fwd, ret_decay_bwd=m_ret_decay_bwd, w_ret_out=m_w_ret_out, w_out=m_w_out,
             g_ffn=m_g_ffn, w_gate_up=m_w_gate_up, w_down=m_w_down)
    v = dict(g_mix=v_g_mix, w_in=v_w_in, g_q_a=v_g_q_a, w_q_b=v_w_q_b, g_kv_a=v_g_kv_a, w_kv_b=v_w_kv_b, g_qn=v_g_qn, g_kn=v_g_kn,
             w_mla_out=v_w_mla_out, ret_decay_fwd=v_ret_decay_fwd, ret_decay_bwd=v_ret_decay_bwd, w_ret_out=v_w_ret_out, w_out=v_w_out,
             g_ffn=v_g_ffn, w_gate_up=v_w_gate_up, w_down=v_w_down)
    gains = {n: w[n].reshape(1, ln) for n, ln in GAINS}

    axis_of = {n: axis for n, _, axis in MATS}
    later = [n for n, _, _ in MATS if n not in FIRST_WEIGHTS]
    gathered = _all_gather([w["w_in"][:, :O_QR].astype(WIRE)] + [w[n].astype(WIRE) for n in FIRST_WEIGHTS])
    W = {n: _unshard(g, axis_of[n]) for n, g in zip(FIRST_WEIGHTS, gathered[1:])}
    W["w_in_head"] = gathered[0][0]
    later_handles = _exchange_start("gather_later_start", [w[n].astype(WIRE) for n in later], True, after=gathered[0])

    def late_weights(after):
        lands = _exchange_wait("gather_later_wait", later_handles, after, True)
        return {n: _unshard(g, axis_of[n]) for n, g in zip(later, lands)}

    grad_groups = []

    def grad_hook(g):
        names = tuple(g)
        handles = _exchange_start("grads_start_%d" % len(grad_groups), [_reshard(g[n], axis_of[n]).astype(GWIRE) for n in names], False)
        grad_groups.append((names, handles))
        return handles[4]

    S = x.shape[1]
    pos = positions.reshape(S, 1).astype(F32)
    loss_rows, grad_x, gG, gW = _local_step(x.reshape(S, D_MODEL), pos, loss_target.reshape(S, D_MODEL), gains, W, late_weights, grad_hook,
                                            start_after=later_handles[4])
    loss = lax.psum(jnp.sum(loss_rows), ("x", "y", "c"))

    last = [n for n, _, _ in MATS if n not in EARLY_GRADS + MID_GRADS]
    pieces = [_reshard(gW[n], axis_of[n]).astype(GWIRE) for n in last]
    pieces.append(jnp.broadcast_to(_pack_gains(gG)[None], (N_DEV, 1, GAIN_PAD)))
    late_parts = _all_to_all(pieces)
    parts = dict(zip(last, late_parts))
    for i, (names, handles) in enumerate(grad_groups):
        parts.update(zip(names, _exchange_wait("grads_wait_%d" % i, handles, late_parts[-1], False)))
    out = [dict() for _ in range(4)]
    for n, _, _ in MATS:
        for o, r in zip(out, _adamw("adamw_" + n, parts[n], w[n], m[n], v[n])):
            o[n] = r
    for o, r in zip(out, _adamw("adamw_gains", late_parts[-1], _pack_gains(w), _pack_gains(m), _pack_gains(v))):
        o.update(_unpack_gains(r))
    return (loss, grad_x.reshape(x.shape), *[o[n] for o in out for n in ORDER])
```

```python
import functools

import numpy as np
import jax
import jax.numpy as jnp
from jax import lax
from jax.experimental import pallas as pl
from jax.experimental.pallas import tpu as pltpu

F32 = jnp.float32
MXU = jnp.bfloat16
WIRE = jnp.bfloat16
GWIRE = jnp.bfloat16

N_DEV = 8
D_MODEL = 1024
HEADS = 8
LANES = 128
Q_RANK, KV_RANK = 256, 128
NOPE, ROPE_M, V_M = 64, 32, 64
QK_M = NOPE + ROPE_M
RQK, RV = 64, 128
CHUNK = 128
FFN = 2816
IN_WIDTH = 5536
THETA = 10000.0
EPS = 1e-6
LR, B1, B2, AEPS, WD, STEP = 0.001, 0.9, 0.999, 1e-08, 0.01, 10
VMEM_LIMIT = 56 * 1024 * 1024

NN = ((1,), (0,))
NT = ((1,), (1,))
TN = ((0,), (0,))

P_GATES, P_VR, P_GR, P_QR, P_KR, P_CQ, P_CKV, P_KROPE, P_WIDTH = 0, 2048, 3072, 4096, 4608, 5120, 5376, 5504, 5632
O_CQ, O_CKV, O_KROPE, O_QR, O_KR, O_VR, O_GR, O_GATES = 0, 256, 384, 416, 928, 1440, 2464, 3488


def _dot(a, b, dims):
    return lax.dot_general(a, b, (dims, ((), ())), preferred_element_type=F32)


def _pick(dim, cands):
    for c in cands:
        if dim % c == 0:
            return c
    return dim


def _pairs(t):
    return t.reshape(t.shape[0], 4, 2, 2, 32).transpose(0, 1, 3, 2, 4).reshape(t.shape[0], 512)


def _win_pad_mla(w):
    z = jnp.zeros((w.shape[0], 48), w.dtype)
    kr = w[:, O_KROPE:O_KROPE + 32]
    return jnp.concatenate([w[:, :O_CKV], w[:, O_CKV:O_KROPE], kr[:, :16], z, kr[:, 16:], z], axis=1)


def _win_pad(w):
    return jnp.concatenate([w[:, O_GATES:], w[:, O_VR:O_VR + 1024], w[:, O_GR:O_GR + 1024], _pairs(w[:, O_QR:O_QR + 512]),
                            _pairs(w[:, O_KR:O_KR + 512]), _win_pad_mla(w[:, :O_QR])], axis=1)


def _win_unpad(g):
    return jnp.concatenate([g[:, P_CQ:P_CQ + 256], g[:, P_CKV:P_CKV + 128], g[:, P_KROPE:P_KROPE + 16], g[:, P_KROPE + 64:P_KROPE + 80],
                            _pairs(g[:, P_QR:P_QR + 512]), _pairs(g[:, P_KR:P_KR + 512]), g[:, P_VR:P_VR + 1024],
                            g[:, P_GR:P_GR + 1024], g[:, P_GATES:P_GATES + 2048]], axis=1)


def _qk_pad(t):
    z = jnp.zeros(t.shape[:-1] + (32,), t.dtype)
    return jnp.concatenate([t[..., 64:80], t[..., 0:48], t[..., 80:96], t[..., 48:64], z], axis=-1)


def _qk_unpad(p):
    return jnp.concatenate([p[..., 16:64], p[..., 80:96], p[..., 0:16], p[..., 64:80]], axis=-1)


def _wq_pad(w):
    return _qk_pad(w.reshape(Q_RANK, HEADS, QK_M)).reshape(Q_RANK, HEADS * LANES)


def _wq_unpad(g):
    return _qk_unpad(g.reshape(Q_RANK, HEADS, LANES)).reshape(Q_RANK, HEADS * QK_M)


def _wkv_pad(w):
    t = w.reshape(KV_RANK, HEADS, NOPE + V_M)
    z = lambda n: jnp.zeros((KV_RANK, HEADS, n), w.dtype)
    wk = jnp.concatenate([z(16), t[..., 0:48], z(16), t[..., 48:64], z(32)], axis=-1)
    wv = jnp.concatenate([t[..., 64:128], z(64)], axis=-1)
    return wk.reshape(KV_RANK, HEADS * LANES), wv.reshape(KV_RANK, HEADS * LANES)


def _wkv_unpad(dwk, dwv):
    k, v = dwk.reshape(KV_RANK, HEADS, LANES), dwv.reshape(KV_RANK, HEADS, LANES)
    return jnp.concatenate([k[..., 16:64], k[..., 80:96], v[..., 0:64]], axis=-1).reshape(KV_RANK, HEADS * (NOPE + V_M))


def _wmla_pad(w):
    t = w.reshape(HEADS, V_M, D_MODEL)
    return jnp.concatenate([t, jnp.zeros_like(t)], axis=1).reshape(HEADS * LANES, D_MODEL)


def _wmla_unpad(g):
    return g.reshape(HEADS, LANES, D_MODEL)[:, :V_M].reshape(HEADS * V_M, D_MODEL)


def _rowwise(name, fn, rows, ts, ins, outs, accs=(), ncol=1):
    n_in, n_out, n_acc = len(ins), len(outs), len(accs)

    def colmap(col):
        if callable(col):
            return lambda i, j: (i, col(j))
        return lambda i, j: (i, col)

    arrays, in_specs = [], []
    for arr, spec in ins:
        arrays.append(arr)
        if spec is None:
            in_specs.append(pl.BlockSpec(arr.shape, functools.partial(lambda i, j, nd: (0,) * nd, nd=arr.ndim)))
        else:
            in_specs.append(pl.BlockSpec((ts, spec[0]), colmap(spec[1])))
    out_shape, out_specs = [], []
    for total, dtype, width, col in outs:
        out_shape.append(jax.ShapeDtypeStruct((rows, total), dtype))
        out_specs.append(pl.BlockSpec((ts, width), colmap(col)))
    for shp in accs:
        out_shape.append(jax.ShapeDtypeStruct(shp, F32))
        out_specs.append(pl.BlockSpec(shp, functools.partial(lambda i, j, nd: (0,) * nd, nd=len(shp))))

    def body(*refs):
        vals = [r[...] for r in refs[:n_in]]
        res = fn(*vals)
        if not isinstance(res, (tuple, list)):
            res = (res,)
        for r, v in zip(refs[n_in:n_in + n_out], res[:n_out]):
            r[...] = v.astype(r.dtype)
        if n_acc:
            first = jnp.logical_and(pl.program_id(0) == 0, pl.program_id(1) == 0)
            for r, v in zip(refs[n_in + n_out:], res[n_out:]):
                @pl.when(first)
                def _(r=r):
                    r[...] = jnp.zeros_like(r)
                r[...] += v.astype(F32)

    res = pl.pallas_call(
        body, name=name, grid=(rows // ts, ncol), in_specs=in_specs, out_specs=out_specs, out_shape=out_shape,
        compiler_params=pltpu.CompilerParams(dimension_semantics=("arbitrary", "arbitrary"), vmem_limit_bytes=VMEM_LIMIT),
    )(*arrays)
    return res


MM_OPERAND_BYTES = 24 * 1024 * 1024


def _mm(name, a, b, mode, add=None, after=None):
    a_halves, b_halves = a.ndim == 3, b.ndim == 3
    assert not a_halves or mode == "nt"
    assert not b_halves or mode == "tn"
    if mode == "nn":
        (M, K), N = a.shape, b.shape[1]
    elif mode == "nt":
        M, K, N = a.shape[-2], a.shape[-1] * (2 if a_halves else 1), b.shape[0]
    else:
        (K, M), N = a.shape, b.shape[-1] * (2 if b_halves else 1)
    tm = _pick(M, (1024, 512, 1408, 256, 128))
    tn = _pick(N // 2 if b_halves else N, (1408, 1024, 512, 256, 128))
    fits = lambda t: 2 * (tm + tn) * t * a.dtype.itemsize <= MM_OPERAND_BYTES
    kdiv = K // 2 if a_halves else K
    tk = next(t for t in (K, 4096, 2816, 2048, 1408, 1024, 512, 256, 128) if kdiv % t == 0 and (fits(t) or t == 128))
    nk = K // tk
    dims = {"nn": NN, "nt": NT, "tn": TN}[mode]
    if a_halves:
        per = kdiv // tk
        a_spec = pl.BlockSpec((None, tm, tk), lambda i, j, k: (k // per, i, k % per))
    else:
        a_spec = pl.BlockSpec((tk, tm), lambda i, j, k: (k, i)) if mode == "tn" else pl.BlockSpec((tm, tk), lambda i, j, k: (i, k))
    if b_halves:
        perj = (N // 2) // tn
        b_spec = pl.BlockSpec((None, tk, tn), lambda i, j, k: (j // perj, k, j % perj))
    else:
        b_spec = pl.BlockSpec((tn, tk), lambda i, j, k: (j, k)) if mode == "nt" else pl.BlockSpec((tk, tn), lambda i, j, k: (k, j))
    o_spec = pl.BlockSpec((tm, tn), lambda i, j, k: (i, j))
    has_add = add is not None

    def body(*refs):
        a_ref, b_ref, o_ref = refs[0], refs[1], refs[-1]
        d = _dot(a_ref[...], b_ref[...], dims)
        first = (d + refs[2][...]) if has_add else d
        if nk == 1:
            o_ref[...] = first
        else:
            k = pl.program_id(2)

            @pl.when(k == 0)
            def _():
                o_ref[...] = first

            @pl.when(k > 0)
            def _():
                o_ref[...] += d

    args = [a, b] + ([add] if has_add else []) + ([] if after is None else [after])
    specs = [a_spec, b_spec] + ([o_spec] if has_add else []) + ([] if after is None else [pl.BlockSpec(memory_space=pl.ANY)])
    return pl.pallas_call(
        body, name=name, grid=(M // tm, N // tn, nk), in_specs=specs, out_specs=o_spec,
        out_shape=jax.ShapeDtypeStruct((M, N), F32),
        compiler_params=pltpu.CompilerParams(dimension_semantics=("parallel", "parallel", "arbitrary"), vmem_limit_bytes=VMEM_LIMIT),
    )(*args)


def _mm_rows(name, a, b, fn, row_ins, whole_ins, outs, accs=(), mode="nn"):
    (M, K), N = a.shape, b.shape[1 if mode == "nn" else 0]
    tm = _pick(M, (512, 256, 128))
    n_in, n_out = 2 + len(row_ins) + len(whole_ins), len(outs)
    windows = [t if isinstance(t, tuple) else (t, (t.shape[1], 0)) for t in row_ins]
    row_ins = [t for t, _ in windows]
    row_specs = [pl.BlockSpec((tm, w), functools.partial(lambda i, col: (i, col), col=col)) for _, (w, col) in windows]

    def body(*refs):
        d = _dot(refs[0][...], refs[1][...], NN if mode == "nn" else NT)
        res = fn(d, *[r[...] for r in refs[2:n_in]])
        for r, v in zip(refs[n_in:n_in + n_out], res[:n_out]):
            r[...] = v.astype(r.dtype)
        for r, v in zip(refs[n_in + n_out:], res[n_out:]):
            @pl.when(pl.program_id(0) == 0)
            def _(r=r):
                r[...] = jnp.zeros_like(r)
            r[...] += v

    row = pl.BlockSpec((tm, N), lambda i: (i, 0))
    whole = lambda t: pl.BlockSpec(t.shape, functools.partial(lambda i, nd: (0,) * nd, nd=t.ndim))
    return pl.pallas_call(
        body, name=name, grid=(M // tm,),
        in_specs=[pl.BlockSpec((tm, K), lambda i: (i, 0)), whole(b)] + row_specs + [whole(t) for t in whole_ins],
        out_specs=[row] * n_out + [pl.BlockSpec(s, functools.partial(lambda i, nd: (0,) * nd, nd=len(s))) for s in accs],
        out_shape=[jax.ShapeDtypeStruct((M, N), dt) for dt in outs] + [jax.ShapeDtypeStruct(s, F32) for s in accs],
        compiler_params=pltpu.CompilerParams(dimension_semantics=("arbitrary",), vmem_limit_bytes=VMEM_LIMIT),
    )(a, b, *row_ins, *whole_ins)


def _ffn_tiles(S):
    return _pick(S, (1024, 512, 256, 128)), _pick(FFN, (1408, 704, 256, 128))


def _gate_up_swiglu(h2, wgu):
    S, K = h2.shape
    tm, tn = _ffn_tiles(S)
    nj = FFN // tn

    def body(a_ref, bg_ref, bu_ref, gu_ref, act_ref):
        a = a_ref[...]
        g, u = _dot(a, bg_ref[...], NN), _dot(a, bu_ref[...], NN)
        gu_ref[0], gu_ref[1] = g.astype(gu_ref.dtype), u.astype(gu_ref.dtype)
        act_ref[...] = _swiglu_fn(g, u).astype(act_ref.dtype)

    return pl.pallas_call(
        body, name="gate_up_swiglu", grid=(S // tm, nj),
        in_specs=[pl.BlockSpec((tm, K), lambda i, j: (i, 0)), pl.BlockSpec((K, tn), lambda i, j: (0, j)),
                  pl.BlockSpec((K, tn), lambda i, j: (0, nj + j))],
        out_specs=[pl.BlockSpec((2, tm, tn), lambda i, j: (0, i, j)), pl.BlockSpec((tm, tn), lambda i, j: (i, j))],
        out_shape=[jax.ShapeDtypeStruct((2, S, FFN), MXU), jax.ShapeDtypeStruct((S, FFN), MXU)],
        compiler_params=pltpu.CompilerParams(dimension_semantics=("parallel", "parallel"), vmem_limit_bytes=VMEM_LIMIT),
    )(h2, wgu, wgu)


def _d_act_swiglu(dx2, wdown, gu):
    S, K = dx2.shape
    tm, tn = _ffn_tiles(S)

    def body(a_ref, b_ref, gu_ref, o_ref):
        dact = _dot(a_ref[...], b_ref[...], NT)
        _, vjp = jax.vjp(_swiglu_fn, gu_ref[0].astype(F32), gu_ref[1].astype(F32))
        dg, du = vjp(dact)
        o_ref[0], o_ref[1] = dg.astype(o_ref.dtype), du.astype(o_ref.dtype)

    stacked = pl.BlockSpec((2, tm, tn), lambda i, j: (0, i, j))
    return pl.pallas_call(
        body, name="d_act_swiglu", grid=(S // tm, FFN // tn),
        in_specs=[pl.BlockSpec((tm, K), lambda i, j: (i, 0)), pl.BlockSpec((tn, K), lambda i, j: (j, 0)), stacked],
        out_specs=stacked, out_shape=jax.ShapeDtypeStruct((2, S, FFN), MXU),
        compiler_params=pltpu.CompilerParams(dimension_semantics=("parallel", "parallel"), vmem_limit_bytes=VMEM_LIMIT),
    )(dx2, wdown, gu)


@jax.custom_vjp
def _swap64(x):
    return pltpu.roll(x, 64, 1)


_swap64.defvjp(lambda x: (_swap64(x), None), lambda _, g: (_swap64(g),))


@jax.custom_vjp
def _mxdot(a, b):
    return _dot(a.astype(MXU), b.astype(MXU), NN)


def _mxdot_bwd(res, g):
    a, b = res
    gb = g.astype(MXU)
    return _dot(gb, b.astype(MXU), NT), _dot(a.astype(MXU), gb, TN)


_mxdot.defvjp(lambda a, b: (_mxdot(a, b), (a, b)), _mxdot_bwd)


def _row_sum(t):
    if t.shape[-1] == LANES:
        return lax.dot_general(t, jnp.ones((LANES, LANES), F32), ((NN), ((), ())), precision=lax.Precision.HIGH,
                               preferred_element_type=F32)
    return jnp.sum(t, axis=-1, keepdims=True)


@functools.partial(jax.custom_vjp, nondiff_argnums=(1,))
def _unit_rms(x, n):
    return x * lax.rsqrt(_row_sum(x * x) * (1.0 / n) + EPS)


def _unit_rms_fwd(x, n):
    r = lax.rsqrt(_row_sum(x * x) * (1.0 / n) + EPS)
    y = x * r
    return y, (y, r)


def _unit_rms_bwd(n, res, g):
    y, r = res
    return (r * (g - y * (_row_sum(g * y) * (1.0 / n))),)


_unit_rms.defvjp(_unit_rms_fwd, _unit_rms_bwd)


def _rms(x):
    return _unit_rms(x, x.shape[-1])


def _rmsg_fn(x, g):
    return _rms(x) * g


def _silu(x):
    return x * jax.nn.sigmoid(x)


def _tables_fn(pos, inv_m, sgn_m, inv_r, sgn_r):
    am, ar = pos * inv_m, pos * inv_r
    return jnp.cos(am), jnp.sin(am) * sgn_m, jnp.cos(ar), jnp.sin(ar) * sgn_r


def _head_blocks(t):
    return [t[:, LANES * h:LANES * (h + 1)] for h in range(t.shape[1] // LANES)]


def _mla_prep_fn(cq, ckv, kr, cosm, sinm, gqa, gkva, gqn, gkn, wq, wk, wv):
    cqn = _rms(cq) * gqa
    ckvn = _rms(ckv) * gkva
    q_raw = _mxdot(cqn, wq)
    k_raw = _mxdot(ckvn, wk)
    lane = lax.broadcasted_iota(jnp.int32, (1, HEADS * LANES), 1)
    v = _mxdot(ckvn, wv) + (lane % LANES == V_M).astype(F32)

    def norm_rope(blocks, g, extra):
        outs = []
        for b in blocks:
            if extra is not None:
                b = b + extra
            n = _unit_rms(b, QK_M) * g
            outs.append(n * cosm + _swap64(n) * sinm)
        return jnp.concatenate(outs, axis=1)

    q = norm_rope(_head_blocks(q_raw), gqn, None)
    k = norm_rope(_head_blocks(k_raw), gkn, kr)
    return q, k, v


def _ret_prep_fn(qr, kr, cosr, sinr):
    def rope(t, scale):
        return jnp.concatenate([(b * cosr + _swap64(b) * sinr) * scale for b in _head_blocks(t)], axis=1)
    return rope(qr, 1.0), rope(kr, RQK ** -0.5)


def _ret_post_fn(rf, rb, gr):
    ret = rf + rb
    outs = []
    for b, g in zip(_head_blocks(ret), _head_blocks(gr)):
        outs.append(_silu(g) * _rms(b))
    return jnp.concatenate(outs, axis=1)


def _merge_fn(ga, gb, ya, yb):
    return jax.nn.sigmoid(ga) * ya + jax.nn.sigmoid(gb) * yb


def _swiglu_fn(gate, up):
    return _silu(gate) * up


def _loss_fn(x2, tgt):
    d = x2 - tgt
    return d * (1.0 / D_MODEL), 0.5 * jnp.sum(d * d, axis=0, keepdims=True) * (1.0 / D_MODEL)


def _adamw_fn(parts, w, m, v):
    g = parts[0].astype(F32)
    for p in range(1, N_DEV):
        g = g + parts[p].astype(F32)
    m2 = B1 * m + (1.0 - B1) * g
    v2 = B2 * v + (1.0 - B2) * jnp.square(g)
    m_hat = m2 / (1.0 - B1 ** STEP)
    v_hat = v2 / (1.0 - B2 ** STEP)
    delta = -LR * (m_hat / (jnp.sqrt(v_hat) + AEPS) + WD * w)
    return g, delta, m2, v2


SCALE = QK_M ** -0.5
LOG2E = 1.4426950408889634
FLASH_ROWS = 32


def _flash_fwd(q, k, v):
    S = q.shape[0]
    tk = _pick(S, (512, 256, 128))
    tq = _pick(S, (1024, 512, 256, 128))
    ncb = tk // LANES
    nkv = S // tk
    assert nkv % 2 == 0, "kv tiles are processed in pairs"
    mrows = 64
    c = SCALE * LOG2E

    def body(q_ref, k_ref, v_ref, o_ref, lse_ref, s_a, p_a, s_b, p_b, m_sc, a_sc, acc_sc):
        m_sc[...] = jnp.full_like(m_sc, -jnp.inf)
        acc_sc[...] = jnp.zeros_like(acc_sc)
        qb = q_ref[...]

        def scores(j, s_buf):
            s_buf[...] = _dot(qb, k_ref[pl.ds(pl.multiple_of(j * tk, tk), tk), :], NT)

        def stage(j, s_buf, p_buf, s_next):
            scores(jnp.minimum(j + 1, nkv - 1), s_next)
            for r in range(tq // mrows):
                rows = slice(r * mrows, (r + 1) * mrows)
                cols = [s_buf[rows, LANES * cb:LANES * (cb + 1)] for cb in range(ncb)]
                m_prev = m_sc[rows, :]
                row_max = jnp.max(functools.reduce(jnp.maximum, cols), axis=-1, keepdims=True)
                m_new = jnp.maximum(m_prev, jnp.broadcast_to(row_max, (mrows, LANES)))
                a_sc[rows, :] = jnp.exp2((m_prev - m_new) * c)
                m_sc[rows, :] = m_new
                for cb in range(ncb):
                    p_buf[rows, LANES * cb:LANES * (cb + 1)] = jnp.exp2((cols[cb] - m_new) * c).astype(p_buf.dtype)
            acc_sc[...] = a_sc[...] * acc_sc[...] + _dot(p_buf[...], v_ref[pl.ds(pl.multiple_of(j * tk, tk), tk), :], NN)

        scores(0, s_a)

        def pair_step(t, carry):
            stage(2 * t, s_a, p_a, s_b)
            stage(2 * t + 1, s_b, p_b, s_a)
            return carry

        lax.fori_loop(0, nkv // 2, pair_step, 0, unroll=4)
        acc = acc_sc[...]
        lane = lax.broadcasted_iota(jnp.int32, (1, LANES), 1)
        l = jnp.sum(jnp.where(lane == V_M, acc, 0.0), axis=-1, keepdims=True)
        o_ref[...] = (acc / l).astype(o_ref.dtype)
        lse_ref[...] = m_sc[...] * c + jnp.log2(jnp.broadcast_to(l, (tq, LANES)))

    qspec = pl.BlockSpec((tq, LANES), lambda h, i: (i, h))
    kspec = pl.BlockSpec((S, LANES), lambda h, i: (0, h))
    return pl.pallas_call(
        body, name="flash_fwd", grid=(HEADS, S // tq), in_specs=[qspec, kspec, kspec], out_specs=[qspec, qspec],
        out_shape=[jax.ShapeDtypeStruct((S, HEADS * LANES), MXU), jax.ShapeDtypeStruct((S, HEADS * LANES), F32)],
        scratch_shapes=[pltpu.VMEM((tq, tk), F32), pltpu.VMEM((tq, tk), MXU)] * 2 + [pltpu.VMEM((tq, LANES), F32)] * 3,
        compiler_params=pltpu.CompilerParams(dimension_semantics=("parallel", "arbitrary"), vmem_limit_bytes=VMEM_LIMIT),
    )(q, k, v)


def _delta_fn(o, do):
    outs = [jnp.broadcast_to(jnp.sum(a * b, axis=-1, keepdims=True), a.shape) for a, b in zip(_head_blocks(o), _head_blocks(do))]
    return do, jnp.concatenate(outs, axis=1)


def _flash_bwd(q, k, v, do, lse, delta):
    S = q.shape[0]
    tq = tk = _pick(S, (512, 256, 128))
    ncb = tk // LANES
    c = SCALE * LOG2E

    nq = S // tq
    assert nq % 2 == 0, "q tiles are processed in pairs"

    def body(q_ref, k_ref, v_ref, do_ref, lse_ref, dl_ref, dq_ref, dk_ref, dv_ref, s_a, dp_a, p_a, ds_a, s_b, dp_b, p_b, ds_b):
        @pl.when(pl.program_id(1) == 0)
        def _():
            dq_ref[...] = jnp.zeros_like(dq_ref)

        dk_ref[...] = jnp.zeros_like(dk_ref)
        dv_ref[...] = jnp.zeros_like(dv_ref)
        kb, vb = k_ref[...], v_ref[...]

        def scores(i, s_buf, dp_buf):
            q_rows = pl.ds(pl.multiple_of(i * tq, tq), tq)
            s_buf[...] = _dot(q_ref[q_rows, :], kb, NT)
            dp_buf[...] = _dot(do_ref[q_rows, :], vb, NT)

        def stage(i, s_buf, dp_buf, p_buf, ds_buf, s_next, dp_next):
            scores(jnp.minimum(i + 1, nq - 1), s_next, dp_next)
            for r in range(tq // FLASH_ROWS):
                rows = slice(r * FLASH_ROWS, (r + 1) * FLASH_ROWS)
                grows = pl.ds(pl.multiple_of(i * tq + r * FLASH_ROWS, FLASH_ROWS), FLASH_ROWS)
                lse_b, dl_b = lse_ref[grows, :], dl_ref[grows, :]
                for cb in range(ncb):
                    sl = slice(LANES * cb, LANES * (cb + 1))
                    p = jnp.exp2(s_buf[rows, sl] * c - lse_b)
                    p_buf[rows, sl] = p.astype(p_buf.dtype)
                    ds_buf[rows, sl] = (p * (dp_buf[rows, sl] - dl_b) * SCALE).astype(ds_buf.dtype)
            q_rows = pl.ds(pl.multiple_of(i * tq, tq), tq)
            dv_ref[...] += _dot(p_buf[...], do_ref[q_rows, :], TN)
            dk_ref[...] += _dot(ds_buf[...], q_ref[q_rows, :], TN)
            dq_ref[q_rows, :] += _dot(ds_buf[...], kb, NN)

        scores(0, s_a, dp_a)

        def pair_step(t, carry):
            stage(2 * t, s_a, dp_a, p_a, ds_a, s_b, dp_b)
            stage(2 * t + 1, s_b, dp_b, p_b, ds_b, s_a, dp_a)
            return carry

        lax.fori_loop(0, nq // 2, pair_step, 0, unroll=2)

    hspec = pl.BlockSpec((S, LANES), lambda h, j: (0, h))
    kspec = pl.BlockSpec((tk, LANES), lambda h, j: (j, h))
    full = jax.ShapeDtypeStruct((S, HEADS * LANES), F32)
    tile_bufs = [pltpu.VMEM((tq, tk), F32), pltpu.VMEM((tq, tk), F32), pltpu.VMEM((tq, tk), MXU), pltpu.VMEM((tq, tk), MXU)]
    return pl.pallas_call(
        body, name="flash_bwd", grid=(HEADS, S // tk), in_specs=[hspec, kspec, kspec, hspec, hspec, hspec],
        out_specs=[hspec, kspec, kspec], out_shape=[full, full, full],
        scratch_shapes=tile_bufs + tile_bufs,
        compiler_params=pltpu.CompilerParams(dimension_semantics=("parallel", "arbitrary"), vmem_limit_bytes=VMEM_LIMIT),
    )(q, k, v, do, lse, delta)


def _ret_consts(lgh, head, rev):
    C = CHUNK
    lane = lax.broadcasted_iota(jnp.int32, (1, LANES), 1)
    hm = ((lane // 32) % 2 == head % 2).astype(F32)
    r = lax.broadcasted_iota(jnp.int32, (C, C), 0)
    c = lax.broadcasted_iota(jnp.int32, (C, C), 1)
    diff = ((c - r) if rev else (r - c)).astype(F32)
    mask = (diff > 0) if rev else (diff >= 0)
    dpos = jnp.maximum(diff, 0.0)
    din = jnp.where(mask, jnp.exp(lgh * dpos), 0.0)
    idx = lax.broadcasted_iota(jnp.int32, (C, 1), 0).astype(F32)
    eq = (C - idx) if rev else (idx + 1.0)
    ek = idx if rev else (C - 1.0 - idx)
    qd, kd = jnp.exp(lgh * eq), jnp.exp(lgh * ek)
    cd = jnp.exp(lgh * jnp.full((1, 1), float(C), F32))
    return hm, din, dpos, qd, kd, cd, eq, ek


RET_HEADS_PER_STEP = 4


def _ret_fwd(name, qt, kt, proj, lg, rev):
    S = qt.shape[0]
    C = CHUNK
    TB = _pick(S, (512, 256, 128))
    cb, nb = TB // C, S // TB
    hps = RET_HEADS_PER_STEP
    blk = (lambda g: nb - 1 - g) if rev else (lambda g: g)

    def body(lg_ref, q_ref, k_ref, v_ref, o_ref, st_ref, state_sc):
        hg, g = pl.program_id(0), pl.program_id(1)

        @pl.when(g == 0)
        def _():
            state_sc[...] = jnp.zeros_like(state_sc)

        consts = [_ret_consts(lg_ref[hg * hps + u], u, rev) for u in range(hps)]
        order = list(reversed(range(cb))) if rev else list(range(cb))
        units = [(cc, u) for cc in order for u in range(hps)]

        def operands(cc, u):
            rows = pl.ds(cc * C, C)
            pair = slice(LANES * (u // 2), LANES * (u // 2 + 1))
            hm = consts[u][0]
            return q_ref[rows, pair] * hm, k_ref[rows, pair] * hm, v_ref[rows, LANES * u:LANES * (u + 1)].astype(MXU)

        a, inc = {}, {}
        for cc, u in units:
            q, k, v = operands(cc, u)
            a[cc, u] = _dot(q.astype(MXU), k.astype(MXU), NT) * consts[u][1]
            inc[cc, u] = _dot((k * consts[u][4]).astype(MXU), v, TN)
        for u in range(hps):
            st = state_sc[u]
            for cc in order:
                st_ref[u, cc] = st
                st = st * consts[u][5] + inc[cc, u]
            state_sc[u] = st
        for cc, u in units:
            q, _, v = operands(cc, u)
            cross = _dot((q * consts[u][3]).astype(MXU), st_ref[u, cc].astype(MXU), NN)
            o_ref[pl.ds(cc * C, C), LANES * u:LANES * (u + 1)] = _dot(a[cc, u].astype(MXU), v, NN) + cross

    qk_spec = pl.BlockSpec((TB, LANES * hps // 2), lambda h, g: (blk(g), h))
    return pl.pallas_call(
        body, name=name, grid=(HEADS // hps, nb),
        in_specs=[pl.BlockSpec(memory_space=pltpu.SMEM), qk_spec, qk_spec,
                  pl.BlockSpec((TB, LANES * hps), lambda h, g: (blk(g), P_VR // (LANES * hps) + h))],
        out_specs=[pl.BlockSpec((TB, LANES * hps), lambda h, g: (blk(g), h)),
                   pl.BlockSpec((hps, cb, LANES, LANES), lambda h, g: (h, blk(g), 0, 0))],
        out_shape=[jax.ShapeDtypeStruct((S, HEADS * LANES), F32), jax.ShapeDtypeStruct((HEADS, S // C, LANES, LANES), F32)],
        scratch_shapes=[pltpu.VMEM((hps, LANES, LANES), F32)],
        compiler_params=pltpu.CompilerParams(dimension_semantics=("parallel", "arbitrary"), vmem_limit_bytes=VMEM_LIMIT),
    )(lg, qt, kt, proj)


def _ret_bwd(name, qt, kt, proj, dret, states, lg, rev):
    S = qt.shape[0]
    C = CHUNK
    TB = _pick(S, (512, 256, 128))
    cb, nb = TB // C, S // TB
    hps = RET_HEADS_PER_STEP
    blk = (lambda g: g) if rev else (lambda g: nb - 1 - g)

    def body(lg_ref, q_ref, k_ref, v_ref, do_ref, st_ref, dq_ref, dk_ref, dv_ref, dlg_ref, ds_sc, acc_cc, acc_q, acc_k, acc_s):
        hg, g = pl.program_id(0), pl.program_id(1)

        @pl.when(g == 0)
        def _():
            ds_sc[...] = jnp.zeros_like(ds_sc)
            acc_cc[...] = jnp.zeros_like(acc_cc)
            acc_q[...] = jnp.zeros_like(acc_q)
            acc_k[...] = jnp.zeros_like(acc_k)
            acc_s[...] = jnp.zeros_like(acc_s)

        lgs = [lg_ref[hg * hps + u] for u in range(hps)]
        consts = [_ret_consts(lgs[u], u, rev) for u in range(hps)]
        order = list(range(cb)) if rev else list(reversed(range(cb)))
        units = [(cc, u) for cc in order for u in range(hps)]

        def operands(cc, u):
            rows = pl.ds(cc * C, C)
            pair = slice(LANES * (u // 2), LANES * (u // 2 + 1))
            head = slice(LANES * u, LANES * (u + 1))
            hm = consts[u][0]
            return q_ref[rows, pair] * hm, k_ref[rows, pair] * hm, v_ref[rows, head].astype(MXU), do_ref[rows, head].astype(MXU)

        a, dp, dqs, inc = {}, {}, {}, {}
        for cc, u in units:
            q, k, vb, dob = operands(cc, u)
            a[cc, u] = _dot(q.astype(MXU), k.astype(MXU), NT)
            dp[cc, u] = _dot(dob, vb, NT)
            dqs[cc, u] = _dot(dob, st_ref[u, cc].astype(MXU), NT)
            inc[cc, u] = _dot((q * consts[u][3]).astype(MXU), dob, TN)
        dsn = {}
        for u in range(hps):
            ds = ds_sc[u]
            for cc in order:
                dsn[cc, u] = ds
                ds = ds * consts[u][5] + inc[cc, u]
            ds_sc[u] = ds
        even = {}
        for cc, u in units:
            hm, din, dpos, qd, kd, cd, eq, ek = consts[u]
            rows, head = pl.ds(cc * C, C), slice(LANES * u, LANES * (u + 1))
            q, k, vb, dob = operands(cc, u)
            qb, kb = q.astype(MXU), k.astype(MXU)
            dsnb = dsn[cc, u].astype(MXU)
            da = (dp[cc, u] * din).astype(MXU)
            vds = _dot(vb, dsnb, NT)
            dq_u = (_dot(da, kb, NN) + dqs[cc, u] * qd) * hm
            dk_u = (_dot(da, qb, TN) + vds * kd) * hm
            if u % 2 == 0:
                even[cc] = (dq_u, dk_u)
            else:
                pair = slice(LANES * (u // 2), LANES * (u // 2 + 1))
                dq_ref[rows, pair] = even[cc][0] + dq_u
                dk_ref[rows, pair] = even[cc][1] + dk_u
            dv_ref[rows, head] = _dot((a[cc, u] * din).astype(MXU), dob, TN) + _dot((k * kd).astype(MXU), dsnb, NN)
            acc_cc[u] += dp[cc, u] * a[cc, u] * din * dpos
            acc_q[u] += dqs[cc, u] * q * (qd * eq)
            acc_k[u] += vds * k * (kd * ek)
            acc_s[u] += dsn[cc, u] * st_ref[u, cc] * (cd * float(C))

        @pl.when(g == nb - 1)
        def _():
            for u in range(hps):
                tot = (jnp.sum(acc_cc[u], keepdims=True) + jnp.sum(acc_q[u], keepdims=True)
                       + jnp.sum(acc_k[u], keepdims=True) + jnp.sum(acc_s[u], keepdims=True))
                dlg_ref[u] = jnp.broadcast_to(tot * lgs[u], (8, LANES))

    full = jax.ShapeDtypeStruct((S, HEADS * LANES), F32)
    hspec = pl.BlockSpec((TB, LANES * hps), lambda h, g: (blk(g), h))
    qk_spec = pl.BlockSpec((TB, LANES * hps // 2), lambda h, g: (blk(g), h))
    return pl.pallas_call(
        body, name=name, grid=(HEADS // hps, nb),
        in_specs=[pl.BlockSpec(memory_space=pltpu.SMEM), qk_spec, qk_spec,
                  pl.BlockSpec((TB, LANES * hps), lambda h, g: (blk(g), P_VR // (LANES * hps) + h)),
                  hspec,
                  pl.BlockSpec((hps, cb, LANES, LANES), lambda h, g: (h, blk(g), 0, 0))],
        out_specs=[qk_spec, qk_spec, hspec, pl.BlockSpec((hps, 8, LANES), lambda h, g: (h, 0, 0))],
        out_shape=[jax.ShapeDtypeStruct(qt.shape, F32), jax.ShapeDtypeStruct(kt.shape, F32), full,
                   jax.ShapeDtypeStruct((HEADS, 8, LANES), F32)],
        scratch_shapes=[pltpu.VMEM((hps, LANES, LANES), F32), pltpu.VMEM((hps, C, C), F32), pltpu.VMEM((hps, C, LANES), F32),
                        pltpu.VMEM((hps, C, LANES), F32), pltpu.VMEM((hps, LANES, LANES), F32)],
        compiler_params=pltpu.CompilerParams(dimension_semantics=("parallel", "arbitrary"), vmem_limit_bytes=VMEM_LIMIT),
    )(lg, qt, kt, proj, dret, states)


def _rope_consts():
    inv16 = THETA ** (-jnp.arange(16, dtype=F32) / 16)
    inv32 = THETA ** (-jnp.arange(32, dtype=F32) / 32)
    lane = np.arange(LANES)
    z48 = jnp.zeros((48,), F32)
    inv_m = jnp.concatenate([inv16, z48, inv16, z48])[None, :]
    sgn_m = jnp.asarray(np.where(lane < 16, -1.0, np.where((lane >= 64) & (lane < 80), 1.0, 0.0)), F32)[None, :]
    inv_r = jnp.concatenate([inv32] * 4)[None, :]
    sgn_r = jnp.asarray(np.where(lane < 64, -1.0, 1.0), F32)[None, :]
    return inv_m, sgn_m, inv_r, sgn_r


FIRST_WEIGHTS = ("w_q_b", "w_kv_b")
EARLY_GRADS = ("w_down", "w_gate_up", "w_out", "w_ret_out")
MID_GRADS = ("w_mla_out", "w_in")


def _local_step(x, pos, tgt, gains, W, late_weights=None, grad_hook=None, start_after=None):
    S = x.shape[0]
    ts = _pick(S, (256, 128))
    ts_light = _pick(S, (512, 256, 128))
    R = lambda a, w=None, c=0: (a, ((a.shape[1] if w is None else w), c))
    W_ = lambda a: (a, None)

    win_mla = _win_pad_mla(W["w_in_head"] if "w_in_head" in W else W["w_in"][:, :O_QR])
    wq = _wq_pad(W["w_q_b"])
    wk, wv = _wkv_pad(W["w_kv_b"])
    gqn, gkn = _qk_pad(gains["g_qn"]), _qk_pad(gains["g_kn"])
    g_mix, g_q_a, g_kv_a, g_ffn = gains["g_mix"], gains["g_q_a"], gains["g_kv_a"], gains["g_ffn"]
    lg_f = -jnp.exp(gains["ret_decay_fwd"][0])
    lg_b = -jnp.exp(gains["ret_decay_bwd"][0])

    consts = list(_rope_consts())
    cosm, sinm, cosr, sinr = _rowwise("rope_tables", _tables_fn, S, ts_light,[R(pos)] + [W_(c) for c in consts],
                                      [(LANES, F32, LANES, 0)] * 4)

    (h,) = _rowwise("rms_mix", _rmsg_fn, S, ts_light,[R(x), W_(g_mix)], [(D_MODEL, MXU, D_MODEL, 0)])
    proj_mla = _mm("in_proj_mla", h, win_mla, "nn", after=start_after)
    mla_seg = lambda off, w: (proj_mla, (w, (off - P_CQ) // w))
    mla_ins = [mla_seg(P_CQ, 256), mla_seg(P_CKV, 128), mla_seg(P_KROPE, 128), R(cosm), R(sinm),
               W_(g_q_a), W_(g_kv_a), W_(gqn), W_(gkn), W_(wq), W_(wk), W_(wv)]
    q, k, v = _rowwise("mla_prep", _mla_prep_fn, S, ts, mla_ins, [(HEADS * LANES, MXU, HEADS * LANES, 0)] * 3)
    o_bf, lse = _flash_fwd(q, k, v)
    if late_weights is not None:
        W = {**W, **late_weights(lse)}
    win = _win_pad(W["w_in"])
    proj = _mm("in_proj", h, win[:, :P_CQ], "nn")
    seg = lambda off, w: (proj, (w, off // w))
    wmla = _wmla_pad(W["w_mla_out"])
    wret, wout, wgu, wdown = W["w_ret_out"], W["w_out"], W["w_gate_up"], W["w_down"]
    y_a = _mm("mla_out", o_bf, wmla, "nn")

    ret_ins = [seg(P_QR, 512), seg(P_KR, 512), R(cosr), R(sinr)]
    qt, kt = _rowwise("ret_prep", _ret_prep_fn, S, ts_light,ret_ins, [(512, F32, 512, 0)] * 2)
    ret_f, st_f = _ret_fwd("ret_fwd_f", qt, kt, proj, lg_f, False)
    ret_b, st_b = _ret_fwd("ret_fwd_b", qt, kt, proj, lg_b, True)
    post_ins = [R(ret_f), R(ret_b), seg(P_GR, 1024)]
    (o_b,) = _rowwise("ret_post", _ret_post_fn, S, ts_light,post_ins, [(1024, MXU, 1024, 0)])
    y_b, merged = _mm_rows("ret_out_merge", o_b, wret, lambda yb, ga, gb, ya: (yb, _merge_fn(ga, gb, ya, yb)),
                           [seg(P_GATES, 1024), (proj, (1024, 1)), R(y_a)], [], [F32, MXU])
    merge_ins = [seg(P_GATES, 1024), (proj, (1024, 1)), R(y_a), R(y_b)]
    def residual_rms(d, xx, g):
        r = d + xx
        return r, _rmsg_fn(r, g)

    x1, h2 = _mm_rows("out_proj_rms_ffn", merged, wout, residual_rms, [x], [g_ffn], [F32, MXU])
    gu, act = _gate_up_swiglu(h2, wgu)

    def residual_loss(d, xx, t):
        dx, rows = _loss_fn(d + xx, t)
        return dx, dx, rows

    dx2, dx2_bf, loss_rows = _mm_rows("down_proj_loss", act, wdown, residual_loss, [x1, tgt], [], [F32, MXU], accs=[(1, D_MODEL)])

    gW = {}
    gW["w_down"] = _mm("d_w_down", act, dx2_bf, "tn")
    dgu = _d_act_swiglu(dx2_bf, wdown, gu)
    gW["w_gate_up"] = _mm("d_w_gate_up", h2, dgu, "tn")
    dh2 = _mm("d_h2", dgu, wgu, "nt")

    def rms_bwd(xx, g, dh, dres):
        _, vjp = jax.vjp(_rmsg_fn, xx, g)
        dx, dg = vjp(dh)
        dx = dx + dres
        return dx, dx, dg

    dx1, dx1_bf, dg_ffn = _rowwise("rms_ffn_bwd", rms_bwd, S, ts_light,[R(x1), W_(g_ffn), R(dh2), R(dx2)],
                                   [(D_MODEL, F32, D_MODEL, 0), (D_MODEL, MXU, D_MODEL, 0)], accs=[(1, D_MODEL)])
    gW["w_out"] = _mm("d_w_out", merged, dx1_bf, "tn")
    def merge_bwd(dm, ga, gb, ya, yb):
        _, vjp = jax.vjp(_merge_fn, ga, gb, ya, yb)
        return vjp(dm)

    dga, dgb, dy_a, dy_b = _mm_rows("d_merged_merge_bwd", dx1_bf, wout, merge_bwd, merge_ins, [], [MXU] * 4, mode="nt")
    gW["w_ret_out"] = _mm("d_w_ret_out", o_b, dy_b, "tn")
    after_early = [] if grad_hook is None else [grad_hook({n: gW[n] for n in EARLY_GRADS})]

    def post_bwd(dob, rf, rb, gr, *_):
        _, vjp = jax.vjp(_ret_post_fn, rf, rb, gr)
        drf, _, dgr = vjp(dob)
        return drf, dgr

    dret, dg_r = _mm_rows("d_o_b_ret_post_bwd", dy_b, wret, post_bwd, post_ins, after_early, [MXU, MXU], mode="nt")
    dq_f, dk_f, dv_f, dlg_f = _ret_bwd("ret_bwd_f", qt, kt, proj, dret, st_f, lg_f, False)
    dq_b, dk_b, dv_b, dlg_b = _ret_bwd("ret_bwd_b", qt, kt, proj, dret, st_b, lg_b, True)

    def ret_prep_bwd(qr, kr, cosr_, sinr_, dqf, dqb, dkf, dkb, dvf, dvb):
        _, vjp = jax.vjp(lambda a, b: _ret_prep_fn(a, b, cosr_, sinr_), qr, kr)
        dqr, dkr = vjp((dqf + dqb, dkf + dkb))
        return dqr, dkr, dvf + dvb

    dq_r, dk_r, dv_r = _rowwise("ret_prep_bwd", ret_prep_bwd, S, ts_light,ret_ins + [R(t) for t in (dq_f, dq_b, dk_f, dk_b, dv_f, dv_b)],
                                [(512, MXU, 512, 0), (512, MXU, 512, 0), (1024, MXU, 1024, 0)])

    gW_mla_p = _mm("d_w_mla_out", o_bf, dy_a, "tn")
    do_bf, delta = _mm_rows("d_o_attn_delta", dy_a, wmla, lambda d, oo, *_: _delta_fn(oo.astype(F32), d), [o_bf], after_early, [MXU, F32], mode="nt")
    dq, dk, dv = _flash_bwd(q, k, v, do_bf, lse, delta)

    def mla_prep_bwd(cq, ckv, kr, cosm_, sinm_, gqa, gkva, gqn_, gkn_, wq_, wk_, wv_, dq_, dk_, dv_):
        f = lambda cq, ckv, kr, gqa, gkva, gqn_, gkn_, wq_, wk_, wv_: _mla_prep_fn(cq, ckv, kr, cosm_, sinm_, gqa, gkva, gqn_, gkn_, wq_, wk_, wv_)
        _, vjp = jax.vjp(f, cq, ckv, kr, gqa, gkva, gqn_, gkn_, wq_.astype(F32), wk_.astype(F32), wv_.astype(F32))
        return vjp((dq_, dk_, dv_))

    mb = _rowwise("mla_prep_bwd", mla_prep_bwd, S, ts, mla_ins + [R(dq), R(dk), R(dv)],
                  [(256, MXU, 256, 0), (128, MXU, 128, 0), (128, MXU, 128, 0)],
                  accs=[(1, 256), (1, 128), (1, LANES), (1, LANES), (256, HEADS * LANES), (128, HEADS * LANES), (128, HEADS * LANES)])
    dc_q, dc_kv, dk_rope, dg_q_a, dg_kv_a, dgqn_p, dgkn_p, dwq_p, dwk_p, dwv_p = mb

    dproj = jnp.concatenate([dga, dgb, dv_r, dg_r, dq_r, dk_r, dc_q, dc_kv, dk_rope], axis=1)
    gW["w_in"] = _win_unpad(_mm("d_w_in", h, dproj, "tn"))
    gW["w_mla_out"] = _wmla_unpad(gW_mla_p)
    after_mid = None if grad_hook is None else grad_hook({n: gW[n] for n in MID_GRADS})
    dh = _mm("d_h", dproj, win, "nt", after=after_mid)
    grad_x, dg_mix = _rowwise("rms_mix_bwd", lambda a, b, c, d, *_: rms_bwd(a, b, c, d)[1:], S, ts_light,
                              [R(x), W_(g_mix), R(dh), R(dx1)] + ([] if after_mid is None else [W_(after_mid)]),
                              [(D_MODEL, F32, D_MODEL, 0)], accs=[(1, D_MODEL)])
    gW["w_q_b"] = _wq_unpad(dwq_p)
    gW["w_kv_b"] = _wkv_unpad(dwk_p, dwv_p)
    gG = {"g_mix": dg_mix, "g_q_a": dg_q_a, "g_kv_a": dg_kv_a, "g_qn": _qk_unpad(dgqn_p),
          "g_kn": _qk_unpad(dgkn_p), "ret_decay_fwd": dlg_f[:, 0, 0][None, :], "ret_decay_bwd": dlg_b[:, 0, 0][None, :],
          "g_ffn": dg_ffn}
    return loss_rows, grad_x, gG, gW


MATS = [("w_in", (1024, 5536), 1), ("w_q_b", (256, 768), 1), ("w_kv_b", (128, 1024), 1), ("w_mla_out", (512, 1024), 1),
        ("w_ret_out", (1024, 1024), 0), ("w_out", (1024, 1024), 0), ("w_gate_up", (1024, 5632), 1), ("w_down", (2816, 1024), 0)]
GAINS = [("g_mix", 1024), ("g_q_a", 256), ("g_kv_a", 128), ("g_qn", 96), ("g_kn", 96), ("ret_decay_fwd", 8), ("ret_decay_bwd", 8),
         ("g_ffn", 1024)]
ORDER = ["g_mix", "w_in", "g_q_a", "w_q_b", "g_kv_a", "w_kv_b", "g_qn", "g_kn", "w_mla_out", "ret_decay_fwd", "ret_decay_bwd",
         "w_ret_out", "w_out", "g_ffn", "w_gate_up", "w_down"]
GAIN_LEN = sum(n for _, n in GAINS)
GAIN_PAD = -(-GAIN_LEN // LANES) * LANES


def _pack_gains(d):
    row = jnp.concatenate([d[n].reshape(1, ln).astype(F32) for n, ln in GAINS], axis=1)
    return jnp.pad(row, ((0, 0), (0, GAIN_PAD - GAIN_LEN)))


def _unpack_gains(row):
    out, off = {}, 0
    for n, ln in GAINS:
        out[n] = row[0, off:off + ln]
        off += ln
    return out


def _unshard(pieces, axis):
    if axis == 0:
        return pieces.reshape((N_DEV * pieces.shape[1], pieces.shape[2]))
    return jnp.concatenate([pieces[p] for p in range(N_DEV)], axis=1)


def _reshard(full, axis):
    if axis == 0:
        return full.reshape((N_DEV, full.shape[0] // N_DEV, full.shape[1]))
    c = full.shape[1] // N_DEV
    return jnp.stack([full[:, c * p:c * (p + 1)] for p in range(N_DEV)])


def _exchange(name, srcs, gather):
    n = len(srcs)

    def body(*refs):
        in_refs, out_refs = refs[:n], refs[n:2 * n]
        send_sems, recv_sems, local_sems = refs[2 * n:]
        my_id = 4 * lax.axis_index("x") + 2 * lax.axis_index("y") + lax.axis_index("c")
        mine = [pltpu.make_async_copy(in_refs[a] if gather else in_refs[a].at[my_id], out_refs[a].at[my_id], local_sems.at[a])
                for a in range(n)]
        copies = _split_copies(in_refs, out_refs, send_sems, recv_sems, gather)
        for cp in mine + copies:
            cp.start()
        for cp in copies:
            cp.wait_recv()
        for cp in copies:
            cp.wait_send()
        for cp in mine:
            cp.wait()

    any_spec = pl.BlockSpec(memory_space=pl.ANY)
    return pl.pallas_call(
        body, name=name,
        out_shape=[jax.ShapeDtypeStruct((N_DEV,) + s.shape if gather else s.shape, s.dtype) for s in srcs],
        in_specs=[any_spec] * n, out_specs=[any_spec] * n,
        scratch_shapes=[pltpu.SemaphoreType.DMA((7 * n,)), pltpu.SemaphoreType.DMA((7 * n,)), pltpu.SemaphoreType.DMA((n,))],
    )(*srcs)


def _flip_peers(x, y, c):
    flips = [(fx, fy, fc) for fx in (0, 1) for fy in (0, 1) for fc in (0, 1)][1:]
    return [(x ^ fx, y ^ fy, c ^ fc) for fx, fy, fc in flips]


def _split_copies(in_refs, land_refs, send_sems, recv_sems, gather):
    x, y, c = lax.axis_index("x"), lax.axis_index("y"), lax.axis_index("c")
    my_id = 4 * x + 2 * y + c
    copies = []
    for kk, p in enumerate(_flip_peers(x, y, c)):
        for a in range(len(in_refs)):
            src = in_refs[a] if gather else in_refs[a].at[4 * p[0] + 2 * p[1] + p[2]]
            copies.append(pltpu.make_async_remote_copy(
                src_ref=src, dst_ref=land_refs[a].at[my_id], send_sem=send_sems.at[a * 7 + kk], recv_sem=recv_sems.at[a * 7 + kk],
                device_id=p, device_id_type=pl.DeviceIdType.MESH))
    return copies


def _exchange_start(name, srcs, gather, after=None):
    n = len(srcs)
    first_out = 2 * n + (0 if after is None else 1)

    def body(*refs):
        for cp in _split_copies(refs[:n], refs[n:2 * n], refs[first_out], refs[first_out + 1], gather):
            cp.start()
        refs[-1][...] = jnp.zeros_like(refs[-1])

    hbm, sem = pl.BlockSpec(memory_space=pltpu.HBM), pl.BlockSpec(memory_space=pltpu.SEMAPHORE)
    land_shapes = [((N_DEV,) + s.shape if gather else s.shape, s.dtype) for s in srcs]
    lands = [pltpu.with_memory_space_constraint(lax.empty(shp, dt), pltpu.HBM) for shp, dt in land_shapes]
    srcs = [pltpu.with_memory_space_constraint(s, pltpu.HBM) for s in srcs]
    res = pl.pallas_call(
        body, name=name,
        out_shape=[pltpu.SemaphoreType.DMA((7 * n,)), pltpu.SemaphoreType.DMA((7 * n,))] + [pltpu.HBM(s.shape, s.dtype) for s in srcs]
        + [pltpu.HBM(shp, dt) for shp, dt in land_shapes] + [jax.ShapeDtypeStruct((8, LANES), F32)],
        in_specs=[hbm] * (2 * n) + ([] if after is None else [pl.BlockSpec(memory_space=pl.ANY)]),
        out_specs=[sem, sem] + [hbm] * (2 * n) + [pl.BlockSpec(memory_space=pltpu.VMEM)],
        input_output_aliases={i: 2 + i for i in range(2 * n)},
        compiler_params=pltpu.CompilerParams(has_side_effects=pltpu.SideEffectType.DATAFLOW_SIDE_EFFECTING),
    )(*srcs, *lands, *([] if after is None else [after]))
    return res[0], res[1], res[2:2 + n], res[2 + n:2 + 2 * n], res[-1]


def _exchange_wait(name, handles, after, gather):
    send_sems, recv_sems, srcs, lands, _ = handles
    n = len(srcs)

    def body(*refs):
        for cp in _split_copies(refs[:n], refs[n:2 * n], refs[2 * n], refs[2 * n + 1], gather):
            cp.wait_send()
            cp.wait_recv()

    hbm, sem = pl.BlockSpec(memory_space=pltpu.HBM), pl.BlockSpec(memory_space=pltpu.SEMAPHORE)
    res = pl.pallas_call(
        body, name=name, out_shape=[pltpu.HBM(t.shape, t.dtype) for t in list(srcs) + list(lands)],
        in_specs=[hbm] * (2 * n) + [sem, sem, pl.BlockSpec(memory_space=pl.ANY)], out_specs=[hbm] * (2 * n),
        input_output_aliases={i: i for i in range(2 * n)},
        compiler_params=pltpu.CompilerParams(has_side_effects=pltpu.SideEffectType.DATAFLOW_SIDE_EFFECTING),
    )(*srcs, *lands, send_sems, recv_sems, after)
    my_id = 4 * lax.axis_index("x") + 2 * lax.axis_index("y") + lax.axis_index("c")
    own = [s if gather else lax.dynamic_index_in_dim(s, my_id, 0, keepdims=False) for s in res[:n]]
    return [lax.dynamic_update_index_in_dim(land, o, my_id, 0) for land, o in zip(res[n:], own)]


def _adamw(name, parts, w, m, v):
    rows, cols = w.shape
    tr = _pick(rows, (128, 64, 32, 16, 8))
    pspec = pl.BlockSpec((N_DEV, tr, cols), lambda i: (0, i, 0))
    rspec = pl.BlockSpec((tr, cols), lambda i: (i, 0))

    def body(p_ref, w_ref, m_ref, v_ref, g_ref, d_ref, m2_ref, v2_ref):
        g, d, m2, v2 = _adamw_fn([p_ref[s] for s in range(N_DEV)], w_ref[...], m_ref[...], v_ref[...])
        g_ref[...], d_ref[...], m2_ref[...], v2_ref[...] = g, d, m2, v2

    return pl.pallas_call(
        body, name=name, grid=(rows // tr,), in_specs=[pspec, rspec, rspec, rspec], out_specs=[rspec] * 4,
        out_shape=[jax.ShapeDtypeStruct((rows, cols), F32)] * 4,
        compiler_params=pltpu.CompilerParams(dimension_semantics=("parallel",), vmem_limit_bytes=VMEM_LIMIT),
    )(parts, w, m, v)


def kernel(x, positions, g_mix, w_in, g_q_a, w_q_b, g_kv_a, w_kv_b, g_qn, g_kn, w_mla_out, ret_decay_fwd, ret_decay_bwd, w_ret_out, w_out, g_ffn, w_gate_up, w_down, loss_target, m_g_mix, m_w_in, m_g_q_a, m_w_q_b, m_g_kv_a, m_w_kv_b, m_g_qn, m_g_kn, m_w_mla_out, m_ret_decay_fwd, m_ret_decay_bwd, m_w_ret_out, m_w_out, m_g_ffn, m_w_gate_up, m_w_down, v_g_mix, v_w_in, v_g_q_a, v_w_q_b, v_g_kv_a, v_w_kv_b, v_g_qn, v_g_kn, v_w_mla_out, v_ret_decay_fwd, v_ret_decay_bwd, v_w_ret_out, v_w_out, v_g_ffn, v_w_gate_up, v_w_down):
    w = dict(g_mix=g_mix, w_in=w_in, g_q_a=g_q_a, w_q_b=w_q_b, g_kv_a=g_kv_a, w_kv_b=w_kv_b, g_qn=g_qn, g_kn=g_kn, w_mla_out=w_mla_out,
             ret_decay_fwd=ret_decay_fwd, ret_decay_bwd=ret_decay_bwd, w_ret_out=w_ret_out, w_out=w_out, g_ffn=g_ffn,
             w_gate_up=w_gate_up, w_down=w_down)
    m = dict(g_mix=m_g_mix, w_in=m_w_in, g_q_a=m_g_q_a, w_q_b=m_w_q_b, g_kv_a=m_g_kv_a, w_kv_b=m_w_kv_b, g_qn=m_g_qn, g_kn=m_g_kn,
             w_mla_out=m_w_mla_out, ret_decay_fwd=m_ret_decay_fwd, ret_decay_bwd=m_ret_decay_bwd, w_ret_out=m_w_ret_out, w_out=m_w_out,
             g_ffn=m_g_ffn, w_gate_up=m_w_gate_up, w_down=m_w_down)
    v = dict(g_mix=v_g_mix, w_in=v_w_in, g_q_a=v_g_q_a, w_q_b=v_w_q_b, g_kv_a=v_g_kv_a, w_kv_b=v_w_kv_b, g_qn=v_g_qn, g_kn=v_g_kn,
             w_mla_out=v_w_mla_out, ret_decay_fwd=v_ret_decay_fwd, ret_decay_bwd=v_ret_decay_bwd, w_ret_out=v_w_ret_out, w_out=v_w_out,
             g_ffn=v_g_ffn, w_gate_up=v_w_gate_up, w_down=v_w_down)
    gains = {n: w[n].reshape(1, ln) for n, ln in GAINS}

    axis_of = {n: axis for n, _, axis in MATS}
    later = [n for n, _, _ in MATS if n not in FIRST_WEIGHTS]
    gathered = _exchange("gather_first", [w["w_in"][:, :O_QR].astype(WIRE)] + [w[n].astype(WIRE) for n in FIRST_WEIGHTS], True)
    W = {n: _unshard(g, axis_of[n]) for n, g in zip(FIRST_WEIGHTS, gathered[1:])}
    W["w_in_head"] = gathered[0][0]
    later_handles = _exchange_start("gather_later_start", [w[n].astype(WIRE) for n in later], True, after=gathered[0])

    def late_weights(after):
        lands = _exchange_wait("gather_later_wait", later_handles, after, True)
        return {n: _unshard(g, axis_of[n]) for n, g in zip(later, lands)}

    grad_groups = []

    def grad_hook(g):
        names = tuple(g)
        handles = _exchange_start("grads_start_%d" % len(grad_groups), [_reshard(g[n], axis_of[n]).astype(GWIRE) for n in names], False)
        grad_groups.append((names, handles))
        return handles[4]

    S = x.shape[1]
    pos = positions.reshape(S, 1).astype(F32)
    loss_rows, grad_x, gG, gW = _local_step(x.reshape(S, D_MODEL), pos, loss_target.reshape(S, D_MODEL), gains, W, late_weights, grad_hook,
                                            start_after=later_handles[4])
    loss = lax.psum(jnp.sum(loss_rows), ("x", "y", "c"))

    last = [n for n, _, _ in MATS if n not in EARLY_GRADS + MID_GRADS]
    pieces = [_reshard(gW[n], axis_of[n]).astype(GWIRE) for n in last]
    pieces.append(jnp.broadcast_to(_pack_gains(gG)[None], (N_DEV, 1, GAIN_PAD)))
    late_parts = _exchange("grads_last", pieces, False)
    parts = dict(zip(last, late_parts))
    for i, (names, handles) in enumerate(grad_groups):
        parts.update(zip(names, _exchange_wait("grads_wait_%d" % i, handles, late_parts[-1], False)))
    out = [dict() for _ in range(4)]
    for n, _, _ in MATS:
        for o, r in zip(out, _adamw("adamw_" + n, parts[n], w[n], m[n], v[n])):
            o[n] = r
    for o, r in zip(out, _adamw("adamw_gains", late_parts[-1], _pack_gains(w), _pack_gains(m), _pack_gains(v))):
        o.update(_unpack_gains(r))
    return (loss, grad_x.reshape(x.shape), *[o[n] for o in out for n in ORDER])
```

```python
import functools

import numpy as np
import jax
import jax.numpy as jnp
from jax import lax
from jax.experimental import pallas as pl
from jax.experimental.pallas import tpu as pltpu

F32 = jnp.float32
MXU = jnp.bfloat16
WIRE = jnp.bfloat16
GWIRE = jnp.bfloat16

N_DEV = 8
D_MODEL = 1024
HEADS = 8
LANES = 128
Q_RANK, KV_RANK = 256, 128
NOPE, ROPE_M, V_M = 64, 32, 64
QK_M = NOPE + ROPE_M
RQK, RV = 64, 128
CHUNK = 128
FFN = 2816
IN_WIDTH = 5536
THETA = 10000.0
EPS = 1e-6
LR, B1, B2, AEPS, WD, STEP = 0.001, 0.9, 0.999, 1e-08, 0.01, 10
VMEM_LIMIT = 56 * 1024 * 1024

NN = ((1,), (0,))
NT = ((1,), (1,))
TN = ((0,), (0,))

P_GATES, P_VR, P_GR, P_QR, P_KR, P_CQ, P_CKV, P_KROPE, P_WIDTH = 0, 2048, 3072, 4096, 4608, 5120, 5376, 5504, 5632
O_CQ, O_CKV, O_KROPE, O_QR, O_KR, O_VR, O_GR, O_GATES = 0, 256, 384, 416, 928, 1440, 2464, 3488


def _dot(a, b, dims):
    return lax.dot_general(a, b, (dims, ((), ())), preferred_element_type=F32)


def _pick(dim, cands):
    for c in cands:
        if dim % c == 0:
            return c
    return dim


def _pairs(t):
    return t.reshape(t.shape[0], 4, 2, 2, 32).transpose(0, 1, 3, 2, 4).reshape(t.shape[0], 512)


def _win_pad(w):
    z = jnp.zeros((w.shape[0], 48), w.dtype)
    kr = w[:, O_KROPE:O_KROPE + 32]
    return jnp.concatenate([w[:, O_GATES:], w[:, O_VR:O_VR + 1024], w[:, O_GR:O_GR + 1024], _pairs(w[:, O_QR:O_QR + 512]),
                            _pairs(w[:, O_KR:O_KR + 512]), w[:, :O_CKV], w[:, O_CKV:O_KROPE], kr[:, :16], z, kr[:, 16:], z], axis=1)


def _win_unpad(g):
    return jnp.concatenate([g[:, P_CQ:P_CQ + 256], g[:, P_CKV:P_CKV + 128], g[:, P_KROPE:P_KROPE + 16], g[:, P_KROPE + 64:P_KROPE + 80],
                            _pairs(g[:, P_QR:P_QR + 512]), _pairs(g[:, P_KR:P_KR + 512]), g[:, P_VR:P_VR + 1024],
                            g[:, P_GR:P_GR + 1024], g[:, P_GATES:P_GATES + 2048]], axis=1)


def _qk_pad(t):
    z = jnp.zeros(t.shape[:-1] + (32,), t.dtype)
    return jnp.concatenate([t[..., 64:80], t[..., 0:48], t[..., 80:96], t[..., 48:64], z], axis=-1)


def _qk_unpad(p):
    return jnp.concatenate([p[..., 16:64], p[..., 80:96], p[..., 0:16], p[..., 64:80]], axis=-1)


def _wq_pad(w):
    return _qk_pad(w.reshape(Q_RANK, HEADS, QK_M)).reshape(Q_RANK, HEADS * LANES)


def _wq_unpad(g):
    return _qk_unpad(g.reshape(Q_RANK, HEADS, LANES)).reshape(Q_RANK, HEADS * QK_M)


def _wkv_pad(w):
    t = w.reshape(KV_RANK, HEADS, NOPE + V_M)
    z = lambda n: jnp.zeros((KV_RANK, HEADS, n), w.dtype)
    wk = jnp.concatenate([z(16), t[..., 0:48], z(16), t[..., 48:64], z(32)], axis=-1)
    wv = jnp.concatenate([t[..., 64:128], z(64)], axis=-1)
    return wk.reshape(KV_RANK, HEADS * LANES), wv.reshape(KV_RANK, HEADS * LANES)


def _wkv_unpad(dwk, dwv):
    k, v = dwk.reshape(KV_RANK, HEADS, LANES), dwv.reshape(KV_RANK, HEADS, LANES)
    return jnp.concatenate([k[..., 16:64], k[..., 80:96], v[..., 0:64]], axis=-1).reshape(KV_RANK, HEADS * (NOPE + V_M))


def _wmla_pad(w):
    t = w.reshape(HEADS, V_M, D_MODEL)
    return jnp.concatenate([t, jnp.zeros_like(t)], axis=1).reshape(HEADS * LANES, D_MODEL)


def _wmla_unpad(g):
    return g.reshape(HEADS, LANES, D_MODEL)[:, :V_M].reshape(HEADS * V_M, D_MODEL)


def _rowwise(name, fn, rows, ts, ins, outs, accs=(), ncol=1):
    n_in, n_out, n_acc = len(ins), len(outs), len(accs)

    def colmap(col):
        if callable(col):
            return lambda i, j: (i, col(j))
        return lambda i, j: (i, col)

    arrays, in_specs = [], []
    for arr, spec in ins:
        arrays.append(arr)
        if spec is None:
            in_specs.append(pl.BlockSpec(arr.shape, functools.partial(lambda i, j, nd: (0,) * nd, nd=arr.ndim)))
        else:
            in_specs.append(pl.BlockSpec((ts, spec[0]), colmap(spec[1])))
    out_shape, out_specs = [], []
    for total, dtype, width, col in outs:
        out_shape.append(jax.ShapeDtypeStruct((rows, total), dtype))
        out_specs.append(pl.BlockSpec((ts, width), colmap(col)))
    for shp in accs:
        out_shape.append(jax.ShapeDtypeStruct(shp, F32))
        out_specs.append(pl.BlockSpec(shp, functools.partial(lambda i, j, nd: (0,) * nd, nd=len(shp))))

    def body(*refs):
        vals = [r[...] for r in refs[:n_in]]
        res = fn(*vals)
        if not isinstance(res, (tuple, list)):
            res = (res,)
        for r, v in zip(refs[n_in:n_in + n_out], res[:n_out]):
            r[...] = v.astype(r.dtype)
        if n_acc:
            first = jnp.logical_and(pl.program_id(0) == 0, pl.program_id(1) == 0)
            for r, v in zip(refs[n_in + n_out:], res[n_out:]):
                @pl.when(first)
                def _(r=r):
                    r[...] = jnp.zeros_like(r)
                r[...] += v.astype(F32)

    res = pl.pallas_call(
        body, name=name, grid=(rows // ts, ncol), in_specs=in_specs, out_specs=out_specs, out_shape=out_shape,
        compiler_params=pltpu.CompilerParams(dimension_semantics=("arbitrary", "arbitrary"), vmem_limit_bytes=VMEM_LIMIT),
    )(*arrays)
    return res


MM_OPERAND_BYTES = 24 * 1024 * 1024


def _mm(name, a, b, mode, add=None, after=None):
    a_halves, b_halves = a.ndim == 3, b.ndim == 3
    assert not a_halves or mode == "nt"
    assert not b_halves or mode == "tn"
    if mode == "nn":
        (M, K), N = a.shape, b.shape[1]
    elif mode == "nt":
        M, K, N = a.shape[-2], a.shape[-1] * (2 if a_halves else 1), b.shape[0]
    else:
        (K, M), N = a.shape, b.shape[-1] * (2 if b_halves else 1)
    tm = _pick(M, (1024, 512, 1408, 256, 128))
    tn = _pick(N // 2 if b_halves else N, (1408, 1024, 512, 256, 128))
    fits = lambda t: 2 * (tm + tn) * t * a.dtype.itemsize <= MM_OPERAND_BYTES
    kdiv = K // 2 if a_halves else K
    tk = next(t for t in (K, 4096, 2816, 2048, 1408, 1024, 512, 256, 128) if kdiv % t == 0 and (fits(t) or t == 128))
    nk = K // tk
    dims = {"nn": NN, "nt": NT, "tn": TN}[mode]
    if a_halves:
        per = kdiv // tk
        a_spec = pl.BlockSpec((None, tm, tk), lambda i, j, k: (k // per, i, k % per))
    else:
        a_spec = pl.BlockSpec((tk, tm), lambda i, j, k: (k, i)) if mode == "tn" else pl.BlockSpec((tm, tk), lambda i, j, k: (i, k))
    if b_halves:
        perj = (N // 2) // tn
        b_spec = pl.BlockSpec((None, tk, tn), lambda i, j, k: (j // perj, k, j % perj))
    else:
        b_spec = pl.BlockSpec((tn, tk), lambda i, j, k: (j, k)) if mode == "nt" else pl.BlockSpec((tk, tn), lambda i, j, k: (k, j))
    o_spec = pl.BlockSpec((tm, tn), lambda i, j, k: (i, j))
    has_add = add is not None

    def body(*refs):
        a_ref, b_ref, o_ref = refs[0], refs[1], refs[-1]
        d = _dot(a_ref[...], b_ref[...], dims)
        first = (d + refs[2][...]) if has_add else d
        if nk == 1:
            o_ref[...] = first
        else:
            k = pl.program_id(2)

            @pl.when(k == 0)
            def _():
                o_ref[...] = first

            @pl.when(k > 0)
            def _():
                o_ref[...] += d

    args = [a, b] + ([add] if has_add else []) + ([] if after is None else [after])
    specs = [a_spec, b_spec] + ([o_spec] if has_add else []) + ([] if after is None else [pl.BlockSpec(memory_space=pl.ANY)])
    return pl.pallas_call(
        body, name=name, grid=(M // tm, N // tn, nk), in_specs=specs, out_specs=o_spec,
        out_shape=jax.ShapeDtypeStruct((M, N), F32),
        compiler_params=pltpu.CompilerParams(dimension_semantics=("parallel", "parallel", "arbitrary"), vmem_limit_bytes=VMEM_LIMIT),
    )(*args)


def _mm_rows(name, a, b, fn, row_ins, whole_ins, outs, accs=(), mode="nn"):
    (M, K), N = a.shape, b.shape[1 if mode == "nn" else 0]
    tm = _pick(M, (512, 256, 128))
    n_in, n_out = 2 + len(row_ins) + len(whole_ins), len(outs)
    windows = [t if isinstance(t, tuple) else (t, (t.shape[1], 0)) for t in row_ins]
    row_ins = [t for t, _ in windows]
    row_specs = [pl.BlockSpec((tm, w), functools.partial(lambda i, col: (i, col), col=col)) for _, (w, col) in windows]

    def body(*refs):
        d = _dot(refs[0][...], refs[1][...], NN if mode == "nn" else NT)
        res = fn(d, *[r[...] for r in refs[2:n_in]])
        for r, v in zip(refs[n_in:n_in + n_out], res[:n_out]):
            r[...] = v.astype(r.dtype)
        for r, v in zip(refs[n_in + n_out:], res[n_out:]):
            @pl.when(pl.program_id(0) == 0)
            def _(r=r):
                r[...] = jnp.zeros_like(r)
            r[...] += v

    row = pl.BlockSpec((tm, N), lambda i: (i, 0))
    whole = lambda t: pl.BlockSpec(t.shape, functools.partial(lambda i, nd: (0,) * nd, nd=t.ndim))
    return pl.pallas_call(
        body, name=name, grid=(M // tm,),
        in_specs=[pl.BlockSpec((tm, K), lambda i: (i, 0)), whole(b)] + row_specs + [whole(t) for t in whole_ins],
        out_specs=[row] * n_out + [pl.BlockSpec(s, functools.partial(lambda i, nd: (0,) * nd, nd=len(s))) for s in accs],
        out_shape=[jax.ShapeDtypeStruct((M, N), dt) for dt in outs] + [jax.ShapeDtypeStruct(s, F32) for s in accs],
        compiler_params=pltpu.CompilerParams(dimension_semantics=("arbitrary",), vmem_limit_bytes=VMEM_LIMIT),
    )(a, b, *row_ins, *whole_ins)


def _ffn_tiles(S):
    return _pick(S, (1024, 512, 256, 128)), _pick(FFN, (1408, 704, 256, 128))


def _gate_up_swiglu(h2, wgu):
    S, K = h2.shape
    tm, tn = _ffn_tiles(S)
    nj = FFN // tn

    def body(a_ref, bg_ref, bu_ref, gu_ref, act_ref):
        a = a_ref[...]
        g, u = _dot(a, bg_ref[...], NN), _dot(a, bu_ref[...], NN)
        gu_ref[0], gu_ref[1] = g.astype(gu_ref.dtype), u.astype(gu_ref.dtype)
        act_ref[...] = _swiglu_fn(g, u).astype(act_ref.dtype)

    return pl.pallas_call(
        body, name="gate_up_swiglu", grid=(S // tm, nj),
        in_specs=[pl.BlockSpec((tm, K), lambda i, j: (i, 0)), pl.BlockSpec((K, tn), lambda i, j: (0, j)),
                  pl.BlockSpec((K, tn), lambda i, j: (0, nj + j))],
        out_specs=[pl.BlockSpec((2, tm, tn), lambda i, j: (0, i, j)), pl.BlockSpec((tm, tn), lambda i, j: (i, j))],
        out_shape=[jax.ShapeDtypeStruct((2, S, FFN), MXU), jax.ShapeDtypeStruct((S, FFN), MXU)],
        compiler_params=pltpu.CompilerParams(dimension_semantics=("parallel", "parallel"), vmem_limit_bytes=VMEM_LIMIT),
    )(h2, wgu, wgu)


def _d_act_swiglu(dx2, wdown, gu):
    S, K = dx2.shape
    tm, tn = _ffn_tiles(S)

    def body(a_ref, b_ref, gu_ref, o_ref):
        dact = _dot(a_ref[...], b_ref[...], NT)
        _, vjp = jax.vjp(_swiglu_fn, gu_ref[0].astype(F32), gu_ref[1].astype(F32))
        dg, du = vjp(dact)
        o_ref[0], o_ref[1] = dg.astype(o_ref.dtype), du.astype(o_ref.dtype)

    stacked = pl.BlockSpec((2, tm, tn), lambda i, j: (0, i, j))
    return pl.pallas_call(
        body, name="d_act_swiglu", grid=(S // tm, FFN // tn),
        in_specs=[pl.BlockSpec((tm, K), lambda i, j: (i, 0)), pl.BlockSpec((tn, K), lambda i, j: (j, 0)), stacked],
        out_specs=stacked, out_shape=jax.ShapeDtypeStruct((2, S, FFN), MXU),
        compiler_params=pltpu.CompilerParams(dimension_semantics=("parallel", "parallel"), vmem_limit_bytes=VMEM_LIMIT),
    )(dx2, wdown, gu)


@jax.custom_vjp
def _swap64(x):
    return pltpu.roll(x, 64, 1)


_swap64.defvjp(lambda x: (_swap64(x), None), lambda _, g: (_swap64(g),))


@jax.custom_vjp
def _mxdot(a, b):
    return _dot(a.astype(MXU), b.astype(MXU), NN)


def _mxdot_bwd(res, g):
    a, b = res
    gb = g.astype(MXU)
    return _dot(gb, b.astype(MXU), NT), _dot(a.astype(MXU), gb, TN)


_mxdot.defvjp(lambda a, b: (_mxdot(a, b), (a, b)), _mxdot_bwd)


def _row_sum(t):
    if t.shape[-1] == LANES:
        return lax.dot_general(t, jnp.ones((LANES, LANES), F32), ((NN), ((), ())), precision=lax.Precision.HIGH,
                               preferred_element_type=F32)
    return jnp.sum(t, axis=-1, keepdims=True)


@functools.partial(jax.custom_vjp, nondiff_argnums=(1,))
def _unit_rms(x, n):
    return x * lax.rsqrt(_row_sum(x * x) * (1.0 / n) + EPS)


def _unit_rms_fwd(x, n):
    r = lax.rsqrt(_row_sum(x * x) * (1.0 / n) + EPS)
    y = x * r
    return y, (y, r)


def _unit_rms_bwd(n, res, g):
    y, r = res
    return (r * (g - y * (_row_sum(g * y) * (1.0 / n))),)


_unit_rms.defvjp(_unit_rms_fwd, _unit_rms_bwd)


def _rms(x):
    return _unit_rms(x, x.shape[-1])


def _rmsg_fn(x, g):
    return _rms(x) * g


def _silu(x):
    return x * jax.nn.sigmoid(x)


def _tables_fn(pos, inv_m, sgn_m, inv_r, sgn_r):
    am, ar = pos * inv_m, pos * inv_r
    return jnp.cos(am), jnp.sin(am) * sgn_m, jnp.cos(ar), jnp.sin(ar) * sgn_r


def _head_blocks(t):
    return [t[:, LANES * h:LANES * (h + 1)] for h in range(t.shape[1] // LANES)]


def _mla_prep_fn(cq, ckv, kr, cosm, sinm, gqa, gkva, gqn, gkn, wq, wk, wv):
    cqn = _rms(cq) * gqa
    ckvn = _rms(ckv) * gkva
    q_raw = _mxdot(cqn, wq)
    k_raw = _mxdot(ckvn, wk)
    lane = lax.broadcasted_iota(jnp.int32, (1, HEADS * LANES), 1)
    v = _mxdot(ckvn, wv) + (lane % LANES == V_M).astype(F32)

    def norm_rope(blocks, g, extra):
        outs = []
        for b in blocks:
            if extra is not None:
                b = b + extra
            n = _unit_rms(b, QK_M) * g
            outs.append(n * cosm + _swap64(n) * sinm)
        return jnp.concatenate(outs, axis=1)

    q = norm_rope(_head_blocks(q_raw), gqn, None)
    k = norm_rope(_head_blocks(k_raw), gkn, kr)
    return q, k, v


def _ret_prep_fn(qr, kr, cosr, sinr):
    def rope(t, scale):
        return jnp.concatenate([(b * cosr + _swap64(b) * sinr) * scale for b in _head_blocks(t)], axis=1)
    return rope(qr, 1.0), rope(kr, RQK ** -0.5)


def _ret_post_fn(rf, rb, gr):
    ret = rf + rb
    outs = []
    for b, g in zip(_head_blocks(ret), _head_blocks(gr)):
        outs.append(_silu(g) * _rms(b))
    return jnp.concatenate(outs, axis=1)


def _merge_fn(ga, gb, ya, yb):
    return jax.nn.sigmoid(ga) * ya + jax.nn.sigmoid(gb) * yb


def _swiglu_fn(gate, up):
    return _silu(gate) * up


def _loss_fn(x2, tgt):
    d = x2 - tgt
    return d * (1.0 / D_MODEL), 0.5 * jnp.sum(d * d, axis=0, keepdims=True) * (1.0 / D_MODEL)


def _adamw_fn(parts, w, m, v):
    g = parts[0].astype(F32)
    for p in range(1, N_DEV):
        g = g + parts[p].astype(F32)
    m2 = B1 * m + (1.0 - B1) * g
    v2 = B2 * v + (1.0 - B2) * jnp.square(g)
    m_hat = m2 / (1.0 - B1 ** STEP)
    v_hat = v2 / (1.0 - B2 ** STEP)
    delta = -LR * (m_hat / (jnp.sqrt(v_hat) + AEPS) + WD * w)
    return g, delta, m2, v2


SCALE = QK_M ** -0.5
LOG2E = 1.4426950408889634
FLASH_ROWS = 32


def _flash_fwd(q, k, v):
    S = q.shape[0]
    tk = _pick(S, (512, 256, 128))
    tq = _pick(S, (1024, 512, 256, 128))
    ncb = tk // LANES
    nkv = S // tk
    assert nkv % 2 == 0, "kv tiles are processed in pairs"
    mrows = 64
    c = SCALE * LOG2E

    def body(q_ref, k_ref, v_ref, o_ref, lse_ref, s_a, p_a, s_b, p_b, m_sc, a_sc, acc_sc):
        m_sc[...] = jnp.full_like(m_sc, -jnp.inf)
        acc_sc[...] = jnp.zeros_like(acc_sc)
        qb = q_ref[...]

        def scores(j, s_buf):
            s_buf[...] = _dot(qb, k_ref[pl.ds(pl.multiple_of(j * tk, tk), tk), :], NT)

        def stage(j, s_buf, p_buf, s_next):
            scores(jnp.minimum(j + 1, nkv - 1), s_next)
            for r in range(tq // mrows):
                rows = slice(r * mrows, (r + 1) * mrows)
                cols = [s_buf[rows, LANES * cb:LANES * (cb + 1)] for cb in range(ncb)]
                m_prev = m_sc[rows, :]
                row_max = jnp.max(functools.reduce(jnp.maximum, cols), axis=-1, keepdims=True)
                m_new = jnp.maximum(m_prev, jnp.broadcast_to(row_max, (mrows, LANES)))
                a_sc[rows, :] = jnp.exp2((m_prev - m_new) * c)
                m_sc[rows, :] = m_new
                for cb in range(ncb):
                    p_buf[rows, LANES * cb:LANES * (cb + 1)] = jnp.exp2((cols[cb] - m_new) * c).astype(p_buf.dtype)
            acc_sc[...] = a_sc[...] * acc_sc[...] + _dot(p_buf[...], v_ref[pl.ds(pl.multiple_of(j * tk, tk), tk), :], NN)

        scores(0, s_a)

        def pair_step(t, carry):
            stage(2 * t, s_a, p_a, s_b)
            stage(2 * t + 1, s_b, p_b, s_a)
            return carry

        lax.fori_loop(0, nkv // 2, pair_step, 0, unroll=4)
        acc = acc_sc[...]
        lane = lax.broadcasted_iota(jnp.int32, (1, LANES), 1)
        l = jnp.sum(jnp.where(lane == V_M, acc, 0.0), axis=-1, keepdims=True)
        o_ref[...] = (acc / l).astype(o_ref.dtype)
        lse_ref[...] = m_sc[...] * c + jnp.log2(jnp.broadcast_to(l, (tq, LANES)))

    qspec = pl.BlockSpec((tq, LANES), lambda h, i: (i, h))
    kspec = pl.BlockSpec((S, LANES), lambda h, i: (0, h))
    return pl.pallas_call(
        body, name="flash_fwd", grid=(HEADS, S // tq), in_specs=[qspec, kspec, kspec], out_specs=[qspec, qspec],
        out_shape=[jax.ShapeDtypeStruct((S, HEADS * LANES), MXU), jax.ShapeDtypeStruct((S, HEADS * LANES), F32)],
        scratch_shapes=[pltpu.VMEM((tq, tk), F32), pltpu.VMEM((tq, tk), MXU)] * 2 + [pltpu.VMEM((tq, LANES), F32)] * 3,
        compiler_params=pltpu.CompilerParams(dimension_semantics=("parallel", "arbitrary"), vmem_limit_bytes=VMEM_LIMIT),
    )(q, k, v)


def _delta_fn(o, do):
    outs = [jnp.broadcast_to(jnp.sum(a * b, axis=-1, keepdims=True), a.shape) for a, b in zip(_head_blocks(o), _head_blocks(do))]
    return do, jnp.concatenate(outs, axis=1)


def _flash_bwd(q, k, v, do, lse, delta):
    S = q.shape[0]
    tq = tk = _pick(S, (512, 256, 128))
    ncb = tk // LANES
    c = SCALE * LOG2E

    nq = S // tq
    assert nq % 2 == 0, "q tiles are processed in pairs"

    def body(q_ref, k_ref, v_ref, do_ref, lse_ref, dl_ref, dq_ref, dk_ref, dv_ref, s_a, dp_a, p_a, ds_a, s_b, dp_b, p_b, ds_b):
        @pl.when(pl.program_id(1) == 0)
        def _():
            dq_ref[...] = jnp.zeros_like(dq_ref)

        dk_ref[...] = jnp.zeros_like(dk_ref)
        dv_ref[...] = jnp.zeros_like(dv_ref)
        kb, vb = k_ref[...], v_ref[...]

        def scores(i, s_buf, dp_buf):
            q_rows = pl.ds(pl.multiple_of(i * tq, tq), tq)
            s_buf[...] = _dot(q_ref[q_rows, :], kb, NT)
            dp_buf[...] = _dot(do_ref[q_rows, :], vb, NT)

        def stage(i, s_buf, dp_buf, p_buf, ds_buf, s_next, dp_next):
            scores(jnp.minimum(i + 1, nq - 1), s_next, dp_next)
            for r in range(tq // FLASH_ROWS):
                rows = slice(r * FLASH_ROWS, (r + 1) * FLASH_ROWS)
                grows = pl.ds(pl.multiple_of(i * tq + r * FLASH_ROWS, FLASH_ROWS), FLASH_ROWS)
                lse_b, dl_b = lse_ref[grows, :], dl_ref[grows, :]
                for cb in range(ncb):
                    sl = slice(LANES * cb, LANES * (cb + 1))
                    p = jnp.exp2(s_buf[rows, sl] * c - lse_b)
                    p_buf[rows, sl] = p.astype(p_buf.dtype)
                    ds_buf[rows, sl] = (p * (dp_buf[rows, sl] - dl_b) * SCALE).astype(ds_buf.dtype)
            q_rows = pl.ds(pl.multiple_of(i * tq, tq), tq)
            dv_ref[...] += _dot(p_buf[...], do_ref[q_rows, :], TN)
            dk_ref[...] += _dot(ds_buf[...], q_ref[q_rows, :], TN)
            dq_ref[q_rows, :] += _dot(ds_buf[...], kb, NN)

        scores(0, s_a, dp_a)

        def pair_step(t, carry):
            stage(2 * t, s_a, dp_a, p_a, ds_a, s_b, dp_b)
            stage(2 * t + 1, s_b, dp_b, p_b, ds_b, s_a, dp_a)
            return carry

        lax.fori_loop(0, nq // 2, pair_step, 0, unroll=2)

    hspec = pl.BlockSpec((S, LANES), lambda h, j: (0, h))
    kspec = pl.BlockSpec((tk, LANES), lambda h, j: (j, h))
    full = jax.ShapeDtypeStruct((S, HEADS * LANES), F32)
    tile_bufs = [pltpu.VMEM((tq, tk), F32), pltpu.VMEM((tq, tk), F32), pltpu.VMEM((tq, tk), MXU), pltpu.VMEM((tq, tk), MXU)]
    return pl.pallas_call(
        body, name="flash_bwd", grid=(HEADS, S // tk), in_specs=[hspec, kspec, kspec, hspec, hspec, hspec],
        out_specs=[hspec, kspec, kspec], out_shape=[full, full, full],
        scratch_shapes=tile_bufs + tile_bufs,
        compiler_params=pltpu.CompilerParams(dimension_semantics=("parallel", "arbitrary"), vmem_limit_bytes=VMEM_LIMIT),
    )(q, k, v, do, lse, delta)


def _ret_consts(lgh, head, rev):
    C = CHUNK
    lane = lax.broadcasted_iota(jnp.int32, (1, LANES), 1)
    hm = ((lane // 32) % 2 == head % 2).astype(F32)
    r = lax.broadcasted_iota(jnp.int32, (C, C), 0)
    c = lax.broadcasted_iota(jnp.int32, (C, C), 1)
    diff = ((c - r) if rev else (r - c)).astype(F32)
    mask = (diff > 0) if rev else (diff >= 0)
    dpos = jnp.maximum(diff, 0.0)
    din = jnp.where(mask, jnp.exp(lgh * dpos), 0.0)
    idx = lax.broadcasted_iota(jnp.int32, (C, 1), 0).astype(F32)
    eq = (C - idx) if rev else (idx + 1.0)
    ek = idx if rev else (C - 1.0 - idx)
    qd, kd = jnp.exp(lgh * eq), jnp.exp(lgh * ek)
    cd = jnp.exp(lgh * jnp.full((1, 1), float(C), F32))
    return hm, din, dpos, qd, kd, cd, eq, ek


RET_HEADS_PER_STEP = 8


def _ret_fwd(name, qt, kt, proj, lg, rev):
    S = qt.shape[0]
    C = CHUNK
    TB = _pick(S, (512, 256, 128))
    cb, nb = TB // C, S // TB
    hps = RET_HEADS_PER_STEP
    blk = (lambda g: nb - 1 - g) if rev else (lambda g: g)

    def body(lg_ref, q_ref, k_ref, v_ref, o_ref, st_ref, state_sc):
        hg, g = pl.program_id(0), pl.program_id(1)

        @pl.when(g == 0)
        def _():
            state_sc[...] = jnp.zeros_like(state_sc)

        consts = [_ret_consts(lg_ref[hg * hps + u], u, rev) for u in range(hps)]
        order = list(reversed(range(cb))) if rev else list(range(cb))
        units = [(cc, u) for cc in order for u in range(hps)]

        def operands(cc, u):
            rows = pl.ds(cc * C, C)
            pair = slice(LANES * (u // 2), LANES * (u // 2 + 1))
            hm = consts[u][0]
            return q_ref[rows, pair] * hm, k_ref[rows, pair] * hm, v_ref[rows, LANES * u:LANES * (u + 1)].astype(MXU)

        a, inc = {}, {}
        for cc, u in units:
            q, k, v = operands(cc, u)
            a[cc, u] = _dot(q.astype(MXU), k.astype(MXU), NT) * consts[u][1]
            inc[cc, u] = _dot((k * consts[u][4]).astype(MXU), v, TN)
        for u in range(hps):
            st = state_sc[u]
            for cc in order:
                st_ref[u, cc] = st
                st = st * consts[u][5] + inc[cc, u]
            state_sc[u] = st
        for cc, u in units:
            q, _, v = operands(cc, u)
            cross = _dot((q * consts[u][3]).astype(MXU), st_ref[u, cc].astype(MXU), NN)
            o_ref[pl.ds(cc * C, C), LANES * u:LANES * (u + 1)] = _dot(a[cc, u].astype(MXU), v, NN) + cross

    qk_spec = pl.BlockSpec((TB, LANES * hps // 2), lambda h, g: (blk(g), h))
    return pl.pallas_call(
        body, name=name, grid=(HEADS // hps, nb),
        in_specs=[pl.BlockSpec(memory_space=pltpu.SMEM), qk_spec, qk_spec,
                  pl.BlockSpec((TB, LANES * hps), lambda h, g: (blk(g), P_VR // (LANES * hps) + h))],
        out_specs=[pl.BlockSpec((TB, LANES * hps), lambda h, g: (blk(g), h)),
                   pl.BlockSpec((hps, cb, LANES, LANES), lambda h, g: (h, blk(g), 0, 0))],
        out_shape=[jax.ShapeDtypeStruct((S, HEADS * LANES), F32), jax.ShapeDtypeStruct((HEADS, S // C, LANES, LANES), F32)],
        scratch_shapes=[pltpu.VMEM((hps, LANES, LANES), F32)],
        compiler_params=pltpu.CompilerParams(dimension_semantics=("parallel", "arbitrary"), vmem_limit_bytes=VMEM_LIMIT),
    )(lg, qt, kt, proj)


def _ret_bwd(name, qt, kt, proj, dret, states, lg, rev):
    S = qt.shape[0]
    C = CHUNK
    TB = _pick(S, (512, 256, 128))
    cb, nb = TB // C, S // TB
    hps = RET_HEADS_PER_STEP
    blk = (lambda g: g) if rev else (lambda g: nb - 1 - g)

    def body(lg_ref, q_ref, k_ref, v_ref, do_ref, st_ref, dq_ref, dk_ref, dv_ref, dlg_ref, ds_sc, acc_cc, acc_q, acc_k, acc_s):
        hg, g = pl.program_id(0), pl.program_id(1)

        @pl.when(g == 0)
        def _():
            ds_sc[...] = jnp.zeros_like(ds_sc)
            acc_cc[...] = jnp.zeros_like(acc_cc)
            acc_q[...] = jnp.zeros_like(acc_q)
            acc_k[...] = jnp.zeros_like(acc_k)
            acc_s[...] = jnp.zeros_like(acc_s)

        lgs = [lg_ref[hg * hps + u] for u in range(hps)]
        consts = [_ret_consts(lgs[u], u, rev) for u in range(hps)]
        order = list(range(cb)) if rev else list(reversed(range(cb)))
        units = [(cc, u) for cc in order for u in range(hps)]

        def operands(cc, u):
            rows = pl.ds(cc * C, C)
            pair = slice(LANES * (u // 2), LANES * (u // 2 + 1))
            head = slice(LANES * u, LANES * (u + 1))
            hm = consts[u][0]
            return q_ref[rows, pair] * hm, k_ref[rows, pair] * hm, v_ref[rows, head].astype(MXU), do_ref[rows, head].astype(MXU)

        a, dp, dqs, inc = {}, {}, {}, {}
        for cc, u in units:
            q, k, vb, dob = operands(cc, u)
            a[cc, u] = _dot(q.astype(MXU), k.astype(MXU), NT)
            dp[cc, u] = _dot(dob, vb, NT)
            dqs[cc, u] = _dot(dob, st_ref[u, cc].astype(MXU), NT)
            inc[cc, u] = _dot((q * consts[u][3]).astype(MXU), dob, TN)
        dsn = {}
        for u in range(hps):
            ds = ds_sc[u]
            for cc in order:
                dsn[cc, u] = ds
                ds = ds * consts[u][5] + inc[cc, u]
            ds_sc[u] = ds
        even = {}
        for cc, u in units:
            hm, din, dpos, qd, kd, cd, eq, ek = consts[u]
            rows, head = pl.ds(cc * C, C), slice(LANES * u, LANES * (u + 1))
            q, k, vb, dob = operands(cc, u)
            qb, kb = q.astype(MXU), k.astype(MXU)
            dsnb = dsn[cc, u].astype(MXU)
            da = (dp[cc, u] * din).astype(MXU)
            vds = _dot(vb, dsnb, NT)
            dq_u = (_dot(da, kb, NN) + dqs[cc, u] * qd) * hm
            dk_u = (_dot(da, qb, TN) + vds * kd) * hm
            if u % 2 == 0:
                even[cc] = (dq_u, dk_u)
            else:
                pair = slice(LANES * (u // 2), LANES * (u // 2 + 1))
                dq_ref[rows, pair] = even[cc][0] + dq_u
                dk_ref[rows, pair] = even[cc][1] + dk_u
            dv_ref[rows, head] = _dot((a[cc, u] * din).astype(MXU), dob, TN) + _dot((k * kd).astype(MXU), dsnb, NN)
            acc_cc[u] += dp[cc, u] * a[cc, u] * din * dpos
            acc_q[u] += dqs[cc, u] * q * (qd * eq)
            acc_k[u] += vds * k * (kd * ek)
            acc_s[u] += dsn[cc, u] * st_ref[u, cc] * (cd * float(C))

        @pl.when(g == nb - 1)
        def _():
            for u in range(hps):
                tot = (jnp.sum(acc_cc[u], keepdims=True) + jnp.sum(acc_q[u], keepdims=True)
                       + jnp.sum(acc_k[u], keepdims=True) + jnp.sum(acc_s[u], keepdims=True))
                dlg_ref[u] = jnp.broadcast_to(tot * lgs[u], (8, LANES))

    full = jax.ShapeDtypeStruct((S, HEADS * LANES), F32)
    hspec = pl.BlockSpec((TB, LANES * hps), lambda h, g: (blk(g), h))
    qk_spec = pl.BlockSpec((TB, LANES * hps // 2), lambda h, g: (blk(g), h))
    return pl.pallas_call(
        body, name=name, grid=(HEADS // hps, nb),
        in_specs=[pl.BlockSpec(memory_space=pltpu.SMEM), qk_spec, qk_spec,
                  pl.BlockSpec((TB, LANES * hps), lambda h, g: (blk(g), P_VR // (LANES * hps) + h)),
                  hspec,
                  pl.BlockSpec((hps, cb, LANES, LANES), lambda h, g: (h, blk(g), 0, 0))],
        out_specs=[qk_spec, qk_spec, hspec, pl.BlockSpec((hps, 8, LANES), lambda h, g: (h, 0, 0))],
        out_shape=[jax.ShapeDtypeStruct(qt.shape, F32), jax.ShapeDtypeStruct(kt.shape, F32), full,
                   jax.ShapeDtypeStruct((HEADS, 8, LANES), F32)],
        scratch_shapes=[pltpu.VMEM((hps, LANES, LANES), F32), pltpu.VMEM((hps, C, C), F32), pltpu.VMEM((hps, C, LANES), F32),
                        pltpu.VMEM((hps, C, LANES), F32), pltpu.VMEM((hps, LANES, LANES), F32)],
        compiler_params=pltpu.CompilerParams(dimension_semantics=("parallel", "arbitrary"), vmem_limit_bytes=VMEM_LIMIT),
    )(lg, qt, kt, proj, dret, states)


def _rope_consts():
    inv16 = THETA ** (-jnp.arange(16, dtype=F32) / 16)
    inv32 = THETA ** (-jnp.arange(32, dtype=F32) / 32)
    lane = np.arange(LANES)
    z48 = jnp.zeros((48,), F32)
    inv_m = jnp.concatenate([inv16, z48, inv16, z48])[None, :]
    sgn_m = jnp.asarray(np.where(lane < 16, -1.0, np.where((lane >= 64) & (lane < 80), 1.0, 0.0)), F32)[None, :]
    inv_r = jnp.concatenate([inv32] * 4)[None, :]
    sgn_r = jnp.asarray(np.where(lane < 64, -1.0, 1.0), F32)[None, :]
    return inv_m, sgn_m, inv_r, sgn_r


FIRST_WEIGHTS = ("w_in", "w_q_b", "w_kv_b")
EARLY_GRADS = ("w_down", "w_gate_up", "w_out", "w_ret_out")
MID_GRADS = ("w_mla_out", "w_in")


def _local_step(x, pos, tgt, gains, W, late_weights=None, grad_hook=None, start_after=None):
    S = x.shape[0]
    ts = _pick(S, (256, 128))
    ts_light = _pick(S, (512, 256, 128))
    R = lambda a, w=None, c=0: (a, ((a.shape[1] if w is None else w), c))
    W_ = lambda a: (a, None)

    win = _win_pad(W["w_in"])
    wq = _wq_pad(W["w_q_b"])
    wk, wv = _wkv_pad(W["w_kv_b"])
    gqn, gkn = _qk_pad(gains["g_qn"]), _qk_pad(gains["g_kn"])
    g_mix, g_q_a, g_kv_a, g_ffn = gains["g_mix"], gains["g_q_a"], gains["g_kv_a"], gains["g_ffn"]
    lg_f = -jnp.exp(gains["ret_decay_fwd"][0])
    lg_b = -jnp.exp(gains["ret_decay_bwd"][0])

    consts = list(_rope_consts())
    cosm, sinm, cosr, sinr = _rowwise("rope_tables", _tables_fn, S, ts_light,[R(pos)] + [W_(c) for c in consts],
                                      [(LANES, F32, LANES, 0)] * 4)

    (h,) = _rowwise("rms_mix", _rmsg_fn, S, ts_light,[R(x), W_(g_mix)], [(D_MODEL, MXU, D_MODEL, 0)])
    proj = _mm("in_proj", h, win, "nn", after=start_after)
    seg = lambda off, w: (proj, (w, off // w))
    mla_ins = [seg(P_CQ, 256), seg(P_CKV, 128), seg(P_KROPE, 128), R(cosm), R(sinm),
               W_(g_q_a), W_(g_kv_a), W_(gqn), W_(gkn), W_(wq), W_(wk), W_(wv)]
    q, k, v = _rowwise("mla_prep", _mla_prep_fn, S, ts, mla_ins, [(HEADS * LANES, MXU, HEADS * LANES, 0)] * 3)
    o_bf, lse = _flash_fwd(q, k, v)
    if late_weights is not None:
        W = {**W, **late_weights(lse)}
    wmla = _wmla_pad(W["w_mla_out"])
    wret, wout, wgu, wdown = W["w_ret_out"], W["w_out"], W["w_gate_up"], W["w_down"]
    y_a = _mm("mla_out", o_bf, wmla, "nn")

    ret_ins = [seg(P_QR, 512), seg(P_KR, 512), R(cosr), R(sinr)]
    qt, kt = _rowwise("ret_prep", _ret_prep_fn, S, ts_light,ret_ins, [(512, F32, 512, 0)] * 2)
    ret_f, st_f = _ret_fwd("ret_fwd_f", qt, kt, proj, lg_f, False)
    ret_b, st_b = _ret_fwd("ret_fwd_b", qt, kt, proj, lg_b, True)
    post_ins = [R(ret_f), R(ret_b), seg(P_GR, 1024)]
    (o_b,) = _rowwise("ret_post", _ret_post_fn, S, ts_light,post_ins, [(1024, MXU, 1024, 0)])
    y_b, merged = _mm_rows("ret_out_merge", o_b, wret, lambda yb, ga, gb, ya: (yb, _merge_fn(ga, gb, ya, yb)),
                           [seg(P_GATES, 1024), (proj, (1024, 1)), R(y_a)], [], [F32, MXU])
    merge_ins = [seg(P_GATES, 1024), (proj, (1024, 1)), R(y_a), R(y_b)]
    def residual_rms(d, xx, g):
        r = d + xx
        return r, _rmsg_fn(r, g)

    x1, h2 = _mm_rows("out_proj_rms_ffn", merged, wout, residual_rms, [x], [g_ffn], [F32, MXU])
    gu, act = _gate_up_swiglu(h2, wgu)

    def residual_loss(d, xx, t):
        dx, rows = _loss_fn(d + xx, t)
        return dx, dx, rows

    dx2, dx2_bf, loss_rows = _mm_rows("down_proj_loss", act, wdown, residual_loss, [x1, tgt], [], [F32, MXU], accs=[(1, D_MODEL)])

    gW = {}
    gW["w_down"] = _mm("d_w_down", act, dx2_bf, "tn")
    dgu = _d_act_swiglu(dx2_bf, wdown, gu)
    gW["w_gate_up"] = _mm("d_w_gate_up", h2, dgu, "tn")
    dh2 = _mm("d_h2", dgu, wgu, "nt")

    def rms_bwd(xx, g, dh, dres):
        _, vjp = jax.vjp(_rmsg_fn, xx, g)
        dx, dg = vjp(dh)
        dx = dx + dres
        return dx, dx, dg

    dx1, dx1_bf, dg_ffn = _rowwise("rms_ffn_bwd", rms_bwd, S, ts_light,[R(x1), W_(g_ffn), R(dh2), R(dx2)],
                                   [(D_MODEL, F32, D_MODEL, 0), (D_MODEL, MXU, D_MODEL, 0)], accs=[(1, D_MODEL)])
    gW["w_out"] = _mm("d_w_out", merged, dx1_bf, "tn")
    def merge_bwd(dm, ga, gb, ya, yb):
        _, vjp = jax.vjp(_merge_fn, ga, gb, ya, yb)
        return vjp(dm)

    dga, dgb, dy_a, dy_b = _mm_rows("d_merged_merge_bwd", dx1_bf, wout, merge_bwd, merge_ins, [], [MXU] * 4, mode="nt")
    gW["w_ret_out"] = _mm("d_w_ret_out", o_b, dy_b, "tn")
    after_early = [] if grad_hook is None else [grad_hook({n: gW[n] for n in EARLY_GRADS})]

    def post_bwd(dob, rf, rb, gr, *_):
        _, vjp = jax.vjp(_ret_post_fn, rf, rb, gr)
        drf, _, dgr = vjp(dob)
        return drf, dgr

    dret, dg_r = _mm_rows("d_o_b_ret_post_bwd", dy_b, wret, post_bwd, post_ins, after_early, [MXU, MXU], mode="nt")
    dq_f, dk_f, dv_f, dlg_f = _ret_bwd("ret_bwd_f", qt, kt, proj, dret, st_f, lg_f, False)
    dq_b, dk_b, dv_b, dlg_b = _ret_bwd("ret_bwd_b", qt, kt, proj, dret, st_b, lg_b, True)

    def ret_prep_bwd(qr, kr, cosr_, sinr_, dqf, dqb, dkf, dkb, dvf, dvb):
        _, vjp = jax.vjp(lambda a, b: _ret_prep_fn(a, b, cosr_, sinr_), qr, kr)
        dqr, dkr = vjp((dqf + dqb, dkf + dkb))
        return dqr, dkr, dvf + dvb

    dq_r, dk_r, dv_r = _rowwise("ret_prep_bwd", ret_prep_bwd, S, ts_light,ret_ins + [R(t) for t in (dq_f, dq_b, dk_f, dk_b, dv_f, dv_b)],
                                [(512, MXU, 512, 0), (512, MXU, 512, 0), (1024, MXU, 1024, 0)])

    gW_mla_p = _mm("d_w_mla_out", o_bf, dy_a, "tn")
    do_bf, delta = _mm_rows("d_o_attn_delta", dy_a, wmla, lambda d, oo, *_: _delta_fn(oo.astype(F32), d), [o_bf], after_early, [MXU, F32], mode="nt")
    dq, dk, dv = _flash_bwd(q, k, v, do_bf, lse, delta)

    def mla_prep_bwd(cq, ckv, kr, cosm_, sinm_, gqa, gkva, gqn_, gkn_, wq_, wk_, wv_, dq_, dk_, dv_):
        f = lambda cq, ckv, kr, gqa, gkva, gqn_, gkn_, wq_, wk_, wv_: _mla_prep_fn(cq, ckv, kr, cosm_, sinm_, gqa, gkva, gqn_, gkn_, wq_, wk_, wv_)
        _, vjp = jax.vjp(f, cq, ckv, kr, gqa, gkva, gqn_, gkn_, wq_.astype(F32), wk_.astype(F32), wv_.astype(F32))
        return vjp((dq_, dk_, dv_))

    mb = _rowwise("mla_prep_bwd", mla_prep_bwd, S, ts, mla_ins + [R(dq), R(dk), R(dv)],
                  [(256, MXU, 256, 0), (128, MXU, 128, 0), (128, MXU, 128, 0)],
                  accs=[(1, 256), (1, 128), (1, LANES), (1, LANES), (256, HEADS * LANES), (128, HEADS * LANES), (128, HEADS * LANES)])
    dc_q, dc_kv, dk_rope, dg_q_a, dg_kv_a, dgqn_p, dgkn_p, dwq_p, dwk_p, dwv_p = mb

    dproj = jnp.concatenate([dga, dgb, dv_r, dg_r, dq_r, dk_r, dc_q, dc_kv, dk_rope], axis=1)
    gW["w_in"] = _win_unpad(_mm("d_w_in", h, dproj, "tn"))
    gW["w_mla_out"] = _wmla_unpad(gW_mla_p)
    after_mid = None if grad_hook is None else grad_hook({n: gW[n] for n in MID_GRADS})
    dh = _mm("d_h", dproj, win, "nt", after=after_mid)
    grad_x, dg_mix = _rowwise("rms_mix_bwd", lambda a, b, c, d, *_: rms_bwd(a, b, c, d)[1:], S, ts_light,
                              [R(x), W_(g_mix), R(dh), R(dx1)] + ([] if after_mid is None else [W_(after_mid)]),
                              [(D_MODEL, F32, D_MODEL, 0)], accs=[(1, D_MODEL)])
    gW["w_q_b"] = _wq_unpad(dwq_p)
    gW["w_kv_b"] = _wkv_unpad(dwk_p, dwv_p)
    gG = {"g_mix": dg_mix, "g_q_a": dg_q_a, "g_kv_a": dg_kv_a, "g_qn": _qk_unpad(dgqn_p),
          "g_kn": _qk_unpad(dgkn_p), "ret_decay_fwd": dlg_f[:, 0, 0][None, :], "ret_decay_bwd": dlg_b[:, 0, 0][None, :],
          "g_ffn": dg_ffn}
    return loss_rows, grad_x, gG, gW


MATS = [("w_in", (1024, 5536), 1), ("w_q_b", (256, 768), 1), ("w_kv_b", (128, 1024), 1), ("w_mla_out", (512, 1024), 1),
        ("w_ret_out", (1024, 1024), 0), ("w_out", (1024, 1024), 0), ("w_gate_up", (1024, 5632), 1), ("w_down", (2816, 1024), 0)]
GAINS = [("g_mix", 1024), ("g_q_a", 256), ("g_kv_a", 128), ("g_qn", 96), ("g_kn", 96), ("ret_decay_fwd", 8), ("ret_decay_bwd", 8),
         ("g_ffn", 1024)]
ORDER = ["g_mix", "w_in", "g_q_a", "w_q_b", "g_kv_a", "w_kv_b", "g_qn", "g_kn", "w_mla_out", "ret_decay_fwd", "ret_decay_bwd",
         "w_ret_out", "w_out", "g_ffn", "w_gate_up", "w_down"]
GAIN_LEN = sum(n for _, n in GAINS)
GAIN_PAD = -(-GAIN_LEN // LANES) * LANES


def _pack_gains(d):
    row = jnp.concatenate([d[n].reshape(1, ln).astype(F32) for n, ln in GAINS], axis=1)
    return jnp.pad(row, ((0, 0), (0, GAIN_PAD - GAIN_LEN)))


def _unpack_gains(row):
    out, off = {}, 0
    for n, ln in GAINS:
        out[n] = row[0, off:off + ln]
        off += ln
    return out


def _unshard(pieces, axis):
    if axis == 0:
        return pieces.reshape((N_DEV * pieces.shape[1], pieces.shape[2]))
    return jnp.concatenate([pieces[p] for p in range(N_DEV)], axis=1)


def _reshard(full, axis):
    if axis == 0:
        return full.reshape((N_DEV, full.shape[0] // N_DEV, full.shape[1]))
    c = full.shape[1] // N_DEV
    return jnp.stack([full[:, c * p:c * (p + 1)] for p in range(N_DEV)])


def _all_gather(shards):
    n = len(shards)

    def body(*refs):
        x_refs, out_refs = refs[:n], refs[n:2 * n]
        send_sems, recv_sems, local_sems = refs[2 * n:]
        x, y, c = lax.axis_index("x"), lax.axis_index("y"), lax.axis_index("c")
        me, sibling = (x, y, c), (x, y, 1 - c)
        chips = [(1 - x, y), (x, 1 - y), (1 - x, 1 - y)]

        def slot(a, px, py, pc):
            return out_refs[a].at[4 * px + 2 * py + pc]

        def copy(a, k, block, to, from_input=False):
            return pltpu.make_async_remote_copy(
                src_ref=x_refs[a] if from_input else slot(a, *block), dst_ref=slot(a, *block),
                send_sem=send_sems.at[a, k], recv_sem=recv_sems.at[a, k], device_id=to, device_id_type=pl.DeviceIdType.MESH)

        mine = [pltpu.make_async_copy(x_refs[a], slot(a, *me), local_sems.at[a]) for a in range(n)]
        first = [copy(a, 0, me, sibling, True) for a in range(n)]
        first += [copy(a, 1 + j, me, (*chip, c), True) for j, chip in enumerate(chips) for a in range(n)]
        for cp in mine + first:
            cp.start()
        passed = []
        for j, chip in enumerate(chips):
            for a in range(n):
                copy(a, 1 + j, (*chip, c), me).wait_recv()
                passed.append(copy(a, 4 + j, (*chip, c), sibling))
                passed[-1].start()
        for a in range(n):
            copy(a, 0, sibling, me).wait_recv()
        for j, chip in enumerate(chips):
            for a in range(n):
                copy(a, 4 + j, (*chip, 1 - c), me).wait_recv()
        for cp in first + passed:
            cp.wait_send()
        for cp in mine:
            cp.wait()

    any_spec = pl.BlockSpec(memory_space=pl.ANY)
    return pl.pallas_call(
        body, name="all_gather_weights", out_shape=[jax.ShapeDtypeStruct((N_DEV,) + s.shape, s.dtype) for s in shards],
        in_specs=[any_spec] * n, out_specs=[any_spec] * n,
        scratch_shapes=[pltpu.SemaphoreType.DMA((n, 7)), pltpu.SemaphoreType.DMA((n, 7)), pltpu.SemaphoreType.DMA((n,))],
    )(*shards)


def _exchange(name, srcs, gather):
    n = len(srcs)

    def body(*refs):
        in_refs, out_refs = refs[:n], refs[n:2 * n]
        send_sems, recv_sems, local_sems = refs[2 * n:]
        my_id = 4 * lax.axis_index("x") + 2 * lax.axis_index("y") + lax.axis_index("c")
        mine = [pltpu.make_async_copy(in_refs[a] if gather else in_refs[a].at[my_id], out_refs[a].at[my_id], local_sems.at[a])
                for a in range(n)]
        copies = _split_copies(in_refs, out_refs, send_sems, recv_sems, gather)
        for cp in mine + copies:
            cp.start()
        for cp in copies:
            cp.wait_recv()
        for cp in copies:
            cp.wait_send()
        for cp in mine:
            cp.wait()

    any_spec = pl.BlockSpec(memory_space=pl.ANY)
    return pl.pallas_call(
        body, name=name,
        out_shape=[jax.ShapeDtypeStruct((N_DEV,) + s.shape if gather else s.shape, s.dtype) for s in srcs],
        in_specs=[any_spec] * n, out_specs=[any_spec] * n,
        scratch_shapes=[pltpu.SemaphoreType.DMA((7 * n,)), pltpu.SemaphoreType.DMA((7 * n,)), pltpu.SemaphoreType.DMA((n,))],
    )(*srcs)


def _flip_peers(x, y, c):
    flips = [(fx, fy, fc) for fx in (0, 1) for fy in (0, 1) for fc in (0, 1)][1:]
    return [(x ^ fx, y ^ fy, c ^ fc) for fx, fy, fc in flips]


def _split_copies(in_refs, land_refs, send_sems, recv_sems, gather):
    x, y, c = lax.axis_index("x"), lax.axis_index("y"), lax.axis_index("c")
    my_id = 4 * x + 2 * y + c
    copies = []
    for kk, p in enumerate(_flip_peers(x, y, c)):
        for a in range(len(in_refs)):
            src = in_refs[a] if gather else in_refs[a].at[4 * p[0] + 2 * p[1] + p[2]]
            copies.append(pltpu.make_async_remote_copy(
                src_ref=src, dst_ref=land_refs[a].at[my_id], send_sem=send_sems.at[a * 7 + kk], recv_sem=recv_sems.at[a * 7 + kk],
                device_id=p, device_id_type=pl.DeviceIdType.MESH))
    return copies


def _exchange_start(name, srcs, gather, after=None):
    n = len(srcs)
    first_out = 2 * n + (0 if after is None else 1)

    def body(*refs):
        for cp in _split_copies(refs[:n], refs[n:2 * n], refs[first_out], refs[first_out + 1], gather):
            cp.start()
        refs[-1][...] = jnp.zeros_like(refs[-1])

    hbm, sem = pl.BlockSpec(memory_space=pltpu.HBM), pl.BlockSpec(memory_space=pltpu.SEMAPHORE)
    land_shapes = [((N_DEV,) + s.shape if gather else s.shape, s.dtype) for s in srcs]
    lands = [pltpu.with_memory_space_constraint(lax.empty(shp, dt), pltpu.HBM) for shp, dt in land_shapes]
    srcs = [pltpu.with_memory_space_constraint(s, pltpu.HBM) for s in srcs]
    res = pl.pallas_call(
        body, name=name,
        out_shape=[pltpu.SemaphoreType.DMA((7 * n,)), pltpu.SemaphoreType.DMA((7 * n,))] + [pltpu.HBM(s.shape, s.dtype) for s in srcs]
        + [pltpu.HBM(shp, dt) for shp, dt in land_shapes] + [jax.ShapeDtypeStruct((8, LANES), F32)],
        in_specs=[hbm] * (2 * n) + ([] if after is None else [pl.BlockSpec(memory_space=pl.ANY)]),
        out_specs=[sem, sem] + [hbm] * (2 * n) + [pl.BlockSpec(memory_space=pltpu.VMEM)],
        input_output_aliases={i: 2 + i for i in range(2 * n)},
        compiler_params=pltpu.CompilerParams(has_side_effects=pltpu.SideEffectType.DATAFLOW_SIDE_EFFECTING),
    )(*srcs, *lands, *([] if after is None else [after]))
    return res[0], res[1], res[2:2 + n], res[2 + n:2 + 2 * n], res[-1]


def _exchange_wait(name, handles, after, gather):
    send_sems, recv_sems, srcs, lands, _ = handles
    n = len(srcs)

    def body(*refs):
        for cp in _split_copies(refs[:n], refs[n:2 * n], refs[2 * n], refs[2 * n + 1], gather):
            cp.wait_send()
            cp.wait_recv()

    hbm, sem = pl.BlockSpec(memory_space=pltpu.HBM), pl.BlockSpec(memory_space=pltpu.SEMAPHORE)
    res = pl.pallas_call(
        body, name=name, out_shape=[pltpu.HBM(t.shape, t.dtype) for t in list(srcs) + list(lands)],
        in_specs=[hbm] * (2 * n) + [sem, sem, pl.BlockSpec(memory_space=pl.ANY)], out_specs=[hbm] * (2 * n),
        input_output_aliases={i: i for i in range(2 * n)},
        compiler_params=pltpu.CompilerParams(has_side_effects=pltpu.SideEffectType.DATAFLOW_SIDE_EFFECTING),
    )(*srcs, *lands, send_sems, recv_sems, after)
    my_id = 4 * lax.axis_index("x") + 2 * lax.axis_index("y") + lax.axis_index("c")
    own = [s if gather else lax.dynamic_index_in_dim(s, my_id, 0, keepdims=False) for s in res[:n]]
    return [lax.dynamic_update_index_in_dim(land, o, my_id, 0) for land, o in zip(res[n:], own)]


def _adamw(name, parts, w, m, v):
    rows, cols = w.shape
    tr = _pick(rows, (128, 64, 32, 16, 8))
    pspec = pl.BlockSpec((N_DEV, tr, cols), lambda i: (0, i, 0))
    rspec = pl.BlockSpec((tr, cols), lambda i: (i, 0))

    def body(p_ref, w_ref, m_ref, v_ref, g_ref, d_ref, m2_ref, v2_ref):
        g, d, m2, v2 = _adamw_fn([p_ref[s] for s in range(N_DEV)], w_ref[...], m_ref[...], v_ref[...])
        g_ref[...], d_ref[...], m2_ref[...], v2_ref[...] = g, d, m2, v2

    return pl.pallas_call(
        body, name=name, grid=(rows // tr,), in_specs=[pspec, rspec, rspec, rspec], out_specs=[rspec] * 4,
        out_shape=[jax.ShapeDtypeStruct((rows, cols), F32)] * 4,
        compiler_params=pltpu.CompilerParams(dimension_semantics=("parallel",), vmem_limit_bytes=VMEM_LIMIT),
    )(parts, w, m, v)


def kernel(x, positions, g_mix, w_in, g_q_a, w_q_b, g_kv_a, w_kv_b, g_qn, g_kn, w_mla_out, ret_decay_fwd, ret_decay_bwd, w_ret_out, w_out, g_ffn, w_gate_up, w_down, loss_target, m_g_mix, m_w_in, m_g_q_a, m_w_q_b, m_g_kv_a, m_w_kv_b, m_g_qn, m_g_kn, m_w_mla_out, m_ret_decay_fwd, m_ret_decay_bwd, m_w_ret_out, m_w_out, m_g_ffn, m_w_gate_up, m_w_down, v_g_mix, v_w_in, v_g_q_a, v_w_q_b, v_g_kv_a, v_w_kv_b, v_g_qn, v_g_kn, v_w_mla_out, v_ret_decay_fwd, v_ret_decay_bwd, v_w_ret_out, v_w_out, v_g_ffn, v_w_gate_up, v_w_down):
    w = dict(g_mix=g_mix, w_in=w_in, g_q_a=g_q_a, w_q_b=w_q_b, g_kv_a=g_kv_a, w_kv_b=w_kv_b, g_qn=g_qn, g_kn=g_kn, w_mla_out=w_mla_out,
             ret_decay_fwd=ret_decay_fwd, ret_decay_bwd=ret_decay_bwd, w_ret_out=w_ret_out, w_out=w_out, g_ffn=g_ffn,
             w_gate_up=w_gate_up, w_down=w_down)
    m = dict(g_mix=m_g_mix, w_in=m_w_in, g_q_a=m_g_q_a, w_q_b=m_w_q_b, g_kv_a=m_g_kv_a, w_kv_b=m_w_kv_b, g_qn=m_g_qn, g_kn=m_g_kn,
             w_mla_out=m_w_mla_out, ret_decay_fwd=m_ret_decay_fwd, ret_decay_bwd=m_ret_decay_bwd, w_ret_out=m_w_ret_out, w_out=m_w_out,
             g_ffn=m_g_ffn, w_gate_up=m_w_gate_up, w_down=m_w_down)
    v = dict(g_mix=v_g_mix, w_in=v_w_in, g_q_a=v_g_q_a, w_q_b=v_w_q_b, g_kv_a=v_g_kv_a, w_kv_b=v_w_kv_b, g_qn=v_g_qn, g_kn=v_g_kn,
             w_mla_out=v_w_mla_out, ret_decay_fwd=v_ret_decay_fwd, ret_decay_bwd=v_ret_decay_bwd, w_ret_out=v_w_ret_out, w_out=v_w_out,
             g_ffn=v_g_ffn, w_gate_up=v_w_gate_up, w_down=v_w_down)
    gains = {n: w[n].reshape(1, ln) for n, ln in GAINS}

    axis_of = {n: axis for n, _, axis in MATS}
    later = [n for n, _, _ in MATS if n not in FIRST_WEIGHTS]
    gathered = _all_gather([w[n].astype(WIRE) for n in FIRST_WEIGHTS])
    W = {n: _unshard(g, axis_of[n]) for n, g in zip(FIRST_WEIGHTS, gathered)}
    later_handles = _exchange_start("gather_later_start", [w[n].astype(WIRE) for n in later], True, after=gathered[0])

    def late_weights(after):
        lands = _exchange_wait("gather_later_wait", later_handles, after, True)
        return {n: _unshard(g, axis_of[n]) for n, g in zip(later, lands)}

    grad_groups = []

    def grad_hook(g):
        names = tuple(g)
        handles = _exchange_start("grads_start_%d" % len(grad_groups), [_reshard(g[n], axis_of[n]).astype(GWIRE) for n in names], False)
        grad_groups.append((names, handles))
        return handles[4]

    S = x.shape[1]
    pos = positions.reshape(S, 1).astype(F32)
    loss_rows, grad_x, gG, gW = _local_step(x.reshape(S, D_MODEL), pos, loss_target.reshape(S, D_MODEL), gains, W, late_weights, grad_hook,
                                            start_after=later_handles[4])
    loss = lax.psum(jnp.sum(loss_rows), ("x", "y", "c"))

    last = [n for n, _, _ in MATS if n not in EARLY_GRADS + MID_GRADS]
    pieces = [_reshard(gW[n], axis_of[n]).astype(GWIRE) for n in last]
    pieces.append(jnp.broadcast_to(_pack_gains(gG)[None], (N_DEV, 1, GAIN_PAD)))
    late_parts = _exchange("grads_last", pieces, False)
    parts = dict(zip(last, late_parts))
    for i, (names, handles) in enumerate(grad_groups):
        parts.update(zip(names, _exchange_wait("grads_wait_%d" % i, handles, late_parts[-1], False)))
    out = [dict() for _ in range(4)]
    for n, _, _ in MATS:
        for o, r in zip(out, _adamw("adamw_" + n, parts[n], w[n], m[n], v[n])):
            o[n] = r
    for o, r in zip(out, _adamw("adamw_gains", late_parts[-1], _pack_gains(w), _pack_gains(m), _pack_gains(v))):
        o.update(_unpack_gains(r))
    return (loss, grad_x.reshape(x.shape), *[o[n] for o in out for n in ORDER])
```

```python
import functools

import numpy as np
import jax
import jax.numpy as jnp
from jax import lax
from jax.experimental import pallas as pl
from jax.experimental.pallas import tpu as pltpu

F32 = jnp.float32
MXU = jnp.bfloat16
WIRE = jnp.bfloat16
GWIRE = jnp.bfloat16

N_DEV = 8
D_MODEL = 1024
HEADS = 8
LANES = 128
Q_RANK, KV_RANK = 256, 128
NOPE, ROPE_M, V_M = 64, 32, 64
QK_M = NOPE + ROPE_M
RQK = 64
CHUNK = 128
FFN = 2816
THETA = 10000.0
EPS = 1e-6
LR, B1, B2, AEPS, WD, STEP = 0.001, 0.9, 0.999, 1e-08, 0.01, 10
VMEM_LIMIT = 56 * 1024 * 1024

NN = ((1,), (0,))
NT = ((1,), (1,))
TN = ((0,), (0,))

P_GATES, P_VR, P_GR, P_QR, P_KR, P_CQ, P_CKV, P_KROPE, P_WIDTH = 0, 2048, 3072, 4096, 4608, 5120, 5376, 5504, 5632
O_CQ, O_CKV, O_KROPE, O_QR, O_KR, O_VR, O_GR, O_GATES = 0, 256, 384, 416, 928, 1440, 2464, 3488


def _dot(a, b, dims):
    return lax.dot_general(a, b, (dims, ((), ())), preferred_element_type=F32)


def _pick(dim, cands):
    for c in cands:
        if dim % c == 0:
            return c
    return dim


def _pairs(t):
    return t.reshape(t.shape[0], 4, 2, 2, 32).transpose(0, 1, 3, 2, 4).reshape(t.shape[0], 512)


def _win_pad(w):
    z = jnp.zeros((w.shape[0], 48), w.dtype)
    kr = w[:, O_KROPE:O_KROPE + 32]
    return jnp.concatenate([w[:, O_GATES:], w[:, O_VR:O_VR + 1024], w[:, O_GR:O_GR + 1024], _pairs(w[:, O_QR:O_QR + 512]),
                            _pairs(w[:, O_KR:O_KR + 512]), w[:, :O_CKV], w[:, O_CKV:O_KROPE], kr[:, :16], z, kr[:, 16:], z], axis=1)


def _win_unpad(g):
    return jnp.concatenate([g[:, P_CQ:P_CQ + 256], g[:, P_CKV:P_CKV + 128], g[:, P_KROPE:P_KROPE + 16], g[:, P_KROPE + 64:P_KROPE + 80],
                            _pairs(g[:, P_QR:P_QR + 512]), _pairs(g[:, P_KR:P_KR + 512]), g[:, P_VR:P_VR + 1024],
                            g[:, P_GR:P_GR + 1024], g[:, P_GATES:P_GATES + 2048]], axis=1)


def _qk_pad(t):
    z = jnp.zeros(t.shape[:-1] + (32,), t.dtype)
    return jnp.concatenate([t[..., 64:80], t[..., 0:48], t[..., 80:96], t[..., 48:64], z], axis=-1)


def _qk_unpad(p):
    return jnp.concatenate([p[..., 16:64], p[..., 80:96], p[..., 0:16], p[..., 64:80]], axis=-1)


def _wq_pad(w):
    return _qk_pad(w.reshape(Q_RANK, HEADS, QK_M)).reshape(Q_RANK, HEADS * LANES)


def _wq_unpad(g):
    return _qk_unpad(g.reshape(Q_RANK, HEADS, LANES)).reshape(Q_RANK, HEADS * QK_M)


def _wkv_pad(w):
    t = w.reshape(KV_RANK, HEADS, NOPE + V_M)
    z = lambda n: jnp.zeros((KV_RANK, HEADS, n), w.dtype)
    wk = jnp.concatenate([z(16), t[..., 0:48], z(16), t[..., 48:64], z(32)], axis=-1)
    wv = jnp.concatenate([t[..., 64:128], z(64)], axis=-1)
    return wk.reshape(KV_RANK, HEADS * LANES), wv.reshape(KV_RANK, HEADS * LANES)


def _wkv_unpad(dwk, dwv):
    k, v = dwk.reshape(KV_RANK, HEADS, LANES), dwv.reshape(KV_RANK, HEADS, LANES)
    return jnp.concatenate([k[..., 16:64], k[..., 80:96], v[..., 0:64]], axis=-1).reshape(KV_RANK, HEADS * (NOPE + V_M))


def _wmla_pad(w):
    t = w.reshape(HEADS, V_M, D_MODEL)
    return jnp.concatenate([t, jnp.zeros_like(t)], axis=1).reshape(HEADS * LANES, D_MODEL)


def _wmla_unpad(g):
    return g.reshape(HEADS, LANES, D_MODEL)[:, :V_M].reshape(HEADS * V_M, D_MODEL)


def _rowwise(name, fn, rows, ts, ins, outs, accs=(), ncol=1):
    n_in, n_out, n_acc = len(ins), len(outs), len(accs)

    def colmap(col):
        if callable(col):
            return lambda i, j: (i, col(j))
        return lambda i, j: (i, col)

    arrays, in_specs = [], []
    for arr, spec in ins:
        arrays.append(arr)
        if spec is None:
            in_specs.append(pl.BlockSpec(arr.shape, functools.partial(lambda i, j, nd: (0,) * nd, nd=arr.ndim)))
        else:
            in_specs.append(pl.BlockSpec((ts, spec[0]), colmap(spec[1])))
    out_shape, out_specs = [], []
    for total, dtype, width, col in outs:
        out_shape.append(jax.ShapeDtypeStruct((rows, total), dtype))
        out_specs.append(pl.BlockSpec((ts, width), colmap(col)))
    for shp in accs:
        out_shape.append(jax.ShapeDtypeStruct(shp, F32))
        out_specs.append(pl.BlockSpec(shp, functools.partial(lambda i, j, nd: (0,) * nd, nd=len(shp))))

    def body(*refs):
        vals = [r[...] for r in refs[:n_in]]
        res = fn(*vals)
        if not isinstance(res, (tuple, list)):
            res = (res,)
        for r, v in zip(refs[n_in:n_in + n_out], res[:n_out]):
            r[...] = v.astype(r.dtype)
        if n_acc:
            first = jnp.logical_and(pl.program_id(0) == 0, pl.program_id(1) == 0)
            for r, v in zip(refs[n_in + n_out:], res[n_out:]):
                @pl.when(first)
                def _(r=r):
                    r[...] = jnp.zeros_like(r)
                r[...] += v.astype(F32)

    res = pl.pallas_call(
        body, name=name, grid=(rows // ts, ncol), in_specs=in_specs, out_specs=out_specs, out_shape=out_shape,
        compiler_params=pltpu.CompilerParams(dimension_semantics=("arbitrary", "arbitrary"), vmem_limit_bytes=VMEM_LIMIT),
    )(*arrays)
    return res


MM_OPERAND_BYTES = 24 * 1024 * 1024


def _mm(name, a, b, mode, add=None, after=None):
    a_halves, b_halves = a.ndim == 3, b.ndim == 3
    assert not a_halves or mode == "nt"
    assert not b_halves or mode == "tn"
    if mode == "nn":
        (M, K), N = a.shape, b.shape[1]
    elif mode == "nt":
        M, K, N = a.shape[-2], a.shape[-1] * (2 if a_halves else 1), b.shape[0]
    else:
        (K, M), N = a.shape, b.shape[-1] * (2 if b_halves else 1)
    tm = _pick(M, (1024, 512, 1408, 256, 128))
    tn = _pick(N // 2 if b_halves else N, (1408, 1024, 512, 256, 128))
    fits = lambda t: 2 * (tm + tn) * t * a.dtype.itemsize <= MM_OPERAND_BYTES
    kdiv = K // 2 if a_halves else K
    tk = next(t for t in (K, 4096, 2816, 2048, 1408, 1024, 512, 256, 128) if kdiv % t == 0 and (fits(t) or t == 128))
    nk = K // tk
    dims = {"nn": NN, "nt": NT, "tn": TN}[mode]
    if a_halves:
        per = kdiv // tk
        a_spec = pl.BlockSpec((None, tm, tk), lambda i, j, k: (k // per, i, k % per))
    else:
        a_spec = pl.BlockSpec((tk, tm), lambda i, j, k: (k, i)) if mode == "tn" else pl.BlockSpec((tm, tk), lambda i, j, k: (i, k))
    if b_halves:
        perj = (N // 2) // tn
        b_spec = pl.BlockSpec((None, tk, tn), lambda i, j, k: (j // perj, k, j % perj))
    else:
        b_spec = pl.BlockSpec((tn, tk), lambda i, j, k: (j, k)) if mode == "nt" else pl.BlockSpec((tk, tn), lambda i, j, k: (k, j))
    o_spec = pl.BlockSpec((tm, tn), lambda i, j, k: (i, j))
    has_add = add is not None

    def body(*refs):
        a_ref, b_ref, o_ref = refs[0], refs[1], refs[-1]
        d = _dot(a_ref[...], b_ref[...], dims)
        first = (d + refs[2][...]) if has_add else d
        if nk == 1:
            o_ref[...] = first
        else:
            k = pl.program_id(2)

            @pl.when(k == 0)
            def _():
                o_ref[...] = first

            @pl.when(k > 0)
            def _():
                o_ref[...] += d

    args = [a, b] + ([add] if has_add else []) + ([] if after is None else [after])
    specs = [a_spec, b_spec] + ([o_spec] if has_add else []) + ([] if after is None else [pl.BlockSpec(memory_space=pl.ANY)])
    return pl.pallas_call(
        body, name=name, grid=(M // tm, N // tn, nk), in_specs=specs, out_specs=o_spec,
        out_shape=jax.ShapeDtypeStruct((M, N), F32),
        compiler_params=pltpu.CompilerParams(dimension_semantics=("parallel", "parallel", "arbitrary"), vmem_limit_bytes=VMEM_LIMIT),
    )(*args)


def _mm_rows(name, a, b, fn, row_ins, whole_ins, outs, accs=(), mode="nn"):
    (M, K), N = a.shape, b.shape[1 if mode == "nn" else 0]
    tm = _pick(M, (512, 256, 128))
    n_in, n_out = 2 + len(row_ins) + len(whole_ins), len(outs)
    windows = [t if isinstance(t, tuple) else (t, (t.shape[1], 0)) for t in row_ins]
    row_ins = [t for t, _ in windows]
    row_specs = [pl.BlockSpec((tm, w), functools.partial(lambda i, col: (i, col), col=col)) for _, (w, col) in windows]

    def body(*refs):
        d = _dot(refs[0][...], refs[1][...], NN if mode == "nn" else NT)
        res = fn(d, *[r[...] for r in refs[2:n_in]])
        for r, v in zip(refs[n_in:n_in + n_out], res[:n_out]):
            r[...] = v.astype(r.dtype)
        for r, v in zip(refs[n_in + n_out:], res[n_out:]):
            @pl.when(pl.program_id(0) == 0)
            def _(r=r):
                r[...] = jnp.zeros_like(r)
            r[...] += v

    row = pl.BlockSpec((tm, N), lambda i: (i, 0))
    whole = lambda t: pl.BlockSpec(t.shape, functools.partial(lambda i, nd: (0,) * nd, nd=t.ndim))
    return pl.pallas_call(
        body, name=name, grid=(M // tm,),
        in_specs=[pl.BlockSpec((tm, K), lambda i: (i, 0)), whole(b)] + row_specs + [whole(t) for t in whole_ins],
        out_specs=[row] * n_out + [pl.BlockSpec(s, functools.partial(lambda i, nd: (0,) * nd, nd=len(s))) for s in accs],
        out_shape=[jax.ShapeDtypeStruct((M, N), dt) for dt in outs] + [jax.ShapeDtypeStruct(s, F32) for s in accs],
        compiler_params=pltpu.CompilerParams(dimension_semantics=("arbitrary",), vmem_limit_bytes=VMEM_LIMIT),
    )(a, b, *row_ins, *whole_ins)


def _ffn_tiles(S):
    return _pick(S, (1024, 512, 256, 128)), _pick(FFN, (1408, 704, 256, 128))


def _gate_up_swiglu(h2, wgu):
    S, K = h2.shape
    tm, tn = _ffn_tiles(S)
    nj = FFN // tn

    def body(a_ref, bg_ref, bu_ref, gu_ref, act_ref):
        a = a_ref[...]
        g, u = _dot(a, bg_ref[...], NN), _dot(a, bu_ref[...], NN)
        gu_ref[0], gu_ref[1] = g.astype(gu_ref.dtype), u.astype(gu_ref.dtype)
        act_ref[...] = _swiglu_fn(g, u).astype(act_ref.dtype)

    return pl.pallas_call(
        body, name="gate_up_swiglu", grid=(S // tm, nj),
        in_specs=[pl.BlockSpec((tm, K), lambda i, j: (i, 0)), pl.BlockSpec((K, tn), lambda i, j: (0, j)),
                  pl.BlockSpec((K, tn), lambda i, j: (0, nj + j))],
        out_specs=[pl.BlockSpec((2, tm, tn), lambda i, j: (0, i, j)), pl.BlockSpec((tm, tn), lambda i, j: (i, j))],
        out_shape=[jax.ShapeDtypeStruct((2, S, FFN), MXU), jax.ShapeDtypeStruct((S, FFN), MXU)],
        compiler_params=pltpu.CompilerParams(dimension_semantics=("parallel", "parallel"), vmem_limit_bytes=VMEM_LIMIT),
    )(h2, wgu, wgu)


def _d_act_swiglu(dx2, wdown, gu):
    S, K = dx2.shape
    tm, tn = _ffn_tiles(S)

    def body(a_ref, b_ref, gu_ref, o_ref):
        dact = _dot(a_ref[...], b_ref[...], NT)
        _, vjp = jax.vjp(_swiglu_fn, gu_ref[0].astype(F32), gu_ref[1].astype(F32))
        dg, du = vjp(dact)
        o_ref[0], o_ref[1] = dg.astype(o_ref.dtype), du.astype(o_ref.dtype)

    stacked = pl.BlockSpec((2, tm, tn), lambda i, j: (0, i, j))
    return pl.pallas_call(
        body, name="d_act_swiglu", grid=(S // tm, FFN // tn),
        in_specs=[pl.BlockSpec((tm, K), lambda i, j: (i, 0)), pl.BlockSpec((tn, K), lambda i, j: (j, 0)), stacked],
        out_specs=stacked, out_shape=jax.ShapeDtypeStruct((2, S, FFN), MXU),
        compiler_params=pltpu.CompilerParams(dimension_semantics=("parallel", "parallel"), vmem_limit_bytes=VMEM_LIMIT),
    )(dx2, wdown, gu)


@jax.custom_vjp
def _swap64(x):
    return pltpu.roll(x, 64, 1)


_swap64.defvjp(lambda x: (_swap64(x), None), lambda _, g: (_swap64(g),))


@jax.custom_vjp
def _mxdot(a, b):
    return _dot(a.astype(MXU), b.astype(MXU), NN)


def _mxdot_bwd(res, g):
    a, b = res
    gb = g.astype(MXU)
    return _dot(gb, b.astype(MXU), NT), _dot(a.astype(MXU), gb, TN)


_mxdot.defvjp(lambda a, b: (_mxdot(a, b), (a, b)), _mxdot_bwd)


def _row_sum(t):
    if t.shape[-1] == LANES:
        return lax.dot_general(t, jnp.ones((LANES, LANES), F32), ((NN), ((), ())), precision=lax.Precision.HIGH,
                               preferred_element_type=F32)
    return jnp.sum(t, axis=-1, keepdims=True)


@functools.partial(jax.custom_vjp, nondiff_argnums=(1,))
def _unit_rms(x, n):
    return x * lax.rsqrt(_row_sum(x * x) * (1.0 / n) + EPS)


def _unit_rms_fwd(x, n):
    r = lax.rsqrt(_row_sum(x * x) * (1.0 / n) + EPS)
    y = x * r
    return y, (y, r)


def _unit_rms_bwd(n, res, g):
    y, r = res
    return (r * (g - y * (_row_sum(g * y) * (1.0 / n))),)


_unit_rms.defvjp(_unit_rms_fwd, _unit_rms_bwd)


def _rms(x):
    return _unit_rms(x, x.shape[-1])


def _rmsg_fn(x, g):
    return _rms(x) * g


def _silu(x):
    return x * jax.nn.sigmoid(x)


def _tables_fn(pos, inv_m, sgn_m, inv_r, sgn_r):
    am, ar = pos * inv_m, pos * inv_r
    return jnp.cos(am), jnp.sin(am) * sgn_m, jnp.cos(ar), jnp.sin(ar) * sgn_r


def _head_blocks(t):
    return [t[:, LANES * h:LANES * (h + 1)] for h in range(t.shape[1] // LANES)]


def _mla_prep_fn(cq, ckv, kr, cosm, sinm, gqa, gkva, gqn, gkn, wq, wk, wv):
    cqn = _rms(cq) * gqa
    ckvn = _rms(ckv) * gkva
    q_raw = _mxdot(cqn, wq)
    k_raw = _mxdot(ckvn, wk)
    lane = lax.broadcasted_iota(jnp.int32, (1, HEADS * LANES), 1)
    v = _mxdot(ckvn, wv) + (lane % LANES == V_M).astype(F32)

    def norm_rope(blocks, g, extra):
        outs = []
        for b in blocks:
            if extra is not None:
                b = b + extra
            n = _unit_rms(b, QK_M) * g
            outs.append(n * cosm + _swap64(n) * sinm)
        return jnp.concatenate(outs, axis=1)

    q = norm_rope(_head_blocks(q_raw), gqn, None)
    k = norm_rope(_head_blocks(k_raw), gkn, kr)
    return q, k, v


def _ret_prep_fn(qr, kr, cosr, sinr):
    def rope(t, scale):
        return jnp.concatenate([(b * cosr + _swap64(b) * sinr) * scale for b in _head_blocks(t)], axis=1)
    return rope(qr, 1.0), rope(kr, RQK ** -0.5)


def _ret_post_fn(rf, rb, gr):
    ret = rf + rb
    outs = []
    for b, g in zip(_head_blocks(ret), _head_blocks(gr)):
        outs.append(_silu(g) * _rms(b))
    return jnp.concatenate(outs, axis=1)


def _merge_fn(ga, gb, ya, yb):
    return jax.nn.sigmoid(ga) * ya + jax.nn.sigmoid(gb) * yb


def _swiglu_fn(gate, up):
    return _silu(gate) * up


def _loss_fn(x2, tgt):
    d = x2 - tgt
    return d * (1.0 / D_MODEL), 0.5 * jnp.sum(d * d, axis=0, keepdims=True) * (1.0 / D_MODEL)


def _adamw_fn(parts, w, m, v):
    g = parts[0].astype(F32)
    for p in range(1, N_DEV):
        g = g + parts[p].astype(F32)
    m2 = B1 * m + (1.0 - B1) * g
    v2 = B2 * v + (1.0 - B2) * jnp.square(g)
    m_hat = m2 / (1.0 - B1 ** STEP)
    v_hat = v2 / (1.0 - B2 ** STEP)
    delta = -LR * (m_hat / (jnp.sqrt(v_hat) + AEPS) + WD * w)
    return g, delta, m2, v2


SCALE = QK_M ** -0.5
LOG2E = 1.4426950408889634
FLASH_ROWS = 32


def _flash_fwd(q, k, v):
    S = q.shape[0]
    tk = _pick(S, (512, 256, 128))
    tq = _pick(S, (1024, 512, 256, 128))
    ncb = tk // LANES
    nkv = S // tk
    assert nkv % 2 == 0, "kv tiles are processed in pairs"
    mrows = 64
    c = SCALE * LOG2E

    def body(q_ref, k_ref, v_ref, o_ref, lse_ref, s_a, p_a, s_b, p_b, m_sc, a_sc, acc_sc):
        m_sc[...] = jnp.full_like(m_sc, -jnp.inf)
        acc_sc[...] = jnp.zeros_like(acc_sc)
        qb = q_ref[...]

        def scores(j, s_buf):
            s_buf[...] = _dot(qb, k_ref[pl.ds(pl.multiple_of(j * tk, tk), tk), :], NT)

        def stage(j, s_buf, p_buf, s_next):
            scores(jnp.minimum(j + 1, nkv - 1), s_next)
            for r in range(tq // mrows):
                rows = slice(r * mrows, (r + 1) * mrows)
                cols = [s_buf[rows, LANES * cb:LANES * (cb + 1)] for cb in range(ncb)]
                m_prev = m_sc[rows, :]
                row_max = jnp.max(functools.reduce(jnp.maximum, cols), axis=-1, keepdims=True)
                m_new = jnp.maximum(m_prev, jnp.broadcast_to(row_max, (mrows, LANES)))
                a_sc[rows, :] = jnp.exp2((m_prev - m_new) * c)
                m_sc[rows, :] = m_new
                for cb in range(ncb):
                    p_buf[rows, LANES * cb:LANES * (cb + 1)] = jnp.exp2((cols[cb] - m_new) * c).astype(p_buf.dtype)
            acc_sc[...] = a_sc[...] * acc_sc[...] + _dot(p_buf[...], v_ref[pl.ds(pl.multiple_of(j * tk, tk), tk), :], NN)

        scores(0, s_a)

        def pair_step(t, carry):
            stage(2 * t, s_a, p_a, s_b)
            stage(2 * t + 1, s_b, p_b, s_a)
            return carry

        lax.fori_loop(0, nkv // 2, pair_step, 0, unroll=4)
        acc = acc_sc[...]
        lane = lax.broadcasted_iota(jnp.int32, (1, LANES), 1)
        l = jnp.sum(jnp.where(lane == V_M, acc, 0.0), axis=-1, keepdims=True)
        o_ref[...] = (acc / l).astype(o_ref.dtype)
        lse_ref[...] = m_sc[...] * c + jnp.log2(jnp.broadcast_to(l, (tq, LANES)))

    qspec = pl.BlockSpec((tq, LANES), lambda h, i: (i, h))
    kspec = pl.BlockSpec((S, LANES), lambda h, i: (0, h))
    return pl.pallas_call(
        body, name="flash_fwd", grid=(HEADS, S // tq), in_specs=[qspec, kspec, kspec], out_specs=[qspec, qspec],
        out_shape=[jax.ShapeDtypeStruct((S, HEADS * LANES), MXU), jax.ShapeDtypeStruct((S, HEADS * LANES), F32)],
        scratch_shapes=[pltpu.VMEM((tq, tk), F32), pltpu.VMEM((tq, tk), MXU)] * 2 + [pltpu.VMEM((tq, LANES), F32)] * 3,
        compiler_params=pltpu.CompilerParams(dimension_semantics=("parallel", "arbitrary"), vmem_limit_bytes=VMEM_LIMIT),
    )(q, k, v)


def _delta_fn(o, do):
    outs = [jnp.broadcast_to(jnp.sum(a * b, axis=-1, keepdims=True), a.shape) for a, b in zip(_head_blocks(o), _head_blocks(do))]
    return do, jnp.concatenate(outs, axis=1)


def _flash_bwd(q, k, v, do, lse, delta):
    S = q.shape[0]
    tq = tk = _pick(S, (512, 256, 128))
    ncb = tk // LANES
    c = SCALE * LOG2E

    nq = S // tq
    assert nq % 2 == 0, "q tiles are processed in pairs"

    def body(q_ref, k_ref, v_ref, do_ref, lse_ref, dl_ref, dq_ref, dk_ref, dv_ref, s_a, dp_a, p_a, ds_a, s_b, dp_b, p_b, ds_b):
        @pl.when(pl.program_id(1) == 0)
        def _():
            dq_ref[...] = jnp.zeros_like(dq_ref)

        dk_ref[...] = jnp.zeros_like(dk_ref)
        dv_ref[...] = jnp.zeros_like(dv_ref)
        kb, vb = k_ref[...], v_ref[...]

        def scores(i, s_buf, dp_buf):
            q_rows = pl.ds(pl.multiple_of(i * tq, tq), tq)
            s_buf[...] = _dot(q_ref[q_rows, :], kb, NT)
            dp_buf[...] = _dot(do_ref[q_rows, :], vb, NT)

        def stage(i, s_buf, dp_buf, p_buf, ds_buf, s_next, dp_next):
            scores(jnp.minimum(i + 1, nq - 1), s_next, dp_next)
            for r in range(tq // FLASH_ROWS):
                rows = slice(r * FLASH_ROWS, (r + 1) * FLASH_ROWS)
                grows = pl.ds(pl.multiple_of(i * tq + r * FLASH_ROWS, FLASH_ROWS), FLASH_ROWS)
                lse_b, dl_b = lse_ref[grows, :], dl_ref[grows, :]
                for cb in range(ncb):
                    sl = slice(LANES * cb, LANES * (cb + 1))
                    p = jnp.exp2(s_buf[rows, sl] * c - lse_b)
                    p_buf[rows, sl] = p.astype(p_buf.dtype)
                    ds_buf[rows, sl] = (p * (dp_buf[rows, sl] - dl_b) * SCALE).astype(ds_buf.dtype)
            q_rows = pl.ds(pl.multiple_of(i * tq, tq), tq)
            dv_ref[...] += _dot(p_buf[...], do_ref[q_rows, :], TN)
            dk_ref[...] += _dot(ds_buf[...], q_ref[q_rows, :], TN)
            dq_ref[q_rows, :] += _dot(ds_buf[...], kb, NN)

        scores(0, s_a, dp_a)

        def pair_step(t, carry):
            stage(2 * t, s_a, dp_a, p_a, ds_a, s_b, dp_b)
            stage(2 * t + 1, s_b, dp_b, p_b, ds_b, s_a, dp_a)
            return carry

        lax.fori_loop(0, nq // 2, pair_step, 0, unroll=4)

    hspec = pl.BlockSpec((S, LANES), lambda h, j: (0, h))
    kspec = pl.BlockSpec((tk, LANES), lambda h, j: (j, h))
    full = jax.ShapeDtypeStruct((S, HEADS * LANES), F32)
    tile_bufs = [pltpu.VMEM((tq, tk), F32), pltpu.VMEM((tq, tk), F32), pltpu.VMEM((tq, tk), MXU), pltpu.VMEM((tq, tk), MXU)]
    return pl.pallas_call(
        body, name="flash_bwd", grid=(HEADS, S // tk), in_specs=[hspec, kspec, kspec, hspec, hspec, hspec],
        out_specs=[hspec, kspec, kspec], out_shape=[full, full, full],
        scratch_shapes=tile_bufs + tile_bufs,
        compiler_params=pltpu.CompilerParams(dimension_semantics=("parallel", "arbitrary"), vmem_limit_bytes=VMEM_LIMIT),
    )(q, k, v, do, lse, delta)


def _ret_consts(lgh, head, rev):
    C = CHUNK
    lane = lax.broadcasted_iota(jnp.int32, (1, LANES), 1)
    hm = ((lane // 32) % 2 == head % 2).astype(F32)
    r = lax.broadcasted_iota(jnp.int32, (C, C), 0)
    c = lax.broadcasted_iota(jnp.int32, (C, C), 1)
    diff = ((c - r) if rev else (r - c)).astype(F32)
    mask = (diff > 0) if rev else (diff >= 0)
    dpos = jnp.maximum(diff, 0.0)
    din = jnp.where(mask, jnp.exp(lgh * dpos), 0.0)
    idx = lax.broadcasted_iota(jnp.int32, (C, 1), 0).astype(F32)
    eq = (C - idx) if rev else (idx + 1.0)
    ek = idx if rev else (C - 1.0 - idx)
    qd, kd = jnp.exp(lgh * eq), jnp.exp(lgh * ek)
    cd = jnp.exp(lgh * jnp.full((1, 1), float(C), F32))
    return hm, din, dpos, qd, kd, cd, eq, ek


RET_HEADS_PER_STEP = 8


def _ret_fwd(name, qt, kt, proj, lg, rev):
    S = qt.shape[0]
    C = CHUNK
    TB = _pick(S, (512, 256, 128))
    cb, nb = TB // C, S // TB
    hps = RET_HEADS_PER_STEP
    blk = (lambda g: nb - 1 - g) if rev else (lambda g: g)

    def body(lg_ref, q_ref, k_ref, v_ref, o_ref, st_ref, state_sc):
        hg, g = pl.program_id(0), pl.program_id(1)

        @pl.when(g == 0)
        def _():
            state_sc[...] = jnp.zeros_like(state_sc)

        consts = [_ret_consts(lg_ref[hg * hps + u], u, rev) for u in range(hps)]
        order = list(reversed(range(cb))) if rev else list(range(cb))
        units = [(cc, u) for cc in order for u in range(hps)]

        def operands(cc, u):
            rows = pl.ds(cc * C, C)
            pair = slice(LANES * (u // 2), LANES * (u // 2 + 1))
            hm = consts[u][0]
            return q_ref[rows, pair] * hm, k_ref[rows, pair] * hm, v_ref[rows, LANES * u:LANES * (u + 1)].astype(MXU)

        a, inc = {}, {}
        for cc, u in units:
            q, k, v = operands(cc, u)
            a[cc, u] = _dot(q.astype(MXU), k.astype(MXU), NT) * consts[u][1]
            inc[cc, u] = _dot((k * consts[u][4]).astype(MXU), v, TN)
        for u in range(hps):
            st = state_sc[u]
            for cc in order:
                st_ref[u, cc] = st
                st = st * consts[u][5] + inc[cc, u]
            state_sc[u] = st
        for cc, u in units:
            q, _, v = operands(cc, u)
            cross = _dot((q * consts[u][3]).astype(MXU), st_ref[u, cc].astype(MXU), NN)
            o_ref[pl.ds(cc * C, C), LANES * u:LANES * (u + 1)] = _dot(a[cc, u].astype(MXU), v, NN) + cross

    qk_spec = pl.BlockSpec((TB, LANES * hps // 2), lambda h, g: (blk(g), h))
    return pl.pallas_call(
        body, name=name, grid=(HEADS // hps, nb),
        in_specs=[pl.BlockSpec(memory_space=pltpu.SMEM), qk_spec, qk_spec,
                  pl.BlockSpec((TB, LANES * hps), lambda h, g: (blk(g), P_VR // (LANES * hps) + h))],
        out_specs=[pl.BlockSpec((TB, LANES * hps), lambda h, g: (blk(g), h)),
                   pl.BlockSpec((hps, cb, LANES, LANES), lambda h, g: (h, blk(g), 0, 0))],
        out_shape=[jax.ShapeDtypeStruct((S, HEADS * LANES), F32), jax.ShapeDtypeStruct((HEADS, S // C, LANES, LANES), F32)],
        scratch_shapes=[pltpu.VMEM((hps, LANES, LANES), F32)],
        compiler_params=pltpu.CompilerParams(dimension_semantics=("parallel", "arbitrary"), vmem_limit_bytes=VMEM_LIMIT),
    )(lg, qt, kt, proj)


def _ret_bwd(name, qt, kt, proj, dret, states, lg, rev):
    S = qt.shape[0]
    C = CHUNK
    TB = _pick(S, (512, 256, 128))
    cb, nb = TB // C, S // TB
    hps = RET_HEADS_PER_STEP
    blk = (lambda g: g) if rev else (lambda g: nb - 1 - g)

    def body(lg_ref, q_ref, k_ref, v_ref, do_ref, st_ref, dq_ref, dk_ref, dv_ref, dlg_ref, ds_sc, acc_cc, acc_q, acc_k, acc_s):
        hg, g = pl.program_id(0), pl.program_id(1)

        @pl.when(g == 0)
        def _():
            ds_sc[...] = jnp.zeros_like(ds_sc)
            acc_cc[...] = jnp.zeros_like(acc_cc)
            acc_q[...] = jnp.zeros_like(acc_q)
            acc_k[...] = jnp.zeros_like(acc_k)
            acc_s[...] = jnp.zeros_like(acc_s)

        lgs = [lg_ref[hg * hps + u] for u in range(hps)]
        consts = [_ret_consts(lgs[u], u, rev) for u in range(hps)]
        order = list(range(cb)) if rev else list(reversed(range(cb)))
        units = [(cc, u) for cc in order for u in range(hps)]

        def operands(cc, u):
            rows = pl.ds(cc * C, C)
            pair = slice(LANES * (u // 2), LANES * (u // 2 + 1))
            head = slice(LANES * u, LANES * (u + 1))
            hm = consts[u][0]
            return q_ref[rows, pair] * hm, k_ref[rows, pair] * hm, v_ref[rows, head].astype(MXU), do_ref[rows, head].astype(MXU)

        a, dp, dqs, inc = {}, {}, {}, {}
        for cc, u in units:
            q, k, vb, dob = operands(cc, u)
            a[cc, u] = _dot(q.astype(MXU), k.astype(MXU), NT)
            dp[cc, u] = _dot(dob, vb, NT)
            dqs[cc, u] = _dot(dob, st_ref[u, cc].astype(MXU), NT)
            inc[cc, u] = _dot((q * consts[u][3]).astype(MXU), dob, TN)
        dsn = {}
        for u in range(hps):
            ds = ds_sc[u]
            for cc in order:
                dsn[cc, u] = ds
                ds = ds * consts[u][5] + inc[cc, u]
            ds_sc[u] = ds
        even = {}
        for cc, u in units:
            hm, din, dpos, qd, kd, cd, eq, ek = consts[u]
            rows, head = pl.ds(cc * C, C), slice(LANES * u, LANES * (u + 1))
            q, k, vb, dob = operands(cc, u)
            qb, kb = q.astype(MXU), k.astype(MXU)
            dsnb = dsn[cc, u].astype(MXU)
            da = (dp[cc, u] * din).astype(MXU)
            vds = _dot(vb, dsnb, NT)
            dq_u = (_dot(da, kb, NN) + dqs[cc, u] * qd) * hm
            dk_u = (_dot(da, qb, TN) + vds * kd) * hm
            if u % 2 == 0:
                even[cc] = (dq_u, dk_u)
            else:
                pair = slice(LANES * (u // 2), LANES * (u // 2 + 1))
                dq_ref[rows, pair] = even[cc][0] + dq_u
                dk_ref[rows, pair] = even[cc][1] + dk_u
            dv_ref[rows, head] = _dot((a[cc, u] * din).astype(MXU), dob, TN) + _dot((k * kd).astype(MXU), dsnb, NN)
            acc_cc[u] += dp[cc, u] * a[cc, u] * din * dpos
            acc_q[u] += dqs[cc, u] * q * (qd * eq)
            acc_k[u] += vds * k * (kd * ek)
            acc_s[u] += dsn[cc, u] * st_ref[u, cc] * (cd * float(C))

        @pl.when(g == nb - 1)
        def _():
            for u in range(hps):
                tot = (jnp.sum(acc_cc[u], keepdims=True) + jnp.sum(acc_q[u], keepdims=True)
                       + jnp.sum(acc_k[u], keepdims=True) + jnp.sum(acc_s[u], keepdims=True))
                dlg_ref[u] = jnp.broadcast_to(tot * lgs[u], (8, LANES))

    full = jax.ShapeDtypeStruct((S, HEADS * LANES), F32)
    hspec = pl.BlockSpec((TB, LANES * hps), lambda h, g: (blk(g), h))
    qk_spec = pl.BlockSpec((TB, LANES * hps // 2), lambda h, g: (blk(g), h))
    return pl.pallas_call(
        body, name=name, grid=(HEADS // hps, nb),
        in_specs=[pl.BlockSpec(memory_space=pltpu.SMEM), qk_spec, qk_spec,
                  pl.BlockSpec((TB, LANES * hps), lambda h, g: (blk(g), P_VR // (LANES * hps) + h)),
                  hspec,
                  pl.BlockSpec((hps, cb, LANES, LANES), lambda h, g: (h, blk(g), 0, 0))],
        out_specs=[qk_spec, qk_spec, hspec, pl.BlockSpec((hps, 8, LANES), lambda h, g: (h, 0, 0))],
        out_shape=[jax.ShapeDtypeStruct(qt.shape, F32), jax.ShapeDtypeStruct(kt.shape, F32), full,
                   jax.ShapeDtypeStruct((HEADS, 8, LANES), F32)],
        scratch_shapes=[pltpu.VMEM((hps, LANES, LANES), F32), pltpu.VMEM((hps, C, C), F32), pltpu.VMEM((hps, C, LANES), F32),
                        pltpu.VMEM((hps, C, LANES), F32), pltpu.VMEM((hps, LANES, LANES), F32)],
        compiler_params=pltpu.CompilerParams(dimension_semantics=("parallel", "arbitrary"), vmem_limit_bytes=VMEM_LIMIT),
    )(lg, qt, kt, proj, dret, states)


def _rope_consts():
    inv16 = THETA ** (-jnp.arange(16, dtype=F32) / 16)
    inv32 = THETA ** (-jnp.arange(32, dtype=F32) / 32)
    lane = np.arange(LANES)
    z48 = jnp.zeros((48,), F32)
    inv_m = jnp.concatenate([inv16, z48, inv16, z48])[None, :]
    sgn_m = jnp.asarray(np.where(lane < 16, -1.0, np.where((lane >= 64) & (lane < 80), 1.0, 0.0)), F32)[None, :]
    inv_r = jnp.concatenate([inv32] * 4)[None, :]
    sgn_r = jnp.asarray(np.where(lane < 64, -1.0, 1.0), F32)[None, :]
    return inv_m, sgn_m, inv_r, sgn_r


FIRST_WEIGHTS = ("w_in", "w_q_b", "w_kv_b")
EARLY_GRADS = ("w_down", "w_gate_up", "w_out", "w_ret_out")
MID_GRADS = ("w_mla_out", "w_in")


def _local_step(x, pos, tgt, gains, W, late_weights=None, grad_hook=None, start_after=None):
    S = x.shape[0]
    ts = _pick(S, (256, 128))
    ts_light = _pick(S, (512, 256, 128))
    R = lambda a, w=None, c=0: (a, ((a.shape[1] if w is None else w), c))
    W_ = lambda a: (a, None)

    win = _win_pad(W["w_in"])
    wq = _wq_pad(W["w_q_b"])
    wk, wv = _wkv_pad(W["w_kv_b"])
    gqn, gkn = _qk_pad(gains["g_qn"]), _qk_pad(gains["g_kn"])
    g_mix, g_q_a, g_kv_a, g_ffn = gains["g_mix"], gains["g_q_a"], gains["g_kv_a"], gains["g_ffn"]
    lg_f = -jnp.exp(gains["ret_decay_fwd"][0])
    lg_b = -jnp.exp(gains["ret_decay_bwd"][0])

    consts = list(_rope_consts())
    cosm, sinm, cosr, sinr = _rowwise("rope_tables", _tables_fn, S, ts_light,[R(pos)] + [W_(c) for c in consts],
                                      [(LANES, F32, LANES, 0)] * 4)

    (h,) = _rowwise("rms_mix", _rmsg_fn, S, ts_light,[R(x), W_(g_mix)], [(D_MODEL, MXU, D_MODEL, 0)])
    proj = _mm("in_proj", h, win, "nn", after=start_after)
    seg = lambda off, w: (proj, (w, off // w))
    mla_ins = [seg(P_CQ, 256), seg(P_CKV, 128), seg(P_KROPE, 128), R(cosm), R(sinm),
               W_(g_q_a), W_(g_kv_a), W_(gqn), W_(gkn), W_(wq), W_(wk), W_(wv)]
    q, k, v = _rowwise("mla_prep", _mla_prep_fn, S, ts, mla_ins, [(HEADS * LANES, MXU, HEADS * LANES, 0)] * 3)
    o_bf, lse = _flash_fwd(q, k, v)
    if late_weights is not None:
        W = {**W, **late_weights(lse)}
    wmla = _wmla_pad(W["w_mla_out"])
    wret, wout, wgu, wdown = W["w_ret_out"], W["w_out"], W["w_gate_up"], W["w_down"]
    y_a = _mm("mla_out", o_bf, wmla, "nn")

    ret_ins = [seg(P_QR, 512), seg(P_KR, 512), R(cosr), R(sinr)]
    qt, kt = _rowwise("ret_prep", _ret_prep_fn, S, ts_light,ret_ins, [(512, F32, 512, 0)] * 2)
    ret_f, st_f = _ret_fwd("ret_fwd_f", qt, kt, proj, lg_f, False)
    ret_b, st_b = _ret_fwd("ret_fwd_b", qt, kt, proj, lg_b, True)
    post_ins = [R(ret_f), R(ret_b), seg(P_GR, 1024)]
    (o_b,) = _rowwise("ret_post", _ret_post_fn, S, ts_light,post_ins, [(1024, MXU, 1024, 0)])
    y_b, merged = _mm_rows("ret_out_merge", o_b, wret, lambda yb, ga, gb, ya: (yb, _merge_fn(ga, gb, ya, yb)),
                           [seg(P_GATES, 1024), (proj, (1024, 1)), R(y_a)], [], [F32, MXU])
    merge_ins = [seg(P_GATES, 1024), (proj, (1024, 1)), R(y_a), R(y_b)]
    def residual_rms(d, xx, g):
        r = d + xx
        return r, _rmsg_fn(r, g)

    x1, h2 = _mm_rows("out_proj_rms_ffn", merged, wout, residual_rms, [x], [g_ffn], [F32, MXU])
    gu, act = _gate_up_swiglu(h2, wgu)

    def residual_loss(d, xx, t):
        dx, rows = _loss_fn(d + xx, t)
        return dx, dx, rows

    dx2, dx2_bf, loss_rows = _mm_rows("down_proj_loss", act, wdown, residual_loss, [x1, tgt], [], [F32, MXU], accs=[(1, D_MODEL)])

    gW = {}
    gW["w_down"] = _mm("d_w_down", act, dx2_bf, "tn")
    dgu = _d_act_swiglu(dx2_bf, wdown, gu)
    gW["w_gate_up"] = _mm("d_w_gate_up", h2, dgu, "tn")
    dh2 = _mm("d_h2", dgu, wgu, "nt")

    def rms_bwd(xx, g, dh, dres):
        _, vjp = jax.vjp(_rmsg_fn, xx, g)
        dx, dg = vjp(dh)
        dx = dx + dres
        return dx, dx, dg

    dx1, dx1_bf, dg_ffn = _rowwise("rms_ffn_bwd", rms_bwd, S, ts_light,[R(x1), W_(g_ffn), R(dh2), R(dx2)],
                                   [(D_MODEL, F32, D_MODEL, 0), (D_MODEL, MXU, D_MODEL, 0)], accs=[(1, D_MODEL)])
    gW["w_out"] = _mm("d_w_out", merged, dx1_bf, "tn")
    def merge_bwd(dm, ga, gb, ya, yb):
        _, vjp = jax.vjp(_merge_fn, ga, gb, ya, yb)
        return vjp(dm)

    dga, dgb, dy_a, dy_b = _mm_rows("d_merged_merge_bwd", dx1_bf, wout, merge_bwd, merge_ins, [], [MXU] * 4, mode="nt")
    gW["w_ret_out"] = _mm("d_w_ret_out", o_b, dy_b, "tn")
    after_early = [] if grad_hook is None else [grad_hook({n: gW[n] for n in EARLY_GRADS})]

    def post_bwd(dob, rf, rb, gr, *_):
        _, vjp = jax.vjp(_ret_post_fn, rf, rb, gr)
        drf, _, dgr = vjp(dob)
        return drf, dgr

    dret, dg_r = _mm_rows("d_o_b_ret_post_bwd", dy_b, wret, post_bwd, post_ins, after_early, [MXU, MXU], mode="nt")
    dq_f, dk_f, dv_f, dlg_f = _ret_bwd("ret_bwd_f", qt, kt, proj, dret, st_f, lg_f, False)
    dq_b, dk_b, dv_b, dlg_b = _ret_bwd("ret_bwd_b", qt, kt, proj, dret, st_b, lg_b, True)

    def ret_prep_bwd(qr, kr, cosr_, sinr_, dqf, dqb, dkf, dkb, dvf, dvb):
        _, vjp = jax.vjp(lambda a, b: _ret_prep_fn(a, b, cosr_, sinr_), qr, kr)
        dqr, dkr = vjp((dqf + dqb, dkf + dkb))
        return dqr, dkr, dvf + dvb

    dq_r, dk_r, dv_r = _rowwise("ret_prep_bwd", ret_prep_bwd, S, ts_light,ret_ins + [R(t) for t in (dq_f, dq_b, dk_f, dk_b, dv_f, dv_b)],
                                [(512, MXU, 512, 0), (512, MXU, 512, 0), (1024, MXU, 1024, 0)])

    gW_mla_p = _mm("d_w_mla_out", o_bf, dy_a, "tn")
    do_bf, delta = _mm_rows("d_o_attn_delta", dy_a, wmla, lambda d, oo, *_: _delta_fn(oo.astype(F32), d), [o_bf], after_early, [MXU, F32], mode="nt")
    dq, dk, dv = _flash_bwd(q, k, v, do_bf, lse, delta)

    def mla_prep_bwd(cq, ckv, kr, cosm_, sinm_, gqa, gkva, gqn_, gkn_, wq_, wk_, wv_, dq_, dk_, dv_):
        f = lambda cq, ckv, kr, gqa, gkva, gqn_, gkn_, wq_, wk_, wv_: _mla_prep_fn(cq, ckv, kr, cosm_, sinm_, gqa, gkva, gqn_, gkn_, wq_, wk_, wv_)
        _, vjp = jax.vjp(f, cq, ckv, kr, gqa, gkva, gqn_, gkn_, wq_.astype(F32), wk_.astype(F32), wv_.astype(F32))
        return vjp((dq_, dk_, dv_))

    mb = _rowwise("mla_prep_bwd", mla_prep_bwd, S, ts, mla_ins + [R(dq), R(dk), R(dv)],
                  [(256, MXU, 256, 0), (128, MXU, 128, 0), (128, MXU, 128, 0)],
                  accs=[(1, 256), (1, 128), (1, LANES), (1, LANES), (256, HEADS * LANES), (128, HEADS * LANES), (128, HEADS * LANES)])
    dc_q, dc_kv, dk_rope, dg_q_a, dg_kv_a, dgqn_p, dgkn_p, dwq_p, dwk_p, dwv_p = mb

    dproj = jnp.concatenate([dga, dgb, dv_r, dg_r, dq_r, dk_r, dc_q, dc_kv, dk_rope], axis=1)
    gW["w_in"] = _win_unpad(_mm("d_w_in", h, dproj, "tn"))
    gW["w_mla_out"] = _wmla_unpad(gW_mla_p)
    after_mid = None if grad_hook is None else grad_hook({n: gW[n] for n in MID_GRADS})
    dh = _mm("d_h", dproj, win, "nt", after=after_mid)
    grad_x, dg_mix = _rowwise("rms_mix_bwd", lambda a, b, c, d, *_: rms_bwd(a, b, c, d)[1:], S, ts_light,
                              [R(x), W_(g_mix), R(dh), R(dx1)] + ([] if after_mid is None else [W_(after_mid)]),
                              [(D_MODEL, F32, D_MODEL, 0)], accs=[(1, D_MODEL)])
    gW["w_q_b"] = _wq_unpad(dwq_p)
    gW["w_kv_b"] = _wkv_unpad(dwk_p, dwv_p)
    gG = {"g_mix": dg_mix, "g_q_a": dg_q_a, "g_kv_a": dg_kv_a, "g_qn": _qk_unpad(dgqn_p),
          "g_kn": _qk_unpad(dgkn_p), "ret_decay_fwd": dlg_f[:, 0, 0][None, :], "ret_decay_bwd": dlg_b[:, 0, 0][None, :],
          "g_ffn": dg_ffn}
    return loss_rows, grad_x, gG, gW


MATS = [("w_in", (1024, 5536), 1), ("w_q_b", (256, 768), 1), ("w_kv_b", (128, 1024), 1), ("w_mla_out", (512, 1024), 1),
        ("w_ret_out", (1024, 1024), 0), ("w_out", (1024, 1024), 0), ("w_gate_up", (1024, 5632), 1), ("w_down", (2816, 1024), 0)]
GAINS = [("g_mix", 1024), ("g_q_a", 256), ("g_kv_a", 128), ("g_qn", 96), ("g_kn", 96), ("ret_decay_fwd", 8), ("ret_decay_bwd", 8),
         ("g_ffn", 1024)]
ORDER = ["g_mix", "w_in", "g_q_a", "w_q_b", "g_kv_a", "w_kv_b", "g_qn", "g_kn", "w_mla_out", "ret_decay_fwd", "ret_decay_bwd",
         "w_ret_out", "w_out", "g_ffn", "w_gate_up", "w_down"]
GAIN_LEN = sum(n for _, n in GAINS)
GAIN_PAD = -(-GAIN_LEN // LANES) * LANES


def _pack_gains(d):
    row = jnp.concatenate([d[n].reshape(1, ln).astype(F32) for n, ln in GAINS], axis=1)
    return jnp.pad(row, ((0, 0), (0, GAIN_PAD - GAIN_LEN)))


def _unpack_gains(row):
    out, off = {}, 0
    for n, ln in GAINS:
        out[n] = row[0, off:off + ln]
        off += ln
    return out


def _unshard(pieces, axis):
    if axis == 0:
        return pieces.reshape((N_DEV * pieces.shape[1], pieces.shape[2]))
    return jnp.concatenate([pieces[p] for p in range(N_DEV)], axis=1)


def _reshard(full, axis):
    if axis == 0:
        return full.reshape((N_DEV, full.shape[0] // N_DEV, full.shape[1]))
    c = full.shape[1] // N_DEV
    return jnp.stack([full[:, c * p:c * (p + 1)] for p in range(N_DEV)])


def _all_gather(shards):
    n = len(shards)

    def body(*refs):
        x_refs, out_refs = refs[:n], refs[n:2 * n]
        send_sems, recv_sems, local_sems = refs[2 * n:]
        x, y, c = lax.axis_index("x"), lax.axis_index("y"), lax.axis_index("c")
        me, sibling = (x, y, c), (x, y, 1 - c)
        chips = [(1 - x, y), (x, 1 - y), (1 - x, 1 - y)]

        def slot(a, px, py, pc):
            return out_refs[a].at[4 * px + 2 * py + pc]

        def copy(a, k, block, to, from_input=False):
            return pltpu.make_async_remote_copy(
                src_ref=x_refs[a] if from_input else slot(a, *block), dst_ref=slot(a, *block),
                send_sem=send_sems.at[a, k], recv_sem=recv_sems.at[a, k], device_id=to, device_id_type=pl.DeviceIdType.MESH)

        mine = [pltpu.make_async_copy(x_refs[a], slot(a, *me), local_sems.at[a]) for a in range(n)]
        first = [copy(a, 0, me, sibling, True) for a in range(n)]
        first += [copy(a, 1 + j, me, (*chip, c), True) for j, chip in enumerate(chips) for a in range(n)]
        for cp in mine + first:
            cp.start()
        passed = []
        for j, chip in enumerate(chips):
            for a in range(n):
                copy(a, 1 + j, (*chip, c), me).wait_recv()
                passed.append(copy(a, 4 + j, (*chip, c), sibling))
                passed[-1].start()
        for a in range(n):
            copy(a, 0, sibling, me).wait_recv()
        for j, chip in enumerate(chips):
            for a in range(n):
                copy(a, 4 + j, (*chip, 1 - c), me).wait_recv()
        for cp in first + passed:
            cp.wait_send()
        for cp in mine:
            cp.wait()

    any_spec = pl.BlockSpec(memory_space=pl.ANY)
    return pl.pallas_call(
        body, name="all_gather_weights", out_shape=[jax.ShapeDtypeStruct((N_DEV,) + s.shape, s.dtype) for s in shards],
        in_specs=[any_spec] * n, out_specs=[any_spec] * n,
        scratch_shapes=[pltpu.SemaphoreType.DMA((n, 7)), pltpu.SemaphoreType.DMA((n, 7)), pltpu.SemaphoreType.DMA((n,))],
    )(*shards)


def _all_to_all(name, pieces):
    srcs, n = pieces, len(pieces)

    def body(*refs):
        in_refs, out_refs = refs[:n], refs[n:2 * n]
        send_sems, recv_sems, local_sems = refs[2 * n:]
        my_id = 4 * lax.axis_index("x") + 2 * lax.axis_index("y") + lax.axis_index("c")
        mine = [pltpu.make_async_copy(in_refs[a].at[my_id], out_refs[a].at[my_id], local_sems.at[a]) for a in range(n)]
        copies = _split_copies(in_refs, out_refs, send_sems, recv_sems, False)
        for cp in mine + copies:
            cp.start()
        for cp in copies:
            cp.wait_recv()
        for cp in copies:
            cp.wait_send()
        for cp in mine:
            cp.wait()

    any_spec = pl.BlockSpec(memory_space=pl.ANY)
    return pl.pallas_call(
        body, name=name, out_shape=[jax.ShapeDtypeStruct(s.shape, s.dtype) for s in srcs],
        in_specs=[any_spec] * n, out_specs=[any_spec] * n,
        scratch_shapes=[pltpu.SemaphoreType.DMA((7 * n,)), pltpu.SemaphoreType.DMA((7 * n,)), pltpu.SemaphoreType.DMA((n,))],
    )(*srcs)


def _flip_peers(x, y, c):
    flips = [(fx, fy, fc) for fx in (0, 1) for fy in (0, 1) for fc in (0, 1)][1:]
    return [(x ^ fx, y ^ fy, c ^ fc) for fx, fy, fc in flips]


def _split_copies(in_refs, land_refs, send_sems, recv_sems, gather):
    x, y, c = lax.axis_index("x"), lax.axis_index("y"), lax.axis_index("c")
    my_id = 4 * x + 2 * y + c
    copies = []
    for kk, p in enumerate(_flip_peers(x, y, c)):
        for a in range(len(in_refs)):
            src = in_refs[a] if gather else in_refs[a].at[4 * p[0] + 2 * p[1] + p[2]]
            copies.append(pltpu.make_async_remote_copy(
                src_ref=src, dst_ref=land_refs[a].at[my_id], send_sem=send_sems.at[a * 7 + kk], recv_sem=recv_sems.at[a * 7 + kk],
                device_id=p, device_id_type=pl.DeviceIdType.MESH))
    return copies


def _exchange_start(name, srcs, gather, after=None):
    n = len(srcs)
    first_out = 2 * n + (0 if after is None else 1)

    def body(*refs):
        for cp in _split_copies(refs[:n], refs[n:2 * n], refs[first_out], refs[first_out + 1], gather):
            cp.start()
        refs[-1][...] = jnp.zeros_like(refs[-1])

    hbm, sem = pl.BlockSpec(memory_space=pltpu.HBM), pl.BlockSpec(memory_space=pltpu.SEMAPHORE)
    land_shapes = [((N_DEV,) + s.shape if gather else s.shape, s.dtype) for s in srcs]
    lands = [pltpu.with_memory_space_constraint(lax.empty(shp, dt), pltpu.HBM) for shp, dt in land_shapes]
    srcs = [pltpu.with_memory_space_constraint(s, pltpu.HBM) for s in srcs]
    res = pl.pallas_call(
        body, name=name,
        out_shape=[pltpu.SemaphoreType.DMA((7 * n,)), pltpu.SemaphoreType.DMA((7 * n,))] + [pltpu.HBM(s.shape, s.dtype) for s in srcs]
        + [pltpu.HBM(shp, dt) for shp, dt in land_shapes] + [jax.ShapeDtypeStruct((8, LANES), F32)],
        in_specs=[hbm] * (2 * n) + ([] if after is None else [pl.BlockSpec(memory_space=pl.ANY)]),
        out_specs=[sem, sem] + [hbm] * (2 * n) + [pl.BlockSpec(memory_space=pltpu.VMEM)],
        input_output_aliases={i: 2 + i for i in range(2 * n)},
        compiler_params=pltpu.CompilerParams(has_side_effects=pltpu.SideEffectType.DATAFLOW_SIDE_EFFECTING),
    )(*srcs, *lands, *([] if after is None else [after]))
    return res[0], res[1], res[2:2 + n], res[2 + n:2 + 2 * n], res[-1]


def _exchange_wait(name, handles, after, gather):
    send_sems, recv_sems, srcs, lands, _ = handles
    n = len(srcs)

    def body(*refs):
        for cp in _split_copies(refs[:n], refs[n:2 * n], refs[2 * n], refs[2 * n + 1], gather):
            cp.wait_send()
            cp.wait_recv()

    hbm, sem = pl.BlockSpec(memory_space=pltpu.HBM), pl.BlockSpec(memory_space=pltpu.SEMAPHORE)
    res = pl.pallas_call(
        body, name=name, out_shape=[pltpu.HBM(t.shape, t.dtype) for t in list(srcs) + list(lands)],
        in_specs=[hbm] * (2 * n) + [sem, sem, pl.BlockSpec(memory_space=pl.ANY)], out_specs=[hbm] * (2 * n),
        input_output_aliases={i: i for i in range(2 * n)},
        compiler_params=pltpu.CompilerParams(has_side_effects=pltpu.SideEffectType.DATAFLOW_SIDE_EFFECTING),
    )(*srcs, *lands, send_sems, recv_sems, after)
    my_id = 4 * lax.axis_index("x") + 2 * lax.axis_index("y") + lax.axis_index("c")
    own = [s if gather else lax.dynamic_index_in_dim(s, my_id, 0, keepdims=False) for s in res[:n]]
    return [lax.dynamic_update_index_in_dim(land, o, my_id, 0) for land, o in zip(res[n:], own)]


def _adamw(name, parts, w, m, v):
    rows, cols = w.shape
    tr = _pick(rows, (128, 64, 32, 16, 8))
    pspec = pl.BlockSpec((N_DEV, tr, cols), lambda i: (0, i, 0))
    rspec = pl.BlockSpec((tr, cols), lambda i: (i, 0))

    def body(p_ref, w_ref, m_ref, v_ref, g_ref, d_ref, m2_ref, v2_ref):
        g, d, m2, v2 = _adamw_fn([p_ref[s] for s in range(N_DEV)], w_ref[...], m_ref[...], v_ref[...])
        g_ref[...], d_ref[...], m2_ref[...], v2_ref[...] = g, d, m2, v2

    return pl.pallas_call(
        body, name=name, grid=(rows // tr,), in_specs=[pspec, rspec, rspec, rspec], out_specs=[rspec] * 4,
        out_shape=[jax.ShapeDtypeStruct((rows, cols), F32)] * 4,
        compiler_params=pltpu.CompilerParams(dimension_semantics=("parallel",), vmem_limit_bytes=VMEM_LIMIT),
    )(parts, w, m, v)


def kernel(x, positions, g_mix, w_in, g_q_a, w_q_b, g_kv_a, w_kv_b, g_qn, g_kn, w_mla_out, ret_decay_fwd, ret_decay_bwd, w_ret_out, w_out, g_ffn, w_gate_up, w_down, loss_target, m_g_mix, m_w_in, m_g_q_a, m_w_q_b, m_g_kv_a, m_w_kv_b, m_g_qn, m_g_kn, m_w_mla_out, m_ret_decay_fwd, m_ret_decay_bwd, m_w_ret_out, m_w_out, m_g_ffn, m_w_gate_up, m_w_down, v_g_mix, v_w_in, v_g_q_a, v_w_q_b, v_g_kv_a, v_w_kv_b, v_g_qn, v_g_kn, v_w_mla_out, v_ret_decay_fwd, v_ret_decay_bwd, v_w_ret_out, v_w_out, v_g_ffn, v_w_gate_up, v_w_down):
    w = dict(g_mix=g_mix, w_in=w_in, g_q_a=g_q_a, w_q_b=w_q_b, g_kv_a=g_kv_a, w_kv_b=w_kv_b, g_qn=g_qn, g_kn=g_kn, w_mla_out=w_mla_out,
             ret_decay_fwd=ret_decay_fwd, ret_decay_bwd=ret_decay_bwd, w_ret_out=w_ret_out, w_out=w_out, g_ffn=g_ffn,
             w_gate_up=w_gate_up, w_down=w_down)
    m = dict(g_mix=m_g_mix, w_in=m_w_in, g_q_a=m_g_q_a, w_q_b=m_w_q_b, g_kv_a=m_g_kv_a, w_kv_b=m_w_kv_b, g_qn=m_g_qn, g_kn=m_g_kn,
             w_mla_out=m_w_mla_out, ret_decay_fwd=m_ret_decay_fwd, ret_decay_bwd=m_ret_decay_bwd, w_ret_out=m_w_ret_out, w_out=m_w_out,
             g_ffn=m_g_ffn, w_gate_up=m_w_gate_up, w_down=m_w_down)
    v = dict(g_mix=v_g_mix, w_in=v_w_in, g_q_a=v_g_q_a, w_q_b=v_w_q_b, g_kv_a=v_g_kv_a, w_kv_b=v_w_kv_b, g_qn=v_g_qn, g_kn=v_g_kn,
             w_mla_out=v_w_mla_out, ret_decay_fwd=v_ret_decay_fwd, ret_decay_bwd=v_ret_decay_bwd, w_ret_out=v_w_ret_out, w_out=v_w_out,
             g_ffn=v_g_ffn, w_gate_up=v_w_gate_up, w_down=v_w_down)
    gains = {n: w[n].reshape(1, ln) for n, ln in GAINS}

    axis_of = {n: axis for n, _, axis in MATS}
    later = [n for n, _, _ in MATS if n not in FIRST_WEIGHTS]
    gathered = _all_gather([w[n].astype(WIRE) for n in FIRST_WEIGHTS])
    W = {n: _unshard(g, axis_of[n]) for n, g in zip(FIRST_WEIGHTS, gathered)}
    later_handles = _exchange_start("gather_later_start", [w[n].astype(WIRE) for n in later], True, after=gathered[0])

    def late_weights(after):
        lands = _exchange_wait("gather_later_wait", later_handles, after, True)
        return {n: _unshard(g, axis_of[n]) for n, g in zip(later, lands)}

    grad_groups = []

    def grad_hook(g):
        names = tuple(g)
        handles = _exchange_start("grads_start_%d" % len(grad_groups), [_reshard(g[n], axis_of[n]).astype(GWIRE) for n in names], False)
        grad_groups.append((names, handles))
        return handles[4]

    S = x.shape[1]
    pos = positions.reshape(S, 1).astype(F32)
    loss_rows, grad_x, gG, gW = _local_step(x.reshape(S, D_MODEL), pos, loss_target.reshape(S, D_MODEL), gains, W, late_weights, grad_hook,
                                            start_after=later_handles[4])
    loss = lax.psum(jnp.sum(loss_rows), ("x", "y", "c"))

    last = [n for n, _, _ in MATS if n not in EARLY_GRADS + MID_GRADS]
    pieces = [_reshard(gW[n], axis_of[n]).astype(GWIRE) for n in last]
    pieces.append(jnp.broadcast_to(_pack_gains(gG)[None], (N_DEV, 1, GAIN_PAD)))
    late_parts = _all_to_all("grads_last", pieces)
    parts = dict(zip(last, late_parts))
    for i, (names, handles) in enumerate(grad_groups):
        parts.update(zip(names, _exchange_wait("grads_wait_%d" % i, handles, late_parts[-1], False)))
    out = [dict() for _ in range(4)]
    for n, _, _ in MATS:
        for o, r in zip(out, _adamw("adamw_" + n, parts[n], w[n], m[n], v[n])):
            o[n] = r
    for o, r in zip(out, _adamw("adamw_gains", late_parts[-1], _pack_gains(w), _pack_gains(m), _pack_gains(v))):
        o.update(_unpack_gains(r))
    return (loss, grad_x.reshape(x.shape), *[o[n] for o in out for n in ORDER])
```

```python
import functools

import numpy as np
import jax
import jax.numpy as jnp
from jax import lax
from jax.experimental import pallas as pl
from jax.experimental.pallas import tpu as pltpu

F32 = jnp.float32
MXU = jnp.bfloat16
WIRE = jnp.bfloat16
GWIRE = jnp.bfloat16

N_DEV = 8
D_MODEL = 1024
HEADS = 8
LANES = 128
Q_RANK, KV_RANK = 256, 128
NOPE, ROPE_M, V_M = 64, 32, 64
QK_M = NOPE + ROPE_M
RQK = 64
CHUNK = 128
FFN = 2816
THETA = 10000.0
EPS = 1e-6
LR, B1, B2, AEPS, WD, STEP = 0.001, 0.9, 0.999, 1e-08, 0.01, 10
VMEM_LIMIT = 56 * 1024 * 1024

NN = ((1,), (0,))
NT = ((1,), (1,))
TN = ((0,), (0,))

P_GATES, P_VR, P_GR, P_QR, P_KR, P_CQ, P_CKV, P_KROPE, P_WIDTH = 0, 2048, 3072, 4096, 4608, 5120, 5376, 5504, 5632
O_CQ, O_CKV, O_KROPE, O_QR, O_KR, O_VR, O_GR, O_GATES = 0, 256, 384, 416, 928, 1440, 2464, 3488


def _dot(a, b, dims):
    return lax.dot_general(a, b, (dims, ((), ())), preferred_element_type=F32)


def _pick(dim, cands):
    for c in cands:
        if dim % c == 0:
            return c
    return dim


def _pairs(t):
    return t.reshape(t.shape[0], 4, 2, 2, 32).transpose(0, 1, 3, 2, 4).reshape(t.shape[0], 512)


def _win_pad(w):
    z = jnp.zeros((w.shape[0], 48), w.dtype)
    kr = w[:, O_KROPE:O_KROPE + 32]
    return jnp.concatenate([w[:, O_GATES:], w[:, O_VR:O_VR + 1024], w[:, O_GR:O_GR + 1024], _pairs(w[:, O_QR:O_QR + 512]),
                            _pairs(w[:, O_KR:O_KR + 512]), w[:, :O_CKV], w[:, O_CKV:O_KROPE], kr[:, :16], z, kr[:, 16:], z], axis=1)


def _win_unpad(g):
    return jnp.concatenate([g[:, P_CQ:P_CQ + 256], g[:, P_CKV:P_CKV + 128], g[:, P_KROPE:P_KROPE + 16], g[:, P_KROPE + 64:P_KROPE + 80],
                            _pairs(g[:, P_QR:P_QR + 512]), _pairs(g[:, P_KR:P_KR + 512]), g[:, P_VR:P_VR + 1024],
                            g[:, P_GR:P_GR + 1024], g[:, P_GATES:P_GATES + 2048]], axis=1)


def _qk_pad(t):
    z = jnp.zeros(t.shape[:-1] + (32,), t.dtype)
    return jnp.concatenate([t[..., 64:80], t[..., 0:48], t[..., 80:96], t[..., 48:64], z], axis=-1)


def _qk_unpad(p):
    return jnp.concatenate([p[..., 16:64], p[..., 80:96], p[..., 0:16], p[..., 64:80]], axis=-1)


def _wq_pad(w):
    return _qk_pad(w.reshape(Q_RANK, HEADS, QK_M)).reshape(Q_RANK, HEADS * LANES)


def _wq_unpad(g):
    return _qk_unpad(g.reshape(Q_RANK, HEADS, LANES)).reshape(Q_RANK, HEADS * QK_M)


def _wkv_pad(w):
    t = w.reshape(KV_RANK, HEADS, NOPE + V_M)
    z = lambda n: jnp.zeros((KV_RANK, HEADS, n), w.dtype)
    wk = jnp.concatenate([z(16), t[..., 0:48], z(16), t[..., 48:64], z(32)], axis=-1)
    wv = jnp.concatenate([t[..., 64:128], z(64)], axis=-1)
    return wk.reshape(KV_RANK, HEADS * LANES), wv.reshape(KV_RANK, HEADS * LANES)


def _wkv_unpad(dwk, dwv):
    k, v = dwk.reshape(KV_RANK, HEADS, LANES), dwv.reshape(KV_RANK, HEADS, LANES)
    return jnp.concatenate([k[..., 16:64], k[..., 80:96], v[..., 0:64]], axis=-1).reshape(KV_RANK, HEADS * (NOPE + V_M))


def _wmla_pad(w):
    t = w.reshape(HEADS, V_M, D_MODEL)
    return jnp.concatenate([t, jnp.zeros_like(t)], axis=1).reshape(HEADS * LANES, D_MODEL)


def _wmla_unpad(g):
    return g.reshape(HEADS, LANES, D_MODEL)[:, :V_M].reshape(HEADS * V_M, D_MODEL)


def _rowwise(name, fn, rows, ts, ins, outs, accs=(), ncol=1):
    n_in, n_out, n_acc = len(ins), len(outs), len(accs)

    def colmap(col):
        if callable(col):
            return lambda i, j: (i, col(j))
        return lambda i, j: (i, col)

    arrays, in_specs = [], []
    for arr, spec in ins:
        arrays.append(arr)
        if spec is None:
            in_specs.append(pl.BlockSpec(arr.shape, functools.partial(lambda i, j, nd: (0,) * nd, nd=arr.ndim)))
        else:
            in_specs.append(pl.BlockSpec((ts, spec[0]), colmap(spec[1])))
    out_shape, out_specs = [], []
    for total, dtype, width, col in outs:
        out_shape.append(jax.ShapeDtypeStruct((rows, total), dtype))
        out_specs.append(pl.BlockSpec((ts, width), colmap(col)))
    for shp in accs:
        out_shape.append(jax.ShapeDtypeStruct(shp, F32))
        out_specs.append(pl.BlockSpec(shp, functools.partial(lambda i, j, nd: (0,) * nd, nd=len(shp))))

    def body(*refs):
        vals = [r[...] for r in refs[:n_in]]
        res = fn(*vals)
        if not isinstance(res, (tuple, list)):
            res = (res,)
        for r, v in zip(refs[n_in:n_in + n_out], res[:n_out]):
            r[...] = v.astype(r.dtype)
        if n_acc:
            first = jnp.logical_and(pl.program_id(0) == 0, pl.program_id(1) == 0)
            for r, v in zip(refs[n_in + n_out:], res[n_out:]):
                @pl.when(first)
                def _(r=r):
                    r[...] = jnp.zeros_like(r)
                r[...] += v.astype(F32)

    res = pl.pallas_call(
        body, name=name, grid=(rows // ts, ncol), in_specs=in_specs, out_specs=out_specs, out_shape=out_shape,
        compiler_params=pltpu.CompilerParams(dimension_semantics=("arbitrary", "arbitrary"), vmem_limit_bytes=VMEM_LIMIT),
    )(*arrays)
    return res


MM_OPERAND_BYTES = 24 * 1024 * 1024


def _mm(name, a, b, mode, add=None, after=None):
    a_halves, b_halves = a.ndim == 3, b.ndim == 3
    assert not a_halves or mode == "nt"
    assert not b_halves or mode == "tn"
    if mode == "nn":
        (M, K), N = a.shape, b.shape[1]
    elif mode == "nt":
        M, K, N = a.shape[-2], a.shape[-1] * (2 if a_halves else 1), b.shape[0]
    else:
        (K, M), N = a.shape, b.shape[-1] * (2 if b_halves else 1)
    tm = _pick(M, (1024, 512, 1408, 256, 128))
    tn = _pick(N // 2 if b_halves else N, (1408, 1024, 512, 256, 128))
    fits = lambda t: 2 * (tm + tn) * t * a.dtype.itemsize <= MM_OPERAND_BYTES
    kdiv = K // 2 if a_halves else K
    tk = next(t for t in (K, 4096, 2816, 2048, 1408, 1024, 512, 256, 128) if kdiv % t == 0 and (fits(t) or t == 128))
    nk = K // tk
    dims = {"nn": NN, "nt": NT, "tn": TN}[mode]
    if a_halves:
        per = kdiv // tk
        a_spec = pl.BlockSpec((None, tm, tk), lambda i, j, k: (k // per, i, k % per))
    else:
        a_spec = pl.BlockSpec((tk, tm), lambda i, j, k: (k, i)) if mode == "tn" else pl.BlockSpec((tm, tk), lambda i, j, k: (i, k))
    if b_halves:
        perj = (N // 2) // tn
        b_spec = pl.BlockSpec((None, tk, tn), lambda i, j, k: (j // perj, k, j % perj))
    else:
        b_spec = pl.BlockSpec((tn, tk), lambda i, j, k: (j, k)) if mode == "nt" else pl.BlockSpec((tk, tn), lambda i, j, k: (k, j))
    o_spec = pl.BlockSpec((tm, tn), lambda i, j, k: (i, j))
    has_add = add is not None

    def body(*refs):
        a_ref, b_ref, o_ref = refs[0], refs[1], refs[-1]
        d = _dot(a_ref[...], b_ref[...], dims)
        first = (d + refs[2][...]) if has_add else d
        if nk == 1:
            o_ref[...] = first
        else:
            k = pl.program_id(2)

            @pl.when(k == 0)
            def _():
                o_ref[...] = first

            @pl.when(k > 0)
            def _():
                o_ref[...] += d

    args = [a, b] + ([add] if has_add else []) + ([] if after is None else [after])
    specs = [a_spec, b_spec] + ([o_spec] if has_add else []) + ([] if after is None else [pl.BlockSpec(memory_space=pl.ANY)])
    return pl.pallas_call(
        body, name=name, grid=(M // tm, N // tn, nk), in_specs=specs, out_specs=o_spec,
        out_shape=jax.ShapeDtypeStruct((M, N), F32),
        compiler_params=pltpu.CompilerParams(dimension_semantics=("parallel", "parallel", "arbitrary"), vmem_limit_bytes=VMEM_LIMIT),
    )(*args)


def _mm_rows(name, a, b, fn, row_ins, whole_ins, outs, accs=(), mode="nn"):
    (M, K), N = a.shape, b.shape[1 if mode == "nn" else 0]
    tm = _pick(M, (512, 256, 128))
    n_in, n_out = 2 + len(row_ins) + len(whole_ins), len(outs)
    windows = [t if isinstance(t, tuple) else (t, (t.shape[1], 0)) for t in row_ins]
    row_ins = [t for t, _ in windows]
    row_specs = [pl.BlockSpec((tm, w), functools.partial(lambda i, col: (i, col), col=col)) for _, (w, col) in windows]

    def body(*refs):
        d = _dot(refs[0][...], refs[1][...], NN if mode == "nn" else NT)
        res = fn(d, *[r[...] for r in refs[2:n_in]])
        for r, v in zip(refs[n_in:n_in + n_out], res[:n_out]):
            r[...] = v.astype(r.dtype)
        for r, v in zip(refs[n_in + n_out:], res[n_out:]):
            @pl.when(pl.program_id(0) == 0)
            def _(r=r):
                r[...] = jnp.zeros_like(r)
            r[...] += v

    row = pl.BlockSpec((tm, N), lambda i: (i, 0))
    whole = lambda t: pl.BlockSpec(t.shape, functools.partial(lambda i, nd: (0,) * nd, nd=t.ndim))
    return pl.pallas_call(
        body, name=name, grid=(M // tm,),
        in_specs=[pl.BlockSpec((tm, K), lambda i: (i, 0)), whole(b)] + row_specs + [whole(t) for t in whole_ins],
        out_specs=[row] * n_out + [pl.BlockSpec(s, functools.partial(lambda i, nd: (0,) * nd, nd=len(s))) for s in accs],
        out_shape=[jax.ShapeDtypeStruct((M, N), dt) for dt in outs] + [jax.ShapeDtypeStruct(s, F32) for s in accs],
        compiler_params=pltpu.CompilerParams(dimension_semantics=("arbitrary",), vmem_limit_bytes=VMEM_LIMIT),
    )(a, b, *row_ins, *whole_ins)


def _ffn_tiles(S):
    return _pick(S, (1024, 512, 256, 128)), _pick(FFN, (1408, 704, 256, 128))


def _gate_up_swiglu(h2, wgu):
    S, K = h2.shape
    tm, tn = _ffn_tiles(S)
    nj = FFN // tn

    def body(a_ref, bg_ref, bu_ref, gu_ref, act_ref):
        a = a_ref[...]
        g, u = _dot(a, bg_ref[...], NN), _dot(a, bu_ref[...], NN)
        gu_ref[0], gu_ref[1] = g.astype(gu_ref.dtype), u.astype(gu_ref.dtype)
        act_ref[...] = _swiglu_fn(g, u).astype(act_ref.dtype)

    return pl.pallas_call(
        body, name="gate_up_swiglu", grid=(S // tm, nj),
        in_specs=[pl.BlockSpec((tm, K), lambda i, j: (i, 0)), pl.BlockSpec((K, tn), lambda i, j: (0, j)),
                  pl.BlockSpec((K, tn), lambda i, j: (0, nj + j))],
        out_specs=[pl.BlockSpec((2, tm, tn), lambda i, j: (0, i, j)), pl.BlockSpec((tm, tn), lambda i, j: (i, j))],
        out_shape=[jax.ShapeDtypeStruct((2, S, FFN), MXU), jax.ShapeDtypeStruct((S, FFN), MXU)],
        compiler_params=pltpu.CompilerParams(dimension_semantics=("parallel", "parallel"), vmem_limit_bytes=VMEM_LIMIT),
    )(h2, wgu, wgu)


def _d_act_swiglu(dx2, wdown, gu):
    S, K = dx2.shape
    tm, tn = _ffn_tiles(S)

    def body(a_ref, b_ref, gu_ref, o_ref):
        dact = _dot(a_ref[...], b_ref[...], NT)
        _, vjp = jax.vjp(_swiglu_fn, gu_ref[0].astype(F32), gu_ref[1].astype(F32))
        dg, du = vjp(dact)
        o_ref[0], o_ref[1] = dg.astype(o_ref.dtype), du.astype(o_ref.dtype)

    stacked = pl.BlockSpec((2, tm, tn), lambda i, j: (0, i, j))
    return pl.pallas_call(
        body, name="d_act_swiglu", grid=(S // tm, FFN // tn),
        in_specs=[pl.BlockSpec((tm, K), lambda i, j: (i, 0)), pl.BlockSpec((tn, K), lambda i, j: (j, 0)), stacked],
        out_specs=stacked, out_shape=jax.ShapeDtypeStruct((2, S, FFN), MXU),
        compiler_params=pltpu.CompilerParams(dimension_semantics=("parallel", "parallel"), vmem_limit_bytes=VMEM_LIMIT),
    )(dx2, wdown, gu)


@jax.custom_vjp
def _swap64(x):
    return pltpu.roll(x, 64, 1)


_swap64.defvjp(lambda x: (_swap64(x), None), lambda _, g: (_swap64(g),))


@jax.custom_vjp
def _mxdot(a, b):
    return _dot(a.astype(MXU), b.astype(MXU), NN)


def _mxdot_bwd(res, g):
    a, b = res
    gb = g.astype(MXU)
    return _dot(gb, b.astype(MXU), NT), _dot(a.astype(MXU), gb, TN)


_mxdot.defvjp(lambda a, b: (_mxdot(a, b), (a, b)), _mxdot_bwd)


def _row_sum(t):
    if t.shape[-1] == LANES:
        return lax.dot_general(t, jnp.ones((LANES, LANES), F32), ((NN), ((), ())), precision=lax.Precision.HIGH,
                               preferred_element_type=F32)
    return jnp.sum(t, axis=-1, keepdims=True)


@functools.partial(jax.custom_vjp, nondiff_argnums=(1,))
def _unit_rms(x, n):
    return x * lax.rsqrt(_row_sum(x * x) * (1.0 / n) + EPS)


def _unit_rms_fwd(x, n):
    r = lax.rsqrt(_row_sum(x * x) * (1.0 / n) + EPS)
    y = x * r
    return y, (y, r)


def _unit_rms_bwd(n, res, g):
    y, r = res
    return (r * (g - y * (_row_sum(g * y) * (1.0 / n))),)


_unit_rms.defvjp(_unit_rms_fwd, _unit_rms_bwd)


def _rms(x):
    return _unit_rms(x, x.shape[-1])


def _rmsg_fn(x, g):
    return _rms(x) * g


def _silu(x):
    return x * jax.nn.sigmoid(x)


def _tables_fn(pos, inv_m, sgn_m, inv_r, sgn_r):
    am, ar = pos * inv_m, pos * inv_r
    return jnp.cos(am), jnp.sin(am) * sgn_m, jnp.cos(ar), jnp.sin(ar) * sgn_r


def _head_blocks(t):
    return [t[:, LANES * h:LANES * (h + 1)] for h in range(t.shape[1] // LANES)]


def _mla_prep_fn(cq, ckv, kr, cosm, sinm, gqa, gkva, gqn, gkn, wq, wk, wv):
    cqn = _rms(cq) * gqa
    ckvn = _rms(ckv) * gkva
    q_raw = _mxdot(cqn, wq)
    k_raw = _mxdot(ckvn, wk)
    lane = lax.broadcasted_iota(jnp.int32, (1, HEADS * LANES), 1)
    v = _mxdot(ckvn, wv) + (lane % LANES == V_M).astype(F32)

    def norm_rope(blocks, g, extra):
        outs = []
        for b in blocks:
            if extra is not None:
                b = b + extra
            n = _unit_rms(b, QK_M) * g
            outs.append(n * cosm + _swap64(n) * sinm)
        return jnp.concatenate(outs, axis=1)

    q = norm_rope(_head_blocks(q_raw), gqn, None)
    k = norm_rope(_head_blocks(k_raw), gkn, kr)
    return q, k, v


def _ret_prep_fn(qr, kr, cosr, sinr):
    def rope(t, scale):
        return jnp.concatenate([(b * cosr + _swap64(b) * sinr) * scale for b in _head_blocks(t)], axis=1)
    return rope(qr, 1.0), rope(kr, RQK ** -0.5)


def _ret_post_fn(rf, rb, gr):
    ret = rf + rb
    outs = []
    for b, g in zip(_head_blocks(ret), _head_blocks(gr)):
        outs.append(_silu(g) * _rms(b))
    return jnp.concatenate(outs, axis=1)


def _merge_fn(ga, gb, ya, yb):
    return jax.nn.sigmoid(ga) * ya + jax.nn.sigmoid(gb) * yb


def _swiglu_fn(gate, up):
    return _silu(gate) * up


def _loss_fn(x2, tgt):
    d = x2 - tgt
    return d * (1.0 / D_MODEL), 0.5 * jnp.sum(d * d, axis=0, keepdims=True) * (1.0 / D_MODEL)


def _adamw_fn(parts, w, m, v):
    g = parts[0].astype(F32)
    for p in range(1, N_DEV):
        g = g + parts[p].astype(F32)
    m2 = B1 * m + (1.0 - B1) * g
    v2 = B2 * v + (1.0 - B2) * jnp.square(g)
    m_hat = m2 / (1.0 - B1 ** STEP)
    v_hat = v2 / (1.0 - B2 ** STEP)
    delta = -LR * (m_hat / (jnp.sqrt(v_hat) + AEPS) + WD * w)
    return g, delta, m2, v2


SCALE = QK_M ** -0.5
LOG2E = 1.4426950408889634
FLASH_ROWS = 32


def _flash_fwd(q, k, v):
    S = q.shape[0]
    tk = _pick(S, (512, 256, 128))
    tq = _pick(S, (1024, 512, 256, 128))
    ncb = tk // LANES
    nkv = S // tk
    assert nkv % 2 == 0, "kv tiles are processed in pairs"
    mrows = 64
    c = SCALE * LOG2E

    def body(q_ref, k_ref, v_ref, o_ref, lse_ref, s_a, p_a, s_b, p_b, m_sc, a_sc, acc_sc):
        m_sc[...] = jnp.full_like(m_sc, -jnp.inf)
        acc_sc[...] = jnp.zeros_like(acc_sc)
        qb = q_ref[...]

        def scores(j, s_buf):
            s_buf[...] = _dot(qb, k_ref[pl.ds(pl.multiple_of(j * tk, tk), tk), :], NT)

        def stage(j, s_buf, p_buf, s_next):
            scores(jnp.minimum(j + 1, nkv - 1), s_next)
            for r in range(tq // mrows):
                rows = slice(r * mrows, (r + 1) * mrows)
                cols = [s_buf[rows, LANES * cb:LANES * (cb + 1)] for cb in range(ncb)]
                m_prev = m_sc[rows, :]
                row_max = jnp.max(functools.reduce(jnp.maximum, cols), axis=-1, keepdims=True)
                m_new = jnp.maximum(m_prev, jnp.broadcast_to(row_max, (mrows, LANES)))
                a_sc[rows, :] = jnp.exp2((m_prev - m_new) * c)
                m_sc[rows, :] = m_new
                for cb in range(ncb):
                    p_buf[rows, LANES * cb:LANES * (cb + 1)] = jnp.exp2((cols[cb] - m_new) * c).astype(p_buf.dtype)
            acc_sc[...] = a_sc[...] * acc_sc[...] + _dot(p_buf[...], v_ref[pl.ds(pl.multiple_of(j * tk, tk), tk), :], NN)

        scores(0, s_a)

        def pair_step(t, carry):
            stage(2 * t, s_a, p_a, s_b)
            stage(2 * t + 1, s_b, p_b, s_a)
            return carry

        lax.fori_loop(0, nkv // 2, pair_step, 0, unroll=8)
        acc = acc_sc[...]
        lane = lax.broadcasted_iota(jnp.int32, (1, LANES), 1)
        l = jnp.sum(jnp.where(lane == V_M, acc, 0.0), axis=-1, keepdims=True)
        o_ref[...] = (acc / l).astype(o_ref.dtype)
        lse_ref[...] = m_sc[...] * c + jnp.log2(jnp.broadcast_to(l, (tq, LANES)))

    qspec = pl.BlockSpec((tq, LANES), lambda h, i: (i, h))
    kspec = pl.BlockSpec((S, LANES), lambda h, i: (0, h))
    return pl.pallas_call(
        body, name="flash_fwd", grid=(HEADS, S // tq), in_specs=[qspec, kspec, kspec], out_specs=[qspec, qspec],
        out_shape=[jax.ShapeDtypeStruct((S, HEADS * LANES), MXU), jax.ShapeDtypeStruct((S, HEADS * LANES), F32)],
        scratch_shapes=[pltpu.VMEM((tq, tk), F32), pltpu.VMEM((tq, tk), MXU)] * 2 + [pltpu.VMEM((tq, LANES), F32)] * 3,
        compiler_params=pltpu.CompilerParams(dimension_semantics=("parallel", "arbitrary"), vmem_limit_bytes=VMEM_LIMIT),
    )(q, k, v)


def _delta_fn(o, do):
    outs = [jnp.broadcast_to(jnp.sum(a * b, axis=-1, keepdims=True), a.shape) for a, b in zip(_head_blocks(o), _head_blocks(do))]
    return do, jnp.concatenate(outs, axis=1)


def _flash_bwd(q, k, v, do, lse, delta):
    S = q.shape[0]
    tq = tk = _pick(S, (512, 256, 128))
    ncb = tk // LANES
    c = SCALE * LOG2E

    nq = S // tq
    assert nq % 2 == 0, "q tiles are processed in pairs"

    def body(q_ref, k_ref, v_ref, do_ref, lse_ref, dl_ref, dq_ref, dk_ref, dv_ref, s_a, dp_a, p_a, ds_a, s_b, dp_b, p_b, ds_b):
        @pl.when(pl.program_id(1) == 0)
        def _():
            dq_ref[...] = jnp.zeros_like(dq_ref)

        dk_ref[...] = jnp.zeros_like(dk_ref)
        dv_ref[...] = jnp.zeros_like(dv_ref)
        kb, vb = k_ref[...], v_ref[...]

        def scores(i, s_buf, dp_buf):
            q_rows = pl.ds(pl.multiple_of(i * tq, tq), tq)
            s_buf[...] = _dot(q_ref[q_rows, :], kb, NT)
            dp_buf[...] = _dot(do_ref[q_rows, :], vb, NT)

        def stage(i, s_buf, dp_buf, p_buf, ds_buf, s_next, dp_next):
            scores(jnp.minimum(i + 1, nq - 1), s_next, dp_next)
            for r in range(tq // FLASH_ROWS):
                rows = slice(r * FLASH_ROWS, (r + 1) * FLASH_ROWS)
                grows = pl.ds(pl.multiple_of(i * tq + r * FLASH_ROWS, FLASH_ROWS), FLASH_ROWS)
                lse_b, dl_b = lse_ref[grows, :], dl_ref[grows, :]
                for cb in range(ncb):
                    sl = slice(LANES * cb, LANES * (cb + 1))
                    p = jnp.exp2(s_buf[rows, sl] * c - lse_b)
                    p_buf[rows, sl] = p.astype(p_buf.dtype)
                    ds_buf[rows, sl] = (p * (dp_buf[rows, sl] - dl_b) * SCALE).astype(ds_buf.dtype)
            q_rows = pl.ds(pl.multiple_of(i * tq, tq), tq)
            dv_ref[...] += _dot(p_buf[...], do_ref[q_rows, :], TN)
            dk_ref[...] += _dot(ds_buf[...], q_ref[q_rows, :], TN)
            dq_ref[q_rows, :] += _dot(ds_buf[...], kb, NN)

        scores(0, s_a, dp_a)

        def pair_step(t, carry):
            stage(2 * t, s_a, dp_a, p_a, ds_a, s_b, dp_b)
            stage(2 * t + 1, s_b, dp_b, p_b, ds_b, s_a, dp_a)
            return carry

        lax.fori_loop(0, nq // 2, pair_step, 0, unroll=8)

    hspec = pl.BlockSpec((S, LANES), lambda h, j: (0, h))
    kspec = pl.BlockSpec((tk, LANES), lambda h, j: (j, h))
    full = jax.ShapeDtypeStruct((S, HEADS * LANES), F32)
    tile_bufs = [pltpu.VMEM((tq, tk), F32), pltpu.VMEM((tq, tk), F32), pltpu.VMEM((tq, tk), MXU), pltpu.VMEM((tq, tk), MXU)]
    return pl.pallas_call(
        body, name="flash_bwd", grid=(HEADS, S // tk), in_specs=[hspec, kspec, kspec, hspec, hspec, hspec],
        out_specs=[hspec, kspec, kspec], out_shape=[full, full, full],
        scratch_shapes=tile_bufs + tile_bufs,
        compiler_params=pltpu.CompilerParams(dimension_semantics=("parallel", "arbitrary"), vmem_limit_bytes=VMEM_LIMIT),
    )(q, k, v, do, lse, delta)


def _ret_consts(lgh, head, rev):
    C = CHUNK
    lane = lax.broadcasted_iota(jnp.int32, (1, LANES), 1)
    hm = ((lane // 32) % 2 == head % 2).astype(F32)
    r = lax.broadcasted_iota(jnp.int32, (C, C), 0)
    c = lax.broadcasted_iota(jnp.int32, (C, C), 1)
    diff = ((c - r) if rev else (r - c)).astype(F32)
    mask = (diff > 0) if rev else (diff >= 0)
    dpos = jnp.maximum(diff, 0.0)
    din = jnp.where(mask, jnp.exp(lgh * dpos), 0.0)
    idx = lax.broadcasted_iota(jnp.int32, (C, 1), 0).astype(F32)
    eq = (C - idx) if rev else (idx + 1.0)
    ek = idx if rev else (C - 1.0 - idx)
    qd, kd = jnp.exp(lgh * eq), jnp.exp(lgh * ek)
    cd = jnp.exp(lgh * jnp.full((1, 1), float(C), F32))
    return hm, din, dpos, qd, kd, cd, eq, ek


RET_HEADS_PER_STEP = 8


def _ret_fwd(name, qt, kt, proj, lg, rev):
    S = qt.shape[0]
    C = CHUNK
    TB = _pick(S, (512, 256, 128))
    cb, nb = TB // C, S // TB
    hps = RET_HEADS_PER_STEP
    blk = (lambda g: nb - 1 - g) if rev else (lambda g: g)

    def body(lg_ref, q_ref, k_ref, v_ref, o_ref, st_ref, state_sc):
        hg, g = pl.program_id(0), pl.program_id(1)

        @pl.when(g == 0)
        def _():
            state_sc[...] = jnp.zeros_like(state_sc)

        consts = [_ret_consts(lg_ref[hg * hps + u], u, rev) for u in range(hps)]
        order = list(reversed(range(cb))) if rev else list(range(cb))
        units = [(cc, u) for cc in order for u in range(hps)]

        def operands(cc, u):
            rows = pl.ds(cc * C, C)
            pair = slice(LANES * (u // 2), LANES * (u // 2 + 1))
            hm = consts[u][0]
            return q_ref[rows, pair] * hm, k_ref[rows, pair] * hm, v_ref[rows, LANES * u:LANES * (u + 1)].astype(MXU)

        a, inc = {}, {}
        for cc, u in units:
            q, k, v = operands(cc, u)
            a[cc, u] = _dot(q.astype(MXU), k.astype(MXU), NT) * consts[u][1]
            inc[cc, u] = _dot((k * consts[u][4]).astype(MXU), v, TN)
        for u in range(hps):
            st = state_sc[u]
            for cc in order:
                st_ref[u, cc] = st
                st = st * consts[u][5] + inc[cc, u]
            state_sc[u] = st
        for cc, u in units:
            q, _, v = operands(cc, u)
            cross = _dot((q * consts[u][3]).astype(MXU), st_ref[u, cc].astype(MXU), NN)
            o_ref[pl.ds(cc * C, C), LANES * u:LANES * (u + 1)] = _dot(a[cc, u].astype(MXU), v, NN) + cross

    qk_spec = pl.BlockSpec((TB, LANES * hps // 2), lambda h, g: (blk(g), h))
    return pl.pallas_call(
        body, name=name, grid=(HEADS // hps, nb),
        in_specs=[pl.BlockSpec(memory_space=pltpu.SMEM), qk_spec, qk_spec,
                  pl.BlockSpec((TB, LANES * hps), lambda h, g: (blk(g), P_VR // (LANES * hps) + h))],
        out_specs=[pl.BlockSpec((TB, LANES * hps), lambda h, g: (blk(g), h)),
                   pl.BlockSpec((hps, cb, LANES, LANES), lambda h, g: (h, blk(g), 0, 0))],
        out_shape=[jax.ShapeDtypeStruct((S, HEADS * LANES), F32), jax.ShapeDtypeStruct((HEADS, S // C, LANES, LANES), F32)],
        scratch_shapes=[pltpu.VMEM((hps, LANES, LANES), F32)],
        compiler_params=pltpu.CompilerParams(dimension_semantics=("parallel", "arbitrary"), vmem_limit_bytes=VMEM_LIMIT),
    )(lg, qt, kt, proj)


def _ret_bwd(name, qt, kt, proj, dret, states, lg, rev):
    S = qt.shape[0]
    C = CHUNK
    TB = _pick(S, (512, 256, 128))
    cb, nb = TB // C, S // TB
    hps = RET_HEADS_PER_STEP
    blk = (lambda g: g) if rev else (lambda g: nb - 1 - g)

    def body(lg_ref, q_ref, k_ref, v_ref, do_ref, st_ref, dq_ref, dk_ref, dv_ref, dlg_ref, ds_sc, acc_cc, acc_q, acc_k, acc_s):
        hg, g = pl.program_id(0), pl.program_id(1)

        @pl.when(g == 0)
        def _():
            ds_sc[...] = jnp.zeros_like(ds_sc)
            acc_cc[...] = jnp.zeros_like(acc_cc)
            acc_q[...] = jnp.zeros_like(acc_q)
            acc_k[...] = jnp.zeros_like(acc_k)
            acc_s[...] = jnp.zeros_like(acc_s)

        lgs = [lg_ref[hg * hps + u] for u in range(hps)]
        consts = [_ret_consts(lgs[u], u, rev) for u in range(hps)]
        order = list(range(cb)) if rev else list(reversed(range(cb)))
        units = [(cc, u) for cc in order for u in range(hps)]

        def operands(cc, u):
            rows = pl.ds(cc * C, C)
            pair = slice(LANES * (u // 2), LANES * (u // 2 + 1))
            head = slice(LANES * u, LANES * (u + 1))
            hm = consts[u][0]
            return q_ref[rows, pair] * hm, k_ref[rows, pair] * hm, v_ref[rows, head].astype(MXU), do_ref[rows, head].astype(MXU)

        a, dp, dqs, inc = {}, {}, {}, {}
        for cc, u in units:
            q, k, vb, dob = operands(cc, u)
            a[cc, u] = _dot(q.astype(MXU), k.astype(MXU), NT)
            dp[cc, u] = _dot(dob, vb, NT)
            dqs[cc, u] = _dot(dob, st_ref[u, cc].astype(MXU), NT)
            inc[cc, u] = _dot((q * consts[u][3]).astype(MXU), dob, TN)
        dsn = {}
        for u in range(hps):
            ds = ds_sc[u]
            for cc in order:
                dsn[cc, u] = ds
                ds = ds * consts[u][5] + inc[cc, u]
            ds_sc[u] = ds
        even = {}
        for cc, u in units:
            hm, din, dpos, qd, kd, cd, eq, ek = consts[u]
            rows, head = pl.ds(cc * C, C), slice(LANES * u, LANES * (u + 1))
            q, k, vb, dob = operands(cc, u)
            qb, kb = q.astype(MXU), k.astype(MXU)
            dsnb = dsn[cc, u].astype(MXU)
            da = (dp[cc, u] * din).astype(MXU)
            vds = _dot(vb, dsnb, NT)
            dq_u = (_dot(da, kb, NN) + dqs[cc, u] * qd) * hm
            dk_u = (_dot(da, qb, TN) + vds * kd) * hm
            if u % 2 == 0:
                even[cc] = (dq_u, dk_u)
            else:
                pair = slice(LANES * (u // 2), LANES * (u // 2 + 1))
                dq_ref[rows, pair] = even[cc][0] + dq_u
                dk_ref[rows, pair] = even[cc][1] + dk_u
            dv_ref[rows, head] = _dot((a[cc, u] * din).astype(MXU), dob, TN) + _dot((k * kd).astype(MXU), dsnb, NN)
            acc_cc[u] += dp[cc, u] * a[cc, u] * din * dpos
            acc_q[u] += dqs[cc, u] * q * (qd * eq)
            acc_k[u] += vds * k * (kd * ek)
            acc_s[u] += dsn[cc, u] * st_ref[u, cc] * (cd * float(C))

        @pl.when(g == nb - 1)
        def _():
            for u in range(hps):
                tot = (jnp.sum(acc_cc[u], keepdims=True) + jnp.sum(acc_q[u], keepdims=True)
                       + jnp.sum(acc_k[u], keepdims=True) + jnp.sum(acc_s[u], keepdims=True))
                dlg_ref[u] = jnp.broadcast_to(tot * lgs[u], (8, LANES))

    full = jax.ShapeDtypeStruct((S, HEADS * LANES), F32)
    hspec = pl.BlockSpec((TB, LANES * hps), lambda h, g: (blk(g), h))
    qk_spec = pl.BlockSpec((TB, LANES * hps // 2), lambda h, g: (blk(g), h))
    return pl.pallas_call(
        body, name=name, grid=(HEADS // hps, nb),
        in_specs=[pl.BlockSpec(memory_space=pltpu.SMEM), qk_spec, qk_spec,
                  pl.BlockSpec((TB, LANES * hps), lambda h, g: (blk(g), P_VR // (LANES * hps) + h)),
                  hspec,
                  pl.BlockSpec((hps, cb, LANES, LANES), lambda h, g: (h, blk(g), 0, 0))],
        out_specs=[qk_spec, qk_spec, hspec, pl.BlockSpec((hps, 8, LANES), lambda h, g: (h, 0, 0))],
        out_shape=[jax.ShapeDtypeStruct(qt.shape, F32), jax.ShapeDtypeStruct(kt.shape, F32), full,
                   jax.ShapeDtypeStruct((HEADS, 8, LANES), F32)],
        scratch_shapes=[pltpu.VMEM((hps, LANES, LANES), F32), pltpu.VMEM((hps, C, C), F32), pltpu.VMEM((hps, C, LANES), F32),
                        pltpu.VMEM((hps, C, LANES), F32), pltpu.VMEM((hps, LANES, LANES), F32)],
        compiler_params=pltpu.CompilerParams(dimension_semantics=("parallel", "arbitrary"), vmem_limit_bytes=VMEM_LIMIT),
    )(lg, qt, kt, proj, dret, states)


def _rope_consts():
    inv16 = THETA ** (-jnp.arange(16, dtype=F32) / 16)
    inv32 = THETA ** (-jnp.arange(32, dtype=F32) / 32)
    lane = np.arange(LANES)
    z48 = jnp.zeros((48,), F32)
    inv_m = jnp.concatenate([inv16, z48, inv16, z48])[None, :]
    sgn_m = jnp.asarray(np.where(lane < 16, -1.0, np.where((lane >= 64) & (lane < 80), 1.0, 0.0)), F32)[None, :]
    inv_r = jnp.concatenate([inv32] * 4)[None, :]
    sgn_r = jnp.asarray(np.where(lane < 64, -1.0, 1.0), F32)[None, :]
    return inv_m, sgn_m, inv_r, sgn_r


FIRST_WEIGHTS = ("w_in", "w_q_b", "w_kv_b")
EARLY_GRADS = ("w_down", "w_gate_up", "w_out", "w_ret_out")
MID_GRADS = ("w_mla_out", "w_in")


def _local_step(x, pos, tgt, gains, W, late_weights=None, grad_hook=None, start_after=None):
    S = x.shape[0]
    ts = _pick(S, (256, 128))
    ts_light = _pick(S, (512, 256, 128))
    R = lambda a, w=None, c=0: (a, ((a.shape[1] if w is None else w), c))
    W_ = lambda a: (a, None)

    win = _win_pad(W["w_in"])
    wq = _wq_pad(W["w_q_b"])
    wk, wv = _wkv_pad(W["w_kv_b"])
    gqn, gkn = _qk_pad(gains["g_qn"]), _qk_pad(gains["g_kn"])
    g_mix, g_q_a, g_kv_a, g_ffn = gains["g_mix"], gains["g_q_a"], gains["g_kv_a"], gains["g_ffn"]
    lg_f = -jnp.exp(gains["ret_decay_fwd"][0])
    lg_b = -jnp.exp(gains["ret_decay_bwd"][0])

    consts = list(_rope_consts())
    cosm, sinm, cosr, sinr = _rowwise("rope_tables", _tables_fn, S, ts_light,[R(pos)] + [W_(c) for c in consts],
                                      [(LANES, F32, LANES, 0)] * 4)

    (h,) = _rowwise("rms_mix", _rmsg_fn, S, ts_light,[R(x), W_(g_mix)], [(D_MODEL, MXU, D_MODEL, 0)])
    proj = _mm("in_proj", h, win, "nn", after=start_after)
    seg = lambda off, w: (proj, (w, off // w))
    mla_ins = [seg(P_CQ, 256), seg(P_CKV, 128), seg(P_KROPE, 128), R(cosm), R(sinm),
               W_(g_q_a), W_(g_kv_a), W_(gqn), W_(gkn), W_(wq), W_(wk), W_(wv)]
    q, k, v = _rowwise("mla_prep", _mla_prep_fn, S, ts, mla_ins, [(HEADS * LANES, MXU, HEADS * LANES, 0)] * 3)
    o_bf, lse = _flash_fwd(q, k, v)
    if late_weights is not None:
        W = {**W, **late_weights(lse)}
    wmla = _wmla_pad(W["w_mla_out"])
    wret, wout, wgu, wdown = W["w_ret_out"], W["w_out"], W["w_gate_up"], W["w_down"]
    y_a = _mm("mla_out", o_bf, wmla, "nn")

    ret_ins = [seg(P_QR, 512), seg(P_KR, 512), R(cosr), R(sinr)]
    qt, kt = _rowwise("ret_prep", _ret_prep_fn, S, ts_light,ret_ins, [(512, F32, 512, 0)] * 2)
    ret_f, st_f = _ret_fwd("ret_fwd_f", qt, kt, proj, lg_f, False)
    ret_b, st_b = _ret_fwd("ret_fwd_b", qt, kt, proj, lg_b, True)
    post_ins = [R(ret_f), R(ret_b), seg(P_GR, 1024)]
    (o_b,) = _rowwise("ret_post", _ret_post_fn, S, ts_light,post_ins, [(1024, MXU, 1024, 0)])
    y_b, merged = _mm_rows("ret_out_merge", o_b, wret, lambda yb, ga, gb, ya: (yb, _merge_fn(ga, gb, ya, yb)),
                           [seg(P_GATES, 1024), (proj, (1024, 1)), R(y_a)], [], [F32, MXU])
    merge_ins = [seg(P_GATES, 1024), (proj, (1024, 1)), R(y_a), R(y_b)]
    def residual_rms(d, xx, g):
        r = d + xx
        return r, _rmsg_fn(r, g)

    x1, h2 = _mm_rows("out_proj_rms_ffn", merged, wout, residual_rms, [x], [g_ffn], [F32, MXU])
    gu, act = _gate_up_swiglu(h2, wgu)

    def residual_loss(d, xx, t):
        dx, rows = _loss_fn(d + xx, t)
        return dx, dx, rows

    dx2, dx2_bf, loss_rows = _mm_rows("down_proj_loss", act, wdown, residual_loss, [x1, tgt], [], [F32, MXU], accs=[(1, D_MODEL)])

    gW = {}
    gW["w_down"] = _mm("d_w_down", act, dx2_bf, "tn")
    dgu = _d_act_swiglu(dx2_bf, wdown, gu)
    gW["w_gate_up"] = _mm("d_w_gate_up", h2, dgu, "tn")
    dh2 = _mm("d_h2", dgu, wgu, "nt")

    def rms_bwd(xx, g, dh, dres):
        _, vjp = jax.vjp(_rmsg_fn, xx, g)
        dx, dg = vjp(dh)
        dx = dx + dres
        return dx, dx, dg

    dx1, dx1_bf, dg_ffn = _rowwise("rms_ffn_bwd", rms_bwd, S, ts_light,[R(x1), W_(g_ffn), R(dh2), R(dx2)],
                                   [(D_MODEL, F32, D_MODEL, 0), (D_MODEL, MXU, D_MODEL, 0)], accs=[(1, D_MODEL)])
    gW["w_out"] = _mm("d_w_out", merged, dx1_bf, "tn")
    def merge_bwd(dm, ga, gb, ya, yb):
        _, vjp = jax.vjp(_merge_fn, ga, gb, ya, yb)
        return vjp(dm)

    dga, dgb, dy_a, dy_b = _mm_rows("d_merged_merge_bwd", dx1_bf, wout, merge_bwd, merge_ins, [], [MXU] * 4, mode="nt")
    gW["w_ret_out"] = _mm("d_w_ret_out", o_b, dy_b, "tn")
    after_early = [] if grad_hook is None else [grad_hook({n: gW[n] for n in EARLY_GRADS})]

    def post_bwd(dob, rf, rb, gr, *_):
        _, vjp = jax.vjp(_ret_post_fn, rf, rb, gr)
        drf, _, dgr = vjp(dob)
        return drf, dgr

    dret, dg_r = _mm_rows("d_o_b_ret_post_bwd", dy_b, wret, post_bwd, post_ins, after_early, [MXU, MXU], mode="nt")
    dq_f, dk_f, dv_f, dlg_f = _ret_bwd("ret_bwd_f", qt, kt, proj, dret, st_f, lg_f, False)
    dq_b, dk_b, dv_b, dlg_b = _ret_bwd("ret_bwd_b", qt, kt, proj, dret, st_b, lg_b, True)

    def ret_prep_bwd(qr, kr, cosr_, sinr_, dqf, dqb, dkf, dkb, dvf, dvb):
        _, vjp = jax.vjp(lambda a, b: _ret_prep_fn(a, b, cosr_, sinr_), qr, kr)
        dqr, dkr = vjp((dqf + dqb, dkf + dkb))
        return dqr, dkr, dvf + dvb

    dq_r, dk_r, dv_r = _rowwise("ret_prep_bwd", ret_prep_bwd, S, ts_light,ret_ins + [R(t) for t in (dq_f, dq_b, dk_f, dk_b, dv_f, dv_b)],
                                [(512, MXU, 512, 0), (512, MXU, 512, 0), (1024, MXU, 1024, 0)])

    gW_mla_p = _mm("d_w_mla_out", o_bf, dy_a, "tn")
    do_bf, delta = _mm_rows("d_o_attn_delta", dy_a, wmla, lambda d, oo, *_: _delta_fn(oo.astype(F32), d), [o_bf], after_early, [MXU, F32], mode="nt")
    dq, dk, dv = _flash_bwd(q, k, v, do_bf, lse, delta)

    def mla_prep_bwd(cq, ckv, kr, cosm_, sinm_, gqa, gkva, gqn_, gkn_, wq_, wk_, wv_, dq_, dk_, dv_):
        f = lambda cq, ckv, kr, gqa, gkva, gqn_, gkn_, wq_, wk_, wv_: _mla_prep_fn(cq, ckv, kr, cosm_, sinm_, gqa, gkva, gqn_, gkn_, wq_, wk_, wv_)
        _, vjp = jax.vjp(f, cq, ckv, kr, gqa, gkva, gqn_, gkn_, wq_.astype(F32), wk_.astype(F32), wv_.astype(F32))
        return vjp((dq_, dk_, dv_))

    mb = _rowwise("mla_prep_bwd", mla_prep_bwd, S, ts, mla_ins + [R(dq), R(dk), R(dv)],
                  [(256, MXU, 256, 0), (128, MXU, 128, 0), (128, MXU, 128, 0)],
                  accs=[(1, 256), (1, 128), (1, LANES), (1, LANES), (256, HEADS * LANES), (128, HEADS * LANES), (128, HEADS * LANES)])
    dc_q, dc_kv, dk_rope, dg_q_a, dg_kv_a, dgqn_p, dgkn_p, dwq_p, dwk_p, dwv_p = mb

    dproj = jnp.concatenate([dga, dgb, dv_r, dg_r, dq_r, dk_r, dc_q, dc_kv, dk_rope], axis=1)
    gW["w_in"] = _win_unpad(_mm("d_w_in", h, dproj, "tn"))
    gW["w_mla_out"] = _wmla_unpad(gW_mla_p)
    after_mid = None if grad_hook is None else grad_hook({n: gW[n] for n in MID_GRADS})
    dh = _mm("d_h", dproj, win, "nt", after=after_mid)
    grad_x, dg_mix = _rowwise("rms_mix_bwd", lambda a, b, c, d, *_: rms_bwd(a, b, c, d)[1:], S, ts_light,
                              [R(x), W_(g_mix), R(dh), R(dx1)] + ([] if after_mid is None else [W_(after_mid)]),
                              [(D_MODEL, F32, D_MODEL, 0)], accs=[(1, D_MODEL)])
    gW["w_q_b"] = _wq_unpad(dwq_p)
    gW["w_kv_b"] = _wkv_unpad(dwk_p, dwv_p)
    gG = {"g_mix": dg_mix, "g_q_a": dg_q_a, "g_kv_a": dg_kv_a, "g_qn": _qk_unpad(dgqn_p),
          "g_kn": _qk_unpad(dgkn_p), "ret_decay_fwd": dlg_f[:, 0, 0][None, :], "ret_decay_bwd": dlg_b[:, 0, 0][None, :],
          "g_ffn": dg_ffn}
    return loss_rows, grad_x, gG, gW


MATS = [("w_in", (1024, 5536), 1), ("w_q_b", (256, 768), 1), ("w_kv_b", (128, 1024), 1), ("w_mla_out", (512, 1024), 1),
        ("w_ret_out", (1024, 1024), 0), ("w_out", (1024, 1024), 0), ("w_gate_up", (1024, 5632), 1), ("w_down", (2816, 1024), 0)]
GAINS = [("g_mix", 1024), ("g_q_a", 256), ("g_kv_a", 128), ("g_qn", 96), ("g_kn", 96), ("ret_decay_fwd", 8), ("ret_decay_bwd", 8),
         ("g_ffn", 1024)]
ORDER = ["g_mix", "w_in", "g_q_a", "w_q_b", "g_kv_a", "w_kv_b", "g_qn", "g_kn", "w_mla_out", "ret_decay_fwd", "ret_decay_bwd",
         "w_ret_out", "w_out", "g_ffn", "w_gate_up", "w_down"]
GAIN_LEN = sum(n for _, n in GAINS)
GAIN_PAD = -(-GAIN_LEN // LANES) * LANES


def _pack_gains(d):
    row = jnp.concatenate([d[n].reshape(1, ln).astype(F32) for n, ln in GAINS], axis=1)
    return jnp.pad(row, ((0, 0), (0, GAIN_PAD - GAIN_LEN)))


def _unpack_gains(row):
    out, off = {}, 0
    for n, ln in GAINS:
        out[n] = row[0, off:off + ln]
        off += ln
    return out


def _unshard(pieces, axis):
    if axis == 0:
        return pieces.reshape((N_DEV * pieces.shape[1], pieces.shape[2]))
    return jnp.concatenate([pieces[p] for p in range(N_DEV)], axis=1)


def _reshard(full, axis):
    if axis == 0:
        return full.reshape((N_DEV, full.shape[0] // N_DEV, full.shape[1]))
    c = full.shape[1] // N_DEV
    return jnp.stack([full[:, c * p:c * (p + 1)] for p in range(N_DEV)])


def _all_gather(shards):
    n = len(shards)

    def body(*refs):
        x_refs, out_refs = refs[:n], refs[n:2 * n]
        send_sems, recv_sems, local_sems = refs[2 * n:]
        x, y, c = lax.axis_index("x"), lax.axis_index("y"), lax.axis_index("c")
        me, sibling = (x, y, c), (x, y, 1 - c)
        chips = [(1 - x, y), (x, 1 - y), (1 - x, 1 - y)]

        def slot(a, px, py, pc):
            return out_refs[a].at[4 * px + 2 * py + pc]

        def copy(a, k, block, to, from_input=False):
            return pltpu.make_async_remote_copy(
                src_ref=x_refs[a] if from_input else slot(a, *block), dst_ref=slot(a, *block),
                send_sem=send_sems.at[a, k], recv_sem=recv_sems.at[a, k], device_id=to, device_id_type=pl.DeviceIdType.MESH)

        mine = [pltpu.make_async_copy(x_refs[a], slot(a, *me), local_sems.at[a]) for a in range(n)]
        first = [copy(a, 0, me, sibling, True) for a in range(n)]
        first += [copy(a, 1 + j, me, (*chip, c), True) for j, chip in enumerate(chips) for a in range(n)]
        for cp in mine + first:
            cp.start()
        passed = []
        for j, chip in enumerate(chips):
            for a in range(n):
                copy(a, 1 + j, (*chip, c), me).wait_recv()
                passed.append(copy(a, 4 + j, (*chip, c), sibling))
                passed[-1].start()
        for a in range(n):
            copy(a, 0, sibling, me).wait_recv()
        for j, chip in enumerate(chips):
            for a in range(n):
                copy(a, 4 + j, (*chip, 1 - c), me).wait_recv()
        for cp in first + passed:
            cp.wait_send()
        for cp in mine:
            cp.wait()

    any_spec = pl.BlockSpec(memory_space=pl.ANY)
    return pl.pallas_call(
        body, name="all_gather_weights", out_shape=[jax.ShapeDtypeStruct((N_DEV,) + s.shape, s.dtype) for s in shards],
        in_specs=[any_spec] * n, out_specs=[any_spec] * n,
        scratch_shapes=[pltpu.SemaphoreType.DMA((n, 7)), pltpu.SemaphoreType.DMA((n, 7)), pltpu.SemaphoreType.DMA((n,))],
    )(*shards)


def _all_to_all(name, pieces):
    srcs, n = pieces, len(pieces)

    def body(*refs):
        in_refs, out_refs = refs[:n], refs[n:2 * n]
        send_sems, recv_sems, local_sems = refs[2 * n:]
        my_id = 4 * lax.axis_index("x") + 2 * lax.axis_index("y") + lax.axis_index("c")
        mine = [pltpu.make_async_copy(in_refs[a].at[my_id], out_refs[a].at[my_id], local_sems.at[a]) for a in range(n)]
        copies = _split_copies(in_refs, out_refs, send_sems, recv_sems, False)
        for cp in mine + copies:
            cp.start()
        for cp in copies:
            cp.wait_recv()
        for cp in copies:
            cp.wait_send()
        for cp in mine:
            cp.wait()

    any_spec = pl.BlockSpec(memory_space=pl.ANY)
    return pl.pallas_call(
        body, name=name, out_shape=[jax.ShapeDtypeStruct(s.shape, s.dtype) for s in srcs],
        in_specs=[any_spec] * n, out_specs=[any_spec] * n,
        scratch_shapes=[pltpu.SemaphoreType.DMA((7 * n,)), pltpu.SemaphoreType.DMA((7 * n,)), pltpu.SemaphoreType.DMA((n,))],
    )(*srcs)


def _flip_peers(x, y, c):
    flips = [(fx, fy, fc) for fx in (0, 1) for fy in (0, 1) for fc in (0, 1)][1:]
    return [(x ^ fx, y ^ fy, c ^ fc) for fx, fy, fc in flips]


def _split_copies(in_refs, land_refs, send_sems, recv_sems, gather):
    x, y, c = lax.axis_index("x"), lax.axis_index("y"), lax.axis_index("c")
    my_id = 4 * x + 2 * y + c
    copies = []
    for kk, p in enumerate(_flip_peers(x, y, c)):
        for a in range(len(in_refs)):
            src = in_refs[a] if gather else in_refs[a].at[4 * p[0] + 2 * p[1] + p[2]]
            copies.append(pltpu.make_async_remote_copy(
                src_ref=src, dst_ref=land_refs[a].at[my_id], send_sem=send_sems.at[a * 7 + kk], recv_sem=recv_sems.at[a * 7 + kk],
                device_id=p, device_id_type=pl.DeviceIdType.MESH))
    return copies


def _exchange_start(name, srcs, gather, after=None):
    n = len(srcs)
    first_out = 2 * n + (0 if after is None else 1)

    def body(*refs):
        for cp in _split_copies(refs[:n], refs[n:2 * n], refs[first_out], refs[first_out + 1], gather):
            cp.start()
        refs[-1][...] = jnp.zeros_like(refs[-1])

    hbm, sem = pl.BlockSpec(memory_space=pltpu.HBM), pl.BlockSpec(memory_space=pltpu.SEMAPHORE)
    land_shapes = [((N_DEV,) + s.shape if gather else s.shape, s.dtype) for s in srcs]
    lands = [pltpu.with_memory_space_constraint(lax.empty(shp, dt), pltpu.HBM) for shp, dt in land_shapes]
    srcs = [pltpu.with_memory_space_constraint(s, pltpu.HBM) for s in srcs]
    res = pl.pallas_call(
        body, name=name,
        out_shape=[pltpu.SemaphoreType.DMA((7 * n,)), pltpu.SemaphoreType.DMA((7 * n,))] + [pltpu.HBM(s.shape, s.dtype) for s in srcs]
        + [pltpu.HBM(shp, dt) for shp, dt in land_shapes] + [jax.ShapeDtypeStruct((8, LANES), F32)],
        in_specs=[hbm] * (2 * n) + ([] if after is None else [pl.BlockSpec(memory_space=pl.ANY)]),
        out_specs=[sem, sem] + [hbm] * (2 * n) + [pl.BlockSpec(memory_space=pltpu.VMEM)],
        input_output_aliases={i: 2 + i for i in range(2 * n)},
        compiler_params=pltpu.CompilerParams(has_side_effects=pltpu.SideEffectType.DATAFLOW_SIDE_EFFECTING),
    )(*srcs, *lands, *([] if after is None else [after]))
    return res[0], res[1], res[2:2 + n], res[2 + n:2 + 2 * n], res[-1]


def _exchange_wait(name, handles, after, gather):
    send_sems, recv_sems, srcs, lands, _ = handles
    n = len(srcs)

    def body(*refs):
        for cp in _split_copies(refs[:n], refs[n:2 * n], refs[2 * n], refs[2 * n + 1], gather):
            cp.wait_send()
            cp.wait_recv()

    hbm, sem = pl.BlockSpec(memory_space=pltpu.HBM), pl.BlockSpec(memory_space=pltpu.SEMAPHORE)
    res = pl.pallas_call(
        body, name=name, out_shape=[pltpu.HBM(t.shape, t.dtype) for t in list(srcs) + list(lands)],
        in_specs=[hbm] * (2 * n) + [sem, sem, pl.BlockSpec(memory_space=pl.ANY)], out_specs=[hbm] * (2 * n),
        input_output_aliases={i: i for i in range(2 * n)},
        compiler_params=pltpu.CompilerParams(has_side_effects=pltpu.SideEffectType.DATAFLOW_SIDE_EFFECTING),
    )(*srcs, *lands, send_sems, recv_sems, after)
    my_id = 4 * lax.axis_index("x") + 2 * lax.axis_index("y") + lax.axis_index("c")
    own = [s if gather else lax.dynamic_index_in_dim(s, my_id, 0, keepdims=False) for s in res[:n]]
    return [lax.dynamic_update_index_in_dim(land, o, my_id, 0) for land, o in zip(res[n:], own)]


def _adamw(name, parts, w, m, v):
    rows, cols = w.shape
    tr = _pick(rows, (128, 64, 32, 16, 8))
    pspec = pl.BlockSpec((N_DEV, tr, cols), lambda i: (0, i, 0))
    rspec = pl.BlockSpec((tr, cols), lambda i: (i, 0))

    def body(p_ref, w_ref, m_ref, v_ref, g_ref, d_ref, m2_ref, v2_ref):
        g, d, m2, v2 = _adamw_fn([p_ref[s] for s in range(N_DEV)], w_ref[...], m_ref[...], v_ref[...])
        g_ref[...], d_ref[...], m2_ref[...], v2_ref[...] = g, d, m2, v2

    return pl.pallas_call(
        body, name=name, grid=(rows // tr,), in_specs=[pspec, rspec, rspec, rspec], out_specs=[rspec] * 4,
        out_shape=[jax.ShapeDtypeStruct((rows, cols), F32)] * 4,
        compiler_params=pltpu.CompilerParams(dimension_semantics=("parallel",), vmem_limit_bytes=VMEM_LIMIT),
    )(parts, w, m, v)


def kernel(x, positions, g_mix, w_in, g_q_a, w_q_b, g_kv_a, w_kv_b, g_qn, g_kn, w_mla_out, ret_decay_fwd, ret_decay_bwd, w_ret_out, w_out, g_ffn, w_gate_up, w_down, loss_target, m_g_mix, m_w_in, m_g_q_a, m_w_q_b, m_g_kv_a, m_w_kv_b, m_g_qn, m_g_kn, m_w_mla_out, m_ret_decay_fwd, m_ret_decay_bwd, m_w_ret_out, m_w_out, m_g_ffn, m_w_gate_up, m_w_down, v_g_mix, v_w_in, v_g_q_a, v_w_q_b, v_g_kv_a, v_w_kv_b, v_g_qn, v_g_kn, v_w_mla_out, v_ret_decay_fwd, v_ret_decay_bwd, v_w_ret_out, v_w_out, v_g_ffn, v_w_gate_up, v_w_down):
    w = dict(g_mix=g_mix, w_in=w_in, g_q_a=g_q_a, w_q_b=w_q_b, g_kv_a=g_kv_a, w_kv_b=w_kv_b, g_qn=g_qn, g_kn=g_kn, w_mla_out=w_mla_out,
             ret_decay_fwd=ret_decay_fwd, ret_decay_bwd=ret_decay_bwd, w_ret_out=w_ret_out, w_out=w_out, g_ffn=g_ffn,
             w_gate_up=w_gate_up, w_down=w_down)
    m = dict(g_mix=m_g_mix, w_in=m_w_in, g_q_a=m_g_q_a, w_q_b=m_w_q_b, g_kv_a=m_g_kv_a, w_kv_b=m_w_kv_b, g_qn=m_g_qn, g_kn=m_g_kn,
             w_mla_out=m_w_mla_out, ret_decay_fwd=m_ret_decay_fwd, ret_decay_bwd=m_ret_decay_bwd, w_ret_out=m_w_ret_out, w_out=m_w_out,
             g_ffn=m_g_ffn, w_gate_up=m_w_gate_up, w_down=m_w_down)
    v = dict(g_mix=v_g_mix, w_in=v_w_in, g_q_a=v_g_q_a, w_q_b=v_w_q_b, g_kv_a=v_g_kv_a, w_kv_b=v_w_kv_b, g_qn=v_g_qn, g_kn=v_g_kn,
             w_mla_out=v_w_mla_out, ret_decay_fwd=v_ret_decay_fwd, ret_decay_bwd=v_ret_decay_bwd, w_ret_out=v_w_ret_out, w_out=v_w_out,
             g_ffn=v_g_ffn, w_gate_up=v_w_gate_up, w_down=v_w_down)
    gains = {n: w[n].reshape(1, ln) for n, ln in GAINS}

    axis_of = {n: axis for n, _, axis in MATS}
    later = [n for n, _, _ in MATS if n not in FIRST_WEIGHTS]
    gathered = _all_gather([w[n].astype(WIRE) for n in FIRST_WEIGHTS])
    W = {n: _unshard(g, axis_of[n]) for n, g in zip(FIRST_WEIGHTS, gathered)}
    later_handles = _exchange_start("gather_later_start", [w[n].astype(WIRE) for n in later], True, after=gathered[0])

    def late_weights(after):
        lands = _exchange_wait("gather_later_wait", later_handles, after, True)
        return {n: _unshard(g, axis_of[n]) for n, g in zip(later, lands)}

    grad_groups = []

    def grad_hook(g):
        names = tuple(g)
        handles = _exchange_start("grads_start_%d" % len(grad_groups), [_reshard(g[n], axis_of[n]).astype(GWIRE) for n in names], False)
        grad_groups.append((names, handles))
        return handles[4]

    S = x.shape[1]
    pos = positions.reshape(S, 1).astype(F32)
    loss_rows, grad_x, gG, gW = _local_step(x.reshape(S, D_MODEL), pos, loss_target.reshape(S, D_MODEL), gains, W, late_weights, grad_hook,
                                            start_after=later_handles[4])
    loss = lax.psum(jnp.sum(loss_rows), ("x", "y", "c"))

    last = [n for n, _, _ in MATS if n not in EARLY_GRADS + MID_GRADS]
    pieces = [_reshard(gW[n], axis_of[n]).astype(GWIRE) for n in last]
    pieces.append(jnp.broadcast_to(_pack_gains(gG)[None], (N_DEV, 1, GAIN_PAD)))
    late_parts = _all_to_all("grads_last", pieces)
    parts = dict(zip(last, late_parts))
    for i, (names, handles) in enumerate(grad_groups):
        parts.update(zip(names, _exchange_wait("grads_wait_%d" % i, handles, late_parts[-1], False)))
    out = [dict() for _ in range(4)]
    for n, _, _ in MATS:
        for o, r in zip(out, _adamw("adamw_" + n, parts[n], w[n], m[n], v[n])):
            o[n] = r
    for o, r in zip(out, _adamw("adamw_gains", late_parts[-1], _pack_gains(w), _pack_gains(m), _pack_gains(v))):
        o.update(_unpack_gains(r))
    return (loss, grad_x.reshape(x.shape), *[o[n] for o in out for n in ORDER])
```

```python
import functools

import numpy as np
import jax
import jax.numpy as jnp
from jax import lax
from jax.experimental import pallas as pl
from jax.experimental.pallas import tpu as pltpu

F32 = jnp.float32
MXU = jnp.bfloat16
WIRE = jnp.bfloat16
GWIRE = jnp.bfloat16

N_DEV = 8
D_MODEL = 1024
HEADS = 8
LANES = 128
Q_RANK, KV_RANK = 256, 128
NOPE, ROPE_M, V_M = 64, 32, 64
QK_M = NOPE + ROPE_M
RQK = 64
CHUNK = 128
FFN = 2816
THETA = 10000.0
EPS = 1e-6
LR, B1, B2, AEPS, WD, STEP = 0.001, 0.9, 0.999, 1e-08, 0.01, 10
VMEM_LIMIT = 56 * 1024 * 1024

NN = ((1,), (0,))
NT = ((1,), (1,))
TN = ((0,), (0,))

P_GATES, P_VR, P_GR, P_QR, P_KR, P_CQ, P_CKV, P_KROPE, P_WIDTH = 0, 2048, 3072, 4096, 4608, 5120, 5376, 5504, 5632
O_CQ, O_CKV, O_KROPE, O_QR, O_KR, O_VR, O_GR, O_GATES = 0, 256, 384, 416, 928, 1440, 2464, 3488


def _dot(a, b, dims):
    return lax.dot_general(a, b, (dims, ((), ())), preferred_element_type=F32)


def _pick(dim, cands):
    for c in cands:
        if dim % c == 0:
            return c
    return dim


def _pairs(t):
    return t.reshape(t.shape[0], 4, 2, 2, 32).transpose(0, 1, 3, 2, 4).reshape(t.shape[0], 512)


def _win_pad(w):
    z = jnp.zeros((w.shape[0], 48), w.dtype)
    kr = w[:, O_KROPE:O_KROPE + 32]
    return jnp.concatenate([w[:, O_GATES:], w[:, O_VR:O_VR + 1024], w[:, O_GR:O_GR + 1024], _pairs(w[:, O_QR:O_QR + 512]),
                            _pairs(w[:, O_KR:O_KR + 512]), w[:, :O_CKV], w[:, O_CKV:O_KROPE], kr[:, :16], z, kr[:, 16:], z], axis=1)


def _win_unpad(g):
    return jnp.concatenate([g[:, P_CQ:P_CQ + 256], g[:, P_CKV:P_CKV + 128], g[:, P_KROPE:P_KROPE + 16], g[:, P_KROPE + 64:P_KROPE + 80],
                            _pairs(g[:, P_QR:P_QR + 512]), _pairs(g[:, P_KR:P_KR + 512]), g[:, P_VR:P_VR + 1024],
                            g[:, P_GR:P_GR + 1024], g[:, P_GATES:P_GATES + 2048]], axis=1)


def _qk_pad(t):
    z = jnp.zeros(t.shape[:-1] + (32,), t.dtype)
    return jnp.concatenate([t[..., 64:80], t[..., 0:48], t[..., 80:96], t[..., 48:64], z], axis=-1)


def _qk_unpad(p):
    return jnp.concatenate([p[..., 16:64], p[..., 80:96], p[..., 0:16], p[..., 64:80]], axis=-1)


def _wq_pad(w):
    return _qk_pad(w.reshape(Q_RANK, HEADS, QK_M)).reshape(Q_RANK, HEADS * LANES)


def _wq_unpad(g):
    return _qk_unpad(g.reshape(Q_RANK, HEADS, LANES)).reshape(Q_RANK, HEADS * QK_M)


def _wkv_pad(w):
    t = w.reshape(KV_RANK, HEADS, NOPE + V_M)
    z = lambda n: jnp.zeros((KV_RANK, HEADS, n), w.dtype)
    wk = jnp.concatenate([z(16), t[..., 0:48], z(16), t[..., 48:64], z(32)], axis=-1)
    wv = jnp.concatenate([t[..., 64:128], z(64)], axis=-1)
    return wk.reshape(KV_RANK, HEADS * LANES), wv.reshape(KV_RANK, HEADS * LANES)


def _wkv_unpad(dwk, dwv):
    k, v = dwk.reshape(KV_RANK, HEADS, LANES), dwv.reshape(KV_RANK, HEADS, LANES)
    return jnp.concatenate([k[..., 16:64], k[..., 80:96], v[..., 0:64]], axis=-1).reshape(KV_RANK, HEADS * (NOPE + V_M))


def _wmla_pad(w):
    t = w.reshape(HEADS, V_M, D_MODEL)
    return jnp.concatenate([t, jnp.zeros_like(t)], axis=1).reshape(HEADS * LANES, D_MODEL)


def _wmla_unpad(g):
    return g.reshape(HEADS, LANES, D_MODEL)[:, :V_M].reshape(HEADS * V_M, D_MODEL)


def _rowwise(name, fn, rows, ts, ins, outs, accs=(), ncol=1):
    n_in, n_out, n_acc = len(ins), len(outs), len(accs)

    def colmap(col):
        if callable(col):
            return lambda i, j: (i, col(j))
        return lambda i, j: (i, col)

    arrays, in_specs = [], []
    for arr, spec in ins:
        arrays.append(arr)
        if spec is None:
            in_specs.append(pl.BlockSpec(arr.shape, functools.partial(lambda i, j, nd: (0,) * nd, nd=arr.ndim)))
        else:
            in_specs.append(pl.BlockSpec((ts, spec[0]), colmap(spec[1])))
    out_shape, out_specs = [], []
    for total, dtype, width, col in outs:
        out_shape.append(jax.ShapeDtypeStruct((rows, total), dtype))
        out_specs.append(pl.BlockSpec((ts, width), colmap(col)))
    for shp in accs:
        out_shape.append(jax.ShapeDtypeStruct(shp, F32))
        out_specs.append(pl.BlockSpec(shp, functools.partial(lambda i, j, nd: (0,) * nd, nd=len(shp))))

    def body(*refs):
        vals = [r[...] for r in refs[:n_in]]
        res = fn(*vals)
        if not isinstance(res, (tuple, list)):
            res = (res,)
        for r, v in zip(refs[n_in:n_in + n_out], res[:n_out]):
            r[...] = v.astype(r.dtype)
        if n_acc:
            first = jnp.logical_and(pl.program_id(0) == 0, pl.program_id(1) == 0)
            for r, v in zip(refs[n_in + n_out:], res[n_out:]):
                @pl.when(first)
                def _(r=r):
                    r[...] = jnp.zeros_like(r)
                r[...] += v.astype(F32)

    res = pl.pallas_call(
        body, name=name, grid=(rows // ts, ncol), in_specs=in_specs, out_specs=out_specs, out_shape=out_shape,
        compiler_params=pltpu.CompilerParams(dimension_semantics=("arbitrary", "arbitrary"), vmem_limit_bytes=VMEM_LIMIT),
    )(*arrays)
    return res


MM_OPERAND_BYTES = 24 * 1024 * 1024


def _mm(name, a, b, mode, add=None, after=None):
    a_halves, b_halves = a.ndim == 3, b.ndim == 3
    assert not a_halves or mode == "nt"
    assert not b_halves or mode == "tn"
    if mode == "nn":
        (M, K), N = a.shape, b.shape[1]
    elif mode == "nt":
        M, K, N = a.shape[-2], a.shape[-1] * (2 if a_halves else 1), b.shape[0]
    else:
        (K, M), N = a.shape, b.shape[-1] * (2 if b_halves else 1)
    tm = _pick(M, (1024, 512, 1408, 256, 128))
    tn = _pick(N // 2 if b_halves else N, (1408, 1024, 512, 256, 128))
    fits = lambda t: 2 * (tm + tn) * t * a.dtype.itemsize <= MM_OPERAND_BYTES
    kdiv = K // 2 if a_halves else K
    tk = next(t for t in (K, 4096, 2816, 2048, 1408, 1024, 512, 256, 128) if kdiv % t == 0 and (fits(t) or t == 128))
    nk = K // tk
    dims = {"nn": NN, "nt": NT, "tn": TN}[mode]
    if a_halves:
        per = kdiv // tk
        a_spec = pl.BlockSpec((None, tm, tk), lambda i, j, k: (k // per, i, k % per))
    else:
        a_spec = pl.BlockSpec((tk, tm), lambda i, j, k: (k, i)) if mode == "tn" else pl.BlockSpec((tm, tk), lambda i, j, k: (i, k))
    if b_halves:
        perj = (N // 2) // tn
        b_spec = pl.BlockSpec((None, tk, tn), lambda i, j, k: (j // perj, k, j % perj))
    else:
        b_spec = pl.BlockSpec((tn, tk), lambda i, j, k: (j, k)) if mode == "nt" else pl.BlockSpec((tk, tn), lambda i, j, k: (k, j))
    o_spec = pl.BlockSpec((tm, tn), lambda i, j, k: (i, j))
    has_add = add is not None

    def body(*refs):
        a_ref, b_ref, o_ref = refs[0], refs[1], refs[-1]
        d = _dot(a_ref[...], b_ref[...], dims)
        first = (d + refs[2][...]) if has_add else d
        if nk == 1:
            o_ref[...] = first
        else:
            k = pl.program_id(2)

            @pl.when(k == 0)
            def _():
                o_ref[...] = first

            @pl.when(k > 0)
            def _():
                o_ref[...] += d

    args = [a, b] + ([add] if has_add else []) + ([] if after is None else [after])
    specs = [a_spec, b_spec] + ([o_spec] if has_add else []) + ([] if after is None else [pl.BlockSpec(memory_space=pl.ANY)])
    return pl.pallas_call(
        body, name=name, grid=(M // tm, N // tn, nk), in_specs=specs, out_specs=o_spec,
        out_shape=jax.ShapeDtypeStruct((M, N), F32),
        compiler_params=pltpu.CompilerParams(dimension_semantics=("parallel", "parallel", "arbitrary"), vmem_limit_bytes=VMEM_LIMIT),
    )(*args)


def _mm_rows(name, a, b, fn, row_ins, whole_ins, outs, accs=(), mode="nn"):
    (M, K), N = a.shape, b.shape[1 if mode == "nn" else 0]
    tm = _pick(M, (512, 256, 128))
    n_in, n_out = 2 + len(row_ins) + len(whole_ins), len(outs)
    windows = [t if isinstance(t, tuple) else (t, (t.shape[1], 0)) for t in row_ins]
    row_ins = [t for t, _ in windows]
    row_specs = [pl.BlockSpec((tm, w), functools.partial(lambda i, col: (i, col), col=col)) for _, (w, col) in windows]

    def body(*refs):
        d = _dot(refs[0][...], refs[1][...], NN if mode == "nn" else NT)
        res = fn(d, *[r[...] for r in refs[2:n_in]])
        for r, v in zip(refs[n_in:n_in + n_out], res[:n_out]):
            r[...] = v.astype(r.dtype)
        for r, v in zip(refs[n_in + n_out:], res[n_out:]):
            @pl.when(pl.program_id(0) == 0)
            def _(r=r):
                r[...] = jnp.zeros_like(r)
            r[...] += v

    row = pl.BlockSpec((tm, N), lambda i: (i, 0))
    whole = lambda t: pl.BlockSpec(t.shape, functools.partial(lambda i, nd: (0,) * nd, nd=t.ndim))
    return pl.pallas_call(
        body, name=name, grid=(M // tm,),
        in_specs=[pl.BlockSpec((tm, K), lambda i: (i, 0)), whole(b)] + row_specs + [whole(t) for t in whole_ins],
        out_specs=[row] * n_out + [pl.BlockSpec(s, functools.partial(lambda i, nd: (0,) * nd, nd=len(s))) for s in accs],
        out_shape=[jax.ShapeDtypeStruct((M, N), dt) for dt in outs] + [jax.ShapeDtypeStruct(s, F32) for s in accs],
        compiler_params=pltpu.CompilerParams(dimension_semantics=("arbitrary",), vmem_limit_bytes=VMEM_LIMIT),
    )(a, b, *row_ins, *whole_ins)


def _ffn_tiles(S):
    return _pick(S, (1024, 512, 256, 128)), _pick(FFN, (1408, 704, 256, 128))


def _gate_up_swiglu(h2, wgu):
    S, K = h2.shape
    tm, tn = _ffn_tiles(S)
    nj = FFN // tn

    def body(a_ref, bg_ref, bu_ref, gu_ref, act_ref):
        a = a_ref[...]
        g, u = _dot(a, bg_ref[...], NN), _dot(a, bu_ref[...], NN)
        gu_ref[0], gu_ref[1] = g.astype(gu_ref.dtype), u.astype(gu_ref.dtype)
        act_ref[...] = _swiglu_fn(g, u).astype(act_ref.dtype)

    return pl.pallas_call(
        body, name="gate_up_swiglu", grid=(S // tm, nj),
        in_specs=[pl.BlockSpec((tm, K), lambda i, j: (i, 0)), pl.BlockSpec((K, tn), lambda i, j: (0, j)),
                  pl.BlockSpec((K, tn), lambda i, j: (0, nj + j))],
        out_specs=[pl.BlockSpec((2, tm, tn), lambda i, j: (0, i, j)), pl.BlockSpec((tm, tn), lambda i, j: (i, j))],
        out_shape=[jax.ShapeDtypeStruct((2, S, FFN), MXU), jax.ShapeDtypeStruct((S, FFN), MXU)],
        compiler_params=pltpu.CompilerParams(dimension_semantics=("parallel", "parallel"), vmem_limit_bytes=VMEM_LIMIT),
    )(h2, wgu, wgu)


def _d_act_swiglu(dx2, wdown, gu):
    S, K = dx2.shape
    tm, tn = _ffn_tiles(S)

    def body(a_ref, b_ref, gu_ref, o_ref):
        dact = _dot(a_ref[...], b_ref[...], NT)
        _, vjp = jax.vjp(_swiglu_fn, gu_ref[0].astype(F32), gu_ref[1].astype(F32))
        dg, du = vjp(dact)
        o_ref[0], o_ref[1] = dg.astype(o_ref.dtype), du.astype(o_ref.dtype)

    stacked = pl.BlockSpec((2, tm, tn), lambda i, j: (0, i, j))
    return pl.pallas_call(
        body, name="d_act_swiglu", grid=(S // tm, FFN // tn),
        in_specs=[pl.BlockSpec((tm, K), lambda i, j: (i, 0)), pl.BlockSpec((tn, K), lambda i, j: (j, 0)), stacked],
        out_specs=stacked, out_shape=jax.ShapeDtypeStruct((2, S, FFN), MXU),
        compiler_params=pltpu.CompilerParams(dimension_semantics=("parallel", "parallel"), vmem_limit_bytes=VMEM_LIMIT),
    )(dx2, wdown, gu)


@jax.custom_vjp
def _swap64(x):
    return pltpu.roll(x, 64, 1)


_swap64.defvjp(lambda x: (_swap64(x), None), lambda _, g: (_swap64(g),))


@jax.custom_vjp
def _mxdot(a, b):
    return _dot(a.astype(MXU), b.astype(MXU), NN)


def _mxdot_bwd(res, g):
    a, b = res
    gb = g.astype(MXU)
    return _dot(gb, b.astype(MXU), NT), _dot(a.astype(MXU), gb, TN)


_mxdot.defvjp(lambda a, b: (_mxdot(a, b), (a, b)), _mxdot_bwd)


def _row_sum(t):
    if t.shape[-1] == LANES:
        return lax.dot_general(t, jnp.ones((LANES, LANES), F32), ((NN), ((), ())), precision=lax.Precision.HIGH,
                               preferred_element_type=F32)
    return jnp.sum(t, axis=-1, keepdims=True)


@functools.partial(jax.custom_vjp, nondiff_argnums=(1,))
def _unit_rms(x, n):
    return x * lax.rsqrt(_row_sum(x * x) * (1.0 / n) + EPS)


def _unit_rms_fwd(x, n):
    r = lax.rsqrt(_row_sum(x * x) * (1.0 / n) + EPS)
    y = x * r
    return y, (y, r)


def _unit_rms_bwd(n, res, g):
    y, r = res
    return (r * (g - y * (_row_sum(g * y) * (1.0 / n))),)


_unit_rms.defvjp(_unit_rms_fwd, _unit_rms_bwd)


def _rms(x):
    return _unit_rms(x, x.shape[-1])


def _rmsg_fn(x, g):
    return _rms(x) * g


def _silu(x):
    return x * jax.nn.sigmoid(x)


def _tables_fn(pos, inv_m, sgn_m, inv_r, sgn_r):
    am, ar = pos * inv_m, pos * inv_r
    return jnp.cos(am), jnp.sin(am) * sgn_m, jnp.cos(ar), jnp.sin(ar) * sgn_r


def _head_blocks(t):
    return [t[:, LANES * h:LANES * (h + 1)] for h in range(t.shape[1] // LANES)]


def _mla_prep_fn(cq, ckv, kr, cosm, sinm, gqa, gkva, gqn, gkn, wq, wk, wv):
    cqn = _rms(cq) * gqa
    ckvn = _rms(ckv) * gkva
    q_raw = _mxdot(cqn, wq)
    k_raw = _mxdot(ckvn, wk)
    lane = lax.broadcasted_iota(jnp.int32, (1, HEADS * LANES), 1)
    v = _mxdot(ckvn, wv) + (lane % LANES == V_M).astype(F32)

    def norm_rope(blocks, g, extra):
        outs = []
        for b in blocks:
            if extra is not None:
                b = b + extra
            n = _unit_rms(b, QK_M) * g
            outs.append(n * cosm + _swap64(n) * sinm)
        return jnp.concatenate(outs, axis=1)

    q = norm_rope(_head_blocks(q_raw), gqn, None)
    k = norm_rope(_head_blocks(k_raw), gkn, kr)
    return q, k, v


def _ret_prep_fn(qr, kr, cosr, sinr):
    def rope(t, scale):
        return jnp.concatenate([(b * cosr + _swap64(b) * sinr) * scale for b in _head_blocks(t)], axis=1)
    return rope(qr, 1.0), rope(kr, RQK ** -0.5)


def _ret_post_fn(rf, rb, gr):
    ret = rf + rb
    outs = []
    for b, g in zip(_head_blocks(ret), _head_blocks(gr)):
        outs.append(_silu(g) * _rms(b))
    return jnp.concatenate(outs, axis=1)


def _merge_fn(ga, gb, ya, yb):
    return jax.nn.sigmoid(ga) * ya + jax.nn.sigmoid(gb) * yb


def _swiglu_fn(gate, up):
    return _silu(gate) * up


def _loss_fn(x2, tgt):
    d = x2 - tgt
    return d * (1.0 / D_MODEL), 0.5 * jnp.sum(d * d, axis=0, keepdims=True) * (1.0 / D_MODEL)


def _adamw_fn(parts, w, m, v):
    g = parts[0].astype(F32)
    for p in range(1, N_DEV):
        g = g + parts[p].astype(F32)
    m2 = B1 * m + (1.0 - B1) * g
    v2 = B2 * v + (1.0 - B2) * jnp.square(g)
    m_hat = m2 / (1.0 - B1 ** STEP)
    v_hat = v2 / (1.0 - B2 ** STEP)
    delta = -LR * (m_hat / (jnp.sqrt(v_hat) + AEPS) + WD * w)
    return g, delta, m2, v2


SCALE = QK_M ** -0.5
LOG2E = 1.4426950408889634
FLASH_ROWS = 32


def _flash_fwd(q, k, v):
    S = q.shape[0]
    tk = _pick(S, (512, 256, 128))
    tq = _pick(S, (1024, 512, 256, 128))
    ncb = tk // LANES
    nkv = S // tk
    mrows = 64
    c = SCALE * LOG2E

    def body(q_ref, k_ref, v_ref, o_ref, lse_ref, s_a, p_a, s_b, p_b, m_sc, a_sc, acc_sc):
        m_sc[...] = jnp.full_like(m_sc, -jnp.inf)
        acc_sc[...] = jnp.zeros_like(acc_sc)
        qb = q_ref[...]

        def scores(j, s_buf):
            s_buf[...] = _dot(qb, k_ref[j * tk:(j + 1) * tk, :], NT)

        def stage(j, s_buf, p_buf, s_next):
            if j + 1 < nkv:
                scores(j + 1, s_next)
            for r in range(tq // mrows):
                rows = slice(r * mrows, (r + 1) * mrows)
                cols = [s_buf[rows, LANES * cb:LANES * (cb + 1)] for cb in range(ncb)]
                m_prev = m_sc[rows, :]
                row_max = jnp.max(functools.reduce(jnp.maximum, cols), axis=-1, keepdims=True)
                m_new = jnp.maximum(m_prev, jnp.broadcast_to(row_max, (mrows, LANES)))
                a_sc[rows, :] = jnp.exp2((m_prev - m_new) * c)
                m_sc[rows, :] = m_new
                for cb in range(ncb):
                    p_buf[rows, LANES * cb:LANES * (cb + 1)] = jnp.exp2((cols[cb] - m_new) * c).astype(p_buf.dtype)
            acc_sc[...] = a_sc[...] * acc_sc[...] + _dot(p_buf[...], v_ref[j * tk:(j + 1) * tk, :], NN)

        scores(0, s_a)
        for j in range(nkv):
            stage(j, *((s_a, p_a, s_b) if j % 2 == 0 else (s_b, p_b, s_a)))
        acc = acc_sc[...]
        lane = lax.broadcasted_iota(jnp.int32, (1, LANES), 1)
        l = jnp.sum(jnp.where(lane == V_M, acc, 0.0), axis=-1, keepdims=True)
        o_ref[...] = (acc / l).astype(o_ref.dtype)
        lse_ref[...] = m_sc[...] * c + jnp.log2(jnp.broadcast_to(l, (tq, LANES)))

    qspec = pl.BlockSpec((tq, LANES), lambda h, i: (i, h))
    kspec = pl.BlockSpec((S, LANES), lambda h, i: (0, h))
    return pl.pallas_call(
        body, name="flash_fwd", grid=(HEADS, S // tq), in_specs=[qspec, kspec, kspec], out_specs=[qspec, qspec],
        out_shape=[jax.ShapeDtypeStruct((S, HEADS * LANES), MXU), jax.ShapeDtypeStruct((S, HEADS * LANES), F32)],
        scratch_shapes=[pltpu.VMEM((tq, tk), F32), pltpu.VMEM((tq, tk), MXU)] * 2 + [pltpu.VMEM((tq, LANES), F32)] * 3,
        compiler_params=pltpu.CompilerParams(dimension_semantics=("parallel", "arbitrary"), vmem_limit_bytes=VMEM_LIMIT),
    )(q, k, v)


def _delta_fn(o, do):
    outs = [jnp.broadcast_to(jnp.sum(a * b, axis=-1, keepdims=True), a.shape) for a, b in zip(_head_blocks(o), _head_blocks(do))]
    return do, jnp.concatenate(outs, axis=1)


def _flash_bwd(q, k, v, do, lse, delta):
    S = q.shape[0]
    tq = tk = _pick(S, (512, 256, 128))
    ncb = tk // LANES
    c = SCALE * LOG2E
    nq = S // tq
    nsub = 2 if (S // tk) % 2 == 0 else 1
    stages = [(sub, i) for sub in range(nsub) for i in range(nq)]

    def body(q_ref, k_ref, v_ref, do_ref, lse_ref, dl_ref, dq_ref, dk_ref, dv_ref, s_a, dp_a, p_a, ds_a, s_b, dp_b, p_b, ds_b):
        @pl.when(pl.program_id(1) == 0)
        def _():
            dq_ref[...] = jnp.zeros_like(dq_ref)

        dk_ref[...] = jnp.zeros_like(dk_ref)
        dv_ref[...] = jnp.zeros_like(dv_ref)
        bufs = [(s_a, dp_a, p_a, ds_a), (s_b, dp_b, p_b, ds_b)]

        def scores(sub, i, s_buf, dp_buf):
            kv_rows, q_rows = slice(sub * tk, (sub + 1) * tk), slice(i * tq, (i + 1) * tq)
            s_buf[...] = _dot(q_ref[q_rows, :], k_ref[kv_rows, :], NT)
            dp_buf[...] = _dot(do_ref[q_rows, :], v_ref[kv_rows, :], NT)

        scores(*stages[0], *bufs[0][:2])
        for t, (sub, i) in enumerate(stages):
            s_buf, dp_buf, p_buf, ds_buf = bufs[t % 2]
            if t + 1 < len(stages):
                scores(*stages[t + 1], *bufs[(t + 1) % 2][:2])
            for r in range(tq // FLASH_ROWS):
                rows = slice(r * FLASH_ROWS, (r + 1) * FLASH_ROWS)
                grows = slice(i * tq + r * FLASH_ROWS, i * tq + (r + 1) * FLASH_ROWS)
                lse_b, dl_b = lse_ref[grows, :], dl_ref[grows, :]
                for cb in range(ncb):
                    sl = slice(LANES * cb, LANES * (cb + 1))
                    p = jnp.exp2(s_buf[rows, sl] * c - lse_b)
                    p_buf[rows, sl] = p.astype(p_buf.dtype)
                    ds_buf[rows, sl] = (p * (dp_buf[rows, sl] - dl_b) * SCALE).astype(ds_buf.dtype)
            kv_rows, q_rows = slice(sub * tk, (sub + 1) * tk), slice(i * tq, (i + 1) * tq)
            dv_ref[kv_rows, :] += _dot(p_buf[...], do_ref[q_rows, :], TN)
            dk_ref[kv_rows, :] += _dot(ds_buf[...], q_ref[q_rows, :], TN)
            dq_ref[q_rows, :] += _dot(ds_buf[...], k_ref[kv_rows, :], NN)

    hspec = pl.BlockSpec((S, LANES), lambda h, j: (0, h))
    kspec = pl.BlockSpec((nsub * tk, LANES), lambda h, j: (j, h))
    full = jax.ShapeDtypeStruct((S, HEADS * LANES), F32)
    tile_bufs = [pltpu.VMEM((tq, tk), F32), pltpu.VMEM((tq, tk), F32), pltpu.VMEM((tq, tk), MXU), pltpu.VMEM((tq, tk), MXU)]
    return pl.pallas_call(
        body, name="flash_bwd", grid=(HEADS, S // (nsub * tk)), in_specs=[hspec, kspec, kspec, hspec, hspec, hspec],
        out_specs=[hspec, kspec, kspec], out_shape=[full, full, full],
        scratch_shapes=tile_bufs + tile_bufs,
        compiler_params=pltpu.CompilerParams(dimension_semantics=("parallel", "arbitrary"), vmem_limit_bytes=VMEM_LIMIT),
    )(q, k, v, do, lse, delta)


def _ret_consts(lgh, head, rev):
    C = CHUNK
    lane = lax.broadcasted_iota(jnp.int32, (1, LANES), 1)
    hm = ((lane // 32) % 2 == head % 2).astype(F32)
    r = lax.broadcasted_iota(jnp.int32, (C, C), 0)
    c = lax.broadcasted_iota(jnp.int32, (C, C), 1)
    diff = ((c - r) if rev else (r - c)).astype(F32)
    mask = (diff > 0) if rev else (diff >= 0)
    dpos = jnp.maximum(diff, 0.0)
    din = jnp.where(mask, jnp.exp(lgh * dpos), 0.0)
    idx = lax.broadcasted_iota(jnp.int32, (C, 1), 0).astype(F32)
    eq = (C - idx) if rev else (idx + 1.0)
    ek = idx if rev else (C - 1.0 - idx)
    qd, kd = jnp.exp(lgh * eq), jnp.exp(lgh * ek)
    cd = jnp.exp(lgh * jnp.full((1, 1), float(C), F32))
    return hm, din, dpos, qd, kd, cd, eq, ek


RET_HEADS_PER_STEP = 8


def _ret_fwd(name, qt, kt, proj, lg, rev):
    S = qt.shape[0]
    C = CHUNK
    TB = _pick(S, (512, 256, 128))
    cb, nb = TB // C, S // TB
    hps = RET_HEADS_PER_STEP
    blk = (lambda g: nb - 1 - g) if rev else (lambda g: g)

    def body(lg_ref, q_ref, k_ref, v_ref, o_ref, st_ref, state_sc):
        hg, g = pl.program_id(0), pl.program_id(1)

        @pl.when(g == 0)
        def _():
            state_sc[...] = jnp.zeros_like(state_sc)

        consts = [_ret_consts(lg_ref[hg * hps + u], u, rev) for u in range(hps)]
        order = list(reversed(range(cb))) if rev else list(range(cb))
        units = [(cc, u) for cc in order for u in range(hps)]

        def operands(cc, u):
            rows = pl.ds(cc * C, C)
            pair = slice(LANES * (u // 2), LANES * (u // 2 + 1))
            hm = consts[u][0]
            return q_ref[rows, pair] * hm, k_ref[rows, pair] * hm, v_ref[rows, LANES * u:LANES * (u + 1)].astype(MXU)

        a, inc = {}, {}
        for cc, u in units:
            q, k, v = operands(cc, u)
            a[cc, u] = _dot(q.astype(MXU), k.astype(MXU), NT) * consts[u][1]
            inc[cc, u] = _dot((k * consts[u][4]).astype(MXU), v, TN)
        for u in range(hps):
            st = state_sc[u]
            for cc in order:
                st_ref[u, cc] = st
                st = st * consts[u][5] + inc[cc, u]
            state_sc[u] = st
        for cc, u in units:
            q, _, v = operands(cc, u)
            cross = _dot((q * consts[u][3]).astype(MXU), st_ref[u, cc].astype(MXU), NN)
            o_ref[pl.ds(cc * C, C), LANES * u:LANES * (u + 1)] = _dot(a[cc, u].astype(MXU), v, NN) + cross

    qk_spec = pl.BlockSpec((TB, LANES * hps // 2), lambda h, g: (blk(g), h))
    return pl.pallas_call(
        body, name=name, grid=(HEADS // hps, nb),
        in_specs=[pl.BlockSpec(memory_space=pltpu.SMEM), qk_spec, qk_spec,
                  pl.BlockSpec((TB, LANES * hps), lambda h, g: (blk(g), P_VR // (LANES * hps) + h))],
        out_specs=[pl.BlockSpec((TB, LANES * hps), lambda h, g: (blk(g), h)),
                   pl.BlockSpec((hps, cb, LANES, LANES), lambda h, g: (h, blk(g), 0, 0))],
        out_shape=[jax.ShapeDtypeStruct((S, HEADS * LANES), F32), jax.ShapeDtypeStruct((HEADS, S // C, LANES, LANES), F32)],
        scratch_shapes=[pltpu.VMEM((hps, LANES, LANES), F32)],
        compiler_params=pltpu.CompilerParams(dimension_semantics=("parallel", "arbitrary"), vmem_limit_bytes=VMEM_LIMIT),
    )(lg, qt, kt, proj)


def _ret_bwd(name, qt, kt, proj, dret, states, lg, rev):
    S = qt.shape[0]
    C = CHUNK
    TB = _pick(S, (512, 256, 128))
    cb, nb = TB // C, S // TB
    hps = RET_HEADS_PER_STEP
    blk = (lambda g: g) if rev else (lambda g: nb - 1 - g)

    def body(lg_ref, q_ref, k_ref, v_ref, do_ref, st_ref, dq_ref, dk_ref, dv_ref, dlg_ref, ds_sc, acc_cc, acc_q, acc_k, acc_s):
        hg, g = pl.program_id(0), pl.program_id(1)

        @pl.when(g == 0)
        def _():
            ds_sc[...] = jnp.zeros_like(ds_sc)
            acc_cc[...] = jnp.zeros_like(acc_cc)
            acc_q[...] = jnp.zeros_like(acc_q)
            acc_k[...] = jnp.zeros_like(acc_k)
            acc_s[...] = jnp.zeros_like(acc_s)

        lgs = [lg_ref[hg * hps + u] for u in range(hps)]
        consts = [_ret_consts(lgs[u], u, rev) for u in range(hps)]
        order = list(range(cb)) if rev else list(reversed(range(cb)))
        units = [(cc, u) for cc in order for u in range(hps)]

        def operands(cc, u):
            rows = pl.ds(cc * C, C)
            pair = slice(LANES * (u // 2), LANES * (u // 2 + 1))
            head = slice(LANES * u, LANES * (u + 1))
            hm = consts[u][0]
            return q_ref[rows, pair] * hm, k_ref[rows, pair] * hm, v_ref[rows, head].astype(MXU), do_ref[rows, head].astype(MXU)

        a, dp, dqs, inc = {}, {}, {}, {}
        for cc, u in units:
            q, k, vb, dob = operands(cc, u)
            a[cc, u] = _dot(q.astype(MXU), k.astype(MXU), NT)
            dp[cc, u] = _dot(dob, vb, NT)
            dqs[cc, u] = _dot(dob, st_ref[u, cc].astype(MXU), NT)
            inc[cc, u] = _dot((q * consts[u][3]).astype(MXU), dob, TN)
        dsn = {}
        for u in range(hps):
            ds = ds_sc[u]
            for cc in order:
                dsn[cc, u] = ds
                ds = ds * consts[u][5] + inc[cc, u]
            ds_sc[u] = ds
        even = {}
        for cc, u in units:
            hm, din, dpos, qd, kd, cd, eq, ek = consts[u]
            rows, head = pl.ds(cc * C, C), slice(LANES * u, LANES * (u + 1))
            q, k, vb, dob = operands(cc, u)
            qb, kb = q.astype(MXU), k.astype(MXU)
            dsnb = dsn[cc, u].astype(MXU)
            da = (dp[cc, u] * din).astype(MXU)
            vds = _dot(vb, dsnb, NT)
            dq_u = (_dot(da, kb, NN) + dqs[cc, u] * qd) * hm
            dk_u = (_dot(da, qb, TN) + vds * kd) * hm
            if u % 2 == 0:
                even[cc] = (dq_u, dk_u)
            else:
                pair = slice(LANES * (u // 2), LANES * (u // 2 + 1))
                dq_ref[rows, pair] = even[cc][0] + dq_u
                dk_ref[rows, pair] = even[cc][1] + dk_u
            dv_ref[rows, head] = _dot((a[cc, u] * din).astype(MXU), dob, TN) + _dot((k * kd).astype(MXU), dsnb, NN)
            acc_cc[u] += dp[cc, u] * a[cc, u] * din * dpos
            acc_q[u] += dqs[cc, u] * q * (qd * eq)
            acc_k[u] += vds * k * (kd * ek)
            acc_s[u] += dsn[cc, u] * st_ref[u, cc] * (cd * float(C))

        @pl.when(g == nb - 1)
        def _():
            for u in range(hps):
                tot = (jnp.sum(acc_cc[u], keepdims=True) + jnp.sum(acc_q[u], keepdims=True)
                       + jnp.sum(acc_k[u], keepdims=True) + jnp.sum(acc_s[u], keepdims=True))
                dlg_ref[u] = jnp.broadcast_to(tot * lgs[u], (8, LANES))

    full = jax.ShapeDtypeStruct((S, HEADS * LANES), F32)
    hspec = pl.BlockSpec((TB, LANES * hps), lambda h, g: (blk(g), h))
    qk_spec = pl.BlockSpec((TB, LANES * hps // 2), lambda h, g: (blk(g), h))
    return pl.pallas_call(
        body, name=name, grid=(HEADS // hps, nb),
        in_specs=[pl.BlockSpec(memory_space=pltpu.SMEM), qk_spec, qk_spec,
                  pl.BlockSpec((TB, LANES * hps), lambda h, g: (blk(g), P_VR // (LANES * hps) + h)),
                  hspec,
                  pl.BlockSpec((hps, cb, LANES, LANES), lambda h, g: (h, blk(g), 0, 0))],
        out_specs=[qk_spec, qk_spec, hspec, pl.BlockSpec((hps, 8, LANES), lambda h, g: (h, 0, 0))],
        out_shape=[jax.ShapeDtypeStruct(qt.shape, F32), jax.ShapeDtypeStruct(kt.shape, F32), full,
                   jax.ShapeDtypeStruct((HEADS, 8, LANES), F32)],
        scratch_shapes=[pltpu.VMEM((hps, LANES, LANES), F32), pltpu.VMEM((hps, C, C), F32), pltpu.VMEM((hps, C, LANES), F32),
                        pltpu.VMEM((hps, C, LANES), F32), pltpu.VMEM((hps, LANES, LANES), F32)],
        compiler_params=pltpu.CompilerParams(dimension_semantics=("parallel", "arbitrary"), vmem_limit_bytes=VMEM_LIMIT),
    )(lg, qt, kt, proj, dret, states)


def _rope_consts():
    inv16 = THETA ** (-jnp.arange(16, dtype=F32) / 16)
    inv32 = THETA ** (-jnp.arange(32, dtype=F32) / 32)
    lane = np.arange(LANES)
    z48 = jnp.zeros((48,), F32)
    inv_m = jnp.concatenate([inv16, z48, inv16, z48])[None, :]
    sgn_m = jnp.asarray(np.where(lane < 16, -1.0, np.where((lane >= 64) & (lane < 80), 1.0, 0.0)), F32)[None, :]
    inv_r = jnp.concatenate([inv32] * 4)[None, :]
    sgn_r = jnp.asarray(np.where(lane < 64, -1.0, 1.0), F32)[None, :]
    return inv_m, sgn_m, inv_r, sgn_r


FIRST_WEIGHTS = ("w_in", "w_q_b", "w_kv_b")
EARLY_GRADS = ("w_down", "w_gate_up", "w_out", "w_ret_out")
MID_GRADS = ("w_mla_out", "w_in")


def _local_step(x, pos, tgt, gains, W, late_weights=None, grad_hook=None, start_after=None):
    S = x.shape[0]
    ts = _pick(S, (256, 128))
    ts_light = _pick(S, (512, 256, 128))
    R = lambda a, w=None, c=0: (a, ((a.shape[1] if w is None else w), c))
    W_ = lambda a: (a, None)

    win = _win_pad(W["w_in"])
    wq = _wq_pad(W["w_q_b"])
    wk, wv = _wkv_pad(W["w_kv_b"])
    gqn, gkn = _qk_pad(gains["g_qn"]), _qk_pad(gains["g_kn"])
    g_mix, g_q_a, g_kv_a, g_ffn = gains["g_mix"], gains["g_q_a"], gains["g_kv_a"], gains["g_ffn"]
    lg_f = -jnp.exp(gains["ret_decay_fwd"][0])
    lg_b = -jnp.exp(gains["ret_decay_bwd"][0])

    consts = list(_rope_consts())
    cosm, sinm, cosr, sinr = _rowwise("rope_tables", _tables_fn, S, ts_light,[R(pos)] + [W_(c) for c in consts],
                                      [(LANES, F32, LANES, 0)] * 4)

    (h,) = _rowwise("rms_mix", _rmsg_fn, S, ts_light,[R(x), W_(g_mix)], [(D_MODEL, MXU, D_MODEL, 0)])
    proj = _mm("in_proj", h, win, "nn", after=start_after)
    seg = lambda off, w: (proj, (w, off // w))
    mla_ins = [seg(P_CQ, 256), seg(P_CKV, 128), seg(P_KROPE, 128), R(cosm), R(sinm),
               W_(g_q_a), W_(g_kv_a), W_(gqn), W_(gkn), W_(wq), W_(wk), W_(wv)]
    q, k, v = _rowwise("mla_prep", _mla_prep_fn, S, ts, mla_ins, [(HEADS * LANES, MXU, HEADS * LANES, 0)] * 3)
    o_bf, lse = _flash_fwd(q, k, v)
    if late_weights is not None:
        W = {**W, **late_weights(lse)}
    wmla = _wmla_pad(W["w_mla_out"])
    wret, wout, wgu, wdown = W["w_ret_out"], W["w_out"], W["w_gate_up"], W["w_down"]
    y_a = _mm("mla_out", o_bf, wmla, "nn")

    ret_ins = [seg(P_QR, 512), seg(P_KR, 512), R(cosr), R(sinr)]
    qt, kt = _rowwise("ret_prep", _ret_prep_fn, S, ts_light,ret_ins, [(512, F32, 512, 0)] * 2)
    ret_f, st_f = _ret_fwd("ret_fwd_f", qt, kt, proj, lg_f, False)
    ret_b, st_b = _ret_fwd("ret_fwd_b", qt, kt, proj, lg_b, True)
    post_ins = [R(ret_f), R(ret_b), seg(P_GR, 1024)]
    (o_b,) = _rowwise("ret_post", _ret_post_fn, S, ts_light,post_ins, [(1024, MXU, 1024, 0)])
    y_b, merged = _mm_rows("ret_out_merge", o_b, wret, lambda yb, ga, gb, ya: (yb, _merge_fn(ga, gb, ya, yb)),
                           [seg(P_GATES, 1024), (proj, (1024, 1)), R(y_a)], [], [F32, MXU])
    merge_ins = [seg(P_GATES, 1024), (proj, (1024, 1)), R(y_a), R(y_b)]
    def residual_rms(d, xx, g):
        r = d + xx
        return r, _rmsg_fn(r, g)

    x1, h2 = _mm_rows("out_proj_rms_ffn", merged, wout, residual_rms, [x], [g_ffn], [F32, MXU])
    gu, act = _gate_up_swiglu(h2, wgu)

    def residual_loss(d, xx, t):
        dx, rows = _loss_fn(d + xx, t)
        return dx, dx, rows

    dx2, dx2_bf, loss_rows = _mm_rows("down_proj_loss", act, wdown, residual_loss, [x1, tgt], [], [F32, MXU], accs=[(1, D_MODEL)])

    gW = {}
    gW["w_down"] = _mm("d_w_down", act, dx2_bf, "tn")
    dgu = _d_act_swiglu(dx2_bf, wdown, gu)
    gW["w_gate_up"] = _mm("d_w_gate_up", h2, dgu, "tn")
    dh2 = _mm("d_h2", dgu, wgu, "nt")

    def rms_bwd(xx, g, dh, dres):
        _, vjp = jax.vjp(_rmsg_fn, xx, g)
        dx, dg = vjp(dh)
        dx = dx + dres
        return dx, dx, dg

    dx1, dx1_bf, dg_ffn = _rowwise("rms_ffn_bwd", rms_bwd, S, ts_light,[R(x1), W_(g_ffn), R(dh2), R(dx2)],
                                   [(D_MODEL, F32, D_MODEL, 0), (D_MODEL, MXU, D_MODEL, 0)], accs=[(1, D_MODEL)])
    gW["w_out"] = _mm("d_w_out", merged, dx1_bf, "tn")
    def merge_bwd(dm, ga, gb, ya, yb):
        _, vjp = jax.vjp(_merge_fn, ga, gb, ya, yb)
        return vjp(dm)

    dga, dgb, dy_a, dy_b = _mm_rows("d_merged_merge_bwd", dx1_bf, wout, merge_bwd, merge_ins, [], [MXU] * 4, mode="nt")
    gW["w_ret_out"] = _mm("d_w_ret_out", o_b, dy_b, "tn")
    after_early = [] if grad_hook is None else [grad_hook({n: gW[n] for n in EARLY_GRADS})]

    def post_bwd(dob, rf, rb, gr, *_):
        _, vjp = jax.vjp(_ret_post_fn, rf, rb, gr)
        drf, _, dgr = vjp(dob)
        return drf, dgr

    dret, dg_r = _mm_rows("d_o_b_ret_post_bwd", dy_b, wret, post_bwd, post_ins, after_early, [MXU, MXU], mode="nt")
    dq_f, dk_f, dv_f, dlg_f = _ret_bwd("ret_bwd_f", qt, kt, proj, dret, st_f, lg_f, False)
    dq_b, dk_b, dv_b, dlg_b = _ret_bwd("ret_bwd_b", qt, kt, proj, dret, st_b, lg_b, True)

    def ret_prep_bwd(qr, kr, cosr_, sinr_, dqf, dqb, dkf, dkb, dvf, dvb):
        _, vjp = jax.vjp(lambda a, b: _ret_prep_fn(a, b, cosr_, sinr_), qr, kr)
        dqr, dkr = vjp((dqf + dqb, dkf + dkb))
        return dqr, dkr, dvf + dvb

    dq_r, dk_r, dv_r = _rowwise("ret_prep_bwd", ret_prep_bwd, S, ts_light,ret_ins + [R(t) for t in (dq_f, dq_b, dk_f, dk_b, dv_f, dv_b)],
                                [(512, MXU, 512, 0), (512, MXU, 512, 0), (1024, MXU, 1024, 0)])

    gW_mla_p = _mm("d_w_mla_out", o_bf, dy_a, "tn")
    do_bf, delta = _mm_rows("d_o_attn_delta", dy_a, wmla, lambda d, oo, *_: _delta_fn(oo.astype(F32), d), [o_bf], after_early, [MXU, F32], mode="nt")
    dq, dk, dv = _flash_bwd(q, k, v, do_bf, lse, delta)

    def mla_prep_bwd(cq, ckv, kr, cosm_, sinm_, gqa, gkva, gqn_, gkn_, wq_, wk_, wv_, dq_, dk_, dv_):
        f = lambda cq, ckv, kr, gqa, gkva, gqn_, gkn_, wq_, wk_, wv_: _mla_prep_fn(cq, ckv, kr, cosm_, sinm_, gqa, gkva, gqn_, gkn_, wq_, wk_, wv_)
        _, vjp = jax.vjp(f, cq, ckv, kr, gqa, gkva, gqn_, gkn_, wq_.astype(F32), wk_.astype(F32), wv_.astype(F32))
        return vjp((dq_, dk_, dv_))

    mb = _rowwise("mla_prep_bwd", mla_prep_bwd, S, ts, mla_ins + [R(dq), R(dk), R(dv)],
                  [(256, MXU, 256, 0), (128, MXU, 128, 0), (128, MXU, 128, 0)],
                  accs=[(1, 256), (1, 128), (1, LANES), (1, LANES), (256, HEADS * LANES), (128, HEADS * LANES), (128, HEADS * LANES)])
    dc_q, dc_kv, dk_rope, dg_q_a, dg_kv_a, dgqn_p, dgkn_p, dwq_p, dwk_p, dwv_p = mb

    dproj = jnp.concatenate([dga, dgb, dv_r, dg_r, dq_r, dk_r, dc_q, dc_kv, dk_rope], axis=1)
    gW["w_in"] = _win_unpad(_mm("d_w_in", h, dproj, "tn"))
    gW["w_mla_out"] = _wmla_unpad(gW_mla_p)
    after_mid = None if grad_hook is None else grad_hook({n: gW[n] for n in MID_GRADS})
    dh = _mm("d_h", dproj, win, "nt", after=after_mid)
    grad_x, dg_mix = _rowwise("rms_mix_bwd", lambda a, b, c, d, *_: rms_bwd(a, b, c, d)[1:], S, ts_light,
                              [R(x), W_(g_mix), R(dh), R(dx1)] + ([] if after_mid is None else [W_(after_mid)]),
                              [(D_MODEL, F32, D_MODEL, 0)], accs=[(1, D_MODEL)])
    gW["w_q_b"] = _wq_unpad(dwq_p)
    gW["w_kv_b"] = _wkv_unpad(dwk_p, dwv_p)
    gG = {"g_mix": dg_mix, "g_q_a": dg_q_a, "g_kv_a": dg_kv_a, "g_qn": _qk_unpad(dgqn_p),
          "g_kn": _qk_unpad(dgkn_p), "ret_decay_fwd": dlg_f[:, 0, 0][None, :], "ret_decay_bwd": dlg_b[:, 0, 0][None, :],
          "g_ffn": dg_ffn}
    return loss_rows, grad_x, gG, gW


MATS = [("w_in", (1024, 5536), 1), ("w_q_b", (256, 768), 1), ("w_kv_b", (128, 1024), 1), ("w_mla_out", (512, 1024), 1),
        ("w_ret_out", (1024, 1024), 0), ("w_out", (1024, 1024), 0), ("w_gate_up", (1024, 5632), 1), ("w_down", (2816, 1024), 0)]
GAINS = [("g_mix", 1024), ("g_q_a", 256), ("g_kv_a", 128), ("g_qn", 96), ("g_kn", 96), ("ret_decay_fwd", 8), ("ret_decay_bwd", 8),
         ("g_ffn", 1024)]
ORDER = ["g_mix", "w_in", "g_q_a", "w_q_b", "g_kv_a", "w_kv_b", "g_qn", "g_kn", "w_mla_out", "ret_decay_fwd", "ret_decay_bwd",
         "w_ret_out", "w_out", "g_ffn", "w_gate_up", "w_down"]
GAIN_LEN = sum(n for _, n in GAINS)
GAIN_PAD = -(-GAIN_LEN // LANES) * LANES


def _pack_gains(d):
    row = jnp.concatenate([d[n].reshape(1, ln).astype(F32) for n, ln in GAINS], axis=1)
    return jnp.pad(row, ((0, 0), (0, GAIN_PAD - GAIN_LEN)))


def _unpack_gains(row):
    out, off = {}, 0
    for n, ln in GAINS:
        out[n] = row[0, off:off + ln]
        off += ln
    return out


def _unshard(pieces, axis):
    if axis == 0:
        return pieces.reshape((N_DEV * pieces.shape[1], pieces.shape[2]))
    return jnp.concatenate([pieces[p] for p in range(N_DEV)], axis=1)


def _reshard(full, axis):
    if axis == 0:
        return full.reshape((N_DEV, full.shape[0] // N_DEV, full.shape[1]))
    c = full.shape[1] // N_DEV
    return jnp.stack([full[:, c * p:c * (p + 1)] for p in range(N_DEV)])


def _all_gather(shards):
    n = len(shards)

    def body(*refs):
        x_refs, out_refs = refs[:n], refs[n:2 * n]
        send_sems, recv_sems, local_sems = refs[2 * n:]
        x, y, c = lax.axis_index("x"), lax.axis_index("y"), lax.axis_index("c")
        me, sibling = (x, y, c), (x, y, 1 - c)
        chips = [(1 - x, y), (x, 1 - y), (1 - x, 1 - y)]

        def slot(a, px, py, pc):
            return out_refs[a].at[4 * px + 2 * py + pc]

        def copy(a, k, block, to, from_input=False):
            return pltpu.make_async_remote_copy(
                src_ref=x_refs[a] if from_input else slot(a, *block), dst_ref=slot(a, *block),
                send_sem=send_sems.at[a, k], recv_sem=recv_sems.at[a, k], device_id=to, device_id_type=pl.DeviceIdType.MESH)

        mine = [pltpu.make_async_copy(x_refs[a], slot(a, *me), local_sems.at[a]) for a in range(n)]
        first = [copy(a, 0, me, sibling, True) for a in range(n)]
        first += [copy(a, 1 + j, me, (*chip, c), True) for j, chip in enumerate(chips) for a in range(n)]
        for cp in mine + first:
            cp.start()
        passed = []
        for j, chip in enumerate(chips):
            for a in range(n):
                copy(a, 1 + j, (*chip, c), me).wait_recv()
                passed.append(copy(a, 4 + j, (*chip, c), sibling))
                passed[-1].start()
        for a in range(n):
            copy(a, 0, sibling, me).wait_recv()
        for j, chip in enumerate(chips):
            for a in range(n):
                copy(a, 4 + j, (*chip, 1 - c), me).wait_recv()
        for cp in first + passed:
            cp.wait_send()
        for cp in mine:
            cp.wait()

    any_spec = pl.BlockSpec(memory_space=pl.ANY)
    return pl.pallas_call(
        body, name="all_gather_weights", out_shape=[jax.ShapeDtypeStruct((N_DEV,) + s.shape, s.dtype) for s in shards],
        in_specs=[any_spec] * n, out_specs=[any_spec] * n,
        scratch_shapes=[pltpu.SemaphoreType.DMA((n, 7)), pltpu.SemaphoreType.DMA((n, 7)), pltpu.SemaphoreType.DMA((n,))],
    )(*shards)


def _all_to_all(name, pieces):
    srcs, n = pieces, len(pieces)

    def body(*refs):
        in_refs, out_refs = refs[:n], refs[n:2 * n]
        send_sems, recv_sems, local_sems = refs[2 * n:]
        my_id = 4 * lax.axis_index("x") + 2 * lax.axis_index("y") + lax.axis_index("c")
        mine = [pltpu.make_async_copy(in_refs[a].at[my_id], out_refs[a].at[my_id], local_sems.at[a]) for a in range(n)]
        copies = _split_copies(in_refs, out_refs, send_sems, recv_sems, False)
        for cp in mine + copies:
            cp.start()
        for cp in copies:
            cp.wait_recv()
        for cp in copies:
            cp.wait_send()
        for cp in mine:
            cp.wait()

    any_spec = pl.BlockSpec(memory_space=pl.ANY)
    return pl.pallas_call(
        body, name=name, out_shape=[jax.ShapeDtypeStruct(s.shape, s.dtype) for s in srcs],
        in_specs=[any_spec] * n, out_specs=[any_spec] * n,
        scratch_shapes=[pltpu.SemaphoreType.DMA((7 * n,)), pltpu.SemaphoreType.DMA((7 * n,)), pltpu.SemaphoreType.DMA((n,))],
    )(*srcs)


def _flip_peers(x, y, c):
    flips = [(fx, fy, fc) for fx in (0, 1) for fy in (0, 1) for fc in (0, 1)][1:]
    return [(x ^ fx, y ^ fy, c ^ fc) for fx, fy, fc in flips]


def _split_copies(in_refs, land_refs, send_sems, recv_sems, gather):
    x, y, c = lax.axis_index("x"), lax.axis_index("y"), lax.axis_index("c")
    my_id = 4 * x + 2 * y + c
    copies = []
    for kk, p in enumerate(_flip_peers(x, y, c)):
        for a in range(len(in_refs)):
            src = in_refs[a] if gather else in_refs[a].at[4 * p[0] + 2 * p[1] + p[2]]
            copies.append(pltpu.make_async_remote_copy(
                src_ref=src, dst_ref=land_refs[a].at[my_id], send_sem=send_sems.at[a * 7 + kk], recv_sem=recv_sems.at[a * 7 + kk],
                device_id=p, device_id_type=pl.DeviceIdType.MESH))
    return copies


def _exchange_start(name, srcs, gather, after=None):
    n = len(srcs)
    first_out = 2 * n + (0 if after is None else 1)

    def body(*refs):
        for cp in _split_copies(refs[:n], refs[n:2 * n], refs[first_out], refs[first_out + 1], gather):
            cp.start()
        refs[-1][...] = jnp.zeros_like(refs[-1])

    hbm, sem = pl.BlockSpec(memory_space=pltpu.HBM), pl.BlockSpec(memory_space=pltpu.SEMAPHORE)
    land_shapes = [((N_DEV,) + s.shape if gather else s.shape, s.dtype) for s in srcs]
    lands = [pltpu.with_memory_space_constraint(lax.empty(shp, dt), pltpu.HBM) for shp, dt in land_shapes]
    srcs = [pltpu.with_memory_space_constraint(s, pltpu.HBM) for s in srcs]
    res = pl.pallas_call(
        body, name=name,
        out_shape=[pltpu.SemaphoreType.DMA((7 * n,)), pltpu.SemaphoreType.DMA((7 * n,))] + [pltpu.HBM(s.shape, s.dtype) for s in srcs]
        + [pltpu.HBM(shp, dt) for shp, dt in land_shapes] + [jax.ShapeDtypeStruct((8, LANES), F32)],
        in_specs=[hbm] * (2 * n) + ([] if after is None else [pl.BlockSpec(memory_space=pl.ANY)]),
        out_specs=[sem, sem] + [hbm] * (2 * n) + [pl.BlockSpec(memory_space=pltpu.VMEM)],
        input_output_aliases={i: 2 + i for i in range(2 * n)},
        compiler_params=pltpu.CompilerParams(has_side_effects=pltpu.SideEffectType.DATAFLOW_SIDE_EFFECTING),
    )(*srcs, *lands, *([] if after is None else [after]))
    return res[0], res[1], res[2:2 + n], res[2 + n:2 + 2 * n], res[-1]


def _exchange_wait(name, handles, after, gather):
    send_sems, recv_sems, srcs, lands, _ = handles
    n = len(srcs)

    def body(*refs):
        for cp in _split_copies(refs[:n], refs[n:2 * n], refs[2 * n], refs[2 * n + 1], gather):
            cp.wait_send()
            cp.wait_recv()

    hbm, sem = pl.BlockSpec(memory_space=pltpu.HBM), pl.BlockSpec(memory_space=pltpu.SEMAPHORE)
    res = pl.pallas_call(
        body, name=name, out_shape=[pltpu.HBM(t.shape, t.dtype) for t in list(srcs) + list(lands)],
        in_specs=[hbm] * (2 * n) + [sem, sem, pl.BlockSpec(memory_space=pl.ANY)], out_specs=[hbm] * (2 * n),
        input_output_aliases={i: i for i in range(2 * n)},
        compiler_params=pltpu.CompilerParams(has_side_effects=pltpu.SideEffectType.DATAFLOW_SIDE_EFFECTING),
    )(*srcs, *lands, send_sems, recv_sems, after)
    my_id = 4 * lax.axis_index("x") + 2 * lax.axis_index("y") + lax.axis_index("c")
    own = [s if gather else lax.dynamic_index_in_dim(s, my_id, 0, keepdims=False) for s in res[:n]]
    return [lax.dynamic_update_index_in_dim(land, o, my_id, 0) for land, o in zip(res[n:], own)]


def _adamw(name, parts, w, m, v):
    rows, cols = w.shape
    tr = _pick(rows, (128, 64, 32, 16, 8))
    pspec = pl.BlockSpec((N_DEV, tr, cols), lambda i: (0, i, 0))
    rspec = pl.BlockSpec((tr, cols), lambda i: (i, 0))

    def body(p_ref, w_ref, m_ref, v_ref, g_ref, d_ref, m2_ref, v2_ref):
        g, d, m2, v2 = _adamw_fn([p_ref[s] for s in range(N_DEV)], w_ref[...], m_ref[...], v_ref[...])
        g_ref[...], d_ref[...], m2_ref[...], v2_ref[...] = g, d, m2, v2

    return pl.pallas_call(
        body, name=name, grid=(rows // tr,), in_specs=[pspec, rspec, rspec, rspec], out_specs=[rspec] * 4,
        out_shape=[jax.ShapeDtypeStruct((rows, cols), F32)] * 4,
        compiler_params=pltpu.CompilerParams(dimension_semantics=("parallel",), vmem_limit_bytes=VMEM_LIMIT),
    )(parts, w, m, v)


def kernel(x, positions, g_mix, w_in, g_q_a, w_q_b, g_kv_a, w_kv_b, g_qn, g_kn, w_mla_out, ret_decay_fwd, ret_decay_bwd, w_ret_out, w_out, g_ffn, w_gate_up, w_down, loss_target, m_g_mix, m_w_in, m_g_q_a, m_w_q_b, m_g_kv_a, m_w_kv_b, m_g_qn, m_g_kn, m_w_mla_out, m_ret_decay_fwd, m_ret_decay_bwd, m_w_ret_out, m_w_out, m_g_ffn, m_w_gate_up, m_w_down, v_g_mix, v_w_in, v_g_q_a, v_w_q_b, v_g_kv_a, v_w_kv_b, v_g_qn, v_g_kn, v_w_mla_out, v_ret_decay_fwd, v_ret_decay_bwd, v_w_ret_out, v_w_out, v_g_ffn, v_w_gate_up, v_w_down):
    w = dict(g_mix=g_mix, w_in=w_in, g_q_a=g_q_a, w_q_b=w_q_b, g_kv_a=g_kv_a, w_kv_b=w_kv_b, g_qn=g_qn, g_kn=g_kn, w_mla_out=w_mla_out,
             ret_decay_fwd=ret_decay_fwd, ret_decay_bwd=ret_decay_bwd, w_ret_out=w_ret_out, w_out=w_out, g_ffn=g_ffn,
             w_gate_up=w_gate_up, w_down=w_down)
    m = dict(g_mix=m_g_mix, w_in=m_w_in, g_q_a=m_g_q_a, w_q_b=m_w_q_b, g_kv_a=m_g_kv_a, w_kv_b=m_w_kv_b, g_qn=m_g_qn, g_kn=m_g_kn,
             w_mla_out=m_w_mla_out, ret_decay_fwd=m_ret_decay_fwd, ret_decay_bwd=m_ret_decay_bwd, w_ret_out=m_w_ret_out, w_out=m_w_out,
             g_ffn=m_g_ffn, w_gate_up=m_w_gate_up, w_down=m_w_down)
    v = dict(g_mix=v_g_mix, w_in=v_w_in, g_q_a=v_g_q_a, w_q_b=v_w_q_b, g_kv_a=v_g_kv_a, w_kv_b=v_w_kv_b, g_qn=v_g_qn, g_kn=v_g_kn,
             w_mla_out=v_w_mla_out, ret_decay_fwd=v_ret_decay_fwd, ret_decay_bwd=v_ret_decay_bwd, w_ret_out=v_w_ret_out, w_out=v_w_out,
             g_ffn=v_g_ffn, w_gate_up=v_w_gate_up, w_down=v_w_down)
    gains = {n: w[n].reshape(1, ln) for n, ln in GAINS}

    axis_of = {n: axis for n, _, axis in MATS}
    later = [n for n, _, _ in MATS if n not in FIRST_WEIGHTS]
    gathered = _all_gather([w[n].astype(WIRE) for n in FIRST_WEIGHTS])
    W = {n: _unshard(g, axis_of[n]) for n, g in zip(FIRST_WEIGHTS, gathered)}
    later_handles = _exchange_start("gather_later_start", [w[n].astype(WIRE) for n in later], True, after=gathered[0])

    def late_weights(after):
        lands = _exchange_wait("gather_later_wait", later_handles, after, True)
        return {n: _unshard(g, axis_of[n]) for n, g in zip(later, lands)}

    grad_groups = []

    def grad_hook(g):
        names = tuple(g)
        handles = _exchange_start("grads_start_%d" % len(grad_groups), [_reshard(g[n], axis_of[n]).astype(GWIRE) for n in names], False)
        grad_groups.append((names, handles))
        return handles[4]

    S = x.shape[1]
    pos = positions.reshape(S, 1).astype(F32)
    loss_rows, grad_x, gG, gW = _local_step(x.reshape(S, D_MODEL), pos, loss_target.reshape(S, D_MODEL), gains, W, late_weights, grad_hook,
                                            start_after=later_handles[4])
    loss = lax.psum(jnp.sum(loss_rows), ("x", "y", "c"))

    last = [n for n, _, _ in MATS if n not in EARLY_GRADS + MID_GRADS]
    pieces = [_reshard(gW[n], axis_of[n]).astype(GWIRE) for n in last]
    pieces.append(jnp.broadcast_to(_pack_gains(gG)[None], (N_DEV, 1, GAIN_PAD)))
    late_parts = _all_to_all("grads_last", pieces)
    parts = dict(zip(last, late_parts))
    for i, (names, handles) in enumerate(grad_groups):
        parts.update(zip(names, _exchange_wait("grads_wait_%d" % i, handles, late_parts[-1], False)))
    out = [dict() for _ in range(4)]
    for n, _, _ in MATS:
        for o, r in zip(out, _adamw("adamw_" + n, parts[n], w[n], m[n], v[n])):
            o[n] = r
    for o, r in zip(out, _adamw("adamw_gains", late_parts[-1], _pack_gains(w), _pack_gains(m), _pack_gains(v))):
        o.update(_unpack_gains(r))
    return (loss, grad_x.reshape(x.shape), *[o[n] for o in out for n in ORDER])
```

```python
import functools

import numpy as np
import jax
import jax.numpy as jnp
from jax import lax
from jax.experimental import pallas as pl
from jax.experimental.pallas import tpu as pltpu

F32 = jnp.float32
MXU = jnp.bfloat16
WIRE = jnp.bfloat16
GWIRE = jnp.bfloat16

N_DEV = 8
D_MODEL = 1024
HEADS = 8
LANES = 128
Q_RANK, KV_RANK = 256, 128
NOPE, ROPE_M, V_M = 64, 32, 64
QK_M = NOPE + ROPE_M
RQK = 64
CHUNK = 128
FFN = 2816
THETA = 10000.0
EPS = 1e-6
LR, B1, B2, AEPS, WD, STEP = 0.001, 0.9, 0.999, 1e-08, 0.01, 10
VMEM_LIMIT = 56 * 1024 * 1024

NN = ((1,), (0,))
NT = ((1,), (1,))
TN = ((0,), (0,))

P_GATES, P_VR, P_GR, P_QR, P_KR, P_CQ, P_CKV, P_KROPE, P_WIDTH = 0, 2048, 3072, 4096, 4608, 5120, 5376, 5504, 5632
O_CQ, O_CKV, O_KROPE, O_QR, O_KR, O_VR, O_GR, O_GATES = 0, 256, 384, 416, 928, 1440, 2464, 3488


def _dot(a, b, dims):
    return lax.dot_general(a, b, (dims, ((), ())), preferred_element_type=F32)


def _pick(dim, cands):
    for c in cands:
        if dim % c == 0:
            return c
    return dim


def _pairs(t):
    return t.reshape(t.shape[0], 4, 2, 2, 32).transpose(0, 1, 3, 2, 4).reshape(t.shape[0], 512)


def _win_pad(w):
    z = jnp.zeros((w.shape[0], 48), w.dtype)
    kr = w[:, O_KROPE:O_KROPE + 32]
    return jnp.concatenate([w[:, O_GATES:], w[:, O_VR:O_VR + 1024], w[:, O_GR:O_GR + 1024], _pairs(w[:, O_QR:O_QR + 512]),
                            _pairs(w[:, O_KR:O_KR + 512]), w[:, :O_CKV], w[:, O_CKV:O_KROPE], kr[:, :16], z, kr[:, 16:], z], axis=1)


def _win_unpad(g):
    return jnp.concatenate([g[:, P_CQ:P_CQ + 256], g[:, P_CKV:P_CKV + 128], g[:, P_KROPE:P_KROPE + 16], g[:, P_KROPE + 64:P_KROPE + 80],
                            _pairs(g[:, P_QR:P_QR + 512]), _pairs(g[:, P_KR:P_KR + 512]), g[:, P_VR:P_VR + 1024],
                            g[:, P_GR:P_GR + 1024], g[:, P_GATES:P_GATES + 2048]], axis=1)


def _qk_pad(t):
    z = jnp.zeros(t.shape[:-1] + (32,), t.dtype)
    return jnp.concatenate([t[..., 64:80], t[..., 0:48], t[..., 80:96], t[..., 48:64], z], axis=-1)


def _qk_unpad(p):
    return jnp.concatenate([p[..., 16:64], p[..., 80:96], p[..., 0:16], p[..., 64:80]], axis=-1)


def _wq_pad(w):
    return _qk_pad(w.reshape(Q_RANK, HEADS, QK_M)).reshape(Q_RANK, HEADS * LANES)


def _wq_unpad(g):
    return _qk_unpad(g.reshape(Q_RANK, HEADS, LANES)).reshape(Q_RANK, HEADS * QK_M)


def _wkv_pad(w):
    t = w.reshape(KV_RANK, HEADS, NOPE + V_M)
    z = lambda n: jnp.zeros((KV_RANK, HEADS, n), w.dtype)
    wk = jnp.concatenate([z(16), t[..., 0:48], z(16), t[..., 48:64], z(32)], axis=-1)
    wv = jnp.concatenate([t[..., 64:128], z(64)], axis=-1)
    return wk.reshape(KV_RANK, HEADS * LANES), wv.reshape(KV_RANK, HEADS * LANES)


def _wkv_unpad(dwk, dwv):
    k, v = dwk.reshape(KV_RANK, HEADS, LANES), dwv.reshape(KV_RANK, HEADS, LANES)
    return jnp.concatenate([k[..., 16:64], k[..., 80:96], v[..., 0:64]], axis=-1).reshape(KV_RANK, HEADS * (NOPE + V_M))


def _wmla_pad(w):
    t = w.reshape(HEADS, V_M, D_MODEL)
    return jnp.concatenate([t, jnp.zeros_like(t)], axis=1).reshape(HEADS * LANES, D_MODEL)


def _wmla_unpad(g):
    return g.reshape(HEADS, LANES, D_MODEL)[:, :V_M].reshape(HEADS * V_M, D_MODEL)


def _rowwise(name, fn, rows, ts, ins, outs, accs=(), ncol=1):
    n_in, n_out, n_acc = len(ins), len(outs), len(accs)

    def colmap(col):
        if callable(col):
            return lambda i, j: (i, col(j))
        return lambda i, j: (i, col)

    arrays, in_specs = [], []
    for arr, spec in ins:
        arrays.append(arr)
        if spec is None:
            in_specs.append(pl.BlockSpec(arr.shape, functools.partial(lambda i, j, nd: (0,) * nd, nd=arr.ndim)))
        else:
            in_specs.append(pl.BlockSpec((ts, spec[0]), colmap(spec[1])))
    out_shape, out_specs = [], []
    for total, dtype, width, col in outs:
        out_shape.append(jax.ShapeDtypeStruct((rows, total), dtype))
        out_specs.append(pl.BlockSpec((ts, width), colmap(col)))
    for shp in accs:
        out_shape.append(jax.ShapeDtypeStruct(shp, F32))
        out_specs.append(pl.BlockSpec(shp, functools.partial(lambda i, j, nd: (0,) * nd, nd=len(shp))))

    def body(*refs):
        vals = [r[...] for r in refs[:n_in]]
        res = fn(*vals)
        if not isinstance(res, (tuple, list)):
            res = (res,)
        for r, v in zip(refs[n_in:n_in + n_out], res[:n_out]):
            r[...] = v.astype(r.dtype)
        if n_acc:
            first = jnp.logical_and(pl.program_id(0) == 0, pl.program_id(1) == 0)
            for r, v in zip(refs[n_in + n_out:], res[n_out:]):
                @pl.when(first)
                def _(r=r):
                    r[...] = jnp.zeros_like(r)
                r[...] += v.astype(F32)

    res = pl.pallas_call(
        body, name=name, grid=(rows // ts, ncol), in_specs=in_specs, out_specs=out_specs, out_shape=out_shape,
        compiler_params=pltpu.CompilerParams(dimension_semantics=("arbitrary", "arbitrary"), vmem_limit_bytes=VMEM_LIMIT),
    )(*arrays)
    return res


MM_OPERAND_BYTES = 24 * 1024 * 1024


def _mm(name, a, b, mode, add=None, after=None):
    a_halves, b_halves = a.ndim == 3, b.ndim == 3
    assert not a_halves or mode == "nt"
    assert not b_halves or mode == "tn"
    if mode == "nn":
        (M, K), N = a.shape, b.shape[1]
    elif mode == "nt":
        M, K, N = a.shape[-2], a.shape[-1] * (2 if a_halves else 1), b.shape[0]
    else:
        (K, M), N = a.shape, b.shape[-1] * (2 if b_halves else 1)
    tm = _pick(M, (1024, 512, 1408, 256, 128))
    tn = _pick(N // 2 if b_halves else N, (1408, 1024, 512, 256, 128))
    fits = lambda t: 2 * (tm + tn) * t * a.dtype.itemsize <= MM_OPERAND_BYTES
    kdiv = K // 2 if a_halves else K
    tk = next(t for t in (K, 4096, 2816, 2048, 1408, 1024, 512, 256, 128) if kdiv % t == 0 and (fits(t) or t == 128))
    nk = K // tk
    dims = {"nn": NN, "nt": NT, "tn": TN}[mode]
    if a_halves:
        per = kdiv // tk
        a_spec = pl.BlockSpec((None, tm, tk), lambda i, j, k: (k // per, i, k % per))
    else:
        a_spec = pl.BlockSpec((tk, tm), lambda i, j, k: (k, i)) if mode == "tn" else pl.BlockSpec((tm, tk), lambda i, j, k: (i, k))
    if b_halves:
        perj = (N // 2) // tn
        b_spec = pl.BlockSpec((None, tk, tn), lambda i, j, k: (j // perj, k, j % perj))
    else:
        b_spec = pl.BlockSpec((tn, tk), lambda i, j, k: (j, k)) if mode == "nt" else pl.BlockSpec((tk, tn), lambda i, j, k: (k, j))
    o_spec = pl.BlockSpec((tm, tn), lambda i, j, k: (i, j))
    has_add = add is not None

    def body(*refs):
        a_ref, b_ref, o_ref = refs[0], refs[1], refs[-1]
        d = _dot(a_ref[...], b_ref[...], dims)
        first = (d + refs[2][...]) if has_add else d
        if nk == 1:
            o_ref[...] = first
        else:
            k = pl.program_id(2)

            @pl.when(k == 0)
            def _():
                o_ref[...] = first

            @pl.when(k > 0)
            def _():
                o_ref[...] += d

    args = [a, b] + ([add] if has_add else []) + ([] if after is None else [after])
    specs = [a_spec, b_spec] + ([o_spec] if has_add else []) + ([] if after is None else [pl.BlockSpec(memory_space=pl.ANY)])
    return pl.pallas_call(
        body, name=name, grid=(M // tm, N // tn, nk), in_specs=specs, out_specs=o_spec,
        out_shape=jax.ShapeDtypeStruct((M, N), F32),
        compiler_params=pltpu.CompilerParams(dimension_semantics=("parallel", "parallel", "arbitrary"), vmem_limit_bytes=VMEM_LIMIT),
    )(*args)


def _mm_rows(name, a, b, fn, row_ins, whole_ins, outs, accs=(), mode="nn"):
    (M, K), N = a.shape, b.shape[1 if mode == "nn" else 0]
    tm = _pick(M, (512, 256, 128))
    n_in, n_out = 2 + len(row_ins) + len(whole_ins), len(outs)
    windows = [t if isinstance(t, tuple) else (t, (t.shape[1], 0)) for t in row_ins]
    row_ins = [t for t, _ in windows]
    row_specs = [pl.BlockSpec((tm, w), functools.partial(lambda i, col: (i, col), col=col)) for _, (w, col) in windows]

    def body(*refs):
        d = _dot(refs[0][...], refs[1][...], NN if mode == "nn" else NT)
        res = fn(d, *[r[...] for r in refs[2:n_in]])
        for r, v in zip(refs[n_in:n_in + n_out], res[:n_out]):
            r[...] = v.astype(r.dtype)
        for r, v in zip(refs[n_in + n_out:], res[n_out:]):
            @pl.when(pl.program_id(0) == 0)
            def _(r=r):
                r[...] = jnp.zeros_like(r)
            r[...] += v

    row = pl.BlockSpec((tm, N), lambda i: (i, 0))
    whole = lambda t: pl.BlockSpec(t.shape, functools.partial(lambda i, nd: (0,) * nd, nd=t.ndim))
    return pl.pallas_call(
        body, name=name, grid=(M // tm,),
        in_specs=[pl.BlockSpec((tm, K), lambda i: (i, 0)), whole(b)] + row_specs + [whole(t) for t in whole_ins],
        out_specs=[row] * n_out + [pl.BlockSpec(s, functools.partial(lambda i, nd: (0,) * nd, nd=len(s))) for s in accs],
        out_shape=[jax.ShapeDtypeStruct((M, N), dt) for dt in outs] + [jax.ShapeDtypeStruct(s, F32) for s in accs],
        compiler_params=pltpu.CompilerParams(dimension_semantics=("arbitrary",), vmem_limit_bytes=VMEM_LIMIT),
    )(a, b, *row_ins, *whole_ins)


def _ffn_tiles(S):
    return _pick(S, (1024, 512, 256, 128)), _pick(FFN, (1408, 704, 256, 128))


def _gate_up_swiglu(h2, wgu):
    S, K = h2.shape
    tm, tn = _ffn_tiles(S)
    nj = FFN // tn

    def body(a_ref, bg_ref, bu_ref, gu_ref, act_ref):
        a = a_ref[...]
        g, u = _dot(a, bg_ref[...], NN), _dot(a, bu_ref[...], NN)
        gu_ref[0], gu_ref[1] = g.astype(gu_ref.dtype), u.astype(gu_ref.dtype)
        act_ref[...] = _swiglu_fn(g, u).astype(act_ref.dtype)

    return pl.pallas_call(
        body, name="gate_up_swiglu", grid=(S // tm, nj),
        in_specs=[pl.BlockSpec((tm, K), lambda i, j: (i, 0)), pl.BlockSpec((K, tn), lambda i, j: (0, j)),
                  pl.BlockSpec((K, tn), lambda i, j: (0, nj + j))],
        out_specs=[pl.BlockSpec((2, tm, tn), lambda i, j: (0, i, j)), pl.BlockSpec((tm, tn), lambda i, j: (i, j))],
        out_shape=[jax.ShapeDtypeStruct((2, S, FFN), MXU), jax.ShapeDtypeStruct((S, FFN), MXU)],
        compiler_params=pltpu.CompilerParams(dimension_semantics=("parallel", "parallel"), vmem_limit_bytes=VMEM_LIMIT),
    )(h2, wgu, wgu)


def _d_act_swiglu(dx2, wdown, gu):
    S, K = dx2.shape
    tm, tn = _ffn_tiles(S)

    def body(a_ref, b_ref, gu_ref, o_ref):
        dact = _dot(a_ref[...], b_ref[...], NT)
        _, vjp = jax.vjp(_swiglu_fn, gu_ref[0].astype(F32), gu_ref[1].astype(F32))
        dg, du = vjp(dact)
        o_ref[0], o_ref[1] = dg.astype(o_ref.dtype), du.astype(o_ref.dtype)

    stacked = pl.BlockSpec((2, tm, tn), lambda i, j: (0, i, j))
    return pl.pallas_call(
        body, name="d_act_swiglu", grid=(S // tm, FFN // tn),
        in_specs=[pl.BlockSpec((tm, K), lambda i, j: (i, 0)), pl.BlockSpec((tn, K), lambda i, j: (j, 0)), stacked],
        out_specs=stacked, out_shape=jax.ShapeDtypeStruct((2, S, FFN), MXU),
        compiler_params=pltpu.CompilerParams(dimension_semantics=("parallel", "parallel"), vmem_limit_bytes=VMEM_LIMIT),
    )(dx2, wdown, gu)


@jax.custom_vjp
def _swap64(x):
    return pltpu.roll(x, 64, 1)


_swap64.defvjp(lambda x: (_swap64(x), None), lambda _, g: (_swap64(g),))


@jax.custom_vjp
def _mxdot(a, b):
    return _dot(a.astype(MXU), b.astype(MXU), NN)


def _mxdot_bwd(res, g):
    a, b = res
    gb = g.astype(MXU)
    return _dot(gb, b.astype(MXU), NT), _dot(a.astype(MXU), gb, TN)


_mxdot.defvjp(lambda a, b: (_mxdot(a, b), (a, b)), _mxdot_bwd)


def _row_sum(t):
    if t.shape[-1] == LANES:
        hi = t.astype(jnp.bfloat16)
        lo = (t - hi.astype(F32)).astype(jnp.bfloat16)
        ones = jnp.ones((LANES, LANES), jnp.bfloat16)
        return _dot(hi, ones, NN) + _dot(lo, ones, NN)
    return jnp.sum(t, axis=-1, keepdims=True)


@functools.partial(jax.custom_vjp, nondiff_argnums=(1,))
def _unit_rms(x, n):
    return x * lax.rsqrt(_row_sum(x * x) * (1.0 / n) + EPS)


def _unit_rms_fwd(x, n):
    r = lax.rsqrt(_row_sum(x * x) * (1.0 / n) + EPS)
    y = x * r
    return y, (y, r)


def _unit_rms_bwd(n, res, g):
    y, r = res
    return (r * (g - y * (_row_sum(g * y) * (1.0 / n))),)


_unit_rms.defvjp(_unit_rms_fwd, _unit_rms_bwd)


def _rms(x):
    return _unit_rms(x, x.shape[-1])


def _rmsg_fn(x, g):
    return _rms(x) * g


def _silu(x):
    return x * jax.nn.sigmoid(x)


def _tables_fn(pos, inv_m, sgn_m, inv_r, sgn_r):
    am, ar = pos * inv_m, pos * inv_r
    return jnp.cos(am), jnp.sin(am) * sgn_m, jnp.cos(ar), jnp.sin(ar) * sgn_r


def _head_blocks(t):
    return [t[:, LANES * h:LANES * (h + 1)] for h in range(t.shape[1] // LANES)]


def _mla_prep_fn(cq, ckv, kr, cosm, sinm, gqa, gkva, gqn, gkn, wq, wk, wv):
    cqn = _rms(cq) * gqa
    ckvn = _rms(ckv) * gkva
    q_raw = _mxdot(cqn, wq)
    k_raw = _mxdot(ckvn, wk)
    lane = lax.broadcasted_iota(jnp.int32, (1, HEADS * LANES), 1)
    v = _mxdot(ckvn, wv) + (lane % LANES == V_M).astype(F32)

    def norm_rope(blocks, g, extra):
        outs = []
        for b in blocks:
            if extra is not None:
                b = b + extra
            n = _unit_rms(b, QK_M) * g
            outs.append(n * cosm + _swap64(n) * sinm)
        return jnp.concatenate(outs, axis=1)

    q = norm_rope(_head_blocks(q_raw), gqn, None)
    k = norm_rope(_head_blocks(k_raw), gkn, kr)
    return q, k, v


def _ret_prep_fn(qr, kr, cosr, sinr):
    def rope(t, scale):
        return jnp.concatenate([(b * cosr + _swap64(b) * sinr) * scale for b in _head_blocks(t)], axis=1)
    return rope(qr, 1.0), rope(kr, RQK ** -0.5)


def _ret_post_fn(rf, rb, gr):
    ret = rf + rb
    outs = []
    for b, g in zip(_head_blocks(ret), _head_blocks(gr)):
        outs.append(_silu(g) * _rms(b))
    return jnp.concatenate(outs, axis=1)


def _merge_fn(ga, gb, ya, yb):
    return jax.nn.sigmoid(ga) * ya + jax.nn.sigmoid(gb) * yb


def _swiglu_fn(gate, up):
    return _silu(gate) * up


def _loss_fn(x2, tgt):
    d = x2 - tgt
    return d * (1.0 / D_MODEL), 0.5 * jnp.sum(d * d, axis=0, keepdims=True) * (1.0 / D_MODEL)


def _adamw_fn(parts, w, m, v):
    g = parts[0].astype(F32)
    for p in range(1, N_DEV):
        g = g + parts[p].astype(F32)
    m2 = B1 * m + (1.0 - B1) * g
    v2 = B2 * v + (1.0 - B2) * jnp.square(g)
    m_hat = m2 / (1.0 - B1 ** STEP)
    v_hat = v2 / (1.0 - B2 ** STEP)
    delta = -LR * (m_hat / (jnp.sqrt(v_hat) + AEPS) + WD * w)
    return g, delta, m2, v2


SCALE = QK_M ** -0.5
LOG2E = 1.4426950408889634
FLASH_ROWS = 32


def _flash_fwd(q, k, v):
    S = q.shape[0]
    tk = _pick(S, (512, 256, 128))
    tq = _pick(S, (1024, 512, 256, 128))
    ncb = tk // LANES
    nkv = S // tk
    mrows = 64
    c = SCALE * LOG2E

    def body(q_ref, k_ref, v_ref, o_ref, lse_ref, s_a, p_a, s_b, p_b, m_sc, a_sc, acc_sc):
        m_sc[...] = jnp.full_like(m_sc, -jnp.inf)
        acc_sc[...] = jnp.zeros_like(acc_sc)
        qb = q_ref[...]

        def scores(j, s_buf):
            s_buf[...] = _dot(qb, k_ref[j * tk:(j + 1) * tk, :], NT)

        def stage(j, s_buf, p_buf, s_next):
            if j + 1 < nkv:
                scores(j + 1, s_next)
            for r in range(tq // mrows):
                rows = slice(r * mrows, (r + 1) * mrows)
                cols = [s_buf[rows, LANES * cb:LANES * (cb + 1)] for cb in range(ncb)]
                m_prev = m_sc[rows, :]
                row_max = jnp.max(functools.reduce(jnp.maximum, cols), axis=-1, keepdims=True)
                m_new = jnp.maximum(m_prev, jnp.broadcast_to(row_max, (mrows, LANES)))
                a_sc[rows, :] = jnp.exp2((m_prev - m_new) * c)
                m_sc[rows, :] = m_new
                for cb in range(ncb):
                    p_buf[rows, LANES * cb:LANES * (cb + 1)] = jnp.exp2((cols[cb] - m_new) * c).astype(p_buf.dtype)
            acc_sc[...] = a_sc[...] * acc_sc[...] + _dot(p_buf[...], v_ref[j * tk:(j + 1) * tk, :], NN)

        scores(0, s_a)
        for j in range(nkv):
            stage(j, *((s_a, p_a, s_b) if j % 2 == 0 else (s_b, p_b, s_a)))
        acc = acc_sc[...]
        lane = lax.broadcasted_iota(jnp.int32, (1, LANES), 1)
        l = jnp.sum(jnp.where(lane == V_M, acc, 0.0), axis=-1, keepdims=True)
        o_ref[...] = (acc / l).astype(o_ref.dtype)
        lse_ref[...] = m_sc[...] * c + jnp.log2(jnp.broadcast_to(l, (tq, LANES)))

    qspec = pl.BlockSpec((tq, LANES), lambda h, i: (i, h))
    kspec = pl.BlockSpec((S, LANES), lambda h, i: (0, h))
    return pl.pallas_call(
        body, name="flash_fwd", grid=(HEADS, S // tq), in_specs=[qspec, kspec, kspec], out_specs=[qspec, qspec],
        out_shape=[jax.ShapeDtypeStruct((S, HEADS * LANES), MXU), jax.ShapeDtypeStruct((S, HEADS * LANES), F32)],
        scratch_shapes=[pltpu.VMEM((tq, tk), F32), pltpu.VMEM((tq, tk), MXU)] * 2 + [pltpu.VMEM((tq, LANES), F32)] * 3,
        compiler_params=pltpu.CompilerParams(dimension_semantics=("parallel", "arbitrary"), vmem_limit_bytes=VMEM_LIMIT),
    )(q, k, v)


def _delta_fn(o, do):
    outs = [jnp.broadcast_to(jnp.sum(a * b, axis=-1, keepdims=True), a.shape) for a, b in zip(_head_blocks(o), _head_blocks(do))]
    return do, jnp.concatenate(outs, axis=1)


def _flash_bwd(q, k, v, do, lse, delta):
    S = q.shape[0]
    tq = tk = _pick(S, (512, 256, 128))
    ncb = tk // LANES
    c = SCALE * LOG2E
    nq = S // tq
    nsub = 2 if (S // tk) % 2 == 0 else 1
    stages = [(sub, i) for sub in range(nsub) for i in range(nq)]

    def body(q_ref, k_ref, v_ref, do_ref, lse_ref, dl_ref, dq_ref, dk_ref, dv_ref, s_a, dp_a, p_a, ds_a, s_b, dp_b, p_b, ds_b):
        @pl.when(pl.program_id(1) == 0)
        def _():
            dq_ref[...] = jnp.zeros_like(dq_ref)

        dk_ref[...] = jnp.zeros_like(dk_ref)
        dv_ref[...] = jnp.zeros_like(dv_ref)
        bufs = [(s_a, dp_a, p_a, ds_a), (s_b, dp_b, p_b, ds_b)]

        def scores(sub, i, s_buf, dp_buf):
            kv_rows, q_rows = slice(sub * tk, (sub + 1) * tk), slice(i * tq, (i + 1) * tq)
            s_buf[...] = _dot(q_ref[q_rows, :], k_ref[kv_rows, :], NT)
            dp_buf[...] = _dot(do_ref[q_rows, :], v_ref[kv_rows, :], NT)

        scores(*stages[0], *bufs[0][:2])
        for t, (sub, i) in enumerate(stages):
            s_buf, dp_buf, p_buf, ds_buf = bufs[t % 2]
            if t + 1 < len(stages):
                scores(*stages[t + 1], *bufs[(t + 1) % 2][:2])
            for r in range(tq // FLASH_ROWS):
                rows = slice(r * FLASH_ROWS, (r + 1) * FLASH_ROWS)
                grows = slice(i * tq + r * FLASH_ROWS, i * tq + (r + 1) * FLASH_ROWS)
                lse_b, dl_b = lse_ref[grows, :], dl_ref[grows, :]
                for cb in range(ncb):
                    sl = slice(LANES * cb, LANES * (cb + 1))
                    p = jnp.exp2(s_buf[rows, sl] * c - lse_b)
                    p_buf[rows, sl] = p.astype(p_buf.dtype)
                    ds_buf[rows, sl] = (p * (dp_buf[rows, sl] - dl_b) * SCALE).astype(ds_buf.dtype)
            kv_rows, q_rows = slice(sub * tk, (sub + 1) * tk), slice(i * tq, (i + 1) * tq)
            dv_ref[kv_rows, :] += _dot(p_buf[...], do_ref[q_rows, :], TN)
            dk_ref[kv_rows, :] += _dot(ds_buf[...], q_ref[q_rows, :], TN)
            dq_ref[q_rows, :] += _dot(ds_buf[...], k_ref[kv_rows, :], NN)

    hspec = pl.BlockSpec((S, LANES), lambda h, j: (0, h))
    kspec = pl.BlockSpec((nsub * tk, LANES), lambda h, j: (j, h))
    full = jax.ShapeDtypeStruct((S, HEADS * LANES), F32)
    tile_bufs = [pltpu.VMEM((tq, tk), F32), pltpu.VMEM((tq, tk), F32), pltpu.VMEM((tq, tk), MXU), pltpu.VMEM((tq, tk), MXU)]
    return pl.pallas_call(
        body, name="flash_bwd", grid=(HEADS, S // (nsub * tk)), in_specs=[hspec, kspec, kspec, hspec, hspec, hspec],
        out_specs=[hspec, kspec, kspec], out_shape=[full, full, full],
        scratch_shapes=tile_bufs + tile_bufs,
        compiler_params=pltpu.CompilerParams(dimension_semantics=("parallel", "arbitrary"), vmem_limit_bytes=VMEM_LIMIT),
    )(q, k, v, do, lse, delta)


def _ret_consts(lgh, head, rev):
    C = CHUNK
    lane = lax.broadcasted_iota(jnp.int32, (1, LANES), 1)
    hm = ((lane // 32) % 2 == head % 2).astype(F32)
    r = lax.broadcasted_iota(jnp.int32, (C, C), 0)
    c = lax.broadcasted_iota(jnp.int32, (C, C), 1)
    diff = ((c - r) if rev else (r - c)).astype(F32)
    mask = (diff > 0) if rev else (diff >= 0)
    dpos = jnp.maximum(diff, 0.0)
    din = jnp.where(mask, jnp.exp(lgh * dpos), 0.0)
    idx = lax.broadcasted_iota(jnp.int32, (C, 1), 0).astype(F32)
    eq = (C - idx) if rev else (idx + 1.0)
    ek = idx if rev else (C - 1.0 - idx)
    qd, kd = jnp.exp(lgh * eq), jnp.exp(lgh * ek)
    cd = jnp.exp(lgh * jnp.full((1, 1), float(C), F32))
    return hm, din, dpos, qd, kd, cd, eq, ek


RET_HEADS_PER_STEP = 8


def _ret_fwd(name, qt, kt, proj, lg, rev):
    S = qt.shape[0]
    C = CHUNK
    TB = _pick(S, (512, 256, 128))
    cb, nb = TB // C, S // TB
    hps = RET_HEADS_PER_STEP
    blk = (lambda g: nb - 1 - g) if rev else (lambda g: g)

    def body(lg_ref, q_ref, k_ref, v_ref, o_ref, st_ref, state_sc):
        hg, g = pl.program_id(0), pl.program_id(1)

        @pl.when(g == 0)
        def _():
            state_sc[...] = jnp.zeros_like(state_sc)

        consts = [_ret_consts(lg_ref[hg * hps + u], u, rev) for u in range(hps)]
        order = list(reversed(range(cb))) if rev else list(range(cb))
        units = [(cc, u) for cc in order for u in range(hps)]

        def operands(cc, u):
            rows = pl.ds(cc * C, C)
            pair = slice(LANES * (u // 2), LANES * (u // 2 + 1))
            hm = consts[u][0]
            return q_ref[rows, pair] * hm, k_ref[rows, pair] * hm, v_ref[rows, LANES * u:LANES * (u + 1)].astype(MXU)

        a, inc = {}, {}
        for cc, u in units:
            q, k, v = operands(cc, u)
            a[cc, u] = _dot(q.astype(MXU), k.astype(MXU), NT) * consts[u][1]
            inc[cc, u] = _dot((k * consts[u][4]).astype(MXU), v, TN)
        for u in range(hps):
            st = state_sc[u]
            for cc in order:
                st_ref[u, cc] = st
                st = st * consts[u][5] + inc[cc, u]
            state_sc[u] = st
        for cc, u in units:
            q, _, v = operands(cc, u)
            cross = _dot((q * consts[u][3]).astype(MXU), st_ref[u, cc].astype(MXU), NN)
            o_ref[pl.ds(cc * C, C), LANES * u:LANES * (u + 1)] = _dot(a[cc, u].astype(MXU), v, NN) + cross

    qk_spec = pl.BlockSpec((TB, LANES * hps // 2), lambda h, g: (blk(g), h))
    return pl.pallas_call(
        body, name=name, grid=(HEADS // hps, nb),
        in_specs=[pl.BlockSpec(memory_space=pltpu.SMEM), qk_spec, qk_spec,
                  pl.BlockSpec((TB, LANES * hps), lambda h, g: (blk(g), P_VR // (LANES * hps) + h))],
        out_specs=[pl.BlockSpec((TB, LANES * hps), lambda h, g: (blk(g), h)),
                   pl.BlockSpec((hps, cb, LANES, LANES), lambda h, g: (h, blk(g), 0, 0))],
        out_shape=[jax.ShapeDtypeStruct((S, HEADS * LANES), F32), jax.ShapeDtypeStruct((HEADS, S // C, LANES, LANES), F32)],
        scratch_shapes=[pltpu.VMEM((hps, LANES, LANES), F32)],
        compiler_params=pltpu.CompilerParams(dimension_semantics=("parallel", "arbitrary"), vmem_limit_bytes=VMEM_LIMIT),
    )(lg, qt, kt, proj)


def _ret_bwd(name, qt, kt, proj, dret, states, lg, rev):
    S = qt.shape[0]
    C = CHUNK
    TB = _pick(S, (512, 256, 128))
    cb, nb = TB // C, S // TB
    hps = RET_HEADS_PER_STEP
    blk = (lambda g: g) if rev else (lambda g: nb - 1 - g)

    def body(lg_ref, q_ref, k_ref, v_ref, do_ref, st_ref, dq_ref, dk_ref, dv_ref, dlg_ref, ds_sc, acc_cc, acc_q, acc_k, acc_s):
        hg, g = pl.program_id(0), pl.program_id(1)

        @pl.when(g == 0)
        def _():
            ds_sc[...] = jnp.zeros_like(ds_sc)
            acc_cc[...] = jnp.zeros_like(acc_cc)
            acc_q[...] = jnp.zeros_like(acc_q)
            acc_k[...] = jnp.zeros_like(acc_k)
            acc_s[...] = jnp.zeros_like(acc_s)

        lgs = [lg_ref[hg * hps + u] for u in range(hps)]
        consts = [_ret_consts(lgs[u], u, rev) for u in range(hps)]
        order = list(range(cb)) if rev else list(reversed(range(cb)))
        units = [(cc, u) for cc in order for u in range(hps)]

        def operands(cc, u):
            rows = pl.ds(cc * C, C)
            pair = slice(LANES * (u // 2), LANES * (u // 2 + 1))
            head = slice(LANES * u, LANES * (u + 1))
            hm = consts[u][0]
            return q_ref[rows, pair] * hm, k_ref[rows, pair] * hm, v_ref[rows, head].astype(MXU), do_ref[rows, head].astype(MXU)

        a, dp, dqs, inc = {}, {}, {}, {}
        for cc, u in units:
            q, k, vb, dob = operands(cc, u)
            a[cc, u] = _dot(q.astype(MXU), k.astype(MXU), NT)
            dp[cc, u] = _dot(dob, vb, NT)
            dqs[cc, u] = _dot(dob, st_ref[u, cc].astype(MXU), NT)
            inc[cc, u] = _dot((q * consts[u][3]).astype(MXU), dob, TN)
        dsn = {}
        for u in range(hps):
            ds = ds_sc[u]
            for cc in order:
                dsn[cc, u] = ds
                ds = ds * consts[u][5] + inc[cc, u]
            ds_sc[u] = ds
        even = {}
        for cc, u in units:
            hm, din, dpos, qd, kd, cd, eq, ek = consts[u]
            rows, head = pl.ds(cc * C, C), slice(LANES * u, LANES * (u + 1))
            q, k, vb, dob = operands(cc, u)
            qb, kb = q.astype(MXU), k.astype(MXU)
            dsnb = dsn[cc, u].astype(MXU)
            da = (dp[cc, u] * din).astype(MXU)
            vds = _dot(vb, dsnb, NT)
            dq_u = (_dot(da, kb, NN) + dqs[cc, u] * qd) * hm
            dk_u = (_dot(da, qb, TN) + vds * kd) * hm
            if u % 2 == 0:
                even[cc] = (dq_u, dk_u)
            else:
                pair = slice(LANES * (u // 2), LANES * (u // 2 + 1))
                dq_ref[rows, pair] = even[cc][0] + dq_u
                dk_ref[rows, pair] = even[cc][1] + dk_u
            dv_ref[rows, head] = _dot((a[cc, u] * din).astype(MXU), dob, TN) + _dot((k * kd).astype(MXU), dsnb, NN)
            acc_cc[u] += dp[cc, u] * a[cc, u] * din * dpos
            acc_q[u] += dqs[cc, u] * q * (qd * eq)
            acc_k[u] += vds * k * (kd * ek)
            acc_s[u] += dsn[cc, u] * st_ref[u, cc] * (cd * float(C))

        @pl.when(g == nb - 1)
        def _():
            for u in range(hps):
                tot = (jnp.sum(acc_cc[u], keepdims=True) + jnp.sum(acc_q[u], keepdims=True)
                       + jnp.sum(acc_k[u], keepdims=True) + jnp.sum(acc_s[u], keepdims=True))
                dlg_ref[u] = jnp.broadcast_to(tot * lgs[u], (8, LANES))

    full = jax.ShapeDtypeStruct((S, HEADS * LANES), F32)
    hspec = pl.BlockSpec((TB, LANES * hps), lambda h, g: (blk(g), h))
    qk_spec = pl.BlockSpec((TB, LANES * hps // 2), lambda h, g: (blk(g), h))
    return pl.pallas_call(
        body, name=name, grid=(HEADS // hps, nb),
        in_specs=[pl.BlockSpec(memory_space=pltpu.SMEM), qk_spec, qk_spec,
                  pl.BlockSpec((TB, LANES * hps), lambda h, g: (blk(g), P_VR // (LANES * hps) + h)),
                  hspec,
                  pl.BlockSpec((hps, cb, LANES, LANES), lambda h, g: (h, blk(g), 0, 0))],
        out_specs=[qk_spec, qk_spec, hspec, pl.BlockSpec((hps, 8, LANES), lambda h, g: (h, 0, 0))],
        out_shape=[jax.ShapeDtypeStruct(qt.shape, F32), jax.ShapeDtypeStruct(kt.shape, F32), full,
                   jax.ShapeDtypeStruct((HEADS, 8, LANES), F32)],
        scratch_shapes=[pltpu.VMEM((hps, LANES, LANES), F32), pltpu.VMEM((hps, C, C), F32), pltpu.VMEM((hps, C, LANES), F32),
                        pltpu.VMEM((hps, C, LANES), F32), pltpu.VMEM((hps, LANES, LANES), F32)],
        compiler_params=pltpu.CompilerParams(dimension_semantics=("parallel", "arbitrary"), vmem_limit_bytes=VMEM_LIMIT),
    )(lg, qt, kt, proj, dret, states)


def _rope_consts():
    inv16 = THETA ** (-jnp.arange(16, dtype=F32) / 16)
    inv32 = THETA ** (-jnp.arange(32, dtype=F32) / 32)
    lane = np.arange(LANES)
    z48 = jnp.zeros((48,), F32)
    inv_m = jnp.concatenate([inv16, z48, inv16, z48])[None, :]
    sgn_m = jnp.asarray(np.where(lane < 16, -1.0, np.where((lane >= 64) & (lane < 80), 1.0, 0.0)), F32)[None, :]
    inv_r = jnp.concatenate([inv32] * 4)[None, :]
    sgn_r = jnp.asarray(np.where(lane < 64, -1.0, 1.0), F32)[None, :]
    return inv_m, sgn_m, inv_r, sgn_r


FIRST_WEIGHTS = ("w_in", "w_q_b", "w_kv_b")
EARLY_GRADS = ("w_down", "w_gate_up", "w_out", "w_ret_out")
MID_GRADS = ("w_mla_out", "w_in")


def _local_step(x, pos, tgt, gains, W, late_weights=None, grad_hook=None, start_after=None):
    S = x.shape[0]
    ts = _pick(S, (256, 128))
    ts_light = _pick(S, (512, 256, 128))
    R = lambda a, w=None, c=0: (a, ((a.shape[1] if w is None else w), c))
    W_ = lambda a: (a, None)

    win = _win_pad(W["w_in"])
    wq = _wq_pad(W["w_q_b"])
    wk, wv = _wkv_pad(W["w_kv_b"])
    gqn, gkn = _qk_pad(gains["g_qn"]), _qk_pad(gains["g_kn"])
    g_mix, g_q_a, g_kv_a, g_ffn = gains["g_mix"], gains["g_q_a"], gains["g_kv_a"], gains["g_ffn"]
    lg_f = -jnp.exp(gains["ret_decay_fwd"][0])
    lg_b = -jnp.exp(gains["ret_decay_bwd"][0])

    consts = list(_rope_consts())
    cosm, sinm, cosr, sinr = _rowwise("rope_tables", _tables_fn, S, ts_light,[R(pos)] + [W_(c) for c in consts],
                                      [(LANES, F32, LANES, 0)] * 4)

    (h,) = _rowwise("rms_mix", _rmsg_fn, S, ts_light,[R(x), W_(g_mix)], [(D_MODEL, MXU, D_MODEL, 0)])
    proj = _mm("in_proj", h, win, "nn", after=start_after)
    seg = lambda off, w: (proj, (w, off // w))
    mla_ins = [seg(P_CQ, 256), seg(P_CKV, 128), seg(P_KROPE, 128), R(cosm), R(sinm),
               W_(g_q_a), W_(g_kv_a), W_(gqn), W_(gkn), W_(wq), W_(wk), W_(wv)]
    q, k, v = _rowwise("mla_prep", _mla_prep_fn, S, ts, mla_ins, [(HEADS * LANES, MXU, HEADS * LANES, 0)] * 3)
    o_bf, lse = _flash_fwd(q, k, v)
    if late_weights is not None:
        W = {**W, **late_weights(lse)}
    wmla = _wmla_pad(W["w_mla_out"])
    wret, wout, wgu, wdown = W["w_ret_out"], W["w_out"], W["w_gate_up"], W["w_down"]
    y_a = _mm("mla_out", o_bf, wmla, "nn")

    ret_ins = [seg(P_QR, 512), seg(P_KR, 512), R(cosr), R(sinr)]
    qt, kt = _rowwise("ret_prep", _ret_prep_fn, S, ts_light,ret_ins, [(512, F32, 512, 0)] * 2)
    ret_f, st_f = _ret_fwd("ret_fwd_f", qt, kt, proj, lg_f, False)
    ret_b, st_b = _ret_fwd("ret_fwd_b", qt, kt, proj, lg_b, True)
    post_ins = [R(ret_f), R(ret_b), seg(P_GR, 1024)]
    (o_b,) = _rowwise("ret_post", _ret_post_fn, S, ts_light,post_ins, [(1024, MXU, 1024, 0)])
    y_b, merged = _mm_rows("ret_out_merge", o_b, wret, lambda yb, ga, gb, ya: (yb, _merge_fn(ga, gb, ya, yb)),
                           [seg(P_GATES, 1024), (proj, (1024, 1)), R(y_a)], [], [F32, MXU])
    merge_ins = [seg(P_GATES, 1024), (proj, (1024, 1)), R(y_a), R(y_b)]
    def residual_rms(d, xx, g):
        r = d + xx
        return r, _rmsg_fn(r, g)

    x1, h2 = _mm_rows("out_proj_rms_ffn", merged, wout, residual_rms, [x], [g_ffn], [F32, MXU])
    gu, act = _gate_up_swiglu(h2, wgu)

    def residual_loss(d, xx, t):
        dx, rows = _loss_fn(d + xx, t)
        return dx, dx, rows

    dx2, dx2_bf, loss_rows = _mm_rows("down_proj_loss", act, wdown, residual_loss, [x1, tgt], [], [F32, MXU], accs=[(1, D_MODEL)])

    gW = {}
    gW["w_down"] = _mm("d_w_down", act, dx2_bf, "tn")
    dgu = _d_act_swiglu(dx2_bf, wdown, gu)
    gW["w_gate_up"] = _mm("d_w_gate_up", h2, dgu, "tn")
    dh2 = _mm("d_h2", dgu, wgu, "nt")

    def rms_bwd(xx, g, dh, dres):
        _, vjp = jax.vjp(_rmsg_fn, xx, g)
        dx, dg = vjp(dh)
        dx = dx + dres
        return dx, dx, dg

    dx1, dx1_bf, dg_ffn = _rowwise("rms_ffn_bwd", rms_bwd, S, ts_light,[R(x1), W_(g_ffn), R(dh2), R(dx2)],
                                   [(D_MODEL, F32, D_MODEL, 0), (D_MODEL, MXU, D_MODEL, 0)], accs=[(1, D_MODEL)])
    gW["w_out"] = _mm("d_w_out", merged, dx1_bf, "tn")
    def merge_bwd(dm, ga, gb, ya, yb):
        _, vjp = jax.vjp(_merge_fn, ga, gb, ya, yb)
        return vjp(dm)

    dga, dgb, dy_a, dy_b = _mm_rows("d_merged_merge_bwd", dx1_bf, wout, merge_bwd, merge_ins, [], [MXU] * 4, mode="nt")
    gW["w_ret_out"] = _mm("d_w_ret_out", o_b, dy_b, "tn")
    after_early = [] if grad_hook is None else [grad_hook({n: gW[n] for n in EARLY_GRADS})]

    def post_bwd(dob, rf, rb, gr, *_):
        _, vjp = jax.vjp(_ret_post_fn, rf, rb, gr)
        drf, _, dgr = vjp(dob)
        return drf, dgr

    dret, dg_r = _mm_rows("d_o_b_ret_post_bwd", dy_b, wret, post_bwd, post_ins, after_early, [MXU, MXU], mode="nt")
    dq_f, dk_f, dv_f, dlg_f = _ret_bwd("ret_bwd_f", qt, kt, proj, dret, st_f, lg_f, False)
    dq_b, dk_b, dv_b, dlg_b = _ret_bwd("ret_bwd_b", qt, kt, proj, dret, st_b, lg_b, True)

    def ret_prep_bwd(qr, kr, cosr_, sinr_, dqf, dqb, dkf, dkb, dvf, dvb):
        _, vjp = jax.vjp(lambda a, b: _ret_prep_fn(a, b, cosr_, sinr_), qr, kr)
        dqr, dkr = vjp((dqf + dqb, dkf + dkb))
        return dqr, dkr, dvf + dvb

    dq_r, dk_r, dv_r = _rowwise("ret_prep_bwd", ret_prep_bwd, S, ts_light,ret_ins + [R(t) for t in (dq_f, dq_b, dk_f, dk_b, dv_f, dv_b)],
                                [(512, MXU, 512, 0), (512, MXU, 512, 0), (1024, MXU, 1024, 0)])

    gW_mla_p = _mm("d_w_mla_out", o_bf, dy_a, "tn")
    do_bf, delta = _mm_rows("d_o_attn_delta", dy_a, wmla, lambda d, oo, *_: _delta_fn(oo.astype(F32), d), [o_bf], after_early, [MXU, F32], mode="nt")
    dq, dk, dv = _flash_bwd(q, k, v, do_bf, lse, delta)

    def mla_prep_bwd(cq, ckv, kr, cosm_, sinm_, gqa, gkva, gqn_, gkn_, wq_, wk_, wv_, dq_, dk_, dv_):
        f = lambda cq, ckv, kr, gqa, gkva, gqn_, gkn_, wq_, wk_, wv_: _mla_prep_fn(cq, ckv, kr, cosm_, sinm_, gqa, gkva, gqn_, gkn_, wq_, wk_, wv_)
        _, vjp = jax.vjp(f, cq, ckv, kr, gqa, gkva, gqn_, gkn_, wq_.astype(F32), wk_.astype(F32), wv_.astype(F32))
        return vjp((dq_, dk_, dv_))

    mb = _rowwise("mla_prep_bwd", mla_prep_bwd, S, ts, mla_ins + [R(dq), R(dk), R(dv)],
                  [(256, MXU, 256, 0), (128, MXU, 128, 0), (128, MXU, 128, 0)],
                  accs=[(1, 256), (1, 128), (1, LANES), (1, LANES), (256, HEADS * LANES), (128, HEADS * LANES), (128, HEADS * LANES)])
    dc_q, dc_kv, dk_rope, dg_q_a, dg_kv_a, dgqn_p, dgkn_p, dwq_p, dwk_p, dwv_p = mb

    dproj = jnp.concatenate([dga, dgb, dv_r, dg_r, dq_r, dk_r, dc_q, dc_kv, dk_rope], axis=1)
    gW["w_in"] = _win_unpad(_mm("d_w_in", h, dproj, "tn"))
    gW["w_mla_out"] = _wmla_unpad(gW_mla_p)
    after_mid = None if grad_hook is None else grad_hook({n: gW[n] for n in MID_GRADS})
    dh = _mm("d_h", dproj, win, "nt", after=after_mid)
    grad_x, dg_mix = _rowwise("rms_mix_bwd", lambda a, b, c, d, *_: rms_bwd(a, b, c, d)[1:], S, ts_light,
                              [R(x), W_(g_mix), R(dh), R(dx1)] + ([] if after_mid is None else [W_(after_mid)]),
                              [(D_MODEL, F32, D_MODEL, 0)], accs=[(1, D_MODEL)])
    gW["w_q_b"] = _wq_unpad(dwq_p)
    gW["w_kv_b"] = _wkv_unpad(dwk_p, dwv_p)
    gG = {"g_mix": dg_mix, "g_q_a": dg_q_a, "g_kv_a": dg_kv_a, "g_qn": _qk_unpad(dgqn_p),
          "g_kn": _qk_unpad(dgkn_p), "ret_decay_fwd": dlg_f[:, 0, 0][None, :], "ret_decay_bwd": dlg_b[:, 0, 0][None, :],
          "g_ffn": dg_ffn}
    return loss_rows, grad_x, gG, gW


MATS = [("w_in", (1024, 5536), 1), ("w_q_b", (256, 768), 1), ("w_kv_b", (128, 1024), 1), ("w_mla_out", (512, 1024), 1),
        ("w_ret_out", (1024, 1024), 0), ("w_out", (1024, 1024), 0), ("w_gate_up", (1024, 5632), 1), ("w_down", (2816, 1024), 0)]
GAINS = [("g_mix", 1024), ("g_q_a", 256), ("g_kv_a", 128), ("g_qn", 96), ("g_kn", 96), ("ret_decay_fwd", 8), ("ret_decay_bwd", 8),
         ("g_ffn", 1024)]
ORDER = ["g_mix", "w_in", "g_q_a", "w_q_b", "g_kv_a", "w_kv_b", "g_qn", "g_kn", "w_mla_out", "ret_decay_fwd", "ret_decay_bwd",
         "w_ret_out", "w_out", "g_ffn", "w_gate_up", "w_down"]
GAIN_LEN = sum(n for _, n in GAINS)
GAIN_PAD = -(-GAIN_LEN // LANES) * LANES


def _pack_gains(d):
    row = jnp.concatenate([d[n].reshape(1, ln).astype(F32) for n, ln in GAINS], axis=1)
    return jnp.pad(row, ((0, 0), (0, GAIN_PAD - GAIN_LEN)))


def _unpack_gains(row):
    out, off = {}, 0
    for n, ln in GAINS:
        out[n] = row[0, off:off + ln]
        off += ln
    return out


def _unshard(pieces, axis):
    if axis == 0:
        return pieces.reshape((N_DEV * pieces.shape[1], pieces.shape[2]))
    return jnp.concatenate([pieces[p] for p in range(N_DEV)], axis=1)


def _reshard(full, axis):
    if axis == 0:
        return full.reshape((N_DEV, full.shape[0] // N_DEV, full.shape[1]))
    c = full.shape[1] // N_DEV
    return jnp.stack([full[:, c * p:c * (p + 1)] for p in range(N_DEV)])


def _all_gather(shards):
    n = len(shards)

    def body(*refs):
        x_refs, out_refs = refs[:n], refs[n:2 * n]
        send_sems, recv_sems, local_sems = refs[2 * n:]
        x, y, c = lax.axis_index("x"), lax.axis_index("y"), lax.axis_index("c")
        me, sibling = (x, y, c), (x, y, 1 - c)
        chips = [(1 - x, y), (x, 1 - y), (1 - x, 1 - y)]

        def slot(a, px, py, pc):
            return out_refs[a].at[4 * px + 2 * py + pc]

        def copy(a, k, block, to, from_input=False):
            return pltpu.make_async_remote_copy(
                src_ref=x_refs[a] if from_input else slot(a, *block), dst_ref=slot(a, *block),
                send_sem=send_sems.at[a, k], recv_sem=recv_sems.at[a, k], device_id=to, device_id_type=pl.DeviceIdType.MESH)

        mine = [pltpu.make_async_copy(x_refs[a], slot(a, *me), local_sems.at[a]) for a in range(n)]
        first = [copy(a, 0, me, sibling, True) for a in range(n)]
        first += [copy(a, 1 + j, me, (*chip, c), True) for j, chip in enumerate(chips) for a in range(n)]
        for cp in mine + first:
            cp.start()
        passed = []
        for j, chip in enumerate(chips):
            for a in range(n):
                copy(a, 1 + j, (*chip, c), me).wait_recv()
                passed.append(copy(a, 4 + j, (*chip, c), sibling))
                passed[-1].start()
        for a in range(n):
            copy(a, 0, sibling, me).wait_recv()
        for j, chip in enumerate(chips):
            for a in range(n):
                copy(a, 4 + j, (*chip, 1 - c), me).wait_recv()
        for cp in first + passed:
            cp.wait_send()
        for cp in mine:
            cp.wait()

    any_spec = pl.BlockSpec(memory_space=pl.ANY)
    return pl.pallas_call(
        body, name="all_gather_weights", out_shape=[jax.ShapeDtypeStruct((N_DEV,) + s.shape, s.dtype) for s in shards],
        in_specs=[any_spec] * n, out_specs=[any_spec] * n,
        scratch_shapes=[pltpu.SemaphoreType.DMA((n, 7)), pltpu.SemaphoreType.DMA((n, 7)), pltpu.SemaphoreType.DMA((n,))],
    )(*shards)


def _all_to_all(name, pieces):
    srcs, n = pieces, len(pieces)

    def body(*refs):
        in_refs, out_refs = refs[:n], refs[n:2 * n]
        send_sems, recv_sems, local_sems = refs[2 * n:]
        my_id = 4 * lax.axis_index("x") + 2 * lax.axis_index("y") + lax.axis_index("c")
        mine = [pltpu.make_async_copy(in_refs[a].at[my_id], out_refs[a].at[my_id], local_sems.at[a]) for a in range(n)]
        copies = _split_copies(in_refs, out_refs, send_sems, recv_sems, False)
        for cp in mine + copies:
            cp.start()
        for cp in copies:
            cp.wait_recv()
        for cp in copies:
            cp.wait_send()
        for cp in mine:
            cp.wait()

    any_spec = pl.BlockSpec(memory_space=pl.ANY)
    return pl.pallas_call(
        body, name=name, out_shape=[jax.ShapeDtypeStruct(s.shape, s.dtype) for s in srcs],
        in_specs=[any_spec] * n, out_specs=[any_spec] * n,
        scratch_shapes=[pltpu.SemaphoreType.DMA((7 * n,)), pltpu.SemaphoreType.DMA((7 * n,)), pltpu.SemaphoreType.DMA((n,))],
    )(*srcs)


def _flip_peers(x, y, c):
    flips = [(fx, fy, fc) for fx in (0, 1) for fy in (0, 1) for fc in (0, 1)][1:]
    return [(x ^ fx, y ^ fy, c ^ fc) for fx, fy, fc in flips]


def _split_copies(in_refs, land_refs, send_sems, recv_sems, gather):
    x, y, c = lax.axis_index("x"), lax.axis_index("y"), lax.axis_index("c")
    my_id = 4 * x + 2 * y + c
    copies = []
    for kk, p in enumerate(_flip_peers(x, y, c)):
        for a in range(len(in_refs)):
            src = in_refs[a] if gather else in_refs[a].at[4 * p[0] + 2 * p[1] + p[2]]
            copies.append(pltpu.make_async_remote_copy(
                src_ref=src, dst_ref=land_refs[a].at[my_id], send_sem=send_sems.at[a * 7 + kk], recv_sem=recv_sems.at[a * 7 + kk],
                device_id=p, device_id_type=pl.DeviceIdType.MESH))
    return copies


def _exchange_start(name, srcs, gather, after=None):
    n = len(srcs)
    first_out = 2 * n + (0 if after is None else 1)

    def body(*refs):
        for cp in _split_copies(refs[:n], refs[n:2 * n], refs[first_out], refs[first_out + 1], gather):
            cp.start()
        refs[-1][...] = jnp.zeros_like(refs[-1])

    hbm, sem = pl.BlockSpec(memory_space=pltpu.HBM), pl.BlockSpec(memory_space=pltpu.SEMAPHORE)
    land_shapes = [((N_DEV,) + s.shape if gather else s.shape, s.dtype) for s in srcs]
    lands = [pltpu.with_memory_space_constraint(lax.empty(shp, dt), pltpu.HBM) for shp, dt in land_shapes]
    srcs = [pltpu.with_memory_space_constraint(s, pltpu.HBM) for s in srcs]
    res = pl.pallas_call(
        body, name=name,
        out_shape=[pltpu.SemaphoreType.DMA((7 * n,)), pltpu.SemaphoreType.DMA((7 * n,))] + [pltpu.HBM(s.shape, s.dtype) for s in srcs]
        + [pltpu.HBM(shp, dt) for shp, dt in land_shapes] + [jax.ShapeDtypeStruct((8, LANES), F32)],
        in_specs=[hbm] * (2 * n) + ([] if after is None else [pl.BlockSpec(memory_space=pl.ANY)]),
        out_specs=[sem, sem] + [hbm] * (2 * n) + [pl.BlockSpec(memory_space=pltpu.VMEM)],
        input_output_aliases={i: 2 + i for i in range(2 * n)},
        compiler_params=pltpu.CompilerParams(has_side_effects=pltpu.SideEffectType.DATAFLOW_SIDE_EFFECTING),
    )(*srcs, *lands, *([] if after is None else [after]))
    return res[0], res[1], res[2:2 + n], res[2 + n:2 + 2 * n], res[-1]


def _exchange_wait(name, handles, after, gather):
    send_sems, recv_sems, srcs, lands, _ = handles
    n = len(srcs)

    def body(*refs):
        for cp in _split_copies(refs[:n], refs[n:2 * n], refs[2 * n], refs[2 * n + 1], gather):
            cp.wait_send()
            cp.wait_recv()

    hbm, sem = pl.BlockSpec(memory_space=pltpu.HBM), pl.BlockSpec(memory_space=pltpu.SEMAPHORE)
    res = pl.pallas_call(
        body, name=name, out_shape=[pltpu.HBM(t.shape, t.dtype) for t in list(srcs) + list(lands)],
        in_specs=[hbm] * (2 * n) + [sem, sem, pl.BlockSpec(memory_space=pl.ANY)], out_specs=[hbm] * (2 * n),
        input_output_aliases={i: i for i in range(2 * n)},
        compiler_params=pltpu.CompilerParams(has_side_effects=pltpu.SideEffectType.DATAFLOW_SIDE_EFFECTING),
    )(*srcs, *lands, send_sems, recv_sems, after)
    my_id = 4 * lax.axis_index("x") + 2 * lax.axis_index("y") + lax.axis_index("c")
    own = [s if gather else lax.dynamic_index_in_dim(s, my_id, 0, keepdims=False) for s in res[:n]]
    return [lax.dynamic_update_index_in_dim(land, o, my_id, 0) for land, o in zip(res[n:], own)]


def _adamw(name, parts, w, m, v):
    rows, cols = w.shape
    tr = _pick(rows, (128, 64, 32, 16, 8))
    pspec = pl.BlockSpec((N_DEV, tr, cols), lambda i: (0, i, 0))
    rspec = pl.BlockSpec((tr, cols), lambda i: (i, 0))

    def body(p_ref, w_ref, m_ref, v_ref, g_ref, d_ref, m2_ref, v2_ref):
        g, d, m2, v2 = _adamw_fn([p_ref[s] for s in range(N_DEV)], w_ref[...], m_ref[...], v_ref[...])
        g_ref[...], d_ref[...], m2_ref[...], v2_ref[...] = g, d, m2, v2

    return pl.pallas_call(
        body, name=name, grid=(rows // tr,), in_specs=[pspec, rspec, rspec, rspec], out_specs=[rspec] * 4,
        out_shape=[jax.ShapeDtypeStruct((rows, cols), F32)] * 4,
        compiler_params=pltpu.CompilerParams(dimension_semantics=("parallel",), vmem_limit_bytes=VMEM_LIMIT),
    )(parts, w, m, v)


def kernel(x, positions, g_mix, w_in, g_q_a, w_q_b, g_kv_a, w_kv_b, g_qn, g_kn, w_mla_out, ret_decay_fwd, ret_decay_bwd, w_ret_out, w_out, g_ffn, w_gate_up, w_down, loss_target, m_g_mix, m_w_in, m_g_q_a, m_w_q_b, m_g_kv_a, m_w_kv_b, m_g_qn, m_g_kn, m_w_mla_out, m_ret_decay_fwd, m_ret_decay_bwd, m_w_ret_out, m_w_out, m_g_ffn, m_w_gate_up, m_w_down, v_g_mix, v_w_in, v_g_q_a, v_w_q_b, v_g_kv_a, v_w_kv_b, v_g_qn, v_g_kn, v_w_mla_out, v_ret_decay_fwd, v_ret_decay_bwd, v_w_ret_out, v_w_out, v_g_ffn, v_w_gate_up, v_w_down):
    w = dict(g_mix=g_mix, w_in=w_in, g_q_a=g_q_a, w_q_b=w_q_b, g_kv_a=g_kv_a, w_kv_b=w_kv_b, g_qn=g_qn, g_kn=g_kn, w_mla_out=w_mla_out,
             ret_decay_fwd=ret_decay_fwd, ret_decay_bwd=ret_decay_bwd, w_ret_out=w_ret_out, w_out=w_out, g_ffn=g_ffn,
             w_gate_up=w_gate_up, w_down=w_down)
    m = dict(g_mix=m_g_mix, w_in=m_w_in, g_q_a=m_g_q_a, w_q_b=m_w_q_b, g_kv_a=m_g_kv_a, w_kv_b=m_w_kv_b, g_qn=m_g_qn, g_kn=m_g_kn,
             w_mla_out=m_w_mla_out, ret_decay_fwd=m_ret_decay_fwd, ret_decay_bwd=m_ret_decay_bwd, w_ret_out=m_w_ret_out, w_out=m_w_out,
             g_ffn=m_g_ffn, w_gate_up=m_w_gate_up, w_down=m_w_down)
    v = dict(g_mix=v_g_mix, w_in=v_w_in, g_q_a=v_g_q_a, w_q_b=v_w_q_b, g_kv_a=v_g_kv_a, w_kv_b=v_w_kv_b, g_qn=v_g_qn, g_kn=v_g_kn,
             w_mla_out=v_w_mla_out, ret_decay_fwd=v_ret_decay_fwd, ret_decay_bwd=v_ret_decay_bwd, w_ret_out=v_w_ret_out, w_out=v_w_out,
             g_ffn=v_g_ffn, w_gate_up=v_w_gate_up, w_down=v_w_down)
    gains = {n: w[n].reshape(1, ln) for n, ln in GAINS}

    axis_of = {n: axis for n, _, axis in MATS}
    later = [n for n, _, _ in MATS if n not in FIRST_WEIGHTS]
    gathered = _all_gather([w[n].astype(WIRE) for n in FIRST_WEIGHTS])
    W = {n: _unshard(g, axis_of[n]) for n, g in zip(FIRST_WEIGHTS, gathered)}
    later_handles = _exchange_start("gather_later_start", [w[n].astype(WIRE) for n in later], True, after=gathered[0])

    def late_weights(after):
        lands = _exchange_wait("gather_later_wait", later_handles, after, True)
        return {n: _unshard(g, axis_of[n]) for n, g in zip(later, lands)}

    grad_groups = []

    def grad_hook(g):
        names = tuple(g)
        handles = _exchange_start("grads_start_%d" % len(grad_groups), [_reshard(g[n], axis_of[n]).astype(GWIRE) for n in names], False)
        grad_groups.append((names, handles))
        return handles[4]

    S = x.shape[1]
    pos = positions.reshape(S, 1).astype(F32)
    loss_rows, grad_x, gG, gW = _local_step(x.reshape(S, D_MODEL), pos, loss_target.reshape(S, D_MODEL), gains, W, late_weights, grad_hook,
                                            start_after=later_handles[4])
    loss = lax.psum(jnp.sum(loss_rows), ("x", "y", "c"))

    last = [n for n, _, _ in MATS if n not in EARLY_GRADS + MID_GRADS]
    pieces = [_reshard(gW[n], axis_of[n]).astype(GWIRE) for n in last]
    pieces.append(jnp.broadcast_to(_pack_gains(gG)[None], (N_DEV, 1, GAIN_PAD)))
    late_parts = _all_to_all("grads_last", pieces)
    parts = dict(zip(last, late_parts))
    for i, (names, handles) in enumerate(grad_groups):
        parts.update(zip(names, _exchange_wait("grads_wait_%d" % i, handles, late_parts[-1], False)))
    out = [dict() for _ in range(4)]
    for n, _, _ in MATS:
        for o, r in zip(out, _adamw("adamw_" + n, parts[n], w[n], m[n], v[n])):
            o[n] = r
    for o, r in zip(out, _adamw("adamw_gains", late_parts[-1], _pack_gains(w), _pack_gains(m), _pack_gains(v))):
        o.update(_unpack_gains(r))
    return (loss, grad_x.reshape(x.shape), *[o[n] for o in out for n in ORDER])
```

```python
import functools

import numpy as np
import jax
import jax.numpy as jnp
from jax import lax
from jax.experimental import pallas as pl
from jax.experimental.pallas import tpu as pltpu

F32 = jnp.float32
MXU = jnp.bfloat16
WIRE = jnp.bfloat16
GWIRE = jnp.bfloat16

N_DEV = 8
D_MODEL = 1024
HEADS = 8
LANES = 128
Q_RANK, KV_RANK = 256, 128
NOPE, ROPE_M, V_M = 64, 32, 64
QK_M = NOPE + ROPE_M
RQK = 64
CHUNK = 128
FFN = 2816
THETA = 10000.0
EPS = 1e-6
LR, B1, B2, AEPS, WD, STEP = 0.001, 0.9, 0.999, 1e-08, 0.01, 10
VMEM_LIMIT = 56 * 1024 * 1024

NN = ((1,), (0,))
NT = ((1,), (1,))
TN = ((0,), (0,))

P_GATES, P_VR, P_GR, P_QR, P_KR, P_CQ, P_CKV, P_KROPE, P_WIDTH = 0, 2048, 3072, 4096, 4608, 5120, 5376, 5504, 5632
O_CQ, O_CKV, O_KROPE, O_QR, O_KR, O_VR, O_GR, O_GATES = 0, 256, 384, 416, 928, 1440, 2464, 3488


def _dot(a, b, dims):
    return lax.dot_general(a, b, (dims, ((), ())), preferred_element_type=F32)


def _pick(dim, cands):
    for c in cands:
        if dim % c == 0:
            return c
    return dim


def _pairs(t):
    return t.reshape(t.shape[0], 4, 2, 2, 32).transpose(0, 1, 3, 2, 4).reshape(t.shape[0], 512)


def _win_pad(w):
    z = jnp.zeros((w.shape[0], 48), w.dtype)
    kr = w[:, O_KROPE:O_KROPE + 32]
    return jnp.concatenate([w[:, O_GATES:], w[:, O_VR:O_VR + 1024], w[:, O_GR:O_GR + 1024], _pairs(w[:, O_QR:O_QR + 512]),
                            _pairs(w[:, O_KR:O_KR + 512]), w[:, :O_CKV], w[:, O_CKV:O_KROPE], kr[:, :16], z, kr[:, 16:], z], axis=1)


def _win_unpad(g):
    return jnp.concatenate([g[:, P_CQ:P_CQ + 256], g[:, P_CKV:P_CKV + 128], g[:, P_KROPE:P_KROPE + 16], g[:, P_KROPE + 64:P_KROPE + 80],
                            _pairs(g[:, P_QR:P_QR + 512]), _pairs(g[:, P_KR:P_KR + 512]), g[:, P_VR:P_VR + 1024],
                            g[:, P_GR:P_GR + 1024], g[:, P_GATES:P_GATES + 2048]], axis=1)


def _qk_pad(t):
    z = jnp.zeros(t.shape[:-1] + (32,), t.dtype)
    return jnp.concatenate([t[..., 64:80], t[..., 0:48], t[..., 80:96], t[..., 48:64], z], axis=-1)


def _qk_unpad(p):
    return jnp.concatenate([p[..., 16:64], p[..., 80:96], p[..., 0:16], p[..., 64:80]], axis=-1)


def _wq_pad(w):
    return _qk_pad(w.reshape(Q_RANK, HEADS, QK_M)).reshape(Q_RANK, HEADS * LANES)


def _wq_unpad(g):
    return _qk_unpad(g.reshape(Q_RANK, HEADS, LANES)).reshape(Q_RANK, HEADS * QK_M)


def _wkv_pad(w):
    t = w.reshape(KV_RANK, HEADS, NOPE + V_M)
    z = lambda n: jnp.zeros((KV_RANK, HEADS, n), w.dtype)
    wk = jnp.concatenate([z(16), t[..., 0:48], z(16), t[..., 48:64], z(32)], axis=-1)
    wv = jnp.concatenate([t[..., 64:128], z(64)], axis=-1)
    return wk.reshape(KV_RANK, HEADS * LANES), wv.reshape(KV_RANK, HEADS * LANES)


def _wkv_unpad(dwk, dwv):
    k, v = dwk.reshape(KV_RANK, HEADS, LANES), dwv.reshape(KV_RANK, HEADS, LANES)
    return jnp.concatenate([k[..., 16:64], k[..., 80:96], v[..., 0:64]], axis=-1).reshape(KV_RANK, HEADS * (NOPE + V_M))


def _wmla_pad(w):
    t = w.reshape(HEADS, V_M, D_MODEL)
    return jnp.concatenate([t, jnp.zeros_like(t)], axis=1).reshape(HEADS * LANES, D_MODEL)


def _wmla_unpad(g):
    return g.reshape(HEADS, LANES, D_MODEL)[:, :V_M].reshape(HEADS * V_M, D_MODEL)


def _rowwise(name, fn, rows, ts, ins, outs, accs=(), ncol=1):
    n_in, n_out, n_acc = len(ins), len(outs), len(accs)

    def colmap(col):
        if callable(col):
            return lambda i, j: (i, col(j))
        return lambda i, j: (i, col)

    arrays, in_specs = [], []
    for arr, spec in ins:
        arrays.append(arr)
        if spec is None:
            in_specs.append(pl.BlockSpec(arr.shape, functools.partial(lambda i, j, nd: (0,) * nd, nd=arr.ndim)))
        else:
            in_specs.append(pl.BlockSpec((ts, spec[0]), colmap(spec[1])))
    out_shape, out_specs = [], []
    for total, dtype, width, col in outs:
        out_shape.append(jax.ShapeDtypeStruct((rows, total), dtype))
        out_specs.append(pl.BlockSpec((ts, width), colmap(col)))
    for shp in accs:
        out_shape.append(jax.ShapeDtypeStruct(shp, F32))
        out_specs.append(pl.BlockSpec(shp, functools.partial(lambda i, j, nd: (0,) * nd, nd=len(shp))))

    def body(*refs):
        vals = [r[...] for r in refs[:n_in]]
        res = fn(*vals)
        if not isinstance(res, (tuple, list)):
            res = (res,)
        for r, v in zip(refs[n_in:n_in + n_out], res[:n_out]):
            r[...] = v.astype(r.dtype)
        if n_acc:
            first = jnp.logical_and(pl.program_id(0) == 0, pl.program_id(1) == 0)
            for r, v in zip(refs[n_in + n_out:], res[n_out:]):
                @pl.when(first)
                def _(r=r):
                    r[...] = jnp.zeros_like(r)
                r[...] += v.astype(F32)

    res = pl.pallas_call(
        body, name=name, grid=(rows // ts, ncol), in_specs=in_specs, out_specs=out_specs, out_shape=out_shape,
        compiler_params=pltpu.CompilerParams(dimension_semantics=("arbitrary", "arbitrary"), vmem_limit_bytes=VMEM_LIMIT),
    )(*arrays)
    return res


MM_OPERAND_BYTES = 24 * 1024 * 1024


def _mm(name, a, b, mode, add=None, after=None):
    b_halves = b.ndim == 3
    assert not b_halves or mode == "tn"
    if mode == "nn":
        (M, K), N = a.shape, b.shape[1]
    elif mode == "nt":
        (M, K), N = a.shape, b.shape[0]
    else:
        (K, M), N = a.shape, b.shape[-1] * (2 if b_halves else 1)
    tm = _pick(M, (1024, 512, 1408, 256, 128))
    tn = _pick(N // 2 if b_halves else N, (1408, 1024, 512, 256, 128))
    fits = lambda t: 2 * (tm + tn) * t * a.dtype.itemsize <= MM_OPERAND_BYTES
    tk = next(t for t in (K, 4096, 2816, 2048, 1408, 1024, 512, 256, 128) if K % t == 0 and (fits(t) or t == 128))
    nk = K // tk
    dims = {"nn": NN, "nt": NT, "tn": TN}[mode]
    a_spec = pl.BlockSpec((tk, tm), lambda i, j, k: (k, i)) if mode == "tn" else pl.BlockSpec((tm, tk), lambda i, j, k: (i, k))
    if b_halves:
        perj = (N // 2) // tn
        b_spec = pl.BlockSpec((None, tk, tn), lambda i, j, k: (j // perj, k, j % perj))
    else:
        b_spec = pl.BlockSpec((tn, tk), lambda i, j, k: (j, k)) if mode == "nt" else pl.BlockSpec((tk, tn), lambda i, j, k: (k, j))
    o_spec = pl.BlockSpec((tm, tn), lambda i, j, k: (i, j))
    has_add = add is not None

    def body(*refs):
        a_ref, b_ref, o_ref = refs[0], refs[1], refs[-1]
        d = _dot(a_ref[...], b_ref[...], dims)
        first = (d + refs[2][...]) if has_add else d
        if nk == 1:
            o_ref[...] = first
        else:
            k = pl.program_id(2)

            @pl.when(k == 0)
            def _():
                o_ref[...] = first

            @pl.when(k > 0)
            def _():
                o_ref[...] += d

    args = [a, b] + ([add] if has_add else []) + ([] if after is None else [after])
    specs = [a_spec, b_spec] + ([o_spec] if has_add else []) + ([] if after is None else [pl.BlockSpec(memory_space=pl.ANY)])
    return pl.pallas_call(
        body, name=name, grid=(M // tm, N // tn, nk), in_specs=specs, out_specs=o_spec,
        out_shape=jax.ShapeDtypeStruct((M, N), F32),
        compiler_params=pltpu.CompilerParams(dimension_semantics=("parallel", "parallel", "arbitrary"), vmem_limit_bytes=VMEM_LIMIT),
    )(*args)


def _mm_rows(name, a, b, fn, row_ins, whole_ins, outs, accs=(), mode="nn"):
    halves = a.ndim == 3
    assert not halves or mode == "nt"
    M, kh = a.shape[-2], a.shape[-1]
    N = b.shape[1 if mode == "nn" else 0]
    tm = _pick(M, (512, 256, 128))
    n_in, n_out = 2 + len(row_ins) + len(whole_ins), len(outs)
    windows = [t if isinstance(t, tuple) else (t, (t.shape[1], 0)) for t in row_ins]
    row_ins = [t for t, _ in windows]
    row_specs = [pl.BlockSpec((tm, w), functools.partial(lambda i, col: (i, col), col=col)) for _, (w, col) in windows]
    a_spec = pl.BlockSpec((2, tm, kh), lambda i: (0, i, 0)) if halves else pl.BlockSpec((tm, kh), lambda i: (i, 0))

    def body(*refs):
        if halves:
            d = _dot(refs[0][0], refs[1][:, :kh], NT) + _dot(refs[0][1], refs[1][:, kh:], NT)
        else:
            d = _dot(refs[0][...], refs[1][...], NN if mode == "nn" else NT)
        res = fn(d, *[r[...] for r in refs[2:n_in]])
        for r, v in zip(refs[n_in:n_in + n_out], res[:n_out]):
            r[...] = v.astype(r.dtype)
        for r, v in zip(refs[n_in + n_out:], res[n_out:]):
            @pl.when(pl.program_id(0) == 0)
            def _(r=r):
                r[...] = jnp.zeros_like(r)
            r[...] += v

    row = pl.BlockSpec((tm, N), lambda i: (i, 0))
    whole = lambda t: pl.BlockSpec(t.shape, functools.partial(lambda i, nd: (0,) * nd, nd=t.ndim))
    return pl.pallas_call(
        body, name=name, grid=(M // tm,),
        in_specs=[a_spec, whole(b)] + row_specs + [whole(t) for t in whole_ins],
        out_specs=[row] * n_out + [pl.BlockSpec(s, functools.partial(lambda i, nd: (0,) * nd, nd=len(s))) for s in accs],
        out_shape=[jax.ShapeDtypeStruct((M, N), dt) for dt in outs] + [jax.ShapeDtypeStruct(s, F32) for s in accs],
        compiler_params=pltpu.CompilerParams(dimension_semantics=("arbitrary",), vmem_limit_bytes=VMEM_LIMIT),
    )(a, b, *row_ins, *whole_ins)


def _ffn_tiles(S):
    return _pick(S, (1024, 512, 256, 128)), _pick(FFN, (1408, 704, 256, 128))


def _gate_up_swiglu(h2, wgu):
    S, K = h2.shape
    tm, tn = _ffn_tiles(S)
    nj = FFN // tn

    def body(a_ref, bg_ref, bu_ref, gu_ref, act_ref):
        a = a_ref[...]
        g, u = _dot(a, bg_ref[...], NN), _dot(a, bu_ref[...], NN)
        gu_ref[0], gu_ref[1] = g.astype(gu_ref.dtype), u.astype(gu_ref.dtype)
        act_ref[...] = _swiglu_fn(g, u).astype(act_ref.dtype)

    return pl.pallas_call(
        body, name="gate_up_swiglu", grid=(S // tm, nj),
        in_specs=[pl.BlockSpec((tm, K), lambda i, j: (i, 0)), pl.BlockSpec((K, tn), lambda i, j: (0, j)),
                  pl.BlockSpec((K, tn), lambda i, j: (0, nj + j))],
        out_specs=[pl.BlockSpec((2, tm, tn), lambda i, j: (0, i, j)), pl.BlockSpec((tm, tn), lambda i, j: (i, j))],
        out_shape=[jax.ShapeDtypeStruct((2, S, FFN), MXU), jax.ShapeDtypeStruct((S, FFN), MXU)],
        compiler_params=pltpu.CompilerParams(dimension_semantics=("parallel", "parallel"), vmem_limit_bytes=VMEM_LIMIT),
    )(h2, wgu, wgu)


def _d_act_swiglu(dx2, wdown, gu):
    S, K = dx2.shape
    tm, tn = _ffn_tiles(S)

    def body(a_ref, b_ref, gu_ref, o_ref):
        dact = _dot(a_ref[...], b_ref[...], NT)
        _, vjp = jax.vjp(_swiglu_fn, gu_ref[0].astype(F32), gu_ref[1].astype(F32))
        dg, du = vjp(dact)
        o_ref[0], o_ref[1] = dg.astype(o_ref.dtype), du.astype(o_ref.dtype)

    stacked = pl.BlockSpec((2, tm, tn), lambda i, j: (0, i, j))
    return pl.pallas_call(
        body, name="d_act_swiglu", grid=(S // tm, FFN // tn),
        in_specs=[pl.BlockSpec((tm, K), lambda i, j: (i, 0)), pl.BlockSpec((tn, K), lambda i, j: (j, 0)), stacked],
        out_specs=stacked, out_shape=jax.ShapeDtypeStruct((2, S, FFN), MXU),
        compiler_params=pltpu.CompilerParams(dimension_semantics=("parallel", "parallel"), vmem_limit_bytes=VMEM_LIMIT),
    )(dx2, wdown, gu)


@jax.custom_vjp
def _swap64(x):
    return pltpu.roll(x, 64, 1)


_swap64.defvjp(lambda x: (_swap64(x), None), lambda _, g: (_swap64(g),))


@jax.custom_vjp
def _mxdot(a, b):
    return _dot(a.astype(MXU), b.astype(MXU), NN)


def _mxdot_bwd(res, g):
    a, b = res
    gb = g.astype(MXU)
    return _dot(gb, b.astype(MXU), NT), _dot(a.astype(MXU), gb, TN)


_mxdot.defvjp(lambda a, b: (_mxdot(a, b), (a, b)), _mxdot_bwd)


def _row_sum(t):
    if t.shape[-1] == LANES:
        hi = t.astype(jnp.bfloat16)
        lo = (t - hi.astype(F32)).astype(jnp.bfloat16)
        ones = jnp.ones((LANES, LANES), jnp.bfloat16)
        return _dot(hi, ones, NN) + _dot(lo, ones, NN)
    return jnp.sum(t, axis=-1, keepdims=True)


@functools.partial(jax.custom_vjp, nondiff_argnums=(1,))
def _unit_rms(x, n):
    return x * lax.rsqrt(_row_sum(x * x) * (1.0 / n) + EPS)


def _unit_rms_fwd(x, n):
    r = lax.rsqrt(_row_sum(x * x) * (1.0 / n) + EPS)
    y = x * r
    return y, (y, r)


def _unit_rms_bwd(n, res, g):
    y, r = res
    return (r * (g - y * (_row_sum(g * y) * (1.0 / n))),)


_unit_rms.defvjp(_unit_rms_fwd, _unit_rms_bwd)


def _rms(x):
    return _unit_rms(x, x.shape[-1])


def _rmsg_fn(x, g):
    return _rms(x) * g


def _silu(x):
    return x * jax.nn.sigmoid(x)


def _tables_fn(pos, inv_m, sgn_m, inv_r, sgn_r):
    am, ar = pos * inv_m, pos * inv_r
    return jnp.cos(am), jnp.sin(am) * sgn_m, jnp.cos(ar), jnp.sin(ar) * sgn_r


def _head_blocks(t):
    return [t[:, LANES * h:LANES * (h + 1)] for h in range(t.shape[1] // LANES)]


def _mla_prep_fn(cq, ckv, kr, cosm, sinm, gqa, gkva, gqn, gkn, wq, wk, wv):
    cqn = _rms(cq) * gqa
    ckvn = _rms(ckv) * gkva
    q_raw = _mxdot(cqn, wq)
    k_raw = _mxdot(ckvn, wk)
    lane = lax.broadcasted_iota(jnp.int32, (1, HEADS * LANES), 1)
    v = _mxdot(ckvn, wv) + (lane % LANES == V_M).astype(F32)

    def norm_rope(blocks, g, extra):
        outs = []
        for b in blocks:
            if extra is not None:
                b = b + extra
            n = _unit_rms(b, QK_M) * g
            outs.append(n * cosm + _swap64(n) * sinm)
        return jnp.concatenate(outs, axis=1)

    q = norm_rope(_head_blocks(q_raw), gqn, None)
    k = norm_rope(_head_blocks(k_raw), gkn, kr)
    return q, k, v


def _ret_prep_fn(qr, kr, cosr, sinr):
    def rope(t, scale):
        return jnp.concatenate([(b * cosr + _swap64(b) * sinr) * scale for b in _head_blocks(t)], axis=1)
    return rope(qr, 1.0), rope(kr, RQK ** -0.5)


def _ret_post_fn(rf, rb, gr):
    ret = rf + rb
    outs = []
    for b, g in zip(_head_blocks(ret), _head_blocks(gr)):
        outs.append(_silu(g) * _rms(b))
    return jnp.concatenate(outs, axis=1)


def _merge_fn(ga, gb, ya, yb):
    return jax.nn.sigmoid(ga) * ya + jax.nn.sigmoid(gb) * yb


def _swiglu_fn(gate, up):
    return _silu(gate) * up


def _loss_fn(x2, tgt):
    d = x2 - tgt
    return d * (1.0 / D_MODEL), 0.5 * jnp.sum(d * d, axis=0, keepdims=True) * (1.0 / D_MODEL)


def _adamw_fn(parts, w, m, v):
    g = parts[0].astype(F32)
    for p in range(1, N_DEV):
        g = g + parts[p].astype(F32)
    m2 = B1 * m + (1.0 - B1) * g
    v2 = B2 * v + (1.0 - B2) * jnp.square(g)
    m_hat = m2 / (1.0 - B1 ** STEP)
    v_hat = v2 / (1.0 - B2 ** STEP)
    delta = -LR * (m_hat / (jnp.sqrt(v_hat) + AEPS) + WD * w)
    return g, delta, m2, v2


SCALE = QK_M ** -0.5
LOG2E = 1.4426950408889634
FLASH_ROWS = 32


def _flash_fwd(q, k, v):
    S = q.shape[0]
    tk = _pick(S, (512, 256, 128))
    tq = _pick(S, (1024, 512, 256, 128))
    ncb = tk // LANES
    nkv = S // tk
    mrows = 64
    c = SCALE * LOG2E

    def body(q_ref, k_ref, v_ref, o_ref, lse_ref, s_a, p_a, s_b, p_b, m_sc, a_sc, acc_sc):
        m_sc[...] = jnp.full_like(m_sc, -jnp.inf)
        acc_sc[...] = jnp.zeros_like(acc_sc)
        qb = q_ref[...]

        def scores(j, s_buf):
            s_buf[...] = _dot(qb, k_ref[j * tk:(j + 1) * tk, :], NT)

        def stage(j, s_buf, p_buf, s_next):
            if j + 1 < nkv:
                scores(j + 1, s_next)
            for r in range(tq // mrows):
                rows = slice(r * mrows, (r + 1) * mrows)
                cols = [s_buf[rows, LANES * cb:LANES * (cb + 1)] for cb in range(ncb)]
                m_prev = m_sc[rows, :]
                row_max = jnp.max(functools.reduce(jnp.maximum, cols), axis=-1, keepdims=True)
                m_new = jnp.maximum(m_prev, jnp.broadcast_to(row_max, (mrows, LANES)))
                a_sc[rows, :] = jnp.exp2((m_prev - m_new) * c)
                m_sc[rows, :] = m_new
                for cb in range(ncb):
                    p_buf[rows, LANES * cb:LANES * (cb + 1)] = jnp.exp2((cols[cb] - m_new) * c).astype(p_buf.dtype)
            acc_sc[...] = a_sc[...] * acc_sc[...] + _dot(p_buf[...], v_ref[j * tk:(j + 1) * tk, :], NN)

        scores(0, s_a)
        for j in range(nkv):
            stage(j, *((s_a, p_a, s_b) if j % 2 == 0 else (s_b, p_b, s_a)))
        acc = acc_sc[...]
        lane = lax.broadcasted_iota(jnp.int32, (1, LANES), 1)
        l = jnp.sum(jnp.where(lane == V_M, acc, 0.0), axis=-1, keepdims=True)
        o_ref[...] = (acc / l).astype(o_ref.dtype)
        lse_ref[...] = m_sc[...] * c + jnp.log2(jnp.broadcast_to(l, (tq, LANES)))

    qspec = pl.BlockSpec((tq, LANES), lambda h, i: (i, h))
    kspec = pl.BlockSpec((S, LANES), lambda h, i: (0, h))
    return pl.pallas_call(
        body, name="flash_fwd", grid=(HEADS, S // tq), in_specs=[qspec, kspec, kspec], out_specs=[qspec, qspec],
        out_shape=[jax.ShapeDtypeStruct((S, HEADS * LANES), MXU), jax.ShapeDtypeStruct((S, HEADS * LANES), F32)],
        scratch_shapes=[pltpu.VMEM((tq, tk), F32), pltpu.VMEM((tq, tk), MXU)] * 2 + [pltpu.VMEM((tq, LANES), F32)] * 3,
        compiler_params=pltpu.CompilerParams(dimension_semantics=("parallel", "arbitrary"), vmem_limit_bytes=VMEM_LIMIT),
    )(q, k, v)


def _delta_fn(o, do):
    outs = [jnp.broadcast_to(jnp.sum(a * b, axis=-1, keepdims=True), a.shape) for a, b in zip(_head_blocks(o), _head_blocks(do))]
    return do, jnp.concatenate(outs, axis=1)


def _flash_bwd(q, k, v, do, lse, delta):
    S = q.shape[0]
    tq = tk = _pick(S, (512, 256, 128))
    ncb = tk // LANES
    c = SCALE * LOG2E
    nq = S // tq
    nsub = 2 if (S // tk) % 2 == 0 else 1
    stages = [(sub, i) for sub in range(nsub) for i in range(nq)]

    def body(q_ref, k_ref, v_ref, do_ref, lse_ref, dl_ref, dq_ref, dk_ref, dv_ref, s_a, dp_a, p_a, ds_a, s_b, dp_b, p_b, ds_b):
        @pl.when(pl.program_id(1) == 0)
        def _():
            dq_ref[...] = jnp.zeros_like(dq_ref)

        dk_ref[...] = jnp.zeros_like(dk_ref)
        dv_ref[...] = jnp.zeros_like(dv_ref)
        bufs = [(s_a, dp_a, p_a, ds_a), (s_b, dp_b, p_b, ds_b)]

        def scores(sub, i, s_buf, dp_buf):
            kv_rows, q_rows = slice(sub * tk, (sub + 1) * tk), slice(i * tq, (i + 1) * tq)
            s_buf[...] = _dot(q_ref[q_rows, :], k_ref[kv_rows, :], NT)
            dp_buf[...] = _dot(do_ref[q_rows, :], v_ref[kv_rows, :], NT)

        scores(*stages[0], *bufs[0][:2])
        for t, (sub, i) in enumerate(stages):
            s_buf, dp_buf, p_buf, ds_buf = bufs[t % 2]
            if t + 1 < len(stages):
                scores(*stages[t + 1], *bufs[(t + 1) % 2][:2])
            for r in range(tq // FLASH_ROWS):
                rows = slice(r * FLASH_ROWS, (r + 1) * FLASH_ROWS)
                grows = slice(i * tq + r * FLASH_ROWS, i * tq + (r + 1) * FLASH_ROWS)
                lse_b, dl_b = lse_ref[grows, :], dl_ref[grows, :]
                for cb in range(ncb):
                    sl = slice(LANES * cb, LANES * (cb + 1))
                    p = jnp.exp2(s_buf[rows, sl] * c - lse_b)
                    p_buf[rows, sl] = p.astype(p_buf.dtype)
                    ds_buf[rows, sl] = (p * (dp_buf[rows, sl] - dl_b) * SCALE).astype(ds_buf.dtype)
            kv_rows, q_rows = slice(sub * tk, (sub + 1) * tk), slice(i * tq, (i + 1) * tq)
            dv_ref[kv_rows, :] += _dot(p_buf[...], do_ref[q_rows, :], TN)
            dk_ref[kv_rows, :] += _dot(ds_buf[...], q_ref[q_rows, :], TN)
            dq_ref[q_rows, :] += _dot(ds_buf[...], k_ref[kv_rows, :], NN)

    hspec = pl.BlockSpec((S, LANES), lambda h, j: (0, h))
    kspec = pl.BlockSpec((nsub * tk, LANES), lambda h, j: (j, h))
    full = jax.ShapeDtypeStruct((S, HEADS * LANES), F32)
    tile_bufs = [pltpu.VMEM((tq, tk), F32), pltpu.VMEM((tq, tk), F32), pltpu.VMEM((tq, tk), MXU), pltpu.VMEM((tq, tk), MXU)]
    return pl.pallas_call(
        body, name="flash_bwd", grid=(HEADS, S // (nsub * tk)), in_specs=[hspec, kspec, kspec, hspec, hspec, hspec],
        out_specs=[hspec, kspec, kspec], out_shape=[full, full, full],
        scratch_shapes=tile_bufs + tile_bufs,
        compiler_params=pltpu.CompilerParams(dimension_semantics=("parallel", "arbitrary"), vmem_limit_bytes=VMEM_LIMIT),
    )(q, k, v, do, lse, delta)


def _ret_consts(lgh, head, rev):
    C = CHUNK
    lane = lax.broadcasted_iota(jnp.int32, (1, LANES), 1)
    hm = ((lane // 32) % 2 == head % 2).astype(F32)
    r = lax.broadcasted_iota(jnp.int32, (C, C), 0)
    c = lax.broadcasted_iota(jnp.int32, (C, C), 1)
    diff = ((c - r) if rev else (r - c)).astype(F32)
    mask = (diff > 0) if rev else (diff >= 0)
    dpos = jnp.maximum(diff, 0.0)
    din = jnp.where(mask, jnp.exp(lgh * dpos), 0.0)
    idx = lax.broadcasted_iota(jnp.int32, (C, 1), 0).astype(F32)
    eq = (C - idx) if rev else (idx + 1.0)
    ek = idx if rev else (C - 1.0 - idx)
    qd, kd = jnp.exp(lgh * eq), jnp.exp(lgh * ek)
    cd = jnp.exp(lgh * jnp.full((1, 1), float(C), F32))
    return hm, din, dpos, qd, kd, cd, eq, ek


RET_HEADS_PER_STEP = 8


def _ret_fwd(name, qt, kt, proj, lg, rev):
    S = qt.shape[0]
    C = CHUNK
    TB = _pick(S, (512, 256, 128))
    cb, nb = TB // C, S // TB
    hps = RET_HEADS_PER_STEP
    blk = (lambda g: nb - 1 - g) if rev else (lambda g: g)

    def body(lg_ref, q_ref, k_ref, v_ref, o_ref, st_ref, state_sc):
        hg, g = pl.program_id(0), pl.program_id(1)

        @pl.when(g == 0)
        def _():
            state_sc[...] = jnp.zeros_like(state_sc)

        consts = [_ret_consts(lg_ref[hg * hps + u], u, rev) for u in range(hps)]
        order = list(reversed(range(cb))) if rev else list(range(cb))
        units = [(cc, u) for cc in order for u in range(hps)]

        def operands(cc, u):
            rows = pl.ds(cc * C, C)
            pair = slice(LANES * (u // 2), LANES * (u // 2 + 1))
            hm = consts[u][0]
            return q_ref[rows, pair] * hm, k_ref[rows, pair] * hm, v_ref[rows, LANES * u:LANES * (u + 1)].astype(MXU)

        a, inc = {}, {}
        for cc, u in units:
            q, k, v = operands(cc, u)
            a[cc, u] = _dot(q.astype(MXU), k.astype(MXU), NT) * consts[u][1]
            inc[cc, u] = _dot((k * consts[u][4]).astype(MXU), v, TN)
        for u in range(hps):
            st = state_sc[u]
            for cc in order:
                st_ref[u, cc] = st
                st = st * consts[u][5] + inc[cc, u]
            state_sc[u] = st
        for cc, u in units:
            q, _, v = operands(cc, u)
            cross = _dot((q * consts[u][3]).astype(MXU), st_ref[u, cc].astype(MXU), NN)
            o_ref[pl.ds(cc * C, C), LANES * u:LANES * (u + 1)] = _dot(a[cc, u].astype(MXU), v, NN) + cross

    qk_spec = pl.BlockSpec((TB, LANES * hps // 2), lambda h, g: (blk(g), h))
    return pl.pallas_call(
        body, name=name, grid=(HEADS // hps, nb),
        in_specs=[pl.BlockSpec(memory_space=pltpu.SMEM), qk_spec, qk_spec,
                  pl.BlockSpec((TB, LANES * hps), lambda h, g: (blk(g), P_VR // (LANES * hps) + h))],
        out_specs=[pl.BlockSpec((TB, LANES * hps), lambda h, g: (blk(g), h)),
                   pl.BlockSpec((hps, cb, LANES, LANES), lambda h, g: (h, blk(g), 0, 0))],
        out_shape=[jax.ShapeDtypeStruct((S, HEADS * LANES), F32), jax.ShapeDtypeStruct((HEADS, S // C, LANES, LANES), F32)],
        scratch_shapes=[pltpu.VMEM((hps, LANES, LANES), F32)],
        compiler_params=pltpu.CompilerParams(dimension_semantics=("parallel", "arbitrary"), vmem_limit_bytes=VMEM_LIMIT),
    )(lg, qt, kt, proj)


def _ret_bwd(name, qt, kt, proj, dret, states, lg, rev):
    S = qt.shape[0]
    C = CHUNK
    TB = _pick(S, (512, 256, 128))
    cb, nb = TB // C, S // TB
    hps = RET_HEADS_PER_STEP
    blk = (lambda g: g) if rev else (lambda g: nb - 1 - g)

    def body(lg_ref, q_ref, k_ref, v_ref, do_ref, st_ref, dq_ref, dk_ref, dv_ref, dlg_ref, ds_sc, acc_cc, acc_q, acc_k, acc_s):
        hg, g = pl.program_id(0), pl.program_id(1)

        @pl.when(g == 0)
        def _():
            ds_sc[...] = jnp.zeros_like(ds_sc)
            acc_cc[...] = jnp.zeros_like(acc_cc)
            acc_q[...] = jnp.zeros_like(acc_q)
            acc_k[...] = jnp.zeros_like(acc_k)
            acc_s[...] = jnp.zeros_like(acc_s)

        lgs = [lg_ref[hg * hps + u] for u in range(hps)]
        consts = [_ret_consts(lgs[u], u, rev) for u in range(hps)]
        order = list(range(cb)) if rev else list(reversed(range(cb)))
        units = [(cc, u) for cc in order for u in range(hps)]

        def operands(cc, u):
            rows = pl.ds(cc * C, C)
            pair = slice(LANES * (u // 2), LANES * (u // 2 + 1))
            head = slice(LANES * u, LANES * (u + 1))
            hm = consts[u][0]
            return q_ref[rows, pair] * hm, k_ref[rows, pair] * hm, v_ref[rows, head].astype(MXU), do_ref[rows, head].astype(MXU)

        a, dp, dqs, inc = {}, {}, {}, {}
        for cc, u in units:
            q, k, vb, dob = operands(cc, u)
            a[cc, u] = _dot(q.astype(MXU), k.astype(MXU), NT)
            dp[cc, u] = _dot(dob, vb, NT)
            dqs[cc, u] = _dot(dob, st_ref[u, cc].astype(MXU), NT)
            inc[cc, u] = _dot((q * consts[u][3]).astype(MXU), dob, TN)
        dsn = {}
        for u in range(hps):
            ds = ds_sc[u]
            for cc in order:
                dsn[cc, u] = ds
                ds = ds * consts[u][5] + inc[cc, u]
            ds_sc[u] = ds
        even = {}
        for cc, u in units:
            hm, din, dpos, qd, kd, cd, eq, ek = consts[u]
            rows, head = pl.ds(cc * C, C), slice(LANES * u, LANES * (u + 1))
            q, k, vb, dob = operands(cc, u)
            qb, kb = q.astype(MXU), k.astype(MXU)
            dsnb = dsn[cc, u].astype(MXU)
            da = (dp[cc, u] * din).astype(MXU)
            vds = _dot(vb, dsnb, NT)
            dq_u = (_dot(da, kb, NN) + dqs[cc, u] * qd) * hm
            dk_u = (_dot(da, qb, TN) + vds * kd) * hm
            if u % 2 == 0:
                even[cc] = (dq_u, dk_u)
            else:
                pair = slice(LANES * (u // 2), LANES * (u // 2 + 1))
                dq_ref[rows, pair] = even[cc][0] + dq_u
                dk_ref[rows, pair] = even[cc][1] + dk_u
            dv_ref[rows, head] = _dot((a[cc, u] * din).astype(MXU), dob, TN) + _dot((k * kd).astype(MXU), dsnb, NN)
            acc_cc[u] += dp[cc, u] * a[cc, u] * din * dpos
            acc_q[u] += dqs[cc, u] * q * (qd * eq)
            acc_k[u] += vds * k * (kd * ek)
            acc_s[u] += dsn[cc, u] * st_ref[u, cc] * (cd * float(C))

        @pl.when(g == nb - 1)
        def _():
            for u in range(hps):
                tot = (jnp.sum(acc_cc[u], keepdims=True) + jnp.sum(acc_q[u], keepdims=True)
                       + jnp.sum(acc_k[u], keepdims=True) + jnp.sum(acc_s[u], keepdims=True))
                dlg_ref[u] = jnp.broadcast_to(tot * lgs[u], (8, LANES))

    full = jax.ShapeDtypeStruct((S, HEADS * LANES), F32)
    hspec = pl.BlockSpec((TB, LANES * hps), lambda h, g: (blk(g), h))
    qk_spec = pl.BlockSpec((TB, LANES * hps // 2), lambda h, g: (blk(g), h))
    return pl.pallas_call(
        body, name=name, grid=(HEADS // hps, nb),
        in_specs=[pl.BlockSpec(memory_space=pltpu.SMEM), qk_spec, qk_spec,
                  pl.BlockSpec((TB, LANES * hps), lambda h, g: (blk(g), P_VR // (LANES * hps) + h)),
                  hspec,
                  pl.BlockSpec((hps, cb, LANES, LANES), lambda h, g: (h, blk(g), 0, 0))],
        out_specs=[qk_spec, qk_spec, hspec, pl.BlockSpec((hps, 8, LANES), lambda h, g: (h, 0, 0))],
        out_shape=[jax.ShapeDtypeStruct(qt.shape, F32), jax.ShapeDtypeStruct(kt.shape, F32), full,
                   jax.ShapeDtypeStruct((HEADS, 8, LANES), F32)],
        scratch_shapes=[pltpu.VMEM((hps, LANES, LANES), F32), pltpu.VMEM((hps, C, C), F32), pltpu.VMEM((hps, C, LANES), F32),
                        pltpu.VMEM((hps, C, LANES), F32), pltpu.VMEM((hps, LANES, LANES), F32)],
        compiler_params=pltpu.CompilerParams(dimension_semantics=("parallel", "arbitrary"), vmem_limit_bytes=VMEM_LIMIT),
    )(lg, qt, kt, proj, dret, states)


def _rope_consts():
    inv16 = THETA ** (-jnp.arange(16, dtype=F32) / 16)
    inv32 = THETA ** (-jnp.arange(32, dtype=F32) / 32)
    lane = np.arange(LANES)
    z48 = jnp.zeros((48,), F32)
    inv_m = jnp.concatenate([inv16, z48, inv16, z48])[None, :]
    sgn_m = jnp.asarray(np.where(lane < 16, -1.0, np.where((lane >= 64) & (lane < 80), 1.0, 0.0)), F32)[None, :]
    inv_r = jnp.concatenate([inv32] * 4)[None, :]
    sgn_r = jnp.asarray(np.where(lane < 64, -1.0, 1.0), F32)[None, :]
    return inv_m, sgn_m, inv_r, sgn_r


FIRST_WEIGHTS = ("w_in", "w_q_b", "w_kv_b")
EARLY_GRADS = ("w_down", "w_gate_up", "w_out", "w_ret_out")
MID_GRADS = ("w_mla_out", "w_in")


def _local_step(x, pos, tgt, gains, W, late_weights=None, grad_hook=None, start_after=None):
    S = x.shape[0]
    ts = _pick(S, (256, 128))
    ts_light = _pick(S, (512, 256, 128))
    R = lambda a, w=None, c=0: (a, ((a.shape[1] if w is None else w), c))
    W_ = lambda a: (a, None)

    win = _win_pad(W["w_in"])
    wq = _wq_pad(W["w_q_b"])
    wk, wv = _wkv_pad(W["w_kv_b"])
    gqn, gkn = _qk_pad(gains["g_qn"]), _qk_pad(gains["g_kn"])
    g_mix, g_q_a, g_kv_a, g_ffn = gains["g_mix"], gains["g_q_a"], gains["g_kv_a"], gains["g_ffn"]
    lg_f = -jnp.exp(gains["ret_decay_fwd"][0])
    lg_b = -jnp.exp(gains["ret_decay_bwd"][0])

    consts = list(_rope_consts())
    cosm, sinm, cosr, sinr = _rowwise("rope_tables", _tables_fn, S, ts_light,[R(pos)] + [W_(c) for c in consts],
                                      [(LANES, F32, LANES, 0)] * 4)

    (h,) = _rowwise("rms_mix", _rmsg_fn, S, ts_light,[R(x), W_(g_mix)], [(D_MODEL, MXU, D_MODEL, 0)])
    proj = _mm("in_proj", h, win, "nn", after=start_after)
    seg = lambda off, w: (proj, (w, off // w))
    mla_ins = [seg(P_CQ, 256), seg(P_CKV, 128), seg(P_KROPE, 128), R(cosm), R(sinm),
               W_(g_q_a), W_(g_kv_a), W_(gqn), W_(gkn), W_(wq), W_(wk), W_(wv)]
    q, k, v = _rowwise("mla_prep", _mla_prep_fn, S, ts, mla_ins, [(HEADS * LANES, MXU, HEADS * LANES, 0)] * 3)
    o_bf, lse = _flash_fwd(q, k, v)
    if late_weights is not None:
        W = {**W, **late_weights(lse)}
    wmla = _wmla_pad(W["w_mla_out"])
    wret, wout, wgu, wdown = W["w_ret_out"], W["w_out"], W["w_gate_up"], W["w_down"]
    y_a = _mm("mla_out", o_bf, wmla, "nn")

    ret_ins = [seg(P_QR, 512), seg(P_KR, 512), R(cosr), R(sinr)]
    qt, kt = _rowwise("ret_prep", _ret_prep_fn, S, ts_light,ret_ins, [(512, F32, 512, 0)] * 2)
    ret_f, st_f = _ret_fwd("ret_fwd_f", qt, kt, proj, lg_f, False)
    ret_b, st_b = _ret_fwd("ret_fwd_b", qt, kt, proj, lg_b, True)
    post_ins = [R(ret_f), R(ret_b), seg(P_GR, 1024)]
    (o_b,) = _rowwise("ret_post", _ret_post_fn, S, ts_light,post_ins, [(1024, MXU, 1024, 0)])
    y_b, merged = _mm_rows("ret_out_merge", o_b, wret, lambda yb, ga, gb, ya: (yb, _merge_fn(ga, gb, ya, yb)),
                           [seg(P_GATES, 1024), (proj, (1024, 1)), R(y_a)], [], [F32, MXU])
    merge_ins = [seg(P_GATES, 1024), (proj, (1024, 1)), R(y_a), R(y_b)]
    def residual_rms(d, xx, g):
        r = d + xx
        return r, _rmsg_fn(r, g)

    x1, h2 = _mm_rows("out_proj_rms_ffn", merged, wout, residual_rms, [x], [g_ffn], [F32, MXU])
    gu, act = _gate_up_swiglu(h2, wgu)

    def residual_loss(d, xx, t):
        dx, rows = _loss_fn(d + xx, t)
        return dx, dx, rows

    dx2, dx2_bf, loss_rows = _mm_rows("down_proj_loss", act, wdown, residual_loss, [x1, tgt], [], [F32, MXU], accs=[(1, D_MODEL)])

    gW = {}
    gW["w_down"] = _mm("d_w_down", act, dx2_bf, "tn")
    dgu = _d_act_swiglu(dx2_bf, wdown, gu)
    gW["w_gate_up"] = _mm("d_w_gate_up", h2, dgu, "tn")
    def rms_bwd(xx, g, dh, dres):
        _, vjp = jax.vjp(_rmsg_fn, xx, g)
        dx, dg = vjp(dh)
        dx = dx + dres
        return dx, dx, dg

    dx1, dx1_bf, dg_ffn = _mm_rows("d_h2_rms_ffn_bwd", dgu, wgu, lambda dh, xx, dres, g: rms_bwd(xx, g, dh, dres),
                                   [x1, dx2], [g_ffn], [F32, MXU], accs=[(1, D_MODEL)], mode="nt")
    gW["w_out"] = _mm("d_w_out", merged, dx1_bf, "tn")
    def merge_bwd(dm, ga, gb, ya, yb):
        _, vjp = jax.vjp(_merge_fn, ga, gb, ya, yb)
        return vjp(dm)

    dga, dgb, dy_a, dy_b = _mm_rows("d_merged_merge_bwd", dx1_bf, wout, merge_bwd, merge_ins, [], [MXU] * 4, mode="nt")
    gW["w_ret_out"] = _mm("d_w_ret_out", o_b, dy_b, "tn")
    after_early = [] if grad_hook is None else [grad_hook({n: gW[n] for n in EARLY_GRADS})]

    def post_bwd(dob, rf, rb, gr, *_):
        _, vjp = jax.vjp(_ret_post_fn, rf, rb, gr)
        drf, _, dgr = vjp(dob)
        return drf, dgr

    dret, dg_r = _mm_rows("d_o_b_ret_post_bwd", dy_b, wret, post_bwd, post_ins, after_early, [MXU, MXU], mode="nt")
    dq_f, dk_f, dv_f, dlg_f = _ret_bwd("ret_bwd_f", qt, kt, proj, dret, st_f, lg_f, False)
    dq_b, dk_b, dv_b, dlg_b = _ret_bwd("ret_bwd_b", qt, kt, proj, dret, st_b, lg_b, True)

    def ret_prep_bwd(qr, kr, cosr_, sinr_, dqf, dqb, dkf, dkb, dvf, dvb):
        _, vjp = jax.vjp(lambda a, b: _ret_prep_fn(a, b, cosr_, sinr_), qr, kr)
        dqr, dkr = vjp((dqf + dqb, dkf + dkb))
        return dqr, dkr, dvf + dvb

    dq_r, dk_r, dv_r = _rowwise("ret_prep_bwd", ret_prep_bwd, S, ts_light,ret_ins + [R(t) for t in (dq_f, dq_b, dk_f, dk_b, dv_f, dv_b)],
                                [(512, MXU, 512, 0), (512, MXU, 512, 0), (1024, MXU, 1024, 0)])

    gW_mla_p = _mm("d_w_mla_out", o_bf, dy_a, "tn")
    do_bf, delta = _mm_rows("d_o_attn_delta", dy_a, wmla, lambda d, oo, *_: _delta_fn(oo.astype(F32), d), [o_bf], after_early, [MXU, F32], mode="nt")
    dq, dk, dv = _flash_bwd(q, k, v, do_bf, lse, delta)

    def mla_prep_bwd(cq, ckv, kr, cosm_, sinm_, gqa, gkva, gqn_, gkn_, wq_, wk_, wv_, dq_, dk_, dv_):
        f = lambda cq, ckv, kr, gqa, gkva, gqn_, gkn_, wq_, wk_, wv_: _mla_prep_fn(cq, ckv, kr, cosm_, sinm_, gqa, gkva, gqn_, gkn_, wq_, wk_, wv_)
        _, vjp = jax.vjp(f, cq, ckv, kr, gqa, gkva, gqn_, gkn_, wq_.astype(F32), wk_.astype(F32), wv_.astype(F32))
        return vjp((dq_, dk_, dv_))

    mb = _rowwise("mla_prep_bwd", mla_prep_bwd, S, ts, mla_ins + [R(dq), R(dk), R(dv)],
                  [(256, MXU, 256, 0), (128, MXU, 128, 0), (128, MXU, 128, 0)],
                  accs=[(1, 256), (1, 128), (1, LANES), (1, LANES), (256, HEADS * LANES), (128, HEADS * LANES), (128, HEADS * LANES)])
    dc_q, dc_kv, dk_rope, dg_q_a, dg_kv_a, dgqn_p, dgkn_p, dwq_p, dwk_p, dwv_p = mb

    dproj = jnp.concatenate([dga, dgb, dv_r, dg_r, dq_r, dk_r, dc_q, dc_kv, dk_rope], axis=1)
    gW["w_in"] = _win_unpad(_mm("d_w_in", h, dproj, "tn"))
    gW["w_mla_out"] = _wmla_unpad(gW_mla_p)
    after_mid = None if grad_hook is None else grad_hook({n: gW[n] for n in MID_GRADS})
    grad_x, dg_mix = _mm_rows("d_h_rms_mix_bwd", dproj, win, lambda dh, xx, dres, g, *_: rms_bwd(xx, g, dh, dres)[1:],
                              [x, dx1], [g_mix] + ([] if after_mid is None else [after_mid]), [F32], accs=[(1, D_MODEL)], mode="nt")
    gW["w_q_b"] = _wq_unpad(dwq_p)
    gW["w_kv_b"] = _wkv_unpad(dwk_p, dwv_p)
    gG = {"g_mix": dg_mix, "g_q_a": dg_q_a, "g_kv_a": dg_kv_a, "g_qn": _qk_unpad(dgqn_p),
          "g_kn": _qk_unpad(dgkn_p), "ret_decay_fwd": dlg_f[:, 0, 0][None, :], "ret_decay_bwd": dlg_b[:, 0, 0][None, :],
          "g_ffn": dg_ffn}
    return loss_rows, grad_x, gG, gW


MATS = [("w_in", (1024, 5536), 1), ("w_q_b", (256, 768), 1), ("w_kv_b", (128, 1024), 1), ("w_mla_out", (512, 1024), 1),
        ("w_ret_out", (1024, 1024), 0), ("w_out", (1024, 1024), 0), ("w_gate_up", (1024, 5632), 1), ("w_down", (2816, 1024), 0)]
GAINS = [("g_mix", 1024), ("g_q_a", 256), ("g_kv_a", 128), ("g_qn", 96), ("g_kn", 96), ("ret_decay_fwd", 8), ("ret_decay_bwd", 8),
         ("g_ffn", 1024)]
ORDER = ["g_mix", "w_in", "g_q_a", "w_q_b", "g_kv_a", "w_kv_b", "g_qn", "g_kn", "w_mla_out", "ret_decay_fwd", "ret_decay_bwd",
         "w_ret_out", "w_out", "g_ffn", "w_gate_up", "w_down"]
GAIN_LEN = sum(n for _, n in GAINS)
GAIN_PAD = -(-GAIN_LEN // LANES) * LANES


def _pack_gains(d):
    row = jnp.concatenate([d[n].reshape(1, ln).astype(F32) for n, ln in GAINS], axis=1)
    return jnp.pad(row, ((0, 0), (0, GAIN_PAD - GAIN_LEN)))


def _unpack_gains(row):
    out, off = {}, 0
    for n, ln in GAINS:
        out[n] = row[0, off:off + ln]
        off += ln
    return out


def _unshard(pieces, axis):
    if axis == 0:
        return pieces.reshape((N_DEV * pieces.shape[1], pieces.shape[2]))
    return jnp.concatenate([pieces[p] for p in range(N_DEV)], axis=1)


def _reshard(full, axis):
    if axis == 0:
        return full.reshape((N_DEV, full.shape[0] // N_DEV, full.shape[1]))
    c = full.shape[1] // N_DEV
    return jnp.stack([full[:, c * p:c * (p + 1)] for p in range(N_DEV)])


def _all_gather(shards):
    n = len(shards)

    def body(*refs):
        x_refs, out_refs = refs[:n], refs[n:2 * n]
        send_sems, recv_sems, local_sems = refs[2 * n:]
        x, y, c = lax.axis_index("x"), lax.axis_index("y"), lax.axis_index("c")
        me, sibling = (x, y, c), (x, y, 1 - c)
        chips = [(1 - x, y), (x, 1 - y), (1 - x, 1 - y)]

        def slot(a, px, py, pc):
            return out_refs[a].at[4 * px + 2 * py + pc]

        def copy(a, k, block, to, from_input=False):
            return pltpu.make_async_remote_copy(
                src_ref=x_refs[a] if from_input else slot(a, *block), dst_ref=slot(a, *block),
                send_sem=send_sems.at[a, k], recv_sem=recv_sems.at[a, k], device_id=to, device_id_type=pl.DeviceIdType.MESH)

        mine = [pltpu.make_async_copy(x_refs[a], slot(a, *me), local_sems.at[a]) for a in range(n)]
        first = [copy(a, 0, me, sibling, True) for a in range(n)]
        first += [copy(a, 1 + j, me, (*chip, c), True) for j, chip in enumerate(chips) for a in range(n)]
        for cp in mine + first:
            cp.start()
        passed = []
        for j, chip in enumerate(chips):
            for a in range(n):
                copy(a, 1 + j, (*chip, c), me).wait_recv()
                passed.append(copy(a, 4 + j, (*chip, c), sibling))
                passed[-1].start()
        for a in range(n):
            copy(a, 0, sibling, me).wait_recv()
        for j, chip in enumerate(chips):
            for a in range(n):
                copy(a, 4 + j, (*chip, 1 - c), me).wait_recv()
        for cp in first + passed:
            cp.wait_send()
        for cp in mine:
            cp.wait()

    any_spec = pl.BlockSpec(memory_space=pl.ANY)
    return pl.pallas_call(
        body, name="all_gather_weights", out_shape=[jax.ShapeDtypeStruct((N_DEV,) + s.shape, s.dtype) for s in shards],
        in_specs=[any_spec] * n, out_specs=[any_spec] * n,
        scratch_shapes=[pltpu.SemaphoreType.DMA((n, 7)), pltpu.SemaphoreType.DMA((n, 7)), pltpu.SemaphoreType.DMA((n,))],
    )(*shards)


def _all_to_all(name, pieces):
    srcs, n = pieces, len(pieces)

    def body(*refs):
        in_refs, out_refs = refs[:n], refs[n:2 * n]
        send_sems, recv_sems, local_sems = refs[2 * n:]
        my_id = 4 * lax.axis_index("x") + 2 * lax.axis_index("y") + lax.axis_index("c")
        mine = [pltpu.make_async_copy(in_refs[a].at[my_id], out_refs[a].at[my_id], local_sems.at[a]) for a in range(n)]
        copies = _split_copies(in_refs, out_refs, send_sems, recv_sems, False)
        for cp in mine + copies:
            cp.start()
        for cp in copies:
            cp.wait_recv()
        for cp in copies:
            cp.wait_send()
        for cp in mine:
            cp.wait()

    any_spec = pl.BlockSpec(memory_space=pl.ANY)
    return pl.pallas_call(
        body, name=name, out_shape=[jax.ShapeDtypeStruct(s.shape, s.dtype) for s in srcs],
        in_specs=[any_spec] * n, out_specs=[any_spec] * n,
        scratch_shapes=[pltpu.SemaphoreType.DMA((7 * n,)), pltpu.SemaphoreType.DMA((7 * n,)), pltpu.SemaphoreType.DMA((n,))],
    )(*srcs)


def _flip_peers(x, y, c):
    flips = [(fx, fy, fc) for fx in (0, 1) for fy in (0, 1) for fc in (0, 1)][1:]
    return [(x ^ fx, y ^ fy, c ^ fc) for fx, fy, fc in flips]


def _split_copies(in_refs, land_refs, send_sems, recv_sems, gather):
    x, y, c = lax.axis_index("x"), lax.axis_index("y"), lax.axis_index("c")
    my_id = 4 * x + 2 * y + c
    copies = []
    for kk, p in enumerate(_flip_peers(x, y, c)):
        for a in range(len(in_refs)):
            src = in_refs[a] if gather else in_refs[a].at[4 * p[0] + 2 * p[1] + p[2]]
            copies.append(pltpu.make_async_remote_copy(
                src_ref=src, dst_ref=land_refs[a].at[my_id], send_sem=send_sems.at[a * 7 + kk], recv_sem=recv_sems.at[a * 7 + kk],
                device_id=p, device_id_type=pl.DeviceIdType.MESH))
    return copies


def _exchange_start(name, srcs, gather, after=None):
    n = len(srcs)
    first_out = 2 * n + (0 if after is None else 1)

    def body(*refs):
        for cp in _split_copies(refs[:n], refs[n:2 * n], refs[first_out], refs[first_out + 1], gather):
            cp.start()
        refs[-1][...] = jnp.zeros_like(refs[-1])

    hbm, sem = pl.BlockSpec(memory_space=pltpu.HBM), pl.BlockSpec(memory_space=pltpu.SEMAPHORE)
    land_shapes = [((N_DEV,) + s.shape if gather else s.shape, s.dtype) for s in srcs]
    lands = [pltpu.with_memory_space_constraint(lax.empty(shp, dt), pltpu.HBM) for shp, dt in land_shapes]
    srcs = [pltpu.with_memory_space_constraint(s, pltpu.HBM) for s in srcs]
    res = pl.pallas_call(
        body, name=name,
        out_shape=[pltpu.SemaphoreType.DMA((7 * n,)), pltpu.SemaphoreType.DMA((7 * n,))] + [pltpu.HBM(s.shape, s.dtype) for s in srcs]
        + [pltpu.HBM(shp, dt) for shp, dt in land_shapes] + [jax.ShapeDtypeStruct((8, LANES), F32)],
        in_specs=[hbm] * (2 * n) + ([] if after is None else [pl.BlockSpec(memory_space=pl.ANY)]),
        out_specs=[sem, sem] + [hbm] * (2 * n) + [pl.BlockSpec(memory_space=pltpu.VMEM)],
        input_output_aliases={i: 2 + i for i in range(2 * n)},
        compiler_params=pltpu.CompilerParams(has_side_effects=pltpu.SideEffectType.DATAFLOW_SIDE_EFFECTING),
    )(*srcs, *lands, *([] if after is None else [after]))
    return res[0], res[1], res[2:2 + n], res[2 + n:2 + 2 * n], res[-1]


def _exchange_wait(name, handles, after, gather):
    send_sems, recv_sems, srcs, lands, _ = handles
    n = len(srcs)

    def body(*refs):
        for cp in _split_copies(refs[:n], refs[n:2 * n], refs[2 * n], refs[2 * n + 1], gather):
            cp.wait_send()
            cp.wait_recv()

    hbm, sem = pl.BlockSpec(memory_space=pltpu.HBM), pl.BlockSpec(memory_space=pltpu.SEMAPHORE)
    res = pl.pallas_call(
        body, name=name, out_shape=[pltpu.HBM(t.shape, t.dtype) for t in list(srcs) + list(lands)],
        in_specs=[hbm] * (2 * n) + [sem, sem, pl.BlockSpec(memory_space=pl.ANY)], out_specs=[hbm] * (2 * n),
        input_output_aliases={i: i for i in range(2 * n)},
        compiler_params=pltpu.CompilerParams(has_side_effects=pltpu.SideEffectType.DATAFLOW_SIDE_EFFECTING),
    )(*srcs, *lands, send_sems, recv_sems, after)
    my_id = 4 * lax.axis_index("x") + 2 * lax.axis_index("y") + lax.axis_index("c")
    own = [s if gather else lax.dynamic_index_in_dim(s, my_id, 0, keepdims=False) for s in res[:n]]
    return [lax.dynamic_update_index_in_dim(land, o, my_id, 0) for land, o in zip(res[n:], own)]


def _adamw(name, parts, w, m, v):
    rows, cols = w.shape
    tr = _pick(rows, (128, 64, 32, 16, 8))
    pspec = pl.BlockSpec((N_DEV, tr, cols), lambda i: (0, i, 0))
    rspec = pl.BlockSpec((tr, cols), lambda i: (i, 0))

    def body(p_ref, w_ref, m_ref, v_ref, g_ref, d_ref, m2_ref, v2_ref):
        g, d, m2, v2 = _adamw_fn([p_ref[s] for s in range(N_DEV)], w_ref[...], m_ref[...], v_ref[...])
        g_ref[...], d_ref[...], m2_ref[...], v2_ref[...] = g, d, m2, v2

    return pl.pallas_call(
        body, name=name, grid=(rows // tr,), in_specs=[pspec, rspec, rspec, rspec], out_specs=[rspec] * 4,
        out_shape=[jax.ShapeDtypeStruct((rows, cols), F32)] * 4,
        compiler_params=pltpu.CompilerParams(dimension_semantics=("parallel",), vmem_limit_bytes=VMEM_LIMIT),
    )(parts, w, m, v)


def kernel(x, positions, g_mix, w_in, g_q_a, w_q_b, g_kv_a, w_kv_b, g_qn, g_kn, w_mla_out, ret_decay_fwd, ret_decay_bwd, w_ret_out, w_out, g_ffn, w_gate_up, w_down, loss_target, m_g_mix, m_w_in, m_g_q_a, m_w_q_b, m_g_kv_a, m_w_kv_b, m_g_qn, m_g_kn, m_w_mla_out, m_ret_decay_fwd, m_ret_decay_bwd, m_w_ret_out, m_w_out, m_g_ffn, m_w_gate_up, m_w_down, v_g_mix, v_w_in, v_g_q_a, v_w_q_b, v_g_kv_a, v_w_kv_b, v_g_qn, v_g_kn, v_w_mla_out, v_ret_decay_fwd, v_ret_decay_bwd, v_w_ret_out, v_w_out, v_g_ffn, v_w_gate_up, v_w_down):
    w = dict(g_mix=g_mix, w_in=w_in, g_q_a=g_q_a, w_q_b=w_q_b, g_kv_a=g_kv_a, w_kv_b=w_kv_b, g_qn=g_qn, g_kn=g_kn, w_mla_out=w_mla_out,
             ret_decay_fwd=ret_decay_fwd, ret_decay_bwd=ret_decay_bwd, w_ret_out=w_ret_out, w_out=w_out, g_ffn=g_ffn,
             w_gate_up=w_gate_up, w_down=w_down)
    m = dict(g_mix=m_g_mix, w_in=m_w_in, g_q_a=m_g_q_a, w_q_b=m_w_q_b, g_kv_a=m_g_kv_a, w_kv_b=m_w_kv_b, g_qn=m_g_qn, g_kn=m_g_kn,
             w_mla_out=m_w_mla_out, ret_decay_fwd=m_ret_decay_fwd, ret_decay_bwd=m_ret_decay_bwd, w_ret_out=m_w_ret_out, w_out=m_w_out,
             g_ffn=m_g_ffn, w_gate_up=m_w_gate_up, w_down=m_w_down)
    v = dict(g_mix=v_g_mix, w_in=v_w_in, g_q_a=v_g_q_a, w_q_b=v_w_q_b, g_kv_a=v_g_kv_a, w_kv_b=v_w_kv_b, g_qn=v_g_qn, g_kn=v_g_kn,
             w_mla_out=v_w_mla_out, ret_decay_fwd=v_ret_decay_fwd, ret_decay_bwd=v_ret_decay_bwd, w_ret_out=v_w_ret_out, w_out=v_w_out,
             g_ffn=v_g_ffn, w_gate_up=v_w_gate_up, w_down=v_w_down)
    gains = {n: w[n].reshape(1, ln) for n, ln in GAINS}

    axis_of = {n: axis for n, _, axis in MATS}
    later = [n for n, _, _ in MATS if n not in FIRST_WEIGHTS]
    gathered = _all_gather([w[n].astype(WIRE) for n in FIRST_WEIGHTS])
    W = {n: _unshard(g, axis_of[n]) for n, g in zip(FIRST_WEIGHTS, gathered)}
    later_handles = _exchange_start("gather_later_start", [w[n].astype(WIRE) for n in later], True, after=gathered[0])

    def late_weights(after):
        lands = _exchange_wait("gather_later_wait", later_handles, after, True)
        return {n: _unshard(g, axis_of[n]) for n, g in zip(later, lands)}

    grad_groups = []

    def grad_hook(g):
        names = tuple(g)
        handles = _exchange_start("grads_start_%d" % len(grad_groups), [_reshard(g[n], axis_of[n]).astype(GWIRE) for n in names], False)
        grad_groups.append((names, handles))
        return handles[4]

    S = x.shape[1]
    pos = positions.reshape(S, 1).astype(F32)
    loss_rows, grad_x, gG, gW = _local_step(x.reshape(S, D_MODEL), pos, loss_target.reshape(S, D_MODEL), gains, W, late_weights, grad_hook,
                                            start_after=later_handles[4])
    loss = lax.psum(jnp.sum(loss_rows), ("x", "y", "c"))

    last = [n for n, _, _ in MATS if n not in EARLY_GRADS + MID_GRADS]
    pieces = [_reshard(gW[n], axis_of[n]).astype(GWIRE) for n in last]
    pieces.append(jnp.broadcast_to(_pack_gains(gG)[None], (N_DEV, 1, GAIN_PAD)))
    late_parts = _all_to_all("grads_last", pieces)
    parts = dict(zip(last, late_parts))
    for i, (names, handles) in enumerate(grad_groups):
        parts.update(zip(names, _exchange_wait("grads_wait_%d" % i, handles, late_parts[-1], False)))
    out = [dict() for _ in range(4)]
    for n, _, _ in MATS:
        for o, r in zip(out, _adamw("adamw_" + n, parts[n], w[n], m[n], v[n])):
            o[n] = r
    for o, r in zip(out, _adamw("adamw_gains", late_parts[-1], _pack_gains(w), _pack_gains(m), _pack_gains(v))):
        o.update(_unpack_gains(r))
    return (loss, grad_x.reshape(x.shape), *[o[n] for o in out for n in ORDER])
```

```python
import functools

import numpy as np
import jax
import jax.numpy as jnp
from jax import lax
from jax.experimental import pallas as pl
from jax.experimental.pallas import tpu as pltpu

F32 = jnp.float32
MXU = jnp.bfloat16
WIRE = jnp.bfloat16
GWIRE = jnp.bfloat16

N_DEV = 8
D_MODEL = 1024
HEADS = 8
LANES = 128
Q_RANK, KV_RANK = 256, 128
NOPE, ROPE_M, V_M = 64, 32, 64
QK_M = NOPE + ROPE_M
RQK = 64
CHUNK = 128
FFN = 2816
THETA = 10000.0
EPS = 1e-6
LR, B1, B2, AEPS, WD, STEP = 0.001, 0.9, 0.999, 1e-08, 0.01, 10
VMEM_LIMIT = 56 * 1024 * 1024

NN = ((1,), (0,))
NT = ((1,), (1,))
TN = ((0,), (0,))

P_GATES, P_VR, P_GR, P_QR, P_KR, P_CQ, P_CKV, P_KROPE, P_WIDTH = 0, 2048, 3072, 4096, 4608, 5120, 5376, 5504, 5632
O_CQ, O_CKV, O_KROPE, O_QR, O_KR, O_VR, O_GR, O_GATES = 0, 256, 384, 416, 928, 1440, 2464, 3488


def _dot(a, b, dims):
    return lax.dot_general(a, b, (dims, ((), ())), preferred_element_type=F32)


def _pick(dim, cands):
    for c in cands:
        if dim % c == 0:
            return c
    return dim


def _pairs(t):
    return t.reshape(t.shape[0], 4, 2, 2, 32).transpose(0, 1, 3, 2, 4).reshape(t.shape[0], 512)


def _win_pad(w):
    z = jnp.zeros((w.shape[0], 48), w.dtype)
    kr = w[:, O_KROPE:O_KROPE + 32]
    return jnp.concatenate([w[:, O_GATES:], w[:, O_VR:O_VR + 1024], w[:, O_GR:O_GR + 1024], _pairs(w[:, O_QR:O_QR + 512]),
                            _pairs(w[:, O_KR:O_KR + 512]), w[:, :O_CKV], w[:, O_CKV:O_KROPE], kr[:, :16], z, kr[:, 16:], z], axis=1)


def _win_unpad(g):
    return jnp.concatenate([g[:, P_CQ:P_CQ + 256], g[:, P_CKV:P_CKV + 128], g[:, P_KROPE:P_KROPE + 16], g[:, P_KROPE + 64:P_KROPE + 80],
                            _pairs(g[:, P_QR:P_QR + 512]), _pairs(g[:, P_KR:P_KR + 512]), g[:, P_VR:P_VR + 1024],
                            g[:, P_GR:P_GR + 1024], g[:, P_GATES:P_GATES + 2048]], axis=1)


def _qk_pad(t):
    z = jnp.zeros(t.shape[:-1] + (32,), t.dtype)
    return jnp.concatenate([t[..., 64:80], t[..., 0:48], t[..., 80:96], t[..., 48:64], z], axis=-1)


def _qk_unpad(p):
    return jnp.concatenate([p[..., 16:64], p[..., 80:96], p[..., 0:16], p[..., 64:80]], axis=-1)


def _wq_pad(w):
    return _qk_pad(w.reshape(Q_RANK, HEADS, QK_M)).reshape(Q_RANK, HEADS * LANES)


def _wq_unpad(g):
    return _qk_unpad(g.reshape(Q_RANK, HEADS, LANES)).reshape(Q_RANK, HEADS * QK_M)


def _wkv_pad(w):
    t = w.reshape(KV_RANK, HEADS, NOPE + V_M)
    z = lambda n: jnp.zeros((KV_RANK, HEADS, n), w.dtype)
    wk = jnp.concatenate([z(16), t[..., 0:48], z(16), t[..., 48:64], z(32)], axis=-1)
    wv = jnp.concatenate([t[..., 64:128], z(64)], axis=-1)
    return wk.reshape(KV_RANK, HEADS * LANES), wv.reshape(KV_RANK, HEADS * LANES)


def _wkv_unpad(dwk, dwv):
    k, v = dwk.reshape(KV_RANK, HEADS, LANES), dwv.reshape(KV_RANK, HEADS, LANES)
    return jnp.concatenate([k[..., 16:64], k[..., 80:96], v[..., 0:64]], axis=-1).reshape(KV_RANK, HEADS * (NOPE + V_M))


def _wmla_pad(w):
    t = w.reshape(HEADS, V_M, D_MODEL)
    return jnp.concatenate([t, jnp.zeros_like(t)], axis=1).reshape(HEADS * LANES, D_MODEL)


def _wmla_unpad(g):
    return g.reshape(HEADS, LANES, D_MODEL)[:, :V_M].reshape(HEADS * V_M, D_MODEL)


def _rowwise(name, fn, rows, ts, ins, outs, accs=(), ncol=1):
    n_in, n_out, n_acc = len(ins), len(outs), len(accs)

    def colmap(col):
        if callable(col):
            return lambda i, j: (i, col(j))
        return lambda i, j: (i, col)

    arrays, in_specs = [], []
    for arr, spec in ins:
        arrays.append(arr)
        if spec is None:
            in_specs.append(pl.BlockSpec(arr.shape, functools.partial(lambda i, j, nd: (0,) * nd, nd=arr.ndim)))
        else:
            in_specs.append(pl.BlockSpec((ts, spec[0]), colmap(spec[1])))
    out_shape, out_specs = [], []
    for total, dtype, width, col in outs:
        out_shape.append(jax.ShapeDtypeStruct((rows, total), dtype))
        out_specs.append(pl.BlockSpec((ts, width), colmap(col)))
    for shp in accs:
        out_shape.append(jax.ShapeDtypeStruct(shp, F32))
        out_specs.append(pl.BlockSpec(shp, functools.partial(lambda i, j, nd: (0,) * nd, nd=len(shp))))

    def body(*refs):
        vals = [r[...] for r in refs[:n_in]]
        res = fn(*vals)
        if not isinstance(res, (tuple, list)):
            res = (res,)
        for r, v in zip(refs[n_in:n_in + n_out], res[:n_out]):
            r[...] = v.astype(r.dtype)
        if n_acc:
            first = jnp.logical_and(pl.program_id(0) == 0, pl.program_id(1) == 0)
            for r, v in zip(refs[n_in + n_out:], res[n_out:]):
                @pl.when(first)
                def _(r=r):
                    r[...] = jnp.zeros_like(r)
                r[...] += v.astype(F32)

    res = pl.pallas_call(
        body, name=name, grid=(rows // ts, ncol), in_specs=in_specs, out_specs=out_specs, out_shape=out_shape,
        compiler_params=pltpu.CompilerParams(dimension_semantics=("arbitrary", "arbitrary"), vmem_limit_bytes=VMEM_LIMIT),
    )(*arrays)
    return res


MM_OPERAND_BYTES = 24 * 1024 * 1024


def _mm(name, a, b, mode, add=None, after=None):
    b_halves = b.ndim == 3
    assert not b_halves or mode == "tn"
    if mode == "nn":
        (M, K), N = a.shape, b.shape[1]
    elif mode == "nt":
        (M, K), N = a.shape, b.shape[0]
    else:
        (K, M), N = a.shape, b.shape[-1] * (2 if b_halves else 1)
    tm = _pick(M, (1024, 512, 1408, 256, 128))
    tn = _pick(N // 2 if b_halves else N, (1408, 1024, 512, 256, 128))
    fits = lambda t: 2 * (tm + tn) * t * a.dtype.itemsize <= MM_OPERAND_BYTES
    tk = next(t for t in (K, 4096, 2816, 2048, 1408, 1024, 512, 256, 128) if K % t == 0 and (fits(t) or t == 128))
    nk = K // tk
    dims = {"nn": NN, "nt": NT, "tn": TN}[mode]
    a_spec = pl.BlockSpec((tk, tm), lambda i, j, k: (k, i)) if mode == "tn" else pl.BlockSpec((tm, tk), lambda i, j, k: (i, k))
    if b_halves:
        perj = (N // 2) // tn
        b_spec = pl.BlockSpec((None, tk, tn), lambda i, j, k: (j // perj, k, j % perj))
    else:
        b_spec = pl.BlockSpec((tn, tk), lambda i, j, k: (j, k)) if mode == "nt" else pl.BlockSpec((tk, tn), lambda i, j, k: (k, j))
    o_spec = pl.BlockSpec((tm, tn), lambda i, j, k: (i, j))
    has_add = add is not None

    def body(*refs):
        a_ref, b_ref, o_ref = refs[0], refs[1], refs[-1]
        d = _dot(a_ref[...], b_ref[...], dims)
        first = (d + refs[2][...]) if has_add else d
        if nk == 1:
            o_ref[...] = first
        else:
            k = pl.program_id(2)

            @pl.when(k == 0)
            def _():
                o_ref[...] = first

            @pl.when(k > 0)
            def _():
                o_ref[...] += d

    args = [a, b] + ([add] if has_add else []) + ([] if after is None else [after])
    specs = [a_spec, b_spec] + ([o_spec] if has_add else []) + ([] if after is None else [pl.BlockSpec(memory_space=pl.ANY)])
    return pl.pallas_call(
        body, name=name, grid=(M // tm, N // tn, nk), in_specs=specs, out_specs=o_spec,
        out_shape=jax.ShapeDtypeStruct((M, N), F32),
        compiler_params=pltpu.CompilerParams(dimension_semantics=("parallel", "parallel", "arbitrary"), vmem_limit_bytes=VMEM_LIMIT),
    )(*args)


def _mm_rows(name, a, b, fn, row_ins, whole_ins, outs, accs=(), mode="nn", lhs_fn=None):
    as_windows = lambda ts: [t if isinstance(t, tuple) else (t, (t.shape[1], 0)) for t in ts]
    halves = lhs_fn is None and a.ndim == 3
    assert not halves or mode == "nt"
    lhs_windows = as_windows(a) if lhs_fn is not None else []
    n_lhs = len(lhs_windows) if lhs_fn is not None else 1
    M = lhs_windows[0][0].shape[0] if lhs_fn is not None else a.shape[-2]
    kh = b.shape[0] if lhs_fn is not None else a.shape[-1]
    N = b.shape[1 if mode == "nn" else 0]
    tm = _pick(M, (512, 256, 128))
    windows = as_windows(row_ins)
    n_in, n_out = n_lhs + 1 + len(windows) + len(whole_ins), len(outs) + (1 if lhs_fn is not None else 0)
    row_spec = lambda w, col: pl.BlockSpec((tm, w), functools.partial(lambda i, col: (i, col), col=col))
    if lhs_fn is not None:
        lhs_specs, lhs_args = [row_spec(w, col) for _, (w, col) in lhs_windows], [t for t, _ in lhs_windows]
    else:
        lhs_specs = [pl.BlockSpec((2, tm, kh), lambda i: (0, i, 0)) if halves else pl.BlockSpec((tm, kh), lambda i: (i, 0))]
        lhs_args = [a]

    def body(*refs):
        rhs = refs[n_lhs]
        if halves:
            d = _dot(refs[0][0], rhs[:, :kh], NT) + _dot(refs[0][1], rhs[:, kh:], NT)
        elif lhs_fn is not None:
            lhs = lhs_fn(*[r[...] for r in refs[:n_lhs]]).astype(MXU)
            d = _dot(lhs, rhs[...], NN)
        else:
            d = _dot(refs[0][...], rhs[...], NN if mode == "nn" else NT)
        res = tuple(fn(d, *[r[...] for r in refs[n_lhs + 1:n_in]]))
        if lhs_fn is not None:
            res = (lhs,) + res
        for r, v in zip(refs[n_in:n_in + n_out], res[:n_out]):
            r[...] = v.astype(r.dtype)
        for r, v in zip(refs[n_in + n_out:], res[n_out:]):
            @pl.when(pl.program_id(0) == 0)
            def _(r=r):
                r[...] = jnp.zeros_like(r)
            r[...] += v

    whole = lambda t: pl.BlockSpec(t.shape, functools.partial(lambda i, nd: (0,) * nd, nd=t.ndim))
    out_rows = ([(kh, MXU)] if lhs_fn is not None else []) + [(N, dt) for dt in outs]
    return pl.pallas_call(
        body, name=name, grid=(M // tm,),
        in_specs=lhs_specs + [whole(b)] + [row_spec(w, col) for _, (w, col) in windows] + [whole(t) for t in whole_ins],
        out_specs=[row_spec(w, 0) for w, _ in out_rows]
        + [pl.BlockSpec(s, functools.partial(lambda i, nd: (0,) * nd, nd=len(s))) for s in accs],
        out_shape=[jax.ShapeDtypeStruct((M, w), dt) for w, dt in out_rows] + [jax.ShapeDtypeStruct(s, F32) for s in accs],
        compiler_params=pltpu.CompilerParams(dimension_semantics=("arbitrary",), vmem_limit_bytes=VMEM_LIMIT),
    )(*lhs_args, b, *[t for t, _ in windows], *whole_ins)


def _ffn_tiles(S):
    return _pick(S, (1024, 512, 256, 128)), _pick(FFN, (1408, 704, 256, 128))


def _gate_up_swiglu(h2, wgu):
    S, K = h2.shape
    tm, tn = _ffn_tiles(S)
    nj = FFN // tn

    def body(a_ref, bg_ref, bu_ref, gu_ref, act_ref):
        a = a_ref[...]
        g, u = _dot(a, bg_ref[...], NN), _dot(a, bu_ref[...], NN)
        gu_ref[0], gu_ref[1] = g.astype(gu_ref.dtype), u.astype(gu_ref.dtype)
        act_ref[...] = _swiglu_fn(g, u).astype(act_ref.dtype)

    return pl.pallas_call(
        body, name="gate_up_swiglu", grid=(S // tm, nj),
        in_specs=[pl.BlockSpec((tm, K), lambda i, j: (i, 0)), pl.BlockSpec((K, tn), lambda i, j: (0, j)),
                  pl.BlockSpec((K, tn), lambda i, j: (0, nj + j))],
        out_specs=[pl.BlockSpec((2, tm, tn), lambda i, j: (0, i, j)), pl.BlockSpec((tm, tn), lambda i, j: (i, j))],
        out_shape=[jax.ShapeDtypeStruct((2, S, FFN), MXU), jax.ShapeDtypeStruct((S, FFN), MXU)],
        compiler_params=pltpu.CompilerParams(dimension_semantics=("parallel", "parallel"), vmem_limit_bytes=VMEM_LIMIT),
    )(h2, wgu, wgu)


def _d_act_swiglu(dx2, wdown, gu):
    S, K = dx2.shape
    tm, tn = _ffn_tiles(S)

    def body(a_ref, b_ref, gu_ref, o_ref):
        dact = _dot(a_ref[...], b_ref[...], NT)
        _, vjp = jax.vjp(_swiglu_fn, gu_ref[0].astype(F32), gu_ref[1].astype(F32))
        dg, du = vjp(dact)
        o_ref[0], o_ref[1] = dg.astype(o_ref.dtype), du.astype(o_ref.dtype)

    stacked = pl.BlockSpec((2, tm, tn), lambda i, j: (0, i, j))
    return pl.pallas_call(
        body, name="d_act_swiglu", grid=(S // tm, FFN // tn),
        in_specs=[pl.BlockSpec((tm, K), lambda i, j: (i, 0)), pl.BlockSpec((tn, K), lambda i, j: (j, 0)), stacked],
        out_specs=stacked, out_shape=jax.ShapeDtypeStruct((2, S, FFN), MXU),
        compiler_params=pltpu.CompilerParams(dimension_semantics=("parallel", "parallel"), vmem_limit_bytes=VMEM_LIMIT),
    )(dx2, wdown, gu)


@jax.custom_vjp
def _swap64(x):
    return pltpu.roll(x, 64, 1)


_swap64.defvjp(lambda x: (_swap64(x), None), lambda _, g: (_swap64(g),))


@jax.custom_vjp
def _mxdot(a, b):
    return _dot(a.astype(MXU), b.astype(MXU), NN)


def _mxdot_bwd(res, g):
    a, b = res
    gb = g.astype(MXU)
    return _dot(gb, b.astype(MXU), NT), _dot(a.astype(MXU), gb, TN)


_mxdot.defvjp(lambda a, b: (_mxdot(a, b), (a, b)), _mxdot_bwd)


def _row_sum(t):
    if t.shape[-1] == LANES:
        hi = t.astype(jnp.bfloat16)
        lo = (t - hi.astype(F32)).astype(jnp.bfloat16)
        ones = jnp.ones((LANES, LANES), jnp.bfloat16)
        return _dot(hi, ones, NN) + _dot(lo, ones, NN)
    return jnp.sum(t, axis=-1, keepdims=True)


@functools.partial(jax.custom_vjp, nondiff_argnums=(1,))
def _unit_rms(x, n):
    return x * lax.rsqrt(_row_sum(x * x) * (1.0 / n) + EPS)


def _unit_rms_fwd(x, n):
    r = lax.rsqrt(_row_sum(x * x) * (1.0 / n) + EPS)
    y = x * r
    return y, (y, r)


def _unit_rms_bwd(n, res, g):
    y, r = res
    return (r * (g - y * (_row_sum(g * y) * (1.0 / n))),)


_unit_rms.defvjp(_unit_rms_fwd, _unit_rms_bwd)


def _rms(x):
    return _unit_rms(x, x.shape[-1])


def _rmsg_fn(x, g):
    return _rms(x) * g


def _silu(x):
    return x * jax.nn.sigmoid(x)


def _tables_fn(pos, inv_m, sgn_m, inv_r, sgn_r):
    am, ar = pos * inv_m, pos * inv_r
    return jnp.cos(am), jnp.sin(am) * sgn_m, jnp.cos(ar), jnp.sin(ar) * sgn_r


def _head_blocks(t):
    return [t[:, LANES * h:LANES * (h + 1)] for h in range(t.shape[1] // LANES)]


def _mla_prep_fn(cq, ckv, kr, cosm, sinm, gqa, gkva, gqn, gkn, wq, wk, wv):
    cqn = _rms(cq) * gqa
    ckvn = _rms(ckv) * gkva
    q_raw = _mxdot(cqn, wq)
    k_raw = _mxdot(ckvn, wk)
    lane = lax.broadcasted_iota(jnp.int32, (1, HEADS * LANES), 1)
    v = _mxdot(ckvn, wv) + (lane % LANES == V_M).astype(F32)

    def norm_rope(blocks, g, extra):
        outs = []
        for b in blocks:
            if extra is not None:
                b = b + extra
            n = _unit_rms(b, QK_M) * g
            outs.append(n * cosm + _swap64(n) * sinm)
        return jnp.concatenate(outs, axis=1)

    q = norm_rope(_head_blocks(q_raw), gqn, None)
    k = norm_rope(_head_blocks(k_raw), gkn, kr)
    return q, k, v


def _ret_prep_fn(qr, kr, cosr, sinr):
    def rope(t, scale):
        return jnp.concatenate([(b * cosr + _swap64(b) * sinr) * scale for b in _head_blocks(t)], axis=1)
    return rope(qr, 1.0), rope(kr, RQK ** -0.5)


def _ret_post_fn(rf, rb, gr):
    ret = rf + rb
    outs = []
    for b, g in zip(_head_blocks(ret), _head_blocks(gr)):
        outs.append(_silu(g) * _rms(b))
    return jnp.concatenate(outs, axis=1)


def _merge_fn(ga, gb, ya, yb):
    return jax.nn.sigmoid(ga) * ya + jax.nn.sigmoid(gb) * yb


def _swiglu_fn(gate, up):
    return _silu(gate) * up


def _loss_fn(x2, tgt):
    d = x2 - tgt
    return d * (1.0 / D_MODEL), 0.5 * jnp.sum(d * d, axis=0, keepdims=True) * (1.0 / D_MODEL)


def _adamw_fn(parts, w, m, v):
    g = parts[0].astype(F32)
    for p in range(1, N_DEV):
        g = g + parts[p].astype(F32)
    m2 = B1 * m + (1.0 - B1) * g
    v2 = B2 * v + (1.0 - B2) * jnp.square(g)
    m_hat = m2 / (1.0 - B1 ** STEP)
    v_hat = v2 / (1.0 - B2 ** STEP)
    delta = -LR * (m_hat / (jnp.sqrt(v_hat) + AEPS) + WD * w)
    return g, delta, m2, v2


SCALE = QK_M ** -0.5
LOG2E = 1.4426950408889634
FLASH_ROWS = 32


def _flash_fwd(q, k, v):
    S = q.shape[0]
    tk = _pick(S, (512, 256, 128))
    tq = _pick(S, (1024, 512, 256, 128))
    ncb = tk // LANES
    nkv = S // tk
    mrows = 64
    c = SCALE * LOG2E

    def body(q_ref, k_ref, v_ref, o_ref, lse_ref, s_a, p_a, s_b, p_b, m_sc, a_sc, acc_sc):
        m_sc[...] = jnp.full_like(m_sc, -jnp.inf)
        acc_sc[...] = jnp.zeros_like(acc_sc)
        qb = q_ref[...]

        def scores(j, s_buf):
            s_buf[...] = _dot(qb, k_ref[j * tk:(j + 1) * tk, :], NT)

        def stage(j, s_buf, p_buf, s_next):
            if j + 1 < nkv:
                scores(j + 1, s_next)
            for r in range(tq // mrows):
                rows = slice(r * mrows, (r + 1) * mrows)
                cols = [s_buf[rows, LANES * cb:LANES * (cb + 1)] for cb in range(ncb)]
                m_prev = m_sc[rows, :]
                row_max = jnp.max(functools.reduce(jnp.maximum, cols), axis=-1, keepdims=True)
                m_new = jnp.maximum(m_prev, jnp.broadcast_to(row_max, (mrows, LANES)))
                a_sc[rows, :] = jnp.exp2((m_prev - m_new) * c)
                m_sc[rows, :] = m_new
                for cb in range(ncb):
                    p_buf[rows, LANES * cb:LANES * (cb + 1)] = jnp.exp2((cols[cb] - m_new) * c).astype(p_buf.dtype)
            acc_sc[...] = a_sc[...] * acc_sc[...] + _dot(p_buf[...], v_ref[j * tk:(j + 1) * tk, :], NN)

        scores(0, s_a)
        for j in range(nkv):
            stage(j, *((s_a, p_a, s_b) if j % 2 == 0 else (s_b, p_b, s_a)))
        acc = acc_sc[...]
        lane = lax.broadcasted_iota(jnp.int32, (1, LANES), 1)
        l = jnp.sum(jnp.where(lane == V_M, acc, 0.0), axis=-1, keepdims=True)
        o_ref[...] = (acc / l).astype(o_ref.dtype)
        lse_ref[...] = m_sc[...] * c + jnp.log2(jnp.broadcast_to(l, (tq, LANES)))

    qspec = pl.BlockSpec((tq, LANES), lambda h, i: (i, h))
    kspec = pl.BlockSpec((S, LANES), lambda h, i: (0, h))
    return pl.pallas_call(
        body, name="flash_fwd", grid=(HEADS, S // tq), in_specs=[qspec, kspec, kspec], out_specs=[qspec, qspec],
        out_shape=[jax.ShapeDtypeStruct((S, HEADS * LANES), MXU), jax.ShapeDtypeStruct((S, HEADS * LANES), F32)],
        scratch_shapes=[pltpu.VMEM((tq, tk), F32), pltpu.VMEM((tq, tk), MXU)] * 2 + [pltpu.VMEM((tq, LANES), F32)] * 3,
        compiler_params=pltpu.CompilerParams(dimension_semantics=("parallel", "arbitrary"), vmem_limit_bytes=VMEM_LIMIT),
    )(q, k, v)


def _delta_fn(o, do):
    outs = [jnp.broadcast_to(jnp.sum(a * b, axis=-1, keepdims=True), a.shape) for a, b in zip(_head_blocks(o), _head_blocks(do))]
    return do, jnp.concatenate(outs, axis=1)


def _flash_bwd(q, k, v, do, lse, delta):
    S = q.shape[0]
    tq = tk = _pick(S, (512, 256, 128))
    ncb = tk // LANES
    c = SCALE * LOG2E
    nq = S // tq
    nsub = 2 if (S // tk) % 2 == 0 else 1
    stages = [(sub, i) for sub in range(nsub) for i in range(nq)]

    def body(q_ref, k_ref, v_ref, do_ref, lse_ref, dl_ref, dq_ref, dk_ref, dv_ref, s_a, dp_a, p_a, ds_a, s_b, dp_b, p_b, ds_b):
        @pl.when(pl.program_id(1) == 0)
        def _():
            dq_ref[...] = jnp.zeros_like(dq_ref)

        dk_ref[...] = jnp.zeros_like(dk_ref)
        dv_ref[...] = jnp.zeros_like(dv_ref)
        bufs = [(s_a, dp_a, p_a, ds_a), (s_b, dp_b, p_b, ds_b)]

        def scores(sub, i, s_buf, dp_buf):
            kv_rows, q_rows = slice(sub * tk, (sub + 1) * tk), slice(i * tq, (i + 1) * tq)
            s_buf[...] = _dot(q_ref[q_rows, :], k_ref[kv_rows, :], NT)
            dp_buf[...] = _dot(do_ref[q_rows, :], v_ref[kv_rows, :], NT)

        scores(*stages[0], *bufs[0][:2])
        for t, (sub, i) in enumerate(stages):
            s_buf, dp_buf, p_buf, ds_buf = bufs[t % 2]
            if t + 1 < len(stages):
                scores(*stages[t + 1], *bufs[(t + 1) % 2][:2])
            for r in range(tq // FLASH_ROWS):
                rows = slice(r * FLASH_ROWS, (r + 1) * FLASH_ROWS)
                grows = slice(i * tq + r * FLASH_ROWS, i * tq + (r + 1) * FLASH_ROWS)
                lse_b, dl_b = lse_ref[grows, :], dl_ref[grows, :]
                for cb in range(ncb):
                    sl = slice(LANES * cb, LANES * (cb + 1))
                    p = jnp.exp2(s_buf[rows, sl] * c - lse_b)
                    p_buf[rows, sl] = p.astype(p_buf.dtype)
                    ds_buf[rows, sl] = (p * (dp_buf[rows, sl] - dl_b) * SCALE).astype(ds_buf.dtype)
            kv_rows, q_rows = slice(sub * tk, (sub + 1) * tk), slice(i * tq, (i + 1) * tq)
            dv_ref[kv_rows, :] += _dot(p_buf[...], do_ref[q_rows, :], TN)
            dk_ref[kv_rows, :] += _dot(ds_buf[...], q_ref[q_rows, :], TN)
            dq_ref[q_rows, :] += _dot(ds_buf[...], k_ref[kv_rows, :], NN)

    hspec = pl.BlockSpec((S, LANES), lambda h, j: (0, h))
    kspec = pl.BlockSpec((nsub * tk, LANES), lambda h, j: (j, h))
    full = jax.ShapeDtypeStruct((S, HEADS * LANES), F32)
    tile_bufs = [pltpu.VMEM((tq, tk), F32), pltpu.VMEM((tq, tk), F32), pltpu.VMEM((tq, tk), MXU), pltpu.VMEM((tq, tk), MXU)]
    return pl.pallas_call(
        body, name="flash_bwd", grid=(HEADS, S // (nsub * tk)), in_specs=[hspec, kspec, kspec, hspec, hspec, hspec],
        out_specs=[hspec, kspec, kspec], out_shape=[full, full, full],
        scratch_shapes=tile_bufs + tile_bufs,
        compiler_params=pltpu.CompilerParams(dimension_semantics=("parallel", "arbitrary"), vmem_limit_bytes=VMEM_LIMIT),
    )(q, k, v, do, lse, delta)


def _ret_consts(lgh, head, rev):
    C = CHUNK
    lane = lax.broadcasted_iota(jnp.int32, (1, LANES), 1)
    hm = ((lane // 32) % 2 == head % 2).astype(F32)
    r = lax.broadcasted_iota(jnp.int32, (C, C), 0)
    c = lax.broadcasted_iota(jnp.int32, (C, C), 1)
    diff = ((c - r) if rev else (r - c)).astype(F32)
    mask = (diff > 0) if rev else (diff >= 0)
    dpos = jnp.maximum(diff, 0.0)
    din = jnp.where(mask, jnp.exp(lgh * dpos), 0.0)
    idx = lax.broadcasted_iota(jnp.int32, (C, 1), 0).astype(F32)
    eq = (C - idx) if rev else (idx + 1.0)
    ek = idx if rev else (C - 1.0 - idx)
    qd, kd = jnp.exp(lgh * eq), jnp.exp(lgh * ek)
    cd = jnp.exp(lgh * jnp.full((1, 1), float(C), F32))
    return hm, din, dpos, qd, kd, cd, eq, ek


RET_HEADS_PER_STEP = 8


def _ret_fwd(name, qt, kt, proj, lg, rev):
    S = qt.shape[0]
    C = CHUNK
    TB = _pick(S, (512, 256, 128))
    cb, nb = TB // C, S // TB
    hps = RET_HEADS_PER_STEP
    blk = (lambda g: nb - 1 - g) if rev else (lambda g: g)

    def body(lg_ref, q_ref, k_ref, v_ref, o_ref, st_ref, state_sc):
        hg, g = pl.program_id(0), pl.program_id(1)

        @pl.when(g == 0)
        def _():
            state_sc[...] = jnp.zeros_like(state_sc)

        consts = [_ret_consts(lg_ref[hg * hps + u], u, rev) for u in range(hps)]
        order = list(reversed(range(cb))) if rev else list(range(cb))
        units = [(cc, u) for cc in order for u in range(hps)]

        def operands(cc, u):
            rows = pl.ds(cc * C, C)
            pair = slice(LANES * (u // 2), LANES * (u // 2 + 1))
            hm = consts[u][0]
            return q_ref[rows, pair] * hm, k_ref[rows, pair] * hm, v_ref[rows, LANES * u:LANES * (u + 1)].astype(MXU)

        a, inc = {}, {}
        for cc, u in units:
            q, k, v = operands(cc, u)
            a[cc, u] = _dot(q.astype(MXU), k.astype(MXU), NT) * consts[u][1]
            inc[cc, u] = _dot((k * consts[u][4]).astype(MXU), v, TN)
        for u in range(hps):
            st = state_sc[u]
            for cc in order:
                st_ref[u, cc] = st
                st = st * consts[u][5] + inc[cc, u]
            state_sc[u] = st
        for cc, u in units:
            q, _, v = operands(cc, u)
            cross = _dot((q * consts[u][3]).astype(MXU), st_ref[u, cc].astype(MXU), NN)
            o_ref[pl.ds(cc * C, C), LANES * u:LANES * (u + 1)] = _dot(a[cc, u].astype(MXU), v, NN) + cross

    qk_spec = pl.BlockSpec((TB, LANES * hps // 2), lambda h, g: (blk(g), h))
    return pl.pallas_call(
        body, name=name, grid=(HEADS // hps, nb),
        in_specs=[pl.BlockSpec(memory_space=pltpu.SMEM), qk_spec, qk_spec,
                  pl.BlockSpec((TB, LANES * hps), lambda h, g: (blk(g), P_VR // (LANES * hps) + h))],
        out_specs=[pl.BlockSpec((TB, LANES * hps), lambda h, g: (blk(g), h)),
                   pl.BlockSpec((hps, cb, LANES, LANES), lambda h, g: (h, blk(g), 0, 0))],
        out_shape=[jax.ShapeDtypeStruct((S, HEADS * LANES), F32), jax.ShapeDtypeStruct((HEADS, S // C, LANES, LANES), F32)],
        scratch_shapes=[pltpu.VMEM((hps, LANES, LANES), F32)],
        compiler_params=pltpu.CompilerParams(dimension_semantics=("parallel", "arbitrary"), vmem_limit_bytes=VMEM_LIMIT),
    )(lg, qt, kt, proj)


def _ret_bwd(name, qt, kt, proj, dret, states, lg, rev):
    S = qt.shape[0]
    C = CHUNK
    TB = _pick(S, (512, 256, 128))
    cb, nb = TB // C, S // TB
    hps = RET_HEADS_PER_STEP
    blk = (lambda g: g) if rev else (lambda g: nb - 1 - g)

    def body(lg_ref, q_ref, k_ref, v_ref, do_ref, st_ref, dq_ref, dk_ref, dv_ref, dlg_ref, ds_sc, acc_cc, acc_q, acc_k, acc_s):
        hg, g = pl.program_id(0), pl.program_id(1)

        @pl.when(g == 0)
        def _():
            ds_sc[...] = jnp.zeros_like(ds_sc)
            acc_cc[...] = jnp.zeros_like(acc_cc)
            acc_q[...] = jnp.zeros_like(acc_q)
            acc_k[...] = jnp.zeros_like(acc_k)
            acc_s[...] = jnp.zeros_like(acc_s)

        lgs = [lg_ref[hg * hps + u] for u in range(hps)]
        consts = [_ret_consts(lgs[u], u, rev) for u in range(hps)]
        order = list(range(cb)) if rev else list(reversed(range(cb)))
        units = [(cc, u) for cc in order for u in range(hps)]

        def operands(cc, u):
            rows = pl.ds(cc * C, C)
            pair = slice(LANES * (u // 2), LANES * (u // 2 + 1))
            head = slice(LANES * u, LANES * (u + 1))
            hm = consts[u][0]
            return q_ref[rows, pair] * hm, k_ref[rows, pair] * hm, v_ref[rows, head].astype(MXU), do_ref[rows, head].astype(MXU)

        a, dp, dqs, inc = {}, {}, {}, {}
        for cc, u in units:
            q, k, vb, dob = operands(cc, u)
            a[cc, u] = _dot(q.astype(MXU), k.astype(MXU), NT)
            dp[cc, u] = _dot(dob, vb, NT)
            dqs[cc, u] = _dot(dob, st_ref[u, cc].astype(MXU), NT)
            inc[cc, u] = _dot((q * consts[u][3]).astype(MXU), dob, TN)
        dsn = {}
        for u in range(hps):
            ds = ds_sc[u]
            for cc in order:
                dsn[cc, u] = ds
                ds = ds * consts[u][5] + inc[cc, u]
            ds_sc[u] = ds
        even = {}
        for cc, u in units:
            hm, din, dpos, qd, kd, cd, eq, ek = consts[u]
            rows, head = pl.ds(cc * C, C), slice(LANES * u, LANES * (u + 1))
            q, k, vb, dob = operands(cc, u)
            qb, kb = q.astype(MXU), k.astype(MXU)
            dsnb = dsn[cc, u].astype(MXU)
            da = (dp[cc, u] * din).astype(MXU)
            vds = _dot(vb, dsnb, NT)
            dq_u = (_dot(da, kb, NN) + dqs[cc, u] * qd) * hm
            dk_u = (_dot(da, qb, TN) + vds * kd) * hm
            if u % 2 == 0:
                even[cc] = (dq_u, dk_u)
            else:
                pair = slice(LANES * (u // 2), LANES * (u // 2 + 1))
                dq_ref[rows, pair] = even[cc][0] + dq_u
                dk_ref[rows, pair] = even[cc][1] + dk_u
            dv_ref[rows, head] = _dot((a[cc, u] * din).astype(MXU), dob, TN) + _dot((k * kd).astype(MXU), dsnb, NN)
            acc_cc[u] += dp[cc, u] * a[cc, u] * din * dpos
            acc_q[u] += dqs[cc, u] * q * (qd * eq)
            acc_k[u] += vds * k * (kd * ek)
            acc_s[u] += dsn[cc, u] * st_ref[u, cc] * (cd * float(C))

        @pl.when(g == nb - 1)
        def _():
            for u in range(hps):
                tot = (jnp.sum(acc_cc[u], keepdims=True) + jnp.sum(acc_q[u], keepdims=True)
                       + jnp.sum(acc_k[u], keepdims=True) + jnp.sum(acc_s[u], keepdims=True))
                dlg_ref[u] = jnp.broadcast_to(tot * lgs[u], (8, LANES))

    full = jax.ShapeDtypeStruct((S, HEADS * LANES), F32)
    hspec = pl.BlockSpec((TB, LANES * hps), lambda h, g: (blk(g), h))
    qk_spec = pl.BlockSpec((TB, LANES * hps // 2), lambda h, g: (blk(g), h))
    return pl.pallas_call(
        body, name=name, grid=(HEADS // hps, nb),
        in_specs=[pl.BlockSpec(memory_space=pltpu.SMEM), qk_spec, qk_spec,
                  pl.BlockSpec((TB, LANES * hps), lambda h, g: (blk(g), P_VR // (LANES * hps) + h)),
                  hspec,
                  pl.BlockSpec((hps, cb, LANES, LANES), lambda h, g: (h, blk(g), 0, 0))],
        out_specs=[qk_spec, qk_spec, hspec, pl.BlockSpec((hps, 8, LANES), lambda h, g: (h, 0, 0))],
        out_shape=[jax.ShapeDtypeStruct(qt.shape, F32), jax.ShapeDtypeStruct(kt.shape, F32), full,
                   jax.ShapeDtypeStruct((HEADS, 8, LANES), F32)],
        scratch_shapes=[pltpu.VMEM((hps, LANES, LANES), F32), pltpu.VMEM((hps, C, C), F32), pltpu.VMEM((hps, C, LANES), F32),
                        pltpu.VMEM((hps, C, LANES), F32), pltpu.VMEM((hps, LANES, LANES), F32)],
        compiler_params=pltpu.CompilerParams(dimension_semantics=("parallel", "arbitrary"), vmem_limit_bytes=VMEM_LIMIT),
    )(lg, qt, kt, proj, dret, states)


def _rope_consts():
    inv16 = THETA ** (-jnp.arange(16, dtype=F32) / 16)
    inv32 = THETA ** (-jnp.arange(32, dtype=F32) / 32)
    lane = np.arange(LANES)
    z48 = jnp.zeros((48,), F32)
    inv_m = jnp.concatenate([inv16, z48, inv16, z48])[None, :]
    sgn_m = jnp.asarray(np.where(lane < 16, -1.0, np.where((lane >= 64) & (lane < 80), 1.0, 0.0)), F32)[None, :]
    inv_r = jnp.concatenate([inv32] * 4)[None, :]
    sgn_r = jnp.asarray(np.where(lane < 64, -1.0, 1.0), F32)[None, :]
    return inv_m, sgn_m, inv_r, sgn_r


FIRST_WEIGHTS = ("w_in", "w_q_b", "w_kv_b")
EARLY_GRADS = ("w_down", "w_gate_up", "w_out", "w_ret_out")
MID_GRADS = ("w_mla_out", "w_in")


def _local_step(x, pos, tgt, gains, W, late_weights=None, grad_hook=None, start_after=None):
    S = x.shape[0]
    ts = _pick(S, (256, 128))
    ts_light = _pick(S, (512, 256, 128))
    R = lambda a, w=None, c=0: (a, ((a.shape[1] if w is None else w), c))
    W_ = lambda a: (a, None)

    win = _win_pad(W["w_in"])
    wq = _wq_pad(W["w_q_b"])
    wk, wv = _wkv_pad(W["w_kv_b"])
    gqn, gkn = _qk_pad(gains["g_qn"]), _qk_pad(gains["g_kn"])
    g_mix, g_q_a, g_kv_a, g_ffn = gains["g_mix"], gains["g_q_a"], gains["g_kv_a"], gains["g_ffn"]
    lg_f = -jnp.exp(gains["ret_decay_fwd"][0])
    lg_b = -jnp.exp(gains["ret_decay_bwd"][0])

    consts = list(_rope_consts())
    cosm, sinm, cosr, sinr = _rowwise("rope_tables", _tables_fn, S, ts_light,[R(pos)] + [W_(c) for c in consts],
                                      [(LANES, F32, LANES, 0)] * 4)

    (h,) = _rowwise("rms_mix", _rmsg_fn, S, ts_light,[R(x), W_(g_mix)], [(D_MODEL, MXU, D_MODEL, 0)])
    proj = _mm("in_proj", h, win, "nn", after=start_after)
    seg = lambda off, w: (proj, (w, off // w))
    mla_ins = [seg(P_CQ, 256), seg(P_CKV, 128), seg(P_KROPE, 128), R(cosm), R(sinm),
               W_(g_q_a), W_(g_kv_a), W_(gqn), W_(gkn), W_(wq), W_(wk), W_(wv)]
    q, k, v = _rowwise("mla_prep", _mla_prep_fn, S, ts, mla_ins, [(HEADS * LANES, MXU, HEADS * LANES, 0)] * 3)
    o_bf, lse = _flash_fwd(q, k, v)
    if late_weights is not None:
        W = {**W, **late_weights(lse)}
    wmla = _wmla_pad(W["w_mla_out"])
    wret, wout, wgu, wdown = W["w_ret_out"], W["w_out"], W["w_gate_up"], W["w_down"]
    y_a = _mm("mla_out", o_bf, wmla, "nn")

    ret_ins = [seg(P_QR, 512), seg(P_KR, 512), R(cosr), R(sinr)]
    qt, kt = _rowwise("ret_prep", _ret_prep_fn, S, ts_light,ret_ins, [(512, F32, 512, 0)] * 2)
    ret_f, st_f = _ret_fwd("ret_fwd_f", qt, kt, proj, lg_f, False)
    ret_b, st_b = _ret_fwd("ret_fwd_b", qt, kt, proj, lg_b, True)
    post_ins = [R(ret_f), R(ret_b), seg(P_GR, 1024)]
    o_b, y_b, merged = _mm_rows("ret_post_out_merge", post_ins, wret, lambda yb, ga, gb, ya: (yb, _merge_fn(ga, gb, ya, yb)),
                                [seg(P_GATES, 1024), (proj, (1024, 1)), R(y_a)], [], [F32, MXU], lhs_fn=_ret_post_fn)
    merge_ins = [seg(P_GATES, 1024), (proj, (1024, 1)), R(y_a), R(y_b)]
    def residual_rms(d, xx, g):
        r = d + xx
        return r, _rmsg_fn(r, g)

    x1, h2 = _mm_rows("out_proj_rms_ffn", merged, wout, residual_rms, [x], [g_ffn], [F32, MXU])
    gu, act = _gate_up_swiglu(h2, wgu)

    def residual_loss(d, xx, t):
        dx, rows = _loss_fn(d + xx, t)
        return dx, dx, rows

    dx2, dx2_bf, loss_rows = _mm_rows("down_proj_loss", act, wdown, residual_loss, [x1, tgt], [], [F32, MXU], accs=[(1, D_MODEL)])

    gW = {}
    gW["w_down"] = _mm("d_w_down", act, dx2_bf, "tn")
    dgu = _d_act_swiglu(dx2_bf, wdown, gu)
    gW["w_gate_up"] = _mm("d_w_gate_up", h2, dgu, "tn")
    def rms_bwd(xx, g, dh, dres):
        _, vjp = jax.vjp(_rmsg_fn, xx, g)
        dx, dg = vjp(dh)
        dx = dx + dres
        return dx, dx, dg

    dx1, dx1_bf, dg_ffn = _mm_rows("d_h2_rms_ffn_bwd", dgu, wgu, lambda dh, xx, dres, g: rms_bwd(xx, g, dh, dres),
                                   [x1, dx2], [g_ffn], [F32, MXU], accs=[(1, D_MODEL)], mode="nt")
    gW["w_out"] = _mm("d_w_out", merged, dx1_bf, "tn")
    def merge_bwd(dm, ga, gb, ya, yb):
        _, vjp = jax.vjp(_merge_fn, ga, gb, ya, yb)
        return vjp(dm)

    dga, dgb, dy_a, dy_b = _mm_rows("d_merged_merge_bwd", dx1_bf, wout, merge_bwd, merge_ins, [], [MXU] * 4, mode="nt")
    gW["w_ret_out"] = _mm("d_w_ret_out", o_b, dy_b, "tn")
    after_early = [] if grad_hook is None else [grad_hook({n: gW[n] for n in EARLY_GRADS})]

    def post_bwd(dob, rf, rb, gr, *_):
        _, vjp = jax.vjp(_ret_post_fn, rf, rb, gr)
        drf, _, dgr = vjp(dob)
        return drf, dgr

    dret, dg_r = _mm_rows("d_o_b_ret_post_bwd", dy_b, wret, post_bwd, post_ins, after_early, [MXU, MXU], mode="nt")
    dq_f, dk_f, dv_f, dlg_f = _ret_bwd("ret_bwd_f", qt, kt, proj, dret, st_f, lg_f, False)
    dq_b, dk_b, dv_b, dlg_b = _ret_bwd("ret_bwd_b", qt, kt, proj, dret, st_b, lg_b, True)

    def ret_prep_bwd(qr, kr, cosr_, sinr_, dqf, dqb, dkf, dkb, dvf, dvb):
        _, vjp = jax.vjp(lambda a, b: _ret_prep_fn(a, b, cosr_, sinr_), qr, kr)
        dqr, dkr = vjp((dqf + dqb, dkf + dkb))
        return dqr, dkr, dvf + dvb

    dq_r, dk_r, dv_r = _rowwise("ret_prep_bwd", ret_prep_bwd, S, ts_light,ret_ins + [R(t) for t in (dq_f, dq_b, dk_f, dk_b, dv_f, dv_b)],
                                [(512, MXU, 512, 0), (512, MXU, 512, 0), (1024, MXU, 1024, 0)])

    gW_mla_p = _mm("d_w_mla_out", o_bf, dy_a, "tn")
    do_bf, delta = _mm_rows("d_o_attn_delta", dy_a, wmla, lambda d, oo, *_: _delta_fn(oo.astype(F32), d), [o_bf], after_early, [MXU, F32], mode="nt")
    dq, dk, dv = _flash_bwd(q, k, v, do_bf, lse, delta)

    def mla_prep_bwd(cq, ckv, kr, cosm_, sinm_, gqa, gkva, gqn_, gkn_, wq_, wk_, wv_, dq_, dk_, dv_):
        f = lambda cq, ckv, kr, gqa, gkva, gqn_, gkn_, wq_, wk_, wv_: _mla_prep_fn(cq, ckv, kr, cosm_, sinm_, gqa, gkva, gqn_, gkn_, wq_, wk_, wv_)
        _, vjp = jax.vjp(f, cq, ckv, kr, gqa, gkva, gqn_, gkn_, wq_.astype(F32), wk_.astype(F32), wv_.astype(F32))
        return vjp((dq_, dk_, dv_))

    mb = _rowwise("mla_prep_bwd", mla_prep_bwd, S, ts, mla_ins + [R(dq), R(dk), R(dv)],
                  [(256, MXU, 256, 0), (128, MXU, 128, 0), (128, MXU, 128, 0)],
                  accs=[(1, 256), (1, 128), (1, LANES), (1, LANES), (256, HEADS * LANES), (128, HEADS * LANES), (128, HEADS * LANES)])
    dc_q, dc_kv, dk_rope, dg_q_a, dg_kv_a, dgqn_p, dgkn_p, dwq_p, dwk_p, dwv_p = mb

    dproj = jnp.concatenate([dga, dgb, dv_r, dg_r, dq_r, dk_r, dc_q, dc_kv, dk_rope], axis=1)
    gW["w_in"] = _win_unpad(_mm("d_w_in", h, dproj, "tn"))
    gW["w_mla_out"] = _wmla_unpad(gW_mla_p)
    after_mid = None if grad_hook is None else grad_hook({n: gW[n] for n in MID_GRADS})
    grad_x, dg_mix = _mm_rows("d_h_rms_mix_bwd", dproj, win, lambda dh, xx, dres, g, *_: rms_bwd(xx, g, dh, dres)[1:],
                              [x, dx1], [g_mix] + ([] if after_mid is None else [after_mid]), [F32], accs=[(1, D_MODEL)], mode="nt")
    gW["w_q_b"] = _wq_unpad(dwq_p)
    gW["w_kv_b"] = _wkv_unpad(dwk_p, dwv_p)
    gG = {"g_mix": dg_mix, "g_q_a": dg_q_a, "g_kv_a": dg_kv_a, "g_qn": _qk_unpad(dgqn_p),
          "g_kn": _qk_unpad(dgkn_p), "ret_decay_fwd": dlg_f[:, 0, 0][None, :], "ret_decay_bwd": dlg_b[:, 0, 0][None, :],
          "g_ffn": dg_ffn}
    return loss_rows, grad_x, gG, gW


MATS = [("w_in", (1024, 5536), 1), ("w_q_b", (256, 768), 1), ("w_kv_b", (128, 1024), 1), ("w_mla_out", (512, 1024), 1),
        ("w_ret_out", (1024, 1024), 0), ("w_out", (1024, 1024), 0), ("w_gate_up", (1024, 5632), 1), ("w_down", (2816, 1024), 0)]
GAINS = [("g_mix", 1024), ("g_q_a", 256), ("g_kv_a", 128), ("g_qn", 96), ("g_kn", 96), ("ret_decay_fwd", 8), ("ret_decay_bwd", 8),
         ("g_ffn", 1024)]
ORDER = ["g_mix", "w_in", "g_q_a", "w_q_b", "g_kv_a", "w_kv_b", "g_qn", "g_kn", "w_mla_out", "ret_decay_fwd", "ret_decay_bwd",
         "w_ret_out", "w_out", "g_ffn", "w_gate_up", "w_down"]
GAIN_LEN = sum(n for _, n in GAINS)
GAIN_PAD = -(-GAIN_LEN // LANES) * LANES


def _pack_gains(d):
    row = jnp.concatenate([d[n].reshape(1, ln).astype(F32) for n, ln in GAINS], axis=1)
    return jnp.pad(row, ((0, 0), (0, GAIN_PAD - GAIN_LEN)))


def _unpack_gains(row):
    out, off = {}, 0
    for n, ln in GAINS:
        out[n] = row[0, off:off + ln]
        off += ln
    return out


def _unshard(pieces, axis):
    if axis == 0:
        return pieces.reshape((N_DEV * pieces.shape[1], pieces.shape[2]))
    return jnp.concatenate([pieces[p] for p in range(N_DEV)], axis=1)


def _reshard(full, axis):
    if axis == 0:
        return full.reshape((N_DEV, full.shape[0] // N_DEV, full.shape[1]))
    c = full.shape[1] // N_DEV
    return jnp.stack([full[:, c * p:c * (p + 1)] for p in range(N_DEV)])


def _all_gather(shards):
    n = len(shards)

    def body(*refs):
        x_refs, out_refs = refs[:n], refs[n:2 * n]
        send_sems, recv_sems, local_sems = refs[2 * n:]
        x, y, c = lax.axis_index("x"), lax.axis_index("y"), lax.axis_index("c")
        me, sibling = (x, y, c), (x, y, 1 - c)
        chips = [(1 - x, y), (x, 1 - y), (1 - x, 1 - y)]

        def slot(a, px, py, pc):
            return out_refs[a].at[4 * px + 2 * py + pc]

        def copy(a, k, block, to, from_input=False):
            return pltpu.make_async_remote_copy(
                src_ref=x_refs[a] if from_input else slot(a, *block), dst_ref=slot(a, *block),
                send_sem=send_sems.at[a, k], recv_sem=recv_sems.at[a, k], device_id=to, device_id_type=pl.DeviceIdType.MESH)

        mine = [pltpu.make_async_copy(x_refs[a], slot(a, *me), local_sems.at[a]) for a in range(n)]
        first = [copy(a, 0, me, sibling, True) for a in range(n)]
        first += [copy(a, 1 + j, me, (*chip, c), True) for j, chip in enumerate(chips) for a in range(n)]
        for cp in mine + first:
            cp.start()
        passed = []
        for j, chip in enumerate(chips):
            for a in range(n):
                copy(a, 1 + j, (*chip, c), me).wait_recv()
                passed.append(copy(a, 4 + j, (*chip, c), sibling))
                passed[-1].start()
        for a in range(n):
            copy(a, 0, sibling, me).wait_recv()
        for j, chip in enumerate(chips):
            for a in range(n):
                copy(a, 4 + j, (*chip, 1 - c), me).wait_recv()
        for cp in first + passed:
            cp.wait_send()
        for cp in mine:
            cp.wait()

    any_spec = pl.BlockSpec(memory_space=pl.ANY)
    return pl.pallas_call(
        body, name="all_gather_weights", out_shape=[jax.ShapeDtypeStruct((N_DEV,) + s.shape, s.dtype) for s in shards],
        in_specs=[any_spec] * n, out_specs=[any_spec] * n,
        scratch_shapes=[pltpu.SemaphoreType.DMA((n, 7)), pltpu.SemaphoreType.DMA((n, 7)), pltpu.SemaphoreType.DMA((n,))],
    )(*shards)


def _all_to_all(name, pieces):
    srcs, n = pieces, len(pieces)

    def body(*refs):
        in_refs, out_refs = refs[:n], refs[n:2 * n]
        send_sems, recv_sems, local_sems = refs[2 * n:]
        my_id = 4 * lax.axis_index("x") + 2 * lax.axis_index("y") + lax.axis_index("c")
        mine = [pltpu.make_async_copy(in_refs[a].at[my_id], out_refs[a].at[my_id], local_sems.at[a]) for a in range(n)]
        copies = _split_copies(in_refs, out_refs, send_sems, recv_sems, False)
        for cp in mine + copies:
            cp.start()
        for cp in copies:
            cp.wait_recv()
        for cp in copies:
            cp.wait_send()
        for cp in mine:
            cp.wait()

    any_spec = pl.BlockSpec(memory_space=pl.ANY)
    return pl.pallas_call(
        body, name=name, out_shape=[jax.ShapeDtypeStruct(s.shape, s.dtype) for s in srcs],
        in_specs=[any_spec] * n, out_specs=[any_spec] * n,
        scratch_shapes=[pltpu.SemaphoreType.DMA((7 * n,)), pltpu.SemaphoreType.DMA((7 * n,)), pltpu.SemaphoreType.DMA((n,))],
    )(*srcs)


def _flip_peers(x, y, c):
    flips = [(fx, fy, fc) for fx in (0, 1) for fy in (0, 1) for fc in (0, 1)][1:]
    return [(x ^ fx, y ^ fy, c ^ fc) for fx, fy, fc in flips]


def _split_copies(in_refs, land_refs, send_sems, recv_sems, gather):
    x, y, c = lax.axis_index("x"), lax.axis_index("y"), lax.axis_index("c")
    my_id = 4 * x + 2 * y + c
    copies = []
    for kk, p in enumerate(_flip_peers(x, y, c)):
        for a in range(len(in_refs)):
            src = in_refs[a] if gather else in_refs[a].at[4 * p[0] + 2 * p[1] + p[2]]
            copies.append(pltpu.make_async_remote_copy(
                src_ref=src, dst_ref=land_refs[a].at[my_id], send_sem=send_sems.at[a * 7 + kk], recv_sem=recv_sems.at[a * 7 + kk],
                device_id=p, device_id_type=pl.DeviceIdType.MESH))
    return copies


def _exchange_start(name, srcs, gather, after=None):
    n = len(srcs)
    first_out = 2 * n + (0 if after is None else 1)

    def body(*refs):
        for cp in _split_copies(refs[:n], refs[n:2 * n], refs[first_out], refs[first_out + 1], gather):
            cp.start()
        refs[-1][...] = jnp.zeros_like(refs[-1])

    hbm, sem = pl.BlockSpec(memory_space=pltpu.HBM), pl.BlockSpec(memory_space=pltpu.SEMAPHORE)
    land_shapes = [((N_DEV,) + s.shape if gather else s.shape, s.dtype) for s in srcs]
    lands = [pltpu.with_memory_space_constraint(lax.empty(shp, dt), pltpu.HBM) for shp, dt in land_shapes]
    srcs = [pltpu.with_memory_space_constraint(s, pltpu.HBM) for s in srcs]
    res = pl.pallas_call(
        body, name=name,
        out_shape=[pltpu.SemaphoreType.DMA((7 * n,)), pltpu.SemaphoreType.DMA((7 * n,))] + [pltpu.HBM(s.shape, s.dtype) for s in srcs]
        + [pltpu.HBM(shp, dt) for shp, dt in land_shapes] + [jax.ShapeDtypeStruct((8, LANES), F32)],
        in_specs=[hbm] * (2 * n) + ([] if after is None else [pl.BlockSpec(memory_space=pl.ANY)]),
        out_specs=[sem, sem] + [hbm] * (2 * n) + [pl.BlockSpec(memory_space=pltpu.VMEM)],
        input_output_aliases={i: 2 + i for i in range(2 * n)},
        compiler_params=pltpu.CompilerParams(has_side_effects=pltpu.SideEffectType.DATAFLOW_SIDE_EFFECTING),
    )(*srcs, *lands, *([] if after is None else [after]))
    return res[0], res[1], res[2:2 + n], res[2 + n:2 + 2 * n], res[-1]


def _exchange_wait(name, handles, after, gather):
    send_sems, recv_sems, srcs, lands, _ = handles
    n = len(srcs)

    def body(*refs):
        for cp in _split_copies(refs[:n], refs[n:2 * n], refs[2 * n], refs[2 * n + 1], gather):
            cp.wait_send()
            cp.wait_recv()

    hbm, sem = pl.BlockSpec(memory_space=pltpu.HBM), pl.BlockSpec(memory_space=pltpu.SEMAPHORE)
    res = pl.pallas_call(
        body, name=name, out_shape=[pltpu.HBM(t.shape, t.dtype) for t in list(srcs) + list(lands)],
        in_specs=[hbm] * (2 * n) + [sem, sem, pl.BlockSpec(memory_space=pl.ANY)], out_specs=[hbm] * (2 * n),
        input_output_aliases={i: i for i in range(2 * n)},
        compiler_params=pltpu.CompilerParams(has_side_effects=pltpu.SideEffectType.DATAFLOW_SIDE_EFFECTING),
    )(*srcs, *lands, send_sems, recv_sems, after)
    my_id = 4 * lax.axis_index("x") + 2 * lax.axis_index("y") + lax.axis_index("c")
    own = [s if gather else lax.dynamic_index_in_dim(s, my_id, 0, keepdims=False) for s in res[:n]]
    return [lax.dynamic_update_index_in_dim(land, o, my_id, 0) for land, o in zip(res[n:], own)]


def _adamw(name, parts, w, m, v):
    rows, cols = w.shape
    tr = _pick(rows, (128, 64, 32, 16, 8))
    pspec = pl.BlockSpec((N_DEV, tr, cols), lambda i: (0, i, 0))
    rspec = pl.BlockSpec((tr, cols), lambda i: (i, 0))

    def body(p_ref, w_ref, m_ref, v_ref, g_ref, d_ref, m2_ref, v2_ref):
        g, d, m2, v2 = _adamw_fn([p_ref[s] for s in range(N_DEV)], w_ref[...], m_ref[...], v_ref[...])
        g_ref[...], d_ref[...], m2_ref[...], v2_ref[...] = g, d, m2, v2

    return pl.pallas_call(
        body, name=name, grid=(rows // tr,), in_specs=[pspec, rspec, rspec, rspec], out_specs=[rspec] * 4,
        out_shape=[jax.ShapeDtypeStruct((rows, cols), F32)] * 4,
        compiler_params=pltpu.CompilerParams(dimension_semantics=("parallel",), vmem_limit_bytes=VMEM_LIMIT),
    )(parts, w, m, v)


def kernel(x, positions, g_mix, w_in, g_q_a, w_q_b, g_kv_a, w_kv_b, g_qn, g_kn, w_mla_out, ret_decay_fwd, ret_decay_bwd, w_ret_out, w_out, g_ffn, w_gate_up, w_down, loss_target, m_g_mix, m_w_in, m_g_q_a, m_w_q_b, m_g_kv_a, m_w_kv_b, m_g_qn, m_g_kn, m_w_mla_out, m_ret_decay_fwd, m_ret_decay_bwd, m_w_ret_out, m_w_out, m_g_ffn, m_w_gate_up, m_w_down, v_g_mix, v_w_in, v_g_q_a, v_w_q_b, v_g_kv_a, v_w_kv_b, v_g_qn, v_g_kn, v_w_mla_out, v_ret_decay_fwd, v_ret_decay_bwd, v_w_ret_out, v_w_out, v_g_ffn, v_w_gate_up, v_w_down):
    w = dict(g_mix=g_mix, w_in=w_in, g_q_a=g_q_a, w_q_b=w_q_b, g_kv_a=g_kv_a, w_kv_b=w_kv_b, g_qn=g_qn, g_kn=g_kn, w_mla_out=w_mla_out,
             ret_decay_fwd=ret_decay_fwd, ret_decay_bwd=ret_decay_bwd, w_ret_out=w_ret_out, w_out=w_out, g_ffn=g_ffn,
             w_gate_up=w_gate_up, w_down=w_down)
    m = dict(g_mix=m_g_mix, w_in=m_w_in, g_q_a=m_g_q_a, w_q_b=m_w_q_b, g_kv_a=m_g_kv_a, w_kv_b=m_w_kv_b, g_qn=m_g_qn, g_kn=m_g_kn,
             w_mla_out=m_w_mla_out, ret_decay_fwd=m_ret_decay_fwd, ret_decay_bwd=m_ret_decay_bwd, w_ret_out=m_w_ret_out, w_out=m_w_out,
             g_ffn=m_g_ffn, w_gate_up=m_w_gate_up, w_down=m_w_down)
    v = dict(g_mix=v_g_mix, w_in=v_w_in, g_q_a=v_g_q_a, w_q_b=v_w_q_b, g_kv_a=v_g_kv_a, w_kv_b=v_w_kv_b, g_qn=v_g_qn, g_kn=v_g_kn,
             w_mla_out=v_w_mla_out, ret_decay_fwd=v_ret_decay_fwd, ret_decay_bwd=v_ret_decay_bwd, w_ret_out=v_w_ret_out, w_out=v_w_out,
             g_ffn=v_g_ffn, w_gate_up=v_w_gate_up, w_down=v_w_down)
    gains = {n: w[n].reshape(1, ln) for n, ln in GAINS}

    axis_of = {n: axis for n, _, axis in MATS}
    later = [n for n, _, _ in MATS if n not in FIRST_WEIGHTS]
    gathered = _all_gather([w[n].astype(WIRE) for n in FIRST_WEIGHTS])
    W = {n: _unshard(g, axis_of[n]) for n, g in zip(FIRST_WEIGHTS, gathered)}
    later_handles = _exchange_start("gather_later_start", [w[n].astype(WIRE) for n in later], True, after=gathered[0])

    def late_weights(after):
        lands = _exchange_wait("gather_later_wait", later_handles, after, True)
        return {n: _unshard(g, axis_of[n]) for n, g in zip(later, lands)}

    grad_groups = []

    def grad_hook(g):
        names = tuple(g)
        handles = _exchange_start("grads_start_%d" % len(grad_groups), [_reshard(g[n], axis_of[n]).astype(GWIRE) for n in names], False)
        grad_groups.append((names, handles))
        return handles[4]

    S = x.shape[1]
    pos = positions.reshape(S, 1).astype(F32)
    loss_rows, grad_x, gG, gW = _local_step(x.reshape(S, D_MODEL), pos, loss_target.reshape(S, D_MODEL), gains, W, late_weights, grad_hook,
                                            start_after=later_handles[4])
    loss = lax.psum(jnp.sum(loss_rows), ("x", "y", "c"))

    last = [n for n, _, _ in MATS if n not in EARLY_GRADS + MID_GRADS]
    pieces = [_reshard(gW[n], axis_of[n]).astype(GWIRE) for n in last]
    pieces.append(jnp.broadcast_to(_pack_gains(gG)[None], (N_DEV, 1, GAIN_PAD)))
    late_parts = _all_to_all("grads_last", pieces)
    parts = dict(zip(last, late_parts))
    for i, (names, handles) in enumerate(grad_groups):
        parts.update(zip(names, _exchange_wait("grads_wait_%d" % i, handles, late_parts[-1], False)))
    out = [dict() for _ in range(4)]
    for n, _, _ in MATS:
        for o, r in zip(out, _adamw("adamw_" + n, parts[n], w[n], m[n], v[n])):
            o[n] = r
    for o, r in zip(out, _adamw("adamw_gains", late_parts[-1], _pack_gains(w), _pack_gains(m), _pack_gains(v))):
        o.update(_unpack_gains(r))
    return (loss, grad_x.reshape(x.shape), *[o[n] for o in out for n in ORDER])
```

```python
import functools

import numpy as np
import jax
import jax.numpy as jnp
from jax import lax
from jax.experimental import pallas as pl
from jax.experimental.pallas import tpu as pltpu

F32 = jnp.float32
MXU = jnp.bfloat16
WIRE = jnp.bfloat16
GWIRE = jnp.bfloat16

N_DEV = 8
D_MODEL = 1024
HEADS = 8
LANES = 128
Q_RANK, KV_RANK = 256, 128
NOPE, ROPE_M, V_M = 64, 32, 64
QK_M = NOPE + ROPE_M
RQK = 64
CHUNK = 128
FFN = 2816
THETA = 10000.0
EPS = 1e-6
LR, B1, B2, AEPS, WD, STEP = 0.001, 0.9, 0.999, 1e-08, 0.01, 10
VMEM_LIMIT = 56 * 1024 * 1024

NN = ((1,), (0,))
NT = ((1,), (1,))
TN = ((0,), (0,))

P_GATES, P_VR, P_GR, P_QR, P_KR, P_CQ, P_CKV, P_KROPE, P_WIDTH = 0, 2048, 3072, 4096, 4608, 5120, 5376, 5504, 5632
O_CQ, O_CKV, O_KROPE, O_QR, O_KR, O_VR, O_GR, O_GATES = 0, 256, 384, 416, 928, 1440, 2464, 3488


def _dot(a, b, dims):
    return lax.dot_general(a, b, (dims, ((), ())), preferred_element_type=F32)


def _pick(dim, cands):
    for c in cands:
        if dim % c == 0:
            return c
    return dim


def _pairs(t):
    return t.reshape(t.shape[0], 4, 2, 2, 32).transpose(0, 1, 3, 2, 4).reshape(t.shape[0], 512)


def _win_pad(w):
    z = jnp.zeros((w.shape[0], 48), w.dtype)
    kr = w[:, O_KROPE:O_KROPE + 32]
    return jnp.concatenate([w[:, O_GATES:], w[:, O_VR:O_VR + 1024], w[:, O_GR:O_GR + 1024], _pairs(w[:, O_QR:O_QR + 512]),
                            _pairs(w[:, O_KR:O_KR + 512]), w[:, :O_CKV], w[:, O_CKV:O_KROPE], kr[:, :16], z, kr[:, 16:], z], axis=1)


def _win_unpad(g):
    return jnp.concatenate([g[:, P_CQ:P_CQ + 256], g[:, P_CKV:P_CKV + 128], g[:, P_KROPE:P_KROPE + 16], g[:, P_KROPE + 64:P_KROPE + 80],
                            _pairs(g[:, P_QR:P_QR + 512]), _pairs(g[:, P_KR:P_KR + 512]), g[:, P_VR:P_VR + 1024],
                            g[:, P_GR:P_GR + 1024], g[:, P_GATES:P_GATES + 2048]], axis=1)


def _qk_pad(t):
    z = jnp.zeros(t.shape[:-1] + (32,), t.dtype)
    return jnp.concatenate([t[..., 64:80], t[..., 0:48], t[..., 80:96], t[..., 48:64], z], axis=-1)


def _qk_unpad(p):
    return jnp.concatenate([p[..., 16:64], p[..., 80:96], p[..., 0:16], p[..., 64:80]], axis=-1)


def _wq_pad(w):
    return _qk_pad(w.reshape(Q_RANK, HEADS, QK_M)).reshape(Q_RANK, HEADS * LANES)


def _wq_unpad(g):
    return _qk_unpad(g.reshape(Q_RANK, HEADS, LANES)).reshape(Q_RANK, HEADS * QK_M)


def _wkv_pad(w):
    t = w.reshape(KV_RANK, HEADS, NOPE + V_M)
    z = lambda n: jnp.zeros((KV_RANK, HEADS, n), w.dtype)
    wk = jnp.concatenate([z(16), t[..., 0:48], z(16), t[..., 48:64], z(32)], axis=-1)
    wv = jnp.concatenate([t[..., 64:128], z(64)], axis=-1)
    return wk.reshape(KV_RANK, HEADS * LANES), wv.reshape(KV_RANK, HEADS * LANES)


def _wkv_unpad(dwk, dwv):
    k, v = dwk.reshape(KV_RANK, HEADS, LANES), dwv.reshape(KV_RANK, HEADS, LANES)
    return jnp.concatenate([k[..., 16:64], k[..., 80:96], v[..., 0:64]], axis=-1).reshape(KV_RANK, HEADS * (NOPE + V_M))


def _wmla_pad(w):
    t = w.reshape(HEADS, V_M, D_MODEL)
    return jnp.concatenate([t, jnp.zeros_like(t)], axis=1).reshape(HEADS * LANES, D_MODEL)


def _wmla_unpad(g):
    return g.reshape(HEADS, LANES, D_MODEL)[:, :V_M].reshape(HEADS * V_M, D_MODEL)


def _rowwise(name, fn, rows, ts, ins, outs, accs=(), ncol=1):
    n_in, n_out, n_acc = len(ins), len(outs), len(accs)

    def colmap(col):
        if callable(col):
            return lambda i, j: (i, col(j))
        return lambda i, j: (i, col)

    arrays, in_specs = [], []
    for arr, spec in ins:
        arrays.append(arr)
        if spec is None:
            in_specs.append(pl.BlockSpec(arr.shape, functools.partial(lambda i, j, nd: (0,) * nd, nd=arr.ndim)))
        else:
            in_specs.append(pl.BlockSpec((ts, spec[0]), colmap(spec[1])))
    out_shape, out_specs = [], []
    for total, dtype, width, col in outs:
        out_shape.append(jax.ShapeDtypeStruct((rows, total), dtype))
        out_specs.append(pl.BlockSpec((ts, width), colmap(col)))
    for shp in accs:
        out_shape.append(jax.ShapeDtypeStruct(shp, F32))
        out_specs.append(pl.BlockSpec(shp, functools.partial(lambda i, j, nd: (0,) * nd, nd=len(shp))))

    def body(*refs):
        vals = [r[...] for r in refs[:n_in]]
        res = fn(*vals)
        if not isinstance(res, (tuple, list)):
            res = (res,)
        for r, v in zip(refs[n_in:n_in + n_out], res[:n_out]):
            r[...] = v.astype(r.dtype)
        if n_acc:
            first = jnp.logical_and(pl.program_id(0) == 0, pl.program_id(1) == 0)
            for r, v in zip(refs[n_in + n_out:], res[n_out:]):
                @pl.when(first)
                def _(r=r):
                    r[...] = jnp.zeros_like(r)
                r[...] += v.astype(F32)

    res = pl.pallas_call(
        body, name=name, grid=(rows // ts, ncol), in_specs=in_specs, out_specs=out_specs, out_shape=out_shape,
        compiler_params=pltpu.CompilerParams(dimension_semantics=("arbitrary", "arbitrary"), vmem_limit_bytes=VMEM_LIMIT),
    )(*arrays)
    return res


MM_OPERAND_BYTES = 24 * 1024 * 1024


def _mm(name, a, b, mode):
    b_halves = b.ndim == 3
    assert not b_halves or mode == "tn"
    if mode == "nn":
        (M, K), N = a.shape, b.shape[1]
    elif mode == "nt":
        (M, K), N = a.shape, b.shape[0]
    else:
        (K, M), N = a.shape, b.shape[-1] * (2 if b_halves else 1)
    tm = _pick(M, (1024, 512, 1408, 256, 128))
    tn = _pick(N // 2 if b_halves else N, (1408, 1024, 512, 256, 128))
    fits = lambda t: 2 * (tm + tn) * t * a.dtype.itemsize <= MM_OPERAND_BYTES
    tk = next(t for t in (K, 4096, 2816, 2048, 1408, 1024, 512, 256, 128) if K % t == 0 and (fits(t) or t == 128))
    nk = K // tk
    dims = {"nn": NN, "nt": NT, "tn": TN}[mode]
    a_spec = pl.BlockSpec((tk, tm), lambda i, j, k: (k, i)) if mode == "tn" else pl.BlockSpec((tm, tk), lambda i, j, k: (i, k))
    if b_halves:
        perj = (N // 2) // tn
        b_spec = pl.BlockSpec((None, tk, tn), lambda i, j, k: (j // perj, k, j % perj))
    else:
        b_spec = pl.BlockSpec((tn, tk), lambda i, j, k: (j, k)) if mode == "nt" else pl.BlockSpec((tk, tn), lambda i, j, k: (k, j))
    o_spec = pl.BlockSpec((tm, tn), lambda i, j, k: (i, j))

    def body(a_ref, b_ref, o_ref):
        d = _dot(a_ref[...], b_ref[...], dims)
        if nk == 1:
            o_ref[...] = d
        else:
            k = pl.program_id(2)

            @pl.when(k == 0)
            def _():
                o_ref[...] = d

            @pl.when(k > 0)
            def _():
                o_ref[...] += d

    return pl.pallas_call(
        body, name=name, grid=(M // tm, N // tn, nk), in_specs=[a_spec, b_spec], out_specs=o_spec,
        out_shape=jax.ShapeDtypeStruct((M, N), F32),
        compiler_params=pltpu.CompilerParams(dimension_semantics=("parallel", "parallel", "arbitrary"), vmem_limit_bytes=VMEM_LIMIT),
    )(a, b)


def _mm_rows(name, a, b, fn, row_ins, whole_ins, outs, accs=(), mode="nn", lhs_fn=None):
    as_windows = lambda ts: [t if isinstance(t, tuple) else (t, (t.shape[1], 0)) for t in ts]
    halves = lhs_fn is None and a.ndim == 3
    assert not halves or mode == "nt"
    lhs_windows = as_windows(a) if lhs_fn is not None else []
    n_lhs = len(lhs_windows) if lhs_fn is not None else 1
    M = lhs_windows[0][0].shape[0] if lhs_fn is not None else a.shape[-2]
    kh = b.shape[0] if lhs_fn is not None else a.shape[-1]
    N = b.shape[1 if mode == "nn" else 0]
    tm = _pick(M, (512, 256, 128))
    windows = as_windows(row_ins)
    n_in, n_out = n_lhs + 1 + len(windows) + len(whole_ins), len(outs) + (1 if lhs_fn is not None else 0)
    row_spec = lambda w, col: pl.BlockSpec((tm, w), functools.partial(lambda i, col: (i, col), col=col))
    if lhs_fn is not None:
        lhs_specs, lhs_args = [row_spec(w, col) for _, (w, col) in lhs_windows], [t for t, _ in lhs_windows]
    else:
        lhs_specs = [pl.BlockSpec((2, tm, kh), lambda i: (0, i, 0)) if halves else pl.BlockSpec((tm, kh), lambda i: (i, 0))]
        lhs_args = [a]

    def body(*refs):
        rhs = refs[n_lhs]
        if halves:
            d = _dot(refs[0][0], rhs[:, :kh], NT) + _dot(refs[0][1], rhs[:, kh:], NT)
        elif lhs_fn is not None:
            lhs = lhs_fn(*[r[...] for r in refs[:n_lhs]], *[r[...] for r in refs[n_in - len(whole_ins):n_in]]).astype(MXU)
            d = _dot(lhs, rhs[...], NN)
        else:
            d = _dot(refs[0][...], rhs[...], NN if mode == "nn" else NT)
        res = tuple(fn(d, *[r[...] for r in refs[n_lhs + 1:n_in]]))
        if lhs_fn is not None:
            res = (lhs,) + res
        for r, v in zip(refs[n_in:n_in + n_out], res[:n_out]):
            r[...] = v.astype(r.dtype)
        for r, v in zip(refs[n_in + n_out:], res[n_out:]):
            @pl.when(pl.program_id(0) == 0)
            def _(r=r):
                r[...] = jnp.zeros_like(r)
            r[...] += v

    whole = lambda t: pl.BlockSpec(t.shape, functools.partial(lambda i, nd: (0,) * nd, nd=t.ndim))
    out_rows = ([(kh, MXU)] if lhs_fn is not None else []) + [(N, dt) for dt in outs]
    return pl.pallas_call(
        body, name=name, grid=(M // tm,),
        in_specs=lhs_specs + [whole(b)] + [row_spec(w, col) for _, (w, col) in windows] + [whole(t) for t in whole_ins],
        out_specs=[row_spec(w, 0) for w, _ in out_rows]
        + [pl.BlockSpec(s, functools.partial(lambda i, nd: (0,) * nd, nd=len(s))) for s in accs],
        out_shape=[jax.ShapeDtypeStruct((M, w), dt) for w, dt in out_rows] + [jax.ShapeDtypeStruct(s, F32) for s in accs],
        compiler_params=pltpu.CompilerParams(dimension_semantics=("arbitrary",), vmem_limit_bytes=VMEM_LIMIT),
    )(*lhs_args, b, *[t for t, _ in windows], *whole_ins)


def _ffn_tiles(S):
    return _pick(S, (1024, 512, 256, 128)), _pick(FFN, (1408, 704, 256, 128))


def _gate_up_swiglu(h2, wgu):
    S, K = h2.shape
    tm, tn = _ffn_tiles(S)
    nj = FFN // tn

    def body(a_ref, bg_ref, bu_ref, gu_ref, act_ref):
        a = a_ref[...]
        g, u = _dot(a, bg_ref[...], NN), _dot(a, bu_ref[...], NN)
        gu_ref[0], gu_ref[1] = g.astype(gu_ref.dtype), u.astype(gu_ref.dtype)
        act_ref[...] = _swiglu_fn(g, u).astype(act_ref.dtype)

    return pl.pallas_call(
        body, name="gate_up_swiglu", grid=(S // tm, nj),
        in_specs=[pl.BlockSpec((tm, K), lambda i, j: (i, 0)), pl.BlockSpec((K, tn), lambda i, j: (0, j)),
                  pl.BlockSpec((K, tn), lambda i, j: (0, nj + j))],
        out_specs=[pl.BlockSpec((2, tm, tn), lambda i, j: (0, i, j)), pl.BlockSpec((tm, tn), lambda i, j: (i, j))],
        out_shape=[jax.ShapeDtypeStruct((2, S, FFN), MXU), jax.ShapeDtypeStruct((S, FFN), MXU)],
        compiler_params=pltpu.CompilerParams(dimension_semantics=("parallel", "parallel"), vmem_limit_bytes=VMEM_LIMIT),
    )(h2, wgu, wgu)


def _d_act_swiglu(dx2, wdown, gu):
    S, K = dx2.shape
    tm, tn = _ffn_tiles(S)

    def body(a_ref, b_ref, gu_ref, o_ref):
        dact = _dot(a_ref[...], b_ref[...], NT)
        _, vjp = jax.vjp(_swiglu_fn, gu_ref[0].astype(F32), gu_ref[1].astype(F32))
        dg, du = vjp(dact)
        o_ref[0], o_ref[1] = dg.astype(o_ref.dtype), du.astype(o_ref.dtype)

    stacked = pl.BlockSpec((2, tm, tn), lambda i, j: (0, i, j))
    return pl.pallas_call(
        body, name="d_act_swiglu", grid=(S // tm, FFN // tn),
        in_specs=[pl.BlockSpec((tm, K), lambda i, j: (i, 0)), pl.BlockSpec((tn, K), lambda i, j: (j, 0)), stacked],
        out_specs=stacked, out_shape=jax.ShapeDtypeStruct((2, S, FFN), MXU),
        compiler_params=pltpu.CompilerParams(dimension_semantics=("parallel", "parallel"), vmem_limit_bytes=VMEM_LIMIT),
    )(dx2, wdown, gu)


@jax.custom_vjp
def _swap64(x):
    return pltpu.roll(x, 64, 1)


_swap64.defvjp(lambda x: (_swap64(x), None), lambda _, g: (_swap64(g),))


@jax.custom_vjp
def _mxdot(a, b):
    return _dot(a.astype(MXU), b.astype(MXU), NN)


def _mxdot_bwd(res, g):
    a, b = res
    gb = g.astype(MXU)
    return _dot(gb, b.astype(MXU), NT), _dot(a.astype(MXU), gb, TN)


_mxdot.defvjp(lambda a, b: (_mxdot(a, b), (a, b)), _mxdot_bwd)


def _row_sum(t):
    if t.shape[-1] == LANES:
        hi = t.astype(jnp.bfloat16)
        lo = (t - hi.astype(F32)).astype(jnp.bfloat16)
        ones = jnp.ones((LANES, LANES), jnp.bfloat16)
        return _dot(hi, ones, NN) + _dot(lo, ones, NN)
    return jnp.sum(t, axis=-1, keepdims=True)


@functools.partial(jax.custom_vjp, nondiff_argnums=(1,))
def _unit_rms(x, n):
    return x * lax.rsqrt(_row_sum(x * x) * (1.0 / n) + EPS)


def _unit_rms_fwd(x, n):
    r = lax.rsqrt(_row_sum(x * x) * (1.0 / n) + EPS)
    y = x * r
    return y, (y, r)


def _unit_rms_bwd(n, res, g):
    y, r = res
    return (r * (g - y * (_row_sum(g * y) * (1.0 / n))),)


_unit_rms.defvjp(_unit_rms_fwd, _unit_rms_bwd)


def _rms(x):
    return _unit_rms(x, x.shape[-1])


def _rmsg_fn(x, g):
    return _rms(x) * g


def _silu(x):
    return x * jax.nn.sigmoid(x)


def _tables_fn(pos, inv_m, sgn_m, inv_r, sgn_r):
    am, ar = pos * inv_m, pos * inv_r
    return jnp.cos(am), jnp.sin(am) * sgn_m, jnp.cos(ar), jnp.sin(ar) * sgn_r


def _head_blocks(t):
    return [t[:, LANES * h:LANES * (h + 1)] for h in range(t.shape[1] // LANES)]


def _mla_prep_fn(cq, ckv, kr, cosm, sinm, gqa, gkva, gqn, gkn, wq, wk, wv):
    cqn = _rms(cq) * gqa
    ckvn = _rms(ckv) * gkva
    q_raw = _mxdot(cqn, wq)
    k_raw = _mxdot(ckvn, wk)
    lane = lax.broadcasted_iota(jnp.int32, (1, HEADS * LANES), 1)
    v = _mxdot(ckvn, wv) + (lane % LANES == V_M).astype(F32)

    def norm_rope(blocks, g, extra):
        outs = []
        for b in blocks:
            if extra is not None:
                b = b + extra
            n = _unit_rms(b, QK_M) * g
            outs.append(n * cosm + _swap64(n) * sinm)
        return jnp.concatenate(outs, axis=1)

    q = norm_rope(_head_blocks(q_raw), gqn, None)
    k = norm_rope(_head_blocks(k_raw), gkn, kr)
    return q, k, v


def _ret_prep_fn(qr, kr, cosr, sinr):
    def rope(t, scale):
        return jnp.concatenate([(b * cosr + _swap64(b) * sinr) * scale for b in _head_blocks(t)], axis=1)
    return rope(qr, 1.0), rope(kr, RQK ** -0.5)


def _ret_post_fn(rf, rb, gr):
    ret = rf + rb
    outs = []
    for b, g in zip(_head_blocks(ret), _head_blocks(gr)):
        outs.append(_silu(g) * _rms(b))
    return jnp.concatenate(outs, axis=1)


def _merge_fn(ga, gb, ya, yb):
    return jax.nn.sigmoid(ga) * ya + jax.nn.sigmoid(gb) * yb


def _swiglu_fn(gate, up):
    return _silu(gate) * up


def _loss_fn(x2, tgt):
    d = x2 - tgt
    return d * (1.0 / D_MODEL), 0.5 * jnp.sum(d * d, axis=0, keepdims=True) * (1.0 / D_MODEL)


def _adamw_fn(parts, w, m, v):
    g = parts[0].astype(F32)
    for p in range(1, N_DEV):
        g = g + parts[p].astype(F32)
    m2 = B1 * m + (1.0 - B1) * g
    v2 = B2 * v + (1.0 - B2) * jnp.square(g)
    m_hat = m2 / (1.0 - B1 ** STEP)
    v_hat = v2 / (1.0 - B2 ** STEP)
    delta = -LR * (m_hat / (jnp.sqrt(v_hat) + AEPS) + WD * w)
    return g, delta, m2, v2


SCALE = QK_M ** -0.5
LOG2E = 1.4426950408889634
FLASH_ROWS = 32


def _flash_fwd(q, k, v):
    S = q.shape[0]
    tk = _pick(S, (512, 256, 128))
    tq = _pick(S, (1024, 512, 256, 128))
    ncb = tk // LANES
    nkv = S // tk
    mrows = 64
    c = SCALE * LOG2E

    def body(q_ref, k_ref, v_ref, o_ref, lse_ref, s_a, p_a, s_b, p_b, m_sc, a_sc, acc_sc):
        m_sc[...] = jnp.full_like(m_sc, -jnp.inf)
        acc_sc[...] = jnp.zeros_like(acc_sc)
        qb = q_ref[...]

        def scores(j, s_buf):
            s_buf[...] = _dot(qb, k_ref[j * tk:(j + 1) * tk, :], NT)

        def stage(j, s_buf, p_buf, s_next):
            if j + 1 < nkv:
                scores(j + 1, s_next)
            for r in range(tq // mrows):
                rows = slice(r * mrows, (r + 1) * mrows)
                cols = [s_buf[rows, LANES * cb:LANES * (cb + 1)] for cb in range(ncb)]
                m_prev = m_sc[rows, :]
                row_max = jnp.max(functools.reduce(jnp.maximum, cols), axis=-1, keepdims=True)
                m_new = jnp.maximum(m_prev, jnp.broadcast_to(row_max, (mrows, LANES)))
                a_sc[rows, :] = jnp.exp2((m_prev - m_new) * c)
                m_sc[rows, :] = m_new
                for cb in range(ncb):
                    p_buf[rows, LANES * cb:LANES * (cb + 1)] = jnp.exp2((cols[cb] - m_new) * c).astype(p_buf.dtype)
            acc_sc[...] = a_sc[...] * acc_sc[...] + _dot(p_buf[...], v_ref[j * tk:(j + 1) * tk, :], NN)

        scores(0, s_a)
        for j in range(nkv):
            stage(j, *((s_a, p_a, s_b) if j % 2 == 0 else (s_b, p_b, s_a)))
        acc = acc_sc[...]
        lane = lax.broadcasted_iota(jnp.int32, (1, LANES), 1)
        l = jnp.sum(jnp.where(lane == V_M, acc, 0.0), axis=-1, keepdims=True)
        o_ref[...] = (acc / l).astype(o_ref.dtype)
        lse_ref[...] = m_sc[...] * c + jnp.log2(jnp.broadcast_to(l, (tq, LANES)))

    qspec = pl.BlockSpec((tq, LANES), lambda h, i: (i, h))
    kspec = pl.BlockSpec((S, LANES), lambda h, i: (0, h))
    return pl.pallas_call(
        body, name="flash_fwd", grid=(HEADS, S // tq), in_specs=[qspec, kspec, kspec], out_specs=[qspec, qspec],
        out_shape=[jax.ShapeDtypeStruct((S, HEADS * LANES), MXU), jax.ShapeDtypeStruct((S, HEADS * LANES), F32)],
        scratch_shapes=[pltpu.VMEM((tq, tk), F32), pltpu.VMEM((tq, tk), MXU)] * 2 + [pltpu.VMEM((tq, LANES), F32)] * 3,
        compiler_params=pltpu.CompilerParams(dimension_semantics=("parallel", "arbitrary"), vmem_limit_bytes=VMEM_LIMIT),
    )(q, k, v)


def _delta_fn(o, do):
    outs = [jnp.broadcast_to(jnp.sum(a * b, axis=-1, keepdims=True), a.shape) for a, b in zip(_head_blocks(o), _head_blocks(do))]
    return do, jnp.concatenate(outs, axis=1)


def _flash_bwd(q, k, v, do, lse, delta):
    S = q.shape[0]
    tq = tk = _pick(S, (512, 256, 128))
    ncb = tk // LANES
    c = SCALE * LOG2E
    nq = S // tq
    nsub = 2 if (S // tk) % 2 == 0 else 1
    stages = [(sub, i) for sub in range(nsub) for i in range(nq)]

    def body(q_ref, k_ref, v_ref, do_ref, lse_ref, dl_ref, dq_ref, dk_ref, dv_ref, s_a, dp_a, p_a, ds_a, s_b, dp_b, p_b, ds_b):
        @pl.when(pl.program_id(1) == 0)
        def _():
            dq_ref[...] = jnp.zeros_like(dq_ref)

        dk_ref[...] = jnp.zeros_like(dk_ref)
        dv_ref[...] = jnp.zeros_like(dv_ref)
        bufs = [(s_a, dp_a, p_a, ds_a), (s_b, dp_b, p_b, ds_b)]

        def scores(sub, i, s_buf, dp_buf):
            kv_rows, q_rows = slice(sub * tk, (sub + 1) * tk), slice(i * tq, (i + 1) * tq)
            s_buf[...] = _dot(q_ref[q_rows, :], k_ref[kv_rows, :], NT)
            dp_buf[...] = _dot(do_ref[q_rows, :], v_ref[kv_rows, :], NT)

        scores(*stages[0], *bufs[0][:2])
        for t, (sub, i) in enumerate(stages):
            s_buf, dp_buf, p_buf, ds_buf = bufs[t % 2]
            if t + 1 < len(stages):
                scores(*stages[t + 1], *bufs[(t + 1) % 2][:2])
            for r in range(tq // FLASH_ROWS):
                rows = slice(r * FLASH_ROWS, (r + 1) * FLASH_ROWS)
                grows = slice(i * tq + r * FLASH_ROWS, i * tq + (r + 1) * FLASH_ROWS)
                lse_b, dl_b = lse_ref[grows, :], dl_ref[grows, :]
                for cb in range(ncb):
                    sl = slice(LANES * cb, LANES * (cb + 1))
                    p = jnp.exp2(s_buf[rows, sl] * c - lse_b)
                    p_buf[rows, sl] = p.astype(p_buf.dtype)
                    ds_buf[rows, sl] = (p * (dp_buf[rows, sl] - dl_b) * SCALE).astype(ds_buf.dtype)
            kv_rows, q_rows = slice(sub * tk, (sub + 1) * tk), slice(i * tq, (i + 1) * tq)
            dv_ref[kv_rows, :] += _dot(p_buf[...], do_ref[q_rows, :], TN)
            dk_ref[kv_rows, :] += _dot(ds_buf[...], q_ref[q_rows, :], TN)
            dq_ref[q_rows, :] += _dot(ds_buf[...], k_ref[kv_rows, :], NN)

    hspec = pl.BlockSpec((S, LANES), lambda h, j: (0, h))
    kspec = pl.BlockSpec((nsub * tk, LANES), lambda h, j: (j, h))
    full = jax.ShapeDtypeStruct((S, HEADS * LANES), F32)
    tile_bufs = [pltpu.VMEM((tq, tk), F32), pltpu.VMEM((tq, tk), F32), pltpu.VMEM((tq, tk), MXU), pltpu.VMEM((tq, tk), MXU)]
    return pl.pallas_call(
        body, name="flash_bwd", grid=(HEADS, S // (nsub * tk)), in_specs=[hspec, kspec, kspec, hspec, hspec, hspec],
        out_specs=[hspec, kspec, kspec], out_shape=[full, full, full],
        scratch_shapes=tile_bufs + tile_bufs,
        compiler_params=pltpu.CompilerParams(dimension_semantics=("parallel", "arbitrary"), vmem_limit_bytes=VMEM_LIMIT),
    )(q, k, v, do, lse, delta)


def _ret_consts(lgh, head, rev):
    C = CHUNK
    lane = lax.broadcasted_iota(jnp.int32, (1, LANES), 1)
    hm = ((lane // 32) % 2 == head % 2).astype(F32)
    r = lax.broadcasted_iota(jnp.int32, (C, C), 0)
    c = lax.broadcasted_iota(jnp.int32, (C, C), 1)
    diff = ((c - r) if rev else (r - c)).astype(F32)
    mask = (diff > 0) if rev else (diff >= 0)
    dpos = jnp.maximum(diff, 0.0)
    din = jnp.where(mask, jnp.exp(lgh * dpos), 0.0)
    idx = lax.broadcasted_iota(jnp.int32, (C, 1), 0).astype(F32)
    eq = (C - idx) if rev else (idx + 1.0)
    ek = idx if rev else (C - 1.0 - idx)
    qd, kd = jnp.exp(lgh * eq), jnp.exp(lgh * ek)
    cd = jnp.exp(lgh * jnp.full((1, 1), float(C), F32))
    return hm, din, dpos, qd, kd, cd, eq, ek


RET_HEADS_PER_STEP = 8


def _ret_fwd(name, qt, kt, proj, lg, rev):
    S = qt.shape[0]
    C = CHUNK
    TB = _pick(S, (512, 256, 128))
    cb, nb = TB // C, S // TB
    hps = RET_HEADS_PER_STEP
    blk = (lambda g: nb - 1 - g) if rev else (lambda g: g)

    def body(lg_ref, q_ref, k_ref, v_ref, o_ref, st_ref, state_sc):
        hg, g = pl.program_id(0), pl.program_id(1)

        @pl.when(g == 0)
        def _():
            state_sc[...] = jnp.zeros_like(state_sc)

        consts = [_ret_consts(lg_ref[hg * hps + u], u, rev) for u in range(hps)]
        order = list(reversed(range(cb))) if rev else list(range(cb))
        units = [(cc, u) for cc in order for u in range(hps)]

        def operands(cc, u):
            rows = pl.ds(cc * C, C)
            pair = slice(LANES * (u // 2), LANES * (u // 2 + 1))
            hm = consts[u][0]
            return q_ref[rows, pair] * hm, k_ref[rows, pair] * hm, v_ref[rows, LANES * u:LANES * (u + 1)].astype(MXU)

        a, inc = {}, {}
        for cc, u in units:
            q, k, v = operands(cc, u)
            a[cc, u] = _dot(q.astype(MXU), k.astype(MXU), NT) * consts[u][1]
            inc[cc, u] = _dot((k * consts[u][4]).astype(MXU), v, TN)
        for u in range(hps):
            st = state_sc[u]
            for cc in order:
                st_ref[u, cc] = st
                st = st * consts[u][5] + inc[cc, u]
            state_sc[u] = st
        for cc, u in units:
            q, _, v = operands(cc, u)
            cross = _dot((q * consts[u][3]).astype(MXU), st_ref[u, cc].astype(MXU), NN)
            o_ref[pl.ds(cc * C, C), LANES * u:LANES * (u + 1)] = _dot(a[cc, u].astype(MXU), v, NN) + cross

    qk_spec = pl.BlockSpec((TB, LANES * hps // 2), lambda h, g: (blk(g), h))
    return pl.pallas_call(
        body, name=name, grid=(HEADS // hps, nb),
        in_specs=[pl.BlockSpec(memory_space=pltpu.SMEM), qk_spec, qk_spec,
                  pl.BlockSpec((TB, LANES * hps), lambda h, g: (blk(g), P_VR // (LANES * hps) + h))],
        out_specs=[pl.BlockSpec((TB, LANES * hps), lambda h, g: (blk(g), h)),
                   pl.BlockSpec((hps, cb, LANES, LANES), lambda h, g: (h, blk(g), 0, 0))],
        out_shape=[jax.ShapeDtypeStruct((S, HEADS * LANES), F32), jax.ShapeDtypeStruct((HEADS, S // C, LANES, LANES), F32)],
        scratch_shapes=[pltpu.VMEM((hps, LANES, LANES), F32)],
        compiler_params=pltpu.CompilerParams(dimension_semantics=("parallel", "arbitrary"), vmem_limit_bytes=VMEM_LIMIT),
    )(lg, qt, kt, proj)


def _ret_bwd(name, qt, kt, proj, dret, states, lg, rev):
    S = qt.shape[0]
    C = CHUNK
    TB = _pick(S, (512, 256, 128))
    cb, nb = TB // C, S // TB
    hps = RET_HEADS_PER_STEP
    blk = (lambda g: g) if rev else (lambda g: nb - 1 - g)

    def body(lg_ref, q_ref, k_ref, v_ref, do_ref, st_ref, dq_ref, dk_ref, dv_ref, dlg_ref, ds_sc, acc_cc, acc_q, acc_k, acc_s):
        hg, g = pl.program_id(0), pl.program_id(1)

        @pl.when(g == 0)
        def _():
            ds_sc[...] = jnp.zeros_like(ds_sc)
            acc_cc[...] = jnp.zeros_like(acc_cc)
            acc_q[...] = jnp.zeros_like(acc_q)
            acc_k[...] = jnp.zeros_like(acc_k)
            acc_s[...] = jnp.zeros_like(acc_s)

        lgs = [lg_ref[hg * hps + u] for u in range(hps)]
        consts = [_ret_consts(lgs[u], u, rev) for u in range(hps)]
        order = list(range(cb)) if rev else list(reversed(range(cb)))
        units = [(cc, u) for cc in order for u in range(hps)]

        def operands(cc, u):
            rows = pl.ds(cc * C, C)
            pair = slice(LANES * (u // 2), LANES * (u // 2 + 1))
            head = slice(LANES * u, LANES * (u + 1))
            hm = consts[u][0]
            return q_ref[rows, pair] * hm, k_ref[rows, pair] * hm, v_ref[rows, head].astype(MXU), do_ref[rows, head].astype(MXU)

        a, dp, dqs, inc = {}, {}, {}, {}
        for cc, u in units:
            q, k, vb, dob = operands(cc, u)
            a[cc, u] = _dot(q.astype(MXU), k.astype(MXU), NT)
            dp[cc, u] = _dot(dob, vb, NT)
            dqs[cc, u] = _dot(dob, st_ref[u, cc].astype(MXU), NT)
            inc[cc, u] = _dot((q * consts[u][3]).astype(MXU), dob, TN)
        dsn = {}
        for u in range(hps):
            ds = ds_sc[u]
            for cc in order:
                dsn[cc, u] = ds
                ds = ds * consts[u][5] + inc[cc, u]
            ds_sc[u] = ds
        even = {}
        for cc, u in units:
            hm, din, dpos, qd, kd, cd, eq, ek = consts[u]
            rows, head = pl.ds(cc * C, C), slice(LANES * u, LANES * (u + 1))
            q, k, vb, dob = operands(cc, u)
            qb, kb = q.astype(MXU), k.astype(MXU)
            dsnb = dsn[cc, u].astype(MXU)
            da = (dp[cc, u] * din).astype(MXU)
            vds = _dot(vb, dsnb, NT)
            dq_u = (_dot(da, kb, NN) + dqs[cc, u] * qd) * hm
            dk_u = (_dot(da, qb, TN) + vds * kd) * hm
            if u % 2 == 0:
                even[cc] = (dq_u, dk_u)
            else:
                pair = slice(LANES * (u // 2), LANES * (u // 2 + 1))
                dq_ref[rows, pair] = even[cc][0] + dq_u
                dk_ref[rows, pair] = even[cc][1] + dk_u
            dv_ref[rows, head] = _dot((a[cc, u] * din).astype(MXU), dob, TN) + _dot((k * kd).astype(MXU), dsnb, NN)
            acc_cc[u] += dp[cc, u] * a[cc, u] * din * dpos
            acc_q[u] += dqs[cc, u] * q * (qd * eq)
            acc_k[u] += vds * k * (kd * ek)
            acc_s[u] += dsn[cc, u] * st_ref[u, cc] * (cd * float(C))

        @pl.when(g == nb - 1)
        def _():
            for u in range(hps):
                tot = (jnp.sum(acc_cc[u], keepdims=True) + jnp.sum(acc_q[u], keepdims=True)
                       + jnp.sum(acc_k[u], keepdims=True) + jnp.sum(acc_s[u], keepdims=True))
                dlg_ref[u] = jnp.broadcast_to(tot * lgs[u], (8, LANES))

    full = jax.ShapeDtypeStruct((S, HEADS * LANES), F32)
    hspec = pl.BlockSpec((TB, LANES * hps), lambda h, g: (blk(g), h))
    qk_spec = pl.BlockSpec((TB, LANES * hps // 2), lambda h, g: (blk(g), h))
    return pl.pallas_call(
        body, name=name, grid=(HEADS // hps, nb),
        in_specs=[pl.BlockSpec(memory_space=pltpu.SMEM), qk_spec, qk_spec,
                  pl.BlockSpec((TB, LANES * hps), lambda h, g: (blk(g), P_VR // (LANES * hps) + h)),
                  hspec,
                  pl.BlockSpec((hps, cb, LANES, LANES), lambda h, g: (h, blk(g), 0, 0))],
        out_specs=[qk_spec, qk_spec, hspec, pl.BlockSpec((hps, 8, LANES), lambda h, g: (h, 0, 0))],
        out_shape=[jax.ShapeDtypeStruct(qt.shape, F32), jax.ShapeDtypeStruct(kt.shape, F32), full,
                   jax.ShapeDtypeStruct((HEADS, 8, LANES), F32)],
        scratch_shapes=[pltpu.VMEM((hps, LANES, LANES), F32), pltpu.VMEM((hps, C, C), F32), pltpu.VMEM((hps, C, LANES), F32),
                        pltpu.VMEM((hps, C, LANES), F32), pltpu.VMEM((hps, LANES, LANES), F32)],
        compiler_params=pltpu.CompilerParams(dimension_semantics=("parallel", "arbitrary"), vmem_limit_bytes=VMEM_LIMIT),
    )(lg, qt, kt, proj, dret, states)


def _rope_consts():
    inv16 = THETA ** (-jnp.arange(16, dtype=F32) / 16)
    inv32 = THETA ** (-jnp.arange(32, dtype=F32) / 32)
    lane = np.arange(LANES)
    z48 = jnp.zeros((48,), F32)
    inv_m = jnp.concatenate([inv16, z48, inv16, z48])[None, :]
    sgn_m = jnp.asarray(np.where(lane < 16, -1.0, np.where((lane >= 64) & (lane < 80), 1.0, 0.0)), F32)[None, :]
    inv_r = jnp.concatenate([inv32] * 4)[None, :]
    sgn_r = jnp.asarray(np.where(lane < 64, -1.0, 1.0), F32)[None, :]
    return inv_m, sgn_m, inv_r, sgn_r


FIRST_WEIGHTS = ("w_in", "w_q_b", "w_kv_b")
EARLY_GRADS = ("w_down", "w_gate_up", "w_out", "w_ret_out")
MID_GRADS = ("w_mla_out", "w_in")


def _local_step(x, pos, tgt, gains, W, late_weights=None, grad_hook=None, start_after=None):
    S = x.shape[0]
    ts = _pick(S, (256, 128))
    ts_light = _pick(S, (512, 256, 128))
    R = lambda a, w=None, c=0: (a, ((a.shape[1] if w is None else w), c))
    W_ = lambda a: (a, None)

    win = _win_pad(W["w_in"])
    wq = _wq_pad(W["w_q_b"])
    wk, wv = _wkv_pad(W["w_kv_b"])
    gqn, gkn = _qk_pad(gains["g_qn"]), _qk_pad(gains["g_kn"])
    g_mix, g_q_a, g_kv_a, g_ffn = gains["g_mix"], gains["g_q_a"], gains["g_kv_a"], gains["g_ffn"]
    lg_f = -jnp.exp(gains["ret_decay_fwd"][0])
    lg_b = -jnp.exp(gains["ret_decay_bwd"][0])

    consts = list(_rope_consts())
    cosm, sinm, cosr, sinr = _rowwise("rope_tables", _tables_fn, S, ts_light,[R(pos)] + [W_(c) for c in consts],
                                      [(LANES, F32, LANES, 0)] * 4)

    h, proj = _mm_rows("rms_mix_in_proj", [x], win, lambda d, *_: (d,), [], [g_mix] + ([] if start_after is None else [start_after]),
                       [F32], lhs_fn=lambda xx, g, *_: _rmsg_fn(xx, g))
    seg = lambda off, w: (proj, (w, off // w))
    mla_ins = [seg(P_CQ, 256), seg(P_CKV, 128), seg(P_KROPE, 128), R(cosm), R(sinm),
               W_(g_q_a), W_(g_kv_a), W_(gqn), W_(gkn), W_(wq), W_(wk), W_(wv)]
    q, k, v = _rowwise("mla_prep", _mla_prep_fn, S, ts, mla_ins, [(HEADS * LANES, MXU, HEADS * LANES, 0)] * 3)
    o_bf, lse = _flash_fwd(q, k, v)
    if late_weights is not None:
        W = {**W, **late_weights(lse)}
    wmla = _wmla_pad(W["w_mla_out"])
    wret, wout, wgu, wdown = W["w_ret_out"], W["w_out"], W["w_gate_up"], W["w_down"]
    y_a = _mm("mla_out", o_bf, wmla, "nn")

    ret_ins = [seg(P_QR, 512), seg(P_KR, 512), R(cosr), R(sinr)]
    qt, kt = _rowwise("ret_prep", _ret_prep_fn, S, ts_light,ret_ins, [(512, F32, 512, 0)] * 2)
    ret_f, st_f = _ret_fwd("ret_fwd_f", qt, kt, proj, lg_f, False)
    ret_b, st_b = _ret_fwd("ret_fwd_b", qt, kt, proj, lg_b, True)
    post_ins = [R(ret_f), R(ret_b), seg(P_GR, 1024)]
    o_b, y_b, merged = _mm_rows("ret_post_out_merge", post_ins, wret, lambda yb, ga, gb, ya: (yb, _merge_fn(ga, gb, ya, yb)),
                                [seg(P_GATES, 1024), (proj, (1024, 1)), R(y_a)], [], [F32, MXU], lhs_fn=_ret_post_fn)
    merge_ins = [seg(P_GATES, 1024), (proj, (1024, 1)), R(y_a), R(y_b)]
    def residual_rms(d, xx, g):
        r = d + xx
        return r, _rmsg_fn(r, g)

    x1, h2 = _mm_rows("out_proj_rms_ffn", merged, wout, residual_rms, [x], [g_ffn], [F32, MXU])
    gu, act = _gate_up_swiglu(h2, wgu)

    def residual_loss(d, xx, t):
        dx, rows = _loss_fn(d + xx, t)
        return dx, dx, rows

    dx2, dx2_bf, loss_rows = _mm_rows("down_proj_loss", act, wdown, residual_loss, [x1, tgt], [], [F32, MXU], accs=[(1, D_MODEL)])

    gW = {}
    gW["w_down"] = _mm("d_w_down", act, dx2_bf, "tn")
    dgu = _d_act_swiglu(dx2_bf, wdown, gu)
    gW["w_gate_up"] = _mm("d_w_gate_up", h2, dgu, "tn")
    def rms_bwd(xx, g, dh, dres):
        _, vjp = jax.vjp(_rmsg_fn, xx, g)
        dx, dg = vjp(dh)
        dx = dx + dres
        return dx, dx, dg

    dx1, dx1_bf, dg_ffn = _mm_rows("d_h2_rms_ffn_bwd", dgu, wgu, lambda dh, xx, dres, g: rms_bwd(xx, g, dh, dres),
                                   [x1, dx2], [g_ffn], [F32, MXU], accs=[(1, D_MODEL)], mode="nt")
    gW["w_out"] = _mm("d_w_out", merged, dx1_bf, "tn")
    def merge_bwd(dm, ga, gb, ya, yb):
        _, vjp = jax.vjp(_merge_fn, ga, gb, ya, yb)
        return vjp(dm)

    dga, dgb, dy_a, dy_b = _mm_rows("d_merged_merge_bwd", dx1_bf, wout, merge_bwd, merge_ins, [], [MXU] * 4, mode="nt")
    gW["w_ret_out"] = _mm("d_w_ret_out", o_b, dy_b, "tn")
    after_early = [] if grad_hook is None else [grad_hook({n: gW[n] for n in EARLY_GRADS})]

    def post_bwd(dob, rf, rb, gr, *_):
        _, vjp = jax.vjp(_ret_post_fn, rf, rb, gr)
        drf, _, dgr = vjp(dob)
        return drf, dgr

    dret, dg_r = _mm_rows("d_o_b_ret_post_bwd", dy_b, wret, post_bwd, post_ins, after_early, [MXU, MXU], mode="nt")
    dq_f, dk_f, dv_f, dlg_f = _ret_bwd("ret_bwd_f", qt, kt, proj, dret, st_f, lg_f, False)
    dq_b, dk_b, dv_b, dlg_b = _ret_bwd("ret_bwd_b", qt, kt, proj, dret, st_b, lg_b, True)

    def ret_prep_bwd(qr, kr, cosr_, sinr_, dqf, dqb, dkf, dkb, dvf, dvb):
        _, vjp = jax.vjp(lambda a, b: _ret_prep_fn(a, b, cosr_, sinr_), qr, kr)
        dqr, dkr = vjp((dqf + dqb, dkf + dkb))
        return dqr, dkr, dvf + dvb

    dq_r, dk_r, dv_r = _rowwise("ret_prep_bwd", ret_prep_bwd, S, ts_light,ret_ins + [R(t) for t in (dq_f, dq_b, dk_f, dk_b, dv_f, dv_b)],
                                [(512, MXU, 512, 0), (512, MXU, 512, 0), (1024, MXU, 1024, 0)])

    gW_mla_p = _mm("d_w_mla_out", o_bf, dy_a, "tn")
    do_bf, delta = _mm_rows("d_o_attn_delta", dy_a, wmla, lambda d, oo, *_: _delta_fn(oo.astype(F32), d), [o_bf], after_early, [MXU, F32], mode="nt")
    dq, dk, dv = _flash_bwd(q, k, v, do_bf, lse, delta)

    def mla_prep_bwd(cq, ckv, kr, cosm_, sinm_, gqa, gkva, gqn_, gkn_, wq_, wk_, wv_, dq_, dk_, dv_):
        f = lambda cq, ckv, kr, gqa, gkva, gqn_, gkn_, wq_, wk_, wv_: _mla_prep_fn(cq, ckv, kr, cosm_, sinm_, gqa, gkva, gqn_, gkn_, wq_, wk_, wv_)
        _, vjp = jax.vjp(f, cq, ckv, kr, gqa, gkva, gqn_, gkn_, wq_.astype(F32), wk_.astype(F32), wv_.astype(F32))
        return vjp((dq_, dk_, dv_))

    mb = _rowwise("mla_prep_bwd", mla_prep_bwd, S, ts, mla_ins + [R(dq), R(dk), R(dv)],
                  [(256, MXU, 256, 0), (128, MXU, 128, 0), (128, MXU, 128, 0)],
                  accs=[(1, 256), (1, 128), (1, LANES), (1, LANES), (256, HEADS * LANES), (128, HEADS * LANES), (128, HEADS * LANES)])
    dc_q, dc_kv, dk_rope, dg_q_a, dg_kv_a, dgqn_p, dgkn_p, dwq_p, dwk_p, dwv_p = mb

    dproj = jnp.concatenate([dga, dgb, dv_r, dg_r, dq_r, dk_r, dc_q, dc_kv, dk_rope], axis=1)
    gW["w_in"] = _win_unpad(_mm("d_w_in", h, dproj, "tn"))
    gW["w_mla_out"] = _wmla_unpad(gW_mla_p)
    after_mid = None if grad_hook is None else grad_hook({n: gW[n] for n in MID_GRADS})
    grad_x, dg_mix = _mm_rows("d_h_rms_mix_bwd", dproj, win, lambda dh, xx, dres, g, *_: rms_bwd(xx, g, dh, dres)[1:],
                              [x, dx1], [g_mix] + ([] if after_mid is None else [after_mid]), [F32], accs=[(1, D_MODEL)], mode="nt")
    gW["w_q_b"] = _wq_unpad(dwq_p)
    gW["w_kv_b"] = _wkv_unpad(dwk_p, dwv_p)
    gG = {"g_mix": dg_mix, "g_q_a": dg_q_a, "g_kv_a": dg_kv_a, "g_qn": _qk_unpad(dgqn_p),
          "g_kn": _qk_unpad(dgkn_p), "ret_decay_fwd": dlg_f[:, 0, 0][None, :], "ret_decay_bwd": dlg_b[:, 0, 0][None, :],
          "g_ffn": dg_ffn}
    return loss_rows, grad_x, gG, gW


MATS = [("w_in", (1024, 5536), 1), ("w_q_b", (256, 768), 1), ("w_kv_b", (128, 1024), 1), ("w_mla_out", (512, 1024), 1),
        ("w_ret_out", (1024, 1024), 0), ("w_out", (1024, 1024), 0), ("w_gate_up", (1024, 5632), 1), ("w_down", (2816, 1024), 0)]
GAINS = [("g_mix", 1024), ("g_q_a", 256), ("g_kv_a", 128), ("g_qn", 96), ("g_kn", 96), ("ret_decay_fwd", 8), ("ret_decay_bwd", 8),
         ("g_ffn", 1024)]
ORDER = ["g_mix", "w_in", "g_q_a", "w_q_b", "g_kv_a", "w_kv_b", "g_qn", "g_kn", "w_mla_out", "ret_decay_fwd", "ret_decay_bwd",
         "w_ret_out", "w_out", "g_ffn", "w_gate_up", "w_down"]
GAIN_LEN = sum(n for _, n in GAINS)
GAIN_PAD = -(-GAIN_LEN // LANES) * LANES


def _pack_gains(d):
    row = jnp.concatenate([d[n].reshape(1, ln).astype(F32) for n, ln in GAINS], axis=1)
    return jnp.pad(row, ((0, 0), (0, GAIN_PAD - GAIN_LEN)))


def _unpack_gains(row):
    out, off = {}, 0
    for n, ln in GAINS:
        out[n] = row[0, off:off + ln]
        off += ln
    return out


def _unshard(pieces, axis):
    if axis == 0:
        return pieces.reshape((N_DEV * pieces.shape[1], pieces.shape[2]))
    return jnp.concatenate([pieces[p] for p in range(N_DEV)], axis=1)


def _reshard(full, axis):
    if axis == 0:
        return full.reshape((N_DEV, full.shape[0] // N_DEV, full.shape[1]))
    c = full.shape[1] // N_DEV
    return jnp.stack([full[:, c * p:c * (p + 1)] for p in range(N_DEV)])


def _all_gather(shards):
    n = len(shards)

    def body(*refs):
        x_refs, out_refs = refs[:n], refs[n:2 * n]
        send_sems, recv_sems, local_sems = refs[2 * n:]
        x, y, c = lax.axis_index("x"), lax.axis_index("y"), lax.axis_index("c")
        me, sibling = (x, y, c), (x, y, 1 - c)
        chips = [(1 - x, y), (x, 1 - y), (1 - x, 1 - y)]

        def slot(a, px, py, pc):
            return out_refs[a].at[4 * px + 2 * py + pc]

        def copy(a, k, block, to, from_input=False):
            return pltpu.make_async_remote_copy(
                src_ref=x_refs[a] if from_input else slot(a, *block), dst_ref=slot(a, *block),
                send_sem=send_sems.at[a, k], recv_sem=recv_sems.at[a, k], device_id=to, device_id_type=pl.DeviceIdType.MESH)

        mine = [pltpu.make_async_copy(x_refs[a], slot(a, *me), local_sems.at[a]) for a in range(n)]
        first = [copy(a, 0, me, sibling, True) for a in range(n)]
        first += [copy(a, 1 + j, me, (*chip, c), True) for j, chip in enumerate(chips) for a in range(n)]
        for cp in mine + first:
            cp.start()
        passed = []
        for j, chip in enumerate(chips):
            for a in range(n):
                copy(a, 1 + j, (*chip, c), me).wait_recv()
                passed.append(copy(a, 4 + j, (*chip, c), sibling))
                passed[-1].start()
        for a in range(n):
            copy(a, 0, sibling, me).wait_recv()
        for j, chip in enumerate(chips):
            for a in range(n):
                copy(a, 4 + j, (*chip, 1 - c), me).wait_recv()
        for cp in first + passed:
            cp.wait_send()
        for cp in mine:
            cp.wait()

    any_spec = pl.BlockSpec(memory_space=pl.ANY)
    return pl.pallas_call(
        body, name="all_gather_weights", out_shape=[jax.ShapeDtypeStruct((N_DEV,) + s.shape, s.dtype) for s in shards],
        in_specs=[any_spec] * n, out_specs=[any_spec] * n,
        scratch_shapes=[pltpu.SemaphoreType.DMA((n, 7)), pltpu.SemaphoreType.DMA((n, 7)), pltpu.SemaphoreType.DMA((n,))],
    )(*shards)


def _all_to_all(name, pieces):
    srcs, n = pieces, len(pieces)

    def body(*refs):
        in_refs, out_refs = refs[:n], refs[n:2 * n]
        send_sems, recv_sems, local_sems = refs[2 * n:]
        my_id = 4 * lax.axis_index("x") + 2 * lax.axis_index("y") + lax.axis_index("c")
        mine = [pltpu.make_async_copy(in_refs[a].at[my_id], out_refs[a].at[my_id], local_sems.at[a]) for a in range(n)]
        copies = _split_copies(in_refs, out_refs, send_sems, recv_sems, False)
        for cp in mine + copies:
            cp.start()
        for cp in copies:
            cp.wait_recv()
        for cp in copies:
            cp.wait_send()
        for cp in mine:
            cp.wait()

    any_spec = pl.BlockSpec(memory_space=pl.ANY)
    return pl.pallas_call(
        body, name=name, out_shape=[jax.ShapeDtypeStruct(s.shape, s.dtype) for s in srcs],
        in_specs=[any_spec] * n, out_specs=[any_spec] * n,
        scratch_shapes=[pltpu.SemaphoreType.DMA((7 * n,)), pltpu.SemaphoreType.DMA((7 * n,)), pltpu.SemaphoreType.DMA((n,))],
    )(*srcs)


def _flip_peers(x, y, c):
    flips = [(fx, fy, fc) for fx in (0, 1) for fy in (0, 1) for fc in (0, 1)][1:]
    return [(x ^ fx, y ^ fy, c ^ fc) for fx, fy, fc in flips]


def _split_copies(in_refs, land_refs, send_sems, recv_sems, gather):
    x, y, c = lax.axis_index("x"), lax.axis_index("y"), lax.axis_index("c")
    my_id = 4 * x + 2 * y + c
    copies = []
    for kk, p in enumerate(_flip_peers(x, y, c)):
        for a in range(len(in_refs)):
            src = in_refs[a] if gather else in_refs[a].at[4 * p[0] + 2 * p[1] + p[2]]
            copies.append(pltpu.make_async_remote_copy(
                src_ref=src, dst_ref=land_refs[a].at[my_id], send_sem=send_sems.at[a * 7 + kk], recv_sem=recv_sems.at[a * 7 + kk],
                device_id=p, device_id_type=pl.DeviceIdType.MESH))
    return copies


def _exchange_start(name, srcs, gather, after=None):
    n = len(srcs)
    first_out = 2 * n + (0 if after is None else 1)

    def body(*refs):
        for cp in _split_copies(refs[:n], refs[n:2 * n], refs[first_out], refs[first_out + 1], gather):
            cp.start()
        refs[-1][...] = jnp.zeros_like(refs[-1])

    hbm, sem = pl.BlockSpec(memory_space=pltpu.HBM), pl.BlockSpec(memory_space=pltpu.SEMAPHORE)
    land_shapes = [((N_DEV,) + s.shape if gather else s.shape, s.dtype) for s in srcs]
    lands = [pltpu.with_memory_space_constraint(lax.empty(shp, dt), pltpu.HBM) for shp, dt in land_shapes]
    srcs = [pltpu.with_memory_space_constraint(s, pltpu.HBM) for s in srcs]
    res = pl.pallas_call(
        body, name=name,
        out_shape=[pltpu.SemaphoreType.DMA((7 * n,)), pltpu.SemaphoreType.DMA((7 * n,))] + [pltpu.HBM(s.shape, s.dtype) for s in srcs]
        + [pltpu.HBM(shp, dt) for shp, dt in land_shapes] + [jax.ShapeDtypeStruct((8, LANES), F32)],
        in_specs=[hbm] * (2 * n) + ([] if after is None else [pl.BlockSpec(memory_space=pl.ANY)]),
        out_specs=[sem, sem] + [hbm] * (2 * n) + [pl.BlockSpec(memory_space=pltpu.VMEM)],
        input_output_aliases={i: 2 + i for i in range(2 * n)},
        compiler_params=pltpu.CompilerParams(has_side_effects=pltpu.SideEffectType.DATAFLOW_SIDE_EFFECTING),
    )(*srcs, *lands, *([] if after is None else [after]))
    return res[0], res[1], res[2:2 + n], res[2 + n:2 + 2 * n], res[-1]


def _exchange_wait(name, handles, after, gather):
    send_sems, recv_sems, srcs, lands, _ = handles
    n = len(srcs)

    def body(*refs):
        for cp in _split_copies(refs[:n], refs[n:2 * n], refs[2 * n], refs[2 * n + 1], gather):
            cp.wait_send()
            cp.wait_recv()

    hbm, sem = pl.BlockSpec(memory_space=pltpu.HBM), pl.BlockSpec(memory_space=pltpu.SEMAPHORE)
    res = pl.pallas_call(
        body, name=name, out_shape=[pltpu.HBM(t.shape, t.dtype) for t in list(srcs) + list(lands)],
        in_specs=[hbm] * (2 * n) + [sem, sem, pl.BlockSpec(memory_space=pl.ANY)], out_specs=[hbm] * (2 * n),
        input_output_aliases={i: i for i in range(2 * n)},
        compiler_params=pltpu.CompilerParams(has_side_effects=pltpu.SideEffectType.DATAFLOW_SIDE_EFFECTING),
    )(*srcs, *lands, send_sems, recv_sems, after)
    my_id = 4 * lax.axis_index("x") + 2 * lax.axis_index("y") + lax.axis_index("c")
    own = [s if gather else lax.dynamic_index_in_dim(s, my_id, 0, keepdims=False) for s in res[:n]]
    return [lax.dynamic_update_index_in_dim(land, o, my_id, 0) for land, o in zip(res[n:], own)]


def _adamw(name, parts, w, m, v):
    rows, cols = w.shape
    tr = _pick(rows, (128, 64, 32, 16, 8))
    pspec = pl.BlockSpec((N_DEV, tr, cols), lambda i: (0, i, 0))
    rspec = pl.BlockSpec((tr, cols), lambda i: (i, 0))

    def body(p_ref, w_ref, m_ref, v_ref, g_ref, d_ref, m2_ref, v2_ref):
        g, d, m2, v2 = _adamw_fn([p_ref[s] for s in range(N_DEV)], w_ref[...], m_ref[...], v_ref[...])
        g_ref[...], d_ref[...], m2_ref[...], v2_ref[...] = g, d, m2, v2

    return pl.pallas_call(
        body, name=name, grid=(rows // tr,), in_specs=[pspec, rspec, rspec, rspec], out_specs=[rspec] * 4,
        out_shape=[jax.ShapeDtypeStruct((rows, cols), F32)] * 4,
        compiler_params=pltpu.CompilerParams(dimension_semantics=("parallel",), vmem_limit_bytes=VMEM_LIMIT),
    )(parts, w, m, v)


def kernel(x, positions, g_mix, w_in, g_q_a, w_q_b, g_kv_a, w_kv_b, g_qn, g_kn, w_mla_out, ret_decay_fwd, ret_decay_bwd, w_ret_out, w_out, g_ffn, w_gate_up, w_down, loss_target, m_g_mix, m_w_in, m_g_q_a, m_w_q_b, m_g_kv_a, m_w_kv_b, m_g_qn, m_g_kn, m_w_mla_out, m_ret_decay_fwd, m_ret_decay_bwd, m_w_ret_out, m_w_out, m_g_ffn, m_w_gate_up, m_w_down, v_g_mix, v_w_in, v_g_q_a, v_w_q_b, v_g_kv_a, v_w_kv_b, v_g_qn, v_g_kn, v_w_mla_out, v_ret_decay_fwd, v_ret_decay_bwd, v_w_ret_out, v_w_out, v_g_ffn, v_w_gate_up, v_w_down):
    w = dict(g_mix=g_mix, w_in=w_in, g_q_a=g_q_a, w_q_b=w_q_b, g_kv_a=g_kv_a, w_kv_b=w_kv_b, g_qn=g_qn, g_kn=g_kn, w_mla_out=w_mla_out,
             ret_decay_fwd=ret_decay_fwd, ret_decay_bwd=ret_decay_bwd, w_ret_out=w_ret_out, w_out=w_out, g_ffn=g_ffn,
             w_gate_up=w_gate_up, w_down=w_down)
    m = dict(g_mix=m_g_mix, w_in=m_w_in, g_q_a=m_g_q_a, w_q_b=m_w_q_b, g_kv_a=m_g_kv_a, w_kv_b=m_w_kv_b, g_qn=m_g_qn, g_kn=m_g_kn,
             w_mla_out=m_w_mla_out, ret_decay_fwd=m_ret_decay_fwd, ret_decay_bwd=m_ret_decay_bwd, w_ret_out=m_w_ret_out, w_out=m_w_out,
             g_ffn=m_g_ffn, w_gate_up=m_w_gate_up, w_down=m_w_down)
    v = dict(g_mix=v_g_mix, w_in=v_w_in, g_q_a=v_g_q_a, w_q_b=v_w_q_b, g_kv_a=v_g_kv_a, w_kv_b=v_w_kv_b, g_qn=v_g_qn, g_kn=v_g_kn,
             w_mla_out=v_w_mla_out, ret_decay_fwd=v_ret_decay_fwd, ret_decay_bwd=v_ret_decay_bwd, w_ret_out=v_w_ret_out, w_out=v_w_out,
             g_ffn=v_g_ffn, w_gate_up=v_w_gate_up, w_down=v_w_down)
    gains = {n: w[n].reshape(1, ln) for n, ln in GAINS}

    axis_of = {n: axis for n, _, axis in MATS}
    later = [n for n, _, _ in MATS if n not in FIRST_WEIGHTS]
    gathered = _all_gather([w[n].astype(WIRE) for n in FIRST_WEIGHTS])
    W = {n: _unshard(g, axis_of[n]) for n, g in zip(FIRST_WEIGHTS, gathered)}
    later_handles = _exchange_start("gather_later_start", [w[n].astype(WIRE) for n in later], True, after=gathered[0])

    def late_weights(after):
        lands = _exchange_wait("gather_later_wait", later_handles, after, True)
        return {n: _unshard(g, axis_of[n]) for n, g in zip(later, lands)}

    grad_groups = []

    def grad_hook(g):
        names = tuple(g)
        handles = _exchange_start("grads_start_%d" % len(grad_groups), [_reshard(g[n], axis_of[n]).astype(GWIRE) for n in names], False)
        grad_groups.append((names, handles))
        return handles[4]

    S = x.shape[1]
    pos = positions.reshape(S, 1).astype(F32)
    loss_rows, grad_x, gG, gW = _local_step(x.reshape(S, D_MODEL), pos, loss_target.reshape(S, D_MODEL), gains, W, late_weights, grad_hook,
                                            start_after=later_handles[4])
    loss = lax.psum(jnp.sum(loss_rows), ("x", "y", "c"))

    last = [n for n, _, _ in MATS if n not in EARLY_GRADS + MID_GRADS]
    pieces = [_reshard(gW[n], axis_of[n]).astype(GWIRE) for n in last]
    pieces.append(jnp.broadcast_to(_pack_gains(gG)[None], (N_DEV, 1, GAIN_PAD)))
    late_parts = _all_to_all("grads_last", pieces)
    parts = dict(zip(last, late_parts))
    for i, (names, handles) in enumerate(grad_groups):
        parts.update(zip(names, _exchange_wait("grads_wait_%d" % i, handles, late_parts[-1], False)))
    out = [dict() for _ in range(4)]
    for n, _, _ in MATS:
        for o, r in zip(out, _adamw("adamw_" + n, parts[n], w[n], m[n], v[n])):
            o[n] = r
    for o, r in zip(out, _adamw("adamw_gains", late_parts[-1], _pack_gains(w), _pack_gains(m), _pack_gains(v))):
        o.update(_unpack_gains(r))
    return (loss, grad_x.reshape(x.shape), *[o[n] for o in out for n in ORDER])
```

```python
import functools

import numpy as np
import jax
import jax.numpy as jnp
from jax import lax
from jax.experimental import pallas as pl
from jax.experimental.pallas import tpu as pltpu

F32 = jnp.float32
MXU = jnp.bfloat16
WIRE = jnp.bfloat16
GWIRE = jnp.bfloat16

N_DEV = 8
D_MODEL = 1024
HEADS = 8
LANES = 128
Q_RANK, KV_RANK = 256, 128
NOPE, ROPE_M, V_M = 64, 32, 64
QK_M = NOPE + ROPE_M
RQK = 64
CHUNK = 128
FFN = 2816
THETA = 10000.0
EPS = 1e-6
LR, B1, B2, AEPS, WD, STEP = 0.001, 0.9, 0.999, 1e-08, 0.01, 10
VMEM_LIMIT = 56 * 1024 * 1024

NN = ((1,), (0,))
NT = ((1,), (1,))
TN = ((0,), (0,))

P_GATES, P_VR, P_GR, P_QR, P_KR, P_CQ, P_CKV, P_KROPE, P_WIDTH = 0, 2048, 3072, 4096, 4608, 5120, 5376, 5504, 5632
O_CQ, O_CKV, O_KROPE, O_QR, O_KR, O_VR, O_GR, O_GATES = 0, 256, 384, 416, 928, 1440, 2464, 3488


def _dot(a, b, dims):
    return lax.dot_general(a, b, (dims, ((), ())), preferred_element_type=F32)


def _pick(dim, cands):
    for c in cands:
        if dim % c == 0:
            return c
    return dim


def _pairs(t):
    return t.reshape(t.shape[0], 4, 2, 2, 32).transpose(0, 1, 3, 2, 4).reshape(t.shape[0], 512)


def _win_pad(w):
    z = jnp.zeros((w.shape[0], 48), w.dtype)
    kr = w[:, O_KROPE:O_KROPE + 32]
    return jnp.concatenate([w[:, O_GATES:], w[:, O_VR:O_VR + 1024], w[:, O_GR:O_GR + 1024], _pairs(w[:, O_QR:O_QR + 512]),
                            _pairs(w[:, O_KR:O_KR + 512]), w[:, :O_CKV], w[:, O_CKV:O_KROPE], kr[:, :16], z, kr[:, 16:], z], axis=1)


def _win_unpad(g):
    return jnp.concatenate([g[:, P_CQ:P_CQ + 256], g[:, P_CKV:P_CKV + 128], g[:, P_KROPE:P_KROPE + 16], g[:, P_KROPE + 64:P_KROPE + 80],
                            _pairs(g[:, P_QR:P_QR + 512]), _pairs(g[:, P_KR:P_KR + 512]), g[:, P_VR:P_VR + 1024],
                            g[:, P_GR:P_GR + 1024], g[:, P_GATES:P_GATES + 2048]], axis=1)


def _qk_pad(t):
    z = jnp.zeros(t.shape[:-1] + (32,), t.dtype)
    return jnp.concatenate([t[..., 64:80], t[..., 0:48], t[..., 80:96], t[..., 48:64], z], axis=-1)


def _qk_unpad(p):
    return jnp.concatenate([p[..., 16:64], p[..., 80:96], p[..., 0:16], p[..., 64:80]], axis=-1)


def _wq_pad(w):
    return _qk_pad(w.reshape(Q_RANK, HEADS, QK_M)).reshape(Q_RANK, HEADS * LANES)


def _wq_unpad(g):
    return _qk_unpad(g.reshape(Q_RANK, HEADS, LANES)).reshape(Q_RANK, HEADS * QK_M)


def _wkv_pad(w):
    t = w.reshape(KV_RANK, HEADS, NOPE + V_M)
    z = lambda n: jnp.zeros((KV_RANK, HEADS, n), w.dtype)
    wk = jnp.concatenate([z(16), t[..., 0:48], z(16), t[..., 48:64], z(32)], axis=-1)
    wv = jnp.concatenate([t[..., 64:128], z(64)], axis=-1)
    return wk.reshape(KV_RANK, HEADS * LANES), wv.reshape(KV_RANK, HEADS * LANES)


def _wkv_unpad(dwk, dwv):
    k, v = dwk.reshape(KV_RANK, HEADS, LANES), dwv.reshape(KV_RANK, HEADS, LANES)
    return jnp.concatenate([k[..., 16:64], k[..., 80:96], v[..., 0:64]], axis=-1).reshape(KV_RANK, HEADS * (NOPE + V_M))


def _wmla_pad(w):
    t = w.reshape(HEADS, V_M, D_MODEL)
    return jnp.concatenate([t, jnp.zeros_like(t)], axis=1).reshape(HEADS * LANES, D_MODEL)


def _wmla_unpad(g):
    return g.reshape(HEADS, LANES, D_MODEL)[:, :V_M].reshape(HEADS * V_M, D_MODEL)


def _rowwise(name, fn, rows, ts, ins, outs, accs=(), ncol=1):
    n_in, n_out, n_acc = len(ins), len(outs), len(accs)

    def colmap(col):
        if callable(col):
            return lambda i, j: (i, col(j))
        return lambda i, j: (i, col)

    arrays, in_specs = [], []
    for arr, spec in ins:
        arrays.append(arr)
        if spec is None:
            in_specs.append(pl.BlockSpec(arr.shape, functools.partial(lambda i, j, nd: (0,) * nd, nd=arr.ndim)))
        else:
            in_specs.append(pl.BlockSpec((ts, spec[0]), colmap(spec[1])))
    out_shape, out_specs = [], []
    for total, dtype, width, col in outs:
        out_shape.append(jax.ShapeDtypeStruct((rows, total), dtype))
        out_specs.append(pl.BlockSpec((ts, width), colmap(col)))
    for shp in accs:
        out_shape.append(jax.ShapeDtypeStruct(shp, F32))
        out_specs.append(pl.BlockSpec(shp, functools.partial(lambda i, j, nd: (0,) * nd, nd=len(shp))))

    def body(*refs):
        vals = [r[...] for r in refs[:n_in]]
        res = fn(*vals)
        if not isinstance(res, (tuple, list)):
            res = (res,)
        for r, v in zip(refs[n_in:n_in + n_out], res[:n_out]):
            r[...] = v.astype(r.dtype)
        if n_acc:
            first = jnp.logical_and(pl.program_id(0) == 0, pl.program_id(1) == 0)
            for r, v in zip(refs[n_in + n_out:], res[n_out:]):
                @pl.when(first)
                def _(r=r):
                    r[...] = jnp.zeros_like(r)
                r[...] += v.astype(F32)

    res = pl.pallas_call(
        body, name=name, grid=(rows // ts, ncol), in_specs=in_specs, out_specs=out_specs, out_shape=out_shape,
        compiler_params=pltpu.CompilerParams(dimension_semantics=("arbitrary", "arbitrary"), vmem_limit_bytes=VMEM_LIMIT),
    )(*arrays)
    return res


MM_OPERAND_BYTES = 24 * 1024 * 1024


def _mm(name, a, b, mode):
    b_halves = b.ndim == 3
    assert not b_halves or mode == "tn"
    if mode == "nn":
        (M, K), N = a.shape, b.shape[1]
    elif mode == "nt":
        (M, K), N = a.shape, b.shape[0]
    else:
        (K, M), N = a.shape, b.shape[-1] * (2 if b_halves else 1)
    tm = _pick(M, (1024, 512, 1408, 256, 128))
    tn = _pick(N // 2 if b_halves else N, (1408, 1024, 512, 256, 128))
    fits = lambda t: 2 * (tm + tn) * t * a.dtype.itemsize <= MM_OPERAND_BYTES
    tk = next(t for t in (K, 4096, 2816, 2048, 1408, 1024, 512, 256, 128) if K % t == 0 and (fits(t) or t == 128))
    nk = K // tk
    dims = {"nn": NN, "nt": NT, "tn": TN}[mode]
    a_spec = pl.BlockSpec((tk, tm), lambda i, j, k: (k, i)) if mode == "tn" else pl.BlockSpec((tm, tk), lambda i, j, k: (i, k))
    if b_halves:
        perj = (N // 2) // tn
        b_spec = pl.BlockSpec((None, tk, tn), lambda i, j, k: (j // perj, k, j % perj))
    else:
        b_spec = pl.BlockSpec((tn, tk), lambda i, j, k: (j, k)) if mode == "nt" else pl.BlockSpec((tk, tn), lambda i, j, k: (k, j))
    o_spec = pl.BlockSpec((tm, tn), lambda i, j, k: (i, j))

    def body(a_ref, b_ref, o_ref):
        d = _dot(a_ref[...], b_ref[...], dims)
        if nk == 1:
            o_ref[...] = d
        else:
            k = pl.program_id(2)

            @pl.when(k == 0)
            def _():
                o_ref[...] = d

            @pl.when(k > 0)
            def _():
                o_ref[...] += d

    return pl.pallas_call(
        body, name=name, grid=(M // tm, N // tn, nk), in_specs=[a_spec, b_spec], out_specs=o_spec,
        out_shape=jax.ShapeDtypeStruct((M, N), F32),
        compiler_params=pltpu.CompilerParams(dimension_semantics=("parallel", "parallel", "arbitrary"), vmem_limit_bytes=VMEM_LIMIT),
    )(a, b)


def _mm_rows(name, a, b, fn, row_ins, whole_ins, outs, accs=(), mode="nn", lhs_fn=None):
    as_windows = lambda ts: [t if isinstance(t, tuple) else (t, (t.shape[1], 0)) for t in ts]
    halves = lhs_fn is None and a.ndim == 3
    assert not halves or mode == "nt"
    lhs_windows = as_windows(a) if lhs_fn is not None else []
    n_lhs = len(lhs_windows) if lhs_fn is not None else 1
    M = lhs_windows[0][0].shape[0] if lhs_fn is not None else a.shape[-2]
    kh = b.shape[0] if lhs_fn is not None else a.shape[-1]
    N = b.shape[1 if mode == "nn" else 0]
    tm = _pick(M, (512, 256, 128))
    windows = as_windows(row_ins)
    n_in, n_out = n_lhs + 1 + len(windows) + len(whole_ins), len(outs) + (1 if lhs_fn is not None else 0)
    row_spec = lambda w, col: pl.BlockSpec((tm, w), functools.partial(lambda i, col: (i, col), col=col))
    if lhs_fn is not None:
        lhs_specs, lhs_args = [row_spec(w, col) for _, (w, col) in lhs_windows], [t for t, _ in lhs_windows]
    else:
        lhs_specs = [pl.BlockSpec((2, tm, kh), lambda i: (0, i, 0)) if halves else pl.BlockSpec((tm, kh), lambda i: (i, 0))]
        lhs_args = [a]

    def body(*refs):
        rhs = refs[n_lhs]
        if halves:
            d = _dot(refs[0][0], rhs[:, :kh], NT) + _dot(refs[0][1], rhs[:, kh:], NT)
        elif lhs_fn is not None:
            lhs = lhs_fn(*[r[...] for r in refs[:n_lhs]], *[r[...] for r in refs[n_in - len(whole_ins):n_in]]).astype(MXU)
            d = _dot(lhs, rhs[...], NN)
        else:
            d = _dot(refs[0][...], rhs[...], NN if mode == "nn" else NT)
        res = tuple(fn(d, *[r[...] for r in refs[n_lhs + 1:n_in]]))
        if lhs_fn is not None:
            res = (lhs,) + res
        for r, v in zip(refs[n_in:n_in + n_out], res[:n_out]):
            r[...] = v.astype(r.dtype)
        for r, v in zip(refs[n_in + n_out:], res[n_out:]):
            @pl.when(pl.program_id(0) == 0)
            def _(r=r):
                r[...] = jnp.zeros_like(r)
            r[...] += v

    whole = lambda t: pl.BlockSpec(t.shape, functools.partial(lambda i, nd: (0,) * nd, nd=t.ndim))
    out_rows = ([(kh, MXU)] if lhs_fn is not None else []) + [(N, dt) for dt in outs]
    return pl.pallas_call(
        body, name=name, grid=(M // tm,),
        in_specs=lhs_specs + [whole(b)] + [row_spec(w, col) for _, (w, col) in windows] + [whole(t) for t in whole_ins],
        out_specs=[row_spec(w, 0) for w, _ in out_rows]
        + [pl.BlockSpec(s, functools.partial(lambda i, nd: (0,) * nd, nd=len(s))) for s in accs],
        out_shape=[jax.ShapeDtypeStruct((M, w), dt) for w, dt in out_rows] + [jax.ShapeDtypeStruct(s, F32) for s in accs],
        compiler_params=pltpu.CompilerParams(dimension_semantics=("arbitrary",), vmem_limit_bytes=VMEM_LIMIT),
    )(*lhs_args, b, *[t for t, _ in windows], *whole_ins)


def _ffn_tiles(S):
    return _pick(S, (1024, 512, 256, 128)), _pick(FFN, (1408, 704, 256, 128))


def _gate_up_swiglu(h2, wgu):
    S, K = h2.shape
    tm, tn = _ffn_tiles(S)
    nj = FFN // tn

    def body(a_ref, bg_ref, bu_ref, gu_ref, act_ref):
        a = a_ref[...]
        g, u = _dot(a, bg_ref[...], NN), _dot(a, bu_ref[...], NN)
        gu_ref[0], gu_ref[1] = g.astype(gu_ref.dtype), u.astype(gu_ref.dtype)
        act_ref[...] = _swiglu_fn(g, u).astype(act_ref.dtype)

    return pl.pallas_call(
        body, name="gate_up_swiglu", grid=(S // tm, nj),
        in_specs=[pl.BlockSpec((tm, K), lambda i, j: (i, 0)), pl.BlockSpec((K, tn), lambda i, j: (0, j)),
                  pl.BlockSpec((K, tn), lambda i, j: (0, nj + j))],
        out_specs=[pl.BlockSpec((2, tm, tn), lambda i, j: (0, i, j)), pl.BlockSpec((tm, tn), lambda i, j: (i, j))],
        out_shape=[jax.ShapeDtypeStruct((2, S, FFN), MXU), jax.ShapeDtypeStruct((S, FFN), MXU)],
        compiler_params=pltpu.CompilerParams(dimension_semantics=("parallel", "parallel"), vmem_limit_bytes=VMEM_LIMIT),
    )(h2, wgu, wgu)


def _d_act_swiglu(dx2, wdown, gu):
    S, K = dx2.shape
    tm, tn = _ffn_tiles(S)

    def body(a_ref, b_ref, gu_ref, o_ref):
        dact = _dot(a_ref[...], b_ref[...], NT)
        _, vjp = jax.vjp(_swiglu_fn, gu_ref[0].astype(F32), gu_ref[1].astype(F32))
        dg, du = vjp(dact)
        o_ref[0], o_ref[1] = dg.astype(o_ref.dtype), du.astype(o_ref.dtype)

    stacked = pl.BlockSpec((2, tm, tn), lambda i, j: (0, i, j))
    return pl.pallas_call(
        body, name="d_act_swiglu", grid=(S // tm, FFN // tn),
        in_specs=[pl.BlockSpec((tm, K), lambda i, j: (i, 0)), pl.BlockSpec((tn, K), lambda i, j: (j, 0)), stacked],
        out_specs=stacked, out_shape=jax.ShapeDtypeStruct((2, S, FFN), MXU),
        compiler_params=pltpu.CompilerParams(dimension_semantics=("parallel", "parallel"), vmem_limit_bytes=VMEM_LIMIT),
    )(dx2, wdown, gu)


@jax.custom_vjp
def _swap64(x):
    return pltpu.roll(x, 64, 1)


_swap64.defvjp(lambda x: (_swap64(x), None), lambda _, g: (_swap64(g),))


@jax.custom_vjp
def _mxdot(a, b):
    return _dot(a.astype(MXU), b.astype(MXU), NN)


def _mxdot_bwd(res, g):
    a, b = res
    gb = g.astype(MXU)
    return _dot(gb, b.astype(MXU), NT), _dot(a.astype(MXU), gb, TN)


_mxdot.defvjp(lambda a, b: (_mxdot(a, b), (a, b)), _mxdot_bwd)


def _row_sum(t):
    if t.shape[-1] == LANES:
        hi = t.astype(jnp.bfloat16)
        lo = (t - hi.astype(F32)).astype(jnp.bfloat16)
        ones = jnp.ones((LANES, LANES), jnp.bfloat16)
        return _dot(hi, ones, NN) + _dot(lo, ones, NN)
    return jnp.sum(t, axis=-1, keepdims=True)


@functools.partial(jax.custom_vjp, nondiff_argnums=(1,))
def _unit_rms(x, n):
    return x * lax.rsqrt(_row_sum(x * x) * (1.0 / n) + EPS)


def _unit_rms_fwd(x, n):
    r = lax.rsqrt(_row_sum(x * x) * (1.0 / n) + EPS)
    y = x * r
    return y, (y, r)


def _unit_rms_bwd(n, res, g):
    y, r = res
    return (r * (g - y * (_row_sum(g * y) * (1.0 / n))),)


_unit_rms.defvjp(_unit_rms_fwd, _unit_rms_bwd)


def _rms(x):
    return _unit_rms(x, x.shape[-1])


def _rmsg_fn(x, g):
    return _rms(x) * g


def _silu(x):
    return x * jax.nn.sigmoid(x)


def _tables_fn(pos, inv_m, sgn_m, inv_r, sgn_r):
    am, ar = pos * inv_m, pos * inv_r
    return jnp.cos(am), jnp.sin(am) * sgn_m, jnp.cos(ar), jnp.sin(ar) * sgn_r


def _head_blocks(t):
    return [t[:, LANES * h:LANES * (h + 1)] for h in range(t.shape[1] // LANES)]


def _mla_prep_fn(cq, ckv, kr, cosm, sinm, gqa, gkva, gqn, gkn, wq, wk, wv):
    cqn = _rms(cq) * gqa
    ckvn = _rms(ckv) * gkva
    q_raw = _mxdot(cqn, wq)
    k_raw = _mxdot(ckvn, wk)
    lane = lax.broadcasted_iota(jnp.int32, (1, HEADS * LANES), 1)
    v = _mxdot(ckvn, wv) + (lane % LANES == V_M).astype(F32)

    def norm_rope(blocks, g, extra):
        outs = []
        for b in blocks:
            if extra is not None:
                b = b + extra
            n = _unit_rms(b, QK_M) * g
            outs.append(n * cosm + _swap64(n) * sinm)
        return jnp.concatenate(outs, axis=1)

    q = norm_rope(_head_blocks(q_raw), gqn, None)
    k = norm_rope(_head_blocks(k_raw), gkn, kr)
    return q, k, v


def _ret_prep_fn(qr, kr, cosr, sinr):
    def rope(t, scale):
        return jnp.concatenate([(b * cosr + _swap64(b) * sinr) * scale for b in _head_blocks(t)], axis=1)
    return rope(qr, 1.0), rope(kr, RQK ** -0.5)


def _ret_post_fn(rf, rb, gr):
    ret = rf + rb
    outs = []
    for b, g in zip(_head_blocks(ret), _head_blocks(gr)):
        outs.append(_silu(g) * _rms(b))
    return jnp.concatenate(outs, axis=1)


def _merge_fn(ga, gb, ya, yb):
    return jax.nn.sigmoid(ga) * ya + jax.nn.sigmoid(gb) * yb


def _swiglu_fn(gate, up):
    return _silu(gate) * up


def _loss_fn(x2, tgt):
    d = x2 - tgt
    return d * (1.0 / D_MODEL), 0.5 * jnp.sum(d * d, axis=0, keepdims=True) * (1.0 / D_MODEL)


def _adamw_fn(parts, w, m, v):
    g = parts[0].astype(F32)
    for p in range(1, N_DEV):
        g = g + parts[p].astype(F32)
    m2 = B1 * m + (1.0 - B1) * g
    v2 = B2 * v + (1.0 - B2) * jnp.square(g)
    m_hat = m2 / (1.0 - B1 ** STEP)
    v_hat = v2 / (1.0 - B2 ** STEP)
    delta = -LR * (m_hat / (jnp.sqrt(v_hat) + AEPS) + WD * w)
    return g, delta, m2, v2


SCALE = QK_M ** -0.5
LOG2E = 1.4426950408889634
FLASH_ROWS = 32


def _flash_fwd(q, k, v):
    S = q.shape[0]
    tk = _pick(S, (512, 256, 128))
    tq = _pick(S, (1024, 512, 256, 128))
    ncb = tk // LANES
    nkv = S // tk
    mrows = 64
    c = SCALE * LOG2E

    def body(q_ref, k_ref, v_ref, o_ref, lse_ref, s_a, p_a, s_b, p_b, m_sc, a_sc, acc_sc):
        m_sc[...] = jnp.full_like(m_sc, -jnp.inf)
        acc_sc[...] = jnp.zeros_like(acc_sc)
        qb = q_ref[...]

        def scores(j, s_buf):
            s_buf[...] = _dot(qb, k_ref[j * tk:(j + 1) * tk, :], NT)

        def stage(j, s_buf, p_buf, s_next):
            if j + 1 < nkv:
                scores(j + 1, s_next)
            for r in range(tq // mrows):
                rows = slice(r * mrows, (r + 1) * mrows)
                cols = [s_buf[rows, LANES * cb:LANES * (cb + 1)] for cb in range(ncb)]
                m_prev = m_sc[rows, :]
                row_max = jnp.max(functools.reduce(jnp.maximum, cols), axis=-1, keepdims=True)
                m_new = jnp.maximum(m_prev, jnp.broadcast_to(row_max, (mrows, LANES)))
                a_sc[rows, :] = jnp.exp2((m_prev - m_new) * c)
                m_sc[rows, :] = m_new
                for cb in range(ncb):
                    p_buf[rows, LANES * cb:LANES * (cb + 1)] = jnp.exp2((cols[cb] - m_new) * c).astype(p_buf.dtype)
            acc_sc[...] = a_sc[...] * acc_sc[...] + _dot(p_buf[...], v_ref[j * tk:(j + 1) * tk, :], NN)

        scores(0, s_a)
        for j in range(nkv):
            stage(j, *((s_a, p_a, s_b) if j % 2 == 0 else (s_b, p_b, s_a)))
        acc = acc_sc[...]
        lane = lax.broadcasted_iota(jnp.int32, (1, LANES), 1)
        l = jnp.sum(jnp.where(lane == V_M, acc, 0.0), axis=-1, keepdims=True)
        o_ref[...] = (acc / l).astype(o_ref.dtype)
        lse_ref[...] = m_sc[...] * c + jnp.log2(jnp.broadcast_to(l, (tq, LANES)))

    qspec = pl.BlockSpec((tq, LANES), lambda h, i: (i, h))
    kspec = pl.BlockSpec((S, LANES), lambda h, i: (0, h))
    return pl.pallas_call(
        body, name="flash_fwd", grid=(HEADS, S // tq), in_specs=[qspec, kspec, kspec], out_specs=[qspec, qspec],
        out_shape=[jax.ShapeDtypeStruct((S, HEADS * LANES), MXU), jax.ShapeDtypeStruct((S, HEADS * LANES), F32)],
        scratch_shapes=[pltpu.VMEM((tq, tk), F32), pltpu.VMEM((tq, tk), MXU)] * 2 + [pltpu.VMEM((tq, LANES), F32)] * 3,
        compiler_params=pltpu.CompilerParams(dimension_semantics=("parallel", "arbitrary"), vmem_limit_bytes=VMEM_LIMIT),
    )(q, k, v)


def _delta_fn(o, do):
    outs = [jnp.broadcast_to(jnp.sum(a * b, axis=-1, keepdims=True), a.shape) for a, b in zip(_head_blocks(o), _head_blocks(do))]
    return do, jnp.concatenate(outs, axis=1)


def _flash_bwd(q, k, v, do, lse, delta):
    S = q.shape[0]
    tq = tk = _pick(S, (512, 256, 128))
    ncb = tk // LANES
    c = SCALE * LOG2E
    nq = S // tq
    nsub = 2 if (S // tk) % 2 == 0 else 1
    stages = [(sub, i) for sub in range(nsub) for i in range(nq)]

    def body(q_ref, k_ref, v_ref, do_ref, lse_ref, dl_ref, dq_ref, dk_ref, dv_ref, s_a, dp_a, p_a, ds_a, s_b, dp_b, p_b, ds_b):
        @pl.when(pl.program_id(1) == 0)
        def _():
            dq_ref[...] = jnp.zeros_like(dq_ref)

        dk_ref[...] = jnp.zeros_like(dk_ref)
        dv_ref[...] = jnp.zeros_like(dv_ref)
        bufs = [(s_a, dp_a, p_a, ds_a), (s_b, dp_b, p_b, ds_b)]

        def scores(sub, i, s_buf, dp_buf):
            kv_rows, q_rows = slice(sub * tk, (sub + 1) * tk), slice(i * tq, (i + 1) * tq)
            s_buf[...] = _dot(q_ref[q_rows, :], k_ref[kv_rows, :], NT)
            dp_buf[...] = _dot(do_ref[q_rows, :], v_ref[kv_rows, :], NT)

        scores(*stages[0], *bufs[0][:2])
        for t, (sub, i) in enumerate(stages):
            s_buf, dp_buf, p_buf, ds_buf = bufs[t % 2]
            if t + 1 < len(stages):
                scores(*stages[t + 1], *bufs[(t + 1) % 2][:2])
            for r in range(tq // FLASH_ROWS):
                rows = slice(r * FLASH_ROWS, (r + 1) * FLASH_ROWS)
                grows = slice(i * tq + r * FLASH_ROWS, i * tq + (r + 1) * FLASH_ROWS)
                lse_b, dl_b = lse_ref[grows, :], dl_ref[grows, :]
                for cb in range(ncb):
                    sl = slice(LANES * cb, LANES * (cb + 1))
                    p = jnp.exp2(s_buf[rows, sl] * c - lse_b)
                    p_buf[rows, sl] = p.astype(p_buf.dtype)
                    ds_buf[rows, sl] = (p * (dp_buf[rows, sl] - dl_b) * SCALE).astype(ds_buf.dtype)
            kv_rows, q_rows = slice(sub * tk, (sub + 1) * tk), slice(i * tq, (i + 1) * tq)
            dv_ref[kv_rows, :] += _dot(p_buf[...], do_ref[q_rows, :], TN)
            dk_ref[kv_rows, :] += _dot(ds_buf[...], q_ref[q_rows, :], TN)
            dq_ref[q_rows, :] += _dot(ds_buf[...], k_ref[kv_rows, :], NN)

    hspec = pl.BlockSpec((S, LANES), lambda h, j: (0, h))
    kspec = pl.BlockSpec((nsub * tk, LANES), lambda h, j: (j, h))
    full = jax.ShapeDtypeStruct((S, HEADS * LANES), F32)
    tile_bufs = [pltpu.VMEM((tq, tk), F32), pltpu.VMEM((tq, tk), F32), pltpu.VMEM((tq, tk), MXU), pltpu.VMEM((tq, tk), MXU)]
    return pl.pallas_call(
        body, name="flash_bwd", grid=(HEADS, S // (nsub * tk)), in_specs=[hspec, kspec, kspec, hspec, hspec, hspec],
        out_specs=[hspec, kspec, kspec], out_shape=[full, full, full],
        scratch_shapes=tile_bufs + tile_bufs,
        compiler_params=pltpu.CompilerParams(dimension_semantics=("parallel", "arbitrary"), vmem_limit_bytes=VMEM_LIMIT),
    )(q, k, v, do, lse, delta)


def _ret_consts(lgh, head, rev):
    C = CHUNK
    lane = lax.broadcasted_iota(jnp.int32, (1, LANES), 1)
    hm = ((lane // 32) % 2 == head % 2).astype(F32)
    r = lax.broadcasted_iota(jnp.int32, (C, C), 0)
    c = lax.broadcasted_iota(jnp.int32, (C, C), 1)
    diff = ((c - r) if rev else (r - c)).astype(F32)
    mask = (diff > 0) if rev else (diff >= 0)
    dpos = jnp.maximum(diff, 0.0)
    din = jnp.where(mask, jnp.exp(lgh * dpos), 0.0)
    idx = lax.broadcasted_iota(jnp.int32, (C, 1), 0).astype(F32)
    eq = (C - idx) if rev else (idx + 1.0)
    ek = idx if rev else (C - 1.0 - idx)
    qd, kd = jnp.exp(lgh * eq), jnp.exp(lgh * ek)
    cd = jnp.exp(lgh * jnp.full((1, 1), float(C), F32))
    return hm, din, dpos, qd, kd, cd, eq, ek


RET_HEADS_PER_STEP = 8


def _ret_fwd(name, qt, kt, proj, lg, rev):
    S = qt.shape[0]
    C = CHUNK
    TB = _pick(S, (1024, 512, 256, 128))
    cb, nb = TB // C, S // TB
    hps = RET_HEADS_PER_STEP
    blk = (lambda g: nb - 1 - g) if rev else (lambda g: g)

    def body(lg_ref, q_ref, k_ref, v_ref, o_ref, st_ref, state_sc):
        hg, g = pl.program_id(0), pl.program_id(1)

        @pl.when(g == 0)
        def _():
            state_sc[...] = jnp.zeros_like(state_sc)

        consts = [_ret_consts(lg_ref[hg * hps + u], u, rev) for u in range(hps)]
        order = list(reversed(range(cb))) if rev else list(range(cb))
        units = [(cc, u) for cc in order for u in range(hps)]

        def operands(cc, u):
            rows = pl.ds(cc * C, C)
            pair = slice(LANES * (u // 2), LANES * (u // 2 + 1))
            hm = consts[u][0]
            return q_ref[rows, pair] * hm, k_ref[rows, pair] * hm, v_ref[rows, LANES * u:LANES * (u + 1)].astype(MXU)

        a, inc = {}, {}
        for cc, u in units:
            q, k, v = operands(cc, u)
            a[cc, u] = _dot(q.astype(MXU), k.astype(MXU), NT) * consts[u][1]
            inc[cc, u] = _dot((k * consts[u][4]).astype(MXU), v, TN)
        for u in range(hps):
            st = state_sc[u]
            for cc in order:
                st_ref[u, cc] = st
                st = st * consts[u][5] + inc[cc, u]
            state_sc[u] = st
        for cc, u in units:
            q, _, v = operands(cc, u)
            cross = _dot((q * consts[u][3]).astype(MXU), st_ref[u, cc].astype(MXU), NN)
            o_ref[pl.ds(cc * C, C), LANES * u:LANES * (u + 1)] = _dot(a[cc, u].astype(MXU), v, NN) + cross

    qk_spec = pl.BlockSpec((TB, LANES * hps // 2), lambda h, g: (blk(g), h))
    return pl.pallas_call(
        body, name=name, grid=(HEADS // hps, nb),
        in_specs=[pl.BlockSpec(memory_space=pltpu.SMEM), qk_spec, qk_spec,
                  pl.BlockSpec((TB, LANES * hps), lambda h, g: (blk(g), P_VR // (LANES * hps) + h))],
        out_specs=[pl.BlockSpec((TB, LANES * hps), lambda h, g: (blk(g), h)),
                   pl.BlockSpec((hps, cb, LANES, LANES), lambda h, g: (h, blk(g), 0, 0))],
        out_shape=[jax.ShapeDtypeStruct((S, HEADS * LANES), F32), jax.ShapeDtypeStruct((HEADS, S // C, LANES, LANES), F32)],
        scratch_shapes=[pltpu.VMEM((hps, LANES, LANES), F32)],
        compiler_params=pltpu.CompilerParams(dimension_semantics=("parallel", "arbitrary"), vmem_limit_bytes=VMEM_LIMIT),
    )(lg, qt, kt, proj)


def _ret_bwd(name, qt, kt, proj, dret, states, lg, rev):
    S = qt.shape[0]
    C = CHUNK
    TB = _pick(S, (512, 256, 128))
    cb, nb = TB // C, S // TB
    hps = RET_HEADS_PER_STEP
    blk = (lambda g: g) if rev else (lambda g: nb - 1 - g)

    def body(lg_ref, q_ref, k_ref, v_ref, do_ref, st_ref, dq_ref, dk_ref, dv_ref, dlg_ref, ds_sc, acc_cc, acc_q, acc_k, acc_s):
        hg, g = pl.program_id(0), pl.program_id(1)

        @pl.when(g == 0)
        def _():
            ds_sc[...] = jnp.zeros_like(ds_sc)
            acc_cc[...] = jnp.zeros_like(acc_cc)
            acc_q[...] = jnp.zeros_like(acc_q)
            acc_k[...] = jnp.zeros_like(acc_k)
            acc_s[...] = jnp.zeros_like(acc_s)

        lgs = [lg_ref[hg * hps + u] for u in range(hps)]
        consts = [_ret_consts(lgs[u], u, rev) for u in range(hps)]
        order = list(range(cb)) if rev else list(reversed(range(cb)))
        units = [(cc, u) for cc in order for u in range(hps)]

        def operands(cc, u):
            rows = pl.ds(cc * C, C)
            pair = slice(LANES * (u // 2), LANES * (u // 2 + 1))
            head = slice(LANES * u, LANES * (u + 1))
            hm = consts[u][0]
            return q_ref[rows, pair] * hm, k_ref[rows, pair] * hm, v_ref[rows, head].astype(MXU), do_ref[rows, head].astype(MXU)

        a, dp, dqs, inc = {}, {}, {}, {}
        for cc, u in units:
            q, k, vb, dob = operands(cc, u)
            a[cc, u] = _dot(q.astype(MXU), k.astype(MXU), NT)
            dp[cc, u] = _dot(dob, vb, NT)
            dqs[cc, u] = _dot(dob, st_ref[u, cc].astype(MXU), NT)
            inc[cc, u] = _dot((q * consts[u][3]).astype(MXU), dob, TN)
        dsn = {}
        for u in range(hps):
            ds = ds_sc[u]
            for cc in order:
                dsn[cc, u] = ds
                ds = ds * consts[u][5] + inc[cc, u]
            ds_sc[u] = ds
        even = {}
        for cc, u in units:
            hm, din, dpos, qd, kd, cd, eq, ek = consts[u]
            rows, head = pl.ds(cc * C, C), slice(LANES * u, LANES * (u + 1))
            q, k, vb, dob = operands(cc, u)
            qb, kb = q.astype(MXU), k.astype(MXU)
            dsnb = dsn[cc, u].astype(MXU)
            da = (dp[cc, u] * din).astype(MXU)
            vds = _dot(vb, dsnb, NT)
            dq_u = (_dot(da, kb, NN) + dqs[cc, u] * qd) * hm
            dk_u = (_dot(da, qb, TN) + vds * kd) * hm
            if u % 2 == 0:
                even[cc] = (dq_u, dk_u)
            else:
                pair = slice(LANES * (u // 2), LANES * (u // 2 + 1))
                dq_ref[rows, pair] = even[cc][0] + dq_u
                dk_ref[rows, pair] = even[cc][1] + dk_u
            dv_ref[rows, head] = _dot((a[cc, u] * din).astype(MXU), dob, TN) + _dot((k * kd).astype(MXU), dsnb, NN)
            acc_cc[u] += dp[cc, u] * a[cc, u] * din * dpos
            acc_q[u] += dqs[cc, u] * q * (qd * eq)
            acc_k[u] += vds * k * (kd * ek)
            acc_s[u] += dsn[cc, u] * st_ref[u, cc] * (cd * float(C))

        @pl.when(g == nb - 1)
        def _():
            for u in range(hps):
                tot = (jnp.sum(acc_cc[u], keepdims=True) + jnp.sum(acc_q[u], keepdims=True)
                       + jnp.sum(acc_k[u], keepdims=True) + jnp.sum(acc_s[u], keepdims=True))
                dlg_ref[u] = jnp.broadcast_to(tot * lgs[u], (8, LANES))

    full = jax.ShapeDtypeStruct((S, HEADS * LANES), F32)
    hspec = pl.BlockSpec((TB, LANES * hps), lambda h, g: (blk(g), h))
    qk_spec = pl.BlockSpec((TB, LANES * hps // 2), lambda h, g: (blk(g), h))
    return pl.pallas_call(
        body, name=name, grid=(HEADS // hps, nb),
        in_specs=[pl.BlockSpec(memory_space=pltpu.SMEM), qk_spec, qk_spec,
                  pl.BlockSpec((TB, LANES * hps), lambda h, g: (blk(g), P_VR // (LANES * hps) + h)),
                  hspec,
                  pl.BlockSpec((hps, cb, LANES, LANES), lambda h, g: (h, blk(g), 0, 0))],
        out_specs=[qk_spec, qk_spec, hspec, pl.BlockSpec((hps, 8, LANES), lambda h, g: (h, 0, 0))],
        out_shape=[jax.ShapeDtypeStruct(qt.shape, F32), jax.ShapeDtypeStruct(kt.shape, F32), full,
                   jax.ShapeDtypeStruct((HEADS, 8, LANES), F32)],
        scratch_shapes=[pltpu.VMEM((hps, LANES, LANES), F32), pltpu.VMEM((hps, C, C), F32), pltpu.VMEM((hps, C, LANES), F32),
                        pltpu.VMEM((hps, C, LANES), F32), pltpu.VMEM((hps, LANES, LANES), F32)],
        compiler_params=pltpu.CompilerParams(dimension_semantics=("parallel", "arbitrary"), vmem_limit_bytes=VMEM_LIMIT),
    )(lg, qt, kt, proj, dret, states)


def _rope_consts():
    inv16 = THETA ** (-jnp.arange(16, dtype=F32) / 16)
    inv32 = THETA ** (-jnp.arange(32, dtype=F32) / 32)
    lane = np.arange(LANES)
    z48 = jnp.zeros((48,), F32)
    inv_m = jnp.concatenate([inv16, z48, inv16, z48])[None, :]
    sgn_m = jnp.asarray(np.where(lane < 16, -1.0, np.where((lane >= 64) & (lane < 80), 1.0, 0.0)), F32)[None, :]
    inv_r = jnp.concatenate([inv32] * 4)[None, :]
    sgn_r = jnp.asarray(np.where(lane < 64, -1.0, 1.0), F32)[None, :]
    return inv_m, sgn_m, inv_r, sgn_r


FIRST_WEIGHTS = ("w_in", "w_q_b", "w_kv_b")
EARLY_GRADS = ("w_down", "w_gate_up", "w_out", "w_ret_out")
MID_GRADS = ("w_mla_out", "w_in")


def _local_step(x, pos, tgt, gains, W, late_weights=None, grad_hook=None, start_after=None):
    S = x.shape[0]
    ts = _pick(S, (256, 128))
    ts_light = _pick(S, (512, 256, 128))
    R = lambda a, w=None, c=0: (a, ((a.shape[1] if w is None else w), c))
    W_ = lambda a: (a, None)

    win = _win_pad(W["w_in"])
    wq = _wq_pad(W["w_q_b"])
    wk, wv = _wkv_pad(W["w_kv_b"])
    gqn, gkn = _qk_pad(gains["g_qn"]), _qk_pad(gains["g_kn"])
    g_mix, g_q_a, g_kv_a, g_ffn = gains["g_mix"], gains["g_q_a"], gains["g_kv_a"], gains["g_ffn"]
    lg_f = -jnp.exp(gains["ret_decay_fwd"][0])
    lg_b = -jnp.exp(gains["ret_decay_bwd"][0])

    consts = list(_rope_consts())
    cosm, sinm, cosr, sinr = _rowwise("rope_tables", _tables_fn, S, ts_light,[R(pos)] + [W_(c) for c in consts],
                                      [(LANES, F32, LANES, 0)] * 4)

    h, proj = _mm_rows("rms_mix_in_proj", [x], win, lambda d, *_: (d,), [], [g_mix] + ([] if start_after is None else [start_after]),
                       [F32], lhs_fn=lambda xx, g, *_: _rmsg_fn(xx, g))
    seg = lambda off, w: (proj, (w, off // w))
    mla_ins = [seg(P_CQ, 256), seg(P_CKV, 128), seg(P_KROPE, 128), R(cosm), R(sinm),
               W_(g_q_a), W_(g_kv_a), W_(gqn), W_(gkn), W_(wq), W_(wk), W_(wv)]
    q, k, v = _rowwise("mla_prep", _mla_prep_fn, S, ts, mla_ins, [(HEADS * LANES, MXU, HEADS * LANES, 0)] * 3)
    o_bf, lse = _flash_fwd(q, k, v)
    if late_weights is not None:
        W = {**W, **late_weights(lse)}
    wmla = _wmla_pad(W["w_mla_out"])
    wret, wout, wgu, wdown = W["w_ret_out"], W["w_out"], W["w_gate_up"], W["w_down"]
    y_a = _mm("mla_out", o_bf, wmla, "nn")

    ret_ins = [seg(P_QR, 512), seg(P_KR, 512), R(cosr), R(sinr)]
    qt, kt = _rowwise("ret_prep", _ret_prep_fn, S, ts_light,ret_ins, [(512, F32, 512, 0)] * 2)
    ret_f, st_f = _ret_fwd("ret_fwd_f", qt, kt, proj, lg_f, False)
    ret_b, st_b = _ret_fwd("ret_fwd_b", qt, kt, proj, lg_b, True)
    post_ins = [R(ret_f), R(ret_b), seg(P_GR, 1024)]
    o_b, y_b, merged = _mm_rows("ret_post_out_merge", post_ins, wret, lambda yb, ga, gb, ya: (yb, _merge_fn(ga, gb, ya, yb)),
                                [seg(P_GATES, 1024), (proj, (1024, 1)), R(y_a)], [], [F32, MXU], lhs_fn=_ret_post_fn)
    merge_ins = [seg(P_GATES, 1024), (proj, (1024, 1)), R(y_a), R(y_b)]
    def residual_rms(d, xx, g):
        r = d + xx
        return r, _rmsg_fn(r, g)

    x1, h2 = _mm_rows("out_proj_rms_ffn", merged, wout, residual_rms, [x], [g_ffn], [F32, MXU])
    gu, act = _gate_up_swiglu(h2, wgu)

    def residual_loss(d, xx, t):
        dx, rows = _loss_fn(d + xx, t)
        return dx, dx, rows

    dx2, dx2_bf, loss_rows = _mm_rows("down_proj_loss", act, wdown, residual_loss, [x1, tgt], [], [F32, MXU], accs=[(1, D_MODEL)])

    gW = {}
    gW["w_down"] = _mm("d_w_down", act, dx2_bf, "tn")
    dgu = _d_act_swiglu(dx2_bf, wdown, gu)
    gW["w_gate_up"] = _mm("d_w_gate_up", h2, dgu, "tn")
    def rms_bwd(xx, g, dh, dres):
        _, vjp = jax.vjp(_rmsg_fn, xx, g)
        dx, dg = vjp(dh)
        dx = dx + dres
        return dx, dx, dg

    dx1, dx1_bf, dg_ffn = _mm_rows("d_h2_rms_ffn_bwd", dgu, wgu, lambda dh, xx, dres, g: rms_bwd(xx, g, dh, dres),
                                   [x1, dx2], [g_ffn], [F32, MXU], accs=[(1, D_MODEL)], mode="nt")
    gW["w_out"] = _mm("d_w_out", merged, dx1_bf, "tn")
    def merge_bwd(dm, ga, gb, ya, yb):
        _, vjp = jax.vjp(_merge_fn, ga, gb, ya, yb)
        return vjp(dm)

    dga, dgb, dy_a, dy_b = _mm_rows("d_merged_merge_bwd", dx1_bf, wout, merge_bwd, merge_ins, [], [MXU] * 4, mode="nt")
    gW["w_ret_out"] = _mm("d_w_ret_out", o_b, dy_b, "tn")
    after_early = [] if grad_hook is None else [grad_hook({n: gW[n] for n in EARLY_GRADS})]

    def post_bwd(dob, rf, rb, gr, *_):
        _, vjp = jax.vjp(_ret_post_fn, rf, rb, gr)
        drf, _, dgr = vjp(dob)
        return drf, dgr

    dret, dg_r = _mm_rows("d_o_b_ret_post_bwd", dy_b, wret, post_bwd, post_ins, after_early, [MXU, MXU], mode="nt")
    dq_f, dk_f, dv_f, dlg_f = _ret_bwd("ret_bwd_f", qt, kt, proj, dret, st_f, lg_f, False)
    dq_b, dk_b, dv_b, dlg_b = _ret_bwd("ret_bwd_b", qt, kt, proj, dret, st_b, lg_b, True)

    def ret_prep_bwd(qr, kr, cosr_, sinr_, dqf, dqb, dkf, dkb, dvf, dvb):
        _, vjp = jax.vjp(lambda a, b: _ret_prep_fn(a, b, cosr_, sinr_), qr, kr)
        dqr, dkr = vjp((dqf + dqb, dkf + dkb))
        return dqr, dkr, dvf + dvb

    dq_r, dk_r, dv_r = _rowwise("ret_prep_bwd", ret_prep_bwd, S, ts_light,ret_ins + [R(t) for t in (dq_f, dq_b, dk_f, dk_b, dv_f, dv_b)],
                                [(512, MXU, 512, 0), (512, MXU, 512, 0), (1024, MXU, 1024, 0)])

    gW_mla_p = _mm("d_w_mla_out", o_bf, dy_a, "tn")
    do_bf, delta = _mm_rows("d_o_attn_delta", dy_a, wmla, lambda d, oo, *_: _delta_fn(oo.astype(F32), d), [o_bf], after_early, [MXU, F32], mode="nt")
    dq, dk, dv = _flash_bwd(q, k, v, do_bf, lse, delta)

    def mla_prep_bwd(cq, ckv, kr, cosm_, sinm_, gqa, gkva, gqn_, gkn_, wq_, wk_, wv_, dq_, dk_, dv_):
        f = lambda cq, ckv, kr, gqa, gkva, gqn_, gkn_, wq_, wk_, wv_: _mla_prep_fn(cq, ckv, kr, cosm_, sinm_, gqa, gkva, gqn_, gkn_, wq_, wk_, wv_)
        _, vjp = jax.vjp(f, cq, ckv, kr, gqa, gkva, gqn_, gkn_, wq_.astype(F32), wk_.astype(F32), wv_.astype(F32))
        return vjp((dq_, dk_, dv_))

    mb = _rowwise("mla_prep_bwd", mla_prep_bwd, S, ts, mla_ins + [R(dq), R(dk), R(dv)],
                  [(256, MXU, 256, 0), (128, MXU, 128, 0), (128, MXU, 128, 0)],
                  accs=[(1, 256), (1, 128), (1, LANES), (1, LANES), (256, HEADS * LANES), (128, HEADS * LANES), (128, HEADS * LANES)])
    dc_q, dc_kv, dk_rope, dg_q_a, dg_kv_a, dgqn_p, dgkn_p, dwq_p, dwk_p, dwv_p = mb

    dproj = jnp.concatenate([dga, dgb, dv_r, dg_r, dq_r, dk_r, dc_q, dc_kv, dk_rope], axis=1)
    gW["w_in"] = _win_unpad(_mm("d_w_in", h, dproj, "tn"))
    gW["w_mla_out"] = _wmla_unpad(gW_mla_p)
    after_mid = None if grad_hook is None else grad_hook({n: gW[n] for n in MID_GRADS})
    grad_x, dg_mix = _mm_rows("d_h_rms_mix_bwd", dproj, win, lambda dh, xx, dres, g, *_: rms_bwd(xx, g, dh, dres)[1:],
                              [x, dx1], [g_mix] + ([] if after_mid is None else [after_mid]), [F32], accs=[(1, D_MODEL)], mode="nt")
    gW["w_q_b"] = _wq_unpad(dwq_p)
    gW["w_kv_b"] = _wkv_unpad(dwk_p, dwv_p)
    gG = {"g_mix": dg_mix, "g_q_a": dg_q_a, "g_kv_a": dg_kv_a, "g_qn": _qk_unpad(dgqn_p),
          "g_kn": _qk_unpad(dgkn_p), "ret_decay_fwd": dlg_f[:, 0, 0][None, :], "ret_decay_bwd": dlg_b[:, 0, 0][None, :],
          "g_ffn": dg_ffn}
    return loss_rows, grad_x, gG, gW


MATS = [("w_in", (1024, 5536), 1), ("w_q_b", (256, 768), 1), ("w_kv_b", (128, 1024), 1), ("w_mla_out", (512, 1024), 1),
        ("w_ret_out", (1024, 1024), 0), ("w_out", (1024, 1024), 0), ("w_gate_up", (1024, 5632), 1), ("w_down", (2816, 1024), 0)]
GAINS = [("g_mix", 1024), ("g_q_a", 256), ("g_kv_a", 128), ("g_qn", 96), ("g_kn", 96), ("ret_decay_fwd", 8), ("ret_decay_bwd", 8),
         ("g_ffn", 1024)]
ORDER = ["g_mix", "w_in", "g_q_a", "w_q_b", "g_kv_a", "w_kv_b", "g_qn", "g_kn", "w_mla_out", "ret_decay_fwd", "ret_decay_bwd",
         "w_ret_out", "w_out", "g_ffn", "w_gate_up", "w_down"]
GAIN_LEN = sum(n for _, n in GAINS)
GAIN_PAD = -(-GAIN_LEN // LANES) * LANES


def _pack_gains(d):
    row = jnp.concatenate([d[n].reshape(1, ln).astype(F32) for n, ln in GAINS], axis=1)
    return jnp.pad(row, ((0, 0), (0, GAIN_PAD - GAIN_LEN)))


def _unpack_gains(row):
    out, off = {}, 0
    for n, ln in GAINS:
        out[n] = row[0, off:off + ln]
        off += ln
    return out


def _unshard(pieces, axis):
    if axis == 0:
        return pieces.reshape((N_DEV * pieces.shape[1], pieces.shape[2]))
    return jnp.concatenate([pieces[p] for p in range(N_DEV)], axis=1)


def _reshard(full, axis):
    if axis == 0:
        return full.reshape((N_DEV, full.shape[0] // N_DEV, full.shape[1]))
    c = full.shape[1] // N_DEV
    return jnp.stack([full[:, c * p:c * (p + 1)] for p in range(N_DEV)])


def _all_gather(shards):
    n = len(shards)

    def body(*refs):
        x_refs, out_refs = refs[:n], refs[n:2 * n]
        send_sems, recv_sems, local_sems = refs[2 * n:]
        x, y, c = lax.axis_index("x"), lax.axis_index("y"), lax.axis_index("c")
        me, sibling = (x, y, c), (x, y, 1 - c)
        chips = [(1 - x, y), (x, 1 - y), (1 - x, 1 - y)]

        def slot(a, px, py, pc):
            return out_refs[a].at[4 * px + 2 * py + pc]

        def copy(a, k, block, to, from_input=False):
            return pltpu.make_async_remote_copy(
                src_ref=x_refs[a] if from_input else slot(a, *block), dst_ref=slot(a, *block),
                send_sem=send_sems.at[a, k], recv_sem=recv_sems.at[a, k], device_id=to, device_id_type=pl.DeviceIdType.MESH)

        mine = [pltpu.make_async_copy(x_refs[a], slot(a, *me), local_sems.at[a]) for a in range(n)]
        first = [copy(a, 0, me, sibling, True) for a in range(n)]
        first += [copy(a, 1 + j, me, (*chip, c), True) for j, chip in enumerate(chips) for a in range(n)]
        for cp in mine + first:
            cp.start()
        passed = []
        for j, chip in enumerate(chips):
            for a in range(n):
                copy(a, 1 + j, (*chip, c), me).wait_recv()
                passed.append(copy(a, 4 + j, (*chip, c), sibling))
                passed[-1].start()
        for a in range(n):
            copy(a, 0, sibling, me).wait_recv()
        for j, chip in enumerate(chips):
            for a in range(n):
                copy(a, 4 + j, (*chip, 1 - c), me).wait_recv()
        for cp in first + passed:
            cp.wait_send()
        for cp in mine:
            cp.wait()

    any_spec = pl.BlockSpec(memory_space=pl.ANY)
    return pl.pallas_call(
        body, name="all_gather_weights", out_shape=[jax.ShapeDtypeStruct((N_DEV,) + s.shape, s.dtype) for s in shards],
        in_specs=[any_spec] * n, out_specs=[any_spec] * n,
        scratch_shapes=[pltpu.SemaphoreType.DMA((n, 7)), pltpu.SemaphoreType.DMA((n, 7)), pltpu.SemaphoreType.DMA((n,))],
    )(*shards)


def _all_to_all(name, pieces):
    srcs, n = pieces, len(pieces)

    def body(*refs):
        in_refs, out_refs = refs[:n], refs[n:2 * n]
        send_sems, recv_sems, local_sems = refs[2 * n:]
        my_id = 4 * lax.axis_index("x") + 2 * lax.axis_index("y") + lax.axis_index("c")
        mine = [pltpu.make_async_copy(in_refs[a].at[my_id], out_refs[a].at[my_id], local_sems.at[a]) for a in range(n)]
        copies = _split_copies(in_refs, out_refs, send_sems, recv_sems, False)
        for cp in mine + copies:
            cp.start()
        for cp in copies:
            cp.wait_recv()
        for cp in copies:
            cp.wait_send()
        for cp in mine:
            cp.wait()

    any_spec = pl.BlockSpec(memory_space=pl.ANY)
    return pl.pallas_call(
        body, name=name, out_shape=[jax.ShapeDtypeStruct(s.shape, s.dtype) for s in srcs],
        in_specs=[any_spec] * n, out_specs=[any_spec] * n,
        scratch_shapes=[pltpu.SemaphoreType.DMA((7 * n,)), pltpu.SemaphoreType.DMA((7 * n,)), pltpu.SemaphoreType.DMA((n,))],
    )(*srcs)


def _flip_peers(x, y, c):
    flips = [(fx, fy, fc) for fx in (0, 1) for fy in (0, 1) for fc in (0, 1)][1:]
    return [(x ^ fx, y ^ fy, c ^ fc) for fx, fy, fc in flips]


def _split_copies(in_refs, land_refs, send_sems, recv_sems, gather):
    x, y, c = lax.axis_index("x"), lax.axis_index("y"), lax.axis_index("c")
    my_id = 4 * x + 2 * y + c
    copies = []
    for kk, p in enumerate(_flip_peers(x, y, c)):
        for a in range(len(in_refs)):
            src = in_refs[a] if gather else in_refs[a].at[4 * p[0] + 2 * p[1] + p[2]]
            copies.append(pltpu.make_async_remote_copy(
                src_ref=src, dst_ref=land_refs[a].at[my_id], send_sem=send_sems.at[a * 7 + kk], recv_sem=recv_sems.at[a * 7 + kk],
                device_id=p, device_id_type=pl.DeviceIdType.MESH))
    return copies


def _exchange_start(name, srcs, gather, after=None):
    n = len(srcs)
    first_out = 2 * n + (0 if after is None else 1)

    def body(*refs):
        for cp in _split_copies(refs[:n], refs[n:2 * n], refs[first_out], refs[first_out + 1], gather):
            cp.start()
        refs[-1][...] = jnp.zeros_like(refs[-1])

    hbm, sem = pl.BlockSpec(memory_space=pltpu.HBM), pl.BlockSpec(memory_space=pltpu.SEMAPHORE)
    land_shapes = [((N_DEV,) + s.shape if gather else s.shape, s.dtype) for s in srcs]
    lands = [pltpu.with_memory_space_constraint(lax.empty(shp, dt), pltpu.HBM) for shp, dt in land_shapes]
    srcs = [pltpu.with_memory_space_constraint(s, pltpu.HBM) for s in srcs]
    res = pl.pallas_call(
        body, name=name,
        out_shape=[pltpu.SemaphoreType.DMA((7 * n,)), pltpu.SemaphoreType.DMA((7 * n,))] + [pltpu.HBM(s.shape, s.dtype) for s in srcs]
        + [pltpu.HBM(shp, dt) for shp, dt in land_shapes] + [jax.ShapeDtypeStruct((8, LANES), F32)],
        in_specs=[hbm] * (2 * n) + ([] if after is None else [pl.BlockSpec(memory_space=pl.ANY)]),
        out_specs=[sem, sem] + [hbm] * (2 * n) + [pl.BlockSpec(memory_space=pltpu.VMEM)],
        input_output_aliases={i: 2 + i for i in range(2 * n)},
        compiler_params=pltpu.CompilerParams(has_side_effects=pltpu.SideEffectType.DATAFLOW_SIDE_EFFECTING),
    )(*srcs, *lands, *([] if after is None else [after]))
    return res[0], res[1], res[2:2 + n], res[2 + n:2 + 2 * n], res[-1]


def _exchange_wait(name, handles, after, gather):
    send_sems, recv_sems, srcs, lands, _ = handles
    n = len(srcs)

    def body(*refs):
        for cp in _split_copies(refs[:n], refs[n:2 * n], refs[2 * n], refs[2 * n + 1], gather):
            cp.wait_send()
            cp.wait_recv()

    hbm, sem = pl.BlockSpec(memory_space=pltpu.HBM), pl.BlockSpec(memory_space=pltpu.SEMAPHORE)
    res = pl.pallas_call(
        body, name=name, out_shape=[pltpu.HBM(t.shape, t.dtype) for t in list(srcs) + list(lands)],
        in_specs=[hbm] * (2 * n) + [sem, sem, pl.BlockSpec(memory_space=pl.ANY)], out_specs=[hbm] * (2 * n),
        input_output_aliases={i: i for i in range(2 * n)},
        compiler_params=pltpu.CompilerParams(has_side_effects=pltpu.SideEffectType.DATAFLOW_SIDE_EFFECTING),
    )(*srcs, *lands, send_sems, recv_sems, after)
    my_id = 4 * lax.axis_index("x") + 2 * lax.axis_index("y") + lax.axis_index("c")
    own = [s if gather else lax.dynamic_index_in_dim(s, my_id, 0, keepdims=False) for s in res[:n]]
    return [lax.dynamic_update_index_in_dim(land, o, my_id, 0) for land, o in zip(res[n:], own)]


def _adamw(name, parts, w, m, v):
    rows, cols = w.shape
    tr = _pick(rows, (128, 64, 32, 16, 8))
    pspec = pl.BlockSpec((N_DEV, tr, cols), lambda i: (0, i, 0))
    rspec = pl.BlockSpec((tr, cols), lambda i: (i, 0))

    def body(p_ref, w_ref, m_ref, v_ref, g_ref, d_ref, m2_ref, v2_ref):
        g, d, m2, v2 = _adamw_fn([p_ref[s] for s in range(N_DEV)], w_ref[...], m_ref[...], v_ref[...])
        g_ref[...], d_ref[...], m2_ref[...], v2_ref[...] = g, d, m2, v2

    return pl.pallas_call(
        body, name=name, grid=(rows // tr,), in_specs=[pspec, rspec, rspec, rspec], out_specs=[rspec] * 4,
        out_shape=[jax.ShapeDtypeStruct((rows, cols), F32)] * 4,
        compiler_params=pltpu.CompilerParams(dimension_semantics=("parallel",), vmem_limit_bytes=VMEM_LIMIT),
    )(parts, w, m, v)


def kernel(x, positions, g_mix, w_in, g_q_a, w_q_b, g_kv_a, w_kv_b, g_qn, g_kn, w_mla_out, ret_decay_fwd, ret_decay_bwd, w_ret_out, w_out, g_ffn, w_gate_up, w_down, loss_target, m_g_mix, m_w_in, m_g_q_a, m_w_q_b, m_g_kv_a, m_w_kv_b, m_g_qn, m_g_kn, m_w_mla_out, m_ret_decay_fwd, m_ret_decay_bwd, m_w_ret_out, m_w_out, m_g_ffn, m_w_gate_up, m_w_down, v_g_mix, v_w_in, v_g_q_a, v_w_q_b, v_g_kv_a, v_w_kv_b, v_g_qn, v_g_kn, v_w_mla_out, v_ret_decay_fwd, v_ret_decay_bwd, v_w_ret_out, v_w_out, v_g_ffn, v_w_gate_up, v_w_down):
    w = dict(g_mix=g_mix, w_in=w_in, g_q_a=g_q_a, w_q_b=w_q_b, g_kv_a=g_kv_a, w_kv_b=w_kv_b, g_qn=g_qn, g_kn=g_kn, w_mla_out=w_mla_out,
             ret_decay_fwd=ret_decay_fwd, ret_decay_bwd=ret_decay_bwd, w_ret_out=w_ret_out, w_out=w_out, g_ffn=g_ffn,
             w_gate_up=w_gate_up, w_down=w_down)
    m = dict(g_mix=m_g_mix, w_in=m_w_in, g_q_a=m_g_q_a, w_q_b=m_w_q_b, g_kv_a=m_g_kv_a, w_kv_b=m_w_kv_b, g_qn=m_g_qn, g_kn=m_g_kn,
             w_mla_out=m_w_mla_out, ret_decay_fwd=m_ret_decay_fwd, ret_decay_bwd=m_ret_decay_bwd, w_ret_out=m_w_ret_out, w_out=m_w_out,
             g_ffn=m_g_ffn, w_gate_up=m_w_gate_up, w_down=m_w_down)
    v = dict(g_mix=v_g_mix, w_in=v_w_in, g_q_a=v_g_q_a, w_q_b=v_w_q_b, g_kv_a=v_g_kv_a, w_kv_b=v_w_kv_b, g_qn=v_g_qn, g_kn=v_g_kn,
             w_mla_out=v_w_mla_out, ret_decay_fwd=v_ret_decay_fwd, ret_decay_bwd=v_ret_decay_bwd, w_ret_out=v_w_ret_out, w_out=v_w_out,
             g_ffn=v_g_ffn, w_gate_up=v_w_gate_up, w_down=v_w_down)
    gains = {n: w[n].reshape(1, ln) for n, ln in GAINS}

    axis_of = {n: axis for n, _, axis in MATS}
    later = [n for n, _, _ in MATS if n not in FIRST_WEIGHTS]
    gathered = _all_gather([w[n].astype(WIRE) for n in FIRST_WEIGHTS])
    W = {n: _unshard(g, axis_of[n]) for n, g in zip(FIRST_WEIGHTS, gathered)}
    later_handles = _exchange_start("gather_later_start", [w[n].astype(WIRE) for n in later], True, after=gathered[0])

    def late_weights(after):
        lands = _exchange_wait("gather_later_wait", later_handles, after, True)
        return {n: _unshard(g, axis_of[n]) for n, g in zip(later, lands)}

    grad_groups = []

    def grad_hook(g):
        names = tuple(g)
        handles = _exchange_start("grads_start_%d" % len(grad_groups), [_reshard(g[n], axis_of[n]).astype(GWIRE) for n in names], False)
        grad_groups.append((names, handles))
        return handles[4]

    S = x.shape[1]
    pos = positions.reshape(S, 1).astype(F32)
    loss_rows, grad_x, gG, gW = _local_step(x.reshape(S, D_MODEL), pos, loss_target.reshape(S, D_MODEL), gains, W, late_weights, grad_hook,
                                            start_after=later_handles[4])
    loss = lax.psum(jnp.sum(loss_rows), ("x", "y", "c"))

    last = [n for n, _, _ in MATS if n not in EARLY_GRADS + MID_GRADS]
    pieces = [_reshard(gW[n], axis_of[n]).astype(GWIRE) for n in last]
    pieces.append(jnp.broadcast_to(_pack_gains(gG)[None], (N_DEV, 1, GAIN_PAD)))
    late_parts = _all_to_all("grads_last", pieces)
    parts = dict(zip(last, late_parts))
    for i, (names, handles) in enumerate(grad_groups):
        parts.update(zip(names, _exchange_wait("grads_wait_%d" % i, handles, late_parts[-1], False)))
    out = [dict() for _ in range(4)]
    for n, _, _ in MATS:
        for o, r in zip(out, _adamw("adamw_" + n, parts[n], w[n], m[n], v[n])):
            o[n] = r
    for o, r in zip(out, _adamw("adamw_gains", late_parts[-1], _pack_gains(w), _pack_gains(m), _pack_gains(v))):
        o.update(_unpack_gains(r))
    return (loss, grad_x.reshape(x.shape), *[o[n] for o in out for n in ORDER])
```
